```python
import math
import jax
import jax.numpy as jnp
from jax import lax
import numpy as np


D_MODEL = 1024
BATCH = 16
SEQ = 4096
DEPTH = 2

N_EVEN = (DEPTH + 1) // 2
N_ODD = DEPTH // 2
N_DIR = 2
EPS = 1e-6
CONV_W = 4

SSD_WIDTH = D_MODEL
SSD_HEADDIM = 64
SSD_HEADS = SSD_WIDTH // SSD_HEADDIM
SSD_GROUPS = 4
SSD_HPG = SSD_HEADS // SSD_GROUPS
SSD_STATE = 128
SSD_XBC = SSD_WIDTH + 2 * SSD_GROUPS * SSD_STATE
SSD_CHUNK = 64

LRU_WIDTH = D_MODEL
LRU_BLOCKS = 16
LRU_BLOCK = LRU_WIDTH // LRU_BLOCKS
LRU_C = 8.0

CONV_CH = SSD_XBC + LRU_WIDTH
EVEN_IN = CONV_CH + SSD_WIDTH + LRU_WIDTH + N_DIR * SSD_HEADS
EVEN_MIX = SSD_WIDTH + LRU_WIDTH

HGRN_WIDTH = D_MODEL
HGRN_HEADDIM = 128
HGRN_HEADS = HGRN_WIDTH // HGRN_HEADDIM
HGRN_CHUNK = 32
HGRN_SCALE = HGRN_HEADDIM ** -0.5
ODD_IN = 5 * HGRN_WIDTH

D_FF = 4 * D_MODEL

kernel_name = 'hybrid_ssd_rglru_hgrn2_encoder'


def rmsnorm(u, g):
    uf = u.astype(jnp.float32)
    uf = uf * lax.rsqrt(jnp.mean(uf * uf, axis=-1, keepdims=True) + EPS)
    return (uf * g.astype(jnp.float32)).astype(u.dtype)


def centred_conv(u, w, b):
    s_ = u.shape[1]
    pad_l = (CONV_W - 1) // 2
    up = jnp.pad(u, ((0, 0), (pad_l, CONV_W - 1 - pad_l), (0, 0)))
    out = b
    for k in range(CONV_W):
        out = out + up[:, k:k + s_] * w[k]
    return out


def dir_stack(u_f, u_b):
    return jnp.stack([u_f, jnp.flip(u_b, axis=1)], axis=0)


def two_dir(u):
    return dir_stack(u, u)


def merge_dir(y):
    return y[0] + jnp.flip(y[1], axis=1)


def linear_scan(a, b):
    def step(h, ab):
        h = ab[0] * h + ab[1]
        return h, h
    h0 = jnp.zeros_like(a[:, :, 0])
    _, h = lax.scan(step, h0, (jnp.moveaxis(a, 2, 0), jnp.moveaxis(b, 2, 0)))
    return jnp.moveaxis(h, 0, 2)


def ssd_chunked(x, dt, a, bm, cm):
    r_, b_, s_ = x.shape[:3]
    n_chunks = s_ // SSD_CHUNK

    def chunks(u):
        u = u.reshape(r_, b_, n_chunks, SSD_CHUNK, *u.shape[3:])
        return jnp.moveaxis(u, 2, 0)

    xdt = (x * dt[..., None]).reshape(r_, b_, s_, SSD_GROUPS, SSD_HPG, SSD_HEADDIM)
    da = (dt * a[:, None, None, :]).reshape(r_, b_, s_, SSD_GROUPS, SSD_HPG)
    mask = jnp.tril(jnp.ones((SSD_CHUNK, SSD_CHUNK), dtype=bool))[:, :, None, None]

    def step(state, inp):
        xc, dac, bc, cc = inp
        cum = jnp.cumsum(dac, axis=2)
        seg = cum[:, :, :, None] - cum[:, :, None, :]
        decay = jnp.exp(jnp.where(mask, seg, -jnp.inf))
        cb = jnp.einsum('rbtgn,rbsgn->rbtsg', cc, bc)
        y_in = jnp.einsum('rbtsg,rbtsge,rbsgep->rbtgep', cb, decay, xc)
        y_st = jnp.einsum('rbtgn,rbgepn->rbtgep', cc, state) * jnp.exp(cum)[..., None]
        last = cum[:, :, -1]
        w_s = jnp.exp(last[:, :, None] - cum)
        state = state * jnp.exp(last)[..., None, None] + jnp.einsum(
            'rbsgn,rbsge,rbsgep->rbgepn', bc, w_s, xc)
        return state, y_in + y_st

    state0 = jnp.zeros((r_, b_, SSD_GROUPS, SSD_HPG, SSD_HEADDIM, SSD_STATE), jnp.float32)
    _, y = lax.scan(step, state0, (chunks(xdt), chunks(da), chunks(bm), chunks(cm)))
    return jnp.moveaxis(y, 0, 2).reshape(r_, b_, s_, SSD_HEADS, SSD_HEADDIM)


def gla_chunked(q, k, v, log_f):
    r_, b_, s_ = q.shape[:3]
    n_chunks = s_ // HGRN_CHUNK

    def chunks(u):
        u = u.reshape(r_, b_, n_chunks, HGRN_CHUNK, *u.shape[3:])
        return jnp.moveaxis(u, 2, 0)

    mask = jnp.tril(jnp.ones((HGRN_CHUNK, HGRN_CHUNK), dtype=bool))

    def step(state, inp):
        qc, kc, vc, gc = inp
        bcum = jnp.cumsum(gc, axis=2)
        q_t = qc * jnp.exp(bcum)
        k_t = kc * jnp.exp(-bcum)
        att = jnp.where(mask, jnp.einsum('rbthk,rbshk->rbhts', q_t, k_t), 0.0)
        y = jnp.einsum('rbhts,rbshv->rbthv', att, vc) + jnp.einsum(
            'rbthk,rbhkv->rbthv', q_t, state)
        last = bcum[:, :, -1]
        state = state * jnp.exp(last)[..., None] + jnp.einsum(
            'rbshk,rbshv->rbhkv', kc * jnp.exp(last[:, :, None] - bcum), vc)
        return state, y

    state0 = jnp.zeros((r_, b_, HGRN_HEADS, HGRN_HEADDIM, HGRN_HEADDIM), jnp.float32)
    _, y = lax.scan(step, state0, (chunks(q), chunks(k), chunks(v), chunks(log_f)))
    return jnp.moveaxis(y, 0, 2).reshape(r_, b_, s_, HGRN_HEADS, HGRN_HEADDIM)


def mamba2_mixer(xbc, z, dt_raw, a_log, dt_bias, d_skip, norm_w):
    f32 = jnp.float32
    b_, s_, _ = xbc.shape
    xbc = xbc.astype(f32)
    xs = xbc[..., :SSD_WIDTH].reshape(b_, s_, SSD_HEADS, SSD_HEADDIM)
    bm = xbc[..., SSD_WIDTH:SSD_WIDTH + SSD_GROUPS * SSD_STATE].reshape(b_, s_, SSD_GROUPS, SSD_STATE)
    cm = xbc[..., SSD_WIDTH + SSD_GROUPS * SSD_STATE:].reshape(b_, s_, SSD_GROUPS, SSD_STATE)
    dt = jax.nn.softplus(dt_raw.astype(f32) + dt_bias.astype(f32))
    dtd = dir_stack(dt[:, :, 0], dt[:, :, 1])
    a = -jnp.exp(a_log.astype(f32))
    y = merge_dir(ssd_chunked(two_dir(xs), dtd, a, two_dir(bm), two_dir(cm)))
    y = (y + d_skip.astype(f32)[:, None] * xs).reshape(b_, s_, SSD_WIDTH)
    u = (y * jax.nn.silu(z.astype(f32))).reshape(b_, s_, SSD_GROUPS, SSD_WIDTH // SSD_GROUPS)
    u = u * lax.rsqrt(jnp.mean(u * u, axis=-1, keepdims=True) + EPS)
    return u.reshape(b_, s_, SSD_WIDTH) * norm_w.astype(f32)


def rglru_mixer(u, gate, w_a, b_a, w_x, b_x, lam):
    f32 = jnp.float32
    b_, s_, _ = u.shape
    ud = two_dir(u.astype(f32))
    ub = ud.reshape(N_DIR, b_, s_, LRU_BLOCKS, LRU_BLOCK)
    r_gate = jax.nn.sigmoid(jnp.einsum('rbsnc,rncd->rbsnd', ub, w_a.astype(f32)).reshape(
        N_DIR, b_, s_, LRU_WIDTH) + b_a.astype(f32)[:, None, None])
    i_gate = jax.nn.sigmoid(jnp.einsum('rbsnc,rncd->rbsnd', ub, w_x.astype(f32)).reshape(
        N_DIR, b_, s_, LRU_WIDTH) + b_x.astype(f32)[:, None, None])
    log_a = -LRU_C * r_gate * jax.nn.softplus(-lam.astype(f32))[:, None, None]
    inp = jnp.sqrt(-jnp.expm1(2.0 * log_a)) * (i_gate * ud)
    h = merge_dir(linear_scan(jnp.exp(log_a), inp))
    return h * jax.nn.gelu(gate.astype(f32))


def hgrn2_mixer(q, f_raw, inp, gate, lb, norm_w):
    f32 = jnp.float32
    b_, s_, _ = q.shape
    f_raw = f_raw.astype(f32)
    lb = lb.astype(f32)
    log_f = jnp.logaddexp(jnp.log(lb), jnp.log1p(-lb) + jax.nn.log_sigmoid(f_raw))
    k = (1.0 - lb) * jax.nn.sigmoid(-f_raw)

    def heads(u):
        return u.reshape(N_DIR, b_, s_, HGRN_HEADS, HGRN_HEADDIM)

    qd = heads(two_dir(q.astype(f32) * HGRN_SCALE))
    kd = heads(dir_stack(k[:, :, 0], k[:, :, 1]))
    gd = heads(dir_stack(log_f[:, :, 0], log_f[:, :, 1]))
    vd = heads(two_dir(inp.astype(f32)))
    o = merge_dir(gla_chunked(qd, kd, vd, gd))
    o = o * lax.rsqrt(jnp.mean(o * o, axis=-1, keepdims=True) + EPS)
    o = o * norm_w.astype(f32).reshape(HGRN_HEADS, HGRN_HEADDIM)
    return o.reshape(b_, s_, HGRN_WIDTH) * jax.nn.silu(gate.astype(f32))


def _fwd_setup_inputs(seed: int = 0) -> dict:
    key = jax.random.key(seed)
    ks = jax.random.split(key, 24)
    f32 = jnp.float32

    def nrm(k, shape, scale):
        return jax.random.normal(k, shape, f32) * scale

    x = nrm(ks[0], (BATCH, SEQ, D_MODEL), 1.0)
    even_w_in = nrm(ks[1], (N_EVEN, D_MODEL, EVEN_IN), D_MODEL ** -0.5)
    even_conv_w = nrm(ks[2], (N_EVEN, CONV_W, CONV_CH), CONV_W ** -0.5)
    even_conv_b = nrm(ks[3], (N_EVEN, CONV_CH), 0.01)
    ssd_a_log = jnp.log(jax.random.uniform(ks[4], (N_EVEN, N_DIR, SSD_HEADS), f32, 1.0, 16.0))
    dt0 = jnp.exp(jax.random.uniform(ks[5], (N_EVEN, N_DIR, SSD_HEADS), f32,
                                     math.log(1e-3), math.log(1e-1)))
    ssd_dt_bias = dt0 + jnp.log(-jnp.expm1(-dt0))
    ssd_d = 1.0 + nrm(ks[6], (N_EVEN, SSD_HEADS), 0.1)
    ssd_norm_w = 1.0 + nrm(ks[7], (N_EVEN, SSD_WIDTH), 0.1)
    lru_w_a = nrm(ks[8], (N_EVEN, N_DIR, LRU_BLOCKS, LRU_BLOCK, LRU_BLOCK), LRU_BLOCK ** -0.5)
    lru_b_a = nrm(ks[9], (N_EVEN, N_DIR, LRU_WIDTH), 0.01)
    lru_w_x = nrm(ks[10], (N_EVEN, N_DIR, LRU_BLOCKS, LRU_BLOCK, LRU_BLOCK), LRU_BLOCK ** -0.5)
    lru_b_x = nrm(ks[11], (N_EVEN, N_DIR, LRU_WIDTH), 0.01)
    a0 = jax.random.uniform(ks[12], (N_EVEN, N_DIR, LRU_WIDTH), f32, 0.9, 0.999)
    p0 = a0 ** (1.0 / LRU_C)
    lru_lambda = jnp.log(p0) - jnp.log1p(-p0)
    even_w_out = nrm(ks[13], (N_EVEN, EVEN_MIX, D_MODEL), EVEN_MIX ** -0.5)
    odd_w_in = nrm(ks[14], (N_ODD, D_MODEL, ODD_IN), D_MODEL ** -0.5)
    hgrn_lb_logits = nrm(ks[15], (DEPTH, HGRN_WIDTH), 0.1)
    hgrn_norm_w = 1.0 + nrm(ks[16], (N_ODD, HGRN_WIDTH), 0.1)
    odd_w_out = nrm(ks[17], (N_ODD, HGRN_WIDTH, D_MODEL), HGRN_WIDTH ** -0.5)
    norm_mix = 1.0 + nrm(ks[18], (DEPTH, D_MODEL), 0.1)
    norm_mlp = 1.0 + nrm(ks[19], (DEPTH, D_MODEL), 0.1)
    mlp_w1 = nrm(ks[20], (DEPTH, D_MODEL, D_FF), D_MODEL ** -0.5)
    mlp_w2 = nrm(ks[21], (DEPTH, D_FF, D_MODEL), D_FF ** -0.5)
    norm_final = 1.0 + nrm(ks[22], (D_MODEL,), 0.1)
    return {'x': x, 'even_w_in': even_w_in, 'even_conv_w': even_conv_w,
            'even_conv_b': even_conv_b, 'ssd_a_log': ssd_a_log, 'ssd_dt_bias': ssd_dt_bias,
            'ssd_d': ssd_d, 'ssd_norm_w': ssd_norm_w, 'lru_w_a': lru_w_a, 'lru_b_a': lru_b_a,
            'lru_w_x': lru_w_x, 'lru_b_x': lru_b_x, 'lru_lambda': lru_lambda,
            'even_w_out': even_w_out, 'odd_w_in': odd_w_in, 'hgrn_lb_logits': hgrn_lb_logits,
            'hgrn_norm_w': hgrn_norm_w, 'odd_w_out': odd_w_out, 'norm_mix': norm_mix,
            'norm_mlp': norm_mlp, 'mlp_w1': mlp_w1, 'mlp_w2': mlp_w2, 'norm_final': norm_final}


def _fwd_reference(x, even_w_in, even_conv_w, even_conv_b, ssd_a_log, ssd_dt_bias, ssd_d,
              ssd_norm_w, lru_w_a, lru_b_a, lru_w_x, lru_b_x, lru_lambda, even_w_out,
              odd_w_in, hgrn_lb_logits, hgrn_norm_w, odd_w_out, norm_mix, norm_mlp,
              mlp_w1, mlp_w2, norm_final):
    b_, s_, _ = x.shape
    p_lb = jax.nn.softmax(hgrn_lb_logits.astype(jnp.float32), axis=0)
    lb_all = jnp.cumsum(p_lb, axis=0) - p_lb[0]
    for l in range(DEPTH):
        h = rmsnorm(x, norm_mix[l])
        if l % 2 == 0:
            j = l // 2
            proj = h @ even_w_in[j]
            conv = centred_conv(proj[..., :CONV_CH], even_conv_w[j], even_conv_b[j])
            xbc = jax.nn.silu(conv[..., :SSD_XBC])
            u_lru = conv[..., SSD_XBC:]
            o0 = CONV_CH
            z = proj[..., o0:o0 + SSD_WIDTH]
            g_lru = proj[..., o0 + SSD_WIDTH:o0 + SSD_WIDTH + LRU_WIDTH]
            dt_raw = proj[..., o0 + SSD_WIDTH + LRU_WIDTH:].reshape(b_, s_, N_DIR, SSD_HEADS)
            y_a = mamba2_mixer(xbc, z, dt_raw, ssd_a_log[j], ssd_dt_bias[j], ssd_d[j],
                               ssd_norm_w[j])
            y_b = rglru_mixer(u_lru, g_lru, lru_w_a[j], lru_b_a[j], lru_w_x[j], lru_b_x[j],
                              lru_lambda[j])
            mix = jnp.concatenate([y_a, y_b], axis=-1) @ even_w_out[j]
        else:
            j = l // 2
            proj = h @ odd_w_in[j]
            w_ = HGRN_WIDTH
            q = proj[..., :w_]
            f_raw = proj[..., w_:3 * w_].reshape(b_, s_, N_DIR, w_)
            inp = proj[..., 3 * w_:4 * w_]
            gate = proj[..., 4 * w_:]
            mix = hgrn2_mixer(q, f_raw, inp, gate, lb_all[l], hgrn_norm_w[j]) @ odd_w_out[j]
        x = x + mix.astype(x.dtype)
        h = rmsnorm(x, norm_mlp[l])
        x = x + (jnp.square(jax.nn.relu(h @ mlp_w1[l])) @ mlp_w2[l]).astype(x.dtype)
    return rmsnorm(x, norm_final)


import jax as _jax
import jax.numpy as _jnp

TWIN_FORMAT = 'train_step'
FWD_PARAMS = ['x', 'even_w_in', 'even_conv_w', 'even_conv_b', 'ssd_a_log', 'ssd_dt_bias', 'ssd_d', 'ssd_norm_w', 'lru_w_a', 'lru_b_a', 'lru_w_x', 'lru_b_x', 'lru_lambda', 'even_w_out', 'odd_w_in', 'hgrn_lb_logits', 'hgrn_norm_w', 'odd_w_out', 'norm_mix', 'norm_mlp', 'mlp_w1', 'mlp_w2', 'norm_final']
TWIN_WEIGHTS = ['even_w_in', 'even_conv_w', 'even_conv_b', 'ssd_a_log', 'ssd_dt_bias', 'ssd_d', 'ssd_norm_w', 'lru_w_a', 'lru_b_a', 'lru_w_x', 'lru_b_x', 'lru_lambda', 'even_w_out', 'odd_w_in', 'hgrn_lb_logits', 'hgrn_norm_w', 'odd_w_out', 'norm_mix', 'norm_mlp', 'mlp_w1', 'mlp_w2', 'norm_final']
TWIN_DIFF_INPUT = 'x'
TWIN_INPUTS = ['x', 'even_w_in', 'even_conv_w', 'even_conv_b', 'ssd_a_log', 'ssd_dt_bias', 'ssd_d', 'ssd_norm_w', 'lru_w_a', 'lru_b_a', 'lru_w_x', 'lru_b_x', 'lru_lambda', 'even_w_out', 'odd_w_in', 'hgrn_lb_logits', 'hgrn_norm_w', 'odd_w_out', 'norm_mix', 'norm_mlp', 'mlp_w1', 'mlp_w2', 'norm_final', 'loss_target', 'm_even_w_in', 'm_even_conv_w', 'm_even_conv_b', 'm_ssd_a_log', 'm_ssd_dt_bias', 'm_ssd_d', 'm_ssd_norm_w', 'm_lru_w_a', 'm_lru_b_a', 'm_lru_w_x', 'm_lru_b_x', 'm_lru_lambda', 'm_even_w_out', 'm_odd_w_in', 'm_hgrn_lb_logits', 'm_hgrn_norm_w', 'm_odd_w_out', 'm_norm_mix', 'm_norm_mlp', 'm_mlp_w1', 'm_mlp_w2', 'm_norm_final', 'v_even_w_in', 'v_even_conv_w', 'v_even_conv_b', 'v_ssd_a_log', 'v_ssd_dt_bias', 'v_ssd_d', 'v_ssd_norm_w', 'v_lru_w_a', 'v_lru_b_a', 'v_lru_w_x', 'v_lru_b_x', 'v_lru_lambda', 'v_even_w_out', 'v_odd_w_in', 'v_hgrn_lb_logits', 'v_hgrn_norm_w', 'v_odd_w_out', 'v_norm_mix', 'v_norm_mlp', 'v_mlp_w1', 'v_mlp_w2', 'v_norm_final']
TWIN_OUTPUTS = ['loss', 'grad_x', 'grad_even_w_in', 'grad_even_conv_w', 'grad_even_conv_b', 'grad_ssd_a_log', 'grad_ssd_dt_bias', 'grad_ssd_d', 'grad_ssd_norm_w', 'grad_lru_w_a', 'grad_lru_b_a', 'grad_lru_w_x', 'grad_lru_b_x', 'grad_lru_lambda', 'grad_even_w_out', 'grad_odd_w_in', 'grad_hgrn_lb_logits', 'grad_hgrn_norm_w', 'grad_odd_w_out', 'grad_norm_mix', 'grad_norm_mlp', 'grad_mlp_w1', 'grad_mlp_w2', 'grad_norm_final', 'delta_even_w_in', 'delta_even_conv_w', 'delta_even_conv_b', 'delta_ssd_a_log', 'delta_ssd_dt_bias', 'delta_ssd_d', 'delta_ssd_norm_w', 'delta_lru_w_a', 'delta_lru_b_a', 'delta_lru_w_x', 'delta_lru_b_x', 'delta_lru_lambda', 'delta_even_w_out', 'delta_odd_w_in', 'delta_hgrn_lb_logits', 'delta_hgrn_norm_w', 'delta_odd_w_out', 'delta_norm_mix', 'delta_norm_mlp', 'delta_mlp_w1', 'delta_mlp_w2', 'delta_norm_final', 'new_m_even_w_in', 'new_m_even_conv_w', 'new_m_even_conv_b', 'new_m_ssd_a_log', 'new_m_ssd_dt_bias', 'new_m_ssd_d', 'new_m_ssd_norm_w', 'new_m_lru_w_a', 'new_m_lru_b_a', 'new_m_lru_w_x', 'new_m_lru_b_x', 'new_m_lru_lambda', 'new_m_even_w_out', 'new_m_odd_w_in', 'new_m_hgrn_lb_logits', 'new_m_hgrn_norm_w', 'new_m_odd_w_out', 'new_m_norm_mix', 'new_m_norm_mlp', 'new_m_mlp_w1', 'new_m_mlp_w2', 'new_m_norm_final', 'new_v_even_w_in', 'new_v_even_conv_w', 'new_v_even_conv_b', 'new_v_ssd_a_log', 'new_v_ssd_dt_bias', 'new_v_ssd_d', 'new_v_ssd_norm_w', 'new_v_lru_w_a', 'new_v_lru_b_a', 'new_v_lru_w_x', 'new_v_lru_b_x', 'new_v_lru_lambda', 'new_v_even_w_out', 'new_v_odd_w_in', 'new_v_hgrn_lb_logits', 'new_v_hgrn_norm_w', 'new_v_odd_w_out', 'new_v_norm_mix', 'new_v_norm_mlp', 'new_v_mlp_w1', 'new_v_mlp_w2', 'new_v_norm_final']
TWIN_LEAF_KINDS = {'loss': 'loss', 'grad_x': 'grad_x', 'grad_even_w_in': 'grad_w', 'grad_even_conv_w': 'grad_w', 'grad_even_conv_b': 'grad_w', 'grad_ssd_a_log': 'grad_w', 'grad_ssd_dt_bias': 'grad_w', 'grad_ssd_d': 'grad_w', 'grad_ssd_norm_w': 'grad_w', 'grad_lru_w_a': 'grad_w', 'grad_lru_b_a': 'grad_w', 'grad_lru_w_x': 'grad_w', 'grad_lru_b_x': 'grad_w', 'grad_lru_lambda': 'grad_w', 'grad_even_w_out': 'grad_w', 'grad_odd_w_in': 'grad_w', 'grad_hgrn_lb_logits': 'grad_w', 'grad_hgrn_norm_w': 'grad_w', 'grad_odd_w_out': 'grad_w', 'grad_norm_mix': 'grad_w', 'grad_norm_mlp': 'grad_w', 'grad_mlp_w1': 'grad_w', 'grad_mlp_w2': 'grad_w', 'grad_norm_final': 'grad_w', 'delta_even_w_in': 'delta_w', 'delta_even_conv_w': 'delta_w', 'delta_even_conv_b': 'delta_w', 'delta_ssd_a_log': 'delta_w', 'delta_ssd_dt_bias': 'delta_w', 'delta_ssd_d': 'delta_w', 'delta_ssd_norm_w': 'delta_w', 'delta_lru_w_a': 'delta_w', 'delta_lru_b_a': 'delta_w', 'delta_lru_w_x': 'delta_w', 'delta_lru_b_x': 'delta_w', 'delta_lru_lambda': 'delta_w', 'delta_even_w_out': 'delta_w', 'delta_odd_w_in': 'delta_w', 'delta_hgrn_lb_logits': 'delta_w', 'delta_hgrn_norm_w': 'delta_w', 'delta_odd_w_out': 'delta_w', 'delta_norm_mix': 'delta_w', 'delta_norm_mlp': 'delta_w', 'delta_mlp_w1': 'delta_w', 'delta_mlp_w2': 'delta_w', 'delta_norm_final': 'delta_w', 'new_m_even_w_in': 'new_m', 'new_m_even_conv_w': 'new_m', 'new_m_even_conv_b': 'new_m', 'new_m_ssd_a_log': 'new_m', 'new_m_ssd_dt_bias': 'new_m', 'new_m_ssd_d': 'new_m', 'new_m_ssd_norm_w': 'new_m', 'new_m_lru_w_a': 'new_m', 'new_m_lru_b_a': 'new_m', 'new_m_lru_w_x': 'new_m', 'new_m_lru_b_x': 'new_m', 'new_m_lru_lambda': 'new_m', 'new_m_even_w_out': 'new_m', 'new_m_odd_w_in': 'new_m', 'new_m_hgrn_lb_logits': 'new_m', 'new_m_hgrn_norm_w': 'new_m', 'new_m_odd_w_out': 'new_m', 'new_m_norm_mix': 'new_m', 'new_m_norm_mlp': 'new_m', 'new_m_mlp_w1': 'new_m', 'new_m_mlp_w2': 'new_m', 'new_m_norm_final': 'new_m', 'new_v_even_w_in': 'new_v', 'new_v_even_conv_w': 'new_v', 'new_v_even_conv_b': 'new_v', 'new_v_ssd_a_log': 'new_v', 'new_v_ssd_dt_bias': 'new_v', 'new_v_ssd_d': 'new_v', 'new_v_ssd_norm_w': 'new_v', 'new_v_lru_w_a': 'new_v', 'new_v_lru_b_a': 'new_v', 'new_v_lru_w_x': 'new_v', 'new_v_lru_b_x': 'new_v', 'new_v_lru_lambda': 'new_v', 'new_v_even_w_out': 'new_v', 'new_v_odd_w_in': 'new_v', 'new_v_hgrn_lb_logits': 'new_v', 'new_v_hgrn_norm_w': 'new_v', 'new_v_odd_w_out': 'new_v', 'new_v_norm_mix': 'new_v', 'new_v_norm_mlp': 'new_v', 'new_v_mlp_w1': 'new_v', 'new_v_mlp_w2': 'new_v', 'new_v_norm_final': 'new_v'}


def _forward(args):
    return _fwd_reference(*[args[k] for k in FWD_PARAMS])


def _output_shape():
    out = _jax.eval_shape(lambda: _forward(_fwd_setup_inputs(0)))
    return out.shape, out.dtype

N_MICROBATCH = 1
ADAM_LR = 0.001
ADAM_B1 = 0.9
ADAM_B2 = 0.999
ADAM_EPS = 1e-08
ADAM_WD = 0.01
ADAM_STEP = 10
PER_EXAMPLE_BATCH_AXIS = {'x': 0, 'loss_target': 0}
SHARED_INPUTS = []
_WEIGHT_DTYPES = {'even_w_in': _jnp.float32, 'even_conv_w': _jnp.float32, 'even_conv_b': _jnp.float32, 'ssd_a_log': _jnp.float32, 'ssd_dt_bias': _jnp.float32, 'ssd_d': _jnp.float32, 'ssd_norm_w': _jnp.float32, 'lru_w_a': _jnp.float32, 'lru_b_a': _jnp.float32, 'lru_w_x': _jnp.float32, 'lru_b_x': _jnp.float32, 'lru_lambda': _jnp.float32, 'even_w_out': _jnp.float32, 'odd_w_in': _jnp.float32, 'hgrn_lb_logits': _jnp.float32, 'hgrn_norm_w': _jnp.float32, 'odd_w_out': _jnp.float32, 'norm_mix': _jnp.float32, 'norm_mlp': _jnp.float32, 'mlp_w1': _jnp.float32, 'mlp_w2': _jnp.float32, 'norm_final': _jnp.float32}
MOMENT_SCALE = {'even_w_in': 1.523984e-01, 'even_conv_w': 1.746207e-01, 'even_conv_b': 2.916776e+00, 'ssd_a_log': 3.760865e-01, 'ssd_dt_bias': 2.867923e-01, 'ssd_d': 8.340407e-01, 'ssd_norm_w': 2.920142e-01, 'lru_w_a': 1.035122e-01, 'lru_b_a': 6.044291e-02, 'lru_w_x': 1.960834e-01, 'lru_b_x': 3.939559e-02, 'lru_lambda': 8.251923e-02, 'even_w_out': 3.302889e-01, 'odd_w_in': 9.496313e-02, 'hgrn_lb_logits': 5.487736e-02, 'hgrn_norm_w': 1.105716e-01, 'odd_w_out': 9.407294e-02, 'norm_mix': 2.995218e-01, 'norm_mlp': 2.845232e-01, 'mlp_w1': 1.407162e-01, 'mlp_w2': 6.793603e-01, 'norm_final': 6.486345e+01}


def _to_microbatches(a, axis):
    t = _jnp.moveaxis(a, axis, 0)
    t = t.reshape((N_MICROBATCH, t.shape[0] // N_MICROBATCH) + t.shape[1:])
    return _jnp.moveaxis(t, 1, axis + 1)


def setup_inputs(seed: int = 0) -> dict:
    inp = _fwd_setup_inputs(seed)
    key = _jax.random.fold_in(_jax.random.key(seed), 7919)
    shape, _ = _output_shape()
    out = dict(inp)
    out["loss_target"] = _jax.random.normal(_jax.random.fold_in(key, 0), shape, _jnp.float32)
    for i, name in enumerate(TWIN_WEIGHTS):
        w = inp[name].astype(_jnp.float32)
        if MOMENT_SCALE is None:
            s = _jnp.sqrt(_jnp.mean(_jnp.square(w)) + 1e-30)
        else:
            s = MOMENT_SCALE[name]
        km, kv = _jax.random.split(_jax.random.fold_in(key, i + 1))
        out[name] = w
        out["m_" + name] = s * _jax.random.normal(km, w.shape, _jnp.float32)
        out["v_" + name] = (s * s) * _jax.random.uniform(kv, w.shape, _jnp.float32, 0.5, 1.5)
    if N_MICROBATCH > 1:
        for name, axis in PER_EXAMPLE_BATCH_AXIS.items():
            out[name] = _to_microbatches(out[name], axis)
    return {'x': out['x'], 'even_w_in': out['even_w_in'], 'even_conv_w': out['even_conv_w'], 'even_conv_b': out['even_conv_b'], 'ssd_a_log': out['ssd_a_log'], 'ssd_dt_bias': out['ssd_dt_bias'], 'ssd_d': out['ssd_d'], 'ssd_norm_w': out['ssd_norm_w'], 'lru_w_a': out['lru_w_a'], 'lru_b_a': out['lru_b_a'], 'lru_w_x': out['lru_w_x'], 'lru_b_x': out['lru_b_x'], 'lru_lambda': out['lru_lambda'], 'even_w_out': out['even_w_out'], 'odd_w_in': out['odd_w_in'], 'hgrn_lb_logits': out['hgrn_lb_logits'], 'hgrn_norm_w': out['hgrn_norm_w'], 'odd_w_out': out['odd_w_out'], 'norm_mix': out['norm_mix'], 'norm_mlp': out['norm_mlp'], 'mlp_w1': out['mlp_w1'], 'mlp_w2': out['mlp_w2'], 'norm_final': out['norm_final'], 'loss_target': out['loss_target'], 'm_even_w_in': out['m_even_w_in'], 'm_even_conv_w': out['m_even_conv_w'], 'm_even_conv_b': out['m_even_conv_b'], 'm_ssd_a_log': out['m_ssd_a_log'], 'm_ssd_dt_bias': out['m_ssd_dt_bias'], 'm_ssd_d': out['m_ssd_d'], 'm_ssd_norm_w': out['m_ssd_norm_w'], 'm_lru_w_a': out['m_lru_w_a'], 'm_lru_b_a': out['m_lru_b_a'], 'm_lru_w_x': out['m_lru_w_x'], 'm_lru_b_x': out['m_lru_b_x'], 'm_lru_lambda': out['m_lru_lambda'], 'm_even_w_out': out['m_even_w_out'], 'm_odd_w_in': out['m_odd_w_in'], 'm_hgrn_lb_logits': out['m_hgrn_lb_logits'], 'm_hgrn_norm_w': out['m_hgrn_norm_w'], 'm_odd_w_out': out['m_odd_w_out'], 'm_norm_mix': out['m_norm_mix'], 'm_norm_mlp': out['m_norm_mlp'], 'm_mlp_w1': out['m_mlp_w1'], 'm_mlp_w2': out['m_mlp_w2'], 'm_norm_final': out['m_norm_final'], 'v_even_w_in': out['v_even_w_in'], 'v_even_conv_w': out['v_even_conv_w'], 'v_even_conv_b': out['v_even_conv_b'], 'v_ssd_a_log': out['v_ssd_a_log'], 'v_ssd_dt_bias': out['v_ssd_dt_bias'], 'v_ssd_d': out['v_ssd_d'], 'v_ssd_norm_w': out['v_ssd_norm_w'], 'v_lru_w_a': out['v_lru_w_a'], 'v_lru_b_a': out['v_lru_b_a'], 'v_lru_w_x': out['v_lru_w_x'], 'v_lru_b_x': out['v_lru_b_x'], 'v_lru_lambda': out['v_lru_lambda'], 'v_even_w_out': out['v_even_w_out'], 'v_odd_w_in': out['v_odd_w_in'], 'v_hgrn_lb_logits': out['v_hgrn_lb_logits'], 'v_hgrn_norm_w': out['v_hgrn_norm_w'], 'v_odd_w_out': out['v_odd_w_out'], 'v_norm_mix': out['v_norm_mix'], 'v_norm_mlp': out['v_norm_mlp'], 'v_mlp_w1': out['v_mlp_w1'], 'v_mlp_w2': out['v_mlp_w2'], 'v_norm_final': out['v_norm_final']}


def _loss(weights, diff, rest, loss_target):
    with _jax.named_scope("forward"):
        args = {**rest, TWIN_DIFF_INPUT: diff, **{k: w.astype(_WEIGHT_DTYPES[k]) for k, w in weights.items()}}
        y = _forward(args)
    with _jax.named_scope("loss_head"):
        err = _jnp.square(y.astype(_jnp.float32) - loss_target)
        return 0.5 * _jnp.sum(_jnp.mean(err, axis=-1)) if err.ndim else 0.5 * err


def _adamw(w, g, m, v):
    m = ADAM_B1 * m + (1.0 - ADAM_B1) * g
    v = ADAM_B2 * v + (1.0 - ADAM_B2) * _jnp.square(g)
    m_hat = m / (1.0 - ADAM_B1 ** ADAM_STEP)
    v_hat = v / (1.0 - ADAM_B2 ** ADAM_STEP)
    delta = -ADAM_LR * (m_hat / (_jnp.sqrt(v_hat) + ADAM_EPS) + ADAM_WD * w)
    return delta, m, v


def reference(x, even_w_in, even_conv_w, even_conv_b, ssd_a_log, ssd_dt_bias, ssd_d, ssd_norm_w, lru_w_a, lru_b_a, lru_w_x, lru_b_x, lru_lambda, even_w_out, odd_w_in, hgrn_lb_logits, hgrn_norm_w, odd_w_out, norm_mix, norm_mlp, mlp_w1, mlp_w2, norm_final, loss_target, m_even_w_in, m_even_conv_w, m_even_conv_b, m_ssd_a_log, m_ssd_dt_bias, m_ssd_d, m_ssd_norm_w, m_lru_w_a, m_lru_b_a, m_lru_w_x, m_lru_b_x, m_lru_lambda, m_even_w_out, m_odd_w_in, m_hgrn_lb_logits, m_hgrn_norm_w, m_odd_w_out, m_norm_mix, m_norm_mlp, m_mlp_w1, m_mlp_w2, m_norm_final, v_even_w_in, v_even_conv_w, v_even_conv_b, v_ssd_a_log, v_ssd_dt_bias, v_ssd_d, v_ssd_norm_w, v_lru_w_a, v_lru_b_a, v_lru_w_x, v_lru_b_x, v_lru_lambda, v_even_w_out, v_odd_w_in, v_hgrn_lb_logits, v_hgrn_norm_w, v_odd_w_out, v_norm_mix, v_norm_mlp, v_mlp_w1, v_mlp_w2, v_norm_final):
    given = dict(x=x, even_w_in=even_w_in, even_conv_w=even_conv_w, even_conv_b=even_conv_b, ssd_a_log=ssd_a_log, ssd_dt_bias=ssd_dt_bias, ssd_d=ssd_d, ssd_norm_w=ssd_norm_w, lru_w_a=lru_w_a, lru_b_a=lru_b_a, lru_w_x=lru_w_x, lru_b_x=lru_b_x, lru_lambda=lru_lambda, even_w_out=even_w_out, odd_w_in=odd_w_in, hgrn_lb_logits=hgrn_lb_logits, hgrn_norm_w=hgrn_norm_w, odd_w_out=odd_w_out, norm_mix=norm_mix, norm_mlp=norm_mlp, mlp_w1=mlp_w1, mlp_w2=mlp_w2, norm_final=norm_final, loss_target=loss_target, m_even_w_in=m_even_w_in, m_even_conv_w=m_even_conv_w, m_even_conv_b=m_even_conv_b, m_ssd_a_log=m_ssd_a_log, m_ssd_dt_bias=m_ssd_dt_bias, m_ssd_d=m_ssd_d, m_ssd_norm_w=m_ssd_norm_w, m_lru_w_a=m_lru_w_a, m_lru_b_a=m_lru_b_a, m_lru_w_x=m_lru_w_x, m_lru_b_x=m_lru_b_x, m_lru_lambda=m_lru_lambda, m_even_w_out=m_even_w_out, m_odd_w_in=m_odd_w_in, m_hgrn_lb_logits=m_hgrn_lb_logits, m_hgrn_norm_w=m_hgrn_norm_w, m_odd_w_out=m_odd_w_out, m_norm_mix=m_norm_mix, m_norm_mlp=m_norm_mlp, m_mlp_w1=m_mlp_w1, m_mlp_w2=m_mlp_w2, m_norm_final=m_norm_final, v_even_w_in=v_even_w_in, v_even_conv_w=v_even_conv_w, v_even_conv_b=v_even_conv_b, v_ssd_a_log=v_ssd_a_log, v_ssd_dt_bias=v_ssd_dt_bias, v_ssd_d=v_ssd_d, v_ssd_norm_w=v_ssd_norm_w, v_lru_w_a=v_lru_w_a, v_lru_b_a=v_lru_b_a, v_lru_w_x=v_lru_w_x, v_lru_b_x=v_lru_b_x, v_lru_lambda=v_lru_lambda, v_even_w_out=v_even_w_out, v_odd_w_in=v_odd_w_in, v_hgrn_lb_logits=v_hgrn_lb_logits, v_hgrn_norm_w=v_hgrn_norm_w, v_odd_w_out=v_odd_w_out, v_norm_mix=v_norm_mix, v_norm_mlp=v_norm_mlp, v_mlp_w1=v_mlp_w1, v_mlp_w2=v_mlp_w2, v_norm_final=v_norm_final)
    weights = {n: given[n] for n in TWIN_WEIGHTS}
    shared = {n: given[n] for n in SHARED_INPUTS}
    per_example = {n: given[n] for n in ['x']}
    grad_fn = _jax.value_and_grad(_loss, argnums=(0, 1))

    def one_microbatch(ex, loss_target):
        ex = dict(ex)
        diff = ex.pop(TWIN_DIFF_INPUT)
        return grad_fn(weights, diff, {**shared, **ex}, loss_target)

    if N_MICROBATCH == 1:
        loss, (grad_w, grad_x) = one_microbatch(per_example, given["loss_target"])
    else:
        def body(carry, xs):
            loss_sum, grad_sum = carry
            l_k, (gw_k, gx_k) = one_microbatch(xs[0], xs[1])
            with _jax.named_scope("update"):
                return (loss_sum + l_k, _jax.tree.map(_jnp.add, grad_sum, gw_k)), gx_k

        init = (_jnp.zeros((), _jnp.float32), _jax.tree.map(_jnp.zeros_like, weights))
        (loss, grad_w), grad_x = _jax.lax.scan(body, init, (per_example, given["loss_target"]))
    with _jax.named_scope("update"):
        delta_w, new_m, new_v = {}, {}, {}
        for n in TWIN_WEIGHTS:
            delta_w[n], new_m[n], new_v[n] = _adamw(weights[n], grad_w[n], given["m_" + n], given["v_" + n])
    return (loss, grad_x, *[grad_w[n] for n in TWIN_WEIGHTS], *[delta_w[n] for n in TWIN_WEIGHTS],
            *[new_m[n] for n in TWIN_WEIGHTS], *[new_v[n] for n in TWIN_WEIGHTS])
```

```python
import functools
import math

import jax
import jax.numpy as jnp
from jax import lax
from jax.experimental import pallas as pl
from jax.experimental.pallas import tpu as pltpu

F32 = jnp.float32
BF16 = jnp.bfloat16
MXU_DTYPE = jnp.bfloat16
HI = lax.Precision.HIGHEST
MESH = pl.DeviceIdType.MESH

D_MODEL = 1024
EPS = 1e-6
SSD_HEADS = 16
SSD_HEADDIM = 64
HEAD_SHIFT = 6
SSD_GROUPS = 4
SSD_STATE = 128
SSD_CHUNK = 128
LRU_C = 8.0
LRU_ROWS = 256
HGRN_HEADS = 8
HGRN_HEADDIM = 128
HGRN_SUB = 32
HGRN_BLOCK = 128
HGRN_SCALE = HGRN_HEADDIM ** -0.5
CONV_ROWS = 512

ADAM_LR = 0.001
ADAM_B1 = 0.9
ADAM_B2 = 0.999
ADAM_EPS = 1e-08
ADAM_WD = 0.01
ADAM_STEP = 10

VMEM_LIMIT = 56 * 1024 * 1024
PACK_COLS = 1024
PACK_ROWS = 8192
WPACK_ROWS = 7456

WEIGHTS = (
    ("even_w_in", (1, 1024, 1288), (1, 1024, 5152), 2),
    ("even_conv_w", (1, 4, 768), (1, 4, 3072), 2),
    ("even_conv_b", (1, 3072), (1, 3072), None),
    ("ssd_a_log", (1, 2, 16), (1, 2, 16), None),
    ("ssd_dt_bias", (1, 2, 16), (1, 2, 16), None),
    ("ssd_d", (1, 16), (1, 16), None),
    ("ssd_norm_w", (1, 1024), (1, 1024), None),
    ("lru_w_a", (1, 2, 16, 64, 64), (1, 2, 16, 64, 64), None),
    ("lru_b_a", (1, 2, 256), (1, 2, 1024), 2),
    ("lru_w_x", (1, 2, 16, 64, 64), (1, 2, 16, 64, 64), None),
    ("lru_b_x", (1, 2, 256), (1, 2, 1024), 2),
    ("lru_lambda", (1, 2, 256), (1, 2, 1024), 2),
    ("even_w_out", (1, 512, 1024), (1, 2048, 1024), 1),
    ("odd_w_in", (1, 1024, 1280), (1, 1024, 5120), 2),
    ("hgrn_lb_logits", (2, 1024), (2, 1024), None),
    ("hgrn_norm_w", (1, 256), (1, 1024), 1),
    ("odd_w_out", (1, 256, 1024), (1, 1024, 1024), 1),
    ("norm_mix", (2, 1024), (2, 1024), None),
    ("norm_mlp", (2, 1024), (2, 1024), None),
    ("mlp_w1", (2, 1024, 1024), (2, 1024, 4096), 2),
    ("mlp_w2", (2, 1024, 1024), (2, 4096, 1024), 1),
    ("norm_final", (1024,), (1024,), None),
)
BIG = ("even_w_in", "even_w_out", "odd_w_in", "odd_w_out", "mlp_w1", "mlp_w2")
SMALL_SHARDED = ("even_conv_w", "lru_b_a", "lru_b_x", "lru_lambda", "hgrn_norm_w")


def _pcall(body, **kw):
    return pl.pallas_call(body, **kw)


def _params(**kw):
    return pltpu.CompilerParams(vmem_limit_bytes=VMEM_LIMIT, **kw)


def _tile(n, pref):
    if n <= pref:
        return n
    t = (pref // 128) * 128
    while n % t:
        t -= 128
    return t


def _dot(a, b, dims=(((1,), (0,)), ((), ())), precision=None):
    return lax.dot_general(a, b, dims, preferred_element_type=F32, precision=precision)


_NN = (((1,), (0,)), ((), ()))
_NT = (((1,), (1,)), ((), ()))
_TN = (((0,), (0,)), ((), ()))


def _mx(v):
    return v.astype(MXU_DTYPE)


def _mm(name, a, b, mode, *, out_dtype=F32, res=None, relu2=False):
    if mode == "nn":
        (m, kk), (_, n) = a.shape, b.shape
    elif mode == "nt":
        (m, kk), (n, _) = a.shape, b.shape
    else:
        (kk, m), (_, n) = a.shape, b.shape
    tm, tn, tk = _tile(m, 512), _tile(n, 1024), _tile(kk, 1024 if mode != "tn" else 512)
    nk = kk // tk
    dims = {"nn": _NN, "nt": _NT, "tn": _TN}[mode]
    a_spec = pl.BlockSpec((tk, tm), lambda i, j, k: (k, i)) if mode == "tn" else pl.BlockSpec((tm, tk), lambda i, j, k: (i, k))
    b_spec = pl.BlockSpec((tn, tk), lambda i, j, k: (j, k)) if mode == "nt" else pl.BlockSpec((tk, tn), lambda i, j, k: (k, j))
    o_spec = pl.BlockSpec((tm, tn), lambda i, j, k: (i, j))
    has_res = res is not None

    def body(*refs):
        a_ref, b_ref = refs[0], refs[1]
        res_ref = refs[2] if has_res else None
        outs = refs[2 + has_res:-1]
        acc = refs[-1]
        k = pl.program_id(2)

        @pl.when(k == 0)
        def _():
            acc[...] = jnp.zeros_like(acc)

        acc[...] += _dot(_mx(a_ref[...]), _mx(b_ref[...]), dims)

        @pl.when(k == nk - 1)
        def _():
            r = acc[...]
            if has_res:
                r = r + res_ref[...]
            if relu2:
                outs[0][...] = r
                outs[1][...] = jnp.square(jnp.maximum(r, 0.0)).astype(outs[1].dtype)
            else:
                outs[0][...] = r.astype(outs[0].dtype)

    in_specs = [a_spec, b_spec] + ([o_spec] if has_res else [])
    if relu2:
        out_shape = (jax.ShapeDtypeStruct((m, n), F32), jax.ShapeDtypeStruct((m, n), BF16))
        out_specs = (o_spec, o_spec)
    else:
        out_shape = jax.ShapeDtypeStruct((m, n), out_dtype)
        out_specs = o_spec
    args = (a, b) + ((res,) if has_res else ())
    return _pcall(body, name=name, grid=(m // tm, n // tn, nk), in_specs=in_specs, out_specs=out_specs,
                  out_shape=out_shape, scratch_shapes=[pltpu.VMEM((tm, tn), F32)], compiler_params=_params())(*args)


def _pw_fwd(name, f, ins, params, out_dtypes, tc, ncol, tm=256):
    t = ins[0][0].shape[0]
    tm = min(tm, t)
    ni, npar = len(ins), len(params)

    def body(*refs):
        vals = f(*[r[...].astype(F32) for r in refs[:ni]], *[r[...] for r in refs[ni:ni + npar]])
        for o, v in zip(refs[ni + npar:], vals):
            o[...] = v.astype(o.dtype)

    in_specs = [pl.BlockSpec((tm, tc), lambda j, i, off=off: (i, off + j)) for _, off in ins]
    in_specs += [pl.BlockSpec((1, tc), lambda j, i, off=off: (0, off + j)) for _, off in params]
    out_specs = tuple(pl.BlockSpec((tm, tc), lambda j, i: (i, j)) for _ in out_dtypes)
    out_shape = tuple(jax.ShapeDtypeStruct((t, ncol * tc), d) for d in out_dtypes)
    return _pcall(body, name=name, grid=(ncol, t // tm), in_specs=in_specs, out_specs=out_specs, out_shape=out_shape,
                  compiler_params=_params())(*[a for a, _ in ins], *[p for p, _ in params])


def _pw_bwd(name, f, ins, params, douts, tc, ncol, want, adds=None, tm=256):
    adds = adds or {}
    t = ins[0][0].shape[0]
    tm = min(tm, t)
    ni, npar, nd, na = len(ins), len(params), len(douts), len(adds)
    add_keys = sorted(adds)

    def body(*refs):
        in_refs, p_refs = refs[:ni], refs[ni:ni + npar]
        d_refs = refs[ni + npar:ni + npar + nd]
        a_refs = refs[ni + npar + nd:ni + npar + nd + na]
        o_refs = refs[ni + npar + nd + na:]
        _, vjp = jax.vjp(f, *[r[...].astype(F32) for r in in_refs], *[r[...] for r in p_refs])
        cts = vjp(tuple(d[...] for d in d_refs))
        for o, kidx in zip(o_refs[:len(want)], want):
            v = cts[kidx]
            if kidx in adds:
                v = v + a_refs[add_keys.index(kidx)][...]
            o[...] = v
        for p in range(npar):
            o = o_refs[len(want) + p]

            @pl.when(pl.program_id(1) == 0)
            def _(o=o):
                o[...] = jnp.zeros_like(o)

            o[...] += cts[ni + p]

    in_specs = [pl.BlockSpec((tm, tc), lambda j, i, off=off: (i, off + j)) for _, off in ins]
    in_specs += [pl.BlockSpec((1, tc), lambda j, i, off=off: (0, off + j)) for _, off in params]
    in_specs += [pl.BlockSpec((tm, tc), lambda j, i: (i, j)) for _ in range(nd + na)]
    out_specs = tuple([pl.BlockSpec((tm, tc), lambda j, i: (i, j)) for _ in want]
                      + [pl.BlockSpec((1, tc), lambda j, i: (0, j)) for _ in params])
    out_shape = tuple([jax.ShapeDtypeStruct((t, ncol * tc), F32) for _ in want]
                      + [jax.ShapeDtypeStruct((1, ncol * tc), F32) for _ in params])
    res = _pcall(body, name=name, grid=(ncol, t // tm), in_specs=in_specs, out_specs=out_specs, out_shape=out_shape,
                 compiler_params=_params())(*[a for a, _ in ins], *[p for p, _ in params], *douts, *[adds[k] for k in add_keys])
    return list(res[:len(want)]), list(res[len(want):])


def _rms(x, g):
    return (x * lax.rsqrt(jnp.mean(x * x, axis=-1, keepdims=True) + EPS)) * g


def _f_norm(x, g):
    return (_rms(x, g),)


def _f_silu(c):
    return (jax.nn.silu(c),)


def _f_softplus(d, b):
    return (jax.nn.softplus(d + b),)


def _f_relu2(a):
    return (jnp.square(jnp.maximum(a, 0.0)),)


def _f_add2(a, b):
    return (a + b,)


def _f_add3(a, b, c):
    return (a + b + c,)


def _f_ssd_post(yf, yb, xs, z, dskip, nw):
    u = (yf + yb + dskip * xs) * jax.nn.silu(z)
    return (_rms(u, nw),)


def _neg_expm1(v):
    t = jnp.tanh(0.5 * v)
    return -2.0 * t / (1.0 - t)


def _f_lru_gates(pre_a, pre_x, u, ba, bx, lam):
    rg = jax.nn.sigmoid(pre_a + ba)
    ig = jax.nn.sigmoid(pre_x + bx)
    log_a = -LRU_C * rg * jax.nn.softplus(-lam)
    return jnp.exp(log_a), jnp.sqrt(_neg_expm1(2.0 * log_a)) * (ig * u)


def _f_lru_post(hf, hb, gate):
    return ((hf + hb) * jax.nn.gelu(gate),)


def _f_hgrn_pre(fr, l0, l1):
    lb = jax.nn.sigmoid(l1 - l0)
    k = (1.0 - lb) * jax.nn.sigmoid(-fr)
    return k, jnp.log1p(-k)


def _f_hgrn_post(of, ob, gate, nw):
    return (_rms(of + ob, nw) * jax.nn.silu(gate),)


def _loss_head(x, tgt, g, tm=256):
    t, d = x.shape
    tm = min(tm, t)

    def body(x_ref, t_ref, g_ref, dx_ref, dg_ref, loss_ref):
        tv = t_ref[...]

        def lf(xv, gv):
            return 0.5 * jnp.sum(jnp.mean(jnp.square(_rms(xv, gv) - tv), axis=-1))

        val, vjp = jax.vjp(lf, x_ref[...], g_ref[...])
        dx, dg = vjp(jnp.ones((), F32))
        dx_ref[...] = dx

        @pl.when(pl.program_id(0) == 0)
        def _():
            dg_ref[...] = jnp.zeros_like(dg_ref)
            loss_ref[...] = jnp.zeros_like(loss_ref)

        dg_ref[...] += dg
        loss_ref[...] += jnp.full(loss_ref.shape, val, F32)

    row = pl.BlockSpec((tm, d), lambda i: (i, 0))
    vec = pl.BlockSpec((1, d), lambda i: (0, 0))
    return _pcall(body, name="loss_head", grid=(t // tm,), in_specs=[row, row, vec],
                  out_specs=(row, vec, pl.BlockSpec((1, 128), lambda i: (0, 0))),
                  out_shape=(jax.ShapeDtypeStruct((t, d), F32), jax.ShapeDtypeStruct((1, d), F32),
                             jax.ShapeDtypeStruct((1, 128), F32)), compiler_params=_params())(x, tgt, g)


def _shifted(x, d, prev, nxt, first, last):
    r = x.shape[0]
    row = lax.broadcasted_iota(jnp.int32, x.shape, 0)
    if d < 0:
        out = pltpu.roll(x, -d, 0)
        for q in range(-d):
            pv = jnp.where(first, 0.0, prev[8 + d + q:8 + d + q + 1, :])
            out = jnp.where(row == q, pv, out)
        return out
    out = pltpu.roll(x, r - d, 0)
    for q in range(d):
        nv = jnp.where(last, 0.0, nxt[q:q + 1, :])
        out = jnp.where(row == r - d + q, nv, out)
    return out


def _halo_specs(ts, tc, s):
    nb8 = s // 8
    cur = pl.BlockSpec((None, ts, tc), lambda n, i, j: (n, i, j))
    prev = pl.BlockSpec((None, 8, tc), lambda n, i, j: (n, jnp.maximum(i * (ts // 8) - 1, 0), j))
    nxt = pl.BlockSpec((None, 8, tc), lambda n, i, j: (n, jnp.minimum((i + 1) * (ts // 8), nb8 - 1), j))
    return cur, prev, nxt


def _conv_fwd(p3, w, b, ncol, tc=1024):
    nbatch, s, _ = p3.shape
    ts = min(CONV_ROWS, s)
    nblk = s // ts

    def body(x_ref, pv_ref, nx_ref, w_ref, b_ref, o_ref):
        i = pl.program_id(1)
        first, last = i == 0, i == nblk - 1
        x, pv, nx = x_ref[...], pv_ref[...], nx_ref[...]
        wv = w_ref[...]
        out = b_ref[...] + wv[1:2] * x
        out = out + wv[0:1] * _shifted(x, -1, pv, nx, first, last)
        out = out + wv[2:3] * _shifted(x, 1, pv, nx, first, last)
        out = out + wv[3:4] * _shifted(x, 2, pv, nx, first, last)
        o_ref[...] = out

    cur, prev, nxt = _halo_specs(ts, tc, s)
    return _pcall(body, name="conv_fwd", grid=(nbatch, nblk, ncol),
                  in_specs=[cur, prev, nxt, pl.BlockSpec((4, tc), lambda n, i, j: (0, j)),
                            pl.BlockSpec((1, tc), lambda n, i, j: (0, j))],
                  out_specs=cur, out_shape=jax.ShapeDtypeStruct((nbatch, s, ncol * tc), F32),
                  compiler_params=_params())(p3, p3, p3, w, b)


def _conv_bwd(dc3, p3, w, ncol, tc=1024):
    nbatch, s, _ = dc3.shape
    ts = min(CONV_ROWS, s)
    nblk = s // ts

    def body(d_ref, dpv_ref, dnx_ref, x_ref, pv_ref, nx_ref, w_ref, dx_ref, dw_ref):
        n, i = pl.program_id(1), pl.program_id(2)
        first, last = i == 0, i == nblk - 1
        d, dpv, dnx = d_ref[...], dpv_ref[...], dnx_ref[...]
        x, pv, nx = x_ref[...], pv_ref[...], nx_ref[...]
        wv = w_ref[...]
        dx = wv[1:2] * d
        dx = dx + wv[0:1] * _shifted(d, 1, dpv, dnx, first, last)
        dx = dx + wv[2:3] * _shifted(d, -1, dpv, dnx, first, last)
        dx = dx + wv[3:4] * _shifted(d, -2, dpv, dnx, first, last)
        dx_ref[...] = dx

        @pl.when((n == 0) & (i == 0))
        def _():
            dw_ref[...] = jnp.zeros_like(dw_ref)

        dw_ref[0:1, :] += jnp.sum(d * _shifted(x, -1, pv, nx, first, last), axis=0, keepdims=True)
        dw_ref[1:2, :] += jnp.sum(d * x, axis=0, keepdims=True)
        dw_ref[2:3, :] += jnp.sum(d * _shifted(x, 1, pv, nx, first, last), axis=0, keepdims=True)
        dw_ref[3:4, :] += jnp.sum(d * _shifted(x, 2, pv, nx, first, last), axis=0, keepdims=True)
        dw_ref[4:5, :] += jnp.sum(d, axis=0, keepdims=True)

    nb8 = s // 8
    cur = pl.BlockSpec((None, ts, tc), lambda j, n, i: (n, i, j))
    prev = pl.BlockSpec((None, 8, tc), lambda j, n, i: (n, jnp.maximum(i * (ts // 8) - 1, 0), j))
    nxt = pl.BlockSpec((None, 8, tc), lambda j, n, i: (n, jnp.minimum((i + 1) * (ts // 8), nb8 - 1), j))
    return _pcall(body, name="conv_bwd", grid=(ncol, nbatch, nblk),
                  in_specs=[cur, prev, nxt, cur, prev, nxt, pl.BlockSpec((4, tc), lambda j, n, i: (0, j))],
                  out_specs=(cur, pl.BlockSpec((8, tc), lambda j, n, i: (0, j))),
                  out_shape=(jax.ShapeDtypeStruct((nbatch, s, ncol * tc), F32), jax.ShapeDtypeStruct((8, ncol * tc), F32)),
                  compiler_params=_params())(dc3, dc3, dc3, p3, p3, p3, w)


def _block_scan(coef, inp, reverse):
    r = coef.shape[0]
    row = lax.broadcasted_iota(jnp.int32, coef.shape, 0)
    a, b = coef, inp
    d = 1
    while d < r:
        if reverse:
            keep = row < r - d
            a_sh, b_sh = pltpu.roll(a, r - d, 0), pltpu.roll(b, r - d, 0)
        else:
            keep = row >= d
            a_sh, b_sh = pltpu.roll(a, d, 0), pltpu.roll(b, d, 0)
        b = b + a * jnp.where(keep, b_sh, 0.0)
        a = a * jnp.where(keep, a_sh, 1.0)
        d *= 2
    return a, b


def _lru_scan(a3, b3, reverse):
    nbatch, s, w = a3.shape
    ts = min(LRU_ROWS, s)
    nblk = s // ts
    edge = 0 if reverse else ts - 1

    def body(a_ref, b_ref, h_ref, carry):
        @pl.when(pl.program_id(1) == 0)
        def _():
            carry[...] = jnp.zeros_like(carry)

        ca, hb = _block_scan(a_ref[...], b_ref[...], reverse)
        h = hb + ca * carry[0:1, :]
        h_ref[...] = h
        carry[0:1, :] = h[edge:edge + 1, :]

    blk = pl.BlockSpec((None, ts, w), (lambda n, i: (n, nblk - 1 - i, 0)) if reverse else (lambda n, i: (n, i, 0)))
    return _pcall(body, name=f"lru_scan_r{int(reverse)}", grid=(nbatch, nblk), in_specs=[blk, blk], out_specs=blk,
                  out_shape=jax.ShapeDtypeStruct((nbatch, s, w), F32), scratch_shapes=[pltpu.VMEM((8, w), F32)],
                  compiler_params=_params())(a3, b3)


def _lru_scan_bwd(a3, h3, dh3, reverse):
    nbatch, s, w = a3.shape
    ts = min(LRU_ROWS, s)
    nblk = s // ts
    nb8 = s // 8
    tpb = ts // 8

    def body(a_ref, aa_ref, h_ref, hh_ref, dh_ref, g_ref, da_ref, carry):
        i = pl.program_id(1)

        @pl.when(i == 0)
        def _():
            carry[...] = jnp.zeros_like(carry)

        a, h = a_ref[...], h_ref[...]
        row = lax.broadcasted_iota(jnp.int32, a.shape, 0)
        if reverse:
            a_edge = jnp.where(i == 0, 0.0, aa_ref[7:8, :])
            c = jnp.where(row == 0, a_edge, pltpu.roll(a, 1, 0))
            h_edge = jnp.where(i == nblk - 1, 0.0, hh_ref[0:1, :])
            h_sh = jnp.where(row == ts - 1, h_edge, pltpu.roll(h, ts - 1, 0))
        else:
            a_edge = jnp.where(i == 0, 0.0, aa_ref[0:1, :])
            c = jnp.where(row == ts - 1, a_edge, pltpu.roll(a, ts - 1, 0))
            h_edge = jnp.where(i == nblk - 1, 0.0, hh_ref[7:8, :])
            h_sh = jnp.where(row == 0, h_edge, pltpu.roll(h, 1, 0))
        cc, gb = _block_scan(c, dh_ref[...], not reverse)
        g = gb + cc * carry[0:1, :]
        g_ref[...] = g
        carry[0:1, :] = g[ts - 1:ts, :] if reverse else g[0:1, :]
        da_ref[...] = g * h_sh

    if reverse:
        bi = lambda i: i
    else:
        bi = lambda i: nblk - 1 - i
    blk = pl.BlockSpec((None, ts, w), lambda n, i: (n, bi(i), 0))
    before = pl.BlockSpec((None, 8, w), lambda n, i: (n, jnp.maximum(bi(i) * tpb - 1, 0), 0))
    after = pl.BlockSpec((None, 8, w), lambda n, i: (n, jnp.minimum((bi(i) + 1) * tpb, nb8 - 1), 0))
    a_tile, h_tile = (before, after) if reverse else (after, before)
    return _pcall(body, name=f"lru_scan_bwd_r{int(reverse)}", grid=(nbatch, nblk), in_specs=[blk, a_tile, blk, h_tile, blk],
                  out_specs=(blk, blk),
                  out_shape=(jax.ShapeDtypeStruct((nbatch, s, w), F32), jax.ShapeDtypeStruct((nbatch, s, w), F32)),
                  scratch_shapes=[pltpu.VMEM((8, w), F32)], compiler_params=_params())(a3, a3, h3, h3, dh3)


def _head_expand(lane0):
    return (jnp.right_shift(lax.broadcasted_iota(jnp.int32, (128, 1024), 1), HEAD_SHIFT) + lane0
            == lax.broadcasted_iota(jnp.int32, (128, 1024), 0)).astype(F32)


def _head_reduce(lane0):
    return (jnp.right_shift(lax.broadcasted_iota(jnp.int32, (1024, 128), 0), HEAD_SHIFT) + lane0
            == lax.broadcasted_iota(jnp.int32, (1024, 128), 1)).astype(F32)


def _time_mask(q, reverse):
    ri = lax.broadcasted_iota(jnp.int32, (q, q), 0)
    ci = lax.broadcasted_iota(jnp.int32, (q, q), 1)
    return (ri <= ci) if reverse else (ri >= ci)


def _ssd_common(xs_ref, bc_ref, dt_ref, al_ref, reverse, lane0):
    q = xs_ref.shape[0]
    edge = 0 if reverse else q - 1
    dt = dt_ref[...]
    a = -jnp.exp(al_ref[...])
    mask = _time_mask(q, reverse)
    expand = _head_expand(lane0)
    cum = _dot(mask.astype(F32), dt * a, precision=HI)
    cum_x = _dot(cum, expand, precision=HI)
    dt_x = _dot(dt, expand, precision=HI)
    last_x = cum_x[edge:edge + 1, :]
    xs = xs_ref[...]
    bc = bc_ref[...]
    return dict(q=q, edge=edge, lane0=lane0, dt=dt, a=a, mask=mask, cum_t=cum.T, cum_x=cum_x, dt_x=dt_x, xs=xs,
                v=xs * dt_x, e_c=jnp.exp(cum_x), w=jnp.exp(last_x - cum_x), e_l=jnp.exp(last_x),
                bm=bc[:, :512], cm=bc[:, 512:])


def _ssd_decay(c, h):
    row = c["lane0"] + h
    seg = c["cum_x"][:, h * SSD_HEADDIM:h * SSD_HEADDIM + 1] - c["cum_t"][row:row + 1, :]
    return jnp.where(c["mask"], jnp.exp(jnp.minimum(seg, 0.0)), 0.0)


def _head_masks():
    lane = jnp.right_shift(lax.broadcasted_iota(jnp.int32, (1, 256), 1), HEAD_SHIFT)
    return [lane == e for e in range(4)]


def _ssd_fwd(xbc3, dt3, alog, reverse):
    nbatch, s, _ = xbc3.shape
    q = min(SSD_CHUNK, s)
    nc = s // q
    lane0 = SSD_HEADS * int(reverse)

    def body(xs_ref, bc_ref, dt_ref, al_ref, y_ref, st_ref, st):
        @pl.when(pl.program_id(1) == 0)
        def _():
            st[...] = jnp.zeros_like(st)

        st_ref[...] = st[...]
        c = _ssd_common(xs_ref, bc_ref, dt_ref, al_ref, reverse, lane0)
        hm = _head_masks()
        for g in range(SSD_GROUPS):
            sl = slice(g * 256, (g + 1) * 256)
            cg, bg = _mx(c["cm"][:, g * 128:(g + 1) * 128]), _mx(c["bm"][:, g * 128:(g + 1) * 128])
            cb = _dot(cg, bg, _NT)
            vg = c["v"][:, sl]
            s0 = st[:, sl]
            yg = _dot(cg, _mx(s0)) * c["e_c"][:, sl]
            for e in range(4):
                m = _ssd_decay(c, 4 * g + e) * cb
                yg = yg + _dot(_mx(m), _mx(jnp.where(hm[e], vg, 0.0)))
            y_ref[:, sl] = yg
            st[:, sl] = c["e_l"][:, sl] * s0 + _dot(bg, _mx(vg * c["w"][:, sl]), _TN)

    ck = (lambda i: nc - 1 - i) if reverse else (lambda i: i)
    xs_spec = pl.BlockSpec((None, q, 1024), lambda n, i: (n, ck(i), 0))
    bc_spec = pl.BlockSpec((None, q, 1024), lambda n, i: (n, ck(i), 1))
    dt_spec = pl.BlockSpec((None, q, 128), lambda n, i: (n, ck(i), 0))
    al_spec = pl.BlockSpec((1, 128), lambda n, i: (0, 0))
    st_spec = pl.BlockSpec((None, None, 128, 1024), lambda n, i: (n, ck(i), 0, 0))
    return _pcall(body, name=f"ssd_fwd_r{int(reverse)}", grid=(nbatch, nc), in_specs=[xs_spec, bc_spec, dt_spec, al_spec],
                  out_specs=(xs_spec, st_spec),
                  out_shape=(jax.ShapeDtypeStruct((nbatch, s, 1024), F32), jax.ShapeDtypeStruct((nbatch, nc, 128, 1024), F32)),
                  scratch_shapes=[pltpu.VMEM((128, 1024), F32)], compiler_params=_params())(xbc3, xbc3, dt3, alog)


def _ssd_bwd(xbc3, dt3, alog, st4, dy3, reverse):
    nbatch, s, _ = xbc3.shape
    q = min(SSD_CHUNK, s)
    nc = s // q
    lane0 = SSD_HEADS * int(reverse)

    def body(xs_ref, bc_ref, dt_ref, al_ref, st0_ref, dy_ref, dxs_ref, dbc_ref, ddt_ref, dal_ref, dst):
        n, i = pl.program_id(0), pl.program_id(1)

        @pl.when(i == 0)
        def _():
            dst[...] = jnp.zeros_like(dst)

        @pl.when((i == 0) & (n == 0))
        def _():
            dal_ref[...] = jnp.zeros_like(dal_ref)

        c = _ssd_common(xs_ref, bc_ref, dt_ref, al_ref, reverse, lane0)
        hm = _head_masks()
        reduce_m = _head_reduce(lane0)
        s0_all, ds1_all, dy = st0_ref[...], dst[...], dy_ref[...]
        lane = lax.broadcasted_iota(jnp.int32, (q, 128), 1)
        sub = lax.broadcasted_iota(jnp.int32, (128, q), 0)
        rowacc = jnp.zeros((q, 128), F32)
        colacc_t = jnp.zeros((128, q), F32)
        dv_l, yst_l, dvbar_l, dk_l, dc_l = [], [], [], [], []
        for g in range(SSD_GROUPS):
            sl = slice(g * 256, (g + 1) * 256)
            cg, bg = _mx(c["cm"][:, g * 128:(g + 1) * 128]), _mx(c["bm"][:, g * 128:(g + 1) * 128])
            cb = _dot(cg, bg, _NT)
            vg, dyg, wg, ecg = c["v"][:, sl], dy[:, sl], c["w"][:, sl], c["e_c"][:, sl]
            s0, ds1 = _mx(s0_all[:, sl]), _mx(ds1_all[:, sl])
            dye = _mx(dyg * ecg)
            yst_l.append(_dot(cg, s0) * ecg)
            dcg = _dot(dye, s0, _NT)
            dst[:, sl] = c["e_l"][:, sl] * ds1_all[:, sl] + _dot(cg, dye, _TN)
            vbar = _mx(vg * wg)
            dvbar = _dot(bg, ds1)
            dvbar_l.append(dvbar)
            dvg = dvbar * wg
            dkg = _dot(vbar, ds1, _NT)
            for e in range(4):
                h = 4 * g + e
                m = _ssd_decay(c, h)
                dyh, vh = _mx(jnp.where(hm[e], dyg, 0.0)), _mx(jnp.where(hm[e], vg, 0.0))
                dvg = dvg + _dot(_mx(m * cb), dyh, _TN)
                dcb = _dot(dyh, vh, _NT) * m
                dcbb = _mx(dcb)
                dcg = dcg + _dot(dcbb, bg)
                dkg = dkg + _dot(dcbb, cg, _TN)
                wmat = dcb * cb
                rowacc = jnp.where(lane == lane0 + h, jnp.sum(wmat, axis=1, keepdims=True), rowacc)
                colacc_t = jnp.where(sub == lane0 + h, jnp.sum(wmat, axis=0, keepdims=True), colacc_t)
            dv_l.append(dvg)
            dk_l.append(dkg)
            dc_l.append(dcg)
        dv = jnp.concatenate(dv_l, axis=1)
        yst = jnp.concatenate(yst_l, axis=1)
        dvbar = jnp.concatenate(dvbar_l, axis=1)
        t1 = _dot(dy * yst, reduce_m, precision=HI)
        t2 = _dot(c["v"] * c["w"] * dvbar, reduce_m, precision=HI)
        dlast = jnp.sum(t2, axis=0, keepdims=True) + _dot(
            c["e_l"] * jnp.sum(ds1_all * s0_all, axis=0, keepdims=True), reduce_m, precision=HI)
        dcum = rowacc - colacc_t.T + t1 - t2
        dcum = dcum + jnp.where(lax.broadcasted_iota(jnp.int32, (q, 128), 0) == c["edge"], dlast, 0.0)
        dda = _dot(c["mask"].astype(F32), dcum, _TN, precision=HI)
        ddt_ref[...] = dda * c["a"] + _dot(dv * c["xs"], reduce_m, precision=HI)
        dal_ref[...] += jnp.sum(dda * c["dt"], axis=0, keepdims=True) * c["a"]
        dxs_ref[...] = dv * c["dt_x"]
        dbc_ref[...] = jnp.concatenate(dk_l + dc_l, axis=1)

    ck = (lambda i: i) if reverse else (lambda i: nc - 1 - i)
    xs_spec = pl.BlockSpec((None, q, 1024), lambda n, i: (n, ck(i), 0))
    bc_spec = pl.BlockSpec((None, q, 1024), lambda n, i: (n, ck(i), 1))
    dt_spec = pl.BlockSpec((None, q, 128), lambda n, i: (n, ck(i), 0))
    al_spec = pl.BlockSpec((1, 128), lambda n, i: (0, 0))
    st_spec = pl.BlockSpec((None, None, 128, 1024), lambda n, i: (n, ck(i), 0, 0))
    return _pcall(body, name=f"ssd_bwd_r{int(reverse)}", grid=(nbatch, nc),
                  in_specs=[xs_spec, bc_spec, dt_spec, al_spec, st_spec, xs_spec],
                  out_specs=(xs_spec, xs_spec, dt_spec, al_spec),
                  out_shape=(jax.ShapeDtypeStruct((nbatch, s, 1024), F32), jax.ShapeDtypeStruct((nbatch, s, 1024), F32),
                             jax.ShapeDtypeStruct((nbatch, s, 128), F32), jax.ShapeDtypeStruct((1, 128), F32)),
                  scratch_shapes=[pltpu.VMEM((128, 1024), F32)], compiler_params=_params())(xbc3, xbc3, dt3, alog, st4, dy3)


def _gla_sub(q_ref, k_ref, g_ref, rs, reverse):
    sq = HGRN_SUB
    edge = 0 if reverse else sq - 1
    mask = _time_mask(sq, reverse)
    bc = _dot(mask.astype(F32), g_ref[rs, :], precision=HI)
    last = bc[edge:edge + 1, :]
    eb, enb, elb = jnp.exp(bc), jnp.exp(-bc), jnp.exp(last - bc)
    kv = k_ref[rs, :]
    return dict(mask=mask, edge=edge, eb=eb, enb=enb, elb=elb, e_l=jnp.exp(last),
                qt=q_ref[rs, :] * HGRN_SCALE * eb, kt=kv * enb, kb=kv * elb)


def _gla_specs(s, w, reverse_order):
    bq = min(HGRN_BLOCK, s)
    nblk = s // bq
    bi = (lambda i: nblk - 1 - i) if reverse_order else (lambda i: i)
    col = lambda cb: pl.BlockSpec((None, bq, w), lambda n, i: (n, bi(i), cb))
    st_spec = pl.BlockSpec((None, bq // HGRN_SUB, 128, w), lambda n, i: (n, bi(i), 0, 0))
    return bq, nblk, col, st_spec


def _gla_fwd(proj3, k3, g3, reverse):
    nbatch, s, w = k3.shape
    bq, nblk, col, st_spec = _gla_specs(s, w, reverse)
    nsub = bq // HGRN_SUB

    def body(q_ref, k_ref, v_ref, g_ref, o_ref, st_ref, st):
        @pl.when(pl.program_id(1) == 0)
        def _():
            st[...] = jnp.zeros_like(st)

        for j in (reversed(range(nsub)) if reverse else range(nsub)):
            rs = slice(j * HGRN_SUB, (j + 1) * HGRN_SUB)
            st_ref[j] = st[...]
            c = _gla_sub(q_ref, k_ref, g_ref, rs, reverse)
            v = v_ref[rs, :]
            for h in range(HGRN_HEADS):
                hs = slice(h * 128, (h + 1) * 128)
                qt, vb = _mx(c["qt"][:, hs]), _mx(v[:, hs])
                att = jnp.where(c["mask"], _dot(qt, _mx(c["kt"][:, hs]), _NT), 0.0)
                s0 = st[:, hs]
                o_ref[rs, hs] = _dot(_mx(att), vb) + _dot(qt, _mx(s0), _NT)
                st[:, hs] = s0 * c["e_l"][:, hs] + _dot(vb, _mx(c["kb"][:, hs]), _TN)

    return _pcall(body, name=f"gla_fwd_r{int(reverse)}", grid=(nbatch, nblk), in_specs=[col(0), col(0), col(3), col(0)],
                  out_specs=(col(0), st_spec),
                  out_shape=(jax.ShapeDtypeStruct((nbatch, s, w), F32),
                             jax.ShapeDtypeStruct((nbatch, s // HGRN_SUB, 128, w), F32)),
                  scratch_shapes=[pltpu.VMEM((128, w), F32)], compiler_params=_params())(proj3, k3, proj3, g3)


def _gla_bwd(proj3, k3, g3, st4, do3, reverse):
    nbatch, s, w = k3.shape
    bq, nblk, col, st_spec = _gla_specs(s, w, not reverse)
    nsub = bq // HGRN_SUB
    sq = HGRN_SUB

    def body(q_ref, k_ref, v_ref, g_ref, st_ref, do_ref, dq_ref, dk_ref, dv_ref, dg_ref, dst):
        @pl.when(pl.program_id(1) == 0)
        def _():
            dst[...] = jnp.zeros_like(dst)

        row = lax.broadcasted_iota(jnp.int32, (sq, 128), 0)
        for j in (range(nsub) if reverse else reversed(range(nsub))):
            rs = slice(j * sq, (j + 1) * sq)
            c = _gla_sub(q_ref, k_ref, g_ref, rs, reverse)
            s0_all, ds1_all = st_ref[j], dst[...]
            v, dy = v_ref[rs, :], do_ref[rs, :]
            db_l = []
            for h in range(HGRN_HEADS):
                hs = slice(h * 128, (h + 1) * 128)
                qt, kt, kb = c["qt"][:, hs], c["kt"][:, hs], c["kb"][:, hs]
                qtb, ktb, kbb, vb, dyb = _mx(qt), _mx(kt), _mx(kb), _mx(v[:, hs]), _mx(dy[:, hs])
                s0, ds1 = s0_all[:, hs], ds1_all[:, hs]
                att = jnp.where(c["mask"], _dot(qtb, ktb, _NT), 0.0)
                datt = _mx(jnp.where(c["mask"], _dot(dyb, vb, _NT), 0.0))
                dqt = _dot(datt, ktb) + _dot(dyb, _mx(s0))
                dkt = _dot(datt, qtb, _TN)
                dkb = _dot(vb, _mx(ds1))
                dv_ref[rs, hs] = _dot(_mx(att), dyb, _TN) + _dot(kbb, _mx(ds1), _NT)
                dst[:, hs] = c["e_l"][:, hs] * ds1 + _dot(dyb, qtb, _TN)
                dq_ref[rs, hs] = dqt * c["eb"][:, hs] * HGRN_SCALE
                dk_ref[rs, hs] = dkt * c["enb"][:, hs] + dkb * c["elb"][:, hs]
                kbk = dkb * kb
                dlast = jnp.sum(kbk, axis=0, keepdims=True) + c["e_l"][:, hs] * jnp.sum(ds1 * s0, axis=0, keepdims=True)
                db_l.append(dqt * qt - dkt * kt - kbk + jnp.where(row == c["edge"], dlast, 0.0))
            dg_ref[rs, :] = _dot(c["mask"].astype(F32), jnp.concatenate(db_l, axis=1), _TN, precision=HI)

    shp = jax.ShapeDtypeStruct((nbatch, s, w), F32)
    return _pcall(body, name=f"gla_bwd_r{int(reverse)}", grid=(nbatch, nblk),
                  in_specs=[col(0), col(0), col(3), col(0), st_spec, col(0)],
                  out_specs=(col(0),) * 4, out_shape=(shp,) * 4,
                  scratch_shapes=[pltpu.VMEM((128, w), F32)], compiler_params=_params())(proj3, k3, proj3, g3, st4, do3)


DIRS = (False, True)


def _block_diag(w):
    eye = jnp.eye(16, dtype=w.dtype)
    return (eye[:, None, :, None] * w[:, :, None, :]).reshape(1024, 1024)


def _diag_blocks(m):
    m4 = m.reshape(16, 64, 16, 64)
    return jnp.stack([m4[i, :, i, :] for i in range(16)], axis=0)


def _pad_lanes(v, n=128):
    return jnp.pad(v, [(0, 0)] * (v.ndim - 1) + [(0, n - v.shape[-1])])


def _mlp_fwd(tag, x, nw, w1, w2):
    (h,) = _pw_fwd(f"{tag}_norm", _f_norm, [(x, 0)], [(nw, 0)], [BF16], 1024, 1)
    a, r = _mm(f"{tag}_up", h, w1, "nn", relu2=True)
    return _mm(f"{tag}_down", r, w2, "nn", res=x), (h, a, r)


def _mlp_bwd(tag, x, nw, w1, w2, saved, dxo):
    h, a, r = saved
    dw2 = _mm(f"{tag}_dw2", r, dxo, "tn")
    dr = _mm(f"{tag}_dr", dxo, w2, "nt")
    (da,), _ = _pw_bwd(f"{tag}_dact", _f_relu2, [(a, 0)], [], [dr], 1024, a.shape[1] // 1024, [0])
    dw1 = _mm(f"{tag}_dw1", h, da, "tn")
    dh = _mm(f"{tag}_dh", da, w1, "nt")
    (dx,), (dnw,) = _pw_bwd(f"{tag}_dnorm", _f_norm, [(x, 0)], [(nw, 0)], [dh], 1024, 1, [0], adds={0: dxo})
    return dx, dw1, dw2, dnw


def _local_step(x3, tgt3, w):
    nb, s, d = x3.shape
    t = nb * s
    x0 = x3.reshape(t, d)
    tgt = tgt3.reshape(t, d)
    grads = {}
    row = lambda v: v.reshape(1, -1)
    to3 = lambda v: v.reshape(nb, s, v.shape[-1])
    to2 = lambda v: v.reshape(-1, v.shape[-1])

    w_in0 = w["even_w_in"][0]
    w_main0 = w_in0[:, :5120]
    w_dt0 = _pad_lanes(w_in0[:, 5120:])
    conv_w, conv_b = w["even_conv_w"][0], row(w["even_conv_b"][0])
    nmix0 = row(w["norm_mix"][0])
    (h0,) = _pw_fwd("l0_norm", _f_norm, [(x0, 0)], [(nmix0, 0)], [BF16], 1024, 1)
    proj0 = _mm("l0_proj", h0, w_main0, "nn")
    dt_raw = _mm("l0_proj_dt", h0, w_dt0, "nn")
    conv = to2(_conv_fwd(to3(proj0), conv_w, conv_b, 3))
    (xbc,) = _pw_fwd("l0_silu", _f_silu, [(conv, 0)], [], [F32], 1024, 2)
    dt_bias = _pad_lanes(w["ssd_dt_bias"][0].reshape(1, 32))
    (dt,) = _pw_fwd("l0_dt", _f_softplus, [(dt_raw, 0)], [(dt_bias, 0)], [F32], 128, 1)
    dt3, xbc3 = to3(dt), to3(xbc)
    alog = _pad_lanes(w["ssd_a_log"][0].reshape(1, 32))
    ssd = [_ssd_fwd(xbc3, dt3, alog, r) for r in DIRS]
    yf, yb = to2(ssd[0][0]), to2(ssd[1][0])
    dskip = jnp.repeat(w["ssd_d"][0], SSD_HEADDIM).reshape(1, 1024)
    snw = row(w["ssd_norm_w"][0])
    ssd_ins = [(yf, 0), (yb, 0), (xbc, 0), (proj0, 12)]
    (ya,) = _pw_fwd("l0_ssd_post", _f_ssd_post, ssd_ins, [(dskip, 0), (snw, 0)], [BF16], 256, 4)
    u_lru = conv[:, 2048:]
    w_gates = [_block_diag(w[k][0, r]).astype(MXU_DTYPE) for r in range(2) for k in ("lru_w_a", "lru_w_x")]
    pre = [_mm(f"l0_lru_pre{i}", u_lru, wg, "nn") for i, wg in enumerate(w_gates)]
    lru_par = [[(row(w[k][0, r]), 0) for k in ("lru_b_a", "lru_b_x", "lru_lambda")] for r in range(2)]
    lru_ins = [[(pre[2 * r], 0), (pre[2 * r + 1], 0), (u_lru, 0)] for r in range(2)]
    ab = [_pw_fwd(f"l0_lru_gates{r}", _f_lru_gates, lru_ins[r], lru_par[r], [F32, F32], 1024, 1) for r in range(2)]
    hs = [_lru_scan(to3(ab[r][0]), to3(ab[r][1]), DIRS[r]) for r in range(2)]
    lru_post_ins = [(to2(hs[0]), 0), (to2(hs[1]), 0), (proj0, 4)]
    (ybm,) = _pw_fwd("l0_lru_post", _f_lru_post, lru_post_ins, [], [BF16], 1024, 1)
    w_out0 = w["even_w_out"][0]
    x1 = _mm("l0_out_a", ya, w_out0[:1024], "nn", res=x0)
    x1 = _mm("l0_out_b", ybm, w_out0[1024:], "nn", res=x1)
    nmlp0 = row(w["norm_mlp"][0])
    x2, mlp0 = _mlp_fwd("l0_mlp", x1, nmlp0, w["mlp_w1"][0], w["mlp_w2"][0])

    w_in1 = w["odd_w_in"][0]
    nmix1 = row(w["norm_mix"][1])
    (h1,) = _pw_fwd("l1_norm", _f_norm, [(x2, 0)], [(nmix1, 0)], [BF16], 1024, 1)
    proj1 = _mm("l1_proj", h1, w_in1, "nn")
    proj1_3 = to3(proj1)
    lb0, lb1 = row(w["hgrn_lb_logits"][0]), row(w["hgrn_lb_logits"][1])
    kg = [_pw_fwd(f"l1_hgrn_pre{r}", _f_hgrn_pre, [(proj1, 1 + r)], [(lb0, 0), (lb1, 0)], [F32, F32], 1024, 1)
          for r in range(2)]
    gla = [_gla_fwd(proj1_3, to3(kg[r][0]), to3(kg[r][1]), DIRS[r]) for r in range(2)]
    hnw = row(w["hgrn_norm_w"][0])
    hpost_ins = [(to2(gla[0][0]), 0), (to2(gla[1][0]), 0), (proj1, 32)]
    (yo,) = _pw_fwd("l1_hgrn_post", _f_hgrn_post, hpost_ins, [(hnw, 0)], [BF16], 128, 8)
    w_out1 = w["odd_w_out"][0]
    x3_ = _mm("l1_out", yo, w_out1, "nn", res=x2)
    nmlp1 = row(w["norm_mlp"][1])
    x4, mlp1 = _mlp_fwd("l1_mlp", x3_, nmlp1, w["mlp_w1"][1], w["mlp_w2"][1])

    dx4, dnf, loss = _loss_head(x4, tgt, row(w["norm_final"]))
    grads["norm_final"] = dnf.reshape(-1)

    dx3, dw1_1, dw2_1, dnmlp1 = _mlp_bwd("l1_mlp", x3_, nmlp1, w["mlp_w1"][1], w["mlp_w2"][1], mlp1, dx4)
    grads["odd_w_out"] = _mm("l1_dwout", yo, dx3, "tn")[None]
    dyo = _mm("l1_dyo", dx3, w_out1, "nt")
    (do, dgate1), (dhnw,) = _pw_bwd("l1_hgrn_post_b", _f_hgrn_post, hpost_ins, [(hnw, 0)], [dyo], 128, 8, [0, 2])
    grads["hgrn_norm_w"] = dhnw
    do3 = to3(do)
    gb = [_gla_bwd(proj1_3, to3(kg[r][0]), to3(kg[r][1]), gla[r][1], do3, DIRS[r]) for r in range(2)]
    (dq,) = _pw_fwd("l1_dq", _f_add2, [(to2(gb[0][0]), 0), (to2(gb[1][0]), 0)], [], [F32], 1024, 1)
    (dvv,) = _pw_fwd("l1_dv", _f_add2, [(to2(gb[0][2]), 0), (to2(gb[1][2]), 0)], [], [F32], 1024, 1)
    dfr, dl0, dl1 = [], [], []
    for r in range(2):
        (df,), (a0, a1) = _pw_bwd(f"l1_hgrn_pre_b{r}", _f_hgrn_pre, [(proj1, 1 + r)], [(lb0, 0), (lb1, 0)],
                                  [to2(gb[r][1]), to2(gb[r][3])], 1024, 1, [0])
        dfr.append(df)
        dl0.append(a0)
        dl1.append(a1)
    grads["hgrn_lb_logits"] = jnp.concatenate([dl0[0] + dl0[1], dl1[0] + dl1[1]], axis=0)
    dparts1 = [dq, dfr[0], dfr[1], dvv, dgate1]
    grads["odd_w_in"] = jnp.concatenate([_mm(f"l1_dwin{i}", h1, dp, "tn") for i, dp in enumerate(dparts1)], axis=1)[None]
    dh1 = None
    for i, dp in enumerate(dparts1):
        dh1 = _mm(f"l1_dh{i}", dp, w_in1[:, i * 1024:(i + 1) * 1024], "nt", res=dh1)
    (dx2,), (dnmix1,) = _pw_bwd("l1_dnorm", _f_norm, [(x2, 0)], [(nmix1, 0)], [dh1], 1024, 1, [0], adds={0: dx3})

    dx1, dw1_0, dw2_0, dnmlp0 = _mlp_bwd("l0_mlp", x1, nmlp0, w["mlp_w1"][0], w["mlp_w2"][0], mlp0, dx2)
    grads["mlp_w1"] = jnp.stack([dw1_0, dw1_1])
    grads["mlp_w2"] = jnp.stack([dw2_0, dw2_1])
    grads["norm_mlp"] = jnp.concatenate([dnmlp0, dnmlp1], axis=0)
    grads["even_w_out"] = jnp.concatenate([_mm("l0_dwout_a", ya, dx1, "tn"), _mm("l0_dwout_b", ybm, dx1, "tn")], axis=0)[None]
    dya = _mm("l0_dya", dx1, w_out0[:1024], "nt")
    dyb = _mm("l0_dyb", dx1, w_out0[1024:], "nt")
    (dh, dgate0), _ = _pw_bwd("l0_lru_post_b", _f_lru_post, lru_post_ins, [], [dyb], 1024, 1, [0, 2])
    dh3 = to3(dh)
    dpre, du_parts, dlru = [], [], {k: [] for k in ("lru_b_a", "lru_b_x", "lru_lambda")}
    for r in range(2):
        g_r, da_r = _lru_scan_bwd(to3(ab[r][0]), hs[r], dh3, DIRS[r])
        (dpa, dpx, du_r), (dba, dbx, dlam) = _pw_bwd(f"l0_lru_gates_b{r}", _f_lru_gates, lru_ins[r], lru_par[r],
                                                     [to2(da_r), to2(g_r)], 1024, 1, [0, 1, 2])
        dpre += [dpa, dpx]
        du_parts.append(du_r)
        dlru["lru_b_a"].append(dba)
        dlru["lru_b_x"].append(dbx)
        dlru["lru_lambda"].append(dlam)
    for k, v in dlru.items():
        grads[k] = jnp.concatenate(v, axis=0)[None]
    dwg = [_diag_blocks(_mm(f"l0_dwgate{i}", u_lru, dp, "tn")) for i, dp in enumerate(dpre)]
    grads["lru_w_a"] = jnp.stack([dwg[0], dwg[2]])[None]
    grads["lru_w_x"] = jnp.stack([dwg[1], dwg[3]])[None]
    (du,) = _pw_fwd("l0_du", _f_add2, [(du_parts[0], 0), (du_parts[1], 0)], [], [F32], 1024, 1)
    for i, dp in enumerate(dpre):
        du = _mm(f"l0_du_gate{i}", dp, w_gates[i], "nt", res=du)
    (dy, dxs_skip, dz), (ddskip, dsnw) = _pw_bwd("l0_ssd_post_b", _f_ssd_post, ssd_ins, [(dskip, 0), (snw, 0)], [dya],
                                                 256, 4, [0, 2, 3])
    grads["ssd_d"] = ddskip.reshape(SSD_HEADS, SSD_HEADDIM).sum(axis=1)[None]
    grads["ssd_norm_w"] = dsnw
    dy3 = to3(dy)
    sb = [_ssd_bwd(xbc3, dt3, alog, ssd[r][1], dy3, DIRS[r]) for r in range(2)]
    grads["ssd_a_log"] = (sb[0][3] + sb[1][3])[:, :32].reshape(1, 2, 16)
    (dxs,) = _pw_fwd("l0_dxs", _f_add3, [(to2(sb[0][0]), 0), (to2(sb[1][0]), 0), (dxs_skip, 0)], [], [F32], 1024, 1)
    (dbc,) = _pw_fwd("l0_dbc", _f_add2, [(to2(sb[0][1]), 0), (to2(sb[1][1]), 0)], [], [F32], 1024, 1)
    dxbc = jnp.concatenate([dxs, dbc], axis=1)
    (dconv_a,), _ = _pw_bwd("l0_silu_b", _f_silu, [(conv, 0)], [], [dxbc], 1024, 2, [0])
    (ddt,) = _pw_fwd("l0_ddt", _f_add2, [(to2(sb[0][2]), 0), (to2(sb[1][2]), 0)], [], [F32], 128, 1)
    (ddt_raw,), (ddtb,) = _pw_bwd("l0_dt_b", _f_softplus, [(dt_raw, 0)], [(dt_bias, 0)], [ddt], 128, 1, [0])
    grads["ssd_dt_bias"] = ddtb[:, :32].reshape(1, 2, 16)
    dconv = jnp.concatenate([dconv_a, du], axis=1)
    dproj_c, dcw = _conv_bwd(to3(dconv), to3(proj0), conv_w, 3)
    grads["even_conv_w"] = dcw[:4][None]
    grads["even_conv_b"] = dcw[4:5]
    dparts0 = [to2(dproj_c)[:, :1024], to2(dproj_c)[:, 1024:2048], to2(dproj_c)[:, 2048:], dz, dgate0]
    dwin0 = [_mm(f"l0_dwin{i}", h0, dp, "tn") for i, dp in enumerate(dparts0)]
    dwin0.append(_mm("l0_dwin_dt", h0, ddt_raw, "tn")[:, :32])
    grads["even_w_in"] = jnp.concatenate(dwin0, axis=1)[None]
    dh0 = _mm("l0_dh_dt", ddt_raw, w_dt0, "nt")
    for i, dp in enumerate(dparts0):
        dh0 = _mm(f"l0_dh{i}", dp, w_main0[:, i * 1024:(i + 1) * 1024], "nt", res=dh0)
    (dx0,), (dnmix0,) = _pw_bwd("l0_dnorm", _f_norm, [(x0, 0)], [(nmix0, 0)], [dh0], 1024, 1, [0], adds={0: dx1})
    grads["norm_mix"] = jnp.concatenate([dnmix0, dnmix1], axis=0)
    return loss, dx0.reshape(nb, s, d), grads


ANY = pl.BlockSpec(memory_space=pl.ANY)


def _place():
    return lax.axis_index("x"), lax.axis_index("y"), lax.axis_index("c")


def _gather_chips(name, shard):
    rows, cols = shard.shape
    half = rows // 2

    def body(x_ref, out_ref, send_sems, recv_sems, local_sem):
        x, y, c = _place()
        sibling = (x, y, 1 - c)
        chips = [(1 - x, y), (x, 1 - y), (1 - x, 1 - y)]

        def blk(px, py, hc):
            return out_ref.at[2 * px + py, pl.ds(hc * half, half), :]

        def copy(k, src, dst, to):
            return pltpu.make_async_remote_copy(src_ref=src, dst_ref=dst, send_sem=send_sems.at[k],
                                                recv_sem=recv_sems.at[k], device_id=to, device_id_type=MESH)

        mine = pltpu.make_async_copy(x_ref, out_ref.at[2 * x + y], local_sem)
        mine.start()
        src = x_ref.at[pl.ds(c * half, half), :]
        first = [copy(j, src, blk(x, y, c), (*chip, c)) for j, chip in enumerate(chips)]
        for cp in first:
            cp.start()
        passed = [copy(3 + j, blk(*chip, c), blk(*chip, c), sibling) for j, chip in enumerate(chips)]
        for j, chip in enumerate(chips):
            copy(j, src, blk(*chip, c), (*chip, c)).wait_recv()
            passed[j].start()
        for j, chip in enumerate(chips):
            copy(3 + j, src, blk(*chip, 1 - c), sibling).wait_recv()
        for cp in first + passed:
            cp.wait_send()
        mine.wait()

    return _pcall(body, name=name, in_specs=[ANY], out_specs=ANY,
                  out_shape=jax.ShapeDtypeStruct((4, rows, cols), shard.dtype),
                  scratch_shapes=[pltpu.SemaphoreType.DMA((6,)), pltpu.SemaphoreType.DMA((6,)), pltpu.SemaphoreType.DMA],
                  compiler_params=_params())(shard)


def _pair_swap(gp):
    _, _, half, cols = gp.shape

    def body(g_ref, land_ref, send_sems, recv_sems):
        x, y, c = _place()
        cps = [pltpu.make_async_remote_copy(src_ref=g_ref.at[j, 1 - c], dst_ref=land_ref.at[j], send_sem=send_sems.at[j],
                                            recv_sem=recv_sems.at[j], device_id=(x, y, 1 - c), device_id_type=MESH)
               for j in range(4)]
        for cp in cps:
            cp.start()
        for cp in cps:
            cp.wait()

    return _pcall(body, name="grad_pair_swap", in_specs=[ANY], out_specs=ANY,
                  out_shape=jax.ShapeDtypeStruct((4, half, cols), F32),
                  scratch_shapes=[pltpu.SemaphoreType.DMA((4,)), pltpu.SemaphoreType.DMA((4,))],
                  compiler_params=_params())(gp)


def _pair_add(gp, land, cidx):
    _, _, half, cols = gp.shape
    tr = _tile(half, 512)

    def body(c_ref, g_ref, l_ref, o_ref):
        o_ref[...] = g_ref[...] + l_ref[...]

    grid_spec = pltpu.PrefetchScalarGridSpec(
        num_scalar_prefetch=1, grid=(4, half // tr),
        in_specs=[pl.BlockSpec((None, None, tr, cols), lambda j, i, c: (j, c[0], i, 0)),
                  pl.BlockSpec((None, tr, cols), lambda j, i, c: (j, i, 0))],
        out_specs=pl.BlockSpec((None, tr, cols), lambda j, i, c: (j, i, 0)))
    return _pcall(body, name="grad_pair_add", grid_spec=grid_spec, out_shape=jax.ShapeDtypeStruct((4, half, cols), F32),
                  compiler_params=_params())(cidx, gp, land)


def _chip_scatter(cs):
    _, half, cols = cs.shape

    def body(s_ref, land_ref, send_sems, recv_sems, local_sem):
        x, y, c = _place()
        me = 2 * x + y
        chips = [(1 - x, y), (x, 1 - y), (1 - x, 1 - y)]
        mine = pltpu.make_async_copy(s_ref.at[me], land_ref.at[me], local_sem)
        mine.start()
        cps = [pltpu.make_async_remote_copy(src_ref=s_ref.at[2 * px + py], dst_ref=land_ref.at[me], send_sem=send_sems.at[j],
                                            recv_sem=recv_sems.at[j], device_id=(px, py, c), device_id_type=MESH)
               for j, (px, py) in enumerate(chips)]
        for cp in cps:
            cp.start()
        for j, (px, py) in enumerate(chips):
            pltpu.make_async_remote_copy(src_ref=s_ref.at[me], dst_ref=land_ref.at[2 * px + py], send_sem=send_sems.at[j],
                                         recv_sem=recv_sems.at[j], device_id=(px, py, c), device_id_type=MESH).wait_recv()
        for cp in cps:
            cp.wait_send()
        mine.wait()

    return _pcall(body, name="grad_chip_scatter", in_specs=[ANY], out_specs=ANY,
                  out_shape=jax.ShapeDtypeStruct((4, half, cols), F32),
                  scratch_shapes=[pltpu.SemaphoreType.DMA((3,)), pltpu.SemaphoreType.DMA((3,)), pltpu.SemaphoreType.DMA],
                  compiler_params=_params())(cs)


def _chip_sum(land):
    _, half, cols = land.shape
    tr = _tile(half, 512)

    def body(l_ref, o_ref):
        o_ref[...] = ((l_ref[0] + l_ref[1]) + l_ref[2]) + l_ref[3]

    return _pcall(body, name="grad_chip_sum", grid=(half // tr,),
                  in_specs=[pl.BlockSpec((4, tr, cols), lambda i: (0, i, 0))],
                  out_specs=pl.BlockSpec((tr, cols), lambda i: (i, 0)),
                  out_shape=jax.ShapeDtypeStruct((half, cols), F32), compiler_params=_params())(land)


def _pair_join(red):
    half, cols = red.shape

    def body(r_ref, out_ref, send_sem, recv_sem, local_sem):
        x, y, c = _place()
        mine = pltpu.make_async_copy(r_ref, out_ref.at[c], local_sem)
        mine.start()
        cp = pltpu.make_async_remote_copy(src_ref=r_ref, dst_ref=out_ref.at[c], send_sem=send_sem, recv_sem=recv_sem,
                                          device_id=(x, y, 1 - c), device_id_type=MESH)
        cp.start()
        pltpu.make_async_remote_copy(src_ref=r_ref, dst_ref=out_ref.at[1 - c], send_sem=send_sem, recv_sem=recv_sem,
                                     device_id=(x, y, 1 - c), device_id_type=MESH).wait_recv()
        cp.wait_send()
        mine.wait()

    return _pcall(body, name="grad_pair_join", in_specs=[ANY], out_specs=ANY,
                  out_shape=jax.ShapeDtypeStruct((2, half, cols), F32),
                  scratch_shapes=[pltpu.SemaphoreType.DMA, pltpu.SemaphoreType.DMA, pltpu.SemaphoreType.DMA],
                  compiler_params=_params())(red)


def _adamw(g, w, m, v):
    rows, cols = g.shape
    tr = _tile(rows, 512)

    def body(g_ref, w_ref, m_ref, v_ref, d_ref, mo_ref, vo_ref):
        gv = g_ref[...]
        mn = ADAM_B1 * m_ref[...] + (1.0 - ADAM_B1) * gv
        vn = ADAM_B2 * v_ref[...] + (1.0 - ADAM_B2) * jnp.square(gv)
        m_hat = mn / (1.0 - ADAM_B1 ** ADAM_STEP)
        v_hat = vn / (1.0 - ADAM_B2 ** ADAM_STEP)
        d_ref[...] = -ADAM_LR * (m_hat / (jnp.sqrt(v_hat) + ADAM_EPS) + ADAM_WD * w_ref[...])
        mo_ref[...] = mn
        vo_ref[...] = vn

    blk = pl.BlockSpec((tr, cols), lambda i: (i, 0))
    shp = jax.ShapeDtypeStruct((rows, cols), F32)
    return _pcall(body, name="adamw", grid=(rows // tr,), in_specs=[blk] * 4, out_specs=(blk,) * 3, out_shape=(shp,) * 3,
                  compiler_params=_params())(g, w, m, v)


def _pack(pieces, rows, dtype):
    flat = jnp.concatenate([p.reshape(-1).astype(dtype) for p in pieces])
    return jnp.pad(flat, (0, rows * PACK_COLS - flat.shape[0])).reshape(rows, PACK_COLS)


def _unpack(pack, shapes):
    flat = pack.reshape(-1)
    out, off = [], 0
    for shp in shapes:
        n = math.prod(shp)
        out.append(flat[off:off + n].reshape(shp))
        off += n
    return out


def _shard_of(full, axis, j):
    n = full.shape[axis] // 4
    return lax.slice_in_dim(full, j * n, (j + 1) * n, axis=axis)


def kernel(x, even_w_in, even_conv_w, even_conv_b, ssd_a_log, ssd_dt_bias, ssd_d, ssd_norm_w, lru_w_a, lru_b_a, lru_w_x, lru_b_x, lru_lambda, even_w_out, odd_w_in, hgrn_lb_logits, hgrn_norm_w, odd_w_out, norm_mix, norm_mlp, mlp_w1, mlp_w2, norm_final, loss_target, m_even_w_in, m_even_conv_w, m_even_conv_b, m_ssd_a_log, m_ssd_dt_bias, m_ssd_d, m_ssd_norm_w, m_lru_w_a, m_lru_b_a, m_lru_w_x, m_lru_b_x, m_lru_lambda, m_even_w_out, m_odd_w_in, m_hgrn_lb_logits, m_hgrn_norm_w, m_odd_w_out, m_norm_mix, m_norm_mlp, m_mlp_w1, m_mlp_w2, m_norm_final, v_even_w_in, v_even_conv_w, v_even_conv_b, v_ssd_a_log, v_ssd_dt_bias, v_ssd_d, v_ssd_norm_w, v_lru_w_a, v_lru_b_a, v_lru_w_x, v_lru_b_x, v_lru_lambda, v_even_w_out, v_odd_w_in, v_hgrn_lb_logits, v_hgrn_norm_w, v_odd_w_out, v_norm_mix, v_norm_mlp, v_mlp_w1, v_mlp_w2, v_norm_final):
    names = [n for n, _, _, _ in WEIGHTS]
    w_loc = dict(zip(names, (even_w_in, even_conv_w, even_conv_b, ssd_a_log, ssd_dt_bias, ssd_d, ssd_norm_w, lru_w_a, lru_b_a, lru_w_x, lru_b_x, lru_lambda, even_w_out, odd_w_in, hgrn_lb_logits, hgrn_norm_w, odd_w_out, norm_mix, norm_mlp, mlp_w1, mlp_w2, norm_final)))
    m_loc = dict(zip(names, (m_even_w_in, m_even_conv_w, m_even_conv_b, m_ssd_a_log, m_ssd_dt_bias, m_ssd_d, m_ssd_norm_w, m_lru_w_a, m_lru_b_a, m_lru_w_x, m_lru_b_x, m_lru_lambda, m_even_w_out, m_odd_w_in, m_hgrn_lb_logits, m_hgrn_norm_w, m_odd_w_out, m_norm_mix, m_norm_mlp, m_mlp_w1, m_mlp_w2, m_norm_final)))
    v_loc = dict(zip(names, (v_even_w_in, v_even_conv_w, v_even_conv_b, v_ssd_a_log, v_ssd_dt_bias, v_ssd_d, v_ssd_norm_w, v_lru_w_a, v_lru_b_a, v_lru_w_x, v_lru_b_x, v_lru_lambda, v_even_w_out, v_odd_w_in, v_hgrn_lb_logits, v_hgrn_norm_w, v_odd_w_out, v_norm_mix, v_norm_mlp, v_mlp_w1, v_mlp_w2, v_norm_final)))
    spec = {n: (blk, full, ax) for n, blk, full, ax in WEIGHTS}

    gbig = _gather_chips("gather_big", _pack([w_loc[n] for n in BIG], WPACK_ROWS, BF16))
    gsmall = _gather_chips("gather_small", _pack([w_loc[n] for n in SMALL_SHARDED], 16, F32))
    w_full = {n: w_loc[n] for n in names if spec[n][2] is None}
    for group, packed in ((BIG, gbig), (SMALL_SHARDED, gsmall)):
        shards = [_unpack(packed[j], [spec[n][0] for n in group]) for j in range(4)]
        for i, n in enumerate(group):
            w_full[n] = jnp.concatenate([shards[j][i] for j in range(4)], axis=spec[n][2])

    loss_vec, grad_x, grads = _local_step(x, loss_target, w_full)
    loss = lax.psum(loss_vec[0, 0], ("x", "y", "c"))

    def dest_pack(j):
        return _pack([grads[n].reshape(spec[n][1]) if spec[n][2] is None else _shard_of(grads[n].reshape(spec[n][1]), spec[n][2], j)
                      for n in names], PACK_ROWS, F32)

    half = PACK_ROWS // 2
    gp = jnp.stack([dest_pack(j) for j in range(4)]).reshape(4, 2, half, PACK_COLS)
    cidx = lax.axis_index("c").astype(jnp.int32).reshape(1)
    chip_sums = _pair_add(gp, _pair_swap(gp), cidx)
    reduced = _pair_join(_chip_sum(_chip_scatter(chip_sums))).reshape(PACK_ROWS, PACK_COLS)

    blocks = [spec[n][0] for n in names]
    wp, mp, vp = (_pack([src[n] for n in names], PACK_ROWS, F32) for src in (w_loc, m_loc, v_loc))
    delta, new_m, new_v = _adamw(reduced, wp, mp, vp)
    return (loss, grad_x, *_unpack(reduced, blocks), *_unpack(delta, blocks), *_unpack(new_m, blocks), *_unpack(new_v, blocks))
```

```python
import functools
import math

import jax
import jax.numpy as jnp
from jax import lax
from jax.experimental import pallas as pl
from jax.experimental.pallas import tpu as pltpu

F32 = jnp.float32
BF16 = jnp.bfloat16
MXU_DTYPE = jnp.bfloat16
HI = lax.Precision.HIGHEST
MESH = pl.DeviceIdType.MESH

D_MODEL = 1024
EPS = 1e-6
SSD_HEADS = 16
SSD_HEADDIM = 64
HEAD_SHIFT = 6
SSD_GROUPS = 4
SSD_STATE = 128
SSD_CHUNK = 128
LRU_C = 8.0
LRU_ROWS = 256
HGRN_HEADS = 8
HGRN_HEADDIM = 128
HGRN_SUB = 32
HGRN_BLOCK = 128
HGRN_SCALE = HGRN_HEADDIM ** -0.5
CONV_ROWS = 512

ADAM_LR = 0.001
ADAM_B1 = 0.9
ADAM_B2 = 0.999
ADAM_EPS = 1e-08
ADAM_WD = 0.01
ADAM_STEP = 10

VMEM_LIMIT = 56 * 1024 * 1024
PACK_COLS = 1024
SMALL_ROWS = 272

WEIGHTS = (
    ("even_w_in", (1, 1024, 1288), (1, 1024, 5152), 2),
    ("even_conv_w", (1, 4, 768), (1, 4, 3072), 2),
    ("even_conv_b", (1, 3072), (1, 3072), None),
    ("ssd_a_log", (1, 2, 16), (1, 2, 16), None),
    ("ssd_dt_bias", (1, 2, 16), (1, 2, 16), None),
    ("ssd_d", (1, 16), (1, 16), None),
    ("ssd_norm_w", (1, 1024), (1, 1024), None),
    ("lru_w_a", (1, 2, 16, 64, 64), (1, 2, 16, 64, 64), None),
    ("lru_b_a", (1, 2, 256), (1, 2, 1024), 2),
    ("lru_w_x", (1, 2, 16, 64, 64), (1, 2, 16, 64, 64), None),
    ("lru_b_x", (1, 2, 256), (1, 2, 1024), 2),
    ("lru_lambda", (1, 2, 256), (1, 2, 1024), 2),
    ("even_w_out", (1, 512, 1024), (1, 2048, 1024), 1),
    ("odd_w_in", (1, 1024, 1280), (1, 1024, 5120), 2),
    ("hgrn_lb_logits", (2, 1024), (2, 1024), None),
    ("hgrn_norm_w", (1, 256), (1, 1024), 1),
    ("odd_w_out", (1, 256, 1024), (1, 1024, 1024), 1),
    ("norm_mix", (2, 1024), (2, 1024), None),
    ("norm_mlp", (2, 1024), (2, 1024), None),
    ("mlp_w1", (2, 1024, 1024), (2, 1024, 4096), 2),
    ("mlp_w2", (2, 1024, 1024), (2, 4096, 1024), 1),
    ("norm_final", (1024,), (1024,), None),
)
BIG = ("even_w_in", "even_w_out", "odd_w_in", "odd_w_out", "mlp_w1", "mlp_w2")
BIG_2D = {"even_w_in": (1024, 1288), "even_w_out": (512, 1024), "odd_w_in": (1024, 1280), "odd_w_out": (256, 1024),
          "mlp_w1": (2048, 1024), "mlp_w2": (2048, 1024)}
SMALL_SHARDED = ("even_conv_w", "lru_b_a", "lru_b_x", "lru_lambda", "hgrn_norm_w")


def _pcall(body, **kw):
    return pl.pallas_call(body, **kw)


def _params(**kw):
    return pltpu.CompilerParams(vmem_limit_bytes=VMEM_LIMIT, **kw)


def _tile(n, pref):
    if n <= pref:
        return n
    t = (pref // 128) * 128
    while n % t:
        t -= 128
    return t


def _dot(a, b, dims=(((1,), (0,)), ((), ())), precision=None):
    return lax.dot_general(a, b, dims, preferred_element_type=F32, precision=precision)


_NN = (((1,), (0,)), ((), ()))
_NT = (((1,), (1,)), ((), ()))
_TN = (((0,), (0,)), ((), ()))


def _mx(v):
    return v.astype(MXU_DTYPE)


def _mm(name, a, b, mode, *, out_dtype=F32, res=None, relu2=False, col_shards=1):
    if mode == "nn":
        (m, kk), (_, n) = a.shape, b.shape
    elif mode == "nt":
        (m, kk), (n, _) = a.shape, b.shape
    else:
        (kk, m), (_, n) = a.shape, b.shape
    tm, tn, tk = _tile(m, 512), _tile(n // col_shards, 1024), _tile(kk, 1024 if mode != "tn" else 512)
    nk = kk // tk
    dims = {"nn": _NN, "nt": _NT, "tn": _TN}[mode]
    a_spec = pl.BlockSpec((tk, tm), lambda i, j, k: (k, i)) if mode == "tn" else pl.BlockSpec((tm, tk), lambda i, j, k: (i, k))
    b_spec = pl.BlockSpec((tn, tk), lambda i, j, k: (j, k)) if mode == "nt" else pl.BlockSpec((tk, tn), lambda i, j, k: (k, j))
    o_spec = pl.BlockSpec((tm, tn), lambda i, j, k: (i, j))
    o_shape = (m, n)
    if col_shards > 1:
        assert tn * col_shards == n and res is None and not relu2
        o_spec = pl.BlockSpec((None, tm, tn), lambda i, j, k: (j, i, 0))
        o_shape = (col_shards, m, tn)
    has_res = res is not None

    def body(*refs):
        a_ref, b_ref = refs[0], refs[1]
        res_ref = refs[2] if has_res else None
        outs = refs[2 + has_res:-1]
        acc = refs[-1]
        k = pl.program_id(2)

        @pl.when(k == 0)
        def _():
            acc[...] = jnp.zeros_like(acc)

        acc[...] += _dot(_mx(a_ref[...]), _mx(b_ref[...]), dims)

        @pl.when(k == nk - 1)
        def _():
            r = acc[...]
            if has_res:
                r = r + res_ref[...]
            if relu2:
                outs[0][...] = r
                outs[1][...] = jnp.square(jnp.maximum(r, 0.0)).astype(outs[1].dtype)
            else:
                outs[0][...] = r.astype(outs[0].dtype)

    in_specs = [a_spec, b_spec] + ([o_spec] if has_res else [])
    if relu2:
        out_shape = (jax.ShapeDtypeStruct((m, n), F32), jax.ShapeDtypeStruct((m, n), BF16))
        out_specs = (o_spec, o_spec)
    else:
        out_shape = jax.ShapeDtypeStruct(o_shape, out_dtype)
        out_specs = o_spec
    args = (a, b) + ((res,) if has_res else ())
    return _pcall(body, name=name, grid=(m // tm, n // tn, nk), in_specs=in_specs, out_specs=out_specs,
                  out_shape=out_shape, scratch_shapes=[pltpu.VMEM((tm, tn), F32)], compiler_params=_params())(*args)


def _pw_fwd(name, f, ins, params, out_dtypes, tc, ncol, tm=256):
    t = ins[0][0].shape[0]
    tm = min(tm, t)
    ni, npar = len(ins), len(params)

    def body(*refs):
        vals = f(*[r[...].astype(F32) for r in refs[:ni]], *[r[...] for r in refs[ni:ni + npar]])
        for o, v in zip(refs[ni + npar:], vals):
            o[...] = v.astype(o.dtype)

    in_specs = [pl.BlockSpec((tm, tc), lambda j, i, off=off: (i, off + j)) for _, off in ins]
    in_specs += [pl.BlockSpec((1, tc), lambda j, i, off=off: (0, off + j)) for _, off in params]
    out_specs = tuple(pl.BlockSpec((tm, tc), lambda j, i: (i, j)) for _ in out_dtypes)
    out_shape = tuple(jax.ShapeDtypeStruct((t, ncol * tc), d) for d in out_dtypes)
    return _pcall(body, name=name, grid=(ncol, t // tm), in_specs=in_specs, out_specs=out_specs, out_shape=out_shape,
                  compiler_params=_params())(*[a for a, _ in ins], *[p for p, _ in params])


def _pw_bwd(name, f, ins, params, douts, tc, ncol, want, adds=None, tm=256):
    adds = adds or {}
    t = ins[0][0].shape[0]
    tm = min(tm, t)
    ni, npar, nd, na = len(ins), len(params), len(douts), len(adds)
    add_keys = sorted(adds)

    def body(*refs):
        in_refs, p_refs = refs[:ni], refs[ni:ni + npar]
        d_refs = refs[ni + npar:ni + npar + nd]
        a_refs = refs[ni + npar + nd:ni + npar + nd + na]
        o_refs = refs[ni + npar + nd + na:]
        _, vjp = jax.vjp(f, *[r[...].astype(F32) for r in in_refs], *[r[...] for r in p_refs])
        cts = vjp(tuple(d[...] for d in d_refs))
        for o, kidx in zip(o_refs[:len(want)], want):
            v = cts[kidx]
            if kidx in adds:
                v = v + a_refs[add_keys.index(kidx)][...]
            o[...] = v
        for p in range(npar):
            o = o_refs[len(want) + p]

            @pl.when(pl.program_id(1) == 0)
            def _(o=o):
                o[...] = jnp.zeros_like(o)

            o[...] += cts[ni + p]

    in_specs = [pl.BlockSpec((tm, tc), lambda j, i, off=off: (i, off + j)) for _, off in ins]
    in_specs += [pl.BlockSpec((1, tc), lambda j, i, off=off: (0, off + j)) for _, off in params]
    in_specs += [pl.BlockSpec((tm, tc), lambda j, i: (i, j)) for _ in range(nd + na)]
    out_specs = tuple([pl.BlockSpec((tm, tc), lambda j, i: (i, j)) for _ in want]
                      + [pl.BlockSpec((1, tc), lambda j, i: (0, j)) for _ in params])
    out_shape = tuple([jax.ShapeDtypeStruct((t, ncol * tc), F32) for _ in want]
                      + [jax.ShapeDtypeStruct((1, ncol * tc), F32) for _ in params])
    res = _pcall(body, name=name, grid=(ncol, t // tm), in_specs=in_specs, out_specs=out_specs, out_shape=out_shape,
                 compiler_params=_params())(*[a for a, _ in ins], *[p for p, _ in params], *douts, *[adds[k] for k in add_keys])
    return list(res[:len(want)]), list(res[len(want):])


def _rms(x, g):
    return (x * lax.rsqrt(jnp.mean(x * x, axis=-1, keepdims=True) + EPS)) * g


def _f_norm(x, g):
    return (_rms(x, g),)


def _f_silu(c):
    return (jax.nn.silu(c),)


def _f_softplus(d, b):
    return (jax.nn.softplus(d + b),)


def _f_relu2(a):
    return (jnp.square(jnp.maximum(a, 0.0)),)


def _f_add2(a, b):
    return (a + b,)


def _f_add3(a, b, c):
    return (a + b + c,)


def _f_ssd_post(yf, yb, xs, z, dskip, nw):
    u = (yf + yb + dskip * xs) * jax.nn.silu(z)
    return (_rms(u, nw),)


def _neg_expm1(v):
    t = jnp.tanh(0.5 * v)
    return -2.0 * t / (1.0 - t)


def _f_lru_gates(pre_a, pre_x, u, ba, bx, lam):
    rg = jax.nn.sigmoid(pre_a + ba)
    ig = jax.nn.sigmoid(pre_x + bx)
    log_a = -LRU_C * rg * jax.nn.softplus(-lam)
    return jnp.exp(log_a), jnp.sqrt(_neg_expm1(2.0 * log_a)) * (ig * u)


def _f_lru_post(hf, hb, gate):
    return ((hf + hb) * jax.nn.gelu(gate),)


def _f_hgrn_pre(fr, l0, l1):
    lb = jax.nn.sigmoid(l1 - l0)
    k = (1.0 - lb) * jax.nn.sigmoid(-fr)
    return k, jnp.log1p(-k)


def _f_hgrn_post(of, ob, gate, nw):
    return (_rms(of + ob, nw) * jax.nn.silu(gate),)


def _loss_head(x, tgt, g, tm=256):
    t, d = x.shape
    tm = min(tm, t)

    def body(x_ref, t_ref, g_ref, dx_ref, dg_ref, loss_ref):
        tv = t_ref[...]

        def lf(xv, gv):
            return 0.5 * jnp.sum(jnp.mean(jnp.square(_rms(xv, gv) - tv), axis=-1))

        val, vjp = jax.vjp(lf, x_ref[...], g_ref[...])
        dx, dg = vjp(jnp.ones((), F32))
        dx_ref[...] = dx

        @pl.when(pl.program_id(0) == 0)
        def _():
            dg_ref[...] = jnp.zeros_like(dg_ref)
            loss_ref[...] = jnp.zeros_like(loss_ref)

        dg_ref[...] += dg
        loss_ref[...] += jnp.full(loss_ref.shape, val, F32)

    row = pl.BlockSpec((tm, d), lambda i: (i, 0))
    vec = pl.BlockSpec((1, d), lambda i: (0, 0))
    return _pcall(body, name="loss_head", grid=(t // tm,), in_specs=[row, row, vec],
                  out_specs=(row, vec, pl.BlockSpec((1, 128), lambda i: (0, 0))),
                  out_shape=(jax.ShapeDtypeStruct((t, d), F32), jax.ShapeDtypeStruct((1, d), F32),
                             jax.ShapeDtypeStruct((1, 128), F32)), compiler_params=_params())(x, tgt, g)


def _shifted(x, d, prev, nxt, first, last):
    r = x.shape[0]
    row = lax.broadcasted_iota(jnp.int32, x.shape, 0)
    if d < 0:
        out = pltpu.roll(x, -d, 0)
        for q in range(-d):
            pv = jnp.where(first, 0.0, prev[8 + d + q:8 + d + q + 1, :])
            out = jnp.where(row == q, pv, out)
        return out
    out = pltpu.roll(x, r - d, 0)
    for q in range(d):
        nv = jnp.where(last, 0.0, nxt[q:q + 1, :])
        out = jnp.where(row == r - d + q, nv, out)
    return out


def _halo_specs(ts, tc, s):
    nb8 = s // 8
    cur = pl.BlockSpec((None, ts, tc), lambda n, i, j: (n, i, j))
    prev = pl.BlockSpec((None, 8, tc), lambda n, i, j: (n, jnp.maximum(i * (ts // 8) - 1, 0), j))
    nxt = pl.BlockSpec((None, 8, tc), lambda n, i, j: (n, jnp.minimum((i + 1) * (ts // 8), nb8 - 1), j))
    return cur, prev, nxt


def _conv_fwd(p3, w, b, ncol, tc=1024):
    nbatch, s, _ = p3.shape
    ts = min(CONV_ROWS, s)
    nblk = s // ts

    def body(x_ref, pv_ref, nx_ref, w_ref, b_ref, o_ref):
        i = pl.program_id(1)
        first, last = i == 0, i == nblk - 1
        x, pv, nx = x_ref[...], pv_ref[...], nx_ref[...]
        wv = w_ref[...]
        out = b_ref[...] + wv[1:2] * x
        out = out + wv[0:1] * _shifted(x, -1, pv, nx, first, last)
        out = out + wv[2:3] * _shifted(x, 1, pv, nx, first, last)
        out = out + wv[3:4] * _shifted(x, 2, pv, nx, first, last)
        o_ref[...] = out

    cur, prev, nxt = _halo_specs(ts, tc, s)
    return _pcall(body, name="conv_fwd", grid=(nbatch, nblk, ncol),
                  in_specs=[cur, prev, nxt, pl.BlockSpec((4, tc), lambda n, i, j: (0, j)),
                            pl.BlockSpec((1, tc), lambda n, i, j: (0, j))],
                  out_specs=cur, out_shape=jax.ShapeDtypeStruct((nbatch, s, ncol * tc), F32),
                  compiler_params=_params())(p3, p3, p3, w, b)


def _conv_bwd(dc3, p3, w, ncol, tc=1024):
    nbatch, s, _ = dc3.shape
    ts = min(CONV_ROWS, s)
    nblk = s // ts

    def body(d_ref, dpv_ref, dnx_ref, x_ref, pv_ref, nx_ref, w_ref, dx_ref, dw_ref):
        n, i = pl.program_id(1), pl.program_id(2)
        first, last = i == 0, i == nblk - 1
        d, dpv, dnx = d_ref[...], dpv_ref[...], dnx_ref[...]
        x, pv, nx = x_ref[...], pv_ref[...], nx_ref[...]
        wv = w_ref[...]
        dx = wv[1:2] * d
        dx = dx + wv[0:1] * _shifted(d, 1, dpv, dnx, first, last)
        dx = dx + wv[2:3] * _shifted(d, -1, dpv, dnx, first, last)
        dx = dx + wv[3:4] * _shifted(d, -2, dpv, dnx, first, last)
        dx_ref[...] = dx

        @pl.when((n == 0) & (i == 0))
        def _():
            dw_ref[...] = jnp.zeros_like(dw_ref)

        dw_ref[0:1, :] += jnp.sum(d * _shifted(x, -1, pv, nx, first, last), axis=0, keepdims=True)
        dw_ref[1:2, :] += jnp.sum(d * x, axis=0, keepdims=True)
        dw_ref[2:3, :] += jnp.sum(d * _shifted(x, 1, pv, nx, first, last), axis=0, keepdims=True)
        dw_ref[3:4, :] += jnp.sum(d * _shifted(x, 2, pv, nx, first, last), axis=0, keepdims=True)
        dw_ref[4:5, :] += jnp.sum(d, axis=0, keepdims=True)

    nb8 = s // 8
    cur = pl.BlockSpec((None, ts, tc), lambda j, n, i: (n, i, j))
    prev = pl.BlockSpec((None, 8, tc), lambda j, n, i: (n, jnp.maximum(i * (ts // 8) - 1, 0), j))
    nxt = pl.BlockSpec((None, 8, tc), lambda j, n, i: (n, jnp.minimum((i + 1) * (ts // 8), nb8 - 1), j))
    return _pcall(body, name="conv_bwd", grid=(ncol, nbatch, nblk),
                  in_specs=[cur, prev, nxt, cur, prev, nxt, pl.BlockSpec((4, tc), lambda j, n, i: (0, j))],
                  out_specs=(cur, pl.BlockSpec((8, tc), lambda j, n, i: (0, j))),
                  out_shape=(jax.ShapeDtypeStruct((nbatch, s, ncol * tc), F32), jax.ShapeDtypeStruct((8, ncol * tc), F32)),
                  compiler_params=_params())(dc3, dc3, dc3, p3, p3, p3, w)


def _block_scan(coef, inp, reverse):
    r = coef.shape[0]
    row = lax.broadcasted_iota(jnp.int32, coef.shape, 0)
    a, b = coef, inp
    d = 1
    while d < r:
        if reverse:
            keep = row < r - d
            a_sh, b_sh = pltpu.roll(a, r - d, 0), pltpu.roll(b, r - d, 0)
        else:
            keep = row >= d
            a_sh, b_sh = pltpu.roll(a, d, 0), pltpu.roll(b, d, 0)
        b = b + a * jnp.where(keep, b_sh, 0.0)
        a = a * jnp.where(keep, a_sh, 1.0)
        d *= 2
    return a, b


def _lru_scan(a3, b3, reverse):
    nbatch, s, w = a3.shape
    ts = min(LRU_ROWS, s)
    nblk = s // ts
    edge = 0 if reverse else ts - 1

    def body(a_ref, b_ref, h_ref, carry):
        @pl.when(pl.program_id(1) == 0)
        def _():
            carry[...] = jnp.zeros_like(carry)

        ca, hb = _block_scan(a_ref[...], b_ref[...], reverse)
        h = hb + ca * carry[0:1, :]
        h_ref[...] = h
        carry[0:1, :] = h[edge:edge + 1, :]

    blk = pl.BlockSpec((None, ts, w), (lambda n, i: (n, nblk - 1 - i, 0)) if reverse else (lambda n, i: (n, i, 0)))
    return _pcall(body, name=f"lru_scan_r{int(reverse)}", grid=(nbatch, nblk), in_specs=[blk, blk], out_specs=blk,
                  out_shape=jax.ShapeDtypeStruct((nbatch, s, w), F32), scratch_shapes=[pltpu.VMEM((8, w), F32)],
                  compiler_params=_params())(a3, b3)


def _lru_scan_bwd(a3, h3, dh3, reverse):
    nbatch, s, w = a3.shape
    ts = min(LRU_ROWS, s)
    nblk = s // ts
    nb8 = s // 8
    tpb = ts // 8

    def body(a_ref, aa_ref, h_ref, hh_ref, dh_ref, g_ref, da_ref, carry):
        i = pl.program_id(1)

        @pl.when(i == 0)
        def _():
            carry[...] = jnp.zeros_like(carry)

        a, h = a_ref[...], h_ref[...]
        row = lax.broadcasted_iota(jnp.int32, a.shape, 0)
        if reverse:
            a_edge = jnp.where(i == 0, 0.0, aa_ref[7:8, :])
            c = jnp.where(row == 0, a_edge, pltpu.roll(a, 1, 0))
            h_edge = jnp.where(i == nblk - 1, 0.0, hh_ref[0:1, :])
            h_sh = jnp.where(row == ts - 1, h_edge, pltpu.roll(h, ts - 1, 0))
        else:
            a_edge = jnp.where(i == 0, 0.0, aa_ref[0:1, :])
            c = jnp.where(row == ts - 1, a_edge, pltpu.roll(a, ts - 1, 0))
            h_edge = jnp.where(i == nblk - 1, 0.0, hh_ref[7:8, :])
            h_sh = jnp.where(row == 0, h_edge, pltpu.roll(h, 1, 0))
        cc, gb = _block_scan(c, dh_ref[...], not reverse)
        g = gb + cc * carry[0:1, :]
        g_ref[...] = g
        carry[0:1, :] = g[ts - 1:ts, :] if reverse else g[0:1, :]
        da_ref[...] = g * h_sh

    if reverse:
        bi = lambda i: i
    else:
        bi = lambda i: nblk - 1 - i
    blk = pl.BlockSpec((None, ts, w), lambda n, i: (n, bi(i), 0))
    before = pl.BlockSpec((None, 8, w), lambda n, i: (n, jnp.maximum(bi(i) * tpb - 1, 0), 0))
    after = pl.BlockSpec((None, 8, w), lambda n, i: (n, jnp.minimum((bi(i) + 1) * tpb, nb8 - 1), 0))
    a_tile, h_tile = (before, after) if reverse else (after, before)
    return _pcall(body, name=f"lru_scan_bwd_r{int(reverse)}", grid=(nbatch, nblk), in_specs=[blk, a_tile, blk, h_tile, blk],
                  out_specs=(blk, blk),
                  out_shape=(jax.ShapeDtypeStruct((nbatch, s, w), F32), jax.ShapeDtypeStruct((nbatch, s, w), F32)),
                  scratch_shapes=[pltpu.VMEM((8, w), F32)], compiler_params=_params())(a3, a3, h3, h3, dh3)


def _head_expand(lane0):
    return (jnp.right_shift(lax.broadcasted_iota(jnp.int32, (128, 1024), 1), HEAD_SHIFT) + lane0
            == lax.broadcasted_iota(jnp.int32, (128, 1024), 0)).astype(F32)


def _head_reduce(lane0):
    return (jnp.right_shift(lax.broadcasted_iota(jnp.int32, (1024, 128), 0), HEAD_SHIFT) + lane0
            == lax.broadcasted_iota(jnp.int32, (1024, 128), 1)).astype(F32)


def _time_mask(q, reverse):
    ri = lax.broadcasted_iota(jnp.int32, (q, q), 0)
    ci = lax.broadcasted_iota(jnp.int32, (q, q), 1)
    return (ri <= ci) if reverse else (ri >= ci)


def _ssd_common(xs_ref, bc_ref, dt_ref, al_ref, reverse, lane0):
    q = xs_ref.shape[0]
    edge = 0 if reverse else q - 1
    dt = dt_ref[...]
    a = -jnp.exp(al_ref[...])
    mask = _time_mask(q, reverse)
    expand = _head_expand(lane0)
    cum = _dot(mask.astype(F32), dt * a, precision=HI)
    cum_x = _dot(cum, expand, precision=HI)
    dt_x = _dot(dt, expand, precision=HI)
    last_x = cum_x[edge:edge + 1, :]
    xs = xs_ref[...]
    bc = bc_ref[...]
    return dict(q=q, edge=edge, lane0=lane0, dt=dt, a=a, mask=mask, cum_t=cum.T, cum_x=cum_x, dt_x=dt_x, xs=xs,
                v=xs * dt_x, e_c=jnp.exp(cum_x), w=jnp.exp(last_x - cum_x), e_l=jnp.exp(last_x),
                bm=bc[:, :512], cm=bc[:, 512:])


def _ssd_decay(c, h):
    row = c["lane0"] + h
    seg = c["cum_x"][:, h * SSD_HEADDIM:h * SSD_HEADDIM + 1] - c["cum_t"][row:row + 1, :]
    return jnp.where(c["mask"], jnp.exp(jnp.minimum(seg, 0.0)), 0.0)


def _head_masks():
    lane = jnp.right_shift(lax.broadcasted_iota(jnp.int32, (1, 256), 1), HEAD_SHIFT)
    return [lane == e for e in range(4)]


def _ssd_fwd(xbc3, dt3, alog, reverse):
    nbatch, s, _ = xbc3.shape
    q = min(SSD_CHUNK, s)
    nc = s // q
    lane0 = SSD_HEADS * int(reverse)

    def body(xs_ref, bc_ref, dt_ref, al_ref, y_ref, st_ref, st):
        @pl.when(pl.program_id(1) == 0)
        def _():
            st[...] = jnp.zeros_like(st)

        st_ref[...] = st[...]
        c = _ssd_common(xs_ref, bc_ref, dt_ref, al_ref, reverse, lane0)
        hm = _head_masks()
        for g in range(SSD_GROUPS):
            sl = slice(g * 256, (g + 1) * 256)
            cg, bg = _mx(c["cm"][:, g * 128:(g + 1) * 128]), _mx(c["bm"][:, g * 128:(g + 1) * 128])
            cb = _dot(cg, bg, _NT)
            vg = c["v"][:, sl]
            s0 = st[:, sl]
            yg = _dot(cg, _mx(s0)) * c["e_c"][:, sl]
            for e in range(4):
                m = _ssd_decay(c, 4 * g + e) * cb
                yg = yg + _dot(_mx(m), _mx(jnp.where(hm[e], vg, 0.0)))
            y_ref[:, sl] = yg
            st[:, sl] = c["e_l"][:, sl] * s0 + _dot(bg, _mx(vg * c["w"][:, sl]), _TN)

    ck = (lambda i: nc - 1 - i) if reverse else (lambda i: i)
    xs_spec = pl.BlockSpec((None, q, 1024), lambda n, i: (n, ck(i), 0))
    bc_spec = pl.BlockSpec((None, q, 1024), lambda n, i: (n, ck(i), 1))
    dt_spec = pl.BlockSpec((None, q, 128), lambda n, i: (n, ck(i), 0))
    al_spec = pl.BlockSpec((1, 128), lambda n, i: (0, 0))
    st_spec = pl.BlockSpec((None, None, 128, 1024), lambda n, i: (n, ck(i), 0, 0))
    return _pcall(body, name=f"ssd_fwd_r{int(reverse)}", grid=(nbatch, nc), in_specs=[xs_spec, bc_spec, dt_spec, al_spec],
                  out_specs=(xs_spec, st_spec),
                  out_shape=(jax.ShapeDtypeStruct((nbatch, s, 1024), F32), jax.ShapeDtypeStruct((nbatch, nc, 128, 1024), F32)),
                  scratch_shapes=[pltpu.VMEM((128, 1024), F32)], compiler_params=_params())(xbc3, xbc3, dt3, alog)


def _ssd_bwd(xbc3, dt3, alog, st4, dy3, reverse):
    nbatch, s, _ = xbc3.shape
    q = min(SSD_CHUNK, s)
    nc = s // q
    lane0 = SSD_HEADS * int(reverse)

    def body(xs_ref, bc_ref, dt_ref, al_ref, st0_ref, dy_ref, dxs_ref, dbc_ref, ddt_ref, dal_ref, dst):
        n, i = pl.program_id(0), pl.program_id(1)

        @pl.when(i == 0)
        def _():
            dst[...] = jnp.zeros_like(dst)

        @pl.when((i == 0) & (n == 0))
        def _():
            dal_ref[...] = jnp.zeros_like(dal_ref)

        c = _ssd_common(xs_ref, bc_ref, dt_ref, al_ref, reverse, lane0)
        hm = _head_masks()
        reduce_m = _head_reduce(lane0)
        s0_all, ds1_all, dy = st0_ref[...], dst[...], dy_ref[...]
        lane = lax.broadcasted_iota(jnp.int32, (q, 128), 1)
        sub = lax.broadcasted_iota(jnp.int32, (128, q), 0)
        rowacc = jnp.zeros((q, 128), F32)
        colacc_t = jnp.zeros((128, q), F32)
        dv_l, yst_l, dvbar_l, dk_l, dc_l = [], [], [], [], []
        for g in range(SSD_GROUPS):
            sl = slice(g * 256, (g + 1) * 256)
            cg, bg = _mx(c["cm"][:, g * 128:(g + 1) * 128]), _mx(c["bm"][:, g * 128:(g + 1) * 128])
            cb = _dot(cg, bg, _NT)
            vg, dyg, wg, ecg = c["v"][:, sl], dy[:, sl], c["w"][:, sl], c["e_c"][:, sl]
            s0, ds1 = _mx(s0_all[:, sl]), _mx(ds1_all[:, sl])
            dye = _mx(dyg * ecg)
            yst_l.append(_dot(cg, s0) * ecg)
            dcg = _dot(dye, s0, _NT)
            dst[:, sl] = c["e_l"][:, sl] * ds1_all[:, sl] + _dot(cg, dye, _TN)
            vbar = _mx(vg * wg)
            dvbar = _dot(bg, ds1)
            dvbar_l.append(dvbar)
            dvg = dvbar * wg
            dkg = _dot(vbar, ds1, _NT)
            for e in range(4):
                h = 4 * g + e
                m = _ssd_decay(c, h)
                dyh, vh = _mx(jnp.where(hm[e], dyg, 0.0)), _mx(jnp.where(hm[e], vg, 0.0))
                dvg = dvg + _dot(_mx(m * cb), dyh, _TN)
                dcb = _dot(dyh, vh, _NT) * m
                dcbb = _mx(dcb)
                dcg = dcg + _dot(dcbb, bg)
                dkg = dkg + _dot(dcbb, cg, _TN)
                wmat = dcb * cb
                rowacc = jnp.where(lane == lane0 + h, jnp.sum(wmat, axis=1, keepdims=True), rowacc)
                colacc_t = jnp.where(sub == lane0 + h, jnp.sum(wmat, axis=0, keepdims=True), colacc_t)
            dv_l.append(dvg)
            dk_l.append(dkg)
            dc_l.append(dcg)
        dv = jnp.concatenate(dv_l, axis=1)
        yst = jnp.concatenate(yst_l, axis=1)
        dvbar = jnp.concatenate(dvbar_l, axis=1)
        t1 = _dot(dy * yst, reduce_m, precision=HI)
        t2 = _dot(c["v"] * c["w"] * dvbar, reduce_m, precision=HI)
        dlast = jnp.sum(t2, axis=0, keepdims=True) + _dot(
            c["e_l"] * jnp.sum(ds1_all * s0_all, axis=0, keepdims=True), reduce_m, precision=HI)
        dcum = rowacc - colacc_t.T + t1 - t2
        dcum = dcum + jnp.where(lax.broadcasted_iota(jnp.int32, (q, 128), 0) == c["edge"], dlast, 0.0)
        dda = _dot(c["mask"].astype(F32), dcum, _TN, precision=HI)
        ddt_ref[...] = dda * c["a"] + _dot(dv * c["xs"], reduce_m, precision=HI)
        dal_ref[...] += jnp.sum(dda * c["dt"], axis=0, keepdims=True) * c["a"]
        dxs_ref[...] = dv * c["dt_x"]
        dbc_ref[...] = jnp.concatenate(dk_l + dc_l, axis=1)

    ck = (lambda i: i) if reverse else (lambda i: nc - 1 - i)
    xs_spec = pl.BlockSpec((None, q, 1024), lambda n, i: (n, ck(i), 0))
    bc_spec = pl.BlockSpec((None, q, 1024), lambda n, i: (n, ck(i), 1))
    dt_spec = pl.BlockSpec((None, q, 128), lambda n, i: (n, ck(i), 0))
    al_spec = pl.BlockSpec((1, 128), lambda n, i: (0, 0))
    st_spec = pl.BlockSpec((None, None, 128, 1024), lambda n, i: (n, ck(i), 0, 0))
    return _pcall(body, name=f"ssd_bwd_r{int(reverse)}", grid=(nbatch, nc),
                  in_specs=[xs_spec, bc_spec, dt_spec, al_spec, st_spec, xs_spec],
                  out_specs=(xs_spec, xs_spec, dt_spec, al_spec),
                  out_shape=(jax.ShapeDtypeStruct((nbatch, s, 1024), F32), jax.ShapeDtypeStruct((nbatch, s, 1024), F32),
                             jax.ShapeDtypeStruct((nbatch, s, 128), F32), jax.ShapeDtypeStruct((1, 128), F32)),
                  scratch_shapes=[pltpu.VMEM((128, 1024), F32)], compiler_params=_params())(xbc3, xbc3, dt3, alog, st4, dy3)


def _gla_sub(q_ref, k_ref, g_ref, rs, reverse):
    sq = HGRN_SUB
    edge = 0 if reverse else sq - 1
    mask = _time_mask(sq, reverse)
    bc = _dot(mask.astype(F32), g_ref[rs, :], precision=HI)
    last = bc[edge:edge + 1, :]
    eb, enb, elb = jnp.exp(bc), jnp.exp(-bc), jnp.exp(last - bc)
    kv = k_ref[rs, :]
    return dict(mask=mask, edge=edge, eb=eb, enb=enb, elb=elb, e_l=jnp.exp(last),
                qt=q_ref[rs, :] * HGRN_SCALE * eb, kt=kv * enb, kb=kv * elb)


def _gla_specs(s, w, reverse_order):
    bq = min(HGRN_BLOCK, s)
    nblk = s // bq
    bi = (lambda i: nblk - 1 - i) if reverse_order else (lambda i: i)
    col = lambda cb: pl.BlockSpec((None, bq, w), lambda n, i: (n, bi(i), cb))
    st_spec = pl.BlockSpec((None, bq // HGRN_SUB, 128, w), lambda n, i: (n, bi(i), 0, 0))
    return bq, nblk, col, st_spec


def _gla_fwd(proj3, k3, g3, reverse):
    nbatch, s, w = k3.shape
    bq, nblk, col, st_spec = _gla_specs(s, w, reverse)
    nsub = bq // HGRN_SUB

    def body(q_ref, k_ref, v_ref, g_ref, o_ref, st_ref, st):
        @pl.when(pl.program_id(1) == 0)
        def _():
            st[...] = jnp.zeros_like(st)

        for j in (reversed(range(nsub)) if reverse else range(nsub)):
            rs = slice(j * HGRN_SUB, (j + 1) * HGRN_SUB)
            st_ref[j] = st[...]
            c = _gla_sub(q_ref, k_ref, g_ref, rs, reverse)
            v = v_ref[rs, :]
            for h in range(HGRN_HEADS):
                hs = slice(h * 128, (h + 1) * 128)
                qt, vb = _mx(c["qt"][:, hs]), _mx(v[:, hs])
                att = jnp.where(c["mask"], _dot(qt, _mx(c["kt"][:, hs]), _NT), 0.0)
                s0 = st[:, hs]
                o_ref[rs, hs] = _dot(_mx(att), vb) + _dot(qt, _mx(s0), _NT)
                st[:, hs] = s0 * c["e_l"][:, hs] + _dot(vb, _mx(c["kb"][:, hs]), _TN)

    return _pcall(body, name=f"gla_fwd_r{int(reverse)}", grid=(nbatch, nblk), in_specs=[col(0), col(0), col(3), col(0)],
                  out_specs=(col(0), st_spec),
                  out_shape=(jax.ShapeDtypeStruct((nbatch, s, w), F32),
                             jax.ShapeDtypeStruct((nbatch, s // HGRN_SUB, 128, w), F32)),
                  scratch_shapes=[pltpu.VMEM((128, w), F32)], compiler_params=_params())(proj3, k3, proj3, g3)


def _gla_bwd(proj3, k3, g3, st4, do3, reverse):
    nbatch, s, w = k3.shape
    bq, nblk, col, st_spec = _gla_specs(s, w, not reverse)
    nsub = bq // HGRN_SUB
    sq = HGRN_SUB

    def body(q_ref, k_ref, v_ref, g_ref, st_ref, do_ref, dq_ref, dk_ref, dv_ref, dg_ref, dst):
        @pl.when(pl.program_id(1) == 0)
        def _():
            dst[...] = jnp.zeros_like(dst)

        row = lax.broadcasted_iota(jnp.int32, (sq, 128), 0)
        for j in (range(nsub) if reverse else reversed(range(nsub))):
            rs = slice(j * sq, (j + 1) * sq)
            c = _gla_sub(q_ref, k_ref, g_ref, rs, reverse)
            s0_all, ds1_all = st_ref[j], dst[...]
            v, dy = v_ref[rs, :], do_ref[rs, :]
            db_l = []
            for h in range(HGRN_HEADS):
                hs = slice(h * 128, (h + 1) * 128)
                qt, kt, kb = c["qt"][:, hs], c["kt"][:, hs], c["kb"][:, hs]
                qtb, ktb, kbb, vb, dyb = _mx(qt), _mx(kt), _mx(kb), _mx(v[:, hs]), _mx(dy[:, hs])
                s0, ds1 = s0_all[:, hs], ds1_all[:, hs]
                att = jnp.where(c["mask"], _dot(qtb, ktb, _NT), 0.0)
                datt = _mx(jnp.where(c["mask"], _dot(dyb, vb, _NT), 0.0))
                dqt = _dot(datt, ktb) + _dot(dyb, _mx(s0))
                dkt = _dot(datt, qtb, _TN)
                dkb = _dot(vb, _mx(ds1))
                dv_ref[rs, hs] = _dot(_mx(att), dyb, _TN) + _dot(kbb, _mx(ds1), _NT)
                dst[:, hs] = c["e_l"][:, hs] * ds1 + _dot(dyb, qtb, _TN)
                dq_ref[rs, hs] = dqt * c["eb"][:, hs] * HGRN_SCALE
                dk_ref[rs, hs] = dkt * c["enb"][:, hs] + dkb * c["elb"][:, hs]
                kbk = dkb * kb
                dlast = jnp.sum(kbk, axis=0, keepdims=True) + c["e_l"][:, hs] * jnp.sum(ds1 * s0, axis=0, keepdims=True)
                db_l.append(dqt * qt - dkt * kt - kbk + jnp.where(row == c["edge"], dlast, 0.0))
            dg_ref[rs, :] = _dot(c["mask"].astype(F32), jnp.concatenate(db_l, axis=1), _TN, precision=HI)

    shp = jax.ShapeDtypeStruct((nbatch, s, w), F32)
    return _pcall(body, name=f"gla_bwd_r{int(reverse)}", grid=(nbatch, nblk),
                  in_specs=[col(0), col(0), col(3), col(0), st_spec, col(0)],
                  out_specs=(col(0),) * 4, out_shape=(shp,) * 4,
                  scratch_shapes=[pltpu.VMEM((128, w), F32)], compiler_params=_params())(proj3, k3, proj3, g3, st4, do3)


DIRS = (False, True)


def _block_diag(w):
    eye = jnp.eye(16, dtype=w.dtype)
    return (eye[:, None, :, None] * w[:, :, None, :]).reshape(1024, 1024)


def _diag_blocks(m):
    m4 = m.reshape(16, 64, 16, 64)
    return jnp.stack([m4[i, :, i, :] for i in range(16)], axis=0)


def _pad_lanes(v, n=128):
    return jnp.pad(v, [(0, 0)] * (v.ndim - 1) + [(0, n - v.shape[-1])])


def _mlp_fwd(tag, x, nw, w1, w2):
    (h,) = _pw_fwd(f"{tag}_norm", _f_norm, [(x, 0)], [(nw, 0)], [BF16], 1024, 1)
    a, r = _mm(f"{tag}_up", h, w1, "nn", relu2=True)
    return _mm(f"{tag}_down", r, w2, "nn", res=x), (h, a, r)


def _mlp_bwd(tag, x, nw, w1, w2, saved, dxo):
    h, a, r = saved
    dw2 = _mm(f"{tag}_dw2", r, dxo, "tn")
    dr = _mm(f"{tag}_dr", dxo, w2, "nt")
    (da,), _ = _pw_bwd(f"{tag}_dact", _f_relu2, [(a, 0)], [], [dr], 1024, a.shape[1] // 1024, [0])
    dw1 = _mm(f"{tag}_dw1", h, da, "tn", col_shards=4)
    dh = _mm(f"{tag}_dh", da, w1, "nt")
    (dx,), (dnw,) = _pw_bwd(f"{tag}_dnorm", _f_norm, [(x, 0)], [(nw, 0)], [dh], 1024, 1, [0], adds={0: dxo})
    return dx, dw1, dw2, dnw


def _split_in0(pieces, dt_piece):
    tm = 256

    def body(p0, p1, p2, p3, p4, p5, o_ref):
        full = jnp.concatenate([p0[...], p1[...], p2[...], p3[...], p4[...], p5[:, :32]], axis=1)
        for j in range(4):
            o_ref[j] = full[:, 1288 * j:1288 * (j + 1)]

    blk = pl.BlockSpec((tm, 1024), lambda i: (i, 0))
    return _pcall(body, name="split_in0", grid=(1024 // tm,), in_specs=[blk] * 5 + [pl.BlockSpec((tm, 128), lambda i: (i, 0))],
                  out_specs=pl.BlockSpec((4, tm, 1288), lambda i: (0, i, 0)),
                  out_shape=jax.ShapeDtypeStruct((4, 1024, 1288), F32), compiler_params=_params())(*pieces, dt_piece)


def _assemble_in0(shards):
    tm = 256

    def body(s_ref, m_ref, d_ref):
        full = jnp.concatenate([s_ref[j] for j in range(4)], axis=1)
        m_ref[...] = full[:, :5120]
        d_ref[...] = jnp.concatenate([full[:, 5120:5152], jnp.zeros((tm, 96), full.dtype)], axis=1)

    return _pcall(body, name="assemble_in0", grid=(1024 // tm,), in_specs=[pl.BlockSpec((4, tm, 1288), lambda i: (0, i, 0))],
                  out_specs=(pl.BlockSpec((tm, 5120), lambda i: (i, 0)), pl.BlockSpec((tm, 128), lambda i: (i, 0))),
                  out_shape=(jax.ShapeDtypeStruct((1024, 5120), shards.dtype), jax.ShapeDtypeStruct((1024, 128), shards.dtype)),
                  compiler_params=_params())(shards)


def _local_step(x3, tgt3, w, w_main0, w_dt0):
    nb, s, d = x3.shape
    t = nb * s
    x0 = x3.reshape(t, d)
    tgt = tgt3.reshape(t, d)
    grads = {}
    row = lambda v: v.reshape(1, -1)
    to3 = lambda v: v.reshape(nb, s, v.shape[-1])
    to2 = lambda v: v.reshape(-1, v.shape[-1])

    conv_w, conv_b = w["even_conv_w"][0], row(w["even_conv_b"][0])
    nmix0 = row(w["norm_mix"][0])
    (h0,) = _pw_fwd("l0_norm", _f_norm, [(x0, 0)], [(nmix0, 0)], [BF16], 1024, 1)
    proj0 = _mm("l0_proj", h0, w_main0, "nn")
    dt_raw = _mm("l0_proj_dt", h0, w_dt0, "nn")
    conv = to2(_conv_fwd(to3(proj0), conv_w, conv_b, 3))
    (xbc,) = _pw_fwd("l0_silu", _f_silu, [(conv, 0)], [], [F32], 1024, 2)
    dt_bias = _pad_lanes(w["ssd_dt_bias"][0].reshape(1, 32))
    (dt,) = _pw_fwd("l0_dt", _f_softplus, [(dt_raw, 0)], [(dt_bias, 0)], [F32], 128, 1)
    dt3, xbc3 = to3(dt), to3(xbc)
    alog = _pad_lanes(w["ssd_a_log"][0].reshape(1, 32))
    ssd = [_ssd_fwd(xbc3, dt3, alog, r) for r in DIRS]
    yf, yb = to2(ssd[0][0]), to2(ssd[1][0])
    dskip = jnp.repeat(w["ssd_d"][0], SSD_HEADDIM).reshape(1, 1024)
    snw = row(w["ssd_norm_w"][0])
    ssd_ins = [(yf, 0), (yb, 0), (xbc, 0), (proj0, 12)]
    (ya,) = _pw_fwd("l0_ssd_post", _f_ssd_post, ssd_ins, [(dskip, 0), (snw, 0)], [BF16], 256, 4)
    u_lru = conv[:, 2048:]
    w_gates = [_block_diag(w[k][0, r]).astype(MXU_DTYPE) for r in range(2) for k in ("lru_w_a", "lru_w_x")]
    pre = [_mm(f"l0_lru_pre{i}", u_lru, wg, "nn") for i, wg in enumerate(w_gates)]
    lru_par = [[(row(w[k][0, r]), 0) for k in ("lru_b_a", "lru_b_x", "lru_lambda")] for r in range(2)]
    lru_ins = [[(pre[2 * r], 0), (pre[2 * r + 1], 0), (u_lru, 0)] for r in range(2)]
    ab = [_pw_fwd(f"l0_lru_gates{r}", _f_lru_gates, lru_ins[r], lru_par[r], [F32, F32], 1024, 1) for r in range(2)]
    hs = [_lru_scan(to3(ab[r][0]), to3(ab[r][1]), DIRS[r]) for r in range(2)]
    lru_post_ins = [(to2(hs[0]), 0), (to2(hs[1]), 0), (proj0, 4)]
    (ybm,) = _pw_fwd("l0_lru_post", _f_lru_post, lru_post_ins, [], [BF16], 1024, 1)
    w_out0 = w["even_w_out"][0]
    x1 = _mm("l0_out_a", ya, w_out0[:1024], "nn", res=x0)
    x1 = _mm("l0_out_b", ybm, w_out0[1024:], "nn", res=x1)
    nmlp0 = row(w["norm_mlp"][0])
    x2, mlp0 = _mlp_fwd("l0_mlp", x1, nmlp0, w["mlp_w1"][0], w["mlp_w2"][0])

    w_in1 = w["odd_w_in"][0]
    nmix1 = row(w["norm_mix"][1])
    (h1,) = _pw_fwd("l1_norm", _f_norm, [(x2, 0)], [(nmix1, 0)], [BF16], 1024, 1)
    proj1 = _mm("l1_proj", h1, w_in1, "nn")
    proj1_3 = to3(proj1)
    lb0, lb1 = row(w["hgrn_lb_logits"][0]), row(w["hgrn_lb_logits"][1])
    kg = [_pw_fwd(f"l1_hgrn_pre{r}", _f_hgrn_pre, [(proj1, 1 + r)], [(lb0, 0), (lb1, 0)], [F32, F32], 1024, 1)
          for r in range(2)]
    gla = [_gla_fwd(proj1_3, to3(kg[r][0]), to3(kg[r][1]), DIRS[r]) for r in range(2)]
    hnw = row(w["hgrn_norm_w"][0])
    hpost_ins = [(to2(gla[0][0]), 0), (to2(gla[1][0]), 0), (proj1, 32)]
    (yo,) = _pw_fwd("l1_hgrn_post", _f_hgrn_post, hpost_ins, [(hnw, 0)], [BF16], 128, 8)
    w_out1 = w["odd_w_out"][0]
    x3_ = _mm("l1_out", yo, w_out1, "nn", res=x2)
    nmlp1 = row(w["norm_mlp"][1])
    x4, mlp1 = _mlp_fwd("l1_mlp", x3_, nmlp1, w["mlp_w1"][1], w["mlp_w2"][1])

    dx4, dnf, loss = _loss_head(x4, tgt, row(w["norm_final"]))
    grads["norm_final"] = dnf.reshape(-1)

    dx3, dw1_1, dw2_1, dnmlp1 = _mlp_bwd("l1_mlp", x3_, nmlp1, w["mlp_w1"][1], w["mlp_w2"][1], mlp1, dx4)
    big = {"odd_w_out": _mm("l1_dwout", yo, dx3, "tn").reshape(4, 256, 1024)}
    dyo = _mm("l1_dyo", dx3, w_out1, "nt")
    (do, dgate1), (dhnw,) = _pw_bwd("l1_hgrn_post_b", _f_hgrn_post, hpost_ins, [(hnw, 0)], [dyo], 128, 8, [0, 2])
    grads["hgrn_norm_w"] = dhnw
    do3 = to3(do)
    gb = [_gla_bwd(proj1_3, to3(kg[r][0]), to3(kg[r][1]), gla[r][1], do3, DIRS[r]) for r in range(2)]
    (dq,) = _pw_fwd("l1_dq", _f_add2, [(to2(gb[0][0]), 0), (to2(gb[1][0]), 0)], [], [F32], 1024, 1)
    (dvv,) = _pw_fwd("l1_dv", _f_add2, [(to2(gb[0][2]), 0), (to2(gb[1][2]), 0)], [], [F32], 1024, 1)
    dfr, dl0, dl1 = [], [], []
    for r in range(2):
        (df,), (a0, a1) = _pw_bwd(f"l1_hgrn_pre_b{r}", _f_hgrn_pre, [(proj1, 1 + r)], [(lb0, 0), (lb1, 0)],
                                  [to2(gb[r][1]), to2(gb[r][3])], 1024, 1, [0])
        dfr.append(df)
        dl0.append(a0)
        dl1.append(a1)
    grads["hgrn_lb_logits"] = jnp.concatenate([dl0[0] + dl0[1], dl1[0] + dl1[1]], axis=0)
    dparts1 = [dq, dfr[0], dfr[1], dvv, dgate1]
    dwin1 = jnp.concatenate([_mm(f"l1_dwin{i}", h1, dp, "tn") for i, dp in enumerate(dparts1)], axis=1)
    big["odd_w_in"] = dwin1.reshape(1024, 4, 1280).transpose(1, 0, 2)
    dh1 = None
    for i, dp in enumerate(dparts1):
        dh1 = _mm(f"l1_dh{i}", dp, w_in1[:, i * 1024:(i + 1) * 1024], "nt", res=dh1)
    (dx2,), (dnmix1,) = _pw_bwd("l1_dnorm", _f_norm, [(x2, 0)], [(nmix1, 0)], [dh1], 1024, 1, [0], adds={0: dx3})

    dx1, dw1_0, dw2_0, dnmlp0 = _mlp_bwd("l0_mlp", x1, nmlp0, w["mlp_w1"][0], w["mlp_w2"][0], mlp0, dx2)
    big["mlp_w1"] = jnp.concatenate([dw1_0, dw1_1], axis=1)
    big["mlp_w2"] = jnp.concatenate([dw2_0.reshape(4, 1024, 1024), dw2_1.reshape(4, 1024, 1024)], axis=1)
    grads["norm_mlp"] = jnp.concatenate([dnmlp0, dnmlp1], axis=0)
    big["even_w_out"] = jnp.concatenate([_mm("l0_dwout_a", ya, dx1, "tn"), _mm("l0_dwout_b", ybm, dx1, "tn")],
                                        axis=0).reshape(4, 512, 1024)
    dya = _mm("l0_dya", dx1, w_out0[:1024], "nt")
    dyb = _mm("l0_dyb", dx1, w_out0[1024:], "nt")
    (dh, dgate0), _ = _pw_bwd("l0_lru_post_b", _f_lru_post, lru_post_ins, [], [dyb], 1024, 1, [0, 2])
    dh3 = to3(dh)
    dpre, du_parts, dlru = [], [], {k: [] for k in ("lru_b_a", "lru_b_x", "lru_lambda")}
    for r in range(2):
        g_r, da_r = _lru_scan_bwd(to3(ab[r][0]), hs[r], dh3, DIRS[r])
        (dpa, dpx, du_r), (dba, dbx, dlam) = _pw_bwd(f"l0_lru_gates_b{r}", _f_lru_gates, lru_ins[r], lru_par[r],
                                                     [to2(da_r), to2(g_r)], 1024, 1, [0, 1, 2])
        dpre += [dpa, dpx]
        du_parts.append(du_r)
        dlru["lru_b_a"].append(dba)
        dlru["lru_b_x"].append(dbx)
        dlru["lru_lambda"].append(dlam)
    for k, v in dlru.items():
        grads[k] = jnp.concatenate(v, axis=0)[None]
    dwg = [_diag_blocks(_mm(f"l0_dwgate{i}", u_lru, dp, "tn")) for i, dp in enumerate(dpre)]
    grads["lru_w_a"] = jnp.stack([dwg[0], dwg[2]])[None]
    grads["lru_w_x"] = jnp.stack([dwg[1], dwg[3]])[None]
    (du,) = _pw_fwd("l0_du", _f_add2, [(du_parts[0], 0), (du_parts[1], 0)], [], [F32], 1024, 1)
    for i, dp in enumerate(dpre):
        du = _mm(f"l0_du_gate{i}", dp, w_gates[i], "nt", res=du)
    (dy, dxs_skip, dz), (ddskip, dsnw) = _pw_bwd("l0_ssd_post_b", _f_ssd_post, ssd_ins, [(dskip, 0), (snw, 0)], [dya],
                                                 256, 4, [0, 2, 3])
    grads["ssd_d"] = ddskip.reshape(SSD_HEADS, SSD_HEADDIM).sum(axis=1)[None]
    grads["ssd_norm_w"] = dsnw
    dy3 = to3(dy)
    sb = [_ssd_bwd(xbc3, dt3, alog, ssd[r][1], dy3, DIRS[r]) for r in range(2)]
    grads["ssd_a_log"] = (sb[0][3] + sb[1][3])[:, :32].reshape(1, 2, 16)
    (dxs,) = _pw_fwd("l0_dxs", _f_add3, [(to2(sb[0][0]), 0), (to2(sb[1][0]), 0), (dxs_skip, 0)], [], [F32], 1024, 1)
    (dbc,) = _pw_fwd("l0_dbc", _f_add2, [(to2(sb[0][1]), 0), (to2(sb[1][1]), 0)], [], [F32], 1024, 1)
    dxbc = jnp.concatenate([dxs, dbc], axis=1)
    (dconv_a,), _ = _pw_bwd("l0_silu_b", _f_silu, [(conv, 0)], [], [dxbc], 1024, 2, [0])
    (ddt,) = _pw_fwd("l0_ddt", _f_add2, [(to2(sb[0][2]), 0), (to2(sb[1][2]), 0)], [], [F32], 128, 1)
    (ddt_raw,), (ddtb,) = _pw_bwd("l0_dt_b", _f_softplus, [(dt_raw, 0)], [(dt_bias, 0)], [ddt], 128, 1, [0])
    grads["ssd_dt_bias"] = ddtb[:, :32].reshape(1, 2, 16)
    dconv = jnp.concatenate([dconv_a, du], axis=1)
    dproj_c, dcw = _conv_bwd(to3(dconv), to3(proj0), conv_w, 3)
    grads["even_conv_w"] = dcw[:4][None]
    grads["even_conv_b"] = dcw[4:5]
    dparts0 = [to2(dproj_c)[:, :1024], to2(dproj_c)[:, 1024:2048], to2(dproj_c)[:, 2048:], dz, dgate0]
    dwin0 = [_mm(f"l0_dwin{i}", h0, dp, "tn") for i, dp in enumerate(dparts0)]
    big["even_w_in"] = _split_in0(dwin0, _mm("l0_dwin_dt", h0, ddt_raw, "tn"))
    dh0 = _mm("l0_dh_dt", ddt_raw, w_dt0, "nt")
    for i, dp in enumerate(dparts0):
        dh0 = _mm(f"l0_dh{i}", dp, w_main0[:, i * 1024:(i + 1) * 1024], "nt", res=dh0)
    (dx0,), (dnmix0,) = _pw_bwd("l0_dnorm", _f_norm, [(x0, 0)], [(nmix0, 0)], [dh0], 1024, 1, [0], adds={0: dx1})
    grads["norm_mix"] = jnp.concatenate([dnmix0, dnmix1], axis=0)
    return loss, dx0.reshape(nb, s, d), grads, [big[n] for n in BIG]


ANY = pl.BlockSpec(memory_space=pl.ANY)


def _place():
    return lax.axis_index("x"), lax.axis_index("y"), lax.axis_index("c")


def _remote(src, dst, send_sems, recv_sems, k, to):
    return pltpu.make_async_remote_copy(src_ref=src, dst_ref=dst, send_sem=send_sems.at[k], recv_sem=recv_sems.at[k],
                                        device_id=to, device_id_type=MESH)


def _gather_chips(shards):
    n = len(shards)
    halves = [s.shape[0] // 2 for s in shards]

    def body(*refs):
        x_refs, out_refs = refs[:n], refs[n:2 * n]
        send_sems, recv_sems, local_sems = refs[2 * n:]
        x, y, c = _place()
        sibling = (x, y, 1 - c)
        chips = [(1 - x, y), (x, 1 - y), (1 - x, 1 - y)]

        def blk(t, px, py, hc):
            return out_refs[t].at[2 * px + py, pl.ds(hc * halves[t], halves[t]), :]

        def src(t):
            return x_refs[t].at[pl.ds(c * halves[t], halves[t]), :]

        mine = [pltpu.make_async_copy(x_refs[t], out_refs[t].at[2 * x + y], local_sems.at[t]) for t in range(n)]
        first = [_remote(src(t), blk(t, x, y, c), send_sems, recv_sems, 6 * t + j, (*chip, c))
                 for t in range(n) for j, chip in enumerate(chips)]
        for cp in mine + first:
            cp.start()
        passed = []
        for t in range(n):
            for j, chip in enumerate(chips):
                _remote(src(t), blk(t, *chip, c), send_sems, recv_sems, 6 * t + j, (*chip, c)).wait_recv()
                cp = _remote(blk(t, *chip, c), blk(t, *chip, c), send_sems, recv_sems, 6 * t + 3 + j, sibling)
                cp.start()
                passed.append(cp)
        for t in range(n):
            for j, chip in enumerate(chips):
                _remote(src(t), blk(t, *chip, 1 - c), send_sems, recv_sems, 6 * t + 3 + j, sibling).wait_recv()
        for cp in first + passed:
            cp.wait_send()
        for cp in mine:
            cp.wait()

    return _pcall(body, name="gather_weights", in_specs=[ANY] * n, out_specs=(ANY,) * n,
                  out_shape=tuple(jax.ShapeDtypeStruct((4,) + s.shape, s.dtype) for s in shards),
                  scratch_shapes=[pltpu.SemaphoreType.DMA((6 * n,)), pltpu.SemaphoreType.DMA((6 * n,)),
                                  pltpu.SemaphoreType.DMA((n,))],
                  compiler_params=_params())(*shards)


def _pair_swap(gps):
    n = len(gps)
    halves = [g.shape[1] // 2 for g in gps]

    def body(*refs):
        g_refs, land_refs = refs[:n], refs[n:2 * n]
        send_sems, recv_sems = refs[2 * n:]
        x, y, c = _place()
        cps = [_remote(g_refs[t].at[j, pl.ds((1 - c) * halves[t], halves[t]), :], land_refs[t].at[j], send_sems, recv_sems,
                       4 * t + j, (x, y, 1 - c)) for t in range(n) for j in range(4)]
        for cp in cps:
            cp.start()
        for cp in cps:
            cp.wait()

    return _pcall(body, name="grad_pair_swap", in_specs=[ANY] * n, out_specs=(ANY,) * n,
                  out_shape=tuple(jax.ShapeDtypeStruct((4, h, g.shape[2]), F32) for g, h in zip(gps, halves)),
                  scratch_shapes=[pltpu.SemaphoreType.DMA((4 * n,)), pltpu.SemaphoreType.DMA((4 * n,))],
                  compiler_params=_params())(*gps)


def _pair_add(name, gp, land, cidx):
    _, half, cols = land.shape
    tr = _tile(half, 512)
    nh = half // tr

    def body(c_ref, g_ref, l_ref, o_ref):
        o_ref[...] = g_ref[...] + l_ref[...]

    grid_spec = pltpu.PrefetchScalarGridSpec(
        num_scalar_prefetch=1, grid=(4, nh),
        in_specs=[pl.BlockSpec((None, tr, cols), lambda j, i, c: (j, c[0] * nh + i, 0)),
                  pl.BlockSpec((None, tr, cols), lambda j, i, c: (j, i, 0))],
        out_specs=pl.BlockSpec((None, tr, cols), lambda j, i, c: (j, i, 0)))
    return _pcall(body, name=f"pair_add_{name}", grid_spec=grid_spec, out_shape=jax.ShapeDtypeStruct((4, half, cols), F32),
                  compiler_params=_params())(cidx, gp, land)


def _chip_scatter(css):
    n = len(css)

    def body(*refs):
        s_refs, land_refs = refs[:n], refs[n:2 * n]
        send_sems, recv_sems, local_sems = refs[2 * n:]
        x, y, c = _place()
        me = 2 * x + y
        chips = [(1 - x, y), (x, 1 - y), (1 - x, 1 - y)]
        mine = [pltpu.make_async_copy(s_refs[t].at[me], land_refs[t].at[me], local_sems.at[t]) for t in range(n)]
        cps = [_remote(s_refs[t].at[2 * px + py], land_refs[t].at[me], send_sems, recv_sems, 3 * t + j, (px, py, c))
               for t in range(n) for j, (px, py) in enumerate(chips)]
        for cp in mine + cps:
            cp.start()
        for t in range(n):
            for j, (px, py) in enumerate(chips):
                _remote(s_refs[t].at[me], land_refs[t].at[2 * px + py], send_sems, recv_sems, 3 * t + j, (px, py, c)).wait_recv()
        for cp in cps:
            cp.wait_send()
        for cp in mine:
            cp.wait()

    return _pcall(body, name="grad_chip_scatter", in_specs=[ANY] * n, out_specs=(ANY,) * n,
                  out_shape=tuple(jax.ShapeDtypeStruct(s.shape, F32) for s in css),
                  scratch_shapes=[pltpu.SemaphoreType.DMA((3 * n,)), pltpu.SemaphoreType.DMA((3 * n,)),
                                  pltpu.SemaphoreType.DMA((n,))],
                  compiler_params=_params())(*css)


def _chip_sum(name, land):
    _, half, cols = land.shape
    tr = _tile(half, 512)

    def body(l_ref, o_ref):
        o_ref[...] = ((l_ref[0] + l_ref[1]) + l_ref[2]) + l_ref[3]

    return _pcall(body, name=f"chip_sum_{name}", grid=(half // tr,),
                  in_specs=[pl.BlockSpec((4, tr, cols), lambda i: (0, i, 0))],
                  out_specs=pl.BlockSpec((tr, cols), lambda i: (i, 0)),
                  out_shape=jax.ShapeDtypeStruct((half, cols), F32), compiler_params=_params())(land)


def _pair_join(reds):
    n = len(reds)

    def body(*refs):
        r_refs, out_refs = refs[:n], refs[n:2 * n]
        send_sems, recv_sems, local_sems = refs[2 * n:]
        x, y, c = _place()
        mine = [pltpu.make_async_copy(r_refs[t], out_refs[t].at[c], local_sems.at[t]) for t in range(n)]
        cps = [_remote(r_refs[t], out_refs[t].at[c], send_sems, recv_sems, t, (x, y, 1 - c)) for t in range(n)]
        for cp in mine + cps:
            cp.start()
        for t in range(n):
            _remote(r_refs[t], out_refs[t].at[1 - c], send_sems, recv_sems, t, (x, y, 1 - c)).wait_recv()
        for cp in cps:
            cp.wait_send()
        for cp in mine:
            cp.wait()

    return _pcall(body, name="grad_pair_join", in_specs=[ANY] * n, out_specs=(ANY,) * n,
                  out_shape=tuple(jax.ShapeDtypeStruct((2,) + r.shape, F32) for r in reds),
                  scratch_shapes=[pltpu.SemaphoreType.DMA((n,)), pltpu.SemaphoreType.DMA((n,)), pltpu.SemaphoreType.DMA((n,))],
                  compiler_params=_params())(*reds)


def _adamw(name, g, w, m, v):
    rows, cols = g.shape
    tr = _tile(rows, 512)

    def body(g_ref, w_ref, m_ref, v_ref, d_ref, mo_ref, vo_ref):
        gv = g_ref[...]
        mn = ADAM_B1 * m_ref[...] + (1.0 - ADAM_B1) * gv
        vn = ADAM_B2 * v_ref[...] + (1.0 - ADAM_B2) * jnp.square(gv)
        m_hat = mn / (1.0 - ADAM_B1 ** ADAM_STEP)
        v_hat = vn / (1.0 - ADAM_B2 ** ADAM_STEP)
        d_ref[...] = -ADAM_LR * (m_hat / (jnp.sqrt(v_hat) + ADAM_EPS) + ADAM_WD * w_ref[...])
        mo_ref[...] = mn
        vo_ref[...] = vn

    blk = pl.BlockSpec((tr, cols), lambda i: (i, 0))
    shp = jax.ShapeDtypeStruct((rows, cols), F32)
    return _pcall(body, name=f"adamw_{name}", grid=(rows // tr,), in_specs=[blk] * 4, out_specs=(blk,) * 3,
                  out_shape=(shp,) * 3, compiler_params=_params())(g, w, m, v)


def _pack(pieces, rows, dtype):
    flat = jnp.concatenate([p.reshape(-1).astype(dtype) for p in pieces])
    return jnp.pad(flat, (0, rows * PACK_COLS - flat.shape[0])).reshape(rows, PACK_COLS)


def _unpack(pack, shapes):
    flat = pack.reshape(-1)
    out, off = [], 0
    for shp in shapes:
        n = math.prod(shp)
        out.append(flat[off:off + n].reshape(shp))
        off += n
    return out


def _shard_of(full, axis, j):
    n = full.shape[axis] // 4
    return lax.slice_in_dim(full, j * n, (j + 1) * n, axis=axis)


def kernel(x, even_w_in, even_conv_w, even_conv_b, ssd_a_log, ssd_dt_bias, ssd_d, ssd_norm_w, lru_w_a, lru_b_a, lru_w_x, lru_b_x, lru_lambda, even_w_out, odd_w_in, hgrn_lb_logits, hgrn_norm_w, odd_w_out, norm_mix, norm_mlp, mlp_w1, mlp_w2, norm_final, loss_target, m_even_w_in, m_even_conv_w, m_even_conv_b, m_ssd_a_log, m_ssd_dt_bias, m_ssd_d, m_ssd_norm_w, m_lru_w_a, m_lru_b_a, m_lru_w_x, m_lru_b_x, m_lru_lambda, m_even_w_out, m_odd_w_in, m_hgrn_lb_logits, m_hgrn_norm_w, m_odd_w_out, m_norm_mix, m_norm_mlp, m_mlp_w1, m_mlp_w2, m_norm_final, v_even_w_in, v_even_conv_w, v_even_conv_b, v_ssd_a_log, v_ssd_dt_bias, v_ssd_d, v_ssd_norm_w, v_lru_w_a, v_lru_b_a, v_lru_w_x, v_lru_b_x, v_lru_lambda, v_even_w_out, v_odd_w_in, v_hgrn_lb_logits, v_hgrn_norm_w, v_odd_w_out, v_norm_mix, v_norm_mlp, v_mlp_w1, v_mlp_w2, v_norm_final):
    names = [n for n, _, _, _ in WEIGHTS]
    w_loc = dict(zip(names, (even_w_in, even_conv_w, even_conv_b, ssd_a_log, ssd_dt_bias, ssd_d, ssd_norm_w, lru_w_a, lru_b_a, lru_w_x, lru_b_x, lru_lambda, even_w_out, odd_w_in, hgrn_lb_logits, hgrn_norm_w, odd_w_out, norm_mix, norm_mlp, mlp_w1, mlp_w2, norm_final)))
    m_loc = dict(zip(names, (m_even_w_in, m_even_conv_w, m_even_conv_b, m_ssd_a_log, m_ssd_dt_bias, m_ssd_d, m_ssd_norm_w, m_lru_w_a, m_lru_b_a, m_lru_w_x, m_lru_b_x, m_lru_lambda, m_even_w_out, m_odd_w_in, m_hgrn_lb_logits, m_hgrn_norm_w, m_odd_w_out, m_norm_mix, m_norm_mlp, m_mlp_w1, m_mlp_w2, m_norm_final)))
    v_loc = dict(zip(names, (v_even_w_in, v_even_conv_w, v_even_conv_b, v_ssd_a_log, v_ssd_dt_bias, v_ssd_d, v_ssd_norm_w, v_lru_w_a, v_lru_b_a, v_lru_w_x, v_lru_b_x, v_lru_lambda, v_even_w_out, v_odd_w_in, v_hgrn_lb_logits, v_hgrn_norm_w, v_odd_w_out, v_norm_mix, v_norm_mlp, v_mlp_w1, v_mlp_w2, v_norm_final)))
    spec = {n: (blk, full, ax) for n, blk, full, ax in WEIGHTS}

    small = [n for n in names if n not in BIG]
    two_d = lambda n, v: v.reshape(BIG_2D[n])

    gathered = _gather_chips([two_d(n, w_loc[n]).astype(BF16) for n in BIG]
                             + [_pack([w_loc[n] for n in SMALL_SHARDED], 16, F32)])
    g_in0, g_out0, g_in1, g_out1, g_w1, g_w2, g_small = gathered
    w_main0, w_dt0 = _assemble_in0(g_in0)
    w_full = {n: w_loc[n] for n in names if spec[n][2] is None}
    w_full["even_w_out"] = g_out0.reshape(1, 2048, 1024)
    w_full["odd_w_in"] = jnp.concatenate([g_in1[j] for j in range(4)], axis=1)[None]
    w_full["odd_w_out"] = g_out1.reshape(1, 1024, 1024)
    w_full["mlp_w1"] = jnp.stack([jnp.concatenate([g_w1[j, l * 1024:(l + 1) * 1024] for j in range(4)], axis=1) for l in range(2)])
    w_full["mlp_w2"] = jnp.stack([jnp.concatenate([g_w2[j, l * 1024:(l + 1) * 1024] for j in range(4)], axis=0) for l in range(2)])
    shards = [_unpack(g_small[j], [spec[n][0] for n in SMALL_SHARDED]) for j in range(4)]
    for i, n in enumerate(SMALL_SHARDED):
        w_full[n] = jnp.concatenate([shards[j][i] for j in range(4)], axis=spec[n][2])

    loss_vec, grad_x, grads, big = _local_step(x, loss_target, w_full, w_main0, w_dt0)
    loss = lax.psum(loss_vec[0, 0], ("x", "y", "c"))

    def dest_pack(j):
        return _pack([grads[n].reshape(spec[n][1]) if spec[n][2] is None else _shard_of(grads[n].reshape(spec[n][1]), spec[n][2], j)
                      for n in small], SMALL_ROWS, F32)

    tensors = big + [jnp.stack([dest_pack(j) for j in range(4)])]
    tags = list(BIG) + ["small"]
    cidx = lax.axis_index("c").astype(jnp.int32).reshape(1)
    chip_sums = [_pair_add(tag, g, land, cidx) for tag, g, land in zip(tags, tensors, _pair_swap(tensors))]
    halves = [_chip_sum(tag, land) for tag, land in zip(tags, _chip_scatter(chip_sums))]
    reduced = [r.reshape(-1, r.shape[-1]) for r in _pair_join(halves)]

    outs = {}
    for n, g in zip(BIG, reduced[:-1]):
        res = (g, *_adamw(n, g, two_d(n, w_loc[n]), two_d(n, m_loc[n]), two_d(n, v_loc[n])))
        outs[n] = [r.reshape(spec[n][0]) for r in res]
    blocks = [spec[n][0] for n in small]
    wp, mp, vp = (_pack([src[n] for n in small], SMALL_ROWS, F32) for src in (w_loc, m_loc, v_loc))
    res = (reduced[-1], *_adamw("small", reduced[-1], wp, mp, vp))
    unpacked = [_unpack(r, blocks) for r in res]
    for i, n in enumerate(small):
        outs[n] = [u[i] for u in unpacked]
    return (loss, grad_x, *[outs[n][k] for k in range(4) for n in names])
```

```python
import functools
import math

import jax
import jax.numpy as jnp
from jax import lax
from jax.experimental import pallas as pl
from jax.experimental.pallas import tpu as pltpu

F32 = jnp.float32
BF16 = jnp.bfloat16
MXU_DTYPE = jnp.bfloat16
HI = lax.Precision.HIGHEST
MESH = pl.DeviceIdType.MESH

D_MODEL = 1024
EPS = 1e-6
SSD_HEADS = 16
SSD_HEADDIM = 64
HEAD_SHIFT = 6
SSD_GROUPS = 4
SSD_STATE = 128
SSD_CHUNK = 128
LRU_C = 8.0
LRU_ROWS = 256
HGRN_HEADS = 8
HGRN_HEADDIM = 128
HGRN_SUB = 32
HGRN_BLOCK = 128
HGRN_SCALE = HGRN_HEADDIM ** -0.5
CONV_ROWS = 512

ADAM_LR = 0.001
ADAM_B1 = 0.9
ADAM_B2 = 0.999
ADAM_EPS = 1e-08
ADAM_WD = 0.01
ADAM_STEP = 10

VMEM_LIMIT = 56 * 1024 * 1024
PACK_COLS = 1024
SMALL_ROWS = 288

WEIGHTS = (
    ("even_w_in", (1, 1024, 1288), (1, 1024, 5152), 2),
    ("even_conv_w", (1, 4, 768), (1, 4, 3072), 2),
    ("even_conv_b", (1, 3072), (1, 3072), None),
    ("ssd_a_log", (1, 2, 16), (1, 2, 16), None),
    ("ssd_dt_bias", (1, 2, 16), (1, 2, 16), None),
    ("ssd_d", (1, 16), (1, 16), None),
    ("ssd_norm_w", (1, 1024), (1, 1024), None),
    ("lru_w_a", (1, 2, 16, 64, 64), (1, 2, 16, 64, 64), None),
    ("lru_b_a", (1, 2, 256), (1, 2, 1024), 2),
    ("lru_w_x", (1, 2, 16, 64, 64), (1, 2, 16, 64, 64), None),
    ("lru_b_x", (1, 2, 256), (1, 2, 1024), 2),
    ("lru_lambda", (1, 2, 256), (1, 2, 1024), 2),
    ("even_w_out", (1, 512, 1024), (1, 2048, 1024), 1),
    ("odd_w_in", (1, 1024, 1280), (1, 1024, 5120), 2),
    ("hgrn_lb_logits", (2, 1024), (2, 1024), None),
    ("hgrn_norm_w", (1, 256), (1, 1024), 1),
    ("odd_w_out", (1, 256, 1024), (1, 1024, 1024), 1),
    ("norm_mix", (2, 1024), (2, 1024), None),
    ("norm_mlp", (2, 1024), (2, 1024), None),
    ("mlp_w1", (2, 1024, 1024), (2, 1024, 4096), 2),
    ("mlp_w2", (2, 1024, 1024), (2, 4096, 1024), 1),
    ("norm_final", (1024,), (1024,), None),
)
BIG = ("even_w_in", "even_w_out", "odd_w_in", "odd_w_out", "mlp_w1", "mlp_w2")
BIG_2D = {"even_w_in": (1024, 1288), "even_w_out": (512, 1024), "odd_w_in": (1024, 1280), "odd_w_out": (256, 1024),
          "mlp_w1": (2048, 1024), "mlp_w2": (2048, 1024)}
SMALL_SHARDED = ("even_conv_w", "lru_b_a", "lru_b_x", "lru_lambda", "hgrn_norm_w")


def _pcall(body, **kw):
    return pl.pallas_call(body, **kw)


def _params(**kw):
    return pltpu.CompilerParams(vmem_limit_bytes=VMEM_LIMIT, **kw)


def _tile(n, pref):
    if n <= pref:
        return n
    t = (pref // 128) * 128
    while n % t:
        t -= 128
    return t


def _dot(a, b, dims=(((1,), (0,)), ((), ())), precision=None):
    return lax.dot_general(a, b, dims, preferred_element_type=F32, precision=precision)


_NN = (((1,), (0,)), ((), ()))
_NT = (((1,), (1,)), ((), ()))
_TN = (((0,), (0,)), ((), ()))


def _mx(v):
    return v.astype(MXU_DTYPE)


def _mm(name, a, b, mode, *, out_dtype=F32, res=None, relu2=False, relu2_of=None, col_shards=1):
    if mode == "nn":
        (m, kk), (_, n) = a.shape, b.shape
    elif mode == "nt":
        (m, kk), (n, _) = a.shape, b.shape
    else:
        (kk, m), (_, n) = a.shape, b.shape
    assert res is None or relu2_of is None
    tm, tn, tk = _tile(m, 1024), _tile(n // col_shards, 1024), _tile(kk, 1024 if mode != "tn" else 512)
    nk = kk // tk
    dims = {"nn": _NN, "nt": _NT, "tn": _TN}[mode]
    a_spec = pl.BlockSpec((tk, tm), lambda i, j, k: (k, i)) if mode == "tn" else pl.BlockSpec((tm, tk), lambda i, j, k: (i, k))
    b_spec = pl.BlockSpec((tn, tk), lambda i, j, k: (j, k)) if mode == "nt" else pl.BlockSpec((tk, tn), lambda i, j, k: (k, j))
    o_spec = pl.BlockSpec((tm, tn), lambda i, j, k: (i, j))
    o_shape = (m, n)
    if col_shards > 1:
        assert tn * col_shards == n and res is None and not relu2
        o_spec = pl.BlockSpec((None, tm, tn), lambda i, j, k: (j, i, 0))
        o_shape = (col_shards, m, tn)
    extra = res if res is not None else relu2_of
    has_res = extra is not None

    def body(*refs):
        a_ref, b_ref = refs[0], refs[1]
        res_ref = refs[2] if has_res else None
        outs = refs[2 + has_res:-1]
        acc = refs[-1]
        k = pl.program_id(2)

        @pl.when(k == 0)
        def _():
            acc[...] = jnp.zeros_like(acc)

        acc[...] += _dot(_mx(a_ref[...]), _mx(b_ref[...]), dims)

        @pl.when(k == nk - 1)
        def _():
            r = acc[...]
            if res is not None:
                r = r + res_ref[...]
            if relu2_of is not None:
                r = r * (2.0 * jnp.maximum(res_ref[...], 0.0))
            if relu2:
                outs[0][...] = r
                outs[1][...] = jnp.square(jnp.maximum(r, 0.0)).astype(outs[1].dtype)
            else:
                outs[0][...] = r.astype(outs[0].dtype)

    in_specs = [a_spec, b_spec] + ([o_spec] if has_res else [])
    if relu2:
        out_shape = (jax.ShapeDtypeStruct((m, n), F32), jax.ShapeDtypeStruct((m, n), BF16))
        out_specs = (o_spec, o_spec)
    else:
        out_shape = jax.ShapeDtypeStruct(o_shape, out_dtype)
        out_specs = o_spec
    args = (a, b) + ((extra,) if has_res else ())
    return _pcall(body, name=name, grid=(m // tm, n // tn, nk), in_specs=in_specs, out_specs=out_specs,
                  out_shape=out_shape, scratch_shapes=[pltpu.VMEM((tm, tn), F32)], compiler_params=_params())(*args)


def _pw_fwd(name, f, ins, params, out_dtypes, tc, ncol, tm=256):
    t = ins[0][0].shape[0]
    tm = min(tm, t)
    ni, npar = len(ins), len(params)

    def body(*refs):
        vals = f(*[r[...].astype(F32) for r in refs[:ni]], *[r[...] for r in refs[ni:ni + npar]])
        for o, v in zip(refs[ni + npar:], vals):
            o[...] = v.astype(o.dtype)

    in_specs = [pl.BlockSpec((tm, tc), lambda j, i, off=off: (i, off + j)) for _, off in ins]
    in_specs += [pl.BlockSpec((1, tc), lambda j, i, off=off: (0, off + j)) for _, off in params]
    out_specs = tuple(pl.BlockSpec((tm, tc), lambda j, i: (i, j)) for _ in out_dtypes)
    out_shape = tuple(jax.ShapeDtypeStruct((t, ncol * tc), d) for d in out_dtypes)
    return _pcall(body, name=name, grid=(ncol, t // tm), in_specs=in_specs, out_specs=out_specs, out_shape=out_shape,
                  compiler_params=_params())(*[a for a, _ in ins], *[p for p, _ in params])


def _pw_bwd(name, f, ins, params, douts, tc, ncol, want, adds=None, tm=256, out_dtypes=None):
    adds = adds or {}
    out_dtypes = out_dtypes or [F32] * len(want)
    t = ins[0][0].shape[0]
    tm = min(tm, t)
    ni, npar, nd, na = len(ins), len(params), len(douts), len(adds)
    add_keys = sorted(adds)

    def body(*refs):
        in_refs, p_refs = refs[:ni], refs[ni:ni + npar]
        d_refs = refs[ni + npar:ni + npar + nd]
        a_refs = refs[ni + npar + nd:ni + npar + nd + na]
        o_refs = refs[ni + npar + nd + na:]
        _, vjp = jax.vjp(f, *[r[...].astype(F32) for r in in_refs], *[r[...] for r in p_refs])
        cts = vjp(tuple(d[...].astype(F32) for d in d_refs))
        for o, kidx in zip(o_refs[:len(want)], want):
            v = cts[kidx]
            if kidx in adds:
                v = v + a_refs[add_keys.index(kidx)][...]
            o[...] = v.astype(o.dtype)
        for p in range(npar):
            o = o_refs[len(want) + p]

            @pl.when(pl.program_id(1) == 0)
            def _(o=o):
                o[...] = jnp.zeros_like(o)

            o[...] += cts[ni + p]

    in_specs = [pl.BlockSpec((tm, tc), lambda j, i, off=off: (i, off + j)) for _, off in ins]
    in_specs += [pl.BlockSpec((1, tc), lambda j, i, off=off: (0, off + j)) for _, off in params]
    in_specs += [pl.BlockSpec((tm, tc), lambda j, i: (i, j)) for _ in range(nd + na)]
    out_specs = tuple([pl.BlockSpec((tm, tc), lambda j, i: (i, j)) for _ in want]
                      + [pl.BlockSpec((1, tc), lambda j, i: (0, j)) for _ in params])
    out_shape = tuple([jax.ShapeDtypeStruct((t, ncol * tc), dt) for dt in out_dtypes]
                      + [jax.ShapeDtypeStruct((1, ncol * tc), F32) for _ in params])
    res = _pcall(body, name=name, grid=(ncol, t // tm), in_specs=in_specs, out_specs=out_specs, out_shape=out_shape,
                 compiler_params=_params())(*[a for a, _ in ins], *[p for p, _ in params], *douts, *[adds[k] for k in add_keys])
    return list(res[:len(want)]), list(res[len(want):])


def _rms(x, g):
    return (x * lax.rsqrt(jnp.mean(x * x, axis=-1, keepdims=True) + EPS)) * g


def _f_norm(x, g):
    return (_rms(x, g),)


def _f_silu(c):
    return (jax.nn.silu(c),)


def _f_softplus(d, b):
    return (jax.nn.softplus(d + b),)


def _f_add2(a, b):
    return (a + b,)


def _f_add3(a, b, c):
    return (a + b + c,)


def _f_ssd_post(yf, yb, xs, z, dskip, nw):
    u = (yf + yb + dskip * xs) * jax.nn.silu(z)
    return (_rms(u, nw),)


def _neg_expm1(v):
    t = jnp.tanh(0.5 * v)
    return -2.0 * t / (1.0 - t)


def _f_lru_gates(pre_a, pre_x, u, ba, bx, lam):
    rg = jax.nn.sigmoid(pre_a + ba)
    ig = jax.nn.sigmoid(pre_x + bx)
    log_a = -LRU_C * rg * jax.nn.softplus(-lam)
    return jnp.exp(log_a), jnp.sqrt(_neg_expm1(2.0 * log_a)) * (ig * u)


def _f_lru_post(hf, hb, gate):
    return ((hf + hb) * jax.nn.gelu(gate),)


def _f_hgrn_pre(fr, l0, l1):
    lb = jax.nn.sigmoid(l1 - l0)
    k = (1.0 - lb) * jax.nn.sigmoid(-fr)
    return k, jnp.log1p(-k)


def _f_hgrn_post(of, ob, gate, nw):
    return (_rms(of + ob, nw) * jax.nn.silu(gate),)


def _loss_head(x, tgt, g, tm=256):
    t, d = x.shape
    tm = min(tm, t)

    def body(x_ref, t_ref, g_ref, dx_ref, dg_ref, loss_ref):
        tv = t_ref[...]

        def lf(xv, gv):
            return 0.5 * jnp.sum(jnp.mean(jnp.square(_rms(xv, gv) - tv), axis=-1))

        val, vjp = jax.vjp(lf, x_ref[...], g_ref[...])
        dx, dg = vjp(jnp.ones((), F32))
        dx_ref[...] = dx

        @pl.when(pl.program_id(0) == 0)
        def _():
            dg_ref[...] = jnp.zeros_like(dg_ref)
            loss_ref[...] = jnp.zeros_like(loss_ref)

        dg_ref[...] += dg
        loss_ref[...] += jnp.full(loss_ref.shape, val, F32)

    row = pl.BlockSpec((tm, d), lambda i: (i, 0))
    vec = pl.BlockSpec((1, d), lambda i: (0, 0))
    return _pcall(body, name="loss_head", grid=(t // tm,), in_specs=[row, row, vec],
                  out_specs=(row, vec, pl.BlockSpec((1, 128), lambda i: (0, 0))),
                  out_shape=(jax.ShapeDtypeStruct((t, d), F32), jax.ShapeDtypeStruct((1, d), F32),
                             jax.ShapeDtypeStruct((1, 128), F32)), compiler_params=_params())(x, tgt, g)


def _shifted(x, d, prev, nxt, first, last):
    r = x.shape[0]
    row = lax.broadcasted_iota(jnp.int32, x.shape, 0)
    if d < 0:
        out = pltpu.roll(x, -d, 0)
        for q in range(-d):
            pv = jnp.where(first, 0.0, prev[8 + d + q:8 + d + q + 1, :])
            out = jnp.where(row == q, pv, out)
        return out
    out = pltpu.roll(x, r - d, 0)
    for q in range(d):
        nv = jnp.where(last, 0.0, nxt[q:q + 1, :])
        out = jnp.where(row == r - d + q, nv, out)
    return out


def _halo_specs(ts, tc, s):
    nb8 = s // 8
    cur = pl.BlockSpec((None, ts, tc), lambda n, i, j: (n, i, j))
    prev = pl.BlockSpec((None, 8, tc), lambda n, i, j: (n, jnp.maximum(i * (ts // 8) - 1, 0), j))
    nxt = pl.BlockSpec((None, 8, tc), lambda n, i, j: (n, jnp.minimum((i + 1) * (ts // 8), nb8 - 1), j))
    return cur, prev, nxt


def _conv_fwd(p3, w, b, ncol, tc=1024):
    nbatch, s, _ = p3.shape
    ts = min(CONV_ROWS, s)
    nblk = s // ts

    def body(x_ref, pv_ref, nx_ref, w_ref, b_ref, o_ref):
        i = pl.program_id(1)
        first, last = i == 0, i == nblk - 1
        x, pv, nx = x_ref[...], pv_ref[...], nx_ref[...]
        wv = w_ref[...]
        out = b_ref[...] + wv[1:2] * x
        out = out + wv[0:1] * _shifted(x, -1, pv, nx, first, last)
        out = out + wv[2:3] * _shifted(x, 1, pv, nx, first, last)
        out = out + wv[3:4] * _shifted(x, 2, pv, nx, first, last)
        o_ref[...] = out

    cur, prev, nxt = _halo_specs(ts, tc, s)
    return _pcall(body, name="conv_fwd", grid=(nbatch, nblk, ncol),
                  in_specs=[cur, prev, nxt, pl.BlockSpec((4, tc), lambda n, i, j: (0, j)),
                            pl.BlockSpec((1, tc), lambda n, i, j: (0, j))],
                  out_specs=cur, out_shape=jax.ShapeDtypeStruct((nbatch, s, ncol * tc), F32),
                  compiler_params=_params())(p3, p3, p3, w, b)


def _conv_bwd(dc3, p3, w, ncol, tc=1024):
    nbatch, s, _ = dc3.shape
    ts = min(CONV_ROWS, s)
    nblk = s // ts

    def body(d_ref, dpv_ref, dnx_ref, x_ref, pv_ref, nx_ref, w_ref, dx_ref, dw_ref):
        n, i = pl.program_id(1), pl.program_id(2)
        first, last = i == 0, i == nblk - 1
        d, dpv, dnx = d_ref[...], dpv_ref[...], dnx_ref[...]
        x, pv, nx = x_ref[...], pv_ref[...], nx_ref[...]
        wv = w_ref[...]
        dx = wv[1:2] * d
        dx = dx + wv[0:1] * _shifted(d, 1, dpv, dnx, first, last)
        dx = dx + wv[2:3] * _shifted(d, -1, dpv, dnx, first, last)
        dx = dx + wv[3:4] * _shifted(d, -2, dpv, dnx, first, last)
        dx_ref[...] = dx.astype(dx_ref.dtype)

        @pl.when((n == 0) & (i == 0))
        def _():
            dw_ref[...] = jnp.zeros_like(dw_ref)

        dw_ref[0:1, :] += jnp.sum(d * _shifted(x, -1, pv, nx, first, last), axis=0, keepdims=True)
        dw_ref[1:2, :] += jnp.sum(d * x, axis=0, keepdims=True)
        dw_ref[2:3, :] += jnp.sum(d * _shifted(x, 1, pv, nx, first, last), axis=0, keepdims=True)
        dw_ref[3:4, :] += jnp.sum(d * _shifted(x, 2, pv, nx, first, last), axis=0, keepdims=True)
        dw_ref[4:5, :] += jnp.sum(d, axis=0, keepdims=True)

    nb8 = s // 8
    cur = pl.BlockSpec((None, ts, tc), lambda j, n, i: (n, i, j))
    prev = pl.BlockSpec((None, 8, tc), lambda j, n, i: (n, jnp.maximum(i * (ts // 8) - 1, 0), j))
    nxt = pl.BlockSpec((None, 8, tc), lambda j, n, i: (n, jnp.minimum((i + 1) * (ts // 8), nb8 - 1), j))
    return _pcall(body, name="conv_bwd", grid=(ncol, nbatch, nblk),
                  in_specs=[cur, prev, nxt, cur, prev, nxt, pl.BlockSpec((4, tc), lambda j, n, i: (0, j))],
                  out_specs=(cur, pl.BlockSpec((8, tc), lambda j, n, i: (0, j))),
                  out_shape=(jax.ShapeDtypeStruct((nbatch, s, ncol * tc), BF16), jax.ShapeDtypeStruct((8, ncol * tc), F32)),
                  compiler_params=_params())(dc3, dc3, dc3, p3, p3, p3, w)


def _block_scan(coef, inp, reverse):
    r = coef.shape[0]
    row = lax.broadcasted_iota(jnp.int32, coef.shape, 0)
    a, b = coef, inp
    d = 1
    while d < r:
        if reverse:
            keep = row < r - d
            a_sh, b_sh = pltpu.roll(a, r - d, 0), pltpu.roll(b, r - d, 0)
        else:
            keep = row >= d
            a_sh, b_sh = pltpu.roll(a, d, 0), pltpu.roll(b, d, 0)
        b = b + a * jnp.where(keep, b_sh, 0.0)
        a = a * jnp.where(keep, a_sh, 1.0)
        d *= 2
    return a, b


def _lru_scan(a3, b3, reverse):
    nbatch, s, w = a3.shape
    ts = min(LRU_ROWS, s)
    nblk = s // ts
    edge = 0 if reverse else ts - 1

    def body(a_ref, b_ref, h_ref, carry):
        @pl.when(pl.program_id(1) == 0)
        def _():
            carry[...] = jnp.zeros_like(carry)

        ca, hb = _block_scan(a_ref[...], b_ref[...], reverse)
        h = hb + ca * carry[0:1, :]
        h_ref[...] = h
        carry[0:1, :] = h[edge:edge + 1, :]

    blk = pl.BlockSpec((None, ts, w), (lambda n, i: (n, nblk - 1 - i, 0)) if reverse else (lambda n, i: (n, i, 0)))
    return _pcall(body, name=f"lru_scan_r{int(reverse)}", grid=(nbatch, nblk), in_specs=[blk, blk], out_specs=blk,
                  out_shape=jax.ShapeDtypeStruct((nbatch, s, w), F32), scratch_shapes=[pltpu.VMEM((8, w), F32)],
                  compiler_params=_params())(a3, b3)


def _lru_scan_bwd(a3, h3, dh3, reverse):
    nbatch, s, w = a3.shape
    ts = min(LRU_ROWS, s)
    nblk = s // ts
    nb8 = s // 8
    tpb = ts // 8

    def body(a_ref, aa_ref, h_ref, hh_ref, dh_ref, g_ref, da_ref, carry):
        i = pl.program_id(1)

        @pl.when(i == 0)
        def _():
            carry[...] = jnp.zeros_like(carry)

        a, h = a_ref[...], h_ref[...]
        row = lax.broadcasted_iota(jnp.int32, a.shape, 0)
        if reverse:
            a_edge = jnp.where(i == 0, 0.0, aa_ref[7:8, :])
            c = jnp.where(row == 0, a_edge, pltpu.roll(a, 1, 0))
            h_edge = jnp.where(i == nblk - 1, 0.0, hh_ref[0:1, :])
            h_sh = jnp.where(row == ts - 1, h_edge, pltpu.roll(h, ts - 1, 0))
        else:
            a_edge = jnp.where(i == 0, 0.0, aa_ref[0:1, :])
            c = jnp.where(row == ts - 1, a_edge, pltpu.roll(a, ts - 1, 0))
            h_edge = jnp.where(i == nblk - 1, 0.0, hh_ref[7:8, :])
            h_sh = jnp.where(row == 0, h_edge, pltpu.roll(h, 1, 0))
        cc, gb = _block_scan(c, dh_ref[...], not reverse)
        g = gb + cc * carry[0:1, :]
        g_ref[...] = g
        carry[0:1, :] = g[ts - 1:ts, :] if reverse else g[0:1, :]
        da_ref[...] = g * h_sh

    if reverse:
        bi = lambda i: i
    else:
        bi = lambda i: nblk - 1 - i
    blk = pl.BlockSpec((None, ts, w), lambda n, i: (n, bi(i), 0))
    before = pl.BlockSpec((None, 8, w), lambda n, i: (n, jnp.maximum(bi(i) * tpb - 1, 0), 0))
    after = pl.BlockSpec((None, 8, w), lambda n, i: (n, jnp.minimum((bi(i) + 1) * tpb, nb8 - 1), 0))
    a_tile, h_tile = (before, after) if reverse else (after, before)
    return _pcall(body, name=f"lru_scan_bwd_r{int(reverse)}", grid=(nbatch, nblk), in_specs=[blk, a_tile, blk, h_tile, blk],
                  out_specs=(blk, blk),
                  out_shape=(jax.ShapeDtypeStruct((nbatch, s, w), F32), jax.ShapeDtypeStruct((nbatch, s, w), F32)),
                  scratch_shapes=[pltpu.VMEM((8, w), F32)], compiler_params=_params())(a3, a3, h3, h3, dh3)


def _head_expand(lane0):
    return (jnp.right_shift(lax.broadcasted_iota(jnp.int32, (128, 1024), 1), HEAD_SHIFT) + lane0
            == lax.broadcasted_iota(jnp.int32, (128, 1024), 0)).astype(F32)


def _head_reduce(lane0):
    return (jnp.right_shift(lax.broadcasted_iota(jnp.int32, (1024, 128), 0), HEAD_SHIFT) + lane0
            == lax.broadcasted_iota(jnp.int32, (1024, 128), 1)).astype(F32)


def _time_mask(q, reverse):
    ri = lax.broadcasted_iota(jnp.int32, (q, q), 0)
    ci = lax.broadcasted_iota(jnp.int32, (q, q), 1)
    return (ri <= ci) if reverse else (ri >= ci)


def _ssd_common(xs_ref, bc_ref, dt_ref, al_ref, reverse, lane0):
    q = xs_ref.shape[0]
    edge = 0 if reverse else q - 1
    dt = dt_ref[...]
    a = -jnp.exp(al_ref[...])
    mask = _time_mask(q, reverse)
    expand = _head_expand(lane0)
    cum = _dot(mask.astype(F32), dt * a, precision=HI)
    cum_x = _dot(cum, expand, precision=HI)
    dt_x = _dot(dt, expand, precision=HI)
    last_x = cum_x[edge:edge + 1, :]
    xs = xs_ref[...]
    bc = bc_ref[...]
    return dict(q=q, edge=edge, lane0=lane0, dt=dt, a=a, mask=mask, cum_t=cum.T, cum_x=cum_x, dt_x=dt_x, xs=xs,
                v=xs * dt_x, e_c=jnp.exp(cum_x), w=jnp.exp(last_x - cum_x), e_l=jnp.exp(last_x),
                bm=bc[:, :512], cm=bc[:, 512:])


def _ssd_decay(c, h):
    row = c["lane0"] + h
    seg = c["cum_x"][:, h * SSD_HEADDIM:h * SSD_HEADDIM + 1] - c["cum_t"][row:row + 1, :]
    return jnp.where(c["mask"], jnp.exp(jnp.minimum(seg, 0.0)), 0.0)


def _head_masks():
    lane = jnp.right_shift(lax.broadcasted_iota(jnp.int32, (1, 256), 1), HEAD_SHIFT)
    return [lane == e for e in range(4)]


def _ssd_fwd(xbc3, dt3, alog, reverse):
    nbatch, s, _ = xbc3.shape
    q = min(SSD_CHUNK, s)
    nc = s // q
    lane0 = SSD_HEADS * int(reverse)

    def body(xs_ref, bc_ref, dt_ref, al_ref, y_ref, st_ref, st):
        @pl.when(pl.program_id(1) == 0)
        def _():
            st[...] = jnp.zeros_like(st)

        st_ref[...] = st[...]
        c = _ssd_common(xs_ref, bc_ref, dt_ref, al_ref, reverse, lane0)
        hm = _head_masks()
        for g in range(SSD_GROUPS):
            sl = slice(g * 256, (g + 1) * 256)
            cg, bg = _mx(c["cm"][:, g * 128:(g + 1) * 128]), _mx(c["bm"][:, g * 128:(g + 1) * 128])
            cb = _dot(cg, bg, _NT)
            vg = c["v"][:, sl]
            s0 = st[:, sl]
            yg = _dot(cg, _mx(s0)) * c["e_c"][:, sl]
            for e in range(4):
                m = _ssd_decay(c, 4 * g + e) * cb
                yg = yg + _dot(_mx(m), _mx(jnp.where(hm[e], vg, 0.0)))
            y_ref[:, sl] = yg
            st[:, sl] = c["e_l"][:, sl] * s0 + _dot(bg, _mx(vg * c["w"][:, sl]), _TN)

    ck = (lambda i: nc - 1 - i) if reverse else (lambda i: i)
    xs_spec = pl.BlockSpec((None, q, 1024), lambda n, i: (n, ck(i), 0))
    bc_spec = pl.BlockSpec((None, q, 1024), lambda n, i: (n, ck(i), 1))
    dt_spec = pl.BlockSpec((None, q, 128), lambda n, i: (n, ck(i), 0))
    al_spec = pl.BlockSpec((1, 128), lambda n, i: (0, 0))
    st_spec = pl.BlockSpec((None, None, 128, 1024), lambda n, i: (n, ck(i), 0, 0))
    return _pcall(body, name=f"ssd_fwd_r{int(reverse)}", grid=(nbatch, nc), in_specs=[xs_spec, bc_spec, dt_spec, al_spec],
                  out_specs=(xs_spec, st_spec),
                  out_shape=(jax.ShapeDtypeStruct((nbatch, s, 1024), F32), jax.ShapeDtypeStruct((nbatch, nc, 128, 1024), F32)),
                  scratch_shapes=[pltpu.VMEM((128, 1024), F32)], compiler_params=_params())(xbc3, xbc3, dt3, alog)


def _ssd_bwd(xbc3, dt3, alog, st4, dy3, reverse):
    nbatch, s, _ = xbc3.shape
    q = min(SSD_CHUNK, s)
    nc = s // q
    lane0 = SSD_HEADS * int(reverse)

    def body(xs_ref, bc_ref, dt_ref, al_ref, st0_ref, dy_ref, dxs_ref, dbc_ref, ddt_ref, dal_ref, dst):
        n, i = pl.program_id(0), pl.program_id(1)

        @pl.when(i == 0)
        def _():
            dst[...] = jnp.zeros_like(dst)

        @pl.when((i == 0) & (n == 0))
        def _():
            dal_ref[...] = jnp.zeros_like(dal_ref)

        c = _ssd_common(xs_ref, bc_ref, dt_ref, al_ref, reverse, lane0)
        hm = _head_masks()
        reduce_m = _head_reduce(lane0)
        s0_all, ds1_all, dy = st0_ref[...], dst[...], dy_ref[...]
        lane = lax.broadcasted_iota(jnp.int32, (q, 128), 1)
        sub = lax.broadcasted_iota(jnp.int32, (128, q), 0)
        rowacc = jnp.zeros((q, 128), F32)
        colacc_t = jnp.zeros((128, q), F32)
        dv_l, yst_l, dvbar_l, dk_l, dc_l = [], [], [], [], []
        for g in range(SSD_GROUPS):
            sl = slice(g * 256, (g + 1) * 256)
            cg, bg = _mx(c["cm"][:, g * 128:(g + 1) * 128]), _mx(c["bm"][:, g * 128:(g + 1) * 128])
            cb = _dot(cg, bg, _NT)
            vg, dyg, wg, ecg = c["v"][:, sl], dy[:, sl], c["w"][:, sl], c["e_c"][:, sl]
            s0, ds1 = _mx(s0_all[:, sl]), _mx(ds1_all[:, sl])
            dye = _mx(dyg * ecg)
            yst_l.append(_dot(cg, s0) * ecg)
            dcg = _dot(dye, s0, _NT)
            dst[:, sl] = c["e_l"][:, sl] * ds1_all[:, sl] + _dot(cg, dye, _TN)
            vbar = _mx(vg * wg)
            dvbar = _dot(bg, ds1)
            dvbar_l.append(dvbar)
            dvg = dvbar * wg
            dkg = _dot(vbar, ds1, _NT)
            for e in range(4):
                h = 4 * g + e
                m = _ssd_decay(c, h)
                dyh, vh = _mx(jnp.where(hm[e], dyg, 0.0)), _mx(jnp.where(hm[e], vg, 0.0))
                dvg = dvg + _dot(_mx(m * cb), dyh, _TN)
                dcb = _dot(dyh, vh, _NT) * m
                dcbb = _mx(dcb)
                dcg = dcg + _dot(dcbb, bg)
                dkg = dkg + _dot(dcbb, cg, _TN)
                wmat = dcb * cb
                rowacc = jnp.where(lane == lane0 + h, jnp.sum(wmat, axis=1, keepdims=True), rowacc)
                colacc_t = jnp.where(sub == lane0 + h, jnp.sum(wmat, axis=0, keepdims=True), colacc_t)
            dv_l.append(dvg)
            dk_l.append(dkg)
            dc_l.append(dcg)
        dv = jnp.concatenate(dv_l, axis=1)
        yst = jnp.concatenate(yst_l, axis=1)
        dvbar = jnp.concatenate(dvbar_l, axis=1)
        t1 = _dot(dy * yst, reduce_m, precision=HI)
        t2 = _dot(c["v"] * c["w"] * dvbar, reduce_m, precision=HI)
        dlast = jnp.sum(t2, axis=0, keepdims=True) + _dot(
            c["e_l"] * jnp.sum(ds1_all * s0_all, axis=0, keepdims=True), reduce_m, precision=HI)
        dcum = rowacc - colacc_t.T + t1 - t2
        dcum = dcum + jnp.where(lax.broadcasted_iota(jnp.int32, (q, 128), 0) == c["edge"], dlast, 0.0)
        dda = _dot(c["mask"].astype(F32), dcum, _TN, precision=HI)
        ddt_ref[...] = dda * c["a"] + _dot(dv * c["xs"], reduce_m, precision=HI)
        dal_ref[...] += jnp.sum(dda * c["dt"], axis=0, keepdims=True) * c["a"]
        dxs_ref[...] = dv * c["dt_x"]
        dbc_ref[...] = jnp.concatenate(dk_l + dc_l, axis=1)

    ck = (lambda i: i) if reverse else (lambda i: nc - 1 - i)
    xs_spec = pl.BlockSpec((None, q, 1024), lambda n, i: (n, ck(i), 0))
    bc_spec = pl.BlockSpec((None, q, 1024), lambda n, i: (n, ck(i), 1))
    dt_spec = pl.BlockSpec((None, q, 128), lambda n, i: (n, ck(i), 0))
    al_spec = pl.BlockSpec((1, 128), lambda n, i: (0, 0))
    st_spec = pl.BlockSpec((None, None, 128, 1024), lambda n, i: (n, ck(i), 0, 0))
    return _pcall(body, name=f"ssd_bwd_r{int(reverse)}", grid=(nbatch, nc),
                  in_specs=[xs_spec, bc_spec, dt_spec, al_spec, st_spec, xs_spec],
                  out_specs=(xs_spec, xs_spec, dt_spec, al_spec),
                  out_shape=(jax.ShapeDtypeStruct((nbatch, s, 1024), F32), jax.ShapeDtypeStruct((nbatch, s, 1024), F32),
                             jax.ShapeDtypeStruct((nbatch, s, 128), F32), jax.ShapeDtypeStruct((1, 128), F32)),
                  scratch_shapes=[pltpu.VMEM((128, 1024), F32)], compiler_params=_params())(xbc3, xbc3, dt3, alog, st4, dy3)


def _gla_sub(q_ref, k_ref, g_ref, rs, reverse):
    sq = HGRN_SUB
    edge = 0 if reverse else sq - 1
    mask = _time_mask(sq, reverse)
    bc = _dot(mask.astype(F32), g_ref[rs, :], precision=HI)
    last = bc[edge:edge + 1, :]
    eb, enb, elb = jnp.exp(bc), jnp.exp(-bc), jnp.exp(last - bc)
    kv = k_ref[rs, :]
    return dict(mask=mask, edge=edge, eb=eb, enb=enb, elb=elb, e_l=jnp.exp(last),
                qt=q_ref[rs, :] * HGRN_SCALE * eb, kt=kv * enb, kb=kv * elb)


def _gla_specs(s, w, reverse_order):
    bq = min(HGRN_BLOCK, s)
    nblk = s // bq
    bi = (lambda i: nblk - 1 - i) if reverse_order else (lambda i: i)
    col = lambda cb: pl.BlockSpec((None, bq, w), lambda n, i: (n, bi(i), cb))
    st_spec = pl.BlockSpec((None, bq // HGRN_SUB, 128, w), lambda n, i: (n, bi(i), 0, 0))
    return bq, nblk, col, st_spec


def _gla_fwd(proj3, k3, g3, reverse):
    nbatch, s, w = k3.shape
    bq, nblk, col, st_spec = _gla_specs(s, w, reverse)
    nsub = bq // HGRN_SUB

    def body(q_ref, k_ref, v_ref, g_ref, o_ref, st_ref, st):
        @pl.when(pl.program_id(1) == 0)
        def _():
            st[...] = jnp.zeros_like(st)

        for j in (reversed(range(nsub)) if reverse else range(nsub)):
            rs = slice(j * HGRN_SUB, (j + 1) * HGRN_SUB)
            st_ref[j] = st[...]
            c = _gla_sub(q_ref, k_ref, g_ref, rs, reverse)
            v = v_ref[rs, :]
            for h in range(HGRN_HEADS):
                hs = slice(h * 128, (h + 1) * 128)
                qt, vb = _mx(c["qt"][:, hs]), _mx(v[:, hs])
                att = jnp.where(c["mask"], _dot(qt, _mx(c["kt"][:, hs]), _NT), 0.0)
                s0 = st[:, hs]
                o_ref[rs, hs] = _dot(_mx(att), vb) + _dot(qt, _mx(s0), _NT)
                st[:, hs] = s0 * c["e_l"][:, hs] + _dot(vb, _mx(c["kb"][:, hs]), _TN)

    return _pcall(body, name=f"gla_fwd_r{int(reverse)}", grid=(nbatch, nblk), in_specs=[col(0), col(0), col(3), col(0)],
                  out_specs=(col(0), st_spec),
                  out_shape=(jax.ShapeDtypeStruct((nbatch, s, w), F32),
                             jax.ShapeDtypeStruct((nbatch, s // HGRN_SUB, 128, w), F32)),
                  scratch_shapes=[pltpu.VMEM((128, w), F32)], compiler_params=_params())(proj3, k3, proj3, g3)


def _gla_bwd(proj3, k3, g3, st4, do3, reverse):
    nbatch, s, w = k3.shape
    bq, nblk, col, st_spec = _gla_specs(s, w, not reverse)
    nsub = bq // HGRN_SUB
    sq = HGRN_SUB

    def body(q_ref, k_ref, v_ref, g_ref, st_ref, do_ref, dq_ref, dk_ref, dv_ref, dg_ref, dst):
        @pl.when(pl.program_id(1) == 0)
        def _():
            dst[...] = jnp.zeros_like(dst)

        row = lax.broadcasted_iota(jnp.int32, (sq, 128), 0)
        for j in (range(nsub) if reverse else reversed(range(nsub))):
            rs = slice(j * sq, (j + 1) * sq)
            c = _gla_sub(q_ref, k_ref, g_ref, rs, reverse)
            s0_all, ds1_all = st_ref[j], dst[...]
            v, dy = v_ref[rs, :], do_ref[rs, :]
            db_l = []
            for h in range(HGRN_HEADS):
                hs = slice(h * 128, (h + 1) * 128)
                qt, kt, kb = c["qt"][:, hs], c["kt"][:, hs], c["kb"][:, hs]
                qtb, ktb, kbb, vb, dyb = _mx(qt), _mx(kt), _mx(kb), _mx(v[:, hs]), _mx(dy[:, hs])
                s0, ds1 = s0_all[:, hs], ds1_all[:, hs]
                att = jnp.where(c["mask"], _dot(qtb, ktb, _NT), 0.0)
                datt = _mx(jnp.where(c["mask"], _dot(dyb, vb, _NT), 0.0))
                dqt = _dot(datt, ktb) + _dot(dyb, _mx(s0))
                dkt = _dot(datt, qtb, _TN)
                dkb = _dot(vb, _mx(ds1))
                dv_ref[rs, hs] = _dot(_mx(att), dyb, _TN) + _dot(kbb, _mx(ds1), _NT)
                dst[:, hs] = c["e_l"][:, hs] * ds1 + _dot(dyb, qtb, _TN)
                dq_ref[rs, hs] = dqt * c["eb"][:, hs] * HGRN_SCALE
                dk_ref[rs, hs] = dkt * c["enb"][:, hs] + dkb * c["elb"][:, hs]
                kbk = dkb * kb
                dlast = jnp.sum(kbk, axis=0, keepdims=True) + c["e_l"][:, hs] * jnp.sum(ds1 * s0, axis=0, keepdims=True)
                db_l.append(dqt * qt - dkt * kt - kbk + jnp.where(row == c["edge"], dlast, 0.0))
            dg_ref[rs, :] = _dot(c["mask"].astype(F32), jnp.concatenate(db_l, axis=1), _TN, precision=HI)

    shp = jax.ShapeDtypeStruct((nbatch, s, w), F32)
    return _pcall(body, name=f"gla_bwd_r{int(reverse)}", grid=(nbatch, nblk),
                  in_specs=[col(0), col(0), col(3), col(0), st_spec, col(0)],
                  out_specs=(col(0),) * 4, out_shape=(shp,) * 4,
                  scratch_shapes=[pltpu.VMEM((128, w), F32)], compiler_params=_params())(proj3, k3, proj3, g3, st4, do3)


DIRS = (False, True)


def _block_diag(w):
    eye = jnp.eye(16, dtype=w.dtype)
    return (eye[:, None, :, None] * w[:, :, None, :]).reshape(1024, 1024)


def _diag_blocks(m):
    m4 = m.reshape(16, 64, 16, 64)
    return jnp.stack([m4[i, :, i, :] for i in range(16)], axis=0)


def _pad_lanes(v, n=128):
    return jnp.pad(v, [(0, 0)] * (v.ndim - 1) + [(0, n - v.shape[-1])])


def _mlp_fwd(tag, x, nw, w1, w2):
    (h,) = _pw_fwd(f"{tag}_norm", _f_norm, [(x, 0)], [(nw, 0)], [BF16], 1024, 1)
    a, r = _mm(f"{tag}_up", h, w1, "nn", relu2=True)
    return _mm(f"{tag}_down", r, w2, "nn", res=x), (h, a, r)


def _mlp_bwd(tag, x, nw, w1, w2, saved, dxo):
    h, a, r = saved
    dw2 = _mm(f"{tag}_dw2", r, dxo, "tn")
    da = _mm(f"{tag}_da", dxo, w2, "nt", relu2_of=a, out_dtype=BF16)
    dw1 = _mm(f"{tag}_dw1", h, da, "tn", col_shards=4)
    dh = _mm(f"{tag}_dh", da, w1, "nt")
    (dx,), (dnw,) = _pw_bwd(f"{tag}_dnorm", _f_norm, [(x, 0)], [(nw, 0)], [dh], 1024, 1, [0], adds={0: dxo})
    return dx, dw1, dw2, dnw


def _split_in0(pieces, dt_piece):
    tm = 256

    def body(p0, p1, p2, p3, p4, p5, o_ref):
        full = jnp.concatenate([p0[...], p1[...], p2[...], p3[...], p4[...], p5[:, :32]], axis=1)
        for j in range(4):
            o_ref[j] = full[:, 1288 * j:1288 * (j + 1)]

    blk = pl.BlockSpec((tm, 1024), lambda i: (i, 0))
    return _pcall(body, name="split_in0", grid=(1024 // tm,), in_specs=[blk] * 5 + [pl.BlockSpec((tm, 128), lambda i: (i, 0))],
                  out_specs=pl.BlockSpec((4, tm, 1288), lambda i: (0, i, 0)),
                  out_shape=jax.ShapeDtypeStruct((4, 1024, 1288), F32), compiler_params=_params())(*pieces, dt_piece)


def _assemble_in0(shards):
    tm = 256

    def body(s_ref, m_ref, d_ref):
        full = jnp.concatenate([s_ref[j] for j in range(4)], axis=1)
        m_ref[...] = full[:, :5120]
        d_ref[...] = jnp.concatenate([full[:, 5120:5152], jnp.zeros((tm, 96), full.dtype)], axis=1)

    return _pcall(body, name="assemble_in0", grid=(1024 // tm,), in_specs=[pl.BlockSpec((4, tm, 1288), lambda i: (0, i, 0))],
                  out_specs=(pl.BlockSpec((tm, 5120), lambda i: (i, 0)), pl.BlockSpec((tm, 128), lambda i: (i, 0))),
                  out_shape=(jax.ShapeDtypeStruct((1024, 5120), shards.dtype), jax.ShapeDtypeStruct((1024, 128), shards.dtype)),
                  compiler_params=_params())(shards)


def _local_step(x3, tgt3, w, w_main0, w_dt0):
    nb, s, d = x3.shape
    t = nb * s
    x0 = x3.reshape(t, d)
    tgt = tgt3.reshape(t, d)
    grads = {}
    row = lambda v: v.reshape(1, -1)
    to3 = lambda v: v.reshape(nb, s, v.shape[-1])
    to2 = lambda v: v.reshape(-1, v.shape[-1])

    conv_w, conv_b = w["even_conv_w"][0], row(w["even_conv_b"][0])
    nmix0 = row(w["norm_mix"][0])
    (h0,) = _pw_fwd("l0_norm", _f_norm, [(x0, 0)], [(nmix0, 0)], [BF16], 1024, 1)
    proj0 = _mm("l0_proj", h0, w_main0, "nn")
    dt_raw = _mm("l0_proj_dt", h0, w_dt0, "nn")
    conv = to2(_conv_fwd(to3(proj0), conv_w, conv_b, 3))
    (xbc,) = _pw_fwd("l0_silu", _f_silu, [(conv, 0)], [], [F32], 1024, 2)
    dt_bias = _pad_lanes(w["ssd_dt_bias"][0].reshape(1, 32))
    (dt,) = _pw_fwd("l0_dt", _f_softplus, [(dt_raw, 0)], [(dt_bias, 0)], [F32], 128, 1)
    dt3, xbc3 = to3(dt), to3(xbc)
    alog = _pad_lanes(w["ssd_a_log"][0].reshape(1, 32))
    ssd = [_ssd_fwd(xbc3, dt3, alog, r) for r in DIRS]
    yf, yb = to2(ssd[0][0]), to2(ssd[1][0])
    dskip = jnp.repeat(w["ssd_d"][0], SSD_HEADDIM).reshape(1, 1024)
    snw = row(w["ssd_norm_w"][0])
    ssd_ins = [(yf, 0), (yb, 0), (xbc, 0), (proj0, 12)]
    (ya,) = _pw_fwd("l0_ssd_post", _f_ssd_post, ssd_ins, [(dskip, 0), (snw, 0)], [BF16], 256, 4)
    u_lru = conv[:, 2048:]
    w_gates = [_block_diag(w[k][0, r]).astype(MXU_DTYPE) for r in range(2) for k in ("lru_w_a", "lru_w_x")]
    pre = [_mm(f"l0_lru_pre{i}", u_lru, wg, "nn") for i, wg in enumerate(w_gates)]
    lru_par = [[(row(w[k][0, r]), 0) for k in ("lru_b_a", "lru_b_x", "lru_lambda")] for r in range(2)]
    lru_ins = [[(pre[2 * r], 0), (pre[2 * r + 1], 0), (u_lru, 0)] for r in range(2)]
    ab = [_pw_fwd(f"l0_lru_gates{r}", _f_lru_gates, lru_ins[r], lru_par[r], [F32, F32], 1024, 1) for r in range(2)]
    hs = [_lru_scan(to3(ab[r][0]), to3(ab[r][1]), DIRS[r]) for r in range(2)]
    lru_post_ins = [(to2(hs[0]), 0), (to2(hs[1]), 0), (proj0, 4)]
    (ybm,) = _pw_fwd("l0_lru_post", _f_lru_post, lru_post_ins, [], [BF16], 1024, 1)
    w_out0 = w["even_w_out"][0]
    x1 = _mm("l0_out_a", ya, w_out0[:1024], "nn", res=x0)
    x1 = _mm("l0_out_b", ybm, w_out0[1024:], "nn", res=x1)
    nmlp0 = row(w["norm_mlp"][0])
    x2, mlp0 = _mlp_fwd("l0_mlp", x1, nmlp0, w["mlp_w1"][0], w["mlp_w2"][0])

    w_in1 = w["odd_w_in"][0]
    nmix1 = row(w["norm_mix"][1])
    (h1,) = _pw_fwd("l1_norm", _f_norm, [(x2, 0)], [(nmix1, 0)], [BF16], 1024, 1)
    proj1 = _mm("l1_proj", h1, w_in1, "nn")
    proj1_3 = to3(proj1)
    lb0, lb1 = row(w["hgrn_lb_logits"][0]), row(w["hgrn_lb_logits"][1])
    kg = [_pw_fwd(f"l1_hgrn_pre{r}", _f_hgrn_pre, [(proj1, 1 + r)], [(lb0, 0), (lb1, 0)], [F32, F32], 1024, 1)
          for r in range(2)]
    gla = [_gla_fwd(proj1_3, to3(kg[r][0]), to3(kg[r][1]), DIRS[r]) for r in range(2)]
    hnw = row(w["hgrn_norm_w"][0])
    hpost_ins = [(to2(gla[0][0]), 0), (to2(gla[1][0]), 0), (proj1, 32)]
    (yo,) = _pw_fwd("l1_hgrn_post", _f_hgrn_post, hpost_ins, [(hnw, 0)], [BF16], 128, 8)
    w_out1 = w["odd_w_out"][0]
    x3_ = _mm("l1_out", yo, w_out1, "nn", res=x2)
    nmlp1 = row(w["norm_mlp"][1])
    x4, mlp1 = _mlp_fwd("l1_mlp", x3_, nmlp1, w["mlp_w1"][1], w["mlp_w2"][1])

    dx4, dnf, loss = _loss_head(x4, tgt, row(w["norm_final"]))
    grads["norm_final"] = dnf.reshape(-1)

    dx3, dw1_1, dw2_1, dnmlp1 = _mlp_bwd("l1_mlp", x3_, nmlp1, w["mlp_w1"][1], w["mlp_w2"][1], mlp1, dx4)
    big = {"odd_w_out": _mm("l1_dwout", yo, dx3, "tn").reshape(4, 256, 1024)}
    dyo = _mm("l1_dyo", dx3, w_out1, "nt")
    (do, dgate1), (dhnw,) = _pw_bwd("l1_hgrn_post_b", _f_hgrn_post, hpost_ins, [(hnw, 0)], [dyo], 128, 8, [0, 2],
                                    out_dtypes=[F32, BF16])
    grads["hgrn_norm_w"] = dhnw
    do3 = to3(do)
    gb = [_gla_bwd(proj1_3, to3(kg[r][0]), to3(kg[r][1]), gla[r][1], do3, DIRS[r]) for r in range(2)]
    (dq,) = _pw_fwd("l1_dq", _f_add2, [(to2(gb[0][0]), 0), (to2(gb[1][0]), 0)], [], [BF16], 1024, 1)
    (dvv,) = _pw_fwd("l1_dv", _f_add2, [(to2(gb[0][2]), 0), (to2(gb[1][2]), 0)], [], [BF16], 1024, 1)
    dfr, dl0, dl1 = [], [], []
    for r in range(2):
        (df,), (a0, a1) = _pw_bwd(f"l1_hgrn_pre_b{r}", _f_hgrn_pre, [(proj1, 1 + r)], [(lb0, 0), (lb1, 0)],
                                  [to2(gb[r][1]), to2(gb[r][3])], 1024, 1, [0], out_dtypes=[BF16])
        dfr.append(df)
        dl0.append(a0)
        dl1.append(a1)
    grads["hgrn_lb_logits"] = jnp.concatenate([dl0[0] + dl0[1], dl1[0] + dl1[1]], axis=0)
    dparts1 = [dq, dfr[0], dfr[1], dvv, dgate1]
    dwin1 = jnp.concatenate([_mm(f"l1_dwin{i}", h1, dp, "tn") for i, dp in enumerate(dparts1)], axis=1)
    big["odd_w_in"] = dwin1.reshape(1024, 4, 1280).transpose(1, 0, 2)
    dh1 = None
    for i, dp in enumerate(dparts1):
        dh1 = _mm(f"l1_dh{i}", dp, w_in1[:, i * 1024:(i + 1) * 1024], "nt", res=dh1)
    (dx2,), (dnmix1,) = _pw_bwd("l1_dnorm", _f_norm, [(x2, 0)], [(nmix1, 0)], [dh1], 1024, 1, [0], adds={0: dx3})

    dx1, dw1_0, dw2_0, dnmlp0 = _mlp_bwd("l0_mlp", x1, nmlp0, w["mlp_w1"][0], w["mlp_w2"][0], mlp0, dx2)
    big["mlp_w1"] = jnp.concatenate([dw1_0, dw1_1], axis=1)
    big["mlp_w2"] = jnp.concatenate([dw2_0.reshape(4, 1024, 1024), dw2_1.reshape(4, 1024, 1024)], axis=1)
    grads["norm_mlp"] = jnp.concatenate([dnmlp0, dnmlp1], axis=0)
    big["even_w_out"] = jnp.concatenate([_mm("l0_dwout_a", ya, dx1, "tn"), _mm("l0_dwout_b", ybm, dx1, "tn")],
                                        axis=0).reshape(4, 512, 1024)
    dya = _mm("l0_dya", dx1, w_out0[:1024], "nt")
    dyb = _mm("l0_dyb", dx1, w_out0[1024:], "nt")
    (dh, dgate0), _ = _pw_bwd("l0_lru_post_b", _f_lru_post, lru_post_ins, [], [dyb], 1024, 1, [0, 2], out_dtypes=[F32, BF16])
    dh3 = to3(dh)
    dpre, du_parts, dlru = [], [], {k: [] for k in ("lru_b_a", "lru_b_x", "lru_lambda")}
    for r in range(2):
        g_r, da_r = _lru_scan_bwd(to3(ab[r][0]), hs[r], dh3, DIRS[r])
        (dpa, dpx, du_r), (dba, dbx, dlam) = _pw_bwd(f"l0_lru_gates_b{r}", _f_lru_gates, lru_ins[r], lru_par[r],
                                                     [to2(da_r), to2(g_r)], 1024, 1, [0, 1, 2],
                                                     out_dtypes=[BF16, BF16, F32])
        dpre += [dpa, dpx]
        du_parts.append(du_r)
        dlru["lru_b_a"].append(dba)
        dlru["lru_b_x"].append(dbx)
        dlru["lru_lambda"].append(dlam)
    for k, v in dlru.items():
        grads[k] = jnp.concatenate(v, axis=0)[None]
    dwg = [_diag_blocks(_mm(f"l0_dwgate{i}", u_lru, dp, "tn")) for i, dp in enumerate(dpre)]
    grads["lru_w_a"] = jnp.stack([dwg[0], dwg[2]])[None]
    grads["lru_w_x"] = jnp.stack([dwg[1], dwg[3]])[None]
    (du,) = _pw_fwd("l0_du", _f_add2, [(du_parts[0], 0), (du_parts[1], 0)], [], [F32], 1024, 1)
    for i, dp in enumerate(dpre):
        du = _mm(f"l0_du_gate{i}", dp, w_gates[i], "nt", res=du)
    (dy, dxs_skip, dz), (ddskip, dsnw) = _pw_bwd("l0_ssd_post_b", _f_ssd_post, ssd_ins, [(dskip, 0), (snw, 0)], [dya],
                                                 256, 4, [0, 2, 3], out_dtypes=[F32, F32, BF16])
    grads["ssd_d"] = ddskip.reshape(SSD_HEADS, SSD_HEADDIM).sum(axis=1)[None]
    grads["ssd_norm_w"] = dsnw
    dy3 = to3(dy)
    sb = [_ssd_bwd(xbc3, dt3, alog, ssd[r][1], dy3, DIRS[r]) for r in range(2)]
    grads["ssd_a_log"] = (sb[0][3] + sb[1][3])[:, :32].reshape(1, 2, 16)
    (dxs,) = _pw_fwd("l0_dxs", _f_add3, [(to2(sb[0][0]), 0), (to2(sb[1][0]), 0), (dxs_skip, 0)], [], [F32], 1024, 1)
    (dbc,) = _pw_fwd("l0_dbc", _f_add2, [(to2(sb[0][1]), 0), (to2(sb[1][1]), 0)], [], [F32], 1024, 1)
    dxbc = jnp.concatenate([dxs, dbc], axis=1)
    (dconv_a,), _ = _pw_bwd("l0_silu_b", _f_silu, [(conv, 0)], [], [dxbc], 1024, 2, [0])
    (ddt,) = _pw_fwd("l0_ddt", _f_add2, [(to2(sb[0][2]), 0), (to2(sb[1][2]), 0)], [], [F32], 128, 1)
    (ddt_raw,), (ddtb,) = _pw_bwd("l0_dt_b", _f_softplus, [(dt_raw, 0)], [(dt_bias, 0)], [ddt], 128, 1, [0])
    grads["ssd_dt_bias"] = ddtb[:, :32].reshape(1, 2, 16)
    dconv = jnp.concatenate([dconv_a, du], axis=1)
    dproj_c, dcw = _conv_bwd(to3(dconv), to3(proj0), conv_w, 3)
    grads["even_conv_w"] = dcw[:4][None]
    grads["even_conv_b"] = dcw[4:5]
    dparts0 = [to2(dproj_c)[:, :1024], to2(dproj_c)[:, 1024:2048], to2(dproj_c)[:, 2048:], dz, dgate0]
    dwin0 = [_mm(f"l0_dwin{i}", h0, dp, "tn") for i, dp in enumerate(dparts0)]
    big["even_w_in"] = _split_in0(dwin0, _mm("l0_dwin_dt", h0, ddt_raw, "tn"))
    dh0 = _mm("l0_dh_dt", ddt_raw, w_dt0, "nt")
    for i, dp in enumerate(dparts0):
        dh0 = _mm(f"l0_dh{i}", dp, w_main0[:, i * 1024:(i + 1) * 1024], "nt", res=dh0)
    (dx0,), (dnmix0,) = _pw_bwd("l0_dnorm", _f_norm, [(x0, 0)], [(nmix0, 0)], [dh0], 1024, 1, [0], adds={0: dx1})
    grads["norm_mix"] = jnp.concatenate([dnmix0, dnmix1], axis=0)
    return loss, dx0.reshape(nb, s, d), grads, [big[n] for n in BIG]


ANY = pl.BlockSpec(memory_space=pl.ANY)


def _place():
    return lax.axis_index("x"), lax.axis_index("y"), lax.axis_index("c")


def _remote(src, dst, send_sems, recv_sems, k, to):
    return pltpu.make_async_remote_copy(src_ref=src, dst_ref=dst, send_sem=send_sems.at[k], recv_sem=recv_sems.at[k],
                                        device_id=to, device_id_type=MESH)


def _gather_chips(shards):
    n = len(shards)
    halves = [s.shape[0] // 2 for s in shards]

    def body(*refs):
        x_refs, out_refs = refs[:n], refs[n:2 * n]
        send_sems, recv_sems, local_sems = refs[2 * n:]
        x, y, c = _place()
        sibling = (x, y, 1 - c)
        chips = [(1 - x, y), (x, 1 - y), (1 - x, 1 - y)]

        def blk(t, px, py, hc):
            return out_refs[t].at[2 * px + py, pl.ds(hc * halves[t], halves[t]), :]

        def src(t):
            return x_refs[t].at[pl.ds(c * halves[t], halves[t]), :]

        mine = [pltpu.make_async_copy(x_refs[t], out_refs[t].at[2 * x + y], local_sems.at[t]) for t in range(n)]
        first = [_remote(src(t), blk(t, x, y, c), send_sems, recv_sems, 6 * t + j, (*chip, c))
                 for t in range(n) for j, chip in enumerate(chips)]
        for cp in mine + first:
            cp.start()
        passed = []
        for t in range(n):
            for j, chip in enumerate(chips):
                _remote(src(t), blk(t, *chip, c), send_sems, recv_sems, 6 * t + j, (*chip, c)).wait_recv()
                cp = _remote(blk(t, *chip, c), blk(t, *chip, c), send_sems, recv_sems, 6 * t + 3 + j, sibling)
                cp.start()
                passed.append(cp)
        for t in range(n):
            for j, chip in enumerate(chips):
                _remote(src(t), blk(t, *chip, 1 - c), send_sems, recv_sems, 6 * t + 3 + j, sibling).wait_recv()
        for cp in first + passed:
            cp.wait_send()
        for cp in mine:
            cp.wait()

    return _pcall(body, name="gather_weights", in_specs=[ANY] * n, out_specs=(ANY,) * n,
                  out_shape=tuple(jax.ShapeDtypeStruct((4,) + s.shape, s.dtype) for s in shards),
                  scratch_shapes=[pltpu.SemaphoreType.DMA((6 * n,)), pltpu.SemaphoreType.DMA((6 * n,)),
                                  pltpu.SemaphoreType.DMA((n,))],
                  compiler_params=_params())(*shards)


def _pair_swap(gps):
    n = len(gps)
    halves = [g.shape[1] // 2 for g in gps]

    def body(*refs):
        g_refs, land_refs = refs[:n], refs[n:2 * n]
        send_sems, recv_sems = refs[2 * n:]
        x, y, c = _place()
        cps = [_remote(g_refs[t].at[j, pl.ds((1 - c) * halves[t], halves[t]), :], land_refs[t].at[j], send_sems, recv_sems,
                       4 * t + j, (x, y, 1 - c)) for t in range(n) for j in range(4)]
        for cp in cps:
            cp.start()
        for cp in cps:
            cp.wait()

    return _pcall(body, name="grad_pair_swap", in_specs=[ANY] * n, out_specs=(ANY,) * n,
                  out_shape=tuple(jax.ShapeDtypeStruct((4, h, g.shape[2]), F32) for g, h in zip(gps, halves)),
                  scratch_shapes=[pltpu.SemaphoreType.DMA((4 * n,)), pltpu.SemaphoreType.DMA((4 * n,))],
                  compiler_params=_params())(*gps)


def _pair_add(name, gp, land, cidx):
    _, half, cols = land.shape
    tr = _tile(half, 512)
    nh = half // tr

    def body(c_ref, g_ref, l_ref, o_ref):
        o_ref[...] = (g_ref[...] + l_ref[...]).astype(o_ref.dtype)

    grid_spec = pltpu.PrefetchScalarGridSpec(
        num_scalar_prefetch=1, grid=(4, nh),
        in_specs=[pl.BlockSpec((None, tr, cols), lambda j, i, c: (j, c[0] * nh + i, 0)),
                  pl.BlockSpec((None, tr, cols), lambda j, i, c: (j, i, 0))],
        out_specs=pl.BlockSpec((None, tr, cols), lambda j, i, c: (j, i, 0)))
    return _pcall(body, name=f"pair_add_{name}", grid_spec=grid_spec, out_shape=jax.ShapeDtypeStruct((4, half, cols), BF16),
                  compiler_params=_params())(cidx, gp, land)


def _chip_scatter(css):
    n = len(css)

    def body(*refs):
        s_refs, land_refs = refs[:n], refs[n:2 * n]
        send_sems, recv_sems, local_sems = refs[2 * n:]
        x, y, c = _place()
        me = 2 * x + y
        chips = [(1 - x, y), (x, 1 - y), (1 - x, 1 - y)]
        mine = [pltpu.make_async_copy(s_refs[t].at[me], land_refs[t].at[me], local_sems.at[t]) for t in range(n)]
        cps = [_remote(s_refs[t].at[2 * px + py], land_refs[t].at[me], send_sems, recv_sems, 3 * t + j, (px, py, c))
               for t in range(n) for j, (px, py) in enumerate(chips)]
        for cp in mine + cps:
            cp.start()
        for t in range(n):
            for j, (px, py) in enumerate(chips):
                _remote(s_refs[t].at[me], land_refs[t].at[2 * px + py], send_sems, recv_sems, 3 * t + j, (px, py, c)).wait_recv()
        for cp in cps:
            cp.wait_send()
        for cp in mine:
            cp.wait()

    return _pcall(body, name="grad_chip_scatter", in_specs=[ANY] * n, out_specs=(ANY,) * n,
                  out_shape=tuple(jax.ShapeDtypeStruct(s.shape, s.dtype) for s in css),
                  scratch_shapes=[pltpu.SemaphoreType.DMA((3 * n,)), pltpu.SemaphoreType.DMA((3 * n,)),
                                  pltpu.SemaphoreType.DMA((n,))],
                  compiler_params=_params())(*css)


def _chip_sum(name, land):
    _, half, cols = land.shape
    tr = _tile(half, 512)

    def body(l_ref, o_ref):
        o_ref[...] = ((l_ref[0].astype(F32) + l_ref[1].astype(F32)) + l_ref[2].astype(F32)) + l_ref[3].astype(F32)

    return _pcall(body, name=f"chip_sum_{name}", grid=(half // tr,),
                  in_specs=[pl.BlockSpec((4, tr, cols), lambda i: (0, i, 0))],
                  out_specs=pl.BlockSpec((tr, cols), lambda i: (i, 0)),
                  out_shape=jax.ShapeDtypeStruct((half, cols), F32), compiler_params=_params())(land)


def _pair_join(reds):
    n = len(reds)

    def body(*refs):
        r_refs, out_refs = refs[:n], refs[n:2 * n]
        send_sems, recv_sems, local_sems = refs[2 * n:]
        x, y, c = _place()
        mine = [pltpu.make_async_copy(r_refs[t], out_refs[t].at[c], local_sems.at[t]) for t in range(n)]
        cps = [_remote(r_refs[t], out_refs[t].at[c], send_sems, recv_sems, t, (x, y, 1 - c)) for t in range(n)]
        for cp in mine + cps:
            cp.start()
        for t in range(n):
            _remote(r_refs[t], out_refs[t].at[1 - c], send_sems, recv_sems, t, (x, y, 1 - c)).wait_recv()
        for cp in cps:
            cp.wait_send()
        for cp in mine:
            cp.wait()

    return _pcall(body, name="grad_pair_join", in_specs=[ANY] * n, out_specs=(ANY,) * n,
                  out_shape=tuple(jax.ShapeDtypeStruct((2,) + r.shape, F32) for r in reds),
                  scratch_shapes=[pltpu.SemaphoreType.DMA((n,)), pltpu.SemaphoreType.DMA((n,)), pltpu.SemaphoreType.DMA((n,))],
                  compiler_params=_params())(*reds)


def _adamw(name, g, w, m, v):
    rows, cols = g.shape
    tr = _tile(rows, 512)

    def body(g_ref, w_ref, m_ref, v_ref, d_ref, mo_ref, vo_ref):
        gv = g_ref[...]
        mn = ADAM_B1 * m_ref[...] + (1.0 - ADAM_B1) * gv
        vn = ADAM_B2 * v_ref[...] + (1.0 - ADAM_B2) * jnp.square(gv)
        m_hat = mn / (1.0 - ADAM_B1 ** ADAM_STEP)
        v_hat = vn / (1.0 - ADAM_B2 ** ADAM_STEP)
        d_ref[...] = -ADAM_LR * (m_hat / (jnp.sqrt(v_hat) + ADAM_EPS) + ADAM_WD * w_ref[...])
        mo_ref[...] = mn
        vo_ref[...] = vn

    blk = pl.BlockSpec((tr, cols), lambda i: (i, 0))
    shp = jax.ShapeDtypeStruct((rows, cols), F32)
    return _pcall(body, name=f"adamw_{name}", grid=(rows // tr,), in_specs=[blk] * 4, out_specs=(blk,) * 3,
                  out_shape=(shp,) * 3, compiler_params=_params())(g, w, m, v)


def _pack(pieces, rows, dtype):
    flat = jnp.concatenate([p.reshape(-1).astype(dtype) for p in pieces])
    return jnp.pad(flat, (0, rows * PACK_COLS - flat.shape[0])).reshape(rows, PACK_COLS)


def _unpack(pack, shapes):
    flat = pack.reshape(-1)
    out, off = [], 0
    for shp in shapes:
        n = math.prod(shp)
        out.append(flat[off:off + n].reshape(shp))
        off += n
    return out


def _shard_of(full, axis, j):
    n = full.shape[axis] // 4
    return lax.slice_in_dim(full, j * n, (j + 1) * n, axis=axis)


def kernel(x, even_w_in, even_conv_w, even_conv_b, ssd_a_log, ssd_dt_bias, ssd_d, ssd_norm_w, lru_w_a, lru_b_a, lru_w_x, lru_b_x, lru_lambda, even_w_out, odd_w_in, hgrn_lb_logits, hgrn_norm_w, odd_w_out, norm_mix, norm_mlp, mlp_w1, mlp_w2, norm_final, loss_target, m_even_w_in, m_even_conv_w, m_even_conv_b, m_ssd_a_log, m_ssd_dt_bias, m_ssd_d, m_ssd_norm_w, m_lru_w_a, m_lru_b_a, m_lru_w_x, m_lru_b_x, m_lru_lambda, m_even_w_out, m_odd_w_in, m_hgrn_lb_logits, m_hgrn_norm_w, m_odd_w_out, m_norm_mix, m_norm_mlp, m_mlp_w1, m_mlp_w2, m_norm_final, v_even_w_in, v_even_conv_w, v_even_conv_b, v_ssd_a_log, v_ssd_dt_bias, v_ssd_d, v_ssd_norm_w, v_lru_w_a, v_lru_b_a, v_lru_w_x, v_lru_b_x, v_lru_lambda, v_even_w_out, v_odd_w_in, v_hgrn_lb_logits, v_hgrn_norm_w, v_odd_w_out, v_norm_mix, v_norm_mlp, v_mlp_w1, v_mlp_w2, v_norm_final):
    names = [n for n, _, _, _ in WEIGHTS]
    w_loc = dict(zip(names, (even_w_in, even_conv_w, even_conv_b, ssd_a_log, ssd_dt_bias, ssd_d, ssd_norm_w, lru_w_a, lru_b_a, lru_w_x, lru_b_x, lru_lambda, even_w_out, odd_w_in, hgrn_lb_logits, hgrn_norm_w, odd_w_out, norm_mix, norm_mlp, mlp_w1, mlp_w2, norm_final)))
    m_loc = dict(zip(names, (m_even_w_in, m_even_conv_w, m_even_conv_b, m_ssd_a_log, m_ssd_dt_bias, m_ssd_d, m_ssd_norm_w, m_lru_w_a, m_lru_b_a, m_lru_w_x, m_lru_b_x, m_lru_lambda, m_even_w_out, m_odd_w_in, m_hgrn_lb_logits, m_hgrn_norm_w, m_odd_w_out, m_norm_mix, m_norm_mlp, m_mlp_w1, m_mlp_w2, m_norm_final)))
    v_loc = dict(zip(names, (v_even_w_in, v_even_conv_w, v_even_conv_b, v_ssd_a_log, v_ssd_dt_bias, v_ssd_d, v_ssd_norm_w, v_lru_w_a, v_lru_b_a, v_lru_w_x, v_lru_b_x, v_lru_lambda, v_even_w_out, v_odd_w_in, v_hgrn_lb_logits, v_hgrn_norm_w, v_odd_w_out, v_norm_mix, v_norm_mlp, v_mlp_w1, v_mlp_w2, v_norm_final)))
    spec = {n: (blk, full, ax) for n, blk, full, ax in WEIGHTS}

    small = [n for n in names if n not in BIG]
    two_d = lambda n, v: v.reshape(BIG_2D[n])

    gathered = _gather_chips([two_d(n, w_loc[n]).astype(BF16) for n in BIG]
                             + [_pack([w_loc[n] for n in SMALL_SHARDED], 16, F32)])
    g_in0, g_out0, g_in1, g_out1, g_w1, g_w2, g_small = gathered
    w_main0, w_dt0 = _assemble_in0(g_in0)
    w_full = {n: w_loc[n] for n in names if spec[n][2] is None}
    w_full["even_w_out"] = g_out0.reshape(1, 2048, 1024)
    w_full["odd_w_in"] = jnp.concatenate([g_in1[j] for j in range(4)], axis=1)[None]
    w_full["odd_w_out"] = g_out1.reshape(1, 1024, 1024)
    w_full["mlp_w1"] = jnp.stack([jnp.concatenate([g_w1[j, l * 1024:(l + 1) * 1024] for j in range(4)], axis=1) for l in range(2)])
    w_full["mlp_w2"] = jnp.stack([jnp.concatenate([g_w2[j, l * 1024:(l + 1) * 1024] for j in range(4)], axis=0) for l in range(2)])
    shards = [_unpack(g_small[j], [spec[n][0] for n in SMALL_SHARDED]) for j in range(4)]
    for i, n in enumerate(SMALL_SHARDED):
        w_full[n] = jnp.concatenate([shards[j][i] for j in range(4)], axis=spec[n][2])

    loss_vec, grad_x, grads, big = _local_step(x, loss_target, w_full, w_main0, w_dt0)
    loss = lax.psum(loss_vec[0, 0], ("x", "y", "c"))

    def dest_pack(j):
        return _pack([grads[n].reshape(spec[n][1]) if spec[n][2] is None else _shard_of(grads[n].reshape(spec[n][1]), spec[n][2], j)
                      for n in small], SMALL_ROWS, F32)

    tensors = big + [jnp.stack([dest_pack(j) for j in range(4)])]
    tags = list(BIG) + ["small"]
    cidx = lax.axis_index("c").astype(jnp.int32).reshape(1)
    chip_sums = [_pair_add(tag, g, land, cidx) for tag, g, land in zip(tags, tensors, _pair_swap(tensors))]
    halves = [_chip_sum(tag, land) for tag, land in zip(tags, _chip_scatter(chip_sums))]
    reduced = [r.reshape(-1, r.shape[-1]) for r in _pair_join(halves)]

    outs = {}
    for n, g in zip(BIG, reduced[:-1]):
        res = (g, *_adamw(n, g, two_d(n, w_loc[n]), two_d(n, m_loc[n]), two_d(n, v_loc[n])))
        outs[n] = [r.reshape(spec[n][0]) for r in res]
    blocks = [spec[n][0] for n in small]
    wp, mp, vp = (_pack([src[n] for n in small], SMALL_ROWS, F32) for src in (w_loc, m_loc, v_loc))
    res = (reduced[-1], *_adamw("small", reduced[-1], wp, mp, vp))
    unpacked = [_unpack(r, blocks) for r in res]
    for i, n in enumerate(small):
        outs[n] = [u[i] for u in unpacked]
    return (loss, grad_x, *[outs[n][k] for k in range(4) for n in names])
```

```python
import functools
import math

import jax
import jax.numpy as jnp
from jax import lax
from jax.experimental import pallas as pl
from jax.experimental.pallas import tpu as pltpu

F32 = jnp.float32
BF16 = jnp.bfloat16
MXU_DTYPE = jnp.bfloat16
HI = lax.Precision.HIGHEST
MESH = pl.DeviceIdType.MESH

D_MODEL = 1024
EPS = 1e-6
SSD_HEADS = 16
SSD_HEADDIM = 64
HEAD_SHIFT = 6
SSD_GROUPS = 4
SSD_STATE = 128
SSD_CHUNK = 128
LRU_C = 8.0
LRU_ROWS = 256
HGRN_HEADS = 8
HGRN_HEADDIM = 128
HGRN_SUB = 32
HGRN_BLOCK = 128
HGRN_SCALE = HGRN_HEADDIM ** -0.5
CONV_ROWS = 512

ADAM_LR = 0.001
ADAM_B1 = 0.9
ADAM_B2 = 0.999
ADAM_EPS = 1e-08
ADAM_WD = 0.01
ADAM_STEP = 10

VMEM_LIMIT = 56 * 1024 * 1024
PACK_COLS = 1024
SMALL_ROWS = 288

WEIGHTS = (
    ("even_w_in", (1, 1024, 1288), (1, 1024, 5152), 2),
    ("even_conv_w", (1, 4, 768), (1, 4, 3072), 2),
    ("even_conv_b", (1, 3072), (1, 3072), None),
    ("ssd_a_log", (1, 2, 16), (1, 2, 16), None),
    ("ssd_dt_bias", (1, 2, 16), (1, 2, 16), None),
    ("ssd_d", (1, 16), (1, 16), None),
    ("ssd_norm_w", (1, 1024), (1, 1024), None),
    ("lru_w_a", (1, 2, 16, 64, 64), (1, 2, 16, 64, 64), None),
    ("lru_b_a", (1, 2, 256), (1, 2, 1024), 2),
    ("lru_w_x", (1, 2, 16, 64, 64), (1, 2, 16, 64, 64), None),
    ("lru_b_x", (1, 2, 256), (1, 2, 1024), 2),
    ("lru_lambda", (1, 2, 256), (1, 2, 1024), 2),
    ("even_w_out", (1, 512, 1024), (1, 2048, 1024), 1),
    ("odd_w_in", (1, 1024, 1280), (1, 1024, 5120), 2),
    ("hgrn_lb_logits", (2, 1024), (2, 1024), None),
    ("hgrn_norm_w", (1, 256), (1, 1024), 1),
    ("odd_w_out", (1, 256, 1024), (1, 1024, 1024), 1),
    ("norm_mix", (2, 1024), (2, 1024), None),
    ("norm_mlp", (2, 1024), (2, 1024), None),
    ("mlp_w1", (2, 1024, 1024), (2, 1024, 4096), 2),
    ("mlp_w2", (2, 1024, 1024), (2, 4096, 1024), 1),
    ("norm_final", (1024,), (1024,), None),
)
BIG = ("even_w_in", "even_w_out", "odd_w_in", "odd_w_out", "mlp_w1", "mlp_w2")
BIG_2D = {"even_w_in": (1024, 1288), "even_w_out": (512, 1024), "odd_w_in": (1024, 1280), "odd_w_out": (256, 1024),
          "mlp_w1": (2048, 1024), "mlp_w2": (2048, 1024)}
SMALL_SHARDED = ("even_conv_w", "lru_b_a", "lru_b_x", "lru_lambda", "hgrn_norm_w")


def _pcall(body, **kw):
    return pl.pallas_call(body, **kw)


def _params(**kw):
    return pltpu.CompilerParams(vmem_limit_bytes=VMEM_LIMIT, **kw)


def _tile(n, pref):
    if n <= pref:
        return n
    t = (pref // 128) * 128
    while n % t:
        t -= 128
    return t


def _dot(a, b, dims=(((1,), (0,)), ((), ())), precision=None):
    return lax.dot_general(a, b, dims, preferred_element_type=F32, precision=precision)


_NN = (((1,), (0,)), ((), ()))
_NT = (((1,), (1,)), ((), ()))
_TN = (((0,), (0,)), ((), ()))


def _mx(v):
    return v.astype(MXU_DTYPE)


def _mm(name, a, b, mode, *, out_dtype=F32, res=None, relu2=False, relu2_of=None, col_shards=1):
    if mode == "nn":
        (m, kk), (_, n) = a.shape, b.shape
    elif mode == "nt":
        (m, kk), (n, _) = a.shape, b.shape
    else:
        (kk, m), (_, n) = a.shape, b.shape
    assert res is None or relu2_of is None
    tk_pref = 1024
    if mode == "tn" and a.dtype.itemsize == 2 and b.dtype.itemsize == 2:
        tk_pref = 2048
    tm, tn, tk = _tile(m, 1024), _tile(n // col_shards, 1024), _tile(kk, tk_pref)
    nk = kk // tk
    dims = {"nn": _NN, "nt": _NT, "tn": _TN}[mode]
    a_spec = pl.BlockSpec((tk, tm), lambda i, j, k: (k, i)) if mode == "tn" else pl.BlockSpec((tm, tk), lambda i, j, k: (i, k))
    b_spec = pl.BlockSpec((tn, tk), lambda i, j, k: (j, k)) if mode == "nt" else pl.BlockSpec((tk, tn), lambda i, j, k: (k, j))
    o_spec = pl.BlockSpec((tm, tn), lambda i, j, k: (i, j))
    o_shape = (m, n)
    if col_shards > 1:
        assert tn * col_shards == n and res is None and not relu2
        o_spec = pl.BlockSpec((None, tm, tn), lambda i, j, k: (j, i, 0))
        o_shape = (col_shards, m, tn)
    extra = res if res is not None else relu2_of
    has_res = extra is not None

    def body(*refs):
        a_ref, b_ref = refs[0], refs[1]
        res_ref = refs[2] if has_res else None
        outs = refs[2 + has_res:2 + has_res + 1 + relu2]

        def finish(r):
            if res is not None:
                r = r + res_ref[...]
            if relu2_of is not None:
                r = r * (2.0 * jnp.maximum(res_ref[...], 0.0))
            if relu2:
                outs[0][...] = r
                outs[1][...] = jnp.square(jnp.maximum(r, 0.0)).astype(outs[1].dtype)
            else:
                outs[0][...] = r.astype(outs[0].dtype)

        prod = _dot(_mx(a_ref[...]), _mx(b_ref[...]), dims)
        if nk == 1:
            finish(prod)
            return
        acc = refs[-1]
        k = pl.program_id(2)

        @pl.when(k == 0)
        def _():
            acc[...] = prod

        @pl.when(k > 0)
        def _():
            acc[...] += prod

        @pl.when(k == nk - 1)
        def _():
            finish(acc[...])

    in_specs = [a_spec, b_spec] + ([o_spec] if has_res else [])
    if relu2:
        out_shape = (jax.ShapeDtypeStruct((m, n), F32), jax.ShapeDtypeStruct((m, n), BF16))
        out_specs = (o_spec, o_spec)
    else:
        out_shape = jax.ShapeDtypeStruct(o_shape, out_dtype)
        out_specs = o_spec
    args = (a, b) + ((extra,) if has_res else ())
    return _pcall(body, name=name, grid=(m // tm, n // tn, nk), in_specs=in_specs, out_specs=out_specs,
                  out_shape=out_shape, scratch_shapes=[pltpu.VMEM((tm, tn), F32)] if nk > 1 else [],
                  compiler_params=_params())(*args)


def _pw_fwd(name, f, ins, params, out_dtypes, tc, ncol, tm=256):
    t = ins[0][0].shape[0]
    tm = min(tm, t)
    ni, npar = len(ins), len(params)

    def body(*refs):
        vals = f(*[r[...].astype(F32) for r in refs[:ni]], *[r[...] for r in refs[ni:ni + npar]])
        for o, v in zip(refs[ni + npar:], vals):
            o[...] = v.astype(o.dtype)

    in_specs = [pl.BlockSpec((tm, tc), lambda j, i, off=off: (i, off + j)) for _, off in ins]
    in_specs += [pl.BlockSpec((1, tc), lambda j, i, off=off: (0, off + j)) for _, off in params]
    out_specs = tuple(pl.BlockSpec((tm, tc), lambda j, i: (i, j)) for _ in out_dtypes)
    out_shape = tuple(jax.ShapeDtypeStruct((t, ncol * tc), d) for d in out_dtypes)
    return _pcall(body, name=name, grid=(ncol, t // tm), in_specs=in_specs, out_specs=out_specs, out_shape=out_shape,
                  compiler_params=_params())(*[a for a, _ in ins], *[p for p, _ in params])


def _pw_bwd(name, f, ins, params, douts, tc, ncol, want, adds=None, tm=256, out_dtypes=None):
    adds = adds or {}
    out_dtypes = out_dtypes or [F32] * len(want)
    t = ins[0][0].shape[0]
    tm = min(tm, t)
    ni, npar, nd, na = len(ins), len(params), len(douts), len(adds)
    add_keys = sorted(adds)

    def body(*refs):
        in_refs, p_refs = refs[:ni], refs[ni:ni + npar]
        d_refs = refs[ni + npar:ni + npar + nd]
        a_refs = refs[ni + npar + nd:ni + npar + nd + na]
        o_refs = refs[ni + npar + nd + na:]
        _, vjp = jax.vjp(f, *[r[...].astype(F32) for r in in_refs], *[r[...] for r in p_refs])
        cts = vjp(tuple(d[...].astype(F32) for d in d_refs))
        for o, kidx in zip(o_refs[:len(want)], want):
            v = cts[kidx]
            if kidx in adds:
                v = v + a_refs[add_keys.index(kidx)][...]
            o[...] = v.astype(o.dtype)
        for p in range(npar):
            o = o_refs[len(want) + p]

            @pl.when(pl.program_id(1) == 0)
            def _(o=o):
                o[...] = jnp.zeros_like(o)

            o[...] += cts[ni + p]

    in_specs = [pl.BlockSpec((tm, tc), lambda j, i, off=off: (i, off + j)) for _, off in ins]
    in_specs += [pl.BlockSpec((1, tc), lambda j, i, off=off: (0, off + j)) for _, off in params]
    in_specs += [pl.BlockSpec((tm, tc), lambda j, i: (i, j)) for _ in range(nd + na)]
    out_specs = tuple([pl.BlockSpec((tm, tc), lambda j, i: (i, j)) for _ in want]
                      + [pl.BlockSpec((1, tc), lambda j, i: (0, j)) for _ in params])
    out_shape = tuple([jax.ShapeDtypeStruct((t, ncol * tc), dt) for dt in out_dtypes]
                      + [jax.ShapeDtypeStruct((1, ncol * tc), F32) for _ in params])
    res = _pcall(body, name=name, grid=(ncol, t // tm), in_specs=in_specs, out_specs=out_specs, out_shape=out_shape,
                 compiler_params=_params())(*[a for a, _ in ins], *[p for p, _ in params], *douts, *[adds[k] for k in add_keys])
    return list(res[:len(want)]), list(res[len(want):])


def _rms(x, g):
    return (x * lax.rsqrt(jnp.mean(x * x, axis=-1, keepdims=True) + EPS)) * g


def _f_norm(x, g):
    return (_rms(x, g),)


def _f_silu(c):
    return (jax.nn.silu(c),)


def _f_softplus(d, b):
    return (jax.nn.softplus(d + b),)


def _f_add2(a, b):
    return (a + b,)


def _f_add3(a, b, c):
    return (a + b + c,)


def _f_ssd_post(yf, yb, xs, z, dskip, nw):
    u = (yf + yb + dskip * xs) * jax.nn.silu(z)
    return (_rms(u, nw),)


def _neg_expm1(v):
    t = jnp.tanh(0.5 * v)
    return -2.0 * t / (1.0 - t)


def _f_lru_gates(pre_a, pre_x, u, ba, bx, lam):
    rg = jax.nn.sigmoid(pre_a + ba)
    ig = jax.nn.sigmoid(pre_x + bx)
    log_a = -LRU_C * rg * jax.nn.softplus(-lam)
    return jnp.exp(log_a), jnp.sqrt(_neg_expm1(2.0 * log_a)) * (ig * u)


def _f_lru_post(hf, hb, gate):
    return ((hf + hb) * jax.nn.gelu(gate),)


def _f_hgrn_pre(fr, l0, l1):
    lb = jax.nn.sigmoid(l1 - l0)
    k = (1.0 - lb) * jax.nn.sigmoid(-fr)
    return k, jnp.log1p(-k)


def _f_hgrn_post(of, ob, gate, nw):
    return (_rms(of + ob, nw) * jax.nn.silu(gate),)


def _loss_head(x, tgt, g, tm=256):
    t, d = x.shape
    tm = min(tm, t)

    def body(x_ref, t_ref, g_ref, dx_ref, dg_ref, loss_ref):
        tv = t_ref[...]

        def lf(xv, gv):
            return 0.5 * jnp.sum(jnp.mean(jnp.square(_rms(xv, gv) - tv), axis=-1))

        val, vjp = jax.vjp(lf, x_ref[...], g_ref[...])
        dx, dg = vjp(jnp.ones((), F32))
        dx_ref[...] = dx

        @pl.when(pl.program_id(0) == 0)
        def _():
            dg_ref[...] = jnp.zeros_like(dg_ref)
            loss_ref[...] = jnp.zeros_like(loss_ref)

        dg_ref[...] += dg
        loss_ref[...] += jnp.full(loss_ref.shape, val, F32)

    row = pl.BlockSpec((tm, d), lambda i: (i, 0))
    vec = pl.BlockSpec((1, d), lambda i: (0, 0))
    return _pcall(body, name="loss_head", grid=(t // tm,), in_specs=[row, row, vec],
                  out_specs=(row, vec, pl.BlockSpec((1, 128), lambda i: (0, 0))),
                  out_shape=(jax.ShapeDtypeStruct((t, d), F32), jax.ShapeDtypeStruct((1, d), F32),
                             jax.ShapeDtypeStruct((1, 128), F32)), compiler_params=_params())(x, tgt, g)


def _shifted(x, d, prev, nxt, first, last):
    r = x.shape[0]
    row = lax.broadcasted_iota(jnp.int32, x.shape, 0)
    if d < 0:
        out = pltpu.roll(x, -d, 0)
        for q in range(-d):
            pv = jnp.where(first, 0.0, prev[8 + d + q:8 + d + q + 1, :])
            out = jnp.where(row == q, pv, out)
        return out
    out = pltpu.roll(x, r - d, 0)
    for q in range(d):
        nv = jnp.where(last, 0.0, nxt[q:q + 1, :])
        out = jnp.where(row == r - d + q, nv, out)
    return out


def _halo_specs(ts, tc, s):
    nb8 = s // 8
    cur = pl.BlockSpec((None, ts, tc), lambda n, i, j: (n, i, j))
    prev = pl.BlockSpec((None, 8, tc), lambda n, i, j: (n, jnp.maximum(i * (ts // 8) - 1, 0), j))
    nxt = pl.BlockSpec((None, 8, tc), lambda n, i, j: (n, jnp.minimum((i + 1) * (ts // 8), nb8 - 1), j))
    return cur, prev, nxt


def _conv_fwd(p3, w, b, ncol, tc=1024):
    nbatch, s, _ = p3.shape
    ts = min(CONV_ROWS, s)
    nblk = s // ts

    def body(x_ref, pv_ref, nx_ref, w_ref, b_ref, o_ref):
        i = pl.program_id(1)
        first, last = i == 0, i == nblk - 1
        x, pv, nx = x_ref[...], pv_ref[...], nx_ref[...]
        wv = w_ref[...]
        out = b_ref[...] + wv[1:2] * x
        out = out + wv[0:1] * _shifted(x, -1, pv, nx, first, last)
        out = out + wv[2:3] * _shifted(x, 1, pv, nx, first, last)
        out = out + wv[3:4] * _shifted(x, 2, pv, nx, first, last)
        o_ref[...] = out

    cur, prev, nxt = _halo_specs(ts, tc, s)
    return _pcall(body, name="conv_fwd", grid=(nbatch, nblk, ncol),
                  in_specs=[cur, prev, nxt, pl.BlockSpec((4, tc), lambda n, i, j: (0, j)),
                            pl.BlockSpec((1, tc), lambda n, i, j: (0, j))],
                  out_specs=cur, out_shape=jax.ShapeDtypeStruct((nbatch, s, ncol * tc), F32),
                  compiler_params=_params())(p3, p3, p3, w, b)


def _conv_bwd(dc3, p3, w, ncol, tc=1024):
    nbatch, s, _ = dc3.shape
    ts = min(CONV_ROWS, s)
    nblk = s // ts

    def body(d_ref, dpv_ref, dnx_ref, x_ref, pv_ref, nx_ref, w_ref, dx_ref, dw_ref):
        n, i = pl.program_id(1), pl.program_id(2)
        first, last = i == 0, i == nblk - 1
        d, dpv, dnx = d_ref[...], dpv_ref[...], dnx_ref[...]
        x, pv, nx = x_ref[...], pv_ref[...], nx_ref[...]
        wv = w_ref[...]
        dx = wv[1:2] * d
        dx = dx + wv[0:1] * _shifted(d, 1, dpv, dnx, first, last)
        dx = dx + wv[2:3] * _shifted(d, -1, dpv, dnx, first, last)
        dx = dx + wv[3:4] * _shifted(d, -2, dpv, dnx, first, last)
        dx_ref[...] = dx.astype(dx_ref.dtype)

        @pl.when((n == 0) & (i == 0))
        def _():
            dw_ref[...] = jnp.zeros_like(dw_ref)

        dw_ref[0:1, :] += jnp.sum(d * _shifted(x, -1, pv, nx, first, last), axis=0, keepdims=True)
        dw_ref[1:2, :] += jnp.sum(d * x, axis=0, keepdims=True)
        dw_ref[2:3, :] += jnp.sum(d * _shifted(x, 1, pv, nx, first, last), axis=0, keepdims=True)
        dw_ref[3:4, :] += jnp.sum(d * _shifted(x, 2, pv, nx, first, last), axis=0, keepdims=True)
        dw_ref[4:5, :] += jnp.sum(d, axis=0, keepdims=True)

    nb8 = s // 8
    cur = pl.BlockSpec((None, ts, tc), lambda j, n, i: (n, i, j))
    prev = pl.BlockSpec((None, 8, tc), lambda j, n, i: (n, jnp.maximum(i * (ts // 8) - 1, 0), j))
    nxt = pl.BlockSpec((None, 8, tc), lambda j, n, i: (n, jnp.minimum((i + 1) * (ts // 8), nb8 - 1), j))
    return _pcall(body, name="conv_bwd", grid=(ncol, nbatch, nblk),
                  in_specs=[cur, prev, nxt, cur, prev, nxt, pl.BlockSpec((4, tc), lambda j, n, i: (0, j))],
                  out_specs=(cur, pl.BlockSpec((8, tc), lambda j, n, i: (0, j))),
                  out_shape=(jax.ShapeDtypeStruct((nbatch, s, ncol * tc), BF16), jax.ShapeDtypeStruct((8, ncol * tc), F32)),
                  compiler_params=_params())(dc3, dc3, dc3, p3, p3, p3, w)


def _block_scan(coef, inp, reverse):
    r = coef.shape[0]
    row = lax.broadcasted_iota(jnp.int32, coef.shape, 0)
    a, b = coef, inp
    d = 1
    while d < r:
        if reverse:
            keep = row < r - d
            a_sh, b_sh = pltpu.roll(a, r - d, 0), pltpu.roll(b, r - d, 0)
        else:
            keep = row >= d
            a_sh, b_sh = pltpu.roll(a, d, 0), pltpu.roll(b, d, 0)
        b = b + a * jnp.where(keep, b_sh, 0.0)
        a = a * jnp.where(keep, a_sh, 1.0)
        d *= 2
    return a, b


def _lru_scan(a3, b3, reverse):
    nbatch, s, w = a3.shape
    ts = min(LRU_ROWS, s)
    nblk = s // ts
    edge = 0 if reverse else ts - 1

    def body(a_ref, b_ref, h_ref, carry):
        @pl.when(pl.program_id(1) == 0)
        def _():
            carry[...] = jnp.zeros_like(carry)

        ca, hb = _block_scan(a_ref[...], b_ref[...], reverse)
        h = hb + ca * carry[0:1, :]
        h_ref[...] = h
        carry[0:1, :] = h[edge:edge + 1, :]

    blk = pl.BlockSpec((None, ts, w), (lambda n, i: (n, nblk - 1 - i, 0)) if reverse else (lambda n, i: (n, i, 0)))
    return _pcall(body, name=f"lru_scan_r{int(reverse)}", grid=(nbatch, nblk), in_specs=[blk, blk], out_specs=blk,
                  out_shape=jax.ShapeDtypeStruct((nbatch, s, w), F32), scratch_shapes=[pltpu.VMEM((8, w), F32)],
                  compiler_params=_params())(a3, b3)


def _lru_scan_bwd(a3, h3, dh3, reverse):
    nbatch, s, w = a3.shape
    ts = min(LRU_ROWS, s)
    nblk = s // ts
    nb8 = s // 8
    tpb = ts // 8

    def body(a_ref, aa_ref, h_ref, hh_ref, dh_ref, g_ref, da_ref, carry):
        i = pl.program_id(1)

        @pl.when(i == 0)
        def _():
            carry[...] = jnp.zeros_like(carry)

        a, h = a_ref[...], h_ref[...]
        row = lax.broadcasted_iota(jnp.int32, a.shape, 0)
        if reverse:
            a_edge = jnp.where(i == 0, 0.0, aa_ref[7:8, :])
            c = jnp.where(row == 0, a_edge, pltpu.roll(a, 1, 0))
            h_edge = jnp.where(i == nblk - 1, 0.0, hh_ref[0:1, :])
            h_sh = jnp.where(row == ts - 1, h_edge, pltpu.roll(h, ts - 1, 0))
        else:
            a_edge = jnp.where(i == 0, 0.0, aa_ref[0:1, :])
            c = jnp.where(row == ts - 1, a_edge, pltpu.roll(a, ts - 1, 0))
            h_edge = jnp.where(i == nblk - 1, 0.0, hh_ref[7:8, :])
            h_sh = jnp.where(row == 0, h_edge, pltpu.roll(h, 1, 0))
        cc, gb = _block_scan(c, dh_ref[...], not reverse)
        g = gb + cc * carry[0:1, :]
        g_ref[...] = g
        carry[0:1, :] = g[ts - 1:ts, :] if reverse else g[0:1, :]
        da_ref[...] = g * h_sh

    if reverse:
        bi = lambda i: i
    else:
        bi = lambda i: nblk - 1 - i
    blk = pl.BlockSpec((None, ts, w), lambda n, i: (n, bi(i), 0))
    before = pl.BlockSpec((None, 8, w), lambda n, i: (n, jnp.maximum(bi(i) * tpb - 1, 0), 0))
    after = pl.BlockSpec((None, 8, w), lambda n, i: (n, jnp.minimum((bi(i) + 1) * tpb, nb8 - 1), 0))
    a_tile, h_tile = (before, after) if reverse else (after, before)
    return _pcall(body, name=f"lru_scan_bwd_r{int(reverse)}", grid=(nbatch, nblk), in_specs=[blk, a_tile, blk, h_tile, blk],
                  out_specs=(blk, blk),
                  out_shape=(jax.ShapeDtypeStruct((nbatch, s, w), F32), jax.ShapeDtypeStruct((nbatch, s, w), F32)),
                  scratch_shapes=[pltpu.VMEM((8, w), F32)], compiler_params=_params())(a3, a3, h3, h3, dh3)


def _head_expand(lane0):
    return (jnp.right_shift(lax.broadcasted_iota(jnp.int32, (128, 1024), 1), HEAD_SHIFT) + lane0
            == lax.broadcasted_iota(jnp.int32, (128, 1024), 0)).astype(F32)


def _head_reduce(lane0):
    return (jnp.right_shift(lax.broadcasted_iota(jnp.int32, (1024, 128), 0), HEAD_SHIFT) + lane0
            == lax.broadcasted_iota(jnp.int32, (1024, 128), 1)).astype(F32)


def _time_mask(q, reverse):
    ri = lax.broadcasted_iota(jnp.int32, (q, q), 0)
    ci = lax.broadcasted_iota(jnp.int32, (q, q), 1)
    return (ri <= ci) if reverse else (ri >= ci)


def _ssd_common(xs_ref, bc_ref, dt_ref, al_ref, reverse, lane0):
    q = xs_ref.shape[0]
    edge = 0 if reverse else q - 1
    dt = dt_ref[...]
    a = -jnp.exp(al_ref[...])
    mask = _time_mask(q, reverse)
    expand = _head_expand(lane0)
    cum = _dot(mask.astype(F32), dt * a, precision=HI)
    cum_x = _dot(cum, expand, precision=HI)
    dt_x = _dot(dt, expand, precision=HI)
    last_x = cum_x[edge:edge + 1, :]
    xs = xs_ref[...]
    bc = bc_ref[...]
    return dict(q=q, edge=edge, lane0=lane0, dt=dt, a=a, mask=mask, cum_t=cum.T, cum_x=cum_x, dt_x=dt_x, xs=xs,
                v=xs * dt_x, e_c=jnp.exp(cum_x), w=jnp.exp(last_x - cum_x), e_l=jnp.exp(last_x),
                bm=bc[:, :512], cm=bc[:, 512:])


def _ssd_decay(c, h):
    row = c["lane0"] + h
    seg = c["cum_x"][:, h * SSD_HEADDIM:h * SSD_HEADDIM + 1] - c["cum_t"][row:row + 1, :]
    return jnp.where(c["mask"], jnp.exp(jnp.minimum(seg, 0.0)), 0.0)


def _head_masks():
    lane = jnp.right_shift(lax.broadcasted_iota(jnp.int32, (1, 256), 1), HEAD_SHIFT)
    return [lane == e for e in range(4)]


def _ssd_fwd(xbc3, dt3, alog, reverse):
    nbatch, s, _ = xbc3.shape
    q = min(SSD_CHUNK, s)
    nc = s // q
    lane0 = SSD_HEADS * int(reverse)

    def body(xs_ref, bc_ref, dt_ref, al_ref, y_ref, st_ref, st):
        @pl.when(pl.program_id(1) == 0)
        def _():
            st[...] = jnp.zeros_like(st)

        st_ref[...] = st[...]
        c = _ssd_common(xs_ref, bc_ref, dt_ref, al_ref, reverse, lane0)
        hm = _head_masks()
        for g in range(SSD_GROUPS):
            sl = slice(g * 256, (g + 1) * 256)
            cg, bg = _mx(c["cm"][:, g * 128:(g + 1) * 128]), _mx(c["bm"][:, g * 128:(g + 1) * 128])
            cb = _dot(cg, bg, _NT)
            vg = c["v"][:, sl]
            s0 = st[:, sl]
            yg = _dot(cg, _mx(s0)) * c["e_c"][:, sl]
            for e in range(4):
                m = _ssd_decay(c, 4 * g + e) * cb
                yg = yg + _dot(_mx(m), _mx(jnp.where(hm[e], vg, 0.0)))
            y_ref[:, sl] = yg
            st[:, sl] = c["e_l"][:, sl] * s0 + _dot(bg, _mx(vg * c["w"][:, sl]), _TN)

    ck = (lambda i: nc - 1 - i) if reverse else (lambda i: i)
    xs_spec = pl.BlockSpec((None, q, 1024), lambda n, i: (n, ck(i), 0))
    bc_spec = pl.BlockSpec((None, q, 1024), lambda n, i: (n, ck(i), 1))
    dt_spec = pl.BlockSpec((None, q, 128), lambda n, i: (n, ck(i), 0))
    al_spec = pl.BlockSpec((1, 128), lambda n, i: (0, 0))
    st_spec = pl.BlockSpec((None, None, 128, 1024), lambda n, i: (n, ck(i), 0, 0))
    return _pcall(body, name=f"ssd_fwd_r{int(reverse)}", grid=(nbatch, nc), in_specs=[xs_spec, bc_spec, dt_spec, al_spec],
                  out_specs=(xs_spec, st_spec),
                  out_shape=(jax.ShapeDtypeStruct((nbatch, s, 1024), F32), jax.ShapeDtypeStruct((nbatch, nc, 128, 1024), F32)),
                  scratch_shapes=[pltpu.VMEM((128, 1024), F32)], compiler_params=_params())(xbc3, xbc3, dt3, alog)


def _ssd_bwd(xbc3, dt3, alog, st4, dy3, reverse):
    nbatch, s, _ = xbc3.shape
    q = min(SSD_CHUNK, s)
    nc = s // q
    lane0 = SSD_HEADS * int(reverse)

    def body(xs_ref, bc_ref, dt_ref, al_ref, st0_ref, dy_ref, dxs_ref, dbc_ref, ddt_ref, dal_ref, dst):
        n, i = pl.program_id(0), pl.program_id(1)

        @pl.when(i == 0)
        def _():
            dst[...] = jnp.zeros_like(dst)

        @pl.when((i == 0) & (n == 0))
        def _():
            dal_ref[...] = jnp.zeros_like(dal_ref)

        c = _ssd_common(xs_ref, bc_ref, dt_ref, al_ref, reverse, lane0)
        hm = _head_masks()
        reduce_m = _head_reduce(lane0)
        s0_all, ds1_all, dy = st0_ref[...], dst[...], dy_ref[...]
        lane = lax.broadcasted_iota(jnp.int32, (q, 128), 1)
        sub = lax.broadcasted_iota(jnp.int32, (128, q), 0)
        rowacc = jnp.zeros((q, 128), F32)
        colacc_t = jnp.zeros((128, q), F32)
        dv_l, yst_l, dvbar_l, dk_l, dc_l = [], [], [], [], []
        for g in range(SSD_GROUPS):
            sl = slice(g * 256, (g + 1) * 256)
            cg, bg = _mx(c["cm"][:, g * 128:(g + 1) * 128]), _mx(c["bm"][:, g * 128:(g + 1) * 128])
            cb = _dot(cg, bg, _NT)
            vg, dyg, wg, ecg = c["v"][:, sl], dy[:, sl], c["w"][:, sl], c["e_c"][:, sl]
            s0, ds1 = _mx(s0_all[:, sl]), _mx(ds1_all[:, sl])
            dye = _mx(dyg * ecg)
            yst_l.append(_dot(cg, s0) * ecg)
            dcg = _dot(dye, s0, _NT)
            dst[:, sl] = c["e_l"][:, sl] * ds1_all[:, sl] + _dot(cg, dye, _TN)
            vbar = _mx(vg * wg)
            dvbar = _dot(bg, ds1)
            dvbar_l.append(dvbar)
            dvg = dvbar * wg
            dkg = _dot(vbar, ds1, _NT)
            for e in range(4):
                h = 4 * g + e
                m = _ssd_decay(c, h)
                dyh, vh = _mx(jnp.where(hm[e], dyg, 0.0)), _mx(jnp.where(hm[e], vg, 0.0))
                dvg = dvg + _dot(_mx(m * cb), dyh, _TN)
                dcb = _dot(dyh, vh, _NT) * m
                dcbb = _mx(dcb)
                dcg = dcg + _dot(dcbb, bg)
                dkg = dkg + _dot(dcbb, cg, _TN)
                wmat = dcb * cb
                rowacc = jnp.where(lane == lane0 + h, jnp.sum(wmat, axis=1, keepdims=True), rowacc)
                colacc_t = jnp.where(sub == lane0 + h, jnp.sum(wmat, axis=0, keepdims=True), colacc_t)
            dv_l.append(dvg)
            dk_l.append(dkg)
            dc_l.append(dcg)
        dv = jnp.concatenate(dv_l, axis=1)
        yst = jnp.concatenate(yst_l, axis=1)
        dvbar = jnp.concatenate(dvbar_l, axis=1)
        t1 = _dot(dy * yst, reduce_m, precision=HI)
        t2 = _dot(c["v"] * c["w"] * dvbar, reduce_m, precision=HI)
        dlast = jnp.sum(t2, axis=0, keepdims=True) + _dot(
            c["e_l"] * jnp.sum(ds1_all * s0_all, axis=0, keepdims=True), reduce_m, precision=HI)
        dcum = rowacc - colacc_t.T + t1 - t2
        dcum = dcum + jnp.where(lax.broadcasted_iota(jnp.int32, (q, 128), 0) == c["edge"], dlast, 0.0)
        dda = _dot(c["mask"].astype(F32), dcum, _TN, precision=HI)
        ddt_ref[...] = dda * c["a"] + _dot(dv * c["xs"], reduce_m, precision=HI)
        dal_ref[...] += jnp.sum(dda * c["dt"], axis=0, keepdims=True) * c["a"]
        dxs_ref[...] = dv * c["dt_x"]
        dbc_ref[...] = jnp.concatenate(dk_l + dc_l, axis=1)

    ck = (lambda i: i) if reverse else (lambda i: nc - 1 - i)
    xs_spec = pl.BlockSpec((None, q, 1024), lambda n, i: (n, ck(i), 0))
    bc_spec = pl.BlockSpec((None, q, 1024), lambda n, i: (n, ck(i), 1))
    dt_spec = pl.BlockSpec((None, q, 128), lambda n, i: (n, ck(i), 0))
    al_spec = pl.BlockSpec((1, 128), lambda n, i: (0, 0))
    st_spec = pl.BlockSpec((None, None, 128, 1024), lambda n, i: (n, ck(i), 0, 0))
    return _pcall(body, name=f"ssd_bwd_r{int(reverse)}", grid=(nbatch, nc),
                  in_specs=[xs_spec, bc_spec, dt_spec, al_spec, st_spec, xs_spec],
                  out_specs=(xs_spec, xs_spec, dt_spec, al_spec),
                  out_shape=(jax.ShapeDtypeStruct((nbatch, s, 1024), F32), jax.ShapeDtypeStruct((nbatch, s, 1024), F32),
                             jax.ShapeDtypeStruct((nbatch, s, 128), F32), jax.ShapeDtypeStruct((1, 128), F32)),
                  scratch_shapes=[pltpu.VMEM((128, 1024), F32)], compiler_params=_params())(xbc3, xbc3, dt3, alog, st4, dy3)


def _gla_sub(q_ref, k_ref, g_ref, b, rs, reverse):
    sq = HGRN_SUB
    edge = 0 if reverse else sq - 1
    mask = _time_mask(sq, reverse)
    bc = _dot(mask.astype(F32), g_ref[b, rs, :], precision=HI)
    last = bc[edge:edge + 1, :]
    eb, enb, elb = jnp.exp(bc), jnp.exp(-bc), jnp.exp(last - bc)
    kv = k_ref[b, rs, :]
    return dict(mask=mask, edge=edge, eb=eb, enb=enb, elb=elb, e_l=jnp.exp(last),
                qt=q_ref[b, rs, :] * HGRN_SCALE * eb, kt=kv * enb, kb=kv * elb)


def _gla_specs(nbatch, s, w, reverse_order):
    bq = min(HGRN_BLOCK, s)
    nblk = s // bq
    bi = (lambda i: nblk - 1 - i) if reverse_order else (lambda i: i)
    col = lambda cb: pl.BlockSpec((nbatch, bq, w), lambda i: (0, bi(i), cb))
    st_spec = pl.BlockSpec((nbatch, bq // HGRN_SUB, 128, w), lambda i: (0, bi(i), 0, 0))
    return bq, nblk, col, st_spec


def _gla_fwd(proj3, k3, g3, reverse):
    nbatch, s, w = k3.shape
    bq, nblk, col, st_spec = _gla_specs(nbatch, s, w, reverse)
    nsub = bq // HGRN_SUB

    def body(q_ref, k_ref, v_ref, g_ref, o_ref, st_ref, st):
        @pl.when(pl.program_id(0) == 0)
        def _():
            st[...] = jnp.zeros_like(st)

        for j in (reversed(range(nsub)) if reverse else range(nsub)):
            rs = slice(j * HGRN_SUB, (j + 1) * HGRN_SUB)
            for b in range(nbatch):
                st_ref[b, j] = st[b]
                c = _gla_sub(q_ref, k_ref, g_ref, b, rs, reverse)
                v = v_ref[b, rs, :]
                for h in range(HGRN_HEADS):
                    hs = slice(h * 128, (h + 1) * 128)
                    qt, vb = _mx(c["qt"][:, hs]), _mx(v[:, hs])
                    att = jnp.where(c["mask"], _dot(qt, _mx(c["kt"][:, hs]), _NT), 0.0)
                    s0 = st[b, :, hs]
                    o_ref[b, rs, hs] = _dot(_mx(att), vb) + _dot(qt, _mx(s0), _NT)
                    st[b, :, hs] = s0 * c["e_l"][:, hs] + _dot(vb, _mx(c["kb"][:, hs]), _TN)

    return _pcall(body, name=f"gla_fwd_r{int(reverse)}", grid=(nblk,), in_specs=[col(0), col(0), col(3), col(0)],
                  out_specs=(col(0), st_spec),
                  out_shape=(jax.ShapeDtypeStruct((nbatch, s, w), F32),
                             jax.ShapeDtypeStruct((nbatch, s // HGRN_SUB, 128, w), F32)),
                  scratch_shapes=[pltpu.VMEM((nbatch, 128, w), F32)], compiler_params=_params())(proj3, k3, proj3, g3)


def _gla_bwd(proj3, k3, g3, st4, do3, reverse):
    nbatch, s, w = k3.shape
    bq, nblk, col, st_spec = _gla_specs(nbatch, s, w, not reverse)
    nsub = bq // HGRN_SUB
    sq = HGRN_SUB

    def body(q_ref, k_ref, v_ref, g_ref, st_ref, do_ref, dq_ref, dk_ref, dv_ref, dg_ref, dst):
        @pl.when(pl.program_id(0) == 0)
        def _():
            dst[...] = jnp.zeros_like(dst)

        row = lax.broadcasted_iota(jnp.int32, (sq, 128), 0)
        for j in (range(nsub) if reverse else reversed(range(nsub))):
            rs = slice(j * sq, (j + 1) * sq)
            for b in range(nbatch):
                c = _gla_sub(q_ref, k_ref, g_ref, b, rs, reverse)
                s0_all, ds1_all = st_ref[b, j], dst[b]
                v, dy = v_ref[b, rs, :], do_ref[b, rs, :]
                db_l = []
                for h in range(HGRN_HEADS):
                    hs = slice(h * 128, (h + 1) * 128)
                    qt, kt, kb = c["qt"][:, hs], c["kt"][:, hs], c["kb"][:, hs]
                    qtb, ktb, kbb, vb, dyb = _mx(qt), _mx(kt), _mx(kb), _mx(v[:, hs]), _mx(dy[:, hs])
                    s0, ds1 = s0_all[:, hs], ds1_all[:, hs]
                    att = jnp.where(c["mask"], _dot(qtb, ktb, _NT), 0.0)
                    datt = _mx(jnp.where(c["mask"], _dot(dyb, vb, _NT), 0.0))
                    dqt = _dot(datt, ktb) + _dot(dyb, _mx(s0))
                    dkt = _dot(datt, qtb, _TN)
                    dkb = _dot(vb, _mx(ds1))
                    dv_ref[b, rs, hs] = _dot(_mx(att), dyb, _TN) + _dot(kbb, _mx(ds1), _NT)
                    dst[b, :, hs] = c["e_l"][:, hs] * ds1 + _dot(dyb, qtb, _TN)
                    dq_ref[b, rs, hs] = dqt * c["eb"][:, hs] * HGRN_SCALE
                    dk_ref[b, rs, hs] = dkt * c["enb"][:, hs] + dkb * c["elb"][:, hs]
                    kbk = dkb * kb
                    dlast = jnp.sum(kbk, axis=0, keepdims=True) + c["e_l"][:, hs] * jnp.sum(ds1 * s0, axis=0, keepdims=True)
                    db_l.append(dqt * qt - dkt * kt - kbk + jnp.where(row == c["edge"], dlast, 0.0))
                dg_ref[b, rs, :] = _dot(c["mask"].astype(F32), jnp.concatenate(db_l, axis=1), _TN, precision=HI)

    shp = jax.ShapeDtypeStruct((nbatch, s, w), F32)
    return _pcall(body, name=f"gla_bwd_r{int(reverse)}", grid=(nblk,),
                  in_specs=[col(0), col(0), col(3), col(0), st_spec, col(0)],
                  out_specs=(col(0),) * 4, out_shape=(shp,) * 4,
                  scratch_shapes=[pltpu.VMEM((nbatch, 128, w), F32)], compiler_params=_params())(proj3, k3, proj3, g3, st4, do3)


DIRS = (False, True)


def _block_diag(w):
    eye = jnp.eye(16, dtype=w.dtype)
    return (eye[:, None, :, None] * w[:, :, None, :]).reshape(1024, 1024)


def _diag_blocks(m):
    m4 = m.reshape(16, 64, 16, 64)
    return jnp.stack([m4[i, :, i, :] for i in range(16)], axis=0)


def _pad_lanes(v, n=128):
    return jnp.pad(v, [(0, 0)] * (v.ndim - 1) + [(0, n - v.shape[-1])])


def _mlp_fwd(tag, x, nw, w1, w2):
    (h,) = _pw_fwd(f"{tag}_norm", _f_norm, [(x, 0)], [(nw, 0)], [BF16], 1024, 1)
    a, r = _mm(f"{tag}_up", h, w1, "nn", relu2=True)
    return _mm(f"{tag}_down", r, w2, "nn", res=x), (h, a, r)


def _mlp_bwd(tag, x, nw, w1, w2, saved, dxo):
    h, a, r = saved
    dw2 = _mm(f"{tag}_dw2", r, dxo, "tn")
    da = _mm(f"{tag}_da", dxo, w2, "nt", relu2_of=a, out_dtype=BF16)
    dw1 = _mm(f"{tag}_dw1", h, da, "tn", col_shards=4)
    dh = _mm(f"{tag}_dh", da, w1, "nt")
    (dx,), (dnw,) = _pw_bwd(f"{tag}_dnorm", _f_norm, [(x, 0)], [(nw, 0)], [dh], 1024, 1, [0], adds={0: dxo})
    return dx, dw1, dw2, dnw


def _split_in0(pieces, dt_piece):
    tm = 256

    def body(p0, p1, p2, p3, p4, p5, o_ref):
        full = jnp.concatenate([p0[...], p1[...], p2[...], p3[...], p4[...], p5[:, :32]], axis=1)
        for j in range(4):
            o_ref[j] = full[:, 1288 * j:1288 * (j + 1)]

    blk = pl.BlockSpec((tm, 1024), lambda i: (i, 0))
    return _pcall(body, name="split_in0", grid=(1024 // tm,), in_specs=[blk] * 5 + [pl.BlockSpec((tm, 128), lambda i: (i, 0))],
                  out_specs=pl.BlockSpec((4, tm, 1288), lambda i: (0, i, 0)),
                  out_shape=jax.ShapeDtypeStruct((4, 1024, 1288), F32), compiler_params=_params())(*pieces, dt_piece)


def _assemble_in0(shards):
    tm = 256

    def body(s_ref, m_ref, d_ref):
        full = jnp.concatenate([s_ref[j] for j in range(4)], axis=1)
        m_ref[...] = full[:, :5120]
        d_ref[...] = jnp.concatenate([full[:, 5120:5152], jnp.zeros((tm, 96), full.dtype)], axis=1)

    return _pcall(body, name="assemble_in0", grid=(1024 // tm,), in_specs=[pl.BlockSpec((4, tm, 1288), lambda i: (0, i, 0))],
                  out_specs=(pl.BlockSpec((tm, 5120), lambda i: (i, 0)), pl.BlockSpec((tm, 128), lambda i: (i, 0))),
                  out_shape=(jax.ShapeDtypeStruct((1024, 5120), shards.dtype), jax.ShapeDtypeStruct((1024, 128), shards.dtype)),
                  compiler_params=_params())(shards)


def _local_step(x3, tgt3, w, w_main0, w_dt0):
    nb, s, d = x3.shape
    t = nb * s
    x0 = x3.reshape(t, d)
    tgt = tgt3.reshape(t, d)
    grads = {}
    row = lambda v: v.reshape(1, -1)
    to3 = lambda v: v.reshape(nb, s, v.shape[-1])
    to2 = lambda v: v.reshape(-1, v.shape[-1])

    conv_w, conv_b = w["even_conv_w"][0], row(w["even_conv_b"][0])
    nmix0 = row(w["norm_mix"][0])
    (h0,) = _pw_fwd("l0_norm", _f_norm, [(x0, 0)], [(nmix0, 0)], [BF16], 1024, 1)
    proj0 = _mm("l0_proj", h0, w_main0, "nn")
    dt_raw = _mm("l0_proj_dt", h0, w_dt0, "nn")
    conv = to2(_conv_fwd(to3(proj0), conv_w, conv_b, 3))
    (xbc,) = _pw_fwd("l0_silu", _f_silu, [(conv, 0)], [], [F32], 1024, 2)
    dt_bias = _pad_lanes(w["ssd_dt_bias"][0].reshape(1, 32))
    (dt,) = _pw_fwd("l0_dt", _f_softplus, [(dt_raw, 0)], [(dt_bias, 0)], [F32], 128, 1)
    dt3, xbc3 = to3(dt), to3(xbc)
    alog = _pad_lanes(w["ssd_a_log"][0].reshape(1, 32))
    ssd = [_ssd_fwd(xbc3, dt3, alog, r) for r in DIRS]
    yf, yb = to2(ssd[0][0]), to2(ssd[1][0])
    dskip = jnp.repeat(w["ssd_d"][0], SSD_HEADDIM).reshape(1, 1024)
    snw = row(w["ssd_norm_w"][0])
    ssd_ins = [(yf, 0), (yb, 0), (xbc, 0), (proj0, 12)]
    (ya,) = _pw_fwd("l0_ssd_post", _f_ssd_post, ssd_ins, [(dskip, 0), (snw, 0)], [BF16], 256, 4)
    u_lru = conv[:, 2048:]
    w_gates = [_block_diag(w[k][0, r]).astype(MXU_DTYPE) for r in range(2) for k in ("lru_w_a", "lru_w_x")]
    pre = [_mm(f"l0_lru_pre{i}", u_lru, wg, "nn") for i, wg in enumerate(w_gates)]
    lru_par = [[(row(w[k][0, r]), 0) for k in ("lru_b_a", "lru_b_x", "lru_lambda")] for r in range(2)]
    lru_ins = [[(pre[2 * r], 0), (pre[2 * r + 1], 0), (u_lru, 0)] for r in range(2)]
    ab = [_pw_fwd(f"l0_lru_gates{r}", _f_lru_gates, lru_ins[r], lru_par[r], [F32, F32], 1024, 1) for r in range(2)]
    hs = [_lru_scan(to3(ab[r][0]), to3(ab[r][1]), DIRS[r]) for r in range(2)]
    lru_post_ins = [(to2(hs[0]), 0), (to2(hs[1]), 0), (proj0, 4)]
    (ybm,) = _pw_fwd("l0_lru_post", _f_lru_post, lru_post_ins, [], [BF16], 1024, 1)
    w_out0 = w["even_w_out"][0]
    x1 = _mm("l0_out_a", ya, w_out0[:1024], "nn", res=x0)
    x1 = _mm("l0_out_b", ybm, w_out0[1024:], "nn", res=x1)
    nmlp0 = row(w["norm_mlp"][0])
    x2, mlp0 = _mlp_fwd("l0_mlp", x1, nmlp0, w["mlp_w1"][0], w["mlp_w2"][0])

    w_in1 = w["odd_w_in"][0]
    nmix1 = row(w["norm_mix"][1])
    (h1,) = _pw_fwd("l1_norm", _f_norm, [(x2, 0)], [(nmix1, 0)], [BF16], 1024, 1)
    proj1 = _mm("l1_proj", h1, w_in1, "nn")
    proj1_3 = to3(proj1)
    lb0, lb1 = row(w["hgrn_lb_logits"][0]), row(w["hgrn_lb_logits"][1])
    kg = [_pw_fwd(f"l1_hgrn_pre{r}", _f_hgrn_pre, [(proj1, 1 + r)], [(lb0, 0), (lb1, 0)], [F32, F32], 1024, 1)
          for r in range(2)]
    gla = [_gla_fwd(proj1_3, to3(kg[r][0]), to3(kg[r][1]), DIRS[r]) for r in range(2)]
    hnw = row(w["hgrn_norm_w"][0])
    hpost_ins = [(to2(gla[0][0]), 0), (to2(gla[1][0]), 0), (proj1, 32)]
    (yo,) = _pw_fwd("l1_hgrn_post", _f_hgrn_post, hpost_ins, [(hnw, 0)], [BF16], 128, 8)
    w_out1 = w["odd_w_out"][0]
    x3_ = _mm("l1_out", yo, w_out1, "nn", res=x2)
    nmlp1 = row(w["norm_mlp"][1])
    x4, mlp1 = _mlp_fwd("l1_mlp", x3_, nmlp1, w["mlp_w1"][1], w["mlp_w2"][1])

    dx4, dnf, loss = _loss_head(x4, tgt, row(w["norm_final"]))
    grads["norm_final"] = dnf.reshape(-1)

    dx3, dw1_1, dw2_1, dnmlp1 = _mlp_bwd("l1_mlp", x3_, nmlp1, w["mlp_w1"][1], w["mlp_w2"][1], mlp1, dx4)
    big = {"odd_w_out": _mm("l1_dwout", yo, dx3, "tn").reshape(4, 256, 1024)}
    dyo = _mm("l1_dyo", dx3, w_out1, "nt")
    (do, dgate1), (dhnw,) = _pw_bwd("l1_hgrn_post_b", _f_hgrn_post, hpost_ins, [(hnw, 0)], [dyo], 128, 8, [0, 2],
                                    out_dtypes=[F32, BF16])
    grads["hgrn_norm_w"] = dhnw
    do3 = to3(do)
    gb = [_gla_bwd(proj1_3, to3(kg[r][0]), to3(kg[r][1]), gla[r][1], do3, DIRS[r]) for r in range(2)]
    (dq,) = _pw_fwd("l1_dq", _f_add2, [(to2(gb[0][0]), 0), (to2(gb[1][0]), 0)], [], [BF16], 1024, 1)
    (dvv,) = _pw_fwd("l1_dv", _f_add2, [(to2(gb[0][2]), 0), (to2(gb[1][2]), 0)], [], [BF16], 1024, 1)
    dfr, dl0, dl1 = [], [], []
    for r in range(2):
        (df,), (a0, a1) = _pw_bwd(f"l1_hgrn_pre_b{r}", _f_hgrn_pre, [(proj1, 1 + r)], [(lb0, 0), (lb1, 0)],
                                  [to2(gb[r][1]), to2(gb[r][3])], 1024, 1, [0], out_dtypes=[BF16])
        dfr.append(df)
        dl0.append(a0)
        dl1.append(a1)
    grads["hgrn_lb_logits"] = jnp.concatenate([dl0[0] + dl0[1], dl1[0] + dl1[1]], axis=0)
    dparts1 = [dq, dfr[0], dfr[1], dvv, dgate1]
    dwin1 = jnp.concatenate([_mm(f"l1_dwin{i}", h1, dp, "tn") for i, dp in enumerate(dparts1)], axis=1)
    big["odd_w_in"] = dwin1.reshape(1024, 4, 1280).transpose(1, 0, 2)
    dh1 = None
    for i, dp in enumerate(dparts1):
        dh1 = _mm(f"l1_dh{i}", dp, w_in1[:, i * 1024:(i + 1) * 1024], "nt", res=dh1)
    (dx2,), (dnmix1,) = _pw_bwd("l1_dnorm", _f_norm, [(x2, 0)], [(nmix1, 0)], [dh1], 1024, 1, [0], adds={0: dx3})

    dx1, dw1_0, dw2_0, dnmlp0 = _mlp_bwd("l0_mlp", x1, nmlp0, w["mlp_w1"][0], w["mlp_w2"][0], mlp0, dx2)
    big["mlp_w1"] = jnp.concatenate([dw1_0, dw1_1], axis=1)
    big["mlp_w2"] = jnp.concatenate([dw2_0.reshape(4, 1024, 1024), dw2_1.reshape(4, 1024, 1024)], axis=1)
    grads["norm_mlp"] = jnp.concatenate([dnmlp0, dnmlp1], axis=0)
    big["even_w_out"] = jnp.concatenate([_mm("l0_dwout_a", ya, dx1, "tn"), _mm("l0_dwout_b", ybm, dx1, "tn")],
                                        axis=0).reshape(4, 512, 1024)
    dya = _mm("l0_dya", dx1, w_out0[:1024], "nt")
    dyb = _mm("l0_dyb", dx1, w_out0[1024:], "nt")
    (dh, dgate0), _ = _pw_bwd("l0_lru_post_b", _f_lru_post, lru_post_ins, [], [dyb], 1024, 1, [0, 2], out_dtypes=[F32, BF16])
    dh3 = to3(dh)
    dpre, du_parts, dlru = [], [], {k: [] for k in ("lru_b_a", "lru_b_x", "lru_lambda")}
    for r in range(2):
        g_r, da_r = _lru_scan_bwd(to3(ab[r][0]), hs[r], dh3, DIRS[r])
        (dpa, dpx, du_r), (dba, dbx, dlam) = _pw_bwd(f"l0_lru_gates_b{r}", _f_lru_gates, lru_ins[r], lru_par[r],
                                                     [to2(da_r), to2(g_r)], 1024, 1, [0, 1, 2],
                                                     out_dtypes=[BF16, BF16, F32])
        dpre += [dpa, dpx]
        du_parts.append(du_r)
        dlru["lru_b_a"].append(dba)
        dlru["lru_b_x"].append(dbx)
        dlru["lru_lambda"].append(dlam)
    for k, v in dlru.items():
        grads[k] = jnp.concatenate(v, axis=0)[None]
    dwg = [_diag_blocks(_mm(f"l0_dwgate{i}", u_lru, dp, "tn")) for i, dp in enumerate(dpre)]
    grads["lru_w_a"] = jnp.stack([dwg[0], dwg[2]])[None]
    grads["lru_w_x"] = jnp.stack([dwg[1], dwg[3]])[None]
    (du,) = _pw_fwd("l0_du", _f_add2, [(du_parts[0], 0), (du_parts[1], 0)], [], [F32], 1024, 1)
    for i, dp in enumerate(dpre):
        du = _mm(f"l0_du_gate{i}", dp, w_gates[i], "nt", res=du)
    (dy, dxs_skip, dz), (ddskip, dsnw) = _pw_bwd("l0_ssd_post_b", _f_ssd_post, ssd_ins, [(dskip, 0), (snw, 0)], [dya],
                                                 256, 4, [0, 2, 3], out_dtypes=[F32, F32, BF16])
    grads["ssd_d"] = ddskip.reshape(SSD_HEADS, SSD_HEADDIM).sum(axis=1)[None]
    grads["ssd_norm_w"] = dsnw
    dy3 = to3(dy)
    sb = [_ssd_bwd(xbc3, dt3, alog, ssd[r][1], dy3, DIRS[r]) for r in range(2)]
    grads["ssd_a_log"] = (sb[0][3] + sb[1][3])[:, :32].reshape(1, 2, 16)
    (dxs,) = _pw_fwd("l0_dxs", _f_add3, [(to2(sb[0][0]), 0), (to2(sb[1][0]), 0), (dxs_skip, 0)], [], [F32], 1024, 1)
    (dbc,) = _pw_fwd("l0_dbc", _f_add2, [(to2(sb[0][1]), 0), (to2(sb[1][1]), 0)], [], [F32], 1024, 1)
    dxbc = jnp.concatenate([dxs, dbc], axis=1)
    (dconv_a,), _ = _pw_bwd("l0_silu_b", _f_silu, [(conv, 0)], [], [dxbc], 1024, 2, [0])
    (ddt,) = _pw_fwd("l0_ddt", _f_add2, [(to2(sb[0][2]), 0), (to2(sb[1][2]), 0)], [], [F32], 128, 1)
    (ddt_raw,), (ddtb,) = _pw_bwd("l0_dt_b", _f_softplus, [(dt_raw, 0)], [(dt_bias, 0)], [ddt], 128, 1, [0])
    grads["ssd_dt_bias"] = ddtb[:, :32].reshape(1, 2, 16)
    dconv = jnp.concatenate([dconv_a, du], axis=1)
    dproj_c, dcw = _conv_bwd(to3(dconv), to3(proj0), conv_w, 3)
    grads["even_conv_w"] = dcw[:4][None]
    grads["even_conv_b"] = dcw[4:5]
    dparts0 = [to2(dproj_c)[:, :1024], to2(dproj_c)[:, 1024:2048], to2(dproj_c)[:, 2048:], dz, dgate0]
    dwin0 = [_mm(f"l0_dwin{i}", h0, dp, "tn") for i, dp in enumerate(dparts0)]
    big["even_w_in"] = _split_in0(dwin0, _mm("l0_dwin_dt", h0, ddt_raw, "tn"))
    dh0 = _mm("l0_dh_dt", ddt_raw, w_dt0, "nt")
    for i, dp in enumerate(dparts0):
        dh0 = _mm(f"l0_dh{i}", dp, w_main0[:, i * 1024:(i + 1) * 1024], "nt", res=dh0)
    (dx0,), (dnmix0,) = _pw_bwd("l0_dnorm", _f_norm, [(x0, 0)], [(nmix0, 0)], [dh0], 1024, 1, [0], adds={0: dx1})
    grads["norm_mix"] = jnp.concatenate([dnmix0, dnmix1], axis=0)
    return loss, dx0.reshape(nb, s, d), grads, [big[n] for n in BIG]


ANY = pl.BlockSpec(memory_space=pl.ANY)


def _place():
    return lax.axis_index("x"), lax.axis_index("y"), lax.axis_index("c")


def _remote(src, dst, send_sems, recv_sems, k, to):
    return pltpu.make_async_remote_copy(src_ref=src, dst_ref=dst, send_sem=send_sems.at[k], recv_sem=recv_sems.at[k],
                                        device_id=to, device_id_type=MESH)


def _gather_chips(shards):
    n = len(shards)
    halves = [s.shape[0] // 2 for s in shards]

    def body(*refs):
        x_refs, out_refs = refs[:n], refs[n:2 * n]
        send_sems, recv_sems = refs[2 * n:]
        x, y, c = _place()
        sibling = (x, y, 1 - c)
        chips = [(1 - x, y), (x, 1 - y), (1 - x, 1 - y)]

        def blk(t, px, py, hc):
            return out_refs[t].at[2 * px + py, pl.ds(hc * halves[t], halves[t]), :]

        def src(t):
            return x_refs[t].at[pl.ds(c * halves[t], halves[t]), :]

        first = [_remote(src(t), blk(t, x, y, c), send_sems, recv_sems, 6 * t + j, (*chip, c))
                 for t in range(n) for j, chip in enumerate(chips)]
        for cp in first:
            cp.start()
        passed = []
        for t in range(n):
            for j, chip in enumerate(chips):
                _remote(src(t), blk(t, *chip, c), send_sems, recv_sems, 6 * t + j, (*chip, c)).wait_recv()
                cp = _remote(blk(t, *chip, c), blk(t, *chip, c), send_sems, recv_sems, 6 * t + 3 + j, sibling)
                cp.start()
                passed.append(cp)
        for t in range(n):
            for j, chip in enumerate(chips):
                _remote(src(t), blk(t, *chip, 1 - c), send_sems, recv_sems, 6 * t + 3 + j, sibling).wait_recv()
        for cp in first + passed:
            cp.wait_send()

    return _pcall(body, name="gather_weights", in_specs=[ANY] * n, out_specs=(ANY,) * n,
                  out_shape=tuple(jax.ShapeDtypeStruct((4,) + s.shape, s.dtype) for s in shards),
                  scratch_shapes=[pltpu.SemaphoreType.DMA((6 * n,)), pltpu.SemaphoreType.DMA((6 * n,))],
                  compiler_params=_params())(*shards)


def _pair_swap(gps):
    n = len(gps)
    halves = [g.shape[1] // 2 for g in gps]

    def body(*refs):
        g_refs, land_refs = refs[:n], refs[n:2 * n]
        send_sems, recv_sems = refs[2 * n:]
        x, y, c = _place()
        cps = [_remote(g_refs[t].at[j, pl.ds((1 - c) * halves[t], halves[t]), :], land_refs[t].at[j], send_sems, recv_sems,
                       4 * t + j, (x, y, 1 - c)) for t in range(n) for j in range(4)]
        for cp in cps:
            cp.start()
        for cp in cps:
            cp.wait()

    return _pcall(body, name="grad_pair_swap", in_specs=[ANY] * n, out_specs=(ANY,) * n,
                  out_shape=tuple(jax.ShapeDtypeStruct((4, h, g.shape[2]), F32) for g, h in zip(gps, halves)),
                  scratch_shapes=[pltpu.SemaphoreType.DMA((4 * n,)), pltpu.SemaphoreType.DMA((4 * n,))],
                  compiler_params=_params())(*gps)


def _pair_add(name, gp, land, cidx):
    _, half, cols = land.shape
    tr = _tile(half, 512)
    nh = half // tr

    def body(c_ref, g_ref, l_ref, o_ref):
        o_ref[...] = (g_ref[...] + l_ref[...]).astype(o_ref.dtype)

    grid_spec = pltpu.PrefetchScalarGridSpec(
        num_scalar_prefetch=1, grid=(4, nh),
        in_specs=[pl.BlockSpec((None, tr, cols), lambda j, i, c: (j, c[0] * nh + i, 0)),
                  pl.BlockSpec((None, tr, cols), lambda j, i, c: (j, i, 0))],
        out_specs=pl.BlockSpec((None, tr, cols), lambda j, i, c: (j, i, 0)))
    return _pcall(body, name=f"pair_add_{name}", grid_spec=grid_spec, out_shape=jax.ShapeDtypeStruct((4, half, cols), BF16),
                  compiler_params=_params())(cidx, gp, land)


def _chip_scatter(css):
    n = len(css)

    def body(*refs):
        s_refs, land_refs = refs[:n], refs[n:2 * n]
        send_sems, recv_sems = refs[2 * n:]
        x, y, c = _place()
        me = 2 * x + y
        chips = [(1 - x, y), (x, 1 - y), (1 - x, 1 - y)]
        cps = [_remote(s_refs[t].at[2 * px + py], land_refs[t].at[me], send_sems, recv_sems, 3 * t + j, (px, py, c))
               for t in range(n) for j, (px, py) in enumerate(chips)]
        for cp in cps:
            cp.start()
        for t in range(n):
            for j, (px, py) in enumerate(chips):
                _remote(s_refs[t].at[me], land_refs[t].at[2 * px + py], send_sems, recv_sems, 3 * t + j, (px, py, c)).wait_recv()
        for cp in cps:
            cp.wait_send()

    return _pcall(body, name="grad_chip_scatter", in_specs=[ANY] * n, out_specs=(ANY,) * n,
                  out_shape=tuple(jax.ShapeDtypeStruct(s.shape, s.dtype) for s in css),
                  scratch_shapes=[pltpu.SemaphoreType.DMA((3 * n,)), pltpu.SemaphoreType.DMA((3 * n,))],
                  compiler_params=_params())(*css)


def _chip_sum(name, land):
    _, half, cols = land.shape
    tr = _tile(half, 512)

    def body(l_ref, o_ref):
        o_ref[...] = ((l_ref[0].astype(F32) + l_ref[1].astype(F32)) + l_ref[2].astype(F32)) + l_ref[3].astype(F32)

    return _pcall(body, name=f"chip_sum_{name}", grid=(half // tr,),
                  in_specs=[pl.BlockSpec((4, tr, cols), lambda i: (0, i, 0))],
                  out_specs=pl.BlockSpec((tr, cols), lambda i: (i, 0)),
                  out_shape=jax.ShapeDtypeStruct((half, cols), F32), compiler_params=_params())(land)


def _pair_join(reds):
    n = len(reds)

    def body(*refs):
        r_refs, out_refs = refs[:n], refs[n:2 * n]
        send_sems, recv_sems = refs[2 * n:]
        x, y, c = _place()
        cps = [_remote(r_refs[t], out_refs[t].at[c], send_sems, recv_sems, t, (x, y, 1 - c)) for t in range(n)]
        for cp in cps:
            cp.start()
        for t in range(n):
            _remote(r_refs[t], out_refs[t].at[1 - c], send_sems, recv_sems, t, (x, y, 1 - c)).wait_recv()
        for cp in cps:
            cp.wait_send()

    return _pcall(body, name="grad_pair_join", in_specs=[ANY] * n, out_specs=(ANY,) * n,
                  out_shape=tuple(jax.ShapeDtypeStruct((2,) + r.shape, F32) for r in reds),
                  scratch_shapes=[pltpu.SemaphoreType.DMA((n,)), pltpu.SemaphoreType.DMA((n,))],
                  compiler_params=_params())(*reds)


def _adamw(name, g, w, m, v):
    rows, cols = g.shape
    tr = _tile(rows, 512)

    def body(g_ref, w_ref, m_ref, v_ref, d_ref, mo_ref, vo_ref):
        gv = g_ref[...]
        mn = ADAM_B1 * m_ref[...] + (1.0 - ADAM_B1) * gv
        vn = ADAM_B2 * v_ref[...] + (1.0 - ADAM_B2) * jnp.square(gv)
        m_hat = mn / (1.0 - ADAM_B1 ** ADAM_STEP)
        v_hat = vn / (1.0 - ADAM_B2 ** ADAM_STEP)
        d_ref[...] = -ADAM_LR * (m_hat / (jnp.sqrt(v_hat) + ADAM_EPS) + ADAM_WD * w_ref[...])
        mo_ref[...] = mn
        vo_ref[...] = vn

    blk = pl.BlockSpec((tr, cols), lambda i: (i, 0))
    shp = jax.ShapeDtypeStruct((rows, cols), F32)
    return _pcall(body, name=f"adamw_{name}", grid=(rows // tr,), in_specs=[blk] * 4, out_specs=(blk,) * 3,
                  out_shape=(shp,) * 3, compiler_params=_params())(g, w, m, v)


def _pack(pieces, rows, dtype):
    flat = jnp.concatenate([p.reshape(-1).astype(dtype) for p in pieces])
    return jnp.pad(flat, (0, rows * PACK_COLS - flat.shape[0])).reshape(rows, PACK_COLS)


def _unpack(pack, shapes):
    flat = pack.reshape(-1)
    out, off = [], 0
    for shp in shapes:
        n = math.prod(shp)
        out.append(flat[off:off + n].reshape(shp))
        off += n
    return out


def _shard_of(full, axis, j):
    n = full.shape[axis] // 4
    return lax.slice_in_dim(full, j * n, (j + 1) * n, axis=axis)


def kernel(x, even_w_in, even_conv_w, even_conv_b, ssd_a_log, ssd_dt_bias, ssd_d, ssd_norm_w, lru_w_a, lru_b_a, lru_w_x, lru_b_x, lru_lambda, even_w_out, odd_w_in, hgrn_lb_logits, hgrn_norm_w, odd_w_out, norm_mix, norm_mlp, mlp_w1, mlp_w2, norm_final, loss_target, m_even_w_in, m_even_conv_w, m_even_conv_b, m_ssd_a_log, m_ssd_dt_bias, m_ssd_d, m_ssd_norm_w, m_lru_w_a, m_lru_b_a, m_lru_w_x, m_lru_b_x, m_lru_lambda, m_even_w_out, m_odd_w_in, m_hgrn_lb_logits, m_hgrn_norm_w, m_odd_w_out, m_norm_mix, m_norm_mlp, m_mlp_w1, m_mlp_w2, m_norm_final, v_even_w_in, v_even_conv_w, v_even_conv_b, v_ssd_a_log, v_ssd_dt_bias, v_ssd_d, v_ssd_norm_w, v_lru_w_a, v_lru_b_a, v_lru_w_x, v_lru_b_x, v_lru_lambda, v_even_w_out, v_odd_w_in, v_hgrn_lb_logits, v_hgrn_norm_w, v_odd_w_out, v_norm_mix, v_norm_mlp, v_mlp_w1, v_mlp_w2, v_norm_final):
    names = [n for n, _, _, _ in WEIGHTS]
    w_loc = dict(zip(names, (even_w_in, even_conv_w, even_conv_b, ssd_a_log, ssd_dt_bias, ssd_d, ssd_norm_w, lru_w_a, lru_b_a, lru_w_x, lru_b_x, lru_lambda, even_w_out, odd_w_in, hgrn_lb_logits, hgrn_norm_w, odd_w_out, norm_mix, norm_mlp, mlp_w1, mlp_w2, norm_final)))
    m_loc = dict(zip(names, (m_even_w_in, m_even_conv_w, m_even_conv_b, m_ssd_a_log, m_ssd_dt_bias, m_ssd_d, m_ssd_norm_w, m_lru_w_a, m_lru_b_a, m_lru_w_x, m_lru_b_x, m_lru_lambda, m_even_w_out, m_odd_w_in, m_hgrn_lb_logits, m_hgrn_norm_w, m_odd_w_out, m_norm_mix, m_norm_mlp, m_mlp_w1, m_mlp_w2, m_norm_final)))
    v_loc = dict(zip(names, (v_even_w_in, v_even_conv_w, v_even_conv_b, v_ssd_a_log, v_ssd_dt_bias, v_ssd_d, v_ssd_norm_w, v_lru_w_a, v_lru_b_a, v_lru_w_x, v_lru_b_x, v_lru_lambda, v_even_w_out, v_odd_w_in, v_hgrn_lb_logits, v_hgrn_norm_w, v_odd_w_out, v_norm_mix, v_norm_mlp, v_mlp_w1, v_mlp_w2, v_norm_final)))
    spec = {n: (blk, full, ax) for n, blk, full, ax in WEIGHTS}

    small = [n for n in names if n not in BIG]
    two_d = lambda n, v: v.reshape(BIG_2D[n])

    me = 2 * lax.axis_index("x") + lax.axis_index("y")
    cc = lax.axis_index("c")
    put = lambda whole, part, k: lax.dynamic_update_slice_in_dim(whole, part[None], k, axis=0)
    own = [two_d(n, w_loc[n]).astype(BF16) for n in BIG] + [_pack([w_loc[n] for n in SMALL_SHARDED], 16, F32)]
    g_in0, g_out0, g_in1, g_out1, g_w1, g_w2, g_small = [put(g, o, me) for g, o in zip(_gather_chips(own), own)]
    w_main0, w_dt0 = _assemble_in0(g_in0)
    w_full = {n: w_loc[n] for n in names if spec[n][2] is None}
    w_full["even_w_out"] = g_out0.reshape(1, 2048, 1024)
    w_full["odd_w_in"] = jnp.concatenate([g_in1[j] for j in range(4)], axis=1)[None]
    w_full["odd_w_out"] = g_out1.reshape(1, 1024, 1024)
    w_full["mlp_w1"] = jnp.stack([jnp.concatenate([g_w1[j, l * 1024:(l + 1) * 1024] for j in range(4)], axis=1) for l in range(2)])
    w_full["mlp_w2"] = jnp.stack([jnp.concatenate([g_w2[j, l * 1024:(l + 1) * 1024] for j in range(4)], axis=0) for l in range(2)])
    shards = [_unpack(g_small[j], [spec[n][0] for n in SMALL_SHARDED]) for j in range(4)]
    for i, n in enumerate(SMALL_SHARDED):
        w_full[n] = jnp.concatenate([shards[j][i] for j in range(4)], axis=spec[n][2])

    loss_vec, grad_x, grads, big = _local_step(x, loss_target, w_full, w_main0, w_dt0)
    loss = lax.psum(loss_vec[0, 0], ("x", "y", "c"))

    def dest_pack(j):
        return _pack([grads[n].reshape(spec[n][1]) if spec[n][2] is None else _shard_of(grads[n].reshape(spec[n][1]), spec[n][2], j)
                      for n in small], SMALL_ROWS, F32)

    tensors = big + [jnp.stack([dest_pack(j) for j in range(4)])]
    tags = list(BIG) + ["small"]
    cidx = cc.astype(jnp.int32).reshape(1)
    chip_sums = [_pair_add(tag, g, land, cidx) for tag, g, land in zip(tags, tensors, _pair_swap(tensors))]
    landed = [put(land, lax.dynamic_index_in_dim(cs, me, axis=0, keepdims=False), me)
              for land, cs in zip(_chip_scatter(chip_sums), chip_sums)]
    halves = [_chip_sum(tag, land) for tag, land in zip(tags, landed)]
    reduced = [put(r, h, cc).reshape(-1, r.shape[-1]) for r, h in zip(_pair_join(halves), halves)]

    outs = {}
    for n, g in zip(BIG, reduced[:-1]):
        res = (g, *_adamw(n, g, two_d(n, w_loc[n]), two_d(n, m_loc[n]), two_d(n, v_loc[n])))
        outs[n] = [r.reshape(spec[n][0]) for r in res]
    blocks = [spec[n][0] for n in small]
    wp, mp, vp = (_pack([src[n] for n in small], SMALL_ROWS, F32) for src in (w_loc, m_loc, v_loc))
    res = (reduced[-1], *_adamw("small", reduced[-1], wp, mp, vp))
    unpacked = [_unpack(r, blocks) for r in res]
    for i, n in enumerate(small):
        outs[n] = [u[i] for u in unpacked]
    return (loss, grad_x, *[outs[n][k] for k in range(4) for n in names])
```

```python
import functools
import math

import jax
import jax.numpy as jnp
from jax import lax
from jax.experimental import pallas as pl
from jax.experimental.pallas import tpu as pltpu

F32 = jnp.float32
BF16 = jnp.bfloat16
MXU_DTYPE = jnp.bfloat16
HI = lax.Precision.HIGHEST
MESH = pl.DeviceIdType.MESH

D_MODEL = 1024
EPS = 1e-6
SSD_HEADS = 16
SSD_HEADDIM = 64
HEAD_SHIFT = 6
SSD_GROUPS = 4
SSD_STATE = 128
SSD_CHUNK = 128
LRU_C = 8.0
LRU_ROWS = 256
HGRN_HEADS = 8
HGRN_HEADDIM = 128
HGRN_SUB = 32
HGRN_BLOCK = 128
HGRN_SCALE = HGRN_HEADDIM ** -0.5
CONV_ROWS = 512

ADAM_LR = 0.001
ADAM_B1 = 0.9
ADAM_B2 = 0.999
ADAM_EPS = 1e-08
ADAM_WD = 0.01
ADAM_STEP = 10

VMEM_LIMIT = 56 * 1024 * 1024
PACK_COLS = 1024
SMALL_ROWS = 288

WEIGHTS = (
    ("even_w_in", (1, 1024, 1288), (1, 1024, 5152), 2),
    ("even_conv_w", (1, 4, 768), (1, 4, 3072), 2),
    ("even_conv_b", (1, 3072), (1, 3072), None),
    ("ssd_a_log", (1, 2, 16), (1, 2, 16), None),
    ("ssd_dt_bias", (1, 2, 16), (1, 2, 16), None),
    ("ssd_d", (1, 16), (1, 16), None),
    ("ssd_norm_w", (1, 1024), (1, 1024), None),
    ("lru_w_a", (1, 2, 16, 64, 64), (1, 2, 16, 64, 64), None),
    ("lru_b_a", (1, 2, 256), (1, 2, 1024), 2),
    ("lru_w_x", (1, 2, 16, 64, 64), (1, 2, 16, 64, 64), None),
    ("lru_b_x", (1, 2, 256), (1, 2, 1024), 2),
    ("lru_lambda", (1, 2, 256), (1, 2, 1024), 2),
    ("even_w_out", (1, 512, 1024), (1, 2048, 1024), 1),
    ("odd_w_in", (1, 1024, 1280), (1, 1024, 5120), 2),
    ("hgrn_lb_logits", (2, 1024), (2, 1024), None),
    ("hgrn_norm_w", (1, 256), (1, 1024), 1),
    ("odd_w_out", (1, 256, 1024), (1, 1024, 1024), 1),
    ("norm_mix", (2, 1024), (2, 1024), None),
    ("norm_mlp", (2, 1024), (2, 1024), None),
    ("mlp_w1", (2, 1024, 1024), (2, 1024, 4096), 2),
    ("mlp_w2", (2, 1024, 1024), (2, 4096, 1024), 1),
    ("norm_final", (1024,), (1024,), None),
)
BIG = ("even_w_in", "even_w_out", "odd_w_in", "odd_w_out", "mlp_w1", "mlp_w2")
BIG_2D = {"even_w_in": (1024, 1288), "even_w_out": (512, 1024), "odd_w_in": (1024, 1280), "odd_w_out": (256, 1024),
          "mlp_w1": (2048, 1024), "mlp_w2": (2048, 1024)}
SMALL_SHARDED = ("even_conv_w", "lru_b_a", "lru_b_x", "lru_lambda", "hgrn_norm_w")


def _pcall(body, **kw):
    return pl.pallas_call(body, **kw)


def _params(**kw):
    return pltpu.CompilerParams(vmem_limit_bytes=VMEM_LIMIT, **kw)


def _tile(n, pref):
    if n <= pref:
        return n
    t = (pref // 128) * 128
    while n % t:
        t -= 128
    return t


def _dot(a, b, dims=(((1,), (0,)), ((), ())), precision=None):
    return lax.dot_general(a, b, dims, preferred_element_type=F32, precision=precision)


_NN = (((1,), (0,)), ((), ()))
_NT = (((1,), (1,)), ((), ()))
_TN = (((0,), (0,)), ((), ()))


def _mx(v):
    return v.astype(MXU_DTYPE)


def _mm(name, a, b, mode, *, out_dtype=F32, res=None, relu2=False, relu2_of=None, col_shards=1):
    if mode == "nn":
        (m, kk), (_, n) = a.shape, b.shape
    elif mode == "nt":
        (m, kk), (n, _) = a.shape, b.shape
    else:
        (kk, m), (_, n) = a.shape, b.shape
    assert res is None or relu2_of is None
    tk_pref = 1024
    if mode == "tn" and a.dtype.itemsize == 2 and b.dtype.itemsize == 2:
        tk_pref = 2048
    tm, tn, tk = _tile(m, 1024), _tile(n // col_shards, 1024), _tile(kk, tk_pref)
    nk = kk // tk
    dims = {"nn": _NN, "nt": _NT, "tn": _TN}[mode]
    a_spec = pl.BlockSpec((tk, tm), lambda i, j, k: (k, i)) if mode == "tn" else pl.BlockSpec((tm, tk), lambda i, j, k: (i, k))
    b_spec = pl.BlockSpec((tn, tk), lambda i, j, k: (j, k)) if mode == "nt" else pl.BlockSpec((tk, tn), lambda i, j, k: (k, j))
    o_spec = pl.BlockSpec((tm, tn), lambda i, j, k: (i, j))
    o_shape = (m, n)
    if col_shards > 1:
        assert tn * col_shards == n and res is None and not relu2
        o_spec = pl.BlockSpec((None, tm, tn), lambda i, j, k: (j, i, 0))
        o_shape = (col_shards, m, tn)
    extra = res if res is not None else relu2_of
    has_res = extra is not None

    def body(*refs):
        a_ref, b_ref = refs[0], refs[1]
        res_ref = refs[2] if has_res else None
        outs = refs[2 + has_res:2 + has_res + 1 + relu2]

        def finish(r):
            if res is not None:
                r = r + res_ref[...]
            if relu2_of is not None:
                r = r * (2.0 * jnp.maximum(res_ref[...], 0.0))
            if relu2:
                outs[0][...] = r
                outs[1][...] = jnp.square(jnp.maximum(r, 0.0)).astype(outs[1].dtype)
            else:
                outs[0][...] = r.astype(outs[0].dtype)

        prod = _dot(_mx(a_ref[...]), _mx(b_ref[...]), dims)
        if nk == 1:
            finish(prod)
            return
        acc = refs[-1]
        k = pl.program_id(2)

        @pl.when(k == 0)
        def _():
            acc[...] = prod

        @pl.when(k > 0)
        def _():
            acc[...] += prod

        @pl.when(k == nk - 1)
        def _():
            finish(acc[...])

    in_specs = [a_spec, b_spec] + ([o_spec] if has_res else [])
    if relu2:
        out_shape = (jax.ShapeDtypeStruct((m, n), F32), jax.ShapeDtypeStruct((m, n), BF16))
        out_specs = (o_spec, o_spec)
    else:
        out_shape = jax.ShapeDtypeStruct(o_shape, out_dtype)
        out_specs = o_spec
    args = (a, b) + ((extra,) if has_res else ())
    return _pcall(body, name=name, grid=(m // tm, n // tn, nk), in_specs=in_specs, out_specs=out_specs,
                  out_shape=out_shape, scratch_shapes=[pltpu.VMEM((tm, tn), F32)] if nk > 1 else [],
                  compiler_params=_params())(*args)


def _mm_sum_nt(name, parts, wblocks):
    m, n, npart = parts[0].shape[0], wblocks[0].shape[0], len(parts)
    tm, tn = _tile(m, 512), _tile(n, 1024)

    def body(*refs):
        acc = _dot(_mx(refs[0][...]), _mx(refs[npart][...]), _NT)
        for k in range(1, npart):
            acc = acc + _dot(_mx(refs[k][...]), _mx(refs[npart + k][...]), _NT)
        refs[-1][...] = acc

    in_specs = [pl.BlockSpec((tm, p.shape[1]), lambda i, j: (i, 0)) for p in parts]
    in_specs += [pl.BlockSpec((tn, w.shape[1]), lambda i, j: (j, 0)) for w in wblocks]
    return _pcall(body, name=name, grid=(m // tm, n // tn), in_specs=in_specs, out_specs=pl.BlockSpec((tm, tn), lambda i, j: (i, j)),
                  out_shape=jax.ShapeDtypeStruct((m, n), F32), compiler_params=_params())(*parts, *wblocks)


def _pw_fwd(name, f, ins, params, out_dtypes, tc, ncol, tm=256, groups=1):
    t = ins[0][0].shape[0]
    tm = min(tm, t)
    ni, npar = len(ins), len(params)
    gw = tc // groups

    def body(*refs):
        for g in range(groups):
            sl = slice(g * gw, (g + 1) * gw)
            vals = f(*[r[:, sl].astype(F32) for r in refs[:ni]], *[r[:, sl] for r in refs[ni:ni + npar]])
            for o, v in zip(refs[ni + npar:], vals):
                o[:, sl] = v.astype(o.dtype)

    in_specs = [pl.BlockSpec((tm, tc), lambda j, i, off=off: (i, off + j)) for _, off in ins]
    in_specs += [pl.BlockSpec((1, tc), lambda j, i, off=off: (0, off + j)) for _, off in params]
    out_specs = tuple(pl.BlockSpec((tm, tc), lambda j, i: (i, j)) for _ in out_dtypes)
    out_shape = tuple(jax.ShapeDtypeStruct((t, ncol * tc), d) for d in out_dtypes)
    return _pcall(body, name=name, grid=(ncol, t // tm), in_specs=in_specs, out_specs=out_specs, out_shape=out_shape,
                  compiler_params=_params())(*[a for a, _ in ins], *[p for p, _ in params])


def _pw_bwd(name, f, ins, params, douts, tc, ncol, want, adds=None, tm=256, out_dtypes=None, groups=1):
    adds = adds or {}
    out_dtypes = out_dtypes or [F32] * len(want)
    t = ins[0][0].shape[0]
    tm = min(tm, t)
    ni, npar, nd, na = len(ins), len(params), len(douts), len(adds)
    add_keys = sorted(adds)
    gw = tc // groups

    def body(*refs):
        in_refs, p_refs = refs[:ni], refs[ni:ni + npar]
        d_refs = refs[ni + npar:ni + npar + nd]
        a_refs = refs[ni + npar + nd:ni + npar + nd + na]
        o_refs = refs[ni + npar + nd + na:]
        for p in range(npar):
            @pl.when(pl.program_id(1) == 0)
            def _(o=o_refs[len(want) + p]):
                o[...] = jnp.zeros_like(o)

        for g in range(groups):
            sl = slice(g * gw, (g + 1) * gw)
            _, vjp = jax.vjp(f, *[r[:, sl].astype(F32) for r in in_refs], *[r[:, sl] for r in p_refs])
            cts = vjp(tuple(d[:, sl].astype(F32) for d in d_refs))
            for o, kidx in zip(o_refs[:len(want)], want):
                v = cts[kidx]
                if kidx in adds:
                    v = v + a_refs[add_keys.index(kidx)][:, sl]
                o[:, sl] = v.astype(o.dtype)
            for p in range(npar):
                o_refs[len(want) + p][:, sl] += cts[ni + p]

    in_specs = [pl.BlockSpec((tm, tc), lambda j, i, off=off: (i, off + j)) for _, off in ins]
    in_specs += [pl.BlockSpec((1, tc), lambda j, i, off=off: (0, off + j)) for _, off in params]
    in_specs += [pl.BlockSpec((tm, tc), lambda j, i: (i, j)) for _ in range(nd + na)]
    out_specs = tuple([pl.BlockSpec((tm, tc), lambda j, i: (i, j)) for _ in want]
                      + [pl.BlockSpec((1, tc), lambda j, i: (0, j)) for _ in params])
    out_shape = tuple([jax.ShapeDtypeStruct((t, ncol * tc), dt) for dt in out_dtypes]
                      + [jax.ShapeDtypeStruct((1, ncol * tc), F32) for _ in params])
    res = _pcall(body, name=name, grid=(ncol, t // tm), in_specs=in_specs, out_specs=out_specs, out_shape=out_shape,
                 compiler_params=_params())(*[a for a, _ in ins], *[p for p, _ in params], *douts, *[adds[k] for k in add_keys])
    return list(res[:len(want)]), list(res[len(want):])


def _rms(x, g):
    return (x * lax.rsqrt(jnp.mean(x * x, axis=-1, keepdims=True) + EPS)) * g


def _f_norm(x, g):
    return (_rms(x, g),)


def _f_silu(c):
    return (jax.nn.silu(c),)


def _f_softplus(d, b):
    return (jax.nn.softplus(d + b),)


def _f_add2(a, b):
    return (a + b,)


def _f_add3(a, b, c):
    return (a + b + c,)


def _f_ssd_post(yf, yb, xs, z, dskip, nw):
    u = (yf + yb + dskip * xs) * jax.nn.silu(z)
    return (_rms(u, nw),)


def _neg_expm1(v):
    t = jnp.tanh(0.5 * v)
    return -2.0 * t / (1.0 - t)


def _f_lru_gates(pre_a, pre_x, u, ba, bx, lam):
    rg = jax.nn.sigmoid(pre_a + ba)
    ig = jax.nn.sigmoid(pre_x + bx)
    log_a = -LRU_C * rg * jax.nn.softplus(-lam)
    return jnp.exp(log_a), jnp.sqrt(_neg_expm1(2.0 * log_a)) * (ig * u)


def _f_lru_post(hf, hb, gate):
    return ((hf + hb) * jax.nn.gelu(gate),)


def _f_hgrn_pre(fr, l0, l1):
    lb = jax.nn.sigmoid(l1 - l0)
    k = (1.0 - lb) * jax.nn.sigmoid(-fr)
    return k, jnp.log1p(-k)


def _f_hgrn_post(of, ob, gate, nw):
    return (_rms(of + ob, nw) * jax.nn.silu(gate),)


def _loss_head(x, tgt, g, tm=256):
    t, d = x.shape
    tm = min(tm, t)

    def body(x_ref, t_ref, g_ref, dx_ref, dg_ref, loss_ref):
        tv = t_ref[...]

        def lf(xv, gv):
            return 0.5 * jnp.sum(jnp.mean(jnp.square(_rms(xv, gv) - tv), axis=-1))

        val, vjp = jax.vjp(lf, x_ref[...], g_ref[...])
        dx, dg = vjp(jnp.ones((), F32))
        dx_ref[...] = dx

        @pl.when(pl.program_id(0) == 0)
        def _():
            dg_ref[...] = jnp.zeros_like(dg_ref)
            loss_ref[...] = jnp.zeros_like(loss_ref)

        dg_ref[...] += dg
        loss_ref[...] += jnp.full(loss_ref.shape, val, F32)

    row = pl.BlockSpec((tm, d), lambda i: (i, 0))
    vec = pl.BlockSpec((1, d), lambda i: (0, 0))
    return _pcall(body, name="loss_head", grid=(t // tm,), in_specs=[row, row, vec],
                  out_specs=(row, vec, pl.BlockSpec((1, 128), lambda i: (0, 0))),
                  out_shape=(jax.ShapeDtypeStruct((t, d), F32), jax.ShapeDtypeStruct((1, d), F32),
                             jax.ShapeDtypeStruct((1, 128), F32)), compiler_params=_params())(x, tgt, g)


def _shifted(x, d, prev, nxt, first, last):
    r = x.shape[0]
    row = lax.broadcasted_iota(jnp.int32, x.shape, 0)
    if d < 0:
        out = pltpu.roll(x, -d, 0)
        for q in range(-d):
            pv = jnp.where(first, 0.0, prev[8 + d + q:8 + d + q + 1, :])
            out = jnp.where(row == q, pv, out)
        return out
    out = pltpu.roll(x, r - d, 0)
    for q in range(d):
        nv = jnp.where(last, 0.0, nxt[q:q + 1, :])
        out = jnp.where(row == r - d + q, nv, out)
    return out


def _halo_specs(ts, tc, s):
    nb8 = s // 8
    cur = pl.BlockSpec((None, ts, tc), lambda n, i, j: (n, i, j))
    prev = pl.BlockSpec((None, 8, tc), lambda n, i, j: (n, jnp.maximum(i * (ts // 8) - 1, 0), j))
    nxt = pl.BlockSpec((None, 8, tc), lambda n, i, j: (n, jnp.minimum((i + 1) * (ts // 8), nb8 - 1), j))
    return cur, prev, nxt


def _conv_fwd(p3, w, b, ncol, tc=1024):
    nbatch, s, _ = p3.shape
    ts = min(CONV_ROWS, s)
    nblk = s // ts

    def body(x_ref, pv_ref, nx_ref, w_ref, b_ref, o_ref):
        i = pl.program_id(1)
        first, last = i == 0, i == nblk - 1
        x, pv, nx = x_ref[...], pv_ref[...], nx_ref[...]
        wv = w_ref[...]
        out = b_ref[...] + wv[1:2] * x
        out = out + wv[0:1] * _shifted(x, -1, pv, nx, first, last)
        out = out + wv[2:3] * _shifted(x, 1, pv, nx, first, last)
        out = out + wv[3:4] * _shifted(x, 2, pv, nx, first, last)
        o_ref[...] = out

    cur, prev, nxt = _halo_specs(ts, tc, s)
    return _pcall(body, name="conv_fwd", grid=(nbatch, nblk, ncol),
                  in_specs=[cur, prev, nxt, pl.BlockSpec((4, tc), lambda n, i, j: (0, j)),
                            pl.BlockSpec((1, tc), lambda n, i, j: (0, j))],
                  out_specs=cur, out_shape=jax.ShapeDtypeStruct((nbatch, s, ncol * tc), F32),
                  compiler_params=_params())(p3, p3, p3, w, b)


def _conv_bwd(dc3, p3, w, ncol, tc=1024):
    nbatch, s, _ = dc3.shape
    ts = min(CONV_ROWS, s)
    nblk = s // ts

    def body(d_ref, dpv_ref, dnx_ref, x_ref, pv_ref, nx_ref, w_ref, dx_ref, dw_ref):
        n, i = pl.program_id(1), pl.program_id(2)
        first, last = i == 0, i == nblk - 1
        d, dpv, dnx = d_ref[...], dpv_ref[...], dnx_ref[...]
        x, pv, nx = x_ref[...], pv_ref[...], nx_ref[...]
        wv = w_ref[...]
        dx = wv[1:2] * d
        dx = dx + wv[0:1] * _shifted(d, 1, dpv, dnx, first, last)
        dx = dx + wv[2:3] * _shifted(d, -1, dpv, dnx, first, last)
        dx = dx + wv[3:4] * _shifted(d, -2, dpv, dnx, first, last)
        dx_ref[...] = dx.astype(dx_ref.dtype)

        @pl.when((n == 0) & (i == 0))
        def _():
            dw_ref[...] = jnp.zeros_like(dw_ref)

        dw_ref[0:1, :] += jnp.sum(d * _shifted(x, -1, pv, nx, first, last), axis=0, keepdims=True)
        dw_ref[1:2, :] += jnp.sum(d * x, axis=0, keepdims=True)
        dw_ref[2:3, :] += jnp.sum(d * _shifted(x, 1, pv, nx, first, last), axis=0, keepdims=True)
        dw_ref[3:4, :] += jnp.sum(d * _shifted(x, 2, pv, nx, first, last), axis=0, keepdims=True)
        dw_ref[4:5, :] += jnp.sum(d, axis=0, keepdims=True)

    nb8 = s // 8
    cur = pl.BlockSpec((None, ts, tc), lambda j, n, i: (n, i, j))
    prev = pl.BlockSpec((None, 8, tc), lambda j, n, i: (n, jnp.maximum(i * (ts // 8) - 1, 0), j))
    nxt = pl.BlockSpec((None, 8, tc), lambda j, n, i: (n, jnp.minimum((i + 1) * (ts // 8), nb8 - 1), j))
    return _pcall(body, name="conv_bwd", grid=(ncol, nbatch, nblk),
                  in_specs=[cur, prev, nxt, cur, prev, nxt, pl.BlockSpec((4, tc), lambda j, n, i: (0, j))],
                  out_specs=(cur, pl.BlockSpec((8, tc), lambda j, n, i: (0, j))),
                  out_shape=(jax.ShapeDtypeStruct((nbatch, s, ncol * tc), BF16), jax.ShapeDtypeStruct((8, ncol * tc), F32)),
                  compiler_params=_params())(dc3, dc3, dc3, p3, p3, p3, w)


def _block_scan(coef, inp, reverse):
    r = coef.shape[0]
    row = lax.broadcasted_iota(jnp.int32, coef.shape, 0)
    a, b = coef, inp
    d = 1
    while d < r:
        if reverse:
            keep = row < r - d
            a_sh, b_sh = pltpu.roll(a, r - d, 0), pltpu.roll(b, r - d, 0)
        else:
            keep = row >= d
            a_sh, b_sh = pltpu.roll(a, d, 0), pltpu.roll(b, d, 0)
        b = b + a * jnp.where(keep, b_sh, 0.0)
        a = a * jnp.where(keep, a_sh, 1.0)
        d *= 2
    return a, b


def _lru_scan(a3, b3, reverse):
    nbatch, s, w = a3.shape
    ts = min(LRU_ROWS, s)
    nblk = s // ts
    edge = 0 if reverse else ts - 1

    def body(a_ref, b_ref, h_ref, carry):
        @pl.when(pl.program_id(1) == 0)
        def _():
            carry[...] = jnp.zeros_like(carry)

        ca, hb = _block_scan(a_ref[...], b_ref[...], reverse)
        h = hb + ca * carry[0:1, :]
        h_ref[...] = h
        carry[0:1, :] = h[edge:edge + 1, :]

    blk = pl.BlockSpec((None, ts, w), (lambda n, i: (n, nblk - 1 - i, 0)) if reverse else (lambda n, i: (n, i, 0)))
    return _pcall(body, name=f"lru_scan_r{int(reverse)}", grid=(nbatch, nblk), in_specs=[blk, blk], out_specs=blk,
                  out_shape=jax.ShapeDtypeStruct((nbatch, s, w), F32), scratch_shapes=[pltpu.VMEM((8, w), F32)],
                  compiler_params=_params())(a3, b3)


def _lru_scan_bwd(a3, h3, dh3, reverse):
    nbatch, s, w = a3.shape
    ts = min(LRU_ROWS, s)
    nblk = s // ts
    nb8 = s // 8
    tpb = ts // 8

    def body(a_ref, aa_ref, h_ref, hh_ref, dh_ref, g_ref, da_ref, carry):
        i = pl.program_id(1)

        @pl.when(i == 0)
        def _():
            carry[...] = jnp.zeros_like(carry)

        a, h = a_ref[...], h_ref[...]
        row = lax.broadcasted_iota(jnp.int32, a.shape, 0)
        if reverse:
            a_edge = jnp.where(i == 0, 0.0, aa_ref[7:8, :])
            c = jnp.where(row == 0, a_edge, pltpu.roll(a, 1, 0))
            h_edge = jnp.where(i == nblk - 1, 0.0, hh_ref[0:1, :])
            h_sh = jnp.where(row == ts - 1, h_edge, pltpu.roll(h, ts - 1, 0))
        else:
            a_edge = jnp.where(i == 0, 0.0, aa_ref[0:1, :])
            c = jnp.where(row == ts - 1, a_edge, pltpu.roll(a, ts - 1, 0))
            h_edge = jnp.where(i == nblk - 1, 0.0, hh_ref[7:8, :])
            h_sh = jnp.where(row == 0, h_edge, pltpu.roll(h, 1, 0))
        cc, gb = _block_scan(c, dh_ref[...], not reverse)
        g = gb + cc * carry[0:1, :]
        g_ref[...] = g
        carry[0:1, :] = g[ts - 1:ts, :] if reverse else g[0:1, :]
        da_ref[...] = g * h_sh

    if reverse:
        bi = lambda i: i
    else:
        bi = lambda i: nblk - 1 - i
    blk = pl.BlockSpec((None, ts, w), lambda n, i: (n, bi(i), 0))
    before = pl.BlockSpec((None, 8, w), lambda n, i: (n, jnp.maximum(bi(i) * tpb - 1, 0), 0))
    after = pl.BlockSpec((None, 8, w), lambda n, i: (n, jnp.minimum((bi(i) + 1) * tpb, nb8 - 1), 0))
    a_tile, h_tile = (before, after) if reverse else (after, before)
    return _pcall(body, name=f"lru_scan_bwd_r{int(reverse)}", grid=(nbatch, nblk), in_specs=[blk, a_tile, blk, h_tile, blk],
                  out_specs=(blk, blk),
                  out_shape=(jax.ShapeDtypeStruct((nbatch, s, w), F32), jax.ShapeDtypeStruct((nbatch, s, w), F32)),
                  scratch_shapes=[pltpu.VMEM((8, w), F32)], compiler_params=_params())(a3, a3, h3, h3, dh3)


def _head_expand(lane0):
    return (jnp.right_shift(lax.broadcasted_iota(jnp.int32, (128, 1024), 1), HEAD_SHIFT) + lane0
            == lax.broadcasted_iota(jnp.int32, (128, 1024), 0)).astype(F32)


def _head_reduce(lane0):
    return (jnp.right_shift(lax.broadcasted_iota(jnp.int32, (1024, 128), 0), HEAD_SHIFT) + lane0
            == lax.broadcasted_iota(jnp.int32, (1024, 128), 1)).astype(F32)


def _time_mask(q, reverse):
    ri = lax.broadcasted_iota(jnp.int32, (q, q), 0)
    ci = lax.broadcasted_iota(jnp.int32, (q, q), 1)
    return (ri <= ci) if reverse else (ri >= ci)


def _ssd_common(xs_ref, bc_ref, dt_ref, al_ref, reverse, lane0):
    q = xs_ref.shape[0]
    edge = 0 if reverse else q - 1
    dt = dt_ref[...]
    a = -jnp.exp(al_ref[...])
    mask = _time_mask(q, reverse)
    expand = _head_expand(lane0)
    cum = _dot(mask.astype(F32), dt * a, precision=HI)
    cum_x = _dot(cum, expand, precision=HI)
    dt_x = _dot(dt, expand, precision=HI)
    last_x = cum_x[edge:edge + 1, :]
    xs = xs_ref[...]
    bc = bc_ref[...]
    return dict(q=q, edge=edge, lane0=lane0, dt=dt, a=a, mask=mask, cum_t=cum.T, cum_x=cum_x, dt_x=dt_x, xs=xs,
                v=xs * dt_x, e_c=jnp.exp(cum_x), w=jnp.exp(last_x - cum_x), e_l=jnp.exp(last_x),
                bm=bc[:, :512], cm=bc[:, 512:])


def _ssd_decay(c, h):
    row = c["lane0"] + h
    seg = c["cum_x"][:, h * SSD_HEADDIM:h * SSD_HEADDIM + 1] - c["cum_t"][row:row + 1, :]
    return jnp.where(c["mask"], jnp.exp(jnp.minimum(seg, 0.0)), 0.0)


def _head_masks():
    lane = jnp.right_shift(lax.broadcasted_iota(jnp.int32, (1, 256), 1), HEAD_SHIFT)
    return [lane == e for e in range(4)]


def _ssd_fwd(xbc3, dt3, alog, reverse):
    nbatch, s, _ = xbc3.shape
    q = min(SSD_CHUNK, s)
    nc = s // q
    lane0 = SSD_HEADS * int(reverse)

    def body(xs_ref, bc_ref, dt_ref, al_ref, y_ref, st_ref, st):
        @pl.when(pl.program_id(1) == 0)
        def _():
            st[...] = jnp.zeros_like(st)

        st_ref[...] = st[...]
        c = _ssd_common(xs_ref, bc_ref, dt_ref, al_ref, reverse, lane0)
        hm = _head_masks()
        for g in range(SSD_GROUPS):
            sl = slice(g * 256, (g + 1) * 256)
            cg, bg = _mx(c["cm"][:, g * 128:(g + 1) * 128]), _mx(c["bm"][:, g * 128:(g + 1) * 128])
            cb = _dot(cg, bg, _NT)
            vg = c["v"][:, sl]
            s0 = st[:, sl]
            yg = _dot(cg, _mx(s0)) * c["e_c"][:, sl]
            for e in range(4):
                m = _ssd_decay(c, 4 * g + e) * cb
                yg = yg + _dot(_mx(m), _mx(jnp.where(hm[e], vg, 0.0)))
            y_ref[:, sl] = yg
            st[:, sl] = c["e_l"][:, sl] * s0 + _dot(bg, _mx(vg * c["w"][:, sl]), _TN)

    ck = (lambda i: nc - 1 - i) if reverse else (lambda i: i)
    xs_spec = pl.BlockSpec((None, q, 1024), lambda n, i: (n, ck(i), 0))
    bc_spec = pl.BlockSpec((None, q, 1024), lambda n, i: (n, ck(i), 1))
    dt_spec = pl.BlockSpec((None, q, 128), lambda n, i: (n, ck(i), 0))
    al_spec = pl.BlockSpec((1, 128), lambda n, i: (0, 0))
    st_spec = pl.BlockSpec((None, None, 128, 1024), lambda n, i: (n, ck(i), 0, 0))
    return _pcall(body, name=f"ssd_fwd_r{int(reverse)}", grid=(nbatch, nc), in_specs=[xs_spec, bc_spec, dt_spec, al_spec],
                  out_specs=(xs_spec, st_spec),
                  out_shape=(jax.ShapeDtypeStruct((nbatch, s, 1024), F32), jax.ShapeDtypeStruct((nbatch, nc, 128, 1024), F32)),
                  scratch_shapes=[pltpu.VMEM((128, 1024), F32)], compiler_params=_params())(xbc3, xbc3, dt3, alog)


def _ssd_bwd(xbc3, dt3, alog, st4, dy3, reverse):
    nbatch, s, _ = xbc3.shape
    q = min(SSD_CHUNK, s)
    nc = s // q
    lane0 = SSD_HEADS * int(reverse)

    def body(xs_ref, bc_ref, dt_ref, al_ref, st0_ref, dy_ref, dxs_ref, dbc_ref, ddt_ref, dal_ref, dst):
        n, i = pl.program_id(0), pl.program_id(1)

        @pl.when(i == 0)
        def _():
            dst[...] = jnp.zeros_like(dst)

        @pl.when((i == 0) & (n == 0))
        def _():
            dal_ref[...] = jnp.zeros_like(dal_ref)

        c = _ssd_common(xs_ref, bc_ref, dt_ref, al_ref, reverse, lane0)
        hm = _head_masks()
        reduce_m = _head_reduce(lane0)
        s0_all, ds1_all, dy = st0_ref[...], dst[...], dy_ref[...]
        lane = lax.broadcasted_iota(jnp.int32, (q, 128), 1)
        sub = lax.broadcasted_iota(jnp.int32, (128, q), 0)
        rowacc = jnp.zeros((q, 128), F32)
        colacc_t = jnp.zeros((128, q), F32)
        dv_l, yst_l, dvbar_l, dk_l, dc_l = [], [], [], [], []
        for g in range(SSD_GROUPS):
            sl = slice(g * 256, (g + 1) * 256)
            cg, bg = _mx(c["cm"][:, g * 128:(g + 1) * 128]), _mx(c["bm"][:, g * 128:(g + 1) * 128])
            cb = _dot(cg, bg, _NT)
            vg, dyg, wg, ecg = c["v"][:, sl], dy[:, sl], c["w"][:, sl], c["e_c"][:, sl]
            s0, ds1 = _mx(s0_all[:, sl]), _mx(ds1_all[:, sl])
            dye = _mx(dyg * ecg)
            yst_l.append(_dot(cg, s0) * ecg)
            dcg = _dot(dye, s0, _NT)
            dst[:, sl] = c["e_l"][:, sl] * ds1_all[:, sl] + _dot(cg, dye, _TN)
            vbar = _mx(vg * wg)
            dvbar = _dot(bg, ds1)
            dvbar_l.append(dvbar)
            dvg = dvbar * wg
            dkg = _dot(vbar, ds1, _NT)
            for e in range(4):
                h = 4 * g + e
                m = _ssd_decay(c, h)
                dyh, vh = _mx(jnp.where(hm[e], dyg, 0.0)), _mx(jnp.where(hm[e], vg, 0.0))
                dvg = dvg + _dot(_mx(m * cb), dyh, _TN)
                dcb = _dot(dyh, vh, _NT) * m
                dcbb = _mx(dcb)
                dcg = dcg + _dot(dcbb, bg)
                dkg = dkg + _dot(dcbb, cg, _TN)
                wmat = dcb * cb
                rowacc = jnp.where(lane == lane0 + h, jnp.sum(wmat, axis=1, keepdims=True), rowacc)
                colacc_t = jnp.where(sub == lane0 + h, jnp.sum(wmat, axis=0, keepdims=True), colacc_t)
            dv_l.append(dvg)
            dk_l.append(dkg)
            dc_l.append(dcg)
        dv = jnp.concatenate(dv_l, axis=1)
        yst = jnp.concatenate(yst_l, axis=1)
        dvbar = jnp.concatenate(dvbar_l, axis=1)
        t1 = _dot(dy * yst, reduce_m, precision=HI)
        t2 = _dot(c["v"] * c["w"] * dvbar, reduce_m, precision=HI)
        dlast = jnp.sum(t2, axis=0, keepdims=True) + _dot(
            c["e_l"] * jnp.sum(ds1_all * s0_all, axis=0, keepdims=True), reduce_m, precision=HI)
        dcum = rowacc - colacc_t.T + t1 - t2
        dcum = dcum + jnp.where(lax.broadcasted_iota(jnp.int32, (q, 128), 0) == c["edge"], dlast, 0.0)
        dda = _dot(c["mask"].astype(F32), dcum, _TN, precision=HI)
        ddt_ref[...] = dda * c["a"] + _dot(dv * c["xs"], reduce_m, precision=HI)
        dal_ref[...] += jnp.sum(dda * c["dt"], axis=0, keepdims=True) * c["a"]
        dxs_ref[...] = dv * c["dt_x"]
        dbc_ref[...] = jnp.concatenate(dk_l + dc_l, axis=1)

    ck = (lambda i: i) if reverse else (lambda i: nc - 1 - i)
    xs_spec = pl.BlockSpec((None, q, 1024), lambda n, i: (n, ck(i), 0))
    bc_spec = pl.BlockSpec((None, q, 1024), lambda n, i: (n, ck(i), 1))
    dt_spec = pl.BlockSpec((None, q, 128), lambda n, i: (n, ck(i), 0))
    al_spec = pl.BlockSpec((1, 128), lambda n, i: (0, 0))
    st_spec = pl.BlockSpec((None, None, 128, 1024), lambda n, i: (n, ck(i), 0, 0))
    return _pcall(body, name=f"ssd_bwd_r{int(reverse)}", grid=(nbatch, nc),
                  in_specs=[xs_spec, bc_spec, dt_spec, al_spec, st_spec, xs_spec],
                  out_specs=(xs_spec, xs_spec, dt_spec, al_spec),
                  out_shape=(jax.ShapeDtypeStruct((nbatch, s, 1024), F32), jax.ShapeDtypeStruct((nbatch, s, 1024), F32),
                             jax.ShapeDtypeStruct((nbatch, s, 128), F32), jax.ShapeDtypeStruct((1, 128), F32)),
                  scratch_shapes=[pltpu.VMEM((128, 1024), F32)], compiler_params=_params())(xbc3, xbc3, dt3, alog, st4, dy3)


def _gla_sub(q_ref, k_ref, g_ref, b, rs, reverse):
    sq = HGRN_SUB
    edge = 0 if reverse else sq - 1
    mask = _time_mask(sq, reverse)
    bc = _dot(mask.astype(F32), g_ref[b, rs, :], precision=HI)
    last = bc[edge:edge + 1, :]
    eb, enb, elb = jnp.exp(bc), jnp.exp(-bc), jnp.exp(last - bc)
    kv = k_ref[b, rs, :]
    return dict(mask=mask, edge=edge, eb=eb, enb=enb, elb=elb, e_l=jnp.exp(last),
                qt=q_ref[b, rs, :] * HGRN_SCALE * eb, kt=kv * enb, kb=kv * elb)


def _gla_specs(nbatch, s, w, reverse_order):
    bq = min(HGRN_BLOCK, s)
    nblk = s // bq
    bi = (lambda i: nblk - 1 - i) if reverse_order else (lambda i: i)
    col = lambda cb: pl.BlockSpec((nbatch, bq, w), lambda i: (0, bi(i), cb))
    st_spec = pl.BlockSpec((nbatch, bq // HGRN_SUB, 128, w), lambda i: (0, bi(i), 0, 0))
    return bq, nblk, col, st_spec


def _gla_fwd(proj3, k3, g3, reverse):
    nbatch, s, w = k3.shape
    bq, nblk, col, st_spec = _gla_specs(nbatch, s, w, reverse)
    nsub = bq // HGRN_SUB

    def body(q_ref, k_ref, v_ref, g_ref, o_ref, st_ref, st):
        @pl.when(pl.program_id(0) == 0)
        def _():
            st[...] = jnp.zeros_like(st)

        for j in (reversed(range(nsub)) if reverse else range(nsub)):
            rs = slice(j * HGRN_SUB, (j + 1) * HGRN_SUB)
            for b in range(nbatch):
                st_ref[b, j] = st[b]
                c = _gla_sub(q_ref, k_ref, g_ref, b, rs, reverse)
                v = v_ref[b, rs, :]
                for h in range(HGRN_HEADS):
                    hs = slice(h * 128, (h + 1) * 128)
                    qt, vb = _mx(c["qt"][:, hs]), _mx(v[:, hs])
                    att = jnp.where(c["mask"], _dot(qt, _mx(c["kt"][:, hs]), _NT), 0.0)
                    s0 = st[b, :, hs]
                    o_ref[b, rs, hs] = _dot(_mx(att), vb) + _dot(qt, _mx(s0), _NT)
                    st[b, :, hs] = s0 * c["e_l"][:, hs] + _dot(vb, _mx(c["kb"][:, hs]), _TN)

    return _pcall(body, name=f"gla_fwd_r{int(reverse)}", grid=(nblk,), in_specs=[col(0), col(0), col(3), col(0)],
                  out_specs=(col(0), st_spec),
                  out_shape=(jax.ShapeDtypeStruct((nbatch, s, w), F32),
                             jax.ShapeDtypeStruct((nbatch, s // HGRN_SUB, 128, w), F32)),
                  scratch_shapes=[pltpu.VMEM((nbatch, 128, w), F32)], compiler_params=_params())(proj3, k3, proj3, g3)


def _gla_bwd(proj3, k3, g3, st4, do3, reverse):
    nbatch, s, w = k3.shape
    bq, nblk, col, st_spec = _gla_specs(nbatch, s, w, not reverse)
    nsub = bq // HGRN_SUB
    sq = HGRN_SUB

    def body(q_ref, k_ref, v_ref, g_ref, st_ref, do_ref, dq_ref, dk_ref, dv_ref, dg_ref, dst):
        @pl.when(pl.program_id(0) == 0)
        def _():
            dst[...] = jnp.zeros_like(dst)

        row = lax.broadcasted_iota(jnp.int32, (sq, 128), 0)
        for j in (range(nsub) if reverse else reversed(range(nsub))):
            rs = slice(j * sq, (j + 1) * sq)
            for b in range(nbatch):
                c = _gla_sub(q_ref, k_ref, g_ref, b, rs, reverse)
                s0_all, ds1_all = st_ref[b, j], dst[b]
                v, dy = v_ref[b, rs, :], do_ref[b, rs, :]
                db_l = []
                for h in range(HGRN_HEADS):
                    hs = slice(h * 128, (h + 1) * 128)
                    qt, kt, kb = c["qt"][:, hs], c["kt"][:, hs], c["kb"][:, hs]
                    qtb, ktb, kbb, vb, dyb = _mx(qt), _mx(kt), _mx(kb), _mx(v[:, hs]), _mx(dy[:, hs])
                    s0, ds1 = s0_all[:, hs], ds1_all[:, hs]
                    att = jnp.where(c["mask"], _dot(qtb, ktb, _NT), 0.0)
                    datt = _mx(jnp.where(c["mask"], _dot(dyb, vb, _NT), 0.0))
                    dqt = _dot(datt, ktb) + _dot(dyb, _mx(s0))
                    dkt = _dot(datt, qtb, _TN)
                    dkb = _dot(vb, _mx(ds1))
                    dv_ref[b, rs, hs] = _dot(_mx(att), dyb, _TN) + _dot(kbb, _mx(ds1), _NT)
                    dst[b, :, hs] = c["e_l"][:, hs] * ds1 + _dot(dyb, qtb, _TN)
                    dq_ref[b, rs, hs] = dqt * c["eb"][:, hs] * HGRN_SCALE
                    dk_ref[b, rs, hs] = dkt * c["enb"][:, hs] + dkb * c["elb"][:, hs]
                    kbk = dkb * kb
                    dlast = jnp.sum(kbk, axis=0, keepdims=True) + c["e_l"][:, hs] * jnp.sum(ds1 * s0, axis=0, keepdims=True)
                    db_l.append(dqt * qt - dkt * kt - kbk + jnp.where(row == c["edge"], dlast, 0.0))
                dg_ref[b, rs, :] = _dot(c["mask"].astype(F32), jnp.concatenate(db_l, axis=1), _TN, precision=HI)

    shp = jax.ShapeDtypeStruct((nbatch, s, w), F32)
    return _pcall(body, name=f"gla_bwd_r{int(reverse)}", grid=(nblk,),
                  in_specs=[col(0), col(0), col(3), col(0), st_spec, col(0)],
                  out_specs=(col(0),) * 4, out_shape=(shp,) * 4,
                  scratch_shapes=[pltpu.VMEM((nbatch, 128, w), F32)], compiler_params=_params())(proj3, k3, proj3, g3, st4, do3)


DIRS = (False, True)


def _block_diag(w):
    eye = jnp.eye(16, dtype=w.dtype)
    return (eye[:, None, :, None] * w[:, :, None, :]).reshape(1024, 1024)


def _diag_blocks(m):
    m4 = m.reshape(16, 64, 16, 64)
    return jnp.stack([m4[i, :, i, :] for i in range(16)], axis=0)


def _pad_lanes(v, n=128):
    return jnp.pad(v, [(0, 0)] * (v.ndim - 1) + [(0, n - v.shape[-1])])


def _mlp_fwd(tag, x, nw, w1, w2):
    (h,) = _pw_fwd(f"{tag}_norm", _f_norm, [(x, 0)], [(nw, 0)], [BF16], 1024, 1)
    a, r = _mm(f"{tag}_up", h, w1, "nn", relu2=True)
    return _mm(f"{tag}_down", r, w2, "nn", res=x), (h, a, r)


def _mlp_bwd(tag, x, nw, w1, w2, saved, dxo):
    h, a, r = saved
    dw2 = _mm(f"{tag}_dw2", r, dxo, "tn")
    da = _mm(f"{tag}_da", dxo, w2, "nt", relu2_of=a, out_dtype=BF16)
    dw1 = _mm(f"{tag}_dw1", h, da, "tn", col_shards=4)
    dh = _mm(f"{tag}_dh", da, w1, "nt")
    (dx,), (dnw,) = _pw_bwd(f"{tag}_dnorm", _f_norm, [(x, 0)], [(nw, 0)], [dh], 1024, 1, [0], adds={0: dxo})
    return dx, dw1, dw2, dnw


def _split_in0(pieces, dt_piece):
    tm = 256

    def body(p0, p1, p2, p3, p4, p5, o_ref):
        full = jnp.concatenate([p0[...], p1[...], p2[...], p3[...], p4[...], p5[:, :32]], axis=1)
        for j in range(4):
            o_ref[j] = full[:, 1288 * j:1288 * (j + 1)]

    blk = pl.BlockSpec((tm, 1024), lambda i: (i, 0))
    return _pcall(body, name="split_in0", grid=(1024 // tm,), in_specs=[blk] * 5 + [pl.BlockSpec((tm, 128), lambda i: (i, 0))],
                  out_specs=pl.BlockSpec((4, tm, 1288), lambda i: (0, i, 0)),
                  out_shape=jax.ShapeDtypeStruct((4, 1024, 1288), F32), compiler_params=_params())(*pieces, dt_piece)


def _assemble_in0(shards):
    tm = 256

    def body(s_ref, m_ref, d_ref):
        full = jnp.concatenate([s_ref[j] for j in range(4)], axis=1)
        m_ref[...] = full[:, :5120]
        d_ref[...] = jnp.concatenate([full[:, 5120:5152], jnp.zeros((tm, 96), full.dtype)], axis=1)

    return _pcall(body, name="assemble_in0", grid=(1024 // tm,), in_specs=[pl.BlockSpec((4, tm, 1288), lambda i: (0, i, 0))],
                  out_specs=(pl.BlockSpec((tm, 5120), lambda i: (i, 0)), pl.BlockSpec((tm, 128), lambda i: (i, 0))),
                  out_shape=(jax.ShapeDtypeStruct((1024, 5120), shards.dtype), jax.ShapeDtypeStruct((1024, 128), shards.dtype)),
                  compiler_params=_params())(shards)


def _local_step(x3, tgt3, w, w_main0, w_dt0):
    nb, s, d = x3.shape
    t = nb * s
    x0 = x3.reshape(t, d)
    tgt = tgt3.reshape(t, d)
    grads = {}
    row = lambda v: v.reshape(1, -1)
    to3 = lambda v: v.reshape(nb, s, v.shape[-1])
    to2 = lambda v: v.reshape(-1, v.shape[-1])

    conv_w, conv_b = w["even_conv_w"][0], row(w["even_conv_b"][0])
    nmix0 = row(w["norm_mix"][0])
    (h0,) = _pw_fwd("l0_norm", _f_norm, [(x0, 0)], [(nmix0, 0)], [BF16], 1024, 1)
    proj0 = _mm("l0_proj", h0, w_main0, "nn")
    dt_raw = _mm("l0_proj_dt", h0, w_dt0, "nn")
    conv = to2(_conv_fwd(to3(proj0), conv_w, conv_b, 3))
    (xbc,) = _pw_fwd("l0_silu", _f_silu, [(conv, 0)], [], [F32], 1024, 2)
    dt_bias = _pad_lanes(w["ssd_dt_bias"][0].reshape(1, 32))
    (dt,) = _pw_fwd("l0_dt", _f_softplus, [(dt_raw, 0)], [(dt_bias, 0)], [F32], 128, 1)
    dt3, xbc3 = to3(dt), to3(xbc)
    alog = _pad_lanes(w["ssd_a_log"][0].reshape(1, 32))
    ssd = [_ssd_fwd(xbc3, dt3, alog, r) for r in DIRS]
    yf, yb = to2(ssd[0][0]), to2(ssd[1][0])
    dskip = jnp.repeat(w["ssd_d"][0], SSD_HEADDIM).reshape(1, 1024)
    snw = row(w["ssd_norm_w"][0])
    ssd_ins = [(yf, 0), (yb, 0), (xbc, 0), (proj0, 3)]
    (ya,) = _pw_fwd("l0_ssd_post", _f_ssd_post, ssd_ins, [(dskip, 0), (snw, 0)], [BF16], 1024, 1, groups=SSD_GROUPS)
    u_lru = conv[:, 2048:]
    w_gates = [_block_diag(w[k][0, r]).astype(MXU_DTYPE) for r in range(2) for k in ("lru_w_a", "lru_w_x")]
    pre = [_mm(f"l0_lru_pre{i}", u_lru, wg, "nn") for i, wg in enumerate(w_gates)]
    lru_par = [[(row(w[k][0, r]), 0) for k in ("lru_b_a", "lru_b_x", "lru_lambda")] for r in range(2)]
    lru_ins = [[(pre[2 * r], 0), (pre[2 * r + 1], 0), (u_lru, 0)] for r in range(2)]
    ab = [_pw_fwd(f"l0_lru_gates{r}", _f_lru_gates, lru_ins[r], lru_par[r], [F32, F32], 1024, 1) for r in range(2)]
    hs = [_lru_scan(to3(ab[r][0]), to3(ab[r][1]), DIRS[r]) for r in range(2)]
    lru_post_ins = [(to2(hs[0]), 0), (to2(hs[1]), 0), (proj0, 4)]
    (ybm,) = _pw_fwd("l0_lru_post", _f_lru_post, lru_post_ins, [], [BF16], 1024, 1)
    w_out0 = w["even_w_out"][0]
    x1 = _mm("l0_out_a", ya, w_out0[:1024], "nn", res=x0)
    x1 = _mm("l0_out_b", ybm, w_out0[1024:], "nn", res=x1)
    nmlp0 = row(w["norm_mlp"][0])
    x2, mlp0 = _mlp_fwd("l0_mlp", x1, nmlp0, w["mlp_w1"][0], w["mlp_w2"][0])

    w_in1 = w["odd_w_in"][0]
    nmix1 = row(w["norm_mix"][1])
    (h1,) = _pw_fwd("l1_norm", _f_norm, [(x2, 0)], [(nmix1, 0)], [BF16], 1024, 1)
    proj1 = _mm("l1_proj", h1, w_in1, "nn")
    proj1_3 = to3(proj1)
    lb0, lb1 = row(w["hgrn_lb_logits"][0]), row(w["hgrn_lb_logits"][1])
    kg = [_pw_fwd(f"l1_hgrn_pre{r}", _f_hgrn_pre, [(proj1, 1 + r)], [(lb0, 0), (lb1, 0)], [F32, F32], 1024, 1)
          for r in range(2)]
    gla = [_gla_fwd(proj1_3, to3(kg[r][0]), to3(kg[r][1]), DIRS[r]) for r in range(2)]
    hnw = row(w["hgrn_norm_w"][0])
    hpost_ins = [(to2(gla[0][0]), 0), (to2(gla[1][0]), 0), (proj1, 4)]
    (yo,) = _pw_fwd("l1_hgrn_post", _f_hgrn_post, hpost_ins, [(hnw, 0)], [BF16], 1024, 1, groups=HGRN_HEADS)
    w_out1 = w["odd_w_out"][0]
    x3_ = _mm("l1_out", yo, w_out1, "nn", res=x2)
    nmlp1 = row(w["norm_mlp"][1])
    x4, mlp1 = _mlp_fwd("l1_mlp", x3_, nmlp1, w["mlp_w1"][1], w["mlp_w2"][1])

    dx4, dnf, loss = _loss_head(x4, tgt, row(w["norm_final"]))
    grads["norm_final"] = dnf.reshape(-1)

    dx3, dw1_1, dw2_1, dnmlp1 = _mlp_bwd("l1_mlp", x3_, nmlp1, w["mlp_w1"][1], w["mlp_w2"][1], mlp1, dx4)
    big = {"odd_w_out": _mm("l1_dwout", yo, dx3, "tn").reshape(4, 256, 1024)}
    dyo = _mm("l1_dyo", dx3, w_out1, "nt")
    (do, dgate1), (dhnw,) = _pw_bwd("l1_hgrn_post_b", _f_hgrn_post, hpost_ins, [(hnw, 0)], [dyo], 1024, 1, [0, 2],
                                    out_dtypes=[F32, BF16], groups=HGRN_HEADS)
    grads["hgrn_norm_w"] = dhnw
    do3 = to3(do)
    gb = [_gla_bwd(proj1_3, to3(kg[r][0]), to3(kg[r][1]), gla[r][1], do3, DIRS[r]) for r in range(2)]
    (dq,) = _pw_fwd("l1_dq", _f_add2, [(to2(gb[0][0]), 0), (to2(gb[1][0]), 0)], [], [BF16], 1024, 1)
    (dvv,) = _pw_fwd("l1_dv", _f_add2, [(to2(gb[0][2]), 0), (to2(gb[1][2]), 0)], [], [BF16], 1024, 1)
    dfr, dl0, dl1 = [], [], []
    for r in range(2):
        (df,), (a0, a1) = _pw_bwd(f"l1_hgrn_pre_b{r}", _f_hgrn_pre, [(proj1, 1 + r)], [(lb0, 0), (lb1, 0)],
                                  [to2(gb[r][1]), to2(gb[r][3])], 1024, 1, [0], out_dtypes=[BF16])
        dfr.append(df)
        dl0.append(a0)
        dl1.append(a1)
    grads["hgrn_lb_logits"] = jnp.concatenate([dl0[0] + dl0[1], dl1[0] + dl1[1]], axis=0)
    dparts1 = [dq, dfr[0], dfr[1], dvv, dgate1]
    dwin1 = jnp.concatenate([_mm(f"l1_dwin{i}", h1, dp, "tn") for i, dp in enumerate(dparts1)], axis=1)
    big["odd_w_in"] = dwin1.reshape(1024, 4, 1280).transpose(1, 0, 2)
    dh1 = _mm_sum_nt("l1_dh", dparts1, [w_in1[:, i * 1024:(i + 1) * 1024] for i in range(5)])
    (dx2,), (dnmix1,) = _pw_bwd("l1_dnorm", _f_norm, [(x2, 0)], [(nmix1, 0)], [dh1], 1024, 1, [0], adds={0: dx3})

    dx1, dw1_0, dw2_0, dnmlp0 = _mlp_bwd("l0_mlp", x1, nmlp0, w["mlp_w1"][0], w["mlp_w2"][0], mlp0, dx2)
    big["mlp_w1"] = jnp.concatenate([dw1_0, dw1_1], axis=1)
    big["mlp_w2"] = jnp.concatenate([dw2_0.reshape(4, 1024, 1024), dw2_1.reshape(4, 1024, 1024)], axis=1)
    grads["norm_mlp"] = jnp.concatenate([dnmlp0, dnmlp1], axis=0)
    big["even_w_out"] = jnp.concatenate([_mm("l0_dwout_a", ya, dx1, "tn"), _mm("l0_dwout_b", ybm, dx1, "tn")],
                                        axis=0).reshape(4, 512, 1024)
    dya = _mm("l0_dya", dx1, w_out0[:1024], "nt")
    dyb = _mm("l0_dyb", dx1, w_out0[1024:], "nt")
    (dh, dgate0), _ = _pw_bwd("l0_lru_post_b", _f_lru_post, lru_post_ins, [], [dyb], 1024, 1, [0, 2], out_dtypes=[F32, BF16])
    dh3 = to3(dh)
    dpre, du_parts, dlru = [], [], {k: [] for k in ("lru_b_a", "lru_b_x", "lru_lambda")}
    for r in range(2):
        g_r, da_r = _lru_scan_bwd(to3(ab[r][0]), hs[r], dh3, DIRS[r])
        (dpa, dpx, du_r), (dba, dbx, dlam) = _pw_bwd(f"l0_lru_gates_b{r}", _f_lru_gates, lru_ins[r], lru_par[r],
                                                     [to2(da_r), to2(g_r)], 1024, 1, [0, 1, 2],
                                                     out_dtypes=[BF16, BF16, F32])
        dpre += [dpa, dpx]
        du_parts.append(du_r)
        dlru["lru_b_a"].append(dba)
        dlru["lru_b_x"].append(dbx)
        dlru["lru_lambda"].append(dlam)
    for k, v in dlru.items():
        grads[k] = jnp.concatenate(v, axis=0)[None]
    dwg = [_diag_blocks(_mm(f"l0_dwgate{i}", u_lru, dp, "tn")) for i, dp in enumerate(dpre)]
    grads["lru_w_a"] = jnp.stack([dwg[0], dwg[2]])[None]
    grads["lru_w_x"] = jnp.stack([dwg[1], dwg[3]])[None]
    du_gate = _mm_sum_nt("l0_du_gate", dpre, w_gates)
    (du,) = _pw_fwd("l0_du", _f_add3, [(du_parts[0], 0), (du_parts[1], 0), (du_gate, 0)], [], [F32], 1024, 1)
    (dy, dxs_skip, dz), (ddskip, dsnw) = _pw_bwd("l0_ssd_post_b", _f_ssd_post, ssd_ins, [(dskip, 0), (snw, 0)], [dya],
                                                 1024, 1, [0, 2, 3], out_dtypes=[F32, F32, BF16], groups=SSD_GROUPS)
    grads["ssd_d"] = ddskip.reshape(SSD_HEADS, SSD_HEADDIM).sum(axis=1)[None]
    grads["ssd_norm_w"] = dsnw
    dy3 = to3(dy)
    sb = [_ssd_bwd(xbc3, dt3, alog, ssd[r][1], dy3, DIRS[r]) for r in range(2)]
    grads["ssd_a_log"] = (sb[0][3] + sb[1][3])[:, :32].reshape(1, 2, 16)
    (dxs,) = _pw_fwd("l0_dxs", _f_add3, [(to2(sb[0][0]), 0), (to2(sb[1][0]), 0), (dxs_skip, 0)], [], [F32], 1024, 1)
    (dbc,) = _pw_fwd("l0_dbc", _f_add2, [(to2(sb[0][1]), 0), (to2(sb[1][1]), 0)], [], [F32], 1024, 1)
    dxbc = jnp.concatenate([dxs, dbc], axis=1)
    (dconv_a,), _ = _pw_bwd("l0_silu_b", _f_silu, [(conv, 0)], [], [dxbc], 1024, 2, [0])
    (ddt,) = _pw_fwd("l0_ddt", _f_add2, [(to2(sb[0][2]), 0), (to2(sb[1][2]), 0)], [], [F32], 128, 1)
    (ddt_raw,), (ddtb,) = _pw_bwd("l0_dt_b", _f_softplus, [(dt_raw, 0)], [(dt_bias, 0)], [ddt], 128, 1, [0])
    grads["ssd_dt_bias"] = ddtb[:, :32].reshape(1, 2, 16)
    dconv = jnp.concatenate([dconv_a, du], axis=1)
    dproj_c, dcw = _conv_bwd(to3(dconv), to3(proj0), conv_w, 3)
    grads["even_conv_w"] = dcw[:4][None]
    grads["even_conv_b"] = dcw[4:5]
    dparts0 = [to2(dproj_c)[:, :1024], to2(dproj_c)[:, 1024:2048], to2(dproj_c)[:, 2048:], dz, dgate0]
    dwin0 = [_mm(f"l0_dwin{i}", h0, dp, "tn") for i, dp in enumerate(dparts0)]
    big["even_w_in"] = _split_in0(dwin0, _mm("l0_dwin_dt", h0, ddt_raw, "tn"))
    dh0 = _mm_sum_nt("l0_dh", dparts0 + [ddt_raw], [w_main0[:, i * 1024:(i + 1) * 1024] for i in range(5)] + [w_dt0])
    (dx0,), (dnmix0,) = _pw_bwd("l0_dnorm", _f_norm, [(x0, 0)], [(nmix0, 0)], [dh0], 1024, 1, [0], adds={0: dx1})
    grads["norm_mix"] = jnp.concatenate([dnmix0, dnmix1], axis=0)
    return loss, dx0.reshape(nb, s, d), grads, [big[n] for n in BIG]


ANY = pl.BlockSpec(memory_space=pl.ANY)


def _place():
    return lax.axis_index("x"), lax.axis_index("y"), lax.axis_index("c")


def _remote(src, dst, send_sems, recv_sems, k, to):
    return pltpu.make_async_remote_copy(src_ref=src, dst_ref=dst, send_sem=send_sems.at[k], recv_sem=recv_sems.at[k],
                                        device_id=to, device_id_type=MESH)


def _gather_chips(shards):
    n = len(shards)
    halves = [s.shape[0] // 2 for s in shards]

    def body(*refs):
        x_refs, out_refs = refs[:n], refs[n:2 * n]
        send_sems, recv_sems = refs[2 * n:]
        x, y, c = _place()
        sibling = (x, y, 1 - c)
        chips = [(1 - x, y), (x, 1 - y), (1 - x, 1 - y)]

        def blk(t, px, py, hc):
            return out_refs[t].at[2 * px + py, pl.ds(hc * halves[t], halves[t]), :]

        def src(t):
            return x_refs[t].at[pl.ds(c * halves[t], halves[t]), :]

        first = [_remote(src(t), blk(t, x, y, c), send_sems, recv_sems, 6 * t + j, (*chip, c))
                 for t in range(n) for j, chip in enumerate(chips)]
        for cp in first:
            cp.start()
        passed = []
        for t in range(n):
            for j, chip in enumerate(chips):
                _remote(src(t), blk(t, *chip, c), send_sems, recv_sems, 6 * t + j, (*chip, c)).wait_recv()
                cp = _remote(blk(t, *chip, c), blk(t, *chip, c), send_sems, recv_sems, 6 * t + 3 + j, sibling)
                cp.start()
                passed.append(cp)
        for t in range(n):
            for j, chip in enumerate(chips):
                _remote(src(t), blk(t, *chip, 1 - c), send_sems, recv_sems, 6 * t + 3 + j, sibling).wait_recv()
        for cp in first + passed:
            cp.wait_send()

    return _pcall(body, name="gather_weights", in_specs=[ANY] * n, out_specs=(ANY,) * n,
                  out_shape=tuple(jax.ShapeDtypeStruct((4,) + s.shape, s.dtype) for s in shards),
                  scratch_shapes=[pltpu.SemaphoreType.DMA((6 * n,)), pltpu.SemaphoreType.DMA((6 * n,))],
                  compiler_params=_params())(*shards)


def _pair_swap(gps):
    n = len(gps)
    halves = [g.shape[1] // 2 for g in gps]

    def body(*refs):
        g_refs, land_refs = refs[:n], refs[n:2 * n]
        send_sems, recv_sems = refs[2 * n:]
        x, y, c = _place()
        cps = [_remote(g_refs[t].at[j, pl.ds((1 - c) * halves[t], halves[t]), :], land_refs[t].at[j], send_sems, recv_sems,
                       4 * t + j, (x, y, 1 - c)) for t in range(n) for j in range(4)]
        for cp in cps:
            cp.start()
        for cp in cps:
            cp.wait()

    return _pcall(body, name="grad_pair_swap", in_specs=[ANY] * n, out_specs=(ANY,) * n,
                  out_shape=tuple(jax.ShapeDtypeStruct((4, h, g.shape[2]), F32) for g, h in zip(gps, halves)),
                  scratch_shapes=[pltpu.SemaphoreType.DMA((4 * n,)), pltpu.SemaphoreType.DMA((4 * n,))],
                  compiler_params=_params())(*gps)


def _pair_add(name, gp, land, cidx):
    _, half, cols = land.shape
    tr = _tile(half, 512)
    nh = half // tr

    def body(c_ref, g_ref, l_ref, o_ref):
        o_ref[...] = (g_ref[...] + l_ref[...]).astype(o_ref.dtype)

    grid_spec = pltpu.PrefetchScalarGridSpec(
        num_scalar_prefetch=1, grid=(4, nh),
        in_specs=[pl.BlockSpec((None, tr, cols), lambda j, i, c: (j, c[0] * nh + i, 0)),
                  pl.BlockSpec((None, tr, cols), lambda j, i, c: (j, i, 0))],
        out_specs=pl.BlockSpec((None, tr, cols), lambda j, i, c: (j, i, 0)))
    return _pcall(body, name=f"pair_add_{name}", grid_spec=grid_spec, out_shape=jax.ShapeDtypeStruct((4, half, cols), BF16),
                  compiler_params=_params())(cidx, gp, land)


def _chip_scatter(css):
    n = len(css)

    def body(*refs):
        s_refs, land_refs = refs[:n], refs[n:2 * n]
        send_sems, recv_sems = refs[2 * n:]
        x, y, c = _place()
        me = 2 * x + y
        chips = [(1 - x, y), (x, 1 - y), (1 - x, 1 - y)]
        cps = [_remote(s_refs[t].at[2 * px + py], land_refs[t].at[me], send_sems, recv_sems, 3 * t + j, (px, py, c))
               for t in range(n) for j, (px, py) in enumerate(chips)]
        for cp in cps:
            cp.start()
        for t in range(n):
            for j, (px, py) in enumerate(chips):
                _remote(s_refs[t].at[me], land_refs[t].at[2 * px + py], send_sems, recv_sems, 3 * t + j, (px, py, c)).wait_recv()
        for cp in cps:
            cp.wait_send()

    return _pcall(body, name="grad_chip_scatter", in_specs=[ANY] * n, out_specs=(ANY,) * n,
                  out_shape=tuple(jax.ShapeDtypeStruct(s.shape, s.dtype) for s in css),
                  scratch_shapes=[pltpu.SemaphoreType.DMA((3 * n,)), pltpu.SemaphoreType.DMA((3 * n,))],
                  compiler_params=_params())(*css)


def _chip_sum(name, land):
    _, half, cols = land.shape
    tr = _tile(half, 512)

    def body(l_ref, o_ref):
        o_ref[...] = ((l_ref[0].astype(F32) + l_ref[1].astype(F32)) + l_ref[2].astype(F32)) + l_ref[3].astype(F32)

    return _pcall(body, name=f"chip_sum_{name}", grid=(half // tr,),
                  in_specs=[pl.BlockSpec((4, tr, cols), lambda i: (0, i, 0))],
                  out_specs=pl.BlockSpec((tr, cols), lambda i: (i, 0)),
                  out_shape=jax.ShapeDtypeStruct((half, cols), F32), compiler_params=_params())(land)


def _pair_join(reds):
    n = len(reds)

    def body(*refs):
        r_refs, out_refs = refs[:n], refs[n:2 * n]
        send_sems, recv_sems = refs[2 * n:]
        x, y, c = _place()
        cps = [_remote(r_refs[t], out_refs[t].at[c], send_sems, recv_sems, t, (x, y, 1 - c)) for t in range(n)]
        for cp in cps:
            cp.start()
        for t in range(n):
            _remote(r_refs[t], out_refs[t].at[1 - c], send_sems, recv_sems, t, (x, y, 1 - c)).wait_recv()
        for cp in cps:
            cp.wait_send()

    return _pcall(body, name="grad_pair_join", in_specs=[ANY] * n, out_specs=(ANY,) * n,
                  out_shape=tuple(jax.ShapeDtypeStruct((2,) + r.shape, F32) for r in reds),
                  scratch_shapes=[pltpu.SemaphoreType.DMA((n,)), pltpu.SemaphoreType.DMA((n,))],
                  compiler_params=_params())(*reds)


def _adamw(name, g, w, m, v):
    rows, cols = g.shape
    tr = _tile(rows, 512)

    def body(g_ref, w_ref, m_ref, v_ref, d_ref, mo_ref, vo_ref):
        gv = g_ref[...]
        mn = ADAM_B1 * m_ref[...] + (1.0 - ADAM_B1) * gv
        vn = ADAM_B2 * v_ref[...] + (1.0 - ADAM_B2) * jnp.square(gv)
        m_hat = mn / (1.0 - ADAM_B1 ** ADAM_STEP)
        v_hat = vn / (1.0 - ADAM_B2 ** ADAM_STEP)
        d_ref[...] = -ADAM_LR * (m_hat / (jnp.sqrt(v_hat) + ADAM_EPS) + ADAM_WD * w_ref[...])
        mo_ref[...] = mn
        vo_ref[...] = vn

    blk = pl.BlockSpec((tr, cols), lambda i: (i, 0))
    shp = jax.ShapeDtypeStruct((rows, cols), F32)
    return _pcall(body, name=f"adamw_{name}", grid=(rows // tr,), in_specs=[blk] * 4, out_specs=(blk,) * 3,
                  out_shape=(shp,) * 3, compiler_params=_params())(g, w, m, v)


def _pack(pieces, rows, dtype):
    flat = jnp.concatenate([p.reshape(-1).astype(dtype) for p in pieces])
    return jnp.pad(flat, (0, rows * PACK_COLS - flat.shape[0])).reshape(rows, PACK_COLS)


def _unpack(pack, shapes):
    flat = pack.reshape(-1)
    out, off = [], 0
    for shp in shapes:
        n = math.prod(shp)
        out.append(flat[off:off + n].reshape(shp))
        off += n
    return out


def _shard_of(full, axis, j):
    n = full.shape[axis] // 4
    return lax.slice_in_dim(full, j * n, (j + 1) * n, axis=axis)


def kernel(x, even_w_in, even_conv_w, even_conv_b, ssd_a_log, ssd_dt_bias, ssd_d, ssd_norm_w, lru_w_a, lru_b_a, lru_w_x, lru_b_x, lru_lambda, even_w_out, odd_w_in, hgrn_lb_logits, hgrn_norm_w, odd_w_out, norm_mix, norm_mlp, mlp_w1, mlp_w2, norm_final, loss_target, m_even_w_in, m_even_conv_w, m_even_conv_b, m_ssd_a_log, m_ssd_dt_bias, m_ssd_d, m_ssd_norm_w, m_lru_w_a, m_lru_b_a, m_lru_w_x, m_lru_b_x, m_lru_lambda, m_even_w_out, m_odd_w_in, m_hgrn_lb_logits, m_hgrn_norm_w, m_odd_w_out, m_norm_mix, m_norm_mlp, m_mlp_w1, m_mlp_w2, m_norm_final, v_even_w_in, v_even_conv_w, v_even_conv_b, v_ssd_a_log, v_ssd_dt_bias, v_ssd_d, v_ssd_norm_w, v_lru_w_a, v_lru_b_a, v_lru_w_x, v_lru_b_x, v_lru_lambda, v_even_w_out, v_odd_w_in, v_hgrn_lb_logits, v_hgrn_norm_w, v_odd_w_out, v_norm_mix, v_norm_mlp, v_mlp_w1, v_mlp_w2, v_norm_final):
    names = [n for n, _, _, _ in WEIGHTS]
    w_loc = dict(zip(names, (even_w_in, even_conv_w, even_conv_b, ssd_a_log, ssd_dt_bias, ssd_d, ssd_norm_w, lru_w_a, lru_b_a, lru_w_x, lru_b_x, lru_lambda, even_w_out, odd_w_in, hgrn_lb_logits, hgrn_norm_w, odd_w_out, norm_mix, norm_mlp, mlp_w1, mlp_w2, norm_final)))
    m_loc = dict(zip(names, (m_even_w_in, m_even_conv_w, m_even_conv_b, m_ssd_a_log, m_ssd_dt_bias, m_ssd_d, m_ssd_norm_w, m_lru_w_a, m_lru_b_a, m_lru_w_x, m_lru_b_x, m_lru_lambda, m_even_w_out, m_odd_w_in, m_hgrn_lb_logits, m_hgrn_norm_w, m_odd_w_out, m_norm_mix, m_norm_mlp, m_mlp_w1, m_mlp_w2, m_norm_final)))
    v_loc = dict(zip(names, (v_even_w_in, v_even_conv_w, v_even_conv_b, v_ssd_a_log, v_ssd_dt_bias, v_ssd_d, v_ssd_norm_w, v_lru_w_a, v_lru_b_a, v_lru_w_x, v_lru_b_x, v_lru_lambda, v_even_w_out, v_odd_w_in, v_hgrn_lb_logits, v_hgrn_norm_w, v_odd_w_out, v_norm_mix, v_norm_mlp, v_mlp_w1, v_mlp_w2, v_norm_final)))
    spec = {n: (blk, full, ax) for n, blk, full, ax in WEIGHTS}

    small = [n for n in names if n not in BIG]
    two_d = lambda n, v: v.reshape(BIG_2D[n])

    me = 2 * lax.axis_index("x") + lax.axis_index("y")
    cc = lax.axis_index("c")
    put = lambda whole, part, k: lax.dynamic_update_slice_in_dim(whole, part[None], k, axis=0)
    own = [two_d(n, w_loc[n]).astype(BF16) for n in BIG] + [_pack([w_loc[n] for n in SMALL_SHARDED], 16, F32)]
    g_in0, g_out0, g_in1, g_out1, g_w1, g_w2, g_small = [put(g, o, me) for g, o in zip(_gather_chips(own), own)]
    w_main0, w_dt0 = _assemble_in0(g_in0)
    w_full = {n: w_loc[n] for n in names if spec[n][2] is None}
    w_full["even_w_out"] = g_out0.reshape(1, 2048, 1024)
    w_full["odd_w_in"] = jnp.concatenate([g_in1[j] for j in range(4)], axis=1)[None]
    w_full["odd_w_out"] = g_out1.reshape(1, 1024, 1024)
    w_full["mlp_w1"] = jnp.stack([jnp.concatenate([g_w1[j, l * 1024:(l + 1) * 1024] for j in range(4)], axis=1) for l in range(2)])
    w_full["mlp_w2"] = jnp.stack([jnp.concatenate([g_w2[j, l * 1024:(l + 1) * 1024] for j in range(4)], axis=0) for l in range(2)])
    shards = [_unpack(g_small[j], [spec[n][0] for n in SMALL_SHARDED]) for j in range(4)]
    for i, n in enumerate(SMALL_SHARDED):
        w_full[n] = jnp.concatenate([shards[j][i] for j in range(4)], axis=spec[n][2])

    loss_vec, grad_x, grads, big = _local_step(x, loss_target, w_full, w_main0, w_dt0)
    loss = lax.psum(loss_vec[0, 0], ("x", "y", "c"))

    def dest_pack(j):
        return _pack([grads[n].reshape(spec[n][1]) if spec[n][2] is None else _shard_of(grads[n].reshape(spec[n][1]), spec[n][2], j)
                      for n in small], SMALL_ROWS, F32)

    tensors = big + [jnp.stack([dest_pack(j) for j in range(4)])]
    tags = list(BIG) + ["small"]
    cidx = cc.astype(jnp.int32).reshape(1)
    chip_sums = [_pair_add(tag, g, land, cidx) for tag, g, land in zip(tags, tensors, _pair_swap(tensors))]
    landed = [put(land, lax.dynamic_index_in_dim(cs, me, axis=0, keepdims=False), me)
              for land, cs in zip(_chip_scatter(chip_sums), chip_sums)]
    halves = [_chip_sum(tag, land) for tag, land in zip(tags, landed)]
    reduced = [put(r, h, cc).reshape(-1, r.shape[-1]) for r, h in zip(_pair_join(halves), halves)]

    outs = {}
    for n, g in zip(BIG, reduced[:-1]):
        res = (g, *_adamw(n, g, two_d(n, w_loc[n]), two_d(n, m_loc[n]), two_d(n, v_loc[n])))
        outs[n] = [r.reshape(spec[n][0]) for r in res]
    blocks = [spec[n][0] for n in small]
    wp, mp, vp = (_pack([src[n] for n in small], SMALL_ROWS, F32) for src in (w_loc, m_loc, v_loc))
    res = (reduced[-1], *_adamw("small", reduced[-1], wp, mp, vp))
    unpacked = [_unpack(r, blocks) for r in res]
    for i, n in enumerate(small):
        outs[n] = [u[i] for u in unpacked]
    return (loss, grad_x, *[outs[n][k] for k in range(4) for n in names])
```

```python
import functools
import math

import jax
import jax.numpy as jnp
from jax import lax
from jax.experimental import pallas as pl
from jax.experimental.pallas import tpu as pltpu

F32 = jnp.float32
BF16 = jnp.bfloat16
MXU_DTYPE = jnp.bfloat16
HI = lax.Precision.HIGHEST
MESH = pl.DeviceIdType.MESH

D_MODEL = 1024
EPS = 1e-6
SSD_HEADS = 16
SSD_HEADDIM = 64
HEAD_SHIFT = 6
SSD_GROUPS = 4
SSD_STATE = 128
SSD_CHUNK = 128
LRU_C = 8.0
LRU_ROWS = 256
HGRN_HEADS = 8
HGRN_HEADDIM = 128
HGRN_SUB = 32
HGRN_SUB_SHIFT = 5
HGRN_BLOCK = 128
HGRN_SCALE = HGRN_HEADDIM ** -0.5
CONV_ROWS = 512

ADAM_LR = 0.001
ADAM_B1 = 0.9
ADAM_B2 = 0.999
ADAM_EPS = 1e-08
ADAM_WD = 0.01
ADAM_STEP = 10

VMEM_LIMIT = 56 * 1024 * 1024
PACK_COLS = 1024
SMALL_ROWS = 288

WEIGHTS = (
    ("even_w_in", (1, 1024, 1288), (1, 1024, 5152), 2),
    ("even_conv_w", (1, 4, 768), (1, 4, 3072), 2),
    ("even_conv_b", (1, 3072), (1, 3072), None),
    ("ssd_a_log", (1, 2, 16), (1, 2, 16), None),
    ("ssd_dt_bias", (1, 2, 16), (1, 2, 16), None),
    ("ssd_d", (1, 16), (1, 16), None),
    ("ssd_norm_w", (1, 1024), (1, 1024), None),
    ("lru_w_a", (1, 2, 16, 64, 64), (1, 2, 16, 64, 64), None),
    ("lru_b_a", (1, 2, 256), (1, 2, 1024), 2),
    ("lru_w_x", (1, 2, 16, 64, 64), (1, 2, 16, 64, 64), None),
    ("lru_b_x", (1, 2, 256), (1, 2, 1024), 2),
    ("lru_lambda", (1, 2, 256), (1, 2, 1024), 2),
    ("even_w_out", (1, 512, 1024), (1, 2048, 1024), 1),
    ("odd_w_in", (1, 1024, 1280), (1, 1024, 5120), 2),
    ("hgrn_lb_logits", (2, 1024), (2, 1024), None),
    ("hgrn_norm_w", (1, 256), (1, 1024), 1),
    ("odd_w_out", (1, 256, 1024), (1, 1024, 1024), 1),
    ("norm_mix", (2, 1024), (2, 1024), None),
    ("norm_mlp", (2, 1024), (2, 1024), None),
    ("mlp_w1", (2, 1024, 1024), (2, 1024, 4096), 2),
    ("mlp_w2", (2, 1024, 1024), (2, 4096, 1024), 1),
    ("norm_final", (1024,), (1024,), None),
)
BIG = ("even_w_in", "even_w_out", "odd_w_in", "odd_w_out", "mlp_w1", "mlp_w2")
BIG_2D = {"even_w_in": (1024, 1288), "even_w_out": (512, 1024), "odd_w_in": (1024, 1280), "odd_w_out": (256, 1024),
          "mlp_w1": (2048, 1024), "mlp_w2": (2048, 1024)}
SMALL_SHARDED = ("even_conv_w", "lru_b_a", "lru_b_x", "lru_lambda", "hgrn_norm_w")


def _pcall(body, **kw):
    return pl.pallas_call(body, **kw)


def _params(**kw):
    return pltpu.CompilerParams(vmem_limit_bytes=VMEM_LIMIT, **kw)


def _tile(n, pref):
    if n <= pref:
        return n
    t = (pref // 128) * 128
    while n % t:
        t -= 128
    return t


def _dot(a, b, dims=(((1,), (0,)), ((), ())), precision=None):
    return lax.dot_general(a, b, dims, preferred_element_type=F32, precision=precision)


_NN = (((1,), (0,)), ((), ()))
_NT = (((1,), (1,)), ((), ()))
_TN = (((0,), (0,)), ((), ()))


def _mx(v):
    return v.astype(MXU_DTYPE)


def _mm(name, a, b, mode, *, out_dtype=F32, res=None, relu2=False, relu2_of=None, col_shards=1):
    if mode == "nn":
        (m, kk), (_, n) = a.shape, b.shape
    elif mode == "nt":
        (m, kk), (n, _) = a.shape, b.shape
    else:
        (kk, m), (_, n) = a.shape, b.shape
    assert res is None or relu2_of is None
    tk_pref = 1024
    if mode == "tn" and a.dtype.itemsize == 2 and b.dtype.itemsize == 2:
        tk_pref = 2048
    tm, tn, tk = _tile(m, 1024), _tile(n // col_shards, 1024), _tile(kk, tk_pref)
    nk = kk // tk
    dims = {"nn": _NN, "nt": _NT, "tn": _TN}[mode]
    a_spec = pl.BlockSpec((tk, tm), lambda i, j, k: (k, i)) if mode == "tn" else pl.BlockSpec((tm, tk), lambda i, j, k: (i, k))
    b_spec = pl.BlockSpec((tn, tk), lambda i, j, k: (j, k)) if mode == "nt" else pl.BlockSpec((tk, tn), lambda i, j, k: (k, j))
    o_spec = pl.BlockSpec((tm, tn), lambda i, j, k: (i, j))
    o_shape = (m, n)
    if col_shards > 1:
        assert tn * col_shards == n and res is None and not relu2
        o_spec = pl.BlockSpec((None, tm, tn), lambda i, j, k: (j, i, 0))
        o_shape = (col_shards, m, tn)
    extra = res if res is not None else relu2_of
    has_res = extra is not None

    def body(*refs):
        a_ref, b_ref = refs[0], refs[1]
        res_ref = refs[2] if has_res else None
        outs = refs[2 + has_res:2 + has_res + 1 + relu2]

        def finish(r):
            if res is not None:
                r = r + res_ref[...]
            if relu2_of is not None:
                r = r * (2.0 * jnp.maximum(res_ref[...], 0.0))
            if relu2:
                outs[0][...] = r
                outs[1][...] = jnp.square(jnp.maximum(r, 0.0)).astype(outs[1].dtype)
            else:
                outs[0][...] = r.astype(outs[0].dtype)

        prod = _dot(_mx(a_ref[...]), _mx(b_ref[...]), dims)
        if nk == 1:
            finish(prod)
            return
        acc = refs[-1]
        k = pl.program_id(2)

        @pl.when(k == 0)
        def _():
            acc[...] = prod

        @pl.when(k > 0)
        def _():
            acc[...] += prod

        @pl.when(k == nk - 1)
        def _():
            finish(acc[...])

    in_specs = [a_spec, b_spec] + ([o_spec] if has_res else [])
    if relu2:
        out_shape = (jax.ShapeDtypeStruct((m, n), F32), jax.ShapeDtypeStruct((m, n), BF16))
        out_specs = (o_spec, o_spec)
    else:
        out_shape = jax.ShapeDtypeStruct(o_shape, out_dtype)
        out_specs = o_spec
    args = (a, b) + ((extra,) if has_res else ())
    return _pcall(body, name=name, grid=(m // tm, n // tn, nk), in_specs=in_specs, out_specs=out_specs,
                  out_shape=out_shape, scratch_shapes=[pltpu.VMEM((tm, tn), F32)] if nk > 1 else [],
                  compiler_params=_params())(*args)


def _mm_sum_nt(name, parts, wblocks):
    m, n, npart = parts[0].shape[0], wblocks[0].shape[0], len(parts)
    tm, tn = _tile(m, 512), _tile(n, 1024)

    def body(*refs):
        acc = _dot(_mx(refs[0][...]), _mx(refs[npart][...]), _NT)
        for k in range(1, npart):
            acc = acc + _dot(_mx(refs[k][...]), _mx(refs[npart + k][...]), _NT)
        refs[-1][...] = acc

    in_specs = [pl.BlockSpec((tm, p.shape[1]), lambda i, j: (i, 0)) for p in parts]
    in_specs += [pl.BlockSpec((tn, w.shape[1]), lambda i, j: (j, 0)) for w in wblocks]
    return _pcall(body, name=name, grid=(m // tm, n // tn), in_specs=in_specs, out_specs=pl.BlockSpec((tm, tn), lambda i, j: (i, j)),
                  out_shape=jax.ShapeDtypeStruct((m, n), F32), compiler_params=_params())(*parts, *wblocks)


def _pw_fwd(name, f, ins, params, out_dtypes, tc, ncol, tm=256, groups=1):
    t = ins[0][0].shape[0]
    tm = min(tm, t)
    ni, npar = len(ins), len(params)
    gw = tc // groups

    def body(*refs):
        for g in range(groups):
            sl = slice(g * gw, (g + 1) * gw)
            vals = f(*[r[:, sl].astype(F32) for r in refs[:ni]], *[r[:, sl] for r in refs[ni:ni + npar]])
            for o, v in zip(refs[ni + npar:], vals):
                o[:, sl] = v.astype(o.dtype)

    in_specs = [pl.BlockSpec((tm, tc), lambda j, i, off=off: (i, off + j)) for _, off in ins]
    in_specs += [pl.BlockSpec((1, tc), lambda j, i, off=off: (0, off + j)) for _, off in params]
    out_specs = tuple(pl.BlockSpec((tm, tc), lambda j, i: (i, j)) for _ in out_dtypes)
    out_shape = tuple(jax.ShapeDtypeStruct((t, ncol * tc), d) for d in out_dtypes)
    return _pcall(body, name=name, grid=(ncol, t // tm), in_specs=in_specs, out_specs=out_specs, out_shape=out_shape,
                  compiler_params=_params())(*[a for a, _ in ins], *[p for p, _ in params])


def _pw_bwd(name, f, ins, params, douts, tc, ncol, want, adds=None, tm=256, out_dtypes=None, groups=1):
    adds = adds or {}
    out_dtypes = out_dtypes or [F32] * len(want)
    t = ins[0][0].shape[0]
    tm = min(tm, t)
    ni, npar, nd, na = len(ins), len(params), len(douts), len(adds)
    add_keys = sorted(adds)
    gw = tc // groups

    def body(*refs):
        in_refs, p_refs = refs[:ni], refs[ni:ni + npar]
        d_refs = refs[ni + npar:ni + npar + nd]
        a_refs = refs[ni + npar + nd:ni + npar + nd + na]
        o_refs = refs[ni + npar + nd + na:]
        for p in range(npar):
            @pl.when(pl.program_id(1) == 0)
            def _(o=o_refs[len(want) + p]):
                o[...] = jnp.zeros_like(o)

        for g in range(groups):
            sl = slice(g * gw, (g + 1) * gw)
            _, vjp = jax.vjp(f, *[r[:, sl].astype(F32) for r in in_refs], *[r[:, sl] for r in p_refs])
            cts = vjp(tuple(d[:, sl].astype(F32) for d in d_refs))
            for o, kidx in zip(o_refs[:len(want)], want):
                v = cts[kidx]
                if kidx in adds:
                    v = v + a_refs[add_keys.index(kidx)][:, sl]
                o[:, sl] = v.astype(o.dtype)
            for p in range(npar):
                o_refs[len(want) + p][:, sl] += cts[ni + p]

    in_specs = [pl.BlockSpec((tm, tc), lambda j, i, off=off: (i, off + j)) for _, off in ins]
    in_specs += [pl.BlockSpec((1, tc), lambda j, i, off=off: (0, off + j)) for _, off in params]
    in_specs += [pl.BlockSpec((tm, tc), lambda j, i: (i, j)) for _ in range(nd + na)]
    out_specs = tuple([pl.BlockSpec((tm, tc), lambda j, i: (i, j)) for _ in want]
                      + [pl.BlockSpec((1, tc), lambda j, i: (0, j)) for _ in params])
    out_shape = tuple([jax.ShapeDtypeStruct((t, ncol * tc), dt) for dt in out_dtypes]
                      + [jax.ShapeDtypeStruct((1, ncol * tc), F32) for _ in params])
    res = _pcall(body, name=name, grid=(ncol, t // tm), in_specs=in_specs, out_specs=out_specs, out_shape=out_shape,
                 compiler_params=_params())(*[a for a, _ in ins], *[p for p, _ in params], *douts, *[adds[k] for k in add_keys])
    return list(res[:len(want)]), list(res[len(want):])


def _rms(x, g):
    return (x * lax.rsqrt(jnp.mean(x * x, axis=-1, keepdims=True) + EPS)) * g


def _f_norm(x, g):
    return (_rms(x, g),)


def _f_silu(c):
    return (jax.nn.silu(c),)


def _f_softplus(d, b):
    return (jax.nn.softplus(d + b),)


def _f_add2(a, b):
    return (a + b,)


def _f_add3(a, b, c):
    return (a + b + c,)


def _f_ssd_post(yf, yb, xs, z, dskip, nw):
    u = (yf + yb + dskip * xs) * jax.nn.silu(z)
    return (_rms(u, nw),)


def _neg_expm1(v):
    t = jnp.tanh(0.5 * v)
    return -2.0 * t / (1.0 - t)


def _f_lru_gates(pre_a, pre_x, u, ba, bx, lam):
    rg = jax.nn.sigmoid(pre_a + ba)
    ig = jax.nn.sigmoid(pre_x + bx)
    log_a = -LRU_C * rg * jax.nn.softplus(-lam)
    return jnp.exp(log_a), jnp.sqrt(_neg_expm1(2.0 * log_a)) * (ig * u)


def _f_lru_post(hf, hb, gate):
    return ((hf + hb) * jax.nn.gelu(gate),)


def _f_hgrn_pre(fr, l0, l1):
    lb = jax.nn.sigmoid(l1 - l0)
    k = (1.0 - lb) * jax.nn.sigmoid(-fr)
    return k, jnp.log1p(-k)


def _f_hgrn_post(of, ob, gate, nw):
    return (_rms(of + ob, nw) * jax.nn.silu(gate),)


def _loss_head(x, tgt, g, tm=256):
    t, d = x.shape
    tm = min(tm, t)

    def body(x_ref, t_ref, g_ref, dx_ref, dg_ref, loss_ref):
        tv = t_ref[...]

        def lf(xv, gv):
            return 0.5 * jnp.sum(jnp.mean(jnp.square(_rms(xv, gv) - tv), axis=-1))

        val, vjp = jax.vjp(lf, x_ref[...], g_ref[...])
        dx, dg = vjp(jnp.ones((), F32))
        dx_ref[...] = dx

        @pl.when(pl.program_id(0) == 0)
        def _():
            dg_ref[...] = jnp.zeros_like(dg_ref)
            loss_ref[...] = jnp.zeros_like(loss_ref)

        dg_ref[...] += dg
        loss_ref[...] += jnp.full(loss_ref.shape, val, F32)

    row = pl.BlockSpec((tm, d), lambda i: (i, 0))
    vec = pl.BlockSpec((1, d), lambda i: (0, 0))
    return _pcall(body, name="loss_head", grid=(t // tm,), in_specs=[row, row, vec],
                  out_specs=(row, vec, pl.BlockSpec((1, 128), lambda i: (0, 0))),
                  out_shape=(jax.ShapeDtypeStruct((t, d), F32), jax.ShapeDtypeStruct((1, d), F32),
                             jax.ShapeDtypeStruct((1, 128), F32)), compiler_params=_params())(x, tgt, g)


def _shifted(x, d, prev, nxt, first, last):
    r = x.shape[0]
    row = lax.broadcasted_iota(jnp.int32, x.shape, 0)
    if d < 0:
        out = pltpu.roll(x, -d, 0)
        for q in range(-d):
            pv = jnp.where(first, 0.0, prev[8 + d + q:8 + d + q + 1, :])
            out = jnp.where(row == q, pv, out)
        return out
    out = pltpu.roll(x, r - d, 0)
    for q in range(d):
        nv = jnp.where(last, 0.0, nxt[q:q + 1, :])
        out = jnp.where(row == r - d + q, nv, out)
    return out


def _halo_specs(ts, tc, s):
    nb8 = s // 8
    cur = pl.BlockSpec((None, ts, tc), lambda n, i, j: (n, i, j))
    prev = pl.BlockSpec((None, 8, tc), lambda n, i, j: (n, jnp.maximum(i * (ts // 8) - 1, 0), j))
    nxt = pl.BlockSpec((None, 8, tc), lambda n, i, j: (n, jnp.minimum((i + 1) * (ts // 8), nb8 - 1), j))
    return cur, prev, nxt


def _conv_fwd(p3, w, b, ncol, tc=1024):
    nbatch, s, _ = p3.shape
    ts = min(CONV_ROWS, s)
    nblk = s // ts

    def body(x_ref, pv_ref, nx_ref, w_ref, b_ref, o_ref):
        i = pl.program_id(1)
        first, last = i == 0, i == nblk - 1
        x, pv, nx = x_ref[...], pv_ref[...], nx_ref[...]
        wv = w_ref[...]
        out = b_ref[...] + wv[1:2] * x
        out = out + wv[0:1] * _shifted(x, -1, pv, nx, first, last)
        out = out + wv[2:3] * _shifted(x, 1, pv, nx, first, last)
        out = out + wv[3:4] * _shifted(x, 2, pv, nx, first, last)
        o_ref[...] = out

    cur, prev, nxt = _halo_specs(ts, tc, s)
    return _pcall(body, name="conv_fwd", grid=(nbatch, nblk, ncol),
                  in_specs=[cur, prev, nxt, pl.BlockSpec((4, tc), lambda n, i, j: (0, j)),
                            pl.BlockSpec((1, tc), lambda n, i, j: (0, j))],
                  out_specs=cur, out_shape=jax.ShapeDtypeStruct((nbatch, s, ncol * tc), F32),
                  compiler_params=_params())(p3, p3, p3, w, b)


def _conv_bwd(dc3, p3, w, ncol, tc=1024):
    nbatch, s, _ = dc3.shape
    ts = min(CONV_ROWS, s)
    nblk = s // ts

    def body(d_ref, dpv_ref, dnx_ref, x_ref, pv_ref, nx_ref, w_ref, dx_ref, dw_ref):
        n, i = pl.program_id(1), pl.program_id(2)
        first, last = i == 0, i == nblk - 1
        d, dpv, dnx = d_ref[...], dpv_ref[...], dnx_ref[...]
        x, pv, nx = x_ref[...], pv_ref[...], nx_ref[...]
        wv = w_ref[...]
        dx = wv[1:2] * d
        dx = dx + wv[0:1] * _shifted(d, 1, dpv, dnx, first, last)
        dx = dx + wv[2:3] * _shifted(d, -1, dpv, dnx, first, last)
        dx = dx + wv[3:4] * _shifted(d, -2, dpv, dnx, first, last)
        dx_ref[...] = dx.astype(dx_ref.dtype)

        @pl.when((n == 0) & (i == 0))
        def _():
            dw_ref[...] = jnp.zeros_like(dw_ref)

        dw_ref[0:1, :] += jnp.sum(d * _shifted(x, -1, pv, nx, first, last), axis=0, keepdims=True)
        dw_ref[1:2, :] += jnp.sum(d * x, axis=0, keepdims=True)
        dw_ref[2:3, :] += jnp.sum(d * _shifted(x, 1, pv, nx, first, last), axis=0, keepdims=True)
        dw_ref[3:4, :] += jnp.sum(d * _shifted(x, 2, pv, nx, first, last), axis=0, keepdims=True)
        dw_ref[4:5, :] += jnp.sum(d, axis=0, keepdims=True)

    nb8 = s // 8
    cur = pl.BlockSpec((None, ts, tc), lambda j, n, i: (n, i, j))
    prev = pl.BlockSpec((None, 8, tc), lambda j, n, i: (n, jnp.maximum(i * (ts // 8) - 1, 0), j))
    nxt = pl.BlockSpec((None, 8, tc), lambda j, n, i: (n, jnp.minimum((i + 1) * (ts // 8), nb8 - 1), j))
    return _pcall(body, name="conv_bwd", grid=(ncol, nbatch, nblk),
                  in_specs=[cur, prev, nxt, cur, prev, nxt, pl.BlockSpec((4, tc), lambda j, n, i: (0, j))],
                  out_specs=(cur, pl.BlockSpec((8, tc), lambda j, n, i: (0, j))),
                  out_shape=(jax.ShapeDtypeStruct((nbatch, s, ncol * tc), BF16), jax.ShapeDtypeStruct((8, ncol * tc), F32)),
                  compiler_params=_params())(dc3, dc3, dc3, p3, p3, p3, w)


def _block_scan(coef, inp, reverse):
    r = coef.shape[0]
    row = lax.broadcasted_iota(jnp.int32, coef.shape, 0)
    a, b = coef, inp
    d = 1
    while d < r:
        if reverse:
            keep = row < r - d
            a_sh, b_sh = pltpu.roll(a, r - d, 0), pltpu.roll(b, r - d, 0)
        else:
            keep = row >= d
            a_sh, b_sh = pltpu.roll(a, d, 0), pltpu.roll(b, d, 0)
        b = b + a * jnp.where(keep, b_sh, 0.0)
        a = a * jnp.where(keep, a_sh, 1.0)
        d *= 2
    return a, b


def _lru_scan(a3, b3, reverse):
    nbatch, s, w = a3.shape
    ts = min(LRU_ROWS, s)
    nblk = s // ts
    edge = 0 if reverse else ts - 1

    def body(a_ref, b_ref, h_ref, carry):
        @pl.when(pl.program_id(1) == 0)
        def _():
            carry[...] = jnp.zeros_like(carry)

        ca, hb = _block_scan(a_ref[...], b_ref[...], reverse)
        h = hb + ca * carry[0:1, :]
        h_ref[...] = h
        carry[0:1, :] = h[edge:edge + 1, :]

    blk = pl.BlockSpec((None, ts, w), (lambda n, i: (n, nblk - 1 - i, 0)) if reverse else (lambda n, i: (n, i, 0)))
    return _pcall(body, name=f"lru_scan_r{int(reverse)}", grid=(nbatch, nblk), in_specs=[blk, blk], out_specs=blk,
                  out_shape=jax.ShapeDtypeStruct((nbatch, s, w), F32), scratch_shapes=[pltpu.VMEM((8, w), F32)],
                  compiler_params=_params())(a3, b3)


def _lru_scan_bwd(a3, h3, dh3, reverse):
    nbatch, s, w = a3.shape
    ts = min(LRU_ROWS, s)
    nblk = s // ts
    nb8 = s // 8
    tpb = ts // 8

    def body(a_ref, aa_ref, h_ref, hh_ref, dh_ref, g_ref, da_ref, carry):
        i = pl.program_id(1)

        @pl.when(i == 0)
        def _():
            carry[...] = jnp.zeros_like(carry)

        a, h = a_ref[...], h_ref[...]
        row = lax.broadcasted_iota(jnp.int32, a.shape, 0)
        if reverse:
            a_edge = jnp.where(i == 0, 0.0, aa_ref[7:8, :])
            c = jnp.where(row == 0, a_edge, pltpu.roll(a, 1, 0))
            h_edge = jnp.where(i == nblk - 1, 0.0, hh_ref[0:1, :])
            h_sh = jnp.where(row == ts - 1, h_edge, pltpu.roll(h, ts - 1, 0))
        else:
            a_edge = jnp.where(i == 0, 0.0, aa_ref[0:1, :])
            c = jnp.where(row == ts - 1, a_edge, pltpu.roll(a, ts - 1, 0))
            h_edge = jnp.where(i == nblk - 1, 0.0, hh_ref[7:8, :])
            h_sh = jnp.where(row == 0, h_edge, pltpu.roll(h, 1, 0))
        cc, gb = _block_scan(c, dh_ref[...], not reverse)
        g = gb + cc * carry[0:1, :]
        g_ref[...] = g
        carry[0:1, :] = g[ts - 1:ts, :] if reverse else g[0:1, :]
        da_ref[...] = g * h_sh

    if reverse:
        bi = lambda i: i
    else:
        bi = lambda i: nblk - 1 - i
    blk = pl.BlockSpec((None, ts, w), lambda n, i: (n, bi(i), 0))
    before = pl.BlockSpec((None, 8, w), lambda n, i: (n, jnp.maximum(bi(i) * tpb - 1, 0), 0))
    after = pl.BlockSpec((None, 8, w), lambda n, i: (n, jnp.minimum((bi(i) + 1) * tpb, nb8 - 1), 0))
    a_tile, h_tile = (before, after) if reverse else (after, before)
    return _pcall(body, name=f"lru_scan_bwd_r{int(reverse)}", grid=(nbatch, nblk), in_specs=[blk, a_tile, blk, h_tile, blk],
                  out_specs=(blk, blk),
                  out_shape=(jax.ShapeDtypeStruct((nbatch, s, w), F32), jax.ShapeDtypeStruct((nbatch, s, w), F32)),
                  scratch_shapes=[pltpu.VMEM((8, w), F32)], compiler_params=_params())(a3, a3, h3, h3, dh3)


def _head_expand(lane0):
    return (jnp.right_shift(lax.broadcasted_iota(jnp.int32, (128, 1024), 1), HEAD_SHIFT) + lane0
            == lax.broadcasted_iota(jnp.int32, (128, 1024), 0)).astype(F32)


def _head_reduce(lane0):
    return (jnp.right_shift(lax.broadcasted_iota(jnp.int32, (1024, 128), 0), HEAD_SHIFT) + lane0
            == lax.broadcasted_iota(jnp.int32, (1024, 128), 1)).astype(F32)


def _time_mask(q, reverse):
    ri = lax.broadcasted_iota(jnp.int32, (q, q), 0)
    ci = lax.broadcasted_iota(jnp.int32, (q, q), 1)
    return (ri <= ci) if reverse else (ri >= ci)


def _ssd_common(xs_ref, bc_ref, dt_ref, al_ref, reverse, lane0):
    q = xs_ref.shape[0]
    edge = 0 if reverse else q - 1
    dt = dt_ref[...]
    a = -jnp.exp(al_ref[...])
    mask = _time_mask(q, reverse)
    expand = _head_expand(lane0)
    cum = _dot(mask.astype(F32), dt * a, precision=HI)
    cum_x = _dot(cum, expand, precision=HI)
    dt_x = _dot(dt, expand, precision=HI)
    last_x = cum_x[edge:edge + 1, :]
    xs = xs_ref[...]
    bc = bc_ref[...]
    return dict(q=q, edge=edge, lane0=lane0, dt=dt, a=a, mask=mask, cum_t=cum.T, cum_x=cum_x, dt_x=dt_x, xs=xs,
                v=xs * dt_x, e_c=jnp.exp(cum_x), w=jnp.exp(last_x - cum_x), e_l=jnp.exp(last_x),
                bm=bc[:, :512], cm=bc[:, 512:])


def _ssd_decay(c, h):
    row = c["lane0"] + h
    seg = c["cum_x"][:, h * SSD_HEADDIM:h * SSD_HEADDIM + 1] - c["cum_t"][row:row + 1, :]
    return jnp.where(c["mask"], jnp.exp(jnp.minimum(seg, 0.0)), 0.0)


def _head_masks():
    lane = jnp.right_shift(lax.broadcasted_iota(jnp.int32, (1, 256), 1), HEAD_SHIFT)
    return [lane == e for e in range(4)]


def _ssd_fwd(xbc3, dt3, alog, reverse):
    nbatch, s, _ = xbc3.shape
    q = min(SSD_CHUNK, s)
    nc = s // q
    lane0 = SSD_HEADS * int(reverse)

    def body(xs_ref, bc_ref, dt_ref, al_ref, y_ref, st_ref, st):
        @pl.when(pl.program_id(1) == 0)
        def _():
            st[...] = jnp.zeros_like(st)

        st_ref[...] = st[...]
        c = _ssd_common(xs_ref, bc_ref, dt_ref, al_ref, reverse, lane0)
        hm = _head_masks()
        for g in range(SSD_GROUPS):
            sl = slice(g * 256, (g + 1) * 256)
            cg, bg = _mx(c["cm"][:, g * 128:(g + 1) * 128]), _mx(c["bm"][:, g * 128:(g + 1) * 128])
            cb = _dot(cg, bg, _NT)
            vg = c["v"][:, sl]
            s0 = st[:, sl]
            yg = _dot(cg, _mx(s0)) * c["e_c"][:, sl]
            for e in range(4):
                m = _ssd_decay(c, 4 * g + e) * cb
                yg = yg + _dot(_mx(m), _mx(jnp.where(hm[e], vg, 0.0)))
            y_ref[:, sl] = yg
            st[:, sl] = c["e_l"][:, sl] * s0 + _dot(bg, _mx(vg * c["w"][:, sl]), _TN)

    ck = (lambda i: nc - 1 - i) if reverse else (lambda i: i)
    xs_spec = pl.BlockSpec((None, q, 1024), lambda n, i: (n, ck(i), 0))
    bc_spec = pl.BlockSpec((None, q, 1024), lambda n, i: (n, ck(i), 1))
    dt_spec = pl.BlockSpec((None, q, 128), lambda n, i: (n, ck(i), 0))
    al_spec = pl.BlockSpec((1, 128), lambda n, i: (0, 0))
    st_spec = pl.BlockSpec((None, None, 128, 1024), lambda n, i: (n, ck(i), 0, 0))
    return _pcall(body, name=f"ssd_fwd_r{int(reverse)}", grid=(nbatch, nc), in_specs=[xs_spec, bc_spec, dt_spec, al_spec],
                  out_specs=(xs_spec, st_spec),
                  out_shape=(jax.ShapeDtypeStruct((nbatch, s, 1024), F32), jax.ShapeDtypeStruct((nbatch, nc, 128, 1024), F32)),
                  scratch_shapes=[pltpu.VMEM((128, 1024), F32)], compiler_params=_params())(xbc3, xbc3, dt3, alog)


def _ssd_bwd(xbc3, dt3, alog, st4, dy3, reverse):
    nbatch, s, _ = xbc3.shape
    q = min(SSD_CHUNK, s)
    nc = s // q
    lane0 = SSD_HEADS * int(reverse)

    def body(xs_ref, bc_ref, dt_ref, al_ref, st0_ref, dy_ref, dxs_ref, dbc_ref, ddt_ref, dal_ref, dst):
        n, i = pl.program_id(0), pl.program_id(1)

        @pl.when(i == 0)
        def _():
            dst[...] = jnp.zeros_like(dst)

        @pl.when((i == 0) & (n == 0))
        def _():
            dal_ref[...] = jnp.zeros_like(dal_ref)

        c = _ssd_common(xs_ref, bc_ref, dt_ref, al_ref, reverse, lane0)
        hm = _head_masks()
        reduce_m = _head_reduce(lane0)
        s0_all, ds1_all, dy = st0_ref[...], dst[...], dy_ref[...]
        lane = lax.broadcasted_iota(jnp.int32, (q, 128), 1)
        sub = lax.broadcasted_iota(jnp.int32, (128, q), 0)
        rowacc = jnp.zeros((q, 128), F32)
        colacc_t = jnp.zeros((128, q), F32)
        dv_l, yst_l, dvbar_l, dk_l, dc_l = [], [], [], [], []
        for g in range(SSD_GROUPS):
            sl = slice(g * 256, (g + 1) * 256)
            cg, bg = _mx(c["cm"][:, g * 128:(g + 1) * 128]), _mx(c["bm"][:, g * 128:(g + 1) * 128])
            cb = _dot(cg, bg, _NT)
            vg, dyg, wg, ecg = c["v"][:, sl], dy[:, sl], c["w"][:, sl], c["e_c"][:, sl]
            s0, ds1 = _mx(s0_all[:, sl]), _mx(ds1_all[:, sl])
            dye = _mx(dyg * ecg)
            yst_l.append(_dot(cg, s0) * ecg)
            dcg = _dot(dye, s0, _NT)
            dst[:, sl] = c["e_l"][:, sl] * ds1_all[:, sl] + _dot(cg, dye, _TN)
            vbar = _mx(vg * wg)
            dvbar = _dot(bg, ds1)
            dvbar_l.append(dvbar)
            dvg = dvbar * wg
            dkg = _dot(vbar, ds1, _NT)
            for e in range(4):
                h = 4 * g + e
                m = _ssd_decay(c, h)
                dyh, vh = _mx(jnp.where(hm[e], dyg, 0.0)), _mx(jnp.where(hm[e], vg, 0.0))
                dvg = dvg + _dot(_mx(m * cb), dyh, _TN)
                dcb = _dot(dyh, vh, _NT) * m
                dcbb = _mx(dcb)
                dcg = dcg + _dot(dcbb, bg)
                dkg = dkg + _dot(dcbb, cg, _TN)
                wmat = dcb * cb
                rowacc = jnp.where(lane == lane0 + h, jnp.sum(wmat, axis=1, keepdims=True), rowacc)
                colacc_t = jnp.where(sub == lane0 + h, jnp.sum(wmat, axis=0, keepdims=True), colacc_t)
            dv_l.append(dvg)
            dk_l.append(dkg)
            dc_l.append(dcg)
        dv = jnp.concatenate(dv_l, axis=1)
        yst = jnp.concatenate(yst_l, axis=1)
        dvbar = jnp.concatenate(dvbar_l, axis=1)
        t1 = _dot(dy * yst, reduce_m, precision=HI)
        t2 = _dot(c["v"] * c["w"] * dvbar, reduce_m, precision=HI)
        dlast = jnp.sum(t2, axis=0, keepdims=True) + _dot(
            c["e_l"] * jnp.sum(ds1_all * s0_all, axis=0, keepdims=True), reduce_m, precision=HI)
        dcum = rowacc - colacc_t.T + t1 - t2
        dcum = dcum + jnp.where(lax.broadcasted_iota(jnp.int32, (q, 128), 0) == c["edge"], dlast, 0.0)
        dda = _dot(c["mask"].astype(F32), dcum, _TN, precision=HI)
        ddt_ref[...] = dda * c["a"] + _dot(dv * c["xs"], reduce_m, precision=HI)
        dal_ref[...] += jnp.sum(dda * c["dt"], axis=0, keepdims=True) * c["a"]
        dxs_ref[...] = dv * c["dt_x"]
        dbc_ref[...] = jnp.concatenate(dk_l + dc_l, axis=1)

    ck = (lambda i: i) if reverse else (lambda i: nc - 1 - i)
    xs_spec = pl.BlockSpec((None, q, 1024), lambda n, i: (n, ck(i), 0))
    bc_spec = pl.BlockSpec((None, q, 1024), lambda n, i: (n, ck(i), 1))
    dt_spec = pl.BlockSpec((None, q, 128), lambda n, i: (n, ck(i), 0))
    al_spec = pl.BlockSpec((1, 128), lambda n, i: (0, 0))
    st_spec = pl.BlockSpec((None, None, 128, 1024), lambda n, i: (n, ck(i), 0, 0))
    return _pcall(body, name=f"ssd_bwd_r{int(reverse)}", grid=(nbatch, nc),
                  in_specs=[xs_spec, bc_spec, dt_spec, al_spec, st_spec, xs_spec],
                  out_specs=(xs_spec, xs_spec, dt_spec, al_spec),
                  out_shape=(jax.ShapeDtypeStruct((nbatch, s, 1024), F32), jax.ShapeDtypeStruct((nbatch, s, 1024), F32),
                             jax.ShapeDtypeStruct((nbatch, s, 128), F32), jax.ShapeDtypeStruct((1, 128), F32)),
                  scratch_shapes=[pltpu.VMEM((128, 1024), F32)], compiler_params=_params())(xbc3, xbc3, dt3, alog, st4, dy3)


def _gla_block(q_ref, k_ref, g_ref, b, reverse):
    bq = g_ref.shape[1]
    nsub = bq // HGRN_SUB
    edge = 0 if reverse else bq - 1
    ri = lax.broadcasted_iota(jnp.int32, (bq, bq), 0)
    ci = lax.broadcasted_iota(jnp.int32, (bq, bq), 1)
    rb, cb = jnp.right_shift(ri, HGRN_SUB_SHIFT), jnp.right_shift(ci, HGRN_SUB_SHIFT)
    mask = (ri <= ci) if reverse else (ri >= ci)
    m_within = (mask & (rb == cb)).astype(F32)
    m_before = ((cb > rb) if reverse else (cb < rb)).astype(F32)
    g = g_ref[b]
    bl = _dot(m_within, g, precision=HI)
    c = _dot(m_before, g, precision=HI)
    last = c[edge:edge + 1, :] + bl[edge:edge + 1, :]
    ebl, enbl, ec, elc = jnp.exp(bl), jnp.exp(-bl), jnp.exp(c), jnp.exp(last - c)
    qh = q_ref[b] * HGRN_SCALE * ebl
    kh = k_ref[b] * enbl
    blk = jnp.right_shift(lax.broadcasted_iota(jnp.int32, (bq, 1), 0), HGRN_SUB_SHIFT)
    scale = []
    for i in range(nsub):
        valid = (blk >= i) if reverse else (blk <= i)
        ex = jnp.where(valid, c[i * HGRN_SUB:i * HGRN_SUB + 1, :] - c, 0.0)
        scale.append(jnp.where(valid, jnp.exp(ex), 0.0))
    return dict(bq=bq, nsub=nsub, edge=edge, mask=mask, m_within=m_within, m_before=m_before, ebl=ebl, enbl=enbl, ec=ec,
                elc=elc, e_l=jnp.exp(last), qh=qh, qt=qh * ec, kh=kh, kb=kh * elc, scale=scale)


def _gla_scores(c, hs):
    keys = [_mx(c["kh"][:, hs] * c["scale"][i][:, hs]) for i in range(c["nsub"])]
    rows = [_dot(_mx(c["qh"][i * HGRN_SUB:(i + 1) * HGRN_SUB, hs]), keys[i], _NT) for i in range(c["nsub"])]
    return jnp.where(c["mask"], jnp.concatenate(rows, axis=0), 0.0), keys


def _gla_specs(nbatch, s, w, reverse_order):
    bq = min(HGRN_BLOCK, s)
    nblk = s // bq
    bi = (lambda i: nblk - 1 - i) if reverse_order else (lambda i: i)
    col = lambda cb: pl.BlockSpec((nbatch, bq, w), lambda i: (0, bi(i), cb))
    st_spec = pl.BlockSpec((nbatch, None, 128, w), lambda i: (0, bi(i), 0, 0))
    return bq, nblk, col, st_spec


def _gla_fwd(proj3, k3, g3, reverse):
    nbatch, s, w = k3.shape
    bq, nblk, col, st_spec = _gla_specs(nbatch, s, w, reverse)

    def body(q_ref, k_ref, v_ref, g_ref, o_ref, st_ref, st):
        @pl.when(pl.program_id(0) == 0)
        def _():
            st[...] = jnp.zeros_like(st)

        for b in range(nbatch):
            st_ref[b] = st[b]
            c = _gla_block(q_ref, k_ref, g_ref, b, reverse)
            v = v_ref[b]
            for h in range(HGRN_HEADS):
                hs = slice(h * 128, (h + 1) * 128)
                att, _ = _gla_scores(c, hs)
                vb = _mx(v[:, hs])
                s0 = st[b, :, hs]
                o_ref[b, :, hs] = _dot(_mx(att), vb) + _dot(_mx(c["qt"][:, hs]), _mx(s0), _NT)
                st[b, :, hs] = s0 * c["e_l"][:, hs] + _dot(vb, _mx(c["kb"][:, hs]), _TN)

    return _pcall(body, name=f"gla_fwd_r{int(reverse)}", grid=(nblk,), in_specs=[col(0), col(0), col(3), col(0)],
                  out_specs=(col(0), st_spec),
                  out_shape=(jax.ShapeDtypeStruct((nbatch, s, w), F32), jax.ShapeDtypeStruct((nbatch, nblk, 128, w), F32)),
                  scratch_shapes=[pltpu.VMEM((nbatch, 128, w), F32)], compiler_params=_params())(proj3, k3, proj3, g3)


def _gla_bwd(proj3, k3, g3, st4, do3, reverse):
    nbatch, s, w = k3.shape
    bq, nblk, col, st_spec = _gla_specs(nbatch, s, w, not reverse)

    def body(q_ref, k_ref, v_ref, g_ref, st_ref, do_ref, dq_ref, dk_ref, dv_ref, dg_ref, dst):
        @pl.when(pl.program_id(0) == 0)
        def _():
            dst[...] = jnp.zeros_like(dst)

        row = lax.broadcasted_iota(jnp.int32, (bq, 128), 0)
        for b in range(nbatch):
            c = _gla_block(q_ref, k_ref, g_ref, b, reverse)
            s0_all, ds1_all = st_ref[b], dst[b]
            v, dy = v_ref[b], do_ref[b]
            dbl_l, dc_l = [], []
            for h in range(HGRN_HEADS):
                hs = slice(h * 128, (h + 1) * 128)
                att, keys = _gla_scores(c, hs)
                qh, qt, kh, kb = c["qh"][:, hs], c["qt"][:, hs], c["kh"][:, hs], c["kb"][:, hs]
                vb, dyb = _mx(v[:, hs]), _mx(dy[:, hs])
                s0, ds1 = s0_all[:, hs], ds1_all[:, hs]
                datt = _mx(jnp.where(c["mask"], _dot(dyb, vb, _NT), 0.0))
                dqh_rows = []
                dkh = jnp.zeros((bq, 128), F32)
                dc = jnp.zeros((bq, 128), F32)
                for i in range(c["nsub"]):
                    rs = slice(i * HGRN_SUB, (i + 1) * HGRN_SUB)
                    dqh_rows.append(_dot(datt[rs], keys[i]))
                    dki = _dot(datt[rs], _mx(qh[rs]), _TN)
                    sc = c["scale"][i][:, hs]
                    dkh = dkh + dki * sc
                    dex = dki * (kh * sc)
                    dc = dc - dex + jnp.where(row == i * HGRN_SUB, jnp.sum(dex, axis=0, keepdims=True), 0.0)
                dqt = _dot(dyb, _mx(s0))
                dkb = _dot(vb, _mx(ds1))
                dv_ref[b, :, hs] = _dot(_mx(att), dyb, _TN) + _dot(_mx(kb), _mx(ds1), _NT)
                dst[b, :, hs] = c["e_l"][:, hs] * ds1 + _dot(dyb, _mx(qt), _TN)
                dqh = jnp.concatenate(dqh_rows, axis=0) + dqt * c["ec"][:, hs]
                dkh = dkh + dkb * c["elc"][:, hs]
                kbk = dkb * kb
                dlast = jnp.sum(kbk, axis=0, keepdims=True) + c["e_l"][:, hs] * jnp.sum(ds1 * s0, axis=0, keepdims=True)
                at_edge = jnp.where(row == c["edge"], dlast, 0.0)
                dc_l.append(dc + dqt * qt - kbk + at_edge)
                dbl_l.append(dqh * qh - dkh * kh + at_edge)
                dq_ref[b, :, hs] = dqh * c["ebl"][:, hs] * HGRN_SCALE
                dk_ref[b, :, hs] = dkh * c["enbl"][:, hs]
            dg_ref[b] = (_dot(c["m_within"], jnp.concatenate(dbl_l, axis=1), _TN, precision=HI)
                         + _dot(c["m_before"], jnp.concatenate(dc_l, axis=1), _TN, precision=HI))

    shp = jax.ShapeDtypeStruct((nbatch, s, w), F32)
    return _pcall(body, name=f"gla_bwd_r{int(reverse)}", grid=(nblk,),
                  in_specs=[col(0), col(0), col(3), col(0), st_spec, col(0)],
                  out_specs=(col(0),) * 4, out_shape=(shp,) * 4,
                  scratch_shapes=[pltpu.VMEM((nbatch, 128, w), F32)], compiler_params=_params())(proj3, k3, proj3, g3, st4, do3)


DIRS = (False, True)


def _block_diag(w):
    eye = jnp.eye(16, dtype=w.dtype)
    return (eye[:, None, :, None] * w[:, :, None, :]).reshape(1024, 1024)


def _diag_blocks(m):
    m4 = m.reshape(16, 64, 16, 64)
    return jnp.stack([m4[i, :, i, :] for i in range(16)], axis=0)


def _pad_lanes(v, n=128):
    return jnp.pad(v, [(0, 0)] * (v.ndim - 1) + [(0, n - v.shape[-1])])


def _mlp_fwd(tag, x, nw, w1, w2):
    (h,) = _pw_fwd(f"{tag}_norm", _f_norm, [(x, 0)], [(nw, 0)], [BF16], 1024, 1)
    a, r = _mm(f"{tag}_up", h, w1, "nn", relu2=True)
    return _mm(f"{tag}_down", r, w2, "nn", res=x), (h, a, r)


def _mlp_bwd(tag, x, nw, w1, w2, saved, dxo):
    h, a, r = saved
    dw2 = _mm(f"{tag}_dw2", r, dxo, "tn")
    da = _mm(f"{tag}_da", dxo, w2, "nt", relu2_of=a, out_dtype=BF16)
    dw1 = _mm(f"{tag}_dw1", h, da, "tn", col_shards=4)
    dh = _mm(f"{tag}_dh", da, w1, "nt")
    (dx,), (dnw,) = _pw_bwd(f"{tag}_dnorm", _f_norm, [(x, 0)], [(nw, 0)], [dh], 1024, 1, [0], adds={0: dxo})
    return dx, dw1, dw2, dnw


def _split_in0(pieces, dt_piece):
    tm = 256

    def body(p0, p1, p2, p3, p4, p5, o_ref):
        full = jnp.concatenate([p0[...], p1[...], p2[...], p3[...], p4[...], p5[:, :32]], axis=1)
        for j in range(4):
            o_ref[j] = full[:, 1288 * j:1288 * (j + 1)]

    blk = pl.BlockSpec((tm, 1024), lambda i: (i, 0))
    return _pcall(body, name="split_in0", grid=(1024 // tm,), in_specs=[blk] * 5 + [pl.BlockSpec((tm, 128), lambda i: (i, 0))],
                  out_specs=pl.BlockSpec((4, tm, 1288), lambda i: (0, i, 0)),
                  out_shape=jax.ShapeDtypeStruct((4, 1024, 1288), F32), compiler_params=_params())(*pieces, dt_piece)


def _assemble_in0(shards):
    tm = 256

    def body(s_ref, m_ref, d_ref):
        full = jnp.concatenate([s_ref[j] for j in range(4)], axis=1)
        m_ref[...] = full[:, :5120]
        d_ref[...] = jnp.concatenate([full[:, 5120:5152], jnp.zeros((tm, 96), full.dtype)], axis=1)

    return _pcall(body, name="assemble_in0", grid=(1024 // tm,), in_specs=[pl.BlockSpec((4, tm, 1288), lambda i: (0, i, 0))],
                  out_specs=(pl.BlockSpec((tm, 5120), lambda i: (i, 0)), pl.BlockSpec((tm, 128), lambda i: (i, 0))),
                  out_shape=(jax.ShapeDtypeStruct((1024, 5120), shards.dtype), jax.ShapeDtypeStruct((1024, 128), shards.dtype)),
                  compiler_params=_params())(shards)


def _local_step(x3, tgt3, w, w_main0, w_dt0):
    nb, s, d = x3.shape
    t = nb * s
    x0 = x3.reshape(t, d)
    tgt = tgt3.reshape(t, d)
    grads = {}
    row = lambda v: v.reshape(1, -1)
    to3 = lambda v: v.reshape(nb, s, v.shape[-1])
    to2 = lambda v: v.reshape(-1, v.shape[-1])

    conv_w, conv_b = w["even_conv_w"][0], row(w["even_conv_b"][0])
    nmix0 = row(w["norm_mix"][0])
    (h0,) = _pw_fwd("l0_norm", _f_norm, [(x0, 0)], [(nmix0, 0)], [BF16], 1024, 1)
    proj0 = _mm("l0_proj", h0, w_main0, "nn")
    dt_raw = _mm("l0_proj_dt", h0, w_dt0, "nn")
    conv = to2(_conv_fwd(to3(proj0), conv_w, conv_b, 3))
    (xbc,) = _pw_fwd("l0_silu", _f_silu, [(conv, 0)], [], [F32], 1024, 2)
    dt_bias = _pad_lanes(w["ssd_dt_bias"][0].reshape(1, 32))
    (dt,) = _pw_fwd("l0_dt", _f_softplus, [(dt_raw, 0)], [(dt_bias, 0)], [F32], 128, 1)
    dt3, xbc3 = to3(dt), to3(xbc)
    alog = _pad_lanes(w["ssd_a_log"][0].reshape(1, 32))
    ssd = [_ssd_fwd(xbc3, dt3, alog, r) for r in DIRS]
    yf, yb = to2(ssd[0][0]), to2(ssd[1][0])
    dskip = jnp.repeat(w["ssd_d"][0], SSD_HEADDIM).reshape(1, 1024)
    snw = row(w["ssd_norm_w"][0])
    ssd_ins = [(yf, 0), (yb, 0), (xbc, 0), (proj0, 3)]
    (ya,) = _pw_fwd("l0_ssd_post", _f_ssd_post, ssd_ins, [(dskip, 0), (snw, 0)], [BF16], 1024, 1, groups=SSD_GROUPS)
    u_lru = conv[:, 2048:]
    w_gates = [_block_diag(w[k][0, r]).astype(MXU_DTYPE) for r in range(2) for k in ("lru_w_a", "lru_w_x")]
    pre = [_mm(f"l0_lru_pre{i}", u_lru, wg, "nn") for i, wg in enumerate(w_gates)]
    lru_par = [[(row(w[k][0, r]), 0) for k in ("lru_b_a", "lru_b_x", "lru_lambda")] for r in range(2)]
    lru_ins = [[(pre[2 * r], 0), (pre[2 * r + 1], 0), (u_lru, 0)] for r in range(2)]
    ab = [_pw_fwd(f"l0_lru_gates{r}", _f_lru_gates, lru_ins[r], lru_par[r], [F32, F32], 1024, 1) for r in range(2)]
    hs = [_lru_scan(to3(ab[r][0]), to3(ab[r][1]), DIRS[r]) for r in range(2)]
    lru_post_ins = [(to2(hs[0]), 0), (to2(hs[1]), 0), (proj0, 4)]
    (ybm,) = _pw_fwd("l0_lru_post", _f_lru_post, lru_post_ins, [], [BF16], 1024, 1)
    w_out0 = w["even_w_out"][0]
    x1 = _mm("l0_out_a", ya, w_out0[:1024], "nn", res=x0)
    x1 = _mm("l0_out_b", ybm, w_out0[1024:], "nn", res=x1)
    nmlp0 = row(w["norm_mlp"][0])
    x2, mlp0 = _mlp_fwd("l0_mlp", x1, nmlp0, w["mlp_w1"][0], w["mlp_w2"][0])

    w_in1 = w["odd_w_in"][0]
    nmix1 = row(w["norm_mix"][1])
    (h1,) = _pw_fwd("l1_norm", _f_norm, [(x2, 0)], [(nmix1, 0)], [BF16], 1024, 1)
    proj1 = _mm("l1_proj", h1, w_in1, "nn")
    proj1_3 = to3(proj1)
    lb0, lb1 = row(w["hgrn_lb_logits"][0]), row(w["hgrn_lb_logits"][1])
    kg = [_pw_fwd(f"l1_hgrn_pre{r}", _f_hgrn_pre, [(proj1, 1 + r)], [(lb0, 0), (lb1, 0)], [F32, F32], 1024, 1)
          for r in range(2)]
    gla = [_gla_fwd(proj1_3, to3(kg[r][0]), to3(kg[r][1]), DIRS[r]) for r in range(2)]
    hnw = row(w["hgrn_norm_w"][0])
    hpost_ins = [(to2(gla[0][0]), 0), (to2(gla[1][0]), 0), (proj1, 4)]
    (yo,) = _pw_fwd("l1_hgrn_post", _f_hgrn_post, hpost_ins, [(hnw, 0)], [BF16], 1024, 1, groups=HGRN_HEADS)
    w_out1 = w["odd_w_out"][0]
    x3_ = _mm("l1_out", yo, w_out1, "nn", res=x2)
    nmlp1 = row(w["norm_mlp"][1])
    x4, mlp1 = _mlp_fwd("l1_mlp", x3_, nmlp1, w["mlp_w1"][1], w["mlp_w2"][1])

    dx4, dnf, loss = _loss_head(x4, tgt, row(w["norm_final"]))
    grads["norm_final"] = dnf.reshape(-1)

    dx3, dw1_1, dw2_1, dnmlp1 = _mlp_bwd("l1_mlp", x3_, nmlp1, w["mlp_w1"][1], w["mlp_w2"][1], mlp1, dx4)
    big = {"odd_w_out": _mm("l1_dwout", yo, dx3, "tn").reshape(4, 256, 1024)}
    dyo = _mm("l1_dyo", dx3, w_out1, "nt")
    (do, dgate1), (dhnw,) = _pw_bwd("l1_hgrn_post_b", _f_hgrn_post, hpost_ins, [(hnw, 0)], [dyo], 1024, 1, [0, 2],
                                    out_dtypes=[F32, BF16], groups=HGRN_HEADS)
    grads["hgrn_norm_w"] = dhnw
    do3 = to3(do)
    gb = [_gla_bwd(proj1_3, to3(kg[r][0]), to3(kg[r][1]), gla[r][1], do3, DIRS[r]) for r in range(2)]
    (dq,) = _pw_fwd("l1_dq", _f_add2, [(to2(gb[0][0]), 0), (to2(gb[1][0]), 0)], [], [BF16], 1024, 1)
    (dvv,) = _pw_fwd("l1_dv", _f_add2, [(to2(gb[0][2]), 0), (to2(gb[1][2]), 0)], [], [BF16], 1024, 1)
    dfr, dl0, dl1 = [], [], []
    for r in range(2):
        (df,), (a0, a1) = _pw_bwd(f"l1_hgrn_pre_b{r}", _f_hgrn_pre, [(proj1, 1 + r)], [(lb0, 0), (lb1, 0)],
                                  [to2(gb[r][1]), to2(gb[r][3])], 1024, 1, [0], out_dtypes=[BF16])
        dfr.append(df)
        dl0.append(a0)
        dl1.append(a1)
    grads["hgrn_lb_logits"] = jnp.concatenate([dl0[0] + dl0[1], dl1[0] + dl1[1]], axis=0)
    dparts1 = [dq, dfr[0], dfr[1], dvv, dgate1]
    dwin1 = jnp.concatenate([_mm(f"l1_dwin{i}", h1, dp, "tn") for i, dp in enumerate(dparts1)], axis=1)
    big["odd_w_in"] = dwin1.reshape(1024, 4, 1280).transpose(1, 0, 2)
    dh1 = _mm_sum_nt("l1_dh", dparts1, [w_in1[:, i * 1024:(i + 1) * 1024] for i in range(5)])
    (dx2,), (dnmix1,) = _pw_bwd("l1_dnorm", _f_norm, [(x2, 0)], [(nmix1, 0)], [dh1], 1024, 1, [0], adds={0: dx3})

    dx1, dw1_0, dw2_0, dnmlp0 = _mlp_bwd("l0_mlp", x1, nmlp0, w["mlp_w1"][0], w["mlp_w2"][0], mlp0, dx2)
    big["mlp_w1"] = jnp.concatenate([dw1_0, dw1_1], axis=1)
    big["mlp_w2"] = jnp.concatenate([dw2_0.reshape(4, 1024, 1024), dw2_1.reshape(4, 1024, 1024)], axis=1)
    grads["norm_mlp"] = jnp.concatenate([dnmlp0, dnmlp1], axis=0)
    big["even_w_out"] = jnp.concatenate([_mm("l0_dwout_a", ya, dx1, "tn"), _mm("l0_dwout_b", ybm, dx1, "tn")],
                                        axis=0).reshape(4, 512, 1024)
    dya = _mm("l0_dya", dx1, w_out0[:1024], "nt")
    dyb = _mm("l0_dyb", dx1, w_out0[1024:], "nt")
    (dh, dgate0), _ = _pw_bwd("l0_lru_post_b", _f_lru_post, lru_post_ins, [], [dyb], 1024, 1, [0, 2], out_dtypes=[F32, BF16])
    dh3 = to3(dh)
    dpre, du_parts, dlru = [], [], {k: [] for k in ("lru_b_a", "lru_b_x", "lru_lambda")}
    for r in range(2):
        g_r, da_r = _lru_scan_bwd(to3(ab[r][0]), hs[r], dh3, DIRS[r])
        (dpa, dpx, du_r), (dba, dbx, dlam) = _pw_bwd(f"l0_lru_gates_b{r}", _f_lru_gates, lru_ins[r], lru_par[r],
                                                     [to2(da_r), to2(g_r)], 1024, 1, [0, 1, 2],
                                                     out_dtypes=[BF16, BF16, F32])
        dpre += [dpa, dpx]
        du_parts.append(du_r)
        dlru["lru_b_a"].append(dba)
        dlru["lru_b_x"].append(dbx)
        dlru["lru_lambda"].append(dlam)
    for k, v in dlru.items():
        grads[k] = jnp.concatenate(v, axis=0)[None]
    dwg = [_diag_blocks(_mm(f"l0_dwgate{i}", u_lru, dp, "tn")) for i, dp in enumerate(dpre)]
    grads["lru_w_a"] = jnp.stack([dwg[0], dwg[2]])[None]
    grads["lru_w_x"] = jnp.stack([dwg[1], dwg[3]])[None]
    du_gate = _mm_sum_nt("l0_du_gate", dpre, w_gates)
    (du,) = _pw_fwd("l0_du", _f_add3, [(du_parts[0], 0), (du_parts[1], 0), (du_gate, 0)], [], [F32], 1024, 1)
    (dy, dxs_skip, dz), (ddskip, dsnw) = _pw_bwd("l0_ssd_post_b", _f_ssd_post, ssd_ins, [(dskip, 0), (snw, 0)], [dya],
                                                 1024, 1, [0, 2, 3], out_dtypes=[F32, F32, BF16], groups=SSD_GROUPS)
    grads["ssd_d"] = ddskip.reshape(SSD_HEADS, SSD_HEADDIM).sum(axis=1)[None]
    grads["ssd_norm_w"] = dsnw
    dy3 = to3(dy)
    sb = [_ssd_bwd(xbc3, dt3, alog, ssd[r][1], dy3, DIRS[r]) for r in range(2)]
    grads["ssd_a_log"] = (sb[0][3] + sb[1][3])[:, :32].reshape(1, 2, 16)
    (dxs,) = _pw_fwd("l0_dxs", _f_add3, [(to2(sb[0][0]), 0), (to2(sb[1][0]), 0), (dxs_skip, 0)], [], [F32], 1024, 1)
    (dbc,) = _pw_fwd("l0_dbc", _f_add2, [(to2(sb[0][1]), 0), (to2(sb[1][1]), 0)], [], [F32], 1024, 1)
    dxbc = jnp.concatenate([dxs, dbc], axis=1)
    (dconv_a,), _ = _pw_bwd("l0_silu_b", _f_silu, [(conv, 0)], [], [dxbc], 1024, 2, [0])
    (ddt,) = _pw_fwd("l0_ddt", _f_add2, [(to2(sb[0][2]), 0), (to2(sb[1][2]), 0)], [], [F32], 128, 1)
    (ddt_raw,), (ddtb,) = _pw_bwd("l0_dt_b", _f_softplus, [(dt_raw, 0)], [(dt_bias, 0)], [ddt], 128, 1, [0])
    grads["ssd_dt_bias"] = ddtb[:, :32].reshape(1, 2, 16)
    dconv = jnp.concatenate([dconv_a, du], axis=1)
    dproj_c, dcw = _conv_bwd(to3(dconv), to3(proj0), conv_w, 3)
    grads["even_conv_w"] = dcw[:4][None]
    grads["even_conv_b"] = dcw[4:5]
    dparts0 = [to2(dproj_c)[:, :1024], to2(dproj_c)[:, 1024:2048], to2(dproj_c)[:, 2048:], dz, dgate0]
    dwin0 = [_mm(f"l0_dwin{i}", h0, dp, "tn") for i, dp in enumerate(dparts0)]
    big["even_w_in"] = _split_in0(dwin0, _mm("l0_dwin_dt", h0, ddt_raw, "tn"))
    dh0 = _mm_sum_nt("l0_dh", dparts0 + [ddt_raw], [w_main0[:, i * 1024:(i + 1) * 1024] for i in range(5)] + [w_dt0])
    (dx0,), (dnmix0,) = _pw_bwd("l0_dnorm", _f_norm, [(x0, 0)], [(nmix0, 0)], [dh0], 1024, 1, [0], adds={0: dx1})
    grads["norm_mix"] = jnp.concatenate([dnmix0, dnmix1], axis=0)
    return loss, dx0.reshape(nb, s, d), grads, [big[n] for n in BIG]


ANY = pl.BlockSpec(memory_space=pl.ANY)


def _place():
    return lax.axis_index("x"), lax.axis_index("y"), lax.axis_index("c")


def _remote(src, dst, send_sems, recv_sems, k, to):
    return pltpu.make_async_remote_copy(src_ref=src, dst_ref=dst, send_sem=send_sems.at[k], recv_sem=recv_sems.at[k],
                                        device_id=to, device_id_type=MESH)


def _gather_chips(shards):
    n = len(shards)
    halves = [s.shape[0] // 2 for s in shards]

    def body(*refs):
        x_refs, out_refs = refs[:n], refs[n:2 * n]
        send_sems, recv_sems = refs[2 * n:]
        x, y, c = _place()
        sibling = (x, y, 1 - c)
        chips = [(1 - x, y), (x, 1 - y), (1 - x, 1 - y)]

        def blk(t, px, py, hc):
            return out_refs[t].at[2 * px + py, pl.ds(hc * halves[t], halves[t]), :]

        def src(t):
            return x_refs[t].at[pl.ds(c * halves[t], halves[t]), :]

        first = [_remote(src(t), blk(t, x, y, c), send_sems, recv_sems, 6 * t + j, (*chip, c))
                 for t in range(n) for j, chip in enumerate(chips)]
        for cp in first:
            cp.start()
        passed = []
        for t in range(n):
            for j, chip in enumerate(chips):
                _remote(src(t), blk(t, *chip, c), send_sems, recv_sems, 6 * t + j, (*chip, c)).wait_recv()
                cp = _remote(blk(t, *chip, c), blk(t, *chip, c), send_sems, recv_sems, 6 * t + 3 + j, sibling)
                cp.start()
                passed.append(cp)
        for t in range(n):
            for j, chip in enumerate(chips):
                _remote(src(t), blk(t, *chip, 1 - c), send_sems, recv_sems, 6 * t + 3 + j, sibling).wait_recv()
        for cp in first + passed:
            cp.wait_send()

    return _pcall(body, name="gather_weights", in_specs=[ANY] * n, out_specs=(ANY,) * n,
                  out_shape=tuple(jax.ShapeDtypeStruct((4,) + s.shape, s.dtype) for s in shards),
                  scratch_shapes=[pltpu.SemaphoreType.DMA((6 * n,)), pltpu.SemaphoreType.DMA((6 * n,))],
                  compiler_params=_params())(*shards)


def _pair_swap(gps):
    n = len(gps)
    halves = [g.shape[1] // 2 for g in gps]

    def body(*refs):
        g_refs, land_refs = refs[:n], refs[n:2 * n]
        send_sems, recv_sems = refs[2 * n:]
        x, y, c = _place()
        cps = [_remote(g_refs[t].at[j, pl.ds((1 - c) * halves[t], halves[t]), :], land_refs[t].at[j], send_sems, recv_sems,
                       4 * t + j, (x, y, 1 - c)) for t in range(n) for j in range(4)]
        for cp in cps:
            cp.start()
        for cp in cps:
            cp.wait()

    return _pcall(body, name="grad_pair_swap", in_specs=[ANY] * n, out_specs=(ANY,) * n,
                  out_shape=tuple(jax.ShapeDtypeStruct((4, h, g.shape[2]), F32) for g, h in zip(gps, halves)),
                  scratch_shapes=[pltpu.SemaphoreType.DMA((4 * n,)), pltpu.SemaphoreType.DMA((4 * n,))],
                  compiler_params=_params())(*gps)


def _pair_add(name, gp, land, cidx):
    _, half, cols = land.shape
    tr = _tile(half, 512)
    nh = half // tr

    def body(c_ref, g_ref, l_ref, o_ref):
        o_ref[...] = (g_ref[...] + l_ref[...]).astype(o_ref.dtype)

    grid_spec = pltpu.PrefetchScalarGridSpec(
        num_scalar_prefetch=1, grid=(4, nh),
        in_specs=[pl.BlockSpec((None, tr, cols), lambda j, i, c: (j, c[0] * nh + i, 0)),
                  pl.BlockSpec((None, tr, cols), lambda j, i, c: (j, i, 0))],
        out_specs=pl.BlockSpec((None, tr, cols), lambda j, i, c: (j, i, 0)))
    return _pcall(body, name=f"pair_add_{name}", grid_spec=grid_spec, out_shape=jax.ShapeDtypeStruct((4, half, cols), BF16),
                  compiler_params=_params())(cidx, gp, land)


def _chip_scatter(css):
    n = len(css)

    def body(*refs):
        s_refs, land_refs = refs[:n], refs[n:2 * n]
        send_sems, recv_sems = refs[2 * n:]
        x, y, c = _place()
        me = 2 * x + y
        chips = [(1 - x, y), (x, 1 - y), (1 - x, 1 - y)]
        cps = [_remote(s_refs[t].at[2 * px + py], land_refs[t].at[me], send_sems, recv_sems, 3 * t + j, (px, py, c))
               for t in range(n) for j, (px, py) in enumerate(chips)]
        for cp in cps:
            cp.start()
        for t in range(n):
            for j, (px, py) in enumerate(chips):
                _remote(s_refs[t].at[me], land_refs[t].at[2 * px + py], send_sems, recv_sems, 3 * t + j, (px, py, c)).wait_recv()
        for cp in cps:
            cp.wait_send()

    return _pcall(body, name="grad_chip_scatter", in_specs=[ANY] * n, out_specs=(ANY,) * n,
                  out_shape=tuple(jax.ShapeDtypeStruct(s.shape, s.dtype) for s in css),
                  scratch_shapes=[pltpu.SemaphoreType.DMA((3 * n,)), pltpu.SemaphoreType.DMA((3 * n,))],
                  compiler_params=_params())(*css)


def _chip_sum(name, land):
    _, half, cols = land.shape
    tr = _tile(half, 512)

    def body(l_ref, o_ref):
        o_ref[...] = ((l_ref[0].astype(F32) + l_ref[1].astype(F32)) + l_ref[2].astype(F32)) + l_ref[3].astype(F32)

    return _pcall(body, name=f"chip_sum_{name}", grid=(half // tr,),
                  in_specs=[pl.BlockSpec((4, tr, cols), lambda i: (0, i, 0))],
                  out_specs=pl.BlockSpec((tr, cols), lambda i: (i, 0)),
                  out_shape=jax.ShapeDtypeStruct((half, cols), F32), compiler_params=_params())(land)


def _pair_join(reds):
    n = len(reds)

    def body(*refs):
        r_refs, out_refs = refs[:n], refs[n:2 * n]
        send_sems, recv_sems = refs[2 * n:]
        x, y, c = _place()
        cps = [_remote(r_refs[t], out_refs[t].at[c], send_sems, recv_sems, t, (x, y, 1 - c)) for t in range(n)]
        for cp in cps:
            cp.start()
        for t in range(n):
            _remote(r_refs[t], out_refs[t].at[1 - c], send_sems, recv_sems, t, (x, y, 1 - c)).wait_recv()
        for cp in cps:
            cp.wait_send()

    return _pcall(body, name="grad_pair_join", in_specs=[ANY] * n, out_specs=(ANY,) * n,
                  out_shape=tuple(jax.ShapeDtypeStruct((2,) + r.shape, F32) for r in reds),
                  scratch_shapes=[pltpu.SemaphoreType.DMA((n,)), pltpu.SemaphoreType.DMA((n,))],
                  compiler_params=_params())(*reds)


def _adamw(name, g, w, m, v):
    rows, cols = g.shape
    tr = _tile(rows, 512)

    def body(g_ref, w_ref, m_ref, v_ref, d_ref, mo_ref, vo_ref):
        gv = g_ref[...]
        mn = ADAM_B1 * m_ref[...] + (1.0 - ADAM_B1) * gv
        vn = ADAM_B2 * v_ref[...] + (1.0 - ADAM_B2) * jnp.square(gv)
        m_hat = mn / (1.0 - ADAM_B1 ** ADAM_STEP)
        v_hat = vn / (1.0 - ADAM_B2 ** ADAM_STEP)
        d_ref[...] = -ADAM_LR * (m_hat / (jnp.sqrt(v_hat) + ADAM_EPS) + ADAM_WD * w_ref[...])
        mo_ref[...] = mn
        vo_ref[...] = vn

    blk = pl.BlockSpec((tr, cols), lambda i: (i, 0))
    shp = jax.ShapeDtypeStruct((rows, cols), F32)
    return _pcall(body, name=f"adamw_{name}", grid=(rows // tr,), in_specs=[blk] * 4, out_specs=(blk,) * 3,
                  out_shape=(shp,) * 3, compiler_params=_params())(g, w, m, v)


def _pack(pieces, rows, dtype):
    flat = jnp.concatenate([p.reshape(-1).astype(dtype) for p in pieces])
    return jnp.pad(flat, (0, rows * PACK_COLS - flat.shape[0])).reshape(rows, PACK_COLS)


def _unpack(pack, shapes):
    flat = pack.reshape(-1)
    out, off = [], 0
    for shp in shapes:
        n = math.prod(shp)
        out.append(flat[off:off + n].reshape(shp))
        off += n
    return out


def _shard_of(full, axis, j):
    n = full.shape[axis] // 4
    return lax.slice_in_dim(full, j * n, (j + 1) * n, axis=axis)


def kernel(x, even_w_in, even_conv_w, even_conv_b, ssd_a_log, ssd_dt_bias, ssd_d, ssd_norm_w, lru_w_a, lru_b_a, lru_w_x, lru_b_x, lru_lambda, even_w_out, odd_w_in, hgrn_lb_logits, hgrn_norm_w, odd_w_out, norm_mix, norm_mlp, mlp_w1, mlp_w2, norm_final, loss_target, m_even_w_in, m_even_conv_w, m_even_conv_b, m_ssd_a_log, m_ssd_dt_bias, m_ssd_d, m_ssd_norm_w, m_lru_w_a, m_lru_b_a, m_lru_w_x, m_lru_b_x, m_lru_lambda, m_even_w_out, m_odd_w_in, m_hgrn_lb_logits, m_hgrn_norm_w, m_odd_w_out, m_norm_mix, m_norm_mlp, m_mlp_w1, m_mlp_w2, m_norm_final, v_even_w_in, v_even_conv_w, v_even_conv_b, v_ssd_a_log, v_ssd_dt_bias, v_ssd_d, v_ssd_norm_w, v_lru_w_a, v_lru_b_a, v_lru_w_x, v_lru_b_x, v_lru_lambda, v_even_w_out, v_odd_w_in, v_hgrn_lb_logits, v_hgrn_norm_w, v_odd_w_out, v_norm_mix, v_norm_mlp, v_mlp_w1, v_mlp_w2, v_norm_final):
    names = [n for n, _, _, _ in WEIGHTS]
    w_loc = dict(zip(names, (even_w_in, even_conv_w, even_conv_b, ssd_a_log, ssd_dt_bias, ssd_d, ssd_norm_w, lru_w_a, lru_b_a, lru_w_x, lru_b_x, lru_lambda, even_w_out, odd_w_in, hgrn_lb_logits, hgrn_norm_w, odd_w_out, norm_mix, norm_mlp, mlp_w1, mlp_w2, norm_final)))
    m_loc = dict(zip(names, (m_even_w_in, m_even_conv_w, m_even_conv_b, m_ssd_a_log, m_ssd_dt_bias, m_ssd_d, m_ssd_norm_w, m_lru_w_a, m_lru_b_a, m_lru_w_x, m_lru_b_x, m_lru_lambda, m_even_w_out, m_odd_w_in, m_hgrn_lb_logits, m_hgrn_norm_w, m_odd_w_out, m_norm_mix, m_norm_mlp, m_mlp_w1, m_mlp_w2, m_norm_final)))
    v_loc = dict(zip(names, (v_even_w_in, v_even_conv_w, v_even_conv_b, v_ssd_a_log, v_ssd_dt_bias, v_ssd_d, v_ssd_norm_w, v_lru_w_a, v_lru_b_a, v_lru_w_x, v_lru_b_x, v_lru_lambda, v_even_w_out, v_odd_w_in, v_hgrn_lb_logits, v_hgrn_norm_w, v_odd_w_out, v_norm_mix, v_norm_mlp, v_mlp_w1, v_mlp_w2, v_norm_final)))
    spec = {n: (blk, full, ax) for n, blk, full, ax in WEIGHTS}

    small = [n for n in names if n not in BIG]
    two_d = lambda n, v: v.reshape(BIG_2D[n])

    me = 2 * lax.axis_index("x") + lax.axis_index("y")
    cc = lax.axis_index("c")
    put = lambda whole, part, k: lax.dynamic_update_slice_in_dim(whole, part[None], k, axis=0)
    own = [two_d(n, w_loc[n]).astype(BF16) for n in BIG] + [_pack([w_loc[n] for n in SMALL_SHARDED], 16, F32)]
    g_in0, g_out0, g_in1, g_out1, g_w1, g_w2, g_small = [put(g, o, me) for g, o in zip(_gather_chips(own), own)]
    w_main0, w_dt0 = _assemble_in0(g_in0)
    w_full = {n: w_loc[n] for n in names if spec[n][2] is None}
    w_full["even_w_out"] = g_out0.reshape(1, 2048, 1024)
    w_full["odd_w_in"] = jnp.concatenate([g_in1[j] for j in range(4)], axis=1)[None]
    w_full["odd_w_out"] = g_out1.reshape(1, 1024, 1024)
    w_full["mlp_w1"] = jnp.stack([jnp.concatenate([g_w1[j, l * 1024:(l + 1) * 1024] for j in range(4)], axis=1) for l in range(2)])
    w_full["mlp_w2"] = jnp.stack([jnp.concatenate([g_w2[j, l * 1024:(l + 1) * 1024] for j in range(4)], axis=0) for l in range(2)])
    shards = [_unpack(g_small[j], [spec[n][0] for n in SMALL_SHARDED]) for j in range(4)]
    for i, n in enumerate(SMALL_SHARDED):
        w_full[n] = jnp.concatenate([shards[j][i] for j in range(4)], axis=spec[n][2])

    loss_vec, grad_x, grads, big = _local_step(x, loss_target, w_full, w_main0, w_dt0)
    loss = lax.psum(loss_vec[0, 0], ("x", "y", "c"))

    def dest_pack(j):
        return _pack([grads[n].reshape(spec[n][1]) if spec[n][2] is None else _shard_of(grads[n].reshape(spec[n][1]), spec[n][2], j)
                      for n in small], SMALL_ROWS, F32)

    tensors = big + [jnp.stack([dest_pack(j) for j in range(4)])]
    tags = list(BIG) + ["small"]
    cidx = cc.astype(jnp.int32).reshape(1)
    chip_sums = [_pair_add(tag, g, land, cidx) for tag, g, land in zip(tags, tensors, _pair_swap(tensors))]
    landed = [put(land, lax.dynamic_index_in_dim(cs, me, axis=0, keepdims=False), me)
              for land, cs in zip(_chip_scatter(chip_sums), chip_sums)]
    halves = [_chip_sum(tag, land) for tag, land in zip(tags, landed)]
    reduced = [put(r, h, cc).reshape(-1, r.shape[-1]) for r, h in zip(_pair_join(halves), halves)]

    outs = {}
    for n, g in zip(BIG, reduced[:-1]):
        res = (g, *_adamw(n, g, two_d(n, w_loc[n]), two_d(n, m_loc[n]), two_d(n, v_loc[n])))
        outs[n] = [r.reshape(spec[n][0]) for r in res]
    blocks = [spec[n][0] for n in small]
    wp, mp, vp = (_pack([src[n] for n in small], SMALL_ROWS, F32) for src in (w_loc, m_loc, v_loc))
    res = (reduced[-1], *_adamw("small", reduced[-1], wp, mp, vp))
    unpacked = [_unpack(r, blocks) for r in res]
    for i, n in enumerate(small):
        outs[n] = [u[i] for u in unpacked]
    return (loss, grad_x, *[outs[n][k] for k in range(4) for n in names])
```

```python
import functools
import math

import jax
import jax.numpy as jnp
from jax import lax
from jax.experimental import pallas as pl
from jax.experimental.pallas import tpu as pltpu

F32 = jnp.float32
BF16 = jnp.bfloat16
MXU_DTYPE = jnp.bfloat16
MESH = pl.DeviceIdType.MESH

D_MODEL = 1024
EPS = 1e-6
SSD_HEADS = 16
SSD_HEADDIM = 64
HEAD_SHIFT = 6
SSD_GROUPS = 4
SSD_STATE = 128
SSD_CHUNK = 128
LRU_C = 8.0
LRU_ROWS = 256
HGRN_HEADS = 8
HGRN_HEADDIM = 128
HGRN_SUB = 32
HGRN_SUB_SHIFT = 5
HGRN_BLOCK = 128
HGRN_SCALE = HGRN_HEADDIM ** -0.5
CONV_ROWS = 512

ADAM_LR = 0.001
ADAM_B1 = 0.9
ADAM_B2 = 0.999
ADAM_EPS = 1e-08
ADAM_WD = 0.01
ADAM_STEP = 10

VMEM_LIMIT = 56 * 1024 * 1024
PACK_COLS = 1024
SMALL_ROWS = 288

WEIGHTS = (
    ("even_w_in", (1, 1024, 1288), (1, 1024, 5152), 2),
    ("even_conv_w", (1, 4, 768), (1, 4, 3072), 2),
    ("even_conv_b", (1, 3072), (1, 3072), None),
    ("ssd_a_log", (1, 2, 16), (1, 2, 16), None),
    ("ssd_dt_bias", (1, 2, 16), (1, 2, 16), None),
    ("ssd_d", (1, 16), (1, 16), None),
    ("ssd_norm_w", (1, 1024), (1, 1024), None),
    ("lru_w_a", (1, 2, 16, 64, 64), (1, 2, 16, 64, 64), None),
    ("lru_b_a", (1, 2, 256), (1, 2, 1024), 2),
    ("lru_w_x", (1, 2, 16, 64, 64), (1, 2, 16, 64, 64), None),
    ("lru_b_x", (1, 2, 256), (1, 2, 1024), 2),
    ("lru_lambda", (1, 2, 256), (1, 2, 1024), 2),
    ("even_w_out", (1, 512, 1024), (1, 2048, 1024), 1),
    ("odd_w_in", (1, 1024, 1280), (1, 1024, 5120), 2),
    ("hgrn_lb_logits", (2, 1024), (2, 1024), None),
    ("hgrn_norm_w", (1, 256), (1, 1024), 1),
    ("odd_w_out", (1, 256, 1024), (1, 1024, 1024), 1),
    ("norm_mix", (2, 1024), (2, 1024), None),
    ("norm_mlp", (2, 1024), (2, 1024), None),
    ("mlp_w1", (2, 1024, 1024), (2, 1024, 4096), 2),
    ("mlp_w2", (2, 1024, 1024), (2, 4096, 1024), 1),
    ("norm_final", (1024,), (1024,), None),
)
BIG = ("even_w_in", "even_w_out", "odd_w_in", "odd_w_out", "mlp_w1", "mlp_w2")
BIG_2D = {"even_w_in": (1024, 1288), "even_w_out": (512, 1024), "odd_w_in": (1024, 1280), "odd_w_out": (256, 1024),
          "mlp_w1": (2048, 1024), "mlp_w2": (2048, 1024)}
SMALL_SHARDED = ("even_conv_w", "lru_b_a", "lru_b_x", "lru_lambda", "hgrn_norm_w")


def _pcall(body, **kw):
    return pl.pallas_call(body, **kw)


def _params(**kw):
    return pltpu.CompilerParams(vmem_limit_bytes=VMEM_LIMIT, **kw)


def _tile(n, pref):
    if n <= pref:
        return n
    t = (pref // 128) * 128
    while n % t:
        t -= 128
    return t


def _dot(a, b, dims=(((1,), (0,)), ((), ()))):
    return lax.dot_general(a, b, dims, preferred_element_type=F32)


_NN = (((1,), (0,)), ((), ()))
_NT = (((1,), (1,)), ((), ()))
_TN = (((0,), (0,)), ((), ()))


def _mx(v):
    return v.astype(MXU_DTYPE)


def _dot01(a, b, dims=_NN, *, split, terms):
    acc, rest = None, (a if split == "a" else b)
    for _ in range(terms):
        piece = _mx(rest)
        part = _dot(piece, _mx(b), dims) if split == "a" else _dot(_mx(a), piece, dims)
        acc = part if acc is None else acc + part
        rest = rest - piece.astype(F32)
    return acc


def _mm(name, a, b, mode, *, out_dtype=F32, res=None, relu2=False, relu2_of=None, col_shards=1):
    if mode == "nn":
        (m, kk), (_, n) = a.shape, b.shape
    elif mode == "nt":
        (m, kk), (n, _) = a.shape, b.shape
    else:
        (kk, m), (_, n) = a.shape, b.shape
    assert res is None or relu2_of is None
    tk_pref = 1024
    if mode == "tn" and a.dtype.itemsize == 2 and b.dtype.itemsize == 2:
        tk_pref = 2048
    tm, tn, tk = _tile(m, 1024), _tile(n // col_shards, 1024), _tile(kk, tk_pref)
    nk = kk // tk
    dims = {"nn": _NN, "nt": _NT, "tn": _TN}[mode]
    a_spec = pl.BlockSpec((tk, tm), lambda i, j, k: (k, i)) if mode == "tn" else pl.BlockSpec((tm, tk), lambda i, j, k: (i, k))
    b_spec = pl.BlockSpec((tn, tk), lambda i, j, k: (j, k)) if mode == "nt" else pl.BlockSpec((tk, tn), lambda i, j, k: (k, j))
    o_spec = pl.BlockSpec((tm, tn), lambda i, j, k: (i, j))
    o_shape = (m, n)
    if col_shards > 1:
        assert tn * col_shards == n and res is None and not relu2
        o_spec = pl.BlockSpec((None, tm, tn), lambda i, j, k: (j, i, 0))
        o_shape = (col_shards, m, tn)
    extra = res if res is not None else relu2_of
    has_res = extra is not None

    def body(*refs):
        a_ref, b_ref = refs[0], refs[1]
        res_ref = refs[2] if has_res else None
        outs = refs[2 + has_res:2 + has_res + 1 + relu2]

        def finish(r):
            if res is not None:
                r = r + res_ref[...]
            if relu2_of is not None:
                r = r * (2.0 * jnp.maximum(res_ref[...], 0.0))
            if relu2:
                outs[0][...] = r
                outs[1][...] = jnp.square(jnp.maximum(r, 0.0)).astype(outs[1].dtype)
            else:
                outs[0][...] = r.astype(outs[0].dtype)

        prod = _dot(_mx(a_ref[...]), _mx(b_ref[...]), dims)
        if nk == 1:
            finish(prod)
            return
        acc = refs[-1]
        k = pl.program_id(2)

        @pl.when(k == 0)
        def _():
            acc[...] = prod

        @pl.when(k > 0)
        def _():
            acc[...] += prod

        @pl.when(k == nk - 1)
        def _():
            finish(acc[...])

    in_specs = [a_spec, b_spec] + ([o_spec] if has_res else [])
    if relu2:
        out_shape = (jax.ShapeDtypeStruct((m, n), F32), jax.ShapeDtypeStruct((m, n), BF16))
        out_specs = (o_spec, o_spec)
    else:
        out_shape = jax.ShapeDtypeStruct(o_shape, out_dtype)
        out_specs = o_spec
    args = (a, b) + ((extra,) if has_res else ())
    return _pcall(body, name=name, grid=(m // tm, n // tn, nk), in_specs=in_specs, out_specs=out_specs,
                  out_shape=out_shape, scratch_shapes=[pltpu.VMEM((tm, tn), F32)] if nk > 1 else [],
                  compiler_params=_params())(*args)


def _mm_sum_nt(name, parts, wblocks):
    m, n, npart = parts[0].shape[0], wblocks[0].shape[0], len(parts)
    tm, tn = _tile(m, 512), _tile(n, 1024)

    def body(*refs):
        acc = _dot(_mx(refs[0][...]), _mx(refs[npart][...]), _NT)
        for k in range(1, npart):
            acc = acc + _dot(_mx(refs[k][...]), _mx(refs[npart + k][...]), _NT)
        refs[-1][...] = acc

    in_specs = [pl.BlockSpec((tm, p.shape[1]), lambda i, j: (i, 0)) for p in parts]
    in_specs += [pl.BlockSpec((tn, w.shape[1]), lambda i, j: (j, 0)) for w in wblocks]
    return _pcall(body, name=name, grid=(m // tm, n // tn), in_specs=in_specs, out_specs=pl.BlockSpec((tm, tn), lambda i, j: (i, j)),
                  out_shape=jax.ShapeDtypeStruct((m, n), F32), compiler_params=_params())(*parts, *wblocks)


def _pw_fwd(name, f, ins, params, out_dtypes, tc, ncol, tm=256, groups=1):
    t = ins[0][0].shape[0]
    tm = min(tm, t)
    ni, npar = len(ins), len(params)
    gw = tc // groups

    def body(*refs):
        for g in range(groups):
            sl = slice(g * gw, (g + 1) * gw)
            vals = f(*[r[:, sl].astype(F32) for r in refs[:ni]], *[r[:, sl] for r in refs[ni:ni + npar]])
            for o, v in zip(refs[ni + npar:], vals):
                o[:, sl] = v.astype(o.dtype)

    in_specs = [pl.BlockSpec((tm, tc), lambda j, i, off=off: (i, off + j)) for _, off in ins]
    in_specs += [pl.BlockSpec((1, tc), lambda j, i, off=off: (0, off + j)) for _, off in params]
    out_specs = tuple(pl.BlockSpec((tm, tc), lambda j, i: (i, j)) for _ in out_dtypes)
    out_shape = tuple(jax.ShapeDtypeStruct((t, ncol * tc), d) for d in out_dtypes)
    return _pcall(body, name=name, grid=(ncol, t // tm), in_specs=in_specs, out_specs=out_specs, out_shape=out_shape,
                  compiler_params=_params())(*[a for a, _ in ins], *[p for p, _ in params])


def _pw_bwd(name, f, ins, params, douts, tc, ncol, want, adds=None, tm=256, out_dtypes=None, groups=1):
    adds = adds or {}
    out_dtypes = out_dtypes or [F32] * len(want)
    t = ins[0][0].shape[0]
    tm = min(tm, t)
    ni, npar, nd, na = len(ins), len(params), len(douts), len(adds)
    add_keys = sorted(adds)
    gw = tc // groups

    def body(*refs):
        in_refs, p_refs = refs[:ni], refs[ni:ni + npar]
        d_refs = refs[ni + npar:ni + npar + nd]
        a_refs = refs[ni + npar + nd:ni + npar + nd + na]
        o_refs = refs[ni + npar + nd + na:]
        for p in range(npar):
            @pl.when(pl.program_id(1) == 0)
            def _(o=o_refs[len(want) + p]):
                o[...] = jnp.zeros_like(o)

        for g in range(groups):
            sl = slice(g * gw, (g + 1) * gw)
            _, vjp = jax.vjp(f, *[r[:, sl].astype(F32) for r in in_refs], *[r[:, sl] for r in p_refs])
            cts = vjp(tuple(d[:, sl].astype(F32) for d in d_refs))
            for o, kidx in zip(o_refs[:len(want)], want):
                v = cts[kidx]
                if kidx in adds:
                    v = v + a_refs[add_keys.index(kidx)][:, sl]
                o[:, sl] = v.astype(o.dtype)
            for p in range(npar):
                o_refs[len(want) + p][:, sl] += cts[ni + p]

    in_specs = [pl.BlockSpec((tm, tc), lambda j, i, off=off: (i, off + j)) for _, off in ins]
    in_specs += [pl.BlockSpec((1, tc), lambda j, i, off=off: (0, off + j)) for _, off in params]
    in_specs += [pl.BlockSpec((tm, tc), lambda j, i: (i, j)) for _ in range(nd + na)]
    out_specs = tuple([pl.BlockSpec((tm, tc), lambda j, i: (i, j)) for _ in want]
                      + [pl.BlockSpec((1, tc), lambda j, i: (0, j)) for _ in params])
    out_shape = tuple([jax.ShapeDtypeStruct((t, ncol * tc), dt) for dt in out_dtypes]
                      + [jax.ShapeDtypeStruct((1, ncol * tc), F32) for _ in params])
    res = _pcall(body, name=name, grid=(ncol, t // tm), in_specs=in_specs, out_specs=out_specs, out_shape=out_shape,
                 compiler_params=_params())(*[a for a, _ in ins], *[p for p, _ in params], *douts, *[adds[k] for k in add_keys])
    return list(res[:len(want)]), list(res[len(want):])


def _rms(x, g):
    return (x * lax.rsqrt(jnp.mean(x * x, axis=-1, keepdims=True) + EPS)) * g


def _f_norm(x, g):
    return (_rms(x, g),)


def _f_silu(c):
    return (jax.nn.silu(c),)


def _f_softplus(d, b):
    return (jax.nn.softplus(d + b),)


def _f_add2(a, b):
    return (a + b,)


def _f_add3(a, b, c):
    return (a + b + c,)


def _f_ssd_post(yf, yb, xs, z, dskip, nw):
    u = (yf + yb + dskip * xs) * jax.nn.silu(z)
    return (_rms(u, nw),)


def _neg_expm1(v):
    t = jnp.tanh(0.5 * v)
    return -2.0 * t / (1.0 - t)


def _f_lru_gates(pre_a, pre_x, u, ba, bx, lam):
    rg = jax.nn.sigmoid(pre_a + ba)
    ig = jax.nn.sigmoid(pre_x + bx)
    log_a = -LRU_C * rg * jax.nn.softplus(-lam)
    return jnp.exp(log_a), jnp.sqrt(_neg_expm1(2.0 * log_a)) * (ig * u)


def _f_lru_post(hf, hb, gate):
    return ((hf + hb) * jax.nn.gelu(gate),)


def _f_hgrn_pre(fr, l0, l1):
    lb = jax.nn.sigmoid(l1 - l0)
    k = (1.0 - lb) * jax.nn.sigmoid(-fr)
    return k, jnp.log1p(-k)


def _f_hgrn_post(of, ob, gate, nw):
    return (_rms(of + ob, nw) * jax.nn.silu(gate),)


def _loss_head(x, tgt, g, tm=256):
    t, d = x.shape
    tm = min(tm, t)

    def body(x_ref, t_ref, g_ref, dx_ref, dg_ref, loss_ref):
        tv = t_ref[...]

        def lf(xv, gv):
            return 0.5 * jnp.sum(jnp.mean(jnp.square(_rms(xv, gv) - tv), axis=-1))

        val, vjp = jax.vjp(lf, x_ref[...], g_ref[...])
        dx, dg = vjp(jnp.ones((), F32))
        dx_ref[...] = dx

        @pl.when(pl.program_id(0) == 0)
        def _():
            dg_ref[...] = jnp.zeros_like(dg_ref)
            loss_ref[...] = jnp.zeros_like(loss_ref)

        dg_ref[...] += dg
        loss_ref[...] += jnp.full(loss_ref.shape, val, F32)

    row = pl.BlockSpec((tm, d), lambda i: (i, 0))
    vec = pl.BlockSpec((1, d), lambda i: (0, 0))
    return _pcall(body, name="loss_head", grid=(t // tm,), in_specs=[row, row, vec],
                  out_specs=(row, vec, pl.BlockSpec((1, 128), lambda i: (0, 0))),
                  out_shape=(jax.ShapeDtypeStruct((t, d), F32), jax.ShapeDtypeStruct((1, d), F32),
                             jax.ShapeDtypeStruct((1, 128), F32)), compiler_params=_params())(x, tgt, g)


def _shifted(x, d, prev, nxt, first, last):
    r = x.shape[0]
    row = lax.broadcasted_iota(jnp.int32, x.shape, 0)
    if d < 0:
        out = pltpu.roll(x, -d, 0)
        for q in range(-d):
            pv = jnp.where(first, 0.0, prev[8 + d + q:8 + d + q + 1, :])
            out = jnp.where(row == q, pv, out)
        return out
    out = pltpu.roll(x, r - d, 0)
    for q in range(d):
        nv = jnp.where(last, 0.0, nxt[q:q + 1, :])
        out = jnp.where(row == r - d + q, nv, out)
    return out


def _halo_specs(ts, tc, s):
    nb8 = s // 8
    cur = pl.BlockSpec((None, ts, tc), lambda n, i, j: (n, i, j))
    prev = pl.BlockSpec((None, 8, tc), lambda n, i, j: (n, jnp.maximum(i * (ts // 8) - 1, 0), j))
    nxt = pl.BlockSpec((None, 8, tc), lambda n, i, j: (n, jnp.minimum((i + 1) * (ts // 8), nb8 - 1), j))
    return cur, prev, nxt


def _conv_fwd(p3, w, b, ncol, tc=1024):
    nbatch, s, _ = p3.shape
    ts = min(CONV_ROWS, s)
    nblk = s // ts

    def body(x_ref, pv_ref, nx_ref, w_ref, b_ref, o_ref):
        i = pl.program_id(1)
        first, last = i == 0, i == nblk - 1
        x, pv, nx = x_ref[...], pv_ref[...], nx_ref[...]
        wv = w_ref[...]
        out = b_ref[...] + wv[1:2] * x
        out = out + wv[0:1] * _shifted(x, -1, pv, nx, first, last)
        out = out + wv[2:3] * _shifted(x, 1, pv, nx, first, last)
        out = out + wv[3:4] * _shifted(x, 2, pv, nx, first, last)
        o_ref[...] = out

    cur, prev, nxt = _halo_specs(ts, tc, s)
    return _pcall(body, name="conv_fwd", grid=(nbatch, nblk, ncol),
                  in_specs=[cur, prev, nxt, pl.BlockSpec((4, tc), lambda n, i, j: (0, j)),
                            pl.BlockSpec((1, tc), lambda n, i, j: (0, j))],
                  out_specs=cur, out_shape=jax.ShapeDtypeStruct((nbatch, s, ncol * tc), F32),
                  compiler_params=_params())(p3, p3, p3, w, b)


def _conv_bwd(dc3, p3, w, col):
    nbatch, s, tc = dc3.shape
    ts = min(CONV_ROWS, s)
    nblk = s // ts

    def body(d_ref, dpv_ref, dnx_ref, x_ref, pv_ref, nx_ref, w_ref, dx_ref, dw_ref):
        n, i = pl.program_id(0), pl.program_id(1)
        first, last = i == 0, i == nblk - 1
        d, dpv, dnx = d_ref[...], dpv_ref[...], dnx_ref[...]
        x, pv, nx = x_ref[...], pv_ref[...], nx_ref[...]
        wv = w_ref[...]
        dx = wv[1:2] * d
        dx = dx + wv[0:1] * _shifted(d, 1, dpv, dnx, first, last)
        dx = dx + wv[2:3] * _shifted(d, -1, dpv, dnx, first, last)
        dx = dx + wv[3:4] * _shifted(d, -2, dpv, dnx, first, last)
        dx_ref[...] = dx.astype(dx_ref.dtype)

        @pl.when((n == 0) & (i == 0))
        def _():
            dw_ref[...] = jnp.zeros_like(dw_ref)

        dw_ref[0:1, :] += jnp.sum(d * _shifted(x, -1, pv, nx, first, last), axis=0, keepdims=True)
        dw_ref[1:2, :] += jnp.sum(d * x, axis=0, keepdims=True)
        dw_ref[2:3, :] += jnp.sum(d * _shifted(x, 1, pv, nx, first, last), axis=0, keepdims=True)
        dw_ref[3:4, :] += jnp.sum(d * _shifted(x, 2, pv, nx, first, last), axis=0, keepdims=True)
        dw_ref[4:5, :] += jnp.sum(d, axis=0, keepdims=True)

    nb8 = s // 8

    def specs(j):
        cur = pl.BlockSpec((None, ts, tc), lambda n, i: (n, i, j))
        prev = pl.BlockSpec((None, 8, tc), lambda n, i: (n, jnp.maximum(i * (ts // 8) - 1, 0), j))
        nxt = pl.BlockSpec((None, 8, tc), lambda n, i: (n, jnp.minimum((i + 1) * (ts // 8), nb8 - 1), j))
        return [cur, prev, nxt]

    return _pcall(body, name=f"conv_bwd{col}", grid=(nbatch, nblk),
                  in_specs=specs(0) + specs(col) + [pl.BlockSpec((4, tc), lambda n, i: (0, col))],
                  out_specs=(specs(0)[0], pl.BlockSpec((8, tc), lambda n, i: (0, 0))),
                  out_shape=(jax.ShapeDtypeStruct((nbatch, s, tc), BF16), jax.ShapeDtypeStruct((8, tc), F32)),
                  compiler_params=_params())(dc3, dc3, dc3, p3, p3, p3, w)


def _block_scan(coef, inp, reverse):
    r = coef.shape[0]
    row = lax.broadcasted_iota(jnp.int32, coef.shape, 0)
    a, b = coef, inp
    d = 1
    while d < r:
        if reverse:
            keep = row < r - d
            a_sh, b_sh = pltpu.roll(a, r - d, 0), pltpu.roll(b, r - d, 0)
        else:
            keep = row >= d
            a_sh, b_sh = pltpu.roll(a, d, 0), pltpu.roll(b, d, 0)
        b = b + a * jnp.where(keep, b_sh, 0.0)
        a = a * jnp.where(keep, a_sh, 1.0)
        d *= 2
    return a, b


def _lru_scan(a3, b3, reverse):
    nbatch, s, w = a3.shape
    ts = min(LRU_ROWS, s)
    nblk = s // ts
    edge = 0 if reverse else ts - 1

    def body(a_ref, b_ref, h_ref, carry):
        @pl.when(pl.program_id(1) == 0)
        def _():
            carry[...] = jnp.zeros_like(carry)

        ca, hb = _block_scan(a_ref[...], b_ref[...], reverse)
        h = hb + ca * carry[0:1, :]
        h_ref[...] = h
        carry[0:1, :] = h[edge:edge + 1, :]

    blk = pl.BlockSpec((None, ts, w), (lambda n, i: (n, nblk - 1 - i, 0)) if reverse else (lambda n, i: (n, i, 0)))
    return _pcall(body, name=f"lru_scan_r{int(reverse)}", grid=(nbatch, nblk), in_specs=[blk, blk], out_specs=blk,
                  out_shape=jax.ShapeDtypeStruct((nbatch, s, w), F32), scratch_shapes=[pltpu.VMEM((8, w), F32)],
                  compiler_params=_params())(a3, b3)


def _lru_scan_bwd(a3, h3, dh3, reverse):
    nbatch, s, w = a3.shape
    ts = min(LRU_ROWS, s)
    nblk = s // ts
    nb8 = s // 8
    tpb = ts // 8

    def body(a_ref, aa_ref, h_ref, hh_ref, dh_ref, g_ref, da_ref, carry):
        i = pl.program_id(1)

        @pl.when(i == 0)
        def _():
            carry[...] = jnp.zeros_like(carry)

        a, h = a_ref[...], h_ref[...]
        row = lax.broadcasted_iota(jnp.int32, a.shape, 0)
        if reverse:
            a_edge = jnp.where(i == 0, 0.0, aa_ref[7:8, :])
            c = jnp.where(row == 0, a_edge, pltpu.roll(a, 1, 0))
            h_edge = jnp.where(i == nblk - 1, 0.0, hh_ref[0:1, :])
            h_sh = jnp.where(row == ts - 1, h_edge, pltpu.roll(h, ts - 1, 0))
        else:
            a_edge = jnp.where(i == 0, 0.0, aa_ref[0:1, :])
            c = jnp.where(row == ts - 1, a_edge, pltpu.roll(a, ts - 1, 0))
            h_edge = jnp.where(i == nblk - 1, 0.0, hh_ref[7:8, :])
            h_sh = jnp.where(row == 0, h_edge, pltpu.roll(h, 1, 0))
        cc, gb = _block_scan(c, dh_ref[...], not reverse)
        g = gb + cc * carry[0:1, :]
        g_ref[...] = g
        carry[0:1, :] = g[ts - 1:ts, :] if reverse else g[0:1, :]
        da_ref[...] = g * h_sh

    if reverse:
        bi = lambda i: i
    else:
        bi = lambda i: nblk - 1 - i
    blk = pl.BlockSpec((None, ts, w), lambda n, i: (n, bi(i), 0))
    before = pl.BlockSpec((None, 8, w), lambda n, i: (n, jnp.maximum(bi(i) * tpb - 1, 0), 0))
    after = pl.BlockSpec((None, 8, w), lambda n, i: (n, jnp.minimum((bi(i) + 1) * tpb, nb8 - 1), 0))
    a_tile, h_tile = (before, after) if reverse else (after, before)
    return _pcall(body, name=f"lru_scan_bwd_r{int(reverse)}", grid=(nbatch, nblk), in_specs=[blk, a_tile, blk, h_tile, blk],
                  out_specs=(blk, blk),
                  out_shape=(jax.ShapeDtypeStruct((nbatch, s, w), F32), jax.ShapeDtypeStruct((nbatch, s, w), F32)),
                  scratch_shapes=[pltpu.VMEM((8, w), F32)], compiler_params=_params())(a3, a3, h3, h3, dh3)


def _head_expand(lane0):
    return (jnp.right_shift(lax.broadcasted_iota(jnp.int32, (128, 1024), 1), HEAD_SHIFT) + lane0
            == lax.broadcasted_iota(jnp.int32, (128, 1024), 0)).astype(F32)


def _head_reduce(lane0):
    return (jnp.right_shift(lax.broadcasted_iota(jnp.int32, (1024, 128), 0), HEAD_SHIFT) + lane0
            == lax.broadcasted_iota(jnp.int32, (1024, 128), 1)).astype(F32)


def _time_mask(q, reverse):
    ri = lax.broadcasted_iota(jnp.int32, (q, q), 0)
    ci = lax.broadcasted_iota(jnp.int32, (q, q), 1)
    return (ri <= ci) if reverse else (ri >= ci)


def _ssd_common(xs_ref, bc_ref, dt_ref, al_ref, reverse, lane0):
    q = xs_ref.shape[0]
    edge = 0 if reverse else q - 1
    dt = dt_ref[...]
    a = -jnp.exp(al_ref[...])
    mask = _time_mask(q, reverse)
    expand = _head_expand(lane0)
    cum = _dot01(mask.astype(F32), dt * a, split="b", terms=3)
    cum_x = _dot01(cum, expand, split="a", terms=3)
    dt_x = _dot01(dt, expand, split="a", terms=2)
    last_x = cum_x[edge:edge + 1, :]
    xs = xs_ref[...]
    bc = bc_ref[...]
    return dict(q=q, edge=edge, lane0=lane0, dt=dt, a=a, mask=mask, cum_t=cum.T, cum_x=cum_x, dt_x=dt_x, xs=xs,
                v=xs * dt_x, e_c=jnp.exp(cum_x), w=jnp.exp(last_x - cum_x), e_l=jnp.exp(last_x),
                bm=bc[:, :512], cm=bc[:, 512:])


def _ssd_decay(c, h):
    row = c["lane0"] + h
    seg = c["cum_x"][:, h * SSD_HEADDIM:h * SSD_HEADDIM + 1] - c["cum_t"][row:row + 1, :]
    return jnp.where(c["mask"], jnp.exp(jnp.minimum(seg, 0.0)), 0.0)


def _head_masks():
    lane = jnp.right_shift(lax.broadcasted_iota(jnp.int32, (1, 256), 1), HEAD_SHIFT)
    return [lane == e for e in range(4)]


def _ssd_fwd(xbc3, dt3, alog, reverse):
    nbatch, s, _ = xbc3.shape
    q = min(SSD_CHUNK, s)
    nc = s // q
    lane0 = SSD_HEADS * int(reverse)

    def body(xs_ref, bc_ref, dt_ref, al_ref, y_ref, st_ref, st):
        @pl.when(pl.program_id(1) == 0)
        def _():
            st[...] = jnp.zeros_like(st)

        st_ref[...] = st[...]
        c = _ssd_common(xs_ref, bc_ref, dt_ref, al_ref, reverse, lane0)
        hm = _head_masks()
        for g in range(SSD_GROUPS):
            sl = slice(g * 256, (g + 1) * 256)
            cg, bg = _mx(c["cm"][:, g * 128:(g + 1) * 128]), _mx(c["bm"][:, g * 128:(g + 1) * 128])
            cb = _dot(cg, bg, _NT)
            vg = c["v"][:, sl]
            s0 = st[:, sl]
            yg = _dot(cg, _mx(s0)) * c["e_c"][:, sl]
            for e in range(4):
                m = _ssd_decay(c, 4 * g + e) * cb
                yg = yg + _dot(_mx(m), _mx(jnp.where(hm[e], vg, 0.0)))
            y_ref[:, sl] = yg
            st[:, sl] = c["e_l"][:, sl] * s0 + _dot(bg, _mx(vg * c["w"][:, sl]), _TN)

    ck = (lambda i: nc - 1 - i) if reverse else (lambda i: i)
    xs_spec = pl.BlockSpec((None, q, 1024), lambda n, i: (n, ck(i), 0))
    bc_spec = pl.BlockSpec((None, q, 1024), lambda n, i: (n, ck(i), 1))
    dt_spec = pl.BlockSpec((None, q, 128), lambda n, i: (n, ck(i), 0))
    al_spec = pl.BlockSpec((1, 128), lambda n, i: (0, 0))
    st_spec = pl.BlockSpec((None, None, 128, 1024), lambda n, i: (n, ck(i), 0, 0))
    return _pcall(body, name=f"ssd_fwd_r{int(reverse)}", grid=(nbatch, nc), in_specs=[xs_spec, bc_spec, dt_spec, al_spec],
                  out_specs=(xs_spec, st_spec),
                  out_shape=(jax.ShapeDtypeStruct((nbatch, s, 1024), F32), jax.ShapeDtypeStruct((nbatch, nc, 128, 1024), F32)),
                  scratch_shapes=[pltpu.VMEM((128, 1024), F32)], compiler_params=_params())(xbc3, xbc3, dt3, alog)


def _ssd_bwd(xbc3, dt3, alog, st4, dy3, reverse):
    nbatch, s, _ = xbc3.shape
    q = min(SSD_CHUNK, s)
    nc = s // q
    lane0 = SSD_HEADS * int(reverse)

    def body(xs_ref, bc_ref, dt_ref, al_ref, st0_ref, dy_ref, dxs_ref, dbc_ref, ddt_ref, dal_ref, dst):
        n, i = pl.program_id(0), pl.program_id(1)

        @pl.when(i == 0)
        def _():
            dst[...] = jnp.zeros_like(dst)

        @pl.when((i == 0) & (n == 0))
        def _():
            dal_ref[...] = jnp.zeros_like(dal_ref)

        c = _ssd_common(xs_ref, bc_ref, dt_ref, al_ref, reverse, lane0)
        hm = _head_masks()
        reduce_m = _head_reduce(lane0)
        s0_all, ds1_all, dy = st0_ref[...], dst[...], dy_ref[...]
        lane = lax.broadcasted_iota(jnp.int32, (q, 128), 1)
        sub = lax.broadcasted_iota(jnp.int32, (128, q), 0)
        rowacc = jnp.zeros((q, 128), F32)
        colacc_t = jnp.zeros((128, q), F32)
        dv_l, yst_l, dvbar_l, dk_l, dc_l = [], [], [], [], []
        for g in range(SSD_GROUPS):
            sl = slice(g * 256, (g + 1) * 256)
            cg, bg = _mx(c["cm"][:, g * 128:(g + 1) * 128]), _mx(c["bm"][:, g * 128:(g + 1) * 128])
            cb = _dot(cg, bg, _NT)
            vg, dyg, wg, ecg = c["v"][:, sl], dy[:, sl], c["w"][:, sl], c["e_c"][:, sl]
            s0, ds1 = _mx(s0_all[:, sl]), _mx(ds1_all[:, sl])
            dye = _mx(dyg * ecg)
            yst_l.append(_dot(cg, s0) * ecg)
            dcg = _dot(dye, s0, _NT)
            dst[:, sl] = c["e_l"][:, sl] * ds1_all[:, sl] + _dot(cg, dye, _TN)
            vbar = _mx(vg * wg)
            dvbar = _dot(bg, ds1)
            dvbar_l.append(dvbar)
            dvg = dvbar * wg
            dkg = _dot(vbar, ds1, _NT)
            for e in range(4):
                h = 4 * g + e
                m = _ssd_decay(c, h)
                dyh, vh = _mx(jnp.where(hm[e], dyg, 0.0)), _mx(jnp.where(hm[e], vg, 0.0))
                dvg = dvg + _dot(_mx(m * cb), dyh, _TN)
                dcb = _dot(dyh, vh, _NT) * m
                dcbb = _mx(dcb)
                dcg = dcg + _dot(dcbb, bg)
                dkg = dkg + _dot(dcbb, cg, _TN)
                wmat = dcb * cb
                rowacc = jnp.where(lane == lane0 + h, jnp.sum(wmat, axis=1, keepdims=True), rowacc)
                colacc_t = jnp.where(sub == lane0 + h, jnp.sum(wmat, axis=0, keepdims=True), colacc_t)
            dv_l.append(dvg)
            dk_l.append(dkg)
            dc_l.append(dcg)
        dv = jnp.concatenate(dv_l, axis=1)
        yst = jnp.concatenate(yst_l, axis=1)
        dvbar = jnp.concatenate(dvbar_l, axis=1)
        t1 = _dot01(dy * yst, reduce_m, split="a", terms=2)
        t2 = _dot01(c["v"] * c["w"] * dvbar, reduce_m, split="a", terms=2)
        dlast = jnp.sum(t2, axis=0, keepdims=True) + _dot01(
            c["e_l"] * jnp.sum(ds1_all * s0_all, axis=0, keepdims=True), reduce_m, split="a", terms=2)
        dcum = rowacc - colacc_t.T + t1 - t2
        dcum = dcum + jnp.where(lax.broadcasted_iota(jnp.int32, (q, 128), 0) == c["edge"], dlast, 0.0)
        dda = _dot01(c["mask"].astype(F32), dcum, _TN, split="b", terms=3)
        ddt_ref[...] = dda * c["a"] + _dot01(dv * c["xs"], reduce_m, split="a", terms=2)
        dal_ref[...] += jnp.sum(dda * c["dt"], axis=0, keepdims=True) * c["a"]
        dxs_ref[...] = dv * c["dt_x"]
        dbc_ref[...] = jnp.concatenate(dk_l + dc_l, axis=1)

    ck = (lambda i: i) if reverse else (lambda i: nc - 1 - i)
    xs_spec = pl.BlockSpec((None, q, 1024), lambda n, i: (n, ck(i), 0))
    bc_spec = pl.BlockSpec((None, q, 1024), lambda n, i: (n, ck(i), 1))
    dt_spec = pl.BlockSpec((None, q, 128), lambda n, i: (n, ck(i), 0))
    al_spec = pl.BlockSpec((1, 128), lambda n, i: (0, 0))
    st_spec = pl.BlockSpec((None, None, 128, 1024), lambda n, i: (n, ck(i), 0, 0))
    return _pcall(body, name=f"ssd_bwd_r{int(reverse)}", grid=(nbatch, nc),
                  in_specs=[xs_spec, bc_spec, dt_spec, al_spec, st_spec, xs_spec],
                  out_specs=(xs_spec, xs_spec, dt_spec, al_spec),
                  out_shape=(jax.ShapeDtypeStruct((nbatch, s, 1024), F32), jax.ShapeDtypeStruct((nbatch, s, 1024), F32),
                             jax.ShapeDtypeStruct((nbatch, s, 128), F32), jax.ShapeDtypeStruct((1, 128), F32)),
                  scratch_shapes=[pltpu.VMEM((128, 1024), F32)], compiler_params=_params())(xbc3, xbc3, dt3, alog, st4, dy3)


def _gla_block(q_ref, k_ref, g_ref, b, reverse):
    bq = g_ref.shape[1]
    nsub = bq // HGRN_SUB
    edge = 0 if reverse else bq - 1
    ri = lax.broadcasted_iota(jnp.int32, (bq, bq), 0)
    ci = lax.broadcasted_iota(jnp.int32, (bq, bq), 1)
    rb, cb = jnp.right_shift(ri, HGRN_SUB_SHIFT), jnp.right_shift(ci, HGRN_SUB_SHIFT)
    mask = (ri <= ci) if reverse else (ri >= ci)
    m_within = (mask & (rb == cb)).astype(F32)
    m_before = ((cb > rb) if reverse else (cb < rb)).astype(F32)
    g = g_ref[b]
    bl = _dot01(m_within, g, split="b", terms=3)
    c = _dot01(m_before, g, split="b", terms=3)
    last = c[edge:edge + 1, :] + bl[edge:edge + 1, :]
    ebl, enbl, ec, elc = jnp.exp(bl), jnp.exp(-bl), jnp.exp(c), jnp.exp(last - c)
    qh = q_ref[b] * HGRN_SCALE * ebl
    kh = k_ref[b] * enbl
    blk = jnp.right_shift(lax.broadcasted_iota(jnp.int32, (bq, 1), 0), HGRN_SUB_SHIFT)
    scale = []
    for i in range(nsub):
        valid = (blk >= i) if reverse else (blk <= i)
        ex = jnp.where(valid, c[i * HGRN_SUB:i * HGRN_SUB + 1, :] - c, 0.0)
        scale.append(jnp.where(valid, jnp.exp(ex), 0.0))
    return dict(bq=bq, nsub=nsub, edge=edge, mask=mask, m_within=m_within, m_before=m_before, ebl=ebl, enbl=enbl, ec=ec,
                elc=elc, e_l=jnp.exp(last), qh=qh, qt=qh * ec, kh=kh, kb=kh * elc, scale=scale)


def _gla_scores(c, hs):
    keys = [_mx(c["kh"][:, hs] * c["scale"][i][:, hs]) for i in range(c["nsub"])]
    rows = [_dot(_mx(c["qh"][i * HGRN_SUB:(i + 1) * HGRN_SUB, hs]), keys[i], _NT) for i in range(c["nsub"])]
    return jnp.where(c["mask"], jnp.concatenate(rows, axis=0), 0.0), keys


def _gla_specs(nbatch, s, w, reverse_order):
    bq = min(HGRN_BLOCK, s)
    nblk = s // bq
    bi = (lambda i: nblk - 1 - i) if reverse_order else (lambda i: i)
    col = lambda cb: pl.BlockSpec((nbatch, bq, w), lambda i: (0, bi(i), cb))
    st_spec = pl.BlockSpec((nbatch, None, 128, w), lambda i: (0, bi(i), 0, 0))
    return bq, nblk, col, st_spec


def _gla_fwd(proj3, k3, g3, reverse):
    nbatch, s, w = k3.shape
    bq, nblk, col, st_spec = _gla_specs(nbatch, s, w, reverse)

    def body(q_ref, k_ref, v_ref, g_ref, o_ref, st_ref, st):
        @pl.when(pl.program_id(0) == 0)
        def _():
            st[...] = jnp.zeros_like(st)

        for b in range(nbatch):
            st_ref[b] = st[b]
            c = _gla_block(q_ref, k_ref, g_ref, b, reverse)
            v = v_ref[b]
            for h in range(HGRN_HEADS):
                hs = slice(h * 128, (h + 1) * 128)
                att, _ = _gla_scores(c, hs)
                vb = _mx(v[:, hs])
                s0 = st[b, :, hs]
                o_ref[b, :, hs] = _dot(_mx(att), vb) + _dot(_mx(c["qt"][:, hs]), _mx(s0), _NT)
                st[b, :, hs] = s0 * c["e_l"][:, hs] + _dot(vb, _mx(c["kb"][:, hs]), _TN)

    return _pcall(body, name=f"gla_fwd_r{int(reverse)}", grid=(nblk,), in_specs=[col(0), col(0), col(3), col(0)],
                  out_specs=(col(0), st_spec),
                  out_shape=(jax.ShapeDtypeStruct((nbatch, s, w), F32), jax.ShapeDtypeStruct((nbatch, nblk, 128, w), F32)),
                  scratch_shapes=[pltpu.VMEM((nbatch, 128, w), F32)], compiler_params=_params())(proj3, k3, proj3, g3)


def _gla_bwd(proj3, k3, g3, st4, do3, reverse):
    nbatch, s, w = k3.shape
    bq, nblk, col, st_spec = _gla_specs(nbatch, s, w, not reverse)

    def body(q_ref, k_ref, v_ref, g_ref, st_ref, do_ref, dq_ref, dk_ref, dv_ref, dg_ref, dst):
        @pl.when(pl.program_id(0) == 0)
        def _():
            dst[...] = jnp.zeros_like(dst)

        row = lax.broadcasted_iota(jnp.int32, (bq, 128), 0)
        for b in range(nbatch):
            c = _gla_block(q_ref, k_ref, g_ref, b, reverse)
            s0_all, ds1_all = st_ref[b], dst[b]
            v, dy = v_ref[b], do_ref[b]
            dbl_l, dc_l = [], []
            for h in range(HGRN_HEADS):
                hs = slice(h * 128, (h + 1) * 128)
                att, keys = _gla_scores(c, hs)
                qh, qt, kh, kb = c["qh"][:, hs], c["qt"][:, hs], c["kh"][:, hs], c["kb"][:, hs]
                vb, dyb = _mx(v[:, hs]), _mx(dy[:, hs])
                s0, ds1 = s0_all[:, hs], ds1_all[:, hs]
                datt = _mx(jnp.where(c["mask"], _dot(dyb, vb, _NT), 0.0))
                dqh_rows = []
                dkh = jnp.zeros((bq, 128), F32)
                dc = jnp.zeros((bq, 128), F32)
                for i in range(c["nsub"]):
                    rs = slice(i * HGRN_SUB, (i + 1) * HGRN_SUB)
                    dqh_rows.append(_dot(datt[rs], keys[i]))
                    dki = _dot(datt[rs], _mx(qh[rs]), _TN)
                    sc = c["scale"][i][:, hs]
                    dkh = dkh + dki * sc
                    dex = dki * (kh * sc)
                    dc = dc - dex + jnp.where(row == i * HGRN_SUB, jnp.sum(dex, axis=0, keepdims=True), 0.0)
                dqt = _dot(dyb, _mx(s0))
                dkb = _dot(vb, _mx(ds1))
                dv_ref[b, :, hs] = _dot(_mx(att), dyb, _TN) + _dot(_mx(kb), _mx(ds1), _NT)
                dst[b, :, hs] = c["e_l"][:, hs] * ds1 + _dot(dyb, _mx(qt), _TN)
                dqh = jnp.concatenate(dqh_rows, axis=0) + dqt * c["ec"][:, hs]
                dkh = dkh + dkb * c["elc"][:, hs]
                kbk = dkb * kb
                dlast = jnp.sum(kbk, axis=0, keepdims=True) + c["e_l"][:, hs] * jnp.sum(ds1 * s0, axis=0, keepdims=True)
                at_edge = jnp.where(row == c["edge"], dlast, 0.0)
                dc_l.append(dc + dqt * qt - kbk + at_edge)
                dbl_l.append(dqh * qh - dkh * kh + at_edge)
                dq_ref[b, :, hs] = dqh * c["ebl"][:, hs] * HGRN_SCALE
                dk_ref[b, :, hs] = dkh * c["enbl"][:, hs]
            dg_ref[b] = (_dot01(c["m_within"], jnp.concatenate(dbl_l, axis=1), _TN, split="b", terms=2)
                         + _dot01(c["m_before"], jnp.concatenate(dc_l, axis=1), _TN, split="b", terms=2))

    shp = jax.ShapeDtypeStruct((nbatch, s, w), F32)
    return _pcall(body, name=f"gla_bwd_r{int(reverse)}", grid=(nblk,),
                  in_specs=[col(0), col(0), col(3), col(0), st_spec, col(0)],
                  out_specs=(col(0),) * 4, out_shape=(shp,) * 4,
                  scratch_shapes=[pltpu.VMEM((nbatch, 128, w), F32)], compiler_params=_params())(proj3, k3, proj3, g3, st4, do3)


DIRS = (False, True)


def _block_diag(w):
    eye = jnp.eye(16, dtype=w.dtype)
    return (eye[:, None, :, None] * w[:, :, None, :]).reshape(1024, 1024)


def _diag_blocks(m):
    m4 = m.reshape(16, 64, 16, 64)
    return jnp.stack([m4[i, :, i, :] for i in range(16)], axis=0)


def _pad_lanes(v, n=128):
    return jnp.pad(v, [(0, 0)] * (v.ndim - 1) + [(0, n - v.shape[-1])])


def _mlp_fwd(tag, x, nw, w1, w2):
    (h,) = _pw_fwd(f"{tag}_norm", _f_norm, [(x, 0)], [(nw, 0)], [BF16], 1024, 1)
    a, r = _mm(f"{tag}_up", h, w1, "nn", relu2=True)
    return _mm(f"{tag}_down", r, w2, "nn", res=x), (h, a, r)


def _mlp_bwd(tag, x, nw, w1, w2, saved, dxo):
    h, a, r = saved
    dw2 = _mm(f"{tag}_dw2", r, dxo, "tn")
    da = _mm(f"{tag}_da", dxo, w2, "nt", relu2_of=a, out_dtype=BF16)
    dw1 = _mm(f"{tag}_dw1", h, da, "tn", col_shards=4)
    dh = _mm(f"{tag}_dh", da, w1, "nt")
    (dx,), (dnw,) = _pw_bwd(f"{tag}_dnorm", _f_norm, [(x, 0)], [(nw, 0)], [dh], 1024, 1, [0], adds={0: dxo})
    return dx, dw1, dw2, dnw


def _split_in0(pieces, dt_piece):
    tm = 256

    def body(p0, p1, p2, p3, p4, p5, o_ref):
        full = jnp.concatenate([p0[...], p1[...], p2[...], p3[...], p4[...], p5[:, :32]], axis=1)
        for j in range(4):
            o_ref[j] = full[:, 1288 * j:1288 * (j + 1)]

    blk = pl.BlockSpec((tm, 1024), lambda i: (i, 0))
    return _pcall(body, name="split_in0", grid=(1024 // tm,), in_specs=[blk] * 5 + [pl.BlockSpec((tm, 128), lambda i: (i, 0))],
                  out_specs=pl.BlockSpec((4, tm, 1288), lambda i: (0, i, 0)),
                  out_shape=jax.ShapeDtypeStruct((4, 1024, 1288), F32), compiler_params=_params())(*pieces, dt_piece)


def _assemble_in0(shards):
    tm = 256

    def body(s_ref, m_ref, d_ref):
        full = jnp.concatenate([s_ref[j] for j in range(4)], axis=1)
        m_ref[...] = full[:, :5120]
        d_ref[...] = jnp.concatenate([full[:, 5120:5152], jnp.zeros((tm, 96), full.dtype)], axis=1)

    return _pcall(body, name="assemble_in0", grid=(1024 // tm,), in_specs=[pl.BlockSpec((4, tm, 1288), lambda i: (0, i, 0))],
                  out_specs=(pl.BlockSpec((tm, 5120), lambda i: (i, 0)), pl.BlockSpec((tm, 128), lambda i: (i, 0))),
                  out_shape=(jax.ShapeDtypeStruct((1024, 5120), shards.dtype), jax.ShapeDtypeStruct((1024, 128), shards.dtype)),
                  compiler_params=_params())(shards)


def _local_step(x3, tgt3, w, w_main0, w_dt0):
    nb, s, d = x3.shape
    t = nb * s
    x0 = x3.reshape(t, d)
    tgt = tgt3.reshape(t, d)
    grads = {}
    row = lambda v: v.reshape(1, -1)
    to3 = lambda v: v.reshape(nb, s, v.shape[-1])
    to2 = lambda v: v.reshape(-1, v.shape[-1])

    conv_w, conv_b = w["even_conv_w"][0], row(w["even_conv_b"][0])
    nmix0 = row(w["norm_mix"][0])
    (h0,) = _pw_fwd("l0_norm", _f_norm, [(x0, 0)], [(nmix0, 0)], [BF16], 1024, 1)
    proj0 = _mm("l0_proj", h0, w_main0, "nn")
    dt_raw = _mm("l0_proj_dt", h0, w_dt0, "nn")
    conv = to2(_conv_fwd(to3(proj0), conv_w, conv_b, 3))
    (xbc,) = _pw_fwd("l0_silu", _f_silu, [(conv, 0)], [], [F32], 1024, 2)
    dt_bias = _pad_lanes(w["ssd_dt_bias"][0].reshape(1, 32))
    (dt,) = _pw_fwd("l0_dt", _f_softplus, [(dt_raw, 0)], [(dt_bias, 0)], [F32], 128, 1)
    dt3, xbc3 = to3(dt), to3(xbc)
    alog = _pad_lanes(w["ssd_a_log"][0].reshape(1, 32))
    ssd = [_ssd_fwd(xbc3, dt3, alog, r) for r in DIRS]
    yf, yb = to2(ssd[0][0]), to2(ssd[1][0])
    dskip = jnp.repeat(w["ssd_d"][0], SSD_HEADDIM).reshape(1, 1024)
    snw = row(w["ssd_norm_w"][0])
    ssd_ins = [(yf, 0), (yb, 0), (xbc, 0), (proj0, 3)]
    (ya,) = _pw_fwd("l0_ssd_post", _f_ssd_post, ssd_ins, [(dskip, 0), (snw, 0)], [BF16], 1024, 1, groups=SSD_GROUPS)
    u_lru = conv[:, 2048:]
    w_gates = [_block_diag(w[k][0, r]).astype(MXU_DTYPE) for r in range(2) for k in ("lru_w_a", "lru_w_x")]
    pre = [_mm(f"l0_lru_pre{i}", u_lru, wg, "nn") for i, wg in enumerate(w_gates)]
    lru_par = [[(row(w[k][0, r]), 0) for k in ("lru_b_a", "lru_b_x", "lru_lambda")] for r in range(2)]
    lru_ins = [[(pre[2 * r], 0), (pre[2 * r + 1], 0), (u_lru, 0)] for r in range(2)]
    ab = [_pw_fwd(f"l0_lru_gates{r}", _f_lru_gates, lru_ins[r], lru_par[r], [F32, F32], 1024, 1) for r in range(2)]
    hs = [_lru_scan(to3(ab[r][0]), to3(ab[r][1]), DIRS[r]) for r in range(2)]
    lru_post_ins = [(to2(hs[0]), 0), (to2(hs[1]), 0), (proj0, 4)]
    (ybm,) = _pw_fwd("l0_lru_post", _f_lru_post, lru_post_ins, [], [BF16], 1024, 1)
    w_out0 = w["even_w_out"][0]
    x1 = _mm("l0_out_a", ya, w_out0[:1024], "nn", res=x0)
    x1 = _mm("l0_out_b", ybm, w_out0[1024:], "nn", res=x1)
    nmlp0 = row(w["norm_mlp"][0])
    x2, mlp0 = _mlp_fwd("l0_mlp", x1, nmlp0, w["mlp_w1"][0], w["mlp_w2"][0])

    w_in1 = w["odd_w_in"][0]
    nmix1 = row(w["norm_mix"][1])
    (h1,) = _pw_fwd("l1_norm", _f_norm, [(x2, 0)], [(nmix1, 0)], [BF16], 1024, 1)
    proj1 = _mm("l1_proj", h1, w_in1, "nn")
    proj1_3 = to3(proj1)
    lb0, lb1 = row(w["hgrn_lb_logits"][0]), row(w["hgrn_lb_logits"][1])
    kg = [_pw_fwd(f"l1_hgrn_pre{r}", _f_hgrn_pre, [(proj1, 1 + r)], [(lb0, 0), (lb1, 0)], [F32, F32], 1024, 1)
          for r in range(2)]
    gla = [_gla_fwd(proj1_3, to3(kg[r][0]), to3(kg[r][1]), DIRS[r]) for r in range(2)]
    hnw = row(w["hgrn_norm_w"][0])
    hpost_ins = [(to2(gla[0][0]), 0), (to2(gla[1][0]), 0), (proj1, 4)]
    (yo,) = _pw_fwd("l1_hgrn_post", _f_hgrn_post, hpost_ins, [(hnw, 0)], [BF16], 1024, 1, groups=HGRN_HEADS)
    w_out1 = w["odd_w_out"][0]
    x3_ = _mm("l1_out", yo, w_out1, "nn", res=x2)
    nmlp1 = row(w["norm_mlp"][1])
    x4, mlp1 = _mlp_fwd("l1_mlp", x3_, nmlp1, w["mlp_w1"][1], w["mlp_w2"][1])

    dx4, dnf, loss = _loss_head(x4, tgt, row(w["norm_final"]))
    grads["norm_final"] = dnf.reshape(-1)

    dx3, dw1_1, dw2_1, dnmlp1 = _mlp_bwd("l1_mlp", x3_, nmlp1, w["mlp_w1"][1], w["mlp_w2"][1], mlp1, dx4)
    big = {"odd_w_out": _mm("l1_dwout", yo, dx3, "tn").reshape(4, 256, 1024)}
    dyo = _mm("l1_dyo", dx3, w_out1, "nt")
    (do, dgate1), (dhnw,) = _pw_bwd("l1_hgrn_post_b", _f_hgrn_post, hpost_ins, [(hnw, 0)], [dyo], 1024, 1, [0, 2],
                                    out_dtypes=[F32, BF16], groups=HGRN_HEADS)
    grads["hgrn_norm_w"] = dhnw
    do3 = to3(do)
    gb = [_gla_bwd(proj1_3, to3(kg[r][0]), to3(kg[r][1]), gla[r][1], do3, DIRS[r]) for r in range(2)]
    (dq,) = _pw_fwd("l1_dq", _f_add2, [(to2(gb[0][0]), 0), (to2(gb[1][0]), 0)], [], [BF16], 1024, 1)
    (dvv,) = _pw_fwd("l1_dv", _f_add2, [(to2(gb[0][2]), 0), (to2(gb[1][2]), 0)], [], [BF16], 1024, 1)
    dfr, dl0, dl1 = [], [], []
    for r in range(2):
        (df,), (a0, a1) = _pw_bwd(f"l1_hgrn_pre_b{r}", _f_hgrn_pre, [(proj1, 1 + r)], [(lb0, 0), (lb1, 0)],
                                  [to2(gb[r][1]), to2(gb[r][3])], 1024, 1, [0], out_dtypes=[BF16])
        dfr.append(df)
        dl0.append(a0)
        dl1.append(a1)
    grads["hgrn_lb_logits"] = jnp.concatenate([dl0[0] + dl0[1], dl1[0] + dl1[1]], axis=0)
    dparts1 = [dq, dfr[0], dfr[1], dvv, dgate1]
    dwin1 = jnp.concatenate([_mm(f"l1_dwin{i}", h1, dp, "tn") for i, dp in enumerate(dparts1)], axis=1)
    big["odd_w_in"] = dwin1.reshape(1024, 4, 1280).transpose(1, 0, 2)
    dh1 = _mm_sum_nt("l1_dh", dparts1, [w_in1[:, i * 1024:(i + 1) * 1024] for i in range(5)])
    (dx2,), (dnmix1,) = _pw_bwd("l1_dnorm", _f_norm, [(x2, 0)], [(nmix1, 0)], [dh1], 1024, 1, [0], adds={0: dx3})

    dx1, dw1_0, dw2_0, dnmlp0 = _mlp_bwd("l0_mlp", x1, nmlp0, w["mlp_w1"][0], w["mlp_w2"][0], mlp0, dx2)
    big["mlp_w1"] = jnp.concatenate([dw1_0, dw1_1], axis=1)
    big["mlp_w2"] = jnp.concatenate([dw2_0.reshape(4, 1024, 1024), dw2_1.reshape(4, 1024, 1024)], axis=1)
    grads["norm_mlp"] = jnp.concatenate([dnmlp0, dnmlp1], axis=0)
    big["even_w_out"] = jnp.concatenate([_mm("l0_dwout_a", ya, dx1, "tn"), _mm("l0_dwout_b", ybm, dx1, "tn")],
                                        axis=0).reshape(4, 512, 1024)
    dya = _mm("l0_dya", dx1, w_out0[:1024], "nt")
    dyb = _mm("l0_dyb", dx1, w_out0[1024:], "nt")
    (dh, dgate0), _ = _pw_bwd("l0_lru_post_b", _f_lru_post, lru_post_ins, [], [dyb], 1024, 1, [0, 2], out_dtypes=[F32, BF16])
    dh3 = to3(dh)
    dpre, du_parts, dlru = [], [], {k: [] for k in ("lru_b_a", "lru_b_x", "lru_lambda")}
    for r in range(2):
        g_r, da_r = _lru_scan_bwd(to3(ab[r][0]), hs[r], dh3, DIRS[r])
        (dpa, dpx, du_r), (dba, dbx, dlam) = _pw_bwd(f"l0_lru_gates_b{r}", _f_lru_gates, lru_ins[r], lru_par[r],
                                                     [to2(da_r), to2(g_r)], 1024, 1, [0, 1, 2],
                                                     out_dtypes=[BF16, BF16, F32])
        dpre += [dpa, dpx]
        du_parts.append(du_r)
        dlru["lru_b_a"].append(dba)
        dlru["lru_b_x"].append(dbx)
        dlru["lru_lambda"].append(dlam)
    for k, v in dlru.items():
        grads[k] = jnp.concatenate(v, axis=0)[None]
    dwg = [_diag_blocks(_mm(f"l0_dwgate{i}", u_lru, dp, "tn")) for i, dp in enumerate(dpre)]
    grads["lru_w_a"] = jnp.stack([dwg[0], dwg[2]])[None]
    grads["lru_w_x"] = jnp.stack([dwg[1], dwg[3]])[None]
    du_gate = _mm_sum_nt("l0_du_gate", dpre, w_gates)
    (du,) = _pw_fwd("l0_du", _f_add3, [(du_parts[0], 0), (du_parts[1], 0), (du_gate, 0)], [], [F32], 1024, 1)
    (dy, dxs_skip, dz), (ddskip, dsnw) = _pw_bwd("l0_ssd_post_b", _f_ssd_post, ssd_ins, [(dskip, 0), (snw, 0)], [dya],
                                                 1024, 1, [0, 2, 3], out_dtypes=[F32, F32, BF16], groups=SSD_GROUPS)
    grads["ssd_d"] = ddskip.reshape(SSD_HEADS, SSD_HEADDIM).sum(axis=1)[None]
    grads["ssd_norm_w"] = dsnw
    dy3 = to3(dy)
    sb = [_ssd_bwd(xbc3, dt3, alog, ssd[r][1], dy3, DIRS[r]) for r in range(2)]
    grads["ssd_a_log"] = (sb[0][3] + sb[1][3])[:, :32].reshape(1, 2, 16)
    (dxs,) = _pw_fwd("l0_dxs", _f_add3, [(to2(sb[0][0]), 0), (to2(sb[1][0]), 0), (dxs_skip, 0)], [], [F32], 1024, 1)
    (dbc,) = _pw_fwd("l0_dbc", _f_add2, [(to2(sb[0][1]), 0), (to2(sb[1][1]), 0)], [], [F32], 1024, 1)
    dconv = [_pw_bwd(f"l0_silu_b{j}", _f_silu, [(conv, j)], [], [d], 1024, 1, [0])[0][0] for j, d in enumerate((dxs, dbc))]
    dconv.append(du)
    (ddt,) = _pw_fwd("l0_ddt", _f_add2, [(to2(sb[0][2]), 0), (to2(sb[1][2]), 0)], [], [F32], 128, 1)
    (ddt_raw,), (ddtb,) = _pw_bwd("l0_dt_b", _f_softplus, [(dt_raw, 0)], [(dt_bias, 0)], [ddt], 128, 1, [0])
    grads["ssd_dt_bias"] = ddtb[:, :32].reshape(1, 2, 16)
    cb = [_conv_bwd(to3(d), to3(proj0), conv_w, j) for j, d in enumerate(dconv)]
    dcw = jnp.concatenate([c_[1] for c_ in cb], axis=1)
    grads["even_conv_w"] = dcw[:4][None]
    grads["even_conv_b"] = dcw[4:5]
    dparts0 = [to2(c_[0]) for c_ in cb] + [dz, dgate0]
    dwin0 = [_mm(f"l0_dwin{i}", h0, dp, "tn") for i, dp in enumerate(dparts0)]
    big["even_w_in"] = _split_in0(dwin0, _mm("l0_dwin_dt", h0, ddt_raw, "tn"))
    dh0 = _mm_sum_nt("l0_dh", dparts0 + [ddt_raw], [w_main0[:, i * 1024:(i + 1) * 1024] for i in range(5)] + [w_dt0])
    (dx0,), (dnmix0,) = _pw_bwd("l0_dnorm", _f_norm, [(x0, 0)], [(nmix0, 0)], [dh0], 1024, 1, [0], adds={0: dx1})
    grads["norm_mix"] = jnp.concatenate([dnmix0, dnmix1], axis=0)
    return loss, dx0.reshape(nb, s, d), grads, [big[n] for n in BIG]


ANY = pl.BlockSpec(memory_space=pl.ANY)


def _place():
    return lax.axis_index("x"), lax.axis_index("y"), lax.axis_index("c")


def _remote(src, dst, send_sems, recv_sems, k, to):
    return pltpu.make_async_remote_copy(src_ref=src, dst_ref=dst, send_sem=send_sems.at[k], recv_sem=recv_sems.at[k],
                                        device_id=to, device_id_type=MESH)


def _gather_chips(shards):
    n = len(shards)
    halves = [s.shape[0] // 2 for s in shards]

    def body(*refs):
        x_refs, out_refs = refs[:n], refs[n:2 * n]
        send_sems, recv_sems = refs[2 * n:]
        x, y, c = _place()
        sibling = (x, y, 1 - c)
        chips = [(1 - x, y), (x, 1 - y), (1 - x, 1 - y)]

        def blk(t, px, py, hc):
            return out_refs[t].at[2 * px + py, pl.ds(hc * halves[t], halves[t]), :]

        def src(t):
            return x_refs[t].at[pl.ds(c * halves[t], halves[t]), :]

        first = [_remote(src(t), blk(t, x, y, c), send_sems, recv_sems, 6 * t + j, (*chip, c))
                 for t in range(n) for j, chip in enumerate(chips)]
        for cp in first:
            cp.start()
        passed = []
        for t in range(n):
            for j, chip in enumerate(chips):
                _remote(src(t), blk(t, *chip, c), send_sems, recv_sems, 6 * t + j, (*chip, c)).wait_recv()
                cp = _remote(blk(t, *chip, c), blk(t, *chip, c), send_sems, recv_sems, 6 * t + 3 + j, sibling)
                cp.start()
                passed.append(cp)
        for t in range(n):
            for j, chip in enumerate(chips):
                _remote(src(t), blk(t, *chip, 1 - c), send_sems, recv_sems, 6 * t + 3 + j, sibling).wait_recv()
        for cp in first + passed:
            cp.wait_send()

    return _pcall(body, name="gather_weights", in_specs=[ANY] * n, out_specs=(ANY,) * n,
                  out_shape=tuple(jax.ShapeDtypeStruct((4,) + s.shape, s.dtype) for s in shards),
                  scratch_shapes=[pltpu.SemaphoreType.DMA((6 * n,)), pltpu.SemaphoreType.DMA((6 * n,))],
                  compiler_params=_params())(*shards)


def _pair_swap(gps):
    n = len(gps)
    halves = [g.shape[1] // 2 for g in gps]

    def body(*refs):
        g_refs, land_refs = refs[:n], refs[n:2 * n]
        send_sems, recv_sems = refs[2 * n:]
        x, y, c = _place()
        cps = [_remote(g_refs[t].at[j, pl.ds((1 - c) * halves[t], halves[t]), :], land_refs[t].at[j], send_sems, recv_sems,
                       4 * t + j, (x, y, 1 - c)) for t in range(n) for j in range(4)]
        for cp in cps:
            cp.start()
        for cp in cps:
            cp.wait()

    return _pcall(body, name="grad_pair_swap", in_specs=[ANY] * n, out_specs=(ANY,) * n,
                  out_shape=tuple(jax.ShapeDtypeStruct((4, h, g.shape[2]), F32) for g, h in zip(gps, halves)),
                  scratch_shapes=[pltpu.SemaphoreType.DMA((4 * n,)), pltpu.SemaphoreType.DMA((4 * n,))],
                  compiler_params=_params())(*gps)


def _pair_add(name, gp, land, cidx):
    _, half, cols = land.shape
    tr = _tile(half, 512)
    nh = half // tr

    def body(c_ref, g_ref, l_ref, o_ref):
        o_ref[...] = (g_ref[...] + l_ref[...]).astype(o_ref.dtype)

    grid_spec = pltpu.PrefetchScalarGridSpec(
        num_scalar_prefetch=1, grid=(4, nh),
        in_specs=[pl.BlockSpec((None, tr, cols), lambda j, i, c: (j, c[0] * nh + i, 0)),
                  pl.BlockSpec((None, tr, cols), lambda j, i, c: (j, i, 0))],
        out_specs=pl.BlockSpec((None, tr, cols), lambda j, i, c: (j, i, 0)))
    return _pcall(body, name=f"pair_add_{name}", grid_spec=grid_spec, out_shape=jax.ShapeDtypeStruct((4, half, cols), BF16),
                  compiler_params=_params())(cidx, gp, land)


def _chip_scatter(css):
    n = len(css)

    def body(*refs):
        s_refs, land_refs = refs[:n], refs[n:2 * n]
        send_sems, recv_sems = refs[2 * n:]
        x, y, c = _place()
        me = 2 * x + y
        chips = [(1 - x, y), (x, 1 - y), (1 - x, 1 - y)]
        cps = [_remote(s_refs[t].at[2 * px + py], land_refs[t].at[me], send_sems, recv_sems, 3 * t + j, (px, py, c))
               for t in range(n) for j, (px, py) in enumerate(chips)]
        for cp in cps:
            cp.start()
        for t in range(n):
            for j, (px, py) in enumerate(chips):
                _remote(s_refs[t].at[me], land_refs[t].at[2 * px + py], send_sems, recv_sems, 3 * t + j, (px, py, c)).wait_recv()
        for cp in cps:
            cp.wait_send()

    return _pcall(body, name="grad_chip_scatter", in_specs=[ANY] * n, out_specs=(ANY,) * n,
                  out_shape=tuple(jax.ShapeDtypeStruct(s.shape, s.dtype) for s in css),
                  scratch_shapes=[pltpu.SemaphoreType.DMA((3 * n,)), pltpu.SemaphoreType.DMA((3 * n,))],
                  compiler_params=_params())(*css)


def _chip_sum(name, land):
    _, half, cols = land.shape
    tr = _tile(half, 512)

    def body(l_ref, o_ref):
        o_ref[...] = ((l_ref[0].astype(F32) + l_ref[1].astype(F32)) + l_ref[2].astype(F32)) + l_ref[3].astype(F32)

    return _pcall(body, name=f"chip_sum_{name}", grid=(half // tr,),
                  in_specs=[pl.BlockSpec((4, tr, cols), lambda i: (0, i, 0))],
                  out_specs=pl.BlockSpec((tr, cols), lambda i: (i, 0)),
                  out_shape=jax.ShapeDtypeStruct((half, cols), F32), compiler_params=_params())(land)


def _pair_join(reds):
    n = len(reds)

    def body(*refs):
        r_refs, out_refs = refs[:n], refs[n:2 * n]
        send_sems, recv_sems = refs[2 * n:]
        x, y, c = _place()
        cps = [_remote(r_refs[t], out_refs[t].at[c], send_sems, recv_sems, t, (x, y, 1 - c)) for t in range(n)]
        for cp in cps:
            cp.start()
        for t in range(n):
            _remote(r_refs[t], out_refs[t].at[1 - c], send_sems, recv_sems, t, (x, y, 1 - c)).wait_recv()
        for cp in cps:
            cp.wait_send()

    return _pcall(body, name="grad_pair_join", in_specs=[ANY] * n, out_specs=(ANY,) * n,
                  out_shape=tuple(jax.ShapeDtypeStruct((2,) + r.shape, F32) for r in reds),
                  scratch_shapes=[pltpu.SemaphoreType.DMA((n,)), pltpu.SemaphoreType.DMA((n,))],
                  compiler_params=_params())(*reds)


def _adamw(name, g, w, m, v):
    rows, cols = g.shape
    tr = _tile(rows, 512)

    def body(g_ref, w_ref, m_ref, v_ref, d_ref, mo_ref, vo_ref):
        gv = g_ref[...]
        mn = ADAM_B1 * m_ref[...] + (1.0 - ADAM_B1) * gv
        vn = ADAM_B2 * v_ref[...] + (1.0 - ADAM_B2) * jnp.square(gv)
        m_hat = mn / (1.0 - ADAM_B1 ** ADAM_STEP)
        v_hat = vn / (1.0 - ADAM_B2 ** ADAM_STEP)
        d_ref[...] = -ADAM_LR * (m_hat / (jnp.sqrt(v_hat) + ADAM_EPS) + ADAM_WD * w_ref[...])
        mo_ref[...] = mn
        vo_ref[...] = vn

    blk = pl.BlockSpec((tr, cols), lambda i: (i, 0))
    shp = jax.ShapeDtypeStruct((rows, cols), F32)
    return _pcall(body, name=f"adamw_{name}", grid=(rows // tr,), in_specs=[blk] * 4, out_specs=(blk,) * 3,
                  out_shape=(shp,) * 3, compiler_params=_params())(g, w, m, v)


def _pack(pieces, rows, dtype):
    flat = jnp.concatenate([p.reshape(-1).astype(dtype) for p in pieces])
    return jnp.pad(flat, (0, rows * PACK_COLS - flat.shape[0])).reshape(rows, PACK_COLS)


def _unpack(pack, shapes):
    flat = pack.reshape(-1)
    out, off = [], 0
    for shp in shapes:
        n = math.prod(shp)
        out.append(flat[off:off + n].reshape(shp))
        off += n
    return out


def _shard_of(full, axis, j):
    n = full.shape[axis] // 4
    return lax.slice_in_dim(full, j * n, (j + 1) * n, axis=axis)


def kernel(x, even_w_in, even_conv_w, even_conv_b, ssd_a_log, ssd_dt_bias, ssd_d, ssd_norm_w, lru_w_a, lru_b_a, lru_w_x, lru_b_x, lru_lambda, even_w_out, odd_w_in, hgrn_lb_logits, hgrn_norm_w, odd_w_out, norm_mix, norm_mlp, mlp_w1, mlp_w2, norm_final, loss_target, m_even_w_in, m_even_conv_w, m_even_conv_b, m_ssd_a_log, m_ssd_dt_bias, m_ssd_d, m_ssd_norm_w, m_lru_w_a, m_lru_b_a, m_lru_w_x, m_lru_b_x, m_lru_lambda, m_even_w_out, m_odd_w_in, m_hgrn_lb_logits, m_hgrn_norm_w, m_odd_w_out, m_norm_mix, m_norm_mlp, m_mlp_w1, m_mlp_w2, m_norm_final, v_even_w_in, v_even_conv_w, v_even_conv_b, v_ssd_a_log, v_ssd_dt_bias, v_ssd_d, v_ssd_norm_w, v_lru_w_a, v_lru_b_a, v_lru_w_x, v_lru_b_x, v_lru_lambda, v_even_w_out, v_odd_w_in, v_hgrn_lb_logits, v_hgrn_norm_w, v_odd_w_out, v_norm_mix, v_norm_mlp, v_mlp_w1, v_mlp_w2, v_norm_final):
    names = [n for n, _, _, _ in WEIGHTS]
    w_loc = dict(zip(names, (even_w_in, even_conv_w, even_conv_b, ssd_a_log, ssd_dt_bias, ssd_d, ssd_norm_w, lru_w_a, lru_b_a, lru_w_x, lru_b_x, lru_lambda, even_w_out, odd_w_in, hgrn_lb_logits, hgrn_norm_w, odd_w_out, norm_mix, norm_mlp, mlp_w1, mlp_w2, norm_final)))
    m_loc = dict(zip(names, (m_even_w_in, m_even_conv_w, m_even_conv_b, m_ssd_a_log, m_ssd_dt_bias, m_ssd_d, m_ssd_norm_w, m_lru_w_a, m_lru_b_a, m_lru_w_x, m_lru_b_x, m_lru_lambda, m_even_w_out, m_odd_w_in, m_hgrn_lb_logits, m_hgrn_norm_w, m_odd_w_out, m_norm_mix, m_norm_mlp, m_mlp_w1, m_mlp_w2, m_norm_final)))
    v_loc = dict(zip(names, (v_even_w_in, v_even_conv_w, v_even_conv_b, v_ssd_a_log, v_ssd_dt_bias, v_ssd_d, v_ssd_norm_w, v_lru_w_a, v_lru_b_a, v_lru_w_x, v_lru_b_x, v_lru_lambda, v_even_w_out, v_odd_w_in, v_hgrn_lb_logits, v_hgrn_norm_w, v_odd_w_out, v_norm_mix, v_norm_mlp, v_mlp_w1, v_mlp_w2, v_norm_final)))
    spec = {n: (blk, full, ax) for n, blk, full, ax in WEIGHTS}

    small = [n for n in names if n not in BIG]
    two_d = lambda n, v: v.reshape(BIG_2D[n])

    me = 2 * lax.axis_index("x") + lax.axis_index("y")
    cc = lax.axis_index("c")
    put = lambda whole, part, k: lax.dynamic_update_slice_in_dim(whole, part[None], k, axis=0)
    own = [two_d(n, w_loc[n]).astype(BF16) for n in BIG] + [_pack([w_loc[n] for n in SMALL_SHARDED], 16, F32)]
    g_in0, g_out0, g_in1, g_out1, g_w1, g_w2, g_small = [put(g, o, me) for g, o in zip(_gather_chips(own), own)]
    w_main0, w_dt0 = _assemble_in0(g_in0)
    w_full = {n: w_loc[n] for n in names if spec[n][2] is None}
    w_full["even_w_out"] = g_out0.reshape(1, 2048, 1024)
    w_full["odd_w_in"] = jnp.concatenate([g_in1[j] for j in range(4)], axis=1)[None]
    w_full["odd_w_out"] = g_out1.reshape(1, 1024, 1024)
    w_full["mlp_w1"] = jnp.stack([jnp.concatenate([g_w1[j, l * 1024:(l + 1) * 1024] for j in range(4)], axis=1) for l in range(2)])
    w_full["mlp_w2"] = jnp.stack([jnp.concatenate([g_w2[j, l * 1024:(l + 1) * 1024] for j in range(4)], axis=0) for l in range(2)])
    shards = [_unpack(g_small[j], [spec[n][0] for n in SMALL_SHARDED]) for j in range(4)]
    for i, n in enumerate(SMALL_SHARDED):
        w_full[n] = jnp.concatenate([shards[j][i] for j in range(4)], axis=spec[n][2])

    loss_vec, grad_x, grads, big = _local_step(x, loss_target, w_full, w_main0, w_dt0)
    loss = lax.psum(loss_vec[0, 0], ("x", "y", "c"))

    def dest_pack(j):
        return _pack([grads[n].reshape(spec[n][1]) if spec[n][2] is None else _shard_of(grads[n].reshape(spec[n][1]), spec[n][2], j)
                      for n in small], SMALL_ROWS, F32)

    tensors = big + [jnp.stack([dest_pack(j) for j in range(4)])]
    tags = list(BIG) + ["small"]
    cidx = cc.astype(jnp.int32).reshape(1)
    chip_sums = [_pair_add(tag, g, land, cidx) for tag, g, land in zip(tags, tensors, _pair_swap(tensors))]
    landed = [put(land, lax.dynamic_index_in_dim(cs, me, axis=0, keepdims=False), me)
              for land, cs in zip(_chip_scatter(chip_sums), chip_sums)]
    halves = [_chip_sum(tag, land) for tag, land in zip(tags, landed)]
    reduced = [put(r, h, cc).reshape(-1, r.shape[-1]) for r, h in zip(_pair_join(halves), halves)]

    outs = {}
    for n, g in zip(BIG, reduced[:-1]):
        res = (g, *_adamw(n, g, two_d(n, w_loc[n]), two_d(n, m_loc[n]), two_d(n, v_loc[n])))
        outs[n] = [r.reshape(spec[n][0]) for r in res]
    blocks = [spec[n][0] for n in small]
    wp, mp, vp = (_pack([src[n] for n in small], SMALL_ROWS, F32) for src in (w_loc, m_loc, v_loc))
    res = (reduced[-1], *_adamw("small", reduced[-1], wp, mp, vp))
    unpacked = [_unpack(r, blocks) for r in res]
    for i, n in enumerate(small):
        outs[n] = [u[i] for u in unpacked]
    return (loss, grad_x, *[outs[n][k] for k in range(4) for n in names])
```

```python
import functools
import math

import jax
import jax.numpy as jnp
from jax import lax
from jax.experimental import pallas as pl
from jax.experimental.pallas import tpu as pltpu

F32 = jnp.float32
BF16 = jnp.bfloat16
MXU_DTYPE = jnp.bfloat16
MESH = pl.DeviceIdType.MESH

D_MODEL = 1024
EPS = 1e-6
SSD_HEADS = 16
SSD_HEADDIM = 64
HEAD_SHIFT = 6
SSD_GROUPS = 4
SSD_STATE = 128
SSD_CHUNK = 128
LRU_C = 8.0
LRU_ROWS = 256
HGRN_HEADS = 8
HGRN_HEADDIM = 128
HGRN_SUB = 32
HGRN_SUB_SHIFT = 5
HGRN_BLOCK = 128
HGRN_SCALE = HGRN_HEADDIM ** -0.5
CONV_ROWS = 512

ADAM_LR = 0.001
ADAM_B1 = 0.9
ADAM_B2 = 0.999
ADAM_EPS = 1e-08
ADAM_WD = 0.01
ADAM_STEP = 10

VMEM_LIMIT = 56 * 1024 * 1024
PACK_COLS = 1024
SMALL_ROWS = 288

WEIGHTS = (
    ("even_w_in", (1, 1024, 1288), (1, 1024, 5152), 2),
    ("even_conv_w", (1, 4, 768), (1, 4, 3072), 2),
    ("even_conv_b", (1, 3072), (1, 3072), None),
    ("ssd_a_log", (1, 2, 16), (1, 2, 16), None),
    ("ssd_dt_bias", (1, 2, 16), (1, 2, 16), None),
    ("ssd_d", (1, 16), (1, 16), None),
    ("ssd_norm_w", (1, 1024), (1, 1024), None),
    ("lru_w_a", (1, 2, 16, 64, 64), (1, 2, 16, 64, 64), None),
    ("lru_b_a", (1, 2, 256), (1, 2, 1024), 2),
    ("lru_w_x", (1, 2, 16, 64, 64), (1, 2, 16, 64, 64), None),
    ("lru_b_x", (1, 2, 256), (1, 2, 1024), 2),
    ("lru_lambda", (1, 2, 256), (1, 2, 1024), 2),
    ("even_w_out", (1, 512, 1024), (1, 2048, 1024), 1),
    ("odd_w_in", (1, 1024, 1280), (1, 1024, 5120), 2),
    ("hgrn_lb_logits", (2, 1024), (2, 1024), None),
    ("hgrn_norm_w", (1, 256), (1, 1024), 1),
    ("odd_w_out", (1, 256, 1024), (1, 1024, 1024), 1),
    ("norm_mix", (2, 1024), (2, 1024), None),
    ("norm_mlp", (2, 1024), (2, 1024), None),
    ("mlp_w1", (2, 1024, 1024), (2, 1024, 4096), 2),
    ("mlp_w2", (2, 1024, 1024), (2, 4096, 1024), 1),
    ("norm_final", (1024,), (1024,), None),
)
BIG = ("even_w_in", "even_w_out", "odd_w_in", "odd_w_out", "mlp_w1", "mlp_w2")
BIG_2D = {"even_w_in": (1024, 1288), "even_w_out": (512, 1024), "odd_w_in": (1024, 1280), "odd_w_out": (256, 1024),
          "mlp_w1": (2048, 1024), "mlp_w2": (2048, 1024)}
SMALL_SHARDED = ("even_conv_w", "lru_b_a", "lru_b_x", "lru_lambda", "hgrn_norm_w")


def _pcall(body, **kw):
    return pl.pallas_call(body, **kw)


def _params(**kw):
    return pltpu.CompilerParams(vmem_limit_bytes=VMEM_LIMIT, **kw)


def _tile(n, pref):
    if n <= pref:
        return n
    t = (pref // 128) * 128
    while n % t:
        t -= 128
    return t


def _dot(a, b, dims=(((1,), (0,)), ((), ()))):
    return lax.dot_general(a, b, dims, preferred_element_type=F32)


_NN = (((1,), (0,)), ((), ()))
_NT = (((1,), (1,)), ((), ()))
_TN = (((0,), (0,)), ((), ()))


def _mx(v):
    return v.astype(MXU_DTYPE)


def _dot01(a, b, dims=_NN, *, split, terms):
    acc, rest = None, (a if split == "a" else b)
    for _ in range(terms):
        piece = _mx(rest)
        part = _dot(piece, _mx(b), dims) if split == "a" else _dot(_mx(a), piece, dims)
        acc = part if acc is None else acc + part
        rest = rest - piece.astype(F32)
    return acc


def _mm(name, a, b, mode, *, out_dtype=F32, res=None, relu2=False, relu2_of=None, col_shards=1):
    if mode == "nn":
        (m, kk), (_, n) = a.shape, b.shape
    elif mode == "nt":
        (m, kk), (n, _) = a.shape, b.shape
    else:
        (kk, m), (_, n) = a.shape, b.shape
    assert res is None or relu2_of is None
    tk_pref = 1024
    if mode == "tn" and a.dtype.itemsize == 2 and b.dtype.itemsize == 2:
        tk_pref = 2048
    tm, tn, tk = _tile(m, 1024), _tile(n // col_shards, 1024), _tile(kk, tk_pref)
    nk = kk // tk
    dims = {"nn": _NN, "nt": _NT, "tn": _TN}[mode]
    a_spec = pl.BlockSpec((tk, tm), lambda i, j, k: (k, i)) if mode == "tn" else pl.BlockSpec((tm, tk), lambda i, j, k: (i, k))
    b_spec = pl.BlockSpec((tn, tk), lambda i, j, k: (j, k)) if mode == "nt" else pl.BlockSpec((tk, tn), lambda i, j, k: (k, j))
    o_spec = pl.BlockSpec((tm, tn), lambda i, j, k: (i, j))
    o_shape = (m, n)
    if col_shards > 1:
        assert tn * col_shards == n and res is None and not relu2
        o_spec = pl.BlockSpec((None, tm, tn), lambda i, j, k: (j, i, 0))
        o_shape = (col_shards, m, tn)
    extra = res if res is not None else relu2_of
    has_res = extra is not None

    def body(*refs):
        a_ref, b_ref = refs[0], refs[1]
        res_ref = refs[2] if has_res else None
        outs = refs[2 + has_res:2 + has_res + 1 + relu2]

        def finish(r):
            if res is not None:
                r = r + res_ref[...]
            if relu2_of is not None:
                r = r * (2.0 * jnp.maximum(res_ref[...], 0.0))
            if relu2:
                outs[0][...] = r
                outs[1][...] = jnp.square(jnp.maximum(r, 0.0)).astype(outs[1].dtype)
            else:
                outs[0][...] = r.astype(outs[0].dtype)

        prod = _dot(_mx(a_ref[...]), _mx(b_ref[...]), dims)
        if nk == 1:
            finish(prod)
            return
        acc = refs[-1]
        k = pl.program_id(2)

        @pl.when(k == 0)
        def _():
            acc[...] = prod

        @pl.when(k > 0)
        def _():
            acc[...] += prod

        @pl.when(k == nk - 1)
        def _():
            finish(acc[...])

    in_specs = [a_spec, b_spec] + ([o_spec] if has_res else [])
    if relu2:
        out_shape = (jax.ShapeDtypeStruct((m, n), F32), jax.ShapeDtypeStruct((m, n), BF16))
        out_specs = (o_spec, o_spec)
    else:
        out_shape = jax.ShapeDtypeStruct(o_shape, out_dtype)
        out_specs = o_spec
    args = (a, b) + ((extra,) if has_res else ())
    return _pcall(body, name=name, grid=(m // tm, n // tn, nk), in_specs=in_specs, out_specs=out_specs,
                  out_shape=out_shape, scratch_shapes=[pltpu.VMEM((tm, tn), F32)] if nk > 1 else [],
                  compiler_params=_params())(*args)


def _mm_sum_nt(name, parts, wblocks):
    m, n, npart = parts[0].shape[0], wblocks[0].shape[0], len(parts)
    tm, tn = _tile(m, 512), _tile(n, 1024)

    def body(*refs):
        acc = _dot(_mx(refs[0][...]), _mx(refs[npart][...]), _NT)
        for k in range(1, npart):
            acc = acc + _dot(_mx(refs[k][...]), _mx(refs[npart + k][...]), _NT)
        refs[-1][...] = acc

    in_specs = [pl.BlockSpec((tm, p.shape[1]), lambda i, j: (i, 0)) for p in parts]
    in_specs += [pl.BlockSpec((tn, w.shape[1]), lambda i, j: (j, 0)) for w in wblocks]
    return _pcall(body, name=name, grid=(m // tm, n // tn), in_specs=in_specs, out_specs=pl.BlockSpec((tm, tn), lambda i, j: (i, j)),
                  out_shape=jax.ShapeDtypeStruct((m, n), F32), compiler_params=_params())(*parts, *wblocks)


def _pw_fwd(name, f, ins, params, out_dtypes, tc, ncol, tm=256, groups=1):
    t = ins[0][0].shape[0]
    tm = min(tm, t)
    ni, npar = len(ins), len(params)
    gw = tc // groups

    def body(*refs):
        for g in range(groups):
            sl = slice(g * gw, (g + 1) * gw)
            vals = f(*[r[:, sl].astype(F32) for r in refs[:ni]], *[r[:, sl] for r in refs[ni:ni + npar]])
            for o, v in zip(refs[ni + npar:], vals):
                o[:, sl] = v.astype(o.dtype)

    in_specs = [pl.BlockSpec((tm, tc), lambda j, i, off=off: (i, off + j)) for _, off in ins]
    in_specs += [pl.BlockSpec((1, tc), lambda j, i, off=off: (0, off + j)) for _, off in params]
    out_specs = tuple(pl.BlockSpec((tm, tc), lambda j, i: (i, j)) for _ in out_dtypes)
    out_shape = tuple(jax.ShapeDtypeStruct((t, ncol * tc), d) for d in out_dtypes)
    return _pcall(body, name=name, grid=(ncol, t // tm), in_specs=in_specs, out_specs=out_specs, out_shape=out_shape,
                  compiler_params=_params())(*[a for a, _ in ins], *[p for p, _ in params])


def _pw_bwd(name, f, ins, params, douts, tc, ncol, want, adds=None, tm=256, out_dtypes=None, groups=1):
    adds = adds or {}
    out_dtypes = out_dtypes or [F32] * len(want)
    t = ins[0][0].shape[0]
    tm = min(tm, t)
    ni, npar, nd, na = len(ins), len(params), len(douts), len(adds)
    add_keys = sorted(adds)
    gw = tc // groups

    def body(*refs):
        in_refs, p_refs = refs[:ni], refs[ni:ni + npar]
        d_refs = refs[ni + npar:ni + npar + nd]
        a_refs = refs[ni + npar + nd:ni + npar + nd + na]
        o_refs = refs[ni + npar + nd + na:]
        for p in range(npar):
            @pl.when(pl.program_id(1) == 0)
            def _(o=o_refs[len(want) + p]):
                o[...] = jnp.zeros_like(o)

        for g in range(groups):
            sl = slice(g * gw, (g + 1) * gw)
            _, vjp = jax.vjp(f, *[r[:, sl].astype(F32) for r in in_refs], *[r[:, sl] for r in p_refs])
            cts = vjp(tuple(d[:, sl].astype(F32) for d in d_refs))
            for o, kidx in zip(o_refs[:len(want)], want):
                v = cts[kidx]
                if kidx in adds:
                    v = v + a_refs[add_keys.index(kidx)][:, sl]
                o[:, sl] = v.astype(o.dtype)
            for p in range(npar):
                o_refs[len(want) + p][:, sl] += cts[ni + p]

    in_specs = [pl.BlockSpec((tm, tc), lambda j, i, off=off: (i, off + j)) for _, off in ins]
    in_specs += [pl.BlockSpec((1, tc), lambda j, i, off=off: (0, off + j)) for _, off in params]
    in_specs += [pl.BlockSpec((tm, tc), lambda j, i: (i, j)) for _ in range(nd + na)]
    out_specs = tuple([pl.BlockSpec((tm, tc), lambda j, i: (i, j)) for _ in want]
                      + [pl.BlockSpec((1, tc), lambda j, i: (0, j)) for _ in params])
    out_shape = tuple([jax.ShapeDtypeStruct((t, ncol * tc), dt) for dt in out_dtypes]
                      + [jax.ShapeDtypeStruct((1, ncol * tc), F32) for _ in params])
    res = _pcall(body, name=name, grid=(ncol, t // tm), in_specs=in_specs, out_specs=out_specs, out_shape=out_shape,
                 compiler_params=_params())(*[a for a, _ in ins], *[p for p, _ in params], *douts, *[adds[k] for k in add_keys])
    return list(res[:len(want)]), list(res[len(want):])


def _rms(x, g):
    return (x * lax.rsqrt(jnp.mean(x * x, axis=-1, keepdims=True) + EPS)) * g


def _f_norm(x, g):
    return (_rms(x, g),)


def _f_silu(c):
    return (jax.nn.silu(c),)


def _f_softplus(d, b):
    return (jax.nn.softplus(d + b),)


def _f_add3(a, b, c):
    return (a + b + c,)


def _f_ssd_post(yf, yb, xs, z, dskip, nw):
    u = (yf + yb + dskip * xs) * jax.nn.silu(z)
    return (_rms(u, nw),)


def _neg_expm1(v):
    t = jnp.tanh(0.5 * v)
    return -2.0 * t / (1.0 - t)


def _f_lru_gates(pre_a, pre_x, u, ba, bx, lam):
    rg = jax.nn.sigmoid(pre_a + ba)
    ig = jax.nn.sigmoid(pre_x + bx)
    log_a = -LRU_C * rg * jax.nn.softplus(-lam)
    return jnp.exp(log_a), jnp.sqrt(_neg_expm1(2.0 * log_a)) * (ig * u)


def _f_lru_post(hf, hb, gate):
    return ((hf + hb) * jax.nn.gelu(gate),)


def _f_hgrn_pre(fr, l0, l1):
    lb = jax.nn.sigmoid(l1 - l0)
    k = (1.0 - lb) * jax.nn.sigmoid(-fr)
    return k, jnp.log1p(-k)


def _f_hgrn_post(of, ob, gate, nw):
    return (_rms(of + ob, nw) * jax.nn.silu(gate),)


def _loss_head(x, tgt, g, tm=256):
    t, d = x.shape
    tm = min(tm, t)

    def body(x_ref, t_ref, g_ref, dx_ref, dg_ref, loss_ref):
        tv = t_ref[...]

        def lf(xv, gv):
            return 0.5 * jnp.sum(jnp.mean(jnp.square(_rms(xv, gv) - tv), axis=-1))

        val, vjp = jax.vjp(lf, x_ref[...], g_ref[...])
        dx, dg = vjp(jnp.ones((), F32))
        dx_ref[...] = dx

        @pl.when(pl.program_id(0) == 0)
        def _():
            dg_ref[...] = jnp.zeros_like(dg_ref)
            loss_ref[...] = jnp.zeros_like(loss_ref)

        dg_ref[...] += dg
        loss_ref[...] += jnp.full(loss_ref.shape, val, F32)

    row = pl.BlockSpec((tm, d), lambda i: (i, 0))
    vec = pl.BlockSpec((1, d), lambda i: (0, 0))
    return _pcall(body, name="loss_head", grid=(t // tm,), in_specs=[row, row, vec],
                  out_specs=(row, vec, pl.BlockSpec((1, 128), lambda i: (0, 0))),
                  out_shape=(jax.ShapeDtypeStruct((t, d), F32), jax.ShapeDtypeStruct((1, d), F32),
                             jax.ShapeDtypeStruct((1, 128), F32)), compiler_params=_params())(x, tgt, g)


def _shifted(x, d, prev, nxt, first, last):
    r = x.shape[0]
    row = lax.broadcasted_iota(jnp.int32, x.shape, 0)
    if d < 0:
        out = pltpu.roll(x, -d, 0)
        for q in range(-d):
            pv = jnp.where(first, 0.0, prev[8 + d + q:8 + d + q + 1, :])
            out = jnp.where(row == q, pv, out)
        return out
    out = pltpu.roll(x, r - d, 0)
    for q in range(d):
        nv = jnp.where(last, 0.0, nxt[q:q + 1, :])
        out = jnp.where(row == r - d + q, nv, out)
    return out


def _halo_specs(ts, tc, s):
    nb8 = s // 8
    cur = pl.BlockSpec((None, ts, tc), lambda n, i, j: (n, i, j))
    prev = pl.BlockSpec((None, 8, tc), lambda n, i, j: (n, jnp.maximum(i * (ts // 8) - 1, 0), j))
    nxt = pl.BlockSpec((None, 8, tc), lambda n, i, j: (n, jnp.minimum((i + 1) * (ts // 8), nb8 - 1), j))
    return cur, prev, nxt


def _conv_fwd(p3, w, b, ncol, tc=1024):
    nbatch, s, _ = p3.shape
    ts = min(CONV_ROWS, s)
    nblk = s // ts

    def body(x_ref, pv_ref, nx_ref, w_ref, b_ref, o_ref):
        i = pl.program_id(1)
        first, last = i == 0, i == nblk - 1
        x, pv, nx = x_ref[...], pv_ref[...], nx_ref[...]
        wv = w_ref[...]
        out = b_ref[...] + wv[1:2] * x
        out = out + wv[0:1] * _shifted(x, -1, pv, nx, first, last)
        out = out + wv[2:3] * _shifted(x, 1, pv, nx, first, last)
        out = out + wv[3:4] * _shifted(x, 2, pv, nx, first, last)
        o_ref[...] = out

    cur, prev, nxt = _halo_specs(ts, tc, s)
    return _pcall(body, name="conv_fwd", grid=(nbatch, nblk, ncol),
                  in_specs=[cur, prev, nxt, pl.BlockSpec((4, tc), lambda n, i, j: (0, j)),
                            pl.BlockSpec((1, tc), lambda n, i, j: (0, j))],
                  out_specs=cur, out_shape=jax.ShapeDtypeStruct((nbatch, s, ncol * tc), F32),
                  compiler_params=_params())(p3, p3, p3, w, b)


def _conv_bwd(dc3, p3, w, col):
    nbatch, s, tc = dc3.shape
    ts = min(CONV_ROWS, s)
    nblk = s // ts

    def body(d_ref, dpv_ref, dnx_ref, x_ref, pv_ref, nx_ref, w_ref, dx_ref, dw_ref):
        n, i = pl.program_id(0), pl.program_id(1)
        first, last = i == 0, i == nblk - 1
        d, dpv, dnx = d_ref[...], dpv_ref[...], dnx_ref[...]
        x, pv, nx = x_ref[...], pv_ref[...], nx_ref[...]
        wv = w_ref[...]
        dx = wv[1:2] * d
        dx = dx + wv[0:1] * _shifted(d, 1, dpv, dnx, first, last)
        dx = dx + wv[2:3] * _shifted(d, -1, dpv, dnx, first, last)
        dx = dx + wv[3:4] * _shifted(d, -2, dpv, dnx, first, last)
        dx_ref[...] = dx.astype(dx_ref.dtype)

        @pl.when((n == 0) & (i == 0))
        def _():
            dw_ref[...] = jnp.zeros_like(dw_ref)

        dw_ref[0:1, :] += jnp.sum(d * _shifted(x, -1, pv, nx, first, last), axis=0, keepdims=True)
        dw_ref[1:2, :] += jnp.sum(d * x, axis=0, keepdims=True)
        dw_ref[2:3, :] += jnp.sum(d * _shifted(x, 1, pv, nx, first, last), axis=0, keepdims=True)
        dw_ref[3:4, :] += jnp.sum(d * _shifted(x, 2, pv, nx, first, last), axis=0, keepdims=True)
        dw_ref[4:5, :] += jnp.sum(d, axis=0, keepdims=True)

    nb8 = s // 8

    def specs(j):
        cur = pl.BlockSpec((None, ts, tc), lambda n, i: (n, i, j))
        prev = pl.BlockSpec((None, 8, tc), lambda n, i: (n, jnp.maximum(i * (ts // 8) - 1, 0), j))
        nxt = pl.BlockSpec((None, 8, tc), lambda n, i: (n, jnp.minimum((i + 1) * (ts // 8), nb8 - 1), j))
        return [cur, prev, nxt]

    return _pcall(body, name=f"conv_bwd{col}", grid=(nbatch, nblk),
                  in_specs=specs(0) + specs(col) + [pl.BlockSpec((4, tc), lambda n, i: (0, col))],
                  out_specs=(specs(0)[0], pl.BlockSpec((8, tc), lambda n, i: (0, 0))),
                  out_shape=(jax.ShapeDtypeStruct((nbatch, s, tc), BF16), jax.ShapeDtypeStruct((8, tc), F32)),
                  compiler_params=_params())(dc3, dc3, dc3, p3, p3, p3, w)


def _block_scan(coef, inp, reverse):
    r = coef.shape[0]
    row = lax.broadcasted_iota(jnp.int32, coef.shape, 0)
    a, b = coef, inp
    d = 1
    while d < r:
        if reverse:
            keep = row < r - d
            a_sh, b_sh = pltpu.roll(a, r - d, 0), pltpu.roll(b, r - d, 0)
        else:
            keep = row >= d
            a_sh, b_sh = pltpu.roll(a, d, 0), pltpu.roll(b, d, 0)
        b = b + a * jnp.where(keep, b_sh, 0.0)
        a = a * jnp.where(keep, a_sh, 1.0)
        d *= 2
    return a, b


def _lru_scan(a3, b3, reverse):
    nbatch, s, w = a3.shape
    ts = min(LRU_ROWS, s)
    nblk = s // ts
    edge = 0 if reverse else ts - 1

    def body(a_ref, b_ref, h_ref, carry):
        @pl.when(pl.program_id(1) == 0)
        def _():
            carry[...] = jnp.zeros_like(carry)

        ca, hb = _block_scan(a_ref[...], b_ref[...], reverse)
        h = hb + ca * carry[0:1, :]
        h_ref[...] = h
        carry[0:1, :] = h[edge:edge + 1, :]

    blk = pl.BlockSpec((None, ts, w), (lambda n, i: (n, nblk - 1 - i, 0)) if reverse else (lambda n, i: (n, i, 0)))
    return _pcall(body, name=f"lru_scan_r{int(reverse)}", grid=(nbatch, nblk), in_specs=[blk, blk], out_specs=blk,
                  out_shape=jax.ShapeDtypeStruct((nbatch, s, w), F32), scratch_shapes=[pltpu.VMEM((8, w), F32)],
                  compiler_params=_params())(a3, b3)


def _lru_scan_bwd(a3, h3, dh3, reverse):
    nbatch, s, w = a3.shape
    ts = min(LRU_ROWS, s)
    nblk = s // ts
    nb8 = s // 8
    tpb = ts // 8

    def body(a_ref, aa_ref, h_ref, hh_ref, dh_ref, g_ref, da_ref, carry):
        i = pl.program_id(1)

        @pl.when(i == 0)
        def _():
            carry[...] = jnp.zeros_like(carry)

        a, h = a_ref[...], h_ref[...]
        row = lax.broadcasted_iota(jnp.int32, a.shape, 0)
        if reverse:
            a_edge = jnp.where(i == 0, 0.0, aa_ref[7:8, :])
            c = jnp.where(row == 0, a_edge, pltpu.roll(a, 1, 0))
            h_edge = jnp.where(i == nblk - 1, 0.0, hh_ref[0:1, :])
            h_sh = jnp.where(row == ts - 1, h_edge, pltpu.roll(h, ts - 1, 0))
        else:
            a_edge = jnp.where(i == 0, 0.0, aa_ref[0:1, :])
            c = jnp.where(row == ts - 1, a_edge, pltpu.roll(a, ts - 1, 0))
            h_edge = jnp.where(i == nblk - 1, 0.0, hh_ref[7:8, :])
            h_sh = jnp.where(row == 0, h_edge, pltpu.roll(h, 1, 0))
        cc, gb = _block_scan(c, dh_ref[...], not reverse)
        g = gb + cc * carry[0:1, :]
        g_ref[...] = g
        carry[0:1, :] = g[ts - 1:ts, :] if reverse else g[0:1, :]
        da_ref[...] = g * h_sh

    if reverse:
        bi = lambda i: i
    else:
        bi = lambda i: nblk - 1 - i
    blk = pl.BlockSpec((None, ts, w), lambda n, i: (n, bi(i), 0))
    before = pl.BlockSpec((None, 8, w), lambda n, i: (n, jnp.maximum(bi(i) * tpb - 1, 0), 0))
    after = pl.BlockSpec((None, 8, w), lambda n, i: (n, jnp.minimum((bi(i) + 1) * tpb, nb8 - 1), 0))
    a_tile, h_tile = (before, after) if reverse else (after, before)
    return _pcall(body, name=f"lru_scan_bwd_r{int(reverse)}", grid=(nbatch, nblk), in_specs=[blk, a_tile, blk, h_tile, blk],
                  out_specs=(blk, blk),
                  out_shape=(jax.ShapeDtypeStruct((nbatch, s, w), F32), jax.ShapeDtypeStruct((nbatch, s, w), F32)),
                  scratch_shapes=[pltpu.VMEM((8, w), F32)], compiler_params=_params())(a3, a3, h3, h3, dh3)


def _head_expand(lane0):
    return (jnp.right_shift(lax.broadcasted_iota(jnp.int32, (128, 1024), 1), HEAD_SHIFT) + lane0
            == lax.broadcasted_iota(jnp.int32, (128, 1024), 0)).astype(F32)


def _head_reduce(lane0):
    return (jnp.right_shift(lax.broadcasted_iota(jnp.int32, (1024, 128), 0), HEAD_SHIFT) + lane0
            == lax.broadcasted_iota(jnp.int32, (1024, 128), 1)).astype(F32)


def _time_mask(q, reverse):
    ri = lax.broadcasted_iota(jnp.int32, (q, q), 0)
    ci = lax.broadcasted_iota(jnp.int32, (q, q), 1)
    return (ri <= ci) if reverse else (ri >= ci)


def _ssd_common(xs_ref, bc_ref, dt_ref, al_ref, reverse, lane0):
    q = xs_ref.shape[0]
    edge = 0 if reverse else q - 1
    dt = dt_ref[...]
    a = -jnp.exp(al_ref[...])
    mask = _time_mask(q, reverse)
    expand = _head_expand(lane0)
    cum = _dot01(mask.astype(F32), dt * a, split="b", terms=3)
    cum_x = _dot01(cum, expand, split="a", terms=3)
    dt_x = _dot01(dt, expand, split="a", terms=2)
    last_x = cum_x[edge:edge + 1, :]
    xs = xs_ref[...]
    bc = bc_ref[...]
    return dict(q=q, edge=edge, lane0=lane0, dt=dt, a=a, mask=mask, cum_t=cum.T, cum_x=cum_x, dt_x=dt_x, xs=xs,
                v=xs * dt_x, e_c=jnp.exp(cum_x), w=jnp.exp(last_x - cum_x), e_l=jnp.exp(last_x),
                bm=bc[:, :512], cm=bc[:, 512:])


def _ssd_decay(c, h):
    row = c["lane0"] + h
    seg = c["cum_x"][:, h * SSD_HEADDIM:h * SSD_HEADDIM + 1] - c["cum_t"][row:row + 1, :]
    return jnp.where(c["mask"], jnp.exp(jnp.minimum(seg, 0.0)), 0.0)


def _head_masks():
    lane = jnp.right_shift(lax.broadcasted_iota(jnp.int32, (1, 256), 1), HEAD_SHIFT)
    return [lane == e for e in range(4)]


def _ssd_fwd(xbc3, dt3, alog, reverse):
    nbatch, s, _ = xbc3.shape
    q = min(SSD_CHUNK, s)
    nc = s // q
    lane0 = SSD_HEADS * int(reverse)

    def body(xs_ref, bc_ref, dt_ref, al_ref, y_ref, st_ref, st):
        @pl.when(pl.program_id(1) == 0)
        def _():
            st[...] = jnp.zeros_like(st)

        st_ref[...] = st[...]
        c = _ssd_common(xs_ref, bc_ref, dt_ref, al_ref, reverse, lane0)
        hm = _head_masks()
        for g in range(SSD_GROUPS):
            sl = slice(g * 256, (g + 1) * 256)
            cg, bg = _mx(c["cm"][:, g * 128:(g + 1) * 128]), _mx(c["bm"][:, g * 128:(g + 1) * 128])
            cb = _dot(cg, bg, _NT)
            vg = c["v"][:, sl]
            s0 = st[:, sl]
            yg = _dot(cg, _mx(s0)) * c["e_c"][:, sl]
            for e in range(4):
                m = _ssd_decay(c, 4 * g + e) * cb
                yg = yg + _dot(_mx(m), _mx(jnp.where(hm[e], vg, 0.0)))
            y_ref[:, sl] = yg
            st[:, sl] = c["e_l"][:, sl] * s0 + _dot(bg, _mx(vg * c["w"][:, sl]), _TN)

    ck = (lambda i: nc - 1 - i) if reverse else (lambda i: i)
    xs_spec = pl.BlockSpec((None, q, 1024), lambda n, i: (n, ck(i), 0))
    bc_spec = pl.BlockSpec((None, q, 1024), lambda n, i: (n, ck(i), 1))
    dt_spec = pl.BlockSpec((None, q, 128), lambda n, i: (n, ck(i), 0))
    al_spec = pl.BlockSpec((1, 128), lambda n, i: (0, 0))
    st_spec = pl.BlockSpec((None, None, 128, 1024), lambda n, i: (n, ck(i), 0, 0))
    return _pcall(body, name=f"ssd_fwd_r{int(reverse)}", grid=(nbatch, nc), in_specs=[xs_spec, bc_spec, dt_spec, al_spec],
                  out_specs=(xs_spec, st_spec),
                  out_shape=(jax.ShapeDtypeStruct((nbatch, s, 1024), F32), jax.ShapeDtypeStruct((nbatch, nc, 128, 1024), F32)),
                  scratch_shapes=[pltpu.VMEM((128, 1024), F32)], compiler_params=_params())(xbc3, xbc3, dt3, alog)


def _ssd_bwd(xbc3, dt3, alog, st4, dy3, reverse, add_to=()):
    nbatch, s, _ = xbc3.shape
    q = min(SSD_CHUNK, s)
    nc = s // q
    lane0 = SSD_HEADS * int(reverse)
    nadd = len(add_to)

    def body(xs_ref, bc_ref, dt_ref, al_ref, st0_ref, dy_ref, *rest):
        adds, (dxs_ref, dbc_ref, ddt_ref, dal_ref, dst) = rest[:nadd], rest[nadd:]
        n, i = pl.program_id(0), pl.program_id(1)

        @pl.when(i == 0)
        def _():
            dst[...] = jnp.zeros_like(dst)

        @pl.when((i == 0) & (n == 0))
        def _():
            dal_ref[...] = jnp.zeros_like(dal_ref)

        c = _ssd_common(xs_ref, bc_ref, dt_ref, al_ref, reverse, lane0)
        hm = _head_masks()
        reduce_m = _head_reduce(lane0)
        s0_all, ds1_all, dy = st0_ref[...], dst[...], dy_ref[...]
        lane = lax.broadcasted_iota(jnp.int32, (q, 128), 1)
        sub = lax.broadcasted_iota(jnp.int32, (128, q), 0)
        rowacc = jnp.zeros((q, 128), F32)
        colacc_t = jnp.zeros((128, q), F32)
        dv_l, yst_l, dvbar_l, dk_l, dc_l = [], [], [], [], []
        for g in range(SSD_GROUPS):
            sl = slice(g * 256, (g + 1) * 256)
            cg, bg = _mx(c["cm"][:, g * 128:(g + 1) * 128]), _mx(c["bm"][:, g * 128:(g + 1) * 128])
            cb = _dot(cg, bg, _NT)
            vg, dyg, wg, ecg = c["v"][:, sl], dy[:, sl], c["w"][:, sl], c["e_c"][:, sl]
            s0, ds1 = _mx(s0_all[:, sl]), _mx(ds1_all[:, sl])
            dye = _mx(dyg * ecg)
            yst_l.append(_dot(cg, s0) * ecg)
            dcg = _dot(dye, s0, _NT)
            dst[:, sl] = c["e_l"][:, sl] * ds1_all[:, sl] + _dot(cg, dye, _TN)
            vbar = _mx(vg * wg)
            dvbar = _dot(bg, ds1)
            dvbar_l.append(dvbar)
            dvg = dvbar * wg
            dkg = _dot(vbar, ds1, _NT)
            for e in range(4):
                h = 4 * g + e
                m = _ssd_decay(c, h)
                dyh, vh = _mx(jnp.where(hm[e], dyg, 0.0)), _mx(jnp.where(hm[e], vg, 0.0))
                dvg = dvg + _dot(_mx(m * cb), dyh, _TN)
                dcb = _dot(dyh, vh, _NT) * m
                dcbb = _mx(dcb)
                dcg = dcg + _dot(dcbb, bg)
                dkg = dkg + _dot(dcbb, cg, _TN)
                wmat = dcb * cb
                rowacc = jnp.where(lane == lane0 + h, jnp.sum(wmat, axis=1, keepdims=True), rowacc)
                colacc_t = jnp.where(sub == lane0 + h, jnp.sum(wmat, axis=0, keepdims=True), colacc_t)
            dv_l.append(dvg)
            dk_l.append(dkg)
            dc_l.append(dcg)
        dv = jnp.concatenate(dv_l, axis=1)
        yst = jnp.concatenate(yst_l, axis=1)
        dvbar = jnp.concatenate(dvbar_l, axis=1)
        t1 = _dot01(dy * yst, reduce_m, split="a", terms=3)
        t2 = _dot01(c["v"] * c["w"] * dvbar, reduce_m, split="a", terms=3)
        dlast = jnp.sum(t2, axis=0, keepdims=True) + _dot01(
            c["e_l"] * jnp.sum(ds1_all * s0_all, axis=0, keepdims=True), reduce_m, split="a", terms=2)
        dcum = rowacc - colacc_t.T + t1 - t2
        dcum = dcum + jnp.where(lax.broadcasted_iota(jnp.int32, (q, 128), 0) == c["edge"], dlast, 0.0)
        dda = _dot01(c["mask"].astype(F32), dcum, _TN, split="b", terms=3)
        ddt = dda * c["a"] + _dot01(dv * c["xs"], reduce_m, split="a", terms=2)
        dal_ref[...] += jnp.sum(dda * c["dt"], axis=0, keepdims=True) * c["a"]
        dxs = dv * c["dt_x"]
        dbc = jnp.concatenate(dk_l + dc_l, axis=1)
        if nadd:
            for a_ref in adds[:-2]:
                dxs = dxs + a_ref[...]
            dbc = dbc + adds[-2][...]
            ddt = ddt + adds[-1][...]
        ddt_ref[...] = ddt
        dxs_ref[...] = dxs
        dbc_ref[...] = dbc

    ck = (lambda i: i) if reverse else (lambda i: nc - 1 - i)
    xs_spec = pl.BlockSpec((None, q, 1024), lambda n, i: (n, ck(i), 0))
    bc_spec = pl.BlockSpec((None, q, 1024), lambda n, i: (n, ck(i), 1))
    dt_spec = pl.BlockSpec((None, q, 128), lambda n, i: (n, ck(i), 0))
    al_spec = pl.BlockSpec((1, 128), lambda n, i: (0, 0))
    st_spec = pl.BlockSpec((None, None, 128, 1024), lambda n, i: (n, ck(i), 0, 0))
    return _pcall(body, name=f"ssd_bwd_r{int(reverse)}", grid=(nbatch, nc),
                  in_specs=[xs_spec, bc_spec, dt_spec, al_spec, st_spec, xs_spec] + [xs_spec] * (nadd - 1) + [dt_spec] * bool(nadd),
                  out_specs=(xs_spec, xs_spec, dt_spec, al_spec),
                  out_shape=(jax.ShapeDtypeStruct((nbatch, s, 1024), F32), jax.ShapeDtypeStruct((nbatch, s, 1024), F32),
                             jax.ShapeDtypeStruct((nbatch, s, 128), F32), jax.ShapeDtypeStruct((1, 128), F32)),
                  scratch_shapes=[pltpu.VMEM((128, 1024), F32)],
                  compiler_params=_params())(xbc3, xbc3, dt3, alog, st4, dy3, *add_to)


def _gla_block(q, k, g, reverse):
    bq = g.shape[0]
    nsub = bq // HGRN_SUB
    edge = 0 if reverse else bq - 1
    ri = lax.broadcasted_iota(jnp.int32, (bq, bq), 0)
    ci = lax.broadcasted_iota(jnp.int32, (bq, bq), 1)
    rb, cb = jnp.right_shift(ri, HGRN_SUB_SHIFT), jnp.right_shift(ci, HGRN_SUB_SHIFT)
    mask = (ri <= ci) if reverse else (ri >= ci)
    m_within = (mask & (rb == cb)).astype(F32)
    m_before = ((cb > rb) if reverse else (cb < rb)).astype(F32)
    bl = _dot01(m_within, g, split="b", terms=3)
    c = _dot01(m_before, g, split="b", terms=3)
    last = c[edge:edge + 1, :] + bl[edge:edge + 1, :]
    ebl, enbl, ec, elc = jnp.exp(bl), jnp.exp(-bl), jnp.exp(c), jnp.exp(last - c)
    qh = q * HGRN_SCALE * ebl
    kh = k * enbl
    blk = jnp.right_shift(lax.broadcasted_iota(jnp.int32, (bq, 1), 0), HGRN_SUB_SHIFT)
    scale = []
    for i in range(nsub):
        valid = (blk >= i) if reverse else (blk <= i)
        ex = jnp.where(valid, c[i * HGRN_SUB:i * HGRN_SUB + 1, :] - c, 0.0)
        scale.append(jnp.where(valid, jnp.exp(ex), 0.0))
    return dict(bq=bq, nsub=nsub, edge=edge, mask=mask, m_within=m_within, m_before=m_before, ebl=ebl, enbl=enbl, ec=ec,
                elc=elc, e_l=jnp.exp(last), qh=qh, qt=qh * ec, kh=kh, kb=kh * elc, scale=scale)


def _gla_scores(c, hs):
    keys = [_mx(c["kh"][:, hs] * c["scale"][i][:, hs]) for i in range(c["nsub"])]
    rows = [_dot(_mx(c["qh"][i * HGRN_SUB:(i + 1) * HGRN_SUB, hs]), keys[i], _NT) for i in range(c["nsub"])]
    return jnp.where(c["mask"], jnp.concatenate(rows, axis=0), 0.0), keys


def _gla_specs(nbatch, s, w, reverse_order):
    bq = min(HGRN_BLOCK, s)
    nblk = s // bq
    bi = (lambda i: nblk - 1 - i) if reverse_order else (lambda i: i)
    col = lambda cb: pl.BlockSpec((nbatch, bq, w), lambda i: (0, bi(i), cb))
    st_spec = pl.BlockSpec((nbatch, None, 128, w), lambda i: (0, bi(i), 0, 0))
    return bq, nblk, col, st_spec


def _gla_fwd(proj3, l0, l1, reverse):
    nbatch, s, w5 = proj3.shape
    w = w5 // 5
    bq, nblk, col, st_spec = _gla_specs(nbatch, s, w, reverse)
    vec = pl.BlockSpec((1, w), lambda i: (0, 0))

    def body(q_ref, f_ref, v_ref, l0_ref, l1_ref, o_ref, st_ref, st):
        @pl.when(pl.program_id(0) == 0)
        def _():
            st[...] = jnp.zeros_like(st)

        for b in range(nbatch):
            st_ref[b] = st[b]
            k, g = _f_hgrn_pre(f_ref[b], l0_ref[...], l1_ref[...])
            c = _gla_block(q_ref[b], k, g, reverse)
            v = v_ref[b]
            for h in range(HGRN_HEADS):
                hs = slice(h * 128, (h + 1) * 128)
                att, _ = _gla_scores(c, hs)
                vb = _mx(v[:, hs])
                s0 = st[b, :, hs]
                o_ref[b, :, hs] = _dot(_mx(att), vb) + _dot(_mx(c["qt"][:, hs]), _mx(s0), _NT)
                st[b, :, hs] = s0 * c["e_l"][:, hs] + _dot(vb, _mx(c["kb"][:, hs]), _TN)

    return _pcall(body, name=f"gla_fwd_r{int(reverse)}", grid=(nblk,),
                  in_specs=[col(0), col(1 + int(reverse)), col(3), vec, vec], out_specs=(col(0), st_spec),
                  out_shape=(jax.ShapeDtypeStruct((nbatch, s, w), F32), jax.ShapeDtypeStruct((nbatch, nblk, 128, w), F32)),
                  scratch_shapes=[pltpu.VMEM((nbatch, 128, w), F32)], compiler_params=_params())(proj3, proj3, proj3, l0, l1)


def _gla_bwd(proj3, l0, l1, st4, do3, reverse, add_to=None):
    nbatch, s, w5 = proj3.shape
    w = w5 // 5
    bq, nblk, col, st_spec = _gla_specs(nbatch, s, w, not reverse)
    nadd = 0 if add_to is None else 2
    vec = pl.BlockSpec((1, w), lambda i: (0, 0))

    def body(q_ref, f_ref, v_ref, l0_ref, l1_ref, st_ref, do_ref, *rest):
        adds, (dq_ref, df_ref, dv_ref, dl0_ref, dl1_ref, dst) = rest[:nadd], rest[nadd:]

        @pl.when(pl.program_id(0) == 0)
        def _():
            dst[...] = jnp.zeros_like(dst)
            dl0_ref[...] = jnp.zeros_like(dl0_ref)
            dl1_ref[...] = jnp.zeros_like(dl1_ref)

        row = lax.broadcasted_iota(jnp.int32, (bq, 128), 0)
        for b in range(nbatch):
            (k, g), pre_vjp = jax.vjp(_f_hgrn_pre, f_ref[b], l0_ref[...], l1_ref[...])
            c = _gla_block(q_ref[b], k, g, reverse)
            s0_all, ds1_all = st_ref[b], dst[b]
            v, dy = v_ref[b], do_ref[b]
            dbl_l, dc_l, dk_l = [], [], []
            for h in range(HGRN_HEADS):
                hs = slice(h * 128, (h + 1) * 128)
                att, keys = _gla_scores(c, hs)
                qh, qt, kh, kb = c["qh"][:, hs], c["qt"][:, hs], c["kh"][:, hs], c["kb"][:, hs]
                vb, dyb = _mx(v[:, hs]), _mx(dy[:, hs])
                s0, ds1 = s0_all[:, hs], ds1_all[:, hs]
                datt = _mx(jnp.where(c["mask"], _dot(dyb, vb, _NT), 0.0))
                dqh_rows = []
                dkh = jnp.zeros((bq, 128), F32)
                dc = jnp.zeros((bq, 128), F32)
                for i in range(c["nsub"]):
                    rs = slice(i * HGRN_SUB, (i + 1) * HGRN_SUB)
                    dqh_rows.append(_dot(datt[rs], keys[i]))
                    dki = _dot(datt[rs], _mx(qh[rs]), _TN)
                    sc = c["scale"][i][:, hs]
                    dkh = dkh + dki * sc
                    dex = dki * (kh * sc)
                    dc = dc - dex + jnp.where(row == i * HGRN_SUB, jnp.sum(dex, axis=0, keepdims=True), 0.0)
                dqt = _dot(dyb, _mx(s0))
                dkb = _dot(vb, _mx(ds1))
                dv = _dot(_mx(att), dyb, _TN) + _dot(_mx(kb), _mx(ds1), _NT)
                dst[b, :, hs] = c["e_l"][:, hs] * ds1 + _dot(dyb, _mx(qt), _TN)
                dqh = jnp.concatenate(dqh_rows, axis=0) + dqt * c["ec"][:, hs]
                dkh = dkh + dkb * c["elc"][:, hs]
                kbk = dkb * kb
                dlast = jnp.sum(kbk, axis=0, keepdims=True) + c["e_l"][:, hs] * jnp.sum(ds1 * s0, axis=0, keepdims=True)
                at_edge = jnp.where(row == c["edge"], dlast, 0.0)
                dc_l.append(dc + dqt * qt - kbk + at_edge)
                dbl_l.append(dqh * qh - dkh * kh + at_edge)
                dq = dqh * c["ebl"][:, hs] * HGRN_SCALE
                if nadd:
                    dq, dv = dq + adds[0][b, :, hs], dv + adds[1][b, :, hs]
                dq_ref[b, :, hs] = dq.astype(dq_ref.dtype)
                dv_ref[b, :, hs] = dv.astype(dv_ref.dtype)
                dk_l.append(dkh * c["enbl"][:, hs])
            dg = (_dot01(c["m_within"], jnp.concatenate(dbl_l, axis=1), _TN, split="b", terms=2)
                  + _dot01(c["m_before"], jnp.concatenate(dc_l, axis=1), _TN, split="b", terms=2))
            df, d0, d1 = pre_vjp((jnp.concatenate(dk_l, axis=1), dg))
            df_ref[b] = df.astype(df_ref.dtype)
            dl0_ref[...] += d0
            dl1_ref[...] += d1

    shp_sum = jax.ShapeDtypeStruct((nbatch, s, w), BF16 if nadd else F32)
    shp_vec = jax.ShapeDtypeStruct((1, w), F32)
    return _pcall(body, name=f"gla_bwd_r{int(reverse)}", grid=(nblk,),
                  in_specs=[col(0), col(1 + int(reverse)), col(3), vec, vec, st_spec, col(0)] + [col(0)] * nadd,
                  out_specs=(col(0), col(0), col(0), vec, vec),
                  out_shape=(shp_sum, jax.ShapeDtypeStruct((nbatch, s, w), BF16), shp_sum, shp_vec, shp_vec),
                  scratch_shapes=[pltpu.VMEM((nbatch, 128, w), F32)],
                  compiler_params=_params())(proj3, proj3, proj3, l0, l1, st4, do3, *(add_to or ()))


DIRS = (False, True)


def _block_diag(w):
    eye = jnp.eye(16, dtype=w.dtype)
    return (eye[:, None, :, None] * w[:, :, None, :]).reshape(1024, 1024)


def _diag_blocks(m):
    m4 = m.reshape(16, 64, 16, 64)
    return jnp.stack([m4[i, :, i, :] for i in range(16)], axis=0)


def _pad_lanes(v, n=128):
    return jnp.pad(v, [(0, 0)] * (v.ndim - 1) + [(0, n - v.shape[-1])])


def _mlp_fwd(tag, x, nw, w1, w2):
    (h,) = _pw_fwd(f"{tag}_norm", _f_norm, [(x, 0)], [(nw, 0)], [BF16], 1024, 1)
    a, r = _mm(f"{tag}_up", h, w1, "nn", relu2=True)
    return _mm(f"{tag}_down", r, w2, "nn", res=x), (h, a, r)


def _mlp_bwd(tag, x, nw, w1, w2, saved, dxo):
    h, a, r = saved
    dw2 = _mm(f"{tag}_dw2", r, dxo, "tn")
    da = _mm(f"{tag}_da", dxo, w2, "nt", relu2_of=a, out_dtype=BF16)
    dw1 = _mm(f"{tag}_dw1", h, da, "tn", col_shards=4)
    dh = _mm(f"{tag}_dh", da, w1, "nt")
    (dx,), (dnw,) = _pw_bwd(f"{tag}_dnorm", _f_norm, [(x, 0)], [(nw, 0)], [dh], 1024, 1, [0], adds={0: dxo})
    return dx, dw1, dw2, dnw


def _split_in0(pieces, dt_piece):
    tm = 256

    def body(p0, p1, p2, p3, p4, p5, o_ref):
        full = jnp.concatenate([p0[...], p1[...], p2[...], p3[...], p4[...], p5[:, :32]], axis=1)
        for j in range(4):
            o_ref[j] = full[:, 1288 * j:1288 * (j + 1)]

    blk = pl.BlockSpec((tm, 1024), lambda i: (i, 0))
    return _pcall(body, name="split_in0", grid=(1024 // tm,), in_specs=[blk] * 5 + [pl.BlockSpec((tm, 128), lambda i: (i, 0))],
                  out_specs=pl.BlockSpec((4, tm, 1288), lambda i: (0, i, 0)),
                  out_shape=jax.ShapeDtypeStruct((4, 1024, 1288), F32), compiler_params=_params())(*pieces, dt_piece)


def _assemble_in0(shards):
    tm = 256

    def body(s_ref, m_ref, d_ref):
        full = jnp.concatenate([s_ref[j] for j in range(4)], axis=1)
        m_ref[...] = full[:, :5120]
        d_ref[...] = jnp.concatenate([full[:, 5120:5152], jnp.zeros((tm, 96), full.dtype)], axis=1)

    return _pcall(body, name="assemble_in0", grid=(1024 // tm,), in_specs=[pl.BlockSpec((4, tm, 1288), lambda i: (0, i, 0))],
                  out_specs=(pl.BlockSpec((tm, 5120), lambda i: (i, 0)), pl.BlockSpec((tm, 128), lambda i: (i, 0))),
                  out_shape=(jax.ShapeDtypeStruct((1024, 5120), shards.dtype), jax.ShapeDtypeStruct((1024, 128), shards.dtype)),
                  compiler_params=_params())(shards)


def _local_step(x3, tgt3, w, w_main0, w_dt0):
    nb, s, d = x3.shape
    t = nb * s
    x0 = x3.reshape(t, d)
    tgt = tgt3.reshape(t, d)
    grads = {}
    row = lambda v: v.reshape(1, -1)
    to3 = lambda v: v.reshape(nb, s, v.shape[-1])
    to2 = lambda v: v.reshape(-1, v.shape[-1])

    conv_w, conv_b = w["even_conv_w"][0], row(w["even_conv_b"][0])
    nmix0 = row(w["norm_mix"][0])
    (h0,) = _pw_fwd("l0_norm", _f_norm, [(x0, 0)], [(nmix0, 0)], [BF16], 1024, 1)
    proj0 = _mm("l0_proj", h0, w_main0, "nn")
    dt_raw = _mm("l0_proj_dt", h0, w_dt0, "nn")
    conv = to2(_conv_fwd(to3(proj0), conv_w, conv_b, 3))
    (xbc,) = _pw_fwd("l0_silu", _f_silu, [(conv, 0)], [], [F32], 1024, 2)
    dt_bias = _pad_lanes(w["ssd_dt_bias"][0].reshape(1, 32))
    (dt,) = _pw_fwd("l0_dt", _f_softplus, [(dt_raw, 0)], [(dt_bias, 0)], [F32], 128, 1)
    dt3, xbc3 = to3(dt), to3(xbc)
    alog = _pad_lanes(w["ssd_a_log"][0].reshape(1, 32))
    ssd = [_ssd_fwd(xbc3, dt3, alog, r) for r in DIRS]
    yf, yb = to2(ssd[0][0]), to2(ssd[1][0])
    dskip = jnp.repeat(w["ssd_d"][0], SSD_HEADDIM).reshape(1, 1024)
    snw = row(w["ssd_norm_w"][0])
    ssd_ins = [(yf, 0), (yb, 0), (xbc, 0), (proj0, 3)]
    (ya,) = _pw_fwd("l0_ssd_post", _f_ssd_post, ssd_ins, [(dskip, 0), (snw, 0)], [BF16], 1024, 1, groups=SSD_GROUPS)
    u_lru = conv[:, 2048:]
    w_gates = [_block_diag(w[k][0, r]).astype(MXU_DTYPE) for r in range(2) for k in ("lru_w_a", "lru_w_x")]
    pre = [_mm(f"l0_lru_pre{i}", u_lru, wg, "nn") for i, wg in enumerate(w_gates)]
    lru_par = [[(row(w[k][0, r]), 0) for k in ("lru_b_a", "lru_b_x", "lru_lambda")] for r in range(2)]
    lru_ins = [[(pre[2 * r], 0), (pre[2 * r + 1], 0), (u_lru, 0)] for r in range(2)]
    ab = [_pw_fwd(f"l0_lru_gates{r}", _f_lru_gates, lru_ins[r], lru_par[r], [F32, F32], 1024, 1) for r in range(2)]
    hs = [_lru_scan(to3(ab[r][0]), to3(ab[r][1]), DIRS[r]) for r in range(2)]
    lru_post_ins = [(to2(hs[0]), 0), (to2(hs[1]), 0), (proj0, 4)]
    (ybm,) = _pw_fwd("l0_lru_post", _f_lru_post, lru_post_ins, [], [BF16], 1024, 1)
    w_out0 = w["even_w_out"][0]
    x1 = _mm("l0_out_a", ya, w_out0[:1024], "nn", res=x0)
    x1 = _mm("l0_out_b", ybm, w_out0[1024:], "nn", res=x1)
    nmlp0 = row(w["norm_mlp"][0])
    x2, mlp0 = _mlp_fwd("l0_mlp", x1, nmlp0, w["mlp_w1"][0], w["mlp_w2"][0])

    w_in1 = w["odd_w_in"][0]
    nmix1 = row(w["norm_mix"][1])
    (h1,) = _pw_fwd("l1_norm", _f_norm, [(x2, 0)], [(nmix1, 0)], [BF16], 1024, 1)
    proj1 = _mm("l1_proj", h1, w_in1, "nn")
    proj1_3 = to3(proj1)
    lb0, lb1 = row(w["hgrn_lb_logits"][0]), row(w["hgrn_lb_logits"][1])
    gla = [_gla_fwd(proj1_3, lb0, lb1, r) for r in DIRS]
    hnw = row(w["hgrn_norm_w"][0])
    hpost_ins = [(to2(gla[0][0]), 0), (to2(gla[1][0]), 0), (proj1, 4)]
    (yo,) = _pw_fwd("l1_hgrn_post", _f_hgrn_post, hpost_ins, [(hnw, 0)], [BF16], 1024, 1, groups=HGRN_HEADS)
    w_out1 = w["odd_w_out"][0]
    x3_ = _mm("l1_out", yo, w_out1, "nn", res=x2)
    nmlp1 = row(w["norm_mlp"][1])
    x4, mlp1 = _mlp_fwd("l1_mlp", x3_, nmlp1, w["mlp_w1"][1], w["mlp_w2"][1])

    dx4, dnf, loss = _loss_head(x4, tgt, row(w["norm_final"]))
    grads["norm_final"] = dnf.reshape(-1)

    dx3, dw1_1, dw2_1, dnmlp1 = _mlp_bwd("l1_mlp", x3_, nmlp1, w["mlp_w1"][1], w["mlp_w2"][1], mlp1, dx4)
    big = {"odd_w_out": _mm("l1_dwout", yo, dx3, "tn").reshape(4, 256, 1024)}
    dyo = _mm("l1_dyo", dx3, w_out1, "nt")
    (do, dgate1), (dhnw,) = _pw_bwd("l1_hgrn_post_b", _f_hgrn_post, hpost_ins, [(hnw, 0)], [dyo], 1024, 1, [0, 2],
                                    out_dtypes=[F32, BF16], groups=HGRN_HEADS)
    grads["hgrn_norm_w"] = dhnw
    do3 = to3(do)
    gb = [_gla_bwd(proj1_3, lb0, lb1, gla[0][1], do3, False)]
    gb.append(_gla_bwd(proj1_3, lb0, lb1, gla[1][1], do3, True, add_to=(gb[0][0], gb[0][2])))
    grads["hgrn_lb_logits"] = jnp.concatenate([gb[0][3] + gb[1][3], gb[0][4] + gb[1][4]], axis=0)
    dparts1 = [to2(gb[1][0]), to2(gb[0][1]), to2(gb[1][1]), to2(gb[1][2]), dgate1]
    dwin1 = jnp.concatenate([_mm(f"l1_dwin{i}", h1, dp, "tn") for i, dp in enumerate(dparts1)], axis=1)
    big["odd_w_in"] = dwin1.reshape(1024, 4, 1280).transpose(1, 0, 2)
    dh1 = _mm_sum_nt("l1_dh", dparts1, [w_in1[:, i * 1024:(i + 1) * 1024] for i in range(5)])
    (dx2,), (dnmix1,) = _pw_bwd("l1_dnorm", _f_norm, [(x2, 0)], [(nmix1, 0)], [dh1], 1024, 1, [0], adds={0: dx3})

    dx1, dw1_0, dw2_0, dnmlp0 = _mlp_bwd("l0_mlp", x1, nmlp0, w["mlp_w1"][0], w["mlp_w2"][0], mlp0, dx2)
    big["mlp_w1"] = jnp.concatenate([dw1_0, dw1_1], axis=1)
    big["mlp_w2"] = jnp.concatenate([dw2_0.reshape(4, 1024, 1024), dw2_1.reshape(4, 1024, 1024)], axis=1)
    grads["norm_mlp"] = jnp.concatenate([dnmlp0, dnmlp1], axis=0)
    big["even_w_out"] = jnp.concatenate([_mm("l0_dwout_a", ya, dx1, "tn"), _mm("l0_dwout_b", ybm, dx1, "tn")],
                                        axis=0).reshape(4, 512, 1024)
    dya = _mm("l0_dya", dx1, w_out0[:1024], "nt")
    dyb = _mm("l0_dyb", dx1, w_out0[1024:], "nt")
    (dh, dgate0), _ = _pw_bwd("l0_lru_post_b", _f_lru_post, lru_post_ins, [], [dyb], 1024, 1, [0, 2], out_dtypes=[F32, BF16])
    dh3 = to3(dh)
    dpre, du_parts, dlru = [], [], {k: [] for k in ("lru_b_a", "lru_b_x", "lru_lambda")}
    for r in range(2):
        g_r, da_r = _lru_scan_bwd(to3(ab[r][0]), hs[r], dh3, DIRS[r])
        (dpa, dpx, du_r), (dba, dbx, dlam) = _pw_bwd(f"l0_lru_gates_b{r}", _f_lru_gates, lru_ins[r], lru_par[r],
                                                     [to2(da_r), to2(g_r)], 1024, 1, [0, 1, 2],
                                                     out_dtypes=[BF16, BF16, F32])
        dpre += [dpa, dpx]
        du_parts.append(du_r)
        dlru["lru_b_a"].append(dba)
        dlru["lru_b_x"].append(dbx)
        dlru["lru_lambda"].append(dlam)
    for k, v in dlru.items():
        grads[k] = jnp.concatenate(v, axis=0)[None]
    dwg = [_diag_blocks(_mm(f"l0_dwgate{i}", u_lru, dp, "tn")) for i, dp in enumerate(dpre)]
    grads["lru_w_a"] = jnp.stack([dwg[0], dwg[2]])[None]
    grads["lru_w_x"] = jnp.stack([dwg[1], dwg[3]])[None]
    du_gate = _mm_sum_nt("l0_du_gate", dpre, w_gates)
    (du,) = _pw_fwd("l0_du", _f_add3, [(du_parts[0], 0), (du_parts[1], 0), (du_gate, 0)], [], [F32], 1024, 1)
    (dy, dxs_skip, dz), (ddskip, dsnw) = _pw_bwd("l0_ssd_post_b", _f_ssd_post, ssd_ins, [(dskip, 0), (snw, 0)], [dya],
                                                 1024, 1, [0, 2, 3], out_dtypes=[F32, F32, BF16], groups=SSD_GROUPS)
    grads["ssd_d"] = ddskip.reshape(SSD_HEADS, SSD_HEADDIM).sum(axis=1)[None]
    grads["ssd_norm_w"] = dsnw
    dy3 = to3(dy)
    sb0 = _ssd_bwd(xbc3, dt3, alog, ssd[0][1], dy3, False)
    sb1 = _ssd_bwd(xbc3, dt3, alog, ssd[1][1], dy3, True, add_to=(sb0[0], to3(dxs_skip), sb0[1], sb0[2]))
    grads["ssd_a_log"] = (sb0[3] + sb1[3])[:, :32].reshape(1, 2, 16)
    dconv = [_pw_bwd(f"l0_silu_b{j}", _f_silu, [(conv, j)], [], [to2(sb1[j])], 1024, 1, [0])[0][0] for j in range(2)]
    dconv.append(du)
    ddt = to2(sb1[2])
    (ddt_raw,), (ddtb,) = _pw_bwd("l0_dt_b", _f_softplus, [(dt_raw, 0)], [(dt_bias, 0)], [ddt], 128, 1, [0])
    grads["ssd_dt_bias"] = ddtb[:, :32].reshape(1, 2, 16)
    cb = [_conv_bwd(to3(d), to3(proj0), conv_w, j) for j, d in enumerate(dconv)]
    dcw = jnp.concatenate([c_[1] for c_ in cb], axis=1)
    grads["even_conv_w"] = dcw[:4][None]
    grads["even_conv_b"] = dcw[4:5]
    dparts0 = [to2(c_[0]) for c_ in cb] + [dz, dgate0]
    dwin0 = [_mm(f"l0_dwin{i}", h0, dp, "tn") for i, dp in enumerate(dparts0)]
    big["even_w_in"] = _split_in0(dwin0, _mm("l0_dwin_dt", h0, ddt_raw, "tn"))
    dh0 = _mm_sum_nt("l0_dh", dparts0 + [ddt_raw], [w_main0[:, i * 1024:(i + 1) * 1024] for i in range(5)] + [w_dt0])
    (dx0,), (dnmix0,) = _pw_bwd("l0_dnorm", _f_norm, [(x0, 0)], [(nmix0, 0)], [dh0], 1024, 1, [0], adds={0: dx1})
    grads["norm_mix"] = jnp.concatenate([dnmix0, dnmix1], axis=0)
    return loss, dx0.reshape(nb, s, d), grads, [big[n] for n in BIG]


ANY = pl.BlockSpec(memory_space=pl.ANY)


def _place():
    return lax.axis_index("x"), lax.axis_index("y"), lax.axis_index("c")


def _remote(src, dst, send_sems, recv_sems, k, to):
    return pltpu.make_async_remote_copy(src_ref=src, dst_ref=dst, send_sem=send_sems.at[k], recv_sem=recv_sems.at[k],
                                        device_id=to, device_id_type=MESH)


def _gather_chips(shards):
    n = len(shards)
    halves = [s.shape[0] // 2 for s in shards]

    def body(*refs):
        x_refs, out_refs = refs[:n], refs[n:2 * n]
        send_sems, recv_sems = refs[2 * n:]
        x, y, c = _place()
        sibling = (x, y, 1 - c)
        chips = [(1 - x, y), (x, 1 - y), (1 - x, 1 - y)]

        def blk(t, px, py, hc):
            return out_refs[t].at[2 * px + py, pl.ds(hc * halves[t], halves[t]), :]

        def src(t):
            return x_refs[t].at[pl.ds(c * halves[t], halves[t]), :]

        first = [_remote(src(t), blk(t, x, y, c), send_sems, recv_sems, 6 * t + j, (*chip, c))
                 for t in range(n) for j, chip in enumerate(chips)]
        for cp in first:
            cp.start()
        passed = []
        for t in range(n):
            for j, chip in enumerate(chips):
                _remote(src(t), blk(t, *chip, c), send_sems, recv_sems, 6 * t + j, (*chip, c)).wait_recv()
                cp = _remote(blk(t, *chip, c), blk(t, *chip, c), send_sems, recv_sems, 6 * t + 3 + j, sibling)
                cp.start()
                passed.append(cp)
        for t in range(n):
            for j, chip in enumerate(chips):
                _remote(src(t), blk(t, *chip, 1 - c), send_sems, recv_sems, 6 * t + 3 + j, sibling).wait_recv()
        for cp in first + passed:
            cp.wait_send()

    return _pcall(body, name="gather_weights", in_specs=[ANY] * n, out_specs=(ANY,) * n,
                  out_shape=tuple(jax.ShapeDtypeStruct((4,) + s.shape, s.dtype) for s in shards),
                  scratch_shapes=[pltpu.SemaphoreType.DMA((6 * n,)), pltpu.SemaphoreType.DMA((6 * n,))],
                  compiler_params=_params())(*shards)


def _pair_swap(gps):
    n = len(gps)
    halves = [g.shape[1] // 2 for g in gps]

    def body(*refs):
        g_refs, land_refs = refs[:n], refs[n:2 * n]
        send_sems, recv_sems = refs[2 * n:]
        x, y, c = _place()
        cps = [_remote(g_refs[t].at[j, pl.ds((1 - c) * halves[t], halves[t]), :], land_refs[t].at[j], send_sems, recv_sems,
                       4 * t + j, (x, y, 1 - c)) for t in range(n) for j in range(4)]
        for cp in cps:
            cp.start()
        for cp in cps:
            cp.wait()

    return _pcall(body, name="grad_pair_swap", in_specs=[ANY] * n, out_specs=(ANY,) * n,
                  out_shape=tuple(jax.ShapeDtypeStruct((4, h, g.shape[2]), F32) for g, h in zip(gps, halves)),
                  scratch_shapes=[pltpu.SemaphoreType.DMA((4 * n,)), pltpu.SemaphoreType.DMA((4 * n,))],
                  compiler_params=_params())(*gps)


def _pair_add(name, gp, land, cidx):
    _, half, cols = land.shape
    tr = _tile(half, 512)
    nh = half // tr

    def body(c_ref, g_ref, l_ref, o_ref):
        o_ref[...] = (g_ref[...] + l_ref[...]).astype(o_ref.dtype)

    grid_spec = pltpu.PrefetchScalarGridSpec(
        num_scalar_prefetch=1, grid=(4, nh),
        in_specs=[pl.BlockSpec((None, tr, cols), lambda j, i, c: (j, c[0] * nh + i, 0)),
                  pl.BlockSpec((None, tr, cols), lambda j, i, c: (j, i, 0))],
        out_specs=pl.BlockSpec((None, tr, cols), lambda j, i, c: (j, i, 0)))
    return _pcall(body, name=f"pair_add_{name}", grid_spec=grid_spec, out_shape=jax.ShapeDtypeStruct((4, half, cols), BF16),
                  compiler_params=_params())(cidx, gp, land)


def _chip_scatter(css):
    n = len(css)

    def body(*refs):
        s_refs, land_refs = refs[:n], refs[n:2 * n]
        send_sems, recv_sems = refs[2 * n:]
        x, y, c = _place()
        me = 2 * x + y
        chips = [(1 - x, y), (x, 1 - y), (1 - x, 1 - y)]
        cps = [_remote(s_refs[t].at[2 * px + py], land_refs[t].at[me], send_sems, recv_sems, 3 * t + j, (px, py, c))
               for t in range(n) for j, (px, py) in enumerate(chips)]
        for cp in cps:
            cp.start()
        for t in range(n):
            for j, (px, py) in enumerate(chips):
                _remote(s_refs[t].at[me], land_refs[t].at[2 * px + py], send_sems, recv_sems, 3 * t + j, (px, py, c)).wait_recv()
        for cp in cps:
            cp.wait_send()

    return _pcall(body, name="grad_chip_scatter", in_specs=[ANY] * n, out_specs=(ANY,) * n,
                  out_shape=tuple(jax.ShapeDtypeStruct(s.shape, s.dtype) for s in css),
                  scratch_shapes=[pltpu.SemaphoreType.DMA((3 * n,)), pltpu.SemaphoreType.DMA((3 * n,))],
                  compiler_params=_params())(*css)


def _chip_sum(name, land):
    _, half, cols = land.shape
    tr = _tile(half, 512)

    def body(l_ref, o_ref):
        o_ref[...] = ((l_ref[0].astype(F32) + l_ref[1].astype(F32)) + l_ref[2].astype(F32)) + l_ref[3].astype(F32)

    return _pcall(body, name=f"chip_sum_{name}", grid=(half // tr,),
                  in_specs=[pl.BlockSpec((4, tr, cols), lambda i: (0, i, 0))],
                  out_specs=pl.BlockSpec((tr, cols), lambda i: (i, 0)),
                  out_shape=jax.ShapeDtypeStruct((half, cols), F32), compiler_params=_params())(land)


def _pair_join(reds):
    n = len(reds)

    def body(*refs):
        r_refs, out_refs = refs[:n], refs[n:2 * n]
        send_sems, recv_sems = refs[2 * n:]
        x, y, c = _place()
        cps = [_remote(r_refs[t], out_refs[t].at[c], send_sems, recv_sems, t, (x, y, 1 - c)) for t in range(n)]
        for cp in cps:
            cp.start()
        for t in range(n):
            _remote(r_refs[t], out_refs[t].at[1 - c], send_sems, recv_sems, t, (x, y, 1 - c)).wait_recv()
        for cp in cps:
            cp.wait_send()

    return _pcall(body, name="grad_pair_join", in_specs=[ANY] * n, out_specs=(ANY,) * n,
                  out_shape=tuple(jax.ShapeDtypeStruct((2,) + r.shape, F32) for r in reds),
                  scratch_shapes=[pltpu.SemaphoreType.DMA((n,)), pltpu.SemaphoreType.DMA((n,))],
                  compiler_params=_params())(*reds)


def _adamw(name, g, w, m, v):
    rows, cols = g.shape
    tr = _tile(rows, 512)

    def body(g_ref, w_ref, m_ref, v_ref, d_ref, mo_ref, vo_ref):
        gv = g_ref[...]
        mn = ADAM_B1 * m_ref[...] + (1.0 - ADAM_B1) * gv
        vn = ADAM_B2 * v_ref[...] + (1.0 - ADAM_B2) * jnp.square(gv)
        m_hat = mn / (1.0 - ADAM_B1 ** ADAM_STEP)
        v_hat = vn / (1.0 - ADAM_B2 ** ADAM_STEP)
        d_ref[...] = -ADAM_LR * (m_hat / (jnp.sqrt(v_hat) + ADAM_EPS) + ADAM_WD * w_ref[...])
        mo_ref[...] = mn
        vo_ref[...] = vn

    blk = pl.BlockSpec((tr, cols), lambda i: (i, 0))
    shp = jax.ShapeDtypeStruct((rows, cols), F32)
    return _pcall(body, name=f"adamw_{name}", grid=(rows // tr,), in_specs=[blk] * 4, out_specs=(blk,) * 3,
                  out_shape=(shp,) * 3, compiler_params=_params())(g, w, m, v)


def _pack(pieces, rows, dtype):
    flat = jnp.concatenate([p.reshape(-1).astype(dtype) for p in pieces])
    return jnp.pad(flat, (0, rows * PACK_COLS - flat.shape[0])).reshape(rows, PACK_COLS)


def _unpack(pack, shapes):
    flat = pack.reshape(-1)
    out, off = [], 0
    for shp in shapes:
        n = math.prod(shp)
        out.append(flat[off:off + n].reshape(shp))
        off += n
    return out


def _shard_of(full, axis, j):
    n = full.shape[axis] // 4
    return lax.slice_in_dim(full, j * n, (j + 1) * n, axis=axis)


def kernel(x, even_w_in, even_conv_w, even_conv_b, ssd_a_log, ssd_dt_bias, ssd_d, ssd_norm_w, lru_w_a, lru_b_a, lru_w_x, lru_b_x, lru_lambda, even_w_out, odd_w_in, hgrn_lb_logits, hgrn_norm_w, odd_w_out, norm_mix, norm_mlp, mlp_w1, mlp_w2, norm_final, loss_target, m_even_w_in, m_even_conv_w, m_even_conv_b, m_ssd_a_log, m_ssd_dt_bias, m_ssd_d, m_ssd_norm_w, m_lru_w_a, m_lru_b_a, m_lru_w_x, m_lru_b_x, m_lru_lambda, m_even_w_out, m_odd_w_in, m_hgrn_lb_logits, m_hgrn_norm_w, m_odd_w_out, m_norm_mix, m_norm_mlp, m_mlp_w1, m_mlp_w2, m_norm_final, v_even_w_in, v_even_conv_w, v_even_conv_b, v_ssd_a_log, v_ssd_dt_bias, v_ssd_d, v_ssd_norm_w, v_lru_w_a, v_lru_b_a, v_lru_w_x, v_lru_b_x, v_lru_lambda, v_even_w_out, v_odd_w_in, v_hgrn_lb_logits, v_hgrn_norm_w, v_odd_w_out, v_norm_mix, v_norm_mlp, v_mlp_w1, v_mlp_w2, v_norm_final):
    names = [n for n, _, _, _ in WEIGHTS]
    w_loc = dict(zip(names, (even_w_in, even_conv_w, even_conv_b, ssd_a_log, ssd_dt_bias, ssd_d, ssd_norm_w, lru_w_a, lru_b_a, lru_w_x, lru_b_x, lru_lambda, even_w_out, odd_w_in, hgrn_lb_logits, hgrn_norm_w, odd_w_out, norm_mix, norm_mlp, mlp_w1, mlp_w2, norm_final)))
    m_loc = dict(zip(names, (m_even_w_in, m_even_conv_w, m_even_conv_b, m_ssd_a_log, m_ssd_dt_bias, m_ssd_d, m_ssd_norm_w, m_lru_w_a, m_lru_b_a, m_lru_w_x, m_lru_b_x, m_lru_lambda, m_even_w_out, m_odd_w_in, m_hgrn_lb_logits, m_hgrn_norm_w, m_odd_w_out, m_norm_mix, m_norm_mlp, m_mlp_w1, m_mlp_w2, m_norm_final)))
    v_loc = dict(zip(names, (v_even_w_in, v_even_conv_w, v_even_conv_b, v_ssd_a_log, v_ssd_dt_bias, v_ssd_d, v_ssd_norm_w, v_lru_w_a, v_lru_b_a, v_lru_w_x, v_lru_b_x, v_lru_lambda, v_even_w_out, v_odd_w_in, v_hgrn_lb_logits, v_hgrn_norm_w, v_odd_w_out, v_norm_mix, v_norm_mlp, v_mlp_w1, v_mlp_w2, v_norm_final)))
    spec = {n: (blk, full, ax) for n, blk, full, ax in WEIGHTS}

    small = [n for n in names if n not in BIG]
    two_d = lambda n, v: v.reshape(BIG_2D[n])

    me = 2 * lax.axis_index("x") + lax.axis_index("y")
    cc = lax.axis_index("c")
    put = lambda whole, part, k: lax.dynamic_update_slice_in_dim(whole, part[None], k, axis=0)
    own = [two_d(n, w_loc[n]).astype(BF16) for n in BIG] + [_pack([w_loc[n] for n in SMALL_SHARDED], 16, F32)]
    g_in0, g_out0, g_in1, g_out1, g_w1, g_w2, g_small = [put(g, o, me) for g, o in zip(_gather_chips(own), own)]
    w_main0, w_dt0 = _assemble_in0(g_in0)
    w_full = {n: w_loc[n] for n in names if spec[n][2] is None}
    w_full["even_w_out"] = g_out0.reshape(1, 2048, 1024)
    w_full["odd_w_in"] = jnp.concatenate([g_in1[j] for j in range(4)], axis=1)[None]
    w_full["odd_w_out"] = g_out1.reshape(1, 1024, 1024)
    w_full["mlp_w1"] = jnp.stack([jnp.concatenate([g_w1[j, l * 1024:(l + 1) * 1024] for j in range(4)], axis=1) for l in range(2)])
    w_full["mlp_w2"] = jnp.stack([jnp.concatenate([g_w2[j, l * 1024:(l + 1) * 1024] for j in range(4)], axis=0) for l in range(2)])
    shards = [_unpack(g_small[j], [spec[n][0] for n in SMALL_SHARDED]) for j in range(4)]
    for i, n in enumerate(SMALL_SHARDED):
        w_full[n] = jnp.concatenate([shards[j][i] for j in range(4)], axis=spec[n][2])

    loss_vec, grad_x, grads, big = _local_step(x, loss_target, w_full, w_main0, w_dt0)
    loss = lax.psum(loss_vec[0, 0], ("x", "y", "c"))

    def dest_pack(j):
        return _pack([grads[n].reshape(spec[n][1]) if spec[n][2] is None else _shard_of(grads[n].reshape(spec[n][1]), spec[n][2], j)
                      for n in small], SMALL_ROWS, F32)

    tensors = big + [jnp.stack([dest_pack(j) for j in range(4)])]
    tags = list(BIG) + ["small"]
    cidx = cc.astype(jnp.int32).reshape(1)
    chip_sums = [_pair_add(tag, g, land, cidx) for tag, g, land in zip(tags, tensors, _pair_swap(tensors))]
    landed = [put(land, lax.dynamic_index_in_dim(cs, me, axis=0, keepdims=False), me)
              for land, cs in zip(_chip_scatter(chip_sums), chip_sums)]
    halves = [_chip_sum(tag, land) for tag, land in zip(tags, landed)]
    reduced = [put(r, h, cc).reshape(-1, r.shape[-1]) for r, h in zip(_pair_join(halves), halves)]

    outs = {}
    for n, g in zip(BIG, reduced[:-1]):
        res = (g, *_adamw(n, g, two_d(n, w_loc[n]), two_d(n, m_loc[n]), two_d(n, v_loc[n])))
        outs[n] = [r.reshape(spec[n][0]) for r in res]
    blocks = [spec[n][0] for n in small]
    wp, mp, vp = (_pack([src[n] for n in small], SMALL_ROWS, F32) for src in (w_loc, m_loc, v_loc))
    res = (reduced[-1], *_adamw("small", reduced[-1], wp, mp, vp))
    unpacked = [_unpack(r, blocks) for r in res]
    for i, n in enumerate(small):
        outs[n] = [u[i] for u in unpacked]
    return (loss, grad_x, *[outs[n][k] for k in range(4) for n in names])
```

```python
import functools
import math

import jax
import jax.numpy as jnp
from jax import lax
from jax.experimental import pallas as pl
from jax.experimental.pallas import tpu as pltpu

F32 = jnp.float32
BF16 = jnp.bfloat16
MXU_DTYPE = jnp.bfloat16
MESH = pl.DeviceIdType.MESH

D_MODEL = 1024
EPS = 1e-6
SSD_HEADS = 16
SSD_HEADDIM = 64
HEAD_SHIFT = 6
SSD_GROUPS = 4
SSD_STATE = 128
SSD_CHUNK = 128
LRU_C = 8.0
LRU_ROWS = 256
HGRN_HEADS = 8
HGRN_HEADDIM = 128
HGRN_SUB = 32
HGRN_SUB_SHIFT = 5
HGRN_BLOCK = 128
HGRN_SCALE = HGRN_HEADDIM ** -0.5
CONV_ROWS = 512

ADAM_LR = 0.001
ADAM_B1 = 0.9
ADAM_B2 = 0.999
ADAM_EPS = 1e-08
ADAM_WD = 0.01
ADAM_STEP = 10

VMEM_LIMIT = 56 * 1024 * 1024
PACK_COLS = 1024
SMALL_ROWS = 288

WEIGHTS = (
    ("even_w_in", (1, 1024, 1288), (1, 1024, 5152), 2),
    ("even_conv_w", (1, 4, 768), (1, 4, 3072), 2),
    ("even_conv_b", (1, 3072), (1, 3072), None),
    ("ssd_a_log", (1, 2, 16), (1, 2, 16), None),
    ("ssd_dt_bias", (1, 2, 16), (1, 2, 16), None),
    ("ssd_d", (1, 16), (1, 16), None),
    ("ssd_norm_w", (1, 1024), (1, 1024), None),
    ("lru_w_a", (1, 2, 16, 64, 64), (1, 2, 16, 64, 64), None),
    ("lru_b_a", (1, 2, 256), (1, 2, 1024), 2),
    ("lru_w_x", (1, 2, 16, 64, 64), (1, 2, 16, 64, 64), None),
    ("lru_b_x", (1, 2, 256), (1, 2, 1024), 2),
    ("lru_lambda", (1, 2, 256), (1, 2, 1024), 2),
    ("even_w_out", (1, 512, 1024), (1, 2048, 1024), 1),
    ("odd_w_in", (1, 1024, 1280), (1, 1024, 5120), 2),
    ("hgrn_lb_logits", (2, 1024), (2, 1024), None),
    ("hgrn_norm_w", (1, 256), (1, 1024), 1),
    ("odd_w_out", (1, 256, 1024), (1, 1024, 1024), 1),
    ("norm_mix", (2, 1024), (2, 1024), None),
    ("norm_mlp", (2, 1024), (2, 1024), None),
    ("mlp_w1", (2, 1024, 1024), (2, 1024, 4096), 2),
    ("mlp_w2", (2, 1024, 1024), (2, 4096, 1024), 1),
    ("norm_final", (1024,), (1024,), None),
)
BIG = ("even_w_in", "even_w_out", "odd_w_in", "odd_w_out", "mlp_w1", "mlp_w2")
BIG_2D = {"even_w_in": (1024, 1288), "even_w_out": (512, 1024), "odd_w_in": (1024, 1280), "odd_w_out": (256, 1024),
          "mlp_w1": (2048, 1024), "mlp_w2": (2048, 1024)}
SMALL_SHARDED = ("even_conv_w", "lru_b_a", "lru_b_x", "lru_lambda", "hgrn_norm_w")


def _pcall(body, **kw):
    return pl.pallas_call(body, **kw)


def _params(**kw):
    return pltpu.CompilerParams(vmem_limit_bytes=VMEM_LIMIT, **kw)


def _tile(n, pref):
    if n <= pref:
        return n
    t = (pref // 128) * 128
    while n % t:
        t -= 128
    return t


def _dot(a, b, dims=(((1,), (0,)), ((), ()))):
    return lax.dot_general(a, b, dims, preferred_element_type=F32)


_NN = (((1,), (0,)), ((), ()))
_NT = (((1,), (1,)), ((), ()))
_TN = (((0,), (0,)), ((), ()))


def _mx(v):
    return v.astype(MXU_DTYPE)


def _dot01(a, b, dims=_NN, *, split, terms):
    acc, rest = None, (a if split == "a" else b)
    for _ in range(terms):
        piece = _mx(rest)
        part = _dot(piece, _mx(b), dims) if split == "a" else _dot(_mx(a), piece, dims)
        acc = part if acc is None else acc + part
        rest = rest - piece.astype(F32)
    return acc


def _mm(name, a, b, mode, *, out_dtype=F32, res=None, relu2=False, relu2_of=None, col_shards=1):
    if mode == "nn":
        (m, kk), (_, n) = a.shape, b.shape
    elif mode == "nt":
        (m, kk), (n, _) = a.shape, b.shape
    else:
        (kk, m), (_, n) = a.shape, b.shape
    assert res is None or relu2_of is None
    tk_pref = 1024
    if mode == "tn" and a.dtype.itemsize == 2 and b.dtype.itemsize == 2:
        tk_pref = 2048
    tm, tn, tk = _tile(m, 1024), _tile(n // col_shards, 1024), _tile(kk, tk_pref)
    nk = kk // tk
    dims = {"nn": _NN, "nt": _NT, "tn": _TN}[mode]
    a_spec = pl.BlockSpec((tk, tm), lambda i, j, k: (k, i)) if mode == "tn" else pl.BlockSpec((tm, tk), lambda i, j, k: (i, k))
    b_spec = pl.BlockSpec((tn, tk), lambda i, j, k: (j, k)) if mode == "nt" else pl.BlockSpec((tk, tn), lambda i, j, k: (k, j))
    o_spec = pl.BlockSpec((tm, tn), lambda i, j, k: (i, j))
    o_shape = (m, n)
    if col_shards > 1:
        assert tn * col_shards == n and res is None and not relu2
        o_spec = pl.BlockSpec((None, tm, tn), lambda i, j, k: (j, i, 0))
        o_shape = (col_shards, m, tn)
    extra = res if res is not None else relu2_of
    has_res = extra is not None

    def body(*refs):
        a_ref, b_ref = refs[0], refs[1]
        res_ref = refs[2] if has_res else None
        outs = refs[2 + has_res:2 + has_res + 1 + relu2]

        def finish(r):
            if res is not None:
                r = r + res_ref[...]
            if relu2_of is not None:
                r = r * (2.0 * jnp.maximum(res_ref[...], 0.0))
            if relu2:
                outs[0][...] = r
                outs[1][...] = jnp.square(jnp.maximum(r, 0.0)).astype(outs[1].dtype)
            else:
                outs[0][...] = r.astype(outs[0].dtype)

        prod = _dot(_mx(a_ref[...]), _mx(b_ref[...]), dims)
        if nk == 1:
            finish(prod)
            return
        acc = refs[-1]
        k = pl.program_id(2)

        @pl.when(k == 0)
        def _():
            acc[...] = prod

        @pl.when(k > 0)
        def _():
            acc[...] += prod

        @pl.when(k == nk - 1)
        def _():
            finish(acc[...])

    in_specs = [a_spec, b_spec] + ([o_spec] if has_res else [])
    if relu2:
        out_shape = (jax.ShapeDtypeStruct((m, n), F32), jax.ShapeDtypeStruct((m, n), BF16))
        out_specs = (o_spec, o_spec)
    else:
        out_shape = jax.ShapeDtypeStruct(o_shape, out_dtype)
        out_specs = o_spec
    args = (a, b) + ((extra,) if has_res else ())
    return _pcall(body, name=name, grid=(m // tm, n // tn, nk), in_specs=in_specs, out_specs=out_specs,
                  out_shape=out_shape, scratch_shapes=[pltpu.VMEM((tm, tn), F32)] if nk > 1 else [],
                  compiler_params=_params())(*args)


def _mm_sum_nt(name, parts, wblocks):
    m, n, npart = parts[0].shape[0], wblocks[0].shape[0], len(parts)
    tm, tn = _tile(m, 512), _tile(n, 1024)

    def body(*refs):
        acc = _dot(_mx(refs[0][...]), _mx(refs[npart][...]), _NT)
        for k in range(1, npart):
            acc = acc + _dot(_mx(refs[k][...]), _mx(refs[npart + k][...]), _NT)
        refs[-1][...] = acc

    in_specs = [pl.BlockSpec((tm, p.shape[1]), lambda i, j: (i, 0)) for p in parts]
    in_specs += [pl.BlockSpec((tn, w.shape[1]), lambda i, j: (j, 0)) for w in wblocks]
    return _pcall(body, name=name, grid=(m // tm, n // tn), in_specs=in_specs, out_specs=pl.BlockSpec((tm, tn), lambda i, j: (i, j)),
                  out_shape=jax.ShapeDtypeStruct((m, n), F32), compiler_params=_params())(*parts, *wblocks)


def _pw_fwd(name, f, ins, params, out_dtypes, tc, ncol, tm=256, groups=1):
    t = ins[0][0].shape[0]
    tm = min(tm, t)
    ni, npar = len(ins), len(params)
    gw = tc // groups

    def body(*refs):
        for g in range(groups):
            sl = slice(g * gw, (g + 1) * gw)
            vals = f(*[r[:, sl].astype(F32) for r in refs[:ni]], *[r[:, sl] for r in refs[ni:ni + npar]])
            for o, v in zip(refs[ni + npar:], vals):
                o[:, sl] = v.astype(o.dtype)

    in_specs = [pl.BlockSpec((tm, tc), lambda j, i, off=off: (i, off + j)) for _, off in ins]
    in_specs += [pl.BlockSpec((1, tc), lambda j, i, off=off: (0, off + j)) for _, off in params]
    out_specs = tuple(pl.BlockSpec((tm, tc), lambda j, i: (i, j)) for _ in out_dtypes)
    out_shape = tuple(jax.ShapeDtypeStruct((t, ncol * tc), d) for d in out_dtypes)
    return _pcall(body, name=name, grid=(ncol, t // tm), in_specs=in_specs, out_specs=out_specs, out_shape=out_shape,
                  compiler_params=_params())(*[a for a, _ in ins], *[p for p, _ in params])


def _pw_bwd(name, f, ins, params, douts, tc, ncol, want, adds=None, tm=256, out_dtypes=None, groups=1):
    adds = adds or {}
    out_dtypes = out_dtypes or [F32] * len(want)
    t = ins[0][0].shape[0]
    tm = min(tm, t)
    ni, npar, nd, na = len(ins), len(params), len(douts), len(adds)
    add_keys = sorted(adds)
    gw = tc // groups

    def body(*refs):
        in_refs, p_refs = refs[:ni], refs[ni:ni + npar]
        d_refs = refs[ni + npar:ni + npar + nd]
        a_refs = refs[ni + npar + nd:ni + npar + nd + na]
        o_refs = refs[ni + npar + nd + na:]
        for p in range(npar):
            @pl.when(pl.program_id(1) == 0)
            def _(o=o_refs[len(want) + p]):
                o[...] = jnp.zeros_like(o)

        for g in range(groups):
            sl = slice(g * gw, (g + 1) * gw)
            _, vjp = jax.vjp(f, *[r[:, sl].astype(F32) for r in in_refs], *[r[:, sl] for r in p_refs])
            cts = vjp(tuple(d[:, sl].astype(F32) for d in d_refs))
            for o, kidx in zip(o_refs[:len(want)], want):
                v = cts[kidx]
                if kidx in adds:
                    v = v + a_refs[add_keys.index(kidx)][:, sl]
                o[:, sl] = v.astype(o.dtype)
            for p in range(npar):
                o_refs[len(want) + p][:, sl] += cts[ni + p]

    in_specs = [pl.BlockSpec((tm, tc), lambda j, i, off=off: (i, off + j)) for _, off in ins]
    in_specs += [pl.BlockSpec((1, tc), lambda j, i, off=off: (0, off + j)) for _, off in params]
    in_specs += [pl.BlockSpec((tm, tc), lambda j, i: (i, j)) for _ in range(nd + na)]
    out_specs = tuple([pl.BlockSpec((tm, tc), lambda j, i: (i, j)) for _ in want]
                      + [pl.BlockSpec((1, tc), lambda j, i: (0, j)) for _ in params])
    out_shape = tuple([jax.ShapeDtypeStruct((t, ncol * tc), dt) for dt in out_dtypes]
                      + [jax.ShapeDtypeStruct((1, ncol * tc), F32) for _ in params])
    res = _pcall(body, name=name, grid=(ncol, t // tm), in_specs=in_specs, out_specs=out_specs, out_shape=out_shape,
                 compiler_params=_params())(*[a for a, _ in ins], *[p for p, _ in params], *douts, *[adds[k] for k in add_keys])
    return list(res[:len(want)]), list(res[len(want):])


def _rms(x, g):
    return (x * lax.rsqrt(jnp.mean(x * x, axis=-1, keepdims=True) + EPS)) * g


def _f_norm(x, g):
    return (_rms(x, g),)


def _f_softplus(d, b):
    return (jax.nn.softplus(d + b),)


def _f_add3(a, b, c):
    return (a + b + c,)


def _f_ssd_post(yf, yb, xs, z, dskip, nw):
    u = (yf + yb + dskip * xs) * jax.nn.silu(z)
    return (_rms(u, nw),)


def _neg_expm1(v):
    t = jnp.tanh(0.5 * v)
    return -2.0 * t / (1.0 - t)


def _f_lru_gates(pre_a, pre_x, u, ba, bx, lam):
    rg = jax.nn.sigmoid(pre_a + ba)
    ig = jax.nn.sigmoid(pre_x + bx)
    log_a = -LRU_C * rg * jax.nn.softplus(-lam)
    return jnp.exp(log_a), jnp.sqrt(_neg_expm1(2.0 * log_a)) * (ig * u)


def _f_lru_post(hf, hb, gate):
    return ((hf + hb) * jax.nn.gelu(gate),)


def _f_hgrn_pre(fr, l0, l1):
    lb = jax.nn.sigmoid(l1 - l0)
    k = (1.0 - lb) * jax.nn.sigmoid(-fr)
    return k, jnp.log1p(-k)


def _f_hgrn_post(of, ob, gate, nw):
    return (_rms(of + ob, nw) * jax.nn.silu(gate),)


def _loss_head(x, tgt, g, tm=256):
    t, d = x.shape
    tm = min(tm, t)

    def body(x_ref, t_ref, g_ref, dx_ref, dg_ref, loss_ref):
        tv = t_ref[...]

        def lf(xv, gv):
            return 0.5 * jnp.sum(jnp.mean(jnp.square(_rms(xv, gv) - tv), axis=-1))

        val, vjp = jax.vjp(lf, x_ref[...], g_ref[...])
        dx, dg = vjp(jnp.ones((), F32))
        dx_ref[...] = dx

        @pl.when(pl.program_id(0) == 0)
        def _():
            dg_ref[...] = jnp.zeros_like(dg_ref)
            loss_ref[...] = jnp.zeros_like(loss_ref)

        dg_ref[...] += dg
        loss_ref[...] += jnp.full(loss_ref.shape, val, F32)

    row = pl.BlockSpec((tm, d), lambda i: (i, 0))
    vec = pl.BlockSpec((1, d), lambda i: (0, 0))
    return _pcall(body, name="loss_head", grid=(t // tm,), in_specs=[row, row, vec],
                  out_specs=(row, vec, pl.BlockSpec((1, 128), lambda i: (0, 0))),
                  out_shape=(jax.ShapeDtypeStruct((t, d), F32), jax.ShapeDtypeStruct((1, d), F32),
                             jax.ShapeDtypeStruct((1, 128), F32)), compiler_params=_params())(x, tgt, g)


def _shifted(x, d, prev, nxt, first, last):
    r = x.shape[0]
    row = lax.broadcasted_iota(jnp.int32, x.shape, 0)
    if d < 0:
        out = pltpu.roll(x, -d, 0)
        for q in range(-d):
            pv = jnp.where(first, 0.0, prev[8 + d + q:8 + d + q + 1, :])
            out = jnp.where(row == q, pv, out)
        return out
    out = pltpu.roll(x, r - d, 0)
    for q in range(d):
        nv = jnp.where(last, 0.0, nxt[q:q + 1, :])
        out = jnp.where(row == r - d + q, nv, out)
    return out


def _conv_fwd(p3, w, b, col0, ncol, silu, tc=1024):
    nbatch, s, _ = p3.shape
    ts = min(CONV_ROWS, s)
    nblk = s // ts

    def body(x_ref, pv_ref, nx_ref, w_ref, b_ref, o_ref, *act_ref):
        i = pl.program_id(1)
        first, last = i == 0, i == nblk - 1
        x, pv, nx = x_ref[...], pv_ref[...], nx_ref[...]
        wv = w_ref[...]
        out = b_ref[...] + wv[1:2] * x
        out = out + wv[0:1] * _shifted(x, -1, pv, nx, first, last)
        out = out + wv[2:3] * _shifted(x, 1, pv, nx, first, last)
        out = out + wv[3:4] * _shifted(x, 2, pv, nx, first, last)
        o_ref[...] = out
        if silu:
            act_ref[0][...] = jax.nn.silu(out)

    nb8 = s // 8
    cur = pl.BlockSpec((None, ts, tc), lambda n, i, j: (n, i, col0 + j))
    prev = pl.BlockSpec((None, 8, tc), lambda n, i, j: (n, jnp.maximum(i * (ts // 8) - 1, 0), col0 + j))
    nxt = pl.BlockSpec((None, 8, tc), lambda n, i, j: (n, jnp.minimum((i + 1) * (ts // 8), nb8 - 1), col0 + j))
    out = pl.BlockSpec((None, ts, tc), lambda n, i, j: (n, i, j))
    shp = jax.ShapeDtypeStruct((nbatch, s, ncol * tc), F32)
    return _pcall(body, name=f"conv_fwd{col0}", grid=(nbatch, nblk, ncol),
                  in_specs=[cur, prev, nxt, pl.BlockSpec((4, tc), lambda n, i, j: (0, col0 + j)),
                            pl.BlockSpec((1, tc), lambda n, i, j: (0, col0 + j))],
                  out_specs=(out, out) if silu else out, out_shape=(shp, shp) if silu else shp,
                  compiler_params=_params())(p3, p3, p3, w, b)


def _conv_bwd(dc3, p3, w, col, conv3=None):
    nbatch, s, tc = dc3.shape
    ts = min(CONV_ROWS, s)
    nblk = s // ts
    silu = conv3 is not None

    def body(d_ref, dpv_ref, dnx_ref, x_ref, pv_ref, nx_ref, w_ref, *rest):
        n, i = pl.program_id(0), pl.program_id(1)
        first, last = i == 0, i == nblk - 1
        d, dpv, dnx = d_ref[...], dpv_ref[...], dnx_ref[...]
        if silu:
            d, dpv, dnx = [jax.vjp(jax.nn.silu, c_ref[...])[1](t)[0] for c_ref, t in zip(rest[:3], (d, dpv, dnx))]
        dx_ref, dw_ref = rest[3 * silu:]
        x, pv, nx = x_ref[...], pv_ref[...], nx_ref[...]
        wv = w_ref[...]
        dx = wv[1:2] * d
        dx = dx + wv[0:1] * _shifted(d, 1, dpv, dnx, first, last)
        dx = dx + wv[2:3] * _shifted(d, -1, dpv, dnx, first, last)
        dx = dx + wv[3:4] * _shifted(d, -2, dpv, dnx, first, last)
        dx_ref[...] = dx.astype(dx_ref.dtype)

        @pl.when((n == 0) & (i == 0))
        def _():
            dw_ref[...] = jnp.zeros_like(dw_ref)

        dw_ref[0:1, :] += jnp.sum(d * _shifted(x, -1, pv, nx, first, last), axis=0, keepdims=True)
        dw_ref[1:2, :] += jnp.sum(d * x, axis=0, keepdims=True)
        dw_ref[2:3, :] += jnp.sum(d * _shifted(x, 1, pv, nx, first, last), axis=0, keepdims=True)
        dw_ref[3:4, :] += jnp.sum(d * _shifted(x, 2, pv, nx, first, last), axis=0, keepdims=True)
        dw_ref[4:5, :] += jnp.sum(d, axis=0, keepdims=True)

    nb8 = s // 8

    def specs(j):
        cur = pl.BlockSpec((None, ts, tc), lambda n, i: (n, i, j))
        prev = pl.BlockSpec((None, 8, tc), lambda n, i: (n, jnp.maximum(i * (ts // 8) - 1, 0), j))
        nxt = pl.BlockSpec((None, 8, tc), lambda n, i: (n, jnp.minimum((i + 1) * (ts // 8), nb8 - 1), j))
        return [cur, prev, nxt]

    return _pcall(body, name=f"conv_bwd{col}", grid=(nbatch, nblk),
                  in_specs=specs(0) + specs(col) + [pl.BlockSpec((4, tc), lambda n, i: (0, col))] + specs(col) * silu,
                  out_specs=(specs(0)[0], pl.BlockSpec((8, tc), lambda n, i: (0, 0))),
                  out_shape=(jax.ShapeDtypeStruct((nbatch, s, tc), BF16), jax.ShapeDtypeStruct((8, tc), F32)),
                  compiler_params=_params())(dc3, dc3, dc3, p3, p3, p3, w, *([conv3] * 3 * silu))


def _block_scan(coef, inp, reverse):
    r = coef.shape[0]
    row = lax.broadcasted_iota(jnp.int32, coef.shape, 0)
    a, b = coef, inp
    d = 1
    while d < r:
        if reverse:
            keep = row < r - d
            a_sh, b_sh = pltpu.roll(a, r - d, 0), pltpu.roll(b, r - d, 0)
        else:
            keep = row >= d
            a_sh, b_sh = pltpu.roll(a, d, 0), pltpu.roll(b, d, 0)
        b = b + a * jnp.where(keep, b_sh, 0.0)
        a = a * jnp.where(keep, a_sh, 1.0)
        d *= 2
    return a, b


def _lru_scan(a3, b3, reverse):
    nbatch, s, w = a3.shape
    ts = min(LRU_ROWS, s)
    nblk = s // ts
    edge = 0 if reverse else ts - 1

    def body(a_ref, b_ref, h_ref, carry):
        @pl.when(pl.program_id(1) == 0)
        def _():
            carry[...] = jnp.zeros_like(carry)

        ca, hb = _block_scan(a_ref[...], b_ref[...], reverse)
        h = hb + ca * carry[0:1, :]
        h_ref[...] = h
        carry[0:1, :] = h[edge:edge + 1, :]

    blk = pl.BlockSpec((None, ts, w), (lambda n, i: (n, nblk - 1 - i, 0)) if reverse else (lambda n, i: (n, i, 0)))
    return _pcall(body, name=f"lru_scan_r{int(reverse)}", grid=(nbatch, nblk), in_specs=[blk, blk], out_specs=blk,
                  out_shape=jax.ShapeDtypeStruct((nbatch, s, w), F32), scratch_shapes=[pltpu.VMEM((8, w), F32)],
                  compiler_params=_params())(a3, b3)


def _lru_scan_bwd(a3, h3, dh3, reverse):
    nbatch, s, w = a3.shape
    ts = min(LRU_ROWS, s)
    nblk = s // ts
    nb8 = s // 8
    tpb = ts // 8

    def body(a_ref, aa_ref, h_ref, hh_ref, dh_ref, g_ref, da_ref, carry):
        i = pl.program_id(1)

        @pl.when(i == 0)
        def _():
            carry[...] = jnp.zeros_like(carry)

        a, h = a_ref[...], h_ref[...]
        row = lax.broadcasted_iota(jnp.int32, a.shape, 0)
        if reverse:
            a_edge = jnp.where(i == 0, 0.0, aa_ref[7:8, :])
            c = jnp.where(row == 0, a_edge, pltpu.roll(a, 1, 0))
            h_edge = jnp.where(i == nblk - 1, 0.0, hh_ref[0:1, :])
            h_sh = jnp.where(row == ts - 1, h_edge, pltpu.roll(h, ts - 1, 0))
        else:
            a_edge = jnp.where(i == 0, 0.0, aa_ref[0:1, :])
            c = jnp.where(row == ts - 1, a_edge, pltpu.roll(a, ts - 1, 0))
            h_edge = jnp.where(i == nblk - 1, 0.0, hh_ref[7:8, :])
            h_sh = jnp.where(row == 0, h_edge, pltpu.roll(h, 1, 0))
        cc, gb = _block_scan(c, dh_ref[...], not reverse)
        g = gb + cc * carry[0:1, :]
        g_ref[...] = g
        carry[0:1, :] = g[ts - 1:ts, :] if reverse else g[0:1, :]
        da_ref[...] = g * h_sh

    if reverse:
        bi = lambda i: i
    else:
        bi = lambda i: nblk - 1 - i
    blk = pl.BlockSpec((None, ts, w), lambda n, i: (n, bi(i), 0))
    before = pl.BlockSpec((None, 8, w), lambda n, i: (n, jnp.maximum(bi(i) * tpb - 1, 0), 0))
    after = pl.BlockSpec((None, 8, w), lambda n, i: (n, jnp.minimum((bi(i) + 1) * tpb, nb8 - 1), 0))
    a_tile, h_tile = (before, after) if reverse else (after, before)
    return _pcall(body, name=f"lru_scan_bwd_r{int(reverse)}", grid=(nbatch, nblk), in_specs=[blk, a_tile, blk, h_tile, blk],
                  out_specs=(blk, blk),
                  out_shape=(jax.ShapeDtypeStruct((nbatch, s, w), F32), jax.ShapeDtypeStruct((nbatch, s, w), F32)),
                  scratch_shapes=[pltpu.VMEM((8, w), F32)], compiler_params=_params())(a3, a3, h3, h3, dh3)


def _head_expand(lane0):
    return (jnp.right_shift(lax.broadcasted_iota(jnp.int32, (128, 1024), 1), HEAD_SHIFT) + lane0
            == lax.broadcasted_iota(jnp.int32, (128, 1024), 0)).astype(F32)


def _head_reduce(lane0):
    return (jnp.right_shift(lax.broadcasted_iota(jnp.int32, (1024, 128), 0), HEAD_SHIFT) + lane0
            == lax.broadcasted_iota(jnp.int32, (1024, 128), 1)).astype(F32)


def _time_mask(q, reverse):
    ri = lax.broadcasted_iota(jnp.int32, (q, q), 0)
    ci = lax.broadcasted_iota(jnp.int32, (q, q), 1)
    return (ri <= ci) if reverse else (ri >= ci)


def _ssd_common(xs_ref, bc_ref, dt_ref, al_ref, reverse, lane0):
    q = xs_ref.shape[0]
    edge = 0 if reverse else q - 1
    dt = dt_ref[...]
    a = -jnp.exp(al_ref[...])
    mask = _time_mask(q, reverse)
    expand = _head_expand(lane0)
    cum = _dot01(mask.astype(F32), dt * a, split="b", terms=3)
    cum_x = _dot01(cum, expand, split="a", terms=3)
    dt_x = _dot01(dt, expand, split="a", terms=2)
    last_x = cum_x[edge:edge + 1, :]
    xs = xs_ref[...]
    bc = bc_ref[...]
    return dict(q=q, edge=edge, lane0=lane0, dt=dt, a=a, mask=mask, cum_t=cum.T, cum_x=cum_x, dt_x=dt_x, xs=xs,
                v=xs * dt_x, e_c=jnp.exp(cum_x), w=jnp.exp(last_x - cum_x), e_l=jnp.exp(last_x),
                bm=bc[:, :512], cm=bc[:, 512:])


def _ssd_decay(c, h):
    row = c["lane0"] + h
    seg = c["cum_x"][:, h * SSD_HEADDIM:h * SSD_HEADDIM + 1] - c["cum_t"][row:row + 1, :]
    return jnp.where(c["mask"], jnp.exp(jnp.minimum(seg, 0.0)), 0.0)


def _head_masks():
    lane = jnp.right_shift(lax.broadcasted_iota(jnp.int32, (1, 256), 1), HEAD_SHIFT)
    return [lane == e for e in range(4)]


def _ssd_fwd(xbc3, dt3, alog, reverse):
    nbatch, s, _ = xbc3.shape
    q = min(SSD_CHUNK, s)
    nc = s // q
    lane0 = SSD_HEADS * int(reverse)

    def body(xs_ref, bc_ref, dt_ref, al_ref, y_ref, st_ref, st):
        @pl.when(pl.program_id(1) == 0)
        def _():
            st[...] = jnp.zeros_like(st)

        st_ref[...] = st[...]
        c = _ssd_common(xs_ref, bc_ref, dt_ref, al_ref, reverse, lane0)
        hm = _head_masks()
        for g in range(SSD_GROUPS):
            sl = slice(g * 256, (g + 1) * 256)
            cg, bg = _mx(c["cm"][:, g * 128:(g + 1) * 128]), _mx(c["bm"][:, g * 128:(g + 1) * 128])
            cb = _dot(cg, bg, _NT)
            vg = c["v"][:, sl]
            s0 = st[:, sl]
            yg = _dot(cg, _mx(s0)) * c["e_c"][:, sl]
            for e in range(4):
                m = _ssd_decay(c, 4 * g + e) * cb
                yg = yg + _dot(_mx(m), _mx(jnp.where(hm[e], vg, 0.0)))
            y_ref[:, sl] = yg
            st[:, sl] = c["e_l"][:, sl] * s0 + _dot(bg, _mx(vg * c["w"][:, sl]), _TN)

    ck = (lambda i: nc - 1 - i) if reverse else (lambda i: i)
    xs_spec = pl.BlockSpec((None, q, 1024), lambda n, i: (n, ck(i), 0))
    bc_spec = pl.BlockSpec((None, q, 1024), lambda n, i: (n, ck(i), 1))
    dt_spec = pl.BlockSpec((None, q, 128), lambda n, i: (n, ck(i), 0))
    al_spec = pl.BlockSpec((1, 128), lambda n, i: (0, 0))
    st_spec = pl.BlockSpec((None, None, 128, 1024), lambda n, i: (n, ck(i), 0, 0))
    return _pcall(body, name=f"ssd_fwd_r{int(reverse)}", grid=(nbatch, nc), in_specs=[xs_spec, bc_spec, dt_spec, al_spec],
                  out_specs=(xs_spec, st_spec),
                  out_shape=(jax.ShapeDtypeStruct((nbatch, s, 1024), F32), jax.ShapeDtypeStruct((nbatch, nc, 128, 1024), F32)),
                  scratch_shapes=[pltpu.VMEM((128, 1024), F32)], compiler_params=_params())(xbc3, xbc3, dt3, alog)


def _ssd_bwd(xbc3, dt3, alog, st4, dy3, reverse, add_to=()):
    nbatch, s, _ = xbc3.shape
    q = min(SSD_CHUNK, s)
    nc = s // q
    lane0 = SSD_HEADS * int(reverse)
    nadd = len(add_to)

    def body(xs_ref, bc_ref, dt_ref, al_ref, st0_ref, dy_ref, *rest):
        adds, (dxs_ref, dbc_ref, ddt_ref, dal_ref, dst) = rest[:nadd], rest[nadd:]
        n, i = pl.program_id(0), pl.program_id(1)

        @pl.when(i == 0)
        def _():
            dst[...] = jnp.zeros_like(dst)

        @pl.when((i == 0) & (n == 0))
        def _():
            dal_ref[...] = jnp.zeros_like(dal_ref)

        c = _ssd_common(xs_ref, bc_ref, dt_ref, al_ref, reverse, lane0)
        hm = _head_masks()
        reduce_m = _head_reduce(lane0)
        s0_all, ds1_all, dy = st0_ref[...], dst[...], dy_ref[...]
        lane = lax.broadcasted_iota(jnp.int32, (q, 128), 1)
        sub = lax.broadcasted_iota(jnp.int32, (128, q), 0)
        rowacc = jnp.zeros((q, 128), F32)
        colacc_t = jnp.zeros((128, q), F32)
        dv_l, yst_l, dvbar_l, dk_l, dc_l = [], [], [], [], []
        for g in range(SSD_GROUPS):
            sl = slice(g * 256, (g + 1) * 256)
            cg, bg = _mx(c["cm"][:, g * 128:(g + 1) * 128]), _mx(c["bm"][:, g * 128:(g + 1) * 128])
            cb = _dot(cg, bg, _NT)
            vg, dyg, wg, ecg = c["v"][:, sl], dy[:, sl], c["w"][:, sl], c["e_c"][:, sl]
            s0, ds1 = _mx(s0_all[:, sl]), _mx(ds1_all[:, sl])
            dye = _mx(dyg * ecg)
            yst_l.append(_dot(cg, s0) * ecg)
            dcg = _dot(dye, s0, _NT)
            dst[:, sl] = c["e_l"][:, sl] * ds1_all[:, sl] + _dot(cg, dye, _TN)
            vbar = _mx(vg * wg)
            dvbar = _dot(bg, ds1)
            dvbar_l.append(dvbar)
            dvg = dvbar * wg
            dkg = _dot(vbar, ds1, _NT)
            for e in range(4):
                h = 4 * g + e
                m = _ssd_decay(c, h)
                dyh, vh = _mx(jnp.where(hm[e], dyg, 0.0)), _mx(jnp.where(hm[e], vg, 0.0))
                dvg = dvg + _dot(_mx(m * cb), dyh, _TN)
                dcb = _dot(dyh, vh, _NT) * m
                dcbb = _mx(dcb)
                dcg = dcg + _dot(dcbb, bg)
                dkg = dkg + _dot(dcbb, cg, _TN)
                wmat = dcb * cb
                rowacc = jnp.where(lane == lane0 + h, jnp.sum(wmat, axis=1, keepdims=True), rowacc)
                colacc_t = jnp.where(sub == lane0 + h, jnp.sum(wmat, axis=0, keepdims=True), colacc_t)
            dv_l.append(dvg)
            dk_l.append(dkg)
            dc_l.append(dcg)
        dv = jnp.concatenate(dv_l, axis=1)
        yst = jnp.concatenate(yst_l, axis=1)
        dvbar = jnp.concatenate(dvbar_l, axis=1)
        t1 = _dot01(dy * yst, reduce_m, split="a", terms=3)
        t2 = _dot01(c["v"] * c["w"] * dvbar, reduce_m, split="a", terms=3)
        dlast = jnp.sum(t2, axis=0, keepdims=True) + _dot01(
            c["e_l"] * jnp.sum(ds1_all * s0_all, axis=0, keepdims=True), reduce_m, split="a", terms=2)
        dcum = rowacc - colacc_t.T + t1 - t2
        dcum = dcum + jnp.where(lax.broadcasted_iota(jnp.int32, (q, 128), 0) == c["edge"], dlast, 0.0)
        dda = _dot01(c["mask"].astype(F32), dcum, _TN, split="b", terms=3)
        ddt = dda * c["a"] + _dot01(dv * c["xs"], reduce_m, split="a", terms=2)
        dal_ref[...] += jnp.sum(dda * c["dt"], axis=0, keepdims=True) * c["a"]
        dxs = dv * c["dt_x"]
        dbc = jnp.concatenate(dk_l + dc_l, axis=1)
        if nadd:
            for a_ref in adds[:-2]:
                dxs = dxs + a_ref[...]
            dbc = dbc + adds[-2][...]
            ddt = ddt + adds[-1][...]
        ddt_ref[...] = ddt
        dxs_ref[...] = dxs
        dbc_ref[...] = dbc

    ck = (lambda i: i) if reverse else (lambda i: nc - 1 - i)
    xs_spec = pl.BlockSpec((None, q, 1024), lambda n, i: (n, ck(i), 0))
    bc_spec = pl.BlockSpec((None, q, 1024), lambda n, i: (n, ck(i), 1))
    dt_spec = pl.BlockSpec((None, q, 128), lambda n, i: (n, ck(i), 0))
    al_spec = pl.BlockSpec((1, 128), lambda n, i: (0, 0))
    st_spec = pl.BlockSpec((None, None, 128, 1024), lambda n, i: (n, ck(i), 0, 0))
    return _pcall(body, name=f"ssd_bwd_r{int(reverse)}", grid=(nbatch, nc),
                  in_specs=[xs_spec, bc_spec, dt_spec, al_spec, st_spec, xs_spec] + [xs_spec] * (nadd - 1) + [dt_spec] * bool(nadd),
                  out_specs=(xs_spec, xs_spec, dt_spec, al_spec),
                  out_shape=(jax.ShapeDtypeStruct((nbatch, s, 1024), F32), jax.ShapeDtypeStruct((nbatch, s, 1024), F32),
                             jax.ShapeDtypeStruct((nbatch, s, 128), F32), jax.ShapeDtypeStruct((1, 128), F32)),
                  scratch_shapes=[pltpu.VMEM((128, 1024), F32)],
                  compiler_params=_params())(xbc3, xbc3, dt3, alog, st4, dy3, *add_to)


def _gla_block(q, k, g, reverse):
    bq = g.shape[0]
    nsub = bq // HGRN_SUB
    edge = 0 if reverse else bq - 1
    ri = lax.broadcasted_iota(jnp.int32, (bq, bq), 0)
    ci = lax.broadcasted_iota(jnp.int32, (bq, bq), 1)
    rb, cb = jnp.right_shift(ri, HGRN_SUB_SHIFT), jnp.right_shift(ci, HGRN_SUB_SHIFT)
    mask = (ri <= ci) if reverse else (ri >= ci)
    m_within = (mask & (rb == cb)).astype(F32)
    m_before = ((cb > rb) if reverse else (cb < rb)).astype(F32)
    bl = _dot01(m_within, g, split="b", terms=3)
    c = _dot01(m_before, g, split="b", terms=3)
    last = c[edge:edge + 1, :] + bl[edge:edge + 1, :]
    ebl, enbl, ec, elc = jnp.exp(bl), jnp.exp(-bl), jnp.exp(c), jnp.exp(last - c)
    qh = q * HGRN_SCALE * ebl
    kh = k * enbl
    blk = jnp.right_shift(lax.broadcasted_iota(jnp.int32, (bq, 1), 0), HGRN_SUB_SHIFT)
    scale = []
    for i in range(nsub):
        valid = (blk >= i) if reverse else (blk <= i)
        ex = jnp.where(valid, c[i * HGRN_SUB:i * HGRN_SUB + 1, :] - c, 0.0)
        scale.append(jnp.where(valid, jnp.exp(ex), 0.0))
    return dict(bq=bq, nsub=nsub, edge=edge, mask=mask, m_within=m_within, m_before=m_before, ebl=ebl, enbl=enbl, ec=ec,
                elc=elc, e_l=jnp.exp(last), qh=qh, qt=qh * ec, kh=kh, kb=kh * elc, scale=scale)


def _gla_scores(c, hs):
    keys = [_mx(c["kh"][:, hs] * c["scale"][i][:, hs]) for i in range(c["nsub"])]
    rows = [_dot(_mx(c["qh"][i * HGRN_SUB:(i + 1) * HGRN_SUB, hs]), keys[i], _NT) for i in range(c["nsub"])]
    return jnp.where(c["mask"], jnp.concatenate(rows, axis=0), 0.0), keys


def _gla_specs(nbatch, s, w, reverse_order):
    bq = min(HGRN_BLOCK, s)
    nblk = s // bq
    bi = (lambda i: nblk - 1 - i) if reverse_order else (lambda i: i)
    col = lambda cb: pl.BlockSpec((nbatch, bq, w), lambda i: (0, bi(i), cb))
    st_spec = pl.BlockSpec((nbatch, None, 128, w), lambda i: (0, bi(i), 0, 0))
    return bq, nblk, col, st_spec


def _gla_fwd(proj3, l0, l1, reverse):
    nbatch, s, w5 = proj3.shape
    w = w5 // 5
    bq, nblk, col, st_spec = _gla_specs(nbatch, s, w, reverse)
    vec = pl.BlockSpec((1, w), lambda i: (0, 0))

    def body(q_ref, f_ref, v_ref, l0_ref, l1_ref, o_ref, st_ref, st):
        @pl.when(pl.program_id(0) == 0)
        def _():
            st[...] = jnp.zeros_like(st)

        for b in range(nbatch):
            st_ref[b] = st[b]
            k, g = _f_hgrn_pre(f_ref[b], l0_ref[...], l1_ref[...])
            c = _gla_block(q_ref[b], k, g, reverse)
            v = v_ref[b]
            for h in range(HGRN_HEADS):
                hs = slice(h * 128, (h + 1) * 128)
                att, _ = _gla_scores(c, hs)
                vb = _mx(v[:, hs])
                s0 = st[b, :, hs]
                o_ref[b, :, hs] = _dot(_mx(att), vb) + _dot(_mx(c["qt"][:, hs]), _mx(s0), _NT)
                st[b, :, hs] = s0 * c["e_l"][:, hs] + _dot(vb, _mx(c["kb"][:, hs]), _TN)

    return _pcall(body, name=f"gla_fwd_r{int(reverse)}", grid=(nblk,),
                  in_specs=[col(0), col(1 + int(reverse)), col(3), vec, vec], out_specs=(col(0), st_spec),
                  out_shape=(jax.ShapeDtypeStruct((nbatch, s, w), F32), jax.ShapeDtypeStruct((nbatch, nblk, 128, w), F32)),
                  scratch_shapes=[pltpu.VMEM((nbatch, 128, w), F32)], compiler_params=_params())(proj3, proj3, proj3, l0, l1)


def _gla_bwd(proj3, l0, l1, st4, do3, reverse, add_to=None):
    nbatch, s, w5 = proj3.shape
    w = w5 // 5
    bq, nblk, col, st_spec = _gla_specs(nbatch, s, w, not reverse)
    nadd = 0 if add_to is None else 2
    vec = pl.BlockSpec((1, w), lambda i: (0, 0))

    def body(q_ref, f_ref, v_ref, l0_ref, l1_ref, st_ref, do_ref, *rest):
        adds, (dq_ref, df_ref, dv_ref, dl0_ref, dl1_ref, dst) = rest[:nadd], rest[nadd:]

        @pl.when(pl.program_id(0) == 0)
        def _():
            dst[...] = jnp.zeros_like(dst)
            dl0_ref[...] = jnp.zeros_like(dl0_ref)
            dl1_ref[...] = jnp.zeros_like(dl1_ref)

        row = lax.broadcasted_iota(jnp.int32, (bq, 128), 0)
        for b in range(nbatch):
            (k, g), pre_vjp = jax.vjp(_f_hgrn_pre, f_ref[b], l0_ref[...], l1_ref[...])
            c = _gla_block(q_ref[b], k, g, reverse)
            s0_all, ds1_all = st_ref[b], dst[b]
            v, dy = v_ref[b], do_ref[b]
            dbl_l, dc_l, dk_l = [], [], []
            for h in range(HGRN_HEADS):
                hs = slice(h * 128, (h + 1) * 128)
                att, keys = _gla_scores(c, hs)
                qh, qt, kh, kb = c["qh"][:, hs], c["qt"][:, hs], c["kh"][:, hs], c["kb"][:, hs]
                vb, dyb = _mx(v[:, hs]), _mx(dy[:, hs])
                s0, ds1 = s0_all[:, hs], ds1_all[:, hs]
                datt = _mx(jnp.where(c["mask"], _dot(dyb, vb, _NT), 0.0))
                dqh_rows = []
                dkh = jnp.zeros((bq, 128), F32)
                dc = jnp.zeros((bq, 128), F32)
                for i in range(c["nsub"]):
                    rs = slice(i * HGRN_SUB, (i + 1) * HGRN_SUB)
                    dqh_rows.append(_dot(datt[rs], keys[i]))
                    dki = _dot(datt[rs], _mx(qh[rs]), _TN)
                    sc = c["scale"][i][:, hs]
                    dkh = dkh + dki * sc
                    dex = dki * (kh * sc)
                    dc = dc - dex + jnp.where(row == i * HGRN_SUB, jnp.sum(dex, axis=0, keepdims=True), 0.0)
                dqt = _dot(dyb, _mx(s0))
                dkb = _dot(vb, _mx(ds1))
                dv = _dot(_mx(att), dyb, _TN) + _dot(_mx(kb), _mx(ds1), _NT)
                dst[b, :, hs] = c["e_l"][:, hs] * ds1 + _dot(dyb, _mx(qt), _TN)
                dqh = jnp.concatenate(dqh_rows, axis=0) + dqt * c["ec"][:, hs]
                dkh = dkh + dkb * c["elc"][:, hs]
                kbk = dkb * kb
                dlast = jnp.sum(kbk, axis=0, keepdims=True) + c["e_l"][:, hs] * jnp.sum(ds1 * s0, axis=0, keepdims=True)
                at_edge = jnp.where(row == c["edge"], dlast, 0.0)
                dc_l.append(dc + dqt * qt - kbk + at_edge)
                dbl_l.append(dqh * qh - dkh * kh + at_edge)
                dq = dqh * c["ebl"][:, hs] * HGRN_SCALE
                if nadd:
                    dq, dv = dq + adds[0][b, :, hs], dv + adds[1][b, :, hs]
                dq_ref[b, :, hs] = dq.astype(dq_ref.dtype)
                dv_ref[b, :, hs] = dv.astype(dv_ref.dtype)
                dk_l.append(dkh * c["enbl"][:, hs])
            dg = (_dot01(c["m_within"], jnp.concatenate(dbl_l, axis=1), _TN, split="b", terms=2)
                  + _dot01(c["m_before"], jnp.concatenate(dc_l, axis=1), _TN, split="b", terms=2))
            df, d0, d1 = pre_vjp((jnp.concatenate(dk_l, axis=1), dg))
            df_ref[b] = df.astype(df_ref.dtype)
            dl0_ref[...] += d0
            dl1_ref[...] += d1

    shp_sum = jax.ShapeDtypeStruct((nbatch, s, w), BF16 if nadd else F32)
    shp_vec = jax.ShapeDtypeStruct((1, w), F32)
    return _pcall(body, name=f"gla_bwd_r{int(reverse)}", grid=(nblk,),
                  in_specs=[col(0), col(1 + int(reverse)), col(3), vec, vec, st_spec, col(0)] + [col(0)] * nadd,
                  out_specs=(col(0), col(0), col(0), vec, vec),
                  out_shape=(shp_sum, jax.ShapeDtypeStruct((nbatch, s, w), BF16), shp_sum, shp_vec, shp_vec),
                  scratch_shapes=[pltpu.VMEM((nbatch, 128, w), F32)],
                  compiler_params=_params())(proj3, proj3, proj3, l0, l1, st4, do3, *(add_to or ()))


DIRS = (False, True)


def _block_diag(w):
    eye = jnp.eye(16, dtype=w.dtype)
    return (eye[:, None, :, None] * w[:, :, None, :]).reshape(1024, 1024)


def _diag_blocks(m):
    m4 = m.reshape(16, 64, 16, 64)
    return jnp.stack([m4[i, :, i, :] for i in range(16)], axis=0)


def _pad_lanes(v, n=128):
    return jnp.pad(v, [(0, 0)] * (v.ndim - 1) + [(0, n - v.shape[-1])])


def _mlp_fwd(tag, x, nw, w1, w2):
    (h,) = _pw_fwd(f"{tag}_norm", _f_norm, [(x, 0)], [(nw, 0)], [BF16], 1024, 1)
    a, r = _mm(f"{tag}_up", h, w1, "nn", relu2=True)
    return _mm(f"{tag}_down", r, w2, "nn", res=x), (h, a, r)


def _mlp_bwd(tag, x, nw, w1, w2, saved, dxo):
    h, a, r = saved
    dw2 = _mm(f"{tag}_dw2", r, dxo, "tn")
    da = _mm(f"{tag}_da", dxo, w2, "nt", relu2_of=a, out_dtype=BF16)
    dw1 = _mm(f"{tag}_dw1", h, da, "tn", col_shards=4)
    dh = _mm(f"{tag}_dh", da, w1, "nt")
    (dx,), (dnw,) = _pw_bwd(f"{tag}_dnorm", _f_norm, [(x, 0)], [(nw, 0)], [dh], 1024, 1, [0], adds={0: dxo})
    return dx, dw1, dw2, dnw


def _split_in0(pieces, dt_piece):
    tm = 256

    def body(p0, p1, p2, p3, p4, p5, o_ref):
        full = jnp.concatenate([p0[...], p1[...], p2[...], p3[...], p4[...], p5[:, :32]], axis=1)
        for j in range(4):
            o_ref[j] = full[:, 1288 * j:1288 * (j + 1)]

    blk = pl.BlockSpec((tm, 1024), lambda i: (i, 0))
    return _pcall(body, name="split_in0", grid=(1024 // tm,), in_specs=[blk] * 5 + [pl.BlockSpec((tm, 128), lambda i: (i, 0))],
                  out_specs=pl.BlockSpec((4, tm, 1288), lambda i: (0, i, 0)),
                  out_shape=jax.ShapeDtypeStruct((4, 1024, 1288), F32), compiler_params=_params())(*pieces, dt_piece)


def _assemble_in0(shards):
    tm = 256

    def body(s_ref, m_ref, d_ref):
        full = jnp.concatenate([s_ref[j] for j in range(4)], axis=1)
        m_ref[...] = full[:, :5120]
        d_ref[...] = jnp.concatenate([full[:, 5120:5152], jnp.zeros((tm, 96), full.dtype)], axis=1)

    return _pcall(body, name="assemble_in0", grid=(1024 // tm,), in_specs=[pl.BlockSpec((4, tm, 1288), lambda i: (0, i, 0))],
                  out_specs=(pl.BlockSpec((tm, 5120), lambda i: (i, 0)), pl.BlockSpec((tm, 128), lambda i: (i, 0))),
                  out_shape=(jax.ShapeDtypeStruct((1024, 5120), shards.dtype), jax.ShapeDtypeStruct((1024, 128), shards.dtype)),
                  compiler_params=_params())(shards)


def _local_step(x3, tgt3, w, w_main0, w_dt0):
    nb, s, d = x3.shape
    t = nb * s
    x0 = x3.reshape(t, d)
    tgt = tgt3.reshape(t, d)
    grads = {}
    row = lambda v: v.reshape(1, -1)
    to3 = lambda v: v.reshape(nb, s, v.shape[-1])
    to2 = lambda v: v.reshape(-1, v.shape[-1])

    conv_w, conv_b = w["even_conv_w"][0], row(w["even_conv_b"][0])
    nmix0 = row(w["norm_mix"][0])
    (h0,) = _pw_fwd("l0_norm", _f_norm, [(x0, 0)], [(nmix0, 0)], [BF16], 1024, 1)
    proj0 = _mm("l0_proj", h0, w_main0, "nn")
    dt_raw = _mm("l0_proj_dt", h0, w_dt0, "nn")
    conv2, xbc3 = _conv_fwd(to3(proj0), conv_w, conv_b, 0, 2, True)
    u_lru = to2(_conv_fwd(to3(proj0), conv_w, conv_b, 2, 1, False))
    xbc = to2(xbc3)
    dt_bias = _pad_lanes(w["ssd_dt_bias"][0].reshape(1, 32))
    (dt,) = _pw_fwd("l0_dt", _f_softplus, [(dt_raw, 0)], [(dt_bias, 0)], [F32], 128, 1)
    dt3 = to3(dt)
    alog = _pad_lanes(w["ssd_a_log"][0].reshape(1, 32))
    ssd = [_ssd_fwd(xbc3, dt3, alog, r) for r in DIRS]
    yf, yb = to2(ssd[0][0]), to2(ssd[1][0])
    dskip = jnp.repeat(w["ssd_d"][0], SSD_HEADDIM).reshape(1, 1024)
    snw = row(w["ssd_norm_w"][0])
    ssd_ins = [(yf, 0), (yb, 0), (xbc, 0), (proj0, 3)]
    (ya,) = _pw_fwd("l0_ssd_post", _f_ssd_post, ssd_ins, [(dskip, 0), (snw, 0)], [BF16], 1024, 1, groups=SSD_GROUPS)
    w_gates = [_block_diag(w[k][0, r]).astype(MXU_DTYPE) for r in range(2) for k in ("lru_w_a", "lru_w_x")]
    pre = [_mm(f"l0_lru_pre{i}", u_lru, wg, "nn") for i, wg in enumerate(w_gates)]
    lru_par = [[(row(w[k][0, r]), 0) for k in ("lru_b_a", "lru_b_x", "lru_lambda")] for r in range(2)]
    lru_ins = [[(pre[2 * r], 0), (pre[2 * r + 1], 0), (u_lru, 0)] for r in range(2)]
    ab = [_pw_fwd(f"l0_lru_gates{r}", _f_lru_gates, lru_ins[r], lru_par[r], [F32, F32], 1024, 1) for r in range(2)]
    hs = [_lru_scan(to3(ab[r][0]), to3(ab[r][1]), DIRS[r]) for r in range(2)]
    lru_post_ins = [(to2(hs[0]), 0), (to2(hs[1]), 0), (proj0, 4)]
    (ybm,) = _pw_fwd("l0_lru_post", _f_lru_post, lru_post_ins, [], [BF16], 1024, 1)
    w_out0 = w["even_w_out"][0]
    x1 = _mm("l0_out_a", ya, w_out0[:1024], "nn", res=x0)
    x1 = _mm("l0_out_b", ybm, w_out0[1024:], "nn", res=x1)
    nmlp0 = row(w["norm_mlp"][0])
    x2, mlp0 = _mlp_fwd("l0_mlp", x1, nmlp0, w["mlp_w1"][0], w["mlp_w2"][0])

    w_in1 = w["odd_w_in"][0]
    nmix1 = row(w["norm_mix"][1])
    (h1,) = _pw_fwd("l1_norm", _f_norm, [(x2, 0)], [(nmix1, 0)], [BF16], 1024, 1)
    proj1 = _mm("l1_proj", h1, w_in1, "nn")
    proj1_3 = to3(proj1)
    lb0, lb1 = row(w["hgrn_lb_logits"][0]), row(w["hgrn_lb_logits"][1])
    gla = [_gla_fwd(proj1_3, lb0, lb1, r) for r in DIRS]
    hnw = row(w["hgrn_norm_w"][0])
    hpost_ins = [(to2(gla[0][0]), 0), (to2(gla[1][0]), 0), (proj1, 4)]
    (yo,) = _pw_fwd("l1_hgrn_post", _f_hgrn_post, hpost_ins, [(hnw, 0)], [BF16], 1024, 1, groups=HGRN_HEADS)
    w_out1 = w["odd_w_out"][0]
    x3_ = _mm("l1_out", yo, w_out1, "nn", res=x2)
    nmlp1 = row(w["norm_mlp"][1])
    x4, mlp1 = _mlp_fwd("l1_mlp", x3_, nmlp1, w["mlp_w1"][1], w["mlp_w2"][1])

    dx4, dnf, loss = _loss_head(x4, tgt, row(w["norm_final"]))
    grads["norm_final"] = dnf.reshape(-1)

    dx3, dw1_1, dw2_1, dnmlp1 = _mlp_bwd("l1_mlp", x3_, nmlp1, w["mlp_w1"][1], w["mlp_w2"][1], mlp1, dx4)
    big = {"odd_w_out": _mm("l1_dwout", yo, dx3, "tn").reshape(4, 256, 1024)}
    dyo = _mm("l1_dyo", dx3, w_out1, "nt")
    (do, dgate1), (dhnw,) = _pw_bwd("l1_hgrn_post_b", _f_hgrn_post, hpost_ins, [(hnw, 0)], [dyo], 1024, 1, [0, 2],
                                    out_dtypes=[F32, BF16], groups=HGRN_HEADS)
    grads["hgrn_norm_w"] = dhnw
    do3 = to3(do)
    gb = [_gla_bwd(proj1_3, lb0, lb1, gla[0][1], do3, False)]
    gb.append(_gla_bwd(proj1_3, lb0, lb1, gla[1][1], do3, True, add_to=(gb[0][0], gb[0][2])))
    grads["hgrn_lb_logits"] = jnp.concatenate([gb[0][3] + gb[1][3], gb[0][4] + gb[1][4]], axis=0)
    dparts1 = [to2(gb[1][0]), to2(gb[0][1]), to2(gb[1][1]), to2(gb[1][2]), dgate1]
    dwin1 = jnp.concatenate([_mm(f"l1_dwin{i}", h1, dp, "tn") for i, dp in enumerate(dparts1)], axis=1)
    big["odd_w_in"] = dwin1.reshape(1024, 4, 1280).transpose(1, 0, 2)
    dh1 = _mm_sum_nt("l1_dh", dparts1, [w_in1[:, i * 1024:(i + 1) * 1024] for i in range(5)])
    (dx2,), (dnmix1,) = _pw_bwd("l1_dnorm", _f_norm, [(x2, 0)], [(nmix1, 0)], [dh1], 1024, 1, [0], adds={0: dx3})

    dx1, dw1_0, dw2_0, dnmlp0 = _mlp_bwd("l0_mlp", x1, nmlp0, w["mlp_w1"][0], w["mlp_w2"][0], mlp0, dx2)
    big["mlp_w1"] = jnp.concatenate([dw1_0, dw1_1], axis=1)
    big["mlp_w2"] = jnp.concatenate([dw2_0.reshape(4, 1024, 1024), dw2_1.reshape(4, 1024, 1024)], axis=1)
    grads["norm_mlp"] = jnp.concatenate([dnmlp0, dnmlp1], axis=0)
    big["even_w_out"] = jnp.concatenate([_mm("l0_dwout_a", ya, dx1, "tn"), _mm("l0_dwout_b", ybm, dx1, "tn")],
                                        axis=0).reshape(4, 512, 1024)
    dya = _mm("l0_dya", dx1, w_out0[:1024], "nt")
    dyb = _mm("l0_dyb", dx1, w_out0[1024:], "nt")
    (dh, dgate0), _ = _pw_bwd("l0_lru_post_b", _f_lru_post, lru_post_ins, [], [dyb], 1024, 1, [0, 2], out_dtypes=[F32, BF16])
    dh3 = to3(dh)
    dpre, du_parts, dlru = [], [], {k: [] for k in ("lru_b_a", "lru_b_x", "lru_lambda")}
    for r in range(2):
        g_r, da_r = _lru_scan_bwd(to3(ab[r][0]), hs[r], dh3, DIRS[r])
        (dpa, dpx, du_r), (dba, dbx, dlam) = _pw_bwd(f"l0_lru_gates_b{r}", _f_lru_gates, lru_ins[r], lru_par[r],
                                                     [to2(da_r), to2(g_r)], 1024, 1, [0, 1, 2],
                                                     out_dtypes=[BF16, BF16, F32])
        dpre += [dpa, dpx]
        du_parts.append(du_r)
        dlru["lru_b_a"].append(dba)
        dlru["lru_b_x"].append(dbx)
        dlru["lru_lambda"].append(dlam)
    for k, v in dlru.items():
        grads[k] = jnp.concatenate(v, axis=0)[None]
    dwg = [_diag_blocks(_mm(f"l0_dwgate{i}", u_lru, dp, "tn")) for i, dp in enumerate(dpre)]
    grads["lru_w_a"] = jnp.stack([dwg[0], dwg[2]])[None]
    grads["lru_w_x"] = jnp.stack([dwg[1], dwg[3]])[None]
    du_gate = _mm_sum_nt("l0_du_gate", dpre, w_gates)
    (du,) = _pw_fwd("l0_du", _f_add3, [(du_parts[0], 0), (du_parts[1], 0), (du_gate, 0)], [], [F32], 1024, 1)
    (dy, dxs_skip, dz), (ddskip, dsnw) = _pw_bwd("l0_ssd_post_b", _f_ssd_post, ssd_ins, [(dskip, 0), (snw, 0)], [dya],
                                                 1024, 1, [0, 2, 3], out_dtypes=[F32, F32, BF16], groups=SSD_GROUPS)
    grads["ssd_d"] = ddskip.reshape(SSD_HEADS, SSD_HEADDIM).sum(axis=1)[None]
    grads["ssd_norm_w"] = dsnw
    dy3 = to3(dy)
    sb0 = _ssd_bwd(xbc3, dt3, alog, ssd[0][1], dy3, False)
    sb1 = _ssd_bwd(xbc3, dt3, alog, ssd[1][1], dy3, True, add_to=(sb0[0], to3(dxs_skip), sb0[1], sb0[2]))
    grads["ssd_a_log"] = (sb0[3] + sb1[3])[:, :32].reshape(1, 2, 16)
    ddt = to2(sb1[2])
    (ddt_raw,), (ddtb,) = _pw_bwd("l0_dt_b", _f_softplus, [(dt_raw, 0)], [(dt_bias, 0)], [ddt], 128, 1, [0])
    grads["ssd_dt_bias"] = ddtb[:, :32].reshape(1, 2, 16)
    cb = [_conv_bwd(sb1[0], to3(proj0), conv_w, 0, conv2), _conv_bwd(sb1[1], to3(proj0), conv_w, 1, conv2),
          _conv_bwd(to3(du), to3(proj0), conv_w, 2)]
    dcw = jnp.concatenate([c_[1] for c_ in cb], axis=1)
    grads["even_conv_w"] = dcw[:4][None]
    grads["even_conv_b"] = dcw[4:5]
    dparts0 = [to2(c_[0]) for c_ in cb] + [dz, dgate0]
    dwin0 = [_mm(f"l0_dwin{i}", h0, dp, "tn") for i, dp in enumerate(dparts0)]
    big["even_w_in"] = _split_in0(dwin0, _mm("l0_dwin_dt", h0, ddt_raw, "tn"))
    dh0 = _mm_sum_nt("l0_dh", dparts0 + [ddt_raw], [w_main0[:, i * 1024:(i + 1) * 1024] for i in range(5)] + [w_dt0])
    (dx0,), (dnmix0,) = _pw_bwd("l0_dnorm", _f_norm, [(x0, 0)], [(nmix0, 0)], [dh0], 1024, 1, [0], adds={0: dx1})
    grads["norm_mix"] = jnp.concatenate([dnmix0, dnmix1], axis=0)
    return loss, dx0.reshape(nb, s, d), grads, [big[n] for n in BIG]


ANY = pl.BlockSpec(memory_space=pl.ANY)


def _place():
    return lax.axis_index("x"), lax.axis_index("y"), lax.axis_index("c")


def _remote(src, dst, send_sems, recv_sems, k, to):
    return pltpu.make_async_remote_copy(src_ref=src, dst_ref=dst, send_sem=send_sems.at[k], recv_sem=recv_sems.at[k],
                                        device_id=to, device_id_type=MESH)


def _gather_chips(shards):
    n = len(shards)
    halves = [s.shape[0] // 2 for s in shards]

    def body(*refs):
        x_refs, out_refs = refs[:n], refs[n:2 * n]
        send_sems, recv_sems = refs[2 * n:]
        x, y, c = _place()
        sibling = (x, y, 1 - c)
        chips = [(1 - x, y), (x, 1 - y), (1 - x, 1 - y)]

        def blk(t, px, py, hc):
            return out_refs[t].at[2 * px + py, pl.ds(hc * halves[t], halves[t]), :]

        def src(t):
            return x_refs[t].at[pl.ds(c * halves[t], halves[t]), :]

        first = [_remote(src(t), blk(t, x, y, c), send_sems, recv_sems, 6 * t + j, (*chip, c))
                 for t in range(n) for j, chip in enumerate(chips)]
        for cp in first:
            cp.start()
        passed = []
        for t in range(n):
            for j, chip in enumerate(chips):
                _remote(src(t), blk(t, *chip, c), send_sems, recv_sems, 6 * t + j, (*chip, c)).wait_recv()
                cp = _remote(blk(t, *chip, c), blk(t, *chip, c), send_sems, recv_sems, 6 * t + 3 + j, sibling)
                cp.start()
                passed.append(cp)
        for t in range(n):
            for j, chip in enumerate(chips):
                _remote(src(t), blk(t, *chip, 1 - c), send_sems, recv_sems, 6 * t + 3 + j, sibling).wait_recv()
        for cp in first + passed:
            cp.wait_send()

    return _pcall(body, name="gather_weights", in_specs=[ANY] * n, out_specs=(ANY,) * n,
                  out_shape=tuple(jax.ShapeDtypeStruct((4,) + s.shape, s.dtype) for s in shards),
                  scratch_shapes=[pltpu.SemaphoreType.DMA((6 * n,)), pltpu.SemaphoreType.DMA((6 * n,))],
                  compiler_params=_params())(*shards)


def _pair_swap(gps):
    n = len(gps)
    halves = [g.shape[1] // 2 for g in gps]

    def body(*refs):
        g_refs, land_refs = refs[:n], refs[n:2 * n]
        send_sems, recv_sems = refs[2 * n:]
        x, y, c = _place()
        cps = [_remote(g_refs[t].at[j, pl.ds((1 - c) * halves[t], halves[t]), :], land_refs[t].at[j], send_sems, recv_sems,
                       4 * t + j, (x, y, 1 - c)) for t in range(n) for j in range(4)]
        for cp in cps:
            cp.start()
        for cp in cps:
            cp.wait()

    return _pcall(body, name="grad_pair_swap", in_specs=[ANY] * n, out_specs=(ANY,) * n,
                  out_shape=tuple(jax.ShapeDtypeStruct((4, h, g.shape[2]), F32) for g, h in zip(gps, halves)),
                  scratch_shapes=[pltpu.SemaphoreType.DMA((4 * n,)), pltpu.SemaphoreType.DMA((4 * n,))],
                  compiler_params=_params())(*gps)


def _pair_add(name, gp, land, cidx):
    _, half, cols = land.shape
    tr = _tile(half, 512)
    nh = half // tr

    def body(c_ref, g_ref, l_ref, o_ref):
        o_ref[...] = (g_ref[...] + l_ref[...]).astype(o_ref.dtype)

    grid_spec = pltpu.PrefetchScalarGridSpec(
        num_scalar_prefetch=1, grid=(4, nh),
        in_specs=[pl.BlockSpec((None, tr, cols), lambda j, i, c: (j, c[0] * nh + i, 0)),
                  pl.BlockSpec((None, tr, cols), lambda j, i, c: (j, i, 0))],
        out_specs=pl.BlockSpec((None, tr, cols), lambda j, i, c: (j, i, 0)))
    return _pcall(body, name=f"pair_add_{name}", grid_spec=grid_spec, out_shape=jax.ShapeDtypeStruct((4, half, cols), BF16),
                  compiler_params=_params())(cidx, gp, land)


def _chip_scatter(css):
    n = len(css)

    def body(*refs):
        s_refs, land_refs = refs[:n], refs[n:2 * n]
        send_sems, recv_sems = refs[2 * n:]
        x, y, c = _place()
        me = 2 * x + y
        chips = [(1 - x, y), (x, 1 - y), (1 - x, 1 - y)]
        cps = [_remote(s_refs[t].at[2 * px + py], land_refs[t].at[me], send_sems, recv_sems, 3 * t + j, (px, py, c))
               for t in range(n) for j, (px, py) in enumerate(chips)]
        for cp in cps:
            cp.start()
        for t in range(n):
            for j, (px, py) in enumerate(chips):
                _remote(s_refs[t].at[me], land_refs[t].at[2 * px + py], send_sems, recv_sems, 3 * t + j, (px, py, c)).wait_recv()
        for cp in cps:
            cp.wait_send()

    return _pcall(body, name="grad_chip_scatter", in_specs=[ANY] * n, out_specs=(ANY,) * n,
                  out_shape=tuple(jax.ShapeDtypeStruct(s.shape, s.dtype) for s in css),
                  scratch_shapes=[pltpu.SemaphoreType.DMA((3 * n,)), pltpu.SemaphoreType.DMA((3 * n,))],
                  compiler_params=_params())(*css)


def _chip_sum(name, land):
    _, half, cols = land.shape
    tr = _tile(half, 512)

    def body(l_ref, o_ref):
        o_ref[...] = ((l_ref[0].astype(F32) + l_ref[1].astype(F32)) + l_ref[2].astype(F32)) + l_ref[3].astype(F32)

    return _pcall(body, name=f"chip_sum_{name}", grid=(half // tr,),
                  in_specs=[pl.BlockSpec((4, tr, cols), lambda i: (0, i, 0))],
                  out_specs=pl.BlockSpec((tr, cols), lambda i: (i, 0)),
                  out_shape=jax.ShapeDtypeStruct((half, cols), F32), compiler_params=_params())(land)


def _pair_join(reds):
    n = len(reds)

    def body(*refs):
        r_refs, out_refs = refs[:n], refs[n:2 * n]
        send_sems, recv_sems = refs[2 * n:]
        x, y, c = _place()
        cps = [_remote(r_refs[t], out_refs[t].at[c], send_sems, recv_sems, t, (x, y, 1 - c)) for t in range(n)]
        for cp in cps:
            cp.start()
        for t in range(n):
            _remote(r_refs[t], out_refs[t].at[1 - c], send_sems, recv_sems, t, (x, y, 1 - c)).wait_recv()
        for cp in cps:
            cp.wait_send()

    return _pcall(body, name="grad_pair_join", in_specs=[ANY] * n, out_specs=(ANY,) * n,
                  out_shape=tuple(jax.ShapeDtypeStruct((2,) + r.shape, F32) for r in reds),
                  scratch_shapes=[pltpu.SemaphoreType.DMA((n,)), pltpu.SemaphoreType.DMA((n,))],
                  compiler_params=_params())(*reds)


def _adamw(name, g, w, m, v):
    rows, cols = g.shape
    tr = _tile(rows, 512)

    def body(g_ref, w_ref, m_ref, v_ref, d_ref, mo_ref, vo_ref):
        gv = g_ref[...]
        mn = ADAM_B1 * m_ref[...] + (1.0 - ADAM_B1) * gv
        vn = ADAM_B2 * v_ref[...] + (1.0 - ADAM_B2) * jnp.square(gv)
        m_hat = mn / (1.0 - ADAM_B1 ** ADAM_STEP)
        v_hat = vn / (1.0 - ADAM_B2 ** ADAM_STEP)
        d_ref[...] = -ADAM_LR * (m_hat / (jnp.sqrt(v_hat) + ADAM_EPS) + ADAM_WD * w_ref[...])
        mo_ref[...] = mn
        vo_ref[...] = vn

    blk = pl.BlockSpec((tr, cols), lambda i: (i, 0))
    shp = jax.ShapeDtypeStruct((rows, cols), F32)
    return _pcall(body, name=f"adamw_{name}", grid=(rows // tr,), in_specs=[blk] * 4, out_specs=(blk,) * 3,
                  out_shape=(shp,) * 3, compiler_params=_params())(g, w, m, v)


def _pack(pieces, rows, dtype):
    flat = jnp.concatenate([p.reshape(-1).astype(dtype) for p in pieces])
    return jnp.pad(flat, (0, rows * PACK_COLS - flat.shape[0])).reshape(rows, PACK_COLS)


def _unpack(pack, shapes):
    flat = pack.reshape(-1)
    out, off = [], 0
    for shp in shapes:
        n = math.prod(shp)
        out.append(flat[off:off + n].reshape(shp))
        off += n
    return out


def _shard_of(full, axis, j):
    n = full.shape[axis] // 4
    return lax.slice_in_dim(full, j * n, (j + 1) * n, axis=axis)


def kernel(x, even_w_in, even_conv_w, even_conv_b, ssd_a_log, ssd_dt_bias, ssd_d, ssd_norm_w, lru_w_a, lru_b_a, lru_w_x, lru_b_x, lru_lambda, even_w_out, odd_w_in, hgrn_lb_logits, hgrn_norm_w, odd_w_out, norm_mix, norm_mlp, mlp_w1, mlp_w2, norm_final, loss_target, m_even_w_in, m_even_conv_w, m_even_conv_b, m_ssd_a_log, m_ssd_dt_bias, m_ssd_d, m_ssd_norm_w, m_lru_w_a, m_lru_b_a, m_lru_w_x, m_lru_b_x, m_lru_lambda, m_even_w_out, m_odd_w_in, m_hgrn_lb_logits, m_hgrn_norm_w, m_odd_w_out, m_norm_mix, m_norm_mlp, m_mlp_w1, m_mlp_w2, m_norm_final, v_even_w_in, v_even_conv_w, v_even_conv_b, v_ssd_a_log, v_ssd_dt_bias, v_ssd_d, v_ssd_norm_w, v_lru_w_a, v_lru_b_a, v_lru_w_x, v_lru_b_x, v_lru_lambda, v_even_w_out, v_odd_w_in, v_hgrn_lb_logits, v_hgrn_norm_w, v_odd_w_out, v_norm_mix, v_norm_mlp, v_mlp_w1, v_mlp_w2, v_norm_final):
    names = [n for n, _, _, _ in WEIGHTS]
    w_loc = dict(zip(names, (even_w_in, even_conv_w, even_conv_b, ssd_a_log, ssd_dt_bias, ssd_d, ssd_norm_w, lru_w_a, lru_b_a, lru_w_x, lru_b_x, lru_lambda, even_w_out, odd_w_in, hgrn_lb_logits, hgrn_norm_w, odd_w_out, norm_mix, norm_mlp, mlp_w1, mlp_w2, norm_final)))
    m_loc = dict(zip(names, (m_even_w_in, m_even_conv_w, m_even_conv_b, m_ssd_a_log, m_ssd_dt_bias, m_ssd_d, m_ssd_norm_w, m_lru_w_a, m_lru_b_a, m_lru_w_x, m_lru_b_x, m_lru_lambda, m_even_w_out, m_odd_w_in, m_hgrn_lb_logits, m_hgrn_norm_w, m_odd_w_out, m_norm_mix, m_norm_mlp, m_mlp_w1, m_mlp_w2, m_norm_final)))
    v_loc = dict(zip(names, (v_even_w_in, v_even_conv_w, v_even_conv_b, v_ssd_a_log, v_ssd_dt_bias, v_ssd_d, v_ssd_norm_w, v_lru_w_a, v_lru_b_a, v_lru_w_x, v_lru_b_x, v_lru_lambda, v_even_w_out, v_odd_w_in, v_hgrn_lb_logits, v_hgrn_norm_w, v_odd_w_out, v_norm_mix, v_norm_mlp, v_mlp_w1, v_mlp_w2, v_norm_final)))
    spec = {n: (blk, full, ax) for n, blk, full, ax in WEIGHTS}

    small = [n for n in names if n not in BIG]
    two_d = lambda n, v: v.reshape(BIG_2D[n])

    me = 2 * lax.axis_index("x") + lax.axis_index("y")
    cc = lax.axis_index("c")
    put = lambda whole, part, k: lax.dynamic_update_slice_in_dim(whole, part[None], k, axis=0)
    own = [two_d(n, w_loc[n]).astype(BF16) for n in BIG] + [_pack([w_loc[n] for n in SMALL_SHARDED], 16, F32)]
    g_in0, g_out0, g_in1, g_out1, g_w1, g_w2, g_small = [put(g, o, me) for g, o in zip(_gather_chips(own), own)]
    w_main0, w_dt0 = _assemble_in0(g_in0)
    w_full = {n: w_loc[n] for n in names if spec[n][2] is None}
    w_full["even_w_out"] = g_out0.reshape(1, 2048, 1024)
    w_full["odd_w_in"] = jnp.concatenate([g_in1[j] for j in range(4)], axis=1)[None]
    w_full["odd_w_out"] = g_out1.reshape(1, 1024, 1024)
    w_full["mlp_w1"] = jnp.stack([jnp.concatenate([g_w1[j, l * 1024:(l + 1) * 1024] for j in range(4)], axis=1) for l in range(2)])
    w_full["mlp_w2"] = jnp.stack([jnp.concatenate([g_w2[j, l * 1024:(l + 1) * 1024] for j in range(4)], axis=0) for l in range(2)])
    shards = [_unpack(g_small[j], [spec[n][0] for n in SMALL_SHARDED]) for j in range(4)]
    for i, n in enumerate(SMALL_SHARDED):
        w_full[n] = jnp.concatenate([shards[j][i] for j in range(4)], axis=spec[n][2])

    loss_vec, grad_x, grads, big = _local_step(x, loss_target, w_full, w_main0, w_dt0)
    loss = lax.psum(loss_vec[0, 0], ("x", "y", "c"))

    def dest_pack(j):
        return _pack([grads[n].reshape(spec[n][1]) if spec[n][2] is None else _shard_of(grads[n].reshape(spec[n][1]), spec[n][2], j)
                      for n in small], SMALL_ROWS, F32)

    tensors = big + [jnp.stack([dest_pack(j) for j in range(4)])]
    tags = list(BIG) + ["small"]
    cidx = cc.astype(jnp.int32).reshape(1)
    chip_sums = [_pair_add(tag, g, land, cidx) for tag, g, land in zip(tags, tensors, _pair_swap(tensors))]
    landed = [put(land, lax.dynamic_index_in_dim(cs, me, axis=0, keepdims=False), me)
              for land, cs in zip(_chip_scatter(chip_sums), chip_sums)]
    halves = [_chip_sum(tag, land) for tag, land in zip(tags, landed)]
    reduced = [put(r, h, cc).reshape(-1, r.shape[-1]) for r, h in zip(_pair_join(halves), halves)]

    outs = {}
    for n, g in zip(BIG, reduced[:-1]):
        res = (g, *_adamw(n, g, two_d(n, w_loc[n]), two_d(n, m_loc[n]), two_d(n, v_loc[n])))
        outs[n] = [r.reshape(spec[n][0]) for r in res]
    blocks = [spec[n][0] for n in small]
    wp, mp, vp = (_pack([src[n] for n in small], SMALL_ROWS, F32) for src in (w_loc, m_loc, v_loc))
    res = (reduced[-1], *_adamw("small", reduced[-1], wp, mp, vp))
    unpacked = [_unpack(r, blocks) for r in res]
    for i, n in enumerate(small):
        outs[n] = [u[i] for u in unpacked]
    return (loss, grad_x, *[outs[n][k] for k in range(4) for n in names])
```

```python
import functools
import math

import jax
import jax.numpy as jnp
from jax import lax
from jax.experimental import pallas as pl
from jax.experimental.pallas import tpu as pltpu

F32 = jnp.float32
BF16 = jnp.bfloat16
MXU_DTYPE = jnp.bfloat16
MESH = pl.DeviceIdType.MESH

D_MODEL = 1024
EPS = 1e-6
SSD_HEADS = 16
SSD_HEADDIM = 64
HEAD_SHIFT = 6
SSD_GROUPS = 4
SSD_STATE = 128
SSD_CHUNK = 128
LRU_C = 8.0
LRU_ROWS = 256
HGRN_HEADS = 8
HGRN_HEADDIM = 128
HGRN_SUB = 32
HGRN_SUB_SHIFT = 5
HGRN_BLOCK = 128
HGRN_SCALE = HGRN_HEADDIM ** -0.5
CONV_ROWS = 512

ADAM_LR = 0.001
ADAM_B1 = 0.9
ADAM_B2 = 0.999
ADAM_EPS = 1e-08
ADAM_WD = 0.01
ADAM_STEP = 10

VMEM_LIMIT = 56 * 1024 * 1024
PACK_COLS = 1024
SMALL_ROWS = 288

WEIGHTS = (
    ("even_w_in", (1, 1024, 1288), (1, 1024, 5152), 2),
    ("even_conv_w", (1, 4, 768), (1, 4, 3072), 2),
    ("even_conv_b", (1, 3072), (1, 3072), None),
    ("ssd_a_log", (1, 2, 16), (1, 2, 16), None),
    ("ssd_dt_bias", (1, 2, 16), (1, 2, 16), None),
    ("ssd_d", (1, 16), (1, 16), None),
    ("ssd_norm_w", (1, 1024), (1, 1024), None),
    ("lru_w_a", (1, 2, 16, 64, 64), (1, 2, 16, 64, 64), None),
    ("lru_b_a", (1, 2, 256), (1, 2, 1024), 2),
    ("lru_w_x", (1, 2, 16, 64, 64), (1, 2, 16, 64, 64), None),
    ("lru_b_x", (1, 2, 256), (1, 2, 1024), 2),
    ("lru_lambda", (1, 2, 256), (1, 2, 1024), 2),
    ("even_w_out", (1, 512, 1024), (1, 2048, 1024), 1),
    ("odd_w_in", (1, 1024, 1280), (1, 1024, 5120), 2),
    ("hgrn_lb_logits", (2, 1024), (2, 1024), None),
    ("hgrn_norm_w", (1, 256), (1, 1024), 1),
    ("odd_w_out", (1, 256, 1024), (1, 1024, 1024), 1),
    ("norm_mix", (2, 1024), (2, 1024), None),
    ("norm_mlp", (2, 1024), (2, 1024), None),
    ("mlp_w1", (2, 1024, 1024), (2, 1024, 4096), 2),
    ("mlp_w2", (2, 1024, 1024), (2, 4096, 1024), 1),
    ("norm_final", (1024,), (1024,), None),
)
BIG = ("even_w_in", "even_w_out", "odd_w_in", "odd_w_out", "mlp_w1", "mlp_w2")
BIG_2D = {"even_w_in": (1024, 1288), "even_w_out": (512, 1024), "odd_w_in": (1024, 1280), "odd_w_out": (256, 1024),
          "mlp_w1": (2048, 1024), "mlp_w2": (2048, 1024)}
SMALL_SHARDED = ("even_conv_w", "lru_b_a", "lru_b_x", "lru_lambda", "hgrn_norm_w")


def _pcall(body, **kw):
    return pl.pallas_call(body, **kw)


def _params(**kw):
    return pltpu.CompilerParams(vmem_limit_bytes=VMEM_LIMIT, **kw)


def _tile(n, pref):
    if n <= pref:
        return n
    t = (pref // 128) * 128
    while n % t:
        t -= 128
    return t


def _dot(a, b, dims=(((1,), (0,)), ((), ()))):
    return lax.dot_general(a, b, dims, preferred_element_type=F32)


_NN = (((1,), (0,)), ((), ()))
_NT = (((1,), (1,)), ((), ()))
_TN = (((0,), (0,)), ((), ()))


def _mx(v):
    return v.astype(MXU_DTYPE)


def _dot01(a, b, dims=_NN, *, split, terms):
    acc, rest = None, (a if split == "a" else b)
    for _ in range(terms):
        piece = _mx(rest)
        part = _dot(piece, _mx(b), dims) if split == "a" else _dot(_mx(a), piece, dims)
        acc = part if acc is None else acc + part
        rest = rest - piece.astype(F32)
    return acc


def _mm(name, a, b, mode, *, out_dtype=F32, res=None, relu2=False, relu2_of=None, col_shards=1):
    if mode == "nn":
        (m, kk), (_, n) = a.shape, b.shape
    elif mode == "nt":
        (m, kk), (n, _) = a.shape, b.shape
    else:
        (kk, m), (_, n) = a.shape, b.shape
    assert res is None or relu2_of is None
    tk_pref = 1024
    if mode == "tn" and a.dtype.itemsize == 2 and b.dtype.itemsize == 2:
        tk_pref = 2048
    tm, tn, tk = _tile(m, 1024), _tile(n // col_shards, 1024), _tile(kk, tk_pref)
    nk = kk // tk
    dims = {"nn": _NN, "nt": _NT, "tn": _TN}[mode]
    a_spec = pl.BlockSpec((tk, tm), lambda i, j, k: (k, i)) if mode == "tn" else pl.BlockSpec((tm, tk), lambda i, j, k: (i, k))
    b_spec = pl.BlockSpec((tn, tk), lambda i, j, k: (j, k)) if mode == "nt" else pl.BlockSpec((tk, tn), lambda i, j, k: (k, j))
    o_spec = pl.BlockSpec((tm, tn), lambda i, j, k: (i, j))
    o_shape = (m, n)
    if col_shards > 1:
        assert tn * col_shards == n and res is None and not relu2
        o_spec = pl.BlockSpec((None, tm, tn), lambda i, j, k: (j, i, 0))
        o_shape = (col_shards, m, tn)
    extra = res if res is not None else relu2_of
    has_res = extra is not None

    def body(*refs):
        a_ref, b_ref = refs[0], refs[1]
        res_ref = refs[2] if has_res else None
        outs = refs[2 + has_res:2 + has_res + 1 + relu2]

        def finish(r):
            if res is not None:
                r = r + res_ref[...]
            if relu2_of is not None:
                r = r * (2.0 * jnp.maximum(res_ref[...], 0.0))
            if relu2:
                outs[0][...] = r
                outs[1][...] = jnp.square(jnp.maximum(r, 0.0)).astype(outs[1].dtype)
            else:
                outs[0][...] = r.astype(outs[0].dtype)

        prod = _dot(_mx(a_ref[...]), _mx(b_ref[...]), dims)
        if nk == 1:
            finish(prod)
            return
        acc = refs[-1]
        k = pl.program_id(2)

        @pl.when(k == 0)
        def _():
            acc[...] = prod

        @pl.when(k > 0)
        def _():
            acc[...] += prod

        @pl.when(k == nk - 1)
        def _():
            finish(acc[...])

    in_specs = [a_spec, b_spec] + ([o_spec] if has_res else [])
    if relu2:
        out_shape = (jax.ShapeDtypeStruct((m, n), F32), jax.ShapeDtypeStruct((m, n), BF16))
        out_specs = (o_spec, o_spec)
    else:
        out_shape = jax.ShapeDtypeStruct(o_shape, out_dtype)
        out_specs = o_spec
    args = (a, b) + ((extra,) if has_res else ())
    return _pcall(body, name=name, grid=(m // tm, n // tn, nk), in_specs=in_specs, out_specs=out_specs,
                  out_shape=out_shape, scratch_shapes=[pltpu.VMEM((tm, tn), F32)] if nk > 1 else [],
                  compiler_params=_params())(*args)


def _mm_sum_nt(name, parts, wblocks):
    m, n, npart = parts[0].shape[0], wblocks[0].shape[0], len(parts)
    tm, tn = _tile(m, 512), _tile(n, 1024)

    def body(*refs):
        acc = _dot(_mx(refs[0][...]), _mx(refs[npart][...]), _NT)
        for k in range(1, npart):
            acc = acc + _dot(_mx(refs[k][...]), _mx(refs[npart + k][...]), _NT)
        refs[-1][...] = acc

    in_specs = [pl.BlockSpec((tm, p.shape[1]), lambda i, j: (i, 0)) for p in parts]
    in_specs += [pl.BlockSpec((tn, w.shape[1]), lambda i, j: (j, 0)) for w in wblocks]
    return _pcall(body, name=name, grid=(m // tm, n // tn), in_specs=in_specs, out_specs=pl.BlockSpec((tm, tn), lambda i, j: (i, j)),
                  out_shape=jax.ShapeDtypeStruct((m, n), F32), compiler_params=_params())(*parts, *wblocks)


def _pw_fwd(name, f, ins, params, out_dtypes, tc, ncol, tm=256, groups=1):
    t = ins[0][0].shape[0]
    tm = min(tm, t)
    ni, npar = len(ins), len(params)
    gw = tc // groups

    def body(*refs):
        for g in range(groups):
            sl = slice(g * gw, (g + 1) * gw)
            vals = f(*[r[:, sl].astype(F32) for r in refs[:ni]], *[r[:, sl] for r in refs[ni:ni + npar]])
            for o, v in zip(refs[ni + npar:], vals):
                o[:, sl] = v.astype(o.dtype)

    in_specs = [pl.BlockSpec((tm, tc), lambda j, i, off=off: (i, off + j)) for _, off in ins]
    in_specs += [pl.BlockSpec((1, tc), lambda j, i, off=off: (0, off + j)) for _, off in params]
    out_specs = tuple(pl.BlockSpec((tm, tc), lambda j, i: (i, j)) for _ in out_dtypes)
    out_shape = tuple(jax.ShapeDtypeStruct((t, ncol * tc), d) for d in out_dtypes)
    return _pcall(body, name=name, grid=(ncol, t // tm), in_specs=in_specs, out_specs=out_specs, out_shape=out_shape,
                  compiler_params=_params())(*[a for a, _ in ins], *[p for p, _ in params])


def _pw_bwd(name, f, ins, params, douts, tc, ncol, want, adds=None, tm=256, out_dtypes=None, groups=1):
    adds = adds or {}
    out_dtypes = out_dtypes or [F32] * len(want)
    t = ins[0][0].shape[0]
    tm = min(tm, t)
    ni, npar, nd, na = len(ins), len(params), len(douts), len(adds)
    add_keys = sorted(adds)
    gw = tc // groups

    def body(*refs):
        in_refs, p_refs = refs[:ni], refs[ni:ni + npar]
        d_refs = refs[ni + npar:ni + npar + nd]
        a_refs = refs[ni + npar + nd:ni + npar + nd + na]
        o_refs = refs[ni + npar + nd + na:]
        for p in range(npar):
            @pl.when(pl.program_id(1) == 0)
            def _(o=o_refs[len(want) + p]):
                o[...] = jnp.zeros_like(o)

        for g in range(groups):
            sl = slice(g * gw, (g + 1) * gw)
            _, vjp = jax.vjp(f, *[r[:, sl].astype(F32) for r in in_refs], *[r[:, sl] for r in p_refs])
            cts = vjp(tuple(d[:, sl].astype(F32) for d in d_refs))
            for o, kidx in zip(o_refs[:len(want)], want):
                v = cts[kidx]
                if kidx in adds:
                    v = v + a_refs[add_keys.index(kidx)][:, sl]
                o[:, sl] = v.astype(o.dtype)
            for p in range(npar):
                o_refs[len(want) + p][:, sl] += cts[ni + p]

    in_specs = [pl.BlockSpec((tm, tc), lambda j, i, off=off: (i, off + j)) for _, off in ins]
    in_specs += [pl.BlockSpec((1, tc), lambda j, i, off=off: (0, off + j)) for _, off in params]
    in_specs += [pl.BlockSpec((tm, tc), lambda j, i: (i, j)) for _ in range(nd + na)]
    out_specs = tuple([pl.BlockSpec((tm, tc), lambda j, i: (i, j)) for _ in want]
                      + [pl.BlockSpec((1, tc), lambda j, i: (0, j)) for _ in params])
    out_shape = tuple([jax.ShapeDtypeStruct((t, ncol * tc), dt) for dt in out_dtypes]
                      + [jax.ShapeDtypeStruct((1, ncol * tc), F32) for _ in params])
    res = _pcall(body, name=name, grid=(ncol, t // tm), in_specs=in_specs, out_specs=out_specs, out_shape=out_shape,
                 compiler_params=_params())(*[a for a, _ in ins], *[p for p, _ in params], *douts, *[adds[k] for k in add_keys])
    return list(res[:len(want)]), list(res[len(want):])


def _rms(x, g):
    return (x * lax.rsqrt(jnp.mean(x * x, axis=-1, keepdims=True) + EPS)) * g


def _f_norm(x, g):
    return (_rms(x, g),)


def _f_softplus(d, b):
    return (jax.nn.softplus(d + b),)


def _f_add3(a, b, c):
    return (a + b + c,)


def _f_ssd_post(yf, yb, xs, z, dskip, nw):
    u = (yf + yb + dskip * xs) * jax.nn.silu(z)
    return (_rms(u, nw),)


def _neg_expm1(v):
    t = jnp.tanh(0.5 * v)
    return -2.0 * t / (1.0 - t)


def _f_lru_gates(pre_a, pre_x, u, ba, bx, lam):
    rg = jax.nn.sigmoid(pre_a + ba)
    ig = jax.nn.sigmoid(pre_x + bx)
    log_a = -LRU_C * rg * jax.nn.softplus(-lam)
    return jnp.exp(log_a), jnp.sqrt(_neg_expm1(2.0 * log_a)) * (ig * u)


def _f_lru_post(hf, hb, gate):
    return ((hf + hb) * jax.nn.gelu(gate),)


def _f_hgrn_pre(fr, l0, l1):
    lb = jax.nn.sigmoid(l1 - l0)
    k = (1.0 - lb) * jax.nn.sigmoid(-fr)
    return k, jnp.log1p(-k)


def _f_hgrn_post(of, ob, gate, nw):
    return (_rms(of + ob, nw) * jax.nn.silu(gate),)


def _loss_head(x, tgt, g, tm=256):
    t, d = x.shape
    tm = min(tm, t)

    def body(x_ref, t_ref, g_ref, dx_ref, dg_ref, loss_ref):
        tv = t_ref[...]

        def lf(xv, gv):
            return 0.5 * jnp.sum(jnp.mean(jnp.square(_rms(xv, gv) - tv), axis=-1))

        val, vjp = jax.vjp(lf, x_ref[...], g_ref[...])
        dx, dg = vjp(jnp.ones((), F32))
        dx_ref[...] = dx

        @pl.when(pl.program_id(0) == 0)
        def _():
            dg_ref[...] = jnp.zeros_like(dg_ref)
            loss_ref[...] = jnp.zeros_like(loss_ref)

        dg_ref[...] += dg
        loss_ref[...] += jnp.full(loss_ref.shape, val, F32)

    row = pl.BlockSpec((tm, d), lambda i: (i, 0))
    vec = pl.BlockSpec((1, d), lambda i: (0, 0))
    return _pcall(body, name="loss_head", grid=(t // tm,), in_specs=[row, row, vec],
                  out_specs=(row, vec, pl.BlockSpec((1, 128), lambda i: (0, 0))),
                  out_shape=(jax.ShapeDtypeStruct((t, d), F32), jax.ShapeDtypeStruct((1, d), F32),
                             jax.ShapeDtypeStruct((1, 128), F32)), compiler_params=_params())(x, tgt, g)


def _shifted(x, d, prev, nxt, first, last):
    r = x.shape[0]
    row = lax.broadcasted_iota(jnp.int32, x.shape, 0)
    if d < 0:
        out = pltpu.roll(x, -d, 0)
        for q in range(-d):
            pv = jnp.where(first, 0.0, prev[8 + d + q:8 + d + q + 1, :])
            out = jnp.where(row == q, pv, out)
        return out
    out = pltpu.roll(x, r - d, 0)
    for q in range(d):
        nv = jnp.where(last, 0.0, nxt[q:q + 1, :])
        out = jnp.where(row == r - d + q, nv, out)
    return out


def _conv_fwd(p3, w, b, col0, ncol, silu, tc=1024):
    nbatch, s, _ = p3.shape
    ts = min(CONV_ROWS, s)
    nblk = s // ts

    def body(x_ref, pv_ref, nx_ref, w_ref, b_ref, o_ref, *act_ref):
        i = pl.program_id(1)
        first, last = i == 0, i == nblk - 1
        x, pv, nx = x_ref[...], pv_ref[...], nx_ref[...]
        wv = w_ref[...]
        out = b_ref[...] + wv[1:2] * x
        out = out + wv[0:1] * _shifted(x, -1, pv, nx, first, last)
        out = out + wv[2:3] * _shifted(x, 1, pv, nx, first, last)
        out = out + wv[3:4] * _shifted(x, 2, pv, nx, first, last)
        o_ref[...] = out
        if silu:
            act_ref[0][...] = jax.nn.silu(out)

    nb8 = s // 8
    cur = pl.BlockSpec((None, ts, tc), lambda n, i, j: (n, i, col0 + j))
    prev = pl.BlockSpec((None, 8, tc), lambda n, i, j: (n, jnp.maximum(i * (ts // 8) - 1, 0), col0 + j))
    nxt = pl.BlockSpec((None, 8, tc), lambda n, i, j: (n, jnp.minimum((i + 1) * (ts // 8), nb8 - 1), col0 + j))
    out = pl.BlockSpec((None, ts, tc), lambda n, i, j: (n, i, j))
    shp = jax.ShapeDtypeStruct((nbatch, s, ncol * tc), F32)
    return _pcall(body, name=f"conv_fwd{col0}", grid=(nbatch, nblk, ncol),
                  in_specs=[cur, prev, nxt, pl.BlockSpec((4, tc), lambda n, i, j: (0, col0 + j)),
                            pl.BlockSpec((1, tc), lambda n, i, j: (0, col0 + j))],
                  out_specs=(out, out) if silu else out, out_shape=(shp, shp) if silu else shp,
                  compiler_params=_params())(p3, p3, p3, w, b)


def _conv_bwd(dc3, p3, w, col, conv3=None):
    nbatch, s, tc = dc3.shape
    ts = min(CONV_ROWS, s)
    nblk = s // ts
    silu = conv3 is not None

    def body(d_ref, dpv_ref, dnx_ref, x_ref, pv_ref, nx_ref, w_ref, *rest):
        n, i = pl.program_id(0), pl.program_id(1)
        first, last = i == 0, i == nblk - 1
        d, dpv, dnx = d_ref[...], dpv_ref[...], dnx_ref[...]
        if silu:
            d, dpv, dnx = [jax.vjp(jax.nn.silu, c_ref[...])[1](t)[0] for c_ref, t in zip(rest[:3], (d, dpv, dnx))]
        dx_ref, dw_ref = rest[3 * silu:]
        x, pv, nx = x_ref[...], pv_ref[...], nx_ref[...]
        wv = w_ref[...]
        dx = wv[1:2] * d
        dx = dx + wv[0:1] * _shifted(d, 1, dpv, dnx, first, last)
        dx = dx + wv[2:3] * _shifted(d, -1, dpv, dnx, first, last)
        dx = dx + wv[3:4] * _shifted(d, -2, dpv, dnx, first, last)
        dx_ref[...] = dx.astype(dx_ref.dtype)

        @pl.when((n == 0) & (i == 0))
        def _():
            dw_ref[...] = jnp.zeros_like(dw_ref)

        dw_ref[0:1, :] += jnp.sum(d * _shifted(x, -1, pv, nx, first, last), axis=0, keepdims=True)
        dw_ref[1:2, :] += jnp.sum(d * x, axis=0, keepdims=True)
        dw_ref[2:3, :] += jnp.sum(d * _shifted(x, 1, pv, nx, first, last), axis=0, keepdims=True)
        dw_ref[3:4, :] += jnp.sum(d * _shifted(x, 2, pv, nx, first, last), axis=0, keepdims=True)
        dw_ref[4:5, :] += jnp.sum(d, axis=0, keepdims=True)

    nb8 = s // 8

    def specs(j):
        cur = pl.BlockSpec((None, ts, tc), lambda n, i: (n, i, j))
        prev = pl.BlockSpec((None, 8, tc), lambda n, i: (n, jnp.maximum(i * (ts // 8) - 1, 0), j))
        nxt = pl.BlockSpec((None, 8, tc), lambda n, i: (n, jnp.minimum((i + 1) * (ts // 8), nb8 - 1), j))
        return [cur, prev, nxt]

    return _pcall(body, name=f"conv_bwd{col}", grid=(nbatch, nblk),
                  in_specs=specs(0) + specs(col) + [pl.BlockSpec((4, tc), lambda n, i: (0, col))] + specs(col) * silu,
                  out_specs=(specs(0)[0], pl.BlockSpec((8, tc), lambda n, i: (0, 0))),
                  out_shape=(jax.ShapeDtypeStruct((nbatch, s, tc), BF16), jax.ShapeDtypeStruct((8, tc), F32)),
                  compiler_params=_params())(dc3, dc3, dc3, p3, p3, p3, w, *([conv3] * 3 * silu))


def _block_scan(coef, inp, reverse):
    r = coef.shape[0]
    row = lax.broadcasted_iota(jnp.int32, coef.shape, 0)
    a, b = coef, inp
    d = 1
    while d < r:
        if reverse:
            keep = row < r - d
            a_sh, b_sh = pltpu.roll(a, r - d, 0), pltpu.roll(b, r - d, 0)
        else:
            keep = row >= d
            a_sh, b_sh = pltpu.roll(a, d, 0), pltpu.roll(b, d, 0)
        b = b + a * jnp.where(keep, b_sh, 0.0)
        a = a * jnp.where(keep, a_sh, 1.0)
        d *= 2
    return a, b


def _lru_scan(a3, b3, reverse):
    nbatch, s, w = a3.shape
    ts = min(LRU_ROWS, s)
    nblk = s // ts
    edge = 0 if reverse else ts - 1

    def body(a_ref, b_ref, h_ref, carry):
        @pl.when(pl.program_id(1) == 0)
        def _():
            carry[...] = jnp.zeros_like(carry)

        ca, hb = _block_scan(a_ref[...], b_ref[...], reverse)
        h = hb + ca * carry[0:1, :]
        h_ref[...] = h
        carry[0:1, :] = h[edge:edge + 1, :]

    blk = pl.BlockSpec((None, ts, w), (lambda n, i: (n, nblk - 1 - i, 0)) if reverse else (lambda n, i: (n, i, 0)))
    return _pcall(body, name=f"lru_scan_r{int(reverse)}", grid=(nbatch, nblk), in_specs=[blk, blk], out_specs=blk,
                  out_shape=jax.ShapeDtypeStruct((nbatch, s, w), F32), scratch_shapes=[pltpu.VMEM((8, w), F32)],
                  compiler_params=_params())(a3, b3)


def _lru_scan_bwd(a3, h3, dh3, reverse):
    nbatch, s, w = a3.shape
    ts = min(LRU_ROWS, s)
    nblk = s // ts
    nb8 = s // 8
    tpb = ts // 8

    def body(a_ref, aa_ref, h_ref, hh_ref, dh_ref, g_ref, da_ref, carry):
        i = pl.program_id(1)

        @pl.when(i == 0)
        def _():
            carry[...] = jnp.zeros_like(carry)

        a, h = a_ref[...], h_ref[...]
        row = lax.broadcasted_iota(jnp.int32, a.shape, 0)
        if reverse:
            a_edge = jnp.where(i == 0, 0.0, aa_ref[7:8, :])
            c = jnp.where(row == 0, a_edge, pltpu.roll(a, 1, 0))
            h_edge = jnp.where(i == nblk - 1, 0.0, hh_ref[0:1, :])
            h_sh = jnp.where(row == ts - 1, h_edge, pltpu.roll(h, ts - 1, 0))
        else:
            a_edge = jnp.where(i == 0, 0.0, aa_ref[0:1, :])
            c = jnp.where(row == ts - 1, a_edge, pltpu.roll(a, ts - 1, 0))
            h_edge = jnp.where(i == nblk - 1, 0.0, hh_ref[7:8, :])
            h_sh = jnp.where(row == 0, h_edge, pltpu.roll(h, 1, 0))
        cc, gb = _block_scan(c, dh_ref[...], not reverse)
        g = gb + cc * carry[0:1, :]
        g_ref[...] = g
        carry[0:1, :] = g[ts - 1:ts, :] if reverse else g[0:1, :]
        da_ref[...] = g * h_sh

    if reverse:
        bi = lambda i: i
    else:
        bi = lambda i: nblk - 1 - i
    blk = pl.BlockSpec((None, ts, w), lambda n, i: (n, bi(i), 0))
    before = pl.BlockSpec((None, 8, w), lambda n, i: (n, jnp.maximum(bi(i) * tpb - 1, 0), 0))
    after = pl.BlockSpec((None, 8, w), lambda n, i: (n, jnp.minimum((bi(i) + 1) * tpb, nb8 - 1), 0))
    a_tile, h_tile = (before, after) if reverse else (after, before)
    return _pcall(body, name=f"lru_scan_bwd_r{int(reverse)}", grid=(nbatch, nblk), in_specs=[blk, a_tile, blk, h_tile, blk],
                  out_specs=(blk, blk),
                  out_shape=(jax.ShapeDtypeStruct((nbatch, s, w), F32), jax.ShapeDtypeStruct((nbatch, s, w), F32)),
                  scratch_shapes=[pltpu.VMEM((8, w), F32)], compiler_params=_params())(a3, a3, h3, h3, dh3)


def _head_expand(lane0):
    return (jnp.right_shift(lax.broadcasted_iota(jnp.int32, (128, 1024), 1), HEAD_SHIFT) + lane0
            == lax.broadcasted_iota(jnp.int32, (128, 1024), 0)).astype(F32)


def _head_reduce(lane0):
    return (jnp.right_shift(lax.broadcasted_iota(jnp.int32, (1024, 128), 0), HEAD_SHIFT) + lane0
            == lax.broadcasted_iota(jnp.int32, (1024, 128), 1)).astype(F32)


def _time_mask(q, reverse):
    ri = lax.broadcasted_iota(jnp.int32, (q, q), 0)
    ci = lax.broadcasted_iota(jnp.int32, (q, q), 1)
    return (ri <= ci) if reverse else (ri >= ci)


def _ssd_common(xs_ref, bc_ref, dt_ref, al_ref, reverse, lane0):
    q = xs_ref.shape[0]
    edge = 0 if reverse else q - 1
    dt = dt_ref[...]
    a = -jnp.exp(al_ref[...])
    mask = _time_mask(q, reverse)
    expand = _head_expand(lane0)
    cum = _dot01(mask.astype(F32), dt * a, split="b", terms=3)
    cum_x = _dot01(cum, expand, split="a", terms=3)
    dt_x = _dot01(dt, expand, split="a", terms=2)
    last_x = cum_x[edge:edge + 1, :]
    xs = xs_ref[...]
    bc = bc_ref[...]
    return dict(q=q, edge=edge, lane0=lane0, dt=dt, a=a, mask=mask, cum_t=cum.T, cum_x=cum_x, dt_x=dt_x, xs=xs,
                v=xs * dt_x, e_c=jnp.exp(cum_x), w=jnp.exp(last_x - cum_x), e_l=jnp.exp(last_x),
                bm=bc[:, :512], cm=bc[:, 512:])


def _ssd_decay(c, h):
    row = c["lane0"] + h
    seg = c["cum_x"][:, h * SSD_HEADDIM:h * SSD_HEADDIM + 1] - c["cum_t"][row:row + 1, :]
    return jnp.where(c["mask"], jnp.exp(jnp.minimum(seg, 0.0)), 0.0)


def _head_masks():
    lane = jnp.right_shift(lax.broadcasted_iota(jnp.int32, (1, 256), 1), HEAD_SHIFT)
    return [lane == e for e in range(4)]


def _ssd_fwd(xbc3, dt3, alog, reverse):
    nbatch, s, _ = xbc3.shape
    q = min(SSD_CHUNK, s)
    nc = s // q
    lane0 = SSD_HEADS * int(reverse)

    def body(xs_ref, bc_ref, dt_ref, al_ref, y_ref, st_ref, st):
        @pl.when(pl.program_id(1) == 0)
        def _():
            st[...] = jnp.zeros_like(st)

        st_ref[...] = st[...]
        c = _ssd_common(xs_ref, bc_ref, dt_ref, al_ref, reverse, lane0)
        hm = _head_masks()
        for g in range(SSD_GROUPS):
            sl = slice(g * 256, (g + 1) * 256)
            cg, bg = _mx(c["cm"][:, g * 128:(g + 1) * 128]), _mx(c["bm"][:, g * 128:(g + 1) * 128])
            cb = _dot(cg, bg, _NT)
            vg = c["v"][:, sl]
            s0 = st[:, sl]
            yg = _dot(cg, _mx(s0)) * c["e_c"][:, sl]
            for e in range(4):
                m = _ssd_decay(c, 4 * g + e) * cb
                yg = yg + _dot(_mx(m), _mx(jnp.where(hm[e], vg, 0.0)))
            y_ref[:, sl] = yg
            st[:, sl] = c["e_l"][:, sl] * s0 + _dot(bg, _mx(vg * c["w"][:, sl]), _TN)

    ck = (lambda i: nc - 1 - i) if reverse else (lambda i: i)
    xs_spec = pl.BlockSpec((None, q, 1024), lambda n, i: (n, ck(i), 0))
    bc_spec = pl.BlockSpec((None, q, 1024), lambda n, i: (n, ck(i), 1))
    dt_spec = pl.BlockSpec((None, q, 128), lambda n, i: (n, ck(i), 0))
    al_spec = pl.BlockSpec((1, 128), lambda n, i: (0, 0))
    st_spec = pl.BlockSpec((None, None, 128, 1024), lambda n, i: (n, ck(i), 0, 0))
    return _pcall(body, name=f"ssd_fwd_r{int(reverse)}", grid=(nbatch, nc), in_specs=[xs_spec, bc_spec, dt_spec, al_spec],
                  out_specs=(xs_spec, st_spec),
                  out_shape=(jax.ShapeDtypeStruct((nbatch, s, 1024), F32), jax.ShapeDtypeStruct((nbatch, nc, 128, 1024), F32)),
                  scratch_shapes=[pltpu.VMEM((128, 1024), F32)], compiler_params=_params())(xbc3, xbc3, dt3, alog)


def _ssd_bwd(xbc3, dt3, alog, st4, dy3, reverse, add_to=(), scatter=()):
    nbatch, s, _ = xbc3.shape
    q = min(SSD_CHUNK, s)
    nc = s // q
    lane0 = SSD_HEADS * int(reverse)
    nadd, ns = len(add_to), len(scatter)

    def body(xs_ref, bc_ref, dt_ref, al_ref, st0_ref, dy_ref, *rest):
        adds, srcs, rest = rest[:nadd], rest[nadd:nadd + ns], rest[nadd + ns:]
        (dxs_ref, dbc_ref, ddt_ref, dal_ref), lands, dst = rest[:4], rest[4:4 + ns], rest[4 + ns]
        n, i = pl.program_id(0), pl.program_id(1)
        if ns:
            sends, arrivals = _scatter_copies(srcs, lands, *rest[5 + ns:])

            @pl.when((n == 0) & (i == 0))
            def _():
                for cp in sends:
                    cp.start()

        @pl.when(i == 0)
        def _():
            dst[...] = jnp.zeros_like(dst)

        @pl.when((i == 0) & (n == 0))
        def _():
            dal_ref[...] = jnp.zeros_like(dal_ref)

        c = _ssd_common(xs_ref, bc_ref, dt_ref, al_ref, reverse, lane0)
        hm = _head_masks()
        reduce_m = _head_reduce(lane0)
        s0_all, ds1_all, dy = st0_ref[...], dst[...], dy_ref[...]
        lane = lax.broadcasted_iota(jnp.int32, (q, 128), 1)
        sub = lax.broadcasted_iota(jnp.int32, (128, q), 0)
        rowacc = jnp.zeros((q, 128), F32)
        colacc_t = jnp.zeros((128, q), F32)
        dv_l, yst_l, dvbar_l, dk_l, dc_l = [], [], [], [], []
        for g in range(SSD_GROUPS):
            sl = slice(g * 256, (g + 1) * 256)
            cg, bg = _mx(c["cm"][:, g * 128:(g + 1) * 128]), _mx(c["bm"][:, g * 128:(g + 1) * 128])
            cb = _dot(cg, bg, _NT)
            vg, dyg, wg, ecg = c["v"][:, sl], dy[:, sl], c["w"][:, sl], c["e_c"][:, sl]
            s0, ds1 = _mx(s0_all[:, sl]), _mx(ds1_all[:, sl])
            dye = _mx(dyg * ecg)
            yst_l.append(_dot(cg, s0) * ecg)
            dcg = _dot(dye, s0, _NT)
            dst[:, sl] = c["e_l"][:, sl] * ds1_all[:, sl] + _dot(cg, dye, _TN)
            vbar = _mx(vg * wg)
            dvbar = _dot(bg, ds1)
            dvbar_l.append(dvbar)
            dvg = dvbar * wg
            dkg = _dot(vbar, ds1, _NT)
            for e in range(4):
                h = 4 * g + e
                m = _ssd_decay(c, h)
                dyh, vh = _mx(jnp.where(hm[e], dyg, 0.0)), _mx(jnp.where(hm[e], vg, 0.0))
                dvg = dvg + _dot(_mx(m * cb), dyh, _TN)
                dcb = _dot(dyh, vh, _NT) * m
                dcbb = _mx(dcb)
                dcg = dcg + _dot(dcbb, bg)
                dkg = dkg + _dot(dcbb, cg, _TN)
                wmat = dcb * cb
                rowacc = jnp.where(lane == lane0 + h, jnp.sum(wmat, axis=1, keepdims=True), rowacc)
                colacc_t = jnp.where(sub == lane0 + h, jnp.sum(wmat, axis=0, keepdims=True), colacc_t)
            dv_l.append(dvg)
            dk_l.append(dkg)
            dc_l.append(dcg)
        dv = jnp.concatenate(dv_l, axis=1)
        yst = jnp.concatenate(yst_l, axis=1)
        dvbar = jnp.concatenate(dvbar_l, axis=1)
        t1 = _dot01(dy * yst, reduce_m, split="a", terms=3)
        t2 = _dot01(c["v"] * c["w"] * dvbar, reduce_m, split="a", terms=3)
        dlast = jnp.sum(t2, axis=0, keepdims=True) + _dot01(
            c["e_l"] * jnp.sum(ds1_all * s0_all, axis=0, keepdims=True), reduce_m, split="a", terms=2)
        dcum = rowacc - colacc_t.T + t1 - t2
        dcum = dcum + jnp.where(lax.broadcasted_iota(jnp.int32, (q, 128), 0) == c["edge"], dlast, 0.0)
        dda = _dot01(c["mask"].astype(F32), dcum, _TN, split="b", terms=3)
        ddt = dda * c["a"] + _dot01(dv * c["xs"], reduce_m, split="a", terms=2)
        dal_ref[...] += jnp.sum(dda * c["dt"], axis=0, keepdims=True) * c["a"]
        dxs = dv * c["dt_x"]
        dbc = jnp.concatenate(dk_l + dc_l, axis=1)
        if nadd:
            for a_ref in adds[:-2]:
                dxs = dxs + a_ref[...]
            dbc = dbc + adds[-2][...]
            ddt = ddt + adds[-1][...]
        ddt_ref[...] = ddt
        dxs_ref[...] = dxs
        dbc_ref[...] = dbc
        if ns:
            @pl.when((n == nbatch - 1) & (i == nc - 1))
            def _():
                for cp in arrivals:
                    cp.wait_recv()
                for cp in sends:
                    cp.wait_send()

    ck = (lambda i: i) if reverse else (lambda i: nc - 1 - i)
    xs_spec = pl.BlockSpec((None, q, 1024), lambda n, i: (n, ck(i), 0))
    bc_spec = pl.BlockSpec((None, q, 1024), lambda n, i: (n, ck(i), 1))
    dt_spec = pl.BlockSpec((None, q, 128), lambda n, i: (n, ck(i), 0))
    al_spec = pl.BlockSpec((1, 128), lambda n, i: (0, 0))
    st_spec = pl.BlockSpec((None, None, 128, 1024), lambda n, i: (n, ck(i), 0, 0))
    return _pcall(body, name=f"ssd_bwd_r{int(reverse)}", grid=(nbatch, nc),
                  in_specs=([xs_spec, bc_spec, dt_spec, al_spec, st_spec, xs_spec] + [xs_spec] * (nadd - 1)
                            + [dt_spec] * bool(nadd) + [ANY] * ns),
                  out_specs=(xs_spec, xs_spec, dt_spec, al_spec) + (ANY,) * ns,
                  out_shape=(jax.ShapeDtypeStruct((nbatch, s, 1024), F32), jax.ShapeDtypeStruct((nbatch, s, 1024), F32),
                             jax.ShapeDtypeStruct((nbatch, s, 128), F32), jax.ShapeDtypeStruct((1, 128), F32))
                  + tuple(jax.ShapeDtypeStruct(c.shape, c.dtype) for c in scatter),
                  scratch_shapes=[pltpu.VMEM((128, 1024), F32)] + (_scatter_scratch(ns) if ns else []),
                  compiler_params=_params())(xbc3, xbc3, dt3, alog, st4, dy3, *add_to, *scatter)


def _gla_block(q, k, g, reverse):
    bq = g.shape[0]
    nsub = bq // HGRN_SUB
    edge = 0 if reverse else bq - 1
    ri = lax.broadcasted_iota(jnp.int32, (bq, bq), 0)
    ci = lax.broadcasted_iota(jnp.int32, (bq, bq), 1)
    rb, cb = jnp.right_shift(ri, HGRN_SUB_SHIFT), jnp.right_shift(ci, HGRN_SUB_SHIFT)
    mask = (ri <= ci) if reverse else (ri >= ci)
    m_within = (mask & (rb == cb)).astype(F32)
    m_before = ((cb > rb) if reverse else (cb < rb)).astype(F32)
    bl = _dot01(m_within, g, split="b", terms=3)
    c = _dot01(m_before, g, split="b", terms=3)
    last = c[edge:edge + 1, :] + bl[edge:edge + 1, :]
    ebl, enbl, ec, elc = jnp.exp(bl), jnp.exp(-bl), jnp.exp(c), jnp.exp(last - c)
    qh = q * HGRN_SCALE * ebl
    kh = k * enbl
    blk = jnp.right_shift(lax.broadcasted_iota(jnp.int32, (bq, 1), 0), HGRN_SUB_SHIFT)
    scale = []
    for i in range(nsub):
        valid = (blk >= i) if reverse else (blk <= i)
        ex = jnp.where(valid, c[i * HGRN_SUB:i * HGRN_SUB + 1, :] - c, 0.0)
        scale.append(jnp.where(valid, jnp.exp(ex), 0.0))
    return dict(bq=bq, nsub=nsub, edge=edge, mask=mask, m_within=m_within, m_before=m_before, ebl=ebl, enbl=enbl, ec=ec,
                elc=elc, e_l=jnp.exp(last), qh=qh, qt=qh * ec, kh=kh, kb=kh * elc, scale=scale)


def _gla_scores(c, hs):
    keys = [_mx(c["kh"][:, hs] * c["scale"][i][:, hs]) for i in range(c["nsub"])]
    rows = [_dot(_mx(c["qh"][i * HGRN_SUB:(i + 1) * HGRN_SUB, hs]), keys[i], _NT) for i in range(c["nsub"])]
    return jnp.where(c["mask"], jnp.concatenate(rows, axis=0), 0.0), keys


def _gla_specs(nbatch, s, w, reverse_order):
    bq = min(HGRN_BLOCK, s)
    nblk = s // bq
    bi = (lambda i: nblk - 1 - i) if reverse_order else (lambda i: i)
    col = lambda cb: pl.BlockSpec((nbatch, bq, w), lambda i: (0, bi(i), cb))
    st_spec = pl.BlockSpec((nbatch, None, 128, w), lambda i: (0, bi(i), 0, 0))
    return bq, nblk, col, st_spec


def _gla_fwd(proj3, l0, l1, reverse):
    nbatch, s, w5 = proj3.shape
    w = w5 // 5
    bq, nblk, col, st_spec = _gla_specs(nbatch, s, w, reverse)
    vec = pl.BlockSpec((1, w), lambda i: (0, 0))

    def body(q_ref, f_ref, v_ref, l0_ref, l1_ref, o_ref, st_ref, st):
        @pl.when(pl.program_id(0) == 0)
        def _():
            st[...] = jnp.zeros_like(st)

        for b in range(nbatch):
            st_ref[b] = st[b]
            k, g = _f_hgrn_pre(f_ref[b], l0_ref[...], l1_ref[...])
            c = _gla_block(q_ref[b], k, g, reverse)
            v = v_ref[b]
            for h in range(HGRN_HEADS):
                hs = slice(h * 128, (h + 1) * 128)
                att, _ = _gla_scores(c, hs)
                vb = _mx(v[:, hs])
                s0 = st[b, :, hs]
                o_ref[b, :, hs] = _dot(_mx(att), vb) + _dot(_mx(c["qt"][:, hs]), _mx(s0), _NT)
                st[b, :, hs] = s0 * c["e_l"][:, hs] + _dot(vb, _mx(c["kb"][:, hs]), _TN)

    return _pcall(body, name=f"gla_fwd_r{int(reverse)}", grid=(nblk,),
                  in_specs=[col(0), col(1 + int(reverse)), col(3), vec, vec], out_specs=(col(0), st_spec),
                  out_shape=(jax.ShapeDtypeStruct((nbatch, s, w), F32), jax.ShapeDtypeStruct((nbatch, nblk, 128, w), F32)),
                  scratch_shapes=[pltpu.VMEM((nbatch, 128, w), F32)], compiler_params=_params())(proj3, proj3, proj3, l0, l1)


def _gla_bwd(proj3, l0, l1, st4, do3, reverse, add_to=None):
    nbatch, s, w5 = proj3.shape
    w = w5 // 5
    bq, nblk, col, st_spec = _gla_specs(nbatch, s, w, not reverse)
    nadd = 0 if add_to is None else 2
    vec = pl.BlockSpec((1, w), lambda i: (0, 0))

    def body(q_ref, f_ref, v_ref, l0_ref, l1_ref, st_ref, do_ref, *rest):
        adds, (dq_ref, df_ref, dv_ref, dl0_ref, dl1_ref, dst) = rest[:nadd], rest[nadd:]

        @pl.when(pl.program_id(0) == 0)
        def _():
            dst[...] = jnp.zeros_like(dst)
            dl0_ref[...] = jnp.zeros_like(dl0_ref)
            dl1_ref[...] = jnp.zeros_like(dl1_ref)

        row = lax.broadcasted_iota(jnp.int32, (bq, 128), 0)
        for b in range(nbatch):
            (k, g), pre_vjp = jax.vjp(_f_hgrn_pre, f_ref[b], l0_ref[...], l1_ref[...])
            c = _gla_block(q_ref[b], k, g, reverse)
            s0_all, ds1_all = st_ref[b], dst[b]
            v, dy = v_ref[b], do_ref[b]
            dbl_l, dc_l, dk_l = [], [], []
            for h in range(HGRN_HEADS):
                hs = slice(h * 128, (h + 1) * 128)
                att, keys = _gla_scores(c, hs)
                qh, qt, kh, kb = c["qh"][:, hs], c["qt"][:, hs], c["kh"][:, hs], c["kb"][:, hs]
                vb, dyb = _mx(v[:, hs]), _mx(dy[:, hs])
                s0, ds1 = s0_all[:, hs], ds1_all[:, hs]
                datt = _mx(jnp.where(c["mask"], _dot(dyb, vb, _NT), 0.0))
                dqh_rows = []
                dkh = jnp.zeros((bq, 128), F32)
                dc = jnp.zeros((bq, 128), F32)
                for i in range(c["nsub"]):
                    rs = slice(i * HGRN_SUB, (i + 1) * HGRN_SUB)
                    dqh_rows.append(_dot(datt[rs], keys[i]))
                    dki = _dot(datt[rs], _mx(qh[rs]), _TN)
                    sc = c["scale"][i][:, hs]
                    dkh = dkh + dki * sc
                    dex = dki * (kh * sc)
                    dc = dc - dex + jnp.where(row == i * HGRN_SUB, jnp.sum(dex, axis=0, keepdims=True), 0.0)
                dqt = _dot(dyb, _mx(s0))
                dkb = _dot(vb, _mx(ds1))
                dv = _dot(_mx(att), dyb, _TN) + _dot(_mx(kb), _mx(ds1), _NT)
                dst[b, :, hs] = c["e_l"][:, hs] * ds1 + _dot(dyb, _mx(qt), _TN)
                dqh = jnp.concatenate(dqh_rows, axis=0) + dqt * c["ec"][:, hs]
                dkh = dkh + dkb * c["elc"][:, hs]
                kbk = dkb * kb
                dlast = jnp.sum(kbk, axis=0, keepdims=True) + c["e_l"][:, hs] * jnp.sum(ds1 * s0, axis=0, keepdims=True)
                at_edge = jnp.where(row == c["edge"], dlast, 0.0)
                dc_l.append(dc + dqt * qt - kbk + at_edge)
                dbl_l.append(dqh * qh - dkh * kh + at_edge)
                dq = dqh * c["ebl"][:, hs] * HGRN_SCALE
                if nadd:
                    dq, dv = dq + adds[0][b, :, hs], dv + adds[1][b, :, hs]
                dq_ref[b, :, hs] = dq.astype(dq_ref.dtype)
                dv_ref[b, :, hs] = dv.astype(dv_ref.dtype)
                dk_l.append(dkh * c["enbl"][:, hs])
            dg = (_dot01(c["m_within"], jnp.concatenate(dbl_l, axis=1), _TN, split="b", terms=2)
                  + _dot01(c["m_before"], jnp.concatenate(dc_l, axis=1), _TN, split="b", terms=2))
            df, d0, d1 = pre_vjp((jnp.concatenate(dk_l, axis=1), dg))
            df_ref[b] = df.astype(df_ref.dtype)
            dl0_ref[...] += d0
            dl1_ref[...] += d1

    shp_sum = jax.ShapeDtypeStruct((nbatch, s, w), BF16 if nadd else F32)
    shp_vec = jax.ShapeDtypeStruct((1, w), F32)
    return _pcall(body, name=f"gla_bwd_r{int(reverse)}", grid=(nblk,),
                  in_specs=[col(0), col(1 + int(reverse)), col(3), vec, vec, st_spec, col(0)] + [col(0)] * nadd,
                  out_specs=(col(0), col(0), col(0), vec, vec),
                  out_shape=(shp_sum, jax.ShapeDtypeStruct((nbatch, s, w), BF16), shp_sum, shp_vec, shp_vec),
                  scratch_shapes=[pltpu.VMEM((nbatch, 128, w), F32)],
                  compiler_params=_params())(proj3, proj3, proj3, l0, l1, st4, do3, *(add_to or ()))


DIRS = (False, True)


def _block_diag(w):
    eye = jnp.eye(16, dtype=w.dtype)
    return (eye[:, None, :, None] * w[:, :, None, :]).reshape(1024, 1024)


def _diag_blocks(m):
    m4 = m.reshape(16, 64, 16, 64)
    return jnp.stack([m4[i, :, i, :] for i in range(16)], axis=0)


def _pad_lanes(v, n=128):
    return jnp.pad(v, [(0, 0)] * (v.ndim - 1) + [(0, n - v.shape[-1])])


def _mlp_fwd(tag, x, nw, w1, w2):
    (h,) = _pw_fwd(f"{tag}_norm", _f_norm, [(x, 0)], [(nw, 0)], [BF16], 1024, 1)
    a, r = _mm(f"{tag}_up", h, w1, "nn", relu2=True)
    return _mm(f"{tag}_down", r, w2, "nn", res=x), (h, a, r)


def _mlp_bwd(tag, x, nw, w1, w2, saved, dxo):
    h, a, r = saved
    dw2 = _mm(f"{tag}_dw2", r, dxo, "tn")
    da = _mm(f"{tag}_da", dxo, w2, "nt", relu2_of=a, out_dtype=BF16)
    dw1 = _mm(f"{tag}_dw1", h, da, "tn", col_shards=4)
    dh = _mm(f"{tag}_dh", da, w1, "nt")
    (dx,), (dnw,) = _pw_bwd(f"{tag}_dnorm", _f_norm, [(x, 0)], [(nw, 0)], [dh], 1024, 1, [0], adds={0: dxo})
    return dx, dw1, dw2, dnw


def _split_in0(pieces, dt_piece):
    tm = 256

    def body(p0, p1, p2, p3, p4, p5, o_ref):
        full = jnp.concatenate([p0[...], p1[...], p2[...], p3[...], p4[...], p5[:, :32]], axis=1)
        for j in range(4):
            o_ref[j] = full[:, 1288 * j:1288 * (j + 1)]

    blk = pl.BlockSpec((tm, 1024), lambda i: (i, 0))
    return _pcall(body, name="split_in0", grid=(1024 // tm,), in_specs=[blk] * 5 + [pl.BlockSpec((tm, 128), lambda i: (i, 0))],
                  out_specs=pl.BlockSpec((4, tm, 1288), lambda i: (0, i, 0)),
                  out_shape=jax.ShapeDtypeStruct((4, 1024, 1288), F32), compiler_params=_params())(*pieces, dt_piece)


def _assemble_in0(shards):
    tm = 256

    def body(s_ref, m_ref, d_ref):
        full = jnp.concatenate([s_ref[j] for j in range(4)], axis=1)
        m_ref[...] = full[:, :5120]
        d_ref[...] = jnp.concatenate([full[:, 5120:5152], jnp.zeros((tm, 96), full.dtype)], axis=1)

    return _pcall(body, name="assemble_in0", grid=(1024 // tm,), in_specs=[pl.BlockSpec((4, tm, 1288), lambda i: (0, i, 0))],
                  out_specs=(pl.BlockSpec((tm, 5120), lambda i: (i, 0)), pl.BlockSpec((tm, 128), lambda i: (i, 0))),
                  out_shape=(jax.ShapeDtypeStruct((1024, 5120), shards.dtype), jax.ShapeDtypeStruct((1024, 128), shards.dtype)),
                  compiler_params=_params())(shards)


EARLY = ("odd_w_in", "odd_w_out", "mlp_w1_l1", "mlp_w2_l1")
LATE = ("even_w_in", "even_w_out", "mlp_w1_l0", "mlp_w2_l0")


def _local_step(x3, tgt3, w, w_main0, w_dt0, pair_reduce=None):
    nb, s, d = x3.shape
    t = nb * s
    x0 = x3.reshape(t, d)
    tgt = tgt3.reshape(t, d)
    grads = {}
    row = lambda v: v.reshape(1, -1)
    to3 = lambda v: v.reshape(nb, s, v.shape[-1])
    to2 = lambda v: v.reshape(-1, v.shape[-1])

    conv_w, conv_b = w["even_conv_w"][0], row(w["even_conv_b"][0])
    nmix0 = row(w["norm_mix"][0])
    (h0,) = _pw_fwd("l0_norm", _f_norm, [(x0, 0)], [(nmix0, 0)], [BF16], 1024, 1)
    proj0 = _mm("l0_proj", h0, w_main0, "nn")
    dt_raw = _mm("l0_proj_dt", h0, w_dt0, "nn")
    conv2, xbc3 = _conv_fwd(to3(proj0), conv_w, conv_b, 0, 2, True)
    u_lru = to2(_conv_fwd(to3(proj0), conv_w, conv_b, 2, 1, False))
    xbc = to2(xbc3)
    dt_bias = _pad_lanes(w["ssd_dt_bias"][0].reshape(1, 32))
    (dt,) = _pw_fwd("l0_dt", _f_softplus, [(dt_raw, 0)], [(dt_bias, 0)], [F32], 128, 1)
    dt3 = to3(dt)
    alog = _pad_lanes(w["ssd_a_log"][0].reshape(1, 32))
    ssd = [_ssd_fwd(xbc3, dt3, alog, r) for r in DIRS]
    yf, yb = to2(ssd[0][0]), to2(ssd[1][0])
    dskip = jnp.repeat(w["ssd_d"][0], SSD_HEADDIM).reshape(1, 1024)
    snw = row(w["ssd_norm_w"][0])
    ssd_ins = [(yf, 0), (yb, 0), (xbc, 0), (proj0, 3)]
    (ya,) = _pw_fwd("l0_ssd_post", _f_ssd_post, ssd_ins, [(dskip, 0), (snw, 0)], [BF16], 1024, 1, groups=SSD_GROUPS)
    w_gates = [_block_diag(w[k][0, r]).astype(MXU_DTYPE) for r in range(2) for k in ("lru_w_a", "lru_w_x")]
    pre = [_mm(f"l0_lru_pre{i}", u_lru, wg, "nn") for i, wg in enumerate(w_gates)]
    lru_par = [[(row(w[k][0, r]), 0) for k in ("lru_b_a", "lru_b_x", "lru_lambda")] for r in range(2)]
    lru_ins = [[(pre[2 * r], 0), (pre[2 * r + 1], 0), (u_lru, 0)] for r in range(2)]
    ab = [_pw_fwd(f"l0_lru_gates{r}", _f_lru_gates, lru_ins[r], lru_par[r], [F32, F32], 1024, 1) for r in range(2)]
    hs = [_lru_scan(to3(ab[r][0]), to3(ab[r][1]), DIRS[r]) for r in range(2)]
    lru_post_ins = [(to2(hs[0]), 0), (to2(hs[1]), 0), (proj0, 4)]
    (ybm,) = _pw_fwd("l0_lru_post", _f_lru_post, lru_post_ins, [], [BF16], 1024, 1)
    w_out0 = w["even_w_out"][0]
    x1 = _mm("l0_out_a", ya, w_out0[:1024], "nn", res=x0)
    x1 = _mm("l0_out_b", ybm, w_out0[1024:], "nn", res=x1)
    nmlp0 = row(w["norm_mlp"][0])
    x2, mlp0 = _mlp_fwd("l0_mlp", x1, nmlp0, w["mlp_w1"][0], w["mlp_w2"][0])

    w_in1 = w["odd_w_in"][0]
    nmix1 = row(w["norm_mix"][1])
    (h1,) = _pw_fwd("l1_norm", _f_norm, [(x2, 0)], [(nmix1, 0)], [BF16], 1024, 1)
    proj1 = _mm("l1_proj", h1, w_in1, "nn")
    proj1_3 = to3(proj1)
    lb0, lb1 = row(w["hgrn_lb_logits"][0]), row(w["hgrn_lb_logits"][1])
    gla = [_gla_fwd(proj1_3, lb0, lb1, r) for r in DIRS]
    hnw = row(w["hgrn_norm_w"][0])
    hpost_ins = [(to2(gla[0][0]), 0), (to2(gla[1][0]), 0), (proj1, 4)]
    (yo,) = _pw_fwd("l1_hgrn_post", _f_hgrn_post, hpost_ins, [(hnw, 0)], [BF16], 1024, 1, groups=HGRN_HEADS)
    w_out1 = w["odd_w_out"][0]
    x3_ = _mm("l1_out", yo, w_out1, "nn", res=x2)
    nmlp1 = row(w["norm_mlp"][1])
    x4, mlp1 = _mlp_fwd("l1_mlp", x3_, nmlp1, w["mlp_w1"][1], w["mlp_w2"][1])

    dx4, dnf, loss = _loss_head(x4, tgt, row(w["norm_final"]))
    grads["norm_final"] = dnf.reshape(-1)

    dx3, dw1_1, dw2_1, dnmlp1 = _mlp_bwd("l1_mlp", x3_, nmlp1, w["mlp_w1"][1], w["mlp_w2"][1], mlp1, dx4)
    big = {"odd_w_out": _mm("l1_dwout", yo, dx3, "tn").reshape(4, 256, 1024)}
    dyo = _mm("l1_dyo", dx3, w_out1, "nt")
    (do, dgate1), (dhnw,) = _pw_bwd("l1_hgrn_post_b", _f_hgrn_post, hpost_ins, [(hnw, 0)], [dyo], 1024, 1, [0, 2],
                                    out_dtypes=[F32, BF16], groups=HGRN_HEADS)
    grads["hgrn_norm_w"] = dhnw
    do3 = to3(do)
    gb = [_gla_bwd(proj1_3, lb0, lb1, gla[0][1], do3, False)]
    gb.append(_gla_bwd(proj1_3, lb0, lb1, gla[1][1], do3, True, add_to=(gb[0][0], gb[0][2])))
    grads["hgrn_lb_logits"] = jnp.concatenate([gb[0][3] + gb[1][3], gb[0][4] + gb[1][4]], axis=0)
    dparts1 = [to2(gb[1][0]), to2(gb[0][1]), to2(gb[1][1]), to2(gb[1][2]), dgate1]
    dwin1 = jnp.concatenate([_mm(f"l1_dwin{i}", h1, dp, "tn") for i, dp in enumerate(dparts1)], axis=1)
    big["odd_w_in"] = dwin1.reshape(1024, 4, 1280).transpose(1, 0, 2)
    dh1 = _mm_sum_nt("l1_dh", dparts1, [w_in1[:, i * 1024:(i + 1) * 1024] for i in range(5)])
    (dx2,), (dnmix1,) = _pw_bwd("l1_dnorm", _f_norm, [(x2, 0)], [(nmix1, 0)], [dh1], 1024, 1, [0], adds={0: dx3})
    big["mlp_w1_l1"], big["mlp_w2_l1"] = dw1_1, dw2_1.reshape(4, 1024, 1024)
    early_sums = tuple(pair_reduce(EARLY, [big[n] for n in EARLY])) if pair_reduce else ()

    dx1, dw1_0, dw2_0, dnmlp0 = _mlp_bwd("l0_mlp", x1, nmlp0, w["mlp_w1"][0], w["mlp_w2"][0], mlp0, dx2)
    big["mlp_w1_l0"], big["mlp_w2_l0"] = dw1_0, dw2_0.reshape(4, 1024, 1024)
    grads["norm_mlp"] = jnp.concatenate([dnmlp0, dnmlp1], axis=0)
    big["even_w_out"] = jnp.concatenate([_mm("l0_dwout_a", ya, dx1, "tn"), _mm("l0_dwout_b", ybm, dx1, "tn")],
                                        axis=0).reshape(4, 512, 1024)
    dya = _mm("l0_dya", dx1, w_out0[:1024], "nt")
    dyb = _mm("l0_dyb", dx1, w_out0[1024:], "nt")
    (dh, dgate0), _ = _pw_bwd("l0_lru_post_b", _f_lru_post, lru_post_ins, [], [dyb], 1024, 1, [0, 2], out_dtypes=[F32, BF16])
    dh3 = to3(dh)
    dpre, du_parts, dlru = [], [], {k: [] for k in ("lru_b_a", "lru_b_x", "lru_lambda")}
    for r in range(2):
        g_r, da_r = _lru_scan_bwd(to3(ab[r][0]), hs[r], dh3, DIRS[r])
        (dpa, dpx, du_r), (dba, dbx, dlam) = _pw_bwd(f"l0_lru_gates_b{r}", _f_lru_gates, lru_ins[r], lru_par[r],
                                                     [to2(da_r), to2(g_r)], 1024, 1, [0, 1, 2],
                                                     out_dtypes=[BF16, BF16, F32])
        dpre += [dpa, dpx]
        du_parts.append(du_r)
        dlru["lru_b_a"].append(dba)
        dlru["lru_b_x"].append(dbx)
        dlru["lru_lambda"].append(dlam)
    for k, v in dlru.items():
        grads[k] = jnp.concatenate(v, axis=0)[None]
    dwg = [_diag_blocks(_mm(f"l0_dwgate{i}", u_lru, dp, "tn")) for i, dp in enumerate(dpre)]
    grads["lru_w_a"] = jnp.stack([dwg[0], dwg[2]])[None]
    grads["lru_w_x"] = jnp.stack([dwg[1], dwg[3]])[None]
    du_gate = _mm_sum_nt("l0_du_gate", dpre, w_gates)
    (du,) = _pw_fwd("l0_du", _f_add3, [(du_parts[0], 0), (du_parts[1], 0), (du_gate, 0)], [], [F32], 1024, 1)
    (dy, dxs_skip, dz), (ddskip, dsnw) = _pw_bwd("l0_ssd_post_b", _f_ssd_post, ssd_ins, [(dskip, 0), (snw, 0)], [dya],
                                                 1024, 1, [0, 2, 3], out_dtypes=[F32, F32, BF16], groups=SSD_GROUPS)
    grads["ssd_d"] = ddskip.reshape(SSD_HEADS, SSD_HEADDIM).sum(axis=1)[None]
    grads["ssd_norm_w"] = dsnw
    dy3 = to3(dy)
    sb0 = _ssd_bwd(xbc3, dt3, alog, ssd[0][1], dy3, False, scatter=early_sums)
    sb1 = _ssd_bwd(xbc3, dt3, alog, ssd[1][1], dy3, True, add_to=(sb0[0], to3(dxs_skip), sb0[1], sb0[2]))
    grads["ssd_a_log"] = (sb0[3] + sb1[3])[:, :32].reshape(1, 2, 16)
    ddt = to2(sb1[2])
    (ddt_raw,), (ddtb,) = _pw_bwd("l0_dt_b", _f_softplus, [(dt_raw, 0)], [(dt_bias, 0)], [ddt], 128, 1, [0])
    grads["ssd_dt_bias"] = ddtb[:, :32].reshape(1, 2, 16)
    cb = [_conv_bwd(sb1[0], to3(proj0), conv_w, 0, conv2), _conv_bwd(sb1[1], to3(proj0), conv_w, 1, conv2),
          _conv_bwd(to3(du), to3(proj0), conv_w, 2)]
    dcw = jnp.concatenate([c_[1] for c_ in cb], axis=1)
    grads["even_conv_w"] = dcw[:4][None]
    grads["even_conv_b"] = dcw[4:5]
    dparts0 = [to2(c_[0]) for c_ in cb] + [dz, dgate0]
    dwin0 = [_mm(f"l0_dwin{i}", h0, dp, "tn") for i, dp in enumerate(dparts0)]
    big["even_w_in"] = _split_in0(dwin0, _mm("l0_dwin_dt", h0, ddt_raw, "tn"))
    dh0 = _mm_sum_nt("l0_dh", dparts0 + [ddt_raw], [w_main0[:, i * 1024:(i + 1) * 1024] for i in range(5)] + [w_dt0])
    (dx0,), (dnmix0,) = _pw_bwd("l0_dnorm", _f_norm, [(x0, 0)], [(nmix0, 0)], [dh0], 1024, 1, [0], adds={0: dx1})
    grads["norm_mix"] = jnp.concatenate([dnmix0, dnmix1], axis=0)
    return loss, dx0.reshape(nb, s, d), grads, big, (early_sums, sb0[4:])


ANY = pl.BlockSpec(memory_space=pl.ANY)


def _place():
    return lax.axis_index("x"), lax.axis_index("y"), lax.axis_index("c")


def _remote(src, dst, send_sems, recv_sems, k, to):
    return pltpu.make_async_remote_copy(src_ref=src, dst_ref=dst, send_sem=send_sems.at[k], recv_sem=recv_sems.at[k],
                                        device_id=to, device_id_type=MESH)


def _gather_chips(shards):
    n = len(shards)
    halves = [s.shape[0] // 2 for s in shards]

    def body(*refs):
        x_refs, out_refs = refs[:n], refs[n:2 * n]
        send_sems, recv_sems = refs[2 * n:]
        x, y, c = _place()
        sibling = (x, y, 1 - c)
        chips = [(1 - x, y), (x, 1 - y), (1 - x, 1 - y)]

        def blk(t, px, py, hc):
            return out_refs[t].at[2 * px + py, pl.ds(hc * halves[t], halves[t]), :]

        def src(t):
            return x_refs[t].at[pl.ds(c * halves[t], halves[t]), :]

        first = [_remote(src(t), blk(t, x, y, c), send_sems, recv_sems, 6 * t + j, (*chip, c))
                 for t in range(n) for j, chip in enumerate(chips)]
        for cp in first:
            cp.start()
        passed = []
        for t in range(n):
            for j, chip in enumerate(chips):
                _remote(src(t), blk(t, *chip, c), send_sems, recv_sems, 6 * t + j, (*chip, c)).wait_recv()
                cp = _remote(blk(t, *chip, c), blk(t, *chip, c), send_sems, recv_sems, 6 * t + 3 + j, sibling)
                cp.start()
                passed.append(cp)
        for t in range(n):
            for j, chip in enumerate(chips):
                _remote(src(t), blk(t, *chip, 1 - c), send_sems, recv_sems, 6 * t + 3 + j, sibling).wait_recv()
        for cp in first + passed:
            cp.wait_send()

    return _pcall(body, name="gather_weights", in_specs=[ANY] * n, out_specs=(ANY,) * n,
                  out_shape=tuple(jax.ShapeDtypeStruct((4,) + s.shape, s.dtype) for s in shards),
                  scratch_shapes=[pltpu.SemaphoreType.DMA((6 * n,)), pltpu.SemaphoreType.DMA((6 * n,))],
                  compiler_params=_params())(*shards)


def _pair_swap(name, gps):
    n = len(gps)
    halves = [g.shape[1] // 2 for g in gps]

    def body(*refs):
        g_refs, land_refs = refs[:n], refs[n:2 * n]
        send_sems, recv_sems = refs[2 * n:]
        x, y, c = _place()
        cps = [_remote(g_refs[t].at[j, pl.ds((1 - c) * halves[t], halves[t]), :], land_refs[t].at[j], send_sems, recv_sems,
                       4 * t + j, (x, y, 1 - c)) for t in range(n) for j in range(4)]
        for cp in cps:
            cp.start()
        for cp in cps:
            cp.wait()

    return _pcall(body, name=f"pair_swap_{name}", in_specs=[ANY] * n, out_specs=(ANY,) * n,
                  out_shape=tuple(jax.ShapeDtypeStruct((4, h, g.shape[2]), F32) for g, h in zip(gps, halves)),
                  scratch_shapes=[pltpu.SemaphoreType.DMA((4 * n,)), pltpu.SemaphoreType.DMA((4 * n,))],
                  compiler_params=_params())(*gps)


def _pair_add(name, gp, land, cidx):
    _, half, cols = land.shape
    tr = _tile(half, 512)
    nh = half // tr

    def body(c_ref, g_ref, l_ref, o_ref):
        o_ref[...] = (g_ref[...] + l_ref[...]).astype(o_ref.dtype)

    grid_spec = pltpu.PrefetchScalarGridSpec(
        num_scalar_prefetch=1, grid=(4, nh),
        in_specs=[pl.BlockSpec((None, tr, cols), lambda j, i, c: (j, c[0] * nh + i, 0)),
                  pl.BlockSpec((None, tr, cols), lambda j, i, c: (j, i, 0))],
        out_specs=pl.BlockSpec((None, tr, cols), lambda j, i, c: (j, i, 0)))
    return _pcall(body, name=f"pair_add_{name}", grid_spec=grid_spec, out_shape=jax.ShapeDtypeStruct((4, half, cols), BF16),
                  compiler_params=_params())(cidx, gp, land)


def _scatter_copies(s_refs, land_refs, send_sems, recv_sems):
    x, y, c = _place()
    me = 2 * x + y
    chips = [(1 - x, y), (x, 1 - y), (1 - x, 1 - y)]
    pairs = [(t, j, px, py) for t in range(len(s_refs)) for j, (px, py) in enumerate(chips)]
    sends = [_remote(s_refs[t].at[2 * px + py], land_refs[t].at[me], send_sems, recv_sems, 3 * t + j, (px, py, c))
             for t, j, px, py in pairs]
    arrivals = [_remote(s_refs[t].at[me], land_refs[t].at[2 * px + py], send_sems, recv_sems, 3 * t + j, (px, py, c))
                for t, j, px, py in pairs]
    return sends, arrivals


def _scatter_scratch(n):
    return [pltpu.SemaphoreType.DMA((3 * n,)), pltpu.SemaphoreType.DMA((3 * n,))]


def _chip_scatter(name, css):
    n = len(css)

    def body(*refs):
        sends, arrivals = _scatter_copies(refs[:n], refs[n:2 * n], *refs[2 * n:])
        for cp in sends:
            cp.start()
        for cp in arrivals:
            cp.wait_recv()
        for cp in sends:
            cp.wait_send()

    return _pcall(body, name=f"chip_scatter_{name}", in_specs=[ANY] * n, out_specs=(ANY,) * n,
                  out_shape=tuple(jax.ShapeDtypeStruct(s.shape, s.dtype) for s in css),
                  scratch_shapes=_scatter_scratch(n), compiler_params=_params())(*css)


def _chip_sum(name, land):
    _, half, cols = land.shape
    tr = _tile(half, 512)

    def body(l_ref, o_ref):
        o_ref[...] = ((l_ref[0].astype(F32) + l_ref[1].astype(F32)) + l_ref[2].astype(F32)) + l_ref[3].astype(F32)

    return _pcall(body, name=f"chip_sum_{name}", grid=(half // tr,),
                  in_specs=[pl.BlockSpec((4, tr, cols), lambda i: (0, i, 0))],
                  out_specs=pl.BlockSpec((tr, cols), lambda i: (i, 0)),
                  out_shape=jax.ShapeDtypeStruct((half, cols), F32), compiler_params=_params())(land)


def _pair_join(reds):
    n = len(reds)

    def body(*refs):
        r_refs, out_refs = refs[:n], refs[n:2 * n]
        send_sems, recv_sems = refs[2 * n:]
        x, y, c = _place()
        cps = [_remote(r_refs[t], out_refs[t].at[c], send_sems, recv_sems, t, (x, y, 1 - c)) for t in range(n)]
        for cp in cps:
            cp.start()
        for t in range(n):
            _remote(r_refs[t], out_refs[t].at[1 - c], send_sems, recv_sems, t, (x, y, 1 - c)).wait_recv()
        for cp in cps:
            cp.wait_send()

    return _pcall(body, name="grad_pair_join", in_specs=[ANY] * n, out_specs=(ANY,) * n,
                  out_shape=tuple(jax.ShapeDtypeStruct((2,) + r.shape, F32) for r in reds),
                  scratch_shapes=[pltpu.SemaphoreType.DMA((n,)), pltpu.SemaphoreType.DMA((n,))],
                  compiler_params=_params())(*reds)


def _adamw(name, g, w, m, v):
    rows, cols = g.shape
    tr = _tile(rows, 512)

    def body(g_ref, w_ref, m_ref, v_ref, d_ref, mo_ref, vo_ref):
        gv = g_ref[...]
        mn = ADAM_B1 * m_ref[...] + (1.0 - ADAM_B1) * gv
        vn = ADAM_B2 * v_ref[...] + (1.0 - ADAM_B2) * jnp.square(gv)
        m_hat = mn / (1.0 - ADAM_B1 ** ADAM_STEP)
        v_hat = vn / (1.0 - ADAM_B2 ** ADAM_STEP)
        d_ref[...] = -ADAM_LR * (m_hat / (jnp.sqrt(v_hat) + ADAM_EPS) + ADAM_WD * w_ref[...])
        mo_ref[...] = mn
        vo_ref[...] = vn

    blk = pl.BlockSpec((tr, cols), lambda i: (i, 0))
    shp = jax.ShapeDtypeStruct((rows, cols), F32)
    return _pcall(body, name=f"adamw_{name}", grid=(rows // tr,), in_specs=[blk] * 4, out_specs=(blk,) * 3,
                  out_shape=(shp,) * 3, compiler_params=_params())(g, w, m, v)


def _pack(pieces, rows, dtype):
    flat = jnp.concatenate([p.reshape(-1).astype(dtype) for p in pieces])
    return jnp.pad(flat, (0, rows * PACK_COLS - flat.shape[0])).reshape(rows, PACK_COLS)


def _unpack(pack, shapes):
    flat = pack.reshape(-1)
    out, off = [], 0
    for shp in shapes:
        n = math.prod(shp)
        out.append(flat[off:off + n].reshape(shp))
        off += n
    return out


def _shard_of(full, axis, j):
    n = full.shape[axis] // 4
    return lax.slice_in_dim(full, j * n, (j + 1) * n, axis=axis)


def kernel(x, even_w_in, even_conv_w, even_conv_b, ssd_a_log, ssd_dt_bias, ssd_d, ssd_norm_w, lru_w_a, lru_b_a, lru_w_x, lru_b_x, lru_lambda, even_w_out, odd_w_in, hgrn_lb_logits, hgrn_norm_w, odd_w_out, norm_mix, norm_mlp, mlp_w1, mlp_w2, norm_final, loss_target, m_even_w_in, m_even_conv_w, m_even_conv_b, m_ssd_a_log, m_ssd_dt_bias, m_ssd_d, m_ssd_norm_w, m_lru_w_a, m_lru_b_a, m_lru_w_x, m_lru_b_x, m_lru_lambda, m_even_w_out, m_odd_w_in, m_hgrn_lb_logits, m_hgrn_norm_w, m_odd_w_out, m_norm_mix, m_norm_mlp, m_mlp_w1, m_mlp_w2, m_norm_final, v_even_w_in, v_even_conv_w, v_even_conv_b, v_ssd_a_log, v_ssd_dt_bias, v_ssd_d, v_ssd_norm_w, v_lru_w_a, v_lru_b_a, v_lru_w_x, v_lru_b_x, v_lru_lambda, v_even_w_out, v_odd_w_in, v_hgrn_lb_logits, v_hgrn_norm_w, v_odd_w_out, v_norm_mix, v_norm_mlp, v_mlp_w1, v_mlp_w2, v_norm_final):
    names = [n for n, _, _, _ in WEIGHTS]
    w_loc = dict(zip(names, (even_w_in, even_conv_w, even_conv_b, ssd_a_log, ssd_dt_bias, ssd_d, ssd_norm_w, lru_w_a, lru_b_a, lru_w_x, lru_b_x, lru_lambda, even_w_out, odd_w_in, hgrn_lb_logits, hgrn_norm_w, odd_w_out, norm_mix, norm_mlp, mlp_w1, mlp_w2, norm_final)))
    m_loc = dict(zip(names, (m_even_w_in, m_even_conv_w, m_even_conv_b, m_ssd_a_log, m_ssd_dt_bias, m_ssd_d, m_ssd_norm_w, m_lru_w_a, m_lru_b_a, m_lru_w_x, m_lru_b_x, m_lru_lambda, m_even_w_out, m_odd_w_in, m_hgrn_lb_logits, m_hgrn_norm_w, m_odd_w_out, m_norm_mix, m_norm_mlp, m_mlp_w1, m_mlp_w2, m_norm_final)))
    v_loc = dict(zip(names, (v_even_w_in, v_even_conv_w, v_even_conv_b, v_ssd_a_log, v_ssd_dt_bias, v_ssd_d, v_ssd_norm_w, v_lru_w_a, v_lru_b_a, v_lru_w_x, v_lru_b_x, v_lru_lambda, v_even_w_out, v_odd_w_in, v_hgrn_lb_logits, v_hgrn_norm_w, v_odd_w_out, v_norm_mix, v_norm_mlp, v_mlp_w1, v_mlp_w2, v_norm_final)))
    spec = {n: (blk, full, ax) for n, blk, full, ax in WEIGHTS}

    small = [n for n in names if n not in BIG]
    two_d = lambda n, v: v.reshape(BIG_2D[n])

    me = 2 * lax.axis_index("x") + lax.axis_index("y")
    cc = lax.axis_index("c")
    put = lambda whole, part, k: lax.dynamic_update_slice_in_dim(whole, part[None], k, axis=0)
    own = [two_d(n, w_loc[n]).astype(BF16) for n in BIG] + [_pack([w_loc[n] for n in SMALL_SHARDED], 16, F32)]
    g_in0, g_out0, g_in1, g_out1, g_w1, g_w2, g_small = [put(g, o, me) for g, o in zip(_gather_chips(own), own)]
    w_main0, w_dt0 = _assemble_in0(g_in0)
    w_full = {n: w_loc[n] for n in names if spec[n][2] is None}
    w_full["even_w_out"] = g_out0.reshape(1, 2048, 1024)
    w_full["odd_w_in"] = jnp.concatenate([g_in1[j] for j in range(4)], axis=1)[None]
    w_full["odd_w_out"] = g_out1.reshape(1, 1024, 1024)
    w_full["mlp_w1"] = jnp.stack([jnp.concatenate([g_w1[j, l * 1024:(l + 1) * 1024] for j in range(4)], axis=1) for l in range(2)])
    w_full["mlp_w2"] = jnp.stack([jnp.concatenate([g_w2[j, l * 1024:(l + 1) * 1024] for j in range(4)], axis=0) for l in range(2)])
    shards = [_unpack(g_small[j], [spec[n][0] for n in SMALL_SHARDED]) for j in range(4)]
    for i, n in enumerate(SMALL_SHARDED):
        w_full[n] = jnp.concatenate([shards[j][i] for j in range(4)], axis=spec[n][2])

    cidx = cc.astype(jnp.int32).reshape(1)

    def pair_reduce(tags, tensors):
        return [_pair_add(tag, g, land, cidx) for tag, g, land in zip(tags, tensors, _pair_swap(tags[0], tensors))]

    loss_vec, grad_x, grads, big, (early_sums, early_landed) = _local_step(x, loss_target, w_full, w_main0, w_dt0, pair_reduce)
    loss = lax.psum(loss_vec[0, 0], ("x", "y", "c"))

    def dest_pack(j):
        return _pack([grads[n].reshape(spec[n][1]) if spec[n][2] is None else _shard_of(grads[n].reshape(spec[n][1]), spec[n][2], j)
                      for n in small], SMALL_ROWS, F32)

    late_tags = LATE + ("small",)
    late_sums = pair_reduce(late_tags, [big[n] for n in LATE] + [jnp.stack([dest_pack(j) for j in range(4)])])
    tags = EARLY + late_tags
    chip_sums = list(early_sums) + late_sums
    landed = [put(land, lax.dynamic_index_in_dim(cs, me, axis=0, keepdims=False), me)
              for land, cs in zip(list(early_landed) + list(_chip_scatter("late", late_sums)), chip_sums)]
    halves = [_chip_sum(tag, land) for tag, land in zip(tags, landed)]
    red = {tag: put(r, h, cc).reshape(-1, r.shape[-1]) for tag, r, h in zip(tags, _pair_join(halves), halves)}
    for n in ("mlp_w1", "mlp_w2"):
        red[n] = jnp.concatenate([red[n + "_l0"], red[n + "_l1"]], axis=0)

    outs = {}
    for n, g in ((n, red[n]) for n in BIG):
        res = (g, *_adamw(n, g, two_d(n, w_loc[n]), two_d(n, m_loc[n]), two_d(n, v_loc[n])))
        outs[n] = [r.reshape(spec[n][0]) for r in res]
    blocks = [spec[n][0] for n in small]
    wp, mp, vp = (_pack([src[n] for n in small], SMALL_ROWS, F32) for src in (w_loc, m_loc, v_loc))
    res = (red["small"], *_adamw("small", red["small"], wp, mp, vp))
    unpacked = [_unpack(r, blocks) for r in res]
    for i, n in enumerate(small):
        outs[n] = [u[i] for u in unpacked]
    return (loss, grad_x, *[outs[n][k] for k in range(4) for n in names])
```

```python
import functools
import math

import jax
import jax.numpy as jnp
from jax import lax
from jax.experimental import pallas as pl
from jax.experimental.pallas import tpu as pltpu

F32 = jnp.float32
BF16 = jnp.bfloat16
MXU_DTYPE = jnp.bfloat16
MESH = pl.DeviceIdType.MESH

D_MODEL = 1024
EPS = 1e-6
SSD_HEADS = 16
SSD_HEADDIM = 64
HEAD_SHIFT = 6
SSD_GROUPS = 4
SSD_STATE = 128
SSD_CHUNK = 128
LRU_C = 8.0
LRU_ROWS = 256
HGRN_HEADS = 8
HGRN_HEADDIM = 128
HGRN_SUB = 32
HGRN_SUB_SHIFT = 5
HGRN_BLOCK = 128
HGRN_SCALE = HGRN_HEADDIM ** -0.5
CONV_ROWS = 512

ADAM_LR = 0.001
ADAM_B1 = 0.9
ADAM_B2 = 0.999
ADAM_EPS = 1e-08
ADAM_WD = 0.01
ADAM_STEP = 10

VMEM_LIMIT = 56 * 1024 * 1024
PACK_COLS = 1024
SMALL_ROWS = 288

WEIGHTS = (
    ("even_w_in", (1, 1024, 1288), (1, 1024, 5152), 2),
    ("even_conv_w", (1, 4, 768), (1, 4, 3072), 2),
    ("even_conv_b", (1, 3072), (1, 3072), None),
    ("ssd_a_log", (1, 2, 16), (1, 2, 16), None),
    ("ssd_dt_bias", (1, 2, 16), (1, 2, 16), None),
    ("ssd_d", (1, 16), (1, 16), None),
    ("ssd_norm_w", (1, 1024), (1, 1024), None),
    ("lru_w_a", (1, 2, 16, 64, 64), (1, 2, 16, 64, 64), None),
    ("lru_b_a", (1, 2, 256), (1, 2, 1024), 2),
    ("lru_w_x", (1, 2, 16, 64, 64), (1, 2, 16, 64, 64), None),
    ("lru_b_x", (1, 2, 256), (1, 2, 1024), 2),
    ("lru_lambda", (1, 2, 256), (1, 2, 1024), 2),
    ("even_w_out", (1, 512, 1024), (1, 2048, 1024), 1),
    ("odd_w_in", (1, 1024, 1280), (1, 1024, 5120), 2),
    ("hgrn_lb_logits", (2, 1024), (2, 1024), None),
    ("hgrn_norm_w", (1, 256), (1, 1024), 1),
    ("odd_w_out", (1, 256, 1024), (1, 1024, 1024), 1),
    ("norm_mix", (2, 1024), (2, 1024), None),
    ("norm_mlp", (2, 1024), (2, 1024), None),
    ("mlp_w1", (2, 1024, 1024), (2, 1024, 4096), 2),
    ("mlp_w2", (2, 1024, 1024), (2, 4096, 1024), 1),
    ("norm_final", (1024,), (1024,), None),
)
BIG = ("even_w_in", "even_w_out", "odd_w_in", "odd_w_out", "mlp_w1", "mlp_w2")
BIG_2D = {"even_w_in": (1024, 1288), "even_w_out": (512, 1024), "odd_w_in": (1024, 1280), "odd_w_out": (256, 1024),
          "mlp_w1": (2048, 1024), "mlp_w2": (2048, 1024)}
SMALL_SHARDED = ("even_conv_w", "lru_b_a", "lru_b_x", "lru_lambda", "hgrn_norm_w")


def _pcall(body, carry=None, **kw):
    if carry is not None:
        srcs, shapes, scratch, start, finish = carry
        grid, inner = kw["grid"], body
        as_tuple = lambda v: tuple(v) if isinstance(v, (tuple, list)) else (v,)
        out_specs, out_shape, own_scratch = as_tuple(kw["out_specs"]), as_tuple(kw["out_shape"]), list(kw.get("scratch_shapes", ()))
        a = len(kw["in_specs"])
        b = a + len(srcs)
        c = b + len(out_specs)
        d = c + len(shapes)
        e = d + len(own_scratch)

        def body(*refs):
            ids = [pl.program_id(ax) for ax in range(len(grid))]
            first = functools.reduce(jnp.logical_and, [i == 0 for i in ids])
            last = functools.reduce(jnp.logical_and, [i == g - 1 for i, g in zip(ids, grid)])
            pl.when(first)(lambda: start(refs[a:b], refs[c:d], *refs[e:]))
            inner(*refs[:a], *refs[b:c], *refs[d:e])
            pl.when(last)(lambda: finish(refs[a:b], refs[c:d], *refs[e:]))

        kw = dict(kw, in_specs=list(kw["in_specs"]) + [ANY] * len(srcs), out_specs=out_specs + (ANY,) * len(shapes),
                  out_shape=out_shape + tuple(shapes), scratch_shapes=own_scratch + list(scratch))
    return pl.pallas_call(body, **kw)


def _params(**kw):
    return pltpu.CompilerParams(vmem_limit_bytes=VMEM_LIMIT, **kw)


def _tile(n, pref):
    if n <= pref:
        return n
    t = (pref // 128) * 128
    while n % t:
        t -= 128
    return t


def _dot(a, b, dims=(((1,), (0,)), ((), ()))):
    return lax.dot_general(a, b, dims, preferred_element_type=F32)


_NN = (((1,), (0,)), ((), ()))
_NT = (((1,), (1,)), ((), ()))
_TN = (((0,), (0,)), ((), ()))


def _mx(v):
    return v.astype(MXU_DTYPE)


def _dot01(a, b, dims=_NN, *, split, terms):
    acc, rest = None, (a if split == "a" else b)
    for _ in range(terms):
        piece = _mx(rest)
        part = _dot(piece, _mx(b), dims) if split == "a" else _dot(_mx(a), piece, dims)
        acc = part if acc is None else acc + part
        rest = rest - piece.astype(F32)
    return acc


def _mm(name, a, b, mode, *, out_dtype=F32, res=None, relu2=False, relu2_of=None, col_shards=1, carry=None):
    if mode == "nn":
        (m, kk), (_, n) = a.shape, b.shape
    elif mode == "nt":
        (m, kk), (n, _) = a.shape, b.shape
    else:
        (kk, m), (_, n) = a.shape, b.shape
    assert res is None or relu2_of is None
    tk_pref = 1024
    if mode == "tn" and a.dtype.itemsize == 2 and b.dtype.itemsize == 2:
        tk_pref = 2048
    tm, tn, tk = _tile(m, 1024), _tile(n // col_shards, 1024), _tile(kk, tk_pref)
    nk = kk // tk
    dims = {"nn": _NN, "nt": _NT, "tn": _TN}[mode]
    a_spec = pl.BlockSpec((tk, tm), lambda i, j, k: (k, i)) if mode == "tn" else pl.BlockSpec((tm, tk), lambda i, j, k: (i, k))
    b_spec = pl.BlockSpec((tn, tk), lambda i, j, k: (j, k)) if mode == "nt" else pl.BlockSpec((tk, tn), lambda i, j, k: (k, j))
    o_spec = pl.BlockSpec((tm, tn), lambda i, j, k: (i, j))
    o_shape = (m, n)
    if col_shards > 1:
        assert tn * col_shards == n and res is None and not relu2
        o_spec = pl.BlockSpec((None, tm, tn), lambda i, j, k: (j, i, 0))
        o_shape = (col_shards, m, tn)
    extra = res if res is not None else relu2_of
    has_res = extra is not None

    def body(*refs):
        a_ref, b_ref = refs[0], refs[1]
        res_ref = refs[2] if has_res else None
        outs = refs[2 + has_res:2 + has_res + 1 + relu2]

        def finish(r):
            if res is not None:
                r = r + res_ref[...]
            if relu2_of is not None:
                r = r * (2.0 * jnp.maximum(res_ref[...], 0.0))
            if relu2:
                outs[0][...] = r
                outs[1][...] = jnp.square(jnp.maximum(r, 0.0)).astype(outs[1].dtype)
            else:
                outs[0][...] = r.astype(outs[0].dtype)

        prod = _dot(_mx(a_ref[...]), _mx(b_ref[...]), dims)
        if nk == 1:
            finish(prod)
            return
        acc = refs[-1]
        k = pl.program_id(2)

        @pl.when(k == 0)
        def _():
            acc[...] = prod

        @pl.when(k > 0)
        def _():
            acc[...] += prod

        @pl.when(k == nk - 1)
        def _():
            finish(acc[...])

    in_specs = [a_spec, b_spec] + ([o_spec] if has_res else [])
    if relu2:
        out_shape = (jax.ShapeDtypeStruct((m, n), F32), jax.ShapeDtypeStruct((m, n), BF16))
        out_specs = (o_spec, o_spec)
    else:
        out_shape = jax.ShapeDtypeStruct(o_shape, out_dtype)
        out_specs = o_spec
    args = (a, b) + ((extra,) if has_res else ()) + (tuple(carry[0]) if carry else ())
    return _pcall(body, carry=carry, name=name, grid=(m // tm, n // tn, nk), in_specs=in_specs, out_specs=out_specs,
                  out_shape=out_shape, scratch_shapes=[pltpu.VMEM((tm, tn), F32)] if nk > 1 else [],
                  compiler_params=_params())(*args)


def _mm_sum_nt(name, parts, wblocks):
    m, n, npart = parts[0].shape[0], wblocks[0].shape[0], len(parts)
    tm, tn = _tile(m, 512), _tile(n, 1024)

    def body(*refs):
        acc = _dot(_mx(refs[0][...]), _mx(refs[npart][...]), _NT)
        for k in range(1, npart):
            acc = acc + _dot(_mx(refs[k][...]), _mx(refs[npart + k][...]), _NT)
        refs[-1][...] = acc

    in_specs = [pl.BlockSpec((tm, p.shape[1]), lambda i, j: (i, 0)) for p in parts]
    in_specs += [pl.BlockSpec((tn, w.shape[1]), lambda i, j: (j, 0)) for w in wblocks]
    return _pcall(body, name=name, grid=(m // tm, n // tn), in_specs=in_specs, out_specs=pl.BlockSpec((tm, tn), lambda i, j: (i, j)),
                  out_shape=jax.ShapeDtypeStruct((m, n), F32), compiler_params=_params())(*parts, *wblocks)


def _pw_fwd(name, f, ins, params, out_dtypes, tc, ncol, tm=256, groups=1):
    t = ins[0][0].shape[0]
    tm = min(tm, t)
    ni, npar = len(ins), len(params)
    gw = tc // groups

    def body(*refs):
        for g in range(groups):
            sl = slice(g * gw, (g + 1) * gw)
            vals = f(*[r[:, sl].astype(F32) for r in refs[:ni]], *[r[:, sl] for r in refs[ni:ni + npar]])
            for o, v in zip(refs[ni + npar:], vals):
                o[:, sl] = v.astype(o.dtype)

    in_specs = [pl.BlockSpec((tm, tc), lambda j, i, off=off: (i, off + j)) for _, off in ins]
    in_specs += [pl.BlockSpec((1, tc), lambda j, i, off=off: (0, off + j)) for _, off in params]
    out_specs = tuple(pl.BlockSpec((tm, tc), lambda j, i: (i, j)) for _ in out_dtypes)
    out_shape = tuple(jax.ShapeDtypeStruct((t, ncol * tc), d) for d in out_dtypes)
    return _pcall(body, name=name, grid=(ncol, t // tm), in_specs=in_specs, out_specs=out_specs, out_shape=out_shape,
                  compiler_params=_params())(*[a for a, _ in ins], *[p for p, _ in params])


def _pw_bwd(name, f, ins, params, douts, tc, ncol, want, adds=None, tm=256, out_dtypes=None, groups=1):
    adds = adds or {}
    out_dtypes = out_dtypes or [F32] * len(want)
    t = ins[0][0].shape[0]
    tm = min(tm, t)
    ni, npar, nd, na = len(ins), len(params), len(douts), len(adds)
    add_keys = sorted(adds)
    gw = tc // groups

    def body(*refs):
        in_refs, p_refs = refs[:ni], refs[ni:ni + npar]
        d_refs = refs[ni + npar:ni + npar + nd]
        a_refs = refs[ni + npar + nd:ni + npar + nd + na]
        o_refs = refs[ni + npar + nd + na:]
        for p in range(npar):
            @pl.when(pl.program_id(1) == 0)
            def _(o=o_refs[len(want) + p]):
                o[...] = jnp.zeros_like(o)

        for g in range(groups):
            sl = slice(g * gw, (g + 1) * gw)
            _, vjp = jax.vjp(f, *[r[:, sl].astype(F32) for r in in_refs], *[r[:, sl] for r in p_refs])
            cts = vjp(tuple(d[:, sl].astype(F32) for d in d_refs))
            for o, kidx in zip(o_refs[:len(want)], want):
                v = cts[kidx]
                if kidx in adds:
                    v = v + a_refs[add_keys.index(kidx)][:, sl]
                o[:, sl] = v.astype(o.dtype)
            for p in range(npar):
                o_refs[len(want) + p][:, sl] += cts[ni + p]

    in_specs = [pl.BlockSpec((tm, tc), lambda j, i, off=off: (i, off + j)) for _, off in ins]
    in_specs += [pl.BlockSpec((1, tc), lambda j, i, off=off: (0, off + j)) for _, off in params]
    in_specs += [pl.BlockSpec((tm, tc), lambda j, i: (i, j)) for _ in range(nd + na)]
    out_specs = tuple([pl.BlockSpec((tm, tc), lambda j, i: (i, j)) for _ in want]
                      + [pl.BlockSpec((1, tc), lambda j, i: (0, j)) for _ in params])
    out_shape = tuple([jax.ShapeDtypeStruct((t, ncol * tc), dt) for dt in out_dtypes]
                      + [jax.ShapeDtypeStruct((1, ncol * tc), F32) for _ in params])
    res = _pcall(body, name=name, grid=(ncol, t // tm), in_specs=in_specs, out_specs=out_specs, out_shape=out_shape,
                 compiler_params=_params())(*[a for a, _ in ins], *[p for p, _ in params], *douts, *[adds[k] for k in add_keys])
    return list(res[:len(want)]), list(res[len(want):])


def _rms(x, g):
    return (x * lax.rsqrt(jnp.mean(x * x, axis=-1, keepdims=True) + EPS)) * g


def _f_norm(x, g):
    return (_rms(x, g),)


def _f_softplus(d, b):
    return (jax.nn.softplus(d + b),)


def _f_add3(a, b, c):
    return (a + b + c,)


def _f_ssd_post(yf, yb, xs, z, dskip, nw):
    u = (yf + yb + dskip * xs) * jax.nn.silu(z)
    return (_rms(u, nw),)


def _neg_expm1(v):
    t = jnp.tanh(0.5 * v)
    return -2.0 * t / (1.0 - t)


def _f_lru_gates(pre_a, pre_x, u, ba, bx, lam):
    rg = jax.nn.sigmoid(pre_a + ba)
    ig = jax.nn.sigmoid(pre_x + bx)
    log_a = -LRU_C * rg * jax.nn.softplus(-lam)
    return jnp.exp(log_a), jnp.sqrt(_neg_expm1(2.0 * log_a)) * (ig * u)


def _f_lru_post(hf, hb, gate):
    return ((hf + hb) * jax.nn.gelu(gate),)


def _f_hgrn_pre(fr, l0, l1):
    lb = jax.nn.sigmoid(l1 - l0)
    k = (1.0 - lb) * jax.nn.sigmoid(-fr)
    return k, jnp.log1p(-k)


def _f_hgrn_post(of, ob, gate, nw):
    return (_rms(of + ob, nw) * jax.nn.silu(gate),)


def _loss_head(x, tgt, g, tm=256):
    t, d = x.shape
    tm = min(tm, t)

    def body(x_ref, t_ref, g_ref, dx_ref, dg_ref, loss_ref):
        tv = t_ref[...]

        def lf(xv, gv):
            return 0.5 * jnp.sum(jnp.mean(jnp.square(_rms(xv, gv) - tv), axis=-1))

        val, vjp = jax.vjp(lf, x_ref[...], g_ref[...])
        dx, dg = vjp(jnp.ones((), F32))
        dx_ref[...] = dx

        @pl.when(pl.program_id(0) == 0)
        def _():
            dg_ref[...] = jnp.zeros_like(dg_ref)
            loss_ref[...] = jnp.zeros_like(loss_ref)

        dg_ref[...] += dg
        loss_ref[...] += jnp.full(loss_ref.shape, val, F32)

    row = pl.BlockSpec((tm, d), lambda i: (i, 0))
    vec = pl.BlockSpec((1, d), lambda i: (0, 0))
    return _pcall(body, name="loss_head", grid=(t // tm,), in_specs=[row, row, vec],
                  out_specs=(row, vec, pl.BlockSpec((1, 128), lambda i: (0, 0))),
                  out_shape=(jax.ShapeDtypeStruct((t, d), F32), jax.ShapeDtypeStruct((1, d), F32),
                             jax.ShapeDtypeStruct((1, 128), F32)), compiler_params=_params())(x, tgt, g)


def _shifted(x, d, prev, nxt, first, last):
    r = x.shape[0]
    row = lax.broadcasted_iota(jnp.int32, x.shape, 0)
    if d < 0:
        out = pltpu.roll(x, -d, 0)
        for q in range(-d):
            pv = jnp.where(first, 0.0, prev[8 + d + q:8 + d + q + 1, :])
            out = jnp.where(row == q, pv, out)
        return out
    out = pltpu.roll(x, r - d, 0)
    for q in range(d):
        nv = jnp.where(last, 0.0, nxt[q:q + 1, :])
        out = jnp.where(row == r - d + q, nv, out)
    return out


def _conv_fwd(p3, w, b, col0, ncol, silu, tc=1024):
    nbatch, s, _ = p3.shape
    ts = min(CONV_ROWS, s)
    nblk = s // ts

    def body(x_ref, pv_ref, nx_ref, w_ref, b_ref, o_ref, *act_ref):
        i = pl.program_id(1)
        first, last = i == 0, i == nblk - 1
        x, pv, nx = x_ref[...], pv_ref[...], nx_ref[...]
        wv = w_ref[...]
        out = b_ref[...] + wv[1:2] * x
        out = out + wv[0:1] * _shifted(x, -1, pv, nx, first, last)
        out = out + wv[2:3] * _shifted(x, 1, pv, nx, first, last)
        out = out + wv[3:4] * _shifted(x, 2, pv, nx, first, last)
        o_ref[...] = out
        if silu:
            act_ref[0][...] = jax.nn.silu(out)

    nb8 = s // 8
    cur = pl.BlockSpec((None, ts, tc), lambda n, i, j: (n, i, col0 + j))
    prev = pl.BlockSpec((None, 8, tc), lambda n, i, j: (n, jnp.maximum(i * (ts // 8) - 1, 0), col0 + j))
    nxt = pl.BlockSpec((None, 8, tc), lambda n, i, j: (n, jnp.minimum((i + 1) * (ts // 8), nb8 - 1), col0 + j))
    out = pl.BlockSpec((None, ts, tc), lambda n, i, j: (n, i, j))
    shp = jax.ShapeDtypeStruct((nbatch, s, ncol * tc), F32)
    return _pcall(body, name=f"conv_fwd{col0}", grid=(nbatch, nblk, ncol),
                  in_specs=[cur, prev, nxt, pl.BlockSpec((4, tc), lambda n, i, j: (0, col0 + j)),
                            pl.BlockSpec((1, tc), lambda n, i, j: (0, col0 + j))],
                  out_specs=(out, out) if silu else out, out_shape=(shp, shp) if silu else shp,
                  compiler_params=_params())(p3, p3, p3, w, b)


def _conv_bwd(dc3, p3, w, col, conv3=None):
    nbatch, s, tc = dc3.shape
    ts = min(CONV_ROWS, s)
    nblk = s // ts
    silu = conv3 is not None

    def body(d_ref, dpv_ref, dnx_ref, x_ref, pv_ref, nx_ref, w_ref, *rest):
        n, i = pl.program_id(0), pl.program_id(1)
        first, last = i == 0, i == nblk - 1
        d, dpv, dnx = d_ref[...], dpv_ref[...], dnx_ref[...]
        if silu:
            d, dpv, dnx = [jax.vjp(jax.nn.silu, c_ref[...])[1](t)[0] for c_ref, t in zip(rest[:3], (d, dpv, dnx))]
        dx_ref, dw_ref = rest[3 * silu:]
        x, pv, nx = x_ref[...], pv_ref[...], nx_ref[...]
        wv = w_ref[...]
        dx = wv[1:2] * d
        dx = dx + wv[0:1] * _shifted(d, 1, dpv, dnx, first, last)
        dx = dx + wv[2:3] * _shifted(d, -1, dpv, dnx, first, last)
        dx = dx + wv[3:4] * _shifted(d, -2, dpv, dnx, first, last)
        dx_ref[...] = dx.astype(dx_ref.dtype)

        @pl.when((n == 0) & (i == 0))
        def _():
            dw_ref[...] = jnp.zeros_like(dw_ref)

        dw_ref[0:1, :] += jnp.sum(d * _shifted(x, -1, pv, nx, first, last), axis=0, keepdims=True)
        dw_ref[1:2, :] += jnp.sum(d * x, axis=0, keepdims=True)
        dw_ref[2:3, :] += jnp.sum(d * _shifted(x, 1, pv, nx, first, last), axis=0, keepdims=True)
        dw_ref[3:4, :] += jnp.sum(d * _shifted(x, 2, pv, nx, first, last), axis=0, keepdims=True)
        dw_ref[4:5, :] += jnp.sum(d, axis=0, keepdims=True)

    nb8 = s // 8

    def specs(j):
        cur = pl.BlockSpec((None, ts, tc), lambda n, i: (n, i, j))
        prev = pl.BlockSpec((None, 8, tc), lambda n, i: (n, jnp.maximum(i * (ts // 8) - 1, 0), j))
        nxt = pl.BlockSpec((None, 8, tc), lambda n, i: (n, jnp.minimum((i + 1) * (ts // 8), nb8 - 1), j))
        return [cur, prev, nxt]

    return _pcall(body, name=f"conv_bwd{col}", grid=(nbatch, nblk),
                  in_specs=specs(0) + specs(col) + [pl.BlockSpec((4, tc), lambda n, i: (0, col))] + specs(col) * silu,
                  out_specs=(specs(0)[0], pl.BlockSpec((8, tc), lambda n, i: (0, 0))),
                  out_shape=(jax.ShapeDtypeStruct((nbatch, s, tc), BF16), jax.ShapeDtypeStruct((8, tc), F32)),
                  compiler_params=_params())(dc3, dc3, dc3, p3, p3, p3, w, *([conv3] * 3 * silu))


def _block_scan(coef, inp, reverse):
    r = coef.shape[0]
    row = lax.broadcasted_iota(jnp.int32, coef.shape, 0)
    a, b = coef, inp
    d = 1
    while d < r:
        if reverse:
            keep = row < r - d
            a_sh, b_sh = pltpu.roll(a, r - d, 0), pltpu.roll(b, r - d, 0)
        else:
            keep = row >= d
            a_sh, b_sh = pltpu.roll(a, d, 0), pltpu.roll(b, d, 0)
        b = b + a * jnp.where(keep, b_sh, 0.0)
        a = a * jnp.where(keep, a_sh, 1.0)
        d *= 2
    return a, b


def _lru_scan(a3, b3, reverse):
    nbatch, s, w = a3.shape
    ts = min(LRU_ROWS, s)
    nblk = s // ts
    edge = 0 if reverse else ts - 1

    def body(a_ref, b_ref, h_ref, carry):
        @pl.when(pl.program_id(1) == 0)
        def _():
            carry[...] = jnp.zeros_like(carry)

        ca, hb = _block_scan(a_ref[...], b_ref[...], reverse)
        h = hb + ca * carry[0:1, :]
        h_ref[...] = h
        carry[0:1, :] = h[edge:edge + 1, :]

    blk = pl.BlockSpec((None, ts, w), (lambda n, i: (n, nblk - 1 - i, 0)) if reverse else (lambda n, i: (n, i, 0)))
    return _pcall(body, name=f"lru_scan_r{int(reverse)}", grid=(nbatch, nblk), in_specs=[blk, blk], out_specs=blk,
                  out_shape=jax.ShapeDtypeStruct((nbatch, s, w), F32), scratch_shapes=[pltpu.VMEM((8, w), F32)],
                  compiler_params=_params())(a3, b3)


def _lru_scan_bwd(a3, h3, dh3, reverse):
    nbatch, s, w = a3.shape
    ts = min(LRU_ROWS, s)
    nblk = s // ts
    nb8 = s // 8
    tpb = ts // 8

    def body(a_ref, aa_ref, h_ref, hh_ref, dh_ref, g_ref, da_ref, carry):
        i = pl.program_id(1)

        @pl.when(i == 0)
        def _():
            carry[...] = jnp.zeros_like(carry)

        a, h = a_ref[...], h_ref[...]
        row = lax.broadcasted_iota(jnp.int32, a.shape, 0)
        if reverse:
            a_edge = jnp.where(i == 0, 0.0, aa_ref[7:8, :])
            c = jnp.where(row == 0, a_edge, pltpu.roll(a, 1, 0))
            h_edge = jnp.where(i == nblk - 1, 0.0, hh_ref[0:1, :])
            h_sh = jnp.where(row == ts - 1, h_edge, pltpu.roll(h, ts - 1, 0))
        else:
            a_edge = jnp.where(i == 0, 0.0, aa_ref[0:1, :])
            c = jnp.where(row == ts - 1, a_edge, pltpu.roll(a, ts - 1, 0))
            h_edge = jnp.where(i == nblk - 1, 0.0, hh_ref[7:8, :])
            h_sh = jnp.where(row == 0, h_edge, pltpu.roll(h, 1, 0))
        cc, gb = _block_scan(c, dh_ref[...], not reverse)
        g = gb + cc * carry[0:1, :]
        g_ref[...] = g
        carry[0:1, :] = g[ts - 1:ts, :] if reverse else g[0:1, :]
        da_ref[...] = g * h_sh

    if reverse:
        bi = lambda i: i
    else:
        bi = lambda i: nblk - 1 - i
    blk = pl.BlockSpec((None, ts, w), lambda n, i: (n, bi(i), 0))
    before = pl.BlockSpec((None, 8, w), lambda n, i: (n, jnp.maximum(bi(i) * tpb - 1, 0), 0))
    after = pl.BlockSpec((None, 8, w), lambda n, i: (n, jnp.minimum((bi(i) + 1) * tpb, nb8 - 1), 0))
    a_tile, h_tile = (before, after) if reverse else (after, before)
    return _pcall(body, name=f"lru_scan_bwd_r{int(reverse)}", grid=(nbatch, nblk), in_specs=[blk, a_tile, blk, h_tile, blk],
                  out_specs=(blk, blk),
                  out_shape=(jax.ShapeDtypeStruct((nbatch, s, w), F32), jax.ShapeDtypeStruct((nbatch, s, w), F32)),
                  scratch_shapes=[pltpu.VMEM((8, w), F32)], compiler_params=_params())(a3, a3, h3, h3, dh3)


def _head_expand(lane0):
    return (jnp.right_shift(lax.broadcasted_iota(jnp.int32, (128, 1024), 1), HEAD_SHIFT) + lane0
            == lax.broadcasted_iota(jnp.int32, (128, 1024), 0)).astype(F32)


def _head_reduce(lane0):
    return (jnp.right_shift(lax.broadcasted_iota(jnp.int32, (1024, 128), 0), HEAD_SHIFT) + lane0
            == lax.broadcasted_iota(jnp.int32, (1024, 128), 1)).astype(F32)


def _time_mask(q, reverse):
    ri = lax.broadcasted_iota(jnp.int32, (q, q), 0)
    ci = lax.broadcasted_iota(jnp.int32, (q, q), 1)
    return (ri <= ci) if reverse else (ri >= ci)


def _ssd_common(xs_ref, bc_ref, dt_ref, al_ref, reverse, lane0):
    q = xs_ref.shape[0]
    edge = 0 if reverse else q - 1
    dt = dt_ref[...]
    a = -jnp.exp(al_ref[...])
    mask = _time_mask(q, reverse)
    expand = _head_expand(lane0)
    cum = _dot01(mask.astype(F32), dt * a, split="b", terms=3)
    cum_x = _dot01(cum, expand, split="a", terms=3)
    dt_x = _dot01(dt, expand, split="a", terms=2)
    last_x = cum_x[edge:edge + 1, :]
    xs = xs_ref[...]
    bc = bc_ref[...]
    return dict(q=q, edge=edge, lane0=lane0, dt=dt, a=a, mask=mask, cum_t=cum.T, cum_x=cum_x, dt_x=dt_x, xs=xs,
                v=xs * dt_x, e_c=jnp.exp(cum_x), w=jnp.exp(last_x - cum_x), e_l=jnp.exp(last_x),
                bm=bc[:, :512], cm=bc[:, 512:])


def _ssd_decay(c, h):
    row = c["lane0"] + h
    seg = c["cum_x"][:, h * SSD_HEADDIM:h * SSD_HEADDIM + 1] - c["cum_t"][row:row + 1, :]
    return jnp.where(c["mask"], jnp.exp(jnp.minimum(seg, 0.0)), 0.0)


def _head_masks():
    lane = jnp.right_shift(lax.broadcasted_iota(jnp.int32, (1, 256), 1), HEAD_SHIFT)
    return [lane == e for e in range(4)]


def _ssd_fwd(xbc3, dt3, alog, reverse, carry=None):
    nbatch, s, _ = xbc3.shape
    q = min(SSD_CHUNK, s)
    nc = s // q
    lane0 = SSD_HEADS * int(reverse)

    def body(xs_ref, bc_ref, dt_ref, al_ref, y_ref, st_ref, st):
        @pl.when(pl.program_id(1) == 0)
        def _():
            st[...] = jnp.zeros_like(st)

        st_ref[...] = st[...]
        c = _ssd_common(xs_ref, bc_ref, dt_ref, al_ref, reverse, lane0)
        hm = _head_masks()
        for g in range(SSD_GROUPS):
            sl = slice(g * 256, (g + 1) * 256)
            cg, bg = _mx(c["cm"][:, g * 128:(g + 1) * 128]), _mx(c["bm"][:, g * 128:(g + 1) * 128])
            cb = _dot(cg, bg, _NT)
            vg = c["v"][:, sl]
            s0 = st[:, sl]
            yg = _dot(cg, _mx(s0)) * c["e_c"][:, sl]
            for e in range(4):
                m = _ssd_decay(c, 4 * g + e) * cb
                yg = yg + _dot(_mx(m), _mx(jnp.where(hm[e], vg, 0.0)))
            y_ref[:, sl] = yg
            st[:, sl] = c["e_l"][:, sl] * s0 + _dot(bg, _mx(vg * c["w"][:, sl]), _TN)

    ck = (lambda i: nc - 1 - i) if reverse else (lambda i: i)
    xs_spec = pl.BlockSpec((None, q, 1024), lambda n, i: (n, ck(i), 0))
    bc_spec = pl.BlockSpec((None, q, 1024), lambda n, i: (n, ck(i), 1))
    dt_spec = pl.BlockSpec((None, q, 128), lambda n, i: (n, ck(i), 0))
    al_spec = pl.BlockSpec((1, 128), lambda n, i: (0, 0))
    st_spec = pl.BlockSpec((None, None, 128, 1024), lambda n, i: (n, ck(i), 0, 0))
    return _pcall(body, carry=carry, name=f"ssd_fwd_r{int(reverse)}", grid=(nbatch, nc),
                  in_specs=[xs_spec, bc_spec, dt_spec, al_spec], out_specs=(xs_spec, st_spec),
                  out_shape=(jax.ShapeDtypeStruct((nbatch, s, 1024), F32), jax.ShapeDtypeStruct((nbatch, nc, 128, 1024), F32)),
                  scratch_shapes=[pltpu.VMEM((128, 1024), F32)],
                  compiler_params=_params())(xbc3, xbc3, dt3, alog, *(carry[0] if carry else ()))


def _ssd_bwd(xbc3, dt3, alog, st4, dy3, reverse, add_to=(), scatter=()):
    nbatch, s, _ = xbc3.shape
    q = min(SSD_CHUNK, s)
    nc = s // q
    lane0 = SSD_HEADS * int(reverse)
    nadd, ns = len(add_to), len(scatter)

    def body(xs_ref, bc_ref, dt_ref, al_ref, st0_ref, dy_ref, *rest):
        adds, srcs, rest = rest[:nadd], rest[nadd:nadd + ns], rest[nadd + ns:]
        (dxs_ref, dbc_ref, ddt_ref, dal_ref), lands, dst = rest[:4], rest[4:4 + ns], rest[4 + ns]
        n, i = pl.program_id(0), pl.program_id(1)
        if ns:
            sends, arrivals = _scatter_copies(srcs, lands, *rest[5 + ns:])

            @pl.when((n == 0) & (i == 0))
            def _():
                for cp in sends:
                    cp.start()

        @pl.when(i == 0)
        def _():
            dst[...] = jnp.zeros_like(dst)

        @pl.when((i == 0) & (n == 0))
        def _():
            dal_ref[...] = jnp.zeros_like(dal_ref)

        c = _ssd_common(xs_ref, bc_ref, dt_ref, al_ref, reverse, lane0)
        hm = _head_masks()
        reduce_m = _head_reduce(lane0)
        s0_all, ds1_all, dy = st0_ref[...], dst[...], dy_ref[...]
        lane = lax.broadcasted_iota(jnp.int32, (q, 128), 1)
        sub = lax.broadcasted_iota(jnp.int32, (128, q), 0)
        rowacc = jnp.zeros((q, 128), F32)
        colacc_t = jnp.zeros((128, q), F32)
        dv_l, yst_l, dvbar_l, dk_l, dc_l = [], [], [], [], []
        for g in range(SSD_GROUPS):
            sl = slice(g * 256, (g + 1) * 256)
            cg, bg = _mx(c["cm"][:, g * 128:(g + 1) * 128]), _mx(c["bm"][:, g * 128:(g + 1) * 128])
            cb = _dot(cg, bg, _NT)
            vg, dyg, wg, ecg = c["v"][:, sl], dy[:, sl], c["w"][:, sl], c["e_c"][:, sl]
            s0, ds1 = _mx(s0_all[:, sl]), _mx(ds1_all[:, sl])
            dye = _mx(dyg * ecg)
            yst_l.append(_dot(cg, s0) * ecg)
            dcg = _dot(dye, s0, _NT)
            dst[:, sl] = c["e_l"][:, sl] * ds1_all[:, sl] + _dot(cg, dye, _TN)
            vbar = _mx(vg * wg)
            dvbar = _dot(bg, ds1)
            dvbar_l.append(dvbar)
            dvg = dvbar * wg
            dkg = _dot(vbar, ds1, _NT)
            for e in range(4):
                h = 4 * g + e
                m = _ssd_decay(c, h)
                dyh, vh = _mx(jnp.where(hm[e], dyg, 0.0)), _mx(jnp.where(hm[e], vg, 0.0))
                dvg = dvg + _dot(_mx(m * cb), dyh, _TN)
                dcb = _dot(dyh, vh, _NT) * m
                dcbb = _mx(dcb)
                dcg = dcg + _dot(dcbb, bg)
                dkg = dkg + _dot(dcbb, cg, _TN)
                wmat = dcb * cb
                rowacc = jnp.where(lane == lane0 + h, jnp.sum(wmat, axis=1, keepdims=True), rowacc)
                colacc_t = jnp.where(sub == lane0 + h, jnp.sum(wmat, axis=0, keepdims=True), colacc_t)
            dv_l.append(dvg)
            dk_l.append(dkg)
            dc_l.append(dcg)
        dv = jnp.concatenate(dv_l, axis=1)
        yst = jnp.concatenate(yst_l, axis=1)
        dvbar = jnp.concatenate(dvbar_l, axis=1)
        t1 = _dot01(dy * yst, reduce_m, split="a", terms=3)
        t2 = _dot01(c["v"] * c["w"] * dvbar, reduce_m, split="a", terms=3)
        dlast = jnp.sum(t2, axis=0, keepdims=True) + _dot01(
            c["e_l"] * jnp.sum(ds1_all * s0_all, axis=0, keepdims=True), reduce_m, split="a", terms=2)
        dcum = rowacc - colacc_t.T + t1 - t2
        dcum = dcum + jnp.where(lax.broadcasted_iota(jnp.int32, (q, 128), 0) == c["edge"], dlast, 0.0)
        dda = _dot01(c["mask"].astype(F32), dcum, _TN, split="b", terms=3)
        ddt = dda * c["a"] + _dot01(dv * c["xs"], reduce_m, split="a", terms=2)
        dal_ref[...] += jnp.sum(dda * c["dt"], axis=0, keepdims=True) * c["a"]
        dxs = dv * c["dt_x"]
        dbc = jnp.concatenate(dk_l + dc_l, axis=1)
        if nadd:
            for a_ref in adds[:-2]:
                dxs = dxs + a_ref[...]
            dbc = dbc + adds[-2][...]
            ddt = ddt + adds[-1][...]
        ddt_ref[...] = ddt
        dxs_ref[...] = dxs
        dbc_ref[...] = dbc
        if ns:
            @pl.when((n == nbatch - 1) & (i == nc - 1))
            def _():
                for cp in arrivals:
                    cp.wait_recv()
                for cp in sends:
                    cp.wait_send()

    ck = (lambda i: i) if reverse else (lambda i: nc - 1 - i)
    xs_spec = pl.BlockSpec((None, q, 1024), lambda n, i: (n, ck(i), 0))
    bc_spec = pl.BlockSpec((None, q, 1024), lambda n, i: (n, ck(i), 1))
    dt_spec = pl.BlockSpec((None, q, 128), lambda n, i: (n, ck(i), 0))
    al_spec = pl.BlockSpec((1, 128), lambda n, i: (0, 0))
    st_spec = pl.BlockSpec((None, None, 128, 1024), lambda n, i: (n, ck(i), 0, 0))
    return _pcall(body, name=f"ssd_bwd_r{int(reverse)}", grid=(nbatch, nc),
                  in_specs=([xs_spec, bc_spec, dt_spec, al_spec, st_spec, xs_spec] + [xs_spec] * (nadd - 1)
                            + [dt_spec] * bool(nadd) + [ANY] * ns),
                  out_specs=(xs_spec, xs_spec, dt_spec, al_spec) + (ANY,) * ns,
                  out_shape=(jax.ShapeDtypeStruct((nbatch, s, 1024), F32), jax.ShapeDtypeStruct((nbatch, s, 1024), F32),
                             jax.ShapeDtypeStruct((nbatch, s, 128), F32), jax.ShapeDtypeStruct((1, 128), F32))
                  + tuple(jax.ShapeDtypeStruct(c.shape, c.dtype) for c in scatter),
                  scratch_shapes=[pltpu.VMEM((128, 1024), F32)] + (_scatter_scratch(ns) if ns else []),
                  compiler_params=_params())(xbc3, xbc3, dt3, alog, st4, dy3, *add_to, *scatter)


def _gla_block(q, k, g, reverse):
    bq = g.shape[0]
    nsub = bq // HGRN_SUB
    edge = 0 if reverse else bq - 1
    ri = lax.broadcasted_iota(jnp.int32, (bq, bq), 0)
    ci = lax.broadcasted_iota(jnp.int32, (bq, bq), 1)
    rb, cb = jnp.right_shift(ri, HGRN_SUB_SHIFT), jnp.right_shift(ci, HGRN_SUB_SHIFT)
    mask = (ri <= ci) if reverse else (ri >= ci)
    m_within = (mask & (rb == cb)).astype(F32)
    m_before = ((cb > rb) if reverse else (cb < rb)).astype(F32)
    bl = _dot01(m_within, g, split="b", terms=3)
    c = _dot01(m_before, g, split="b", terms=3)
    last = c[edge:edge + 1, :] + bl[edge:edge + 1, :]
    ebl, enbl, ec, elc = jnp.exp(bl), jnp.exp(-bl), jnp.exp(c), jnp.exp(last - c)
    qh = q * HGRN_SCALE * ebl
    kh = k * enbl
    blk = jnp.right_shift(lax.broadcasted_iota(jnp.int32, (bq, 1), 0), HGRN_SUB_SHIFT)
    scale = []
    for i in range(nsub):
        valid = (blk >= i) if reverse else (blk <= i)
        ex = jnp.where(valid, c[i * HGRN_SUB:i * HGRN_SUB + 1, :] - c, 0.0)
        scale.append(jnp.where(valid, jnp.exp(ex), 0.0))
    return dict(bq=bq, nsub=nsub, edge=edge, mask=mask, m_within=m_within, m_before=m_before, ebl=ebl, enbl=enbl, ec=ec,
                elc=elc, e_l=jnp.exp(last), qh=qh, qt=qh * ec, kh=kh, kb=kh * elc, scale=scale)


def _gla_scores(c, hs):
    keys = [_mx(c["kh"][:, hs] * c["scale"][i][:, hs]) for i in range(c["nsub"])]
    rows = [_dot(_mx(c["qh"][i * HGRN_SUB:(i + 1) * HGRN_SUB, hs]), keys[i], _NT) for i in range(c["nsub"])]
    return jnp.where(c["mask"], jnp.concatenate(rows, axis=0), 0.0), keys


def _gla_specs(nbatch, s, w, reverse_order):
    bq = min(HGRN_BLOCK, s)
    nblk = s // bq
    bi = (lambda i: nblk - 1 - i) if reverse_order else (lambda i: i)
    col = lambda cb: pl.BlockSpec((nbatch, bq, w), lambda i: (0, bi(i), cb))
    st_spec = pl.BlockSpec((nbatch, None, 128, w), lambda i: (0, bi(i), 0, 0))
    return bq, nblk, col, st_spec


def _gla_fwd(proj3, l0, l1, reverse):
    nbatch, s, w5 = proj3.shape
    w = w5 // 5
    bq, nblk, col, st_spec = _gla_specs(nbatch, s, w, reverse)
    vec = pl.BlockSpec((1, w), lambda i: (0, 0))

    def body(q_ref, f_ref, v_ref, l0_ref, l1_ref, o_ref, st_ref, st):
        @pl.when(pl.program_id(0) == 0)
        def _():
            st[...] = jnp.zeros_like(st)

        for b in range(nbatch):
            st_ref[b] = st[b]
            k, g = _f_hgrn_pre(f_ref[b], l0_ref[...], l1_ref[...])
            c = _gla_block(q_ref[b], k, g, reverse)
            v = v_ref[b]
            for h in range(HGRN_HEADS):
                hs = slice(h * 128, (h + 1) * 128)
                att, _ = _gla_scores(c, hs)
                vb = _mx(v[:, hs])
                s0 = st[b, :, hs]
                o_ref[b, :, hs] = _dot(_mx(att), vb) + _dot(_mx(c["qt"][:, hs]), _mx(s0), _NT)
                st[b, :, hs] = s0 * c["e_l"][:, hs] + _dot(vb, _mx(c["kb"][:, hs]), _TN)

    return _pcall(body, name=f"gla_fwd_r{int(reverse)}", grid=(nblk,),
                  in_specs=[col(0), col(1 + int(reverse)), col(3), vec, vec], out_specs=(col(0), st_spec),
                  out_shape=(jax.ShapeDtypeStruct((nbatch, s, w), F32), jax.ShapeDtypeStruct((nbatch, nblk, 128, w), F32)),
                  scratch_shapes=[pltpu.VMEM((nbatch, 128, w), F32)], compiler_params=_params())(proj3, proj3, proj3, l0, l1)


def _gla_bwd(proj3, l0, l1, st4, do3, reverse, add_to=None):
    nbatch, s, w5 = proj3.shape
    w = w5 // 5
    bq, nblk, col, st_spec = _gla_specs(nbatch, s, w, not reverse)
    nadd = 0 if add_to is None else 2
    vec = pl.BlockSpec((1, w), lambda i: (0, 0))

    def body(q_ref, f_ref, v_ref, l0_ref, l1_ref, st_ref, do_ref, *rest):
        adds, (dq_ref, df_ref, dv_ref, dl0_ref, dl1_ref, dst) = rest[:nadd], rest[nadd:]

        @pl.when(pl.program_id(0) == 0)
        def _():
            dst[...] = jnp.zeros_like(dst)
            dl0_ref[...] = jnp.zeros_like(dl0_ref)
            dl1_ref[...] = jnp.zeros_like(dl1_ref)

        row = lax.broadcasted_iota(jnp.int32, (bq, 128), 0)
        for b in range(nbatch):
            (k, g), pre_vjp = jax.vjp(_f_hgrn_pre, f_ref[b], l0_ref[...], l1_ref[...])
            c = _gla_block(q_ref[b], k, g, reverse)
            s0_all, ds1_all = st_ref[b], dst[b]
            v, dy = v_ref[b], do_ref[b]
            dbl_l, dc_l, dk_l = [], [], []
            for h in range(HGRN_HEADS):
                hs = slice(h * 128, (h + 1) * 128)
                att, keys = _gla_scores(c, hs)
                qh, qt, kh, kb = c["qh"][:, hs], c["qt"][:, hs], c["kh"][:, hs], c["kb"][:, hs]
                vb, dyb = _mx(v[:, hs]), _mx(dy[:, hs])
                s0, ds1 = s0_all[:, hs], ds1_all[:, hs]
                datt = _mx(jnp.where(c["mask"], _dot(dyb, vb, _NT), 0.0))
                dqh_rows = []
                dkh = jnp.zeros((bq, 128), F32)
                dc = jnp.zeros((bq, 128), F32)
                for i in range(c["nsub"]):
                    rs = slice(i * HGRN_SUB, (i + 1) * HGRN_SUB)
                    dqh_rows.append(_dot(datt[rs], keys[i]))
                    dki = _dot(datt[rs], _mx(qh[rs]), _TN)
                    sc = c["scale"][i][:, hs]
                    dkh = dkh + dki * sc
                    dex = dki * (kh * sc)
                    dc = dc - dex + jnp.where(row == i * HGRN_SUB, jnp.sum(dex, axis=0, keepdims=True), 0.0)
                dqt = _dot(dyb, _mx(s0))
                dkb = _dot(vb, _mx(ds1))
                dv = _dot(_mx(att), dyb, _TN) + _dot(_mx(kb), _mx(ds1), _NT)
                dst[b, :, hs] = c["e_l"][:, hs] * ds1 + _dot(dyb, _mx(qt), _TN)
                dqh = jnp.concatenate(dqh_rows, axis=0) + dqt * c["ec"][:, hs]
                dkh = dkh + dkb * c["elc"][:, hs]
                kbk = dkb * kb
                dlast = jnp.sum(kbk, axis=0, keepdims=True) + c["e_l"][:, hs] * jnp.sum(ds1 * s0, axis=0, keepdims=True)
                at_edge = jnp.where(row == c["edge"], dlast, 0.0)
                dc_l.append(dc + dqt * qt - kbk + at_edge)
                dbl_l.append(dqh * qh - dkh * kh + at_edge)
                dq = dqh * c["ebl"][:, hs] * HGRN_SCALE
                if nadd:
                    dq, dv = dq + adds[0][b, :, hs], dv + adds[1][b, :, hs]
                dq_ref[b, :, hs] = dq.astype(dq_ref.dtype)
                dv_ref[b, :, hs] = dv.astype(dv_ref.dtype)
                dk_l.append(dkh * c["enbl"][:, hs])
            dg = (_dot01(c["m_within"], jnp.concatenate(dbl_l, axis=1), _TN, split="b", terms=2)
                  + _dot01(c["m_before"], jnp.concatenate(dc_l, axis=1), _TN, split="b", terms=2))
            df, d0, d1 = pre_vjp((jnp.concatenate(dk_l, axis=1), dg))
            df_ref[b] = df.astype(df_ref.dtype)
            dl0_ref[...] += d0
            dl1_ref[...] += d1

    shp_sum = jax.ShapeDtypeStruct((nbatch, s, w), BF16 if nadd else F32)
    shp_vec = jax.ShapeDtypeStruct((1, w), F32)
    return _pcall(body, name=f"gla_bwd_r{int(reverse)}", grid=(nblk,),
                  in_specs=[col(0), col(1 + int(reverse)), col(3), vec, vec, st_spec, col(0)] + [col(0)] * nadd,
                  out_specs=(col(0), col(0), col(0), vec, vec),
                  out_shape=(shp_sum, jax.ShapeDtypeStruct((nbatch, s, w), BF16), shp_sum, shp_vec, shp_vec),
                  scratch_shapes=[pltpu.VMEM((nbatch, 128, w), F32)],
                  compiler_params=_params())(proj3, proj3, proj3, l0, l1, st4, do3, *(add_to or ()))


DIRS = (False, True)


def _block_diag(w):
    eye = jnp.eye(16, dtype=w.dtype)
    return (eye[:, None, :, None] * w[:, :, None, :]).reshape(1024, 1024)


def _diag_blocks(m):
    m4 = m.reshape(16, 64, 16, 64)
    return jnp.stack([m4[i, :, i, :] for i in range(16)], axis=0)


def _pad_lanes(v, n=128):
    return jnp.pad(v, [(0, 0)] * (v.ndim - 1) + [(0, n - v.shape[-1])])


def _mlp_fwd(tag, x, nw, w1, w2, carry=None):
    (h,) = _pw_fwd(f"{tag}_norm", _f_norm, [(x, 0)], [(nw, 0)], [BF16], 1024, 1)
    a, r, *got = _mm(f"{tag}_up", h, w1, "nn", relu2=True, carry=carry)
    return _mm(f"{tag}_down", r, w2, "nn", res=x), (h, a, r), got


def _mlp_bwd(tag, x, nw, w1, w2, saved, dxo):
    h, a, r = saved
    dw2 = _mm(f"{tag}_dw2", r, dxo, "tn")
    da = _mm(f"{tag}_da", dxo, w2, "nt", relu2_of=a, out_dtype=BF16)
    dw1 = _mm(f"{tag}_dw1", h, da, "tn", col_shards=4)
    dh = _mm(f"{tag}_dh", da, w1, "nt")
    (dx,), (dnw,) = _pw_bwd(f"{tag}_dnorm", _f_norm, [(x, 0)], [(nw, 0)], [dh], 1024, 1, [0], adds={0: dxo})
    return dx, dw1, dw2, dnw


def _split_in0(pieces, dt_piece):
    tm = 256

    def body(p0, p1, p2, p3, p4, p5, o_ref):
        full = jnp.concatenate([p0[...], p1[...], p2[...], p3[...], p4[...], p5[:, :32]], axis=1)
        for j in range(4):
            o_ref[j] = full[:, 1288 * j:1288 * (j + 1)]

    blk = pl.BlockSpec((tm, 1024), lambda i: (i, 0))
    return _pcall(body, name="split_in0", grid=(1024 // tm,), in_specs=[blk] * 5 + [pl.BlockSpec((tm, 128), lambda i: (i, 0))],
                  out_specs=pl.BlockSpec((4, tm, 1288), lambda i: (0, i, 0)),
                  out_shape=jax.ShapeDtypeStruct((4, 1024, 1288), F32), compiler_params=_params())(*pieces, dt_piece)


def _assemble_in0(shards):
    tm = 256

    def body(s_ref, m_ref, d_ref):
        full = jnp.concatenate([s_ref[j] for j in range(4)], axis=1)
        m_ref[...] = full[:, :5120]
        d_ref[...] = jnp.concatenate([full[:, 5120:5152], jnp.zeros((tm, 96), full.dtype)], axis=1)

    return _pcall(body, name="assemble_in0", grid=(1024 // tm,), in_specs=[pl.BlockSpec((4, tm, 1288), lambda i: (0, i, 0))],
                  out_specs=(pl.BlockSpec((tm, 5120), lambda i: (i, 0)), pl.BlockSpec((tm, 128), lambda i: (i, 0))),
                  out_shape=(jax.ShapeDtypeStruct((1024, 5120), shards.dtype), jax.ShapeDtypeStruct((1024, 128), shards.dtype)),
                  compiler_params=_params())(shards)


EARLY = ("odd_w_in", "odd_w_out", "mlp_w1_l1", "mlp_w2_l1")
MID = ("even_w_out", "mlp_w1_l0", "mlp_w2_l0")
LATE = ("even_w_in",)


def _local_step(x3, tgt3, w, w_main0, w_dt0, pair_reduce=None, late=None):
    nb, s, d = x3.shape
    carries, arrived = late if late else ({}, None)
    t = nb * s
    x0 = x3.reshape(t, d)
    tgt = tgt3.reshape(t, d)
    grads = {}
    row = lambda v: v.reshape(1, -1)
    to3 = lambda v: v.reshape(nb, s, v.shape[-1])
    to2 = lambda v: v.reshape(-1, v.shape[-1])

    conv_w, conv_b = w["even_conv_w"][0], row(w["even_conv_b"][0])
    nmix0 = row(w["norm_mix"][0])
    (h0,) = _pw_fwd("l0_norm", _f_norm, [(x0, 0)], [(nmix0, 0)], [BF16], 1024, 1)
    proj0 = _mm("l0_proj", h0, w_main0, "nn")
    dt_raw = _mm("l0_proj_dt", h0, w_dt0, "nn")
    conv2, xbc3 = _conv_fwd(to3(proj0), conv_w, conv_b, 0, 2, True)
    u_lru = to2(_conv_fwd(to3(proj0), conv_w, conv_b, 2, 1, False))
    xbc = to2(xbc3)
    dt_bias = _pad_lanes(w["ssd_dt_bias"][0].reshape(1, 32))
    (dt,) = _pw_fwd("l0_dt", _f_softplus, [(dt_raw, 0)], [(dt_bias, 0)], [F32], 128, 1)
    dt3 = to3(dt)
    alog = _pad_lanes(w["ssd_a_log"][0].reshape(1, 32))
    ssd = [_ssd_fwd(xbc3, dt3, alog, r, carry=carries.get(key)) for r, key in zip(DIRS, ("mlp_w1", "mlp_w2"))]
    if late:
        w = {**w, **arrived("mlp_w1", ssd[0][2:]), **arrived("mlp_w2", ssd[1][2:])}
    yf, yb = to2(ssd[0][0]), to2(ssd[1][0])
    dskip = jnp.repeat(w["ssd_d"][0], SSD_HEADDIM).reshape(1, 1024)
    snw = row(w["ssd_norm_w"][0])
    ssd_ins = [(yf, 0), (yb, 0), (xbc, 0), (proj0, 3)]
    (ya,) = _pw_fwd("l0_ssd_post", _f_ssd_post, ssd_ins, [(dskip, 0), (snw, 0)], [BF16], 1024, 1, groups=SSD_GROUPS)
    w_gates = [_block_diag(w[k][0, r]).astype(MXU_DTYPE) for r in range(2) for k in ("lru_w_a", "lru_w_x")]
    pre = [_mm(f"l0_lru_pre{i}", u_lru, wg, "nn") for i, wg in enumerate(w_gates)]
    lru_par = [[(row(w[k][0, r]), 0) for k in ("lru_b_a", "lru_b_x", "lru_lambda")] for r in range(2)]
    lru_ins = [[(pre[2 * r], 0), (pre[2 * r + 1], 0), (u_lru, 0)] for r in range(2)]
    ab = [_pw_fwd(f"l0_lru_gates{r}", _f_lru_gates, lru_ins[r], lru_par[r], [F32, F32], 1024, 1) for r in range(2)]
    hs = [_lru_scan(to3(ab[r][0]), to3(ab[r][1]), DIRS[r]) for r in range(2)]
    lru_post_ins = [(to2(hs[0]), 0), (to2(hs[1]), 0), (proj0, 4)]
    (ybm,) = _pw_fwd("l0_lru_post", _f_lru_post, lru_post_ins, [], [BF16], 1024, 1)
    w_out0 = w["even_w_out"][0]
    x1 = _mm("l0_out_a", ya, w_out0[:1024], "nn", res=x0)
    x1 = _mm("l0_out_b", ybm, w_out0[1024:], "nn", res=x1)
    nmlp0 = row(w["norm_mlp"][0])
    x2, mlp0, got = _mlp_fwd("l0_mlp", x1, nmlp0, w["mlp_w1"][0], w["mlp_w2"][0], carry=carries.get("odd"))
    if late:
        w = {**w, **arrived("odd", got)}

    w_in1 = w["odd_w_in"][0]
    nmix1 = row(w["norm_mix"][1])
    (h1,) = _pw_fwd("l1_norm", _f_norm, [(x2, 0)], [(nmix1, 0)], [BF16], 1024, 1)
    proj1 = _mm("l1_proj", h1, w_in1, "nn")
    proj1_3 = to3(proj1)
    lb0, lb1 = row(w["hgrn_lb_logits"][0]), row(w["hgrn_lb_logits"][1])
    gla = [_gla_fwd(proj1_3, lb0, lb1, r) for r in DIRS]
    hnw = row(w["hgrn_norm_w"][0])
    hpost_ins = [(to2(gla[0][0]), 0), (to2(gla[1][0]), 0), (proj1, 4)]
    (yo,) = _pw_fwd("l1_hgrn_post", _f_hgrn_post, hpost_ins, [(hnw, 0)], [BF16], 1024, 1, groups=HGRN_HEADS)
    w_out1 = w["odd_w_out"][0]
    x3_ = _mm("l1_out", yo, w_out1, "nn", res=x2)
    nmlp1 = row(w["norm_mlp"][1])
    x4, mlp1, _ = _mlp_fwd("l1_mlp", x3_, nmlp1, w["mlp_w1"][1], w["mlp_w2"][1])

    dx4, dnf, loss = _loss_head(x4, tgt, row(w["norm_final"]))
    grads["norm_final"] = dnf.reshape(-1)

    dx3, dw1_1, dw2_1, dnmlp1 = _mlp_bwd("l1_mlp", x3_, nmlp1, w["mlp_w1"][1], w["mlp_w2"][1], mlp1, dx4)
    big = {"odd_w_out": _mm("l1_dwout", yo, dx3, "tn").reshape(4, 256, 1024)}
    dyo = _mm("l1_dyo", dx3, w_out1, "nt")
    (do, dgate1), (dhnw,) = _pw_bwd("l1_hgrn_post_b", _f_hgrn_post, hpost_ins, [(hnw, 0)], [dyo], 1024, 1, [0, 2],
                                    out_dtypes=[F32, BF16], groups=HGRN_HEADS)
    grads["hgrn_norm_w"] = dhnw
    do3 = to3(do)
    gb = [_gla_bwd(proj1_3, lb0, lb1, gla[0][1], do3, False)]
    gb.append(_gla_bwd(proj1_3, lb0, lb1, gla[1][1], do3, True, add_to=(gb[0][0], gb[0][2])))
    grads["hgrn_lb_logits"] = jnp.concatenate([gb[0][3] + gb[1][3], gb[0][4] + gb[1][4]], axis=0)
    dparts1 = [to2(gb[1][0]), to2(gb[0][1]), to2(gb[1][1]), to2(gb[1][2]), dgate1]
    dwin1 = jnp.concatenate([_mm(f"l1_dwin{i}", h1, dp, "tn") for i, dp in enumerate(dparts1)], axis=1)
    big["odd_w_in"] = dwin1.reshape(1024, 4, 1280).transpose(1, 0, 2)
    dh1 = _mm_sum_nt("l1_dh", dparts1, [w_in1[:, i * 1024:(i + 1) * 1024] for i in range(5)])
    (dx2,), (dnmix1,) = _pw_bwd("l1_dnorm", _f_norm, [(x2, 0)], [(nmix1, 0)], [dh1], 1024, 1, [0], adds={0: dx3})
    big["mlp_w1_l1"], big["mlp_w2_l1"] = dw1_1, dw2_1.reshape(4, 1024, 1024)
    early_sums = tuple(pair_reduce(EARLY, [big[n] for n in EARLY])) if pair_reduce else ()

    dx1, dw1_0, dw2_0, dnmlp0 = _mlp_bwd("l0_mlp", x1, nmlp0, w["mlp_w1"][0], w["mlp_w2"][0], mlp0, dx2)
    big["mlp_w1_l0"], big["mlp_w2_l0"] = dw1_0, dw2_0.reshape(4, 1024, 1024)
    grads["norm_mlp"] = jnp.concatenate([dnmlp0, dnmlp1], axis=0)
    big["even_w_out"] = jnp.concatenate([_mm("l0_dwout_a", ya, dx1, "tn"), _mm("l0_dwout_b", ybm, dx1, "tn")],
                                        axis=0).reshape(4, 512, 1024)
    mid_sums = tuple(pair_reduce(MID, [big[n] for n in MID])) if pair_reduce else ()
    dya = _mm("l0_dya", dx1, w_out0[:1024], "nt")
    dyb = _mm("l0_dyb", dx1, w_out0[1024:], "nt")
    (dh, dgate0), _ = _pw_bwd("l0_lru_post_b", _f_lru_post, lru_post_ins, [], [dyb], 1024, 1, [0, 2], out_dtypes=[F32, BF16])
    dh3 = to3(dh)
    dpre, du_parts, dlru = [], [], {k: [] for k in ("lru_b_a", "lru_b_x", "lru_lambda")}
    for r in range(2):
        g_r, da_r = _lru_scan_bwd(to3(ab[r][0]), hs[r], dh3, DIRS[r])
        (dpa, dpx, du_r), (dba, dbx, dlam) = _pw_bwd(f"l0_lru_gates_b{r}", _f_lru_gates, lru_ins[r], lru_par[r],
                                                     [to2(da_r), to2(g_r)], 1024, 1, [0, 1, 2],
                                                     out_dtypes=[BF16, BF16, F32])
        dpre += [dpa, dpx]
        du_parts.append(du_r)
        dlru["lru_b_a"].append(dba)
        dlru["lru_b_x"].append(dbx)
        dlru["lru_lambda"].append(dlam)
    for k, v in dlru.items():
        grads[k] = jnp.concatenate(v, axis=0)[None]
    dwg = [_diag_blocks(_mm(f"l0_dwgate{i}", u_lru, dp, "tn")) for i, dp in enumerate(dpre)]
    grads["lru_w_a"] = jnp.stack([dwg[0], dwg[2]])[None]
    grads["lru_w_x"] = jnp.stack([dwg[1], dwg[3]])[None]
    du_gate = _mm_sum_nt("l0_du_gate", dpre, w_gates)
    (du,) = _pw_fwd("l0_du", _f_add3, [(du_parts[0], 0), (du_parts[1], 0), (du_gate, 0)], [], [F32], 1024, 1)
    (dy, dxs_skip, dz), (ddskip, dsnw) = _pw_bwd("l0_ssd_post_b", _f_ssd_post, ssd_ins, [(dskip, 0), (snw, 0)], [dya],
                                                 1024, 1, [0, 2, 3], out_dtypes=[F32, F32, BF16], groups=SSD_GROUPS)
    grads["ssd_d"] = ddskip.reshape(SSD_HEADS, SSD_HEADDIM).sum(axis=1)[None]
    grads["ssd_norm_w"] = dsnw
    dy3 = to3(dy)
    sb0 = _ssd_bwd(xbc3, dt3, alog, ssd[0][1], dy3, False, scatter=early_sums)
    sb1 = _ssd_bwd(xbc3, dt3, alog, ssd[1][1], dy3, True, add_to=(sb0[0], to3(dxs_skip), sb0[1], sb0[2]), scatter=mid_sums)
    grads["ssd_a_log"] = (sb0[3] + sb1[3])[:, :32].reshape(1, 2, 16)
    ddt = to2(sb1[2])
    (ddt_raw,), (ddtb,) = _pw_bwd("l0_dt_b", _f_softplus, [(dt_raw, 0)], [(dt_bias, 0)], [ddt], 128, 1, [0])
    grads["ssd_dt_bias"] = ddtb[:, :32].reshape(1, 2, 16)
    cb = [_conv_bwd(sb1[0], to3(proj0), conv_w, 0, conv2), _conv_bwd(sb1[1], to3(proj0), conv_w, 1, conv2),
          _conv_bwd(to3(du), to3(proj0), conv_w, 2)]
    dcw = jnp.concatenate([c_[1] for c_ in cb], axis=1)
    grads["even_conv_w"] = dcw[:4][None]
    grads["even_conv_b"] = dcw[4:5]
    dparts0 = [to2(c_[0]) for c_ in cb] + [dz, dgate0]
    dwin0 = [_mm(f"l0_dwin{i}", h0, dp, "tn") for i, dp in enumerate(dparts0)]
    big["even_w_in"] = _split_in0(dwin0, _mm("l0_dwin_dt", h0, ddt_raw, "tn"))
    dh0 = _mm_sum_nt("l0_dh", dparts0 + [ddt_raw], [w_main0[:, i * 1024:(i + 1) * 1024] for i in range(5)] + [w_dt0])
    (dx0,), (dnmix0,) = _pw_bwd("l0_dnorm", _f_norm, [(x0, 0)], [(nmix0, 0)], [dh0], 1024, 1, [0], adds={0: dx1})
    grads["norm_mix"] = jnp.concatenate([dnmix0, dnmix1], axis=0)
    return loss, dx0.reshape(nb, s, d), grads, big, (early_sums + mid_sums, sb0[4:] + sb1[4:])


ANY = pl.BlockSpec(memory_space=pl.ANY)


def _place():
    return lax.axis_index("x"), lax.axis_index("y"), lax.axis_index("c")


def _remote(src, dst, send_sems, recv_sems, k, to):
    return pltpu.make_async_remote_copy(src_ref=src, dst_ref=dst, send_sem=send_sems.at[k], recv_sem=recv_sems.at[k],
                                        device_id=to, device_id_type=MESH)


def _gather_start(x_refs, out_refs, send_sems, recv_sems, finish=False):
    n = len(x_refs)
    halves = [r.shape[0] // 2 for r in x_refs]
    x, y, c = _place()
    sibling = (x, y, 1 - c)
    chips = [(1 - x, y), (x, 1 - y), (1 - x, 1 - y)]

    def blk(t, px, py, hc):
        return out_refs[t].at[2 * px + py, pl.ds(hc * halves[t], halves[t]), :]

    def src(t):
        return x_refs[t].at[pl.ds(c * halves[t], halves[t]), :]

    first = [_remote(src(t), blk(t, x, y, c), send_sems, recv_sems, 6 * t + j, (*chip, c))
             for t in range(n) for j, chip in enumerate(chips)]
    if not finish:
        for cp in first:
            cp.start()
        return
    passed = []
    for t in range(n):
        for j, chip in enumerate(chips):
            _remote(src(t), blk(t, *chip, c), send_sems, recv_sems, 6 * t + j, (*chip, c)).wait_recv()
            cp = _remote(blk(t, *chip, c), blk(t, *chip, c), send_sems, recv_sems, 6 * t + 3 + j, sibling)
            cp.start()
            passed.append(cp)
    for t in range(n):
        for j, chip in enumerate(chips):
            _remote(src(t), blk(t, *chip, 1 - c), send_sems, recv_sems, 6 * t + 3 + j, sibling).wait_recv()
    for cp in first + passed:
        cp.wait_send()


_gather_finish = functools.partial(_gather_start, finish=True)


def _gather_carry(shards):
    n = len(shards)
    return (list(shards), [jax.ShapeDtypeStruct((4,) + s.shape, s.dtype) for s in shards],
            [pltpu.SemaphoreType.DMA((6 * n,)), pltpu.SemaphoreType.DMA((6 * n,))], _gather_start, _gather_finish)


def _gather_chips(shards):
    n = len(shards)
    srcs, shapes, scratch, start, finish = _gather_carry(shards)

    def body(*refs):
        start(refs[:n], refs[n:2 * n], *refs[2 * n:])
        finish(refs[:n], refs[n:2 * n], *refs[2 * n:])

    return _pcall(body, name="gather_weights", in_specs=[ANY] * n, out_specs=(ANY,) * n, out_shape=tuple(shapes),
                  scratch_shapes=scratch, compiler_params=_params())(*shards)


def _pair_swap(name, gps):
    n = len(gps)
    halves = [g.shape[1] // 2 for g in gps]

    def body(*refs):
        g_refs, land_refs = refs[:n], refs[n:2 * n]
        send_sems, recv_sems = refs[2 * n:]
        x, y, c = _place()
        cps = [_remote(g_refs[t].at[j, pl.ds((1 - c) * halves[t], halves[t]), :], land_refs[t].at[j], send_sems, recv_sems,
                       4 * t + j, (x, y, 1 - c)) for t in range(n) for j in range(4)]
        for cp in cps:
            cp.start()
        for cp in cps:
            cp.wait()

    return _pcall(body, name=f"pair_swap_{name}", in_specs=[ANY] * n, out_specs=(ANY,) * n,
                  out_shape=tuple(jax.ShapeDtypeStruct((4, h, g.shape[2]), F32) for g, h in zip(gps, halves)),
                  scratch_shapes=[pltpu.SemaphoreType.DMA((4 * n,)), pltpu.SemaphoreType.DMA((4 * n,))],
                  compiler_params=_params())(*gps)


def _pair_add(name, gp, land, cidx):
    _, half, cols = land.shape
    tr = _tile(half, 512)
    nh = half // tr

    def body(c_ref, g_ref, l_ref, o_ref):
        o_ref[...] = (g_ref[...] + l_ref[...]).astype(o_ref.dtype)

    grid_spec = pltpu.PrefetchScalarGridSpec(
        num_scalar_prefetch=1, grid=(4, nh),
        in_specs=[pl.BlockSpec((None, tr, cols), lambda j, i, c: (j, c[0] * nh + i, 0)),
                  pl.BlockSpec((None, tr, cols), lambda j, i, c: (j, i, 0))],
        out_specs=pl.BlockSpec((None, tr, cols), lambda j, i, c: (j, i, 0)))
    return _pcall(body, name=f"pair_add_{name}", grid_spec=grid_spec, out_shape=jax.ShapeDtypeStruct((4, half, cols), BF16),
                  compiler_params=_params())(cidx, gp, land)


def _scatter_copies(s_refs, land_refs, send_sems, recv_sems):
    x, y, c = _place()
    me = 2 * x + y
    chips = [(1 - x, y), (x, 1 - y), (1 - x, 1 - y)]
    pairs = [(t, j, px, py) for t in range(len(s_refs)) for j, (px, py) in enumerate(chips)]
    sends = [_remote(s_refs[t].at[2 * px + py], land_refs[t].at[me], send_sems, recv_sems, 3 * t + j, (px, py, c))
             for t, j, px, py in pairs]
    arrivals = [_remote(s_refs[t].at[me], land_refs[t].at[2 * px + py], send_sems, recv_sems, 3 * t + j, (px, py, c))
                for t, j, px, py in pairs]
    return sends, arrivals


def _scatter_scratch(n):
    return [pltpu.SemaphoreType.DMA((3 * n,)), pltpu.SemaphoreType.DMA((3 * n,))]


def _chip_scatter(name, css):
    n = len(css)

    def body(*refs):
        sends, arrivals = _scatter_copies(refs[:n], refs[n:2 * n], *refs[2 * n:])
        for cp in sends:
            cp.start()
        for cp in arrivals:
            cp.wait_recv()
        for cp in sends:
            cp.wait_send()

    return _pcall(body, name=f"chip_scatter_{name}", in_specs=[ANY] * n, out_specs=(ANY,) * n,
                  out_shape=tuple(jax.ShapeDtypeStruct(s.shape, s.dtype) for s in css),
                  scratch_shapes=_scatter_scratch(n), compiler_params=_params())(*css)


def _chip_sum(name, land):
    _, half, cols = land.shape
    tr = _tile(half, 512)

    def body(l_ref, o_ref):
        o_ref[...] = ((l_ref[0].astype(F32) + l_ref[1].astype(F32)) + l_ref[2].astype(F32)) + l_ref[3].astype(F32)

    return _pcall(body, name=f"chip_sum_{name}", grid=(half // tr,),
                  in_specs=[pl.BlockSpec((4, tr, cols), lambda i: (0, i, 0))],
                  out_specs=pl.BlockSpec((tr, cols), lambda i: (i, 0)),
                  out_shape=jax.ShapeDtypeStruct((half, cols), F32), compiler_params=_params())(land)


def _pair_join(reds):
    n = len(reds)

    def body(*refs):
        r_refs, out_refs = refs[:n], refs[n:2 * n]
        send_sems, recv_sems = refs[2 * n:]
        x, y, c = _place()
        cps = [_remote(r_refs[t], out_refs[t].at[c], send_sems, recv_sems, t, (x, y, 1 - c)) for t in range(n)]
        for cp in cps:
            cp.start()
        for t in range(n):
            _remote(r_refs[t], out_refs[t].at[1 - c], send_sems, recv_sems, t, (x, y, 1 - c)).wait_recv()
        for cp in cps:
            cp.wait_send()

    return _pcall(body, name="grad_pair_join", in_specs=[ANY] * n, out_specs=(ANY,) * n,
                  out_shape=tuple(jax.ShapeDtypeStruct((2,) + r.shape, F32) for r in reds),
                  scratch_shapes=[pltpu.SemaphoreType.DMA((n,)), pltpu.SemaphoreType.DMA((n,))],
                  compiler_params=_params())(*reds)


def _adamw(name, g, w, m, v):
    rows, cols = g.shape
    tr = _tile(rows, 512)

    def body(g_ref, w_ref, m_ref, v_ref, d_ref, mo_ref, vo_ref):
        gv = g_ref[...]
        mn = ADAM_B1 * m_ref[...] + (1.0 - ADAM_B1) * gv
        vn = ADAM_B2 * v_ref[...] + (1.0 - ADAM_B2) * jnp.square(gv)
        m_hat = mn / (1.0 - ADAM_B1 ** ADAM_STEP)
        v_hat = vn / (1.0 - ADAM_B2 ** ADAM_STEP)
        d_ref[...] = -ADAM_LR * (m_hat / (jnp.sqrt(v_hat) + ADAM_EPS) + ADAM_WD * w_ref[...])
        mo_ref[...] = mn
        vo_ref[...] = vn

    blk = pl.BlockSpec((tr, cols), lambda i: (i, 0))
    shp = jax.ShapeDtypeStruct((rows, cols), F32)
    return _pcall(body, name=f"adamw_{name}", grid=(rows // tr,), in_specs=[blk] * 4, out_specs=(blk,) * 3,
                  out_shape=(shp,) * 3, compiler_params=_params())(g, w, m, v)


def _pack(pieces, rows, dtype):
    flat = jnp.concatenate([p.reshape(-1).astype(dtype) for p in pieces])
    return jnp.pad(flat, (0, rows * PACK_COLS - flat.shape[0])).reshape(rows, PACK_COLS)


def _unpack(pack, shapes):
    flat = pack.reshape(-1)
    out, off = [], 0
    for shp in shapes:
        n = math.prod(shp)
        out.append(flat[off:off + n].reshape(shp))
        off += n
    return out


def _shard_of(full, axis, j):
    n = full.shape[axis] // 4
    return lax.slice_in_dim(full, j * n, (j + 1) * n, axis=axis)


def kernel(x, even_w_in, even_conv_w, even_conv_b, ssd_a_log, ssd_dt_bias, ssd_d, ssd_norm_w, lru_w_a, lru_b_a, lru_w_x, lru_b_x, lru_lambda, even_w_out, odd_w_in, hgrn_lb_logits, hgrn_norm_w, odd_w_out, norm_mix, norm_mlp, mlp_w1, mlp_w2, norm_final, loss_target, m_even_w_in, m_even_conv_w, m_even_conv_b, m_ssd_a_log, m_ssd_dt_bias, m_ssd_d, m_ssd_norm_w, m_lru_w_a, m_lru_b_a, m_lru_w_x, m_lru_b_x, m_lru_lambda, m_even_w_out, m_odd_w_in, m_hgrn_lb_logits, m_hgrn_norm_w, m_odd_w_out, m_norm_mix, m_norm_mlp, m_mlp_w1, m_mlp_w2, m_norm_final, v_even_w_in, v_even_conv_w, v_even_conv_b, v_ssd_a_log, v_ssd_dt_bias, v_ssd_d, v_ssd_norm_w, v_lru_w_a, v_lru_b_a, v_lru_w_x, v_lru_b_x, v_lru_lambda, v_even_w_out, v_odd_w_in, v_hgrn_lb_logits, v_hgrn_norm_w, v_odd_w_out, v_norm_mix, v_norm_mlp, v_mlp_w1, v_mlp_w2, v_norm_final):
    names = [n for n, _, _, _ in WEIGHTS]
    w_loc = dict(zip(names, (even_w_in, even_conv_w, even_conv_b, ssd_a_log, ssd_dt_bias, ssd_d, ssd_norm_w, lru_w_a, lru_b_a, lru_w_x, lru_b_x, lru_lambda, even_w_out, odd_w_in, hgrn_lb_logits, hgrn_norm_w, odd_w_out, norm_mix, norm_mlp, mlp_w1, mlp_w2, norm_final)))
    m_loc = dict(zip(names, (m_even_w_in, m_even_conv_w, m_even_conv_b, m_ssd_a_log, m_ssd_dt_bias, m_ssd_d, m_ssd_norm_w, m_lru_w_a, m_lru_b_a, m_lru_w_x, m_lru_b_x, m_lru_lambda, m_even_w_out, m_odd_w_in, m_hgrn_lb_logits, m_hgrn_norm_w, m_odd_w_out, m_norm_mix, m_norm_mlp, m_mlp_w1, m_mlp_w2, m_norm_final)))
    v_loc = dict(zip(names, (v_even_w_in, v_even_conv_w, v_even_conv_b, v_ssd_a_log, v_ssd_dt_bias, v_ssd_d, v_ssd_norm_w, v_lru_w_a, v_lru_b_a, v_lru_w_x, v_lru_b_x, v_lru_lambda, v_even_w_out, v_odd_w_in, v_hgrn_lb_logits, v_hgrn_norm_w, v_odd_w_out, v_norm_mix, v_norm_mlp, v_mlp_w1, v_mlp_w2, v_norm_final)))
    spec = {n: (blk, full, ax) for n, blk, full, ax in WEIGHTS}

    small = [n for n in names if n not in BIG]
    two_d = lambda n, v: v.reshape(BIG_2D[n])

    me = 2 * lax.axis_index("x") + lax.axis_index("y")
    cc = lax.axis_index("c")
    put = lambda whole, part, k: lax.dynamic_update_slice_in_dim(whole, part[None], k, axis=0)
    own = {n: two_d(n, w_loc[n]).astype(BF16) for n in BIG}
    own["small"] = _pack([w_loc[n] for n in SMALL_SHARDED], 16, F32)
    fill = lambda got, keys: [put(g, own[k], me) for g, k in zip(got, keys)]
    first = ("even_w_in", "even_w_out", "small")
    g_in0, g_out0, g_small = fill(_gather_chips([own[k] for k in first]), first)
    w_main0, w_dt0 = _assemble_in0(g_in0)
    w_full = {n: w_loc[n] for n in names if spec[n][2] is None}
    w_full["even_w_out"] = g_out0.reshape(1, 2048, 1024)
    shards = [_unpack(g_small[j], [spec[n][0] for n in SMALL_SHARDED]) for j in range(4)]
    carries = {"mlp_w1": _gather_carry([own["mlp_w1"]]), "mlp_w2": _gather_carry([own["mlp_w2"]]),
               "odd": _gather_carry([own["odd_w_in"], own["odd_w_out"]])}

    def arrived(key, got):
        if key == "odd":
            g_in1, g_out1 = fill(got, ("odd_w_in", "odd_w_out"))
            return {"odd_w_in": jnp.concatenate([g_in1[j] for j in range(4)], axis=1)[None],
                    "odd_w_out": g_out1.reshape(1, 1024, 1024)}
        (g,) = fill(got, (key,))
        axis = 1 if key == "mlp_w1" else 0
        return {key: jnp.stack([jnp.concatenate([g[j, l * 1024:(l + 1) * 1024] for j in range(4)], axis=axis) for l in range(2)])}

    for i, n in enumerate(SMALL_SHARDED):
        w_full[n] = jnp.concatenate([shards[j][i] for j in range(4)], axis=spec[n][2])

    cidx = cc.astype(jnp.int32).reshape(1)

    def pair_reduce(tags, tensors):
        return [_pair_add(tag, g, land, cidx) for tag, g, land in zip(tags, tensors, _pair_swap(tags[0], tensors))]

    loss_vec, grad_x, grads, big, (early_sums, early_landed) = _local_step(
        x, loss_target, w_full, w_main0, w_dt0, pair_reduce, (carries, arrived))
    loss = lax.psum(loss_vec[0, 0], ("x", "y", "c"))

    def dest_pack(j):
        return _pack([grads[n].reshape(spec[n][1]) if spec[n][2] is None else _shard_of(grads[n].reshape(spec[n][1]), spec[n][2], j)
                      for n in small], SMALL_ROWS, F32)

    late_tags = LATE + ("small",)
    late_sums = pair_reduce(late_tags, [big[n] for n in LATE] + [jnp.stack([dest_pack(j) for j in range(4)])])
    tags = EARLY + MID + late_tags
    chip_sums = list(early_sums) + late_sums
    landed = [put(land, lax.dynamic_index_in_dim(cs, me, axis=0, keepdims=False), me)
              for land, cs in zip(list(early_landed) + list(_chip_scatter("late", late_sums)), chip_sums)]
    halves = [_chip_sum(tag, land) for tag, land in zip(tags, landed)]
    red = {tag: put(r, h, cc).reshape(-1, r.shape[-1]) for tag, r, h in zip(tags, _pair_join(halves), halves)}
    for n in ("mlp_w1", "mlp_w2"):
        red[n] = jnp.concatenate([red[n + "_l0"], red[n + "_l1"]], axis=0)

    outs = {}
    for n, g in ((n, red[n]) for n in BIG):
        res = (g, *_adamw(n, g, two_d(n, w_loc[n]), two_d(n, m_loc[n]), two_d(n, v_loc[n])))
        outs[n] = [r.reshape(spec[n][0]) for r in res]
    blocks = [spec[n][0] for n in small]
    wp, mp, vp = (_pack([src[n] for n in small], SMALL_ROWS, F32) for src in (w_loc, m_loc, v_loc))
    res = (red["small"], *_adamw("small", red["small"], wp, mp, vp))
    unpacked = [_unpack(r, blocks) for r in res]
    for i, n in enumerate(small):
        outs[n] = [u[i] for u in unpacked]
    return (loss, grad_x, *[outs[n][k] for k in range(4) for n in names])
```

```python
import functools
import math

import jax
import jax.numpy as jnp
from jax import lax
from jax.experimental import pallas as pl
from jax.experimental.pallas import tpu as pltpu

F32 = jnp.float32
BF16 = jnp.bfloat16
MXU_DTYPE = jnp.bfloat16
MESH = pl.DeviceIdType.MESH

D_MODEL = 1024
EPS = 1e-6
SSD_HEADS = 16
SSD_HEADDIM = 64
HEAD_SHIFT = 6
SSD_GROUPS = 4
SSD_STATE = 128
SSD_CHUNK = 128
LRU_C = 8.0
LRU_ROWS = 256
HGRN_HEADS = 8
HGRN_HEADDIM = 128
HGRN_SUB = 32
HGRN_SUB_SHIFT = 5
HGRN_BLOCK = 128
HGRN_SCALE = HGRN_HEADDIM ** -0.5
CONV_ROWS = 512
ROWS_FWD = 512
ROWS_BWD = 256

ADAM_LR = 0.001
ADAM_B1 = 0.9
ADAM_B2 = 0.999
ADAM_EPS = 1e-08
ADAM_WD = 0.01
ADAM_STEP = 10

VMEM_LIMIT = 56 * 1024 * 1024
PACK_COLS = 1024
SMALL_ROWS = 288

WEIGHTS = (
    ("even_w_in", (1, 1024, 1288), (1, 1024, 5152), 2),
    ("even_conv_w", (1, 4, 768), (1, 4, 3072), 2),
    ("even_conv_b", (1, 3072), (1, 3072), None),
    ("ssd_a_log", (1, 2, 16), (1, 2, 16), None),
    ("ssd_dt_bias", (1, 2, 16), (1, 2, 16), None),
    ("ssd_d", (1, 16), (1, 16), None),
    ("ssd_norm_w", (1, 1024), (1, 1024), None),
    ("lru_w_a", (1, 2, 16, 64, 64), (1, 2, 16, 64, 64), None),
    ("lru_b_a", (1, 2, 256), (1, 2, 1024), 2),
    ("lru_w_x", (1, 2, 16, 64, 64), (1, 2, 16, 64, 64), None),
    ("lru_b_x", (1, 2, 256), (1, 2, 1024), 2),
    ("lru_lambda", (1, 2, 256), (1, 2, 1024), 2),
    ("even_w_out", (1, 512, 1024), (1, 2048, 1024), 1),
    ("odd_w_in", (1, 1024, 1280), (1, 1024, 5120), 2),
    ("hgrn_lb_logits", (2, 1024), (2, 1024), None),
    ("hgrn_norm_w", (1, 256), (1, 1024), 1),
    ("odd_w_out", (1, 256, 1024), (1, 1024, 1024), 1),
    ("norm_mix", (2, 1024), (2, 1024), None),
    ("norm_mlp", (2, 1024), (2, 1024), None),
    ("mlp_w1", (2, 1024, 1024), (2, 1024, 4096), 2),
    ("mlp_w2", (2, 1024, 1024), (2, 4096, 1024), 1),
    ("norm_final", (1024,), (1024,), None),
)
BIG = ("even_w_in", "even_w_out", "odd_w_in", "odd_w_out", "mlp_w1", "mlp_w2")
BIG_2D = {"even_w_in": (1024, 1288), "even_w_out": (512, 1024), "odd_w_in": (1024, 1280), "odd_w_out": (256, 1024),
          "mlp_w1": (2048, 1024), "mlp_w2": (2048, 1024)}
SMALL_SHARDED = ("even_conv_w", "lru_b_a", "lru_b_x", "lru_lambda", "hgrn_norm_w")


def _pcall(body, carry=None, **kw):
    if carry is not None:
        srcs, shapes, scratch, start, finish = carry
        grid, inner = kw["grid"], body
        as_tuple = lambda v: tuple(v) if isinstance(v, (tuple, list)) else (v,)
        out_specs, out_shape, own_scratch = as_tuple(kw["out_specs"]), as_tuple(kw["out_shape"]), list(kw.get("scratch_shapes", ()))
        a = len(kw["in_specs"])
        b = a + len(srcs)
        c = b + len(out_specs)
        d = c + len(shapes)
        e = d + len(own_scratch)

        def body(*refs):
            ids = [pl.program_id(ax) for ax in range(len(grid))]
            first = functools.reduce(jnp.logical_and, [i == 0 for i in ids])
            last = functools.reduce(jnp.logical_and, [i == g - 1 for i, g in zip(ids, grid)])
            pl.when(first)(lambda: start(refs[a:b], refs[c:d], *refs[e:]))
            inner(*refs[:a], *refs[b:c], *refs[d:e])
            pl.when(last)(lambda: finish(refs[a:b], refs[c:d], *refs[e:]))

        kw = dict(kw, in_specs=list(kw["in_specs"]) + [ANY] * len(srcs), out_specs=out_specs + (ANY,) * len(shapes),
                  out_shape=out_shape + tuple(shapes), scratch_shapes=own_scratch + list(scratch))
    return pl.pallas_call(body, **kw)


def _params(**kw):
    return pltpu.CompilerParams(vmem_limit_bytes=VMEM_LIMIT, **kw)


def _tile(n, pref):
    if n <= pref:
        return n
    t = (pref // 128) * 128
    while n % t:
        t -= 128
    return t


def _dot(a, b, dims=(((1,), (0,)), ((), ()))):
    return lax.dot_general(a, b, dims, preferred_element_type=F32)


_NN = (((1,), (0,)), ((), ()))
_NT = (((1,), (1,)), ((), ()))
_TN = (((0,), (0,)), ((), ()))


def _mx(v):
    return v.astype(MXU_DTYPE)


def _dot01(a, b, dims=_NN, *, split, terms):
    acc, rest = None, (a if split == "a" else b)
    for _ in range(terms):
        piece = _mx(rest)
        part = _dot(piece, _mx(b), dims) if split == "a" else _dot(_mx(a), piece, dims)
        acc = part if acc is None else acc + part
        rest = rest - piece.astype(F32)
    return acc


def _mm(name, a, b, mode, *, out_dtype=F32, res=None, relu2=False, relu2_of=None, col_shards=1, carry=None):
    if mode == "nn":
        (m, kk), (_, n) = a.shape, b.shape
    elif mode == "nt":
        (m, kk), (n, _) = a.shape, b.shape
    else:
        (kk, m), (_, n) = a.shape, b.shape
    assert res is None or relu2_of is None
    tk_pref = 1024
    if mode == "tn" and a.dtype.itemsize == 2 and b.dtype.itemsize == 2:
        tk_pref = 2048
    tm, tn, tk = _tile(m, 1024), _tile(n // col_shards, 1024), _tile(kk, tk_pref)
    nk = kk // tk
    dims = {"nn": _NN, "nt": _NT, "tn": _TN}[mode]
    a_spec = pl.BlockSpec((tk, tm), lambda i, j, k: (k, i)) if mode == "tn" else pl.BlockSpec((tm, tk), lambda i, j, k: (i, k))
    b_spec = pl.BlockSpec((tn, tk), lambda i, j, k: (j, k)) if mode == "nt" else pl.BlockSpec((tk, tn), lambda i, j, k: (k, j))
    o_spec = pl.BlockSpec((tm, tn), lambda i, j, k: (i, j))
    o_shape = (m, n)
    if col_shards > 1:
        assert tn * col_shards == n and res is None and not relu2
        o_spec = pl.BlockSpec((None, tm, tn), lambda i, j, k: (j, i, 0))
        o_shape = (col_shards, m, tn)
    extra = res if res is not None else relu2_of
    has_res = extra is not None

    def body(*refs):
        a_ref, b_ref = refs[0], refs[1]
        res_ref = refs[2] if has_res else None
        outs = refs[2 + has_res:2 + has_res + 1 + relu2]

        def finish(r):
            if res is not None:
                r = r + res_ref[...]
            if relu2_of is not None:
                r = r * (2.0 * jnp.maximum(res_ref[...].astype(F32), 0.0))
            if relu2:
                outs[0][...] = r.astype(outs[0].dtype)
                outs[1][...] = jnp.square(jnp.maximum(r, 0.0)).astype(outs[1].dtype)
            else:
                outs[0][...] = r.astype(outs[0].dtype)

        prod = _dot(_mx(a_ref[...]), _mx(b_ref[...]), dims)
        if nk == 1:
            finish(prod)
            return
        acc = refs[-1]
        k = pl.program_id(2)

        @pl.when(k == 0)
        def _():
            acc[...] = prod

        @pl.when(k > 0)
        def _():
            acc[...] += prod

        @pl.when(k == nk - 1)
        def _():
            finish(acc[...])

    in_specs = [a_spec, b_spec] + ([o_spec] if has_res else [])
    if relu2:
        out_shape = (jax.ShapeDtypeStruct((m, n), BF16), jax.ShapeDtypeStruct((m, n), BF16))
        out_specs = (o_spec, o_spec)
    else:
        out_shape = jax.ShapeDtypeStruct(o_shape, out_dtype)
        out_specs = o_spec
    args = (a, b) + ((extra,) if has_res else ()) + (tuple(carry[0]) if carry else ())
    return _pcall(body, carry=carry, name=name, grid=(m // tm, n // tn, nk), in_specs=in_specs, out_specs=out_specs,
                  out_shape=out_shape, scratch_shapes=[pltpu.VMEM((tm, tn), F32)] if nk > 1 else [],
                  compiler_params=_params())(*args)


def _mm_sum_nt(name, parts, wblocks):
    m, n, npart = parts[0].shape[0], wblocks[0].shape[0], len(parts)
    tm, tn = _tile(m, 512), _tile(n, 1024)

    def body(*refs):
        acc = _dot(_mx(refs[0][...]), _mx(refs[npart][...]), _NT)
        for k in range(1, npart):
            acc = acc + _dot(_mx(refs[k][...]), _mx(refs[npart + k][...]), _NT)
        refs[-1][...] = acc

    in_specs = [pl.BlockSpec((tm, p.shape[1]), lambda i, j: (i, 0)) for p in parts]
    in_specs += [pl.BlockSpec((tn, w.shape[1]), lambda i, j: (j, 0)) for w in wblocks]
    return _pcall(body, name=name, grid=(m // tm, n // tn), in_specs=in_specs, out_specs=pl.BlockSpec((tm, tn), lambda i, j: (i, j)),
                  out_shape=jax.ShapeDtypeStruct((m, n), F32), compiler_params=_params())(*parts, *wblocks)


def _pw_fwd(name, f, ins, params, out_dtypes, tc, ncol, tm=ROWS_FWD, groups=1):
    t = ins[0][0].shape[0]
    tm = min(tm, t)
    ni, npar = len(ins), len(params)
    gw = tc // groups

    def body(*refs):
        for g in range(groups):
            sl = slice(g * gw, (g + 1) * gw)
            vals = f(*[r[:, sl].astype(F32) for r in refs[:ni]], *[r[:, sl] for r in refs[ni:ni + npar]])
            for o, v in zip(refs[ni + npar:], vals):
                o[:, sl] = v.astype(o.dtype)

    in_specs = [pl.BlockSpec((tm, tc), lambda j, i, off=off: (i, off + j)) for _, off in ins]
    in_specs += [pl.BlockSpec((1, tc), lambda j, i, off=off: (0, off + j)) for _, off in params]
    out_specs = tuple(pl.BlockSpec((tm, tc), lambda j, i: (i, j)) for _ in out_dtypes)
    out_shape = tuple(jax.ShapeDtypeStruct((t, ncol * tc), d) for d in out_dtypes)
    return _pcall(body, name=name, grid=(ncol, t // tm), in_specs=in_specs, out_specs=out_specs, out_shape=out_shape,
                  compiler_params=_params())(*[a for a, _ in ins], *[p for p, _ in params])


def _pw_bwd(name, f, ins, params, douts, tc, ncol, want, adds=None, tm=ROWS_BWD, out_dtypes=None, groups=1):
    adds = adds or {}
    out_dtypes = out_dtypes or [F32] * len(want)
    t = ins[0][0].shape[0]
    tm = min(tm, t)
    ni, npar, nd, na = len(ins), len(params), len(douts), len(adds)
    add_keys = sorted(adds)
    gw = tc // groups

    def body(*refs):
        in_refs, p_refs = refs[:ni], refs[ni:ni + npar]
        d_refs = refs[ni + npar:ni + npar + nd]
        a_refs = refs[ni + npar + nd:ni + npar + nd + na]
        o_refs = refs[ni + npar + nd + na:]
        for p in range(npar):
            @pl.when(pl.program_id(1) == 0)
            def _(o=o_refs[len(want) + p]):
                o[...] = jnp.zeros_like(o)

        for g in range(groups):
            sl = slice(g * gw, (g + 1) * gw)
            _, vjp = jax.vjp(f, *[r[:, sl].astype(F32) for r in in_refs], *[r[:, sl] for r in p_refs])
            cts = vjp(tuple(d[:, sl].astype(F32) for d in d_refs))
            for o, kidx in zip(o_refs[:len(want)], want):
                v = cts[kidx]
                if kidx in adds:
                    v = v + a_refs[add_keys.index(kidx)][:, sl]
                o[:, sl] = v.astype(o.dtype)
            for p in range(npar):
                o_refs[len(want) + p][:, sl] += cts[ni + p]

    in_specs = [pl.BlockSpec((tm, tc), lambda j, i, off=off: (i, off + j)) for _, off in ins]
    in_specs += [pl.BlockSpec((1, tc), lambda j, i, off=off: (0, off + j)) for _, off in params]
    in_specs += [pl.BlockSpec((tm, tc), lambda j, i: (i, j)) for _ in range(nd + na)]
    out_specs = tuple([pl.BlockSpec((tm, tc), lambda j, i: (i, j)) for _ in want]
                      + [pl.BlockSpec((1, tc), lambda j, i: (0, j)) for _ in params])
    out_shape = tuple([jax.ShapeDtypeStruct((t, ncol * tc), dt) for dt in out_dtypes]
                      + [jax.ShapeDtypeStruct((1, ncol * tc), F32) for _ in params])
    res = _pcall(body, name=name, grid=(ncol, t // tm), in_specs=in_specs, out_specs=out_specs, out_shape=out_shape,
                 compiler_params=_params())(*[a for a, _ in ins], *[p for p, _ in params], *douts, *[adds[k] for k in add_keys])
    return list(res[:len(want)]), list(res[len(want):])


def _rms(x, g):
    return (x * lax.rsqrt(jnp.mean(x * x, axis=-1, keepdims=True) + EPS)) * g


def _f_norm(x, g):
    return (_rms(x, g),)


def _f_softplus(d, b):
    return (jax.nn.softplus(d + b),)


def _f_add3(a, b, c):
    return (a + b + c,)


def _f_ssd_post(yf, yb, xs, z, dskip, nw):
    u = (yf + yb + dskip * xs) * jax.nn.silu(z)
    return (_rms(u, nw),)


def _neg_expm1(v):
    t = jnp.tanh(0.5 * v)
    return -2.0 * t / (1.0 - t)


def _f_lru_gates(pre_a, pre_x, u, ba, bx, lam):
    rg = jax.nn.sigmoid(pre_a + ba)
    ig = jax.nn.sigmoid(pre_x + bx)
    log_a = -LRU_C * rg * jax.nn.softplus(-lam)
    return jnp.exp(log_a), jnp.sqrt(_neg_expm1(2.0 * log_a)) * (ig * u)


def _f_lru_post(hf, hb, gate):
    return ((hf + hb) * jax.nn.gelu(gate),)


def _f_hgrn_pre(fr, l0, l1):
    lb = jax.nn.sigmoid(l1 - l0)
    k = (1.0 - lb) * jax.nn.sigmoid(-fr)
    return k, jnp.log1p(-k)


def _f_hgrn_post(of, ob, gate, nw):
    return (_rms(of + ob, nw) * jax.nn.silu(gate),)


def _loss_head(x, tgt, g, tm=ROWS_FWD):
    t, d = x.shape
    tm = min(tm, t)

    def body(x_ref, t_ref, g_ref, dx_ref, dg_ref, loss_ref):
        tv = t_ref[...]

        def lf(xv, gv):
            return 0.5 * jnp.sum(jnp.mean(jnp.square(_rms(xv, gv) - tv), axis=-1))

        val, vjp = jax.vjp(lf, x_ref[...], g_ref[...])
        dx, dg = vjp(jnp.ones((), F32))
        dx_ref[...] = dx

        @pl.when(pl.program_id(0) == 0)
        def _():
            dg_ref[...] = jnp.zeros_like(dg_ref)
            loss_ref[...] = jnp.zeros_like(loss_ref)

        dg_ref[...] += dg
        loss_ref[...] += jnp.full(loss_ref.shape, val, F32)

    row = pl.BlockSpec((tm, d), lambda i: (i, 0))
    vec = pl.BlockSpec((1, d), lambda i: (0, 0))
    return _pcall(body, name="loss_head", grid=(t // tm,), in_specs=[row, row, vec],
                  out_specs=(row, vec, pl.BlockSpec((1, 128), lambda i: (0, 0))),
                  out_shape=(jax.ShapeDtypeStruct((t, d), F32), jax.ShapeDtypeStruct((1, d), F32),
                             jax.ShapeDtypeStruct((1, 128), F32)), compiler_params=_params())(x, tgt, g)


def _shifted(x, d, prev, nxt, first, last):
    r = x.shape[0]
    row = lax.broadcasted_iota(jnp.int32, x.shape, 0)
    if d < 0:
        out = pltpu.roll(x, -d, 0)
        for q in range(-d):
            pv = jnp.where(first, 0.0, prev[8 + d + q:8 + d + q + 1, :])
            out = jnp.where(row == q, pv, out)
        return out
    out = pltpu.roll(x, r - d, 0)
    for q in range(d):
        nv = jnp.where(last, 0.0, nxt[q:q + 1, :])
        out = jnp.where(row == r - d + q, nv, out)
    return out


def _conv_fwd(p3, w, b, col0, ncol, silu, tc=1024):
    nbatch, s, _ = p3.shape
    ts = min(CONV_ROWS, s)
    nblk = s // ts

    def body(x_ref, pv_ref, nx_ref, w_ref, b_ref, o_ref, *act_ref):
        i = pl.program_id(1)
        first, last = i == 0, i == nblk - 1
        x, pv, nx = x_ref[...], pv_ref[...], nx_ref[...]
        wv = w_ref[...]
        out = b_ref[...] + wv[1:2] * x
        out = out + wv[0:1] * _shifted(x, -1, pv, nx, first, last)
        out = out + wv[2:3] * _shifted(x, 1, pv, nx, first, last)
        out = out + wv[3:4] * _shifted(x, 2, pv, nx, first, last)
        o_ref[...] = out
        if silu:
            act_ref[0][...] = jax.nn.silu(out)

    nb8 = s // 8
    cur = pl.BlockSpec((None, ts, tc), lambda n, i, j: (n, i, col0 + j))
    prev = pl.BlockSpec((None, 8, tc), lambda n, i, j: (n, jnp.maximum(i * (ts // 8) - 1, 0), col0 + j))
    nxt = pl.BlockSpec((None, 8, tc), lambda n, i, j: (n, jnp.minimum((i + 1) * (ts // 8), nb8 - 1), col0 + j))
    out = pl.BlockSpec((None, ts, tc), lambda n, i, j: (n, i, j))
    shp = jax.ShapeDtypeStruct((nbatch, s, ncol * tc), F32)
    return _pcall(body, name=f"conv_fwd{col0}", grid=(nbatch, nblk, ncol),
                  in_specs=[cur, prev, nxt, pl.BlockSpec((4, tc), lambda n, i, j: (0, col0 + j)),
                            pl.BlockSpec((1, tc), lambda n, i, j: (0, col0 + j))],
                  out_specs=(out, out) if silu else out, out_shape=(shp, shp) if silu else shp,
                  compiler_params=_params())(p3, p3, p3, w, b)


def _conv_bwd(dc3, p3, w, col, conv3=None):
    nbatch, s, tc = dc3.shape
    ts = min(CONV_ROWS, s)
    nblk = s // ts
    silu = conv3 is not None

    def body(d_ref, dpv_ref, dnx_ref, x_ref, pv_ref, nx_ref, w_ref, *rest):
        n, i = pl.program_id(0), pl.program_id(1)
        first, last = i == 0, i == nblk - 1
        d, dpv, dnx = d_ref[...], dpv_ref[...], dnx_ref[...]
        if silu:
            d, dpv, dnx = [jax.vjp(jax.nn.silu, c_ref[...])[1](t)[0] for c_ref, t in zip(rest[:3], (d, dpv, dnx))]
        dx_ref, dw_ref = rest[3 * silu:]
        x, pv, nx = x_ref[...], pv_ref[...], nx_ref[...]
        wv = w_ref[...]
        dx = wv[1:2] * d
        dx = dx + wv[0:1] * _shifted(d, 1, dpv, dnx, first, last)
        dx = dx + wv[2:3] * _shifted(d, -1, dpv, dnx, first, last)
        dx = dx + wv[3:4] * _shifted(d, -2, dpv, dnx, first, last)
        dx_ref[...] = dx.astype(dx_ref.dtype)

        @pl.when((n == 0) & (i == 0))
        def _():
            dw_ref[...] = jnp.zeros_like(dw_ref)

        dw_ref[0:1, :] += jnp.sum(d * _shifted(x, -1, pv, nx, first, last), axis=0, keepdims=True)
        dw_ref[1:2, :] += jnp.sum(d * x, axis=0, keepdims=True)
        dw_ref[2:3, :] += jnp.sum(d * _shifted(x, 1, pv, nx, first, last), axis=0, keepdims=True)
        dw_ref[3:4, :] += jnp.sum(d * _shifted(x, 2, pv, nx, first, last), axis=0, keepdims=True)
        dw_ref[4:5, :] += jnp.sum(d, axis=0, keepdims=True)

    nb8 = s // 8

    def specs(j):
        cur = pl.BlockSpec((None, ts, tc), lambda n, i: (n, i, j))
        prev = pl.BlockSpec((None, 8, tc), lambda n, i: (n, jnp.maximum(i * (ts // 8) - 1, 0), j))
        nxt = pl.BlockSpec((None, 8, tc), lambda n, i: (n, jnp.minimum((i + 1) * (ts // 8), nb8 - 1), j))
        return [cur, prev, nxt]

    return _pcall(body, name=f"conv_bwd{col}", grid=(nbatch, nblk),
                  in_specs=specs(0) + specs(col) + [pl.BlockSpec((4, tc), lambda n, i: (0, col))] + specs(col) * silu,
                  out_specs=(specs(0)[0], pl.BlockSpec((8, tc), lambda n, i: (0, 0))),
                  out_shape=(jax.ShapeDtypeStruct((nbatch, s, tc), BF16), jax.ShapeDtypeStruct((8, tc), F32)),
                  compiler_params=_params())(dc3, dc3, dc3, p3, p3, p3, w, *([conv3] * 3 * silu))


def _block_scan(coef, inp, reverse):
    r = coef.shape[0]
    row = lax.broadcasted_iota(jnp.int32, coef.shape, 0)
    a, b = coef, inp
    d = 1
    while d < r:
        if reverse:
            keep = row < r - d
            a_sh, b_sh = pltpu.roll(a, r - d, 0), pltpu.roll(b, r - d, 0)
        else:
            keep = row >= d
            a_sh, b_sh = pltpu.roll(a, d, 0), pltpu.roll(b, d, 0)
        b = b + a * jnp.where(keep, b_sh, 0.0)
        a = a * jnp.where(keep, a_sh, 1.0)
        d *= 2
    return a, b


def _lru_scan(a3, b3, reverse):
    nbatch, s, w = a3.shape
    ts = min(LRU_ROWS, s)
    nblk = s // ts
    edge = 0 if reverse else ts - 1

    def body(a_ref, b_ref, h_ref, carry):
        @pl.when(pl.program_id(1) == 0)
        def _():
            carry[...] = jnp.zeros_like(carry)

        ca, hb = _block_scan(a_ref[...], b_ref[...], reverse)
        h = hb + ca * carry[0:1, :]
        h_ref[...] = h
        carry[0:1, :] = h[edge:edge + 1, :]

    blk = pl.BlockSpec((None, ts, w), (lambda n, i: (n, nblk - 1 - i, 0)) if reverse else (lambda n, i: (n, i, 0)))
    return _pcall(body, name=f"lru_scan_r{int(reverse)}", grid=(nbatch, nblk), in_specs=[blk, blk], out_specs=blk,
                  out_shape=jax.ShapeDtypeStruct((nbatch, s, w), F32), scratch_shapes=[pltpu.VMEM((8, w), F32)],
                  compiler_params=_params())(a3, b3)


def _lru_scan_bwd(a3, h3, dh3, reverse):
    nbatch, s, w = a3.shape
    ts = min(LRU_ROWS, s)
    nblk = s // ts
    nb8 = s // 8
    tpb = ts // 8

    def body(a_ref, aa_ref, h_ref, hh_ref, dh_ref, g_ref, da_ref, carry):
        i = pl.program_id(1)

        @pl.when(i == 0)
        def _():
            carry[...] = jnp.zeros_like(carry)

        a, h = a_ref[...], h_ref[...]
        row = lax.broadcasted_iota(jnp.int32, a.shape, 0)
        if reverse:
            a_edge = jnp.where(i == 0, 0.0, aa_ref[7:8, :])
            c = jnp.where(row == 0, a_edge, pltpu.roll(a, 1, 0))
            h_edge = jnp.where(i == nblk - 1, 0.0, hh_ref[0:1, :])
            h_sh = jnp.where(row == ts - 1, h_edge, pltpu.roll(h, ts - 1, 0))
        else:
            a_edge = jnp.where(i == 0, 0.0, aa_ref[0:1, :])
            c = jnp.where(row == ts - 1, a_edge, pltpu.roll(a, ts - 1, 0))
            h_edge = jnp.where(i == nblk - 1, 0.0, hh_ref[7:8, :])
            h_sh = jnp.where(row == 0, h_edge, pltpu.roll(h, 1, 0))
        cc, gb = _block_scan(c, dh_ref[...], not reverse)
        g = gb + cc * carry[0:1, :]
        g_ref[...] = g
        carry[0:1, :] = g[ts - 1:ts, :] if reverse else g[0:1, :]
        da_ref[...] = g * h_sh

    if reverse:
        bi = lambda i: i
    else:
        bi = lambda i: nblk - 1 - i
    blk = pl.BlockSpec((None, ts, w), lambda n, i: (n, bi(i), 0))
    before = pl.BlockSpec((None, 8, w), lambda n, i: (n, jnp.maximum(bi(i) * tpb - 1, 0), 0))
    after = pl.BlockSpec((None, 8, w), lambda n, i: (n, jnp.minimum((bi(i) + 1) * tpb, nb8 - 1), 0))
    a_tile, h_tile = (before, after) if reverse else (after, before)
    return _pcall(body, name=f"lru_scan_bwd_r{int(reverse)}", grid=(nbatch, nblk), in_specs=[blk, a_tile, blk, h_tile, blk],
                  out_specs=(blk, blk),
                  out_shape=(jax.ShapeDtypeStruct((nbatch, s, w), F32), jax.ShapeDtypeStruct((nbatch, s, w), F32)),
                  scratch_shapes=[pltpu.VMEM((8, w), F32)], compiler_params=_params())(a3, a3, h3, h3, dh3)


def _head_expand(lane0):
    return (jnp.right_shift(lax.broadcasted_iota(jnp.int32, (128, 1024), 1), HEAD_SHIFT) + lane0
            == lax.broadcasted_iota(jnp.int32, (128, 1024), 0)).astype(F32)


def _head_reduce(lane0):
    return (jnp.right_shift(lax.broadcasted_iota(jnp.int32, (1024, 128), 0), HEAD_SHIFT) + lane0
            == lax.broadcasted_iota(jnp.int32, (1024, 128), 1)).astype(F32)


def _time_mask(q, reverse):
    ri = lax.broadcasted_iota(jnp.int32, (q, q), 0)
    ci = lax.broadcasted_iota(jnp.int32, (q, q), 1)
    return (ri <= ci) if reverse else (ri >= ci)


def _ssd_common(xs_ref, bc_ref, dt_ref, al_ref, reverse, lane0):
    q = xs_ref.shape[0]
    edge = 0 if reverse else q - 1
    dt = dt_ref[...]
    a = -jnp.exp(al_ref[...])
    mask = _time_mask(q, reverse)
    expand = _head_expand(lane0)
    cum = _dot01(mask.astype(F32), dt * a, split="b", terms=3)
    cum_x = _dot01(cum, expand, split="a", terms=3)
    dt_x = _dot01(dt, expand, split="a", terms=2)
    last_x = cum_x[edge:edge + 1, :]
    xs = xs_ref[...]
    bc = bc_ref[...]
    return dict(q=q, edge=edge, lane0=lane0, dt=dt, a=a, mask=mask, cum_t=cum.T, cum_x=cum_x, dt_x=dt_x, xs=xs,
                v=xs * dt_x, e_c=jnp.exp(cum_x), w=jnp.exp(last_x - cum_x), e_l=jnp.exp(last_x),
                bm=bc[:, :512], cm=bc[:, 512:])


def _ssd_decay(c, h):
    row = c["lane0"] + h
    seg = c["cum_x"][:, h * SSD_HEADDIM:h * SSD_HEADDIM + 1] - c["cum_t"][row:row + 1, :]
    return jnp.where(c["mask"], jnp.exp(jnp.minimum(seg, 0.0)), 0.0)


def _head_masks():
    lane = jnp.right_shift(lax.broadcasted_iota(jnp.int32, (1, 256), 1), HEAD_SHIFT)
    return [lane == e for e in range(4)]


def _ssd_fwd(xbc3, dt3, alog, reverse, carry=None):
    nbatch, s, _ = xbc3.shape
    q = min(SSD_CHUNK, s)
    nc = s // q
    lane0 = SSD_HEADS * int(reverse)

    def body(xs_ref, bc_ref, dt_ref, al_ref, y_ref, st_ref, st):
        @pl.when(pl.program_id(1) == 0)
        def _():
            st[...] = jnp.zeros_like(st)

        st_ref[...] = st[...]
        c = _ssd_common(xs_ref, bc_ref, dt_ref, al_ref, reverse, lane0)
        hm = _head_masks()
        for g in range(SSD_GROUPS):
            sl = slice(g * 256, (g + 1) * 256)
            cg, bg = _mx(c["cm"][:, g * 128:(g + 1) * 128]), _mx(c["bm"][:, g * 128:(g + 1) * 128])
            cb = _dot(cg, bg, _NT)
            vg = c["v"][:, sl]
            s0 = st[:, sl]
            yg = _dot(cg, _mx(s0)) * c["e_c"][:, sl]
            for e in range(4):
                m = _ssd_decay(c, 4 * g + e) * cb
                yg = yg + _dot(_mx(m), _mx(jnp.where(hm[e], vg, 0.0)))
            y_ref[:, sl] = yg
            st[:, sl] = c["e_l"][:, sl] * s0 + _dot(bg, _mx(vg * c["w"][:, sl]), _TN)

    ck = (lambda i: nc - 1 - i) if reverse else (lambda i: i)
    xs_spec = pl.BlockSpec((None, q, 1024), lambda n, i: (n, ck(i), 0))
    bc_spec = pl.BlockSpec((None, q, 1024), lambda n, i: (n, ck(i), 1))
    dt_spec = pl.BlockSpec((None, q, 128), lambda n, i: (n, ck(i), 0))
    al_spec = pl.BlockSpec((1, 128), lambda n, i: (0, 0))
    st_spec = pl.BlockSpec((None, None, 128, 1024), lambda n, i: (n, ck(i), 0, 0))
    return _pcall(body, carry=carry, name=f"ssd_fwd_r{int(reverse)}", grid=(nbatch, nc),
                  in_specs=[xs_spec, bc_spec, dt_spec, al_spec], out_specs=(xs_spec, st_spec),
                  out_shape=(jax.ShapeDtypeStruct((nbatch, s, 1024), F32), jax.ShapeDtypeStruct((nbatch, nc, 128, 1024), F32)),
                  scratch_shapes=[pltpu.VMEM((128, 1024), F32)],
                  compiler_params=_params())(xbc3, xbc3, dt3, alog, *(carry[0] if carry else ()))


def _ssd_bwd(xbc3, dt3, alog, st4, dy3, reverse, add_to=(), scatter=()):
    nbatch, s, _ = xbc3.shape
    q = min(SSD_CHUNK, s)
    nc = s // q
    lane0 = SSD_HEADS * int(reverse)
    nadd, ns = len(add_to), len(scatter)

    def body(xs_ref, bc_ref, dt_ref, al_ref, st0_ref, dy_ref, *rest):
        adds, srcs, rest = rest[:nadd], rest[nadd:nadd + ns], rest[nadd + ns:]
        (dxs_ref, dbc_ref, ddt_ref, dal_ref), lands, dst = rest[:4], rest[4:4 + ns], rest[4 + ns]
        n, i = pl.program_id(0), pl.program_id(1)
        if ns:
            sends, arrivals = _scatter_copies(srcs, lands, *rest[5 + ns:])

            @pl.when((n == 0) & (i == 0))
            def _():
                for cp in sends:
                    cp.start()

        @pl.when(i == 0)
        def _():
            dst[...] = jnp.zeros_like(dst)

        @pl.when((i == 0) & (n == 0))
        def _():
            dal_ref[...] = jnp.zeros_like(dal_ref)

        c = _ssd_common(xs_ref, bc_ref, dt_ref, al_ref, reverse, lane0)
        hm = _head_masks()
        reduce_m = _head_reduce(lane0)
        s0_all, ds1_all, dy = st0_ref[...], dst[...], dy_ref[...]
        lane = lax.broadcasted_iota(jnp.int32, (q, 128), 1)
        sub = lax.broadcasted_iota(jnp.int32, (128, q), 0)
        rowacc = jnp.zeros((q, 128), F32)
        colacc_t = jnp.zeros((128, q), F32)
        dv_l, yst_l, dvbar_l, dk_l, dc_l = [], [], [], [], []
        for g in range(SSD_GROUPS):
            sl = slice(g * 256, (g + 1) * 256)
            cg, bg = _mx(c["cm"][:, g * 128:(g + 1) * 128]), _mx(c["bm"][:, g * 128:(g + 1) * 128])
            cb = _dot(cg, bg, _NT)
            vg, dyg, wg, ecg = c["v"][:, sl], dy[:, sl], c["w"][:, sl], c["e_c"][:, sl]
            s0, ds1 = _mx(s0_all[:, sl]), _mx(ds1_all[:, sl])
            dye = _mx(dyg * ecg)
            yst_l.append(_dot(cg, s0) * ecg)
            dcg = _dot(dye, s0, _NT)
            dst[:, sl] = c["e_l"][:, sl] * ds1_all[:, sl] + _dot(cg, dye, _TN)
            vbar = _mx(vg * wg)
            dvbar = _dot(bg, ds1)
            dvbar_l.append(dvbar)
            dvg = dvbar * wg
            dkg = _dot(vbar, ds1, _NT)
            for e in range(4):
                h = 4 * g + e
                m = _ssd_decay(c, h)
                dyh, vh = _mx(jnp.where(hm[e], dyg, 0.0)), _mx(jnp.where(hm[e], vg, 0.0))
                dvg = dvg + _dot(_mx(m * cb), dyh, _TN)
                dcb = _dot(dyh, vh, _NT) * m
                dcbb = _mx(dcb)
                dcg = dcg + _dot(dcbb, bg)
                dkg = dkg + _dot(dcbb, cg, _TN)
                wmat = dcb * cb
                rowacc = jnp.where(lane == lane0 + h, jnp.sum(wmat, axis=1, keepdims=True), rowacc)
                colacc_t = jnp.where(sub == lane0 + h, jnp.sum(wmat, axis=0, keepdims=True), colacc_t)
            dv_l.append(dvg)
            dk_l.append(dkg)
            dc_l.append(dcg)
        dv = jnp.concatenate(dv_l, axis=1)
        yst = jnp.concatenate(yst_l, axis=1)
        dvbar = jnp.concatenate(dvbar_l, axis=1)
        t1 = _dot01(dy * yst, reduce_m, split="a", terms=3)
        t2 = _dot01(c["v"] * c["w"] * dvbar, reduce_m, split="a", terms=3)
        dlast = jnp.sum(t2, axis=0, keepdims=True) + _dot01(
            c["e_l"] * jnp.sum(ds1_all * s0_all, axis=0, keepdims=True), reduce_m, split="a", terms=2)
        dcum = rowacc - colacc_t.T + t1 - t2
        dcum = dcum + jnp.where(lax.broadcasted_iota(jnp.int32, (q, 128), 0) == c["edge"], dlast, 0.0)
        dda = _dot01(c["mask"].astype(F32), dcum, _TN, split="b", terms=3)
        ddt = dda * c["a"] + _dot01(dv * c["xs"], reduce_m, split="a", terms=2)
        dal_ref[...] += jnp.sum(dda * c["dt"], axis=0, keepdims=True) * c["a"]
        dxs = dv * c["dt_x"]
        dbc = jnp.concatenate(dk_l + dc_l, axis=1)
        if nadd:
            for a_ref in adds[:-2]:
                dxs = dxs + a_ref[...]
            dbc = dbc + adds[-2][...]
            ddt = ddt + adds[-1][...]
        ddt_ref[...] = ddt
        dxs_ref[...] = dxs
        dbc_ref[...] = dbc
        if ns:
            @pl.when((n == nbatch - 1) & (i == nc - 1))
            def _():
                for cp in arrivals:
                    cp.wait_recv()
                for cp in sends:
                    cp.wait_send()

    ck = (lambda i: i) if reverse else (lambda i: nc - 1 - i)
    xs_spec = pl.BlockSpec((None, q, 1024), lambda n, i: (n, ck(i), 0))
    bc_spec = pl.BlockSpec((None, q, 1024), lambda n, i: (n, ck(i), 1))
    dt_spec = pl.BlockSpec((None, q, 128), lambda n, i: (n, ck(i), 0))
    al_spec = pl.BlockSpec((1, 128), lambda n, i: (0, 0))
    st_spec = pl.BlockSpec((None, None, 128, 1024), lambda n, i: (n, ck(i), 0, 0))
    return _pcall(body, name=f"ssd_bwd_r{int(reverse)}", grid=(nbatch, nc),
                  in_specs=([xs_spec, bc_spec, dt_spec, al_spec, st_spec, xs_spec] + [xs_spec] * (nadd - 1)
                            + [dt_spec] * bool(nadd) + [ANY] * ns),
                  out_specs=(xs_spec, xs_spec, dt_spec, al_spec) + (ANY,) * ns,
                  out_shape=(jax.ShapeDtypeStruct((nbatch, s, 1024), F32), jax.ShapeDtypeStruct((nbatch, s, 1024), F32),
                             jax.ShapeDtypeStruct((nbatch, s, 128), F32), jax.ShapeDtypeStruct((1, 128), F32))
                  + tuple(jax.ShapeDtypeStruct(c.shape, c.dtype) for c in scatter),
                  scratch_shapes=[pltpu.VMEM((128, 1024), F32)] + (_scatter_scratch(ns) if ns else []),
                  compiler_params=_params())(xbc3, xbc3, dt3, alog, st4, dy3, *add_to, *scatter)


def _gla_block(q, k, g, reverse):
    bq = g.shape[0]
    nsub = bq // HGRN_SUB
    edge = 0 if reverse else bq - 1
    ri = lax.broadcasted_iota(jnp.int32, (bq, bq), 0)
    ci = lax.broadcasted_iota(jnp.int32, (bq, bq), 1)
    rb, cb = jnp.right_shift(ri, HGRN_SUB_SHIFT), jnp.right_shift(ci, HGRN_SUB_SHIFT)
    mask = (ri <= ci) if reverse else (ri >= ci)
    m_within = (mask & (rb == cb)).astype(F32)
    m_before = ((cb > rb) if reverse else (cb < rb)).astype(F32)
    bl = _dot01(m_within, g, split="b", terms=3)
    c = _dot01(m_before, g, split="b", terms=3)
    last = c[edge:edge + 1, :] + bl[edge:edge + 1, :]
    ebl, enbl, ec, elc = jnp.exp(bl), jnp.exp(-bl), jnp.exp(c), jnp.exp(last - c)
    qh = q * HGRN_SCALE * ebl
    kh = k * enbl
    blk = jnp.right_shift(lax.broadcasted_iota(jnp.int32, (bq, 1), 0), HGRN_SUB_SHIFT)
    scale = []
    for i in range(nsub):
        valid = (blk >= i) if reverse else (blk <= i)
        ex = jnp.where(valid, c[i * HGRN_SUB:i * HGRN_SUB + 1, :] - c, 0.0)
        scale.append(jnp.where(valid, jnp.exp(ex), 0.0))
    return dict(bq=bq, nsub=nsub, edge=edge, mask=mask, m_within=m_within, m_before=m_before, ebl=ebl, enbl=enbl, ec=ec,
                elc=elc, e_l=jnp.exp(last), qh=qh, qt=qh * ec, kh=kh, kb=kh * elc, scale=scale)


def _gla_scores(c, hs):
    keys = [_mx(c["kh"][:, hs] * c["scale"][i][:, hs]) for i in range(c["nsub"])]
    rows = [_dot(_mx(c["qh"][i * HGRN_SUB:(i + 1) * HGRN_SUB, hs]), keys[i], _NT) for i in range(c["nsub"])]
    return jnp.where(c["mask"], jnp.concatenate(rows, axis=0), 0.0), keys


def _gla_specs(nbatch, s, w, reverse_order):
    bq = min(HGRN_BLOCK, s)
    nblk = s // bq
    bi = (lambda i: nblk - 1 - i) if reverse_order else (lambda i: i)
    col = lambda cb: pl.BlockSpec((nbatch, bq, w), lambda i: (0, bi(i), cb))
    st_spec = pl.BlockSpec((nbatch, None, 128, w), lambda i: (0, bi(i), 0, 0))
    return bq, nblk, col, st_spec


def _gla_fwd(proj3, l0, l1, reverse):
    nbatch, s, w5 = proj3.shape
    w = w5 // 5
    bq, nblk, col, st_spec = _gla_specs(nbatch, s, w, reverse)
    vec = pl.BlockSpec((1, w), lambda i: (0, 0))

    def body(q_ref, f_ref, v_ref, l0_ref, l1_ref, o_ref, st_ref, st):
        @pl.when(pl.program_id(0) == 0)
        def _():
            st[...] = jnp.zeros_like(st)

        for b in range(nbatch):
            st_ref[b] = st[b]
            k, g = _f_hgrn_pre(f_ref[b], l0_ref[...], l1_ref[...])
            c = _gla_block(q_ref[b], k, g, reverse)
            v = v_ref[b]
            for h in range(HGRN_HEADS):
                hs = slice(h * 128, (h + 1) * 128)
                att, _ = _gla_scores(c, hs)
                vb = _mx(v[:, hs])
                s0 = st[b, :, hs]
                o_ref[b, :, hs] = _dot(_mx(att), vb) + _dot(_mx(c["qt"][:, hs]), _mx(s0), _NT)
                st[b, :, hs] = s0 * c["e_l"][:, hs] + _dot(vb, _mx(c["kb"][:, hs]), _TN)

    return _pcall(body, name=f"gla_fwd_r{int(reverse)}", grid=(nblk,),
                  in_specs=[col(0), col(1 + int(reverse)), col(3), vec, vec], out_specs=(col(0), st_spec),
                  out_shape=(jax.ShapeDtypeStruct((nbatch, s, w), F32), jax.ShapeDtypeStruct((nbatch, nblk, 128, w), F32)),
                  scratch_shapes=[pltpu.VMEM((nbatch, 128, w), F32)], compiler_params=_params())(proj3, proj3, proj3, l0, l1)


def _gla_bwd(proj3, l0, l1, st4, do3, reverse, add_to=None):
    nbatch, s, w5 = proj3.shape
    w = w5 // 5
    bq, nblk, col, st_spec = _gla_specs(nbatch, s, w, not reverse)
    nadd = 0 if add_to is None else 2
    vec = pl.BlockSpec((1, w), lambda i: (0, 0))

    def body(q_ref, f_ref, v_ref, l0_ref, l1_ref, st_ref, do_ref, *rest):
        adds, (dq_ref, df_ref, dv_ref, dl0_ref, dl1_ref, dst) = rest[:nadd], rest[nadd:]

        @pl.when(pl.program_id(0) == 0)
        def _():
            dst[...] = jnp.zeros_like(dst)
            dl0_ref[...] = jnp.zeros_like(dl0_ref)
            dl1_ref[...] = jnp.zeros_like(dl1_ref)

        row = lax.broadcasted_iota(jnp.int32, (bq, 128), 0)
        for b in range(nbatch):
            (k, g), pre_vjp = jax.vjp(_f_hgrn_pre, f_ref[b], l0_ref[...], l1_ref[...])
            c = _gla_block(q_ref[b], k, g, reverse)
            s0_all, ds1_all = st_ref[b], dst[b]
            v, dy = v_ref[b], do_ref[b]
            dbl_l, dc_l, dk_l = [], [], []
            for h in range(HGRN_HEADS):
                hs = slice(h * 128, (h + 1) * 128)
                att, keys = _gla_scores(c, hs)
                qh, qt, kh, kb = c["qh"][:, hs], c["qt"][:, hs], c["kh"][:, hs], c["kb"][:, hs]
                vb, dyb = _mx(v[:, hs]), _mx(dy[:, hs])
                s0, ds1 = s0_all[:, hs], ds1_all[:, hs]
                datt = _mx(jnp.where(c["mask"], _dot(dyb, vb, _NT), 0.0))
                dqh_rows = []
                dkh = jnp.zeros((bq, 128), F32)
                dc = jnp.zeros((bq, 128), F32)
                for i in range(c["nsub"]):
                    rs = slice(i * HGRN_SUB, (i + 1) * HGRN_SUB)
                    dqh_rows.append(_dot(datt[rs], keys[i]))
                    dki = _dot(datt[rs], _mx(qh[rs]), _TN)
                    sc = c["scale"][i][:, hs]
                    dkh = dkh + dki * sc
                    dex = dki * (kh * sc)
                    dc = dc - dex + jnp.where(row == i * HGRN_SUB, jnp.sum(dex, axis=0, keepdims=True), 0.0)
                dqt = _dot(dyb, _mx(s0))
                dkb = _dot(vb, _mx(ds1))
                dv = _dot(_mx(att), dyb, _TN) + _dot(_mx(kb), _mx(ds1), _NT)
                dst[b, :, hs] = c["e_l"][:, hs] * ds1 + _dot(dyb, _mx(qt), _TN)
                dqh = jnp.concatenate(dqh_rows, axis=0) + dqt * c["ec"][:, hs]
                dkh = dkh + dkb * c["elc"][:, hs]
                kbk = dkb * kb
                dlast = jnp.sum(kbk, axis=0, keepdims=True) + c["e_l"][:, hs] * jnp.sum(ds1 * s0, axis=0, keepdims=True)
                at_edge = jnp.where(row == c["edge"], dlast, 0.0)
                dc_l.append(dc + dqt * qt - kbk + at_edge)
                dbl_l.append(dqh * qh - dkh * kh + at_edge)
                dq = dqh * c["ebl"][:, hs] * HGRN_SCALE
                if nadd:
                    dq, dv = dq + adds[0][b, :, hs], dv + adds[1][b, :, hs]
                dq_ref[b, :, hs] = dq.astype(dq_ref.dtype)
                dv_ref[b, :, hs] = dv.astype(dv_ref.dtype)
                dk_l.append(dkh * c["enbl"][:, hs])
            dg = (_dot01(c["m_within"], jnp.concatenate(dbl_l, axis=1), _TN, split="b", terms=2)
                  + _dot01(c["m_before"], jnp.concatenate(dc_l, axis=1), _TN, split="b", terms=2))
            df, d0, d1 = pre_vjp((jnp.concatenate(dk_l, axis=1), dg))
            df_ref[b] = df.astype(df_ref.dtype)
            dl0_ref[...] += d0
            dl1_ref[...] += d1

    shp_sum = jax.ShapeDtypeStruct((nbatch, s, w), BF16 if nadd else F32)
    shp_vec = jax.ShapeDtypeStruct((1, w), F32)
    return _pcall(body, name=f"gla_bwd_r{int(reverse)}", grid=(nblk,),
                  in_specs=[col(0), col(1 + int(reverse)), col(3), vec, vec, st_spec, col(0)] + [col(0)] * nadd,
                  out_specs=(col(0), col(0), col(0), vec, vec),
                  out_shape=(shp_sum, jax.ShapeDtypeStruct((nbatch, s, w), BF16), shp_sum, shp_vec, shp_vec),
                  scratch_shapes=[pltpu.VMEM((nbatch, 128, w), F32)],
                  compiler_params=_params())(proj3, proj3, proj3, l0, l1, st4, do3, *(add_to or ()))


DIRS = (False, True)


def _block_diag(w):
    eye = jnp.eye(16, dtype=w.dtype)
    return (eye[:, None, :, None] * w[:, :, None, :]).reshape(1024, 1024)


def _diag_blocks(m):
    m4 = m.reshape(16, 64, 16, 64)
    return jnp.stack([m4[i, :, i, :] for i in range(16)], axis=0)


def _pad_lanes(v, n=128):
    return jnp.pad(v, [(0, 0)] * (v.ndim - 1) + [(0, n - v.shape[-1])])


def _mlp_fwd(tag, x, nw, w1, w2, carry=None):
    (h,) = _pw_fwd(f"{tag}_norm", _f_norm, [(x, 0)], [(nw, 0)], [BF16], 1024, 1)
    a, r, *got = _mm(f"{tag}_up", h, w1, "nn", relu2=True, carry=carry)
    return _mm(f"{tag}_down", r, w2, "nn", res=x), (h, a, r), got


def _mlp_bwd(tag, x, nw, w1, w2, saved, dxo):
    h, a, r = saved
    dw2 = _mm(f"{tag}_dw2", r, dxo, "tn")
    da = _mm(f"{tag}_da", dxo, w2, "nt", relu2_of=a, out_dtype=BF16)
    dw1 = _mm(f"{tag}_dw1", h, da, "tn", col_shards=4)
    dh = _mm(f"{tag}_dh", da, w1, "nt")
    (dx,), (dnw,) = _pw_bwd(f"{tag}_dnorm", _f_norm, [(x, 0)], [(nw, 0)], [dh], 1024, 1, [0], adds={0: dxo}, tm=ROWS_FWD)
    return dx, dw1, dw2, dnw


def _split_in0(pieces, dt_piece):
    tm = 256

    def body(p0, p1, p2, p3, p4, p5, o_ref):
        full = jnp.concatenate([p0[...], p1[...], p2[...], p3[...], p4[...], p5[:, :32]], axis=1)
        for j in range(4):
            o_ref[j] = full[:, 1288 * j:1288 * (j + 1)]

    blk = pl.BlockSpec((tm, 1024), lambda i: (i, 0))
    return _pcall(body, name="split_in0", grid=(1024 // tm,), in_specs=[blk] * 5 + [pl.BlockSpec((tm, 128), lambda i: (i, 0))],
                  out_specs=pl.BlockSpec((4, tm, 1288), lambda i: (0, i, 0)),
                  out_shape=jax.ShapeDtypeStruct((4, 1024, 1288), F32), compiler_params=_params())(*pieces, dt_piece)


def _assemble_in0(shards):
    tm = 256

    def body(s_ref, m_ref, d_ref):
        full = jnp.concatenate([s_ref[j] for j in range(4)], axis=1)
        m_ref[...] = full[:, :5120]
        d_ref[...] = jnp.concatenate([full[:, 5120:5152], jnp.zeros((tm, 96), full.dtype)], axis=1)

    return _pcall(body, name="assemble_in0", grid=(1024 // tm,), in_specs=[pl.BlockSpec((4, tm, 1288), lambda i: (0, i, 0))],
                  out_specs=(pl.BlockSpec((tm, 5120), lambda i: (i, 0)), pl.BlockSpec((tm, 128), lambda i: (i, 0))),
                  out_shape=(jax.ShapeDtypeStruct((1024, 5120), shards.dtype), jax.ShapeDtypeStruct((1024, 128), shards.dtype)),
                  compiler_params=_params())(shards)


EARLY = ("odd_w_in", "odd_w_out", "mlp_w1_l1", "mlp_w2_l1")
MID = ("even_w_out", "mlp_w1_l0", "mlp_w2_l0")
LATE = ("even_w_in",)


def _local_step(x3, tgt3, w, w_main0, w_dt0, pair_reduce=None, late=None):
    nb, s, d = x3.shape
    carries, arrived = late if late else ({}, None)
    t = nb * s
    x0 = x3.reshape(t, d)
    tgt = tgt3.reshape(t, d)
    grads = {}
    row = lambda v: v.reshape(1, -1)
    to3 = lambda v: v.reshape(nb, s, v.shape[-1])
    to2 = lambda v: v.reshape(-1, v.shape[-1])

    conv_w, conv_b = w["even_conv_w"][0], row(w["even_conv_b"][0])
    nmix0 = row(w["norm_mix"][0])
    (h0,) = _pw_fwd("l0_norm", _f_norm, [(x0, 0)], [(nmix0, 0)], [BF16], 1024, 1)
    proj0 = _mm("l0_proj", h0, w_main0, "nn")
    dt_raw = _mm("l0_proj_dt", h0, w_dt0, "nn")
    conv2, xbc3 = _conv_fwd(to3(proj0), conv_w, conv_b, 0, 2, True)
    u_lru = to2(_conv_fwd(to3(proj0), conv_w, conv_b, 2, 1, False))
    xbc = to2(xbc3)
    dt_bias = _pad_lanes(w["ssd_dt_bias"][0].reshape(1, 32))
    (dt,) = _pw_fwd("l0_dt", _f_softplus, [(dt_raw, 0)], [(dt_bias, 0)], [F32], 128, 1)
    dt3 = to3(dt)
    alog = _pad_lanes(w["ssd_a_log"][0].reshape(1, 32))
    ssd = [_ssd_fwd(xbc3, dt3, alog, r, carry=carries.get(key)) for r, key in zip(DIRS, ("mlp_w1", "mlp_w2"))]
    if late:
        w = {**w, **arrived("mlp_w1", ssd[0][2:]), **arrived("mlp_w2", ssd[1][2:])}
    yf, yb = to2(ssd[0][0]), to2(ssd[1][0])
    dskip = jnp.repeat(w["ssd_d"][0], SSD_HEADDIM).reshape(1, 1024)
    snw = row(w["ssd_norm_w"][0])
    ssd_ins = [(yf, 0), (yb, 0), (xbc, 0), (proj0, 3)]
    (ya,) = _pw_fwd("l0_ssd_post", _f_ssd_post, ssd_ins, [(dskip, 0), (snw, 0)], [BF16], 1024, 1, groups=SSD_GROUPS)
    w_gates = [_block_diag(w[k][0, r]).astype(MXU_DTYPE) for r in range(2) for k in ("lru_w_a", "lru_w_x")]
    pre = [_mm(f"l0_lru_pre{i}", u_lru, wg, "nn") for i, wg in enumerate(w_gates)]
    lru_par = [[(row(w[k][0, r]), 0) for k in ("lru_b_a", "lru_b_x", "lru_lambda")] for r in range(2)]
    lru_ins = [[(pre[2 * r], 0), (pre[2 * r + 1], 0), (u_lru, 0)] for r in range(2)]
    ab = [_pw_fwd(f"l0_lru_gates{r}", _f_lru_gates, lru_ins[r], lru_par[r], [F32, F32], 1024, 1) for r in range(2)]
    hs = [_lru_scan(to3(ab[r][0]), to3(ab[r][1]), DIRS[r]) for r in range(2)]
    lru_post_ins = [(to2(hs[0]), 0), (to2(hs[1]), 0), (proj0, 4)]
    (ybm,) = _pw_fwd("l0_lru_post", _f_lru_post, lru_post_ins, [], [BF16], 1024, 1)
    w_out0 = w["even_w_out"][0]
    x1 = _mm("l0_out_a", ya, w_out0[:1024], "nn", res=x0)
    x1 = _mm("l0_out_b", ybm, w_out0[1024:], "nn", res=x1)
    nmlp0 = row(w["norm_mlp"][0])
    x2, mlp0, got = _mlp_fwd("l0_mlp", x1, nmlp0, w["mlp_w1"][0], w["mlp_w2"][0], carry=carries.get("odd"))
    if late:
        w = {**w, **arrived("odd", got)}

    w_in1 = w["odd_w_in"][0]
    nmix1 = row(w["norm_mix"][1])
    (h1,) = _pw_fwd("l1_norm", _f_norm, [(x2, 0)], [(nmix1, 0)], [BF16], 1024, 1)
    proj1 = _mm("l1_proj", h1, w_in1, "nn")
    proj1_3 = to3(proj1)
    lb0, lb1 = row(w["hgrn_lb_logits"][0]), row(w["hgrn_lb_logits"][1])
    gla = [_gla_fwd(proj1_3, lb0, lb1, r) for r in DIRS]
    hnw = row(w["hgrn_norm_w"][0])
    hpost_ins = [(to2(gla[0][0]), 0), (to2(gla[1][0]), 0), (proj1, 4)]
    (yo,) = _pw_fwd("l1_hgrn_post", _f_hgrn_post, hpost_ins, [(hnw, 0)], [BF16], 1024, 1, groups=HGRN_HEADS)
    w_out1 = w["odd_w_out"][0]
    x3_ = _mm("l1_out", yo, w_out1, "nn", res=x2)
    nmlp1 = row(w["norm_mlp"][1])
    x4, mlp1, _ = _mlp_fwd("l1_mlp", x3_, nmlp1, w["mlp_w1"][1], w["mlp_w2"][1])

    dx4, dnf, loss = _loss_head(x4, tgt, row(w["norm_final"]))
    grads["norm_final"] = dnf.reshape(-1)

    dx3, dw1_1, dw2_1, dnmlp1 = _mlp_bwd("l1_mlp", x3_, nmlp1, w["mlp_w1"][1], w["mlp_w2"][1], mlp1, dx4)
    big = {"odd_w_out": _mm("l1_dwout", yo, dx3, "tn").reshape(4, 256, 1024)}
    dyo = _mm("l1_dyo", dx3, w_out1, "nt")
    (do, dgate1), (dhnw,) = _pw_bwd("l1_hgrn_post_b", _f_hgrn_post, hpost_ins, [(hnw, 0)], [dyo], 1024, 1, [0, 2],
                                    out_dtypes=[F32, BF16], groups=HGRN_HEADS)
    grads["hgrn_norm_w"] = dhnw
    do3 = to3(do)
    gb = [_gla_bwd(proj1_3, lb0, lb1, gla[0][1], do3, False)]
    gb.append(_gla_bwd(proj1_3, lb0, lb1, gla[1][1], do3, True, add_to=(gb[0][0], gb[0][2])))
    grads["hgrn_lb_logits"] = jnp.concatenate([gb[0][3] + gb[1][3], gb[0][4] + gb[1][4]], axis=0)
    dparts1 = [to2(gb[1][0]), to2(gb[0][1]), to2(gb[1][1]), to2(gb[1][2]), dgate1]
    dwin1 = jnp.concatenate([_mm(f"l1_dwin{i}", h1, dp, "tn") for i, dp in enumerate(dparts1)], axis=1)
    big["odd_w_in"] = dwin1.reshape(1024, 4, 1280).transpose(1, 0, 2)
    dh1 = _mm_sum_nt("l1_dh", dparts1, [w_in1[:, i * 1024:(i + 1) * 1024] for i in range(5)])
    (dx2,), (dnmix1,) = _pw_bwd("l1_dnorm", _f_norm, [(x2, 0)], [(nmix1, 0)], [dh1], 1024, 1, [0], adds={0: dx3}, tm=ROWS_FWD)
    big["mlp_w1_l1"], big["mlp_w2_l1"] = dw1_1, dw2_1.reshape(4, 1024, 1024)
    early_sums = tuple(pair_reduce(EARLY, [big[n] for n in EARLY])) if pair_reduce else ()

    dx1, dw1_0, dw2_0, dnmlp0 = _mlp_bwd("l0_mlp", x1, nmlp0, w["mlp_w1"][0], w["mlp_w2"][0], mlp0, dx2)
    big["mlp_w1_l0"], big["mlp_w2_l0"] = dw1_0, dw2_0.reshape(4, 1024, 1024)
    grads["norm_mlp"] = jnp.concatenate([dnmlp0, dnmlp1], axis=0)
    big["even_w_out"] = jnp.concatenate([_mm("l0_dwout_a", ya, dx1, "tn"), _mm("l0_dwout_b", ybm, dx1, "tn")],
                                        axis=0).reshape(4, 512, 1024)
    mid_sums = tuple(pair_reduce(MID, [big[n] for n in MID])) if pair_reduce else ()
    dya = _mm("l0_dya", dx1, w_out0[:1024], "nt")
    dyb = _mm("l0_dyb", dx1, w_out0[1024:], "nt")
    (dh, dgate0), _ = _pw_bwd("l0_lru_post_b", _f_lru_post, lru_post_ins, [], [dyb], 1024, 1, [0, 2], out_dtypes=[F32, BF16])
    dh3 = to3(dh)
    dpre, du_parts, dlru = [], [], {k: [] for k in ("lru_b_a", "lru_b_x", "lru_lambda")}
    for r in range(2):
        g_r, da_r = _lru_scan_bwd(to3(ab[r][0]), hs[r], dh3, DIRS[r])
        (dpa, dpx, du_r), (dba, dbx, dlam) = _pw_bwd(f"l0_lru_gates_b{r}", _f_lru_gates, lru_ins[r], lru_par[r],
                                                     [to2(da_r), to2(g_r)], 1024, 1, [0, 1, 2],
                                                     out_dtypes=[BF16, BF16, F32])
        dpre += [dpa, dpx]
        du_parts.append(du_r)
        dlru["lru_b_a"].append(dba)
        dlru["lru_b_x"].append(dbx)
        dlru["lru_lambda"].append(dlam)
    for k, v in dlru.items():
        grads[k] = jnp.concatenate(v, axis=0)[None]
    dwg = [_diag_blocks(_mm(f"l0_dwgate{i}", u_lru, dp, "tn")) for i, dp in enumerate(dpre)]
    grads["lru_w_a"] = jnp.stack([dwg[0], dwg[2]])[None]
    grads["lru_w_x"] = jnp.stack([dwg[1], dwg[3]])[None]
    du_gate = _mm_sum_nt("l0_du_gate", dpre, w_gates)
    (du,) = _pw_fwd("l0_du", _f_add3, [(du_parts[0], 0), (du_parts[1], 0), (du_gate, 0)], [], [F32], 1024, 1)
    (dy, dxs_skip, dz), (ddskip, dsnw) = _pw_bwd("l0_ssd_post_b", _f_ssd_post, ssd_ins, [(dskip, 0), (snw, 0)], [dya],
                                                 1024, 1, [0, 2, 3], out_dtypes=[F32, F32, BF16], groups=SSD_GROUPS)
    grads["ssd_d"] = ddskip.reshape(SSD_HEADS, SSD_HEADDIM).sum(axis=1)[None]
    grads["ssd_norm_w"] = dsnw
    dy3 = to3(dy)
    sb0 = _ssd_bwd(xbc3, dt3, alog, ssd[0][1], dy3, False, scatter=early_sums)
    sb1 = _ssd_bwd(xbc3, dt3, alog, ssd[1][1], dy3, True, add_to=(sb0[0], to3(dxs_skip), sb0[1], sb0[2]), scatter=mid_sums)
    grads["ssd_a_log"] = (sb0[3] + sb1[3])[:, :32].reshape(1, 2, 16)
    ddt = to2(sb1[2])
    (ddt_raw,), (ddtb,) = _pw_bwd("l0_dt_b", _f_softplus, [(dt_raw, 0)], [(dt_bias, 0)], [ddt], 128, 1, [0])
    grads["ssd_dt_bias"] = ddtb[:, :32].reshape(1, 2, 16)
    cb = [_conv_bwd(sb1[0], to3(proj0), conv_w, 0, conv2), _conv_bwd(sb1[1], to3(proj0), conv_w, 1, conv2),
          _conv_bwd(to3(du), to3(proj0), conv_w, 2)]
    dcw = jnp.concatenate([c_[1] for c_ in cb], axis=1)
    grads["even_conv_w"] = dcw[:4][None]
    grads["even_conv_b"] = dcw[4:5]
    dparts0 = [to2(c_[0]) for c_ in cb] + [dz, dgate0]
    dwin0 = [_mm(f"l0_dwin{i}", h0, dp, "tn") for i, dp in enumerate(dparts0)]
    big["even_w_in"] = _split_in0(dwin0, _mm("l0_dwin_dt", h0, ddt_raw, "tn"))
    dh0 = _mm_sum_nt("l0_dh", dparts0 + [ddt_raw], [w_main0[:, i * 1024:(i + 1) * 1024] for i in range(5)] + [w_dt0])
    (dx0,), (dnmix0,) = _pw_bwd("l0_dnorm", _f_norm, [(x0, 0)], [(nmix0, 0)], [dh0], 1024, 1, [0], adds={0: dx1}, tm=ROWS_FWD)
    grads["norm_mix"] = jnp.concatenate([dnmix0, dnmix1], axis=0)
    return loss, dx0.reshape(nb, s, d), grads, big, (early_sums + mid_sums, sb0[4:] + sb1[4:])


ANY = pl.BlockSpec(memory_space=pl.ANY)


def _place():
    return lax.axis_index("x"), lax.axis_index("y"), lax.axis_index("c")


def _remote(src, dst, send_sems, recv_sems, k, to):
    return pltpu.make_async_remote_copy(src_ref=src, dst_ref=dst, send_sem=send_sems.at[k], recv_sem=recv_sems.at[k],
                                        device_id=to, device_id_type=MESH)


def _gather_start(x_refs, out_refs, send_sems, recv_sems, finish=False):
    n = len(x_refs)
    halves = [r.shape[0] // 2 for r in x_refs]
    x, y, c = _place()
    sibling = (x, y, 1 - c)
    chips = [(1 - x, y), (x, 1 - y), (1 - x, 1 - y)]

    def blk(t, px, py, hc):
        return out_refs[t].at[2 * px + py, pl.ds(hc * halves[t], halves[t]), :]

    def src(t):
        return x_refs[t].at[pl.ds(c * halves[t], halves[t]), :]

    first = [_remote(src(t), blk(t, x, y, c), send_sems, recv_sems, 6 * t + j, (*chip, c))
             for t in range(n) for j, chip in enumerate(chips)]
    if not finish:
        for cp in first:
            cp.start()
        return
    passed = []
    for t in range(n):
        for j, chip in enumerate(chips):
            _remote(src(t), blk(t, *chip, c), send_sems, recv_sems, 6 * t + j, (*chip, c)).wait_recv()
            cp = _remote(blk(t, *chip, c), blk(t, *chip, c), send_sems, recv_sems, 6 * t + 3 + j, sibling)
            cp.start()
            passed.append(cp)
    for t in range(n):
        for j, chip in enumerate(chips):
            _remote(src(t), blk(t, *chip, 1 - c), send_sems, recv_sems, 6 * t + 3 + j, sibling).wait_recv()
    for cp in first + passed:
        cp.wait_send()


_gather_finish = functools.partial(_gather_start, finish=True)


def _gather_carry(shards):
    n = len(shards)
    return (list(shards), [jax.ShapeDtypeStruct((4,) + s.shape, s.dtype) for s in shards],
            [pltpu.SemaphoreType.DMA((6 * n,)), pltpu.SemaphoreType.DMA((6 * n,))], _gather_start, _gather_finish)


def _gather_chips(shards):
    n = len(shards)
    srcs, shapes, scratch, start, finish = _gather_carry(shards)

    def body(*refs):
        start(refs[:n], refs[n:2 * n], *refs[2 * n:])
        finish(refs[:n], refs[n:2 * n], *refs[2 * n:])

    return _pcall(body, name="gather_weights", in_specs=[ANY] * n, out_specs=(ANY,) * n, out_shape=tuple(shapes),
                  scratch_shapes=scratch, compiler_params=_params())(*shards)


def _pair_swap(name, gps):
    n = len(gps)
    halves = [g.shape[1] // 2 for g in gps]

    def body(*refs):
        g_refs, land_refs = refs[:n], refs[n:2 * n]
        send_sems, recv_sems = refs[2 * n:]
        x, y, c = _place()
        cps = [_remote(g_refs[t].at[j, pl.ds((1 - c) * halves[t], halves[t]), :], land_refs[t].at[j], send_sems, recv_sems,
                       4 * t + j, (x, y, 1 - c)) for t in range(n) for j in range(4)]
        for cp in cps:
            cp.start()
        for cp in cps:
            cp.wait()

    return _pcall(body, name=f"pair_swap_{name}", in_specs=[ANY] * n, out_specs=(ANY,) * n,
                  out_shape=tuple(jax.ShapeDtypeStruct((4, h, g.shape[2]), F32) for g, h in zip(gps, halves)),
                  scratch_shapes=[pltpu.SemaphoreType.DMA((4 * n,)), pltpu.SemaphoreType.DMA((4 * n,))],
                  compiler_params=_params())(*gps)


def _pair_add(name, gp, land, cidx):
    _, half, cols = land.shape
    tr = _tile(half, 512)
    nh = half // tr

    def body(c_ref, g_ref, l_ref, o_ref):
        o_ref[...] = (g_ref[...] + l_ref[...]).astype(o_ref.dtype)

    grid_spec = pltpu.PrefetchScalarGridSpec(
        num_scalar_prefetch=1, grid=(4, nh),
        in_specs=[pl.BlockSpec((None, tr, cols), lambda j, i, c: (j, c[0] * nh + i, 0)),
                  pl.BlockSpec((None, tr, cols), lambda j, i, c: (j, i, 0))],
        out_specs=pl.BlockSpec((None, tr, cols), lambda j, i, c: (j, i, 0)))
    return _pcall(body, name=f"pair_add_{name}", grid_spec=grid_spec, out_shape=jax.ShapeDtypeStruct((4, half, cols), BF16),
                  compiler_params=_params())(cidx, gp, land)


def _scatter_copies(s_refs, land_refs, send_sems, recv_sems):
    x, y, c = _place()
    me = 2 * x + y
    chips = [(1 - x, y), (x, 1 - y), (1 - x, 1 - y)]
    pairs = [(t, j, px, py) for t in range(len(s_refs)) for j, (px, py) in enumerate(chips)]
    sends = [_remote(s_refs[t].at[2 * px + py], land_refs[t].at[me], send_sems, recv_sems, 3 * t + j, (px, py, c))
             for t, j, px, py in pairs]
    arrivals = [_remote(s_refs[t].at[me], land_refs[t].at[2 * px + py], send_sems, recv_sems, 3 * t + j, (px, py, c))
                for t, j, px, py in pairs]
    return sends, arrivals


def _scatter_scratch(n):
    return [pltpu.SemaphoreType.DMA((3 * n,)), pltpu.SemaphoreType.DMA((3 * n,))]


def _chip_scatter(name, css):
    n = len(css)

    def body(*refs):
        sends, arrivals = _scatter_copies(refs[:n], refs[n:2 * n], *refs[2 * n:])
        for cp in sends:
            cp.start()
        for cp in arrivals:
            cp.wait_recv()
        for cp in sends:
            cp.wait_send()

    return _pcall(body, name=f"chip_scatter_{name}", in_specs=[ANY] * n, out_specs=(ANY,) * n,
                  out_shape=tuple(jax.ShapeDtypeStruct(s.shape, s.dtype) for s in css),
                  scratch_shapes=_scatter_scratch(n), compiler_params=_params())(*css)


def _chip_sum(name, land):
    _, half, cols = land.shape
    tr = _tile(half, 512)

    def body(l_ref, o_ref):
        o_ref[...] = ((l_ref[0].astype(F32) + l_ref[1].astype(F32)) + l_ref[2].astype(F32)) + l_ref[3].astype(F32)

    return _pcall(body, name=f"chip_sum_{name}", grid=(half // tr,),
                  in_specs=[pl.BlockSpec((4, tr, cols), lambda i: (0, i, 0))],
                  out_specs=pl.BlockSpec((tr, cols), lambda i: (i, 0)),
                  out_shape=jax.ShapeDtypeStruct((half, cols), F32), compiler_params=_params())(land)


def _pair_join(reds):
    n = len(reds)

    def body(*refs):
        r_refs, out_refs = refs[:n], refs[n:2 * n]
        send_sems, recv_sems = refs[2 * n:]
        x, y, c = _place()
        cps = [_remote(r_refs[t], out_refs[t].at[c], send_sems, recv_sems, t, (x, y, 1 - c)) for t in range(n)]
        for cp in cps:
            cp.start()
        for t in range(n):
            _remote(r_refs[t], out_refs[t].at[1 - c], send_sems, recv_sems, t, (x, y, 1 - c)).wait_recv()
        for cp in cps:
            cp.wait_send()

    return _pcall(body, name="grad_pair_join", in_specs=[ANY] * n, out_specs=(ANY,) * n,
                  out_shape=tuple(jax.ShapeDtypeStruct((2,) + r.shape, F32) for r in reds),
                  scratch_shapes=[pltpu.SemaphoreType.DMA((n,)), pltpu.SemaphoreType.DMA((n,))],
                  compiler_params=_params())(*reds)


def _adamw(name, g, w, m, v):
    rows, cols = g.shape
    tr = _tile(rows, 512)

    def body(g_ref, w_ref, m_ref, v_ref, d_ref, mo_ref, vo_ref):
        gv = g_ref[...]
        mn = ADAM_B1 * m_ref[...] + (1.0 - ADAM_B1) * gv
        vn = ADAM_B2 * v_ref[...] + (1.0 - ADAM_B2) * jnp.square(gv)
        m_hat = mn / (1.0 - ADAM_B1 ** ADAM_STEP)
        v_hat = vn / (1.0 - ADAM_B2 ** ADAM_STEP)
        d_ref[...] = -ADAM_LR * (m_hat / (jnp.sqrt(v_hat) + ADAM_EPS) + ADAM_WD * w_ref[...])
        mo_ref[...] = mn
        vo_ref[...] = vn

    blk = pl.BlockSpec((tr, cols), lambda i: (i, 0))
    shp = jax.ShapeDtypeStruct((rows, cols), F32)
    return _pcall(body, name=f"adamw_{name}", grid=(rows // tr,), in_specs=[blk] * 4, out_specs=(blk,) * 3,
                  out_shape=(shp,) * 3, compiler_params=_params())(g, w, m, v)


def _pack(pieces, rows, dtype):
    flat = jnp.concatenate([p.reshape(-1).astype(dtype) for p in pieces])
    return jnp.pad(flat, (0, rows * PACK_COLS - flat.shape[0])).reshape(rows, PACK_COLS)


def _unpack(pack, shapes):
    flat = pack.reshape(-1)
    out, off = [], 0
    for shp in shapes:
        n = math.prod(shp)
        out.append(flat[off:off + n].reshape(shp))
        off += n
    return out


def _shard_of(full, axis, j):
    n = full.shape[axis] // 4
    return lax.slice_in_dim(full, j * n, (j + 1) * n, axis=axis)


def kernel(x, even_w_in, even_conv_w, even_conv_b, ssd_a_log, ssd_dt_bias, ssd_d, ssd_norm_w, lru_w_a, lru_b_a, lru_w_x, lru_b_x, lru_lambda, even_w_out, odd_w_in, hgrn_lb_logits, hgrn_norm_w, odd_w_out, norm_mix, norm_mlp, mlp_w1, mlp_w2, norm_final, loss_target, m_even_w_in, m_even_conv_w, m_even_conv_b, m_ssd_a_log, m_ssd_dt_bias, m_ssd_d, m_ssd_norm_w, m_lru_w_a, m_lru_b_a, m_lru_w_x, m_lru_b_x, m_lru_lambda, m_even_w_out, m_odd_w_in, m_hgrn_lb_logits, m_hgrn_norm_w, m_odd_w_out, m_norm_mix, m_norm_mlp, m_mlp_w1, m_mlp_w2, m_norm_final, v_even_w_in, v_even_conv_w, v_even_conv_b, v_ssd_a_log, v_ssd_dt_bias, v_ssd_d, v_ssd_norm_w, v_lru_w_a, v_lru_b_a, v_lru_w_x, v_lru_b_x, v_lru_lambda, v_even_w_out, v_odd_w_in, v_hgrn_lb_logits, v_hgrn_norm_w, v_odd_w_out, v_norm_mix, v_norm_mlp, v_mlp_w1, v_mlp_w2, v_norm_final):
    names = [n for n, _, _, _ in WEIGHTS]
    w_loc = dict(zip(names, (even_w_in, even_conv_w, even_conv_b, ssd_a_log, ssd_dt_bias, ssd_d, ssd_norm_w, lru_w_a, lru_b_a, lru_w_x, lru_b_x, lru_lambda, even_w_out, odd_w_in, hgrn_lb_logits, hgrn_norm_w, odd_w_out, norm_mix, norm_mlp, mlp_w1, mlp_w2, norm_final)))
    m_loc = dict(zip(names, (m_even_w_in, m_even_conv_w, m_even_conv_b, m_ssd_a_log, m_ssd_dt_bias, m_ssd_d, m_ssd_norm_w, m_lru_w_a, m_lru_b_a, m_lru_w_x, m_lru_b_x, m_lru_lambda, m_even_w_out, m_odd_w_in, m_hgrn_lb_logits, m_hgrn_norm_w, m_odd_w_out, m_norm_mix, m_norm_mlp, m_mlp_w1, m_mlp_w2, m_norm_final)))
    v_loc = dict(zip(names, (v_even_w_in, v_even_conv_w, v_even_conv_b, v_ssd_a_log, v_ssd_dt_bias, v_ssd_d, v_ssd_norm_w, v_lru_w_a, v_lru_b_a, v_lru_w_x, v_lru_b_x, v_lru_lambda, v_even_w_out, v_odd_w_in, v_hgrn_lb_logits, v_hgrn_norm_w, v_odd_w_out, v_norm_mix, v_norm_mlp, v_mlp_w1, v_mlp_w2, v_norm_final)))
    spec = {n: (blk, full, ax) for n, blk, full, ax in WEIGHTS}

    small = [n for n in names if n not in BIG]
    two_d = lambda n, v: v.reshape(BIG_2D[n])

    me = 2 * lax.axis_index("x") + lax.axis_index("y")
    cc = lax.axis_index("c")
    put = lambda whole, part, k: lax.dynamic_update_slice_in_dim(whole, part[None], k, axis=0)
    own = {n: two_d(n, w_loc[n]).astype(BF16) for n in BIG}
    own["small"] = _pack([w_loc[n] for n in SMALL_SHARDED], 16, F32)
    fill = lambda got, keys: [put(g, own[k], me) for g, k in zip(got, keys)]
    first = ("even_w_in", "even_w_out", "small")
    g_in0, g_out0, g_small = fill(_gather_chips([own[k] for k in first]), first)
    w_main0, w_dt0 = _assemble_in0(g_in0)
    w_full = {n: w_loc[n] for n in names if spec[n][2] is None}
    w_full["even_w_out"] = g_out0.reshape(1, 2048, 1024)
    shards = [_unpack(g_small[j], [spec[n][0] for n in SMALL_SHARDED]) for j in range(4)]
    carries = {"mlp_w1": _gather_carry([own["mlp_w1"]]), "mlp_w2": _gather_carry([own["mlp_w2"]]),
               "odd": _gather_carry([own["odd_w_in"], own["odd_w_out"]])}

    def arrived(key, got):
        if key == "odd":
            g_in1, g_out1 = fill(got, ("odd_w_in", "odd_w_out"))
            return {"odd_w_in": jnp.concatenate([g_in1[j] for j in range(4)], axis=1)[None],
                    "odd_w_out": g_out1.reshape(1, 1024, 1024)}
        (g,) = fill(got, (key,))
        axis = 1 if key == "mlp_w1" else 0
        return {key: jnp.stack([jnp.concatenate([g[j, l * 1024:(l + 1) * 1024] for j in range(4)], axis=axis) for l in range(2)])}

    for i, n in enumerate(SMALL_SHARDED):
        w_full[n] = jnp.concatenate([shards[j][i] for j in range(4)], axis=spec[n][2])

    cidx = cc.astype(jnp.int32).reshape(1)

    def pair_reduce(tags, tensors):
        return [_pair_add(tag, g, land, cidx) for tag, g, land in zip(tags, tensors, _pair_swap(tags[0], tensors))]

    loss_vec, grad_x, grads, big, (early_sums, early_landed) = _local_step(
        x, loss_target, w_full, w_main0, w_dt0, pair_reduce, (carries, arrived))
    loss = lax.psum(loss_vec[0, 0], ("x", "y", "c"))

    def dest_pack(j):
        return _pack([grads[n].reshape(spec[n][1]) if spec[n][2] is None else _shard_of(grads[n].reshape(spec[n][1]), spec[n][2], j)
                      for n in small], SMALL_ROWS, F32)

    late_tags = LATE + ("small",)
    late_sums = pair_reduce(late_tags, [big[n] for n in LATE] + [jnp.stack([dest_pack(j) for j in range(4)])])
    tags = EARLY + MID + late_tags
    chip_sums = list(early_sums) + late_sums
    landed = [put(land, lax.dynamic_index_in_dim(cs, me, axis=0, keepdims=False), me)
              for land, cs in zip(list(early_landed) + list(_chip_scatter("late", late_sums)), chip_sums)]
    halves = [_chip_sum(tag, land) for tag, land in zip(tags, landed)]
    red = {tag: put(r, h, cc).reshape(-1, r.shape[-1]) for tag, r, h in zip(tags, _pair_join(halves), halves)}
    for n in ("mlp_w1", "mlp_w2"):
        red[n] = jnp.concatenate([red[n + "_l0"], red[n + "_l1"]], axis=0)

    outs = {}
    for n, g in ((n, red[n]) for n in BIG):
        res = (g, *_adamw(n, g, two_d(n, w_loc[n]), two_d(n, m_loc[n]), two_d(n, v_loc[n])))
        outs[n] = [r.reshape(spec[n][0]) for r in res]
    blocks = [spec[n][0] for n in small]
    wp, mp, vp = (_pack([src[n] for n in small], SMALL_ROWS, F32) for src in (w_loc, m_loc, v_loc))
    res = (red["small"], *_adamw("small", red["small"], wp, mp, vp))
    unpacked = [_unpack(r, blocks) for r in res]
    for i, n in enumerate(small):
        outs[n] = [u[i] for u in unpacked]
    return (loss, grad_x, *[outs[n][k] for k in range(4) for n in names])
```

```python
import functools
import math

import jax
import jax.numpy as jnp
from jax import lax
from jax.experimental import pallas as pl
from jax.experimental.pallas import tpu as pltpu

F32 = jnp.float32
BF16 = jnp.bfloat16
MXU_DTYPE = jnp.bfloat16
MESH = pl.DeviceIdType.MESH

D_MODEL = 1024
EPS = 1e-6
SSD_HEADS = 16
SSD_HEADDIM = 64
HEAD_SHIFT = 6
SSD_GROUPS = 4
SSD_STATE = 128
SSD_CHUNK = 128
LRU_C = 8.0
LRU_ROWS = 256
HGRN_HEADS = 8
HGRN_HEADDIM = 128
HGRN_SUB = 32
HGRN_SUB_SHIFT = 5
HGRN_BLOCK = 128
HGRN_SCALE = HGRN_HEADDIM ** -0.5
CONV_ROWS = 512
ROWS_FWD = 512
ROWS_BWD = 256

ADAM_LR = 0.001
ADAM_B1 = 0.9
ADAM_B2 = 0.999
ADAM_EPS = 1e-08
ADAM_WD = 0.01
ADAM_STEP = 10

VMEM_LIMIT = 56 * 1024 * 1024
PACK_COLS = 1024
SMALL_ROWS = 288

WEIGHTS = (
    ("even_w_in", (1, 1024, 1288), (1, 1024, 5152), 2),
    ("even_conv_w", (1, 4, 768), (1, 4, 3072), 2),
    ("even_conv_b", (1, 3072), (1, 3072), None),
    ("ssd_a_log", (1, 2, 16), (1, 2, 16), None),
    ("ssd_dt_bias", (1, 2, 16), (1, 2, 16), None),
    ("ssd_d", (1, 16), (1, 16), None),
    ("ssd_norm_w", (1, 1024), (1, 1024), None),
    ("lru_w_a", (1, 2, 16, 64, 64), (1, 2, 16, 64, 64), None),
    ("lru_b_a", (1, 2, 256), (1, 2, 1024), 2),
    ("lru_w_x", (1, 2, 16, 64, 64), (1, 2, 16, 64, 64), None),
    ("lru_b_x", (1, 2, 256), (1, 2, 1024), 2),
    ("lru_lambda", (1, 2, 256), (1, 2, 1024), 2),
    ("even_w_out", (1, 512, 1024), (1, 2048, 1024), 1),
    ("odd_w_in", (1, 1024, 1280), (1, 1024, 5120), 2),
    ("hgrn_lb_logits", (2, 1024), (2, 1024), None),
    ("hgrn_norm_w", (1, 256), (1, 1024), 1),
    ("odd_w_out", (1, 256, 1024), (1, 1024, 1024), 1),
    ("norm_mix", (2, 1024), (2, 1024), None),
    ("norm_mlp", (2, 1024), (2, 1024), None),
    ("mlp_w1", (2, 1024, 1024), (2, 1024, 4096), 2),
    ("mlp_w2", (2, 1024, 1024), (2, 4096, 1024), 1),
    ("norm_final", (1024,), (1024,), None),
)
BIG = ("even_w_in", "even_w_out", "odd_w_in", "odd_w_out", "mlp_w1", "mlp_w2")
BIG_2D = {"even_w_in": (1024, 1288), "even_w_out": (512, 1024), "odd_w_in": (1024, 1280), "odd_w_out": (256, 1024),
          "mlp_w1": (2048, 1024), "mlp_w2": (2048, 1024)}
SMALL_SHARDED = ("even_conv_w", "lru_b_a", "lru_b_x", "lru_lambda", "hgrn_norm_w")


def _pcall(body, carry=None, **kw):
    if carry is not None:
        srcs, shapes, scratch, start, finish = carry
        grid, inner = kw["grid"], body
        as_tuple = lambda v: tuple(v) if isinstance(v, (tuple, list)) else (v,)
        out_specs, out_shape, own_scratch = as_tuple(kw["out_specs"]), as_tuple(kw["out_shape"]), list(kw.get("scratch_shapes", ()))
        a = len(kw["in_specs"])
        b = a + len(srcs)
        c = b + len(out_specs)
        d = c + len(shapes)
        e = d + len(own_scratch)

        def body(*refs):
            ids = [pl.program_id(ax) for ax in range(len(grid))]
            first = functools.reduce(jnp.logical_and, [i == 0 for i in ids])
            last = functools.reduce(jnp.logical_and, [i == g - 1 for i, g in zip(ids, grid)])
            pl.when(first)(lambda: start(refs[a:b], refs[c:d], *refs[e:]))
            inner(*refs[:a], *refs[b:c], *refs[d:e])
            pl.when(last)(lambda: finish(refs[a:b], refs[c:d], *refs[e:]))

        kw = dict(kw, in_specs=list(kw["in_specs"]) + [ANY] * len(srcs), out_specs=out_specs + (ANY,) * len(shapes),
                  out_shape=out_shape + tuple(shapes), scratch_shapes=own_scratch + list(scratch))
    return pl.pallas_call(body, **kw)


def _params(**kw):
    return pltpu.CompilerParams(vmem_limit_bytes=VMEM_LIMIT, **kw)


def _tile(n, pref):
    if n <= pref:
        return n
    t = (pref // 128) * 128
    while n % t:
        t -= 128
    return t


def _dot(a, b, dims=(((1,), (0,)), ((), ()))):
    return lax.dot_general(a, b, dims, preferred_element_type=F32)


_NN = (((1,), (0,)), ((), ()))
_NT = (((1,), (1,)), ((), ()))
_TN = (((0,), (0,)), ((), ()))


def _mx(v):
    return v.astype(MXU_DTYPE)


def _dot01(a, b, dims=_NN, *, split, terms):
    acc, rest = None, (a if split == "a" else b)
    for _ in range(terms):
        piece = _mx(rest)
        part = _dot(piece, _mx(b), dims) if split == "a" else _dot(_mx(a), piece, dims)
        acc = part if acc is None else acc + part
        rest = rest - piece.astype(F32)
    return acc


def _mm(name, a, b, mode, *, out_dtype=F32, res=None, relu2=False, relu2_of=None, col_shards=1, carry=None):
    shards = b.shape[0] if b.ndim == 3 else 0
    b2 = b.shape[1:] if shards else b.shape
    if mode == "nn":
        (m, kk), n = a.shape, b2[1] * max(shards, 1)
    elif mode == "nt":
        (m, kk), n = a.shape, b2[0]
    else:
        (kk, m), (_, n) = a.shape, b.shape
    assert res is None or relu2_of is None
    tk_pref = 1024
    if mode == "tn" and a.dtype.itemsize == 2 and b.dtype.itemsize == 2:
        tk_pref = 2048
    tm, tn, tk = _tile(m, 1024), _tile(n // col_shards, 1024), _tile(kk, tk_pref)
    nk = kk // tk
    dims = {"nn": _NN, "nt": _NT, "tn": _TN}[mode]
    a_spec = pl.BlockSpec((tk, tm), lambda i, j, k: (k, i)) if mode == "tn" else pl.BlockSpec((tm, tk), lambda i, j, k: (i, k))
    b_spec = pl.BlockSpec((tn, tk), lambda i, j, k: (j, k)) if mode == "nt" else pl.BlockSpec((tk, tn), lambda i, j, k: (k, j))
    if shards and mode == "nn":
        assert tn == b2[1]
        b_spec = pl.BlockSpec((None, tk, tn), lambda i, j, k: (j, k, 0))
    if shards and mode == "nt":
        assert tk == b2[1]
        b_spec = pl.BlockSpec((None, tn, tk), lambda i, j, k: (k, j, 0))
    o_spec = pl.BlockSpec((tm, tn), lambda i, j, k: (i, j))
    o_shape = (m, n)
    if col_shards > 1:
        assert tn * col_shards == n and res is None and not relu2
        o_spec = pl.BlockSpec((None, tm, tn), lambda i, j, k: (j, i, 0))
        o_shape = (col_shards, m, tn)
    extra = res if res is not None else relu2_of
    has_res = extra is not None

    def body(*refs):
        a_ref, b_ref = refs[0], refs[1]
        res_ref = refs[2] if has_res else None
        outs = refs[2 + has_res:2 + has_res + 1 + relu2]

        def finish(r):
            if res is not None:
                r = r + res_ref[...]
            if relu2_of is not None:
                r = r * (2.0 * jnp.maximum(res_ref[...].astype(F32), 0.0))
            if relu2:
                outs[0][...] = r.astype(outs[0].dtype)
                outs[1][...] = jnp.square(jnp.maximum(r, 0.0)).astype(outs[1].dtype)
            else:
                outs[0][...] = r.astype(outs[0].dtype)

        prod = _dot(_mx(a_ref[...]), _mx(b_ref[...]), dims)
        if nk == 1:
            finish(prod)
            return
        acc = refs[-1]
        k = pl.program_id(2)

        @pl.when(k == 0)
        def _():
            acc[...] = prod

        @pl.when(k > 0)
        def _():
            acc[...] += prod

        @pl.when(k == nk - 1)
        def _():
            finish(acc[...])

    in_specs = [a_spec, b_spec] + ([o_spec] if has_res else [])
    if relu2:
        out_shape = (jax.ShapeDtypeStruct((m, n), BF16), jax.ShapeDtypeStruct((m, n), BF16))
        out_specs = (o_spec, o_spec)
    else:
        out_shape = jax.ShapeDtypeStruct(o_shape, out_dtype)
        out_specs = o_spec
    args = (a, b) + ((extra,) if has_res else ()) + (tuple(carry[0]) if carry else ())
    return _pcall(body, carry=carry, name=name, grid=(m // tm, n // tn, nk), in_specs=in_specs, out_specs=out_specs,
                  out_shape=out_shape, scratch_shapes=[pltpu.VMEM((tm, tn), F32)] if nk > 1 else [],
                  compiler_params=_params())(*args)


def _mm_sum_nt(name, parts, wblocks):
    m, n, npart = parts[0].shape[0], wblocks[0][0].shape[0], len(parts)
    tm, tn = _tile(m, 512), _tile(n, 1024)

    def body(*refs):
        acc = _dot(_mx(refs[0][...]), _mx(refs[npart][...]), _NT)
        for k in range(1, npart):
            acc = acc + _dot(_mx(refs[k][...]), _mx(refs[npart + k][...]), _NT)
        refs[-1][...] = acc

    in_specs = [pl.BlockSpec((tm, p.shape[1]), lambda i, j: (i, 0)) for p in parts]
    in_specs += [pl.BlockSpec((tn, p.shape[1]), lambda i, j, cb=cb: (j, cb)) for p, (_, cb) in zip(parts, wblocks)]
    return _pcall(body, name=name, grid=(m // tm, n // tn), in_specs=in_specs, out_specs=pl.BlockSpec((tm, tn), lambda i, j: (i, j)),
                  out_shape=jax.ShapeDtypeStruct((m, n), F32), compiler_params=_params())(*parts, *[w for w, _ in wblocks])


def _pw_fwd(name, f, ins, params, out_dtypes, tc, ncol, tm=ROWS_FWD, groups=1):
    t = ins[0][0].shape[0]
    tm = min(tm, t)
    ni, npar = len(ins), len(params)
    gw = tc // groups

    def body(*refs):
        for g in range(groups):
            sl = slice(g * gw, (g + 1) * gw)
            vals = f(*[r[:, sl].astype(F32) for r in refs[:ni]], *[r[:, sl] for r in refs[ni:ni + npar]])
            for o, v in zip(refs[ni + npar:], vals):
                o[:, sl] = v.astype(o.dtype)

    in_specs = [pl.BlockSpec((tm, tc), lambda j, i, off=off: (i, off + j)) for _, off in ins]
    in_specs += [pl.BlockSpec((1, tc), lambda j, i, off=off: (0, off + j)) for _, off in params]
    out_specs = tuple(pl.BlockSpec((tm, tc), lambda j, i: (i, j)) for _ in out_dtypes)
    out_shape = tuple(jax.ShapeDtypeStruct((t, ncol * tc), d) for d in out_dtypes)
    return _pcall(body, name=name, grid=(ncol, t // tm), in_specs=in_specs, out_specs=out_specs, out_shape=out_shape,
                  compiler_params=_params())(*[a for a, _ in ins], *[p for p, _ in params])


def _pw_bwd(name, f, ins, params, douts, tc, ncol, want, adds=None, tm=ROWS_BWD, out_dtypes=None, groups=1):
    adds = adds or {}
    out_dtypes = out_dtypes or [F32] * len(want)
    t = ins[0][0].shape[0]
    tm = min(tm, t)
    ni, npar, nd, na = len(ins), len(params), len(douts), len(adds)
    add_keys = sorted(adds)
    gw = tc // groups

    def body(*refs):
        in_refs, p_refs = refs[:ni], refs[ni:ni + npar]
        d_refs = refs[ni + npar:ni + npar + nd]
        a_refs = refs[ni + npar + nd:ni + npar + nd + na]
        o_refs = refs[ni + npar + nd + na:]
        for p in range(npar):
            @pl.when(pl.program_id(1) == 0)
            def _(o=o_refs[len(want) + p]):
                o[...] = jnp.zeros_like(o)

        for g in range(groups):
            sl = slice(g * gw, (g + 1) * gw)
            _, vjp = jax.vjp(f, *[r[:, sl].astype(F32) for r in in_refs], *[r[:, sl] for r in p_refs])
            cts = vjp(tuple(d[:, sl].astype(F32) for d in d_refs))
            for o, kidx in zip(o_refs[:len(want)], want):
                v = cts[kidx]
                if kidx in adds:
                    v = v + a_refs[add_keys.index(kidx)][:, sl]
                o[:, sl] = v.astype(o.dtype)
            for p in range(npar):
                o_refs[len(want) + p][:, sl] += cts[ni + p]

    in_specs = [pl.BlockSpec((tm, tc), lambda j, i, off=off: (i, off + j)) for _, off in ins]
    in_specs += [pl.BlockSpec((1, tc), lambda j, i, off=off: (0, off + j)) for _, off in params]
    in_specs += [pl.BlockSpec((tm, tc), lambda j, i: (i, j)) for _ in range(nd + na)]
    out_specs = tuple([pl.BlockSpec((tm, tc), lambda j, i: (i, j)) for _ in want]
                      + [pl.BlockSpec((1, tc), lambda j, i: (0, j)) for _ in params])
    out_shape = tuple([jax.ShapeDtypeStruct((t, ncol * tc), dt) for dt in out_dtypes]
                      + [jax.ShapeDtypeStruct((1, ncol * tc), F32) for _ in params])
    res = _pcall(body, name=name, grid=(ncol, t // tm), in_specs=in_specs, out_specs=out_specs, out_shape=out_shape,
                 compiler_params=_params())(*[a for a, _ in ins], *[p for p, _ in params], *douts, *[adds[k] for k in add_keys])
    return list(res[:len(want)]), list(res[len(want):])


def _rms(x, g):
    return (x * lax.rsqrt(jnp.mean(x * x, axis=-1, keepdims=True) + EPS)) * g


def _f_norm(x, g):
    return (_rms(x, g),)


def _f_softplus(d, b):
    return (jax.nn.softplus(d + b),)


def _f_add3(a, b, c):
    return (a + b + c,)


def _f_ssd_post(yf, yb, xs, z, dskip, nw):
    u = (yf + yb + dskip * xs) * jax.nn.silu(z)
    return (_rms(u, nw),)


def _neg_expm1(v):
    t = jnp.tanh(0.5 * v)
    return -2.0 * t / (1.0 - t)


def _f_lru_gates(pre_a, pre_x, u, ba, bx, lam):
    rg = jax.nn.sigmoid(pre_a + ba)
    ig = jax.nn.sigmoid(pre_x + bx)
    log_a = -LRU_C * rg * jax.nn.softplus(-lam)
    return jnp.exp(log_a), jnp.sqrt(_neg_expm1(2.0 * log_a)) * (ig * u)


def _f_lru_post(hf, hb, gate):
    return ((hf + hb) * jax.nn.gelu(gate),)


def _f_hgrn_pre(fr, l0, l1):
    lb = jax.nn.sigmoid(l1 - l0)
    k = (1.0 - lb) * jax.nn.sigmoid(-fr)
    return k, jnp.log1p(-k)


def _f_hgrn_post(of, ob, gate, nw):
    return (_rms(of + ob, nw) * jax.nn.silu(gate),)


def _loss_head(x, tgt, g, tm=ROWS_FWD):
    t, d = x.shape
    tm = min(tm, t)

    def body(x_ref, t_ref, g_ref, dx_ref, dg_ref, loss_ref):
        tv = t_ref[...]

        def lf(xv, gv):
            return 0.5 * jnp.sum(jnp.mean(jnp.square(_rms(xv, gv) - tv), axis=-1))

        val, vjp = jax.vjp(lf, x_ref[...], g_ref[...])
        dx, dg = vjp(jnp.ones((), F32))
        dx_ref[...] = dx

        @pl.when(pl.program_id(0) == 0)
        def _():
            dg_ref[...] = jnp.zeros_like(dg_ref)
            loss_ref[...] = jnp.zeros_like(loss_ref)

        dg_ref[...] += dg
        loss_ref[...] += jnp.full(loss_ref.shape, val, F32)

    row = pl.BlockSpec((tm, d), lambda i: (i, 0))
    vec = pl.BlockSpec((1, d), lambda i: (0, 0))
    return _pcall(body, name="loss_head", grid=(t // tm,), in_specs=[row, row, vec],
                  out_specs=(row, vec, pl.BlockSpec((1, 128), lambda i: (0, 0))),
                  out_shape=(jax.ShapeDtypeStruct((t, d), F32), jax.ShapeDtypeStruct((1, d), F32),
                             jax.ShapeDtypeStruct((1, 128), F32)), compiler_params=_params())(x, tgt, g)


def _shifted(x, d, prev, nxt, first, last):
    r = x.shape[0]
    row = lax.broadcasted_iota(jnp.int32, x.shape, 0)
    if d < 0:
        out = pltpu.roll(x, -d, 0)
        for q in range(-d):
            pv = jnp.where(first, 0.0, prev[8 + d + q:8 + d + q + 1, :])
            out = jnp.where(row == q, pv, out)
        return out
    out = pltpu.roll(x, r - d, 0)
    for q in range(d):
        nv = jnp.where(last, 0.0, nxt[q:q + 1, :])
        out = jnp.where(row == r - d + q, nv, out)
    return out


def _conv_fwd(p3, w, b, col0, ncol, silu, tc=1024):
    nbatch, s, _ = p3.shape
    ts = min(CONV_ROWS, s)
    nblk = s // ts

    def body(x_ref, pv_ref, nx_ref, w_ref, b_ref, o_ref, *act_ref):
        i = pl.program_id(1)
        first, last = i == 0, i == nblk - 1
        x, pv, nx = x_ref[...], pv_ref[...], nx_ref[...]
        wv = w_ref[...]
        out = b_ref[...] + wv[1:2] * x
        out = out + wv[0:1] * _shifted(x, -1, pv, nx, first, last)
        out = out + wv[2:3] * _shifted(x, 1, pv, nx, first, last)
        out = out + wv[3:4] * _shifted(x, 2, pv, nx, first, last)
        o_ref[...] = out
        if silu:
            act_ref[0][...] = jax.nn.silu(out)

    nb8 = s // 8
    cur = pl.BlockSpec((None, ts, tc), lambda n, i, j: (n, i, col0 + j))
    prev = pl.BlockSpec((None, 8, tc), lambda n, i, j: (n, jnp.maximum(i * (ts // 8) - 1, 0), col0 + j))
    nxt = pl.BlockSpec((None, 8, tc), lambda n, i, j: (n, jnp.minimum((i + 1) * (ts // 8), nb8 - 1), col0 + j))
    out = pl.BlockSpec((None, ts, tc), lambda n, i, j: (n, i, j))
    shp = jax.ShapeDtypeStruct((nbatch, s, ncol * tc), F32)
    return _pcall(body, name=f"conv_fwd{col0}", grid=(nbatch, nblk, ncol),
                  in_specs=[cur, prev, nxt, pl.BlockSpec((4, tc), lambda n, i, j: (0, col0 + j)),
                            pl.BlockSpec((1, tc), lambda n, i, j: (0, col0 + j))],
                  out_specs=(out, out) if silu else out, out_shape=(shp, shp) if silu else shp,
                  compiler_params=_params())(p3, p3, p3, w, b)


def _conv_bwd(dc3, p3, w, col, conv3=None):
    nbatch, s, tc = dc3.shape
    ts = min(CONV_ROWS, s)
    nblk = s // ts
    silu = conv3 is not None

    def body(d_ref, dpv_ref, dnx_ref, x_ref, pv_ref, nx_ref, w_ref, *rest):
        n, i = pl.program_id(0), pl.program_id(1)
        first, last = i == 0, i == nblk - 1
        d, dpv, dnx = d_ref[...], dpv_ref[...], dnx_ref[...]
        if silu:
            d, dpv, dnx = [jax.vjp(jax.nn.silu, c_ref[...])[1](t)[0] for c_ref, t in zip(rest[:3], (d, dpv, dnx))]
        dx_ref, dw_ref = rest[3 * silu:]
        x, pv, nx = x_ref[...], pv_ref[...], nx_ref[...]
        wv = w_ref[...]
        dx = wv[1:2] * d
        dx = dx + wv[0:1] * _shifted(d, 1, dpv, dnx, first, last)
        dx = dx + wv[2:3] * _shifted(d, -1, dpv, dnx, first, last)
        dx = dx + wv[3:4] * _shifted(d, -2, dpv, dnx, first, last)
        dx_ref[...] = dx.astype(dx_ref.dtype)

        @pl.when((n == 0) & (i == 0))
        def _():
            dw_ref[...] = jnp.zeros_like(dw_ref)

        dw_ref[0:1, :] += jnp.sum(d * _shifted(x, -1, pv, nx, first, last), axis=0, keepdims=True)
        dw_ref[1:2, :] += jnp.sum(d * x, axis=0, keepdims=True)
        dw_ref[2:3, :] += jnp.sum(d * _shifted(x, 1, pv, nx, first, last), axis=0, keepdims=True)
        dw_ref[3:4, :] += jnp.sum(d * _shifted(x, 2, pv, nx, first, last), axis=0, keepdims=True)
        dw_ref[4:5, :] += jnp.sum(d, axis=0, keepdims=True)

    nb8 = s // 8

    def specs(j):
        cur = pl.BlockSpec((None, ts, tc), lambda n, i: (n, i, j))
        prev = pl.BlockSpec((None, 8, tc), lambda n, i: (n, jnp.maximum(i * (ts // 8) - 1, 0), j))
        nxt = pl.BlockSpec((None, 8, tc), lambda n, i: (n, jnp.minimum((i + 1) * (ts // 8), nb8 - 1), j))
        return [cur, prev, nxt]

    return _pcall(body, name=f"conv_bwd{col}", grid=(nbatch, nblk),
                  in_specs=specs(0) + specs(col) + [pl.BlockSpec((4, tc), lambda n, i: (0, col))] + specs(col) * silu,
                  out_specs=(specs(0)[0], pl.BlockSpec((8, tc), lambda n, i: (0, 0))),
                  out_shape=(jax.ShapeDtypeStruct((nbatch, s, tc), BF16), jax.ShapeDtypeStruct((8, tc), F32)),
                  compiler_params=_params())(dc3, dc3, dc3, p3, p3, p3, w, *([conv3] * 3 * silu))


def _block_scan(coef, inp, reverse):
    r = coef.shape[0]
    row = lax.broadcasted_iota(jnp.int32, coef.shape, 0)
    a, b = coef, inp
    d = 1
    while d < r:
        if reverse:
            keep = row < r - d
            a_sh, b_sh = pltpu.roll(a, r - d, 0), pltpu.roll(b, r - d, 0)
        else:
            keep = row >= d
            a_sh, b_sh = pltpu.roll(a, d, 0), pltpu.roll(b, d, 0)
        b = b + a * jnp.where(keep, b_sh, 0.0)
        a = a * jnp.where(keep, a_sh, 1.0)
        d *= 2
    return a, b


def _lru_scan(a3, b3, reverse):
    nbatch, s, w = a3.shape
    ts = min(LRU_ROWS, s)
    nblk = s // ts
    edge = 0 if reverse else ts - 1

    def body(a_ref, b_ref, h_ref, carry):
        @pl.when(pl.program_id(1) == 0)
        def _():
            carry[...] = jnp.zeros_like(carry)

        ca, hb = _block_scan(a_ref[...], b_ref[...], reverse)
        h = hb + ca * carry[0:1, :]
        h_ref[...] = h
        carry[0:1, :] = h[edge:edge + 1, :]

    blk = pl.BlockSpec((None, ts, w), (lambda n, i: (n, nblk - 1 - i, 0)) if reverse else (lambda n, i: (n, i, 0)))
    return _pcall(body, name=f"lru_scan_r{int(reverse)}", grid=(nbatch, nblk), in_specs=[blk, blk], out_specs=blk,
                  out_shape=jax.ShapeDtypeStruct((nbatch, s, w), F32), scratch_shapes=[pltpu.VMEM((8, w), F32)],
                  compiler_params=_params())(a3, b3)


def _lru_scan_bwd(a3, h3, dh3, reverse):
    nbatch, s, w = a3.shape
    ts = min(LRU_ROWS, s)
    nblk = s // ts
    nb8 = s // 8
    tpb = ts // 8

    def body(a_ref, aa_ref, h_ref, hh_ref, dh_ref, g_ref, da_ref, carry):
        i = pl.program_id(1)

        @pl.when(i == 0)
        def _():
            carry[...] = jnp.zeros_like(carry)

        a, h = a_ref[...], h_ref[...]
        row = lax.broadcasted_iota(jnp.int32, a.shape, 0)
        if reverse:
            a_edge = jnp.where(i == 0, 0.0, aa_ref[7:8, :])
            c = jnp.where(row == 0, a_edge, pltpu.roll(a, 1, 0))
            h_edge = jnp.where(i == nblk - 1, 0.0, hh_ref[0:1, :])
            h_sh = jnp.where(row == ts - 1, h_edge, pltpu.roll(h, ts - 1, 0))
        else:
            a_edge = jnp.where(i == 0, 0.0, aa_ref[0:1, :])
            c = jnp.where(row == ts - 1, a_edge, pltpu.roll(a, ts - 1, 0))
            h_edge = jnp.where(i == nblk - 1, 0.0, hh_ref[7:8, :])
            h_sh = jnp.where(row == 0, h_edge, pltpu.roll(h, 1, 0))
        cc, gb = _block_scan(c, dh_ref[...], not reverse)
        g = gb + cc * carry[0:1, :]
        g_ref[...] = g
        carry[0:1, :] = g[ts - 1:ts, :] if reverse else g[0:1, :]
        da_ref[...] = g * h_sh

    if reverse:
        bi = lambda i: i
    else:
        bi = lambda i: nblk - 1 - i
    blk = pl.BlockSpec((None, ts, w), lambda n, i: (n, bi(i), 0))
    before = pl.BlockSpec((None, 8, w), lambda n, i: (n, jnp.maximum(bi(i) * tpb - 1, 0), 0))
    after = pl.BlockSpec((None, 8, w), lambda n, i: (n, jnp.minimum((bi(i) + 1) * tpb, nb8 - 1), 0))
    a_tile, h_tile = (before, after) if reverse else (after, before)
    return _pcall(body, name=f"lru_scan_bwd_r{int(reverse)}", grid=(nbatch, nblk), in_specs=[blk, a_tile, blk, h_tile, blk],
                  out_specs=(blk, blk),
                  out_shape=(jax.ShapeDtypeStruct((nbatch, s, w), F32), jax.ShapeDtypeStruct((nbatch, s, w), F32)),
                  scratch_shapes=[pltpu.VMEM((8, w), F32)], compiler_params=_params())(a3, a3, h3, h3, dh3)


def _head_expand(lane0):
    return (jnp.right_shift(lax.broadcasted_iota(jnp.int32, (128, 1024), 1), HEAD_SHIFT) + lane0
            == lax.broadcasted_iota(jnp.int32, (128, 1024), 0)).astype(F32)


def _head_reduce(lane0):
    return (jnp.right_shift(lax.broadcasted_iota(jnp.int32, (1024, 128), 0), HEAD_SHIFT) + lane0
            == lax.broadcasted_iota(jnp.int32, (1024, 128), 1)).astype(F32)


def _time_mask(q, reverse):
    ri = lax.broadcasted_iota(jnp.int32, (q, q), 0)
    ci = lax.broadcasted_iota(jnp.int32, (q, q), 1)
    return (ri <= ci) if reverse else (ri >= ci)


def _ssd_common(xs_ref, bc_ref, dt_ref, al_ref, reverse, lane0):
    q = xs_ref.shape[0]
    edge = 0 if reverse else q - 1
    dt = dt_ref[...]
    a = -jnp.exp(al_ref[...])
    mask = _time_mask(q, reverse)
    expand = _head_expand(lane0)
    cum = _dot01(mask.astype(F32), dt * a, split="b", terms=3)
    cum_x = _dot01(cum, expand, split="a", terms=3)
    dt_x = _dot01(dt, expand, split="a", terms=2)
    last_x = cum_x[edge:edge + 1, :]
    xs = xs_ref[...]
    bc = bc_ref[...]
    return dict(q=q, edge=edge, lane0=lane0, dt=dt, a=a, mask=mask, cum_t=cum.T, cum_x=cum_x, dt_x=dt_x, xs=xs,
                v=xs * dt_x, e_c=jnp.exp(cum_x), w=jnp.exp(last_x - cum_x), e_l=jnp.exp(last_x),
                bm=bc[:, :512], cm=bc[:, 512:])


def _ssd_decay(c, h):
    row = c["lane0"] + h
    seg = c["cum_x"][:, h * SSD_HEADDIM:h * SSD_HEADDIM + 1] - c["cum_t"][row:row + 1, :]
    return jnp.where(c["mask"], jnp.exp(jnp.minimum(seg, 0.0)), 0.0)


def _head_masks():
    lane = jnp.right_shift(lax.broadcasted_iota(jnp.int32, (1, 256), 1), HEAD_SHIFT)
    return [lane == e for e in range(4)]


def _ssd_fwd(xbc3, dt3, alog, reverse, carry=None):
    nbatch, s, _ = xbc3.shape
    q = min(SSD_CHUNK, s)
    nc = s // q
    lane0 = SSD_HEADS * int(reverse)

    def body(xs_ref, bc_ref, dt_ref, al_ref, y_ref, st_ref, st):
        @pl.when(pl.program_id(1) == 0)
        def _():
            st[...] = jnp.zeros_like(st)

        st_ref[...] = st[...]
        c = _ssd_common(xs_ref, bc_ref, dt_ref, al_ref, reverse, lane0)
        hm = _head_masks()
        for g in range(SSD_GROUPS):
            sl = slice(g * 256, (g + 1) * 256)
            cg, bg = _mx(c["cm"][:, g * 128:(g + 1) * 128]), _mx(c["bm"][:, g * 128:(g + 1) * 128])
            cb = _dot(cg, bg, _NT)
            vg = c["v"][:, sl]
            s0 = st[:, sl]
            yg = _dot(cg, _mx(s0)) * c["e_c"][:, sl]
            for e in range(4):
                m = _ssd_decay(c, 4 * g + e) * cb
                yg = yg + _dot(_mx(m), _mx(jnp.where(hm[e], vg, 0.0)))
            y_ref[:, sl] = yg
            st[:, sl] = c["e_l"][:, sl] * s0 + _dot(bg, _mx(vg * c["w"][:, sl]), _TN)

    ck = (lambda i: nc - 1 - i) if reverse else (lambda i: i)
    xs_spec = pl.BlockSpec((None, q, 1024), lambda n, i: (n, ck(i), 0))
    bc_spec = pl.BlockSpec((None, q, 1024), lambda n, i: (n, ck(i), 1))
    dt_spec = pl.BlockSpec((None, q, 128), lambda n, i: (n, ck(i), 0))
    al_spec = pl.BlockSpec((1, 128), lambda n, i: (0, 0))
    st_spec = pl.BlockSpec((None, None, 128, 1024), lambda n, i: (n, ck(i), 0, 0))
    return _pcall(body, carry=carry, name=f"ssd_fwd_r{int(reverse)}", grid=(nbatch, nc),
                  in_specs=[xs_spec, bc_spec, dt_spec, al_spec], out_specs=(xs_spec, st_spec),
                  out_shape=(jax.ShapeDtypeStruct((nbatch, s, 1024), F32), jax.ShapeDtypeStruct((nbatch, nc, 128, 1024), F32)),
                  scratch_shapes=[pltpu.VMEM((128, 1024), F32)],
                  compiler_params=_params())(xbc3, xbc3, dt3, alog, *(carry[0] if carry else ()))


def _ssd_bwd(xbc3, dt3, alog, st4, dy3, reverse, add_to=(), scatter=()):
    nbatch, s, _ = xbc3.shape
    q = min(SSD_CHUNK, s)
    nc = s // q
    lane0 = SSD_HEADS * int(reverse)
    nadd, ns = len(add_to), len(scatter)

    def body(xs_ref, bc_ref, dt_ref, al_ref, st0_ref, dy_ref, *rest):
        adds, srcs, rest = rest[:nadd], rest[nadd:nadd + ns], rest[nadd + ns:]
        (dxs_ref, dbc_ref, ddt_ref, dal_ref), lands, dst = rest[:4], rest[4:4 + ns], rest[4 + ns]
        n, i = pl.program_id(0), pl.program_id(1)
        if ns:
            sends, arrivals = _scatter_copies(srcs, lands, *rest[5 + ns:])

            @pl.when((n == 0) & (i == 0))
            def _():
                for cp in sends:
                    cp.start()

        @pl.when(i == 0)
        def _():
            dst[...] = jnp.zeros_like(dst)

        @pl.when((i == 0) & (n == 0))
        def _():
            dal_ref[...] = jnp.zeros_like(dal_ref)

        c = _ssd_common(xs_ref, bc_ref, dt_ref, al_ref, reverse, lane0)
        hm = _head_masks()
        reduce_m = _head_reduce(lane0)
        s0_all, ds1_all, dy = st0_ref[...], dst[...], dy_ref[...]
        lane = lax.broadcasted_iota(jnp.int32, (q, 128), 1)
        sub = lax.broadcasted_iota(jnp.int32, (128, q), 0)
        rowacc = jnp.zeros((q, 128), F32)
        colacc_t = jnp.zeros((128, q), F32)
        dv_l, yst_l, dvbar_l, dk_l, dc_l = [], [], [], [], []
        for g in range(SSD_GROUPS):
            sl = slice(g * 256, (g + 1) * 256)
            cg, bg = _mx(c["cm"][:, g * 128:(g + 1) * 128]), _mx(c["bm"][:, g * 128:(g + 1) * 128])
            cb = _dot(cg, bg, _NT)
            vg, dyg, wg, ecg = c["v"][:, sl], dy[:, sl], c["w"][:, sl], c["e_c"][:, sl]
            s0, ds1 = _mx(s0_all[:, sl]), _mx(ds1_all[:, sl])
            dye = _mx(dyg * ecg)
            yst_l.append(_dot(cg, s0) * ecg)
            dcg = _dot(dye, s0, _NT)
            dst[:, sl] = c["e_l"][:, sl] * ds1_all[:, sl] + _dot(cg, dye, _TN)
            vbar = _mx(vg * wg)
            dvbar = _dot(bg, ds1)
            dvbar_l.append(dvbar)
            dvg = dvbar * wg
            dkg = _dot(vbar, ds1, _NT)
            for e in range(4):
                h = 4 * g + e
                m = _ssd_decay(c, h)
                dyh, vh = _mx(jnp.where(hm[e], dyg, 0.0)), _mx(jnp.where(hm[e], vg, 0.0))
                dvg = dvg + _dot(_mx(m * cb), dyh, _TN)
                dcb = _dot(dyh, vh, _NT) * m
                dcbb = _mx(dcb)
                dcg = dcg + _dot(dcbb, bg)
                dkg = dkg + _dot(dcbb, cg, _TN)
                wmat = dcb * cb
                rowacc = jnp.where(lane == lane0 + h, jnp.sum(wmat, axis=1, keepdims=True), rowacc)
                colacc_t = jnp.where(sub == lane0 + h, jnp.sum(wmat, axis=0, keepdims=True), colacc_t)
            dv_l.append(dvg)
            dk_l.append(dkg)
            dc_l.append(dcg)
        dv = jnp.concatenate(dv_l, axis=1)
        yst = jnp.concatenate(yst_l, axis=1)
        dvbar = jnp.concatenate(dvbar_l, axis=1)
        t1 = _dot01(dy * yst, reduce_m, split="a", terms=3)
        t2 = _dot01(c["v"] * c["w"] * dvbar, reduce_m, split="a", terms=3)
        dlast = jnp.sum(t2, axis=0, keepdims=True) + _dot01(
            c["e_l"] * jnp.sum(ds1_all * s0_all, axis=0, keepdims=True), reduce_m, split="a", terms=2)
        dcum = rowacc - colacc_t.T + t1 - t2
        dcum = dcum + jnp.where(lax.broadcasted_iota(jnp.int32, (q, 128), 0) == c["edge"], dlast, 0.0)
        dda = _dot01(c["mask"].astype(F32), dcum, _TN, split="b", terms=3)
        ddt = dda * c["a"] + _dot01(dv * c["xs"], reduce_m, split="a", terms=2)
        dal_ref[...] += jnp.sum(dda * c["dt"], axis=0, keepdims=True) * c["a"]
        dxs = dv * c["dt_x"]
        dbc = jnp.concatenate(dk_l + dc_l, axis=1)
        if nadd:
            for a_ref in adds[:-2]:
                dxs = dxs + a_ref[...]
            dbc = dbc + adds[-2][...]
            ddt = ddt + adds[-1][...]
        ddt_ref[...] = ddt
        dxs_ref[...] = dxs
        dbc_ref[...] = dbc
        if ns:
            @pl.when((n == nbatch - 1) & (i == nc - 1))
            def _():
                for cp in arrivals:
                    cp.wait_recv()
                for cp in sends:
                    cp.wait_send()

    ck = (lambda i: i) if reverse else (lambda i: nc - 1 - i)
    xs_spec = pl.BlockSpec((None, q, 1024), lambda n, i: (n, ck(i), 0))
    bc_spec = pl.BlockSpec((None, q, 1024), lambda n, i: (n, ck(i), 1))
    dt_spec = pl.BlockSpec((None, q, 128), lambda n, i: (n, ck(i), 0))
    al_spec = pl.BlockSpec((1, 128), lambda n, i: (0, 0))
    st_spec = pl.BlockSpec((None, None, 128, 1024), lambda n, i: (n, ck(i), 0, 0))
    return _pcall(body, name=f"ssd_bwd_r{int(reverse)}", grid=(nbatch, nc),
                  in_specs=([xs_spec, bc_spec, dt_spec, al_spec, st_spec, xs_spec] + [xs_spec] * (nadd - 1)
                            + [dt_spec] * bool(nadd) + [ANY] * ns),
                  out_specs=(xs_spec, xs_spec, dt_spec, al_spec) + (ANY,) * ns,
                  out_shape=(jax.ShapeDtypeStruct((nbatch, s, 1024), F32), jax.ShapeDtypeStruct((nbatch, s, 1024), F32),
                             jax.ShapeDtypeStruct((nbatch, s, 128), F32), jax.ShapeDtypeStruct((1, 128), F32))
                  + tuple(jax.ShapeDtypeStruct(c.shape, c.dtype) for c in scatter),
                  scratch_shapes=[pltpu.VMEM((128, 1024), F32)] + (_scatter_scratch(ns) if ns else []),
                  compiler_params=_params())(xbc3, xbc3, dt3, alog, st4, dy3, *add_to, *scatter)


def _gla_block(q, k, g, reverse):
    bq = g.shape[0]
    nsub = bq // HGRN_SUB
    edge = 0 if reverse else bq - 1
    ri = lax.broadcasted_iota(jnp.int32, (bq, bq), 0)
    ci = lax.broadcasted_iota(jnp.int32, (bq, bq), 1)
    rb, cb = jnp.right_shift(ri, HGRN_SUB_SHIFT), jnp.right_shift(ci, HGRN_SUB_SHIFT)
    mask = (ri <= ci) if reverse else (ri >= ci)
    m_within = (mask & (rb == cb)).astype(F32)
    m_before = ((cb > rb) if reverse else (cb < rb)).astype(F32)
    bl = _dot01(m_within, g, split="b", terms=3)
    c = _dot01(m_before, g, split="b", terms=3)
    last = c[edge:edge + 1, :] + bl[edge:edge + 1, :]
    ebl, enbl, ec, elc = jnp.exp(bl), jnp.exp(-bl), jnp.exp(c), jnp.exp(last - c)
    qh = q * HGRN_SCALE * ebl
    kh = k * enbl
    blk = jnp.right_shift(lax.broadcasted_iota(jnp.int32, (bq, 1), 0), HGRN_SUB_SHIFT)
    scale = []
    for i in range(nsub):
        valid = (blk >= i) if reverse else (blk <= i)
        ex = jnp.where(valid, c[i * HGRN_SUB:i * HGRN_SUB + 1, :] - c, 0.0)
        scale.append(jnp.where(valid, jnp.exp(ex), 0.0))
    return dict(bq=bq, nsub=nsub, edge=edge, mask=mask, m_within=m_within, m_before=m_before, ebl=ebl, enbl=enbl, ec=ec,
                elc=elc, e_l=jnp.exp(last), qh=qh, qt=qh * ec, kh=kh, kb=kh * elc, scale=scale)


def _gla_scores(c, hs):
    keys = [_mx(c["kh"][:, hs] * c["scale"][i][:, hs]) for i in range(c["nsub"])]
    rows = [_dot(_mx(c["qh"][i * HGRN_SUB:(i + 1) * HGRN_SUB, hs]), keys[i], _NT) for i in range(c["nsub"])]
    return jnp.where(c["mask"], jnp.concatenate(rows, axis=0), 0.0), keys


def _gla_specs(nbatch, s, w, reverse_order):
    bq = min(HGRN_BLOCK, s)
    nblk = s // bq
    bi = (lambda i: nblk - 1 - i) if reverse_order else (lambda i: i)
    col = lambda cb: pl.BlockSpec((nbatch, bq, w), lambda i: (0, bi(i), cb))
    st_spec = pl.BlockSpec((nbatch, None, 128, w), lambda i: (0, bi(i), 0, 0))
    return bq, nblk, col, st_spec


def _gla_fwd(proj3, l0, l1, reverse):
    nbatch, s, w5 = proj3.shape
    w = w5 // 5
    bq, nblk, col, st_spec = _gla_specs(nbatch, s, w, reverse)
    vec = pl.BlockSpec((1, w), lambda i: (0, 0))

    def body(q_ref, f_ref, v_ref, l0_ref, l1_ref, o_ref, st_ref, st):
        @pl.when(pl.program_id(0) == 0)
        def _():
            st[...] = jnp.zeros_like(st)

        for b in range(nbatch):
            st_ref[b] = st[b]
            k, g = _f_hgrn_pre(f_ref[b], l0_ref[...], l1_ref[...])
            c = _gla_block(q_ref[b], k, g, reverse)
            v = v_ref[b]
            for h in range(HGRN_HEADS):
                hs = slice(h * 128, (h + 1) * 128)
                att, _ = _gla_scores(c, hs)
                vb = _mx(v[:, hs])
                s0 = st[b, :, hs]
                o_ref[b, :, hs] = _dot(_mx(att), vb) + _dot(_mx(c["qt"][:, hs]), _mx(s0), _NT)
                st[b, :, hs] = s0 * c["e_l"][:, hs] + _dot(vb, _mx(c["kb"][:, hs]), _TN)

    return _pcall(body, name=f"gla_fwd_r{int(reverse)}", grid=(nblk,),
                  in_specs=[col(0), col(1 + int(reverse)), col(3), vec, vec], out_specs=(col(0), st_spec),
                  out_shape=(jax.ShapeDtypeStruct((nbatch, s, w), F32), jax.ShapeDtypeStruct((nbatch, nblk, 128, w), F32)),
                  scratch_shapes=[pltpu.VMEM((nbatch, 128, w), F32)], compiler_params=_params())(proj3, proj3, proj3, l0, l1)


def _gla_bwd(proj3, l0, l1, st4, do3, reverse, add_to=None):
    nbatch, s, w5 = proj3.shape
    w = w5 // 5
    bq, nblk, col, st_spec = _gla_specs(nbatch, s, w, not reverse)
    nadd = 0 if add_to is None else 2
    vec = pl.BlockSpec((1, w), lambda i: (0, 0))

    def body(q_ref, f_ref, v_ref, l0_ref, l1_ref, st_ref, do_ref, *rest):
        adds, (dq_ref, df_ref, dv_ref, dl0_ref, dl1_ref, dst) = rest[:nadd], rest[nadd:]

        @pl.when(pl.program_id(0) == 0)
        def _():
            dst[...] = jnp.zeros_like(dst)
            dl0_ref[...] = jnp.zeros_like(dl0_ref)
            dl1_ref[...] = jnp.zeros_like(dl1_ref)

        row = lax.broadcasted_iota(jnp.int32, (bq, 128), 0)
        for b in range(nbatch):
            (k, g), pre_vjp = jax.vjp(_f_hgrn_pre, f_ref[b], l0_ref[...], l1_ref[...])
            c = _gla_block(q_ref[b], k, g, reverse)
            s0_all, ds1_all = st_ref[b], dst[b]
            v, dy = v_ref[b], do_ref[b]
            dbl_l, dc_l, dk_l = [], [], []
            for h in range(HGRN_HEADS):
                hs = slice(h * 128, (h + 1) * 128)
                att, keys = _gla_scores(c, hs)
                qh, qt, kh, kb = c["qh"][:, hs], c["qt"][:, hs], c["kh"][:, hs], c["kb"][:, hs]
                vb, dyb = _mx(v[:, hs]), _mx(dy[:, hs])
                s0, ds1 = s0_all[:, hs], ds1_all[:, hs]
                datt = _mx(jnp.where(c["mask"], _dot(dyb, vb, _NT), 0.0))
                dqh_rows = []
                dkh = jnp.zeros((bq, 128), F32)
                dc = jnp.zeros((bq, 128), F32)
                for i in range(c["nsub"]):
                    rs = slice(i * HGRN_SUB, (i + 1) * HGRN_SUB)
                    dqh_rows.append(_dot(datt[rs], keys[i]))
                    dki = _dot(datt[rs], _mx(qh[rs]), _TN)
                    sc = c["scale"][i][:, hs]
                    dkh = dkh + dki * sc
                    dex = dki * (kh * sc)
                    dc = dc - dex + jnp.where(row == i * HGRN_SUB, jnp.sum(dex, axis=0, keepdims=True), 0.0)
                dqt = _dot(dyb, _mx(s0))
                dkb = _dot(vb, _mx(ds1))
                dv = _dot(_mx(att), dyb, _TN) + _dot(_mx(kb), _mx(ds1), _NT)
                dst[b, :, hs] = c["e_l"][:, hs] * ds1 + _dot(dyb, _mx(qt), _TN)
                dqh = jnp.concatenate(dqh_rows, axis=0) + dqt * c["ec"][:, hs]
                dkh = dkh + dkb * c["elc"][:, hs]
                kbk = dkb * kb
                dlast = jnp.sum(kbk, axis=0, keepdims=True) + c["e_l"][:, hs] * jnp.sum(ds1 * s0, axis=0, keepdims=True)
                at_edge = jnp.where(row == c["edge"], dlast, 0.0)
                dc_l.append(dc + dqt * qt - kbk + at_edge)
                dbl_l.append(dqh * qh - dkh * kh + at_edge)
                dq = dqh * c["ebl"][:, hs] * HGRN_SCALE
                if nadd:
                    dq, dv = dq + adds[0][b, :, hs], dv + adds[1][b, :, hs]
                dq_ref[b, :, hs] = dq.astype(dq_ref.dtype)
                dv_ref[b, :, hs] = dv.astype(dv_ref.dtype)
                dk_l.append(dkh * c["enbl"][:, hs])
            dg = (_dot01(c["m_within"], jnp.concatenate(dbl_l, axis=1), _TN, split="b", terms=2)
                  + _dot01(c["m_before"], jnp.concatenate(dc_l, axis=1), _TN, split="b", terms=2))
            df, d0, d1 = pre_vjp((jnp.concatenate(dk_l, axis=1), dg))
            df_ref[b] = df.astype(df_ref.dtype)
            dl0_ref[...] += d0
            dl1_ref[...] += d1

    shp_sum = jax.ShapeDtypeStruct((nbatch, s, w), BF16 if nadd else F32)
    shp_vec = jax.ShapeDtypeStruct((1, w), F32)
    return _pcall(body, name=f"gla_bwd_r{int(reverse)}", grid=(nblk,),
                  in_specs=[col(0), col(1 + int(reverse)), col(3), vec, vec, st_spec, col(0)] + [col(0)] * nadd,
                  out_specs=(col(0), col(0), col(0), vec, vec),
                  out_shape=(shp_sum, jax.ShapeDtypeStruct((nbatch, s, w), BF16), shp_sum, shp_vec, shp_vec),
                  scratch_shapes=[pltpu.VMEM((nbatch, 128, w), F32)],
                  compiler_params=_params())(proj3, proj3, proj3, l0, l1, st4, do3, *(add_to or ()))


DIRS = (False, True)


def _block_diag(w):
    eye = jnp.eye(16, dtype=w.dtype)
    return (eye[:, None, :, None] * w[:, :, None, :]).reshape(1024, 1024)


def _diag_blocks(m):
    m4 = m.reshape(16, 64, 16, 64)
    return jnp.stack([m4[i, :, i, :] for i in range(16)], axis=0)


def _pad_lanes(v, n=128):
    return jnp.pad(v, [(0, 0)] * (v.ndim - 1) + [(0, n - v.shape[-1])])


def _mlp_fwd(tag, x, nw, w1, w2, carry=None):
    (h,) = _pw_fwd(f"{tag}_norm", _f_norm, [(x, 0)], [(nw, 0)], [BF16], 1024, 1)
    a, r, *got = _mm(f"{tag}_up", h, w1, "nn", relu2=True, carry=carry)
    return _mm(f"{tag}_down", r, w2, "nn", res=x), (h, a, r), got


def _mlp_bwd(tag, x, nw, w1, w2, saved, dxo):
    h, a, r = saved
    dw2 = _mm(f"{tag}_dw2", r, dxo, "tn")
    da = _mm(f"{tag}_da", dxo, w2, "nt", relu2_of=a, out_dtype=BF16)
    dw1 = _mm(f"{tag}_dw1", h, da, "tn", col_shards=4)
    dh = _mm(f"{tag}_dh", da, w1, "nt")
    (dx,), (dnw,) = _pw_bwd(f"{tag}_dnorm", _f_norm, [(x, 0)], [(nw, 0)], [dh], 1024, 1, [0], adds={0: dxo}, tm=ROWS_FWD)
    return dx, dw1, dw2, dnw


def _split_in0(pieces, dt_piece):
    tm = 256

    def body(p0, p1, p2, p3, p4, p5, o_ref):
        full = jnp.concatenate([p0[...], p1[...], p2[...], p3[...], p4[...], p5[:, :32]], axis=1)
        for j in range(4):
            o_ref[j] = full[:, 1288 * j:1288 * (j + 1)]

    blk = pl.BlockSpec((tm, 1024), lambda i: (i, 0))
    return _pcall(body, name="split_in0", grid=(1024 // tm,), in_specs=[blk] * 5 + [pl.BlockSpec((tm, 128), lambda i: (i, 0))],
                  out_specs=pl.BlockSpec((4, tm, 1288), lambda i: (0, i, 0)),
                  out_shape=jax.ShapeDtypeStruct((4, 1024, 1288), F32), compiler_params=_params())(*pieces, dt_piece)


def _assemble_in0(shards):
    tm = 256

    def body(s_ref, m_ref, d_ref):
        full = jnp.concatenate([s_ref[j] for j in range(4)], axis=1)
        m_ref[...] = full[:, :5120]
        d_ref[...] = jnp.concatenate([full[:, 5120:5152], jnp.zeros((tm, 96), full.dtype)], axis=1)

    return _pcall(body, name="assemble_in0", grid=(1024 // tm,), in_specs=[pl.BlockSpec((4, tm, 1288), lambda i: (0, i, 0))],
                  out_specs=(pl.BlockSpec((tm, 5120), lambda i: (i, 0)), pl.BlockSpec((tm, 128), lambda i: (i, 0))),
                  out_shape=(jax.ShapeDtypeStruct((1024, 5120), shards.dtype), jax.ShapeDtypeStruct((1024, 128), shards.dtype)),
                  compiler_params=_params())(shards)


EARLY = ("odd_w_in", "odd_w_out", "mlp_w1_l1", "mlp_w2_l1")
MID = ("even_w_out", "mlp_w1_l0", "mlp_w2_l0")
LATE = ("even_w_in",)


def _local_step(x3, tgt3, w, w_main0, w_dt0, pair_reduce=None, late=None):
    nb, s, d = x3.shape
    carries, arrived = late if late else ({}, None)
    t = nb * s
    x0 = x3.reshape(t, d)
    tgt = tgt3.reshape(t, d)
    grads = {}
    row = lambda v: v.reshape(1, -1)
    to3 = lambda v: v.reshape(nb, s, v.shape[-1])
    to2 = lambda v: v.reshape(-1, v.shape[-1])

    conv_w, conv_b = w["even_conv_w"][0], row(w["even_conv_b"][0])
    nmix0 = row(w["norm_mix"][0])
    (h0,) = _pw_fwd("l0_norm", _f_norm, [(x0, 0)], [(nmix0, 0)], [BF16], 1024, 1)
    proj0 = _mm("l0_proj", h0, w_main0, "nn")
    dt_raw = _mm("l0_proj_dt", h0, w_dt0, "nn")
    conv2, xbc3 = _conv_fwd(to3(proj0), conv_w, conv_b, 0, 2, True)
    u_lru = to2(_conv_fwd(to3(proj0), conv_w, conv_b, 2, 1, False))
    xbc = to2(xbc3)
    dt_bias = _pad_lanes(w["ssd_dt_bias"][0].reshape(1, 32))
    (dt,) = _pw_fwd("l0_dt", _f_softplus, [(dt_raw, 0)], [(dt_bias, 0)], [F32], 128, 1)
    dt3 = to3(dt)
    alog = _pad_lanes(w["ssd_a_log"][0].reshape(1, 32))
    ssd = [_ssd_fwd(xbc3, dt3, alog, r, carry=carries.get(key)) for r, key in zip(DIRS, ("mlp_w1", "mlp_w2"))]
    if late:
        w = {**w, **arrived("mlp_w1", ssd[0][2:]), **arrived("mlp_w2", ssd[1][2:])}
    yf, yb = to2(ssd[0][0]), to2(ssd[1][0])
    dskip = jnp.repeat(w["ssd_d"][0], SSD_HEADDIM).reshape(1, 1024)
    snw = row(w["ssd_norm_w"][0])
    ssd_ins = [(yf, 0), (yb, 0), (xbc, 0), (proj0, 3)]
    (ya,) = _pw_fwd("l0_ssd_post", _f_ssd_post, ssd_ins, [(dskip, 0), (snw, 0)], [BF16], 1024, 1, groups=SSD_GROUPS)
    w_gates = [_block_diag(w[k][0, r]).astype(MXU_DTYPE) for r in range(2) for k in ("lru_w_a", "lru_w_x")]
    pre = [_mm(f"l0_lru_pre{i}", u_lru, wg, "nn") for i, wg in enumerate(w_gates)]
    lru_par = [[(row(w[k][0, r]), 0) for k in ("lru_b_a", "lru_b_x", "lru_lambda")] for r in range(2)]
    lru_ins = [[(pre[2 * r], 0), (pre[2 * r + 1], 0), (u_lru, 0)] for r in range(2)]
    ab = [_pw_fwd(f"l0_lru_gates{r}", _f_lru_gates, lru_ins[r], lru_par[r], [F32, F32], 1024, 1) for r in range(2)]
    hs = [_lru_scan(to3(ab[r][0]), to3(ab[r][1]), DIRS[r]) for r in range(2)]
    lru_post_ins = [(to2(hs[0]), 0), (to2(hs[1]), 0), (proj0, 4)]
    (ybm,) = _pw_fwd("l0_lru_post", _f_lru_post, lru_post_ins, [], [BF16], 1024, 1)
    w_out0 = w["even_w_out"][0]
    x1 = _mm("l0_out_a", ya, w_out0[:1024], "nn", res=x0)
    x1 = _mm("l0_out_b", ybm, w_out0[1024:], "nn", res=x1)
    nmlp0 = row(w["norm_mlp"][0])
    x2, mlp0, got = _mlp_fwd("l0_mlp", x1, nmlp0, w["mlp_w1"][0], w["mlp_w2"][0], carry=carries.get("odd"))
    if late:
        w = {**w, **arrived("odd", got)}

    w_in1 = w["odd_w_in"][0]
    nmix1 = row(w["norm_mix"][1])
    (h1,) = _pw_fwd("l1_norm", _f_norm, [(x2, 0)], [(nmix1, 0)], [BF16], 1024, 1)
    proj1 = _mm("l1_proj", h1, w_in1, "nn")
    proj1_3 = to3(proj1)
    lb0, lb1 = row(w["hgrn_lb_logits"][0]), row(w["hgrn_lb_logits"][1])
    gla = [_gla_fwd(proj1_3, lb0, lb1, r) for r in DIRS]
    hnw = row(w["hgrn_norm_w"][0])
    hpost_ins = [(to2(gla[0][0]), 0), (to2(gla[1][0]), 0), (proj1, 4)]
    (yo,) = _pw_fwd("l1_hgrn_post", _f_hgrn_post, hpost_ins, [(hnw, 0)], [BF16], 1024, 1, groups=HGRN_HEADS)
    w_out1 = w["odd_w_out"][0]
    x3_ = _mm("l1_out", yo, w_out1, "nn", res=x2)
    nmlp1 = row(w["norm_mlp"][1])
    x4, mlp1, _ = _mlp_fwd("l1_mlp", x3_, nmlp1, w["mlp_w1"][1], w["mlp_w2"][1])

    dx4, dnf, loss = _loss_head(x4, tgt, row(w["norm_final"]))
    grads["norm_final"] = dnf.reshape(-1)

    dx3, dw1_1, dw2_1, dnmlp1 = _mlp_bwd("l1_mlp", x3_, nmlp1, w["mlp_w1"][1], w["mlp_w2"][1], mlp1, dx4)
    big = {"odd_w_out": _mm("l1_dwout", yo, dx3, "tn").reshape(4, 256, 1024)}
    dyo = _mm("l1_dyo", dx3, w_out1, "nt")
    (do, dgate1), (dhnw,) = _pw_bwd("l1_hgrn_post_b", _f_hgrn_post, hpost_ins, [(hnw, 0)], [dyo], 1024, 1, [0, 2],
                                    out_dtypes=[F32, BF16], groups=HGRN_HEADS, tm=ROWS_FWD)
    grads["hgrn_norm_w"] = dhnw
    do3 = to3(do)
    gb = [_gla_bwd(proj1_3, lb0, lb1, gla[0][1], do3, False)]
    gb.append(_gla_bwd(proj1_3, lb0, lb1, gla[1][1], do3, True, add_to=(gb[0][0], gb[0][2])))
    grads["hgrn_lb_logits"] = jnp.concatenate([gb[0][3] + gb[1][3], gb[0][4] + gb[1][4]], axis=0)
    dparts1 = [to2(gb[1][0]), to2(gb[0][1]), to2(gb[1][1]), to2(gb[1][2]), dgate1]
    dwin1 = jnp.concatenate([_mm(f"l1_dwin{i}", h1, dp, "tn") for i, dp in enumerate(dparts1)], axis=1)
    big["odd_w_in"] = dwin1.reshape(1024, 4, 1280).transpose(1, 0, 2)
    dh1 = _mm_sum_nt("l1_dh", dparts1, [(w_in1, i) for i in range(5)])
    (dx2,), (dnmix1,) = _pw_bwd("l1_dnorm", _f_norm, [(x2, 0)], [(nmix1, 0)], [dh1], 1024, 1, [0], adds={0: dx3}, tm=ROWS_FWD)
    big["mlp_w1_l1"], big["mlp_w2_l1"] = dw1_1, dw2_1.reshape(4, 1024, 1024)
    early_sums = tuple(pair_reduce(EARLY, [big[n] for n in EARLY])) if pair_reduce else ()

    dx1, dw1_0, dw2_0, dnmlp0 = _mlp_bwd("l0_mlp", x1, nmlp0, w["mlp_w1"][0], w["mlp_w2"][0], mlp0, dx2)
    big["mlp_w1_l0"], big["mlp_w2_l0"] = dw1_0, dw2_0.reshape(4, 1024, 1024)
    grads["norm_mlp"] = jnp.concatenate([dnmlp0, dnmlp1], axis=0)
    big["even_w_out"] = jnp.concatenate([_mm("l0_dwout_a", ya, dx1, "tn"), _mm("l0_dwout_b", ybm, dx1, "tn")],
                                        axis=0).reshape(4, 512, 1024)
    mid_sums = tuple(pair_reduce(MID, [big[n] for n in MID])) if pair_reduce else ()
    dya = _mm("l0_dya", dx1, w_out0[:1024], "nt")
    dyb = _mm("l0_dyb", dx1, w_out0[1024:], "nt")
    (dh, dgate0), _ = _pw_bwd("l0_lru_post_b", _f_lru_post, lru_post_ins, [], [dyb], 1024, 1, [0, 2], out_dtypes=[F32, BF16],
                               tm=ROWS_FWD)
    dh3 = to3(dh)
    dpre, du_parts, dlru = [], [], {k: [] for k in ("lru_b_a", "lru_b_x", "lru_lambda")}
    for r in range(2):
        g_r, da_r = _lru_scan_bwd(to3(ab[r][0]), hs[r], dh3, DIRS[r])
        (dpa, dpx, du_r), (dba, dbx, dlam) = _pw_bwd(f"l0_lru_gates_b{r}", _f_lru_gates, lru_ins[r], lru_par[r],
                                                     [to2(da_r), to2(g_r)], 1024, 1, [0, 1, 2],
                                                     out_dtypes=[BF16, BF16, F32])
        dpre += [dpa, dpx]
        du_parts.append(du_r)
        dlru["lru_b_a"].append(dba)
        dlru["lru_b_x"].append(dbx)
        dlru["lru_lambda"].append(dlam)
    for k, v in dlru.items():
        grads[k] = jnp.concatenate(v, axis=0)[None]
    dwg = [_diag_blocks(_mm(f"l0_dwgate{i}", u_lru, dp, "tn")) for i, dp in enumerate(dpre)]
    grads["lru_w_a"] = jnp.stack([dwg[0], dwg[2]])[None]
    grads["lru_w_x"] = jnp.stack([dwg[1], dwg[3]])[None]
    du_gate = _mm_sum_nt("l0_du_gate", dpre, [(wg, 0) for wg in w_gates])
    (du,) = _pw_fwd("l0_du", _f_add3, [(du_parts[0], 0), (du_parts[1], 0), (du_gate, 0)], [], [F32], 1024, 1)
    (dy, dxs_skip, dz), (ddskip, dsnw) = _pw_bwd("l0_ssd_post_b", _f_ssd_post, ssd_ins, [(dskip, 0), (snw, 0)], [dya],
                                                 1024, 1, [0, 2, 3], out_dtypes=[F32, F32, BF16], groups=SSD_GROUPS)
    grads["ssd_d"] = ddskip.reshape(SSD_HEADS, SSD_HEADDIM).sum(axis=1)[None]
    grads["ssd_norm_w"] = dsnw
    dy3 = to3(dy)
    sb0 = _ssd_bwd(xbc3, dt3, alog, ssd[0][1], dy3, False, scatter=early_sums)
    sb1 = _ssd_bwd(xbc3, dt3, alog, ssd[1][1], dy3, True, add_to=(sb0[0], to3(dxs_skip), sb0[1], sb0[2]), scatter=mid_sums)
    grads["ssd_a_log"] = (sb0[3] + sb1[3])[:, :32].reshape(1, 2, 16)
    ddt = to2(sb1[2])
    (ddt_raw,), (ddtb,) = _pw_bwd("l0_dt_b", _f_softplus, [(dt_raw, 0)], [(dt_bias, 0)], [ddt], 128, 1, [0])
    grads["ssd_dt_bias"] = ddtb[:, :32].reshape(1, 2, 16)
    cb = [_conv_bwd(sb1[0], to3(proj0), conv_w, 0, conv2), _conv_bwd(sb1[1], to3(proj0), conv_w, 1, conv2),
          _conv_bwd(to3(du), to3(proj0), conv_w, 2)]
    dcw = jnp.concatenate([c_[1] for c_ in cb], axis=1)
    grads["even_conv_w"] = dcw[:4][None]
    grads["even_conv_b"] = dcw[4:5]
    dparts0 = [to2(c_[0]) for c_ in cb] + [dz, dgate0]
    dwin0 = [_mm(f"l0_dwin{i}", h0, dp, "tn") for i, dp in enumerate(dparts0)]
    big["even_w_in"] = _split_in0(dwin0, _mm("l0_dwin_dt", h0, ddt_raw, "tn"))
    dh0 = _mm_sum_nt("l0_dh", dparts0 + [ddt_raw], [(w_main0, i) for i in range(5)] + [(w_dt0, 0)])
    (dx0,), (dnmix0,) = _pw_bwd("l0_dnorm", _f_norm, [(x0, 0)], [(nmix0, 0)], [dh0], 1024, 1, [0], adds={0: dx1}, tm=ROWS_FWD)
    grads["norm_mix"] = jnp.concatenate([dnmix0, dnmix1], axis=0)
    return loss, dx0.reshape(nb, s, d), grads, big, (early_sums + mid_sums, sb0[4:] + sb1[4:])


ANY = pl.BlockSpec(memory_space=pl.ANY)


def _place():
    return lax.axis_index("x"), lax.axis_index("y"), lax.axis_index("c")


def _remote(src, dst, send_sems, recv_sems, k, to):
    return pltpu.make_async_remote_copy(src_ref=src, dst_ref=dst, send_sem=send_sems.at[k], recv_sem=recv_sems.at[k],
                                        device_id=to, device_id_type=MESH)


def _gather_start(x_refs, out_refs, send_sems, recv_sems, finish=False):
    n = len(x_refs)
    halves = [r.shape[0] // 2 for r in x_refs]
    x, y, c = _place()
    sibling = (x, y, 1 - c)
    chips = [(1 - x, y), (x, 1 - y), (1 - x, 1 - y)]

    def blk(t, px, py, hc):
        return out_refs[t].at[2 * px + py, pl.ds(hc * halves[t], halves[t]), :]

    def src(t):
        return x_refs[t].at[pl.ds(c * halves[t], halves[t]), :]

    first = [_remote(src(t), blk(t, x, y, c), send_sems, recv_sems, 6 * t + j, (*chip, c))
             for t in range(n) for j, chip in enumerate(chips)]
    if not finish:
        for cp in first:
            cp.start()
        return
    passed = []
    for t in range(n):
        for j, chip in enumerate(chips):
            _remote(src(t), blk(t, *chip, c), send_sems, recv_sems, 6 * t + j, (*chip, c)).wait_recv()
            cp = _remote(blk(t, *chip, c), blk(t, *chip, c), send_sems, recv_sems, 6 * t + 3 + j, sibling)
            cp.start()
            passed.append(cp)
    for t in range(n):
        for j, chip in enumerate(chips):
            _remote(src(t), blk(t, *chip, 1 - c), send_sems, recv_sems, 6 * t + 3 + j, sibling).wait_recv()
    for cp in first + passed:
        cp.wait_send()


_gather_finish = functools.partial(_gather_start, finish=True)


def _gather_carry(shards):
    n = len(shards)
    return (list(shards), [jax.ShapeDtypeStruct((4,) + s.shape, s.dtype) for s in shards],
            [pltpu.SemaphoreType.DMA((6 * n,)), pltpu.SemaphoreType.DMA((6 * n,))], _gather_start, _gather_finish)


def _gather_chips(shards):
    n = len(shards)
    srcs, shapes, scratch, start, finish = _gather_carry(shards)

    def body(*refs):
        start(refs[:n], refs[n:2 * n], *refs[2 * n:])
        finish(refs[:n], refs[n:2 * n], *refs[2 * n:])

    return _pcall(body, name="gather_weights", in_specs=[ANY] * n, out_specs=(ANY,) * n, out_shape=tuple(shapes),
                  scratch_shapes=scratch, compiler_params=_params())(*shards)


def _pair_swap(name, gps):
    n = len(gps)
    halves = [g.shape[1] // 2 for g in gps]

    def body(*refs):
        g_refs, land_refs = refs[:n], refs[n:2 * n]
        send_sems, recv_sems = refs[2 * n:]
        x, y, c = _place()
        cps = [_remote(g_refs[t].at[j, pl.ds((1 - c) * halves[t], halves[t]), :], land_refs[t].at[j], send_sems, recv_sems,
                       4 * t + j, (x, y, 1 - c)) for t in range(n) for j in range(4)]
        for cp in cps:
            cp.start()
        for cp in cps:
            cp.wait()

    return _pcall(body, name=f"pair_swap_{name}", in_specs=[ANY] * n, out_specs=(ANY,) * n,
                  out_shape=tuple(jax.ShapeDtypeStruct((4, h, g.shape[2]), F32) for g, h in zip(gps, halves)),
                  scratch_shapes=[pltpu.SemaphoreType.DMA((4 * n,)), pltpu.SemaphoreType.DMA((4 * n,))],
                  compiler_params=_params())(*gps)


def _pair_add(name, gp, land, cidx):
    _, half, cols = land.shape
    tr = _tile(half, 512)
    nh = half // tr

    def body(c_ref, g_ref, l_ref, o_ref):
        o_ref[...] = (g_ref[...] + l_ref[...]).astype(o_ref.dtype)

    grid_spec = pltpu.PrefetchScalarGridSpec(
        num_scalar_prefetch=1, grid=(4, nh),
        in_specs=[pl.BlockSpec((None, tr, cols), lambda j, i, c: (j, c[0] * nh + i, 0)),
                  pl.BlockSpec((None, tr, cols), lambda j, i, c: (j, i, 0))],
        out_specs=pl.BlockSpec((None, tr, cols), lambda j, i, c: (j, i, 0)))
    return _pcall(body, name=f"pair_add_{name}", grid_spec=grid_spec, out_shape=jax.ShapeDtypeStruct((4, half, cols), BF16),
                  compiler_params=_params())(cidx, gp, land)


def _scatter_copies(s_refs, land_refs, send_sems, recv_sems):
    x, y, c = _place()
    me = 2 * x + y
    chips = [(1 - x, y), (x, 1 - y), (1 - x, 1 - y)]
    pairs = [(t, j, px, py) for t in range(len(s_refs)) for j, (px, py) in enumerate(chips)]
    sends = [_remote(s_refs[t].at[2 * px + py], land_refs[t].at[me], send_sems, recv_sems, 3 * t + j, (px, py, c))
             for t, j, px, py in pairs]
    arrivals = [_remote(s_refs[t].at[me], land_refs[t].at[2 * px + py], send_sems, recv_sems, 3 * t + j, (px, py, c))
                for t, j, px, py in pairs]
    return sends, arrivals


def _scatter_scratch(n):
    return [pltpu.SemaphoreType.DMA((3 * n,)), pltpu.SemaphoreType.DMA((3 * n,))]


def _chip_scatter(name, css):
    n = len(css)

    def body(*refs):
        sends, arrivals = _scatter_copies(refs[:n], refs[n:2 * n], *refs[2 * n:])
        for cp in sends:
            cp.start()
        for cp in arrivals:
            cp.wait_recv()
        for cp in sends:
            cp.wait_send()

    return _pcall(body, name=f"chip_scatter_{name}", in_specs=[ANY] * n, out_specs=(ANY,) * n,
                  out_shape=tuple(jax.ShapeDtypeStruct(s.shape, s.dtype) for s in css),
                  scratch_shapes=_scatter_scratch(n), compiler_params=_params())(*css)


def _chip_sum(name, land):
    _, half, cols = land.shape
    tr = _tile(half, 512)

    def body(l_ref, o_ref):
        o_ref[...] = ((l_ref[0].astype(F32) + l_ref[1].astype(F32)) + l_ref[2].astype(F32)) + l_ref[3].astype(F32)

    return _pcall(body, name=f"chip_sum_{name}", grid=(half // tr,),
                  in_specs=[pl.BlockSpec((4, tr, cols), lambda i: (0, i, 0))],
                  out_specs=pl.BlockSpec((tr, cols), lambda i: (i, 0)),
                  out_shape=jax.ShapeDtypeStruct((half, cols), F32), compiler_params=_params())(land)


def _pair_join(reds):
    n = len(reds)

    def body(*refs):
        r_refs, out_refs = refs[:n], refs[n:2 * n]
        send_sems, recv_sems = refs[2 * n:]
        x, y, c = _place()
        cps = [_remote(r_refs[t], out_refs[t].at[c], send_sems, recv_sems, t, (x, y, 1 - c)) for t in range(n)]
        for cp in cps:
            cp.start()
        for t in range(n):
            _remote(r_refs[t], out_refs[t].at[1 - c], send_sems, recv_sems, t, (x, y, 1 - c)).wait_recv()
        for cp in cps:
            cp.wait_send()

    return _pcall(body, name="grad_pair_join", in_specs=[ANY] * n, out_specs=(ANY,) * n,
                  out_shape=tuple(jax.ShapeDtypeStruct((2,) + r.shape, F32) for r in reds),
                  scratch_shapes=[pltpu.SemaphoreType.DMA((n,)), pltpu.SemaphoreType.DMA((n,))],
                  compiler_params=_params())(*reds)


def _adamw(name, g, w, m, v):
    rows, cols = g.shape
    tr = _tile(rows, 512)

    def body(g_ref, w_ref, m_ref, v_ref, d_ref, mo_ref, vo_ref):
        gv = g_ref[...]
        mn = ADAM_B1 * m_ref[...] + (1.0 - ADAM_B1) * gv
        vn = ADAM_B2 * v_ref[...] + (1.0 - ADAM_B2) * jnp.square(gv)
        m_hat = mn / (1.0 - ADAM_B1 ** ADAM_STEP)
        v_hat = vn / (1.0 - ADAM_B2 ** ADAM_STEP)
        d_ref[...] = -ADAM_LR * (m_hat / (jnp.sqrt(v_hat) + ADAM_EPS) + ADAM_WD * w_ref[...])
        mo_ref[...] = mn
        vo_ref[...] = vn

    blk = pl.BlockSpec((tr, cols), lambda i: (i, 0))
    shp = jax.ShapeDtypeStruct((rows, cols), F32)
    return _pcall(body, name=f"adamw_{name}", grid=(rows // tr,), in_specs=[blk] * 4, out_specs=(blk,) * 3,
                  out_shape=(shp,) * 3, compiler_params=_params())(g, w, m, v)


def _pack(pieces, rows, dtype):
    flat = jnp.concatenate([p.reshape(-1).astype(dtype) for p in pieces])
    return jnp.pad(flat, (0, rows * PACK_COLS - flat.shape[0])).reshape(rows, PACK_COLS)


def _unpack(pack, shapes):
    flat = pack.reshape(-1)
    out, off = [], 0
    for shp in shapes:
        n = math.prod(shp)
        out.append(flat[off:off + n].reshape(shp))
        off += n
    return out


def _shard_of(full, axis, j):
    n = full.shape[axis] // 4
    return lax.slice_in_dim(full, j * n, (j + 1) * n, axis=axis)


def kernel(x, even_w_in, even_conv_w, even_conv_b, ssd_a_log, ssd_dt_bias, ssd_d, ssd_norm_w, lru_w_a, lru_b_a, lru_w_x, lru_b_x, lru_lambda, even_w_out, odd_w_in, hgrn_lb_logits, hgrn_norm_w, odd_w_out, norm_mix, norm_mlp, mlp_w1, mlp_w2, norm_final, loss_target, m_even_w_in, m_even_conv_w, m_even_conv_b, m_ssd_a_log, m_ssd_dt_bias, m_ssd_d, m_ssd_norm_w, m_lru_w_a, m_lru_b_a, m_lru_w_x, m_lru_b_x, m_lru_lambda, m_even_w_out, m_odd_w_in, m_hgrn_lb_logits, m_hgrn_norm_w, m_odd_w_out, m_norm_mix, m_norm_mlp, m_mlp_w1, m_mlp_w2, m_norm_final, v_even_w_in, v_even_conv_w, v_even_conv_b, v_ssd_a_log, v_ssd_dt_bias, v_ssd_d, v_ssd_norm_w, v_lru_w_a, v_lru_b_a, v_lru_w_x, v_lru_b_x, v_lru_lambda, v_even_w_out, v_odd_w_in, v_hgrn_lb_logits, v_hgrn_norm_w, v_odd_w_out, v_norm_mix, v_norm_mlp, v_mlp_w1, v_mlp_w2, v_norm_final):
    names = [n for n, _, _, _ in WEIGHTS]
    w_loc = dict(zip(names, (even_w_in, even_conv_w, even_conv_b, ssd_a_log, ssd_dt_bias, ssd_d, ssd_norm_w, lru_w_a, lru_b_a, lru_w_x, lru_b_x, lru_lambda, even_w_out, odd_w_in, hgrn_lb_logits, hgrn_norm_w, odd_w_out, norm_mix, norm_mlp, mlp_w1, mlp_w2, norm_final)))
    m_loc = dict(zip(names, (m_even_w_in, m_even_conv_w, m_even_conv_b, m_ssd_a_log, m_ssd_dt_bias, m_ssd_d, m_ssd_norm_w, m_lru_w_a, m_lru_b_a, m_lru_w_x, m_lru_b_x, m_lru_lambda, m_even_w_out, m_odd_w_in, m_hgrn_lb_logits, m_hgrn_norm_w, m_odd_w_out, m_norm_mix, m_norm_mlp, m_mlp_w1, m_mlp_w2, m_norm_final)))
    v_loc = dict(zip(names, (v_even_w_in, v_even_conv_w, v_even_conv_b, v_ssd_a_log, v_ssd_dt_bias, v_ssd_d, v_ssd_norm_w, v_lru_w_a, v_lru_b_a, v_lru_w_x, v_lru_b_x, v_lru_lambda, v_even_w_out, v_odd_w_in, v_hgrn_lb_logits, v_hgrn_norm_w, v_odd_w_out, v_norm_mix, v_norm_mlp, v_mlp_w1, v_mlp_w2, v_norm_final)))
    spec = {n: (blk, full, ax) for n, blk, full, ax in WEIGHTS}

    small = [n for n in names if n not in BIG]
    two_d = lambda n, v: v.reshape(BIG_2D[n])

    me = 2 * lax.axis_index("x") + lax.axis_index("y")
    cc = lax.axis_index("c")
    put = lambda whole, part, k: lax.dynamic_update_slice_in_dim(whole, part[None], k, axis=0)
    own = {n: two_d(n, w_loc[n]).astype(BF16) for n in BIG}
    own["small"] = _pack([w_loc[n] for n in SMALL_SHARDED], 16, F32)
    fill = lambda got, keys: [put(g, own[k], me) for g, k in zip(got, keys)]
    first = ("even_w_in", "even_w_out", "small")
    g_in0, g_out0, g_small = fill(_gather_chips([own[k] for k in first]), first)
    w_main0, w_dt0 = _assemble_in0(g_in0)
    w_full = {n: w_loc[n] for n in names if spec[n][2] is None}
    w_full["even_w_out"] = g_out0.reshape(1, 2048, 1024)
    shards = [_unpack(g_small[j], [spec[n][0] for n in SMALL_SHARDED]) for j in range(4)]
    for n in ("mlp_w1", "mlp_w2"):
        for l in range(2):
            own[f"{n}_l{l}"] = w_loc[n][l].astype(BF16)
    layers = lambda n: (f"{n}_l0", f"{n}_l1")
    carries = {"mlp_w1": _gather_carry([own[k] for k in layers("mlp_w1")]),
               "mlp_w2": _gather_carry([own[k] for k in layers("mlp_w2")]),
               "odd": _gather_carry([own["odd_w_in"], own["odd_w_out"]])}

    def arrived(key, got):
        if key == "odd":
            g_in1, g_out1 = fill(got, ("odd_w_in", "odd_w_out"))
            return {"odd_w_in": jnp.concatenate([g_in1[j] for j in range(4)], axis=1)[None],
                    "odd_w_out": g_out1.reshape(1, 1024, 1024)}
        g = fill(got, layers(key))
        return {key: g if key == "mlp_w1" else [v.reshape(4096, 1024) for v in g]}

    for i, n in enumerate(SMALL_SHARDED):
        w_full[n] = jnp.concatenate([shards[j][i] for j in range(4)], axis=spec[n][2])

    cidx = cc.astype(jnp.int32).reshape(1)

    def pair_reduce(tags, tensors):
        return [_pair_add(tag, g, land, cidx) for tag, g, land in zip(tags, tensors, _pair_swap(tags[0], tensors))]

    loss_vec, grad_x, grads, big, (early_sums, early_landed) = _local_step(
        x, loss_target, w_full, w_main0, w_dt0, pair_reduce, (carries, arrived))
    loss = lax.psum(loss_vec[0, 0], ("x", "y", "c"))

    def dest_pack(j):
        return _pack([grads[n].reshape(spec[n][1]) if spec[n][2] is None else _shard_of(grads[n].reshape(spec[n][1]), spec[n][2], j)
                      for n in small], SMALL_ROWS, F32)

    late_tags = LATE + ("small",)
    late_sums = pair_reduce(late_tags, [big[n] for n in LATE] + [jnp.stack([dest_pack(j) for j in range(4)])])
    tags = EARLY + MID + late_tags
    chip_sums = list(early_sums) + late_sums
    landed = [put(land, lax.dynamic_index_in_dim(cs, me, axis=0, keepdims=False), me)
              for land, cs in zip(list(early_landed) + list(_chip_scatter("late", late_sums)), chip_sums)]
    halves = [_chip_sum(tag, land) for tag, land in zip(tags, landed)]
    red = {tag: put(r, h, cc).reshape(-1, r.shape[-1]) for tag, r, h in zip(tags, _pair_join(halves), halves)}
    for n in ("mlp_w1", "mlp_w2"):
        red[n] = jnp.concatenate([red[n + "_l0"], red[n + "_l1"]], axis=0)

    outs = {}
    for n, g in ((n, red[n]) for n in BIG):
        res = (g, *_adamw(n, g, two_d(n, w_loc[n]), two_d(n, m_loc[n]), two_d(n, v_loc[n])))
        outs[n] = [r.reshape(spec[n][0]) for r in res]
    blocks = [spec[n][0] for n in small]
    wp, mp, vp = (_pack([src[n] for n in small], SMALL_ROWS, F32) for src in (w_loc, m_loc, v_loc))
    res = (red["small"], *_adamw("small", red["small"], wp, mp, vp))
    unpacked = [_unpack(r, blocks) for r in res]
    for i, n in enumerate(small):
        outs[n] = [u[i] for u in unpacked]
    return (loss, grad_x, *[outs[n][k] for k in range(4) for n in names])
```

```python
import functools
import math

import jax
import jax.numpy as jnp
from jax import lax
from jax.experimental import pallas as pl
from jax.experimental.pallas import tpu as pltpu

F32 = jnp.float32
BF16 = jnp.bfloat16
MXU_DTYPE = jnp.bfloat16
MESH = pl.DeviceIdType.MESH

D_MODEL = 1024
EPS = 1e-6
SSD_HEADS = 16
SSD_HEADDIM = 64
HEAD_SHIFT = 6
SSD_GROUPS = 4
SSD_STATE = 128
SSD_CHUNK = 128
LRU_C = 8.0
LRU_ROWS = 256
HGRN_HEADS = 8
HGRN_HEADDIM = 128
HGRN_SUB = 32
HGRN_SUB_SHIFT = 5
HGRN_BLOCK = 128
HGRN_SCALE = HGRN_HEADDIM ** -0.5
CONV_ROWS = 512
ROWS_FWD = 512
ROWS_BWD = 256

ADAM_LR = 0.001
ADAM_B1 = 0.9
ADAM_B2 = 0.999
ADAM_EPS = 1e-08
ADAM_WD = 0.01
ADAM_STEP = 10

VMEM_LIMIT = 56 * 1024 * 1024
PACK_COLS = 1024
SMALL_ROWS = 288

WEIGHTS = (
    ("even_w_in", (1, 1024, 1288), (1, 1024, 5152), 2),
    ("even_conv_w", (1, 4, 768), (1, 4, 3072), 2),
    ("even_conv_b", (1, 3072), (1, 3072), None),
    ("ssd_a_log", (1, 2, 16), (1, 2, 16), None),
    ("ssd_dt_bias", (1, 2, 16), (1, 2, 16), None),
    ("ssd_d", (1, 16), (1, 16), None),
    ("ssd_norm_w", (1, 1024), (1, 1024), None),
    ("lru_w_a", (1, 2, 16, 64, 64), (1, 2, 16, 64, 64), None),
    ("lru_b_a", (1, 2, 256), (1, 2, 1024), 2),
    ("lru_w_x", (1, 2, 16, 64, 64), (1, 2, 16, 64, 64), None),
    ("lru_b_x", (1, 2, 256), (1, 2, 1024), 2),
    ("lru_lambda", (1, 2, 256), (1, 2, 1024), 2),
    ("even_w_out", (1, 512, 1024), (1, 2048, 1024), 1),
    ("odd_w_in", (1, 1024, 1280), (1, 1024, 5120), 2),
    ("hgrn_lb_logits", (2, 1024), (2, 1024), None),
    ("hgrn_norm_w", (1, 256), (1, 1024), 1),
    ("odd_w_out", (1, 256, 1024), (1, 1024, 1024), 1),
    ("norm_mix", (2, 1024), (2, 1024), None),
    ("norm_mlp", (2, 1024), (2, 1024), None),
    ("mlp_w1", (2, 1024, 1024), (2, 1024, 4096), 2),
    ("mlp_w2", (2, 1024, 1024), (2, 4096, 1024), 1),
    ("norm_final", (1024,), (1024,), None),
)
BIG = ("even_w_in", "even_w_out", "odd_w_in", "odd_w_out", "mlp_w1", "mlp_w2")
BIG_2D = {"even_w_in": (1024, 1288), "even_w_out": (512, 1024), "odd_w_in": (1024, 1280), "odd_w_out": (256, 1024),
          "mlp_w1": (2048, 1024), "mlp_w2": (2048, 1024)}
SMALL_SHARDED = ("even_conv_w", "lru_b_a", "lru_b_x", "lru_lambda", "hgrn_norm_w")


def _pcall(body, carry=None, **kw):
    if carry is not None:
        srcs, shapes, scratch, start, finish = carry
        grid, inner = kw["grid"], body
        as_tuple = lambda v: tuple(v) if isinstance(v, (tuple, list)) else (v,)
        out_specs, out_shape, own_scratch = as_tuple(kw["out_specs"]), as_tuple(kw["out_shape"]), list(kw.get("scratch_shapes", ()))
        a = len(kw["in_specs"])
        b = a + len(srcs)
        c = b + len(out_specs)
        d = c + len(shapes)
        e = d + len(own_scratch)

        def body(*refs):
            ids = [pl.program_id(ax) for ax in range(len(grid))]
            first = functools.reduce(jnp.logical_and, [i == 0 for i in ids])
            last = functools.reduce(jnp.logical_and, [i == g - 1 for i, g in zip(ids, grid)])
            pl.when(first)(lambda: start(refs[a:b], refs[c:d], *refs[e:]))
            inner(*refs[:a], *refs[b:c], *refs[d:e])
            pl.when(last)(lambda: finish(refs[a:b], refs[c:d], *refs[e:]))

        kw = dict(kw, in_specs=list(kw["in_specs"]) + [ANY] * len(srcs), out_specs=out_specs + (ANY,) * len(shapes),
                  out_shape=out_shape + tuple(shapes), scratch_shapes=own_scratch + list(scratch))
    return pl.pallas_call(body, **kw)


def _params(**kw):
    return pltpu.CompilerParams(vmem_limit_bytes=VMEM_LIMIT, **kw)


def _tile(n, pref):
    if n <= pref:
        return n
    t = (pref // 128) * 128
    while n % t:
        t -= 128
    return t


def _dot(a, b, dims=(((1,), (0,)), ((), ()))):
    return lax.dot_general(a, b, dims, preferred_element_type=F32)


_NN = (((1,), (0,)), ((), ()))
_NT = (((1,), (1,)), ((), ()))
_TN = (((0,), (0,)), ((), ()))


def _mx(v):
    return v.astype(MXU_DTYPE)


def _dot01(a, b, dims=_NN, *, split, terms):
    acc, rest = None, (a if split == "a" else b)
    for _ in range(terms):
        piece = _mx(rest)
        part = _dot(piece, _mx(b), dims) if split == "a" else _dot(_mx(a), piece, dims)
        acc = part if acc is None else acc + part
        rest = rest - piece.astype(F32)
    return acc


def _mm(name, a, b, mode, *, out_dtype=F32, res=None, relu2=False, relu2_of=None, col_shards=1, carry=None):
    shards = b.shape[0] if b.ndim == 3 else 0
    b2 = b.shape[1:] if shards else b.shape
    if mode == "nn":
        (m, kk), n = a.shape, b2[1] * max(shards, 1)
    elif mode == "nt":
        (m, kk), n = a.shape, b2[0]
    else:
        (kk, m), (_, n) = a.shape, b.shape
    assert res is None or relu2_of is None
    tk_pref = 1024
    if mode == "tn" and a.dtype.itemsize == 2 and b.dtype.itemsize == 2:
        tk_pref = 2048
    tm, tn, tk = _tile(m, 1024), _tile(n // col_shards, 1024), _tile(kk, tk_pref)
    nk = kk // tk
    dims = {"nn": _NN, "nt": _NT, "tn": _TN}[mode]
    a_spec = pl.BlockSpec((tk, tm), lambda i, j, k: (k, i)) if mode == "tn" else pl.BlockSpec((tm, tk), lambda i, j, k: (i, k))
    b_spec = pl.BlockSpec((tn, tk), lambda i, j, k: (j, k)) if mode == "nt" else pl.BlockSpec((tk, tn), lambda i, j, k: (k, j))
    if shards and mode == "nn":
        assert tn == b2[1]
        b_spec = pl.BlockSpec((None, tk, tn), lambda i, j, k: (j, k, 0))
    o_spec = pl.BlockSpec((tm, tn), lambda i, j, k: (i, j))
    o_shape = (m, n)
    if col_shards > 1:
        assert tn * col_shards == n and res is None and not relu2
        o_spec = pl.BlockSpec((None, tm, tn), lambda i, j, k: (j, i, 0))
        o_shape = (col_shards, m, tn)
    extra = res if res is not None else relu2_of
    has_res = extra is not None

    def body(*refs):
        a_ref, b_ref = refs[0], refs[1]
        res_ref = refs[2] if has_res else None
        outs = refs[2 + has_res:2 + has_res + 1 + relu2]

        def finish(r):
            if res is not None:
                r = r + res_ref[...]
            if relu2_of is not None:
                r = r * (2.0 * jnp.maximum(res_ref[...].astype(F32), 0.0))
            if relu2:
                outs[0][...] = r.astype(outs[0].dtype)
                outs[1][...] = jnp.square(jnp.maximum(r, 0.0)).astype(outs[1].dtype)
            else:
                outs[0][...] = r.astype(outs[0].dtype)

        prod = _dot(_mx(a_ref[...]), _mx(b_ref[...]), dims)
        if nk == 1:
            finish(prod)
            return
        acc = refs[-1]
        k = pl.program_id(2)

        @pl.when(k == 0)
        def _():
            acc[...] = prod

        @pl.when(k > 0)
        def _():
            acc[...] += prod

        @pl.when(k == nk - 1)
        def _():
            finish(acc[...])

    in_specs = [a_spec, b_spec] + ([o_spec] if has_res else [])
    if relu2:
        out_shape = (jax.ShapeDtypeStruct((m, n), BF16), jax.ShapeDtypeStruct((m, n), BF16))
        out_specs = (o_spec, o_spec)
    else:
        out_shape = jax.ShapeDtypeStruct(o_shape, out_dtype)
        out_specs = o_spec
    args = (a, b) + ((extra,) if has_res else ()) + (tuple(carry[0]) if carry else ())
    return _pcall(body, carry=carry, name=name, grid=(m // tm, n // tn, nk), in_specs=in_specs, out_specs=out_specs,
                  out_shape=out_shape, scratch_shapes=[pltpu.VMEM((tm, tn), F32)] if nk > 1 else [],
                  compiler_params=_params())(*args)


def _mm_sum_nt(name, parts, wblocks, norm_bwd=None):
    parts = [p if isinstance(p, tuple) else (p, 0, p.shape[1]) for p in parts]
    m, npart = parts[0][0].shape[0], len(parts)
    n = wblocks[0][0].shape[-2]
    tm, tn = _tile(m, 512), _tile(n, 1024)
    assert norm_bwd is None or tn == n

    def body(*refs):
        acc = _dot(_mx(refs[0][...]), _mx(refs[npart][...]), _NT)
        for k in range(1, npart):
            acc = acc + _dot(_mx(refs[k][...]), _mx(refs[npart + k][...]), _NT)
        if norm_bwd is None:
            refs[-1][...] = acc
            return
        x_ref, g_ref, res_ref, dx_ref, dg_ref = refs[2 * npart:]
        _, vjp = jax.vjp(_f_norm, x_ref[...], g_ref[...])
        dx, dg = vjp((acc,))
        dx_ref[...] = dx + res_ref[...]

        @pl.when(pl.program_id(0) == 0)
        def _():
            dg_ref[...] = jnp.zeros_like(dg_ref)

        dg_ref[...] += dg

    row = pl.BlockSpec((tm, tn), lambda i, j: (i, j))
    vec = pl.BlockSpec((1, tn), lambda i, j: (0, j))
    in_specs = [pl.BlockSpec((tm, wd), lambda i, j, cb=cb: (i, cb)) for _, cb, wd in parts]
    for (_, _, wd), (w, cb) in zip(parts, wblocks):
        in_specs.append(pl.BlockSpec((None, tn, wd), lambda i, j, cb=cb: (cb, j, 0)) if w.ndim == 3
                        else pl.BlockSpec((tn, wd), lambda i, j, cb=cb: (j, cb)))
    args = [p for p, _, _ in parts] + [w for w, _ in wblocks]
    if norm_bwd is None:
        return _pcall(body, name=name, grid=(m // tm, n // tn), in_specs=in_specs, out_specs=row,
                      out_shape=jax.ShapeDtypeStruct((m, n), F32), compiler_params=_params())(*args)
    return _pcall(body, name=name, grid=(m // tm, 1), in_specs=in_specs + [row, vec, row], out_specs=(row, vec),
                  out_shape=(jax.ShapeDtypeStruct((m, n), F32), jax.ShapeDtypeStruct((1, n), F32)),
                  compiler_params=_params())(*args, *norm_bwd)


def _pw_fwd(name, f, ins, params, out_dtypes, tc, ncol, tm=ROWS_FWD, groups=1):
    t = ins[0][0].shape[0]
    tm = min(tm, t)
    ni, npar = len(ins), len(params)
    gw = tc // groups

    def body(*refs):
        for g in range(groups):
            sl = slice(g * gw, (g + 1) * gw)
            vals = f(*[r[:, sl].astype(F32) for r in refs[:ni]], *[r[:, sl] for r in refs[ni:ni + npar]])
            for o, v in zip(refs[ni + npar:], vals):
                o[:, sl] = v.astype(o.dtype)

    in_specs = [pl.BlockSpec((tm, tc), lambda j, i, off=off: (i, off + j)) for _, off in ins]
    in_specs += [pl.BlockSpec((1, tc), lambda j, i, off=off: (0, off + j)) for _, off in params]
    out_specs = tuple(pl.BlockSpec((tm, tc), lambda j, i: (i, j)) for _ in out_dtypes)
    out_shape = tuple(jax.ShapeDtypeStruct((t, ncol * tc), d) for d in out_dtypes)
    return _pcall(body, name=name, grid=(ncol, t // tm), in_specs=in_specs, out_specs=out_specs, out_shape=out_shape,
                  compiler_params=_params())(*[a for a, _ in ins], *[p for p, _ in params])


def _pw_bwd(name, f, ins, params, douts, tc, ncol, want, adds=None, tm=ROWS_BWD, out_dtypes=None, groups=1):
    adds = adds or {}
    out_dtypes = out_dtypes or [F32] * len(want)
    t = ins[0][0].shape[0]
    tm = min(tm, t)
    ni, npar, nd, na = len(ins), len(params), len(douts), len(adds)
    add_keys = sorted(adds)
    gw = tc // groups

    def body(*refs):
        in_refs, p_refs = refs[:ni], refs[ni:ni + npar]
        d_refs = refs[ni + npar:ni + npar + nd]
        a_refs = refs[ni + npar + nd:ni + npar + nd + na]
        o_refs = refs[ni + npar + nd + na:]
        for p in range(npar):
            @pl.when(pl.program_id(1) == 0)
            def _(o=o_refs[len(want) + p]):
                o[...] = jnp.zeros_like(o)

        for g in range(groups):
            sl = slice(g * gw, (g + 1) * gw)
            _, vjp = jax.vjp(f, *[r[:, sl].astype(F32) for r in in_refs], *[r[:, sl] for r in p_refs])
            cts = vjp(tuple(d[:, sl].astype(F32) for d in d_refs))
            for o, kidx in zip(o_refs[:len(want)], want):
                v = cts[kidx]
                if kidx in adds:
                    v = v + a_refs[add_keys.index(kidx)][:, sl]
                o[:, sl] = v.astype(o.dtype)
            for p in range(npar):
                o_refs[len(want) + p][:, sl] += cts[ni + p]

    in_specs = [pl.BlockSpec((tm, tc), lambda j, i, off=off: (i, off + j)) for _, off in ins]
    in_specs += [pl.BlockSpec((1, tc), lambda j, i, off=off: (0, off + j)) for _, off in params]
    in_specs += [pl.BlockSpec((tm, tc), lambda j, i: (i, j)) for _ in range(nd + na)]
    out_specs = tuple([pl.BlockSpec((tm, tc), lambda j, i: (i, j)) for _ in want]
                      + [pl.BlockSpec((1, tc), lambda j, i: (0, j)) for _ in params])
    out_shape = tuple([jax.ShapeDtypeStruct((t, ncol * tc), dt) for dt in out_dtypes]
                      + [jax.ShapeDtypeStruct((1, ncol * tc), F32) for _ in params])
    res = _pcall(body, name=name, grid=(ncol, t // tm), in_specs=in_specs, out_specs=out_specs, out_shape=out_shape,
                 compiler_params=_params())(*[a for a, _ in ins], *[p for p, _ in params], *douts, *[adds[k] for k in add_keys])
    return list(res[:len(want)]), list(res[len(want):])


def _rms(x, g):
    return (x * lax.rsqrt(jnp.mean(x * x, axis=-1, keepdims=True) + EPS)) * g


def _f_norm(x, g):
    return (_rms(x, g),)


def _f_softplus(d, b):
    return (jax.nn.softplus(d + b),)


def _f_add3(a, b, c):
    return (a + b + c,)


def _f_ssd_post(yf, yb, xs, z, dskip, nw):
    u = (yf + yb + dskip * xs) * jax.nn.silu(z)
    return (_rms(u, nw),)


def _neg_expm1(v):
    t = jnp.tanh(0.5 * v)
    return -2.0 * t / (1.0 - t)


def _f_lru_gates(pre_a, pre_x, u, ba, bx, lam):
    rg = jax.nn.sigmoid(pre_a + ba)
    ig = jax.nn.sigmoid(pre_x + bx)
    log_a = -LRU_C * rg * jax.nn.softplus(-lam)
    return jnp.exp(log_a), jnp.sqrt(_neg_expm1(2.0 * log_a)) * (ig * u)


def _f_lru_post(hf, hb, gate):
    return ((hf + hb) * jax.nn.gelu(gate),)


def _f_hgrn_pre(fr, l0, l1):
    lb = jax.nn.sigmoid(l1 - l0)
    k = (1.0 - lb) * jax.nn.sigmoid(-fr)
    return k, jnp.log1p(-k)


def _f_hgrn_post(of, ob, gate, nw):
    return (_rms(of + ob, nw) * jax.nn.silu(gate),)


def _loss_head(x, tgt, g, tm=ROWS_FWD):
    t, d = x.shape
    tm = min(tm, t)

    def body(x_ref, t_ref, g_ref, dx_ref, dg_ref, loss_ref):
        tv = t_ref[...]

        def lf(xv, gv):
            return 0.5 * jnp.sum(jnp.mean(jnp.square(_rms(xv, gv) - tv), axis=-1))

        val, vjp = jax.vjp(lf, x_ref[...], g_ref[...])
        dx, dg = vjp(jnp.ones((), F32))
        dx_ref[...] = dx

        @pl.when(pl.program_id(0) == 0)
        def _():
            dg_ref[...] = jnp.zeros_like(dg_ref)
            loss_ref[...] = jnp.zeros_like(loss_ref)

        dg_ref[...] += dg
        loss_ref[...] += jnp.full(loss_ref.shape, val, F32)

    row = pl.BlockSpec((tm, d), lambda i: (i, 0))
    vec = pl.BlockSpec((1, d), lambda i: (0, 0))
    return _pcall(body, name="loss_head", grid=(t // tm,), in_specs=[row, row, vec],
                  out_specs=(row, vec, pl.BlockSpec((1, 128), lambda i: (0, 0))),
                  out_shape=(jax.ShapeDtypeStruct((t, d), F32), jax.ShapeDtypeStruct((1, d), F32),
                             jax.ShapeDtypeStruct((1, 128), F32)), compiler_params=_params())(x, tgt, g)


def _shifted(x, d, prev, nxt, first, last):
    r = x.shape[0]
    row = lax.broadcasted_iota(jnp.int32, x.shape, 0)
    if d < 0:
        out = pltpu.roll(x, -d, 0)
        for q in range(-d):
            pv = jnp.where(first, 0.0, prev[8 + d + q:8 + d + q + 1, :])
            out = jnp.where(row == q, pv, out)
        return out
    out = pltpu.roll(x, r - d, 0)
    for q in range(d):
        nv = jnp.where(last, 0.0, nxt[q:q + 1, :])
        out = jnp.where(row == r - d + q, nv, out)
    return out


def _conv_fwd(p3, w, b, col0, ncol, silu, tc=1024):
    nbatch, s, _ = p3.shape
    ts = min(CONV_ROWS, s)
    nblk = s // ts

    def body(x_ref, pv_ref, nx_ref, w_ref, b_ref, o_ref, *act_ref):
        i = pl.program_id(1)
        first, last = i == 0, i == nblk - 1
        x, pv, nx = x_ref[...], pv_ref[...], nx_ref[...]
        wv = w_ref[...]
        out = b_ref[...] + wv[1:2] * x
        out = out + wv[0:1] * _shifted(x, -1, pv, nx, first, last)
        out = out + wv[2:3] * _shifted(x, 1, pv, nx, first, last)
        out = out + wv[3:4] * _shifted(x, 2, pv, nx, first, last)
        o_ref[...] = out
        if silu:
            act_ref[0][...] = jax.nn.silu(out)

    nb8 = s // 8
    cur = pl.BlockSpec((None, ts, tc), lambda n, i, j: (n, i, col0 + j))
    prev = pl.BlockSpec((None, 8, tc), lambda n, i, j: (n, jnp.maximum(i * (ts // 8) - 1, 0), col0 + j))
    nxt = pl.BlockSpec((None, 8, tc), lambda n, i, j: (n, jnp.minimum((i + 1) * (ts // 8), nb8 - 1), col0 + j))
    out = pl.BlockSpec((None, ts, tc), lambda n, i, j: (n, i, j))
    shp = jax.ShapeDtypeStruct((nbatch, s, ncol * tc), F32)
    return _pcall(body, name=f"conv_fwd{col0}", grid=(nbatch, nblk, ncol),
                  in_specs=[cur, prev, nxt, pl.BlockSpec((4, tc), lambda n, i, j: (0, col0 + j)),
                            pl.BlockSpec((1, tc), lambda n, i, j: (0, col0 + j))],
                  out_specs=(out, out) if silu else out, out_shape=(shp, shp) if silu else shp,
                  compiler_params=_params())(p3, p3, p3, w, b)


def _conv_bwd(dc3, p3, w, col, conv3=None):
    nbatch, s, tc = dc3.shape
    ts = min(CONV_ROWS, s)
    nblk = s // ts
    silu = conv3 is not None

    def body(d_ref, dpv_ref, dnx_ref, x_ref, pv_ref, nx_ref, w_ref, *rest):
        n, i = pl.program_id(0), pl.program_id(1)
        first, last = i == 0, i == nblk - 1
        d, dpv, dnx = d_ref[...], dpv_ref[...], dnx_ref[...]
        if silu:
            d, dpv, dnx = [jax.vjp(jax.nn.silu, c_ref[...])[1](t)[0] for c_ref, t in zip(rest[:3], (d, dpv, dnx))]
        dx_ref, dw_ref = rest[3 * silu:]
        x, pv, nx = x_ref[...], pv_ref[...], nx_ref[...]
        wv = w_ref[...]
        dx = wv[1:2] * d
        dx = dx + wv[0:1] * _shifted(d, 1, dpv, dnx, first, last)
        dx = dx + wv[2:3] * _shifted(d, -1, dpv, dnx, first, last)
        dx = dx + wv[3:4] * _shifted(d, -2, dpv, dnx, first, last)
        dx_ref[...] = dx.astype(dx_ref.dtype)

        @pl.when((n == 0) & (i == 0))
        def _():
            dw_ref[...] = jnp.zeros_like(dw_ref)

        dw_ref[0:1, :] += jnp.sum(d * _shifted(x, -1, pv, nx, first, last), axis=0, keepdims=True)
        dw_ref[1:2, :] += jnp.sum(d * x, axis=0, keepdims=True)
        dw_ref[2:3, :] += jnp.sum(d * _shifted(x, 1, pv, nx, first, last), axis=0, keepdims=True)
        dw_ref[3:4, :] += jnp.sum(d * _shifted(x, 2, pv, nx, first, last), axis=0, keepdims=True)
        dw_ref[4:5, :] += jnp.sum(d, axis=0, keepdims=True)

    nb8 = s // 8

    def specs(j):
        cur = pl.BlockSpec((None, ts, tc), lambda n, i: (n, i, j))
        prev = pl.BlockSpec((None, 8, tc), lambda n, i: (n, jnp.maximum(i * (ts // 8) - 1, 0), j))
        nxt = pl.BlockSpec((None, 8, tc), lambda n, i: (n, jnp.minimum((i + 1) * (ts // 8), nb8 - 1), j))
        return [cur, prev, nxt]

    return _pcall(body, name=f"conv_bwd{col}", grid=(nbatch, nblk),
                  in_specs=specs(0) + specs(col) + [pl.BlockSpec((4, tc), lambda n, i: (0, col))] + specs(col) * silu,
                  out_specs=(specs(0)[0], pl.BlockSpec((8, tc), lambda n, i: (0, 0))),
                  out_shape=(jax.ShapeDtypeStruct((nbatch, s, tc), BF16), jax.ShapeDtypeStruct((8, tc), F32)),
                  compiler_params=_params())(dc3, dc3, dc3, p3, p3, p3, w, *([conv3] * 3 * silu))


def _block_scan(coef, inp, reverse):
    r = coef.shape[0]
    row = lax.broadcasted_iota(jnp.int32, coef.shape, 0)
    a, b = coef, inp
    d = 1
    while d < r:
        if reverse:
            keep = row < r - d
            a_sh, b_sh = pltpu.roll(a, r - d, 0), pltpu.roll(b, r - d, 0)
        else:
            keep = row >= d
            a_sh, b_sh = pltpu.roll(a, d, 0), pltpu.roll(b, d, 0)
        b = b + a * jnp.where(keep, b_sh, 0.0)
        a = a * jnp.where(keep, a_sh, 1.0)
        d *= 2
    return a, b


def _lru_scan(a3, b3, reverse):
    nbatch, s, w = a3.shape
    ts = min(LRU_ROWS, s)
    nblk = s // ts
    edge = 0 if reverse else ts - 1

    def body(a_ref, b_ref, h_ref, carry):
        @pl.when(pl.program_id(1) == 0)
        def _():
            carry[...] = jnp.zeros_like(carry)

        ca, hb = _block_scan(a_ref[...], b_ref[...], reverse)
        h = hb + ca * carry[0:1, :]
        h_ref[...] = h
        carry[0:1, :] = h[edge:edge + 1, :]

    blk = pl.BlockSpec((None, ts, w), (lambda n, i: (n, nblk - 1 - i, 0)) if reverse else (lambda n, i: (n, i, 0)))
    return _pcall(body, name=f"lru_scan_r{int(reverse)}", grid=(nbatch, nblk), in_specs=[blk, blk], out_specs=blk,
                  out_shape=jax.ShapeDtypeStruct((nbatch, s, w), F32), scratch_shapes=[pltpu.VMEM((8, w), F32)],
                  compiler_params=_params())(a3, b3)


def _lru_scan_bwd(a3, h3, dh3, reverse):
    nbatch, s, w = a3.shape
    ts = min(LRU_ROWS, s)
    nblk = s // ts
    nb8 = s // 8
    tpb = ts // 8

    def body(a_ref, aa_ref, h_ref, hh_ref, dh_ref, g_ref, da_ref, carry):
        i = pl.program_id(1)

        @pl.when(i == 0)
        def _():
            carry[...] = jnp.zeros_like(carry)

        a, h = a_ref[...], h_ref[...]
        row = lax.broadcasted_iota(jnp.int32, a.shape, 0)
        if reverse:
            a_edge = jnp.where(i == 0, 0.0, aa_ref[7:8, :])
            c = jnp.where(row == 0, a_edge, pltpu.roll(a, 1, 0))
            h_edge = jnp.where(i == nblk - 1, 0.0, hh_ref[0:1, :])
            h_sh = jnp.where(row == ts - 1, h_edge, pltpu.roll(h, ts - 1, 0))
        else:
            a_edge = jnp.where(i == 0, 0.0, aa_ref[0:1, :])
            c = jnp.where(row == ts - 1, a_edge, pltpu.roll(a, ts - 1, 0))
            h_edge = jnp.where(i == nblk - 1, 0.0, hh_ref[7:8, :])
            h_sh = jnp.where(row == 0, h_edge, pltpu.roll(h, 1, 0))
        cc, gb = _block_scan(c, dh_ref[...], not reverse)
        g = gb + cc * carry[0:1, :]
        g_ref[...] = g
        carry[0:1, :] = g[ts - 1:ts, :] if reverse else g[0:1, :]
        da_ref[...] = g * h_sh

    if reverse:
        bi = lambda i: i
    else:
        bi = lambda i: nblk - 1 - i
    blk = pl.BlockSpec((None, ts, w), lambda n, i: (n, bi(i), 0))
    before = pl.BlockSpec((None, 8, w), lambda n, i: (n, jnp.maximum(bi(i) * tpb - 1, 0), 0))
    after = pl.BlockSpec((None, 8, w), lambda n, i: (n, jnp.minimum((bi(i) + 1) * tpb, nb8 - 1), 0))
    a_tile, h_tile = (before, after) if reverse else (after, before)
    return _pcall(body, name=f"lru_scan_bwd_r{int(reverse)}", grid=(nbatch, nblk), in_specs=[blk, a_tile, blk, h_tile, blk],
                  out_specs=(blk, blk),
                  out_shape=(jax.ShapeDtypeStruct((nbatch, s, w), F32), jax.ShapeDtypeStruct((nbatch, s, w), F32)),
                  scratch_shapes=[pltpu.VMEM((8, w), F32)], compiler_params=_params())(a3, a3, h3, h3, dh3)


def _head_expand(lane0):
    return (jnp.right_shift(lax.broadcasted_iota(jnp.int32, (128, 1024), 1), HEAD_SHIFT) + lane0
            == lax.broadcasted_iota(jnp.int32, (128, 1024), 0)).astype(F32)


def _head_reduce(lane0):
    return (jnp.right_shift(lax.broadcasted_iota(jnp.int32, (1024, 128), 0), HEAD_SHIFT) + lane0
            == lax.broadcasted_iota(jnp.int32, (1024, 128), 1)).astype(F32)


def _time_mask(q, reverse):
    ri = lax.broadcasted_iota(jnp.int32, (q, q), 0)
    ci = lax.broadcasted_iota(jnp.int32, (q, q), 1)
    return (ri <= ci) if reverse else (ri >= ci)


def _ssd_common(xs_ref, bc_ref, dt_ref, al_ref, reverse, lane0):
    q = xs_ref.shape[0]
    edge = 0 if reverse else q - 1
    dt = dt_ref[...]
    a = -jnp.exp(al_ref[...])
    mask = _time_mask(q, reverse)
    expand = _head_expand(lane0)
    cum = _dot01(mask.astype(F32), dt * a, split="b", terms=3)
    cum_x = _dot01(cum, expand, split="a", terms=3)
    dt_x = _dot01(dt, expand, split="a", terms=2)
    last_x = cum_x[edge:edge + 1, :]
    xs = xs_ref[...]
    bc = bc_ref[...]
    return dict(q=q, edge=edge, lane0=lane0, dt=dt, a=a, mask=mask, cum_t=cum.T, cum_x=cum_x, dt_x=dt_x, xs=xs,
                v=xs * dt_x, e_c=jnp.exp(cum_x), w=jnp.exp(last_x - cum_x), e_l=jnp.exp(last_x),
                bm=bc[:, :512], cm=bc[:, 512:])


def _ssd_decay(c, h):
    row = c["lane0"] + h
    seg = c["cum_x"][:, h * SSD_HEADDIM:h * SSD_HEADDIM + 1] - c["cum_t"][row:row + 1, :]
    return jnp.where(c["mask"], jnp.exp(jnp.minimum(seg, 0.0)), 0.0)


def _head_masks():
    lane = jnp.right_shift(lax.broadcasted_iota(jnp.int32, (1, 256), 1), HEAD_SHIFT)
    return [lane == e for e in range(4)]


def _ssd_fwd(xbc3, dt3, alog, reverse, carry=None):
    nbatch, s, _ = xbc3.shape
    q = min(SSD_CHUNK, s)
    nc = s // q
    lane0 = SSD_HEADS * int(reverse)

    def body(xs_ref, bc_ref, dt_ref, al_ref, y_ref, st_ref, st):
        @pl.when(pl.program_id(1) == 0)
        def _():
            st[...] = jnp.zeros_like(st)

        st_ref[...] = st[...]
        c = _ssd_common(xs_ref, bc_ref, dt_ref, al_ref, reverse, lane0)
        hm = _head_masks()
        for g in range(SSD_GROUPS):
            sl = slice(g * 256, (g + 1) * 256)
            cg, bg = _mx(c["cm"][:, g * 128:(g + 1) * 128]), _mx(c["bm"][:, g * 128:(g + 1) * 128])
            cb = _dot(cg, bg, _NT)
            vg = c["v"][:, sl]
            s0 = st[:, sl]
            yg = _dot(cg, _mx(s0)) * c["e_c"][:, sl]
            for e in range(4):
                m = _ssd_decay(c, 4 * g + e) * cb
                yg = yg + _dot(_mx(m), _mx(jnp.where(hm[e], vg, 0.0)))
            y_ref[:, sl] = yg
            st[:, sl] = c["e_l"][:, sl] * s0 + _dot(bg, _mx(vg * c["w"][:, sl]), _TN)

    ck = (lambda i: nc - 1 - i) if reverse else (lambda i: i)
    xs_spec = pl.BlockSpec((None, q, 1024), lambda n, i: (n, ck(i), 0))
    bc_spec = pl.BlockSpec((None, q, 1024), lambda n, i: (n, ck(i), 1))
    dt_spec = pl.BlockSpec((None, q, 128), lambda n, i: (n, ck(i), 0))
    al_spec = pl.BlockSpec((1, 128), lambda n, i: (0, 0))
    st_spec = pl.BlockSpec((None, None, 128, 1024), lambda n, i: (n, ck(i), 0, 0))
    return _pcall(body, carry=carry, name=f"ssd_fwd_r{int(reverse)}", grid=(nbatch, nc),
                  in_specs=[xs_spec, bc_spec, dt_spec, al_spec], out_specs=(xs_spec, st_spec),
                  out_shape=(jax.ShapeDtypeStruct((nbatch, s, 1024), F32), jax.ShapeDtypeStruct((nbatch, nc, 128, 1024), F32)),
                  scratch_shapes=[pltpu.VMEM((128, 1024), F32)],
                  compiler_params=_params())(xbc3, xbc3, dt3, alog, *(carry[0] if carry else ()))


def _ssd_bwd(xbc3, dt3, alog, st4, dy3, reverse, add_to=(), scatter=()):
    nbatch, s, _ = xbc3.shape
    q = min(SSD_CHUNK, s)
    nc = s // q
    lane0 = SSD_HEADS * int(reverse)
    nadd, ns = len(add_to), len(scatter)

    def body(xs_ref, bc_ref, dt_ref, al_ref, st0_ref, dy_ref, *rest):
        adds, srcs, rest = rest[:nadd], rest[nadd:nadd + ns], rest[nadd + ns:]
        (dxs_ref, dbc_ref, ddt_ref, dal_ref), lands, dst = rest[:4], rest[4:4 + ns], rest[4 + ns]
        n, i = pl.program_id(0), pl.program_id(1)
        if ns:
            sends, arrivals = _scatter_copies(srcs, lands, *rest[5 + ns:])

            @pl.when((n == 0) & (i == 0))
            def _():
                for cp in sends:
                    cp.start()

        @pl.when(i == 0)
        def _():
            dst[...] = jnp.zeros_like(dst)

        @pl.when((i == 0) & (n == 0))
        def _():
            dal_ref[...] = jnp.zeros_like(dal_ref)

        c = _ssd_common(xs_ref, bc_ref, dt_ref, al_ref, reverse, lane0)
        hm = _head_masks()
        reduce_m = _head_reduce(lane0)
        s0_all, ds1_all, dy = st0_ref[...], dst[...], dy_ref[...]
        lane = lax.broadcasted_iota(jnp.int32, (q, 128), 1)
        sub = lax.broadcasted_iota(jnp.int32, (128, q), 0)
        rowacc = jnp.zeros((q, 128), F32)
        colacc_t = jnp.zeros((128, q), F32)
        dv_l, yst_l, dvbar_l, dk_l, dc_l = [], [], [], [], []
        for g in range(SSD_GROUPS):
            sl = slice(g * 256, (g + 1) * 256)
            cg, bg = _mx(c["cm"][:, g * 128:(g + 1) * 128]), _mx(c["bm"][:, g * 128:(g + 1) * 128])
            cb = _dot(cg, bg, _NT)
            vg, dyg, wg, ecg = c["v"][:, sl], dy[:, sl], c["w"][:, sl], c["e_c"][:, sl]
            s0, ds1 = _mx(s0_all[:, sl]), _mx(ds1_all[:, sl])
            dye = _mx(dyg * ecg)
            yst_l.append(_dot(cg, s0) * ecg)
            dcg = _dot(dye, s0, _NT)
            dst[:, sl] = c["e_l"][:, sl] * ds1_all[:, sl] + _dot(cg, dye, _TN)
            vbar = _mx(vg * wg)
            dvbar = _dot(bg, ds1)
            dvbar_l.append(dvbar)
            dvg = dvbar * wg
            dkg = _dot(vbar, ds1, _NT)
            for e in range(4):
                h = 4 * g + e
                m = _ssd_decay(c, h)
                dyh, vh = _mx(jnp.where(hm[e], dyg, 0.0)), _mx(jnp.where(hm[e], vg, 0.0))
                dvg = dvg + _dot(_mx(m * cb), dyh, _TN)
                dcb = _dot(dyh, vh, _NT) * m
                dcbb = _mx(dcb)
                dcg = dcg + _dot(dcbb, bg)
                dkg = dkg + _dot(dcbb, cg, _TN)
                wmat = dcb * cb
                rowacc = jnp.where(lane == lane0 + h, jnp.sum(wmat, axis=1, keepdims=True), rowacc)
                colacc_t = jnp.where(sub == lane0 + h, jnp.sum(wmat, axis=0, keepdims=True), colacc_t)
            dv_l.append(dvg)
            dk_l.append(dkg)
            dc_l.append(dcg)
        dv = jnp.concatenate(dv_l, axis=1)
        yst = jnp.concatenate(yst_l, axis=1)
        dvbar = jnp.concatenate(dvbar_l, axis=1)
        t1 = _dot01(dy * yst, reduce_m, split="a", terms=3)
        t2 = _dot01(c["v"] * c["w"] * dvbar, reduce_m, split="a", terms=3)
        dlast = jnp.sum(t2, axis=0, keepdims=True) + _dot01(
            c["e_l"] * jnp.sum(ds1_all * s0_all, axis=0, keepdims=True), reduce_m, split="a", terms=2)
        dcum = rowacc - colacc_t.T + t1 - t2
        dcum = dcum + jnp.where(lax.broadcasted_iota(jnp.int32, (q, 128), 0) == c["edge"], dlast, 0.0)
        dda = _dot01(c["mask"].astype(F32), dcum, _TN, split="b", terms=3)
        ddt = dda * c["a"] + _dot01(dv * c["xs"], reduce_m, split="a", terms=2)
        dal_ref[...] += jnp.sum(dda * c["dt"], axis=0, keepdims=True) * c["a"]
        dxs = dv * c["dt_x"]
        dbc = jnp.concatenate(dk_l + dc_l, axis=1)
        if nadd:
            for a_ref in adds[:-2]:
                dxs = dxs + a_ref[...]
            dbc = dbc + adds[-2][...]
            ddt = ddt + adds[-1][...]
        ddt_ref[...] = ddt
        dxs_ref[...] = dxs
        dbc_ref[...] = dbc
        if ns:
            @pl.when((n == nbatch - 1) & (i == nc - 1))
            def _():
                for cp in arrivals:
                    cp.wait_recv()
                for cp in sends:
                    cp.wait_send()

    ck = (lambda i: i) if reverse else (lambda i: nc - 1 - i)
    xs_spec = pl.BlockSpec((None, q, 1024), lambda n, i: (n, ck(i), 0))
    bc_spec = pl.BlockSpec((None, q, 1024), lambda n, i: (n, ck(i), 1))
    dt_spec = pl.BlockSpec((None, q, 128), lambda n, i: (n, ck(i), 0))
    al_spec = pl.BlockSpec((1, 128), lambda n, i: (0, 0))
    st_spec = pl.BlockSpec((None, None, 128, 1024), lambda n, i: (n, ck(i), 0, 0))
    return _pcall(body, name=f"ssd_bwd_r{int(reverse)}", grid=(nbatch, nc),
                  in_specs=([xs_spec, bc_spec, dt_spec, al_spec, st_spec, xs_spec] + [xs_spec] * (nadd - 1)
                            + [dt_spec] * bool(nadd) + [ANY] * ns),
                  out_specs=(xs_spec, xs_spec, dt_spec, al_spec) + (ANY,) * ns,
                  out_shape=(jax.ShapeDtypeStruct((nbatch, s, 1024), F32), jax.ShapeDtypeStruct((nbatch, s, 1024), F32),
                             jax.ShapeDtypeStruct((nbatch, s, 128), F32), jax.ShapeDtypeStruct((1, 128), F32))
                  + tuple(jax.ShapeDtypeStruct(c.shape, c.dtype) for c in scatter),
                  scratch_shapes=[pltpu.VMEM((128, 1024), F32)] + (_scatter_scratch(ns) if ns else []),
                  compiler_params=_params())(xbc3, xbc3, dt3, alog, st4, dy3, *add_to, *scatter)


def _gla_block(q, k, g, reverse):
    bq = g.shape[0]
    nsub = bq // HGRN_SUB
    edge = 0 if reverse else bq - 1
    ri = lax.broadcasted_iota(jnp.int32, (bq, bq), 0)
    ci = lax.broadcasted_iota(jnp.int32, (bq, bq), 1)
    rb, cb = jnp.right_shift(ri, HGRN_SUB_SHIFT), jnp.right_shift(ci, HGRN_SUB_SHIFT)
    mask = (ri <= ci) if reverse else (ri >= ci)
    m_within = (mask & (rb == cb)).astype(F32)
    m_before = ((cb > rb) if reverse else (cb < rb)).astype(F32)
    bl = _dot01(m_within, g, split="b", terms=3)
    c = _dot01(m_before, g, split="b", terms=3)
    last = c[edge:edge + 1, :] + bl[edge:edge + 1, :]
    ebl, enbl, ec, elc = jnp.exp(bl), jnp.exp(-bl), jnp.exp(c), jnp.exp(last - c)
    qh = q * HGRN_SCALE * ebl
    kh = k * enbl
    blk = jnp.right_shift(lax.broadcasted_iota(jnp.int32, (bq, 1), 0), HGRN_SUB_SHIFT)
    scale = []
    for i in range(nsub):
        valid = (blk >= i) if reverse else (blk <= i)
        ex = jnp.where(valid, c[i * HGRN_SUB:i * HGRN_SUB + 1, :] - c, 0.0)
        scale.append(jnp.where(valid, jnp.exp(ex), 0.0))
    return dict(bq=bq, nsub=nsub, edge=edge, mask=mask, m_within=m_within, m_before=m_before, ebl=ebl, enbl=enbl, ec=ec,
                elc=elc, e_l=jnp.exp(last), qh=qh, qt=qh * ec, kh=kh, kb=kh * elc, scale=scale)


def _gla_scores(c, hs):
    keys = [_mx(c["kh"][:, hs] * c["scale"][i][:, hs]) for i in range(c["nsub"])]
    rows = [_dot(_mx(c["qh"][i * HGRN_SUB:(i + 1) * HGRN_SUB, hs]), keys[i], _NT) for i in range(c["nsub"])]
    return jnp.where(c["mask"], jnp.concatenate(rows, axis=0), 0.0), keys


def _gla_specs(nbatch, s, w, reverse_order):
    bq = min(HGRN_BLOCK, s)
    nblk = s // bq
    bi = (lambda i: nblk - 1 - i) if reverse_order else (lambda i: i)
    col = lambda cb: pl.BlockSpec((nbatch, bq, w), lambda i: (0, bi(i), cb))
    st_spec = pl.BlockSpec((nbatch, None, 128, w), lambda i: (0, bi(i), 0, 0))
    return bq, nblk, col, st_spec


def _gla_fwd(proj3, l0, l1, reverse):
    nbatch, s, w5 = proj3.shape
    w = w5 // 5
    bq, nblk, col, st_spec = _gla_specs(nbatch, s, w, reverse)
    vec = pl.BlockSpec((1, w), lambda i: (0, 0))

    def body(q_ref, f_ref, v_ref, l0_ref, l1_ref, o_ref, st_ref, st):
        @pl.when(pl.program_id(0) == 0)
        def _():
            st[...] = jnp.zeros_like(st)

        for b in range(nbatch):
            st_ref[b] = st[b]
            k, g = _f_hgrn_pre(f_ref[b], l0_ref[...], l1_ref[...])
            c = _gla_block(q_ref[b], k, g, reverse)
            v = v_ref[b]
            for h in range(HGRN_HEADS):
                hs = slice(h * 128, (h + 1) * 128)
                att, _ = _gla_scores(c, hs)
                vb = _mx(v[:, hs])
                s0 = st[b, :, hs]
                o_ref[b, :, hs] = _dot(_mx(att), vb) + _dot(_mx(c["qt"][:, hs]), _mx(s0), _NT)
                st[b, :, hs] = s0 * c["e_l"][:, hs] + _dot(vb, _mx(c["kb"][:, hs]), _TN)

    return _pcall(body, name=f"gla_fwd_r{int(reverse)}", grid=(nblk,),
                  in_specs=[col(0), col(1 + int(reverse)), col(3), vec, vec], out_specs=(col(0), st_spec),
                  out_shape=(jax.ShapeDtypeStruct((nbatch, s, w), F32), jax.ShapeDtypeStruct((nbatch, nblk, 128, w), F32)),
                  scratch_shapes=[pltpu.VMEM((nbatch, 128, w), F32)], compiler_params=_params())(proj3, proj3, proj3, l0, l1)


def _gla_bwd(proj3, l0, l1, st4, do3, reverse, add_to=None):
    nbatch, s, w5 = proj3.shape
    w = w5 // 5
    bq, nblk, col, st_spec = _gla_specs(nbatch, s, w, not reverse)
    nadd = 0 if add_to is None else 2
    vec = pl.BlockSpec((1, w), lambda i: (0, 0))

    def body(q_ref, f_ref, v_ref, l0_ref, l1_ref, st_ref, do_ref, *rest):
        adds, (dq_ref, df_ref, dv_ref, dl0_ref, dl1_ref, dst) = rest[:nadd], rest[nadd:]

        @pl.when(pl.program_id(0) == 0)
        def _():
            dst[...] = jnp.zeros_like(dst)
            dl0_ref[...] = jnp.zeros_like(dl0_ref)
            dl1_ref[...] = jnp.zeros_like(dl1_ref)

        row = lax.broadcasted_iota(jnp.int32, (bq, 128), 0)
        for b in range(nbatch):
            (k, g), pre_vjp = jax.vjp(_f_hgrn_pre, f_ref[b], l0_ref[...], l1_ref[...])
            c = _gla_block(q_ref[b], k, g, reverse)
            s0_all, ds1_all = st_ref[b], dst[b]
            v, dy = v_ref[b], do_ref[b]
            dbl_l, dc_l, dk_l = [], [], []
            for h in range(HGRN_HEADS):
                hs = slice(h * 128, (h + 1) * 128)
                att, keys = _gla_scores(c, hs)
                qh, qt, kh, kb = c["qh"][:, hs], c["qt"][:, hs], c["kh"][:, hs], c["kb"][:, hs]
                vb, dyb = _mx(v[:, hs]), _mx(dy[:, hs])
                s0, ds1 = s0_all[:, hs], ds1_all[:, hs]
                datt = _mx(jnp.where(c["mask"], _dot(dyb, vb, _NT), 0.0))
                dqh_rows = []
                dkh = jnp.zeros((bq, 128), F32)
                dc = jnp.zeros((bq, 128), F32)
                for i in range(c["nsub"]):
                    rs = slice(i * HGRN_SUB, (i + 1) * HGRN_SUB)
                    dqh_rows.append(_dot(datt[rs], keys[i]))
                    dki = _dot(datt[rs], _mx(qh[rs]), _TN)
                    sc = c["scale"][i][:, hs]
                    dkh = dkh + dki * sc
                    dex = dki * (kh * sc)
                    dc = dc - dex + jnp.where(row == i * HGRN_SUB, jnp.sum(dex, axis=0, keepdims=True), 0.0)
                dqt = _dot(dyb, _mx(s0))
                dkb = _dot(vb, _mx(ds1))
                dv = _dot(_mx(att), dyb, _TN) + _dot(_mx(kb), _mx(ds1), _NT)
                dst[b, :, hs] = c["e_l"][:, hs] * ds1 + _dot(dyb, _mx(qt), _TN)
                dqh = jnp.concatenate(dqh_rows, axis=0) + dqt * c["ec"][:, hs]
                dkh = dkh + dkb * c["elc"][:, hs]
                kbk = dkb * kb
                dlast = jnp.sum(kbk, axis=0, keepdims=True) + c["e_l"][:, hs] * jnp.sum(ds1 * s0, axis=0, keepdims=True)
                at_edge = jnp.where(row == c["edge"], dlast, 0.0)
                dc_l.append(dc + dqt * qt - kbk + at_edge)
                dbl_l.append(dqh * qh - dkh * kh + at_edge)
                dq = dqh * c["ebl"][:, hs] * HGRN_SCALE
                if nadd:
                    dq, dv = dq + adds[0][b, :, hs], dv + adds[1][b, :, hs]
                dq_ref[b, :, hs] = dq.astype(dq_ref.dtype)
                dv_ref[b, :, hs] = dv.astype(dv_ref.dtype)
                dk_l.append(dkh * c["enbl"][:, hs])
            dg = (_dot01(c["m_within"], jnp.concatenate(dbl_l, axis=1), _TN, split="b", terms=2)
                  + _dot01(c["m_before"], jnp.concatenate(dc_l, axis=1), _TN, split="b", terms=2))
            df, d0, d1 = pre_vjp((jnp.concatenate(dk_l, axis=1), dg))
            df_ref[b] = df.astype(df_ref.dtype)
            dl0_ref[...] += d0
            dl1_ref[...] += d1

    shp_sum = jax.ShapeDtypeStruct((nbatch, s, w), BF16 if nadd else F32)
    shp_vec = jax.ShapeDtypeStruct((1, w), F32)
    return _pcall(body, name=f"gla_bwd_r{int(reverse)}", grid=(nblk,),
                  in_specs=[col(0), col(1 + int(reverse)), col(3), vec, vec, st_spec, col(0)] + [col(0)] * nadd,
                  out_specs=(col(0), col(0), col(0), vec, vec),
                  out_shape=(shp_sum, jax.ShapeDtypeStruct((nbatch, s, w), BF16), shp_sum, shp_vec, shp_vec),
                  scratch_shapes=[pltpu.VMEM((nbatch, 128, w), F32)],
                  compiler_params=_params())(proj3, proj3, proj3, l0, l1, st4, do3, *(add_to or ()))


DIRS = (False, True)


def _block_diag(w):
    eye = jnp.eye(16, dtype=w.dtype)
    return (eye[:, None, :, None] * w[:, :, None, :]).reshape(1024, 1024)


def _diag_blocks(m):
    m4 = m.reshape(16, 64, 16, 64)
    return jnp.stack([m4[i, :, i, :] for i in range(16)], axis=0)


def _pad_lanes(v, n=128):
    return jnp.pad(v, [(0, 0)] * (v.ndim - 1) + [(0, n - v.shape[-1])])


def _mlp_fwd(tag, x, nw, w1, w2, carry=None):
    (h,) = _pw_fwd(f"{tag}_norm", _f_norm, [(x, 0)], [(nw, 0)], [BF16], 1024, 1)
    a, r, *got = _mm(f"{tag}_up", h, w1, "nn", relu2=True, carry=carry)
    return _mm(f"{tag}_down", r, w2, "nn", res=x), (h, a, r), got


def _mlp_bwd(tag, x, nw, w1, w2, saved, dxo):
    h, a, r = saved
    dw2 = _mm(f"{tag}_dw2", r, dxo, "tn")
    da = _mm(f"{tag}_da", dxo, w2, "nt", relu2_of=a, out_dtype=BF16)
    dw1 = _mm(f"{tag}_dw1", h, da, "tn", col_shards=4)
    dx, dnw = _mm_sum_nt(f"{tag}_dh", [(da, k, 1024) for k in range(4)], [(w1, k) for k in range(4)], norm_bwd=(x, nw, dxo))
    return dx, dw1, dw2, dnw


def _split_in0(pieces, dt_piece):
    tm = 256

    def body(p0, p1, p2, p3, p4, p5, o_ref):
        full = jnp.concatenate([p0[...], p1[...], p2[...], p3[...], p4[...], p5[:, :32]], axis=1)
        for j in range(4):
            o_ref[j] = full[:, 1288 * j:1288 * (j + 1)]

    blk = pl.BlockSpec((tm, 1024), lambda i: (i, 0))
    return _pcall(body, name="split_in0", grid=(1024 // tm,), in_specs=[blk] * 5 + [pl.BlockSpec((tm, 128), lambda i: (i, 0))],
                  out_specs=pl.BlockSpec((4, tm, 1288), lambda i: (0, i, 0)),
                  out_shape=jax.ShapeDtypeStruct((4, 1024, 1288), F32), compiler_params=_params())(*pieces, dt_piece)


def _assemble_in0(shards):
    tm = 256

    def body(s_ref, m_ref, d_ref):
        full = jnp.concatenate([s_ref[j] for j in range(4)], axis=1)
        m_ref[...] = full[:, :5120]
        d_ref[...] = jnp.concatenate([full[:, 5120:5152], jnp.zeros((tm, 96), full.dtype)], axis=1)

    return _pcall(body, name="assemble_in0", grid=(1024 // tm,), in_specs=[pl.BlockSpec((4, tm, 1288), lambda i: (0, i, 0))],
                  out_specs=(pl.BlockSpec((tm, 5120), lambda i: (i, 0)), pl.BlockSpec((tm, 128), lambda i: (i, 0))),
                  out_shape=(jax.ShapeDtypeStruct((1024, 5120), shards.dtype), jax.ShapeDtypeStruct((1024, 128), shards.dtype)),
                  compiler_params=_params())(shards)


EARLY = ("odd_w_in", "odd_w_out", "mlp_w1_l1", "mlp_w2_l1")
MID = ("even_w_out", "mlp_w1_l0", "mlp_w2_l0")
LATE = ("even_w_in",)


def _local_step(x3, tgt3, w, w_main0, w_dt0, pair_reduce=None, late=None):
    nb, s, d = x3.shape
    carries, arrived = late if late else ({}, None)
    t = nb * s
    x0 = x3.reshape(t, d)
    tgt = tgt3.reshape(t, d)
    grads = {}
    row = lambda v: v.reshape(1, -1)
    to3 = lambda v: v.reshape(nb, s, v.shape[-1])
    to2 = lambda v: v.reshape(-1, v.shape[-1])

    conv_w, conv_b = w["even_conv_w"][0], row(w["even_conv_b"][0])
    nmix0 = row(w["norm_mix"][0])
    (h0,) = _pw_fwd("l0_norm", _f_norm, [(x0, 0)], [(nmix0, 0)], [BF16], 1024, 1)
    proj0 = _mm("l0_proj", h0, w_main0, "nn")
    dt_raw = _mm("l0_proj_dt", h0, w_dt0, "nn")
    conv2, xbc3 = _conv_fwd(to3(proj0), conv_w, conv_b, 0, 2, True)
    u_lru = to2(_conv_fwd(to3(proj0), conv_w, conv_b, 2, 1, False))
    xbc = to2(xbc3)
    dt_bias = _pad_lanes(w["ssd_dt_bias"][0].reshape(1, 32))
    (dt,) = _pw_fwd("l0_dt", _f_softplus, [(dt_raw, 0)], [(dt_bias, 0)], [F32], 128, 1)
    dt3 = to3(dt)
    alog = _pad_lanes(w["ssd_a_log"][0].reshape(1, 32))
    ssd = [_ssd_fwd(xbc3, dt3, alog, r, carry=carries.get(key)) for r, key in zip(DIRS, ("mlp_w1", "mlp_w2"))]
    if late:
        w = {**w, **arrived("mlp_w1", ssd[0][2:]), **arrived("mlp_w2", ssd[1][2:])}
    yf, yb = to2(ssd[0][0]), to2(ssd[1][0])
    dskip = jnp.repeat(w["ssd_d"][0], SSD_HEADDIM).reshape(1, 1024)
    snw = row(w["ssd_norm_w"][0])
    ssd_ins = [(yf, 0), (yb, 0), (xbc, 0), (proj0, 3)]
    (ya,) = _pw_fwd("l0_ssd_post", _f_ssd_post, ssd_ins, [(dskip, 0), (snw, 0)], [BF16], 1024, 1, groups=SSD_GROUPS)
    w_gates = [_block_diag(w[k][0, r]).astype(MXU_DTYPE) for r in range(2) for k in ("lru_w_a", "lru_w_x")]
    pre = [_mm(f"l0_lru_pre{i}", u_lru, wg, "nn") for i, wg in enumerate(w_gates)]
    lru_par = [[(row(w[k][0, r]), 0) for k in ("lru_b_a", "lru_b_x", "lru_lambda")] for r in range(2)]
    lru_ins = [[(pre[2 * r], 0), (pre[2 * r + 1], 0), (u_lru, 0)] for r in range(2)]
    ab = [_pw_fwd(f"l0_lru_gates{r}", _f_lru_gates, lru_ins[r], lru_par[r], [F32, F32], 1024, 1) for r in range(2)]
    hs = [_lru_scan(to3(ab[r][0]), to3(ab[r][1]), DIRS[r]) for r in range(2)]
    lru_post_ins = [(to2(hs[0]), 0), (to2(hs[1]), 0), (proj0, 4)]
    (ybm,) = _pw_fwd("l0_lru_post", _f_lru_post, lru_post_ins, [], [BF16], 1024, 1)
    w_out0 = w["even_w_out"][0]
    x1 = _mm("l0_out_a", ya, w_out0[:1024], "nn", res=x0)
    x1 = _mm("l0_out_b", ybm, w_out0[1024:], "nn", res=x1)
    nmlp0 = row(w["norm_mlp"][0])
    x2, mlp0, got = _mlp_fwd("l0_mlp", x1, nmlp0, w["mlp_w1"][0], w["mlp_w2"][0], carry=carries.get("odd"))
    if late:
        w = {**w, **arrived("odd", got)}

    w_in1 = w["odd_w_in"][0]
    nmix1 = row(w["norm_mix"][1])
    (h1,) = _pw_fwd("l1_norm", _f_norm, [(x2, 0)], [(nmix1, 0)], [BF16], 1024, 1)
    proj1 = _mm("l1_proj", h1, w_in1, "nn")
    proj1_3 = to3(proj1)
    lb0, lb1 = row(w["hgrn_lb_logits"][0]), row(w["hgrn_lb_logits"][1])
    gla = [_gla_fwd(proj1_3, lb0, lb1, r) for r in DIRS]
    hnw = row(w["hgrn_norm_w"][0])
    hpost_ins = [(to2(gla[0][0]), 0), (to2(gla[1][0]), 0), (proj1, 4)]
    (yo,) = _pw_fwd("l1_hgrn_post", _f_hgrn_post, hpost_ins, [(hnw, 0)], [BF16], 1024, 1, groups=HGRN_HEADS)
    w_out1 = w["odd_w_out"][0]
    x3_ = _mm("l1_out", yo, w_out1, "nn", res=x2)
    nmlp1 = row(w["norm_mlp"][1])
    x4, mlp1, _ = _mlp_fwd("l1_mlp", x3_, nmlp1, w["mlp_w1"][1], w["mlp_w2"][1])

    dx4, dnf, loss = _loss_head(x4, tgt, row(w["norm_final"]))
    grads["norm_final"] = dnf.reshape(-1)

    dx3, dw1_1, dw2_1, dnmlp1 = _mlp_bwd("l1_mlp", x3_, nmlp1, w["mlp_w1"][1], w["mlp_w2"][1], mlp1, dx4)
    big = {"odd_w_out": _mm("l1_dwout", yo, dx3, "tn").reshape(4, 256, 1024)}
    dyo = _mm("l1_dyo", dx3, w_out1, "nt")
    (do, dgate1), (dhnw,) = _pw_bwd("l1_hgrn_post_b", _f_hgrn_post, hpost_ins, [(hnw, 0)], [dyo], 1024, 1, [0, 2],
                                    out_dtypes=[F32, BF16], groups=HGRN_HEADS, tm=ROWS_FWD)
    grads["hgrn_norm_w"] = dhnw
    do3 = to3(do)
    gb = [_gla_bwd(proj1_3, lb0, lb1, gla[0][1], do3, False)]
    gb.append(_gla_bwd(proj1_3, lb0, lb1, gla[1][1], do3, True, add_to=(gb[0][0], gb[0][2])))
    grads["hgrn_lb_logits"] = jnp.concatenate([gb[0][3] + gb[1][3], gb[0][4] + gb[1][4]], axis=0)
    dparts1 = [to2(gb[1][0]), to2(gb[0][1]), to2(gb[1][1]), to2(gb[1][2]), dgate1]
    dwin1 = jnp.concatenate([_mm(f"l1_dwin{i}", h1, dp, "tn") for i, dp in enumerate(dparts1)], axis=1)
    big["odd_w_in"] = dwin1.reshape(1024, 4, 1280).transpose(1, 0, 2)
    dx2, dnmix1 = _mm_sum_nt("l1_dh", dparts1, [(w_in1, i) for i in range(5)], norm_bwd=(x2, nmix1, dx3))
    big["mlp_w1_l1"], big["mlp_w2_l1"] = dw1_1, dw2_1.reshape(4, 1024, 1024)
    early_sums = tuple(pair_reduce(EARLY, [big[n] for n in EARLY])) if pair_reduce else ()

    dx1, dw1_0, dw2_0, dnmlp0 = _mlp_bwd("l0_mlp", x1, nmlp0, w["mlp_w1"][0], w["mlp_w2"][0], mlp0, dx2)
    big["mlp_w1_l0"], big["mlp_w2_l0"] = dw1_0, dw2_0.reshape(4, 1024, 1024)
    grads["norm_mlp"] = jnp.concatenate([dnmlp0, dnmlp1], axis=0)
    big["even_w_out"] = jnp.concatenate([_mm("l0_dwout_a", ya, dx1, "tn"), _mm("l0_dwout_b", ybm, dx1, "tn")],
                                        axis=0).reshape(4, 512, 1024)
    mid_sums = tuple(pair_reduce(MID, [big[n] for n in MID])) if pair_reduce else ()
    dya = _mm("l0_dya", dx1, w_out0[:1024], "nt")
    dyb = _mm("l0_dyb", dx1, w_out0[1024:], "nt")
    (dh, dgate0), _ = _pw_bwd("l0_lru_post_b", _f_lru_post, lru_post_ins, [], [dyb], 1024, 1, [0, 2], out_dtypes=[F32, BF16],
                               tm=ROWS_FWD)
    dh3 = to3(dh)
    dpre, du_parts, dlru = [], [], {k: [] for k in ("lru_b_a", "lru_b_x", "lru_lambda")}
    for r in range(2):
        g_r, da_r = _lru_scan_bwd(to3(ab[r][0]), hs[r], dh3, DIRS[r])
        (dpa, dpx, du_r), (dba, dbx, dlam) = _pw_bwd(f"l0_lru_gates_b{r}", _f_lru_gates, lru_ins[r], lru_par[r],
                                                     [to2(da_r), to2(g_r)], 1024, 1, [0, 1, 2],
                                                     out_dtypes=[BF16, BF16, F32])
        dpre += [dpa, dpx]
        du_parts.append(du_r)
        dlru["lru_b_a"].append(dba)
        dlru["lru_b_x"].append(dbx)
        dlru["lru_lambda"].append(dlam)
    for k, v in dlru.items():
        grads[k] = jnp.concatenate(v, axis=0)[None]
    dwg = [_diag_blocks(_mm(f"l0_dwgate{i}", u_lru, dp, "tn")) for i, dp in enumerate(dpre)]
    grads["lru_w_a"] = jnp.stack([dwg[0], dwg[2]])[None]
    grads["lru_w_x"] = jnp.stack([dwg[1], dwg[3]])[None]
    du_gate = _mm_sum_nt("l0_du_gate", dpre, [(wg, 0) for wg in w_gates])
    (du,) = _pw_fwd("l0_du", _f_add3, [(du_parts[0], 0), (du_parts[1], 0), (du_gate, 0)], [], [F32], 1024, 1)
    (dy, dxs_skip, dz), (ddskip, dsnw) = _pw_bwd("l0_ssd_post_b", _f_ssd_post, ssd_ins, [(dskip, 0), (snw, 0)], [dya],
                                                 1024, 1, [0, 2, 3], out_dtypes=[F32, F32, BF16], groups=SSD_GROUPS)
    grads["ssd_d"] = ddskip.reshape(SSD_HEADS, SSD_HEADDIM).sum(axis=1)[None]
    grads["ssd_norm_w"] = dsnw
    dy3 = to3(dy)
    sb0 = _ssd_bwd(xbc3, dt3, alog, ssd[0][1], dy3, False, scatter=early_sums)
    sb1 = _ssd_bwd(xbc3, dt3, alog, ssd[1][1], dy3, True, add_to=(sb0[0], to3(dxs_skip), sb0[1], sb0[2]), scatter=mid_sums)
    grads["ssd_a_log"] = (sb0[3] + sb1[3])[:, :32].reshape(1, 2, 16)
    ddt = to2(sb1[2])
    (ddt_raw,), (ddtb,) = _pw_bwd("l0_dt_b", _f_softplus, [(dt_raw, 0)], [(dt_bias, 0)], [ddt], 128, 1, [0])
    grads["ssd_dt_bias"] = ddtb[:, :32].reshape(1, 2, 16)
    cb = [_conv_bwd(sb1[0], to3(proj0), conv_w, 0, conv2), _conv_bwd(sb1[1], to3(proj0), conv_w, 1, conv2),
          _conv_bwd(to3(du), to3(proj0), conv_w, 2)]
    dcw = jnp.concatenate([c_[1] for c_ in cb], axis=1)
    grads["even_conv_w"] = dcw[:4][None]
    grads["even_conv_b"] = dcw[4:5]
    dparts0 = [to2(c_[0]) for c_ in cb] + [dz, dgate0]
    dwin0 = [_mm(f"l0_dwin{i}", h0, dp, "tn") for i, dp in enumerate(dparts0)]
    big["even_w_in"] = _split_in0(dwin0, _mm("l0_dwin_dt", h0, ddt_raw, "tn"))
    dx0, dnmix0 = _mm_sum_nt("l0_dh", dparts0 + [ddt_raw], [(w_main0, i) for i in range(5)] + [(w_dt0, 0)],
                             norm_bwd=(x0, nmix0, dx1))
    grads["norm_mix"] = jnp.concatenate([dnmix0, dnmix1], axis=0)
    return loss, dx0.reshape(nb, s, d), grads, big, (early_sums + mid_sums, sb0[4:] + sb1[4:])


ANY = pl.BlockSpec(memory_space=pl.ANY)


def _place():
    return lax.axis_index("x"), lax.axis_index("y"), lax.axis_index("c")


def _remote(src, dst, send_sems, recv_sems, k, to):
    return pltpu.make_async_remote_copy(src_ref=src, dst_ref=dst, send_sem=send_sems.at[k], recv_sem=recv_sems.at[k],
                                        device_id=to, device_id_type=MESH)


def _gather_start(x_refs, out_refs, send_sems, recv_sems, finish=False):
    n = len(x_refs)
    halves = [r.shape[0] // 2 for r in x_refs]
    x, y, c = _place()
    sibling = (x, y, 1 - c)
    chips = [(1 - x, y), (x, 1 - y), (1 - x, 1 - y)]

    def blk(t, px, py, hc):
        return out_refs[t].at[2 * px + py, pl.ds(hc * halves[t], halves[t]), :]

    def src(t):
        return x_refs[t].at[pl.ds(c * halves[t], halves[t]), :]

    first = [_remote(src(t), blk(t, x, y, c), send_sems, recv_sems, 6 * t + j, (*chip, c))
             for t in range(n) for j, chip in enumerate(chips)]
    if not finish:
        for cp in first:
            cp.start()
        return
    passed = []
    for t in range(n):
        for j, chip in enumerate(chips):
            _remote(src(t), blk(t, *chip, c), send_sems, recv_sems, 6 * t + j, (*chip, c)).wait_recv()
            cp = _remote(blk(t, *chip, c), blk(t, *chip, c), send_sems, recv_sems, 6 * t + 3 + j, sibling)
            cp.start()
            passed.append(cp)
    for t in range(n):
        for j, chip in enumerate(chips):
            _remote(src(t), blk(t, *chip, 1 - c), send_sems, recv_sems, 6 * t + 3 + j, sibling).wait_recv()
    for cp in first + passed:
        cp.wait_send()


_gather_finish = functools.partial(_gather_start, finish=True)


def _gather_carry(shards):
    n = len(shards)
    return (list(shards), [jax.ShapeDtypeStruct((4,) + s.shape, s.dtype) for s in shards],
            [pltpu.SemaphoreType.DMA((6 * n,)), pltpu.SemaphoreType.DMA((6 * n,))], _gather_start, _gather_finish)


def _gather_chips(shards):
    n = len(shards)
    srcs, shapes, scratch, start, finish = _gather_carry(shards)

    def body(*refs):
        start(refs[:n], refs[n:2 * n], *refs[2 * n:])
        finish(refs[:n], refs[n:2 * n], *refs[2 * n:])

    return _pcall(body, name="gather_weights", in_specs=[ANY] * n, out_specs=(ANY,) * n, out_shape=tuple(shapes),
                  scratch_shapes=scratch, compiler_params=_params())(*shards)


def _pair_swap(name, gps):
    n = len(gps)
    halves = [g.shape[1] // 2 for g in gps]

    def body(*refs):
        g_refs, land_refs = refs[:n], refs[n:2 * n]
        send_sems, recv_sems = refs[2 * n:]
        x, y, c = _place()
        cps = [_remote(g_refs[t].at[j, pl.ds((1 - c) * halves[t], halves[t]), :], land_refs[t].at[j], send_sems, recv_sems,
                       4 * t + j, (x, y, 1 - c)) for t in range(n) for j in range(4)]
        for cp in cps:
            cp.start()
        for cp in cps:
            cp.wait()

    return _pcall(body, name=f"pair_swap_{name}", in_specs=[ANY] * n, out_specs=(ANY,) * n,
                  out_shape=tuple(jax.ShapeDtypeStruct((4, h, g.shape[2]), F32) for g, h in zip(gps, halves)),
                  scratch_shapes=[pltpu.SemaphoreType.DMA((4 * n,)), pltpu.SemaphoreType.DMA((4 * n,))],
                  compiler_params=_params())(*gps)


def _pair_add(name, gp, land, cidx):
    _, half, cols = land.shape
    tr = _tile(half, 512)
    nh = half // tr

    def body(c_ref, g_ref, l_ref, o_ref):
        o_ref[...] = (g_ref[...] + l_ref[...]).astype(o_ref.dtype)

    grid_spec = pltpu.PrefetchScalarGridSpec(
        num_scalar_prefetch=1, grid=(4, nh),
        in_specs=[pl.BlockSpec((None, tr, cols), lambda j, i, c: (j, c[0] * nh + i, 0)),
                  pl.BlockSpec((None, tr, cols), lambda j, i, c: (j, i, 0))],
        out_specs=pl.BlockSpec((None, tr, cols), lambda j, i, c: (j, i, 0)))
    return _pcall(body, name=f"pair_add_{name}", grid_spec=grid_spec, out_shape=jax.ShapeDtypeStruct((4, half, cols), BF16),
                  compiler_params=_params())(cidx, gp, land)


def _scatter_copies(s_refs, land_refs, send_sems, recv_sems):
    x, y, c = _place()
    me = 2 * x + y
    chips = [(1 - x, y), (x, 1 - y), (1 - x, 1 - y)]
    pairs = [(t, j, px, py) for t in range(len(s_refs)) for j, (px, py) in enumerate(chips)]
    sends = [_remote(s_refs[t].at[2 * px + py], land_refs[t].at[me], send_sems, recv_sems, 3 * t + j, (px, py, c))
             for t, j, px, py in pairs]
    arrivals = [_remote(s_refs[t].at[me], land_refs[t].at[2 * px + py], send_sems, recv_sems, 3 * t + j, (px, py, c))
                for t, j, px, py in pairs]
    return sends, arrivals


def _scatter_scratch(n):
    return [pltpu.SemaphoreType.DMA((3 * n,)), pltpu.SemaphoreType.DMA((3 * n,))]


def _chip_scatter(name, css):
    n = len(css)

    def body(*refs):
        sends, arrivals = _scatter_copies(refs[:n], refs[n:2 * n], *refs[2 * n:])
        for cp in sends:
            cp.start()
        for cp in arrivals:
            cp.wait_recv()
        for cp in sends:
            cp.wait_send()

    return _pcall(body, name=f"chip_scatter_{name}", in_specs=[ANY] * n, out_specs=(ANY,) * n,
                  out_shape=tuple(jax.ShapeDtypeStruct(s.shape, s.dtype) for s in css),
                  scratch_shapes=_scatter_scratch(n), compiler_params=_params())(*css)


def _chip_sum(name, land):
    _, half, cols = land.shape
    tr = _tile(half, 512)

    def body(l_ref, o_ref):
        o_ref[...] = ((l_ref[0].astype(F32) + l_ref[1].astype(F32)) + l_ref[2].astype(F32)) + l_ref[3].astype(F32)

    return _pcall(body, name=f"chip_sum_{name}", grid=(half // tr,),
                  in_specs=[pl.BlockSpec((4, tr, cols), lambda i: (0, i, 0))],
                  out_specs=pl.BlockSpec((tr, cols), lambda i: (i, 0)),
                  out_shape=jax.ShapeDtypeStruct((half, cols), F32), compiler_params=_params())(land)


def _pair_join(reds):
    n = len(reds)

    def body(*refs):
        r_refs, out_refs = refs[:n], refs[n:2 * n]
        send_sems, recv_sems = refs[2 * n:]
        x, y, c = _place()
        cps = [_remote(r_refs[t], out_refs[t].at[c], send_sems, recv_sems, t, (x, y, 1 - c)) for t in range(n)]
        for cp in cps:
            cp.start()
        for t in range(n):
            _remote(r_refs[t], out_refs[t].at[1 - c], send_sems, recv_sems, t, (x, y, 1 - c)).wait_recv()
        for cp in cps:
            cp.wait_send()

    return _pcall(body, name="grad_pair_join", in_specs=[ANY] * n, out_specs=(ANY,) * n,
                  out_shape=tuple(jax.ShapeDtypeStruct((2,) + r.shape, F32) for r in reds),
                  scratch_shapes=[pltpu.SemaphoreType.DMA((n,)), pltpu.SemaphoreType.DMA((n,))],
                  compiler_params=_params())(*reds)


def _adamw(name, g, w, m, v):
    rows, cols = g.shape
    tr = _tile(rows, 512)

    def body(g_ref, w_ref, m_ref, v_ref, d_ref, mo_ref, vo_ref):
        gv = g_ref[...]
        mn = ADAM_B1 * m_ref[...] + (1.0 - ADAM_B1) * gv
        vn = ADAM_B2 * v_ref[...] + (1.0 - ADAM_B2) * jnp.square(gv)
        m_hat = mn / (1.0 - ADAM_B1 ** ADAM_STEP)
        v_hat = vn / (1.0 - ADAM_B2 ** ADAM_STEP)
        d_ref[...] = -ADAM_LR * (m_hat / (jnp.sqrt(v_hat) + ADAM_EPS) + ADAM_WD * w_ref[...])
        mo_ref[...] = mn
        vo_ref[...] = vn

    blk = pl.BlockSpec((tr, cols), lambda i: (i, 0))
    shp = jax.ShapeDtypeStruct((rows, cols), F32)
    return _pcall(body, name=f"adamw_{name}", grid=(rows // tr,), in_specs=[blk] * 4, out_specs=(blk,) * 3,
                  out_shape=(shp,) * 3, compiler_params=_params())(g, w, m, v)


def _pack(pieces, rows, dtype):
    flat = jnp.concatenate([p.reshape(-1).astype(dtype) for p in pieces])
    return jnp.pad(flat, (0, rows * PACK_COLS - flat.shape[0])).reshape(rows, PACK_COLS)


def _unpack(pack, shapes):
    flat = pack.reshape(-1)
    out, off = [], 0
    for shp in shapes:
        n = math.prod(shp)
        out.append(flat[off:off + n].reshape(shp))
        off += n
    return out


def _shard_of(full, axis, j):
    n = full.shape[axis] // 4
    return lax.slice_in_dim(full, j * n, (j + 1) * n, axis=axis)


def kernel(x, even_w_in, even_conv_w, even_conv_b, ssd_a_log, ssd_dt_bias, ssd_d, ssd_norm_w, lru_w_a, lru_b_a, lru_w_x, lru_b_x, lru_lambda, even_w_out, odd_w_in, hgrn_lb_logits, hgrn_norm_w, odd_w_out, norm_mix, norm_mlp, mlp_w1, mlp_w2, norm_final, loss_target, m_even_w_in, m_even_conv_w, m_even_conv_b, m_ssd_a_log, m_ssd_dt_bias, m_ssd_d, m_ssd_norm_w, m_lru_w_a, m_lru_b_a, m_lru_w_x, m_lru_b_x, m_lru_lambda, m_even_w_out, m_odd_w_in, m_hgrn_lb_logits, m_hgrn_norm_w, m_odd_w_out, m_norm_mix, m_norm_mlp, m_mlp_w1, m_mlp_w2, m_norm_final, v_even_w_in, v_even_conv_w, v_even_conv_b, v_ssd_a_log, v_ssd_dt_bias, v_ssd_d, v_ssd_norm_w, v_lru_w_a, v_lru_b_a, v_lru_w_x, v_lru_b_x, v_lru_lambda, v_even_w_out, v_odd_w_in, v_hgrn_lb_logits, v_hgrn_norm_w, v_odd_w_out, v_norm_mix, v_norm_mlp, v_mlp_w1, v_mlp_w2, v_norm_final):
    names = [n for n, _, _, _ in WEIGHTS]
    w_loc = dict(zip(names, (even_w_in, even_conv_w, even_conv_b, ssd_a_log, ssd_dt_bias, ssd_d, ssd_norm_w, lru_w_a, lru_b_a, lru_w_x, lru_b_x, lru_lambda, even_w_out, odd_w_in, hgrn_lb_logits, hgrn_norm_w, odd_w_out, norm_mix, norm_mlp, mlp_w1, mlp_w2, norm_final)))
    m_loc = dict(zip(names, (m_even_w_in, m_even_conv_w, m_even_conv_b, m_ssd_a_log, m_ssd_dt_bias, m_ssd_d, m_ssd_norm_w, m_lru_w_a, m_lru_b_a, m_lru_w_x, m_lru_b_x, m_lru_lambda, m_even_w_out, m_odd_w_in, m_hgrn_lb_logits, m_hgrn_norm_w, m_odd_w_out, m_norm_mix, m_norm_mlp, m_mlp_w1, m_mlp_w2, m_norm_final)))
    v_loc = dict(zip(names, (v_even_w_in, v_even_conv_w, v_even_conv_b, v_ssd_a_log, v_ssd_dt_bias, v_ssd_d, v_ssd_norm_w, v_lru_w_a, v_lru_b_a, v_lru_w_x, v_lru_b_x, v_lru_lambda, v_even_w_out, v_odd_w_in, v_hgrn_lb_logits, v_hgrn_norm_w, v_odd_w_out, v_norm_mix, v_norm_mlp, v_mlp_w1, v_mlp_w2, v_norm_final)))
    spec = {n: (blk, full, ax) for n, blk, full, ax in WEIGHTS}

    small = [n for n in names if n not in BIG]
    two_d = lambda n, v: v.reshape(BIG_2D[n])

    me = 2 * lax.axis_index("x") + lax.axis_index("y")
    cc = lax.axis_index("c")
    put = lambda whole, part, k: lax.dynamic_update_slice_in_dim(whole, part[None], k, axis=0)
    own = {n: two_d(n, w_loc[n]).astype(BF16) for n in BIG}
    own["small"] = _pack([w_loc[n] for n in SMALL_SHARDED], 16, F32)
    fill = lambda got, keys: [put(g, own[k], me) for g, k in zip(got, keys)]
    first = ("even_w_in", "even_w_out", "small")
    g_in0, g_out0, g_small = fill(_gather_chips([own[k] for k in first]), first)
    w_main0, w_dt0 = _assemble_in0(g_in0)
    w_full = {n: w_loc[n] for n in names if spec[n][2] is None}
    w_full["even_w_out"] = g_out0.reshape(1, 2048, 1024)
    shards = [_unpack(g_small[j], [spec[n][0] for n in SMALL_SHARDED]) for j in range(4)]
    for n in ("mlp_w1", "mlp_w2"):
        for l in range(2):
            own[f"{n}_l{l}"] = w_loc[n][l].astype(BF16)
    layers = lambda n: (f"{n}_l0", f"{n}_l1")
    carries = {"mlp_w1": _gather_carry([own[k] for k in layers("mlp_w1")]),
               "mlp_w2": _gather_carry([own[k] for k in layers("mlp_w2")]),
               "odd": _gather_carry([own["odd_w_in"], own["odd_w_out"]])}

    def arrived(key, got):
        if key == "odd":
            g_in1, g_out1 = fill(got, ("odd_w_in", "odd_w_out"))
            return {"odd_w_in": jnp.concatenate([g_in1[j] for j in range(4)], axis=1)[None],
                    "odd_w_out": g_out1.reshape(1, 1024, 1024)}
        g = fill(got, layers(key))
        return {key: g if key == "mlp_w1" else [v.reshape(4096, 1024) for v in g]}

    for i, n in enumerate(SMALL_SHARDED):
        w_full[n] = jnp.concatenate([shards[j][i] for j in range(4)], axis=spec[n][2])

    cidx = cc.astype(jnp.int32).reshape(1)

    def pair_reduce(tags, tensors):
        return [_pair_add(tag, g, land, cidx) for tag, g, land in zip(tags, tensors, _pair_swap(tags[0], tensors))]

    loss_vec, grad_x, grads, big, (early_sums, early_landed) = _local_step(
        x, loss_target, w_full, w_main0, w_dt0, pair_reduce, (carries, arrived))
    loss = lax.psum(loss_vec[0, 0], ("x", "y", "c"))

    def dest_pack(j):
        return _pack([grads[n].reshape(spec[n][1]) if spec[n][2] is None else _shard_of(grads[n].reshape(spec[n][1]), spec[n][2], j)
                      for n in small], SMALL_ROWS, F32)

    late_tags = LATE + ("small",)
    late_sums = pair_reduce(late_tags, [big[n] for n in LATE] + [jnp.stack([dest_pack(j) for j in range(4)])])
    tags = EARLY + MID + late_tags
    chip_sums = list(early_sums) + late_sums
    landed = [put(land, lax.dynamic_index_in_dim(cs, me, axis=0, keepdims=False), me)
              for land, cs in zip(list(early_landed) + list(_chip_scatter("late", late_sums)), chip_sums)]
    halves = [_chip_sum(tag, land) for tag, land in zip(tags, landed)]
    red = {tag: put(r, h, cc).reshape(-1, r.shape[-1]) for tag, r, h in zip(tags, _pair_join(halves), halves)}
    for n in ("mlp_w1", "mlp_w2"):
        red[n] = jnp.concatenate([red[n + "_l0"], red[n + "_l1"]], axis=0)

    outs = {}
    for n, g in ((n, red[n]) for n in BIG):
        res = (g, *_adamw(n, g, two_d(n, w_loc[n]), two_d(n, m_loc[n]), two_d(n, v_loc[n])))
        outs[n] = [r.reshape(spec[n][0]) for r in res]
    blocks = [spec[n][0] for n in small]
    wp, mp, vp = (_pack([src[n] for n in small], SMALL_ROWS, F32) for src in (w_loc, m_loc, v_loc))
    res = (red["small"], *_adamw("small", red["small"], wp, mp, vp))
    unpacked = [_unpack(r, blocks) for r in res]
    for i, n in enumerate(small):
        outs[n] = [u[i] for u in unpacked]
    return (loss, grad_x, *[outs[n][k] for k in range(4) for n in names])
```

```python
import functools
import math

import jax
import jax.numpy as jnp
from jax import lax
from jax.experimental import pallas as pl
from jax.experimental.pallas import tpu as pltpu

F32 = jnp.float32
BF16 = jnp.bfloat16
MXU_DTYPE = jnp.bfloat16
MESH = pl.DeviceIdType.MESH

D_MODEL = 1024
EPS = 1e-6
SSD_HEADS = 16
SSD_HEADDIM = 64
HEAD_SHIFT = 6
SSD_GROUPS = 4
SSD_STATE = 128
SSD_CHUNK = 128
LRU_C = 8.0
LRU_ROWS = 256
HGRN_HEADS = 8
HGRN_HEADDIM = 128
HGRN_SUB = 32
HGRN_SUB_SHIFT = 5
HGRN_BLOCK = 128
HGRN_SCALE = HGRN_HEADDIM ** -0.5
CONV_ROWS = 512
ROWS_FWD = 512
ROWS_BWD = 256

ADAM_LR = 0.001
ADAM_B1 = 0.9
ADAM_B2 = 0.999
ADAM_EPS = 1e-08
ADAM_WD = 0.01
ADAM_STEP = 10

VMEM_LIMIT = 56 * 1024 * 1024
PACK_COLS = 1024
SMALL_ROWS = 288

WEIGHTS = (
    ("even_w_in", (1, 1024, 1288), (1, 1024, 5152), 2),
    ("even_conv_w", (1, 4, 768), (1, 4, 3072), 2),
    ("even_conv_b", (1, 3072), (1, 3072), None),
    ("ssd_a_log", (1, 2, 16), (1, 2, 16), None),
    ("ssd_dt_bias", (1, 2, 16), (1, 2, 16), None),
    ("ssd_d", (1, 16), (1, 16), None),
    ("ssd_norm_w", (1, 1024), (1, 1024), None),
    ("lru_w_a", (1, 2, 16, 64, 64), (1, 2, 16, 64, 64), None),
    ("lru_b_a", (1, 2, 256), (1, 2, 1024), 2),
    ("lru_w_x", (1, 2, 16, 64, 64), (1, 2, 16, 64, 64), None),
    ("lru_b_x", (1, 2, 256), (1, 2, 1024), 2),
    ("lru_lambda", (1, 2, 256), (1, 2, 1024), 2),
    ("even_w_out", (1, 512, 1024), (1, 2048, 1024), 1),
    ("odd_w_in", (1, 1024, 1280), (1, 1024, 5120), 2),
    ("hgrn_lb_logits", (2, 1024), (2, 1024), None),
    ("hgrn_norm_w", (1, 256), (1, 1024), 1),
    ("odd_w_out", (1, 256, 1024), (1, 1024, 1024), 1),
    ("norm_mix", (2, 1024), (2, 1024), None),
    ("norm_mlp", (2, 1024), (2, 1024), None),
    ("mlp_w1", (2, 1024, 1024), (2, 1024, 4096), 2),
    ("mlp_w2", (2, 1024, 1024), (2, 4096, 1024), 1),
    ("norm_final", (1024,), (1024,), None),
)
BIG = ("even_w_in", "even_w_out", "odd_w_in", "odd_w_out", "mlp_w1", "mlp_w2")
BIG_2D = {"even_w_in": (1024, 1288), "even_w_out": (512, 1024), "odd_w_in": (1024, 1280), "odd_w_out": (256, 1024),
          "mlp_w1": (2048, 1024), "mlp_w2": (2048, 1024)}
SMALL_SHARDED = ("even_conv_w", "lru_b_a", "lru_b_x", "lru_lambda", "hgrn_norm_w")


def _pcall(body, carry=None, **kw):
    if carry is not None:
        srcs, shapes, scratch, start, finish = carry
        grid, inner = kw["grid"], body
        as_tuple = lambda v: tuple(v) if isinstance(v, (tuple, list)) else (v,)
        out_specs, out_shape, own_scratch = as_tuple(kw["out_specs"]), as_tuple(kw["out_shape"]), list(kw.get("scratch_shapes", ()))
        a = len(kw["in_specs"])
        b = a + len(srcs)
        c = b + len(out_specs)
        d = c + len(shapes)
        e = d + len(own_scratch)

        def body(*refs):
            ids = [pl.program_id(ax) for ax in range(len(grid))]
            first = functools.reduce(jnp.logical_and, [i == 0 for i in ids])
            last = functools.reduce(jnp.logical_and, [i == g - 1 for i, g in zip(ids, grid)])
            pl.when(first)(lambda: start(refs[a:b], refs[c:d], *refs[e:]))
            inner(*refs[:a], *refs[b:c], *refs[d:e])
            pl.when(last)(lambda: finish(refs[a:b], refs[c:d], *refs[e:]))

        kw = dict(kw, in_specs=list(kw["in_specs"]) + [ANY] * len(srcs), out_specs=out_specs + (ANY,) * len(shapes),
                  out_shape=out_shape + tuple(shapes), scratch_shapes=own_scratch + list(scratch))
    return pl.pallas_call(body, **kw)


def _params(**kw):
    return pltpu.CompilerParams(vmem_limit_bytes=VMEM_LIMIT, **kw)


def _tile(n, pref):
    if n <= pref:
        return n
    t = (pref // 128) * 128
    while n % t:
        t -= 128
    return t


def _dot(a, b, dims=(((1,), (0,)), ((), ()))):
    return lax.dot_general(a, b, dims, preferred_element_type=F32)


_NN = (((1,), (0,)), ((), ()))
_NT = (((1,), (1,)), ((), ()))
_TN = (((0,), (0,)), ((), ()))


def _mx(v):
    return v.astype(MXU_DTYPE)


def _dot01(a, b, dims=_NN, *, split, terms):
    acc, rest = None, (a if split == "a" else b)
    for _ in range(terms):
        piece = _mx(rest)
        part = _dot(piece, _mx(b), dims) if split == "a" else _dot(_mx(a), piece, dims)
        acc = part if acc is None else acc + part
        rest = rest - piece.astype(F32)
    return acc


def _mm(name, a, b, mode, *, out_dtype=F32, res=None, relu2=False, relu2_of=None, col_shards=1, carry=None):
    shards = b.shape[0] if b.ndim == 3 else 0
    b2 = b.shape[1:] if shards else b.shape
    if mode == "nn":
        (m, kk), n = a.shape, b2[1] * max(shards, 1)
    elif mode == "nt":
        (m, kk), n = a.shape, b2[0]
    else:
        (kk, m), (_, n) = a.shape, b.shape
    assert res is None or relu2_of is None
    tk_pref = 1024
    if mode == "tn" and a.dtype.itemsize == 2 and b.dtype.itemsize == 2:
        tk_pref = 2048
    tm, tn, tk = _tile(m, 1024), _tile(n // col_shards, 1024), _tile(kk, tk_pref)
    nk = kk // tk
    dims = {"nn": _NN, "nt": _NT, "tn": _TN}[mode]
    a_spec = pl.BlockSpec((tk, tm), lambda i, j, k: (k, i)) if mode == "tn" else pl.BlockSpec((tm, tk), lambda i, j, k: (i, k))
    b_spec = pl.BlockSpec((tn, tk), lambda i, j, k: (j, k)) if mode == "nt" else pl.BlockSpec((tk, tn), lambda i, j, k: (k, j))
    if shards and mode == "nn":
        assert tn == b2[1]
        b_spec = pl.BlockSpec((None, tk, tn), lambda i, j, k: (j, k, 0))
    o_spec = pl.BlockSpec((tm, tn), lambda i, j, k: (i, j))
    o_shape = (m, n)
    if col_shards > 1:
        assert tn * col_shards == n and res is None and not relu2
        o_spec = pl.BlockSpec((None, tm, tn), lambda i, j, k: (j, i, 0))
        o_shape = (col_shards, m, tn)
    extra = res if res is not None else relu2_of
    has_res = extra is not None

    def body(*refs):
        a_ref, b_ref = refs[0], refs[1]
        res_ref = refs[2] if has_res else None
        outs = refs[2 + has_res:2 + has_res + 1 + relu2]

        def finish(r):
            if res is not None:
                r = r + res_ref[...]
            if relu2_of is not None:
                r = r * (2.0 * jnp.maximum(res_ref[...].astype(F32), 0.0))
            if relu2:
                outs[0][...] = r.astype(outs[0].dtype)
                outs[1][...] = jnp.square(jnp.maximum(r, 0.0)).astype(outs[1].dtype)
            else:
                outs[0][...] = r.astype(outs[0].dtype)

        prod = _dot(_mx(a_ref[...]), _mx(b_ref[...]), dims)
        if nk == 1:
            finish(prod)
            return
        acc = refs[-1]
        k = pl.program_id(2)

        @pl.when(k == 0)
        def _():
            acc[...] = prod

        @pl.when(k > 0)
        def _():
            acc[...] += prod

        @pl.when(k == nk - 1)
        def _():
            finish(acc[...])

    in_specs = [a_spec, b_spec] + ([o_spec] if has_res else [])
    if relu2:
        out_shape = (jax.ShapeDtypeStruct((m, n), BF16), jax.ShapeDtypeStruct((m, n), BF16))
        out_specs = (o_spec, o_spec)
    else:
        out_shape = jax.ShapeDtypeStruct(o_shape, out_dtype)
        out_specs = o_spec
    args = (a, b) + ((extra,) if has_res else ()) + (tuple(carry[0]) if carry else ())
    return _pcall(body, carry=carry, name=name, grid=(m // tm, n // tn, nk), in_specs=in_specs, out_specs=out_specs,
                  out_shape=out_shape, scratch_shapes=[pltpu.VMEM((tm, tn), F32)] if nk > 1 else [],
                  compiler_params=_params())(*args)


def _mm_sum_nt(name, parts, wblocks, norm_bwd=None):
    parts = [p if isinstance(p, tuple) else (p, 0, p.shape[1]) for p in parts]
    m, npart = parts[0][0].shape[0], len(parts)
    n = wblocks[0][0].shape[-2]
    tm, tn = _tile(m, 512), _tile(n, 1024)
    assert norm_bwd is None or tn == n

    def body(*refs):
        acc = _dot(_mx(refs[0][...]), _mx(refs[npart][...]), _NT)
        for k in range(1, npart):
            acc = acc + _dot(_mx(refs[k][...]), _mx(refs[npart + k][...]), _NT)
        if norm_bwd is None:
            refs[-1][...] = acc
            return
        x_ref, g_ref, res_ref, dx_ref, dg_ref = refs[2 * npart:]
        _, vjp = jax.vjp(_f_norm, x_ref[...], g_ref[...])
        dx, dg = vjp((acc,))
        dx_ref[...] = dx + res_ref[...]

        @pl.when(pl.program_id(0) == 0)
        def _():
            dg_ref[...] = jnp.zeros_like(dg_ref)

        dg_ref[...] += dg

    row = pl.BlockSpec((tm, tn), lambda i, j: (i, j))
    vec = pl.BlockSpec((1, tn), lambda i, j: (0, j))
    in_specs = [pl.BlockSpec((tm, wd), lambda i, j, cb=cb: (i, cb)) for _, cb, wd in parts]
    for (_, _, wd), (w, cb) in zip(parts, wblocks):
        in_specs.append(pl.BlockSpec((None, tn, wd), lambda i, j, cb=cb: (cb, j, 0)) if w.ndim == 3
                        else pl.BlockSpec((tn, wd), lambda i, j, cb=cb: (j, cb)))
    args = [p for p, _, _ in parts] + [w for w, _ in wblocks]
    if norm_bwd is None:
        return _pcall(body, name=name, grid=(m // tm, n // tn), in_specs=in_specs, out_specs=row,
                      out_shape=jax.ShapeDtypeStruct((m, n), F32), compiler_params=_params())(*args)
    return _pcall(body, name=name, grid=(m // tm, 1), in_specs=in_specs + [row, vec, row], out_specs=(row, vec),
                  out_shape=(jax.ShapeDtypeStruct((m, n), F32), jax.ShapeDtypeStruct((1, n), F32)),
                  compiler_params=_params())(*args, *norm_bwd)


def _pw_fwd(name, f, ins, params, out_dtypes, tc, ncol, tm=ROWS_FWD, groups=1):
    t = ins[0][0].shape[0]
    tm = min(tm, t)
    ni, npar = len(ins), len(params)
    gw = tc // groups

    def body(*refs):
        for g in range(groups):
            sl = slice(g * gw, (g + 1) * gw)
            vals = f(*[r[:, sl].astype(F32) for r in refs[:ni]], *[r[:, sl] for r in refs[ni:ni + npar]])
            for o, v in zip(refs[ni + npar:], vals):
                o[:, sl] = v.astype(o.dtype)

    in_specs = [pl.BlockSpec((tm, tc), lambda j, i, off=off: (i, off + j)) for _, off in ins]
    in_specs += [pl.BlockSpec((1, tc), lambda j, i, off=off: (0, off + j)) for _, off in params]
    out_specs = tuple(pl.BlockSpec((tm, tc), lambda j, i: (i, j)) for _ in out_dtypes)
    out_shape = tuple(jax.ShapeDtypeStruct((t, ncol * tc), d) for d in out_dtypes)
    return _pcall(body, name=name, grid=(ncol, t // tm), in_specs=in_specs, out_specs=out_specs, out_shape=out_shape,
                  compiler_params=_params())(*[a for a, _ in ins], *[p for p, _ in params])


def _pw_bwd(name, f, ins, params, douts, tc, ncol, want, adds=None, tm=ROWS_BWD, out_dtypes=None, groups=1):
    adds = adds or {}
    out_dtypes = out_dtypes or [F32] * len(want)
    t = ins[0][0].shape[0]
    tm = min(tm, t)
    ni, npar, nd, na = len(ins), len(params), len(douts), len(adds)
    add_keys = sorted(adds)
    gw = tc // groups

    def body(*refs):
        in_refs, p_refs = refs[:ni], refs[ni:ni + npar]
        d_refs = refs[ni + npar:ni + npar + nd]
        a_refs = refs[ni + npar + nd:ni + npar + nd + na]
        o_refs = refs[ni + npar + nd + na:]
        for p in range(npar):
            @pl.when(pl.program_id(1) == 0)
            def _(o=o_refs[len(want) + p]):
                o[...] = jnp.zeros_like(o)

        for g in range(groups):
            sl = slice(g * gw, (g + 1) * gw)
            _, vjp = jax.vjp(f, *[r[:, sl].astype(F32) for r in in_refs], *[r[:, sl] for r in p_refs])
            cts = vjp(tuple(d[:, sl].astype(F32) for d in d_refs))
            for o, kidx in zip(o_refs[:len(want)], want):
                v = cts[kidx]
                if kidx in adds:
                    v = v + a_refs[add_keys.index(kidx)][:, sl]
                o[:, sl] = v.astype(o.dtype)
            for p in range(npar):
                o_refs[len(want) + p][:, sl] += cts[ni + p]

    in_specs = [pl.BlockSpec((tm, tc), lambda j, i, off=off: (i, off + j)) for _, off in ins]
    in_specs += [pl.BlockSpec((1, tc), lambda j, i, off=off: (0, off + j)) for _, off in params]
    in_specs += [pl.BlockSpec((tm, tc), lambda j, i: (i, j)) for _ in range(nd + na)]
    out_specs = tuple([pl.BlockSpec((tm, tc), lambda j, i: (i, j)) for _ in want]
                      + [pl.BlockSpec((1, tc), lambda j, i: (0, j)) for _ in params])
    out_shape = tuple([jax.ShapeDtypeStruct((t, ncol * tc), dt) for dt in out_dtypes]
                      + [jax.ShapeDtypeStruct((1, ncol * tc), F32) for _ in params])
    res = _pcall(body, name=name, grid=(ncol, t // tm), in_specs=in_specs, out_specs=out_specs, out_shape=out_shape,
                 compiler_params=_params())(*[a for a, _ in ins], *[p for p, _ in params], *douts, *[adds[k] for k in add_keys])
    return list(res[:len(want)]), list(res[len(want):])


def _rms(x, g):
    return (x * lax.rsqrt(jnp.mean(x * x, axis=-1, keepdims=True) + EPS)) * g


def _f_norm(x, g):
    return (_rms(x, g),)


def _f_softplus(d, b):
    return (jax.nn.softplus(d + b),)


def _f_add3(a, b, c):
    return (a + b + c,)


def _f_ssd_post(yf, yb, xs, z, dskip, nw):
    u = (yf + yb + dskip * xs) * jax.nn.silu(z)
    return (_rms(u, nw),)


def _neg_expm1(v):
    t = jnp.tanh(0.5 * v)
    return -2.0 * t / (1.0 - t)


def _f_lru_gates(pre_a, pre_x, u, ba, bx, lam):
    rg = jax.nn.sigmoid(pre_a + ba)
    ig = jax.nn.sigmoid(pre_x + bx)
    log_a = -LRU_C * rg * jax.nn.softplus(-lam)
    return jnp.exp(log_a), jnp.sqrt(_neg_expm1(2.0 * log_a)) * (ig * u)


def _f_lru_post(hf, hb, gate):
    return ((hf + hb) * jax.nn.gelu(gate),)


def _f_hgrn_pre(fr, l0, l1):
    lb = jax.nn.sigmoid(l1 - l0)
    k = (1.0 - lb) * jax.nn.sigmoid(-fr)
    return k, jnp.log1p(-k)


def _f_hgrn_post(of, ob, gate, nw):
    return (_rms(of + ob, nw) * jax.nn.silu(gate),)


def _loss_head(x, tgt, g, tm=ROWS_FWD):
    t, d = x.shape
    tm = min(tm, t)

    def body(x_ref, t_ref, g_ref, dx_ref, dg_ref, loss_ref):
        tv = t_ref[...]

        def lf(xv, gv):
            return 0.5 * jnp.sum(jnp.mean(jnp.square(_rms(xv, gv) - tv), axis=-1))

        val, vjp = jax.vjp(lf, x_ref[...], g_ref[...])
        dx, dg = vjp(jnp.ones((), F32))
        dx_ref[...] = dx

        @pl.when(pl.program_id(0) == 0)
        def _():
            dg_ref[...] = jnp.zeros_like(dg_ref)
            loss_ref[...] = jnp.zeros_like(loss_ref)

        dg_ref[...] += dg
        loss_ref[...] += jnp.full(loss_ref.shape, val, F32)

    row = pl.BlockSpec((tm, d), lambda i: (i, 0))
    vec = pl.BlockSpec((1, d), lambda i: (0, 0))
    return _pcall(body, name="loss_head", grid=(t // tm,), in_specs=[row, row, vec],
                  out_specs=(row, vec, pl.BlockSpec((1, 128), lambda i: (0, 0))),
                  out_shape=(jax.ShapeDtypeStruct((t, d), F32), jax.ShapeDtypeStruct((1, d), F32),
                             jax.ShapeDtypeStruct((1, 128), F32)), compiler_params=_params())(x, tgt, g)


def _shifted(x, d, prev, nxt, first, last):
    r = x.shape[0]
    row = lax.broadcasted_iota(jnp.int32, x.shape, 0)
    if d < 0:
        out = pltpu.roll(x, -d, 0)
        for q in range(-d):
            pv = jnp.where(first, 0.0, prev[8 + d + q:8 + d + q + 1, :])
            out = jnp.where(row == q, pv, out)
        return out
    out = pltpu.roll(x, r - d, 0)
    for q in range(d):
        nv = jnp.where(last, 0.0, nxt[q:q + 1, :])
        out = jnp.where(row == r - d + q, nv, out)
    return out


def _conv_fwd(p3, w, b, col0, ncol, silu, tc=1024):
    nbatch, s, _ = p3.shape
    ts = min(CONV_ROWS, s)
    nblk = s // ts

    def body(x_ref, pv_ref, nx_ref, w_ref, b_ref, o_ref, *act_ref):
        i = pl.program_id(1)
        first, last = i == 0, i == nblk - 1
        x, pv, nx = x_ref[...], pv_ref[...], nx_ref[...]
        wv = w_ref[...]
        out = b_ref[...] + wv[1:2] * x
        out = out + wv[0:1] * _shifted(x, -1, pv, nx, first, last)
        out = out + wv[2:3] * _shifted(x, 1, pv, nx, first, last)
        out = out + wv[3:4] * _shifted(x, 2, pv, nx, first, last)
        o_ref[...] = out
        if silu:
            act_ref[0][...] = jax.nn.silu(out)

    nb8 = s // 8
    cur = pl.BlockSpec((None, ts, tc), lambda n, i, j: (n, i, col0 + j))
    prev = pl.BlockSpec((None, 8, tc), lambda n, i, j: (n, jnp.maximum(i * (ts // 8) - 1, 0), col0 + j))
    nxt = pl.BlockSpec((None, 8, tc), lambda n, i, j: (n, jnp.minimum((i + 1) * (ts // 8), nb8 - 1), col0 + j))
    out = pl.BlockSpec((None, ts, tc), lambda n, i, j: (n, i, j))
    shp = jax.ShapeDtypeStruct((nbatch, s, ncol * tc), F32)
    return _pcall(body, name=f"conv_fwd{col0}", grid=(nbatch, nblk, ncol),
                  in_specs=[cur, prev, nxt, pl.BlockSpec((4, tc), lambda n, i, j: (0, col0 + j)),
                            pl.BlockSpec((1, tc), lambda n, i, j: (0, col0 + j))],
                  out_specs=(out, out) if silu else out, out_shape=(shp, shp) if silu else shp,
                  compiler_params=_params())(p3, p3, p3, w, b)


def _conv_bwd(dc3, p3, w, col, conv3=None):
    nbatch, s, tc = dc3.shape
    ts = min(CONV_ROWS, s)
    nblk = s // ts
    silu = conv3 is not None

    def body(d_ref, dpv_ref, dnx_ref, x_ref, pv_ref, nx_ref, w_ref, *rest):
        n, i = pl.program_id(0), pl.program_id(1)
        first, last = i == 0, i == nblk - 1
        d, dpv, dnx = d_ref[...], dpv_ref[...], dnx_ref[...]
        if silu:
            d, dpv, dnx = [jax.vjp(jax.nn.silu, c_ref[...])[1](t)[0] for c_ref, t in zip(rest[:3], (d, dpv, dnx))]
        dx_ref, dw_ref = rest[3 * silu:]
        x, pv, nx = x_ref[...], pv_ref[...], nx_ref[...]
        wv = w_ref[...]
        dx = wv[1:2] * d
        dx = dx + wv[0:1] * _shifted(d, 1, dpv, dnx, first, last)
        dx = dx + wv[2:3] * _shifted(d, -1, dpv, dnx, first, last)
        dx = dx + wv[3:4] * _shifted(d, -2, dpv, dnx, first, last)
        dx_ref[...] = dx.astype(dx_ref.dtype)

        @pl.when((n == 0) & (i == 0))
        def _():
            dw_ref[...] = jnp.zeros_like(dw_ref)

        dw_ref[0:1, :] += jnp.sum(d * _shifted(x, -1, pv, nx, first, last), axis=0, keepdims=True)
        dw_ref[1:2, :] += jnp.sum(d * x, axis=0, keepdims=True)
        dw_ref[2:3, :] += jnp.sum(d * _shifted(x, 1, pv, nx, first, last), axis=0, keepdims=True)
        dw_ref[3:4, :] += jnp.sum(d * _shifted(x, 2, pv, nx, first, last), axis=0, keepdims=True)
        dw_ref[4:5, :] += jnp.sum(d, axis=0, keepdims=True)

    nb8 = s // 8

    def specs(j):
        cur = pl.BlockSpec((None, ts, tc), lambda n, i: (n, i, j))
        prev = pl.BlockSpec((None, 8, tc), lambda n, i: (n, jnp.maximum(i * (ts // 8) - 1, 0), j))
        nxt = pl.BlockSpec((None, 8, tc), lambda n, i: (n, jnp.minimum((i + 1) * (ts // 8), nb8 - 1), j))
        return [cur, prev, nxt]

    return _pcall(body, name=f"conv_bwd{col}", grid=(nbatch, nblk),
                  in_specs=specs(0) + specs(col) + [pl.BlockSpec((4, tc), lambda n, i: (0, col))] + specs(col) * silu,
                  out_specs=(specs(0)[0], pl.BlockSpec((8, tc), lambda n, i: (0, 0))),
                  out_shape=(jax.ShapeDtypeStruct((nbatch, s, tc), BF16), jax.ShapeDtypeStruct((8, tc), F32)),
                  compiler_params=_params())(dc3, dc3, dc3, p3, p3, p3, w, *([conv3] * 3 * silu))


def _block_scan(coef, inp, reverse):
    r = coef.shape[0]
    row = lax.broadcasted_iota(jnp.int32, coef.shape, 0)
    a, b = coef, inp
    d = 1
    while d < r:
        if reverse:
            keep = row < r - d
            a_sh, b_sh = pltpu.roll(a, r - d, 0), pltpu.roll(b, r - d, 0)
        else:
            keep = row >= d
            a_sh, b_sh = pltpu.roll(a, d, 0), pltpu.roll(b, d, 0)
        b = b + a * jnp.where(keep, b_sh, 0.0)
        a = a * jnp.where(keep, a_sh, 1.0)
        d *= 2
    return a, b


def _lru_scan(a3, b3, reverse):
    nbatch, s, w = a3.shape
    ts = min(LRU_ROWS, s)
    nblk = s // ts
    edge = 0 if reverse else ts - 1

    def body(a_ref, b_ref, h_ref, carry):
        @pl.when(pl.program_id(1) == 0)
        def _():
            carry[...] = jnp.zeros_like(carry)

        ca, hb = _block_scan(a_ref[...], b_ref[...], reverse)
        h = hb + ca * carry[0:1, :]
        h_ref[...] = h
        carry[0:1, :] = h[edge:edge + 1, :]

    blk = pl.BlockSpec((None, ts, w), (lambda n, i: (n, nblk - 1 - i, 0)) if reverse else (lambda n, i: (n, i, 0)))
    return _pcall(body, name=f"lru_scan_r{int(reverse)}", grid=(nbatch, nblk), in_specs=[blk, blk], out_specs=blk,
                  out_shape=jax.ShapeDtypeStruct((nbatch, s, w), F32), scratch_shapes=[pltpu.VMEM((8, w), F32)],
                  compiler_params=_params())(a3, b3)


def _lru_scan_bwd(a3, h3, dh3, reverse, carry=None):
    nbatch, s, w = a3.shape
    ts = min(LRU_ROWS, s)
    nblk = s // ts
    nb8 = s // 8
    tpb = ts // 8

    def body(a_ref, aa_ref, h_ref, hh_ref, dh_ref, g_ref, da_ref, carry):
        i = pl.program_id(1)

        @pl.when(i == 0)
        def _():
            carry[...] = jnp.zeros_like(carry)

        a, h = a_ref[...], h_ref[...]
        row = lax.broadcasted_iota(jnp.int32, a.shape, 0)
        if reverse:
            a_edge = jnp.where(i == 0, 0.0, aa_ref[7:8, :])
            c = jnp.where(row == 0, a_edge, pltpu.roll(a, 1, 0))
            h_edge = jnp.where(i == nblk - 1, 0.0, hh_ref[0:1, :])
            h_sh = jnp.where(row == ts - 1, h_edge, pltpu.roll(h, ts - 1, 0))
        else:
            a_edge = jnp.where(i == 0, 0.0, aa_ref[0:1, :])
            c = jnp.where(row == ts - 1, a_edge, pltpu.roll(a, ts - 1, 0))
            h_edge = jnp.where(i == nblk - 1, 0.0, hh_ref[7:8, :])
            h_sh = jnp.where(row == 0, h_edge, pltpu.roll(h, 1, 0))
        cc, gb = _block_scan(c, dh_ref[...], not reverse)
        g = gb + cc * carry[0:1, :]
        g_ref[...] = g
        carry[0:1, :] = g[ts - 1:ts, :] if reverse else g[0:1, :]
        da_ref[...] = g * h_sh

    if reverse:
        bi = lambda i: i
    else:
        bi = lambda i: nblk - 1 - i
    blk = pl.BlockSpec((None, ts, w), lambda n, i: (n, bi(i), 0))
    before = pl.BlockSpec((None, 8, w), lambda n, i: (n, jnp.maximum(bi(i) * tpb - 1, 0), 0))
    after = pl.BlockSpec((None, 8, w), lambda n, i: (n, jnp.minimum((bi(i) + 1) * tpb, nb8 - 1), 0))
    a_tile, h_tile = (before, after) if reverse else (after, before)
    return _pcall(body, carry=carry, name=f"lru_scan_bwd_r{int(reverse)}", grid=(nbatch, nblk),
                  in_specs=[blk, a_tile, blk, h_tile, blk], out_specs=(blk, blk),
                  out_shape=(jax.ShapeDtypeStruct((nbatch, s, w), F32), jax.ShapeDtypeStruct((nbatch, s, w), F32)),
                  scratch_shapes=[pltpu.VMEM((8, w), F32)],
                  compiler_params=_params())(a3, a3, h3, h3, dh3, *(carry[0] if carry else ()))


def _head_expand(lane0):
    return (jnp.right_shift(lax.broadcasted_iota(jnp.int32, (128, 1024), 1), HEAD_SHIFT) + lane0
            == lax.broadcasted_iota(jnp.int32, (128, 1024), 0)).astype(F32)


def _head_reduce(lane0):
    return (jnp.right_shift(lax.broadcasted_iota(jnp.int32, (1024, 128), 0), HEAD_SHIFT) + lane0
            == lax.broadcasted_iota(jnp.int32, (1024, 128), 1)).astype(F32)


def _time_mask(q, reverse):
    ri = lax.broadcasted_iota(jnp.int32, (q, q), 0)
    ci = lax.broadcasted_iota(jnp.int32, (q, q), 1)
    return (ri <= ci) if reverse else (ri >= ci)


def _ssd_common(xs_ref, bc_ref, dt_ref, al_ref, reverse, lane0):
    q = xs_ref.shape[0]
    edge = 0 if reverse else q - 1
    dt = dt_ref[...]
    a = -jnp.exp(al_ref[...])
    mask = _time_mask(q, reverse)
    expand = _head_expand(lane0)
    cum = _dot01(mask.astype(F32), dt * a, split="b", terms=3)
    cum_x = _dot01(cum, expand, split="a", terms=3)
    dt_x = _dot01(dt, expand, split="a", terms=2)
    last_x = cum_x[edge:edge + 1, :]
    xs = xs_ref[...]
    bc = bc_ref[...]
    return dict(q=q, edge=edge, lane0=lane0, dt=dt, a=a, mask=mask, cum_t=cum.T, cum_x=cum_x, dt_x=dt_x, xs=xs,
                v=xs * dt_x, e_c=jnp.exp(cum_x), w=jnp.exp(last_x - cum_x), e_l=jnp.exp(last_x),
                bm=bc[:, :512], cm=bc[:, 512:])


def _ssd_decay(c, h):
    row = c["lane0"] + h
    seg = c["cum_x"][:, h * SSD_HEADDIM:h * SSD_HEADDIM + 1] - c["cum_t"][row:row + 1, :]
    return jnp.where(c["mask"], jnp.exp(jnp.minimum(seg, 0.0)), 0.0)


def _head_masks():
    lane = jnp.right_shift(lax.broadcasted_iota(jnp.int32, (1, 256), 1), HEAD_SHIFT)
    return [lane == e for e in range(4)]


def _ssd_fwd(xbc3, dt3, alog, reverse, carry=None):
    nbatch, s, _ = xbc3.shape
    q = min(SSD_CHUNK, s)
    nc = s // q
    lane0 = SSD_HEADS * int(reverse)

    def body(xs_ref, bc_ref, dt_ref, al_ref, y_ref, st_ref, st):
        @pl.when(pl.program_id(1) == 0)
        def _():
            st[...] = jnp.zeros_like(st)

        st_ref[...] = st[...]
        c = _ssd_common(xs_ref, bc_ref, dt_ref, al_ref, reverse, lane0)
        hm = _head_masks()
        for g in range(SSD_GROUPS):
            sl = slice(g * 256, (g + 1) * 256)
            cg, bg = _mx(c["cm"][:, g * 128:(g + 1) * 128]), _mx(c["bm"][:, g * 128:(g + 1) * 128])
            cb = _dot(cg, bg, _NT)
            vg = c["v"][:, sl]
            s0 = st[:, sl]
            yg = _dot(cg, _mx(s0)) * c["e_c"][:, sl]
            for e in range(4):
                m = _ssd_decay(c, 4 * g + e) * cb
                yg = yg + _dot(_mx(m), _mx(jnp.where(hm[e], vg, 0.0)))
            y_ref[:, sl] = yg
            st[:, sl] = c["e_l"][:, sl] * s0 + _dot(bg, _mx(vg * c["w"][:, sl]), _TN)

    ck = (lambda i: nc - 1 - i) if reverse else (lambda i: i)
    xs_spec = pl.BlockSpec((None, q, 1024), lambda n, i: (n, ck(i), 0))
    bc_spec = pl.BlockSpec((None, q, 1024), lambda n, i: (n, ck(i), 1))
    dt_spec = pl.BlockSpec((None, q, 128), lambda n, i: (n, ck(i), 0))
    al_spec = pl.BlockSpec((1, 128), lambda n, i: (0, 0))
    st_spec = pl.BlockSpec((None, None, 128, 1024), lambda n, i: (n, ck(i), 0, 0))
    return _pcall(body, carry=carry, name=f"ssd_fwd_r{int(reverse)}", grid=(nbatch, nc),
                  in_specs=[xs_spec, bc_spec, dt_spec, al_spec], out_specs=(xs_spec, st_spec),
                  out_shape=(jax.ShapeDtypeStruct((nbatch, s, 1024), F32), jax.ShapeDtypeStruct((nbatch, nc, 128, 1024), F32)),
                  scratch_shapes=[pltpu.VMEM((128, 1024), F32)],
                  compiler_params=_params())(xbc3, xbc3, dt3, alog, *(carry[0] if carry else ()))


def _ssd_bwd(xbc3, dt3, alog, st4, dy3, reverse, add_to=(), scatter=()):
    nbatch, s, _ = xbc3.shape
    q = min(SSD_CHUNK, s)
    nc = s // q
    lane0 = SSD_HEADS * int(reverse)
    nadd, ns = len(add_to), len(scatter)

    def body(xs_ref, bc_ref, dt_ref, al_ref, st0_ref, dy_ref, *rest):
        adds, srcs, rest = rest[:nadd], rest[nadd:nadd + ns], rest[nadd + ns:]
        (dxs_ref, dbc_ref, ddt_ref, dal_ref), lands, dst = rest[:4], rest[4:4 + ns], rest[4 + ns]
        n, i = pl.program_id(0), pl.program_id(1)
        if ns:
            sends, arrivals = _scatter_copies(srcs, lands, *rest[5 + ns:])

            @pl.when((n == 0) & (i == 0))
            def _():
                for cp in sends:
                    cp.start()

        @pl.when(i == 0)
        def _():
            dst[...] = jnp.zeros_like(dst)

        @pl.when((i == 0) & (n == 0))
        def _():
            dal_ref[...] = jnp.zeros_like(dal_ref)

        c = _ssd_common(xs_ref, bc_ref, dt_ref, al_ref, reverse, lane0)
        hm = _head_masks()
        reduce_m = _head_reduce(lane0)
        s0_all, ds1_all, dy = st0_ref[...], dst[...], dy_ref[...]
        lane = lax.broadcasted_iota(jnp.int32, (q, 128), 1)
        sub = lax.broadcasted_iota(jnp.int32, (128, q), 0)
        rowacc = jnp.zeros((q, 128), F32)
        colacc_t = jnp.zeros((128, q), F32)
        dv_l, yst_l, dvbar_l, dk_l, dc_l = [], [], [], [], []
        for g in range(SSD_GROUPS):
            sl = slice(g * 256, (g + 1) * 256)
            cg, bg = _mx(c["cm"][:, g * 128:(g + 1) * 128]), _mx(c["bm"][:, g * 128:(g + 1) * 128])
            cb = _dot(cg, bg, _NT)
            vg, dyg, wg, ecg = c["v"][:, sl], dy[:, sl], c["w"][:, sl], c["e_c"][:, sl]
            s0, ds1 = _mx(s0_all[:, sl]), _mx(ds1_all[:, sl])
            dye = _mx(dyg * ecg)
            yst_l.append(_dot(cg, s0) * ecg)
            dcg = _dot(dye, s0, _NT)
            dst[:, sl] = c["e_l"][:, sl] * ds1_all[:, sl] + _dot(cg, dye, _TN)
            vbar = _mx(vg * wg)
            dvbar = _dot(bg, ds1)
            dvbar_l.append(dvbar)
            dvg = dvbar * wg
            dkg = _dot(vbar, ds1, _NT)
            for e in range(4):
                h = 4 * g + e
                m = _ssd_decay(c, h)
                dyh, vh = _mx(jnp.where(hm[e], dyg, 0.0)), _mx(jnp.where(hm[e], vg, 0.0))
                dvg = dvg + _dot(_mx(m * cb), dyh, _TN)
                dcb = _dot(dyh, vh, _NT) * m
                dcbb = _mx(dcb)
                dcg = dcg + _dot(dcbb, bg)
                dkg = dkg + _dot(dcbb, cg, _TN)
                wmat = dcb * cb
                rowacc = jnp.where(lane == lane0 + h, jnp.sum(wmat, axis=1, keepdims=True), rowacc)
                colacc_t = jnp.where(sub == lane0 + h, jnp.sum(wmat, axis=0, keepdims=True), colacc_t)
            dv_l.append(dvg)
            dk_l.append(dkg)
            dc_l.append(dcg)
        dv = jnp.concatenate(dv_l, axis=1)
        yst = jnp.concatenate(yst_l, axis=1)
        dvbar = jnp.concatenate(dvbar_l, axis=1)
        t1 = _dot01(dy * yst, reduce_m, split="a", terms=3)
        t2 = _dot01(c["v"] * c["w"] * dvbar, reduce_m, split="a", terms=3)
        dlast = jnp.sum(t2, axis=0, keepdims=True) + _dot01(
            c["e_l"] * jnp.sum(ds1_all * s0_all, axis=0, keepdims=True), reduce_m, split="a", terms=2)
        dcum = rowacc - colacc_t.T + t1 - t2
        dcum = dcum + jnp.where(lax.broadcasted_iota(jnp.int32, (q, 128), 0) == c["edge"], dlast, 0.0)
        dda = _dot01(c["mask"].astype(F32), dcum, _TN, split="b", terms=3)
        ddt = dda * c["a"] + _dot01(dv * c["xs"], reduce_m, split="a", terms=2)
        dal_ref[...] += jnp.sum(dda * c["dt"], axis=0, keepdims=True) * c["a"]
        dxs = dv * c["dt_x"]
        dbc = jnp.concatenate(dk_l + dc_l, axis=1)
        if nadd:
            for a_ref in adds[:-2]:
                dxs = dxs + a_ref[...]
            dbc = dbc + adds[-2][...]
            ddt = ddt + adds[-1][...]
        ddt_ref[...] = ddt
        dxs_ref[...] = dxs
        dbc_ref[...] = dbc
        if ns:
            @pl.when((n == nbatch - 1) & (i == nc - 1))
            def _():
                for cp in arrivals:
                    cp.wait_recv()
                for cp in sends:
                    cp.wait_send()

    ck = (lambda i: i) if reverse else (lambda i: nc - 1 - i)
    xs_spec = pl.BlockSpec((None, q, 1024), lambda n, i: (n, ck(i), 0))
    bc_spec = pl.BlockSpec((None, q, 1024), lambda n, i: (n, ck(i), 1))
    dt_spec = pl.BlockSpec((None, q, 128), lambda n, i: (n, ck(i), 0))
    al_spec = pl.BlockSpec((1, 128), lambda n, i: (0, 0))
    st_spec = pl.BlockSpec((None, None, 128, 1024), lambda n, i: (n, ck(i), 0, 0))
    return _pcall(body, name=f"ssd_bwd_r{int(reverse)}", grid=(nbatch, nc),
                  in_specs=([xs_spec, bc_spec, dt_spec, al_spec, st_spec, xs_spec] + [xs_spec] * (nadd - 1)
                            + [dt_spec] * bool(nadd) + [ANY] * ns),
                  out_specs=(xs_spec, xs_spec, dt_spec, al_spec) + (ANY,) * ns,
                  out_shape=(jax.ShapeDtypeStruct((nbatch, s, 1024), F32), jax.ShapeDtypeStruct((nbatch, s, 1024), F32),
                             jax.ShapeDtypeStruct((nbatch, s, 128), F32), jax.ShapeDtypeStruct((1, 128), F32))
                  + tuple(jax.ShapeDtypeStruct(c.shape, c.dtype) for c in scatter),
                  scratch_shapes=[pltpu.VMEM((128, 1024), F32)] + (_scatter_scratch(ns) if ns else []),
                  compiler_params=_params())(xbc3, xbc3, dt3, alog, st4, dy3, *add_to, *scatter)


def _gla_block(q, k, g, reverse):
    bq = g.shape[0]
    nsub = bq // HGRN_SUB
    edge = 0 if reverse else bq - 1
    ri = lax.broadcasted_iota(jnp.int32, (bq, bq), 0)
    ci = lax.broadcasted_iota(jnp.int32, (bq, bq), 1)
    rb, cb = jnp.right_shift(ri, HGRN_SUB_SHIFT), jnp.right_shift(ci, HGRN_SUB_SHIFT)
    mask = (ri <= ci) if reverse else (ri >= ci)
    m_within = (mask & (rb == cb)).astype(F32)
    m_before = ((cb > rb) if reverse else (cb < rb)).astype(F32)
    bl = _dot01(m_within, g, split="b", terms=3)
    c = _dot01(m_before, g, split="b", terms=3)
    last = c[edge:edge + 1, :] + bl[edge:edge + 1, :]
    ebl, enbl, ec, elc = jnp.exp(bl), jnp.exp(-bl), jnp.exp(c), jnp.exp(last - c)
    qh = q * HGRN_SCALE * ebl
    kh = k * enbl
    blk = jnp.right_shift(lax.broadcasted_iota(jnp.int32, (bq, 1), 0), HGRN_SUB_SHIFT)
    scale = []
    for i in range(nsub):
        valid = (blk >= i) if reverse else (blk <= i)
        ex = jnp.where(valid, c[i * HGRN_SUB:i * HGRN_SUB + 1, :] - c, 0.0)
        scale.append(jnp.where(valid, jnp.exp(ex), 0.0))
    return dict(bq=bq, nsub=nsub, edge=edge, mask=mask, m_within=m_within, m_before=m_before, ebl=ebl, enbl=enbl, ec=ec,
                elc=elc, e_l=jnp.exp(last), qh=qh, qt=qh * ec, kh=kh, kb=kh * elc, scale=scale)


def _gla_scores(c, hs):
    keys = [_mx(c["kh"][:, hs] * c["scale"][i][:, hs]) for i in range(c["nsub"])]
    rows = [_dot(_mx(c["qh"][i * HGRN_SUB:(i + 1) * HGRN_SUB, hs]), keys[i], _NT) for i in range(c["nsub"])]
    return jnp.where(c["mask"], jnp.concatenate(rows, axis=0), 0.0), keys


def _gla_specs(nbatch, s, w, reverse_order):
    bq = min(HGRN_BLOCK, s)
    nblk = s // bq
    bi = (lambda i: nblk - 1 - i) if reverse_order else (lambda i: i)
    col = lambda cb: pl.BlockSpec((nbatch, bq, w), lambda i: (0, bi(i), cb))
    st_spec = pl.BlockSpec((nbatch, None, 128, w), lambda i: (0, bi(i), 0, 0))
    return bq, nblk, col, st_spec


def _gla_fwd(proj3, l0, l1, reverse):
    nbatch, s, w5 = proj3.shape
    w = w5 // 5
    bq, nblk, col, st_spec = _gla_specs(nbatch, s, w, reverse)
    vec = pl.BlockSpec((1, w), lambda i: (0, 0))

    def body(q_ref, f_ref, v_ref, l0_ref, l1_ref, o_ref, st_ref, st):
        @pl.when(pl.program_id(0) == 0)
        def _():
            st[...] = jnp.zeros_like(st)

        for b in range(nbatch):
            st_ref[b] = st[b]
            k, g = _f_hgrn_pre(f_ref[b], l0_ref[...], l1_ref[...])
            c = _gla_block(q_ref[b], k, g, reverse)
            v = v_ref[b]
            for h in range(HGRN_HEADS):
                hs = slice(h * 128, (h + 1) * 128)
                att, _ = _gla_scores(c, hs)
                vb = _mx(v[:, hs])
                s0 = st[b, :, hs]
                o_ref[b, :, hs] = _dot(_mx(att), vb) + _dot(_mx(c["qt"][:, hs]), _mx(s0), _NT)
                st[b, :, hs] = s0 * c["e_l"][:, hs] + _dot(vb, _mx(c["kb"][:, hs]), _TN)

    return _pcall(body, name=f"gla_fwd_r{int(reverse)}", grid=(nblk,),
                  in_specs=[col(0), col(1 + int(reverse)), col(3), vec, vec], out_specs=(col(0), st_spec),
                  out_shape=(jax.ShapeDtypeStruct((nbatch, s, w), F32), jax.ShapeDtypeStruct((nbatch, nblk, 128, w), F32)),
                  scratch_shapes=[pltpu.VMEM((nbatch, 128, w), F32)], compiler_params=_params())(proj3, proj3, proj3, l0, l1)


def _gla_bwd(proj3, l0, l1, st4, do3, reverse, add_to=None):
    nbatch, s, w5 = proj3.shape
    w = w5 // 5
    bq, nblk, col, st_spec = _gla_specs(nbatch, s, w, not reverse)
    nadd = 0 if add_to is None else 2
    vec = pl.BlockSpec((1, w), lambda i: (0, 0))

    def body(q_ref, f_ref, v_ref, l0_ref, l1_ref, st_ref, do_ref, *rest):
        adds, (dq_ref, df_ref, dv_ref, dl0_ref, dl1_ref, dst) = rest[:nadd], rest[nadd:]

        @pl.when(pl.program_id(0) == 0)
        def _():
            dst[...] = jnp.zeros_like(dst)
            dl0_ref[...] = jnp.zeros_like(dl0_ref)
            dl1_ref[...] = jnp.zeros_like(dl1_ref)

        row = lax.broadcasted_iota(jnp.int32, (bq, 128), 0)
        for b in range(nbatch):
            (k, g), pre_vjp = jax.vjp(_f_hgrn_pre, f_ref[b], l0_ref[...], l1_ref[...])
            c = _gla_block(q_ref[b], k, g, reverse)
            s0_all, ds1_all = st_ref[b], dst[b]
            v, dy = v_ref[b], do_ref[b]
            dbl_l, dc_l, dk_l = [], [], []
            for h in range(HGRN_HEADS):
                hs = slice(h * 128, (h + 1) * 128)
                att, keys = _gla_scores(c, hs)
                qh, qt, kh, kb = c["qh"][:, hs], c["qt"][:, hs], c["kh"][:, hs], c["kb"][:, hs]
                vb, dyb = _mx(v[:, hs]), _mx(dy[:, hs])
                s0, ds1 = s0_all[:, hs], ds1_all[:, hs]
                datt = _mx(jnp.where(c["mask"], _dot(dyb, vb, _NT), 0.0))
                dqh_rows = []
                dkh = jnp.zeros((bq, 128), F32)
                dc = jnp.zeros((bq, 128), F32)
                for i in range(c["nsub"]):
                    rs = slice(i * HGRN_SUB, (i + 1) * HGRN_SUB)
                    dqh_rows.append(_dot(datt[rs], keys[i]))
                    dki = _dot(datt[rs], _mx(qh[rs]), _TN)
                    sc = c["scale"][i][:, hs]
                    dkh = dkh + dki * sc
                    dex = dki * (kh * sc)
                    dc = dc - dex + jnp.where(row == i * HGRN_SUB, jnp.sum(dex, axis=0, keepdims=True), 0.0)
                dqt = _dot(dyb, _mx(s0))
                dkb = _dot(vb, _mx(ds1))
                dv = _dot(_mx(att), dyb, _TN) + _dot(_mx(kb), _mx(ds1), _NT)
                dst[b, :, hs] = c["e_l"][:, hs] * ds1 + _dot(dyb, _mx(qt), _TN)
                dqh = jnp.concatenate(dqh_rows, axis=0) + dqt * c["ec"][:, hs]
                dkh = dkh + dkb * c["elc"][:, hs]
                kbk = dkb * kb
                dlast = jnp.sum(kbk, axis=0, keepdims=True) + c["e_l"][:, hs] * jnp.sum(ds1 * s0, axis=0, keepdims=True)
                at_edge = jnp.where(row == c["edge"], dlast, 0.0)
                dc_l.append(dc + dqt * qt - kbk + at_edge)
                dbl_l.append(dqh * qh - dkh * kh + at_edge)
                dq = dqh * c["ebl"][:, hs] * HGRN_SCALE
                if nadd:
                    dq, dv = dq + adds[0][b, :, hs], dv + adds[1][b, :, hs]
                dq_ref[b, :, hs] = dq.astype(dq_ref.dtype)
                dv_ref[b, :, hs] = dv.astype(dv_ref.dtype)
                dk_l.append(dkh * c["enbl"][:, hs])
            dg = (_dot01(c["m_within"], jnp.concatenate(dbl_l, axis=1), _TN, split="b", terms=2)
                  + _dot01(c["m_before"], jnp.concatenate(dc_l, axis=1), _TN, split="b", terms=2))
            df, d0, d1 = pre_vjp((jnp.concatenate(dk_l, axis=1), dg))
            df_ref[b] = df.astype(df_ref.dtype)
            dl0_ref[...] += d0
            dl1_ref[...] += d1

    shp_sum = jax.ShapeDtypeStruct((nbatch, s, w), BF16 if nadd else F32)
    shp_vec = jax.ShapeDtypeStruct((1, w), F32)
    return _pcall(body, name=f"gla_bwd_r{int(reverse)}", grid=(nblk,),
                  in_specs=[col(0), col(1 + int(reverse)), col(3), vec, vec, st_spec, col(0)] + [col(0)] * nadd,
                  out_specs=(col(0), col(0), col(0), vec, vec),
                  out_shape=(shp_sum, jax.ShapeDtypeStruct((nbatch, s, w), BF16), shp_sum, shp_vec, shp_vec),
                  scratch_shapes=[pltpu.VMEM((nbatch, 128, w), F32)],
                  compiler_params=_params())(proj3, proj3, proj3, l0, l1, st4, do3, *(add_to or ()))


DIRS = (False, True)


def _block_diag(w):
    eye = jnp.eye(16, dtype=w.dtype)
    return (eye[:, None, :, None] * w[:, :, None, :]).reshape(1024, 1024)


def _diag_blocks(m):
    m4 = m.reshape(16, 64, 16, 64)
    return jnp.stack([m4[i, :, i, :] for i in range(16)], axis=0)


def _pad_lanes(v, n=128):
    return jnp.pad(v, [(0, 0)] * (v.ndim - 1) + [(0, n - v.shape[-1])])


def _mlp_fwd(tag, x, nw, w1, w2, carry=None):
    (h,) = _pw_fwd(f"{tag}_norm", _f_norm, [(x, 0)], [(nw, 0)], [BF16], 1024, 1)
    a, r, *got = _mm(f"{tag}_up", h, w1, "nn", relu2=True, carry=carry)
    return _mm(f"{tag}_down", r, w2, "nn", res=x), (h, a, r), got


def _mlp_bwd(tag, x, nw, w1, w2, saved, dxo, carry=None):
    h, a, r = saved
    dw2, *got = _mm(f"{tag}_dw2", r, dxo, "tn", carry=carry) if carry else (_mm(f"{tag}_dw2", r, dxo, "tn"),)
    da = _mm(f"{tag}_da", dxo, w2, "nt", relu2_of=a, out_dtype=BF16)
    dw1 = _mm(f"{tag}_dw1", h, da, "tn", col_shards=4)
    dx, dnw = _mm_sum_nt(f"{tag}_dh", [(da, k, 1024) for k in range(4)], [(w1, k) for k in range(4)], norm_bwd=(x, nw, dxo))
    return dx, dw1, dw2, dnw, got


def _split_in0(pieces, dt_piece):
    tm = 256

    def body(p0, p1, p2, p3, p4, p5, o_ref):
        full = jnp.concatenate([p0[...], p1[...], p2[...], p3[...], p4[...], p5[:, :32]], axis=1)
        for j in range(4):
            o_ref[j] = full[:, 1288 * j:1288 * (j + 1)]

    blk = pl.BlockSpec((tm, 1024), lambda i: (i, 0))
    return _pcall(body, name="split_in0", grid=(1024 // tm,), in_specs=[blk] * 5 + [pl.BlockSpec((tm, 128), lambda i: (i, 0))],
                  out_specs=pl.BlockSpec((4, tm, 1288), lambda i: (0, i, 0)),
                  out_shape=jax.ShapeDtypeStruct((4, 1024, 1288), F32), compiler_params=_params())(*pieces, dt_piece)


def _assemble_in0(shards):
    tm = 256

    def body(s_ref, m_ref, d_ref):
        full = jnp.concatenate([s_ref[j] for j in range(4)], axis=1)
        m_ref[...] = full[:, :5120]
        d_ref[...] = jnp.concatenate([full[:, 5120:5152], jnp.zeros((tm, 96), full.dtype)], axis=1)

    return _pcall(body, name="assemble_in0", grid=(1024 // tm,), in_specs=[pl.BlockSpec((4, tm, 1288), lambda i: (0, i, 0))],
                  out_specs=(pl.BlockSpec((tm, 5120), lambda i: (i, 0)), pl.BlockSpec((tm, 128), lambda i: (i, 0))),
                  out_shape=(jax.ShapeDtypeStruct((1024, 5120), shards.dtype), jax.ShapeDtypeStruct((1024, 128), shards.dtype)),
                  compiler_params=_params())(shards)


EARLY = ("odd_w_in", "odd_w_out", "mlp_w1_l1", "mlp_w2_l1")
MID = ("even_w_out", "mlp_w1_l0", "mlp_w2_l0")
LATE = ("even_w_in",)


def _local_step(x3, tgt3, w, w_main0, w_dt0, pair_reduce=None, late=None):
    nb, s, d = x3.shape
    carries, arrived = late if late else ({}, None)
    t = nb * s
    x0 = x3.reshape(t, d)
    tgt = tgt3.reshape(t, d)
    grads = {}
    row = lambda v: v.reshape(1, -1)
    to3 = lambda v: v.reshape(nb, s, v.shape[-1])
    to2 = lambda v: v.reshape(-1, v.shape[-1])

    conv_w, conv_b = w["even_conv_w"][0], row(w["even_conv_b"][0])
    nmix0 = row(w["norm_mix"][0])
    (h0,) = _pw_fwd("l0_norm", _f_norm, [(x0, 0)], [(nmix0, 0)], [BF16], 1024, 1)
    proj0 = _mm("l0_proj", h0, w_main0, "nn")
    dt_raw = _mm("l0_proj_dt", h0, w_dt0, "nn")
    conv2, xbc3 = _conv_fwd(to3(proj0), conv_w, conv_b, 0, 2, True)
    u_lru = to2(_conv_fwd(to3(proj0), conv_w, conv_b, 2, 1, False))
    xbc = to2(xbc3)
    dt_bias = _pad_lanes(w["ssd_dt_bias"][0].reshape(1, 32))
    (dt,) = _pw_fwd("l0_dt", _f_softplus, [(dt_raw, 0)], [(dt_bias, 0)], [F32], 128, 1)
    dt3 = to3(dt)
    alog = _pad_lanes(w["ssd_a_log"][0].reshape(1, 32))
    ssd = [_ssd_fwd(xbc3, dt3, alog, r, carry=carries.get(key)) for r, key in zip(DIRS, ("mlp_w1", "mlp_w2"))]
    if late:
        w = {**w, **arrived("mlp_w1", ssd[0][2:]), **arrived("mlp_w2", ssd[1][2:])}
    yf, yb = to2(ssd[0][0]), to2(ssd[1][0])
    dskip = jnp.repeat(w["ssd_d"][0], SSD_HEADDIM).reshape(1, 1024)
    snw = row(w["ssd_norm_w"][0])
    ssd_ins = [(yf, 0), (yb, 0), (xbc, 0), (proj0, 3)]
    (ya,) = _pw_fwd("l0_ssd_post", _f_ssd_post, ssd_ins, [(dskip, 0), (snw, 0)], [BF16], 1024, 1, groups=SSD_GROUPS)
    w_gates = [_block_diag(w[k][0, r]).astype(MXU_DTYPE) for r in range(2) for k in ("lru_w_a", "lru_w_x")]
    pre = [_mm(f"l0_lru_pre{i}", u_lru, wg, "nn") for i, wg in enumerate(w_gates)]
    lru_par = [[(row(w[k][0, r]), 0) for k in ("lru_b_a", "lru_b_x", "lru_lambda")] for r in range(2)]
    lru_ins = [[(pre[2 * r], 0), (pre[2 * r + 1], 0), (u_lru, 0)] for r in range(2)]
    ab = [_pw_fwd(f"l0_lru_gates{r}", _f_lru_gates, lru_ins[r], lru_par[r], [F32, F32], 1024, 1) for r in range(2)]
    hs = [_lru_scan(to3(ab[r][0]), to3(ab[r][1]), DIRS[r]) for r in range(2)]
    lru_post_ins = [(to2(hs[0]), 0), (to2(hs[1]), 0), (proj0, 4)]
    (ybm,) = _pw_fwd("l0_lru_post", _f_lru_post, lru_post_ins, [], [BF16], 1024, 1)
    w_out0 = w["even_w_out"][0]
    x1 = _mm("l0_out_a", ya, w_out0[:1024], "nn", res=x0)
    x1 = _mm("l0_out_b", ybm, w_out0[1024:], "nn", res=x1)
    nmlp0 = row(w["norm_mlp"][0])
    x2, mlp0, got = _mlp_fwd("l0_mlp", x1, nmlp0, w["mlp_w1"][0], w["mlp_w2"][0], carry=carries.get("odd"))
    if late:
        w = {**w, **arrived("odd", got)}

    w_in1 = w["odd_w_in"][0]
    nmix1 = row(w["norm_mix"][1])
    (h1,) = _pw_fwd("l1_norm", _f_norm, [(x2, 0)], [(nmix1, 0)], [BF16], 1024, 1)
    proj1 = _mm("l1_proj", h1, w_in1, "nn")
    proj1_3 = to3(proj1)
    lb0, lb1 = row(w["hgrn_lb_logits"][0]), row(w["hgrn_lb_logits"][1])
    gla = [_gla_fwd(proj1_3, lb0, lb1, r) for r in DIRS]
    hnw = row(w["hgrn_norm_w"][0])
    hpost_ins = [(to2(gla[0][0]), 0), (to2(gla[1][0]), 0), (proj1, 4)]
    (yo,) = _pw_fwd("l1_hgrn_post", _f_hgrn_post, hpost_ins, [(hnw, 0)], [BF16], 1024, 1, groups=HGRN_HEADS)
    w_out1 = w["odd_w_out"][0]
    x3_ = _mm("l1_out", yo, w_out1, "nn", res=x2)
    nmlp1 = row(w["norm_mlp"][1])
    x4, mlp1, _ = _mlp_fwd("l1_mlp", x3_, nmlp1, w["mlp_w1"][1], w["mlp_w2"][1])

    dx4, dnf, loss = _loss_head(x4, tgt, row(w["norm_final"]))
    grads["norm_final"] = dnf.reshape(-1)

    dx3, dw1_1, dw2_1, dnmlp1, _ = _mlp_bwd("l1_mlp", x3_, nmlp1, w["mlp_w1"][1], w["mlp_w2"][1], mlp1, dx4)
    big = {"odd_w_out": _mm("l1_dwout", yo, dx3, "tn").reshape(4, 256, 1024)}
    dyo = _mm("l1_dyo", dx3, w_out1, "nt")
    (do, dgate1), (dhnw,) = _pw_bwd("l1_hgrn_post_b", _f_hgrn_post, hpost_ins, [(hnw, 0)], [dyo], 1024, 1, [0, 2],
                                    out_dtypes=[F32, BF16], groups=HGRN_HEADS, tm=ROWS_FWD)
    grads["hgrn_norm_w"] = dhnw
    do3 = to3(do)
    gb = [_gla_bwd(proj1_3, lb0, lb1, gla[0][1], do3, False)]
    gb.append(_gla_bwd(proj1_3, lb0, lb1, gla[1][1], do3, True, add_to=(gb[0][0], gb[0][2])))
    grads["hgrn_lb_logits"] = jnp.concatenate([gb[0][3] + gb[1][3], gb[0][4] + gb[1][4]], axis=0)
    dparts1 = [to2(gb[1][0]), to2(gb[0][1]), to2(gb[1][1]), to2(gb[1][2]), dgate1]
    dwin1 = jnp.concatenate([_mm(f"l1_dwin{i}", h1, dp, "tn") for i, dp in enumerate(dparts1)], axis=1)
    big["odd_w_in"] = dwin1.reshape(1024, 4, 1280).transpose(1, 0, 2)
    dx2, dnmix1 = _mm_sum_nt("l1_dh", dparts1, [(w_in1, i) for i in range(5)], norm_bwd=(x2, nmix1, dx3))
    big["mlp_w1_l1"], big["mlp_w2_l1"] = dw1_1, dw2_1.reshape(4, 1024, 1024)
    box = {}

    def mlp0_bwd(carry=None):
        box["mlp0"] = _mlp_bwd("l0_mlp", x1, nmlp0, w["mlp_w1"][0], w["mlp_w2"][0], mlp0, dx2, carry=carry)
        return box["mlp0"][4]

    early_sums = tuple(pair_reduce(EARLY, [big[n] for n in EARLY], mlp0_bwd)) if pair_reduce else tuple(mlp0_bwd())

    dx1, dw1_0, dw2_0, dnmlp0 = box["mlp0"][:4]
    big["mlp_w1_l0"], big["mlp_w2_l0"] = dw1_0, dw2_0.reshape(4, 1024, 1024)
    grads["norm_mlp"] = jnp.concatenate([dnmlp0, dnmlp1], axis=0)
    big["even_w_out"] = jnp.concatenate([_mm("l0_dwout_a", ya, dx1, "tn"), _mm("l0_dwout_b", ybm, dx1, "tn")],
                                        axis=0).reshape(4, 512, 1024)
    dya = _mm("l0_dya", dx1, w_out0[:1024], "nt")
    dyb = _mm("l0_dyb", dx1, w_out0[1024:], "nt")
    (dh, dgate0), _ = _pw_bwd("l0_lru_post_b", _f_lru_post, lru_post_ins, [], [dyb], 1024, 1, [0, 2], out_dtypes=[F32, BF16],
                               tm=ROWS_FWD)
    dh3 = to3(dh)

    def lru0_bwd(carry=None):
        box["lru0"] = _lru_scan_bwd(to3(ab[0][0]), hs[0], dh3, DIRS[0], carry=carry)
        return box["lru0"][2:]

    mid_sums = tuple(pair_reduce(MID, [big[n] for n in MID], lru0_bwd)) if pair_reduce else tuple(lru0_bwd())
    dpre, du_parts, dlru = [], [], {k: [] for k in ("lru_b_a", "lru_b_x", "lru_lambda")}
    for r in range(2):
        g_r, da_r = box["lru0"][:2] if r == 0 else _lru_scan_bwd(to3(ab[r][0]), hs[r], dh3, DIRS[r])
        (dpa, dpx, du_r), (dba, dbx, dlam) = _pw_bwd(f"l0_lru_gates_b{r}", _f_lru_gates, lru_ins[r], lru_par[r],
                                                     [to2(da_r), to2(g_r)], 1024, 1, [0, 1, 2],
                                                     out_dtypes=[BF16, BF16, F32])
        dpre += [dpa, dpx]
        du_parts.append(du_r)
        dlru["lru_b_a"].append(dba)
        dlru["lru_b_x"].append(dbx)
        dlru["lru_lambda"].append(dlam)
    for k, v in dlru.items():
        grads[k] = jnp.concatenate(v, axis=0)[None]
    dwg = [_diag_blocks(_mm(f"l0_dwgate{i}", u_lru, dp, "tn")) for i, dp in enumerate(dpre)]
    grads["lru_w_a"] = jnp.stack([dwg[0], dwg[2]])[None]
    grads["lru_w_x"] = jnp.stack([dwg[1], dwg[3]])[None]
    du_gate = _mm_sum_nt("l0_du_gate", dpre, [(wg, 0) for wg in w_gates])
    (du,) = _pw_fwd("l0_du", _f_add3, [(du_parts[0], 0), (du_parts[1], 0), (du_gate, 0)], [], [F32], 1024, 1)
    (dy, dxs_skip, dz), (ddskip, dsnw) = _pw_bwd("l0_ssd_post_b", _f_ssd_post, ssd_ins, [(dskip, 0), (snw, 0)], [dya],
                                                 1024, 1, [0, 2, 3], out_dtypes=[F32, F32, BF16], groups=SSD_GROUPS)
    grads["ssd_d"] = ddskip.reshape(SSD_HEADS, SSD_HEADDIM).sum(axis=1)[None]
    grads["ssd_norm_w"] = dsnw
    dy3 = to3(dy)
    sb0 = _ssd_bwd(xbc3, dt3, alog, ssd[0][1], dy3, False, scatter=early_sums)
    sb1 = _ssd_bwd(xbc3, dt3, alog, ssd[1][1], dy3, True, add_to=(sb0[0], to3(dxs_skip), sb0[1], sb0[2]), scatter=mid_sums)
    grads["ssd_a_log"] = (sb0[3] + sb1[3])[:, :32].reshape(1, 2, 16)
    ddt = to2(sb1[2])
    (ddt_raw,), (ddtb,) = _pw_bwd("l0_dt_b", _f_softplus, [(dt_raw, 0)], [(dt_bias, 0)], [ddt], 128, 1, [0])
    grads["ssd_dt_bias"] = ddtb[:, :32].reshape(1, 2, 16)
    cb = [_conv_bwd(sb1[0], to3(proj0), conv_w, 0, conv2), _conv_bwd(sb1[1], to3(proj0), conv_w, 1, conv2),
          _conv_bwd(to3(du), to3(proj0), conv_w, 2)]
    dcw = jnp.concatenate([c_[1] for c_ in cb], axis=1)
    grads["even_conv_w"] = dcw[:4][None]
    grads["even_conv_b"] = dcw[4:5]
    dparts0 = [to2(c_[0]) for c_ in cb] + [dz, dgate0]
    dwin0 = [_mm(f"l0_dwin{i}", h0, dp, "tn") for i, dp in enumerate(dparts0)]
    big["even_w_in"] = _split_in0(dwin0, _mm("l0_dwin_dt", h0, ddt_raw, "tn"))
    dx0, dnmix0 = _mm_sum_nt("l0_dh", dparts0 + [ddt_raw], [(w_main0, i) for i in range(5)] + [(w_dt0, 0)],
                             norm_bwd=(x0, nmix0, dx1))
    grads["norm_mix"] = jnp.concatenate([dnmix0, dnmix1], axis=0)
    return loss, dx0.reshape(nb, s, d), grads, big, (early_sums + mid_sums, sb0[4:] + sb1[4:])


ANY = pl.BlockSpec(memory_space=pl.ANY)


def _place():
    return lax.axis_index("x"), lax.axis_index("y"), lax.axis_index("c")


def _remote(src, dst, send_sems, recv_sems, k, to):
    return pltpu.make_async_remote_copy(src_ref=src, dst_ref=dst, send_sem=send_sems.at[k], recv_sem=recv_sems.at[k],
                                        device_id=to, device_id_type=MESH)


def _gather_start(x_refs, out_refs, send_sems, recv_sems, finish=False):
    n = len(x_refs)
    halves = [r.shape[0] // 2 for r in x_refs]
    x, y, c = _place()
    sibling = (x, y, 1 - c)
    chips = [(1 - x, y), (x, 1 - y), (1 - x, 1 - y)]

    def blk(t, px, py, hc):
        return out_refs[t].at[2 * px + py, pl.ds(hc * halves[t], halves[t]), :]

    def src(t):
        return x_refs[t].at[pl.ds(c * halves[t], halves[t]), :]

    first = [_remote(src(t), blk(t, x, y, c), send_sems, recv_sems, 6 * t + j, (*chip, c))
             for t in range(n) for j, chip in enumerate(chips)]
    if not finish:
        for cp in first:
            cp.start()
        return
    passed = []
    for t in range(n):
        for j, chip in enumerate(chips):
            _remote(src(t), blk(t, *chip, c), send_sems, recv_sems, 6 * t + j, (*chip, c)).wait_recv()
            cp = _remote(blk(t, *chip, c), blk(t, *chip, c), send_sems, recv_sems, 6 * t + 3 + j, sibling)
            cp.start()
            passed.append(cp)
    for t in range(n):
        for j, chip in enumerate(chips):
            _remote(src(t), blk(t, *chip, 1 - c), send_sems, recv_sems, 6 * t + 3 + j, sibling).wait_recv()
    for cp in first + passed:
        cp.wait_send()


_gather_finish = functools.partial(_gather_start, finish=True)


def _gather_carry(shards):
    n = len(shards)
    return (list(shards), [jax.ShapeDtypeStruct((4,) + s.shape, s.dtype) for s in shards],
            [pltpu.SemaphoreType.DMA((6 * n,)), pltpu.SemaphoreType.DMA((6 * n,))], _gather_start, _gather_finish)


def _gather_chips(shards):
    n = len(shards)
    srcs, shapes, scratch, start, finish = _gather_carry(shards)

    def body(*refs):
        start(refs[:n], refs[n:2 * n], *refs[2 * n:])
        finish(refs[:n], refs[n:2 * n], *refs[2 * n:])

    return _pcall(body, name="gather_weights", in_specs=[ANY] * n, out_specs=(ANY,) * n, out_shape=tuple(shapes),
                  scratch_shapes=scratch, compiler_params=_params())(*shards)


def _pair_swap_start(g_refs, land_refs, send_sems, recv_sems, finish=False):
    x, y, c = _place()
    cps = []
    for t, g in enumerate(g_refs):
        half = g.shape[1] // 2
        cps += [_remote(g.at[j, pl.ds((1 - c) * half, half), :], land_refs[t].at[j], send_sems, recv_sems, 4 * t + j,
                        (x, y, 1 - c)) for j in range(4)]
    for cp in cps:
        cp.wait() if finish else cp.start()


_pair_swap_finish = functools.partial(_pair_swap_start, finish=True)


def _pair_swap_carry(gps):
    n = len(gps)
    return (list(gps), [jax.ShapeDtypeStruct((4, g.shape[1] // 2, g.shape[2]), F32) for g in gps],
            [pltpu.SemaphoreType.DMA((4 * n,)), pltpu.SemaphoreType.DMA((4 * n,))], _pair_swap_start, _pair_swap_finish)


def _pair_swap(name, gps):
    n = len(gps)
    srcs, shapes, scratch, start, finish = _pair_swap_carry(gps)

    def body(*refs):
        start(refs[:n], refs[n:2 * n], *refs[2 * n:])
        finish(refs[:n], refs[n:2 * n], *refs[2 * n:])

    return _pcall(body, name=f"pair_swap_{name}", in_specs=[ANY] * n, out_specs=(ANY,) * n, out_shape=tuple(shapes),
                  scratch_shapes=scratch, compiler_params=_params())(*gps)


def _pair_add(name, gp, land, cidx):
    _, half, cols = land.shape
    tr = _tile(half, 512)
    nh = half // tr

    def body(c_ref, g_ref, l_ref, o_ref):
        o_ref[...] = (g_ref[...] + l_ref[...]).astype(o_ref.dtype)

    grid_spec = pltpu.PrefetchScalarGridSpec(
        num_scalar_prefetch=1, grid=(4, nh),
        in_specs=[pl.BlockSpec((None, tr, cols), lambda j, i, c: (j, c[0] * nh + i, 0)),
                  pl.BlockSpec((None, tr, cols), lambda j, i, c: (j, i, 0))],
        out_specs=pl.BlockSpec((None, tr, cols), lambda j, i, c: (j, i, 0)))
    return _pcall(body, name=f"pair_add_{name}", grid_spec=grid_spec, out_shape=jax.ShapeDtypeStruct((4, half, cols), BF16),
                  compiler_params=_params())(cidx, gp, land)


def _scatter_copies(s_refs, land_refs, send_sems, recv_sems):
    x, y, c = _place()
    me = 2 * x + y
    chips = [(1 - x, y), (x, 1 - y), (1 - x, 1 - y)]
    pairs = [(t, j, px, py) for t in range(len(s_refs)) for j, (px, py) in enumerate(chips)]
    sends = [_remote(s_refs[t].at[2 * px + py], land_refs[t].at[me], send_sems, recv_sems, 3 * t + j, (px, py, c))
             for t, j, px, py in pairs]
    arrivals = [_remote(s_refs[t].at[me], land_refs[t].at[2 * px + py], send_sems, recv_sems, 3 * t + j, (px, py, c))
                for t, j, px, py in pairs]
    return sends, arrivals


def _scatter_scratch(n):
    return [pltpu.SemaphoreType.DMA((3 * n,)), pltpu.SemaphoreType.DMA((3 * n,))]


def _chip_scatter(name, css):
    n = len(css)

    def body(*refs):
        sends, arrivals = _scatter_copies(refs[:n], refs[n:2 * n], *refs[2 * n:])
        for cp in sends:
            cp.start()
        for cp in arrivals:
            cp.wait_recv()
        for cp in sends:
            cp.wait_send()

    return _pcall(body, name=f"chip_scatter_{name}", in_specs=[ANY] * n, out_specs=(ANY,) * n,
                  out_shape=tuple(jax.ShapeDtypeStruct(s.shape, s.dtype) for s in css),
                  scratch_shapes=_scatter_scratch(n), compiler_params=_params())(*css)


def _chip_sum(name, land):
    _, half, cols = land.shape
    tr = _tile(half, 512)

    def body(l_ref, o_ref):
        o_ref[...] = ((l_ref[0].astype(F32) + l_ref[1].astype(F32)) + l_ref[2].astype(F32)) + l_ref[3].astype(F32)

    return _pcall(body, name=f"chip_sum_{name}", grid=(half // tr,),
                  in_specs=[pl.BlockSpec((4, tr, cols), lambda i: (0, i, 0))],
                  out_specs=pl.BlockSpec((tr, cols), lambda i: (i, 0)),
                  out_shape=jax.ShapeDtypeStruct((half, cols), F32), compiler_params=_params())(land)


def _pair_join(reds):
    n = len(reds)

    def body(*refs):
        r_refs, out_refs = refs[:n], refs[n:2 * n]
        send_sems, recv_sems = refs[2 * n:]
        x, y, c = _place()
        cps = [_remote(r_refs[t], out_refs[t].at[c], send_sems, recv_sems, t, (x, y, 1 - c)) for t in range(n)]
        for cp in cps:
            cp.start()
        for t in range(n):
            _remote(r_refs[t], out_refs[t].at[1 - c], send_sems, recv_sems, t, (x, y, 1 - c)).wait_recv()
        for cp in cps:
            cp.wait_send()

    return _pcall(body, name="grad_pair_join", in_specs=[ANY] * n, out_specs=(ANY,) * n,
                  out_shape=tuple(jax.ShapeDtypeStruct((2,) + r.shape, F32) for r in reds),
                  scratch_shapes=[pltpu.SemaphoreType.DMA((n,)), pltpu.SemaphoreType.DMA((n,))],
                  compiler_params=_params())(*reds)


def _adamw(name, g, w, m, v):
    rows, cols = g.shape
    tr = _tile(rows, 512)

    def body(g_ref, w_ref, m_ref, v_ref, d_ref, mo_ref, vo_ref):
        gv = g_ref[...]
        mn = ADAM_B1 * m_ref[...] + (1.0 - ADAM_B1) * gv
        vn = ADAM_B2 * v_ref[...] + (1.0 - ADAM_B2) * jnp.square(gv)
        m_hat = mn / (1.0 - ADAM_B1 ** ADAM_STEP)
        v_hat = vn / (1.0 - ADAM_B2 ** ADAM_STEP)
        d_ref[...] = -ADAM_LR * (m_hat / (jnp.sqrt(v_hat) + ADAM_EPS) + ADAM_WD * w_ref[...])
        mo_ref[...] = mn
        vo_ref[...] = vn

    blk = pl.BlockSpec((tr, cols), lambda i: (i, 0))
    shp = jax.ShapeDtypeStruct((rows, cols), F32)
    return _pcall(body, name=f"adamw_{name}", grid=(rows // tr,), in_specs=[blk] * 4, out_specs=(blk,) * 3,
                  out_shape=(shp,) * 3, compiler_params=_params())(g, w, m, v)


def _pack(pieces, rows, dtype):
    flat = jnp.concatenate([p.reshape(-1).astype(dtype) for p in pieces])
    return jnp.pad(flat, (0, rows * PACK_COLS - flat.shape[0])).reshape(rows, PACK_COLS)


def _unpack(pack, shapes):
    flat = pack.reshape(-1)
    out, off = [], 0
    for shp in shapes:
        n = math.prod(shp)
        out.append(flat[off:off + n].reshape(shp))
        off += n
    return out


def _shard_of(full, axis, j):
    n = full.shape[axis] // 4
    return lax.slice_in_dim(full, j * n, (j + 1) * n, axis=axis)


def kernel(x, even_w_in, even_conv_w, even_conv_b, ssd_a_log, ssd_dt_bias, ssd_d, ssd_norm_w, lru_w_a, lru_b_a, lru_w_x, lru_b_x, lru_lambda, even_w_out, odd_w_in, hgrn_lb_logits, hgrn_norm_w, odd_w_out, norm_mix, norm_mlp, mlp_w1, mlp_w2, norm_final, loss_target, m_even_w_in, m_even_conv_w, m_even_conv_b, m_ssd_a_log, m_ssd_dt_bias, m_ssd_d, m_ssd_norm_w, m_lru_w_a, m_lru_b_a, m_lru_w_x, m_lru_b_x, m_lru_lambda, m_even_w_out, m_odd_w_in, m_hgrn_lb_logits, m_hgrn_norm_w, m_odd_w_out, m_norm_mix, m_norm_mlp, m_mlp_w1, m_mlp_w2, m_norm_final, v_even_w_in, v_even_conv_w, v_even_conv_b, v_ssd_a_log, v_ssd_dt_bias, v_ssd_d, v_ssd_norm_w, v_lru_w_a, v_lru_b_a, v_lru_w_x, v_lru_b_x, v_lru_lambda, v_even_w_out, v_odd_w_in, v_hgrn_lb_logits, v_hgrn_norm_w, v_odd_w_out, v_norm_mix, v_norm_mlp, v_mlp_w1, v_mlp_w2, v_norm_final):
    names = [n for n, _, _, _ in WEIGHTS]
    w_loc = dict(zip(names, (even_w_in, even_conv_w, even_conv_b, ssd_a_log, ssd_dt_bias, ssd_d, ssd_norm_w, lru_w_a, lru_b_a, lru_w_x, lru_b_x, lru_lambda, even_w_out, odd_w_in, hgrn_lb_logits, hgrn_norm_w, odd_w_out, norm_mix, norm_mlp, mlp_w1, mlp_w2, norm_final)))
    m_loc = dict(zip(names, (m_even_w_in, m_even_conv_w, m_even_conv_b, m_ssd_a_log, m_ssd_dt_bias, m_ssd_d, m_ssd_norm_w, m_lru_w_a, m_lru_b_a, m_lru_w_x, m_lru_b_x, m_lru_lambda, m_even_w_out, m_odd_w_in, m_hgrn_lb_logits, m_hgrn_norm_w, m_odd_w_out, m_norm_mix, m_norm_mlp, m_mlp_w1, m_mlp_w2, m_norm_final)))
    v_loc = dict(zip(names, (v_even_w_in, v_even_conv_w, v_even_conv_b, v_ssd_a_log, v_ssd_dt_bias, v_ssd_d, v_ssd_norm_w, v_lru_w_a, v_lru_b_a, v_lru_w_x, v_lru_b_x, v_lru_lambda, v_even_w_out, v_odd_w_in, v_hgrn_lb_logits, v_hgrn_norm_w, v_odd_w_out, v_norm_mix, v_norm_mlp, v_mlp_w1, v_mlp_w2, v_norm_final)))
    spec = {n: (blk, full, ax) for n, blk, full, ax in WEIGHTS}

    small = [n for n in names if n not in BIG]
    two_d = lambda n, v: v.reshape(BIG_2D[n])

    me = 2 * lax.axis_index("x") + lax.axis_index("y")
    cc = lax.axis_index("c")
    put = lambda whole, part, k: lax.dynamic_update_slice_in_dim(whole, part[None], k, axis=0)
    own = {n: two_d(n, w_loc[n]).astype(BF16) for n in BIG}
    own["small"] = _pack([w_loc[n] for n in SMALL_SHARDED], 16, F32)
    fill = lambda got, keys: [put(g, own[k], me) for g, k in zip(got, keys)]
    first = ("even_w_in", "small")
    g_in0, g_small = fill(_gather_chips([own[k] for k in first]), first)
    w_main0, w_dt0 = _assemble_in0(g_in0)
    w_full = {n: w_loc[n] for n in names if spec[n][2] is None}
    shards = [_unpack(g_small[j], [spec[n][0] for n in SMALL_SHARDED]) for j in range(4)]
    for n in ("mlp_w1", "mlp_w2"):
        for l in range(2):
            own[f"{n}_l{l}"] = w_loc[n][l].astype(BF16)
    layers = lambda n: (f"{n}_l0", f"{n}_l1")
    carries = {"mlp_w1": _gather_carry([own[k] for k in layers("mlp_w1") + ("even_w_out",)]),
               "mlp_w2": _gather_carry([own[k] for k in layers("mlp_w2")]),
               "odd": _gather_carry([own["odd_w_in"], own["odd_w_out"]])}

    def arrived(key, got):
        if key == "odd":
            g_in1, g_out1 = fill(got, ("odd_w_in", "odd_w_out"))
            return {"odd_w_in": jnp.concatenate([g_in1[j] for j in range(4)], axis=1)[None],
                    "odd_w_out": g_out1.reshape(1, 1024, 1024)}
        g = fill(got[:2], layers(key))
        if key == "mlp_w2":
            return {key: [v.reshape(4096, 1024) for v in g]}
        (g_out0,) = fill(got[2:], ("even_w_out",))
        return {key: g, "even_w_out": g_out0.reshape(1, 2048, 1024)}

    for i, n in enumerate(SMALL_SHARDED):
        w_full[n] = jnp.concatenate([shards[j][i] for j in range(4)], axis=spec[n][2])

    cidx = cc.astype(jnp.int32).reshape(1)

    def pair_reduce(tags, tensors, run=None):
        lands = run(_pair_swap_carry(tensors)) if run else _pair_swap(tags[0], tensors)
        return [_pair_add(tag, g, land, cidx) for tag, g, land in zip(tags, tensors, lands)]

    loss_vec, grad_x, grads, big, (early_sums, early_landed) = _local_step(
        x, loss_target, w_full, w_main0, w_dt0, pair_reduce, (carries, arrived))
    loss = lax.psum(loss_vec[0, 0], ("x", "y", "c"))

    def dest_pack(j):
        return _pack([grads[n].reshape(spec[n][1]) if spec[n][2] is None else _shard_of(grads[n].reshape(spec[n][1]), spec[n][2], j)
                      for n in small], SMALL_ROWS, F32)

    late_tags = LATE + ("small",)
    late_sums = pair_reduce(late_tags, [big[n] for n in LATE] + [jnp.stack([dest_pack(j) for j in range(4)])])
    tags = EARLY + MID + late_tags
    chip_sums = list(early_sums) + late_sums
    landed = [put(land, lax.dynamic_index_in_dim(cs, me, axis=0, keepdims=False), me)
              for land, cs in zip(list(early_landed) + list(_chip_scatter("late", late_sums)), chip_sums)]
    halves = [_chip_sum(tag, land) for tag, land in zip(tags, landed)]
    red = {tag: put(r, h, cc).reshape(-1, r.shape[-1]) for tag, r, h in zip(tags, _pair_join(halves), halves)}
    for n in ("mlp_w1", "mlp_w2"):
        red[n] = jnp.concatenate([red[n + "_l0"], red[n + "_l1"]], axis=0)

    outs = {}
    for n, g in ((n, red[n]) for n in BIG):
        res = (g, *_adamw(n, g, two_d(n, w_loc[n]), two_d(n, m_loc[n]), two_d(n, v_loc[n])))
        outs[n] = [r.reshape(spec[n][0]) for r in res]
    blocks = [spec[n][0] for n in small]
    wp, mp, vp = (_pack([src[n] for n in small], SMALL_ROWS, F32) for src in (w_loc, m_loc, v_loc))
    res = (red["small"], *_adamw("small", red["small"], wp, mp, vp))
    unpacked = [_unpack(r, blocks) for r in res]
    for i, n in enumerate(small):
        outs[n] = [u[i] for u in unpacked]
    return (loss, grad_x, *[outs[n][k] for k in range(4) for n in names])
```

```python
import functools
import math

import jax
import jax.numpy as jnp
from jax import lax
from jax.experimental import pallas as pl
from jax.experimental.pallas import tpu as pltpu

F32 = jnp.float32
BF16 = jnp.bfloat16
MXU_DTYPE = jnp.bfloat16
MESH = pl.DeviceIdType.MESH

D_MODEL = 1024
EPS = 1e-6
SSD_HEADS = 16
SSD_HEADDIM = 64
HEAD_SHIFT = 6
SSD_GROUPS = 4
SSD_STATE = 128
SSD_CHUNK = 128
LRU_C = 8.0
LRU_ROWS = 256
HGRN_HEADS = 8
HGRN_HEADDIM = 128
HGRN_SUB = 32
HGRN_SUB_SHIFT = 5
HGRN_BLOCK = 128
HGRN_SCALE = HGRN_HEADDIM ** -0.5
CONV_ROWS = 512
ROWS_FWD = 512
ROWS_BWD = 256

ADAM_LR = 0.001
ADAM_B1 = 0.9
ADAM_B2 = 0.999
ADAM_EPS = 1e-08
ADAM_WD = 0.01
ADAM_STEP = 10

VMEM_LIMIT = 56 * 1024 * 1024
PACK_COLS = 1024
SMALL_ROWS = 288

WEIGHTS = (
    ("even_w_in", (1, 1024, 1288), (1, 1024, 5152), 2),
    ("even_conv_w", (1, 4, 768), (1, 4, 3072), 2),
    ("even_conv_b", (1, 3072), (1, 3072), None),
    ("ssd_a_log", (1, 2, 16), (1, 2, 16), None),
    ("ssd_dt_bias", (1, 2, 16), (1, 2, 16), None),
    ("ssd_d", (1, 16), (1, 16), None),
    ("ssd_norm_w", (1, 1024), (1, 1024), None),
    ("lru_w_a", (1, 2, 16, 64, 64), (1, 2, 16, 64, 64), None),
    ("lru_b_a", (1, 2, 256), (1, 2, 1024), 2),
    ("lru_w_x", (1, 2, 16, 64, 64), (1, 2, 16, 64, 64), None),
    ("lru_b_x", (1, 2, 256), (1, 2, 1024), 2),
    ("lru_lambda", (1, 2, 256), (1, 2, 1024), 2),
    ("even_w_out", (1, 512, 1024), (1, 2048, 1024), 1),
    ("odd_w_in", (1, 1024, 1280), (1, 1024, 5120), 2),
    ("hgrn_lb_logits", (2, 1024), (2, 1024), None),
    ("hgrn_norm_w", (1, 256), (1, 1024), 1),
    ("odd_w_out", (1, 256, 1024), (1, 1024, 1024), 1),
    ("norm_mix", (2, 1024), (2, 1024), None),
    ("norm_mlp", (2, 1024), (2, 1024), None),
    ("mlp_w1", (2, 1024, 1024), (2, 1024, 4096), 2),
    ("mlp_w2", (2, 1024, 1024), (2, 4096, 1024), 1),
    ("norm_final", (1024,), (1024,), None),
)
BIG = ("even_w_in", "even_w_out", "odd_w_in", "odd_w_out", "mlp_w1", "mlp_w2")
BIG_2D = {"even_w_in": (1024, 1288), "even_w_out": (512, 1024), "odd_w_in": (1024, 1280), "odd_w_out": (256, 1024),
          "mlp_w1": (2048, 1024), "mlp_w2": (2048, 1024)}
SMALL_SHARDED = ("even_conv_w", "lru_b_a", "lru_b_x", "lru_lambda", "hgrn_norm_w")


def _pcall(body, carry=None, **kw):
    if carry is not None:
        srcs, shapes, scratch, start, finish = carry
        grid, inner = kw["grid"], body
        as_tuple = lambda v: tuple(v) if isinstance(v, (tuple, list)) else (v,)
        out_specs, out_shape, own_scratch = as_tuple(kw["out_specs"]), as_tuple(kw["out_shape"]), list(kw.get("scratch_shapes", ()))
        a = len(kw["in_specs"])
        b = a + len(srcs)
        c = b + len(out_specs)
        d = c + len(shapes)
        e = d + len(own_scratch)

        def body(*refs):
            ids = [pl.program_id(ax) for ax in range(len(grid))]
            first = functools.reduce(jnp.logical_and, [i == 0 for i in ids])
            last = functools.reduce(jnp.logical_and, [i == g - 1 for i, g in zip(ids, grid)])
            pl.when(first)(lambda: start(refs[a:b], refs[c:d], *refs[e:]))
            inner(*refs[:a], *refs[b:c], *refs[d:e])
            pl.when(last)(lambda: finish(refs[a:b], refs[c:d], *refs[e:]))

        kw = dict(kw, in_specs=list(kw["in_specs"]) + [ANY] * len(srcs), out_specs=out_specs + (ANY,) * len(shapes),
                  out_shape=out_shape + tuple(shapes), scratch_shapes=own_scratch + list(scratch))
    return pl.pallas_call(body, **kw)


def _params(**kw):
    return pltpu.CompilerParams(vmem_limit_bytes=VMEM_LIMIT, **kw)


def _tile(n, pref):
    if n <= pref:
        return n
    t = (pref // 128) * 128
    while n % t:
        t -= 128
    return t


def _dot(a, b, dims=(((1,), (0,)), ((), ()))):
    return lax.dot_general(a, b, dims, preferred_element_type=F32)


_NN = (((1,), (0,)), ((), ()))
_NT = (((1,), (1,)), ((), ()))
_TN = (((0,), (0,)), ((), ()))


def _mx(v):
    return v.astype(MXU_DTYPE)


def _dot01(a, b, dims=_NN, *, split, terms):
    acc, rest = None, (a if split == "a" else b)
    for _ in range(terms):
        piece = _mx(rest)
        part = _dot(piece, _mx(b), dims) if split == "a" else _dot(_mx(a), piece, dims)
        acc = part if acc is None else acc + part
        rest = rest - piece.astype(F32)
    return acc


def _mm(name, a, b, mode, *, out_dtype=F32, res=None, relu2=False, relu2_of=None, col_shards=1, carry=None):
    shards = b.shape[0] if b.ndim == 3 else 0
    b2 = b.shape[1:] if shards else b.shape
    if mode == "nn":
        (m, kk), n = a.shape, b2[1] * max(shards, 1)
    elif mode == "nt":
        (m, kk), n = a.shape, b2[0]
    else:
        (kk, m), (_, n) = a.shape, b.shape
    assert res is None or relu2_of is None
    tk_pref = 1024
    if mode == "tn" and a.dtype.itemsize == 2 and b.dtype.itemsize == 2:
        tk_pref = 2048
    tm, tn, tk = _tile(m, 1024), _tile(n // col_shards, 1024), _tile(kk, tk_pref)
    nk = kk // tk
    dims = {"nn": _NN, "nt": _NT, "tn": _TN}[mode]
    a_spec = pl.BlockSpec((tk, tm), lambda i, j, k: (k, i)) if mode == "tn" else pl.BlockSpec((tm, tk), lambda i, j, k: (i, k))
    b_spec = pl.BlockSpec((tn, tk), lambda i, j, k: (j, k)) if mode == "nt" else pl.BlockSpec((tk, tn), lambda i, j, k: (k, j))
    if shards and mode == "nn":
        assert tn == b2[1]
        b_spec = pl.BlockSpec((None, tk, tn), lambda i, j, k: (j, k, 0))
    o_spec = pl.BlockSpec((tm, tn), lambda i, j, k: (i, j))
    o_shape = (m, n)
    if col_shards > 1:
        assert tn * col_shards == n and res is None and not relu2
        o_spec = pl.BlockSpec((None, tm, tn), lambda i, j, k: (j, i, 0))
        o_shape = (col_shards, m, tn)
    extra = res if res is not None else relu2_of
    has_res = extra is not None

    def body(*refs):
        a_ref, b_ref = refs[0], refs[1]
        res_ref = refs[2] if has_res else None
        outs = refs[2 + has_res:2 + has_res + 1 + relu2]

        def finish(r):
            if res is not None:
                r = r + res_ref[...]
            if relu2_of is not None:
                r = r * (2.0 * jnp.maximum(res_ref[...].astype(F32), 0.0))
            if relu2:
                outs[0][...] = r.astype(outs[0].dtype)
                outs[1][...] = jnp.square(jnp.maximum(r, 0.0)).astype(outs[1].dtype)
            else:
                outs[0][...] = r.astype(outs[0].dtype)

        prod = _dot(_mx(a_ref[...]), _mx(b_ref[...]), dims)
        if nk == 1:
            finish(prod)
            return
        acc = refs[-1]
        k = pl.program_id(2)

        @pl.when(k == 0)
        def _():
            acc[...] = prod

        @pl.when(k > 0)
        def _():
            acc[...] += prod

        @pl.when(k == nk - 1)
        def _():
            finish(acc[...])

    in_specs = [a_spec, b_spec] + ([o_spec] if has_res else [])
    if relu2:
        out_shape = (jax.ShapeDtypeStruct((m, n), BF16), jax.ShapeDtypeStruct((m, n), BF16))
        out_specs = (o_spec, o_spec)
    else:
        out_shape = jax.ShapeDtypeStruct(o_shape, out_dtype)
        out_specs = o_spec
    args = (a, b) + ((extra,) if has_res else ()) + (tuple(carry[0]) if carry else ())
    return _pcall(body, carry=carry, name=name, grid=(m // tm, n // tn, nk), in_specs=in_specs, out_specs=out_specs,
                  out_shape=out_shape, scratch_shapes=[pltpu.VMEM((tm, tn), F32)] if nk > 1 else [],
                  compiler_params=_params())(*args)


def _mm_sum_nt(name, parts, wblocks, norm_bwd=None, add=()):
    parts = [p if isinstance(p, tuple) else (p, 0, p.shape[1]) for p in parts]
    m, npart = parts[0][0].shape[0], len(parts)
    n = wblocks[0][0].shape[-2]
    tm, tn = _tile(m, 512), _tile(n, 1024)
    assert norm_bwd is None or tn == n

    def body(*refs):
        acc = _dot(_mx(refs[0][...]), _mx(refs[npart][...]), _NT)
        for k in range(1, npart):
            acc = acc + _dot(_mx(refs[k][...]), _mx(refs[npart + k][...]), _NT)
        if norm_bwd is None:
            for r in refs[2 * npart:-1]:
                acc = acc + r[...]
            refs[-1][...] = acc
            return
        x_ref, g_ref, res_ref, dx_ref, dg_ref = refs[2 * npart:]
        _, vjp = jax.vjp(_f_norm, x_ref[...], g_ref[...])
        dx, dg = vjp((acc,))
        dx_ref[...] = dx + res_ref[...]

        @pl.when(pl.program_id(0) == 0)
        def _():
            dg_ref[...] = jnp.zeros_like(dg_ref)

        dg_ref[...] += dg

    row = pl.BlockSpec((tm, tn), lambda i, j: (i, j))
    vec = pl.BlockSpec((1, tn), lambda i, j: (0, j))
    in_specs = [pl.BlockSpec((tm, wd), lambda i, j, cb=cb: (i, cb)) for _, cb, wd in parts]
    for (_, _, wd), (w, cb) in zip(parts, wblocks):
        in_specs.append(pl.BlockSpec((None, tn, wd), lambda i, j, cb=cb: (cb, j, 0)) if w.ndim == 3
                        else pl.BlockSpec((tn, wd), lambda i, j, cb=cb: (j, cb)))
    args = [p for p, _, _ in parts] + [w for w, _ in wblocks]
    if norm_bwd is None:
        return _pcall(body, name=name, grid=(m // tm, n // tn), in_specs=in_specs + [row] * len(add), out_specs=row,
                      out_shape=jax.ShapeDtypeStruct((m, n), F32), compiler_params=_params())(*args, *add)
    return _pcall(body, name=name, grid=(m // tm, 1), in_specs=in_specs + [row, vec, row], out_specs=(row, vec),
                  out_shape=(jax.ShapeDtypeStruct((m, n), F32), jax.ShapeDtypeStruct((1, n), F32)),
                  compiler_params=_params())(*args, *norm_bwd)


def _pw_fwd(name, f, ins, params, out_dtypes, tc, ncol, tm=ROWS_FWD, groups=1):
    t = ins[0][0].shape[0]
    tm = min(tm, t)
    ni, npar = len(ins), len(params)
    gw = tc // groups

    def body(*refs):
        for g in range(groups):
            sl = slice(g * gw, (g + 1) * gw)
            vals = f(*[r[:, sl].astype(F32) for r in refs[:ni]], *[r[:, sl] for r in refs[ni:ni + npar]])
            for o, v in zip(refs[ni + npar:], vals):
                o[:, sl] = v.astype(o.dtype)

    in_specs = [pl.BlockSpec((tm, tc), lambda j, i, off=off: (i, off + j)) for _, off in ins]
    in_specs += [pl.BlockSpec((1, tc), lambda j, i, off=off: (0, off + j)) for _, off in params]
    out_specs = tuple(pl.BlockSpec((tm, tc), lambda j, i: (i, j)) for _ in out_dtypes)
    out_shape = tuple(jax.ShapeDtypeStruct((t, ncol * tc), d) for d in out_dtypes)
    return _pcall(body, name=name, grid=(ncol, t // tm), in_specs=in_specs, out_specs=out_specs, out_shape=out_shape,
                  compiler_params=_params())(*[a for a, _ in ins], *[p for p, _ in params])


def _pw_bwd(name, f, ins, params, douts, tc, ncol, want, adds=None, tm=ROWS_BWD, out_dtypes=None, groups=1):
    adds = adds or {}
    out_dtypes = out_dtypes or [F32] * len(want)
    t = ins[0][0].shape[0]
    tm = min(tm, t)
    ni, npar, nd, na = len(ins), len(params), len(douts), len(adds)
    add_keys = sorted(adds)
    gw = tc // groups

    def body(*refs):
        in_refs, p_refs = refs[:ni], refs[ni:ni + npar]
        d_refs = refs[ni + npar:ni + npar + nd]
        a_refs = refs[ni + npar + nd:ni + npar + nd + na]
        o_refs = refs[ni + npar + nd + na:]
        for p in range(npar):
            @pl.when(pl.program_id(1) == 0)
            def _(o=o_refs[len(want) + p]):
                o[...] = jnp.zeros_like(o)

        for g in range(groups):
            sl = slice(g * gw, (g + 1) * gw)
            _, vjp = jax.vjp(f, *[r[:, sl].astype(F32) for r in in_refs], *[r[:, sl] for r in p_refs])
            cts = vjp(tuple(d[:, sl].astype(F32) for d in d_refs))
            for o, kidx in zip(o_refs[:len(want)], want):
                v = cts[kidx]
                if kidx in adds:
                    v = v + a_refs[add_keys.index(kidx)][:, sl]
                o[:, sl] = v.astype(o.dtype)
            for p in range(npar):
                o_refs[len(want) + p][:, sl] += cts[ni + p]

    in_specs = [pl.BlockSpec((tm, tc), lambda j, i, off=off: (i, off + j)) for _, off in ins]
    in_specs += [pl.BlockSpec((1, tc), lambda j, i, off=off: (0, off + j)) for _, off in params]
    in_specs += [pl.BlockSpec((tm, tc), lambda j, i: (i, j)) for _ in range(nd + na)]
    out_specs = tuple([pl.BlockSpec((tm, tc), lambda j, i: (i, j)) for _ in want]
                      + [pl.BlockSpec((1, tc), lambda j, i: (0, j)) for _ in params])
    out_shape = tuple([jax.ShapeDtypeStruct((t, ncol * tc), dt) for dt in out_dtypes]
                      + [jax.ShapeDtypeStruct((1, ncol * tc), F32) for _ in params])
    res = _pcall(body, name=name, grid=(ncol, t // tm), in_specs=in_specs, out_specs=out_specs, out_shape=out_shape,
                 compiler_params=_params())(*[a for a, _ in ins], *[p for p, _ in params], *douts, *[adds[k] for k in add_keys])
    return list(res[:len(want)]), list(res[len(want):])


def _rms(x, g):
    return (x * lax.rsqrt(jnp.mean(x * x, axis=-1, keepdims=True) + EPS)) * g


def _f_norm(x, g):
    return (_rms(x, g),)


def _f_softplus(d, b):
    return (jax.nn.softplus(d + b),)


def _f_ssd_post(yf, yb, xs, z, dskip, nw):
    u = (yf + yb + dskip * xs) * jax.nn.silu(z)
    return (_rms(u, nw),)


def _neg_expm1(v):
    t = jnp.tanh(0.5 * v)
    return -2.0 * t / (1.0 - t)


def _f_lru_gates(pre_a, pre_x, u, ba, bx, lam):
    rg = jax.nn.sigmoid(pre_a + ba)
    ig = jax.nn.sigmoid(pre_x + bx)
    log_a = -LRU_C * rg * jax.nn.softplus(-lam)
    return jnp.exp(log_a), jnp.sqrt(_neg_expm1(2.0 * log_a)) * (ig * u)


def _f_lru_post(hf, hb, gate):
    return ((hf + hb) * jax.nn.gelu(gate),)


def _f_hgrn_pre(fr, l0, l1):
    lb = jax.nn.sigmoid(l1 - l0)
    k = (1.0 - lb) * jax.nn.sigmoid(-fr)
    return k, jnp.log1p(-k)


def _f_hgrn_post(of, ob, gate, nw):
    return (_rms(of + ob, nw) * jax.nn.silu(gate),)


def _loss_head(x, tgt, g, tm=ROWS_FWD):
    t, d = x.shape
    tm = min(tm, t)

    def body(x_ref, t_ref, g_ref, dx_ref, dg_ref, loss_ref):
        tv = t_ref[...]

        def lf(xv, gv):
            return 0.5 * jnp.sum(jnp.mean(jnp.square(_rms(xv, gv) - tv), axis=-1))

        val, vjp = jax.vjp(lf, x_ref[...], g_ref[...])
        dx, dg = vjp(jnp.ones((), F32))
        dx_ref[...] = dx

        @pl.when(pl.program_id(0) == 0)
        def _():
            dg_ref[...] = jnp.zeros_like(dg_ref)
            loss_ref[...] = jnp.zeros_like(loss_ref)

        dg_ref[...] += dg
        loss_ref[...] += jnp.full(loss_ref.shape, val, F32)

    row = pl.BlockSpec((tm, d), lambda i: (i, 0))
    vec = pl.BlockSpec((1, d), lambda i: (0, 0))
    return _pcall(body, name="loss_head", grid=(t // tm,), in_specs=[row, row, vec],
                  out_specs=(row, vec, pl.BlockSpec((1, 128), lambda i: (0, 0))),
                  out_shape=(jax.ShapeDtypeStruct((t, d), F32), jax.ShapeDtypeStruct((1, d), F32),
                             jax.ShapeDtypeStruct((1, 128), F32)), compiler_params=_params())(x, tgt, g)


def _shifted(x, d, prev, nxt, first, last):
    r = x.shape[0]
    row = lax.broadcasted_iota(jnp.int32, x.shape, 0)
    if d < 0:
        out = pltpu.roll(x, -d, 0)
        for q in range(-d):
            pv = jnp.where(first, 0.0, prev[8 + d + q:8 + d + q + 1, :])
            out = jnp.where(row == q, pv, out)
        return out
    out = pltpu.roll(x, r - d, 0)
    for q in range(d):
        nv = jnp.where(last, 0.0, nxt[q:q + 1, :])
        out = jnp.where(row == r - d + q, nv, out)
    return out


def _conv_fwd(p3, w, b, col0, ncol, silu, tc=1024):
    nbatch, s, _ = p3.shape
    ts = min(CONV_ROWS, s)
    nblk = s // ts

    def body(x_ref, pv_ref, nx_ref, w_ref, b_ref, o_ref, *act_ref):
        i = pl.program_id(1)
        first, last = i == 0, i == nblk - 1
        x, pv, nx = x_ref[...], pv_ref[...], nx_ref[...]
        wv = w_ref[...]
        out = b_ref[...] + wv[1:2] * x
        out = out + wv[0:1] * _shifted(x, -1, pv, nx, first, last)
        out = out + wv[2:3] * _shifted(x, 1, pv, nx, first, last)
        out = out + wv[3:4] * _shifted(x, 2, pv, nx, first, last)
        o_ref[...] = out
        if silu:
            act_ref[0][...] = jax.nn.silu(out)

    nb8 = s // 8
    cur = pl.BlockSpec((None, ts, tc), lambda n, i, j: (n, i, col0 + j))
    prev = pl.BlockSpec((None, 8, tc), lambda n, i, j: (n, jnp.maximum(i * (ts // 8) - 1, 0), col0 + j))
    nxt = pl.BlockSpec((None, 8, tc), lambda n, i, j: (n, jnp.minimum((i + 1) * (ts // 8), nb8 - 1), col0 + j))
    out = pl.BlockSpec((None, ts, tc), lambda n, i, j: (n, i, j))
    shp = jax.ShapeDtypeStruct((nbatch, s, ncol * tc), F32)
    return _pcall(body, name=f"conv_fwd{col0}", grid=(nbatch, nblk, ncol),
                  in_specs=[cur, prev, nxt, pl.BlockSpec((4, tc), lambda n, i, j: (0, col0 + j)),
                            pl.BlockSpec((1, tc), lambda n, i, j: (0, col0 + j))],
                  out_specs=(out, out) if silu else out, out_shape=(shp, shp) if silu else shp,
                  compiler_params=_params())(p3, p3, p3, w, b)


def _conv_bwd(dc3, p3, w, col, conv3=None):
    nbatch, s, tc = dc3.shape
    ts = min(CONV_ROWS, s)
    nblk = s // ts
    silu = conv3 is not None

    def body(d_ref, dpv_ref, dnx_ref, x_ref, pv_ref, nx_ref, w_ref, *rest):
        n, i = pl.program_id(0), pl.program_id(1)
        first, last = i == 0, i == nblk - 1
        d, dpv, dnx = d_ref[...], dpv_ref[...], dnx_ref[...]
        if silu:
            d, dpv, dnx = [jax.vjp(jax.nn.silu, c_ref[...])[1](t)[0] for c_ref, t in zip(rest[:3], (d, dpv, dnx))]
        dx_ref, dw_ref = rest[3 * silu:]
        x, pv, nx = x_ref[...], pv_ref[...], nx_ref[...]
        wv = w_ref[...]
        dx = wv[1:2] * d
        dx = dx + wv[0:1] * _shifted(d, 1, dpv, dnx, first, last)
        dx = dx + wv[2:3] * _shifted(d, -1, dpv, dnx, first, last)
        dx = dx + wv[3:4] * _shifted(d, -2, dpv, dnx, first, last)
        dx_ref[...] = dx.astype(dx_ref.dtype)

        @pl.when((n == 0) & (i == 0))
        def _():
            dw_ref[...] = jnp.zeros_like(dw_ref)

        dw_ref[0:1, :] += jnp.sum(d * _shifted(x, -1, pv, nx, first, last), axis=0, keepdims=True)
        dw_ref[1:2, :] += jnp.sum(d * x, axis=0, keepdims=True)
        dw_ref[2:3, :] += jnp.sum(d * _shifted(x, 1, pv, nx, first, last), axis=0, keepdims=True)
        dw_ref[3:4, :] += jnp.sum(d * _shifted(x, 2, pv, nx, first, last), axis=0, keepdims=True)
        dw_ref[4:5, :] += jnp.sum(d, axis=0, keepdims=True)

    nb8 = s // 8

    def specs(j):
        cur = pl.BlockSpec((None, ts, tc), lambda n, i: (n, i, j))
        prev = pl.BlockSpec((None, 8, tc), lambda n, i: (n, jnp.maximum(i * (ts // 8) - 1, 0), j))
        nxt = pl.BlockSpec((None, 8, tc), lambda n, i: (n, jnp.minimum((i + 1) * (ts // 8), nb8 - 1), j))
        return [cur, prev, nxt]

    return _pcall(body, name=f"conv_bwd{col}", grid=(nbatch, nblk),
                  in_specs=specs(0) + specs(col) + [pl.BlockSpec((4, tc), lambda n, i: (0, col))] + specs(col) * silu,
                  out_specs=(specs(0)[0], pl.BlockSpec((8, tc), lambda n, i: (0, 0))),
                  out_shape=(jax.ShapeDtypeStruct((nbatch, s, tc), BF16), jax.ShapeDtypeStruct((8, tc), F32)),
                  compiler_params=_params())(dc3, dc3, dc3, p3, p3, p3, w, *([conv3] * 3 * silu))


def _block_scan(coef, inp, reverse):
    r = coef.shape[0]
    row = lax.broadcasted_iota(jnp.int32, coef.shape, 0)
    a, b = coef, inp
    d = 1
    while d < r:
        if reverse:
            keep = row < r - d
            a_sh, b_sh = pltpu.roll(a, r - d, 0), pltpu.roll(b, r - d, 0)
        else:
            keep = row >= d
            a_sh, b_sh = pltpu.roll(a, d, 0), pltpu.roll(b, d, 0)
        b = b + a * jnp.where(keep, b_sh, 0.0)
        a = a * jnp.where(keep, a_sh, 1.0)
        d *= 2
    return a, b


def _lru_scan(a3, b3, reverse):
    nbatch, s, w = a3.shape
    ts = min(LRU_ROWS, s)
    nblk = s // ts
    edge = 0 if reverse else ts - 1

    def body(a_ref, b_ref, h_ref, carry):
        @pl.when(pl.program_id(1) == 0)
        def _():
            carry[...] = jnp.zeros_like(carry)

        ca, hb = _block_scan(a_ref[...], b_ref[...], reverse)
        h = hb + ca * carry[0:1, :]
        h_ref[...] = h
        carry[0:1, :] = h[edge:edge + 1, :]

    blk = pl.BlockSpec((None, ts, w), (lambda n, i: (n, nblk - 1 - i, 0)) if reverse else (lambda n, i: (n, i, 0)))
    return _pcall(body, name=f"lru_scan_r{int(reverse)}", grid=(nbatch, nblk), in_specs=[blk, blk], out_specs=blk,
                  out_shape=jax.ShapeDtypeStruct((nbatch, s, w), F32), scratch_shapes=[pltpu.VMEM((8, w), F32)],
                  compiler_params=_params())(a3, b3)


def _lru_scan_bwd(a3, h3, dh3, reverse, carry=None):
    nbatch, s, w = a3.shape
    ts = min(LRU_ROWS, s)
    nblk = s // ts
    nb8 = s // 8
    tpb = ts // 8

    def body(a_ref, aa_ref, h_ref, hh_ref, dh_ref, g_ref, da_ref, carry):
        i = pl.program_id(1)

        @pl.when(i == 0)
        def _():
            carry[...] = jnp.zeros_like(carry)

        a, h = a_ref[...], h_ref[...]
        row = lax.broadcasted_iota(jnp.int32, a.shape, 0)
        if reverse:
            a_edge = jnp.where(i == 0, 0.0, aa_ref[7:8, :])
            c = jnp.where(row == 0, a_edge, pltpu.roll(a, 1, 0))
            h_edge = jnp.where(i == nblk - 1, 0.0, hh_ref[0:1, :])
            h_sh = jnp.where(row == ts - 1, h_edge, pltpu.roll(h, ts - 1, 0))
        else:
            a_edge = jnp.where(i == 0, 0.0, aa_ref[0:1, :])
            c = jnp.where(row == ts - 1, a_edge, pltpu.roll(a, ts - 1, 0))
            h_edge = jnp.where(i == nblk - 1, 0.0, hh_ref[7:8, :])
            h_sh = jnp.where(row == 0, h_edge, pltpu.roll(h, 1, 0))
        cc, gb = _block_scan(c, dh_ref[...], not reverse)
        g = gb + cc * carry[0:1, :]
        g_ref[...] = g
        carry[0:1, :] = g[ts - 1:ts, :] if reverse else g[0:1, :]
        da_ref[...] = g * h_sh

    if reverse:
        bi = lambda i: i
    else:
        bi = lambda i: nblk - 1 - i
    blk = pl.BlockSpec((None, ts, w), lambda n, i: (n, bi(i), 0))
    before = pl.BlockSpec((None, 8, w), lambda n, i: (n, jnp.maximum(bi(i) * tpb - 1, 0), 0))
    after = pl.BlockSpec((None, 8, w), lambda n, i: (n, jnp.minimum((bi(i) + 1) * tpb, nb8 - 1), 0))
    a_tile, h_tile = (before, after) if reverse else (after, before)
    return _pcall(body, carry=carry, name=f"lru_scan_bwd_r{int(reverse)}", grid=(nbatch, nblk),
                  in_specs=[blk, a_tile, blk, h_tile, blk], out_specs=(blk, blk),
                  out_shape=(jax.ShapeDtypeStruct((nbatch, s, w), F32), jax.ShapeDtypeStruct((nbatch, s, w), F32)),
                  scratch_shapes=[pltpu.VMEM((8, w), F32)],
                  compiler_params=_params())(a3, a3, h3, h3, dh3, *(carry[0] if carry else ()))


def _head_expand(lane0):
    return (jnp.right_shift(lax.broadcasted_iota(jnp.int32, (128, 1024), 1), HEAD_SHIFT) + lane0
            == lax.broadcasted_iota(jnp.int32, (128, 1024), 0)).astype(F32)


def _head_reduce(lane0):
    return (jnp.right_shift(lax.broadcasted_iota(jnp.int32, (1024, 128), 0), HEAD_SHIFT) + lane0
            == lax.broadcasted_iota(jnp.int32, (1024, 128), 1)).astype(F32)


def _time_mask(q, reverse):
    ri = lax.broadcasted_iota(jnp.int32, (q, q), 0)
    ci = lax.broadcasted_iota(jnp.int32, (q, q), 1)
    return (ri <= ci) if reverse else (ri >= ci)


def _ssd_common(xs_ref, bc_ref, dt_ref, al_ref, reverse, lane0):
    q = xs_ref.shape[0]
    edge = 0 if reverse else q - 1
    dt = dt_ref[...]
    a = -jnp.exp(al_ref[...])
    mask = _time_mask(q, reverse)
    expand = _head_expand(lane0)
    cum = _dot01(mask.astype(F32), dt * a, split="b", terms=3)
    cum_x = _dot01(cum, expand, split="a", terms=2)
    dt_x = _dot01(dt, expand, split="a", terms=2)
    last_x = cum_x[edge:edge + 1, :]
    xs = xs_ref[...]
    bc = bc_ref[...]
    return dict(q=q, edge=edge, lane0=lane0, dt=dt, a=a, mask=mask, cum_t=cum.T, cum_x=cum_x, dt_x=dt_x, xs=xs,
                v=xs * dt_x, e_c=jnp.exp(cum_x), w=jnp.exp(last_x - cum_x), e_l=jnp.exp(last_x),
                bm=bc[:, :512], cm=bc[:, 512:])


def _ssd_decay(c, h):
    row = c["lane0"] + h
    seg = c["cum_x"][:, h * SSD_HEADDIM:h * SSD_HEADDIM + 1] - c["cum_t"][row:row + 1, :]
    return jnp.where(c["mask"], jnp.exp(jnp.minimum(seg, 0.0)), 0.0)


def _head_masks():
    lane = jnp.right_shift(lax.broadcasted_iota(jnp.int32, (1, 256), 1), HEAD_SHIFT)
    return [lane == e for e in range(4)]


def _ssd_fwd(xbc3, dt3, alog, reverse, carry=None):
    nbatch, s, _ = xbc3.shape
    q = min(SSD_CHUNK, s)
    nc = s // q
    lane0 = SSD_HEADS * int(reverse)

    def body(xs_ref, bc_ref, dt_ref, al_ref, y_ref, st_ref, st):
        @pl.when(pl.program_id(1) == 0)
        def _():
            st[...] = jnp.zeros_like(st)

        st_ref[...] = st[...]
        c = _ssd_common(xs_ref, bc_ref, dt_ref, al_ref, reverse, lane0)
        hm = _head_masks()
        for g in range(SSD_GROUPS):
            sl = slice(g * 256, (g + 1) * 256)
            cg, bg = _mx(c["cm"][:, g * 128:(g + 1) * 128]), _mx(c["bm"][:, g * 128:(g + 1) * 128])
            cb = _dot(cg, bg, _NT)
            vg = c["v"][:, sl]
            s0 = st[:, sl]
            yg = _dot(cg, _mx(s0)) * c["e_c"][:, sl]
            for e in range(4):
                m = _ssd_decay(c, 4 * g + e) * cb
                yg = yg + _dot(_mx(m), _mx(jnp.where(hm[e], vg, 0.0)))
            y_ref[:, sl] = yg
            st[:, sl] = c["e_l"][:, sl] * s0 + _dot(bg, _mx(vg * c["w"][:, sl]), _TN)

    ck = (lambda i: nc - 1 - i) if reverse else (lambda i: i)
    xs_spec = pl.BlockSpec((None, q, 1024), lambda n, i: (n, ck(i), 0))
    bc_spec = pl.BlockSpec((None, q, 1024), lambda n, i: (n, ck(i), 1))
    dt_spec = pl.BlockSpec((None, q, 128), lambda n, i: (n, ck(i), 0))
    al_spec = pl.BlockSpec((1, 128), lambda n, i: (0, 0))
    st_spec = pl.BlockSpec((None, None, 128, 1024), lambda n, i: (n, ck(i), 0, 0))
    return _pcall(body, carry=carry, name=f"ssd_fwd_r{int(reverse)}", grid=(nbatch, nc),
                  in_specs=[xs_spec, bc_spec, dt_spec, al_spec], out_specs=(xs_spec, st_spec),
                  out_shape=(jax.ShapeDtypeStruct((nbatch, s, 1024), F32), jax.ShapeDtypeStruct((nbatch, nc, 128, 1024), F32)),
                  scratch_shapes=[pltpu.VMEM((128, 1024), F32)],
                  compiler_params=_params())(xbc3, xbc3, dt3, alog, *(carry[0] if carry else ()))


def _ssd_bwd(xbc3, dt3, alog, st4, dy3, reverse, add_to=(), scatter=()):
    nbatch, s, _ = xbc3.shape
    q = min(SSD_CHUNK, s)
    nc = s // q
    lane0 = SSD_HEADS * int(reverse)
    nadd, ns = len(add_to), len(scatter)

    def body(xs_ref, bc_ref, dt_ref, al_ref, st0_ref, dy_ref, *rest):
        adds, srcs, rest = rest[:nadd], rest[nadd:nadd + ns], rest[nadd + ns:]
        (dxs_ref, dbc_ref, ddt_ref, dal_ref), lands, dst = rest[:4], rest[4:4 + ns], rest[4 + ns]
        n, i = pl.program_id(0), pl.program_id(1)
        if ns:
            sends, arrivals = _scatter_copies(srcs, lands, *rest[5 + ns:])

            @pl.when((n == 0) & (i == 0))
            def _():
                for cp in sends:
                    cp.start()

        @pl.when(i == 0)
        def _():
            dst[...] = jnp.zeros_like(dst)

        @pl.when((i == 0) & (n == 0))
        def _():
            dal_ref[...] = jnp.zeros_like(dal_ref)

        c = _ssd_common(xs_ref, bc_ref, dt_ref, al_ref, reverse, lane0)
        hm = _head_masks()
        reduce_m = _head_reduce(lane0)
        s0_all, ds1_all, dy = st0_ref[...], dst[...], dy_ref[...]
        lane = lax.broadcasted_iota(jnp.int32, (q, 128), 1)
        sub = lax.broadcasted_iota(jnp.int32, (128, q), 0)
        rowacc = jnp.zeros((q, 128), F32)
        colacc_t = jnp.zeros((128, q), F32)
        dv_l, yst_l, dvbar_l, dk_l, dc_l = [], [], [], [], []
        for g in range(SSD_GROUPS):
            sl = slice(g * 256, (g + 1) * 256)
            cg, bg = _mx(c["cm"][:, g * 128:(g + 1) * 128]), _mx(c["bm"][:, g * 128:(g + 1) * 128])
            cb = _dot(cg, bg, _NT)
            vg, dyg, wg, ecg = c["v"][:, sl], dy[:, sl], c["w"][:, sl], c["e_c"][:, sl]
            s0, ds1 = _mx(s0_all[:, sl]), _mx(ds1_all[:, sl])
            dye = _mx(dyg * ecg)
            yst_l.append(_dot(cg, s0) * ecg)
            dcg = _dot(dye, s0, _NT)
            dst[:, sl] = c["e_l"][:, sl] * ds1_all[:, sl] + _dot(cg, dye, _TN)
            vbar = _mx(vg * wg)
            dvbar = _dot(bg, ds1)
            dvbar_l.append(dvbar)
            dvg = dvbar * wg
            dkg = _dot(vbar, ds1, _NT)
            for e in range(4):
                h = 4 * g + e
                m = _ssd_decay(c, h)
                dyh, vh = _mx(jnp.where(hm[e], dyg, 0.0)), _mx(jnp.where(hm[e], vg, 0.0))
                dvg = dvg + _dot(_mx(m * cb), dyh, _TN)
                dcb = _dot(dyh, vh, _NT) * m
                dcbb = _mx(dcb)
                dcg = dcg + _dot(dcbb, bg)
                dkg = dkg + _dot(dcbb, cg, _TN)
                wmat = dcb * cb
                rowacc = jnp.where(lane == lane0 + h, jnp.sum(wmat, axis=1, keepdims=True), rowacc)
                colacc_t = jnp.where(sub == lane0 + h, jnp.sum(wmat, axis=0, keepdims=True), colacc_t)
            dv_l.append(dvg)
            dk_l.append(dkg)
            dc_l.append(dcg)
        dv = jnp.concatenate(dv_l, axis=1)
        yst = jnp.concatenate(yst_l, axis=1)
        dvbar = jnp.concatenate(dvbar_l, axis=1)
        t1 = _dot01(dy * yst, reduce_m, split="a", terms=3)
        t2 = _dot01(c["v"] * c["w"] * dvbar, reduce_m, split="a", terms=3)
        dlast = jnp.sum(t2, axis=0, keepdims=True) + _dot01(
            c["e_l"] * jnp.sum(ds1_all * s0_all, axis=0, keepdims=True), reduce_m, split="a", terms=2)
        dcum = rowacc - colacc_t.T + t1 - t2
        dcum = dcum + jnp.where(lax.broadcasted_iota(jnp.int32, (q, 128), 0) == c["edge"], dlast, 0.0)
        dda = _dot01(c["mask"].astype(F32), dcum, _TN, split="b", terms=3)
        ddt = dda * c["a"] + _dot01(dv * c["xs"], reduce_m, split="a", terms=2)
        dal_ref[...] += jnp.sum(dda * c["dt"], axis=0, keepdims=True) * c["a"]
        dxs = dv * c["dt_x"]
        dbc = jnp.concatenate(dk_l + dc_l, axis=1)
        if nadd:
            for a_ref in adds[:-2]:
                dxs = dxs + a_ref[...]
            dbc = dbc + adds[-2][...]
            ddt = ddt + adds[-1][...]
        ddt_ref[...] = ddt
        dxs_ref[...] = dxs
        dbc_ref[...] = dbc
        if ns:
            @pl.when((n == nbatch - 1) & (i == nc - 1))
            def _():
                for cp in arrivals:
                    cp.wait_recv()
                for cp in sends:
                    cp.wait_send()

    ck = (lambda i: i) if reverse else (lambda i: nc - 1 - i)
    xs_spec = pl.BlockSpec((None, q, 1024), lambda n, i: (n, ck(i), 0))
    bc_spec = pl.BlockSpec((None, q, 1024), lambda n, i: (n, ck(i), 1))
    dt_spec = pl.BlockSpec((None, q, 128), lambda n, i: (n, ck(i), 0))
    al_spec = pl.BlockSpec((1, 128), lambda n, i: (0, 0))
    st_spec = pl.BlockSpec((None, None, 128, 1024), lambda n, i: (n, ck(i), 0, 0))
    return _pcall(body, name=f"ssd_bwd_r{int(reverse)}", grid=(nbatch, nc),
                  in_specs=([xs_spec, bc_spec, dt_spec, al_spec, st_spec, xs_spec] + [xs_spec] * (nadd - 1)
                            + [dt_spec] * bool(nadd) + [ANY] * ns),
                  out_specs=(xs_spec, xs_spec, dt_spec, al_spec) + (ANY,) * ns,
                  out_shape=(jax.ShapeDtypeStruct((nbatch, s, 1024), F32), jax.ShapeDtypeStruct((nbatch, s, 1024), F32),
                             jax.ShapeDtypeStruct((nbatch, s, 128), F32), jax.ShapeDtypeStruct((1, 128), F32))
                  + tuple(jax.ShapeDtypeStruct(c.shape, c.dtype) for c in scatter),
                  scratch_shapes=[pltpu.VMEM((128, 1024), F32)] + (_scatter_scratch(ns) if ns else []),
                  compiler_params=_params())(xbc3, xbc3, dt3, alog, st4, dy3, *add_to, *scatter)


def _gla_block(q, k, g, reverse):
    bq = g.shape[0]
    nsub = bq // HGRN_SUB
    edge = 0 if reverse else bq - 1
    ri = lax.broadcasted_iota(jnp.int32, (bq, bq), 0)
    ci = lax.broadcasted_iota(jnp.int32, (bq, bq), 1)
    rb, cb = jnp.right_shift(ri, HGRN_SUB_SHIFT), jnp.right_shift(ci, HGRN_SUB_SHIFT)
    mask = (ri <= ci) if reverse else (ri >= ci)
    m_within = (mask & (rb == cb)).astype(F32)
    m_before = ((cb > rb) if reverse else (cb < rb)).astype(F32)
    bl = _dot01(m_within, g, split="b", terms=3)
    c = _dot01(m_before, g, split="b", terms=3)
    last = c[edge:edge + 1, :] + bl[edge:edge + 1, :]
    ebl, enbl, ec, elc = jnp.exp(bl), jnp.exp(-bl), jnp.exp(c), jnp.exp(last - c)
    qh = q * HGRN_SCALE * ebl
    kh = k * enbl
    blk = jnp.right_shift(lax.broadcasted_iota(jnp.int32, (bq, 1), 0), HGRN_SUB_SHIFT)
    scale = []
    for i in range(nsub):
        valid = (blk >= i) if reverse else (blk <= i)
        ex = jnp.where(valid, c[i * HGRN_SUB:i * HGRN_SUB + 1, :] - c, 0.0)
        scale.append(jnp.where(valid, jnp.exp(ex), 0.0))
    return dict(bq=bq, nsub=nsub, edge=edge, mask=mask, m_within=m_within, m_before=m_before, ebl=ebl, enbl=enbl, ec=ec,
                elc=elc, e_l=jnp.exp(last), qh=qh, qt=qh * ec, kh=kh, kb=kh * elc, scale=scale)


def _gla_scores(c, hs):
    keys = [_mx(c["kh"][:, hs] * c["scale"][i][:, hs]) for i in range(c["nsub"])]
    rows = [_dot(_mx(c["qh"][i * HGRN_SUB:(i + 1) * HGRN_SUB, hs]), keys[i], _NT) for i in range(c["nsub"])]
    return jnp.where(c["mask"], jnp.concatenate(rows, axis=0), 0.0), keys


def _gla_specs(nbatch, s, w, reverse_order):
    bq = min(HGRN_BLOCK, s)
    nblk = s // bq
    bi = (lambda i: nblk - 1 - i) if reverse_order else (lambda i: i)
    col = lambda cb: pl.BlockSpec((nbatch, bq, w), lambda i: (0, bi(i), cb))
    st_spec = pl.BlockSpec((nbatch, None, 128, w), lambda i: (0, bi(i), 0, 0))
    return bq, nblk, col, st_spec


def _gla_fwd(proj3, l0, l1, reverse):
    nbatch, s, w5 = proj3.shape
    w = w5 // 5
    bq, nblk, col, st_spec = _gla_specs(nbatch, s, w, reverse)
    vec = pl.BlockSpec((1, w), lambda i: (0, 0))

    def body(q_ref, f_ref, v_ref, l0_ref, l1_ref, o_ref, st_ref, st):
        @pl.when(pl.program_id(0) == 0)
        def _():
            st[...] = jnp.zeros_like(st)

        for b in range(nbatch):
            st_ref[b] = st[b]
            k, g = _f_hgrn_pre(f_ref[b], l0_ref[...], l1_ref[...])
            c = _gla_block(q_ref[b], k, g, reverse)
            v = v_ref[b]
            for h in range(HGRN_HEADS):
                hs = slice(h * 128, (h + 1) * 128)
                att, _ = _gla_scores(c, hs)
                vb = _mx(v[:, hs])
                s0 = st[b, :, hs]
                o_ref[b, :, hs] = _dot(_mx(att), vb) + _dot(_mx(c["qt"][:, hs]), _mx(s0), _NT)
                st[b, :, hs] = s0 * c["e_l"][:, hs] + _dot(vb, _mx(c["kb"][:, hs]), _TN)

    return _pcall(body, name=f"gla_fwd_r{int(reverse)}", grid=(nblk,),
                  in_specs=[col(0), col(1 + int(reverse)), col(3), vec, vec], out_specs=(col(0), st_spec),
                  out_shape=(jax.ShapeDtypeStruct((nbatch, s, w), F32), jax.ShapeDtypeStruct((nbatch, nblk, 128, w), F32)),
                  scratch_shapes=[pltpu.VMEM((nbatch, 128, w), F32)], compiler_params=_params())(proj3, proj3, proj3, l0, l1)


def _gla_bwd(proj3, l0, l1, st4, do3, reverse, add_to=None):
    nbatch, s, w5 = proj3.shape
    w = w5 // 5
    bq, nblk, col, st_spec = _gla_specs(nbatch, s, w, not reverse)
    nadd = 0 if add_to is None else 2
    vec = pl.BlockSpec((1, w), lambda i: (0, 0))

    def body(q_ref, f_ref, v_ref, l0_ref, l1_ref, st_ref, do_ref, *rest):
        adds, (dq_ref, df_ref, dv_ref, dl0_ref, dl1_ref, dst) = rest[:nadd], rest[nadd:]

        @pl.when(pl.program_id(0) == 0)
        def _():
            dst[...] = jnp.zeros_like(dst)
            dl0_ref[...] = jnp.zeros_like(dl0_ref)
            dl1_ref[...] = jnp.zeros_like(dl1_ref)

        row = lax.broadcasted_iota(jnp.int32, (bq, 128), 0)
        for b in range(nbatch):
            (k, g), pre_vjp = jax.vjp(_f_hgrn_pre, f_ref[b], l0_ref[...], l1_ref[...])
            c = _gla_block(q_ref[b], k, g, reverse)
            s0_all, ds1_all = st_ref[b], dst[b]
            v, dy = v_ref[b], do_ref[b]
            dbl_l, dc_l, dk_l = [], [], []
            for h in range(HGRN_HEADS):
                hs = slice(h * 128, (h + 1) * 128)
                att, keys = _gla_scores(c, hs)
                qh, qt, kh, kb = c["qh"][:, hs], c["qt"][:, hs], c["kh"][:, hs], c["kb"][:, hs]
                vb, dyb = _mx(v[:, hs]), _mx(dy[:, hs])
                s0, ds1 = s0_all[:, hs], ds1_all[:, hs]
                datt = _mx(jnp.where(c["mask"], _dot(dyb, vb, _NT), 0.0))
                dqh_rows = []
                dkh = jnp.zeros((bq, 128), F32)
                dc = jnp.zeros((bq, 128), F32)
                for i in range(c["nsub"]):
                    rs = slice(i * HGRN_SUB, (i + 1) * HGRN_SUB)
                    dqh_rows.append(_dot(datt[rs], keys[i]))
                    dki = _dot(datt[rs], _mx(qh[rs]), _TN)
                    sc = c["scale"][i][:, hs]
                    dkh = dkh + dki * sc
                    dex = dki * (kh * sc)
                    dc = dc - dex + jnp.where(row == i * HGRN_SUB, jnp.sum(dex, axis=0, keepdims=True), 0.0)
                dqt = _dot(dyb, _mx(s0))
                dkb = _dot(vb, _mx(ds1))
                dv = _dot(_mx(att), dyb, _TN) + _dot(_mx(kb), _mx(ds1), _NT)
                dst[b, :, hs] = c["e_l"][:, hs] * ds1 + _dot(dyb, _mx(qt), _TN)
                dqh = jnp.concatenate(dqh_rows, axis=0) + dqt * c["ec"][:, hs]
                dkh = dkh + dkb * c["elc"][:, hs]
                kbk = dkb * kb
                dlast = jnp.sum(kbk, axis=0, keepdims=True) + c["e_l"][:, hs] * jnp.sum(ds1 * s0, axis=0, keepdims=True)
                at_edge = jnp.where(row == c["edge"], dlast, 0.0)
                dc_l.append(dc + dqt * qt - kbk + at_edge)
                dbl_l.append(dqh * qh - dkh * kh + at_edge)
                dq = dqh * c["ebl"][:, hs] * HGRN_SCALE
                if nadd:
                    dq, dv = dq + adds[0][b, :, hs], dv + adds[1][b, :, hs]
                dq_ref[b, :, hs] = dq.astype(dq_ref.dtype)
                dv_ref[b, :, hs] = dv.astype(dv_ref.dtype)
                dk_l.append(dkh * c["enbl"][:, hs])
            dg = (_dot01(c["m_within"], jnp.concatenate(dbl_l, axis=1), _TN, split="b", terms=2)
                  + _dot01(c["m_before"], jnp.concatenate(dc_l, axis=1), _TN, split="b", terms=2))
            df, d0, d1 = pre_vjp((jnp.concatenate(dk_l, axis=1), dg))
            df_ref[b] = df.astype(df_ref.dtype)
            dl0_ref[...] += d0
            dl1_ref[...] += d1

    shp_sum = jax.ShapeDtypeStruct((nbatch, s, w), BF16 if nadd else F32)
    shp_vec = jax.ShapeDtypeStruct((1, w), F32)
    return _pcall(body, name=f"gla_bwd_r{int(reverse)}", grid=(nblk,),
                  in_specs=[col(0), col(1 + int(reverse)), col(3), vec, vec, st_spec, col(0)] + [col(0)] * nadd,
                  out_specs=(col(0), col(0), col(0), vec, vec),
                  out_shape=(shp_sum, jax.ShapeDtypeStruct((nbatch, s, w), BF16), shp_sum, shp_vec, shp_vec),
                  scratch_shapes=[pltpu.VMEM((nbatch, 128, w), F32)],
                  compiler_params=_params())(proj3, proj3, proj3, l0, l1, st4, do3, *(add_to or ()))


DIRS = (False, True)


def _block_diag(w):
    eye = jnp.eye(16, dtype=w.dtype)
    return (eye[:, None, :, None] * w[:, :, None, :]).reshape(1024, 1024)


def _diag_blocks(m):
    m4 = m.reshape(16, 64, 16, 64)
    return jnp.stack([m4[i, :, i, :] for i in range(16)], axis=0)


def _pad_lanes(v, n=128):
    return jnp.pad(v, [(0, 0)] * (v.ndim - 1) + [(0, n - v.shape[-1])])


def _mlp_fwd(tag, x, nw, w1, w2, carry=None):
    (h,) = _pw_fwd(f"{tag}_norm", _f_norm, [(x, 0)], [(nw, 0)], [BF16], 1024, 1)
    a, r, *got = _mm(f"{tag}_up", h, w1, "nn", relu2=True, carry=carry)
    return _mm(f"{tag}_down", r, w2, "nn", res=x), (h, a, r), got


def _mlp_bwd(tag, x, nw, w1, w2, saved, dxo, carry=None):
    h, a, r = saved
    dw2, *got = _mm(f"{tag}_dw2", r, dxo, "tn", carry=carry) if carry else (_mm(f"{tag}_dw2", r, dxo, "tn"),)
    da = _mm(f"{tag}_da", dxo, w2, "nt", relu2_of=a, out_dtype=BF16)
    dw1 = _mm(f"{tag}_dw1", h, da, "tn", col_shards=4)
    dx, dnw = _mm_sum_nt(f"{tag}_dh", [(da, k, 1024) for k in range(4)], [(w1, k) for k in range(4)], norm_bwd=(x, nw, dxo))
    return dx, dw1, dw2, dnw, got


def _split_in0(pieces, dt_piece):
    tm = 256

    def body(p0, p1, p2, p3, p4, p5, o_ref):
        full = jnp.concatenate([p0[...], p1[...], p2[...], p3[...], p4[...], p5[:, :32]], axis=1)
        for j in range(4):
            o_ref[j] = full[:, 1288 * j:1288 * (j + 1)]

    blk = pl.BlockSpec((tm, 1024), lambda i: (i, 0))
    return _pcall(body, name="split_in0", grid=(1024 // tm,), in_specs=[blk] * 5 + [pl.BlockSpec((tm, 128), lambda i: (i, 0))],
                  out_specs=pl.BlockSpec((4, tm, 1288), lambda i: (0, i, 0)),
                  out_shape=jax.ShapeDtypeStruct((4, 1024, 1288), F32), compiler_params=_params())(*pieces, dt_piece)


def _assemble_in0(shards):
    tm = 256

    def body(s_ref, m_ref, d_ref):
        full = jnp.concatenate([s_ref[j] for j in range(4)], axis=1)
        m_ref[...] = full[:, :5120]
        d_ref[...] = jnp.concatenate([full[:, 5120:5152], jnp.zeros((tm, 96), full.dtype)], axis=1)

    return _pcall(body, name="assemble_in0", grid=(1024 // tm,), in_specs=[pl.BlockSpec((4, tm, 1288), lambda i: (0, i, 0))],
                  out_specs=(pl.BlockSpec((tm, 5120), lambda i: (i, 0)), pl.BlockSpec((tm, 128), lambda i: (i, 0))),
                  out_shape=(jax.ShapeDtypeStruct((1024, 5120), shards.dtype), jax.ShapeDtypeStruct((1024, 128), shards.dtype)),
                  compiler_params=_params())(shards)


EARLY = ("odd_w_in", "odd_w_out", "mlp_w1_l1", "mlp_w2_l1")
MID = ("even_w_out", "mlp_w1_l0", "mlp_w2_l0")
LATE = ("even_w_in",)


def _local_step(x3, tgt3, w, w_main0, w_dt0, pair_reduce=None, late=None):
    nb, s, d = x3.shape
    carries, arrived = late if late else ({}, None)
    t = nb * s
    x0 = x3.reshape(t, d)
    tgt = tgt3.reshape(t, d)
    grads = {}
    row = lambda v: v.reshape(1, -1)
    to3 = lambda v: v.reshape(nb, s, v.shape[-1])
    to2 = lambda v: v.reshape(-1, v.shape[-1])

    conv_w, conv_b = w["even_conv_w"][0], row(w["even_conv_b"][0])
    nmix0 = row(w["norm_mix"][0])
    (h0,) = _pw_fwd("l0_norm", _f_norm, [(x0, 0)], [(nmix0, 0)], [BF16], 1024, 1)
    proj0 = _mm("l0_proj", h0, w_main0, "nn")
    dt_raw = _mm("l0_proj_dt", h0, w_dt0, "nn")
    conv2, xbc3 = _conv_fwd(to3(proj0), conv_w, conv_b, 0, 2, True)
    u_lru = to2(_conv_fwd(to3(proj0), conv_w, conv_b, 2, 1, False))
    xbc = to2(xbc3)
    dt_bias = _pad_lanes(w["ssd_dt_bias"][0].reshape(1, 32))
    (dt,) = _pw_fwd("l0_dt", _f_softplus, [(dt_raw, 0)], [(dt_bias, 0)], [F32], 128, 1)
    dt3 = to3(dt)
    alog = _pad_lanes(w["ssd_a_log"][0].reshape(1, 32))
    ssd = [_ssd_fwd(xbc3, dt3, alog, r, carry=carries.get(key)) for r, key in zip(DIRS, ("mlp_w1", "mlp_w2"))]
    if late:
        w = {**w, **arrived("mlp_w1", ssd[0][2:]), **arrived("mlp_w2", ssd[1][2:])}
    yf, yb = to2(ssd[0][0]), to2(ssd[1][0])
    dskip = jnp.repeat(w["ssd_d"][0], SSD_HEADDIM).reshape(1, 1024)
    snw = row(w["ssd_norm_w"][0])
    ssd_ins = [(yf, 0), (yb, 0), (xbc, 0), (proj0, 3)]
    (ya,) = _pw_fwd("l0_ssd_post", _f_ssd_post, ssd_ins, [(dskip, 0), (snw, 0)], [BF16], 1024, 1, groups=SSD_GROUPS)
    w_gates = [_block_diag(w[k][0, r]).astype(MXU_DTYPE) for r in range(2) for k in ("lru_w_a", "lru_w_x")]
    pre = [_mm(f"l0_lru_pre{i}", u_lru, wg, "nn") for i, wg in enumerate(w_gates)]
    lru_par = [[(row(w[k][0, r]), 0) for k in ("lru_b_a", "lru_b_x", "lru_lambda")] for r in range(2)]
    lru_ins = [[(pre[2 * r], 0), (pre[2 * r + 1], 0), (u_lru, 0)] for r in range(2)]
    ab = [_pw_fwd(f"l0_lru_gates{r}", _f_lru_gates, lru_ins[r], lru_par[r], [F32, F32], 1024, 1) for r in range(2)]
    hs = [_lru_scan(to3(ab[r][0]), to3(ab[r][1]), DIRS[r]) for r in range(2)]
    lru_post_ins = [(to2(hs[0]), 0), (to2(hs[1]), 0), (proj0, 4)]
    (ybm,) = _pw_fwd("l0_lru_post", _f_lru_post, lru_post_ins, [], [BF16], 1024, 1)
    w_out0 = w["even_w_out"][0]
    x1 = _mm("l0_out_a", ya, w_out0[:1024], "nn", res=x0)
    x1 = _mm("l0_out_b", ybm, w_out0[1024:], "nn", res=x1)
    nmlp0 = row(w["norm_mlp"][0])
    x2, mlp0, got = _mlp_fwd("l0_mlp", x1, nmlp0, w["mlp_w1"][0], w["mlp_w2"][0], carry=carries.get("odd"))
    if late:
        w = {**w, **arrived("odd", got)}

    w_in1 = w["odd_w_in"][0]
    nmix1 = row(w["norm_mix"][1])
    (h1,) = _pw_fwd("l1_norm", _f_norm, [(x2, 0)], [(nmix1, 0)], [BF16], 1024, 1)
    proj1 = _mm("l1_proj", h1, w_in1, "nn")
    proj1_3 = to3(proj1)
    lb0, lb1 = row(w["hgrn_lb_logits"][0]), row(w["hgrn_lb_logits"][1])
    gla = [_gla_fwd(proj1_3, lb0, lb1, r) for r in DIRS]
    hnw = row(w["hgrn_norm_w"][0])
    hpost_ins = [(to2(gla[0][0]), 0), (to2(gla[1][0]), 0), (proj1, 4)]
    (yo,) = _pw_fwd("l1_hgrn_post", _f_hgrn_post, hpost_ins, [(hnw, 0)], [BF16], 1024, 1, groups=HGRN_HEADS)
    w_out1 = w["odd_w_out"][0]
    x3_ = _mm("l1_out", yo, w_out1, "nn", res=x2)
    nmlp1 = row(w["norm_mlp"][1])
    x4, mlp1, _ = _mlp_fwd("l1_mlp", x3_, nmlp1, w["mlp_w1"][1], w["mlp_w2"][1])

    dx4, dnf, loss = _loss_head(x4, tgt, row(w["norm_final"]))
    grads["norm_final"] = dnf.reshape(-1)

    dx3, dw1_1, dw2_1, dnmlp1, _ = _mlp_bwd("l1_mlp", x3_, nmlp1, w["mlp_w1"][1], w["mlp_w2"][1], mlp1, dx4)
    big = {"odd_w_out": _mm("l1_dwout", yo, dx3, "tn").reshape(4, 256, 1024)}
    dyo = _mm("l1_dyo", dx3, w_out1, "nt")
    (do, dgate1), (dhnw,) = _pw_bwd("l1_hgrn_post_b", _f_hgrn_post, hpost_ins, [(hnw, 0)], [dyo], 1024, 1, [0, 2],
                                    out_dtypes=[F32, BF16], groups=HGRN_HEADS, tm=ROWS_FWD)
    grads["hgrn_norm_w"] = dhnw
    do3 = to3(do)
    gb = [_gla_bwd(proj1_3, lb0, lb1, gla[0][1], do3, False)]
    gb.append(_gla_bwd(proj1_3, lb0, lb1, gla[1][1], do3, True, add_to=(gb[0][0], gb[0][2])))
    grads["hgrn_lb_logits"] = jnp.concatenate([gb[0][3] + gb[1][3], gb[0][4] + gb[1][4]], axis=0)
    dparts1 = [to2(gb[1][0]), to2(gb[0][1]), to2(gb[1][1]), to2(gb[1][2]), dgate1]
    dwin1 = jnp.concatenate([_mm(f"l1_dwin{i}", h1, dp, "tn") for i, dp in enumerate(dparts1)], axis=1)
    big["odd_w_in"] = dwin1.reshape(1024, 4, 1280).transpose(1, 0, 2)
    dx2, dnmix1 = _mm_sum_nt("l1_dh", dparts1, [(w_in1, i) for i in range(5)], norm_bwd=(x2, nmix1, dx3))
    big["mlp_w1_l1"], big["mlp_w2_l1"] = dw1_1, dw2_1.reshape(4, 1024, 1024)
    box = {}

    def mlp0_bwd(carry=None):
        box["mlp0"] = _mlp_bwd("l0_mlp", x1, nmlp0, w["mlp_w1"][0], w["mlp_w2"][0], mlp0, dx2, carry=carry)
        return box["mlp0"][4]

    early_sums = tuple(pair_reduce(EARLY, [big[n] for n in EARLY], mlp0_bwd)) if pair_reduce else tuple(mlp0_bwd())

    dx1, dw1_0, dw2_0, dnmlp0 = box["mlp0"][:4]
    big["mlp_w1_l0"], big["mlp_w2_l0"] = dw1_0, dw2_0.reshape(4, 1024, 1024)
    grads["norm_mlp"] = jnp.concatenate([dnmlp0, dnmlp1], axis=0)
    big["even_w_out"] = jnp.concatenate([_mm("l0_dwout_a", ya, dx1, "tn"), _mm("l0_dwout_b", ybm, dx1, "tn")],
                                        axis=0).reshape(4, 512, 1024)
    dya = _mm("l0_dya", dx1, w_out0[:1024], "nt")
    dyb = _mm("l0_dyb", dx1, w_out0[1024:], "nt")
    (dh, dgate0), _ = _pw_bwd("l0_lru_post_b", _f_lru_post, lru_post_ins, [], [dyb], 1024, 1, [0, 2], out_dtypes=[F32, BF16],
                               tm=ROWS_FWD)
    dh3 = to3(dh)

    def lru0_bwd(carry=None):
        box["lru0"] = _lru_scan_bwd(to3(ab[0][0]), hs[0], dh3, DIRS[0], carry=carry)
        return box["lru0"][2:]

    mid_sums = tuple(pair_reduce(MID, [big[n] for n in MID], lru0_bwd)) if pair_reduce else tuple(lru0_bwd())
    dpre, du_parts, dlru = [], [], {k: [] for k in ("lru_b_a", "lru_b_x", "lru_lambda")}
    for r in range(2):
        g_r, da_r = box["lru0"][:2] if r == 0 else _lru_scan_bwd(to3(ab[r][0]), hs[r], dh3, DIRS[r])
        (dpa, dpx, du_r), (dba, dbx, dlam) = _pw_bwd(f"l0_lru_gates_b{r}", _f_lru_gates, lru_ins[r], lru_par[r],
                                                     [to2(da_r), to2(g_r)], 1024, 1, [0, 1, 2],
                                                     out_dtypes=[BF16, BF16, F32])
        dpre += [dpa, dpx]
        du_parts.append(du_r)
        dlru["lru_b_a"].append(dba)
        dlru["lru_b_x"].append(dbx)
        dlru["lru_lambda"].append(dlam)
    for k, v in dlru.items():
        grads[k] = jnp.concatenate(v, axis=0)[None]
    dwg = [_diag_blocks(_mm(f"l0_dwgate{i}", u_lru, dp, "tn")) for i, dp in enumerate(dpre)]
    grads["lru_w_a"] = jnp.stack([dwg[0], dwg[2]])[None]
    grads["lru_w_x"] = jnp.stack([dwg[1], dwg[3]])[None]
    du = _mm_sum_nt("l0_du", dpre, [(wg, 0) for wg in w_gates], add=du_parts)
    (dy, dxs_skip, dz), (ddskip, dsnw) = _pw_bwd("l0_ssd_post_b", _f_ssd_post, ssd_ins, [(dskip, 0), (snw, 0)], [dya],
                                                 1024, 1, [0, 2, 3], out_dtypes=[F32, F32, BF16], groups=SSD_GROUPS)
    grads["ssd_d"] = ddskip.reshape(SSD_HEADS, SSD_HEADDIM).sum(axis=1)[None]
    grads["ssd_norm_w"] = dsnw
    dy3 = to3(dy)
    sb0 = _ssd_bwd(xbc3, dt3, alog, ssd[0][1], dy3, False, scatter=early_sums)
    sb1 = _ssd_bwd(xbc3, dt3, alog, ssd[1][1], dy3, True, add_to=(sb0[0], to3(dxs_skip), sb0[1], sb0[2]), scatter=mid_sums)
    grads["ssd_a_log"] = (sb0[3] + sb1[3])[:, :32].reshape(1, 2, 16)
    ddt = to2(sb1[2])
    (ddt_raw,), (ddtb,) = _pw_bwd("l0_dt_b", _f_softplus, [(dt_raw, 0)], [(dt_bias, 0)], [ddt], 128, 1, [0])
    grads["ssd_dt_bias"] = ddtb[:, :32].reshape(1, 2, 16)
    cb = [_conv_bwd(sb1[0], to3(proj0), conv_w, 0, conv2), _conv_bwd(sb1[1], to3(proj0), conv_w, 1, conv2),
          _conv_bwd(to3(du), to3(proj0), conv_w, 2)]
    dcw = jnp.concatenate([c_[1] for c_ in cb], axis=1)
    grads["even_conv_w"] = dcw[:4][None]
    grads["even_conv_b"] = dcw[4:5]
    dparts0 = [to2(c_[0]) for c_ in cb] + [dz, dgate0]
    dwin0 = [_mm(f"l0_dwin{i}", h0, dp, "tn") for i, dp in enumerate(dparts0)]
    big["even_w_in"] = _split_in0(dwin0, _mm("l0_dwin_dt", h0, ddt_raw, "tn"))
    dx0, dnmix0 = _mm_sum_nt("l0_dh", dparts0 + [ddt_raw], [(w_main0, i) for i in range(5)] + [(w_dt0, 0)],
                             norm_bwd=(x0, nmix0, dx1))
    grads["norm_mix"] = jnp.concatenate([dnmix0, dnmix1], axis=0)
    return loss, dx0.reshape(nb, s, d), grads, big, (early_sums + mid_sums, sb0[4:] + sb1[4:])


ANY = pl.BlockSpec(memory_space=pl.ANY)


def _place():
    return lax.axis_index("x"), lax.axis_index("y"), lax.axis_index("c")


def _remote(src, dst, send_sems, recv_sems, k, to):
    return pltpu.make_async_remote_copy(src_ref=src, dst_ref=dst, send_sem=send_sems.at[k], recv_sem=recv_sems.at[k],
                                        device_id=to, device_id_type=MESH)


def _gather_start(x_refs, out_refs, send_sems, recv_sems, finish=False):
    n = len(x_refs)
    halves = [r.shape[0] // 2 for r in x_refs]
    x, y, c = _place()
    sibling = (x, y, 1 - c)
    chips = [(1 - x, y), (x, 1 - y), (1 - x, 1 - y)]

    def blk(t, px, py, hc):
        return out_refs[t].at[2 * px + py, pl.ds(hc * halves[t], halves[t]), :]

    def src(t):
        return x_refs[t].at[pl.ds(c * halves[t], halves[t]), :]

    first = [_remote(src(t), blk(t, x, y, c), send_sems, recv_sems, 6 * t + j, (*chip, c))
             for t in range(n) for j, chip in enumerate(chips)]
    if not finish:
        for cp in first:
            cp.start()
        return
    passed = []
    for t in range(n):
        for j, chip in enumerate(chips):
            _remote(src(t), blk(t, *chip, c), send_sems, recv_sems, 6 * t + j, (*chip, c)).wait_recv()
            cp = _remote(blk(t, *chip, c), blk(t, *chip, c), send_sems, recv_sems, 6 * t + 3 + j, sibling)
            cp.start()
            passed.append(cp)
    for t in range(n):
        for j, chip in enumerate(chips):
            _remote(src(t), blk(t, *chip, 1 - c), send_sems, recv_sems, 6 * t + 3 + j, sibling).wait_recv()
    for cp in first + passed:
        cp.wait_send()


_gather_finish = functools.partial(_gather_start, finish=True)


def _gather_carry(shards):
    n = len(shards)
    return (list(shards), [jax.ShapeDtypeStruct((4,) + s.shape, s.dtype) for s in shards],
            [pltpu.SemaphoreType.DMA((6 * n,)), pltpu.SemaphoreType.DMA((6 * n,))], _gather_start, _gather_finish)


def _gather_chips(shards):
    n = len(shards)
    srcs, shapes, scratch, start, finish = _gather_carry(shards)

    def body(*refs):
        start(refs[:n], refs[n:2 * n], *refs[2 * n:])
        finish(refs[:n], refs[n:2 * n], *refs[2 * n:])

    return _pcall(body, name="gather_weights", in_specs=[ANY] * n, out_specs=(ANY,) * n, out_shape=tuple(shapes),
                  scratch_shapes=scratch, compiler_params=_params())(*shards)


def _pair_swap_start(g_refs, land_refs, send_sems, recv_sems, finish=False):
    x, y, c = _place()
    cps = []
    for t, g in enumerate(g_refs):
        half = g.shape[1] // 2
        cps += [_remote(g.at[j, pl.ds((1 - c) * half, half), :], land_refs[t].at[j], send_sems, recv_sems, 4 * t + j,
                        (x, y, 1 - c)) for j in range(4)]
    for cp in cps:
        cp.wait() if finish else cp.start()


_pair_swap_finish = functools.partial(_pair_swap_start, finish=True)


def _pair_swap_carry(gps):
    n = len(gps)
    return (list(gps), [jax.ShapeDtypeStruct((4, g.shape[1] // 2, g.shape[2]), F32) for g in gps],
            [pltpu.SemaphoreType.DMA((4 * n,)), pltpu.SemaphoreType.DMA((4 * n,))], _pair_swap_start, _pair_swap_finish)


def _pair_swap(name, gps):
    n = len(gps)
    srcs, shapes, scratch, start, finish = _pair_swap_carry(gps)

    def body(*refs):
        start(refs[:n], refs[n:2 * n], *refs[2 * n:])
        finish(refs[:n], refs[n:2 * n], *refs[2 * n:])

    return _pcall(body, name=f"pair_swap_{name}", in_specs=[ANY] * n, out_specs=(ANY,) * n, out_shape=tuple(shapes),
                  scratch_shapes=scratch, compiler_params=_params())(*gps)


def _pair_add(name, gp, land, cidx):
    _, half, cols = land.shape
    tr = _tile(half, 512)
    nh = half // tr

    def body(c_ref, g_ref, l_ref, o_ref):
        o_ref[...] = (g_ref[...] + l_ref[...]).astype(o_ref.dtype)

    grid_spec = pltpu.PrefetchScalarGridSpec(
        num_scalar_prefetch=1, grid=(4, nh),
        in_specs=[pl.BlockSpec((None, tr, cols), lambda j, i, c: (j, c[0] * nh + i, 0)),
                  pl.BlockSpec((None, tr, cols), lambda j, i, c: (j, i, 0))],
        out_specs=pl.BlockSpec((None, tr, cols), lambda j, i, c: (j, i, 0)))
    return _pcall(body, name=f"pair_add_{name}", grid_spec=grid_spec, out_shape=jax.ShapeDtypeStruct((4, half, cols), BF16),
                  compiler_params=_params())(cidx, gp, land)


def _scatter_copies(s_refs, land_refs, send_sems, recv_sems):
    x, y, c = _place()
    me = 2 * x + y
    chips = [(1 - x, y), (x, 1 - y), (1 - x, 1 - y)]
    pairs = [(t, j, px, py) for t in range(len(s_refs)) for j, (px, py) in enumerate(chips)]
    sends = [_remote(s_refs[t].at[2 * px + py], land_refs[t].at[me], send_sems, recv_sems, 3 * t + j, (px, py, c))
             for t, j, px, py in pairs]
    arrivals = [_remote(s_refs[t].at[me], land_refs[t].at[2 * px + py], send_sems, recv_sems, 3 * t + j, (px, py, c))
                for t, j, px, py in pairs]
    return sends, arrivals


def _scatter_scratch(n):
    return [pltpu.SemaphoreType.DMA((3 * n,)), pltpu.SemaphoreType.DMA((3 * n,))]


def _chip_scatter(name, css):
    n = len(css)

    def body(*refs):
        sends, arrivals = _scatter_copies(refs[:n], refs[n:2 * n], *refs[2 * n:])
        for cp in sends:
            cp.start()
        for cp in arrivals:
            cp.wait_recv()
        for cp in sends:
            cp.wait_send()

    return _pcall(body, name=f"chip_scatter_{name}", in_specs=[ANY] * n, out_specs=(ANY,) * n,
                  out_shape=tuple(jax.ShapeDtypeStruct(s.shape, s.dtype) for s in css),
                  scratch_shapes=_scatter_scratch(n), compiler_params=_params())(*css)


def _chip_sum(name, land):
    _, half, cols = land.shape
    tr = _tile(half, 512)

    def body(l_ref, o_ref):
        o_ref[...] = ((l_ref[0].astype(F32) + l_ref[1].astype(F32)) + l_ref[2].astype(F32)) + l_ref[3].astype(F32)

    return _pcall(body, name=f"chip_sum_{name}", grid=(half // tr,),
                  in_specs=[pl.BlockSpec((4, tr, cols), lambda i: (0, i, 0))],
                  out_specs=pl.BlockSpec((tr, cols), lambda i: (i, 0)),
                  out_shape=jax.ShapeDtypeStruct((half, cols), F32), compiler_params=_params())(land)


def _pair_join(reds):
    n = len(reds)

    def body(*refs):
        r_refs, out_refs = refs[:n], refs[n:2 * n]
        send_sems, recv_sems = refs[2 * n:]
        x, y, c = _place()
        cps = [_remote(r_refs[t], out_refs[t].at[c], send_sems, recv_sems, t, (x, y, 1 - c)) for t in range(n)]
        for cp in cps:
            cp.start()
        for t in range(n):
            _remote(r_refs[t], out_refs[t].at[1 - c], send_sems, recv_sems, t, (x, y, 1 - c)).wait_recv()
        for cp in cps:
            cp.wait_send()

    return _pcall(body, name="grad_pair_join", in_specs=[ANY] * n, out_specs=(ANY,) * n,
                  out_shape=tuple(jax.ShapeDtypeStruct((2,) + r.shape, F32) for r in reds),
                  scratch_shapes=[pltpu.SemaphoreType.DMA((n,)), pltpu.SemaphoreType.DMA((n,))],
                  compiler_params=_params())(*reds)


def _adamw(name, g, w, m, v):
    rows, cols = g.shape
    tr = _tile(rows, 512)

    def body(g_ref, w_ref, m_ref, v_ref, d_ref, mo_ref, vo_ref):
        gv = g_ref[...]
        mn = ADAM_B1 * m_ref[...] + (1.0 - ADAM_B1) * gv
        vn = ADAM_B2 * v_ref[...] + (1.0 - ADAM_B2) * jnp.square(gv)
        m_hat = mn / (1.0 - ADAM_B1 ** ADAM_STEP)
        v_hat = vn / (1.0 - ADAM_B2 ** ADAM_STEP)
        d_ref[...] = -ADAM_LR * (m_hat / (jnp.sqrt(v_hat) + ADAM_EPS) + ADAM_WD * w_ref[...])
        mo_ref[...] = mn
        vo_ref[...] = vn

    blk = pl.BlockSpec((tr, cols), lambda i: (i, 0))
    shp = jax.ShapeDtypeStruct((rows, cols), F32)
    return _pcall(body, name=f"adamw_{name}", grid=(rows // tr,), in_specs=[blk] * 4, out_specs=(blk,) * 3,
                  out_shape=(shp,) * 3, compiler_params=_params())(g, w, m, v)


def _pack(pieces, rows, dtype):
    flat = jnp.concatenate([p.reshape(-1).astype(dtype) for p in pieces])
    return jnp.pad(flat, (0, rows * PACK_COLS - flat.shape[0])).reshape(rows, PACK_COLS)


def _unpack(pack, shapes):
    flat = pack.reshape(-1)
    out, off = [], 0
    for shp in shapes:
        n = math.prod(shp)
        out.append(flat[off:off + n].reshape(shp))
        off += n
    return out


def _shard_of(full, axis, j):
    n = full.shape[axis] // 4
    return lax.slice_in_dim(full, j * n, (j + 1) * n, axis=axis)


def kernel(x, even_w_in, even_conv_w, even_conv_b, ssd_a_log, ssd_dt_bias, ssd_d, ssd_norm_w, lru_w_a, lru_b_a, lru_w_x, lru_b_x, lru_lambda, even_w_out, odd_w_in, hgrn_lb_logits, hgrn_norm_w, odd_w_out, norm_mix, norm_mlp, mlp_w1, mlp_w2, norm_final, loss_target, m_even_w_in, m_even_conv_w, m_even_conv_b, m_ssd_a_log, m_ssd_dt_bias, m_ssd_d, m_ssd_norm_w, m_lru_w_a, m_lru_b_a, m_lru_w_x, m_lru_b_x, m_lru_lambda, m_even_w_out, m_odd_w_in, m_hgrn_lb_logits, m_hgrn_norm_w, m_odd_w_out, m_norm_mix, m_norm_mlp, m_mlp_w1, m_mlp_w2, m_norm_final, v_even_w_in, v_even_conv_w, v_even_conv_b, v_ssd_a_log, v_ssd_dt_bias, v_ssd_d, v_ssd_norm_w, v_lru_w_a, v_lru_b_a, v_lru_w_x, v_lru_b_x, v_lru_lambda, v_even_w_out, v_odd_w_in, v_hgrn_lb_logits, v_hgrn_norm_w, v_odd_w_out, v_norm_mix, v_norm_mlp, v_mlp_w1, v_mlp_w2, v_norm_final):
    names = [n for n, _, _, _ in WEIGHTS]
    w_loc = dict(zip(names, (even_w_in, even_conv_w, even_conv_b, ssd_a_log, ssd_dt_bias, ssd_d, ssd_norm_w, lru_w_a, lru_b_a, lru_w_x, lru_b_x, lru_lambda, even_w_out, odd_w_in, hgrn_lb_logits, hgrn_norm_w, odd_w_out, norm_mix, norm_mlp, mlp_w1, mlp_w2, norm_final)))
    m_loc = dict(zip(names, (m_even_w_in, m_even_conv_w, m_even_conv_b, m_ssd_a_log, m_ssd_dt_bias, m_ssd_d, m_ssd_norm_w, m_lru_w_a, m_lru_b_a, m_lru_w_x, m_lru_b_x, m_lru_lambda, m_even_w_out, m_odd_w_in, m_hgrn_lb_logits, m_hgrn_norm_w, m_odd_w_out, m_norm_mix, m_norm_mlp, m_mlp_w1, m_mlp_w2, m_norm_final)))
    v_loc = dict(zip(names, (v_even_w_in, v_even_conv_w, v_even_conv_b, v_ssd_a_log, v_ssd_dt_bias, v_ssd_d, v_ssd_norm_w, v_lru_w_a, v_lru_b_a, v_lru_w_x, v_lru_b_x, v_lru_lambda, v_even_w_out, v_odd_w_in, v_hgrn_lb_logits, v_hgrn_norm_w, v_odd_w_out, v_norm_mix, v_norm_mlp, v_mlp_w1, v_mlp_w2, v_norm_final)))
    spec = {n: (blk, full, ax) for n, blk, full, ax in WEIGHTS}

    small = [n for n in names if n not in BIG]
    two_d = lambda n, v: v.reshape(BIG_2D[n])

    me = 2 * lax.axis_index("x") + lax.axis_index("y")
    cc = lax.axis_index("c")
    put = lambda whole, part, k: lax.dynamic_update_slice_in_dim(whole, part[None], k, axis=0)
    own = {n: two_d(n, w_loc[n]).astype(BF16) for n in BIG}
    own["small"] = _pack([w_loc[n] for n in SMALL_SHARDED], 16, F32)
    fill = lambda got, keys: [put(g, own[k], me) for g, k in zip(got, keys)]
    first = ("even_w_in", "small")
    g_in0, g_small = fill(_gather_chips([own[k] for k in first]), first)
    w_main0, w_dt0 = _assemble_in0(g_in0)
    w_full = {n: w_loc[n] for n in names if spec[n][2] is None}
    shards = [_unpack(g_small[j], [spec[n][0] for n in SMALL_SHARDED]) for j in range(4)]
    for n in ("mlp_w1", "mlp_w2"):
        for l in range(2):
            own[f"{n}_l{l}"] = w_loc[n][l].astype(BF16)
    layers = lambda n: (f"{n}_l0", f"{n}_l1")
    carries = {"mlp_w1": _gather_carry([own[k] for k in layers("mlp_w1") + ("even_w_out",)]),
               "mlp_w2": _gather_carry([own[k] for k in layers("mlp_w2")]),
               "odd": _gather_carry([own["odd_w_in"], own["odd_w_out"]])}

    def arrived(key, got):
        if key == "odd":
            g_in1, g_out1 = fill(got, ("odd_w_in", "odd_w_out"))
            return {"odd_w_in": jnp.concatenate([g_in1[j] for j in range(4)], axis=1)[None],
                    "odd_w_out": g_out1.reshape(1, 1024, 1024)}
        g = fill(got[:2], layers(key))
        if key == "mlp_w2":
            return {key: [v.reshape(4096, 1024) for v in g]}
        (g_out0,) = fill(got[2:], ("even_w_out",))
        return {key: g, "even_w_out": g_out0.reshape(1, 2048, 1024)}

    for i, n in enumerate(SMALL_SHARDED):
        w_full[n] = jnp.concatenate([shards[j][i] for j in range(4)], axis=spec[n][2])

    cidx = cc.astype(jnp.int32).reshape(1)

    def pair_reduce(tags, tensors, run=None):
        lands = run(_pair_swap_carry(tensors)) if run else _pair_swap(tags[0], tensors)
        return [_pair_add(tag, g, land, cidx) for tag, g, land in zip(tags, tensors, lands)]

    loss_vec, grad_x, grads, big, (early_sums, early_landed) = _local_step(
        x, loss_target, w_full, w_main0, w_dt0, pair_reduce, (carries, arrived))
    loss = lax.psum(loss_vec[0, 0], ("x", "y", "c"))

    def dest_pack(j):
        return _pack([grads[n].reshape(spec[n][1]) if spec[n][2] is None else _shard_of(grads[n].reshape(spec[n][1]), spec[n][2], j)
                      for n in small], SMALL_ROWS, F32)

    late_tags = LATE + ("small",)
    late_sums = pair_reduce(late_tags, [big[n] for n in LATE] + [jnp.stack([dest_pack(j) for j in range(4)])])
    tags = EARLY + MID + late_tags
    chip_sums = list(early_sums) + late_sums
    landed = [put(land, lax.dynamic_index_in_dim(cs, me, axis=0, keepdims=False), me)
              for land, cs in zip(list(early_landed) + list(_chip_scatter("late", late_sums)), chip_sums)]
    halves = [_chip_sum(tag, land) for tag, land in zip(tags, landed)]
    red = {tag: put(r, h, cc).reshape(-1, r.shape[-1]) for tag, r, h in zip(tags, _pair_join(halves), halves)}
    for n in ("mlp_w1", "mlp_w2"):
        red[n] = jnp.concatenate([red[n + "_l0"], red[n + "_l1"]], axis=0)

    outs = {}
    for n, g in ((n, red[n]) for n in BIG):
        res = (g, *_adamw(n, g, two_d(n, w_loc[n]), two_d(n, m_loc[n]), two_d(n, v_loc[n])))
        outs[n] = [r.reshape(spec[n][0]) for r in res]
    blocks = [spec[n][0] for n in small]
    wp, mp, vp = (_pack([src[n] for n in small], SMALL_ROWS, F32) for src in (w_loc, m_loc, v_loc))
    res = (red["small"], *_adamw("small", red["small"], wp, mp, vp))
    unpacked = [_unpack(r, blocks) for r in res]
    for i, n in enumerate(small):
        outs[n] = [u[i] for u in unpacked]
    return (loss, grad_x, *[outs[n][k] for k in range(4) for n in names])
```

```python
import functools
import math

import jax
import jax.numpy as jnp
from jax import lax
from jax.experimental import pallas as pl
from jax.experimental.pallas import tpu as pltpu

F32 = jnp.float32
BF16 = jnp.bfloat16
MXU_DTYPE = jnp.bfloat16
MESH = pl.DeviceIdType.MESH

D_MODEL = 1024
EPS = 1e-6
SSD_HEADS = 16
SSD_HEADDIM = 64
HEAD_SHIFT = 6
SSD_GROUPS = 4
SSD_STATE = 128
SSD_CHUNK = 128
LRU_C = 8.0
LRU_ROWS = 256
HGRN_HEADS = 8
HGRN_HEADDIM = 128
HGRN_SUB = 32
HGRN_SUB_SHIFT = 5
HGRN_BLOCK = 128
HGRN_SCALE = HGRN_HEADDIM ** -0.5
CONV_ROWS = 512
ROWS_FWD = 512
ROWS_BWD = 256

ADAM_LR = 0.001
ADAM_B1 = 0.9
ADAM_B2 = 0.999
ADAM_EPS = 1e-08
ADAM_WD = 0.01
ADAM_STEP = 10

VMEM_LIMIT = 56 * 1024 * 1024
PACK_COLS = 1024
SMALL_ROWS = 288

WEIGHTS = (
    ("even_w_in", (1, 1024, 1288), (1, 1024, 5152), 2),
    ("even_conv_w", (1, 4, 768), (1, 4, 3072), 2),
    ("even_conv_b", (1, 3072), (1, 3072), None),
    ("ssd_a_log", (1, 2, 16), (1, 2, 16), None),
    ("ssd_dt_bias", (1, 2, 16), (1, 2, 16), None),
    ("ssd_d", (1, 16), (1, 16), None),
    ("ssd_norm_w", (1, 1024), (1, 1024), None),
    ("lru_w_a", (1, 2, 16, 64, 64), (1, 2, 16, 64, 64), None),
    ("lru_b_a", (1, 2, 256), (1, 2, 1024), 2),
    ("lru_w_x", (1, 2, 16, 64, 64), (1, 2, 16, 64, 64), None),
    ("lru_b_x", (1, 2, 256), (1, 2, 1024), 2),
    ("lru_lambda", (1, 2, 256), (1, 2, 1024), 2),
    ("even_w_out", (1, 512, 1024), (1, 2048, 1024), 1),
    ("odd_w_in", (1, 1024, 1280), (1, 1024, 5120), 2),
    ("hgrn_lb_logits", (2, 1024), (2, 1024), None),
    ("hgrn_norm_w", (1, 256), (1, 1024), 1),
    ("odd_w_out", (1, 256, 1024), (1, 1024, 1024), 1),
    ("norm_mix", (2, 1024), (2, 1024), None),
    ("norm_mlp", (2, 1024), (2, 1024), None),
    ("mlp_w1", (2, 1024, 1024), (2, 1024, 4096), 2),
    ("mlp_w2", (2, 1024, 1024), (2, 4096, 1024), 1),
    ("norm_final", (1024,), (1024,), None),
)
BIG = ("even_w_in", "even_w_out", "odd_w_in", "odd_w_out", "mlp_w1", "mlp_w2")
BIG_2D = {"even_w_in": (1024, 1288), "even_w_out": (512, 1024), "odd_w_in": (1024, 1280), "odd_w_out": (256, 1024),
          "mlp_w1": (2048, 1024), "mlp_w2": (2048, 1024)}
SMALL_SHARDED = ("even_conv_w", "lru_b_a", "lru_b_x", "lru_lambda", "hgrn_norm_w")


def _pcall(body, carry=None, **kw):
    if carry is not None:
        srcs, shapes, scratch, start, finish = carry
        grid, inner = kw["grid"], body
        as_tuple = lambda v: tuple(v) if isinstance(v, (tuple, list)) else (v,)
        out_specs, out_shape, own_scratch = as_tuple(kw["out_specs"]), as_tuple(kw["out_shape"]), list(kw.get("scratch_shapes", ()))
        a = len(kw["in_specs"])
        b = a + len(srcs)
        c = b + len(out_specs)
        d = c + len(shapes)
        e = d + len(own_scratch)

        def body(*refs):
            ids = [pl.program_id(ax) for ax in range(len(grid))]
            first = functools.reduce(jnp.logical_and, [i == 0 for i in ids])
            last = functools.reduce(jnp.logical_and, [i == g - 1 for i, g in zip(ids, grid)])
            pl.when(first)(lambda: start(refs[a:b], refs[c:d], *refs[e:]))
            inner(*refs[:a], *refs[b:c], *refs[d:e])
            pl.when(last)(lambda: finish(refs[a:b], refs[c:d], *refs[e:]))

        kw = dict(kw, in_specs=list(kw["in_specs"]) + [ANY] * len(srcs), out_specs=out_specs + (ANY,) * len(shapes),
                  out_shape=out_shape + tuple(shapes), scratch_shapes=own_scratch + list(scratch))
    return pl.pallas_call(body, **kw)


def _params(**kw):
    return pltpu.CompilerParams(vmem_limit_bytes=VMEM_LIMIT, **kw)


def _tile(n, pref):
    if n <= pref:
        return n
    t = (pref // 128) * 128
    while n % t:
        t -= 128
    return t


def _dot(a, b, dims=(((1,), (0,)), ((), ()))):
    return lax.dot_general(a, b, dims, preferred_element_type=F32)


_NN = (((1,), (0,)), ((), ()))
_NT = (((1,), (1,)), ((), ()))
_TN = (((0,), (0,)), ((), ()))


def _mx(v):
    return v.astype(MXU_DTYPE)


def _dot01(a, b, dims=_NN, *, split, terms):
    acc, rest = None, (a if split == "a" else b)
    for _ in range(terms):
        piece = _mx(rest)
        part = _dot(piece, _mx(b), dims) if split == "a" else _dot(_mx(a), piece, dims)
        acc = part if acc is None else acc + part
        rest = rest - piece.astype(F32)
    return acc


def _mm(name, a, b, mode, *, out_dtype=F32, res=None, relu2=False, relu2_of=None, col_shards=1, carry=None):
    shards = b.shape[0] if b.ndim == 3 else 0
    b2 = b.shape[1:] if shards else b.shape
    if mode == "nn":
        (m, kk), n = a.shape, b2[1] * max(shards, 1)
    elif mode == "nt":
        (m, kk), n = a.shape, b2[0]
    else:
        (kk, m), (_, n) = a.shape, b.shape
    assert res is None or relu2_of is None
    tk_pref = 1024
    if mode == "tn" and a.dtype.itemsize == 2 and b.dtype.itemsize == 2:
        tk_pref = 2048
    tm, tn, tk = _tile(m, 1024), _tile(n // col_shards, 1024), _tile(kk, tk_pref)
    nk = kk // tk
    dims = {"nn": _NN, "nt": _NT, "tn": _TN}[mode]
    a_spec = pl.BlockSpec((tk, tm), lambda i, j, k: (k, i)) if mode == "tn" else pl.BlockSpec((tm, tk), lambda i, j, k: (i, k))
    b_spec = pl.BlockSpec((tn, tk), lambda i, j, k: (j, k)) if mode == "nt" else pl.BlockSpec((tk, tn), lambda i, j, k: (k, j))
    if shards and mode == "nn":
        assert tn == b2[1]
        b_spec = pl.BlockSpec((None, tk, tn), lambda i, j, k: (j, k, 0))
    o_spec = pl.BlockSpec((tm, tn), lambda i, j, k: (i, j))
    o_shape = (m, n)
    if col_shards > 1:
        assert tn * col_shards == n and res is None and not relu2
        o_spec = pl.BlockSpec((None, tm, tn), lambda i, j, k: (j, i, 0))
        o_shape = (col_shards, m, tn)
    extra = res if res is not None else relu2_of
    has_res = extra is not None

    def body(*refs):
        a_ref, b_ref = refs[0], refs[1]
        res_ref = refs[2] if has_res else None
        outs = refs[2 + has_res:2 + has_res + 1 + relu2]

        def finish(r):
            if res is not None:
                r = r + res_ref[...]
            if relu2_of is not None:
                r = r * (2.0 * jnp.maximum(res_ref[...].astype(F32), 0.0))
            if relu2:
                outs[0][...] = r.astype(outs[0].dtype)
                outs[1][...] = jnp.square(jnp.maximum(r, 0.0)).astype(outs[1].dtype)
            else:
                outs[0][...] = r.astype(outs[0].dtype)

        prod = _dot(_mx(a_ref[...]), _mx(b_ref[...]), dims)
        if nk == 1:
            finish(prod)
            return
        acc = refs[-1]
        k = pl.program_id(2)

        @pl.when(k == 0)
        def _():
            acc[...] = prod

        @pl.when(k > 0)
        def _():
            acc[...] += prod

        @pl.when(k == nk - 1)
        def _():
            finish(acc[...])

    in_specs = [a_spec, b_spec] + ([o_spec] if has_res else [])
    if relu2:
        out_shape = (jax.ShapeDtypeStruct((m, n), BF16), jax.ShapeDtypeStruct((m, n), BF16))
        out_specs = (o_spec, o_spec)
    else:
        out_shape = jax.ShapeDtypeStruct(o_shape, out_dtype)
        out_specs = o_spec
    args = (a, b) + ((extra,) if has_res else ()) + (tuple(carry[0]) if carry else ())
    return _pcall(body, carry=carry, name=name, grid=(m // tm, n // tn, nk), in_specs=in_specs, out_specs=out_specs,
                  out_shape=out_shape, scratch_shapes=[pltpu.VMEM((tm, tn), F32)] if nk > 1 else [],
                  compiler_params=_params())(*args)


def _mm_sum_nt(name, parts, wblocks, norm_bwd=None, add=()):
    parts = [p if isinstance(p, tuple) else (p, 0, p.shape[1]) for p in parts]
    m, npart = parts[0][0].shape[0], len(parts)
    n = wblocks[0][0].shape[-2]
    tm, tn = _tile(m, 512), _tile(n, 1024)
    assert norm_bwd is None or tn == n

    def body(*refs):
        acc = _dot(_mx(refs[0][...]), _mx(refs[npart][...]), _NT)
        for k in range(1, npart):
            acc = acc + _dot(_mx(refs[k][...]), _mx(refs[npart + k][...]), _NT)
        if norm_bwd is None:
            for r in refs[2 * npart:-1]:
                acc = acc + r[...]
            refs[-1][...] = acc
            return
        x_ref, g_ref, res_ref, dx_ref, dg_ref = refs[2 * npart:]
        _, vjp = jax.vjp(_f_norm, x_ref[...], g_ref[...])
        dx, dg = vjp((acc,))
        dx_ref[...] = dx + res_ref[...]

        @pl.when(pl.program_id(0) == 0)
        def _():
            dg_ref[...] = jnp.zeros_like(dg_ref)

        dg_ref[...] += dg

    row = pl.BlockSpec((tm, tn), lambda i, j: (i, j))
    vec = pl.BlockSpec((1, tn), lambda i, j: (0, j))
    in_specs = [pl.BlockSpec((tm, wd), lambda i, j, cb=cb: (i, cb)) for _, cb, wd in parts]
    for (_, _, wd), (w, cb) in zip(parts, wblocks):
        in_specs.append(pl.BlockSpec((None, tn, wd), lambda i, j, cb=cb: (cb, j, 0)) if w.ndim == 3
                        else pl.BlockSpec((tn, wd), lambda i, j, cb=cb: (j, cb)))
    args = [p for p, _, _ in parts] + [w for w, _ in wblocks]
    if norm_bwd is None:
        return _pcall(body, name=name, grid=(m // tm, n // tn), in_specs=in_specs + [row] * len(add), out_specs=row,
                      out_shape=jax.ShapeDtypeStruct((m, n), F32), compiler_params=_params())(*args, *add)
    return _pcall(body, name=name, grid=(m // tm, 1), in_specs=in_specs + [row, vec, row], out_specs=(row, vec),
                  out_shape=(jax.ShapeDtypeStruct((m, n), F32), jax.ShapeDtypeStruct((1, n), F32)),
                  compiler_params=_params())(*args, *norm_bwd)


def _pw_fwd(name, f, ins, params, out_dtypes, tc, ncol, tm=ROWS_FWD, groups=1):
    t = ins[0][0].shape[0]
    tm = min(tm, t)
    ni, npar = len(ins), len(params)
    gw = tc // groups

    def body(*refs):
        for g in range(groups):
            sl = slice(g * gw, (g + 1) * gw)
            vals = f(*[r[:, sl].astype(F32) for r in refs[:ni]], *[r[:, sl] for r in refs[ni:ni + npar]])
            for o, v in zip(refs[ni + npar:], vals):
                o[:, sl] = v.astype(o.dtype)

    in_specs = [pl.BlockSpec((tm, tc), lambda j, i, off=off: (i, off + j)) for _, off in ins]
    in_specs += [pl.BlockSpec((1, tc), lambda j, i, off=off: (0, off + j)) for _, off in params]
    out_specs = tuple(pl.BlockSpec((tm, tc), lambda j, i: (i, j)) for _ in out_dtypes)
    out_shape = tuple(jax.ShapeDtypeStruct((t, ncol * tc), d) for d in out_dtypes)
    return _pcall(body, name=name, grid=(ncol, t // tm), in_specs=in_specs, out_specs=out_specs, out_shape=out_shape,
                  compiler_params=_params())(*[a for a, _ in ins], *[p for p, _ in params])


def _pw_bwd(name, f, ins, params, douts, tc, ncol, want, adds=None, tm=ROWS_BWD, out_dtypes=None, groups=1):
    adds = adds or {}
    out_dtypes = out_dtypes or [F32] * len(want)
    t = ins[0][0].shape[0]
    tm = min(tm, t)
    ni, npar, nd, na = len(ins), len(params), len(douts), len(adds)
    add_keys = sorted(adds)
    gw = tc // groups

    def body(*refs):
        in_refs, p_refs = refs[:ni], refs[ni:ni + npar]
        d_refs = refs[ni + npar:ni + npar + nd]
        a_refs = refs[ni + npar + nd:ni + npar + nd + na]
        o_refs = refs[ni + npar + nd + na:]
        for p in range(npar):
            @pl.when(pl.program_id(1) == 0)
            def _(o=o_refs[len(want) + p]):
                o[...] = jnp.zeros_like(o)

        for g in range(groups):
            sl = slice(g * gw, (g + 1) * gw)
            _, vjp = jax.vjp(f, *[r[:, sl].astype(F32) for r in in_refs], *[r[:, sl] for r in p_refs])
            cts = vjp(tuple(d[:, sl].astype(F32) for d in d_refs))
            for o, kidx in zip(o_refs[:len(want)], want):
                v = cts[kidx]
                if kidx in adds:
                    v = v + a_refs[add_keys.index(kidx)][:, sl]
                o[:, sl] = v.astype(o.dtype)
            for p in range(npar):
                o_refs[len(want) + p][:, sl] += cts[ni + p]

    in_specs = [pl.BlockSpec((tm, tc), lambda j, i, off=off: (i, off + j)) for _, off in ins]
    in_specs += [pl.BlockSpec((1, tc), lambda j, i, off=off: (0, off + j)) for _, off in params]
    in_specs += [pl.BlockSpec((tm, tc), lambda j, i: (i, j)) for _ in range(nd + na)]
    out_specs = tuple([pl.BlockSpec((tm, tc), lambda j, i: (i, j)) for _ in want]
                      + [pl.BlockSpec((1, tc), lambda j, i: (0, j)) for _ in params])
    out_shape = tuple([jax.ShapeDtypeStruct((t, ncol * tc), dt) for dt in out_dtypes]
                      + [jax.ShapeDtypeStruct((1, ncol * tc), F32) for _ in params])
    res = _pcall(body, name=name, grid=(ncol, t // tm), in_specs=in_specs, out_specs=out_specs, out_shape=out_shape,
                 compiler_params=_params())(*[a for a, _ in ins], *[p for p, _ in params], *douts, *[adds[k] for k in add_keys])
    return list(res[:len(want)]), list(res[len(want):])


def _rms(x, g):
    return (x * lax.rsqrt(jnp.mean(x * x, axis=-1, keepdims=True) + EPS)) * g


def _f_norm(x, g):
    return (_rms(x, g),)


def _f_softplus(d, b):
    return (jax.nn.softplus(d + b),)


def _f_ssd_post(yf, yb, xs, z, dskip, nw):
    u = (yf + yb + dskip * xs) * jax.nn.silu(z)
    return (_rms(u, nw),)


def _neg_expm1(v):
    t = jnp.tanh(0.5 * v)
    return -2.0 * t / (1.0 - t)


def _f_lru_gates(pre_a, pre_x, u, ba, bx, lam):
    rg = jax.nn.sigmoid(pre_a + ba)
    ig = jax.nn.sigmoid(pre_x + bx)
    log_a = -LRU_C * rg * jax.nn.softplus(-lam)
    return jnp.exp(log_a), jnp.sqrt(_neg_expm1(2.0 * log_a)) * (ig * u)


def _f_lru_post(hf, hb, gate):
    return ((hf + hb) * jax.nn.gelu(gate),)


def _f_hgrn_pre(fr, l0, l1):
    lb = jax.nn.sigmoid(l1 - l0)
    k = (1.0 - lb) * jax.nn.sigmoid(-fr)
    return k, jnp.log1p(-k)


def _f_hgrn_post(of, ob, gate, nw):
    return (_rms(of + ob, nw) * jax.nn.silu(gate),)


def _loss_head(x, tgt, g, tm=ROWS_FWD):
    t, d = x.shape
    tm = min(tm, t)

    def body(x_ref, t_ref, g_ref, dx_ref, dg_ref, loss_ref):
        tv = t_ref[...]

        def lf(xv, gv):
            return 0.5 * jnp.sum(jnp.mean(jnp.square(_rms(xv, gv) - tv), axis=-1))

        val, vjp = jax.vjp(lf, x_ref[...], g_ref[...])
        dx, dg = vjp(jnp.ones((), F32))
        dx_ref[...] = dx

        @pl.when(pl.program_id(0) == 0)
        def _():
            dg_ref[...] = jnp.zeros_like(dg_ref)
            loss_ref[...] = jnp.zeros_like(loss_ref)

        dg_ref[...] += dg
        loss_ref[...] += jnp.full(loss_ref.shape, val, F32)

    row = pl.BlockSpec((tm, d), lambda i: (i, 0))
    vec = pl.BlockSpec((1, d), lambda i: (0, 0))
    return _pcall(body, name="loss_head", grid=(t // tm,), in_specs=[row, row, vec],
                  out_specs=(row, vec, pl.BlockSpec((1, 128), lambda i: (0, 0))),
                  out_shape=(jax.ShapeDtypeStruct((t, d), F32), jax.ShapeDtypeStruct((1, d), F32),
                             jax.ShapeDtypeStruct((1, 128), F32)), compiler_params=_params())(x, tgt, g)


def _shifted(x, d, prev, nxt, first, last):
    r = x.shape[0]
    if d < 0:
        rolled = pltpu.roll(x, -d, 0)
        return rolled, [(q, jnp.where(first, 0.0, prev[8 + d + q:8 + d + q + 1, :]) - x[r + d + q:r + d + q + 1, :])
                        for q in range(-d)]
    rolled = pltpu.roll(x, r - d, 0)
    return rolled, [(r - d + q, jnp.where(last, 0.0, nxt[q:q + 1, :]) - x[q:q + 1, :]) for q in range(d)]


def _conv_fwd(p3, w, b, col0, ncol, silu, tc=1024):
    nbatch, s, _ = p3.shape
    ts = min(CONV_ROWS, s)
    nblk = s // ts

    def body(x_ref, pv_ref, nx_ref, w_ref, b_ref, o_ref, *act_ref):
        i = pl.program_id(1)
        first, last = i == 0, i == nblk - 1
        x, pv, nx = x_ref[...], pv_ref[...], nx_ref[...]
        wv = w_ref[...]
        out = b_ref[...] + wv[1:2] * x
        fix = {}
        for k, d in ((0, -1), (2, 1), (3, 2)):
            rolled, patches = _shifted(x, d, pv, nx, first, last)
            out = out + wv[k:k + 1] * rolled
            for rw, delta in patches:
                fix[rw] = fix.get(rw, 0.0) + wv[k:k + 1] * delta
        o_ref[...] = out
        for rw, delta in fix.items():
            o_ref[rw:rw + 1, :] = out[rw:rw + 1, :] + delta
        if silu:
            act_ref[0][...] = jax.nn.silu(o_ref[...])

    nb8 = s // 8
    cur = pl.BlockSpec((None, ts, tc), lambda n, i, j: (n, i, col0 + j))
    prev = pl.BlockSpec((None, 8, tc), lambda n, i, j: (n, jnp.maximum(i * (ts // 8) - 1, 0), col0 + j))
    nxt = pl.BlockSpec((None, 8, tc), lambda n, i, j: (n, jnp.minimum((i + 1) * (ts // 8), nb8 - 1), col0 + j))
    out = pl.BlockSpec((None, ts, tc), lambda n, i, j: (n, i, j))
    shp = jax.ShapeDtypeStruct((nbatch, s, ncol * tc), F32)
    return _pcall(body, name=f"conv_fwd{col0}", grid=(nbatch, nblk, ncol),
                  in_specs=[cur, prev, nxt, pl.BlockSpec((4, tc), lambda n, i, j: (0, col0 + j)),
                            pl.BlockSpec((1, tc), lambda n, i, j: (0, col0 + j))],
                  out_specs=(out, out) if silu else out, out_shape=(shp, shp) if silu else shp,
                  compiler_params=_params())(p3, p3, p3, w, b)


def _conv_bwd(dc3, p3, w, col, conv3=None):
    nbatch, s, tc = dc3.shape
    ts = min(CONV_ROWS, s)
    nblk = s // ts
    silu = conv3 is not None

    def body(d_ref, dpv_ref, dnx_ref, x_ref, pv_ref, nx_ref, w_ref, *rest):
        n, i = pl.program_id(0), pl.program_id(1)
        first, last = i == 0, i == nblk - 1
        d, dpv, dnx = d_ref[...], dpv_ref[...], dnx_ref[...]
        if silu:
            d, dpv, dnx = [jax.vjp(jax.nn.silu, c_ref[...])[1](t)[0] for c_ref, t in zip(rest[:3], (d, dpv, dnx))]
        dx_ref, dw_ref = rest[3 * silu:]
        x, pv, nx = x_ref[...], pv_ref[...], nx_ref[...]
        wv = w_ref[...]
        dx = wv[1:2] * d
        fix = {}
        for k, sh in ((0, 1), (2, -1), (3, -2)):
            rolled, patches = _shifted(d, sh, dpv, dnx, first, last)
            dx = dx + wv[k:k + 1] * rolled
            for rw, delta in patches:
                fix[rw] = fix.get(rw, 0.0) + wv[k:k + 1] * delta
        dx_ref[...] = dx.astype(dx_ref.dtype)
        for rw, delta in fix.items():
            dx_ref[rw:rw + 1, :] = (dx[rw:rw + 1, :] + delta).astype(dx_ref.dtype)

        @pl.when((n == 0) & (i == 0))
        def _():
            dw_ref[...] = jnp.zeros_like(dw_ref)

        dw_ref[1:2, :] += jnp.sum(d * x, axis=0, keepdims=True)
        dw_ref[4:5, :] += jnp.sum(d, axis=0, keepdims=True)
        for k, sh in ((0, -1), (2, 1), (3, 2)):
            rolled, patches = _shifted(x, sh, pv, nx, first, last)
            dw = jnp.sum(d * rolled, axis=0, keepdims=True)
            for rw, delta in patches:
                dw = dw + d[rw:rw + 1, :] * delta
            dw_ref[k:k + 1, :] += dw

    nb8 = s // 8

    def specs(j):
        cur = pl.BlockSpec((None, ts, tc), lambda n, i: (n, i, j))
        prev = pl.BlockSpec((None, 8, tc), lambda n, i: (n, jnp.maximum(i * (ts // 8) - 1, 0), j))
        nxt = pl.BlockSpec((None, 8, tc), lambda n, i: (n, jnp.minimum((i + 1) * (ts // 8), nb8 - 1), j))
        return [cur, prev, nxt]

    return _pcall(body, name=f"conv_bwd{col}", grid=(nbatch, nblk),
                  in_specs=specs(0) + specs(col) + [pl.BlockSpec((4, tc), lambda n, i: (0, col))] + specs(col) * silu,
                  out_specs=(specs(0)[0], pl.BlockSpec((8, tc), lambda n, i: (0, 0))),
                  out_shape=(jax.ShapeDtypeStruct((nbatch, s, tc), BF16), jax.ShapeDtypeStruct((8, tc), F32)),
                  compiler_params=_params())(dc3, dc3, dc3, p3, p3, p3, w, *([conv3] * 3 * silu))


def _block_scan(coef, inp, reverse):
    r = coef.shape[0]
    row = lax.broadcasted_iota(jnp.int32, coef.shape, 0)
    a, b = coef, inp
    d = 1
    while d < r:
        if reverse:
            keep = row < r - d
            a_sh, b_sh = pltpu.roll(a, r - d, 0), pltpu.roll(b, r - d, 0)
        else:
            keep = row >= d
            a_sh, b_sh = pltpu.roll(a, d, 0), pltpu.roll(b, d, 0)
        b = b + a * jnp.where(keep, b_sh, 0.0)
        a = a * jnp.where(keep, a_sh, 1.0)
        d *= 2
    return a, b


def _lru_scan(a3, b3, reverse):
    nbatch, s, w = a3.shape
    ts = min(LRU_ROWS, s)
    nblk = s // ts
    edge = 0 if reverse else ts - 1

    def body(a_ref, b_ref, h_ref, carry):
        @pl.when(pl.program_id(1) == 0)
        def _():
            carry[...] = jnp.zeros_like(carry)

        ca, hb = _block_scan(a_ref[...], b_ref[...], reverse)
        h = hb + ca * carry[0:1, :]
        h_ref[...] = h
        carry[0:1, :] = h[edge:edge + 1, :]

    blk = pl.BlockSpec((None, ts, w), (lambda n, i: (n, nblk - 1 - i, 0)) if reverse else (lambda n, i: (n, i, 0)))
    return _pcall(body, name=f"lru_scan_r{int(reverse)}", grid=(nbatch, nblk), in_specs=[blk, blk], out_specs=blk,
                  out_shape=jax.ShapeDtypeStruct((nbatch, s, w), F32), scratch_shapes=[pltpu.VMEM((8, w), F32)],
                  compiler_params=_params())(a3, b3)


def _lru_scan_bwd(a3, h3, dh3, reverse, carry=None):
    nbatch, s, w = a3.shape
    ts = min(LRU_ROWS, s)
    nblk = s // ts
    nb8 = s // 8
    tpb = ts // 8

    def body(a_ref, aa_ref, h_ref, hh_ref, dh_ref, g_ref, da_ref, carry):
        i = pl.program_id(1)

        @pl.when(i == 0)
        def _():
            carry[...] = jnp.zeros_like(carry)

        a, h = a_ref[...], h_ref[...]
        row = lax.broadcasted_iota(jnp.int32, a.shape, 0)
        if reverse:
            a_edge = jnp.where(i == 0, 0.0, aa_ref[7:8, :])
            c = jnp.where(row == 0, a_edge, pltpu.roll(a, 1, 0))
            h_edge = jnp.where(i == nblk - 1, 0.0, hh_ref[0:1, :])
            h_sh = jnp.where(row == ts - 1, h_edge, pltpu.roll(h, ts - 1, 0))
        else:
            a_edge = jnp.where(i == 0, 0.0, aa_ref[0:1, :])
            c = jnp.where(row == ts - 1, a_edge, pltpu.roll(a, ts - 1, 0))
            h_edge = jnp.where(i == nblk - 1, 0.0, hh_ref[7:8, :])
            h_sh = jnp.where(row == 0, h_edge, pltpu.roll(h, 1, 0))
        cc, gb = _block_scan(c, dh_ref[...], not reverse)
        g = gb + cc * carry[0:1, :]
        g_ref[...] = g
        carry[0:1, :] = g[ts - 1:ts, :] if reverse else g[0:1, :]
        da_ref[...] = g * h_sh

    if reverse:
        bi = lambda i: i
    else:
        bi = lambda i: nblk - 1 - i
    blk = pl.BlockSpec((None, ts, w), lambda n, i: (n, bi(i), 0))
    before = pl.BlockSpec((None, 8, w), lambda n, i: (n, jnp.maximum(bi(i) * tpb - 1, 0), 0))
    after = pl.BlockSpec((None, 8, w), lambda n, i: (n, jnp.minimum((bi(i) + 1) * tpb, nb8 - 1), 0))
    a_tile, h_tile = (before, after) if reverse else (after, before)
    return _pcall(body, carry=carry, name=f"lru_scan_bwd_r{int(reverse)}", grid=(nbatch, nblk),
                  in_specs=[blk, a_tile, blk, h_tile, blk], out_specs=(blk, blk),
                  out_shape=(jax.ShapeDtypeStruct((nbatch, s, w), F32), jax.ShapeDtypeStruct((nbatch, s, w), F32)),
                  scratch_shapes=[pltpu.VMEM((8, w), F32)],
                  compiler_params=_params())(a3, a3, h3, h3, dh3, *(carry[0] if carry else ()))


def _head_expand(lane0):
    return (jnp.right_shift(lax.broadcasted_iota(jnp.int32, (128, 1024), 1), HEAD_SHIFT) + lane0
            == lax.broadcasted_iota(jnp.int32, (128, 1024), 0)).astype(F32)


def _head_reduce(lane0):
    return (jnp.right_shift(lax.broadcasted_iota(jnp.int32, (1024, 128), 0), HEAD_SHIFT) + lane0
            == lax.broadcasted_iota(jnp.int32, (1024, 128), 1)).astype(F32)


def _time_mask(q, reverse):
    ri = lax.broadcasted_iota(jnp.int32, (q, q), 0)
    ci = lax.broadcasted_iota(jnp.int32, (q, q), 1)
    return (ri <= ci) if reverse else (ri >= ci)


def _ssd_common(xs_ref, bc_ref, dt_ref, al_ref, reverse, lane0):
    q = xs_ref.shape[0]
    edge = 0 if reverse else q - 1
    dt = dt_ref[...]
    a = -jnp.exp(al_ref[...])
    mask = _time_mask(q, reverse)
    expand = _head_expand(lane0)
    cum = _dot01(mask.astype(F32), dt * a, split="b", terms=3)
    cum_x = _dot01(cum, expand, split="a", terms=2)
    dt_x = _dot01(dt, expand, split="a", terms=2)
    last_x = cum_x[edge:edge + 1, :]
    xs = xs_ref[...]
    bc = bc_ref[...]
    return dict(q=q, edge=edge, lane0=lane0, dt=dt, a=a, mask=mask, cum_t=cum.T, cum_x=cum_x, dt_x=dt_x, xs=xs,
                v=xs * dt_x, e_c=jnp.exp(cum_x), w=jnp.exp(last_x - cum_x), e_l=jnp.exp(last_x),
                bm=bc[:, :512], cm=bc[:, 512:])


def _ssd_decay(c, h):
    row = c["lane0"] + h
    seg = c["cum_x"][:, h * SSD_HEADDIM:h * SSD_HEADDIM + 1] - c["cum_t"][row:row + 1, :]
    return jnp.where(c["mask"], jnp.exp(jnp.minimum(seg, 0.0)), 0.0)


def _head_masks():
    lane = jnp.right_shift(lax.broadcasted_iota(jnp.int32, (1, 256), 1), HEAD_SHIFT)
    return [lane == e for e in range(4)]


def _ssd_fwd(xbc3, dt3, alog, reverse, carry=None):
    nbatch, s, _ = xbc3.shape
    q = min(SSD_CHUNK, s)
    nc = s // q
    lane0 = SSD_HEADS * int(reverse)

    def body(xs_ref, bc_ref, dt_ref, al_ref, y_ref, st_ref, st):
        @pl.when(pl.program_id(1) == 0)
        def _():
            st[...] = jnp.zeros_like(st)

        st_ref[...] = st[...]
        c = _ssd_common(xs_ref, bc_ref, dt_ref, al_ref, reverse, lane0)
        hm = _head_masks()
        for g in range(SSD_GROUPS):
            sl = slice(g * 256, (g + 1) * 256)
            cg, bg = _mx(c["cm"][:, g * 128:(g + 1) * 128]), _mx(c["bm"][:, g * 128:(g + 1) * 128])
            cb = _dot(cg, bg, _NT)
            vg = c["v"][:, sl]
            s0 = st[:, sl]
            yg = _dot(cg, _mx(s0)) * c["e_c"][:, sl]
            for e in range(4):
                m = _ssd_decay(c, 4 * g + e) * cb
                yg = yg + _dot(_mx(m), _mx(jnp.where(hm[e], vg, 0.0)))
            y_ref[:, sl] = yg
            st[:, sl] = c["e_l"][:, sl] * s0 + _dot(bg, _mx(vg * c["w"][:, sl]), _TN)

    ck = (lambda i: nc - 1 - i) if reverse else (lambda i: i)
    xs_spec = pl.BlockSpec((None, q, 1024), lambda n, i: (n, ck(i), 0))
    bc_spec = pl.BlockSpec((None, q, 1024), lambda n, i: (n, ck(i), 1))
    dt_spec = pl.BlockSpec((None, q, 128), lambda n, i: (n, ck(i), 0))
    al_spec = pl.BlockSpec((1, 128), lambda n, i: (0, 0))
    st_spec = pl.BlockSpec((None, None, 128, 1024), lambda n, i: (n, ck(i), 0, 0))
    return _pcall(body, carry=carry, name=f"ssd_fwd_r{int(reverse)}", grid=(nbatch, nc),
                  in_specs=[xs_spec, bc_spec, dt_spec, al_spec], out_specs=(xs_spec, st_spec),
                  out_shape=(jax.ShapeDtypeStruct((nbatch, s, 1024), F32), jax.ShapeDtypeStruct((nbatch, nc, 128, 1024), F32)),
                  scratch_shapes=[pltpu.VMEM((128, 1024), F32)],
                  compiler_params=_params())(xbc3, xbc3, dt3, alog, *(carry[0] if carry else ()))


def _ssd_bwd(xbc3, dt3, alog, st4, dy3, reverse, add_to=(), scatter=()):
    nbatch, s, _ = xbc3.shape
    q = min(SSD_CHUNK, s)
    nc = s // q
    lane0 = SSD_HEADS * int(reverse)
    nadd, ns = len(add_to), len(scatter)

    def body(xs_ref, bc_ref, dt_ref, al_ref, st0_ref, dy_ref, *rest):
        adds, srcs, rest = rest[:nadd], rest[nadd:nadd + ns], rest[nadd + ns:]
        (dxs_ref, dbc_ref, ddt_ref, dal_ref), lands, dst = rest[:4], rest[4:4 + ns], rest[4 + ns]
        n, i = pl.program_id(0), pl.program_id(1)
        if ns:
            sends, arrivals = _scatter_copies(srcs, lands, *rest[5 + ns:])

            @pl.when((n == 0) & (i == 0))
            def _():
                for cp in sends:
                    cp.start()

        @pl.when(i == 0)
        def _():
            dst[...] = jnp.zeros_like(dst)

        @pl.when((i == 0) & (n == 0))
        def _():
            dal_ref[...] = jnp.zeros_like(dal_ref)

        c = _ssd_common(xs_ref, bc_ref, dt_ref, al_ref, reverse, lane0)
        hm = _head_masks()
        reduce_m = _head_reduce(lane0)
        s0_all, ds1_all, dy = st0_ref[...], dst[...], dy_ref[...]
        lane = lax.broadcasted_iota(jnp.int32, (q, 128), 1)
        sub = lax.broadcasted_iota(jnp.int32, (128, q), 0)
        rowacc = jnp.zeros((q, 128), F32)
        colacc_t = jnp.zeros((128, q), F32)
        dv_l, yst_l, dvbar_l, dk_l, dc_l = [], [], [], [], []
        for g in range(SSD_GROUPS):
            sl = slice(g * 256, (g + 1) * 256)
            cg, bg = _mx(c["cm"][:, g * 128:(g + 1) * 128]), _mx(c["bm"][:, g * 128:(g + 1) * 128])
            cb = _dot(cg, bg, _NT)
            vg, dyg, wg, ecg = c["v"][:, sl], dy[:, sl], c["w"][:, sl], c["e_c"][:, sl]
            s0, ds1 = _mx(s0_all[:, sl]), _mx(ds1_all[:, sl])
            dye = _mx(dyg * ecg)
            yst_l.append(_dot(cg, s0) * ecg)
            dcg = _dot(dye, s0, _NT)
            dst[:, sl] = c["e_l"][:, sl] * ds1_all[:, sl] + _dot(cg, dye, _TN)
            vbar = _mx(vg * wg)
            dvbar = _dot(bg, ds1)
            dvbar_l.append(dvbar)
            dvg = dvbar * wg
            dkg = _dot(vbar, ds1, _NT)
            for e in range(4):
                h = 4 * g + e
                m = _ssd_decay(c, h)
                dyh, vh = _mx(jnp.where(hm[e], dyg, 0.0)), _mx(jnp.where(hm[e], vg, 0.0))
                dvg = dvg + _dot(_mx(m * cb), dyh, _TN)
                dcb = _dot(dyh, vh, _NT) * m
                dcbb = _mx(dcb)
                dcg = dcg + _dot(dcbb, bg)
                dkg = dkg + _dot(dcbb, cg, _TN)
                wmat = dcb * cb
                rowacc = jnp.where(lane == lane0 + h, jnp.sum(wmat, axis=1, keepdims=True), rowacc)
                colacc_t = jnp.where(sub == lane0 + h, jnp.sum(wmat, axis=0, keepdims=True), colacc_t)
            dv_l.append(dvg)
            dk_l.append(dkg)
            dc_l.append(dcg)
        dv = jnp.concatenate(dv_l, axis=1)
        yst = jnp.concatenate(yst_l, axis=1)
        dvbar = jnp.concatenate(dvbar_l, axis=1)
        t1 = _dot01(dy * yst, reduce_m, split="a", terms=3)
        t2 = _dot01(c["v"] * c["w"] * dvbar, reduce_m, split="a", terms=3)
        dlast = jnp.sum(t2, axis=0, keepdims=True) + _dot01(
            c["e_l"] * jnp.sum(ds1_all * s0_all, axis=0, keepdims=True), reduce_m, split="a", terms=2)
        dcum = rowacc - colacc_t.T + t1 - t2
        dcum = dcum + jnp.where(lax.broadcasted_iota(jnp.int32, (q, 128), 0) == c["edge"], dlast, 0.0)
        dda = _dot01(c["mask"].astype(F32), dcum, _TN, split="b", terms=3)
        ddt = dda * c["a"] + _dot01(dv * c["xs"], reduce_m, split="a", terms=2)
        dal_ref[...] += jnp.sum(dda * c["dt"], axis=0, keepdims=True) * c["a"]
        dxs = dv * c["dt_x"]
        dbc = jnp.concatenate(dk_l + dc_l, axis=1)
        if nadd:
            for a_ref in adds[:-2]:
                dxs = dxs + a_ref[...]
            dbc = dbc + adds[-2][...]
            ddt = ddt + adds[-1][...]
        ddt_ref[...] = ddt
        dxs_ref[...] = dxs
        dbc_ref[...] = dbc
        if ns:
            @pl.when((n == nbatch - 1) & (i == nc - 1))
            def _():
                for cp in arrivals:
                    cp.wait_recv()
                for cp in sends:
                    cp.wait_send()

    ck = (lambda i: i) if reverse else (lambda i: nc - 1 - i)
    xs_spec = pl.BlockSpec((None, q, 1024), lambda n, i: (n, ck(i), 0))
    bc_spec = pl.BlockSpec((None, q, 1024), lambda n, i: (n, ck(i), 1))
    dt_spec = pl.BlockSpec((None, q, 128), lambda n, i: (n, ck(i), 0))
    al_spec = pl.BlockSpec((1, 128), lambda n, i: (0, 0))
    st_spec = pl.BlockSpec((None, None, 128, 1024), lambda n, i: (n, ck(i), 0, 0))
    return _pcall(body, name=f"ssd_bwd_r{int(reverse)}", grid=(nbatch, nc),
                  in_specs=([xs_spec, bc_spec, dt_spec, al_spec, st_spec, xs_spec] + [xs_spec] * (nadd - 1)
                            + [dt_spec] * bool(nadd) + [ANY] * ns),
                  out_specs=(xs_spec, xs_spec, dt_spec, al_spec) + (ANY,) * ns,
                  out_shape=(jax.ShapeDtypeStruct((nbatch, s, 1024), F32), jax.ShapeDtypeStruct((nbatch, s, 1024), F32),
                             jax.ShapeDtypeStruct((nbatch, s, 128), F32), jax.ShapeDtypeStruct((1, 128), F32))
                  + tuple(jax.ShapeDtypeStruct(c.shape, c.dtype) for c in scatter),
                  scratch_shapes=[pltpu.VMEM((128, 1024), F32)] + (_scatter_scratch(ns) if ns else []),
                  compiler_params=_params())(xbc3, xbc3, dt3, alog, st4, dy3, *add_to, *scatter)


def _gla_block(q, k, g, reverse):
    bq = g.shape[0]
    nsub = bq // HGRN_SUB
    edge = 0 if reverse else bq - 1
    ri = lax.broadcasted_iota(jnp.int32, (bq, bq), 0)
    ci = lax.broadcasted_iota(jnp.int32, (bq, bq), 1)
    rb, cb = jnp.right_shift(ri, HGRN_SUB_SHIFT), jnp.right_shift(ci, HGRN_SUB_SHIFT)
    mask = (ri <= ci) if reverse else (ri >= ci)
    m_within = (mask & (rb == cb)).astype(F32)
    m_before = ((cb > rb) if reverse else (cb < rb)).astype(F32)
    bl = _dot01(m_within, g, split="b", terms=3)
    c = _dot01(m_before, g, split="b", terms=3)
    last = c[edge:edge + 1, :] + bl[edge:edge + 1, :]
    ebl, enbl, ec, elc = jnp.exp(bl), jnp.exp(-bl), jnp.exp(c), jnp.exp(last - c)
    qh = q * HGRN_SCALE * ebl
    kh = k * enbl
    blk = jnp.right_shift(lax.broadcasted_iota(jnp.int32, (bq, 1), 0), HGRN_SUB_SHIFT)
    scale = []
    for i in range(nsub):
        valid = (blk >= i) if reverse else (blk <= i)
        ex = jnp.where(valid, c[i * HGRN_SUB:i * HGRN_SUB + 1, :] - c, 0.0)
        scale.append(jnp.where(valid, jnp.exp(ex), 0.0))
    return dict(bq=bq, nsub=nsub, edge=edge, mask=mask, m_within=m_within, m_before=m_before, ebl=ebl, enbl=enbl, ec=ec,
                elc=elc, e_l=jnp.exp(last), qh=qh, qt=qh * ec, kh=kh, kb=kh * elc, scale=scale)


def _gla_scores(c, hs):
    keys = [_mx(c["kh"][:, hs] * c["scale"][i][:, hs]) for i in range(c["nsub"])]
    rows = [_dot(_mx(c["qh"][i * HGRN_SUB:(i + 1) * HGRN_SUB, hs]), keys[i], _NT) for i in range(c["nsub"])]
    return jnp.where(c["mask"], jnp.concatenate(rows, axis=0), 0.0), keys


def _gla_specs(nbatch, s, w, reverse_order):
    bq = min(HGRN_BLOCK, s)
    nblk = s // bq
    bi = (lambda i: nblk - 1 - i) if reverse_order else (lambda i: i)
    col = lambda cb: pl.BlockSpec((nbatch, bq, w), lambda i: (0, bi(i), cb))
    st_spec = pl.BlockSpec((nbatch, None, 128, w), lambda i: (0, bi(i), 0, 0))
    return bq, nblk, col, st_spec


def _gla_fwd(proj3, l0, l1, reverse):
    nbatch, s, w5 = proj3.shape
    w = w5 // 5
    bq, nblk, col, st_spec = _gla_specs(nbatch, s, w, reverse)
    vec = pl.BlockSpec((1, w), lambda i: (0, 0))

    def body(q_ref, f_ref, v_ref, l0_ref, l1_ref, o_ref, st_ref, st):
        @pl.when(pl.program_id(0) == 0)
        def _():
            st[...] = jnp.zeros_like(st)

        for b in range(nbatch):
            st_ref[b] = st[b]
            k, g = _f_hgrn_pre(f_ref[b], l0_ref[...], l1_ref[...])
            c = _gla_block(q_ref[b], k, g, reverse)
            v = v_ref[b]
            for h in range(HGRN_HEADS):
                hs = slice(h * 128, (h + 1) * 128)
                att, _ = _gla_scores(c, hs)
                vb = _mx(v[:, hs])
                s0 = st[b, :, hs]
                o_ref[b, :, hs] = _dot(_mx(att), vb) + _dot(_mx(c["qt"][:, hs]), _mx(s0), _NT)
                st[b, :, hs] = s0 * c["e_l"][:, hs] + _dot(vb, _mx(c["kb"][:, hs]), _TN)

    return _pcall(body, name=f"gla_fwd_r{int(reverse)}", grid=(nblk,),
                  in_specs=[col(0), col(1 + int(reverse)), col(3), vec, vec], out_specs=(col(0), st_spec),
                  out_shape=(jax.ShapeDtypeStruct((nbatch, s, w), F32), jax.ShapeDtypeStruct((nbatch, nblk, 128, w), F32)),
                  scratch_shapes=[pltpu.VMEM((nbatch, 128, w), F32)], compiler_params=_params())(proj3, proj3, proj3, l0, l1)


def _gla_bwd(proj3, l0, l1, st4, do3, reverse, add_to=None):
    nbatch, s, w5 = proj3.shape
    w = w5 // 5
    bq, nblk, col, st_spec = _gla_specs(nbatch, s, w, not reverse)
    nadd = 0 if add_to is None else 2
    vec = pl.BlockSpec((1, w), lambda i: (0, 0))

    def body(q_ref, f_ref, v_ref, l0_ref, l1_ref, st_ref, do_ref, *rest):
        adds, (dq_ref, df_ref, dv_ref, dl0_ref, dl1_ref, dst) = rest[:nadd], rest[nadd:]

        @pl.when(pl.program_id(0) == 0)
        def _():
            dst[...] = jnp.zeros_like(dst)
            dl0_ref[...] = jnp.zeros_like(dl0_ref)
            dl1_ref[...] = jnp.zeros_like(dl1_ref)

        row = lax.broadcasted_iota(jnp.int32, (bq, 128), 0)
        for b in range(nbatch):
            (k, g), pre_vjp = jax.vjp(_f_hgrn_pre, f_ref[b], l0_ref[...], l1_ref[...])
            c = _gla_block(q_ref[b], k, g, reverse)
            s0_all, ds1_all = st_ref[b], dst[b]
            v, dy = v_ref[b], do_ref[b]
            dbl_l, dc_l, dk_l = [], [], []
            for h in range(HGRN_HEADS):
                hs = slice(h * 128, (h + 1) * 128)
                att, keys = _gla_scores(c, hs)
                qh, qt, kh, kb = c["qh"][:, hs], c["qt"][:, hs], c["kh"][:, hs], c["kb"][:, hs]
                vb, dyb = _mx(v[:, hs]), _mx(dy[:, hs])
                s0, ds1 = s0_all[:, hs], ds1_all[:, hs]
                datt = _mx(jnp.where(c["mask"], _dot(dyb, vb, _NT), 0.0))
                dqh_rows = []
                dkh = jnp.zeros((bq, 128), F32)
                dc = jnp.zeros((bq, 128), F32)
                for i in range(c["nsub"]):
                    rs = slice(i * HGRN_SUB, (i + 1) * HGRN_SUB)
                    dqh_rows.append(_dot(datt[rs], keys[i]))
                    dki = _dot(datt[rs], _mx(qh[rs]), _TN)
                    sc = c["scale"][i][:, hs]
                    dkh = dkh + dki * sc
                    dex = dki * (kh * sc)
                    dc = dc - dex + jnp.where(row == i * HGRN_SUB, jnp.sum(dex, axis=0, keepdims=True), 0.0)
                dqt = _dot(dyb, _mx(s0))
                dkb = _dot(vb, _mx(ds1))
                dv = _dot(_mx(att), dyb, _TN) + _dot(_mx(kb), _mx(ds1), _NT)
                dst[b, :, hs] = c["e_l"][:, hs] * ds1 + _dot(dyb, _mx(qt), _TN)
                dqh = jnp.concatenate(dqh_rows, axis=0) + dqt * c["ec"][:, hs]
                dkh = dkh + dkb * c["elc"][:, hs]
                kbk = dkb * kb
                dlast = jnp.sum(kbk, axis=0, keepdims=True) + c["e_l"][:, hs] * jnp.sum(ds1 * s0, axis=0, keepdims=True)
                at_edge = jnp.where(row == c["edge"], dlast, 0.0)
                dc_l.append(dc + dqt * qt - kbk + at_edge)
                dbl_l.append(dqh * qh - dkh * kh + at_edge)
                dq = dqh * c["ebl"][:, hs] * HGRN_SCALE
                if nadd:
                    dq, dv = dq + adds[0][b, :, hs], dv + adds[1][b, :, hs]
                dq_ref[b, :, hs] = dq.astype(dq_ref.dtype)
                dv_ref[b, :, hs] = dv.astype(dv_ref.dtype)
                dk_l.append(dkh * c["enbl"][:, hs])
            dg = (_dot01(c["m_within"], jnp.concatenate(dbl_l, axis=1), _TN, split="b", terms=2)
                  + _dot01(c["m_before"], jnp.concatenate(dc_l, axis=1), _TN, split="b", terms=2))
            df, d0, d1 = pre_vjp((jnp.concatenate(dk_l, axis=1), dg))
            df_ref[b] = df.astype(df_ref.dtype)
            dl0_ref[...] += d0
            dl1_ref[...] += d1

    shp_sum = jax.ShapeDtypeStruct((nbatch, s, w), BF16 if nadd else F32)
    shp_vec = jax.ShapeDtypeStruct((1, w), F32)
    return _pcall(body, name=f"gla_bwd_r{int(reverse)}", grid=(nblk,),
                  in_specs=[col(0), col(1 + int(reverse)), col(3), vec, vec, st_spec, col(0)] + [col(0)] * nadd,
                  out_specs=(col(0), col(0), col(0), vec, vec),
                  out_shape=(shp_sum, jax.ShapeDtypeStruct((nbatch, s, w), BF16), shp_sum, shp_vec, shp_vec),
                  scratch_shapes=[pltpu.VMEM((nbatch, 128, w), F32)],
                  compiler_params=_params())(proj3, proj3, proj3, l0, l1, st4, do3, *(add_to or ()))


DIRS = (False, True)


def _block_diag(w):
    eye = jnp.eye(16, dtype=w.dtype)
    return (eye[:, None, :, None] * w[:, :, None, :]).reshape(1024, 1024)


def _diag_blocks(m):
    m4 = m.reshape(16, 64, 16, 64)
    return jnp.stack([m4[i, :, i, :] for i in range(16)], axis=0)


def _pad_lanes(v, n=128):
    return jnp.pad(v, [(0, 0)] * (v.ndim - 1) + [(0, n - v.shape[-1])])


def _mlp_fwd(tag, x, nw, w1, w2, carry=None):
    (h,) = _pw_fwd(f"{tag}_norm", _f_norm, [(x, 0)], [(nw, 0)], [BF16], 1024, 1)
    a, r, *got = _mm(f"{tag}_up", h, w1, "nn", relu2=True, carry=carry)
    return _mm(f"{tag}_down", r, w2, "nn", res=x), (h, a, r), got


def _mlp_bwd(tag, x, nw, w1, w2, saved, dxo, carry=None):
    h, a, r = saved
    dw2, *got = _mm(f"{tag}_dw2", r, dxo, "tn", carry=carry) if carry else (_mm(f"{tag}_dw2", r, dxo, "tn"),)
    da = _mm(f"{tag}_da", dxo, w2, "nt", relu2_of=a, out_dtype=BF16)
    dw1 = _mm(f"{tag}_dw1", h, da, "tn", col_shards=4)
    dx, dnw = _mm_sum_nt(f"{tag}_dh", [(da, k, 1024) for k in range(4)], [(w1, k) for k in range(4)], norm_bwd=(x, nw, dxo))
    return dx, dw1, dw2, dnw, got


def _split_in0(pieces, dt_piece):
    tm = 256

    def body(p0, p1, p2, p3, p4, p5, o_ref):
        full = jnp.concatenate([p0[...], p1[...], p2[...], p3[...], p4[...], p5[:, :32]], axis=1)
        for j in range(4):
            o_ref[j] = full[:, 1288 * j:1288 * (j + 1)]

    blk = pl.BlockSpec((tm, 1024), lambda i: (i, 0))
    return _pcall(body, name="split_in0", grid=(1024 // tm,), in_specs=[blk] * 5 + [pl.BlockSpec((tm, 128), lambda i: (i, 0))],
                  out_specs=pl.BlockSpec((4, tm, 1288), lambda i: (0, i, 0)),
                  out_shape=jax.ShapeDtypeStruct((4, 1024, 1288), F32), compiler_params=_params())(*pieces, dt_piece)


def _assemble_in0(shards):
    tm = 256

    def body(s_ref, m_ref, d_ref):
        full = jnp.concatenate([s_ref[j] for j in range(4)], axis=1)
        m_ref[...] = full[:, :5120]
        d_ref[...] = jnp.concatenate([full[:, 5120:5152], jnp.zeros((tm, 96), full.dtype)], axis=1)

    return _pcall(body, name="assemble_in0", grid=(1024 // tm,), in_specs=[pl.BlockSpec((4, tm, 1288), lambda i: (0, i, 0))],
                  out_specs=(pl.BlockSpec((tm, 5120), lambda i: (i, 0)), pl.BlockSpec((tm, 128), lambda i: (i, 0))),
                  out_shape=(jax.ShapeDtypeStruct((1024, 5120), shards.dtype), jax.ShapeDtypeStruct((1024, 128), shards.dtype)),
                  compiler_params=_params())(shards)


EARLY = ("odd_w_in", "odd_w_out", "mlp_w1_l1", "mlp_w2_l1")
MID = ("even_w_out", "mlp_w1_l0", "mlp_w2_l0")
LATE = ("even_w_in",)


def _local_step(x3, tgt3, w, w_main0, w_dt0, pair_reduce=None, late=None):
    nb, s, d = x3.shape
    carries, arrived = late if late else ({}, None)
    t = nb * s
    x0 = x3.reshape(t, d)
    tgt = tgt3.reshape(t, d)
    grads = {}
    row = lambda v: v.reshape(1, -1)
    to3 = lambda v: v.reshape(nb, s, v.shape[-1])
    to2 = lambda v: v.reshape(-1, v.shape[-1])

    conv_w, conv_b = w["even_conv_w"][0], row(w["even_conv_b"][0])
    nmix0 = row(w["norm_mix"][0])
    (h0,) = _pw_fwd("l0_norm", _f_norm, [(x0, 0)], [(nmix0, 0)], [BF16], 1024, 1)
    proj0 = _mm("l0_proj", h0, w_main0, "nn")
    dt_raw = _mm("l0_proj_dt", h0, w_dt0, "nn")
    conv2, xbc3 = _conv_fwd(to3(proj0), conv_w, conv_b, 0, 2, True)
    u_lru = to2(_conv_fwd(to3(proj0), conv_w, conv_b, 2, 1, False))
    xbc = to2(xbc3)
    dt_bias = _pad_lanes(w["ssd_dt_bias"][0].reshape(1, 32))
    (dt,) = _pw_fwd("l0_dt", _f_softplus, [(dt_raw, 0)], [(dt_bias, 0)], [F32], 128, 1)
    dt3 = to3(dt)
    alog = _pad_lanes(w["ssd_a_log"][0].reshape(1, 32))
    ssd = [_ssd_fwd(xbc3, dt3, alog, r, carry=carries.get(key)) for r, key in zip(DIRS, ("mlp_w1", "mlp_w2"))]
    if late:
        w = {**w, **arrived("mlp_w1", ssd[0][2:]), **arrived("mlp_w2", ssd[1][2:])}
    yf, yb = to2(ssd[0][0]), to2(ssd[1][0])
    dskip = jnp.repeat(w["ssd_d"][0], SSD_HEADDIM).reshape(1, 1024)
    snw = row(w["ssd_norm_w"][0])
    ssd_ins = [(yf, 0), (yb, 0), (xbc, 0), (proj0, 3)]
    (ya,) = _pw_fwd("l0_ssd_post", _f_ssd_post, ssd_ins, [(dskip, 0), (snw, 0)], [BF16], 1024, 1, groups=SSD_GROUPS)
    w_gates = [_block_diag(w[k][0, r]).astype(MXU_DTYPE) for r in range(2) for k in ("lru_w_a", "lru_w_x")]
    pre = _mm("l0_lru_pre", u_lru, jnp.concatenate(w_gates, axis=1), "nn")
    lru_par = [[(row(w[k][0, r]), 0) for k in ("lru_b_a", "lru_b_x", "lru_lambda")] for r in range(2)]
    lru_ins = [[(pre, 2 * r), (pre, 2 * r + 1), (u_lru, 0)] for r in range(2)]
    ab = [_pw_fwd(f"l0_lru_gates{r}", _f_lru_gates, lru_ins[r], lru_par[r], [F32, F32], 1024, 1) for r in range(2)]
    hs = [_lru_scan(to3(ab[r][0]), to3(ab[r][1]), DIRS[r]) for r in range(2)]
    lru_post_ins = [(to2(hs[0]), 0), (to2(hs[1]), 0), (proj0, 4)]
    (ybm,) = _pw_fwd("l0_lru_post", _f_lru_post, lru_post_ins, [], [BF16], 1024, 1)
    w_out0 = w["even_w_out"][0]
    x1 = _mm("l0_out_a", ya, w_out0[:1024], "nn", res=x0)
    x1 = _mm("l0_out_b", ybm, w_out0[1024:], "nn", res=x1)
    nmlp0 = row(w["norm_mlp"][0])
    x2, mlp0, got = _mlp_fwd("l0_mlp", x1, nmlp0, w["mlp_w1"][0], w["mlp_w2"][0], carry=carries.get("odd"))
    if late:
        w = {**w, **arrived("odd", got)}

    w_in1 = w["odd_w_in"][0]
    nmix1 = row(w["norm_mix"][1])
    (h1,) = _pw_fwd("l1_norm", _f_norm, [(x2, 0)], [(nmix1, 0)], [BF16], 1024, 1)
    proj1 = _mm("l1_proj", h1, w_in1, "nn")
    proj1_3 = to3(proj1)
    lb0, lb1 = row(w["hgrn_lb_logits"][0]), row(w["hgrn_lb_logits"][1])
    gla = [_gla_fwd(proj1_3, lb0, lb1, r) for r in DIRS]
    hnw = row(w["hgrn_norm_w"][0])
    hpost_ins = [(to2(gla[0][0]), 0), (to2(gla[1][0]), 0), (proj1, 4)]
    (yo,) = _pw_fwd("l1_hgrn_post", _f_hgrn_post, hpost_ins, [(hnw, 0)], [BF16], 1024, 1, groups=HGRN_HEADS)
    w_out1 = w["odd_w_out"][0]
    x3_ = _mm("l1_out", yo, w_out1, "nn", res=x2)
    nmlp1 = row(w["norm_mlp"][1])
    x4, mlp1, _ = _mlp_fwd("l1_mlp", x3_, nmlp1, w["mlp_w1"][1], w["mlp_w2"][1])

    dx4, dnf, loss = _loss_head(x4, tgt, row(w["norm_final"]))
    grads["norm_final"] = dnf.reshape(-1)

    dx3, dw1_1, dw2_1, dnmlp1, _ = _mlp_bwd("l1_mlp", x3_, nmlp1, w["mlp_w1"][1], w["mlp_w2"][1], mlp1, dx4)
    big = {"odd_w_out": _mm("l1_dwout", yo, dx3, "tn").reshape(4, 256, 1024)}
    dyo = _mm("l1_dyo", dx3, w_out1, "nt")
    (do, dgate1), (dhnw,) = _pw_bwd("l1_hgrn_post_b", _f_hgrn_post, hpost_ins, [(hnw, 0)], [dyo], 1024, 1, [0, 2],
                                    out_dtypes=[F32, BF16], groups=HGRN_HEADS, tm=ROWS_FWD)
    grads["hgrn_norm_w"] = dhnw
    do3 = to3(do)
    gb = [_gla_bwd(proj1_3, lb0, lb1, gla[0][1], do3, False)]
    gb.append(_gla_bwd(proj1_3, lb0, lb1, gla[1][1], do3, True, add_to=(gb[0][0], gb[0][2])))
    grads["hgrn_lb_logits"] = jnp.concatenate([gb[0][3] + gb[1][3], gb[0][4] + gb[1][4]], axis=0)
    dparts1 = [to2(gb[1][0]), to2(gb[0][1]), to2(gb[1][1]), to2(gb[1][2]), dgate1]
    dwin1 = jnp.concatenate([_mm(f"l1_dwin{i}", h1, dp, "tn") for i, dp in enumerate(dparts1)], axis=1)
    big["odd_w_in"] = dwin1.reshape(1024, 4, 1280).transpose(1, 0, 2)
    dx2, dnmix1 = _mm_sum_nt("l1_dh", dparts1, [(w_in1, i) for i in range(5)], norm_bwd=(x2, nmix1, dx3))
    big["mlp_w1_l1"], big["mlp_w2_l1"] = dw1_1, dw2_1.reshape(4, 1024, 1024)
    box = {}

    def mlp0_bwd(carry=None):
        box["mlp0"] = _mlp_bwd("l0_mlp", x1, nmlp0, w["mlp_w1"][0], w["mlp_w2"][0], mlp0, dx2, carry=carry)
        return box["mlp0"][4]

    early_sums = tuple(pair_reduce(EARLY, [big[n] for n in EARLY], mlp0_bwd)) if pair_reduce else tuple(mlp0_bwd())

    dx1, dw1_0, dw2_0, dnmlp0 = box["mlp0"][:4]
    big["mlp_w1_l0"], big["mlp_w2_l0"] = dw1_0, dw2_0.reshape(4, 1024, 1024)
    grads["norm_mlp"] = jnp.concatenate([dnmlp0, dnmlp1], axis=0)
    big["even_w_out"] = jnp.concatenate([_mm("l0_dwout_a", ya, dx1, "tn"), _mm("l0_dwout_b", ybm, dx1, "tn")],
                                        axis=0).reshape(4, 512, 1024)
    dya = _mm("l0_dya", dx1, w_out0[:1024], "nt")
    dyb = _mm("l0_dyb", dx1, w_out0[1024:], "nt")
    (dh, dgate0), _ = _pw_bwd("l0_lru_post_b", _f_lru_post, lru_post_ins, [], [dyb], 1024, 1, [0, 2], out_dtypes=[F32, BF16],
                               tm=ROWS_FWD)
    dh3 = to3(dh)

    def lru0_bwd(carry=None):
        box["lru0"] = _lru_scan_bwd(to3(ab[0][0]), hs[0], dh3, DIRS[0], carry=carry)
        return box["lru0"][2:]

    mid_sums = tuple(pair_reduce(MID, [big[n] for n in MID], lru0_bwd)) if pair_reduce else tuple(lru0_bwd())
    dpre, du_parts, dlru = [], [], {k: [] for k in ("lru_b_a", "lru_b_x", "lru_lambda")}
    for r in range(2):
        g_r, da_r = box["lru0"][:2] if r == 0 else _lru_scan_bwd(to3(ab[r][0]), hs[r], dh3, DIRS[r])
        (dpa, dpx, du_r), (dba, dbx, dlam) = _pw_bwd(f"l0_lru_gates_b{r}", _f_lru_gates, lru_ins[r], lru_par[r],
                                                     [to2(da_r), to2(g_r)], 1024, 1, [0, 1, 2],
                                                     out_dtypes=[BF16, BF16, F32])
        dpre += [dpa, dpx]
        du_parts.append(du_r)
        dlru["lru_b_a"].append(dba)
        dlru["lru_b_x"].append(dbx)
        dlru["lru_lambda"].append(dlam)
    for k, v in dlru.items():
        grads[k] = jnp.concatenate(v, axis=0)[None]
    dwg = [_diag_blocks(_mm(f"l0_dwgate{i}", u_lru, dp, "tn")) for i, dp in enumerate(dpre)]
    grads["lru_w_a"] = jnp.stack([dwg[0], dwg[2]])[None]
    grads["lru_w_x"] = jnp.stack([dwg[1], dwg[3]])[None]
    du = _mm_sum_nt("l0_du", dpre, [(wg, 0) for wg in w_gates], add=du_parts)
    (dy, dxs_skip, dz), (ddskip, dsnw) = _pw_bwd("l0_ssd_post_b", _f_ssd_post, ssd_ins, [(dskip, 0), (snw, 0)], [dya],
                                                 1024, 1, [0, 2, 3], out_dtypes=[F32, F32, BF16], groups=SSD_GROUPS)
    grads["ssd_d"] = ddskip.reshape(SSD_HEADS, SSD_HEADDIM).sum(axis=1)[None]
    grads["ssd_norm_w"] = dsnw
    dy3 = to3(dy)
    sb0 = _ssd_bwd(xbc3, dt3, alog, ssd[0][1], dy3, False, scatter=early_sums)
    sb1 = _ssd_bwd(xbc3, dt3, alog, ssd[1][1], dy3, True, add_to=(sb0[0], to3(dxs_skip), sb0[1], sb0[2]), scatter=mid_sums)
    grads["ssd_a_log"] = (sb0[3] + sb1[3])[:, :32].reshape(1, 2, 16)
    ddt = to2(sb1[2])
    (ddt_raw,), (ddtb,) = _pw_bwd("l0_dt_b", _f_softplus, [(dt_raw, 0)], [(dt_bias, 0)], [ddt], 128, 1, [0])
    grads["ssd_dt_bias"] = ddtb[:, :32].reshape(1, 2, 16)
    cb = [_conv_bwd(sb1[0], to3(proj0), conv_w, 0, conv2), _conv_bwd(sb1[1], to3(proj0), conv_w, 1, conv2),
          _conv_bwd(to3(du), to3(proj0), conv_w, 2)]
    dcw = jnp.concatenate([c_[1] for c_ in cb], axis=1)
    grads["even_conv_w"] = dcw[:4][None]
    grads["even_conv_b"] = dcw[4:5]
    dparts0 = [to2(c_[0]) for c_ in cb] + [dz, dgate0]
    dwin0 = [_mm(f"l0_dwin{i}", h0, dp, "tn") for i, dp in enumerate(dparts0)]
    big["even_w_in"] = _split_in0(dwin0, _mm("l0_dwin_dt", h0, ddt_raw, "tn"))
    dx0, dnmix0 = _mm_sum_nt("l0_dh", dparts0 + [ddt_raw], [(w_main0, i) for i in range(5)] + [(w_dt0, 0)],
                             norm_bwd=(x0, nmix0, dx1))
    grads["norm_mix"] = jnp.concatenate([dnmix0, dnmix1], axis=0)
    return loss, dx0.reshape(nb, s, d), grads, big, (early_sums + mid_sums, sb0[4:] + sb1[4:])


ANY = pl.BlockSpec(memory_space=pl.ANY)


def _place():
    return lax.axis_index("x"), lax.axis_index("y"), lax.axis_index("c")


def _remote(src, dst, send_sems, recv_sems, k, to):
    return pltpu.make_async_remote_copy(src_ref=src, dst_ref=dst, send_sem=send_sems.at[k], recv_sem=recv_sems.at[k],
                                        device_id=to, device_id_type=MESH)


def _gather_start(x_refs, out_refs, send_sems, recv_sems, finish=False):
    n = len(x_refs)
    halves = [r.shape[0] // 2 for r in x_refs]
    x, y, c = _place()
    sibling = (x, y, 1 - c)
    chips = [(1 - x, y), (x, 1 - y), (1 - x, 1 - y)]

    def blk(t, px, py, hc):
        return out_refs[t].at[2 * px + py, pl.ds(hc * halves[t], halves[t]), :]

    def src(t):
        return x_refs[t].at[pl.ds(c * halves[t], halves[t]), :]

    first = [_remote(src(t), blk(t, x, y, c), send_sems, recv_sems, 6 * t + j, (*chip, c))
             for t in range(n) for j, chip in enumerate(chips)]
    if not finish:
        for cp in first:
            cp.start()
        return
    passed = []
    for t in range(n):
        for j, chip in enumerate(chips):
            _remote(src(t), blk(t, *chip, c), send_sems, recv_sems, 6 * t + j, (*chip, c)).wait_recv()
            cp = _remote(blk(t, *chip, c), blk(t, *chip, c), send_sems, recv_sems, 6 * t + 3 + j, sibling)
            cp.start()
            passed.append(cp)
    for t in range(n):
        for j, chip in enumerate(chips):
            _remote(src(t), blk(t, *chip, 1 - c), send_sems, recv_sems, 6 * t + 3 + j, sibling).wait_recv()
    for cp in first + passed:
        cp.wait_send()


_gather_finish = functools.partial(_gather_start, finish=True)


def _gather_carry(shards):
    n = len(shards)
    return (list(shards), [jax.ShapeDtypeStruct((4,) + s.shape, s.dtype) for s in shards],
            [pltpu.SemaphoreType.DMA((6 * n,)), pltpu.SemaphoreType.DMA((6 * n,))], _gather_start, _gather_finish)


def _gather_chips(shards):
    n = len(shards)
    srcs, shapes, scratch, start, finish = _gather_carry(shards)

    def body(*refs):
        start(refs[:n], refs[n:2 * n], *refs[2 * n:])
        finish(refs[:n], refs[n:2 * n], *refs[2 * n:])

    return _pcall(body, name="gather_weights", in_specs=[ANY] * n, out_specs=(ANY,) * n, out_shape=tuple(shapes),
                  scratch_shapes=scratch, compiler_params=_params())(*shards)


def _pair_swap_start(g_refs, land_refs, send_sems, recv_sems, finish=False):
    x, y, c = _place()
    cps = []
    for t, g in enumerate(g_refs):
        half = g.shape[1] // 2
        cps += [_remote(g.at[j, pl.ds((1 - c) * half, half), :], land_refs[t].at[j], send_sems, recv_sems, 4 * t + j,
                        (x, y, 1 - c)) for j in range(4)]
    for cp in cps:
        cp.wait() if finish else cp.start()


_pair_swap_finish = functools.partial(_pair_swap_start, finish=True)


def _pair_swap_carry(gps):
    n = len(gps)
    return (list(gps), [jax.ShapeDtypeStruct((4, g.shape[1] // 2, g.shape[2]), F32) for g in gps],
            [pltpu.SemaphoreType.DMA((4 * n,)), pltpu.SemaphoreType.DMA((4 * n,))], _pair_swap_start, _pair_swap_finish)


def _pair_swap(name, gps):
    n = len(gps)
    srcs, shapes, scratch, start, finish = _pair_swap_carry(gps)

    def body(*refs):
        start(refs[:n], refs[n:2 * n], *refs[2 * n:])
        finish(refs[:n], refs[n:2 * n], *refs[2 * n:])

    return _pcall(body, name=f"pair_swap_{name}", in_specs=[ANY] * n, out_specs=(ANY,) * n, out_shape=tuple(shapes),
                  scratch_shapes=scratch, compiler_params=_params())(*gps)


def _pair_add(name, gp, land, cidx):
    _, half, cols = land.shape
    tr = _tile(half, 512)
    nh = half // tr

    def body(c_ref, g_ref, l_ref, o_ref):
        o_ref[...] = (g_ref[...] + l_ref[...]).astype(o_ref.dtype)

    grid_spec = pltpu.PrefetchScalarGridSpec(
        num_scalar_prefetch=1, grid=(4, nh),
        in_specs=[pl.BlockSpec((None, tr, cols), lambda j, i, c: (j, c[0] * nh + i, 0)),
                  pl.BlockSpec((None, tr, cols), lambda j, i, c: (j, i, 0))],
        out_specs=pl.BlockSpec((None, tr, cols), lambda j, i, c: (j, i, 0)))
    return _pcall(body, name=f"pair_add_{name}", grid_spec=grid_spec, out_shape=jax.ShapeDtypeStruct((4, half, cols), BF16),
                  compiler_params=_params())(cidx, gp, land)


def _scatter_copies(s_refs, land_refs, send_sems, recv_sems):
    x, y, c = _place()
    me = 2 * x + y
    chips = [(1 - x, y), (x, 1 - y), (1 - x, 1 - y)]
    pairs = [(t, j, px, py) for t in range(len(s_refs)) for j, (px, py) in enumerate(chips)]
    sends = [_remote(s_refs[t].at[2 * px + py], land_refs[t].at[me], send_sems, recv_sems, 3 * t + j, (px, py, c))
             for t, j, px, py in pairs]
    arrivals = [_remote(s_refs[t].at[me], land_refs[t].at[2 * px + py], send_sems, recv_sems, 3 * t + j, (px, py, c))
                for t, j, px, py in pairs]
    return sends, arrivals


def _scatter_scratch(n):
    return [pltpu.SemaphoreType.DMA((3 * n,)), pltpu.SemaphoreType.DMA((3 * n,))]


def _chip_scatter(name, css):
    n = len(css)

    def body(*refs):
        sends, arrivals = _scatter_copies(refs[:n], refs[n:2 * n], *refs[2 * n:])
        for cp in sends:
            cp.start()
        for cp in arrivals:
            cp.wait_recv()
        for cp in sends:
            cp.wait_send()

    return _pcall(body, name=f"chip_scatter_{name}", in_specs=[ANY] * n, out_specs=(ANY,) * n,
                  out_shape=tuple(jax.ShapeDtypeStruct(s.shape, s.dtype) for s in css),
                  scratch_shapes=_scatter_scratch(n), compiler_params=_params())(*css)


def _chip_sum(name, land):
    _, half, cols = land.shape
    tr = _tile(half, 512)

    def body(l_ref, o_ref):
        o_ref[...] = ((l_ref[0].astype(F32) + l_ref[1].astype(F32)) + l_ref[2].astype(F32)) + l_ref[3].astype(F32)

    return _pcall(body, name=f"chip_sum_{name}", grid=(half // tr,),
                  in_specs=[pl.BlockSpec((4, tr, cols), lambda i: (0, i, 0))],
                  out_specs=pl.BlockSpec((tr, cols), lambda i: (i, 0)),
                  out_shape=jax.ShapeDtypeStruct((half, cols), F32), compiler_params=_params())(land)


def _pair_join(reds):
    n = len(reds)

    def body(*refs):
        r_refs, out_refs = refs[:n], refs[n:2 * n]
        send_sems, recv_sems = refs[2 * n:]
        x, y, c = _place()
        cps = [_remote(r_refs[t], out_refs[t].at[c], send_sems, recv_sems, t, (x, y, 1 - c)) for t in range(n)]
        for cp in cps:
            cp.start()
        for t in range(n):
            _remote(r_refs[t], out_refs[t].at[1 - c], send_sems, recv_sems, t, (x, y, 1 - c)).wait_recv()
        for cp in cps:
            cp.wait_send()

    return _pcall(body, name="grad_pair_join", in_specs=[ANY] * n, out_specs=(ANY,) * n,
                  out_shape=tuple(jax.ShapeDtypeStruct((2,) + r.shape, F32) for r in reds),
                  scratch_shapes=[pltpu.SemaphoreType.DMA((n,)), pltpu.SemaphoreType.DMA((n,))],
                  compiler_params=_params())(*reds)


def _adamw(name, g, w, m, v):
    rows, cols = g.shape
    tr = _tile(rows, 512)

    def body(g_ref, w_ref, m_ref, v_ref, d_ref, mo_ref, vo_ref):
        gv = g_ref[...]
        mn = ADAM_B1 * m_ref[...] + (1.0 - ADAM_B1) * gv
        vn = ADAM_B2 * v_ref[...] + (1.0 - ADAM_B2) * jnp.square(gv)
        m_hat = mn / (1.0 - ADAM_B1 ** ADAM_STEP)
        v_hat = vn / (1.0 - ADAM_B2 ** ADAM_STEP)
        d_ref[...] = -ADAM_LR * (m_hat / (jnp.sqrt(v_hat) + ADAM_EPS) + ADAM_WD * w_ref[...])
        mo_ref[...] = mn
        vo_ref[...] = vn

    blk = pl.BlockSpec((tr, cols), lambda i: (i, 0))
    shp = jax.ShapeDtypeStruct((rows, cols), F32)
    return _pcall(body, name=f"adamw_{name}", grid=(rows // tr,), in_specs=[blk] * 4, out_specs=(blk,) * 3,
                  out_shape=(shp,) * 3, compiler_params=_params())(g, w, m, v)


def _pack(pieces, rows, dtype):
    flat = jnp.concatenate([p.reshape(-1).astype(dtype) for p in pieces])
    return jnp.pad(flat, (0, rows * PACK_COLS - flat.shape[0])).reshape(rows, PACK_COLS)


def _unpack(pack, shapes):
    flat = pack.reshape(-1)
    out, off = [], 0
    for shp in shapes:
        n = math.prod(shp)
        out.append(flat[off:off + n].reshape(shp))
        off += n
    return out


def _shard_of(full, axis, j):
    n = full.shape[axis] // 4
    return lax.slice_in_dim(full, j * n, (j + 1) * n, axis=axis)


def kernel(x, even_w_in, even_conv_w, even_conv_b, ssd_a_log, ssd_dt_bias, ssd_d, ssd_norm_w, lru_w_a, lru_b_a, lru_w_x, lru_b_x, lru_lambda, even_w_out, odd_w_in, hgrn_lb_logits, hgrn_norm_w, odd_w_out, norm_mix, norm_mlp, mlp_w1, mlp_w2, norm_final, loss_target, m_even_w_in, m_even_conv_w, m_even_conv_b, m_ssd_a_log, m_ssd_dt_bias, m_ssd_d, m_ssd_norm_w, m_lru_w_a, m_lru_b_a, m_lru_w_x, m_lru_b_x, m_lru_lambda, m_even_w_out, m_odd_w_in, m_hgrn_lb_logits, m_hgrn_norm_w, m_odd_w_out, m_norm_mix, m_norm_mlp, m_mlp_w1, m_mlp_w2, m_norm_final, v_even_w_in, v_even_conv_w, v_even_conv_b, v_ssd_a_log, v_ssd_dt_bias, v_ssd_d, v_ssd_norm_w, v_lru_w_a, v_lru_b_a, v_lru_w_x, v_lru_b_x, v_lru_lambda, v_even_w_out, v_odd_w_in, v_hgrn_lb_logits, v_hgrn_norm_w, v_odd_w_out, v_norm_mix, v_norm_mlp, v_mlp_w1, v_mlp_w2, v_norm_final):
    names = [n for n, _, _, _ in WEIGHTS]
    w_loc = dict(zip(names, (even_w_in, even_conv_w, even_conv_b, ssd_a_log, ssd_dt_bias, ssd_d, ssd_norm_w, lru_w_a, lru_b_a, lru_w_x, lru_b_x, lru_lambda, even_w_out, odd_w_in, hgrn_lb_logits, hgrn_norm_w, odd_w_out, norm_mix, norm_mlp, mlp_w1, mlp_w2, norm_final)))
    m_loc = dict(zip(names, (m_even_w_in, m_even_conv_w, m_even_conv_b, m_ssd_a_log, m_ssd_dt_bias, m_ssd_d, m_ssd_norm_w, m_lru_w_a, m_lru_b_a, m_lru_w_x, m_lru_b_x, m_lru_lambda, m_even_w_out, m_odd_w_in, m_hgrn_lb_logits, m_hgrn_norm_w, m_odd_w_out, m_norm_mix, m_norm_mlp, m_mlp_w1, m_mlp_w2, m_norm_final)))
    v_loc = dict(zip(names, (v_even_w_in, v_even_conv_w, v_even_conv_b, v_ssd_a_log, v_ssd_dt_bias, v_ssd_d, v_ssd_norm_w, v_lru_w_a, v_lru_b_a, v_lru_w_x, v_lru_b_x, v_lru_lambda, v_even_w_out, v_odd_w_in, v_hgrn_lb_logits, v_hgrn_norm_w, v_odd_w_out, v_norm_mix, v_norm_mlp, v_mlp_w1, v_mlp_w2, v_norm_final)))
    spec = {n: (blk, full, ax) for n, blk, full, ax in WEIGHTS}

    small = [n for n in names if n not in BIG]
    two_d = lambda n, v: v.reshape(BIG_2D[n])

    me = 2 * lax.axis_index("x") + lax.axis_index("y")
    cc = lax.axis_index("c")
    put = lambda whole, part, k: lax.dynamic_update_slice_in_dim(whole, part[None], k, axis=0)
    own = {n: two_d(n, w_loc[n]).astype(BF16) for n in BIG}
    own["small"] = _pack([w_loc[n] for n in SMALL_SHARDED], 16, F32)
    fill = lambda got, keys: [put(g, own[k], me) for g, k in zip(got, keys)]
    first = ("even_w_in", "small")
    g_in0, g_small = fill(_gather_chips([own[k] for k in first]), first)
    w_main0, w_dt0 = _assemble_in0(g_in0)
    w_full = {n: w_loc[n] for n in names if spec[n][2] is None}
    shards = [_unpack(g_small[j], [spec[n][0] for n in SMALL_SHARDED]) for j in range(4)]
    for n in ("mlp_w1", "mlp_w2"):
        for l in range(2):
            own[f"{n}_l{l}"] = w_loc[n][l].astype(BF16)
    layers = lambda n: (f"{n}_l0", f"{n}_l1")
    carries = {"mlp_w1": _gather_carry([own[k] for k in layers("mlp_w1") + ("even_w_out",)]),
               "mlp_w2": _gather_carry([own[k] for k in layers("mlp_w2")]),
               "odd": _gather_carry([own["odd_w_in"], own["odd_w_out"]])}

    def arrived(key, got):
        if key == "odd":
            g_in1, g_out1 = fill(got, ("odd_w_in", "odd_w_out"))
            return {"odd_w_in": jnp.concatenate([g_in1[j] for j in range(4)], axis=1)[None],
                    "odd_w_out": g_out1.reshape(1, 1024, 1024)}
        g = fill(got[:2], layers(key))
        if key == "mlp_w2":
            return {key: [v.reshape(4096, 1024) for v in g]}
        (g_out0,) = fill(got[2:], ("even_w_out",))
        return {key: g, "even_w_out": g_out0.reshape(1, 2048, 1024)}

    for i, n in enumerate(SMALL_SHARDED):
        w_full[n] = jnp.concatenate([shards[j][i] for j in range(4)], axis=spec[n][2])

    cidx = cc.astype(jnp.int32).reshape(1)

    def pair_reduce(tags, tensors, run=None):
        lands = run(_pair_swap_carry(tensors)) if run else _pair_swap(tags[0], tensors)
        return [_pair_add(tag, g, land, cidx) for tag, g, land in zip(tags, tensors, lands)]

    loss_vec, grad_x, grads, big, (early_sums, early_landed) = _local_step(
        x, loss_target, w_full, w_main0, w_dt0, pair_reduce, (carries, arrived))
    loss = lax.psum(loss_vec[0, 0], ("x", "y", "c"))

    def dest_pack(j):
        return _pack([grads[n].reshape(spec[n][1]) if spec[n][2] is None else _shard_of(grads[n].reshape(spec[n][1]), spec[n][2], j)
                      for n in small], SMALL_ROWS, F32)

    late_tags = LATE + ("small",)
    late_sums = pair_reduce(late_tags, [big[n] for n in LATE] + [jnp.stack([dest_pack(j) for j in range(4)])])
    tags = EARLY + MID + late_tags
    chip_sums = list(early_sums) + late_sums
    landed = [put(land, lax.dynamic_index_in_dim(cs, me, axis=0, keepdims=False), me)
              for land, cs in zip(list(early_landed) + list(_chip_scatter("late", late_sums)), chip_sums)]
    halves = [_chip_sum(tag, land) for tag, land in zip(tags, landed)]
    red = {tag: put(r, h, cc).reshape(-1, r.shape[-1]) for tag, r, h in zip(tags, _pair_join(halves), halves)}
    for n in ("mlp_w1", "mlp_w2"):
        red[n] = jnp.concatenate([red[n + "_l0"], red[n + "_l1"]], axis=0)

    outs = {}
    for n, g in ((n, red[n]) for n in BIG):
        res = (g, *_adamw(n, g, two_d(n, w_loc[n]), two_d(n, m_loc[n]), two_d(n, v_loc[n])))
        outs[n] = [r.reshape(spec[n][0]) for r in res]
    blocks = [spec[n][0] for n in small]
    wp, mp, vp = (_pack([src[n] for n in small], SMALL_ROWS, F32) for src in (w_loc, m_loc, v_loc))
    res = (red["small"], *_adamw("small", red["small"], wp, mp, vp))
    unpacked = [_unpack(r, blocks) for r in res]
    for i, n in enumerate(small):
        outs[n] = [u[i] for u in unpacked]
    return (loss, grad_x, *[outs[n][k] for k in range(4) for n in names])
```

```python
import functools
import math

import jax
import jax.numpy as jnp
from jax import lax
from jax.experimental import pallas as pl
from jax.experimental.pallas import tpu as pltpu

F32 = jnp.float32
BF16 = jnp.bfloat16
MXU_DTYPE = jnp.bfloat16
MESH = pl.DeviceIdType.MESH

D_MODEL = 1024
EPS = 1e-6
SSD_HEADS = 16
SSD_HEADDIM = 64
HEAD_SHIFT = 6
SSD_GROUPS = 4
SSD_STATE = 128
SSD_CHUNK = 128
LRU_C = 8.0
LRU_ROWS = 256
HGRN_HEADS = 8
HGRN_HEADDIM = 128
HGRN_SUB = 32
HGRN_SUB_SHIFT = 5
HGRN_BLOCK = 128
HGRN_SCALE = HGRN_HEADDIM ** -0.5
CONV_ROWS = 512
ROWS_FWD = 512
ROWS_BWD = 256

ADAM_LR = 0.001
ADAM_B1 = 0.9
ADAM_B2 = 0.999
ADAM_EPS = 1e-08
ADAM_WD = 0.01
ADAM_STEP = 10

VMEM_LIMIT = 56 * 1024 * 1024
PACK_COLS = 1024
SMALL_ROWS = 288

WEIGHTS = (
    ("even_w_in", (1, 1024, 1288), (1, 1024, 5152), 2),
    ("even_conv_w", (1, 4, 768), (1, 4, 3072), 2),
    ("even_conv_b", (1, 3072), (1, 3072), None),
    ("ssd_a_log", (1, 2, 16), (1, 2, 16), None),
    ("ssd_dt_bias", (1, 2, 16), (1, 2, 16), None),
    ("ssd_d", (1, 16), (1, 16), None),
    ("ssd_norm_w", (1, 1024), (1, 1024), None),
    ("lru_w_a", (1, 2, 16, 64, 64), (1, 2, 16, 64, 64), None),
    ("lru_b_a", (1, 2, 256), (1, 2, 1024), 2),
    ("lru_w_x", (1, 2, 16, 64, 64), (1, 2, 16, 64, 64), None),
    ("lru_b_x", (1, 2, 256), (1, 2, 1024), 2),
    ("lru_lambda", (1, 2, 256), (1, 2, 1024), 2),
    ("even_w_out", (1, 512, 1024), (1, 2048, 1024), 1),
    ("odd_w_in", (1, 1024, 1280), (1, 1024, 5120), 2),
    ("hgrn_lb_logits", (2, 1024), (2, 1024), None),
    ("hgrn_norm_w", (1, 256), (1, 1024), 1),
    ("odd_w_out", (1, 256, 1024), (1, 1024, 1024), 1),
    ("norm_mix", (2, 1024), (2, 1024), None),
    ("norm_mlp", (2, 1024), (2, 1024), None),
    ("mlp_w1", (2, 1024, 1024), (2, 1024, 4096), 2),
    ("mlp_w2", (2, 1024, 1024), (2, 4096, 1024), 1),
    ("norm_final", (1024,), (1024,), None),
)
BIG = ("even_w_in", "even_w_out", "odd_w_in", "odd_w_out", "mlp_w1", "mlp_w2")
BIG_2D = {"even_w_in": (1024, 1288), "even_w_out": (512, 1024), "odd_w_in": (1024, 1280), "odd_w_out": (256, 1024),
          "mlp_w1": (2048, 1024), "mlp_w2": (2048, 1024)}
SMALL_SHARDED = ("even_conv_w", "lru_b_a", "lru_b_x", "lru_lambda", "hgrn_norm_w")


def _pcall(body, carry=None, **kw):
    if carry is not None:
        srcs, shapes, scratch, start, finish = carry
        grid, inner = kw["grid"], body
        as_tuple = lambda v: tuple(v) if isinstance(v, (tuple, list)) else (v,)
        out_specs, out_shape, own_scratch = as_tuple(kw["out_specs"]), as_tuple(kw["out_shape"]), list(kw.get("scratch_shapes", ()))
        a = len(kw["in_specs"])
        b = a + len(srcs)
        c = b + len(out_specs)
        d = c + len(shapes)
        e = d + len(own_scratch)

        def body(*refs):
            ids = [pl.program_id(ax) for ax in range(len(grid))]
            first = functools.reduce(jnp.logical_and, [i == 0 for i in ids])
            last = functools.reduce(jnp.logical_and, [i == g - 1 for i, g in zip(ids, grid)])
            pl.when(first)(lambda: start(refs[a:b], refs[c:d], *refs[e:]))
            inner(*refs[:a], *refs[b:c], *refs[d:e])
            pl.when(last)(lambda: finish(refs[a:b], refs[c:d], *refs[e:]))

        kw = dict(kw, in_specs=list(kw["in_specs"]) + [ANY] * len(srcs), out_specs=out_specs + (ANY,) * len(shapes),
                  out_shape=out_shape + tuple(shapes), scratch_shapes=own_scratch + list(scratch))
    return pl.pallas_call(body, **kw)


def _params(**kw):
    return pltpu.CompilerParams(vmem_limit_bytes=VMEM_LIMIT, **kw)


def _tile(n, pref):
    if n <= pref:
        return n
    t = (pref // 128) * 128
    while n % t:
        t -= 128
    return t


def _dot(a, b, dims=(((1,), (0,)), ((), ()))):
    return lax.dot_general(a, b, dims, preferred_element_type=F32)


_NN = (((1,), (0,)), ((), ()))
_NT = (((1,), (1,)), ((), ()))
_TN = (((0,), (0,)), ((), ()))


def _mx(v):
    return v.astype(MXU_DTYPE)


def _dot01(a, b, dims=_NN, *, split, terms):
    acc, rest = None, (a if split == "a" else b)
    for _ in range(terms):
        piece = _mx(rest)
        part = _dot(piece, _mx(b), dims) if split == "a" else _dot(_mx(a), piece, dims)
        acc = part if acc is None else acc + part
        rest = rest - piece.astype(F32)
    return acc


def _mm(name, a, b, mode, *, out_dtype=F32, res=None, relu2=False, relu2_of=None, col_shards=1, carry=None):
    shards = b.shape[0] if b.ndim == 3 else 0
    b2 = b.shape[1:] if shards else b.shape
    if mode == "nn":
        (m, kk), n = a.shape, b2[1] * max(shards, 1)
    elif mode == "nt":
        (m, kk), n = a.shape, b2[0]
    else:
        (kk, m), (_, n) = a.shape, b.shape
    assert res is None or relu2_of is None
    tk_pref = 1024
    if mode == "tn" and a.dtype.itemsize == 2 and b.dtype.itemsize == 2:
        tk_pref = 2048
    tm, tn, tk = _tile(m, 1024), _tile(n // col_shards, 1024), _tile(kk, tk_pref)
    nk = kk // tk
    dims = {"nn": _NN, "nt": _NT, "tn": _TN}[mode]
    a_spec = pl.BlockSpec((tk, tm), lambda i, j, k: (k, i)) if mode == "tn" else pl.BlockSpec((tm, tk), lambda i, j, k: (i, k))
    b_spec = pl.BlockSpec((tn, tk), lambda i, j, k: (j, k)) if mode == "nt" else pl.BlockSpec((tk, tn), lambda i, j, k: (k, j))
    if shards and mode == "nn":
        assert tn == b2[1]
        b_spec = pl.BlockSpec((None, tk, tn), lambda i, j, k: (j, k, 0))
    o_spec = pl.BlockSpec((tm, tn), lambda i, j, k: (i, j))
    o_shape = (m, n)
    if col_shards > 1:
        assert tn * col_shards == n and res is None and not relu2
        o_spec = pl.BlockSpec((None, tm, tn), lambda i, j, k: (j, i, 0))
        o_shape = (col_shards, m, tn)
    extra = res if res is not None else relu2_of
    has_res = extra is not None

    def body(*refs):
        a_ref, b_ref = refs[0], refs[1]
        res_ref = refs[2] if has_res else None
        outs = refs[2 + has_res:2 + has_res + 1 + relu2]

        def finish(r):
            if res is not None:
                r = r + res_ref[...]
            if relu2_of is not None:
                r = r * (2.0 * jnp.maximum(res_ref[...].astype(F32), 0.0))
            if relu2:
                outs[0][...] = r.astype(outs[0].dtype)
                outs[1][...] = jnp.square(jnp.maximum(r, 0.0)).astype(outs[1].dtype)
            else:
                outs[0][...] = r.astype(outs[0].dtype)

        prod = _dot(_mx(a_ref[...]), _mx(b_ref[...]), dims)
        if nk == 1:
            finish(prod)
            return
        acc = refs[-1]
        k = pl.program_id(2)

        @pl.when(k == 0)
        def _():
            acc[...] = prod

        @pl.when(k > 0)
        def _():
            acc[...] += prod

        @pl.when(k == nk - 1)
        def _():
            finish(acc[...])

    in_specs = [a_spec, b_spec] + ([o_spec] if has_res else [])
    if relu2:
        out_shape = (jax.ShapeDtypeStruct((m, n), BF16), jax.ShapeDtypeStruct((m, n), BF16))
        out_specs = (o_spec, o_spec)
    else:
        out_shape = jax.ShapeDtypeStruct(o_shape, out_dtype)
        out_specs = o_spec
    args = (a, b) + ((extra,) if has_res else ()) + (tuple(carry[0]) if carry else ())
    return _pcall(body, carry=carry, name=name, grid=(m // tm, n // tn, nk), in_specs=in_specs, out_specs=out_specs,
                  out_shape=out_shape, scratch_shapes=[pltpu.VMEM((tm, tn), F32)] if nk > 1 else [],
                  compiler_params=_params())(*args)


def _mm_sum_nt(name, parts, wblocks, norm_bwd=None, add=()):
    parts = [p if isinstance(p, tuple) else (p, 0, p.shape[1]) for p in parts]
    m, npart = parts[0][0].shape[0], len(parts)
    n = wblocks[0][0].shape[-2]
    tm, tn = _tile(m, 512), _tile(n, 1024)
    assert norm_bwd is None or tn == n

    def body(*refs):
        acc = _dot(_mx(refs[0][...]), _mx(refs[npart][...]), _NT)
        for k in range(1, npart):
            acc = acc + _dot(_mx(refs[k][...]), _mx(refs[npart + k][...]), _NT)
        if norm_bwd is None:
            for r in refs[2 * npart:-1]:
                acc = acc + r[...]
            refs[-1][...] = acc
            return
        x_ref, g_ref, res_ref, dx_ref, dg_ref = refs[2 * npart:]
        _, vjp = jax.vjp(_f_norm, x_ref[...], g_ref[...])
        dx, dg = vjp((acc,))
        dx_ref[...] = dx + res_ref[...]

        @pl.when(pl.program_id(0) == 0)
        def _():
            dg_ref[...] = jnp.zeros_like(dg_ref)

        dg_ref[...] += dg

    row = pl.BlockSpec((tm, tn), lambda i, j: (i, j))
    vec = pl.BlockSpec((1, tn), lambda i, j: (0, j))
    in_specs = [pl.BlockSpec((tm, wd), lambda i, j, cb=cb: (i, cb)) for _, cb, wd in parts]
    for (_, _, wd), (w, cb) in zip(parts, wblocks):
        in_specs.append(pl.BlockSpec((None, tn, wd), lambda i, j, cb=cb: (cb, j, 0)) if w.ndim == 3
                        else pl.BlockSpec((tn, wd), lambda i, j, cb=cb: (j, cb)))
    args = [p for p, _, _ in parts] + [w for w, _ in wblocks]
    if norm_bwd is None:
        return _pcall(body, name=name, grid=(m // tm, n // tn), in_specs=in_specs + [row] * len(add), out_specs=row,
                      out_shape=jax.ShapeDtypeStruct((m, n), F32), compiler_params=_params())(*args, *add)
    return _pcall(body, name=name, grid=(m // tm, 1), in_specs=in_specs + [row, vec, row], out_specs=(row, vec),
                  out_shape=(jax.ShapeDtypeStruct((m, n), F32), jax.ShapeDtypeStruct((1, n), F32)),
                  compiler_params=_params())(*args, *norm_bwd)


def _pw_fwd(name, f, ins, params, out_dtypes, tc, ncol, tm=ROWS_FWD, groups=1):
    t = ins[0][0].shape[0]
    tm = min(tm, t)
    ni, npar = len(ins), len(params)
    gw = tc // groups

    def body(*refs):
        for g in range(groups):
            sl = slice(g * gw, (g + 1) * gw)
            vals = f(*[r[:, sl].astype(F32) for r in refs[:ni]], *[r[:, sl] for r in refs[ni:ni + npar]])
            for o, v in zip(refs[ni + npar:], vals):
                o[:, sl] = v.astype(o.dtype)

    in_specs = [pl.BlockSpec((tm, tc), lambda j, i, off=off: (i, off + j)) for _, off in ins]
    in_specs += [pl.BlockSpec((1, tc), lambda j, i, off=off: (0, off + j)) for _, off in params]
    out_specs = tuple(pl.BlockSpec((tm, tc), lambda j, i: (i, j)) for _ in out_dtypes)
    out_shape = tuple(jax.ShapeDtypeStruct((t, ncol * tc), d) for d in out_dtypes)
    return _pcall(body, name=name, grid=(ncol, t // tm), in_specs=in_specs, out_specs=out_specs, out_shape=out_shape,
                  compiler_params=_params())(*[a for a, _ in ins], *[p for p, _ in params])


def _pw_bwd(name, f, ins, params, douts, tc, ncol, want, adds=None, tm=ROWS_BWD, out_dtypes=None, groups=1):
    adds = adds or {}
    out_dtypes = out_dtypes or [F32] * len(want)
    t = ins[0][0].shape[0]
    tm = min(tm, t)
    ni, npar, nd, na = len(ins), len(params), len(douts), len(adds)
    add_keys = sorted(adds)
    gw = tc // groups

    def body(*refs):
        in_refs, p_refs = refs[:ni], refs[ni:ni + npar]
        d_refs = refs[ni + npar:ni + npar + nd]
        a_refs = refs[ni + npar + nd:ni + npar + nd + na]
        o_refs = refs[ni + npar + nd + na:]
        for p in range(npar):
            @pl.when(pl.program_id(1) == 0)
            def _(o=o_refs[len(want) + p]):
                o[...] = jnp.zeros_like(o)

        for g in range(groups):
            sl = slice(g * gw, (g + 1) * gw)
            _, vjp = jax.vjp(f, *[r[:, sl].astype(F32) for r in in_refs], *[r[:, sl] for r in p_refs])
            cts = vjp(tuple(d[:, sl].astype(F32) for d in d_refs))
            for o, kidx in zip(o_refs[:len(want)], want):
                v = cts[kidx]
                if kidx in adds:
                    v = v + a_refs[add_keys.index(kidx)][:, sl]
                o[:, sl] = v.astype(o.dtype)
            for p in range(npar):
                o_refs[len(want) + p][:, sl] += cts[ni + p]

    in_specs = [pl.BlockSpec((tm, tc), lambda j, i, off=off: (i, off + j)) for _, off in ins]
    in_specs += [pl.BlockSpec((1, tc), lambda j, i, off=off: (0, off + j)) for _, off in params]
    in_specs += [pl.BlockSpec((tm, tc), lambda j, i: (i, j)) for _ in range(nd + na)]
    out_specs = tuple([pl.BlockSpec((tm, tc), lambda j, i: (i, j)) for _ in want]
                      + [pl.BlockSpec((1, tc), lambda j, i: (0, j)) for _ in params])
    out_shape = tuple([jax.ShapeDtypeStruct((t, ncol * tc), dt) for dt in out_dtypes]
                      + [jax.ShapeDtypeStruct((1, ncol * tc), F32) for _ in params])
    res = _pcall(body, name=name, grid=(ncol, t // tm), in_specs=in_specs, out_specs=out_specs, out_shape=out_shape,
                 compiler_params=_params())(*[a for a, _ in ins], *[p for p, _ in params], *douts, *[adds[k] for k in add_keys])
    return list(res[:len(want)]), list(res[len(want):])


def _rms(x, g):
    return (x * lax.rsqrt(jnp.mean(x * x, axis=-1, keepdims=True) + EPS)) * g


def _f_norm(x, g):
    return (_rms(x, g),)


def _f_softplus(d, b):
    return (jax.nn.softplus(d + b),)


def _f_ssd_post(yf, yb, xs, z, dskip, nw):
    u = (yf + yb + dskip * xs) * jax.nn.silu(z)
    return (_rms(u, nw),)


def _neg_expm1(v):
    t = jnp.tanh(0.5 * v)
    return -2.0 * t / (1.0 - t)


def _f_lru_gates(pre_a, pre_x, u, ba, bx, lam):
    rg = jax.nn.sigmoid(pre_a + ba)
    ig = jax.nn.sigmoid(pre_x + bx)
    log_a = -LRU_C * rg * jax.nn.softplus(-lam)
    return jnp.exp(log_a), jnp.sqrt(_neg_expm1(2.0 * log_a)) * (ig * u)


def _f_lru_post(hf, hb, gate):
    return ((hf + hb) * jax.nn.gelu(gate),)


def _f_hgrn_pre(fr, l0, l1):
    lb = jax.nn.sigmoid(l1 - l0)
    k = (1.0 - lb) * jax.nn.sigmoid(-fr)
    return k, jnp.log1p(-k)


def _f_hgrn_post(of, ob, gate, nw):
    return (_rms(of + ob, nw) * jax.nn.silu(gate),)


def _loss_head(x, tgt, g, tm=ROWS_FWD):
    t, d = x.shape
    tm = min(tm, t)

    def body(x_ref, t_ref, g_ref, dx_ref, dg_ref, loss_ref):
        tv = t_ref[...]

        def lf(xv, gv):
            return 0.5 * jnp.sum(jnp.mean(jnp.square(_rms(xv, gv) - tv), axis=-1))

        val, vjp = jax.vjp(lf, x_ref[...], g_ref[...])
        dx, dg = vjp(jnp.ones((), F32))
        dx_ref[...] = dx

        @pl.when(pl.program_id(0) == 0)
        def _():
            dg_ref[...] = jnp.zeros_like(dg_ref)
            loss_ref[...] = jnp.zeros_like(loss_ref)

        dg_ref[...] += dg
        loss_ref[...] += jnp.full(loss_ref.shape, val, F32)

    row = pl.BlockSpec((tm, d), lambda i: (i, 0))
    vec = pl.BlockSpec((1, d), lambda i: (0, 0))
    return _pcall(body, name="loss_head", grid=(t // tm,), in_specs=[row, row, vec],
                  out_specs=(row, vec, pl.BlockSpec((1, 128), lambda i: (0, 0))),
                  out_shape=(jax.ShapeDtypeStruct((t, d), F32), jax.ShapeDtypeStruct((1, d), F32),
                             jax.ShapeDtypeStruct((1, 128), F32)), compiler_params=_params())(x, tgt, g)


def _shifted(x, d, prev, nxt, first, last):
    r = x.shape[0]
    row = lax.broadcasted_iota(jnp.int32, x.shape, 0)
    if d < 0:
        out = pltpu.roll(x, -d, 0)
        for q in range(-d):
            pv = jnp.where(first, 0.0, prev[8 + d + q:8 + d + q + 1, :])
            out = jnp.where(row == q, pv, out)
        return out
    out = pltpu.roll(x, r - d, 0)
    for q in range(d):
        nv = jnp.where(last, 0.0, nxt[q:q + 1, :])
        out = jnp.where(row == r - d + q, nv, out)
    return out


def _conv_fwd(p3, w, b, col0, ncol, silu, tc=1024):
    nbatch, s, _ = p3.shape
    ts = min(CONV_ROWS, s)
    nblk = s // ts

    def body(x_ref, pv_ref, nx_ref, w_ref, b_ref, o_ref, *act_ref):
        i = pl.program_id(1)
        first, last = i == 0, i == nblk - 1
        x, pv, nx = x_ref[...], pv_ref[...], nx_ref[...]
        wv = w_ref[...]
        out = b_ref[...] + wv[1:2] * x
        out = out + wv[0:1] * _shifted(x, -1, pv, nx, first, last)
        out = out + wv[2:3] * _shifted(x, 1, pv, nx, first, last)
        out = out + wv[3:4] * _shifted(x, 2, pv, nx, first, last)
        o_ref[...] = out
        if silu:
            act_ref[0][...] = jax.nn.silu(out)

    nb8 = s // 8
    cur = pl.BlockSpec((None, ts, tc), lambda n, i, j: (n, i, col0 + j))
    prev = pl.BlockSpec((None, 8, tc), lambda n, i, j: (n, jnp.maximum(i * (ts // 8) - 1, 0), col0 + j))
    nxt = pl.BlockSpec((None, 8, tc), lambda n, i, j: (n, jnp.minimum((i + 1) * (ts // 8), nb8 - 1), col0 + j))
    out = pl.BlockSpec((None, ts, tc), lambda n, i, j: (n, i, j))
    shp = jax.ShapeDtypeStruct((nbatch, s, ncol * tc), F32)
    return _pcall(body, name=f"conv_fwd{col0}", grid=(nbatch, nblk, ncol),
                  in_specs=[cur, prev, nxt, pl.BlockSpec((4, tc), lambda n, i, j: (0, col0 + j)),
                            pl.BlockSpec((1, tc), lambda n, i, j: (0, col0 + j))],
                  out_specs=(out, out) if silu else out, out_shape=(shp, shp) if silu else shp,
                  compiler_params=_params())(p3, p3, p3, w, b)


def _conv_bwd(dc3, p3, w, col, conv3=None):
    nbatch, s, tc = dc3.shape
    ts = min(CONV_ROWS, s)
    nblk = s // ts
    silu = conv3 is not None

    def body(d_ref, dpv_ref, dnx_ref, x_ref, pv_ref, nx_ref, w_ref, *rest):
        n, i = pl.program_id(0), pl.program_id(1)
        first, last = i == 0, i == nblk - 1
        d, dpv, dnx = d_ref[...], dpv_ref[...], dnx_ref[...]
        if silu:
            d, dpv, dnx = [jax.vjp(jax.nn.silu, c_ref[...])[1](t)[0] for c_ref, t in zip(rest[:3], (d, dpv, dnx))]
        dx_ref, dw_ref = rest[3 * silu:]
        x, pv, nx = x_ref[...], pv_ref[...], nx_ref[...]
        wv = w_ref[...]
        dx = wv[1:2] * d
        dx = dx + wv[0:1] * _shifted(d, 1, dpv, dnx, first, last)
        dx = dx + wv[2:3] * _shifted(d, -1, dpv, dnx, first, last)
        dx = dx + wv[3:4] * _shifted(d, -2, dpv, dnx, first, last)
        dx_ref[...] = dx.astype(dx_ref.dtype)

        @pl.when((n == 0) & (i == 0))
        def _():
            dw_ref[...] = jnp.zeros_like(dw_ref)

        dw_ref[0:1, :] += jnp.sum(d * _shifted(x, -1, pv, nx, first, last), axis=0, keepdims=True)
        dw_ref[1:2, :] += jnp.sum(d * x, axis=0, keepdims=True)
        dw_ref[2:3, :] += jnp.sum(d * _shifted(x, 1, pv, nx, first, last), axis=0, keepdims=True)
        dw_ref[3:4, :] += jnp.sum(d * _shifted(x, 2, pv, nx, first, last), axis=0, keepdims=True)
        dw_ref[4:5, :] += jnp.sum(d, axis=0, keepdims=True)

    nb8 = s // 8

    def specs(j):
        cur = pl.BlockSpec((None, ts, tc), lambda n, i: (n, i, j))
        prev = pl.BlockSpec((None, 8, tc), lambda n, i: (n, jnp.maximum(i * (ts // 8) - 1, 0), j))
        nxt = pl.BlockSpec((None, 8, tc), lambda n, i: (n, jnp.minimum((i + 1) * (ts // 8), nb8 - 1), j))
        return [cur, prev, nxt]

    return _pcall(body, name=f"conv_bwd{col}", grid=(nbatch, nblk),
                  in_specs=specs(0) + specs(col) + [pl.BlockSpec((4, tc), lambda n, i: (0, col))] + specs(col) * silu,
                  out_specs=(specs(0)[0], pl.BlockSpec((8, tc), lambda n, i: (0, 0))),
                  out_shape=(jax.ShapeDtypeStruct((nbatch, s, tc), BF16), jax.ShapeDtypeStruct((8, tc), F32)),
                  compiler_params=_params())(dc3, dc3, dc3, p3, p3, p3, w, *([conv3] * 3 * silu))


def _block_scan(coef, inp, reverse):
    r = coef.shape[0]
    row = lax.broadcasted_iota(jnp.int32, coef.shape, 0)
    a, b = coef, inp
    d = 1
    while d < r:
        if reverse:
            keep = row < r - d
            a_sh, b_sh = pltpu.roll(a, r - d, 0), pltpu.roll(b, r - d, 0)
        else:
            keep = row >= d
            a_sh, b_sh = pltpu.roll(a, d, 0), pltpu.roll(b, d, 0)
        b = b + a * jnp.where(keep, b_sh, 0.0)
        a = a * jnp.where(keep, a_sh, 1.0)
        d *= 2
    return a, b


def _lru_scan(a3, b3, reverse):
    nbatch, s, w = a3.shape
    ts = min(LRU_ROWS, s)
    nblk = s // ts
    edge = 0 if reverse else ts - 1

    def body(a_ref, b_ref, h_ref, carry):
        @pl.when(pl.program_id(1) == 0)
        def _():
            carry[...] = jnp.zeros_like(carry)

        ca, hb = _block_scan(a_ref[...], b_ref[...], reverse)
        h = hb + ca * carry[0:1, :]
        h_ref[...] = h
        carry[0:1, :] = h[edge:edge + 1, :]

    blk = pl.BlockSpec((None, ts, w), (lambda n, i: (n, nblk - 1 - i, 0)) if reverse else (lambda n, i: (n, i, 0)))
    return _pcall(body, name=f"lru_scan_r{int(reverse)}", grid=(nbatch, nblk), in_specs=[blk, blk], out_specs=blk,
                  out_shape=jax.ShapeDtypeStruct((nbatch, s, w), F32), scratch_shapes=[pltpu.VMEM((8, w), F32)],
                  compiler_params=_params())(a3, b3)


def _lru_scan_bwd(a3, h3, dh3, reverse, carry=None):
    nbatch, s, w = a3.shape
    ts = min(LRU_ROWS, s)
    nblk = s // ts
    nb8 = s // 8
    tpb = ts // 8

    def body(a_ref, aa_ref, h_ref, hh_ref, dh_ref, g_ref, da_ref, carry):
        i = pl.program_id(1)

        @pl.when(i == 0)
        def _():
            carry[...] = jnp.zeros_like(carry)

        a, h = a_ref[...], h_ref[...]
        row = lax.broadcasted_iota(jnp.int32, a.shape, 0)
        if reverse:
            a_edge = jnp.where(i == 0, 0.0, aa_ref[7:8, :])
            c = jnp.where(row == 0, a_edge, pltpu.roll(a, 1, 0))
            h_edge = jnp.where(i == nblk - 1, 0.0, hh_ref[0:1, :])
            h_sh = jnp.where(row == ts - 1, h_edge, pltpu.roll(h, ts - 1, 0))
        else:
            a_edge = jnp.where(i == 0, 0.0, aa_ref[0:1, :])
            c = jnp.where(row == ts - 1, a_edge, pltpu.roll(a, ts - 1, 0))
            h_edge = jnp.where(i == nblk - 1, 0.0, hh_ref[7:8, :])
            h_sh = jnp.where(row == 0, h_edge, pltpu.roll(h, 1, 0))
        cc, gb = _block_scan(c, dh_ref[...], not reverse)
        g = gb + cc * carry[0:1, :]
        g_ref[...] = g
        carry[0:1, :] = g[ts - 1:ts, :] if reverse else g[0:1, :]
        da_ref[...] = g * h_sh

    if reverse:
        bi = lambda i: i
    else:
        bi = lambda i: nblk - 1 - i
    blk = pl.BlockSpec((None, ts, w), lambda n, i: (n, bi(i), 0))
    before = pl.BlockSpec((None, 8, w), lambda n, i: (n, jnp.maximum(bi(i) * tpb - 1, 0), 0))
    after = pl.BlockSpec((None, 8, w), lambda n, i: (n, jnp.minimum((bi(i) + 1) * tpb, nb8 - 1), 0))
    a_tile, h_tile = (before, after) if reverse else (after, before)
    return _pcall(body, carry=carry, name=f"lru_scan_bwd_r{int(reverse)}", grid=(nbatch, nblk),
                  in_specs=[blk, a_tile, blk, h_tile, blk], out_specs=(blk, blk),
                  out_shape=(jax.ShapeDtypeStruct((nbatch, s, w), F32), jax.ShapeDtypeStruct((nbatch, s, w), F32)),
                  scratch_shapes=[pltpu.VMEM((8, w), F32)],
                  compiler_params=_params())(a3, a3, h3, h3, dh3, *(carry[0] if carry else ()))


def _head_expand(lane0):
    return (jnp.right_shift(lax.broadcasted_iota(jnp.int32, (128, 1024), 1), HEAD_SHIFT) + lane0
            == lax.broadcasted_iota(jnp.int32, (128, 1024), 0)).astype(F32)


def _head_reduce(lane0):
    return (jnp.right_shift(lax.broadcasted_iota(jnp.int32, (1024, 128), 0), HEAD_SHIFT) + lane0
            == lax.broadcasted_iota(jnp.int32, (1024, 128), 1)).astype(F32)


def _time_mask(q, reverse):
    ri = lax.broadcasted_iota(jnp.int32, (q, q), 0)
    ci = lax.broadcasted_iota(jnp.int32, (q, q), 1)
    return (ri <= ci) if reverse else (ri >= ci)


def _ssd_common(xs_ref, bc_ref, dt_ref, al_ref, reverse, lane0):
    q = xs_ref.shape[0]
    edge = 0 if reverse else q - 1
    dt = dt_ref[...]
    a = -jnp.exp(al_ref[...])
    mask = _time_mask(q, reverse)
    expand = _head_expand(lane0)
    cum = _dot01(mask.astype(F32), dt * a, split="b", terms=3)
    cum_x = _dot01(cum, expand, split="a", terms=2)
    dt_x = _dot01(dt, expand, split="a", terms=2)
    last_x = cum_x[edge:edge + 1, :]
    xs = xs_ref[...]
    bc = bc_ref[...]
    return dict(q=q, edge=edge, lane0=lane0, dt=dt, a=a, mask=mask, cum_t=cum.T, cum_x=cum_x, dt_x=dt_x, xs=xs,
                v=xs * dt_x, e_c=jnp.exp(cum_x), w=jnp.exp(last_x - cum_x), e_l=jnp.exp(last_x),
                bm=bc[:, :512], cm=bc[:, 512:])


def _ssd_decay(c, h):
    row = c["lane0"] + h
    seg = c["cum_x"][:, h * SSD_HEADDIM:h * SSD_HEADDIM + 1] - c["cum_t"][row:row + 1, :]
    return jnp.where(c["mask"], jnp.exp(jnp.minimum(seg, 0.0)), 0.0)


def _head_masks():
    lane = jnp.right_shift(lax.broadcasted_iota(jnp.int32, (1, 256), 1), HEAD_SHIFT)
    return [lane == e for e in range(4)]


def _ssd_fwd(xbc3, dt3, alog, reverse, carry=None):
    nbatch, s, _ = xbc3.shape
    q = min(SSD_CHUNK, s)
    nc = s // q
    lane0 = SSD_HEADS * int(reverse)

    def body(xs_ref, bc_ref, dt_ref, al_ref, y_ref, st_ref, st):
        @pl.when(pl.program_id(1) == 0)
        def _():
            st[...] = jnp.zeros_like(st)

        st_ref[...] = st[...]
        c = _ssd_common(xs_ref, bc_ref, dt_ref, al_ref, reverse, lane0)
        hm = _head_masks()
        for g in range(SSD_GROUPS):
            sl = slice(g * 256, (g + 1) * 256)
            cg, bg = _mx(c["cm"][:, g * 128:(g + 1) * 128]), _mx(c["bm"][:, g * 128:(g + 1) * 128])
            cb = _dot(cg, bg, _NT)
            vg = c["v"][:, sl]
            s0 = st[:, sl]
            yg = _dot(cg, _mx(s0)) * c["e_c"][:, sl]
            for e in range(4):
                m = _ssd_decay(c, 4 * g + e) * cb
                yg = yg + _dot(_mx(m), _mx(jnp.where(hm[e], vg, 0.0)))
            y_ref[:, sl] = yg
            st[:, sl] = c["e_l"][:, sl] * s0 + _dot(bg, _mx(vg * c["w"][:, sl]), _TN)

    ck = (lambda i: nc - 1 - i) if reverse else (lambda i: i)
    xs_spec = pl.BlockSpec((None, q, 1024), lambda n, i: (n, ck(i), 0))
    bc_spec = pl.BlockSpec((None, q, 1024), lambda n, i: (n, ck(i), 1))
    dt_spec = pl.BlockSpec((None, q, 128), lambda n, i: (n, ck(i), 0))
    al_spec = pl.BlockSpec((1, 128), lambda n, i: (0, 0))
    st_spec = pl.BlockSpec((None, None, 128, 1024), lambda n, i: (n, ck(i), 0, 0))
    return _pcall(body, carry=carry, name=f"ssd_fwd_r{int(reverse)}", grid=(nbatch, nc),
                  in_specs=[xs_spec, bc_spec, dt_spec, al_spec], out_specs=(xs_spec, st_spec),
                  out_shape=(jax.ShapeDtypeStruct((nbatch, s, 1024), F32), jax.ShapeDtypeStruct((nbatch, nc, 128, 1024), F32)),
                  scratch_shapes=[pltpu.VMEM((128, 1024), F32)],
                  compiler_params=_params())(xbc3, xbc3, dt3, alog, *(carry[0] if carry else ()))


def _ssd_bwd(xbc3, dt3, alog, st4, dy3, reverse, add_to=(), scatter=()):
    nbatch, s, _ = xbc3.shape
    q = min(SSD_CHUNK, s)
    nc = s // q
    lane0 = SSD_HEADS * int(reverse)
    nadd, ns = len(add_to), len(scatter)

    def body(xs_ref, bc_ref, dt_ref, al_ref, st0_ref, dy_ref, *rest):
        adds, srcs, rest = rest[:nadd], rest[nadd:nadd + ns], rest[nadd + ns:]
        (dxs_ref, dbc_ref, ddt_ref, dal_ref), lands, dst = rest[:4], rest[4:4 + ns], rest[4 + ns]
        n, i = pl.program_id(0), pl.program_id(1)
        if ns:
            sends, arrivals = _scatter_copies(srcs, lands, *rest[5 + ns:])

            @pl.when((n == 0) & (i == 0))
            def _():
                for cp in sends:
                    cp.start()

        @pl.when(i == 0)
        def _():
            dst[...] = jnp.zeros_like(dst)

        @pl.when((i == 0) & (n == 0))
        def _():
            dal_ref[...] = jnp.zeros_like(dal_ref)

        c = _ssd_common(xs_ref, bc_ref, dt_ref, al_ref, reverse, lane0)
        hm = _head_masks()
        reduce_m = _head_reduce(lane0)
        s0_all, ds1_all, dy = st0_ref[...], dst[...], dy_ref[...]
        lane = lax.broadcasted_iota(jnp.int32, (q, 128), 1)
        sub = lax.broadcasted_iota(jnp.int32, (128, q), 0)
        rowacc = jnp.zeros((q, 128), F32)
        colacc_t = jnp.zeros((128, q), F32)
        dv_l, yst_l, dvbar_l, dk_l, dc_l = [], [], [], [], []
        for g in range(SSD_GROUPS):
            sl = slice(g * 256, (g + 1) * 256)
            cg, bg = _mx(c["cm"][:, g * 128:(g + 1) * 128]), _mx(c["bm"][:, g * 128:(g + 1) * 128])
            cb = _dot(cg, bg, _NT)
            vg, dyg, wg, ecg = c["v"][:, sl], dy[:, sl], c["w"][:, sl], c["e_c"][:, sl]
            s0, ds1 = _mx(s0_all[:, sl]), _mx(ds1_all[:, sl])
            dye = _mx(dyg * ecg)
            yst_l.append(_dot(cg, s0) * ecg)
            dcg = _dot(dye, s0, _NT)
            dst[:, sl] = c["e_l"][:, sl] * ds1_all[:, sl] + _dot(cg, dye, _TN)
            vbar = _mx(vg * wg)
            dvbar = _dot(bg, ds1)
            dvbar_l.append(dvbar)
            dvg = dvbar * wg
            dkg = _dot(vbar, ds1, _NT)
            for e in range(4):
                h = 4 * g + e
                m = _ssd_decay(c, h)
                dyh, vh = _mx(jnp.where(hm[e], dyg, 0.0)), _mx(jnp.where(hm[e], vg, 0.0))
                dvg = dvg + _dot(_mx(m * cb), dyh, _TN)
                dcb = _dot(dyh, vh, _NT) * m
                dcbb = _mx(dcb)
                dcg = dcg + _dot(dcbb, bg)
                dkg = dkg + _dot(dcbb, cg, _TN)
                wmat = dcb * cb
                rowacc = jnp.where(lane == lane0 + h, jnp.sum(wmat, axis=1, keepdims=True), rowacc)
                colacc_t = jnp.where(sub == lane0 + h, jnp.sum(wmat, axis=0, keepdims=True), colacc_t)
            dv_l.append(dvg)
            dk_l.append(dkg)
            dc_l.append(dcg)
        dv = jnp.concatenate(dv_l, axis=1)
        yst = jnp.concatenate(yst_l, axis=1)
        dvbar = jnp.concatenate(dvbar_l, axis=1)
        t1 = _dot01(dy * yst, reduce_m, split="a", terms=3)
        t2 = _dot01(c["v"] * c["w"] * dvbar, reduce_m, split="a", terms=3)
        dlast = jnp.sum(t2, axis=0, keepdims=True) + _dot01(
            c["e_l"] * jnp.sum(ds1_all * s0_all, axis=0, keepdims=True), reduce_m, split="a", terms=2)
        dcum = rowacc - colacc_t.T + t1 - t2
        dcum = dcum + jnp.where(lax.broadcasted_iota(jnp.int32, (q, 128), 0) == c["edge"], dlast, 0.0)
        dda = _dot01(c["mask"].astype(F32), dcum, _TN, split="b", terms=3)
        ddt = dda * c["a"] + _dot01(dv * c["xs"], reduce_m, split="a", terms=2)
        dal_ref[...] += jnp.sum(dda * c["dt"], axis=0, keepdims=True) * c["a"]
        dxs = dv * c["dt_x"]
        dbc = jnp.concatenate(dk_l + dc_l, axis=1)
        if nadd:
            for a_ref in adds[:-2]:
                dxs = dxs + a_ref[...]
            dbc = dbc + adds[-2][...]
            ddt = ddt + adds[-1][...]
        ddt_ref[...] = ddt
        dxs_ref[...] = dxs
        dbc_ref[...] = dbc
        if ns:
            @pl.when((n == nbatch - 1) & (i == nc - 1))
            def _():
                for cp in arrivals:
                    cp.wait_recv()
                for cp in sends:
                    cp.wait_send()

    ck = (lambda i: i) if reverse else (lambda i: nc - 1 - i)
    xs_spec = pl.BlockSpec((None, q, 1024), lambda n, i: (n, ck(i), 0))
    bc_spec = pl.BlockSpec((None, q, 1024), lambda n, i: (n, ck(i), 1))
    dt_spec = pl.BlockSpec((None, q, 128), lambda n, i: (n, ck(i), 0))
    al_spec = pl.BlockSpec((1, 128), lambda n, i: (0, 0))
    st_spec = pl.BlockSpec((None, None, 128, 1024), lambda n, i: (n, ck(i), 0, 0))
    return _pcall(body, name=f"ssd_bwd_r{int(reverse)}", grid=(nbatch, nc),
                  in_specs=([xs_spec, bc_spec, dt_spec, al_spec, st_spec, xs_spec] + [xs_spec] * (nadd - 1)
                            + [dt_spec] * bool(nadd) + [ANY] * ns),
                  out_specs=(xs_spec, xs_spec, dt_spec, al_spec) + (ANY,) * ns,
                  out_shape=(jax.ShapeDtypeStruct((nbatch, s, 1024), F32), jax.ShapeDtypeStruct((nbatch, s, 1024), F32),
                             jax.ShapeDtypeStruct((nbatch, s, 128), F32), jax.ShapeDtypeStruct((1, 128), F32))
                  + tuple(jax.ShapeDtypeStruct(c.shape, c.dtype) for c in scatter),
                  scratch_shapes=[pltpu.VMEM((128, 1024), F32)] + (_scatter_scratch(ns) if ns else []),
                  compiler_params=_params())(xbc3, xbc3, dt3, alog, st4, dy3, *add_to, *scatter)


def _gla_block(q, k, g, reverse):
    bq = g.shape[0]
    nsub = bq // HGRN_SUB
    edge = 0 if reverse else bq - 1
    ri = lax.broadcasted_iota(jnp.int32, (bq, bq), 0)
    ci = lax.broadcasted_iota(jnp.int32, (bq, bq), 1)
    rb, cb = jnp.right_shift(ri, HGRN_SUB_SHIFT), jnp.right_shift(ci, HGRN_SUB_SHIFT)
    mask = (ri <= ci) if reverse else (ri >= ci)
    m_within = (mask & (rb == cb)).astype(F32)
    m_before = ((cb > rb) if reverse else (cb < rb)).astype(F32)
    bl = _dot01(m_within, g, split="b", terms=3)
    c = _dot01(m_before, g, split="b", terms=3)
    last = c[edge:edge + 1, :] + bl[edge:edge + 1, :]
    ebl, enbl, ec, elc = jnp.exp(bl), jnp.exp(-bl), jnp.exp(c), jnp.exp(last - c)
    qh = q * HGRN_SCALE * ebl
    kh = k * enbl
    blk = jnp.right_shift(lax.broadcasted_iota(jnp.int32, (bq, 1), 0), HGRN_SUB_SHIFT)
    scale = []
    for i in range(nsub):
        valid = (blk >= i) if reverse else (blk <= i)
        ex = jnp.where(valid, c[i * HGRN_SUB:i * HGRN_SUB + 1, :] - c, 0.0)
        scale.append(jnp.where(valid, jnp.exp(ex), 0.0))
    return dict(bq=bq, nsub=nsub, edge=edge, mask=mask, m_within=m_within, m_before=m_before, ebl=ebl, enbl=enbl, ec=ec,
                elc=elc, e_l=jnp.exp(last), qh=qh, qt=qh * ec, kh=kh, kb=kh * elc, scale=scale)


def _gla_scores(c, hs):
    keys = [_mx(c["kh"][:, hs] * c["scale"][i][:, hs]) for i in range(c["nsub"])]
    rows = [_dot(_mx(c["qh"][i * HGRN_SUB:(i + 1) * HGRN_SUB, hs]), keys[i], _NT) for i in range(c["nsub"])]
    return jnp.where(c["mask"], jnp.concatenate(rows, axis=0), 0.0), keys


def _gla_specs(nbatch, s, w, reverse_order):
    bq = min(HGRN_BLOCK, s)
    nblk = s // bq
    bi = (lambda i: nblk - 1 - i) if reverse_order else (lambda i: i)
    col = lambda cb: pl.BlockSpec((nbatch, bq, w), lambda i: (0, bi(i), cb))
    st_spec = pl.BlockSpec((nbatch, None, 128, w), lambda i: (0, bi(i), 0, 0))
    return bq, nblk, col, st_spec


def _gla_fwd(proj3, l0, l1, reverse, carry=None):
    nbatch, s, w5 = proj3.shape
    w = w5 // 5
    bq, nblk, col, st_spec = _gla_specs(nbatch, s, w, reverse)
    vec = pl.BlockSpec((1, w), lambda i: (0, 0))

    def body(q_ref, f_ref, v_ref, l0_ref, l1_ref, o_ref, st_ref, st):
        @pl.when(pl.program_id(0) == 0)
        def _():
            st[...] = jnp.zeros_like(st)

        for b in range(nbatch):
            st_ref[b] = st[b]
            k, g = _f_hgrn_pre(f_ref[b], l0_ref[...], l1_ref[...])
            c = _gla_block(q_ref[b], k, g, reverse)
            v = v_ref[b]
            for h in range(HGRN_HEADS):
                hs = slice(h * 128, (h + 1) * 128)
                att, _ = _gla_scores(c, hs)
                vb = _mx(v[:, hs])
                s0 = st[b, :, hs]
                o_ref[b, :, hs] = _dot(_mx(att), vb) + _dot(_mx(c["qt"][:, hs]), _mx(s0), _NT)
                st[b, :, hs] = s0 * c["e_l"][:, hs] + _dot(vb, _mx(c["kb"][:, hs]), _TN)

    return _pcall(body, carry=carry, name=f"gla_fwd_r{int(reverse)}", grid=(nblk,),
                  in_specs=[col(0), col(1 + int(reverse)), col(3), vec, vec], out_specs=(col(0), st_spec),
                  out_shape=(jax.ShapeDtypeStruct((nbatch, s, w), F32), jax.ShapeDtypeStruct((nbatch, nblk, 128, w), F32)),
                  scratch_shapes=[pltpu.VMEM((nbatch, 128, w), F32)],
                  compiler_params=_params())(proj3, proj3, proj3, l0, l1, *(carry[0] if carry else ()))


def _gla_bwd(proj3, l0, l1, st4, do3, reverse, add_to=None):
    nbatch, s, w5 = proj3.shape
    w = w5 // 5
    bq, nblk, col, st_spec = _gla_specs(nbatch, s, w, not reverse)
    nadd = 0 if add_to is None else 2
    vec = pl.BlockSpec((1, w), lambda i: (0, 0))

    def body(q_ref, f_ref, v_ref, l0_ref, l1_ref, st_ref, do_ref, *rest):
        adds, (dq_ref, df_ref, dv_ref, dl0_ref, dl1_ref, dst) = rest[:nadd], rest[nadd:]

        @pl.when(pl.program_id(0) == 0)
        def _():
            dst[...] = jnp.zeros_like(dst)
            dl0_ref[...] = jnp.zeros_like(dl0_ref)
            dl1_ref[...] = jnp.zeros_like(dl1_ref)

        row = lax.broadcasted_iota(jnp.int32, (bq, 128), 0)
        for b in range(nbatch):
            (k, g), pre_vjp = jax.vjp(_f_hgrn_pre, f_ref[b], l0_ref[...], l1_ref[...])
            c = _gla_block(q_ref[b], k, g, reverse)
            s0_all, ds1_all = st_ref[b], dst[b]
            v, dy = v_ref[b], do_ref[b]
            dbl_l, dc_l, dk_l = [], [], []
            for h in range(HGRN_HEADS):
                hs = slice(h * 128, (h + 1) * 128)
                att, keys = _gla_scores(c, hs)
                qh, qt, kh, kb = c["qh"][:, hs], c["qt"][:, hs], c["kh"][:, hs], c["kb"][:, hs]
                vb, dyb = _mx(v[:, hs]), _mx(dy[:, hs])
                s0, ds1 = s0_all[:, hs], ds1_all[:, hs]
                datt = _mx(jnp.where(c["mask"], _dot(dyb, vb, _NT), 0.0))
                dqh_rows = []
                dkh = jnp.zeros((bq, 128), F32)
                dc = jnp.zeros((bq, 128), F32)
                for i in range(c["nsub"]):
                    rs = slice(i * HGRN_SUB, (i + 1) * HGRN_SUB)
                    dqh_rows.append(_dot(datt[rs], keys[i]))
                    dki = _dot(datt[rs], _mx(qh[rs]), _TN)
                    sc = c["scale"][i][:, hs]
                    dkh = dkh + dki * sc
                    dex = dki * (kh * sc)
                    dc = dc - dex + jnp.where(row == i * HGRN_SUB, jnp.sum(dex, axis=0, keepdims=True), 0.0)
                dqt = _dot(dyb, _mx(s0))
                dkb = _dot(vb, _mx(ds1))
                dv = _dot(_mx(att), dyb, _TN) + _dot(_mx(kb), _mx(ds1), _NT)
                dst[b, :, hs] = c["e_l"][:, hs] * ds1 + _dot(dyb, _mx(qt), _TN)
                dqh = jnp.concatenate(dqh_rows, axis=0) + dqt * c["ec"][:, hs]
                dkh = dkh + dkb * c["elc"][:, hs]
                kbk = dkb * kb
                dlast = jnp.sum(kbk, axis=0, keepdims=True) + c["e_l"][:, hs] * jnp.sum(ds1 * s0, axis=0, keepdims=True)
                at_edge = jnp.where(row == c["edge"], dlast, 0.0)
                dc_l.append(dc + dqt * qt - kbk + at_edge)
                dbl_l.append(dqh * qh - dkh * kh + at_edge)
                dq = dqh * c["ebl"][:, hs] * HGRN_SCALE
                if nadd:
                    dq, dv = dq + adds[0][b, :, hs], dv + adds[1][b, :, hs]
                dq_ref[b, :, hs] = dq.astype(dq_ref.dtype)
                dv_ref[b, :, hs] = dv.astype(dv_ref.dtype)
                dk_l.append(dkh * c["enbl"][:, hs])
            dg = (_dot01(c["m_within"], jnp.concatenate(dbl_l, axis=1), _TN, split="b", terms=2)
                  + _dot01(c["m_before"], jnp.concatenate(dc_l, axis=1), _TN, split="b", terms=2))
            df, d0, d1 = pre_vjp((jnp.concatenate(dk_l, axis=1), dg))
            df_ref[b] = df.astype(df_ref.dtype)
            dl0_ref[...] += d0
            dl1_ref[...] += d1

    shp_sum = jax.ShapeDtypeStruct((nbatch, s, w), BF16 if nadd else F32)
    shp_vec = jax.ShapeDtypeStruct((1, w), F32)
    return _pcall(body, name=f"gla_bwd_r{int(reverse)}", grid=(nblk,),
                  in_specs=[col(0), col(1 + int(reverse)), col(3), vec, vec, st_spec, col(0)] + [col(0)] * nadd,
                  out_specs=(col(0), col(0), col(0), vec, vec),
                  out_shape=(shp_sum, jax.ShapeDtypeStruct((nbatch, s, w), BF16), shp_sum, shp_vec, shp_vec),
                  scratch_shapes=[pltpu.VMEM((nbatch, 128, w), F32)],
                  compiler_params=_params())(proj3, proj3, proj3, l0, l1, st4, do3, *(add_to or ()))


DIRS = (False, True)


def _block_diag(w):
    eye = jnp.eye(16, dtype=w.dtype)
    return (eye[:, None, :, None] * w[:, :, None, :]).reshape(1024, 1024)


def _diag_blocks(m):
    m4 = m.reshape(16, 64, 16, 64)
    return jnp.stack([m4[i, :, i, :] for i in range(16)], axis=0)


def _pad_lanes(v, n=128):
    return jnp.pad(v, [(0, 0)] * (v.ndim - 1) + [(0, n - v.shape[-1])])


def _mlp_fwd(tag, x, nw, w1, w2, carry=None):
    (h,) = _pw_fwd(f"{tag}_norm", _f_norm, [(x, 0)], [(nw, 0)], [BF16], 1024, 1)
    a, r, *got = _mm(f"{tag}_up", h, w1, "nn", relu2=True, carry=carry)
    return _mm(f"{tag}_down", r, w2, "nn", res=x), (h, a, r), got


def _mlp_bwd(tag, x, nw, w1, w2, saved, dxo, carry=None):
    h, a, r = saved
    dw2, *got = _mm(f"{tag}_dw2", r, dxo, "tn", carry=carry) if carry else (_mm(f"{tag}_dw2", r, dxo, "tn"),)
    da = _mm(f"{tag}_da", dxo, w2, "nt", relu2_of=a, out_dtype=BF16)
    dw1 = _mm(f"{tag}_dw1", h, da, "tn", col_shards=4)
    dx, dnw = _mm_sum_nt(f"{tag}_dh", [(da, k, 1024) for k in range(4)], [(w1, k) for k in range(4)], norm_bwd=(x, nw, dxo))
    return dx, dw1, dw2, dnw, got


def _split_in0(pieces, dt_piece):
    tm = 256

    def body(p0, p1, p2, p3, p4, p5, o_ref):
        full = jnp.concatenate([p0[...], p1[...], p2[...], p3[...], p4[...], p5[:, :32]], axis=1)
        for j in range(4):
            o_ref[j] = full[:, 1288 * j:1288 * (j + 1)]

    blk = pl.BlockSpec((tm, 1024), lambda i: (i, 0))
    return _pcall(body, name="split_in0", grid=(1024 // tm,), in_specs=[blk] * 5 + [pl.BlockSpec((tm, 128), lambda i: (i, 0))],
                  out_specs=pl.BlockSpec((4, tm, 1288), lambda i: (0, i, 0)),
                  out_shape=jax.ShapeDtypeStruct((4, 1024, 1288), F32), compiler_params=_params())(*pieces, dt_piece)


def _assemble_in0(shards):
    tm = 256

    def body(s_ref, m_ref, d_ref):
        full = jnp.concatenate([s_ref[j] for j in range(4)], axis=1)
        m_ref[...] = full[:, :5120]
        d_ref[...] = jnp.concatenate([full[:, 5120:5152], jnp.zeros((tm, 96), full.dtype)], axis=1)

    return _pcall(body, name="assemble_in0", grid=(1024 // tm,), in_specs=[pl.BlockSpec((4, tm, 1288), lambda i: (0, i, 0))],
                  out_specs=(pl.BlockSpec((tm, 5120), lambda i: (i, 0)), pl.BlockSpec((tm, 128), lambda i: (i, 0))),
                  out_shape=(jax.ShapeDtypeStruct((1024, 5120), shards.dtype), jax.ShapeDtypeStruct((1024, 128), shards.dtype)),
                  compiler_params=_params())(shards)


EARLY = ("odd_w_in", "odd_w_out", "mlp_w1_l1", "mlp_w2_l1")
MID = ("even_w_out", "mlp_w1_l0", "mlp_w2_l0")
LATE = ("even_w_in",)


def _local_step(x3, tgt3, w, w_main0, w_dt0, pair_reduce=None, late=None):
    nb, s, d = x3.shape
    carries, arrived = late if late else ({}, None)
    t = nb * s
    x0 = x3.reshape(t, d)
    tgt = tgt3.reshape(t, d)
    grads = {}
    row = lambda v: v.reshape(1, -1)
    to3 = lambda v: v.reshape(nb, s, v.shape[-1])
    to2 = lambda v: v.reshape(-1, v.shape[-1])

    conv_w, conv_b = w["even_conv_w"][0], row(w["even_conv_b"][0])
    nmix0 = row(w["norm_mix"][0])
    (h0,) = _pw_fwd("l0_norm", _f_norm, [(x0, 0)], [(nmix0, 0)], [BF16], 1024, 1)
    proj0 = _mm("l0_proj", h0, w_main0, "nn")
    dt_raw = _mm("l0_proj_dt", h0, w_dt0, "nn")
    conv2, xbc3 = _conv_fwd(to3(proj0), conv_w, conv_b, 0, 2, True)
    u_lru = to2(_conv_fwd(to3(proj0), conv_w, conv_b, 2, 1, False))
    xbc = to2(xbc3)
    dt_bias = _pad_lanes(w["ssd_dt_bias"][0].reshape(1, 32))
    (dt,) = _pw_fwd("l0_dt", _f_softplus, [(dt_raw, 0)], [(dt_bias, 0)], [F32], 128, 1)
    dt3 = to3(dt)
    alog = _pad_lanes(w["ssd_a_log"][0].reshape(1, 32))
    def merge(upd):
        out = dict(w)
        for k, v in upd.items():
            if isinstance(k, tuple):
                both = list(out.get(k[0]) or [None, None])
                both[k[1]] = v
                out[k[0]] = both
            else:
                out[k] = v
        return out

    ssd = [_ssd_fwd(xbc3, dt3, alog, r, carry=carries.get(key)) for r, key in zip(DIRS, ("ssd0", "ssd1"))]
    if late:
        w = merge(arrived("ssd0", ssd[0][2:]))
        w = merge(arrived("ssd1", ssd[1][2:]))
    yf, yb = to2(ssd[0][0]), to2(ssd[1][0])
    dskip = jnp.repeat(w["ssd_d"][0], SSD_HEADDIM).reshape(1, 1024)
    snw = row(w["ssd_norm_w"][0])
    ssd_ins = [(yf, 0), (yb, 0), (xbc, 0), (proj0, 3)]
    (ya,) = _pw_fwd("l0_ssd_post", _f_ssd_post, ssd_ins, [(dskip, 0), (snw, 0)], [BF16], 1024, 1, groups=SSD_GROUPS)
    w_gates = [_block_diag(w[k][0, r]).astype(MXU_DTYPE) for r in range(2) for k in ("lru_w_a", "lru_w_x")]
    pre = [_mm(f"l0_lru_pre{i}", u_lru, wg, "nn") for i, wg in enumerate(w_gates)]
    lru_par = [[(row(w[k][0, r]), 0) for k in ("lru_b_a", "lru_b_x", "lru_lambda")] for r in range(2)]
    lru_ins = [[(pre[2 * r], 0), (pre[2 * r + 1], 0), (u_lru, 0)] for r in range(2)]
    ab = [_pw_fwd(f"l0_lru_gates{r}", _f_lru_gates, lru_ins[r], lru_par[r], [F32, F32], 1024, 1) for r in range(2)]
    hs = [_lru_scan(to3(ab[r][0]), to3(ab[r][1]), DIRS[r]) for r in range(2)]
    lru_post_ins = [(to2(hs[0]), 0), (to2(hs[1]), 0), (proj0, 4)]
    (ybm,) = _pw_fwd("l0_lru_post", _f_lru_post, lru_post_ins, [], [BF16], 1024, 1)
    w_out0 = w["even_w_out"][0]
    x1 = _mm("l0_out_a", ya, w_out0[:1024], "nn", res=x0)
    x1 = _mm("l0_out_b", ybm, w_out0[1024:], "nn", res=x1)
    nmlp0 = row(w["norm_mlp"][0])
    x2, mlp0, got = _mlp_fwd("l0_mlp", x1, nmlp0, w["mlp_w1"][0], w["mlp_w2"][0], carry=carries.get("odd"))
    if late:
        w = merge(arrived("odd", got))

    w_in1 = w["odd_w_in"][0]
    nmix1 = row(w["norm_mix"][1])
    (h1,) = _pw_fwd("l1_norm", _f_norm, [(x2, 0)], [(nmix1, 0)], [BF16], 1024, 1)
    proj1 = _mm("l1_proj", h1, w_in1, "nn")
    proj1_3 = to3(proj1)
    lb0, lb1 = row(w["hgrn_lb_logits"][0]), row(w["hgrn_lb_logits"][1])
    gla = [_gla_fwd(proj1_3, lb0, lb1, r, carry=carries.get(key)) for r, key in zip(DIRS, ("gla0", "gla1"))]
    if late:
        w = merge(arrived("gla0", gla[0][2:]))
        w = merge(arrived("gla1", gla[1][2:]))
    hnw = row(w["hgrn_norm_w"][0])
    hpost_ins = [(to2(gla[0][0]), 0), (to2(gla[1][0]), 0), (proj1, 4)]
    (yo,) = _pw_fwd("l1_hgrn_post", _f_hgrn_post, hpost_ins, [(hnw, 0)], [BF16], 1024, 1, groups=HGRN_HEADS)
    w_out1 = w["odd_w_out"][0]
    x3_ = _mm("l1_out", yo, w_out1, "nn", res=x2)
    nmlp1 = row(w["norm_mlp"][1])
    x4, mlp1, _ = _mlp_fwd("l1_mlp", x3_, nmlp1, w["mlp_w1"][1], w["mlp_w2"][1])

    dx4, dnf, loss = _loss_head(x4, tgt, row(w["norm_final"]))
    grads["norm_final"] = dnf.reshape(-1)

    dx3, dw1_1, dw2_1, dnmlp1, _ = _mlp_bwd("l1_mlp", x3_, nmlp1, w["mlp_w1"][1], w["mlp_w2"][1], mlp1, dx4)
    big = {"odd_w_out": _mm("l1_dwout", yo, dx3, "tn").reshape(4, 256, 1024)}
    dyo = _mm("l1_dyo", dx3, w_out1, "nt")
    (do, dgate1), (dhnw,) = _pw_bwd("l1_hgrn_post_b", _f_hgrn_post, hpost_ins, [(hnw, 0)], [dyo], 1024, 1, [0, 2],
                                    out_dtypes=[F32, BF16], groups=HGRN_HEADS, tm=ROWS_FWD)
    grads["hgrn_norm_w"] = dhnw
    do3 = to3(do)
    gb = [_gla_bwd(proj1_3, lb0, lb1, gla[0][1], do3, False)]
    gb.append(_gla_bwd(proj1_3, lb0, lb1, gla[1][1], do3, True, add_to=(gb[0][0], gb[0][2])))
    grads["hgrn_lb_logits"] = jnp.concatenate([gb[0][3] + gb[1][3], gb[0][4] + gb[1][4]], axis=0)
    dparts1 = [to2(gb[1][0]), to2(gb[0][1]), to2(gb[1][1]), to2(gb[1][2]), dgate1]
    dwin1 = jnp.concatenate([_mm(f"l1_dwin{i}", h1, dp, "tn") for i, dp in enumerate(dparts1)], axis=1)
    big["odd_w_in"] = dwin1.reshape(1024, 4, 1280).transpose(1, 0, 2)
    dx2, dnmix1 = _mm_sum_nt("l1_dh", dparts1, [(w_in1, i) for i in range(5)], norm_bwd=(x2, nmix1, dx3))
    big["mlp_w1_l1"], big["mlp_w2_l1"] = dw1_1, dw2_1.reshape(4, 1024, 1024)
    box = {}

    def mlp0_bwd(carry=None):
        box["mlp0"] = _mlp_bwd("l0_mlp", x1, nmlp0, w["mlp_w1"][0], w["mlp_w2"][0], mlp0, dx2, carry=carry)
        return box["mlp0"][4]

    early_sums = tuple(pair_reduce(EARLY, [big[n] for n in EARLY], mlp0_bwd)) if pair_reduce else tuple(mlp0_bwd())

    dx1, dw1_0, dw2_0, dnmlp0 = box["mlp0"][:4]
    big["mlp_w1_l0"], big["mlp_w2_l0"] = dw1_0, dw2_0.reshape(4, 1024, 1024)
    grads["norm_mlp"] = jnp.concatenate([dnmlp0, dnmlp1], axis=0)
    big["even_w_out"] = jnp.concatenate([_mm("l0_dwout_a", ya, dx1, "tn"), _mm("l0_dwout_b", ybm, dx1, "tn")],
                                        axis=0).reshape(4, 512, 1024)
    dya = _mm("l0_dya", dx1, w_out0[:1024], "nt")
    dyb = _mm("l0_dyb", dx1, w_out0[1024:], "nt")
    (dh, dgate0), _ = _pw_bwd("l0_lru_post_b", _f_lru_post, lru_post_ins, [], [dyb], 1024, 1, [0, 2], out_dtypes=[F32, BF16],
                               tm=ROWS_FWD)
    dh3 = to3(dh)

    def lru0_bwd(carry=None):
        box["lru0"] = _lru_scan_bwd(to3(ab[0][0]), hs[0], dh3, DIRS[0], carry=carry)
        return box["lru0"][2:]

    mid_sums = tuple(pair_reduce(MID, [big[n] for n in MID], lru0_bwd)) if pair_reduce else tuple(lru0_bwd())
    dpre, du_parts, dlru = [], [], {k: [] for k in ("lru_b_a", "lru_b_x", "lru_lambda")}
    for r in range(2):
        g_r, da_r = box["lru0"][:2] if r == 0 else _lru_scan_bwd(to3(ab[r][0]), hs[r], dh3, DIRS[r])
        (dpa, dpx, du_r), (dba, dbx, dlam) = _pw_bwd(f"l0_lru_gates_b{r}", _f_lru_gates, lru_ins[r], lru_par[r],
                                                     [to2(da_r), to2(g_r)], 1024, 1, [0, 1, 2],
                                                     out_dtypes=[BF16, BF16, F32])
        dpre += [dpa, dpx]
        du_parts.append(du_r)
        dlru["lru_b_a"].append(dba)
        dlru["lru_b_x"].append(dbx)
        dlru["lru_lambda"].append(dlam)
    for k, v in dlru.items():
        grads[k] = jnp.concatenate(v, axis=0)[None]
    dwg = [_diag_blocks(_mm(f"l0_dwgate{i}", u_lru, dp, "tn")) for i, dp in enumerate(dpre)]
    grads["lru_w_a"] = jnp.stack([dwg[0], dwg[2]])[None]
    grads["lru_w_x"] = jnp.stack([dwg[1], dwg[3]])[None]
    du = _mm_sum_nt("l0_du", dpre, [(wg, 0) for wg in w_gates], add=du_parts)
    (dy, dxs_skip, dz), (ddskip, dsnw) = _pw_bwd("l0_ssd_post_b", _f_ssd_post, ssd_ins, [(dskip, 0), (snw, 0)], [dya],
                                                 1024, 1, [0, 2, 3], out_dtypes=[F32, F32, BF16], groups=SSD_GROUPS)
    grads["ssd_d"] = ddskip.reshape(SSD_HEADS, SSD_HEADDIM).sum(axis=1)[None]
    grads["ssd_norm_w"] = dsnw
    dy3 = to3(dy)
    sb0 = _ssd_bwd(xbc3, dt3, alog, ssd[0][1], dy3, False, scatter=early_sums)
    sb1 = _ssd_bwd(xbc3, dt3, alog, ssd[1][1], dy3, True, add_to=(sb0[0], to3(dxs_skip), sb0[1], sb0[2]), scatter=mid_sums)
    grads["ssd_a_log"] = (sb0[3] + sb1[3])[:, :32].reshape(1, 2, 16)
    ddt = to2(sb1[2])
    (ddt_raw,), (ddtb,) = _pw_bwd("l0_dt_b", _f_softplus, [(dt_raw, 0)], [(dt_bias, 0)], [ddt], 128, 1, [0])
    grads["ssd_dt_bias"] = ddtb[:, :32].reshape(1, 2, 16)
    cb = [_conv_bwd(sb1[0], to3(proj0), conv_w, 0, conv2), _conv_bwd(sb1[1], to3(proj0), conv_w, 1, conv2),
          _conv_bwd(to3(du), to3(proj0), conv_w, 2)]
    dcw = jnp.concatenate([c_[1] for c_ in cb], axis=1)
    grads["even_conv_w"] = dcw[:4][None]
    grads["even_conv_b"] = dcw[4:5]
    dparts0 = [to2(c_[0]) for c_ in cb] + [dz, dgate0]
    dwin0 = [_mm(f"l0_dwin{i}", h0, dp, "tn") for i, dp in enumerate(dparts0)]
    big["even_w_in"] = _split_in0(dwin0, _mm("l0_dwin_dt", h0, ddt_raw, "tn"))
    dx0, dnmix0 = _mm_sum_nt("l0_dh", dparts0 + [ddt_raw], [(w_main0, i) for i in range(5)] + [(w_dt0, 0)],
                             norm_bwd=(x0, nmix0, dx1))
    grads["norm_mix"] = jnp.concatenate([dnmix0, dnmix1], axis=0)
    return loss, dx0.reshape(nb, s, d), grads, big, (early_sums + mid_sums, sb0[4:] + sb1[4:])


ANY = pl.BlockSpec(memory_space=pl.ANY)


def _place():
    return lax.axis_index("x"), lax.axis_index("y"), lax.axis_index("c")


def _remote(src, dst, send_sems, recv_sems, k, to):
    return pltpu.make_async_remote_copy(src_ref=src, dst_ref=dst, send_sem=send_sems.at[k], recv_sem=recv_sems.at[k],
                                        device_id=to, device_id_type=MESH)


def _gather_start(x_refs, out_refs, send_sems, recv_sems, finish=False):
    n = len(x_refs)
    halves = [r.shape[0] // 2 for r in x_refs]
    x, y, c = _place()
    sibling = (x, y, 1 - c)
    chips = [(1 - x, y), (x, 1 - y), (1 - x, 1 - y)]

    def blk(t, px, py, hc):
        return out_refs[t].at[2 * px + py, pl.ds(hc * halves[t], halves[t]), :]

    def src(t):
        return x_refs[t].at[pl.ds(c * halves[t], halves[t]), :]

    first = [_remote(src(t), blk(t, x, y, c), send_sems, recv_sems, 6 * t + j, (*chip, c))
             for t in range(n) for j, chip in enumerate(chips)]
    if not finish:
        for cp in first:
            cp.start()
        return
    passed = []
    for t in range(n):
        for j, chip in enumerate(chips):
            _remote(src(t), blk(t, *chip, c), send_sems, recv_sems, 6 * t + j, (*chip, c)).wait_recv()
            cp = _remote(blk(t, *chip, c), blk(t, *chip, c), send_sems, recv_sems, 6 * t + 3 + j, sibling)
            cp.start()
            passed.append(cp)
    for t in range(n):
        for j, chip in enumerate(chips):
            _remote(src(t), blk(t, *chip, 1 - c), send_sems, recv_sems, 6 * t + 3 + j, sibling).wait_recv()
    for cp in first + passed:
        cp.wait_send()


_gather_finish = functools.partial(_gather_start, finish=True)


def _gather_carry(shards):
    n = len(shards)
    return (list(shards), [jax.ShapeDtypeStruct((4,) + s.shape, s.dtype) for s in shards],
            [pltpu.SemaphoreType.DMA((6 * n,)), pltpu.SemaphoreType.DMA((6 * n,))], _gather_start, _gather_finish)


def _gather_chips(shards):
    n = len(shards)
    srcs, shapes, scratch, start, finish = _gather_carry(shards)

    def body(*refs):
        start(refs[:n], refs[n:2 * n], *refs[2 * n:])
        finish(refs[:n], refs[n:2 * n], *refs[2 * n:])

    return _pcall(body, name="gather_weights", in_specs=[ANY] * n, out_specs=(ANY,) * n, out_shape=tuple(shapes),
                  scratch_shapes=scratch, compiler_params=_params())(*shards)


def _pair_swap_start(g_refs, land_refs, send_sems, recv_sems, finish=False):
    x, y, c = _place()
    cps = []
    for t, g in enumerate(g_refs):
        half = g.shape[1] // 2
        cps += [_remote(g.at[j, pl.ds((1 - c) * half, half), :], land_refs[t].at[j], send_sems, recv_sems, 4 * t + j,
                        (x, y, 1 - c)) for j in range(4)]
    for cp in cps:
        cp.wait() if finish else cp.start()


_pair_swap_finish = functools.partial(_pair_swap_start, finish=True)


def _pair_swap_carry(gps):
    n = len(gps)
    return (list(gps), [jax.ShapeDtypeStruct((4, g.shape[1] // 2, g.shape[2]), F32) for g in gps],
            [pltpu.SemaphoreType.DMA((4 * n,)), pltpu.SemaphoreType.DMA((4 * n,))], _pair_swap_start, _pair_swap_finish)


def _pair_swap(name, gps):
    n = len(gps)
    srcs, shapes, scratch, start, finish = _pair_swap_carry(gps)

    def body(*refs):
        start(refs[:n], refs[n:2 * n], *refs[2 * n:])
        finish(refs[:n], refs[n:2 * n], *refs[2 * n:])

    return _pcall(body, name=f"pair_swap_{name}", in_specs=[ANY] * n, out_specs=(ANY,) * n, out_shape=tuple(shapes),
                  scratch_shapes=scratch, compiler_params=_params())(*gps)


def _pair_add(name, gp, land, cidx):
    _, half, cols = land.shape
    tr = _tile(half, 512)
    nh = half // tr

    def body(c_ref, g_ref, l_ref, o_ref):
        o_ref[...] = (g_ref[...] + l_ref[...]).astype(o_ref.dtype)

    grid_spec = pltpu.PrefetchScalarGridSpec(
        num_scalar_prefetch=1, grid=(4, nh),
        in_specs=[pl.BlockSpec((None, tr, cols), lambda j, i, c: (j, c[0] * nh + i, 0)),
                  pl.BlockSpec((None, tr, cols), lambda j, i, c: (j, i, 0))],
        out_specs=pl.BlockSpec((None, tr, cols), lambda j, i, c: (j, i, 0)))
    return _pcall(body, name=f"pair_add_{name}", grid_spec=grid_spec, out_shape=jax.ShapeDtypeStruct((4, half, cols), BF16),
                  compiler_params=_params())(cidx, gp, land)


def _scatter_copies(s_refs, land_refs, send_sems, recv_sems):
    x, y, c = _place()
    me = 2 * x + y
    chips = [(1 - x, y), (x, 1 - y), (1 - x, 1 - y)]
    pairs = [(t, j, px, py) for t in range(len(s_refs)) for j, (px, py) in enumerate(chips)]
    sends = [_remote(s_refs[t].at[2 * px + py], land_refs[t].at[me], send_sems, recv_sems, 3 * t + j, (px, py, c))
             for t, j, px, py in pairs]
    arrivals = [_remote(s_refs[t].at[me], land_refs[t].at[2 * px + py], send_sems, recv_sems, 3 * t + j, (px, py, c))
                for t, j, px, py in pairs]
    return sends, arrivals


def _scatter_scratch(n):
    return [pltpu.SemaphoreType.DMA((3 * n,)), pltpu.SemaphoreType.DMA((3 * n,))]


def _chip_scatter(name, css):
    n = len(css)

    def body(*refs):
        sends, arrivals = _scatter_copies(refs[:n], refs[n:2 * n], *refs[2 * n:])
        for cp in sends:
            cp.start()
        for cp in arrivals:
            cp.wait_recv()
        for cp in sends:
            cp.wait_send()

    return _pcall(body, name=f"chip_scatter_{name}", in_specs=[ANY] * n, out_specs=(ANY,) * n,
                  out_shape=tuple(jax.ShapeDtypeStruct(s.shape, s.dtype) for s in css),
                  scratch_shapes=_scatter_scratch(n), compiler_params=_params())(*css)


def _chip_sum(name, land):
    _, half, cols = land.shape
    tr = _tile(half, 512)

    def body(l_ref, o_ref):
        o_ref[...] = ((l_ref[0].astype(F32) + l_ref[1].astype(F32)) + l_ref[2].astype(F32)) + l_ref[3].astype(F32)

    return _pcall(body, name=f"chip_sum_{name}", grid=(half // tr,),
                  in_specs=[pl.BlockSpec((4, tr, cols), lambda i: (0, i, 0))],
                  out_specs=pl.BlockSpec((tr, cols), lambda i: (i, 0)),
                  out_shape=jax.ShapeDtypeStruct((half, cols), F32), compiler_params=_params())(land)


def _pair_join(reds):
    n = len(reds)

    def body(*refs):
        r_refs, out_refs = refs[:n], refs[n:2 * n]
        send_sems, recv_sems = refs[2 * n:]
        x, y, c = _place()
        cps = [_remote(r_refs[t], out_refs[t].at[c], send_sems, recv_sems, t, (x, y, 1 - c)) for t in range(n)]
        for cp in cps:
            cp.start()
        for t in range(n):
            _remote(r_refs[t], out_refs[t].at[1 - c], send_sems, recv_sems, t, (x, y, 1 - c)).wait_recv()
        for cp in cps:
            cp.wait_send()

    return _pcall(body, name="grad_pair_join", in_specs=[ANY] * n, out_specs=(ANY,) * n,
                  out_shape=tuple(jax.ShapeDtypeStruct((2,) + r.shape, F32) for r in reds),
                  scratch_shapes=[pltpu.SemaphoreType.DMA((n,)), pltpu.SemaphoreType.DMA((n,))],
                  compiler_params=_params())(*reds)


def _adamw(name, g, w, m, v):
    rows, cols = g.shape
    tr = _tile(rows, 512)

    def body(g_ref, w_ref, m_ref, v_ref, d_ref, mo_ref, vo_ref):
        gv = g_ref[...]
        mn = ADAM_B1 * m_ref[...] + (1.0 - ADAM_B1) * gv
        vn = ADAM_B2 * v_ref[...] + (1.0 - ADAM_B2) * jnp.square(gv)
        m_hat = mn / (1.0 - ADAM_B1 ** ADAM_STEP)
        v_hat = vn / (1.0 - ADAM_B2 ** ADAM_STEP)
        d_ref[...] = -ADAM_LR * (m_hat / (jnp.sqrt(v_hat) + ADAM_EPS) + ADAM_WD * w_ref[...])
        mo_ref[...] = mn
        vo_ref[...] = vn

    blk = pl.BlockSpec((tr, cols), lambda i: (i, 0))
    shp = jax.ShapeDtypeStruct((rows, cols), F32)
    return _pcall(body, name=f"adamw_{name}", grid=(rows // tr,), in_specs=[blk] * 4, out_specs=(blk,) * 3,
                  out_shape=(shp,) * 3, compiler_params=_params())(g, w, m, v)


def _pack(pieces, rows, dtype):
    flat = jnp.concatenate([p.reshape(-1).astype(dtype) for p in pieces])
    return jnp.pad(flat, (0, rows * PACK_COLS - flat.shape[0])).reshape(rows, PACK_COLS)


def _unpack(pack, shapes):
    flat = pack.reshape(-1)
    out, off = [], 0
    for shp in shapes:
        n = math.prod(shp)
        out.append(flat[off:off + n].reshape(shp))
        off += n
    return out


def _shard_of(full, axis, j):
    n = full.shape[axis] // 4
    return lax.slice_in_dim(full, j * n, (j + 1) * n, axis=axis)


def kernel(x, even_w_in, even_conv_w, even_conv_b, ssd_a_log, ssd_dt_bias, ssd_d, ssd_norm_w, lru_w_a, lru_b_a, lru_w_x, lru_b_x, lru_lambda, even_w_out, odd_w_in, hgrn_lb_logits, hgrn_norm_w, odd_w_out, norm_mix, norm_mlp, mlp_w1, mlp_w2, norm_final, loss_target, m_even_w_in, m_even_conv_w, m_even_conv_b, m_ssd_a_log, m_ssd_dt_bias, m_ssd_d, m_ssd_norm_w, m_lru_w_a, m_lru_b_a, m_lru_w_x, m_lru_b_x, m_lru_lambda, m_even_w_out, m_odd_w_in, m_hgrn_lb_logits, m_hgrn_norm_w, m_odd_w_out, m_norm_mix, m_norm_mlp, m_mlp_w1, m_mlp_w2, m_norm_final, v_even_w_in, v_even_conv_w, v_even_conv_b, v_ssd_a_log, v_ssd_dt_bias, v_ssd_d, v_ssd_norm_w, v_lru_w_a, v_lru_b_a, v_lru_w_x, v_lru_b_x, v_lru_lambda, v_even_w_out, v_odd_w_in, v_hgrn_lb_logits, v_hgrn_norm_w, v_odd_w_out, v_norm_mix, v_norm_mlp, v_mlp_w1, v_mlp_w2, v_norm_final):
    names = [n for n, _, _, _ in WEIGHTS]
    w_loc = dict(zip(names, (even_w_in, even_conv_w, even_conv_b, ssd_a_log, ssd_dt_bias, ssd_d, ssd_norm_w, lru_w_a, lru_b_a, lru_w_x, lru_b_x, lru_lambda, even_w_out, odd_w_in, hgrn_lb_logits, hgrn_norm_w, odd_w_out, norm_mix, norm_mlp, mlp_w1, mlp_w2, norm_final)))
    m_loc = dict(zip(names, (m_even_w_in, m_even_conv_w, m_even_conv_b, m_ssd_a_log, m_ssd_dt_bias, m_ssd_d, m_ssd_norm_w, m_lru_w_a, m_lru_b_a, m_lru_w_x, m_lru_b_x, m_lru_lambda, m_even_w_out, m_odd_w_in, m_hgrn_lb_logits, m_hgrn_norm_w, m_odd_w_out, m_norm_mix, m_norm_mlp, m_mlp_w1, m_mlp_w2, m_norm_final)))
    v_loc = dict(zip(names, (v_even_w_in, v_even_conv_w, v_even_conv_b, v_ssd_a_log, v_ssd_dt_bias, v_ssd_d, v_ssd_norm_w, v_lru_w_a, v_lru_b_a, v_lru_w_x, v_lru_b_x, v_lru_lambda, v_even_w_out, v_odd_w_in, v_hgrn_lb_logits, v_hgrn_norm_w, v_odd_w_out, v_norm_mix, v_norm_mlp, v_mlp_w1, v_mlp_w2, v_norm_final)))
    spec = {n: (blk, full, ax) for n, blk, full, ax in WEIGHTS}

    small = [n for n in names if n not in BIG]
    two_d = lambda n, v: v.reshape(BIG_2D[n])

    me = 2 * lax.axis_index("x") + lax.axis_index("y")
    cc = lax.axis_index("c")
    put = lambda whole, part, k: lax.dynamic_update_slice_in_dim(whole, part[None], k, axis=0)
    own = {n: two_d(n, w_loc[n]).astype(BF16) for n in BIG}
    own["small"] = _pack([w_loc[n] for n in SMALL_SHARDED], 16, F32)
    fill = lambda got, keys: [put(g, own[k], me) for g, k in zip(got, keys)]
    first = ("even_w_in", "small")
    g_in0, g_small = fill(_gather_chips([own[k] for k in first]), first)
    w_main0, w_dt0 = _assemble_in0(g_in0)
    w_full = {n: w_loc[n] for n in names if spec[n][2] is None}
    shards = [_unpack(g_small[j], [spec[n][0] for n in SMALL_SHARDED]) for j in range(4)]
    for n in ("mlp_w1", "mlp_w2"):
        for l in range(2):
            own[f"{n}_l{l}"] = w_loc[n][l].astype(BF16)
    riders = {"ssd0": ("mlp_w1_l0", "even_w_out"), "ssd1": ("mlp_w2_l0",), "odd": ("odd_w_in", "odd_w_out"),
              "gla0": ("mlp_w1_l1",), "gla1": ("mlp_w2_l1",)}
    carries = {key: _gather_carry([own[k] for k in ks]) for key, ks in riders.items()}

    def arrived(key, got):
        out = {}
        for k, g in zip(riders[key], fill(got, riders[key])):
            if k == "odd_w_in":
                out[k] = jnp.concatenate([g[j] for j in range(4)], axis=1)[None]
            elif k in ("odd_w_out", "even_w_out"):
                out[k] = g.reshape(spec[k][1])
            else:
                out[(k[:6], int(k[-1]))] = g if k.startswith("mlp_w1") else g.reshape(4096, 1024)
        return out

    for i, n in enumerate(SMALL_SHARDED):
        w_full[n] = jnp.concatenate([shards[j][i] for j in range(4)], axis=spec[n][2])

    cidx = cc.astype(jnp.int32).reshape(1)

    def pair_reduce(tags, tensors, run=None):
        lands = run(_pair_swap_carry(tensors)) if run else _pair_swap(tags[0], tensors)
        return [_pair_add(tag, g, land, cidx) for tag, g, land in zip(tags, tensors, lands)]

    loss_vec, grad_x, grads, big, (early_sums, early_landed) = _local_step(
        x, loss_target, w_full, w_main0, w_dt0, pair_reduce, (carries, arrived))
    loss = lax.psum(loss_vec[0, 0], ("x", "y", "c"))

    def dest_pack(j):
        return _pack([grads[n].reshape(spec[n][1]) if spec[n][2] is None else _shard_of(grads[n].reshape(spec[n][1]), spec[n][2], j)
                      for n in small], SMALL_ROWS, F32)

    late_tags = LATE + ("small",)
    late_sums = pair_reduce(late_tags, [big[n] for n in LATE] + [jnp.stack([dest_pack(j) for j in range(4)])])
    tags = EARLY + MID + late_tags
    chip_sums = list(early_sums) + late_sums
    landed = [put(land, lax.dynamic_index_in_dim(cs, me, axis=0, keepdims=False), me)
              for land, cs in zip(list(early_landed) + list(_chip_scatter("late", late_sums)), chip_sums)]
    halves = [_chip_sum(tag, land) for tag, land in zip(tags, landed)]
    red = {tag: put(r, h, cc).reshape(-1, r.shape[-1]) for tag, r, h in zip(tags, _pair_join(halves), halves)}
    for n in ("mlp_w1", "mlp_w2"):
        red[n] = jnp.concatenate([red[n + "_l0"], red[n + "_l1"]], axis=0)

    outs = {}
    for n, g in ((n, red[n]) for n in BIG):
        res = (g, *_adamw(n, g, two_d(n, w_loc[n]), two_d(n, m_loc[n]), two_d(n, v_loc[n])))
        outs[n] = [r.reshape(spec[n][0]) for r in res]
    blocks = [spec[n][0] for n in small]
    wp, mp, vp = (_pack([src[n] for n in small], SMALL_ROWS, F32) for src in (w_loc, m_loc, v_loc))
    res = (red["small"], *_adamw("small", red["small"], wp, mp, vp))
    unpacked = [_unpack(r, blocks) for r in res]
    for i, n in enumerate(small):
        outs[n] = [u[i] for u in unpacked]
    return (loss, grad_x, *[outs[n][k] for k in range(4) for n in names])
```

```python
import functools
import math

import jax
import jax.numpy as jnp
from jax import lax
from jax.experimental import pallas as pl
from jax.experimental.pallas import tpu as pltpu

F32 = jnp.float32
BF16 = jnp.bfloat16
MXU_DTYPE = jnp.bfloat16
MESH = pl.DeviceIdType.MESH

EPS = 1e-6
SSD_HEADS = 16
SSD_HEADDIM = 64
HEAD_SHIFT = 6
SSD_GROUPS = 4
SSD_CHUNK = 128
LRU_C = 8.0
LRU_ROWS = 256
HGRN_HEADS = 8
HGRN_HEADDIM = 128
HGRN_SUB = 32
HGRN_SUB_SHIFT = 5
HGRN_BLOCK = 128
HGRN_SCALE = HGRN_HEADDIM ** -0.5
CONV_ROWS = 512
ROWS_FWD = 512
ROWS_BWD = 256

ADAM_LR = 0.001
ADAM_B1 = 0.9
ADAM_B2 = 0.999
ADAM_EPS = 1e-08
ADAM_WD = 0.01
ADAM_STEP = 10

VMEM_LIMIT = 56 * 1024 * 1024
PACK_COLS = 1024
SMALL_ROWS = 288

WEIGHTS = (
    ("even_w_in", (1, 1024, 1288), (1, 1024, 5152), 2),
    ("even_conv_w", (1, 4, 768), (1, 4, 3072), 2),
    ("even_conv_b", (1, 3072), (1, 3072), None),
    ("ssd_a_log", (1, 2, 16), (1, 2, 16), None),
    ("ssd_dt_bias", (1, 2, 16), (1, 2, 16), None),
    ("ssd_d", (1, 16), (1, 16), None),
    ("ssd_norm_w", (1, 1024), (1, 1024), None),
    ("lru_w_a", (1, 2, 16, 64, 64), (1, 2, 16, 64, 64), None),
    ("lru_b_a", (1, 2, 256), (1, 2, 1024), 2),
    ("lru_w_x", (1, 2, 16, 64, 64), (1, 2, 16, 64, 64), None),
    ("lru_b_x", (1, 2, 256), (1, 2, 1024), 2),
    ("lru_lambda", (1, 2, 256), (1, 2, 1024), 2),
    ("even_w_out", (1, 512, 1024), (1, 2048, 1024), 1),
    ("odd_w_in", (1, 1024, 1280), (1, 1024, 5120), 2),
    ("hgrn_lb_logits", (2, 1024), (2, 1024), None),
    ("hgrn_norm_w", (1, 256), (1, 1024), 1),
    ("odd_w_out", (1, 256, 1024), (1, 1024, 1024), 1),
    ("norm_mix", (2, 1024), (2, 1024), None),
    ("norm_mlp", (2, 1024), (2, 1024), None),
    ("mlp_w1", (2, 1024, 1024), (2, 1024, 4096), 2),
    ("mlp_w2", (2, 1024, 1024), (2, 4096, 1024), 1),
    ("norm_final", (1024,), (1024,), None),
)
BIG = ("even_w_in", "even_w_out", "odd_w_in", "odd_w_out", "mlp_w1", "mlp_w2")
BIG_2D = {"even_w_in": (1024, 1288), "even_w_out": (512, 1024), "odd_w_in": (1024, 1280), "odd_w_out": (256, 1024),
          "mlp_w1": (2048, 1024), "mlp_w2": (2048, 1024)}
SMALL_SHARDED = ("even_conv_w", "lru_b_a", "lru_b_x", "lru_lambda", "hgrn_norm_w")


def _pcall(body, carry=None, **kw):
    if carry is not None:
        srcs, shapes, scratch, start, finish = carry
        grid, inner = kw["grid"], body
        as_tuple = lambda v: tuple(v) if isinstance(v, (tuple, list)) else (v,)
        out_specs, out_shape, own_scratch = as_tuple(kw["out_specs"]), as_tuple(kw["out_shape"]), list(kw.get("scratch_shapes", ()))
        a = len(kw["in_specs"])
        b = a + len(srcs)
        c = b + len(out_specs)
        d = c + len(shapes)
        e = d + len(own_scratch)

        def body(*refs):
            ids = [pl.program_id(ax) for ax in range(len(grid))]
            first = functools.reduce(jnp.logical_and, [i == 0 for i in ids])
            last = functools.reduce(jnp.logical_and, [i == g - 1 for i, g in zip(ids, grid)])
            pl.when(first)(lambda: start(refs[a:b], refs[c:d], *refs[e:]))
            inner(*refs[:a], *refs[b:c], *refs[d:e])
            pl.when(last)(lambda: finish(refs[a:b], refs[c:d], *refs[e:]))

        kw = dict(kw, in_specs=list(kw["in_specs"]) + [ANY] * len(srcs), out_specs=out_specs + (ANY,) * len(shapes),
                  out_shape=out_shape + tuple(shapes), scratch_shapes=own_scratch + list(scratch))
    return pl.pallas_call(body, **kw)


def _params(**kw):
    return pltpu.CompilerParams(vmem_limit_bytes=VMEM_LIMIT, **kw)


def _tile(n, pref):
    if n <= pref:
        return n
    t = (pref // 128) * 128
    while n % t:
        t -= 128
    return t


def _dot(a, b, dims=(((1,), (0,)), ((), ()))):
    return lax.dot_general(a, b, dims, preferred_element_type=F32)


_NN = (((1,), (0,)), ((), ()))
_NT = (((1,), (1,)), ((), ()))
_TN = (((0,), (0,)), ((), ()))


def _mx(v):
    return v.astype(MXU_DTYPE)


def _dot01(a, b, dims=_NN, *, split, terms):
    acc, rest = None, (a if split == "a" else b)
    for _ in range(terms):
        piece = _mx(rest)
        part = _dot(piece, _mx(b), dims) if split == "a" else _dot(_mx(a), piece, dims)
        acc = part if acc is None else acc + part
        rest = rest - piece.astype(F32)
    return acc


def _mm(name, a, b, mode, *, out_dtype=F32, res=None, relu2=False, relu2_of=None, col_shards=1, carry=None):
    shards = b.shape[0] if b.ndim == 3 else 0
    b2 = b.shape[1:] if shards else b.shape
    if mode == "nn":
        (m, kk), n = a.shape, b2[1] * max(shards, 1)
    elif mode == "nt":
        (m, kk), n = a.shape, b2[0]
    else:
        (kk, m), (_, n) = a.shape, b.shape
    assert res is None or relu2_of is None
    tk_pref = 1024
    if mode == "tn" and a.dtype.itemsize == 2 and b.dtype.itemsize == 2:
        tk_pref = 2048
    tm, tn, tk = _tile(m, 1024), _tile(n // col_shards, 1024), _tile(kk, tk_pref)
    nk = kk // tk
    dims = {"nn": _NN, "nt": _NT, "tn": _TN}[mode]
    a_spec = pl.BlockSpec((tk, tm), lambda i, j, k: (k, i)) if mode == "tn" else pl.BlockSpec((tm, tk), lambda i, j, k: (i, k))
    b_spec = pl.BlockSpec((tn, tk), lambda i, j, k: (j, k)) if mode == "nt" else pl.BlockSpec((tk, tn), lambda i, j, k: (k, j))
    if shards and mode == "nn":
        assert tn == b2[1]
        b_spec = pl.BlockSpec((None, tk, tn), lambda i, j, k: (j, k, 0))
    o_spec = pl.BlockSpec((tm, tn), lambda i, j, k: (i, j))
    o_shape = (m, n)
    if col_shards > 1:
        assert tn * col_shards == n and res is None and not relu2
        o_spec = pl.BlockSpec((None, tm, tn), lambda i, j, k: (j, i, 0))
        o_shape = (col_shards, m, tn)
    extra = res if res is not None else relu2_of
    has_res = extra is not None

    def body(*refs):
        a_ref, b_ref = refs[0], refs[1]
        res_ref = refs[2] if has_res else None
        outs = refs[2 + has_res:2 + has_res + 1 + relu2]

        def finish(r):
            if res is not None:
                r = r + res_ref[...]
            if relu2_of is not None:
                r = r * (2.0 * jnp.maximum(res_ref[...].astype(F32), 0.0))
            if relu2:
                outs[0][...] = r.astype(outs[0].dtype)
                outs[1][...] = jnp.square(jnp.maximum(r, 0.0)).astype(outs[1].dtype)
            else:
                outs[0][...] = r.astype(outs[0].dtype)

        prod = _dot(_mx(a_ref[...]), _mx(b_ref[...]), dims)
        if nk == 1:
            finish(prod)
            return
        acc = refs[-1]
        k = pl.program_id(2)

        @pl.when(k == 0)
        def _():
            acc[...] = prod

        @pl.when(k > 0)
        def _():
            acc[...] += prod

        @pl.when(k == nk - 1)
        def _():
            finish(acc[...])

    in_specs = [a_spec, b_spec] + ([o_spec] if has_res else [])
    if relu2:
        out_shape = (jax.ShapeDtypeStruct((m, n), BF16), jax.ShapeDtypeStruct((m, n), BF16))
        out_specs = (o_spec, o_spec)
    else:
        out_shape = jax.ShapeDtypeStruct(o_shape, out_dtype)
        out_specs = o_spec
    args = (a, b) + ((extra,) if has_res else ()) + (tuple(carry[0]) if carry else ())
    return _pcall(body, carry=carry, name=name, grid=(m // tm, n // tn, nk), in_specs=in_specs, out_specs=out_specs,
                  out_shape=out_shape, scratch_shapes=[pltpu.VMEM((tm, tn), F32)] if nk > 1 else [],
                  compiler_params=_params())(*args)


def _mm_sum_nt(name, parts, wblocks, norm_bwd=None, add=()):
    parts = [p if isinstance(p, tuple) else (p, 0, p.shape[1]) for p in parts]
    m, npart = parts[0][0].shape[0], len(parts)
    n = wblocks[0][0].shape[-2]
    tm, tn = _tile(m, 512), _tile(n, 1024)
    assert norm_bwd is None or tn == n

    def body(*refs):
        acc = _dot(_mx(refs[0][...]), _mx(refs[npart][...]), _NT)
        for k in range(1, npart):
            acc = acc + _dot(_mx(refs[k][...]), _mx(refs[npart + k][...]), _NT)
        if norm_bwd is None:
            for r in refs[2 * npart:-1]:
                acc = acc + r[...]
            refs[-1][...] = acc
            return
        x_ref, g_ref, res_ref, dx_ref, dg_ref = refs[2 * npart:]
        _, vjp = jax.vjp(_f_norm, x_ref[...], g_ref[...])
        dx, dg = vjp((acc,))
        dx_ref[...] = dx + res_ref[...]

        @pl.when(pl.program_id(0) == 0)
        def _():
            dg_ref[...] = jnp.zeros_like(dg_ref)

        dg_ref[...] += dg

    row = pl.BlockSpec((tm, tn), lambda i, j: (i, j))
    vec = pl.BlockSpec((1, tn), lambda i, j: (0, j))
    in_specs = [pl.BlockSpec((tm, wd), lambda i, j, cb=cb: (i, cb)) for _, cb, wd in parts]
    for (_, _, wd), (w, cb) in zip(parts, wblocks):
        in_specs.append(pl.BlockSpec((None, tn, wd), lambda i, j, cb=cb: (cb, j, 0)) if w.ndim == 3
                        else pl.BlockSpec((tn, wd), lambda i, j, cb=cb: (j, cb)))
    args = [p for p, _, _ in parts] + [w for w, _ in wblocks]
    if norm_bwd is None:
        return _pcall(body, name=name, grid=(m // tm, n // tn), in_specs=in_specs + [row] * len(add), out_specs=row,
                      out_shape=jax.ShapeDtypeStruct((m, n), F32), compiler_params=_params())(*args, *add)
    return _pcall(body, name=name, grid=(m // tm, 1), in_specs=in_specs + [row, vec, row], out_specs=(row, vec),
                  out_shape=(jax.ShapeDtypeStruct((m, n), F32), jax.ShapeDtypeStruct((1, n), F32)),
                  compiler_params=_params())(*args, *norm_bwd)


def _pw_fwd(name, f, ins, params, out_dtypes, tc, ncol, tm=ROWS_FWD, groups=1):
    t = ins[0][0].shape[0]
    tm = min(tm, t)
    ni, npar = len(ins), len(params)
    gw = tc // groups

    def body(*refs):
        for g in range(groups):
            sl = slice(g * gw, (g + 1) * gw)
            vals = f(*[r[:, sl].astype(F32) for r in refs[:ni]], *[r[:, sl] for r in refs[ni:ni + npar]])
            for o, v in zip(refs[ni + npar:], vals):
                o[:, sl] = v.astype(o.dtype)

    in_specs = [pl.BlockSpec((tm, tc), lambda j, i, off=off: (i, off + j)) for _, off in ins]
    in_specs += [pl.BlockSpec((1, tc), lambda j, i, off=off: (0, off + j)) for _, off in params]
    out_specs = tuple(pl.BlockSpec((tm, tc), lambda j, i: (i, j)) for _ in out_dtypes)
    out_shape = tuple(jax.ShapeDtypeStruct((t, ncol * tc), d) for d in out_dtypes)
    return _pcall(body, name=name, grid=(ncol, t // tm), in_specs=in_specs, out_specs=out_specs, out_shape=out_shape,
                  compiler_params=_params())(*[a for a, _ in ins], *[p for p, _ in params])


def _pw_bwd(name, f, ins, params, douts, tc, ncol, want, adds=None, tm=ROWS_BWD, out_dtypes=None, groups=1):
    adds = adds or {}
    out_dtypes = out_dtypes or [F32] * len(want)
    t = ins[0][0].shape[0]
    tm = min(tm, t)
    ni, npar, nd, na = len(ins), len(params), len(douts), len(adds)
    add_keys = sorted(adds)
    gw = tc // groups

    def body(*refs):
        in_refs, p_refs = refs[:ni], refs[ni:ni + npar]
        d_refs = refs[ni + npar:ni + npar + nd]
        a_refs = refs[ni + npar + nd:ni + npar + nd + na]
        o_refs = refs[ni + npar + nd + na:]
        for p in range(npar):
            @pl.when(pl.program_id(1) == 0)
            def _(o=o_refs[len(want) + p]):
                o[...] = jnp.zeros_like(o)

        for g in range(groups):
            sl = slice(g * gw, (g + 1) * gw)
            _, vjp = jax.vjp(f, *[r[:, sl].astype(F32) for r in in_refs], *[r[:, sl] for r in p_refs])
            cts = vjp(tuple(d[:, sl].astype(F32) for d in d_refs))
            for o, kidx in zip(o_refs[:len(want)], want):
                v = cts[kidx]
                if kidx in adds:
                    v = v + a_refs[add_keys.index(kidx)][:, sl]
                o[:, sl] = v.astype(o.dtype)
            for p in range(npar):
                o_refs[len(want) + p][:, sl] += cts[ni + p]

    in_specs = [pl.BlockSpec((tm, tc), lambda j, i, off=off: (i, off + j)) for _, off in ins]
    in_specs += [pl.BlockSpec((1, tc), lambda j, i, off=off: (0, off + j)) for _, off in params]
    in_specs += [pl.BlockSpec((tm, tc), lambda j, i: (i, j)) for _ in range(nd + na)]
    out_specs = tuple([pl.BlockSpec((tm, tc), lambda j, i: (i, j)) for _ in want]
                      + [pl.BlockSpec((1, tc), lambda j, i: (0, j)) for _ in params])
    out_shape = tuple([jax.ShapeDtypeStruct((t, ncol * tc), dt) for dt in out_dtypes]
                      + [jax.ShapeDtypeStruct((1, ncol * tc), F32) for _ in params])
    res = _pcall(body, name=name, grid=(ncol, t // tm), in_specs=in_specs, out_specs=out_specs, out_shape=out_shape,
                 compiler_params=_params())(*[a for a, _ in ins], *[p for p, _ in params], *douts, *[adds[k] for k in add_keys])
    return list(res[:len(want)]), list(res[len(want):])


def _rms(x, g):
    return (x * lax.rsqrt(jnp.mean(x * x, axis=-1, keepdims=True) + EPS)) * g


def _f_norm(x, g):
    return (_rms(x, g),)


def _f_softplus(d, b):
    return (jax.nn.softplus(d + b),)


def _f_ssd_post(yf, yb, xs, z, dskip, nw):
    u = (yf + yb + dskip * xs) * jax.nn.silu(z)
    return (_rms(u, nw),)


def _neg_expm1(v):
    t = jnp.tanh(0.5 * v)
    return -2.0 * t / (1.0 - t)


def _f_lru_gates(pre_a, pre_x, u, ba, bx, lam):
    rg = jax.nn.sigmoid(pre_a + ba)
    ig = jax.nn.sigmoid(pre_x + bx)
    log_a = -LRU_C * rg * jax.nn.softplus(-lam)
    return jnp.exp(log_a), jnp.sqrt(_neg_expm1(2.0 * log_a)) * (ig * u)


def _f_lru_post(hf, hb, gate):
    return ((hf + hb) * jax.nn.gelu(gate),)


def _f_hgrn_pre(fr, l0, l1):
    lb = jax.nn.sigmoid(l1 - l0)
    k = (1.0 - lb) * jax.nn.sigmoid(-fr)
    return k, jnp.log1p(-k)


def _f_hgrn_post(of, ob, gate, nw):
    return (_rms(of + ob, nw) * jax.nn.silu(gate),)


def _loss_head(x, tgt, g, tm=ROWS_FWD):
    t, d = x.shape
    tm = min(tm, t)

    def body(x_ref, t_ref, g_ref, dx_ref, dg_ref, loss_ref):
        tv = t_ref[...]

        def lf(xv, gv):
            return 0.5 * jnp.sum(jnp.mean(jnp.square(_rms(xv, gv) - tv), axis=-1))

        val, vjp = jax.vjp(lf, x_ref[...], g_ref[...])
        dx, dg = vjp(jnp.ones((), F32))
        dx_ref[...] = dx

        @pl.when(pl.program_id(0) == 0)
        def _():
            dg_ref[...] = jnp.zeros_like(dg_ref)
            loss_ref[...] = jnp.zeros_like(loss_ref)

        dg_ref[...] += dg
        loss_ref[...] += jnp.full(loss_ref.shape, val, F32)

    row = pl.BlockSpec((tm, d), lambda i: (i, 0))
    vec = pl.BlockSpec((1, d), lambda i: (0, 0))
    return _pcall(body, name="loss_head", grid=(t // tm,), in_specs=[row, row, vec],
                  out_specs=(row, vec, pl.BlockSpec((1, 128), lambda i: (0, 0))),
                  out_shape=(jax.ShapeDtypeStruct((t, d), F32), jax.ShapeDtypeStruct((1, d), F32),
                             jax.ShapeDtypeStruct((1, 128), F32)), compiler_params=_params())(x, tgt, g)


def _shifted(x, d, prev, nxt, first, last):
    r = x.shape[0]
    row = lax.broadcasted_iota(jnp.int32, x.shape, 0)
    if d < 0:
        out = pltpu.roll(x, -d, 0)
        for q in range(-d):
            pv = jnp.where(first, 0.0, prev[8 + d + q:8 + d + q + 1, :])
            out = jnp.where(row == q, pv, out)
        return out
    out = pltpu.roll(x, r - d, 0)
    for q in range(d):
        nv = jnp.where(last, 0.0, nxt[q:q + 1, :])
        out = jnp.where(row == r - d + q, nv, out)
    return out


def _conv_fwd(p3, w, b, col0, ncol, silu, tc=1024):
    nbatch, s, _ = p3.shape
    ts = min(CONV_ROWS, s)
    nblk = s // ts

    def body(x_ref, pv_ref, nx_ref, w_ref, b_ref, o_ref, *act_ref):
        i = pl.program_id(1)
        first, last = i == 0, i == nblk - 1
        x, pv, nx = x_ref[...], pv_ref[...], nx_ref[...]
        wv = w_ref[...]
        out = b_ref[...] + wv[1:2] * x
        out = out + wv[0:1] * _shifted(x, -1, pv, nx, first, last)
        out = out + wv[2:3] * _shifted(x, 1, pv, nx, first, last)
        out = out + wv[3:4] * _shifted(x, 2, pv, nx, first, last)
        o_ref[...] = out
        if silu:
            act_ref[0][...] = jax.nn.silu(out)

    nb8 = s // 8
    cur = pl.BlockSpec((None, ts, tc), lambda n, i, j: (n, i, col0 + j))
    prev = pl.BlockSpec((None, 8, tc), lambda n, i, j: (n, jnp.maximum(i * (ts // 8) - 1, 0), col0 + j))
    nxt = pl.BlockSpec((None, 8, tc), lambda n, i, j: (n, jnp.minimum((i + 1) * (ts // 8), nb8 - 1), col0 + j))
    out = pl.BlockSpec((None, ts, tc), lambda n, i, j: (n, i, j))
    shp = jax.ShapeDtypeStruct((nbatch, s, ncol * tc), F32)
    return _pcall(body, name=f"conv_fwd{col0}", grid=(nbatch, nblk, ncol),
                  in_specs=[cur, prev, nxt, pl.BlockSpec((4, tc), lambda n, i, j: (0, col0 + j)),
                            pl.BlockSpec((1, tc), lambda n, i, j: (0, col0 + j))],
                  out_specs=(out, out) if silu else out, out_shape=(shp, shp) if silu else shp,
                  compiler_params=_params())(p3, p3, p3, w, b)


def _conv_bwd(dc3, p3, w, col, conv3=None):
    nbatch, s, tc = dc3.shape
    ts = min(CONV_ROWS, s)
    nblk = s // ts
    silu = conv3 is not None

    def body(d_ref, dpv_ref, dnx_ref, x_ref, pv_ref, nx_ref, w_ref, *rest):
        n, i = pl.program_id(0), pl.program_id(1)
        first, last = i == 0, i == nblk - 1
        d, dpv, dnx = d_ref[...], dpv_ref[...], dnx_ref[...]
        if silu:
            d, dpv, dnx = [jax.vjp(jax.nn.silu, c_ref[...])[1](t)[0] for c_ref, t in zip(rest[:3], (d, dpv, dnx))]
        dx_ref, dw_ref = rest[3 * silu:]
        x, pv, nx = x_ref[...], pv_ref[...], nx_ref[...]
        wv = w_ref[...]
        dx = wv[1:2] * d
        dx = dx + wv[0:1] * _shifted(d, 1, dpv, dnx, first, last)
        dx = dx + wv[2:3] * _shifted(d, -1, dpv, dnx, first, last)
        dx = dx + wv[3:4] * _shifted(d, -2, dpv, dnx, first, last)
        dx_ref[...] = dx.astype(dx_ref.dtype)

        @pl.when((n == 0) & (i == 0))
        def _():
            dw_ref[...] = jnp.zeros_like(dw_ref)

        dw_ref[0:1, :] += jnp.sum(d * _shifted(x, -1, pv, nx, first, last), axis=0, keepdims=True)
        dw_ref[1:2, :] += jnp.sum(d * x, axis=0, keepdims=True)
        dw_ref[2:3, :] += jnp.sum(d * _shifted(x, 1, pv, nx, first, last), axis=0, keepdims=True)
        dw_ref[3:4, :] += jnp.sum(d * _shifted(x, 2, pv, nx, first, last), axis=0, keepdims=True)
        dw_ref[4:5, :] += jnp.sum(d, axis=0, keepdims=True)

    nb8 = s // 8

    def specs(j):
        cur = pl.BlockSpec((None, ts, tc), lambda n, i: (n, i, j))
        prev = pl.BlockSpec((None, 8, tc), lambda n, i: (n, jnp.maximum(i * (ts // 8) - 1, 0), j))
        nxt = pl.BlockSpec((None, 8, tc), lambda n, i: (n, jnp.minimum((i + 1) * (ts // 8), nb8 - 1), j))
        return [cur, prev, nxt]

    return _pcall(body, name=f"conv_bwd{col}", grid=(nbatch, nblk),
                  in_specs=specs(0) + specs(col) + [pl.BlockSpec((4, tc), lambda n, i: (0, col))] + specs(col) * silu,
                  out_specs=(specs(0)[0], pl.BlockSpec((8, tc), lambda n, i: (0, 0))),
                  out_shape=(jax.ShapeDtypeStruct((nbatch, s, tc), BF16), jax.ShapeDtypeStruct((8, tc), F32)),
                  compiler_params=_params())(dc3, dc3, dc3, p3, p3, p3, w, *([conv3] * 3 * silu))


def _block_scan(coef, inp, reverse):
    r = coef.shape[0]
    row = lax.broadcasted_iota(jnp.int32, coef.shape, 0)
    a, b = coef, inp
    d = 1
    while d < r:
        if reverse:
            keep = row < r - d
            a_sh, b_sh = pltpu.roll(a, r - d, 0), pltpu.roll(b, r - d, 0)
        else:
            keep = row >= d
            a_sh, b_sh = pltpu.roll(a, d, 0), pltpu.roll(b, d, 0)
        b = b + a * jnp.where(keep, b_sh, 0.0)
        a = a * jnp.where(keep, a_sh, 1.0)
        d *= 2
    return a, b


def _lru_scan(a3, b3, reverse):
    nbatch, s, w = a3.shape
    ts = min(LRU_ROWS, s)
    nblk = s // ts
    edge = 0 if reverse else ts - 1

    def body(a_ref, b_ref, h_ref, carry):
        @pl.when(pl.program_id(1) == 0)
        def _():
            carry[...] = jnp.zeros_like(carry)

        ca, hb = _block_scan(a_ref[...], b_ref[...], reverse)
        h = hb + ca * carry[0:1, :]
        h_ref[...] = h
        carry[0:1, :] = h[edge:edge + 1, :]

    blk = pl.BlockSpec((None, ts, w), (lambda n, i: (n, nblk - 1 - i, 0)) if reverse else (lambda n, i: (n, i, 0)))
    return _pcall(body, name=f"lru_scan_r{int(reverse)}", grid=(nbatch, nblk), in_specs=[blk, blk], out_specs=blk,
                  out_shape=jax.ShapeDtypeStruct((nbatch, s, w), F32), scratch_shapes=[pltpu.VMEM((8, w), F32)],
                  compiler_params=_params())(a3, b3)


def _lru_scan_bwd(a3, h3, dh3, reverse, carry=None):
    nbatch, s, w = a3.shape
    ts = min(LRU_ROWS, s)
    nblk = s // ts
    nb8 = s // 8
    tpb = ts // 8

    def body(a_ref, aa_ref, h_ref, hh_ref, dh_ref, g_ref, da_ref, carry):
        i = pl.program_id(1)

        @pl.when(i == 0)
        def _():
            carry[...] = jnp.zeros_like(carry)

        a, h = a_ref[...], h_ref[...]
        row = lax.broadcasted_iota(jnp.int32, a.shape, 0)
        if reverse:
            a_edge = jnp.where(i == 0, 0.0, aa_ref[7:8, :])
            c = jnp.where(row == 0, a_edge, pltpu.roll(a, 1, 0))
            h_edge = jnp.where(i == nblk - 1, 0.0, hh_ref[0:1, :])
            h_sh = jnp.where(row == ts - 1, h_edge, pltpu.roll(h, ts - 1, 0))
        else:
            a_edge = jnp.where(i == 0, 0.0, aa_ref[0:1, :])
            c = jnp.where(row == ts - 1, a_edge, pltpu.roll(a, ts - 1, 0))
            h_edge = jnp.where(i == nblk - 1, 0.0, hh_ref[7:8, :])
            h_sh = jnp.where(row == 0, h_edge, pltpu.roll(h, 1, 0))
        cc, gb = _block_scan(c, dh_ref[...], not reverse)
        g = gb + cc * carry[0:1, :]
        g_ref[...] = g
        carry[0:1, :] = g[ts - 1:ts, :] if reverse else g[0:1, :]
        da_ref[...] = g * h_sh

    if reverse:
        bi = lambda i: i
    else:
        bi = lambda i: nblk - 1 - i
    blk = pl.BlockSpec((None, ts, w), lambda n, i: (n, bi(i), 0))
    before = pl.BlockSpec((None, 8, w), lambda n, i: (n, jnp.maximum(bi(i) * tpb - 1, 0), 0))
    after = pl.BlockSpec((None, 8, w), lambda n, i: (n, jnp.minimum((bi(i) + 1) * tpb, nb8 - 1), 0))
    a_tile, h_tile = (before, after) if reverse else (after, before)
    return _pcall(body, carry=carry, name=f"lru_scan_bwd_r{int(reverse)}", grid=(nbatch, nblk),
                  in_specs=[blk, a_tile, blk, h_tile, blk], out_specs=(blk, blk),
                  out_shape=(jax.ShapeDtypeStruct((nbatch, s, w), F32), jax.ShapeDtypeStruct((nbatch, s, w), F32)),
                  scratch_shapes=[pltpu.VMEM((8, w), F32)],
                  compiler_params=_params())(a3, a3, h3, h3, dh3, *(carry[0] if carry else ()))


def _head_expand(lane0):
    return (jnp.right_shift(lax.broadcasted_iota(jnp.int32, (128, 1024), 1), HEAD_SHIFT) + lane0
            == lax.broadcasted_iota(jnp.int32, (128, 1024), 0)).astype(F32)


def _head_reduce(lane0):
    return (jnp.right_shift(lax.broadcasted_iota(jnp.int32, (1024, 128), 0), HEAD_SHIFT) + lane0
            == lax.broadcasted_iota(jnp.int32, (1024, 128), 1)).astype(F32)


def _time_mask(q, reverse):
    ri = lax.broadcasted_iota(jnp.int32, (q, q), 0)
    ci = lax.broadcasted_iota(jnp.int32, (q, q), 1)
    return (ri <= ci) if reverse else (ri >= ci)


def _ssd_common(xs_ref, bc_ref, dt_ref, al_ref, reverse, lane0):
    q = xs_ref.shape[0]
    edge = 0 if reverse else q - 1
    dt = dt_ref[...]
    a = -jnp.exp(al_ref[...])
    mask = _time_mask(q, reverse)
    expand = _head_expand(lane0)
    cum = _dot01(mask.astype(F32), dt * a, split="b", terms=3)
    cum_x = _dot01(cum, expand, split="a", terms=2)
    dt_x = _dot01(dt, expand, split="a", terms=2)
    last_x = cum_x[edge:edge + 1, :]
    xs = xs_ref[...]
    bc = bc_ref[...]
    return dict(q=q, edge=edge, lane0=lane0, dt=dt, a=a, mask=mask, cum_t=cum.T, cum_x=cum_x, dt_x=dt_x, xs=xs,
                v=xs * dt_x, e_c=jnp.exp(cum_x), w=jnp.exp(last_x - cum_x), e_l=jnp.exp(last_x),
                bm=bc[:, :512], cm=bc[:, 512:])


def _ssd_decay(c, h):
    row = c["lane0"] + h
    seg = c["cum_x"][:, h * SSD_HEADDIM:h * SSD_HEADDIM + 1] - c["cum_t"][row:row + 1, :]
    return jnp.where(c["mask"], jnp.exp(jnp.minimum(seg, 0.0)), 0.0)


def _head_masks():
    lane = jnp.right_shift(lax.broadcasted_iota(jnp.int32, (1, 256), 1), HEAD_SHIFT)
    return [lane == e for e in range(4)]


def _ssd_fwd(xbc3, dt3, alog, reverse, carry=None):
    nbatch, s, _ = xbc3.shape
    q = min(SSD_CHUNK, s)
    nc = s // q
    lane0 = SSD_HEADS * int(reverse)

    def body(xs_ref, bc_ref, dt_ref, al_ref, y_ref, st_ref, st):
        @pl.when(pl.program_id(1) == 0)
        def _():
            st[...] = jnp.zeros_like(st)

        st_ref[...] = st[...]
        c = _ssd_common(xs_ref, bc_ref, dt_ref, al_ref, reverse, lane0)
        hm = _head_masks()
        for g in range(SSD_GROUPS):
            sl = slice(g * 256, (g + 1) * 256)
            cg, bg = _mx(c["cm"][:, g * 128:(g + 1) * 128]), _mx(c["bm"][:, g * 128:(g + 1) * 128])
            cb = _dot(cg, bg, _NT)
            vg = c["v"][:, sl]
            s0 = st[:, sl]
            yg = _dot(cg, _mx(s0)) * c["e_c"][:, sl]
            for e in range(4):
                m = _ssd_decay(c, 4 * g + e) * cb
                yg = yg + _dot(_mx(m), _mx(jnp.where(hm[e], vg, 0.0)))
            y_ref[:, sl] = yg
            st[:, sl] = c["e_l"][:, sl] * s0 + _dot(bg, _mx(vg * c["w"][:, sl]), _TN)

    ck = (lambda i: nc - 1 - i) if reverse else (lambda i: i)
    xs_spec = pl.BlockSpec((None, q, 1024), lambda n, i: (n, ck(i), 0))
    bc_spec = pl.BlockSpec((None, q, 1024), lambda n, i: (n, ck(i), 1))
    dt_spec = pl.BlockSpec((None, q, 128), lambda n, i: (n, ck(i), 0))
    al_spec = pl.BlockSpec((1, 128), lambda n, i: (0, 0))
    st_spec = pl.BlockSpec((None, None, 128, 1024), lambda n, i: (n, ck(i), 0, 0))
    return _pcall(body, carry=carry, name=f"ssd_fwd_r{int(reverse)}", grid=(nbatch, nc),
                  in_specs=[xs_spec, bc_spec, dt_spec, al_spec], out_specs=(xs_spec, st_spec),
                  out_shape=(jax.ShapeDtypeStruct((nbatch, s, 1024), F32), jax.ShapeDtypeStruct((nbatch, nc, 128, 1024), F32)),
                  scratch_shapes=[pltpu.VMEM((128, 1024), F32)],
                  compiler_params=_params())(xbc3, xbc3, dt3, alog, *(carry[0] if carry else ()))


def _ssd_bwd(xbc3, dt3, alog, st4, dy3, reverse, add_to=(), scatter=()):
    nbatch, s, _ = xbc3.shape
    q = min(SSD_CHUNK, s)
    nc = s // q
    lane0 = SSD_HEADS * int(reverse)
    nadd, ns = len(add_to), len(scatter)

    def body(xs_ref, bc_ref, dt_ref, al_ref, st0_ref, dy_ref, *rest):
        adds, srcs, rest = rest[:nadd], rest[nadd:nadd + ns], rest[nadd + ns:]
        (dxs_ref, dbc_ref, ddt_ref, dal_ref), lands, dst = rest[:4], rest[4:4 + ns], rest[4 + ns]
        n, i = pl.program_id(0), pl.program_id(1)
        if ns:
            sends, arrivals = _scatter_copies(srcs, lands, *rest[5 + ns:])

            @pl.when((n == 0) & (i == 0))
            def _():
                for cp in sends:
                    cp.start()

        @pl.when(i == 0)
        def _():
            dst[...] = jnp.zeros_like(dst)

        @pl.when((i == 0) & (n == 0))
        def _():
            dal_ref[...] = jnp.zeros_like(dal_ref)

        c = _ssd_common(xs_ref, bc_ref, dt_ref, al_ref, reverse, lane0)
        hm = _head_masks()
        reduce_m = _head_reduce(lane0)
        s0_all, ds1_all, dy = st0_ref[...], dst[...], dy_ref[...]
        lane = lax.broadcasted_iota(jnp.int32, (q, 128), 1)
        sub = lax.broadcasted_iota(jnp.int32, (128, q), 0)
        rowacc = jnp.zeros((q, 128), F32)
        colacc_t = jnp.zeros((128, q), F32)
        dv_l, yst_l, dvbar_l, dk_l, dc_l = [], [], [], [], []
        for g in range(SSD_GROUPS):
            sl = slice(g * 256, (g + 1) * 256)
            cg, bg = _mx(c["cm"][:, g * 128:(g + 1) * 128]), _mx(c["bm"][:, g * 128:(g + 1) * 128])
            cb = _dot(cg, bg, _NT)
            vg, dyg, wg, ecg = c["v"][:, sl], dy[:, sl], c["w"][:, sl], c["e_c"][:, sl]
            s0, ds1 = _mx(s0_all[:, sl]), _mx(ds1_all[:, sl])
            dye = _mx(dyg * ecg)
            yst_l.append(_dot(cg, s0) * ecg)
            dcg = _dot(dye, s0, _NT)
            dst[:, sl] = c["e_l"][:, sl] * ds1_all[:, sl] + _dot(cg, dye, _TN)
            vbar = _mx(vg * wg)
            dvbar = _dot(bg, ds1)
            dvbar_l.append(dvbar)
            dvg = dvbar * wg
            dkg = _dot(vbar, ds1, _NT)
            for e in range(4):
                h = 4 * g + e
                m = _ssd_decay(c, h)
                dyh, vh = _mx(jnp.where(hm[e], dyg, 0.0)), _mx(jnp.where(hm[e], vg, 0.0))
                dvg = dvg + _dot(_mx(m * cb), dyh, _TN)
                dcb = _dot(dyh, vh, _NT) * m
                dcbb = _mx(dcb)
                dcg = dcg + _dot(dcbb, bg)
                dkg = dkg + _dot(dcbb, cg, _TN)
                wmat = dcb * cb
                rowacc = jnp.where(lane == lane0 + h, jnp.sum(wmat, axis=1, keepdims=True), rowacc)
                colacc_t = jnp.where(sub == lane0 + h, jnp.sum(wmat, axis=0, keepdims=True), colacc_t)
            dv_l.append(dvg)
            dk_l.append(dkg)
            dc_l.append(dcg)
        dv = jnp.concatenate(dv_l, axis=1)
        yst = jnp.concatenate(yst_l, axis=1)
        dvbar = jnp.concatenate(dvbar_l, axis=1)
        t1 = _dot01(dy * yst, reduce_m, split="a", terms=3)
        t2 = _dot01(c["v"] * c["w"] * dvbar, reduce_m, split="a", terms=3)
        dlast = jnp.sum(t2, axis=0, keepdims=True) + _dot01(
            c["e_l"] * jnp.sum(ds1_all * s0_all, axis=0, keepdims=True), reduce_m, split="a", terms=2)
        dcum = rowacc - colacc_t.T + t1 - t2
        dcum = dcum + jnp.where(lax.broadcasted_iota(jnp.int32, (q, 128), 0) == c["edge"], dlast, 0.0)
        dda = _dot01(c["mask"].astype(F32), dcum, _TN, split="b", terms=3)
        ddt = dda * c["a"] + _dot01(dv * c["xs"], reduce_m, split="a", terms=2)
        dal_ref[...] += jnp.sum(dda * c["dt"], axis=0, keepdims=True) * c["a"]
        dxs = dv * c["dt_x"]
        dbc = jnp.concatenate(dk_l + dc_l, axis=1)
        if nadd:
            for a_ref in adds[:-2]:
                dxs = dxs + a_ref[...]
            dbc = dbc + adds[-2][...]
            ddt = ddt + adds[-1][...]
        ddt_ref[...] = ddt
        dxs_ref[...] = dxs
        dbc_ref[...] = dbc
        if ns:
            @pl.when((n == nbatch - 1) & (i == nc - 1))
            def _():
                for cp in arrivals:
                    cp.wait_recv()
                for cp in sends:
                    cp.wait_send()

    ck = (lambda i: i) if reverse else (lambda i: nc - 1 - i)
    xs_spec = pl.BlockSpec((None, q, 1024), lambda n, i: (n, ck(i), 0))
    bc_spec = pl.BlockSpec((None, q, 1024), lambda n, i: (n, ck(i), 1))
    dt_spec = pl.BlockSpec((None, q, 128), lambda n, i: (n, ck(i), 0))
    al_spec = pl.BlockSpec((1, 128), lambda n, i: (0, 0))
    st_spec = pl.BlockSpec((None, None, 128, 1024), lambda n, i: (n, ck(i), 0, 0))
    return _pcall(body, name=f"ssd_bwd_r{int(reverse)}", grid=(nbatch, nc),
                  in_specs=([xs_spec, bc_spec, dt_spec, al_spec, st_spec, xs_spec] + [xs_spec] * (nadd - 1)
                            + [dt_spec] * bool(nadd) + [ANY] * ns),
                  out_specs=(xs_spec, xs_spec, dt_spec, al_spec) + (ANY,) * ns,
                  out_shape=(jax.ShapeDtypeStruct((nbatch, s, 1024), F32), jax.ShapeDtypeStruct((nbatch, s, 1024), F32),
                             jax.ShapeDtypeStruct((nbatch, s, 128), F32), jax.ShapeDtypeStruct((1, 128), F32))
                  + tuple(jax.ShapeDtypeStruct(c.shape, c.dtype) for c in scatter),
                  scratch_shapes=[pltpu.VMEM((128, 1024), F32)] + (_scatter_scratch(ns) if ns else []),
                  compiler_params=_params())(xbc3, xbc3, dt3, alog, st4, dy3, *add_to, *scatter)


def _gla_block(q, k, g, reverse):
    bq = g.shape[0]
    nsub = bq // HGRN_SUB
    edge = 0 if reverse else bq - 1
    ri = lax.broadcasted_iota(jnp.int32, (bq, bq), 0)
    ci = lax.broadcasted_iota(jnp.int32, (bq, bq), 1)
    rb, cb = jnp.right_shift(ri, HGRN_SUB_SHIFT), jnp.right_shift(ci, HGRN_SUB_SHIFT)
    mask = (ri <= ci) if reverse else (ri >= ci)
    m_within = (mask & (rb == cb)).astype(F32)
    m_before = ((cb > rb) if reverse else (cb < rb)).astype(F32)
    bl = _dot01(m_within, g, split="b", terms=3)
    c = _dot01(m_before, g, split="b", terms=3)
    last = c[edge:edge + 1, :] + bl[edge:edge + 1, :]
    ebl, enbl, ec, elc = jnp.exp(bl), jnp.exp(-bl), jnp.exp(c), jnp.exp(last - c)
    qh = q * HGRN_SCALE * ebl
    kh = k * enbl
    blk = jnp.right_shift(lax.broadcasted_iota(jnp.int32, (bq, 1), 0), HGRN_SUB_SHIFT)
    scale = []
    for i in range(nsub):
        valid = (blk >= i) if reverse else (blk <= i)
        ex = jnp.where(valid, c[i * HGRN_SUB:i * HGRN_SUB + 1, :] - c, 0.0)
        scale.append(jnp.where(valid, jnp.exp(ex), 0.0))
    return dict(bq=bq, nsub=nsub, edge=edge, mask=mask, m_within=m_within, m_before=m_before, ebl=ebl, enbl=enbl, ec=ec,
                elc=elc, e_l=jnp.exp(last), qh=qh, qt=qh * ec, kh=kh, kb=kh * elc, scale=scale)


def _gla_scores(c, hs):
    keys = [_mx(c["kh"][:, hs] * c["scale"][i][:, hs]) for i in range(c["nsub"])]
    rows = [_dot(_mx(c["qh"][i * HGRN_SUB:(i + 1) * HGRN_SUB, hs]), keys[i], _NT) for i in range(c["nsub"])]
    return jnp.where(c["mask"], jnp.concatenate(rows, axis=0), 0.0), keys


def _gla_specs(nbatch, s, w, reverse_order):
    bq = min(HGRN_BLOCK, s)
    nblk = s // bq
    bi = (lambda i: nblk - 1 - i) if reverse_order else (lambda i: i)
    col = lambda cb: pl.BlockSpec((nbatch, bq, w), lambda i: (0, bi(i), cb))
    st_spec = pl.BlockSpec((nbatch, None, 128, w), lambda i: (0, bi(i), 0, 0))
    return bq, nblk, col, st_spec


def _gla_fwd(proj3, l0, l1, reverse, carry=None):
    nbatch, s, w5 = proj3.shape
    w = w5 // 5
    bq, nblk, col, st_spec = _gla_specs(nbatch, s, w, reverse)
    vec = pl.BlockSpec((1, w), lambda i: (0, 0))

    def body(q_ref, f_ref, v_ref, l0_ref, l1_ref, o_ref, st_ref, st):
        @pl.when(pl.program_id(0) == 0)
        def _():
            st[...] = jnp.zeros_like(st)

        for b in range(nbatch):
            st_ref[b] = st[b]
            k, g = _f_hgrn_pre(f_ref[b], l0_ref[...], l1_ref[...])
            c = _gla_block(q_ref[b], k, g, reverse)
            v = v_ref[b]
            for h in range(HGRN_HEADS):
                hs = slice(h * 128, (h + 1) * 128)
                att, _ = _gla_scores(c, hs)
                vb = _mx(v[:, hs])
                s0 = st[b, :, hs]
                o_ref[b, :, hs] = _dot(_mx(att), vb) + _dot(_mx(c["qt"][:, hs]), _mx(s0), _NT)
                st[b, :, hs] = s0 * c["e_l"][:, hs] + _dot(vb, _mx(c["kb"][:, hs]), _TN)

    return _pcall(body, carry=carry, name=f"gla_fwd_r{int(reverse)}", grid=(nblk,),
                  in_specs=[col(0), col(1 + int(reverse)), col(3), vec, vec], out_specs=(col(0), st_spec),
                  out_shape=(jax.ShapeDtypeStruct((nbatch, s, w), F32), jax.ShapeDtypeStruct((nbatch, nblk, 128, w), F32)),
                  scratch_shapes=[pltpu.VMEM((nbatch, 128, w), F32)],
                  compiler_params=_params())(proj3, proj3, proj3, l0, l1, *(carry[0] if carry else ()))


def _gla_bwd(proj3, l0, l1, st4, do3, reverse, add_to=None):
    nbatch, s, w5 = proj3.shape
    w = w5 // 5
    bq, nblk, col, st_spec = _gla_specs(nbatch, s, w, not reverse)
    nadd = 0 if add_to is None else 2
    vec = pl.BlockSpec((1, w), lambda i: (0, 0))

    def body(q_ref, f_ref, v_ref, l0_ref, l1_ref, st_ref, do_ref, *rest):
        adds, (dq_ref, df_ref, dv_ref, dl0_ref, dl1_ref, dst) = rest[:nadd], rest[nadd:]

        @pl.when(pl.program_id(0) == 0)
        def _():
            dst[...] = jnp.zeros_like(dst)
            dl0_ref[...] = jnp.zeros_like(dl0_ref)
            dl1_ref[...] = jnp.zeros_like(dl1_ref)

        row = lax.broadcasted_iota(jnp.int32, (bq, 128), 0)
        for b in range(nbatch):
            (k, g), pre_vjp = jax.vjp(_f_hgrn_pre, f_ref[b], l0_ref[...], l1_ref[...])
            c = _gla_block(q_ref[b], k, g, reverse)
            s0_all, ds1_all = st_ref[b], dst[b]
            v, dy = v_ref[b], do_ref[b]
            dbl_l, dc_l, dk_l = [], [], []
            for h in range(HGRN_HEADS):
                hs = slice(h * 128, (h + 1) * 128)
                att, keys = _gla_scores(c, hs)
                qh, qt, kh, kb = c["qh"][:, hs], c["qt"][:, hs], c["kh"][:, hs], c["kb"][:, hs]
                vb, dyb = _mx(v[:, hs]), _mx(dy[:, hs])
                s0, ds1 = s0_all[:, hs], ds1_all[:, hs]
                datt = _mx(jnp.where(c["mask"], _dot(dyb, vb, _NT), 0.0))
                dqh_rows = []
                dkh = jnp.zeros((bq, 128), F32)
                dc = jnp.zeros((bq, 128), F32)
                for i in range(c["nsub"]):
                    rs = slice(i * HGRN_SUB, (i + 1) * HGRN_SUB)
                    dqh_rows.append(_dot(datt[rs], keys[i]))
                    dki = _dot(datt[rs], _mx(qh[rs]), _TN)
                    sc = c["scale"][i][:, hs]
                    dkh = dkh + dki * sc
                    dex = dki * (kh * sc)
                    dc = dc - dex + jnp.where(row == i * HGRN_SUB, jnp.sum(dex, axis=0, keepdims=True), 0.0)
                dqt = _dot(dyb, _mx(s0))
                dkb = _dot(vb, _mx(ds1))
                dv = _dot(_mx(att), dyb, _TN) + _dot(_mx(kb), _mx(ds1), _NT)
                dst[b, :, hs] = c["e_l"][:, hs] * ds1 + _dot(dyb, _mx(qt), _TN)
                dqh = jnp.concatenate(dqh_rows, axis=0) + dqt * c["ec"][:, hs]
                dkh = dkh + dkb * c["elc"][:, hs]
                kbk = dkb * kb
                dlast = jnp.sum(kbk, axis=0, keepdims=True) + c["e_l"][:, hs] * jnp.sum(ds1 * s0, axis=0, keepdims=True)
                at_edge = jnp.where(row == c["edge"], dlast, 0.0)
                dc_l.append(dc + dqt * qt - kbk + at_edge)
                dbl_l.append(dqh * qh - dkh * kh + at_edge)
                dq = dqh * c["ebl"][:, hs] * HGRN_SCALE
                if nadd:
                    dq, dv = dq + adds[0][b, :, hs], dv + adds[1][b, :, hs]
                dq_ref[b, :, hs] = dq.astype(dq_ref.dtype)
                dv_ref[b, :, hs] = dv.astype(dv_ref.dtype)
                dk_l.append(dkh * c["enbl"][:, hs])
            dg = (_dot01(c["m_within"], jnp.concatenate(dbl_l, axis=1), _TN, split="b", terms=2)
                  + _dot01(c["m_before"], jnp.concatenate(dc_l, axis=1), _TN, split="b", terms=2))
            df, d0, d1 = pre_vjp((jnp.concatenate(dk_l, axis=1), dg))
            df_ref[b] = df.astype(df_ref.dtype)
            dl0_ref[...] += d0
            dl1_ref[...] += d1

    shp_sum = jax.ShapeDtypeStruct((nbatch, s, w), BF16 if nadd else F32)
    shp_vec = jax.ShapeDtypeStruct((1, w), F32)
    return _pcall(body, name=f"gla_bwd_r{int(reverse)}", grid=(nblk,),
                  in_specs=[col(0), col(1 + int(reverse)), col(3), vec, vec, st_spec, col(0)] + [col(0)] * nadd,
                  out_specs=(col(0), col(0), col(0), vec, vec),
                  out_shape=(shp_sum, jax.ShapeDtypeStruct((nbatch, s, w), BF16), shp_sum, shp_vec, shp_vec),
                  scratch_shapes=[pltpu.VMEM((nbatch, 128, w), F32)],
                  compiler_params=_params())(proj3, proj3, proj3, l0, l1, st4, do3, *(add_to or ()))


DIRS = (False, True)


def _block_diag(w):
    eye = jnp.eye(16, dtype=w.dtype)
    return (eye[:, None, :, None] * w[:, :, None, :]).reshape(1024, 1024)


def _diag_blocks(m):
    m4 = m.reshape(16, 64, 16, 64)
    return jnp.stack([m4[i, :, i, :] for i in range(16)], axis=0)


def _pad_lanes(v, n=128):
    return jnp.pad(v, [(0, 0)] * (v.ndim - 1) + [(0, n - v.shape[-1])])


def _mlp_fwd(tag, x, nw, w1, w2, carry=None):
    (h,) = _pw_fwd(f"{tag}_norm", _f_norm, [(x, 0)], [(nw, 0)], [BF16], 1024, 1)
    a, r, *got = _mm(f"{tag}_up", h, w1, "nn", relu2=True, carry=carry)
    return _mm(f"{tag}_down", r, w2, "nn", res=x), (h, a, r), got


def _mlp_bwd(tag, x, nw, w1, w2, saved, dxo, carry=None):
    h, a, r = saved
    dw2, *got = _mm(f"{tag}_dw2", r, dxo, "tn", carry=carry) if carry else (_mm(f"{tag}_dw2", r, dxo, "tn"),)
    da = _mm(f"{tag}_da", dxo, w2, "nt", relu2_of=a, out_dtype=BF16)
    dw1 = _mm(f"{tag}_dw1", h, da, "tn", col_shards=4)
    dx, dnw = _mm_sum_nt(f"{tag}_dh", [(da, k, 1024) for k in range(4)], [(w1, k) for k in range(4)], norm_bwd=(x, nw, dxo))
    return dx, dw1, dw2, dnw, got


def _split_in0(pieces, dt_piece):
    tm = 256

    def body(p0, p1, p2, p3, p4, p5, o_ref):
        full = jnp.concatenate([p0[...], p1[...], p2[...], p3[...], p4[...], p5[:, :32]], axis=1)
        for j in range(4):
            o_ref[j] = full[:, 1288 * j:1288 * (j + 1)]

    blk = pl.BlockSpec((tm, 1024), lambda i: (i, 0))
    return _pcall(body, name="split_in0", grid=(1024 // tm,), in_specs=[blk] * 5 + [pl.BlockSpec((tm, 128), lambda i: (i, 0))],
                  out_specs=pl.BlockSpec((4, tm, 1288), lambda i: (0, i, 0)),
                  out_shape=jax.ShapeDtypeStruct((4, 1024, 1288), F32), compiler_params=_params())(*pieces, dt_piece)


def _assemble_in0(shards):
    tm = 256

    def body(s_ref, m_ref, d_ref):
        full = jnp.concatenate([s_ref[j] for j in range(4)], axis=1)
        m_ref[...] = full[:, :5120]
        d_ref[...] = jnp.concatenate([full[:, 5120:5152], jnp.zeros((tm, 96), full.dtype)], axis=1)

    return _pcall(body, name="assemble_in0", grid=(1024 // tm,), in_specs=[pl.BlockSpec((4, tm, 1288), lambda i: (0, i, 0))],
                  out_specs=(pl.BlockSpec((tm, 5120), lambda i: (i, 0)), pl.BlockSpec((tm, 128), lambda i: (i, 0))),
                  out_shape=(jax.ShapeDtypeStruct((1024, 5120), shards.dtype), jax.ShapeDtypeStruct((1024, 128), shards.dtype)),
                  compiler_params=_params())(shards)


EARLY = ("odd_w_in", "odd_w_out", "mlp_w1_l1", "mlp_w2_l1")
MID = ("even_w_out", "mlp_w1_l0", "mlp_w2_l0")
LATE = ("even_w_in",)


def _local_step(x3, tgt3, w, w_main0, w_dt0, pair_reduce=None, late=None):
    nb, s, d = x3.shape
    carries, arrived = late if late else ({}, None)
    t = nb * s
    x0 = x3.reshape(t, d)
    tgt = tgt3.reshape(t, d)
    grads = {}
    row = lambda v: v.reshape(1, -1)
    to3 = lambda v: v.reshape(nb, s, v.shape[-1])
    to2 = lambda v: v.reshape(-1, v.shape[-1])

    conv_w, conv_b = w["even_conv_w"][0], row(w["even_conv_b"][0])
    nmix0 = row(w["norm_mix"][0])
    (h0,) = _pw_fwd("l0_norm", _f_norm, [(x0, 0)], [(nmix0, 0)], [BF16], 1024, 1)
    proj0 = _mm("l0_proj", h0, w_main0, "nn")
    dt_raw = _mm("l0_proj_dt", h0, w_dt0, "nn")
    conv2, xbc3 = _conv_fwd(to3(proj0), conv_w, conv_b, 0, 2, True)
    u_lru = to2(_conv_fwd(to3(proj0), conv_w, conv_b, 2, 1, False))
    xbc = to2(xbc3)
    dt_bias = _pad_lanes(w["ssd_dt_bias"][0].reshape(1, 32))
    (dt,) = _pw_fwd("l0_dt", _f_softplus, [(dt_raw, 0)], [(dt_bias, 0)], [F32], 128, 1)
    dt3 = to3(dt)
    alog = _pad_lanes(w["ssd_a_log"][0].reshape(1, 32))
    def merge(upd):
        out = dict(w)
        for k, v in upd.items():
            if isinstance(k, tuple):
                both = list(out.get(k[0]) or [None, None])
                both[k[1]] = v
                out[k[0]] = both
            else:
                out[k] = v
        return out

    ssd = [_ssd_fwd(xbc3, dt3, alog, r, carry=carries.get(key)) for r, key in zip(DIRS, ("ssd0", "ssd1"))]
    if late:
        w = merge(arrived("ssd0", ssd[0][2:]))
        w = merge(arrived("ssd1", ssd[1][2:]))
    yf, yb = to2(ssd[0][0]), to2(ssd[1][0])
    dskip = jnp.repeat(w["ssd_d"][0], SSD_HEADDIM).reshape(1, 1024)
    snw = row(w["ssd_norm_w"][0])
    ssd_ins = [(yf, 0), (yb, 0), (xbc, 0), (proj0, 3)]
    (ya,) = _pw_fwd("l0_ssd_post", _f_ssd_post, ssd_ins, [(dskip, 0), (snw, 0)], [BF16], 1024, 1, groups=SSD_GROUPS)
    w_gates = [_block_diag(w[k][0, r]).astype(MXU_DTYPE) for r in range(2) for k in ("lru_w_a", "lru_w_x")]
    pre = [_mm(f"l0_lru_pre{i}", u_lru, wg, "nn") for i, wg in enumerate(w_gates)]
    lru_par = [[(row(w[k][0, r]), 0) for k in ("lru_b_a", "lru_b_x", "lru_lambda")] for r in range(2)]
    lru_ins = [[(pre[2 * r], 0), (pre[2 * r + 1], 0), (u_lru, 0)] for r in range(2)]
    ab = [_pw_fwd(f"l0_lru_gates{r}", _f_lru_gates, lru_ins[r], lru_par[r], [F32, F32], 1024, 1) for r in range(2)]
    hs = [_lru_scan(to3(ab[r][0]), to3(ab[r][1]), DIRS[r]) for r in range(2)]
    lru_post_ins = [(to2(hs[0]), 0), (to2(hs[1]), 0), (proj0, 4)]
    (ybm,) = _pw_fwd("l0_lru_post", _f_lru_post, lru_post_ins, [], [BF16], 1024, 1)
    w_out0 = w["even_w_out"][0]
    x1 = _mm("l0_out_a", ya, w_out0[:1024], "nn", res=x0)
    x1 = _mm("l0_out_b", ybm, w_out0[1024:], "nn", res=x1)
    nmlp0 = row(w["norm_mlp"][0])
    x2, mlp0, got = _mlp_fwd("l0_mlp", x1, nmlp0, w["mlp_w1"][0], w["mlp_w2"][0], carry=carries.get("odd"))
    if late:
        w = merge(arrived("odd", got))

    w_in1 = w["odd_w_in"][0]
    nmix1 = row(w["norm_mix"][1])
    (h1,) = _pw_fwd("l1_norm", _f_norm, [(x2, 0)], [(nmix1, 0)], [BF16], 1024, 1)
    proj1 = _mm("l1_proj", h1, w_in1, "nn")
    proj1_3 = to3(proj1)
    lb0, lb1 = row(w["hgrn_lb_logits"][0]), row(w["hgrn_lb_logits"][1])
    gla = [_gla_fwd(proj1_3, lb0, lb1, r, carry=carries.get(key)) for r, key in zip(DIRS, ("gla0", "gla1"))]
    if late:
        w = merge(arrived("gla0", gla[0][2:]))
        w = merge(arrived("gla1", gla[1][2:]))
    hnw = row(w["hgrn_norm_w"][0])
    hpost_ins = [(to2(gla[0][0]), 0), (to2(gla[1][0]), 0), (proj1, 4)]
    (yo,) = _pw_fwd("l1_hgrn_post", _f_hgrn_post, hpost_ins, [(hnw, 0)], [BF16], 1024, 1, groups=HGRN_HEADS)
    w_out1 = w["odd_w_out"][0]
    x3_ = _mm("l1_out", yo, w_out1, "nn", res=x2)
    nmlp1 = row(w["norm_mlp"][1])
    x4, mlp1, _ = _mlp_fwd("l1_mlp", x3_, nmlp1, w["mlp_w1"][1], w["mlp_w2"][1])

    dx4, dnf, loss = _loss_head(x4, tgt, row(w["norm_final"]))
    grads["norm_final"] = dnf.reshape(-1)

    dx3, dw1_1, dw2_1, dnmlp1, _ = _mlp_bwd("l1_mlp", x3_, nmlp1, w["mlp_w1"][1], w["mlp_w2"][1], mlp1, dx4)
    big = {"odd_w_out": _mm("l1_dwout", yo, dx3, "tn").reshape(4, 256, 1024)}
    dyo = _mm("l1_dyo", dx3, w_out1, "nt")
    (do, dgate1), (dhnw,) = _pw_bwd("l1_hgrn_post_b", _f_hgrn_post, hpost_ins, [(hnw, 0)], [dyo], 1024, 1, [0, 2],
                                    out_dtypes=[F32, BF16], groups=HGRN_HEADS, tm=ROWS_FWD)
    grads["hgrn_norm_w"] = dhnw
    do3 = to3(do)
    gb = [_gla_bwd(proj1_3, lb0, lb1, gla[0][1], do3, False)]
    gb.append(_gla_bwd(proj1_3, lb0, lb1, gla[1][1], do3, True, add_to=(gb[0][0], gb[0][2])))
    grads["hgrn_lb_logits"] = jnp.concatenate([gb[0][3] + gb[1][3], gb[0][4] + gb[1][4]], axis=0)
    dparts1 = [to2(gb[1][0]), to2(gb[0][1]), to2(gb[1][1]), to2(gb[1][2]), dgate1]
    dwin1 = jnp.concatenate([_mm(f"l1_dwin{i}", h1, dp, "tn") for i, dp in enumerate(dparts1)], axis=1)
    big["odd_w_in"] = dwin1.reshape(1024, 4, 1280).transpose(1, 0, 2)
    dx2, dnmix1 = _mm_sum_nt("l1_dh", dparts1, [(w_in1, i) for i in range(5)], norm_bwd=(x2, nmix1, dx3))
    big["mlp_w1_l1"], big["mlp_w2_l1"] = dw1_1, dw2_1.reshape(4, 1024, 1024)
    box = {}

    def mlp0_bwd(carry=None):
        box["mlp0"] = _mlp_bwd("l0_mlp", x1, nmlp0, w["mlp_w1"][0], w["mlp_w2"][0], mlp0, dx2, carry=carry)
        return box["mlp0"][4]

    early_sums = tuple(pair_reduce(EARLY, [big[n] for n in EARLY], mlp0_bwd)) if pair_reduce else tuple(mlp0_bwd())

    dx1, dw1_0, dw2_0, dnmlp0 = box["mlp0"][:4]
    big["mlp_w1_l0"], big["mlp_w2_l0"] = dw1_0, dw2_0.reshape(4, 1024, 1024)
    grads["norm_mlp"] = jnp.concatenate([dnmlp0, dnmlp1], axis=0)
    big["even_w_out"] = jnp.concatenate([_mm("l0_dwout_a", ya, dx1, "tn"), _mm("l0_dwout_b", ybm, dx1, "tn")],
                                        axis=0).reshape(4, 512, 1024)
    dya = _mm("l0_dya", dx1, w_out0[:1024], "nt")
    dyb = _mm("l0_dyb", dx1, w_out0[1024:], "nt")
    (dh, dgate0), _ = _pw_bwd("l0_lru_post_b", _f_lru_post, lru_post_ins, [], [dyb], 1024, 1, [0, 2], out_dtypes=[F32, BF16],
                               tm=ROWS_FWD)
    dh3 = to3(dh)

    def lru0_bwd(carry=None):
        box["lru0"] = _lru_scan_bwd(to3(ab[0][0]), hs[0], dh3, DIRS[0], carry=carry)
        return box["lru0"][2:]

    mid_sums = tuple(pair_reduce(MID, [big[n] for n in MID], lru0_bwd)) if pair_reduce else tuple(lru0_bwd())
    dpre, du_parts, dlru = [], [], {k: [] for k in ("lru_b_a", "lru_b_x", "lru_lambda")}
    for r in range(2):
        g_r, da_r = box["lru0"][:2] if r == 0 else _lru_scan_bwd(to3(ab[r][0]), hs[r], dh3, DIRS[r])
        (dpa, dpx, du_r), (dba, dbx, dlam) = _pw_bwd(f"l0_lru_gates_b{r}", _f_lru_gates, lru_ins[r], lru_par[r],
                                                     [to2(da_r), to2(g_r)], 1024, 1, [0, 1, 2],
                                                     out_dtypes=[BF16, BF16, F32])
        dpre += [dpa, dpx]
        du_parts.append(du_r)
        dlru["lru_b_a"].append(dba)
        dlru["lru_b_x"].append(dbx)
        dlru["lru_lambda"].append(dlam)
    for k, v in dlru.items():
        grads[k] = jnp.concatenate(v, axis=0)[None]
    dwg = [_diag_blocks(_mm(f"l0_dwgate{i}", u_lru, dp, "tn")) for i, dp in enumerate(dpre)]
    grads["lru_w_a"] = jnp.stack([dwg[0], dwg[2]])[None]
    grads["lru_w_x"] = jnp.stack([dwg[1], dwg[3]])[None]
    du = _mm_sum_nt("l0_du", dpre, [(wg, 0) for wg in w_gates], add=du_parts)
    (dy, dxs_skip, dz), (ddskip, dsnw) = _pw_bwd("l0_ssd_post_b", _f_ssd_post, ssd_ins, [(dskip, 0), (snw, 0)], [dya],
                                                 1024, 1, [0, 2, 3], out_dtypes=[F32, F32, BF16], groups=SSD_GROUPS,
                                                 tm=ROWS_FWD)
    grads["ssd_d"] = ddskip.reshape(SSD_HEADS, SSD_HEADDIM).sum(axis=1)[None]
    grads["ssd_norm_w"] = dsnw
    dy3 = to3(dy)
    sb0 = _ssd_bwd(xbc3, dt3, alog, ssd[0][1], dy3, False, scatter=early_sums)
    sb1 = _ssd_bwd(xbc3, dt3, alog, ssd[1][1], dy3, True, add_to=(sb0[0], to3(dxs_skip), sb0[1], sb0[2]), scatter=mid_sums)
    grads["ssd_a_log"] = (sb0[3] + sb1[3])[:, :32].reshape(1, 2, 16)
    ddt = to2(sb1[2])
    (ddt_raw,), (ddtb,) = _pw_bwd("l0_dt_b", _f_softplus, [(dt_raw, 0)], [(dt_bias, 0)], [ddt], 128, 1, [0])
    grads["ssd_dt_bias"] = ddtb[:, :32].reshape(1, 2, 16)
    cb = [_conv_bwd(sb1[0], to3(proj0), conv_w, 0, conv2), _conv_bwd(sb1[1], to3(proj0), conv_w, 1, conv2),
          _conv_bwd(to3(du), to3(proj0), conv_w, 2)]
    dcw = jnp.concatenate([c_[1] for c_ in cb], axis=1)
    grads["even_conv_w"] = dcw[:4][None]
    grads["even_conv_b"] = dcw[4:5]
    dparts0 = [to2(c_[0]) for c_ in cb] + [dz, dgate0]
    dwin0 = [_mm(f"l0_dwin{i}", h0, dp, "tn") for i, dp in enumerate(dparts0)]
    big["even_w_in"] = _split_in0(dwin0, _mm("l0_dwin_dt", h0, ddt_raw, "tn"))
    dx0, dnmix0 = _mm_sum_nt("l0_dh", dparts0 + [ddt_raw], [(w_main0, i) for i in range(5)] + [(w_dt0, 0)],
                             norm_bwd=(x0, nmix0, dx1))
    grads["norm_mix"] = jnp.concatenate([dnmix0, dnmix1], axis=0)
    return loss, dx0.reshape(nb, s, d), grads, big, (early_sums + mid_sums, sb0[4:] + sb1[4:])


ANY = pl.BlockSpec(memory_space=pl.ANY)


def _place():
    return lax.axis_index("x"), lax.axis_index("y"), lax.axis_index("c")


def _remote(src, dst, send_sems, recv_sems, k, to):
    return pltpu.make_async_remote_copy(src_ref=src, dst_ref=dst, send_sem=send_sems.at[k], recv_sem=recv_sems.at[k],
                                        device_id=to, device_id_type=MESH)


def _gather_start(x_refs, out_refs, send_sems, recv_sems, finish=False):
    n = len(x_refs)
    halves = [r.shape[0] // 2 for r in x_refs]
    x, y, c = _place()
    sibling = (x, y, 1 - c)
    chips = [(1 - x, y), (x, 1 - y), (1 - x, 1 - y)]

    def blk(t, px, py, hc):
        return out_refs[t].at[2 * px + py, pl.ds(hc * halves[t], halves[t]), :]

    def src(t):
        return x_refs[t].at[pl.ds(c * halves[t], halves[t]), :]

    first = [_remote(src(t), blk(t, x, y, c), send_sems, recv_sems, 6 * t + j, (*chip, c))
             for t in range(n) for j, chip in enumerate(chips)]
    if not finish:
        for cp in first:
            cp.start()
        return
    passed = []
    for t in range(n):
        for j, chip in enumerate(chips):
            _remote(src(t), blk(t, *chip, c), send_sems, recv_sems, 6 * t + j, (*chip, c)).wait_recv()
            cp = _remote(blk(t, *chip, c), blk(t, *chip, c), send_sems, recv_sems, 6 * t + 3 + j, sibling)
            cp.start()
            passed.append(cp)
    for t in range(n):
        for j, chip in enumerate(chips):
            _remote(src(t), blk(t, *chip, 1 - c), send_sems, recv_sems, 6 * t + 3 + j, sibling).wait_recv()
    for cp in first + passed:
        cp.wait_send()


_gather_finish = functools.partial(_gather_start, finish=True)


def _gather_carry(shards):
    n = len(shards)
    return (list(shards), [jax.ShapeDtypeStruct((4,) + s.shape, s.dtype) for s in shards],
            [pltpu.SemaphoreType.DMA((6 * n,)), pltpu.SemaphoreType.DMA((6 * n,))], _gather_start, _gather_finish)


def _gather_chips(shards):
    n = len(shards)
    srcs, shapes, scratch, start, finish = _gather_carry(shards)

    def body(*refs):
        start(refs[:n], refs[n:2 * n], *refs[2 * n:])
        finish(refs[:n], refs[n:2 * n], *refs[2 * n:])

    return _pcall(body, name="gather_weights", in_specs=[ANY] * n, out_specs=(ANY,) * n, out_shape=tuple(shapes),
                  scratch_shapes=scratch, compiler_params=_params())(*shards)


def _pair_swap_start(g_refs, land_refs, send_sems, recv_sems, finish=False):
    x, y, c = _place()
    cps = []
    for t, g in enumerate(g_refs):
        half = g.shape[1] // 2
        cps += [_remote(g.at[j, pl.ds((1 - c) * half, half), :], land_refs[t].at[j], send_sems, recv_sems, 4 * t + j,
                        (x, y, 1 - c)) for j in range(4)]
    for cp in cps:
        cp.wait() if finish else cp.start()


_pair_swap_finish = functools.partial(_pair_swap_start, finish=True)


def _pair_swap_carry(gps):
    n = len(gps)
    return (list(gps), [jax.ShapeDtypeStruct((4, g.shape[1] // 2, g.shape[2]), F32) for g in gps],
            [pltpu.SemaphoreType.DMA((4 * n,)), pltpu.SemaphoreType.DMA((4 * n,))], _pair_swap_start, _pair_swap_finish)


def _pair_swap(name, gps):
    n = len(gps)
    srcs, shapes, scratch, start, finish = _pair_swap_carry(gps)

    def body(*refs):
        start(refs[:n], refs[n:2 * n], *refs[2 * n:])
        finish(refs[:n], refs[n:2 * n], *refs[2 * n:])

    return _pcall(body, name=f"pair_swap_{name}", in_specs=[ANY] * n, out_specs=(ANY,) * n, out_shape=tuple(shapes),
                  scratch_shapes=scratch, compiler_params=_params())(*gps)


def _pair_add(name, gp, land, cidx):
    _, half, cols = land.shape
    tr = _tile(half, 512)
    nh = half // tr

    def body(c_ref, g_ref, l_ref, o_ref):
        o_ref[...] = (g_ref[...] + l_ref[...]).astype(o_ref.dtype)

    grid_spec = pltpu.PrefetchScalarGridSpec(
        num_scalar_prefetch=1, grid=(4, nh),
        in_specs=[pl.BlockSpec((None, tr, cols), lambda j, i, c: (j, c[0] * nh + i, 0)),
                  pl.BlockSpec((None, tr, cols), lambda j, i, c: (j, i, 0))],
        out_specs=pl.BlockSpec((None, tr, cols), lambda j, i, c: (j, i, 0)))
    return _pcall(body, name=f"pair_add_{name}", grid_spec=grid_spec, out_shape=jax.ShapeDtypeStruct((4, half, cols), BF16),
                  compiler_params=_params())(cidx, gp, land)


def _scatter_copies(s_refs, land_refs, send_sems, recv_sems):
    x, y, c = _place()
    me = 2 * x + y
    chips = [(1 - x, y), (x, 1 - y), (1 - x, 1 - y)]
    pairs = [(t, j, px, py) for t in range(len(s_refs)) for j, (px, py) in enumerate(chips)]
    sends = [_remote(s_refs[t].at[2 * px + py], land_refs[t].at[me], send_sems, recv_sems, 3 * t + j, (px, py, c))
             for t, j, px, py in pairs]
    arrivals = [_remote(s_refs[t].at[me], land_refs[t].at[2 * px + py], send_sems, recv_sems, 3 * t + j, (px, py, c))
                for t, j, px, py in pairs]
    return sends, arrivals


def _scatter_scratch(n):
    return [pltpu.SemaphoreType.DMA((3 * n,)), pltpu.SemaphoreType.DMA((3 * n,))]


def _chip_scatter(name, css):
    n = len(css)

    def body(*refs):
        sends, arrivals = _scatter_copies(refs[:n], refs[n:2 * n], *refs[2 * n:])
        for cp in sends:
            cp.start()
        for cp in arrivals:
            cp.wait_recv()
        for cp in sends:
            cp.wait_send()

    return _pcall(body, name=f"chip_scatter_{name}", in_specs=[ANY] * n, out_specs=(ANY,) * n,
                  out_shape=tuple(jax.ShapeDtypeStruct(s.shape, s.dtype) for s in css),
                  scratch_shapes=_scatter_scratch(n), compiler_params=_params())(*css)


def _chip_sum(name, land):
    _, half, cols = land.shape
    tr = _tile(half, 512)

    def body(l_ref, o_ref):
        o_ref[...] = ((l_ref[0].astype(F32) + l_ref[1].astype(F32)) + l_ref[2].astype(F32)) + l_ref[3].astype(F32)

    return _pcall(body, name=f"chip_sum_{name}", grid=(half // tr,),
                  in_specs=[pl.BlockSpec((4, tr, cols), lambda i: (0, i, 0))],
                  out_specs=pl.BlockSpec((tr, cols), lambda i: (i, 0)),
                  out_shape=jax.ShapeDtypeStruct((half, cols), F32), compiler_params=_params())(land)


def _pair_join(reds):
    n = len(reds)

    def body(*refs):
        r_refs, out_refs = refs[:n], refs[n:2 * n]
        send_sems, recv_sems = refs[2 * n:]
        x, y, c = _place()
        cps = [_remote(r_refs[t], out_refs[t].at[c], send_sems, recv_sems, t, (x, y, 1 - c)) for t in range(n)]
        for cp in cps:
            cp.start()
        for t in range(n):
            _remote(r_refs[t], out_refs[t].at[1 - c], send_sems, recv_sems, t, (x, y, 1 - c)).wait_recv()
        for cp in cps:
            cp.wait_send()

    return _pcall(body, name="grad_pair_join", in_specs=[ANY] * n, out_specs=(ANY,) * n,
                  out_shape=tuple(jax.ShapeDtypeStruct((2,) + r.shape, F32) for r in reds),
                  scratch_shapes=[pltpu.SemaphoreType.DMA((n,)), pltpu.SemaphoreType.DMA((n,))],
                  compiler_params=_params())(*reds)


def _adamw(name, g, w, m, v):
    rows, cols = g.shape
    tr = _tile(rows, 512)

    def body(g_ref, w_ref, m_ref, v_ref, d_ref, mo_ref, vo_ref):
        gv = g_ref[...]
        mn = ADAM_B1 * m_ref[...] + (1.0 - ADAM_B1) * gv
        vn = ADAM_B2 * v_ref[...] + (1.0 - ADAM_B2) * jnp.square(gv)
        m_hat = mn / (1.0 - ADAM_B1 ** ADAM_STEP)
        v_hat = vn / (1.0 - ADAM_B2 ** ADAM_STEP)
        d_ref[...] = -ADAM_LR * (m_hat / (jnp.sqrt(v_hat) + ADAM_EPS) + ADAM_WD * w_ref[...])
        mo_ref[...] = mn
        vo_ref[...] = vn

    blk = pl.BlockSpec((tr, cols), lambda i: (i, 0))
    shp = jax.ShapeDtypeStruct((rows, cols), F32)
    return _pcall(body, name=f"adamw_{name}", grid=(rows // tr,), in_specs=[blk] * 4, out_specs=(blk,) * 3,
                  out_shape=(shp,) * 3, compiler_params=_params())(g, w, m, v)


def _pack(pieces, rows, dtype):
    flat = jnp.concatenate([p.reshape(-1).astype(dtype) for p in pieces])
    return jnp.pad(flat, (0, rows * PACK_COLS - flat.shape[0])).reshape(rows, PACK_COLS)


def _unpack(pack, shapes):
    flat = pack.reshape(-1)
    out, off = [], 0
    for shp in shapes:
        n = math.prod(shp)
        out.append(flat[off:off + n].reshape(shp))
        off += n
    return out


def _shard_of(full, axis, j):
    n = full.shape[axis] // 4
    return lax.slice_in_dim(full, j * n, (j + 1) * n, axis=axis)


def kernel(x, even_w_in, even_conv_w, even_conv_b, ssd_a_log, ssd_dt_bias, ssd_d, ssd_norm_w, lru_w_a, lru_b_a, lru_w_x, lru_b_x, lru_lambda, even_w_out, odd_w_in, hgrn_lb_logits, hgrn_norm_w, odd_w_out, norm_mix, norm_mlp, mlp_w1, mlp_w2, norm_final, loss_target, m_even_w_in, m_even_conv_w, m_even_conv_b, m_ssd_a_log, m_ssd_dt_bias, m_ssd_d, m_ssd_norm_w, m_lru_w_a, m_lru_b_a, m_lru_w_x, m_lru_b_x, m_lru_lambda, m_even_w_out, m_odd_w_in, m_hgrn_lb_logits, m_hgrn_norm_w, m_odd_w_out, m_norm_mix, m_norm_mlp, m_mlp_w1, m_mlp_w2, m_norm_final, v_even_w_in, v_even_conv_w, v_even_conv_b, v_ssd_a_log, v_ssd_dt_bias, v_ssd_d, v_ssd_norm_w, v_lru_w_a, v_lru_b_a, v_lru_w_x, v_lru_b_x, v_lru_lambda, v_even_w_out, v_odd_w_in, v_hgrn_lb_logits, v_hgrn_norm_w, v_odd_w_out, v_norm_mix, v_norm_mlp, v_mlp_w1, v_mlp_w2, v_norm_final):
    names = [n for n, _, _, _ in WEIGHTS]
    w_loc = dict(zip(names, (even_w_in, even_conv_w, even_conv_b, ssd_a_log, ssd_dt_bias, ssd_d, ssd_norm_w, lru_w_a, lru_b_a, lru_w_x, lru_b_x, lru_lambda, even_w_out, odd_w_in, hgrn_lb_logits, hgrn_norm_w, odd_w_out, norm_mix, norm_mlp, mlp_w1, mlp_w2, norm_final)))
    m_loc = dict(zip(names, (m_even_w_in, m_even_conv_w, m_even_conv_b, m_ssd_a_log, m_ssd_dt_bias, m_ssd_d, m_ssd_norm_w, m_lru_w_a, m_lru_b_a, m_lru_w_x, m_lru_b_x, m_lru_lambda, m_even_w_out, m_odd_w_in, m_hgrn_lb_logits, m_hgrn_norm_w, m_odd_w_out, m_norm_mix, m_norm_mlp, m_mlp_w1, m_mlp_w2, m_norm_final)))
    v_loc = dict(zip(names, (v_even_w_in, v_even_conv_w, v_even_conv_b, v_ssd_a_log, v_ssd_dt_bias, v_ssd_d, v_ssd_norm_w, v_lru_w_a, v_lru_b_a, v_lru_w_x, v_lru_b_x, v_lru_lambda, v_even_w_out, v_odd_w_in, v_hgrn_lb_logits, v_hgrn_norm_w, v_odd_w_out, v_norm_mix, v_norm_mlp, v_mlp_w1, v_mlp_w2, v_norm_final)))
    spec = {n: (blk, full, ax) for n, blk, full, ax in WEIGHTS}

    small = [n for n in names if n not in BIG]
    two_d = lambda n, v: v.reshape(BIG_2D[n])

    me = 2 * lax.axis_index("x") + lax.axis_index("y")
    cc = lax.axis_index("c")
    put = lambda whole, part, k: lax.dynamic_update_slice_in_dim(whole, part[None], k, axis=0)
    own = {n: two_d(n, w_loc[n]).astype(BF16) for n in BIG if not n.startswith("mlp")}
    own["small"] = _pack([w_loc[n] for n in SMALL_SHARDED], 16, F32)
    fill = lambda got, keys: [put(g, own[k], me) for g, k in zip(got, keys)]
    first = ("even_w_in", "small")
    g_in0, g_small = fill(_gather_chips([own[k] for k in first]), first)
    w_main0, w_dt0 = _assemble_in0(g_in0)
    w_full = {n: w_loc[n] for n in names if spec[n][2] is None}
    shards = [_unpack(g_small[j], [spec[n][0] for n in SMALL_SHARDED]) for j in range(4)]
    for n in ("mlp_w1", "mlp_w2"):
        for l in range(2):
            own[f"{n}_l{l}"] = w_loc[n][l].astype(BF16)
    riders = {"ssd0": ("mlp_w1_l0", "even_w_out"), "ssd1": ("mlp_w2_l0",), "odd": ("odd_w_in", "odd_w_out"),
              "gla0": ("mlp_w1_l1",), "gla1": ("mlp_w2_l1",)}
    carries = {key: _gather_carry([own[k] for k in ks]) for key, ks in riders.items()}

    def arrived(key, got):
        out = {}
        for k, g in zip(riders[key], fill(got, riders[key])):
            if k == "odd_w_in":
                out[k] = jnp.concatenate([g[j] for j in range(4)], axis=1)[None]
            elif k in ("odd_w_out", "even_w_out"):
                out[k] = g.reshape(spec[k][1])
            else:
                out[(k[:6], int(k[-1]))] = g if k.startswith("mlp_w1") else g.reshape(4096, 1024)
        return out

    for i, n in enumerate(SMALL_SHARDED):
        w_full[n] = jnp.concatenate([shards[j][i] for j in range(4)], axis=spec[n][2])

    cidx = cc.astype(jnp.int32).reshape(1)

    def pair_reduce(tags, tensors, run=None):
        lands = run(_pair_swap_carry(tensors)) if run else _pair_swap(tags[0], tensors)
        return [_pair_add(tag, g, land, cidx) for tag, g, land in zip(tags, tensors, lands)]

    loss_vec, grad_x, grads, big, (early_sums, early_landed) = _local_step(
        x, loss_target, w_full, w_main0, w_dt0, pair_reduce, (carries, arrived))
    loss = lax.psum(loss_vec[0, 0], ("x", "y", "c"))

    def dest_pack(j):
        return _pack([grads[n].reshape(spec[n][1]) if spec[n][2] is None else _shard_of(grads[n].reshape(spec[n][1]), spec[n][2], j)
                      for n in small], SMALL_ROWS, F32)

    late_tags = LATE + ("small",)
    late_sums = pair_reduce(late_tags, [big[n] for n in LATE] + [jnp.stack([dest_pack(j) for j in range(4)])])
    tags = EARLY + MID + late_tags
    chip_sums = list(early_sums) + late_sums
    landed = [put(land, lax.dynamic_index_in_dim(cs, me, axis=0, keepdims=False), me)
              for land, cs in zip(list(early_landed) + list(_chip_scatter("late", late_sums)), chip_sums)]
    halves = [_chip_sum(tag, land) for tag, land in zip(tags, landed)]
    red = {tag: put(r, h, cc).reshape(-1, r.shape[-1]) for tag, r, h in zip(tags, _pair_join(halves), halves)}
    for n in ("mlp_w1", "mlp_w2"):
        red[n] = jnp.concatenate([red[n + "_l0"], red[n + "_l1"]], axis=0)

    outs = {}
    for n, g in ((n, red[n]) for n in BIG):
        res = (g, *_adamw(n, g, two_d(n, w_loc[n]), two_d(n, m_loc[n]), two_d(n, v_loc[n])))
        outs[n] = [r.reshape(spec[n][0]) for r in res]
    blocks = [spec[n][0] for n in small]
    wp, mp, vp = (_pack([src[n] for n in small], SMALL_ROWS, F32) for src in (w_loc, m_loc, v_loc))
    res = (red["small"], *_adamw("small", red["small"], wp, mp, vp))
    unpacked = [_unpack(r, blocks) for r in res]
    for i, n in enumerate(small):
        outs[n] = [u[i] for u in unpacked]
    return (loss, grad_x, *[outs[n][k] for k in range(4) for n in names])
```

```python
import functools
import math

import jax
import jax.numpy as jnp
from jax import lax
from jax.experimental import pallas as pl
from jax.experimental.pallas import tpu as pltpu

F32 = jnp.float32
BF16 = jnp.bfloat16
MXU_DTYPE = jnp.bfloat16
MESH = pl.DeviceIdType.MESH

EPS = 1e-6
SSD_HEADS = 16
SSD_HEADDIM = 64
HEAD_SHIFT = 6
SSD_GROUPS = 4
SSD_CHUNK = 128
LRU_C = 8.0
LRU_ROWS = 256
HGRN_HEADS = 8
HGRN_HEADDIM = 128
HGRN_SUB = 32
HGRN_SUB_SHIFT = 5
HGRN_BLOCK = 128
HGRN_SCALE = HGRN_HEADDIM ** -0.5
CONV_ROWS = 512
ROWS_FWD = 512
ROWS_BWD = 256

ADAM_LR = 0.001
ADAM_B1 = 0.9
ADAM_B2 = 0.999
ADAM_EPS = 1e-08
ADAM_WD = 0.01
ADAM_STEP = 10

VMEM_LIMIT = 56 * 1024 * 1024
PACK_COLS = 1024
SMALL_ROWS = 288

WEIGHTS = (
    ("even_w_in", (1, 1024, 1288), (1, 1024, 5152), 2),
    ("even_conv_w", (1, 4, 768), (1, 4, 3072), 2),
    ("even_conv_b", (1, 3072), (1, 3072), None),
    ("ssd_a_log", (1, 2, 16), (1, 2, 16), None),
    ("ssd_dt_bias", (1, 2, 16), (1, 2, 16), None),
    ("ssd_d", (1, 16), (1, 16), None),
    ("ssd_norm_w", (1, 1024), (1, 1024), None),
    ("lru_w_a", (1, 2, 16, 64, 64), (1, 2, 16, 64, 64), None),
    ("lru_b_a", (1, 2, 256), (1, 2, 1024), 2),
    ("lru_w_x", (1, 2, 16, 64, 64), (1, 2, 16, 64, 64), None),
    ("lru_b_x", (1, 2, 256), (1, 2, 1024), 2),
    ("lru_lambda", (1, 2, 256), (1, 2, 1024), 2),
    ("even_w_out", (1, 512, 1024), (1, 2048, 1024), 1),
    ("odd_w_in", (1, 1024, 1280), (1, 1024, 5120), 2),
    ("hgrn_lb_logits", (2, 1024), (2, 1024), None),
    ("hgrn_norm_w", (1, 256), (1, 1024), 1),
    ("odd_w_out", (1, 256, 1024), (1, 1024, 1024), 1),
    ("norm_mix", (2, 1024), (2, 1024), None),
    ("norm_mlp", (2, 1024), (2, 1024), None),
    ("mlp_w1", (2, 1024, 1024), (2, 1024, 4096), 2),
    ("mlp_w2", (2, 1024, 1024), (2, 4096, 1024), 1),
    ("norm_final", (1024,), (1024,), None),
)
BIG = ("even_w_in", "even_w_out", "odd_w_in", "odd_w_out", "mlp_w1", "mlp_w2")
BIG_2D = {"even_w_in": (1024, 1288), "even_w_out": (512, 1024), "odd_w_in": (1024, 1280), "odd_w_out": (256, 1024),
          "mlp_w1": (2048, 1024), "mlp_w2": (2048, 1024)}
SMALL_SHARDED = ("even_conv_w", "lru_b_a", "lru_b_x", "lru_lambda", "hgrn_norm_w")


def _pcall(body, carry=None, **kw):
    if carry is not None:
        srcs, shapes, scratch, start, finish = carry
        grid, inner = kw["grid"], body
        as_tuple = lambda v: tuple(v) if isinstance(v, (tuple, list)) else (v,)
        out_specs, out_shape, own_scratch = as_tuple(kw["out_specs"]), as_tuple(kw["out_shape"]), list(kw.get("scratch_shapes", ()))
        a = len(kw["in_specs"])
        b = a + len(srcs)
        c = b + len(out_specs)
        d = c + len(shapes)
        e = d + len(own_scratch)

        def body(*refs):
            ids = [pl.program_id(ax) for ax in range(len(grid))]
            first = functools.reduce(jnp.logical_and, [i == 0 for i in ids])
            last = functools.reduce(jnp.logical_and, [i == g - 1 for i, g in zip(ids, grid)])
            pl.when(first)(lambda: start(refs[a:b], refs[c:d], *refs[e:]))
            inner(*refs[:a], *refs[b:c], *refs[d:e])
            pl.when(last)(lambda: finish(refs[a:b], refs[c:d], *refs[e:]))

        kw = dict(kw, in_specs=list(kw["in_specs"]) + [ANY] * len(srcs), out_specs=out_specs + (ANY,) * len(shapes),
                  out_shape=out_shape + tuple(shapes), scratch_shapes=own_scratch + list(scratch))
    return pl.pallas_call(body, **kw)


def _params(**kw):
    return pltpu.CompilerParams(vmem_limit_bytes=VMEM_LIMIT, **kw)


def _tile(n, pref):
    if n <= pref:
        return n
    t = (pref // 128) * 128
    while n % t:
        t -= 128
    return t


def _dot(a, b, dims=(((1,), (0,)), ((), ()))):
    return lax.dot_general(a, b, dims, preferred_element_type=F32)


_NN = (((1,), (0,)), ((), ()))
_NT = (((1,), (1,)), ((), ()))
_TN = (((0,), (0,)), ((), ()))


def _mx(v):
    return v.astype(MXU_DTYPE)


def _dot01(a, b, dims=_NN, *, split, terms):
    acc, rest = None, (a if split == "a" else b)
    for _ in range(terms):
        piece = _mx(rest)
        part = _dot(piece, _mx(b), dims) if split == "a" else _dot(_mx(a), piece, dims)
        acc = part if acc is None else acc + part
        rest = rest - piece.astype(F32)
    return acc


def _mm(name, a, b, mode, *, out_dtype=F32, res=None, relu2=False, relu2_of=None, col_shards=1, carry=None):
    shards = b.shape[0] if b.ndim == 3 else 0
    b2 = b.shape[1:] if shards else b.shape
    if mode == "nn":
        (m, kk), n = a.shape, b2[1] * max(shards, 1)
    elif mode == "nt":
        (m, kk), n = a.shape, b2[0]
    else:
        (kk, m), (_, n) = a.shape, b.shape
    assert res is None or relu2_of is None
    tk_pref = 1024
    if mode == "tn" and a.dtype.itemsize == 2 and b.dtype.itemsize == 2:
        tk_pref = 2048
    tm, tn, tk = _tile(m, 1024), _tile(n // col_shards, 1024), _tile(kk, tk_pref)
    nk = kk // tk
    dims = {"nn": _NN, "nt": _NT, "tn": _TN}[mode]
    a_spec = pl.BlockSpec((tk, tm), lambda i, j, k: (k, i)) if mode == "tn" else pl.BlockSpec((tm, tk), lambda i, j, k: (i, k))
    b_spec = pl.BlockSpec((tn, tk), lambda i, j, k: (j, k)) if mode == "nt" else pl.BlockSpec((tk, tn), lambda i, j, k: (k, j))
    if shards and mode == "nn":
        assert tn == b2[1]
        b_spec = pl.BlockSpec((None, tk, tn), lambda i, j, k: (j, k, 0))
    o_spec = pl.BlockSpec((tm, tn), lambda i, j, k: (i, j))
    o_shape = (m, n)
    if col_shards > 1:
        assert tn * col_shards == n and res is None and not relu2
        o_spec = pl.BlockSpec((None, tm, tn), lambda i, j, k: (j, i, 0))
        o_shape = (col_shards, m, tn)
    extra = res if res is not None else relu2_of
    has_res = extra is not None

    def body(*refs):
        a_ref, b_ref = refs[0], refs[1]
        res_ref = refs[2] if has_res else None
        outs = refs[2 + has_res:2 + has_res + 1 + relu2]

        def finish(r):
            if res is not None:
                r = r + res_ref[...]
            if relu2_of is not None:
                r = r * (2.0 * jnp.maximum(res_ref[...].astype(F32), 0.0))
            if relu2:
                outs[0][...] = r.astype(outs[0].dtype)
                outs[1][...] = jnp.square(jnp.maximum(r, 0.0)).astype(outs[1].dtype)
            else:
                outs[0][...] = r.astype(outs[0].dtype)

        prod = _dot(_mx(a_ref[...]), _mx(b_ref[...]), dims)
        if nk == 1:
            finish(prod)
            return
        acc = refs[-1]
        k = pl.program_id(2)

        @pl.when(k == 0)
        def _():
            acc[...] = prod

        @pl.when(k > 0)
        def _():
            acc[...] += prod

        @pl.when(k == nk - 1)
        def _():
            finish(acc[...])

    in_specs = [a_spec, b_spec] + ([o_spec] if has_res else [])
    if relu2:
        out_shape = (jax.ShapeDtypeStruct((m, n), BF16), jax.ShapeDtypeStruct((m, n), BF16))
        out_specs = (o_spec, o_spec)
    else:
        out_shape = jax.ShapeDtypeStruct(o_shape, out_dtype)
        out_specs = o_spec
    args = (a, b) + ((extra,) if has_res else ()) + (tuple(carry[0]) if carry else ())
    return _pcall(body, carry=carry, name=name, grid=(m // tm, n // tn, nk), in_specs=in_specs, out_specs=out_specs,
                  out_shape=out_shape, scratch_shapes=[pltpu.VMEM((tm, tn), F32)] if nk > 1 else [],
                  compiler_params=_params())(*args)


def _mm_sum_nt(name, parts, wblocks, norm_bwd=None, add=()):
    parts = [p if isinstance(p, tuple) else (p, 0, p.shape[1]) for p in parts]
    m, npart = parts[0][0].shape[0], len(parts)
    n = wblocks[0][0].shape[-2]
    tm, tn = _tile(m, 512), _tile(n, 1024)
    assert norm_bwd is None or tn == n

    def body(*refs):
        acc = _dot(_mx(refs[0][...]), _mx(refs[npart][...]), _NT)
        for k in range(1, npart):
            acc = acc + _dot(_mx(refs[k][...]), _mx(refs[npart + k][...]), _NT)
        if norm_bwd is None:
            for r in refs[2 * npart:-1]:
                acc = acc + r[...]
            refs[-1][...] = acc
            return
        x_ref, g_ref, res_ref, dx_ref, dg_ref = refs[2 * npart:]
        _, vjp = jax.vjp(_f_norm, x_ref[...], g_ref[...])
        dx, dg = vjp((acc,))
        dx_ref[...] = dx + res_ref[...]

        @pl.when(pl.program_id(0) == 0)
        def _():
            dg_ref[...] = jnp.zeros_like(dg_ref)

        dg_ref[...] += dg

    row = pl.BlockSpec((tm, tn), lambda i, j: (i, j))
    vec = pl.BlockSpec((1, tn), lambda i, j: (0, j))
    in_specs = [pl.BlockSpec((tm, wd), lambda i, j, cb=cb: (i, cb)) for _, cb, wd in parts]
    for (_, _, wd), (w, cb) in zip(parts, wblocks):
        in_specs.append(pl.BlockSpec((None, tn, wd), lambda i, j, cb=cb: (cb, j, 0)) if w.ndim == 3
                        else pl.BlockSpec((tn, wd), lambda i, j, cb=cb: (j, cb)))
    args = [p for p, _, _ in parts] + [w for w, _ in wblocks]
    if norm_bwd is None:
        return _pcall(body, name=name, grid=(m // tm, n // tn), in_specs=in_specs + [row] * len(add), out_specs=row,
                      out_shape=jax.ShapeDtypeStruct((m, n), F32), compiler_params=_params())(*args, *add)
    return _pcall(body, name=name, grid=(m // tm, 1), in_specs=in_specs + [row, vec, row], out_specs=(row, vec),
                  out_shape=(jax.ShapeDtypeStruct((m, n), F32), jax.ShapeDtypeStruct((1, n), F32)),
                  compiler_params=_params())(*args, *norm_bwd)


def _pw_fwd(name, f, ins, params, out_dtypes, tc, ncol, tm=ROWS_FWD, groups=1):
    t = ins[0][0].shape[0]
    tm = min(tm, t)
    ni, npar = len(ins), len(params)
    gw = tc // groups

    def body(*refs):
        for g in range(groups):
            sl = slice(g * gw, (g + 1) * gw)
            vals = f(*[r[:, sl].astype(F32) for r in refs[:ni]], *[r[:, sl] for r in refs[ni:ni + npar]])
            for o, v in zip(refs[ni + npar:], vals):
                o[:, sl] = v.astype(o.dtype)

    in_specs = [pl.BlockSpec((tm, tc), lambda j, i, off=off: (i, off + j)) for _, off in ins]
    in_specs += [pl.BlockSpec((1, tc), lambda j, i, off=off: (0, off + j)) for _, off in params]
    out_specs = tuple(pl.BlockSpec((tm, tc), lambda j, i: (i, j)) for _ in out_dtypes)
    out_shape = tuple(jax.ShapeDtypeStruct((t, ncol * tc), d) for d in out_dtypes)
    return _pcall(body, name=name, grid=(ncol, t // tm), in_specs=in_specs, out_specs=out_specs, out_shape=out_shape,
                  compiler_params=_params())(*[a for a, _ in ins], *[p for p, _ in params])


def _pw_bwd(name, f, ins, params, douts, tc, ncol, want, adds=None, tm=ROWS_BWD, out_dtypes=None, groups=1):
    adds = adds or {}
    out_dtypes = out_dtypes or [F32] * len(want)
    t = ins[0][0].shape[0]
    tm = min(tm, t)
    ni, npar, nd, na = len(ins), len(params), len(douts), len(adds)
    add_keys = sorted(adds)
    gw = tc // groups

    def body(*refs):
        in_refs, p_refs = refs[:ni], refs[ni:ni + npar]
        d_refs = refs[ni + npar:ni + npar + nd]
        a_refs = refs[ni + npar + nd:ni + npar + nd + na]
        o_refs = refs[ni + npar + nd + na:]
        for p in range(npar):
            @pl.when(pl.program_id(1) == 0)
            def _(o=o_refs[len(want) + p]):
                o[...] = jnp.zeros_like(o)

        for g in range(groups):
            sl = slice(g * gw, (g + 1) * gw)
            _, vjp = jax.vjp(f, *[r[:, sl].astype(F32) for r in in_refs], *[r[:, sl] for r in p_refs])
            cts = vjp(tuple(d[:, sl].astype(F32) for d in d_refs))
            for o, kidx in zip(o_refs[:len(want)], want):
                v = cts[kidx]
                if kidx in adds:
                    v = v + a_refs[add_keys.index(kidx)][:, sl]
                o[:, sl] = v.astype(o.dtype)
            for p in range(npar):
                o_refs[len(want) + p][:, sl] += cts[ni + p]

    in_specs = [pl.BlockSpec((tm, tc), lambda j, i, off=off: (i, off + j)) for _, off in ins]
    in_specs += [pl.BlockSpec((1, tc), lambda j, i, off=off: (0, off + j)) for _, off in params]
    in_specs += [pl.BlockSpec((tm, tc), lambda j, i: (i, j)) for _ in range(nd + na)]
    out_specs = tuple([pl.BlockSpec((tm, tc), lambda j, i: (i, j)) for _ in want]
                      + [pl.BlockSpec((1, tc), lambda j, i: (0, j)) for _ in params])
    out_shape = tuple([jax.ShapeDtypeStruct((t, ncol * tc), dt) for dt in out_dtypes]
                      + [jax.ShapeDtypeStruct((1, ncol * tc), F32) for _ in params])
    res = _pcall(body, name=name, grid=(ncol, t // tm), in_specs=in_specs, out_specs=out_specs, out_shape=out_shape,
                 compiler_params=_params())(*[a for a, _ in ins], *[p for p, _ in params], *douts, *[adds[k] for k in add_keys])
    return list(res[:len(want)]), list(res[len(want):])


def _rms(x, g):
    return (x * lax.rsqrt(jnp.mean(x * x, axis=-1, keepdims=True) + EPS)) * g


def _f_norm(x, g):
    return (_rms(x, g),)


def _f_softplus(d, b):
    return (jax.nn.softplus(d + b),)


def _f_ssd_post(yf, yb, xs, z, dskip, nw):
    u = (yf + yb + dskip * xs) * jax.nn.silu(z)
    return (_rms(u, nw),)


def _neg_expm1(v):
    t = jnp.tanh(0.5 * v)
    return -2.0 * t / (1.0 - t)


def _f_lru_gates(pre_a, pre_x, u, ba, bx, lam):
    rg = jax.nn.sigmoid(pre_a + ba)
    ig = jax.nn.sigmoid(pre_x + bx)
    log_a = -LRU_C * rg * jax.nn.softplus(-lam)
    return jnp.exp(log_a), jnp.sqrt(_neg_expm1(2.0 * log_a)) * (ig * u)


def _f_lru_post(hf, hb, gate):
    return ((hf + hb) * jax.nn.gelu(gate),)


def _f_hgrn_pre(fr, l0, l1):
    lb = jax.nn.sigmoid(l1 - l0)
    k = (1.0 - lb) * jax.nn.sigmoid(-fr)
    return k, jnp.log1p(-k)


def _f_hgrn_post(of, ob, gate, nw):
    return (_rms(of + ob, nw) * jax.nn.silu(gate),)


def _loss_head(x, tgt, g, tm=ROWS_FWD):
    t, d = x.shape
    tm = min(tm, t)

    def body(x_ref, t_ref, g_ref, dx_ref, dg_ref, loss_ref):
        tv = t_ref[...]

        def lf(xv, gv):
            return 0.5 * jnp.sum(jnp.mean(jnp.square(_rms(xv, gv) - tv), axis=-1))

        val, vjp = jax.vjp(lf, x_ref[...], g_ref[...])
        dx, dg = vjp(jnp.ones((), F32))
        dx_ref[...] = dx

        @pl.when(pl.program_id(0) == 0)
        def _():
            dg_ref[...] = jnp.zeros_like(dg_ref)
            loss_ref[...] = jnp.zeros_like(loss_ref)

        dg_ref[...] += dg
        loss_ref[...] += jnp.full(loss_ref.shape, val, F32)

    row = pl.BlockSpec((tm, d), lambda i: (i, 0))
    vec = pl.BlockSpec((1, d), lambda i: (0, 0))
    return _pcall(body, name="loss_head", grid=(t // tm,), in_specs=[row, row, vec],
                  out_specs=(row, vec, pl.BlockSpec((1, 128), lambda i: (0, 0))),
                  out_shape=(jax.ShapeDtypeStruct((t, d), F32), jax.ShapeDtypeStruct((1, d), F32),
                             jax.ShapeDtypeStruct((1, 128), F32)), compiler_params=_params())(x, tgt, g)


def _shifted(x, d, prev, nxt, first, last):
    r = x.shape[0]
    row = lax.broadcasted_iota(jnp.int32, x.shape, 0)
    if d < 0:
        out = pltpu.roll(x, -d, 0)
        for q in range(-d):
            pv = jnp.where(first, 0.0, prev[8 + d + q:8 + d + q + 1, :])
            out = jnp.where(row == q, pv, out)
        return out
    out = pltpu.roll(x, r - d, 0)
    for q in range(d):
        nv = jnp.where(last, 0.0, nxt[q:q + 1, :])
        out = jnp.where(row == r - d + q, nv, out)
    return out


def _conv_fwd(p3, w, b, col0, ncol, silu, tc=1024):
    nbatch, s, _ = p3.shape
    ts = min(CONV_ROWS, s)
    nblk = s // ts

    def body(x_ref, pv_ref, nx_ref, w_ref, b_ref, o_ref, *act_ref):
        i = pl.program_id(1)
        first, last = i == 0, i == nblk - 1
        x, pv, nx = x_ref[...], pv_ref[...], nx_ref[...]
        wv = w_ref[...]
        out = b_ref[...] + wv[1:2] * x
        out = out + wv[0:1] * _shifted(x, -1, pv, nx, first, last)
        out = out + wv[2:3] * _shifted(x, 1, pv, nx, first, last)
        out = out + wv[3:4] * _shifted(x, 2, pv, nx, first, last)
        o_ref[...] = out
        if silu:
            act_ref[0][...] = jax.nn.silu(out)

    nb8 = s // 8
    cur = pl.BlockSpec((None, ts, tc), lambda n, i, j: (n, i, col0 + j))
    prev = pl.BlockSpec((None, 8, tc), lambda n, i, j: (n, jnp.maximum(i * (ts // 8) - 1, 0), col0 + j))
    nxt = pl.BlockSpec((None, 8, tc), lambda n, i, j: (n, jnp.minimum((i + 1) * (ts // 8), nb8 - 1), col0 + j))
    out = pl.BlockSpec((None, ts, tc), lambda n, i, j: (n, i, j))
    shp = jax.ShapeDtypeStruct((nbatch, s, ncol * tc), F32)
    return _pcall(body, name=f"conv_fwd{col0}", grid=(nbatch, nblk, ncol),
                  in_specs=[cur, prev, nxt, pl.BlockSpec((4, tc), lambda n, i, j: (0, col0 + j)),
                            pl.BlockSpec((1, tc), lambda n, i, j: (0, col0 + j))],
                  out_specs=(out, out) if silu else out, out_shape=(shp, shp) if silu else shp,
                  compiler_params=_params())(p3, p3, p3, w, b)


def _conv_bwd(dc3, p3, w, col, conv3=None):
    nbatch, s, tc = dc3.shape
    ts = min(CONV_ROWS, s)
    nblk = s // ts
    silu = conv3 is not None

    def body(d_ref, dpv_ref, dnx_ref, x_ref, pv_ref, nx_ref, w_ref, *rest):
        n, i = pl.program_id(0), pl.program_id(1)
        first, last = i == 0, i == nblk - 1
        d, dpv, dnx = d_ref[...], dpv_ref[...], dnx_ref[...]
        if silu:
            d, dpv, dnx = [jax.vjp(jax.nn.silu, c_ref[...])[1](t)[0] for c_ref, t in zip(rest[:3], (d, dpv, dnx))]
        dx_ref, dw_ref = rest[3 * silu:]
        x, pv, nx = x_ref[...], pv_ref[...], nx_ref[...]
        wv = w_ref[...]
        dx = wv[1:2] * d
        dx = dx + wv[0:1] * _shifted(d, 1, dpv, dnx, first, last)
        dx = dx + wv[2:3] * _shifted(d, -1, dpv, dnx, first, last)
        dx = dx + wv[3:4] * _shifted(d, -2, dpv, dnx, first, last)
        dx_ref[...] = dx.astype(dx_ref.dtype)

        @pl.when((n == 0) & (i == 0))
        def _():
            dw_ref[...] = jnp.zeros_like(dw_ref)

        dw_ref[0:1, :] += jnp.sum(d * _shifted(x, -1, pv, nx, first, last), axis=0, keepdims=True)
        dw_ref[1:2, :] += jnp.sum(d * x, axis=0, keepdims=True)
        dw_ref[2:3, :] += jnp.sum(d * _shifted(x, 1, pv, nx, first, last), axis=0, keepdims=True)
        dw_ref[3:4, :] += jnp.sum(d * _shifted(x, 2, pv, nx, first, last), axis=0, keepdims=True)
        dw_ref[4:5, :] += jnp.sum(d, axis=0, keepdims=True)

    nb8 = s // 8

    def specs(j):
        cur = pl.BlockSpec((None, ts, tc), lambda n, i: (n, i, j))
        prev = pl.BlockSpec((None, 8, tc), lambda n, i: (n, jnp.maximum(i * (ts // 8) - 1, 0), j))
        nxt = pl.BlockSpec((None, 8, tc), lambda n, i: (n, jnp.minimum((i + 1) * (ts // 8), nb8 - 1), j))
        return [cur, prev, nxt]

    return _pcall(body, name=f"conv_bwd{col}", grid=(nbatch, nblk),
                  in_specs=specs(0) + specs(col) + [pl.BlockSpec((4, tc), lambda n, i: (0, col))] + specs(col) * silu,
                  out_specs=(specs(0)[0], pl.BlockSpec((8, tc), lambda n, i: (0, 0))),
                  out_shape=(jax.ShapeDtypeStruct((nbatch, s, tc), BF16), jax.ShapeDtypeStruct((8, tc), F32)),
                  compiler_params=_params())(dc3, dc3, dc3, p3, p3, p3, w, *([conv3] * 3 * silu))


def _block_scan(coef, inp, reverse):
    r = coef.shape[0]
    row = lax.broadcasted_iota(jnp.int32, coef.shape, 0)
    a, b = coef, inp
    d = 1
    while d < r:
        if reverse:
            keep = row < r - d
            a_sh, b_sh = pltpu.roll(a, r - d, 0), pltpu.roll(b, r - d, 0)
        else:
            keep = row >= d
            a_sh, b_sh = pltpu.roll(a, d, 0), pltpu.roll(b, d, 0)
        b = b + a * jnp.where(keep, b_sh, 0.0)
        a = a * jnp.where(keep, a_sh, 1.0)
        d *= 2
    return a, b


def _lru_scan(a3, b3, reverse):
    nbatch, s, w = a3.shape
    ts = min(LRU_ROWS, s)
    nblk = s // ts
    edge = 0 if reverse else ts - 1

    def body(a_ref, b_ref, h_ref, carry):
        @pl.when(pl.program_id(1) == 0)
        def _():
            carry[...] = jnp.zeros_like(carry)

        ca, hb = _block_scan(a_ref[...], b_ref[...], reverse)
        h = hb + ca * carry[0:1, :]
        h_ref[...] = h
        carry[0:1, :] = h[edge:edge + 1, :]

    blk = pl.BlockSpec((None, ts, w), (lambda n, i: (n, nblk - 1 - i, 0)) if reverse else (lambda n, i: (n, i, 0)))
    return _pcall(body, name=f"lru_scan_r{int(reverse)}", grid=(nbatch, nblk), in_specs=[blk, blk], out_specs=blk,
                  out_shape=jax.ShapeDtypeStruct((nbatch, s, w), F32), scratch_shapes=[pltpu.VMEM((8, w), F32)],
                  compiler_params=_params())(a3, b3)


def _lru_scan_bwd(a3, h3, dh3, reverse, carry=None):
    nbatch, s, w = a3.shape
    ts = min(LRU_ROWS, s)
    nblk = s // ts
    nb8 = s // 8
    tpb = ts // 8

    def body(a_ref, aa_ref, h_ref, hh_ref, dh_ref, g_ref, da_ref, carry):
        i = pl.program_id(1)

        @pl.when(i == 0)
        def _():
            carry[...] = jnp.zeros_like(carry)

        a, h = a_ref[...], h_ref[...]
        row = lax.broadcasted_iota(jnp.int32, a.shape, 0)
        if reverse:
            a_edge = jnp.where(i == 0, 0.0, aa_ref[7:8, :])
            c = jnp.where(row == 0, a_edge, pltpu.roll(a, 1, 0))
            h_edge = jnp.where(i == nblk - 1, 0.0, hh_ref[0:1, :])
            h_sh = jnp.where(row == ts - 1, h_edge, pltpu.roll(h, ts - 1, 0))
        else:
            a_edge = jnp.where(i == 0, 0.0, aa_ref[0:1, :])
            c = jnp.where(row == ts - 1, a_edge, pltpu.roll(a, ts - 1, 0))
            h_edge = jnp.where(i == nblk - 1, 0.0, hh_ref[7:8, :])
            h_sh = jnp.where(row == 0, h_edge, pltpu.roll(h, 1, 0))
        cc, gb = _block_scan(c, dh_ref[...], not reverse)
        g = gb + cc * carry[0:1, :]
        g_ref[...] = g
        carry[0:1, :] = g[ts - 1:ts, :] if reverse else g[0:1, :]
        da_ref[...] = g * h_sh

    if reverse:
        bi = lambda i: i
    else:
        bi = lambda i: nblk - 1 - i
    blk = pl.BlockSpec((None, ts, w), lambda n, i: (n, bi(i), 0))
    before = pl.BlockSpec((None, 8, w), lambda n, i: (n, jnp.maximum(bi(i) * tpb - 1, 0), 0))
    after = pl.BlockSpec((None, 8, w), lambda n, i: (n, jnp.minimum((bi(i) + 1) * tpb, nb8 - 1), 0))
    a_tile, h_tile = (before, after) if reverse else (after, before)
    return _pcall(body, carry=carry, name=f"lru_scan_bwd_r{int(reverse)}", grid=(nbatch, nblk),
                  in_specs=[blk, a_tile, blk, h_tile, blk], out_specs=(blk, blk),
                  out_shape=(jax.ShapeDtypeStruct((nbatch, s, w), F32), jax.ShapeDtypeStruct((nbatch, s, w), F32)),
                  scratch_shapes=[pltpu.VMEM((8, w), F32)],
                  compiler_params=_params())(a3, a3, h3, h3, dh3, *(carry[0] if carry else ()))


def _head_expand(lane0):
    return (jnp.right_shift(lax.broadcasted_iota(jnp.int32, (128, 1024), 1), HEAD_SHIFT) + lane0
            == lax.broadcasted_iota(jnp.int32, (128, 1024), 0)).astype(F32)


def _head_reduce(lane0):
    return (jnp.right_shift(lax.broadcasted_iota(jnp.int32, (1024, 128), 0), HEAD_SHIFT) + lane0
            == lax.broadcasted_iota(jnp.int32, (1024, 128), 1)).astype(F32)


def _time_mask(q, reverse):
    ri = lax.broadcasted_iota(jnp.int32, (q, q), 0)
    ci = lax.broadcasted_iota(jnp.int32, (q, q), 1)
    return (ri <= ci) if reverse else (ri >= ci)


def _ssd_common(xs_ref, bc_ref, dt_ref, al_ref, reverse, lane0):
    q = xs_ref.shape[0]
    edge = 0 if reverse else q - 1
    dt = dt_ref[...]
    a = -jnp.exp(al_ref[...])
    mask = _time_mask(q, reverse)
    expand = _head_expand(lane0)
    cum = _dot01(mask.astype(F32), dt * a, split="b", terms=3)
    cum_x = _dot01(cum, expand, split="a", terms=2)
    dt_x = _dot01(dt, expand, split="a", terms=2)
    last_x = cum_x[edge:edge + 1, :]
    xs = xs_ref[...]
    bc = bc_ref[...]
    return dict(q=q, edge=edge, lane0=lane0, dt=dt, a=a, mask=mask, cum_t=cum.T, cum_x=cum_x, dt_x=dt_x, xs=xs,
                v=xs * dt_x, e_c=jnp.exp(cum_x), w=jnp.exp(last_x - cum_x), e_l=jnp.exp(last_x),
                bm=bc[:, :512], cm=bc[:, 512:])


def _ssd_decay(c, h):
    row = c["lane0"] + h
    seg = c["cum_x"][:, h * SSD_HEADDIM:h * SSD_HEADDIM + 1] - c["cum_t"][row:row + 1, :]
    return jnp.where(c["mask"], jnp.exp(jnp.minimum(seg, 0.0)), 0.0)


def _head_masks():
    lane = jnp.right_shift(lax.broadcasted_iota(jnp.int32, (1, 256), 1), HEAD_SHIFT)
    return [lane == e for e in range(4)]


def _ssd_fwd(xbc3, dt3, alog, reverse, carry=None):
    nbatch, s, _ = xbc3.shape
    q = min(SSD_CHUNK, s)
    nc = s // q
    lane0 = SSD_HEADS * int(reverse)

    def body(xs_ref, bc_ref, dt_ref, al_ref, y_ref, st_ref, st):
        @pl.when(pl.program_id(1) == 0)
        def _():
            st[...] = jnp.zeros_like(st)

        st_ref[...] = st[...]
        c = _ssd_common(xs_ref, bc_ref, dt_ref, al_ref, reverse, lane0)
        hm = _head_masks()
        for g in range(SSD_GROUPS):
            sl = slice(g * 256, (g + 1) * 256)
            cg, bg = _mx(c["cm"][:, g * 128:(g + 1) * 128]), _mx(c["bm"][:, g * 128:(g + 1) * 128])
            cb = _dot(cg, bg, _NT)
            vg = c["v"][:, sl]
            s0 = st[:, sl]
            yg = _dot(cg, _mx(s0)) * c["e_c"][:, sl]
            for e in range(4):
                m = _ssd_decay(c, 4 * g + e) * cb
                yg = yg + _dot(_mx(m), _mx(jnp.where(hm[e], vg, 0.0)))
            y_ref[:, sl] = yg
            st[:, sl] = c["e_l"][:, sl] * s0 + _dot(bg, _mx(vg * c["w"][:, sl]), _TN)

    ck = (lambda i: nc - 1 - i) if reverse else (lambda i: i)
    xs_spec = pl.BlockSpec((None, q, 1024), lambda n, i: (n, ck(i), 0))
    bc_spec = pl.BlockSpec((None, q, 1024), lambda n, i: (n, ck(i), 1))
    dt_spec = pl.BlockSpec((None, q, 128), lambda n, i: (n, ck(i), 0))
    al_spec = pl.BlockSpec((1, 128), lambda n, i: (0, 0))
    st_spec = pl.BlockSpec((None, None, 128, 1024), lambda n, i: (n, ck(i), 0, 0))
    return _pcall(body, carry=carry, name=f"ssd_fwd_r{int(reverse)}", grid=(nbatch, nc),
                  in_specs=[xs_spec, bc_spec, dt_spec, al_spec], out_specs=(xs_spec, st_spec),
                  out_shape=(jax.ShapeDtypeStruct((nbatch, s, 1024), F32), jax.ShapeDtypeStruct((nbatch, nc, 128, 1024), F32)),
                  scratch_shapes=[pltpu.VMEM((128, 1024), F32)],
                  compiler_params=_params())(xbc3, xbc3, dt3, alog, *(carry[0] if carry else ()))


def _ssd_bwd(xbc3, dt3, alog, st4, dy3, reverse, add_to=(), scatter=()):
    nbatch, s, _ = xbc3.shape
    q = min(SSD_CHUNK, s)
    nc = s // q
    lane0 = SSD_HEADS * int(reverse)
    nadd, ns = len(add_to), len(scatter)

    def body(xs_ref, bc_ref, dt_ref, al_ref, st0_ref, dy_ref, *rest):
        adds, srcs, rest = rest[:nadd], rest[nadd:nadd + ns], rest[nadd + ns:]
        (dxs_ref, dbc_ref, ddt_ref, dal_ref), lands, dst = rest[:4], rest[4:4 + ns], rest[4 + ns]
        n, i = pl.program_id(0), pl.program_id(1)
        if ns:
            sends, arrivals = _scatter_copies(srcs, lands, *rest[5 + ns:])

            @pl.when((n == 0) & (i == 0))
            def _():
                for cp in sends:
                    cp.start()

        @pl.when(i == 0)
        def _():
            dst[...] = jnp.zeros_like(dst)

        @pl.when((i == 0) & (n == 0))
        def _():
            dal_ref[...] = jnp.zeros_like(dal_ref)

        c = _ssd_common(xs_ref, bc_ref, dt_ref, al_ref, reverse, lane0)
        hm = _head_masks()
        reduce_m = _head_reduce(lane0)
        s0_all, ds1_all, dy = st0_ref[...], dst[...], dy_ref[...]
        lane = lax.broadcasted_iota(jnp.int32, (q, 128), 1)
        sub = lax.broadcasted_iota(jnp.int32, (128, q), 0)
        rowacc = jnp.zeros((q, 128), F32)
        colacc_t = jnp.zeros((128, q), F32)
        dv_l, yst_l, dvbar_l, dk_l, dc_l = [], [], [], [], []
        for g in range(SSD_GROUPS):
            sl = slice(g * 256, (g + 1) * 256)
            cg, bg = _mx(c["cm"][:, g * 128:(g + 1) * 128]), _mx(c["bm"][:, g * 128:(g + 1) * 128])
            cb = _dot(cg, bg, _NT)
            vg, dyg, wg, ecg = c["v"][:, sl], dy[:, sl], c["w"][:, sl], c["e_c"][:, sl]
            s0, ds1 = _mx(s0_all[:, sl]), _mx(ds1_all[:, sl])
            dye = _mx(dyg * ecg)
            yst_l.append(_dot(cg, s0) * ecg)
            dcg = _dot(dye, s0, _NT)
            dst[:, sl] = c["e_l"][:, sl] * ds1_all[:, sl] + _dot(cg, dye, _TN)
            vbar = _mx(vg * wg)
            dvbar = _dot(bg, ds1)
            dvbar_l.append(dvbar)
            dvg = dvbar * wg
            dkg = _dot(vbar, ds1, _NT)
            for e in range(4):
                h = 4 * g + e
                m = _ssd_decay(c, h)
                dyh, vh = _mx(jnp.where(hm[e], dyg, 0.0)), _mx(jnp.where(hm[e], vg, 0.0))
                dvg = dvg + _dot(_mx(m * cb), dyh, _TN)
                dcb = _dot(dyh, vh, _NT) * m
                dcbb = _mx(dcb)
                dcg = dcg + _dot(dcbb, bg)
                dkg = dkg + _dot(dcbb, cg, _TN)
                wmat = dcb * cb
                rowacc = jnp.where(lane == lane0 + h, jnp.sum(wmat, axis=1, keepdims=True), rowacc)
                colacc_t = jnp.where(sub == lane0 + h, jnp.sum(wmat, axis=0, keepdims=True), colacc_t)
            dv_l.append(dvg)
            dk_l.append(dkg)
            dc_l.append(dcg)
        dv = jnp.concatenate(dv_l, axis=1)
        yst = jnp.concatenate(yst_l, axis=1)
        dvbar = jnp.concatenate(dvbar_l, axis=1)
        t1 = _dot01(dy * yst, reduce_m, split="a", terms=2)
        t2 = _dot01(c["v"] * c["w"] * dvbar, reduce_m, split="a", terms=2)
        dlast = jnp.sum(t2, axis=0, keepdims=True) + _dot01(
            c["e_l"] * jnp.sum(ds1_all * s0_all, axis=0, keepdims=True), reduce_m, split="a", terms=2)
        dcum = rowacc - colacc_t.T + t1 - t2
        dcum = dcum + jnp.where(lax.broadcasted_iota(jnp.int32, (q, 128), 0) == c["edge"], dlast, 0.0)
        dda = _dot01(c["mask"].astype(F32), dcum, _TN, split="b", terms=3)
        ddt = dda * c["a"] + _dot01(dv * c["xs"], reduce_m, split="a", terms=2)
        dal_ref[...] += jnp.sum(dda * c["dt"], axis=0, keepdims=True) * c["a"]
        dxs = dv * c["dt_x"]
        dbc = jnp.concatenate(dk_l + dc_l, axis=1)
        if nadd:
            for a_ref in adds[:-2]:
                dxs = dxs + a_ref[...]
            dbc = dbc + adds[-2][...]
            ddt = ddt + adds[-1][...]
        ddt_ref[...] = ddt
        dxs_ref[...] = dxs
        dbc_ref[...] = dbc
        if ns:
            @pl.when((n == nbatch - 1) & (i == nc - 1))
            def _():
                for cp in arrivals:
                    cp.wait_recv()
                for cp in sends:
                    cp.wait_send()

    ck = (lambda i: i) if reverse else (lambda i: nc - 1 - i)
    xs_spec = pl.BlockSpec((None, q, 1024), lambda n, i: (n, ck(i), 0))
    bc_spec = pl.BlockSpec((None, q, 1024), lambda n, i: (n, ck(i), 1))
    dt_spec = pl.BlockSpec((None, q, 128), lambda n, i: (n, ck(i), 0))
    al_spec = pl.BlockSpec((1, 128), lambda n, i: (0, 0))
    st_spec = pl.BlockSpec((None, None, 128, 1024), lambda n, i: (n, ck(i), 0, 0))
    return _pcall(body, name=f"ssd_bwd_r{int(reverse)}", grid=(nbatch, nc),
                  in_specs=([xs_spec, bc_spec, dt_spec, al_spec, st_spec, xs_spec] + [xs_spec] * (nadd - 1)
                            + [dt_spec] * bool(nadd) + [ANY] * ns),
                  out_specs=(xs_spec, xs_spec, dt_spec, al_spec) + (ANY,) * ns,
                  out_shape=(jax.ShapeDtypeStruct((nbatch, s, 1024), F32), jax.ShapeDtypeStruct((nbatch, s, 1024), F32),
                             jax.ShapeDtypeStruct((nbatch, s, 128), F32), jax.ShapeDtypeStruct((1, 128), F32))
                  + tuple(jax.ShapeDtypeStruct(c.shape, c.dtype) for c in scatter),
                  scratch_shapes=[pltpu.VMEM((128, 1024), F32)] + (_scatter_scratch(ns) if ns else []),
                  compiler_params=_params())(xbc3, xbc3, dt3, alog, st4, dy3, *add_to, *scatter)


def _gla_block(q, k, g, reverse):
    bq = g.shape[0]
    nsub = bq // HGRN_SUB
    edge = 0 if reverse else bq - 1
    ri = lax.broadcasted_iota(jnp.int32, (bq, bq), 0)
    ci = lax.broadcasted_iota(jnp.int32, (bq, bq), 1)
    rb, cb = jnp.right_shift(ri, HGRN_SUB_SHIFT), jnp.right_shift(ci, HGRN_SUB_SHIFT)
    mask = (ri <= ci) if reverse else (ri >= ci)
    m_within = (mask & (rb == cb)).astype(F32)
    m_before = ((cb > rb) if reverse else (cb < rb)).astype(F32)
    bl = _dot01(m_within, g, split="b", terms=3)
    c = _dot01(m_before, g, split="b", terms=3)
    last = c[edge:edge + 1, :] + bl[edge:edge + 1, :]
    ebl, enbl, ec, elc = jnp.exp(bl), jnp.exp(-bl), jnp.exp(c), jnp.exp(last - c)
    qh = q * HGRN_SCALE * ebl
    kh = k * enbl
    blk = jnp.right_shift(lax.broadcasted_iota(jnp.int32, (bq, 1), 0), HGRN_SUB_SHIFT)
    scale = []
    for i in range(nsub):
        valid = (blk >= i) if reverse else (blk <= i)
        ex = jnp.where(valid, c[i * HGRN_SUB:i * HGRN_SUB + 1, :] - c, 0.0)
        scale.append(jnp.where(valid, jnp.exp(ex), 0.0))
    return dict(bq=bq, nsub=nsub, edge=edge, mask=mask, m_within=m_within, m_before=m_before, ebl=ebl, enbl=enbl, ec=ec,
                elc=elc, e_l=jnp.exp(last), qh=qh, qt=qh * ec, kh=kh, kb=kh * elc, scale=scale)


def _gla_scores(c, hs):
    keys = [_mx(c["kh"][:, hs] * c["scale"][i][:, hs]) for i in range(c["nsub"])]
    rows = [_dot(_mx(c["qh"][i * HGRN_SUB:(i + 1) * HGRN_SUB, hs]), keys[i], _NT) for i in range(c["nsub"])]
    return jnp.where(c["mask"], jnp.concatenate(rows, axis=0), 0.0), keys


def _gla_specs(nbatch, s, w, reverse_order):
    bq = min(HGRN_BLOCK, s)
    nblk = s // bq
    bi = (lambda i: nblk - 1 - i) if reverse_order else (lambda i: i)
    col = lambda cb: pl.BlockSpec((nbatch, bq, w), lambda i: (0, bi(i), cb))
    st_spec = pl.BlockSpec((nbatch, None, 128, w), lambda i: (0, bi(i), 0, 0))
    return bq, nblk, col, st_spec


def _gla_fwd(proj3, l0, l1, reverse, carry=None):
    nbatch, s, w5 = proj3.shape
    w = w5 // 5
    bq, nblk, col, st_spec = _gla_specs(nbatch, s, w, reverse)
    vec = pl.BlockSpec((1, w), lambda i: (0, 0))

    def body(q_ref, f_ref, v_ref, l0_ref, l1_ref, o_ref, st_ref, st):
        @pl.when(pl.program_id(0) == 0)
        def _():
            st[...] = jnp.zeros_like(st)

        for b in range(nbatch):
            st_ref[b] = st[b]
            k, g = _f_hgrn_pre(f_ref[b], l0_ref[...], l1_ref[...])
            c = _gla_block(q_ref[b], k, g, reverse)
            v = v_ref[b]
            for h in range(HGRN_HEADS):
                hs = slice(h * 128, (h + 1) * 128)
                att, _ = _gla_scores(c, hs)
                vb = _mx(v[:, hs])
                s0 = st[b, :, hs]
                o_ref[b, :, hs] = _dot(_mx(att), vb) + _dot(_mx(c["qt"][:, hs]), _mx(s0), _NT)
                st[b, :, hs] = s0 * c["e_l"][:, hs] + _dot(vb, _mx(c["kb"][:, hs]), _TN)

    return _pcall(body, carry=carry, name=f"gla_fwd_r{int(reverse)}", grid=(nblk,),
                  in_specs=[col(0), col(1 + int(reverse)), col(3), vec, vec], out_specs=(col(0), st_spec),
                  out_shape=(jax.ShapeDtypeStruct((nbatch, s, w), F32), jax.ShapeDtypeStruct((nbatch, nblk, 128, w), F32)),
                  scratch_shapes=[pltpu.VMEM((nbatch, 128, w), F32)],
                  compiler_params=_params())(proj3, proj3, proj3, l0, l1, *(carry[0] if carry else ()))


def _gla_bwd(proj3, l0, l1, st4, do3, reverse, add_to=None):
    nbatch, s, w5 = proj3.shape
    w = w5 // 5
    bq, nblk, col, st_spec = _gla_specs(nbatch, s, w, not reverse)
    nadd = 0 if add_to is None else 2
    vec = pl.BlockSpec((1, w), lambda i: (0, 0))

    def body(q_ref, f_ref, v_ref, l0_ref, l1_ref, st_ref, do_ref, *rest):
        adds, (dq_ref, df_ref, dv_ref, dl0_ref, dl1_ref, dst) = rest[:nadd], rest[nadd:]

        @pl.when(pl.program_id(0) == 0)
        def _():
            dst[...] = jnp.zeros_like(dst)
            dl0_ref[...] = jnp.zeros_like(dl0_ref)
            dl1_ref[...] = jnp.zeros_like(dl1_ref)

        row = lax.broadcasted_iota(jnp.int32, (bq, 128), 0)
        for b in range(nbatch):
            (k, g), pre_vjp = jax.vjp(_f_hgrn_pre, f_ref[b], l0_ref[...], l1_ref[...])
            c = _gla_block(q_ref[b], k, g, reverse)
            s0_all, ds1_all = st_ref[b], dst[b]
            v, dy = v_ref[b], do_ref[b]
            dbl_l, dc_l, dk_l = [], [], []
            for h in range(HGRN_HEADS):
                hs = slice(h * 128, (h + 1) * 128)
                att, keys = _gla_scores(c, hs)
                qh, qt, kh, kb = c["qh"][:, hs], c["qt"][:, hs], c["kh"][:, hs], c["kb"][:, hs]
                vb, dyb = _mx(v[:, hs]), _mx(dy[:, hs])
                s0, ds1 = s0_all[:, hs], ds1_all[:, hs]
                datt = _mx(jnp.where(c["mask"], _dot(dyb, vb, _NT), 0.0))
                dqh_rows = []
                dkh = jnp.zeros((bq, 128), F32)
                dc = jnp.zeros((bq, 128), F32)
                for i in range(c["nsub"]):
                    rs = slice(i * HGRN_SUB, (i + 1) * HGRN_SUB)
                    dqh_rows.append(_dot(datt[rs], keys[i]))
                    dki = _dot(datt[rs], _mx(qh[rs]), _TN)
                    sc = c["scale"][i][:, hs]
                    dkh = dkh + dki * sc
                    dex = dki * (kh * sc)
                    dc = dc - dex + jnp.where(row == i * HGRN_SUB, jnp.sum(dex, axis=0, keepdims=True), 0.0)
                dqt = _dot(dyb, _mx(s0))
                dkb = _dot(vb, _mx(ds1))
                dv = _dot(_mx(att), dyb, _TN) + _dot(_mx(kb), _mx(ds1), _NT)
                dst[b, :, hs] = c["e_l"][:, hs] * ds1 + _dot(dyb, _mx(qt), _TN)
                dqh = jnp.concatenate(dqh_rows, axis=0) + dqt * c["ec"][:, hs]
                dkh = dkh + dkb * c["elc"][:, hs]
                kbk = dkb * kb
                dlast = jnp.sum(kbk, axis=0, keepdims=True) + c["e_l"][:, hs] * jnp.sum(ds1 * s0, axis=0, keepdims=True)
                at_edge = jnp.where(row == c["edge"], dlast, 0.0)
                dc_l.append(dc + dqt * qt - kbk + at_edge)
                dbl_l.append(dqh * qh - dkh * kh + at_edge)
                dq = dqh * c["ebl"][:, hs] * HGRN_SCALE
                if nadd:
                    dq, dv = dq + adds[0][b, :, hs], dv + adds[1][b, :, hs]
                dq_ref[b, :, hs] = dq.astype(dq_ref.dtype)
                dv_ref[b, :, hs] = dv.astype(dv_ref.dtype)
                dk_l.append(dkh * c["enbl"][:, hs])
            dg = (_dot01(c["m_within"], jnp.concatenate(dbl_l, axis=1), _TN, split="b", terms=2)
                  + _dot01(c["m_before"], jnp.concatenate(dc_l, axis=1), _TN, split="b", terms=2))
            df, d0, d1 = pre_vjp((jnp.concatenate(dk_l, axis=1), dg))
            df_ref[b] = df.astype(df_ref.dtype)
            dl0_ref[...] += d0
            dl1_ref[...] += d1

    shp_sum = jax.ShapeDtypeStruct((nbatch, s, w), BF16 if nadd else F32)
    shp_vec = jax.ShapeDtypeStruct((1, w), F32)
    return _pcall(body, name=f"gla_bwd_r{int(reverse)}", grid=(nblk,),
                  in_specs=[col(0), col(1 + int(reverse)), col(3), vec, vec, st_spec, col(0)] + [col(0)] * nadd,
                  out_specs=(col(0), col(0), col(0), vec, vec),
                  out_shape=(shp_sum, jax.ShapeDtypeStruct((nbatch, s, w), BF16), shp_sum, shp_vec, shp_vec),
                  scratch_shapes=[pltpu.VMEM((nbatch, 128, w), F32)],
                  compiler_params=_params())(proj3, proj3, proj3, l0, l1, st4, do3, *(add_to or ()))


DIRS = (False, True)


def _block_diag(w):
    eye = jnp.eye(16, dtype=w.dtype)
    return (eye[:, None, :, None] * w[:, :, None, :]).reshape(1024, 1024)


def _diag_blocks(m):
    m4 = m.reshape(16, 64, 16, 64)
    return jnp.stack([m4[i, :, i, :] for i in range(16)], axis=0)


def _pad_lanes(v, n=128):
    return jnp.pad(v, [(0, 0)] * (v.ndim - 1) + [(0, n - v.shape[-1])])


def _mlp_fwd(tag, x, nw, w1, w2, carry=None):
    (h,) = _pw_fwd(f"{tag}_norm", _f_norm, [(x, 0)], [(nw, 0)], [BF16], 1024, 1)
    a, r, *got = _mm(f"{tag}_up", h, w1, "nn", relu2=True, carry=carry)
    return _mm(f"{tag}_down", r, w2, "nn", res=x), (h, a, r), got


def _mlp_bwd(tag, x, nw, w1, w2, saved, dxo, carry=None):
    h, a, r = saved
    dw2, *got = _mm(f"{tag}_dw2", r, dxo, "tn", carry=carry) if carry else (_mm(f"{tag}_dw2", r, dxo, "tn"),)
    da = _mm(f"{tag}_da", dxo, w2, "nt", relu2_of=a, out_dtype=BF16)
    dw1 = _mm(f"{tag}_dw1", h, da, "tn", col_shards=4)
    dx, dnw = _mm_sum_nt(f"{tag}_dh", [(da, k, 1024) for k in range(4)], [(w1, k) for k in range(4)], norm_bwd=(x, nw, dxo))
    return dx, dw1, dw2, dnw, got


def _split_in0(pieces, dt_piece):
    tm = 256

    def body(p0, p1, p2, p3, p4, p5, o_ref):
        full = jnp.concatenate([p0[...], p1[...], p2[...], p3[...], p4[...], p5[:, :32]], axis=1)
        for j in range(4):
            o_ref[j] = full[:, 1288 * j:1288 * (j + 1)]

    blk = pl.BlockSpec((tm, 1024), lambda i: (i, 0))
    return _pcall(body, name="split_in0", grid=(1024 // tm,), in_specs=[blk] * 5 + [pl.BlockSpec((tm, 128), lambda i: (i, 0))],
                  out_specs=pl.BlockSpec((4, tm, 1288), lambda i: (0, i, 0)),
                  out_shape=jax.ShapeDtypeStruct((4, 1024, 1288), F32), compiler_params=_params())(*pieces, dt_piece)


def _assemble_in0(shards):
    tm = 256

    def body(s_ref, m_ref, d_ref):
        full = jnp.concatenate([s_ref[j] for j in range(4)], axis=1)
        m_ref[...] = full[:, :5120]
        d_ref[...] = jnp.concatenate([full[:, 5120:5152], jnp.zeros((tm, 96), full.dtype)], axis=1)

    return _pcall(body, name="assemble_in0", grid=(1024 // tm,), in_specs=[pl.BlockSpec((4, tm, 1288), lambda i: (0, i, 0))],
                  out_specs=(pl.BlockSpec((tm, 5120), lambda i: (i, 0)), pl.BlockSpec((tm, 128), lambda i: (i, 0))),
                  out_shape=(jax.ShapeDtypeStruct((1024, 5120), shards.dtype), jax.ShapeDtypeStruct((1024, 128), shards.dtype)),
                  compiler_params=_params())(shards)


EARLY = ("odd_w_in", "odd_w_out", "mlp_w1_l1", "mlp_w2_l1")
MID = ("even_w_out", "mlp_w1_l0", "mlp_w2_l0")
LATE = ("even_w_in",)


def _local_step(x3, tgt3, w, w_main0, w_dt0, pair_reduce=None, late=None):
    nb, s, d = x3.shape
    carries, arrived = late if late else ({}, None)
    t = nb * s
    x0 = x3.reshape(t, d)
    tgt = tgt3.reshape(t, d)
    grads = {}
    row = lambda v: v.reshape(1, -1)
    to3 = lambda v: v.reshape(nb, s, v.shape[-1])
    to2 = lambda v: v.reshape(-1, v.shape[-1])

    conv_w, conv_b = w["even_conv_w"][0], row(w["even_conv_b"][0])
    nmix0 = row(w["norm_mix"][0])
    (h0,) = _pw_fwd("l0_norm", _f_norm, [(x0, 0)], [(nmix0, 0)], [BF16], 1024, 1)
    proj0 = _mm("l0_proj", h0, w_main0, "nn")
    dt_raw = _mm("l0_proj_dt", h0, w_dt0, "nn")
    conv2, xbc3 = _conv_fwd(to3(proj0), conv_w, conv_b, 0, 2, True)
    u_lru = to2(_conv_fwd(to3(proj0), conv_w, conv_b, 2, 1, False))
    xbc = to2(xbc3)
    dt_bias = _pad_lanes(w["ssd_dt_bias"][0].reshape(1, 32))
    (dt,) = _pw_fwd("l0_dt", _f_softplus, [(dt_raw, 0)], [(dt_bias, 0)], [F32], 128, 1)
    dt3 = to3(dt)
    alog = _pad_lanes(w["ssd_a_log"][0].reshape(1, 32))
    def merge(upd):
        out = dict(w)
        for k, v in upd.items():
            if isinstance(k, tuple):
                both = list(out.get(k[0]) or [None, None])
                both[k[1]] = v
                out[k[0]] = both
            else:
                out[k] = v
        return out

    ssd = [_ssd_fwd(xbc3, dt3, alog, r, carry=carries.get(key)) for r, key in zip(DIRS, ("ssd0", "ssd1"))]
    if late:
        w = merge(arrived("ssd0", ssd[0][2:]))
        w = merge(arrived("ssd1", ssd[1][2:]))
    yf, yb = to2(ssd[0][0]), to2(ssd[1][0])
    dskip = jnp.repeat(w["ssd_d"][0], SSD_HEADDIM).reshape(1, 1024)
    snw = row(w["ssd_norm_w"][0])
    ssd_ins = [(yf, 0), (yb, 0), (xbc, 0), (proj0, 3)]
    (ya,) = _pw_fwd("l0_ssd_post", _f_ssd_post, ssd_ins, [(dskip, 0), (snw, 0)], [BF16], 1024, 1, groups=SSD_GROUPS)
    w_gates = [_block_diag(w[k][0, r]).astype(MXU_DTYPE) for r in range(2) for k in ("lru_w_a", "lru_w_x")]
    pre = [_mm(f"l0_lru_pre{i}", u_lru, wg, "nn") for i, wg in enumerate(w_gates)]
    lru_par = [[(row(w[k][0, r]), 0) for k in ("lru_b_a", "lru_b_x", "lru_lambda")] for r in range(2)]
    lru_ins = [[(pre[2 * r], 0), (pre[2 * r + 1], 0), (u_lru, 0)] for r in range(2)]
    ab = [_pw_fwd(f"l0_lru_gates{r}", _f_lru_gates, lru_ins[r], lru_par[r], [F32, F32], 1024, 1) for r in range(2)]
    hs = [_lru_scan(to3(ab[r][0]), to3(ab[r][1]), DIRS[r]) for r in range(2)]
    lru_post_ins = [(to2(hs[0]), 0), (to2(hs[1]), 0), (proj0, 4)]
    (ybm,) = _pw_fwd("l0_lru_post", _f_lru_post, lru_post_ins, [], [BF16], 1024, 1)
    w_out0 = w["even_w_out"][0]
    x1 = _mm("l0_out_a", ya, w_out0[:1024], "nn", res=x0)
    x1 = _mm("l0_out_b", ybm, w_out0[1024:], "nn", res=x1)
    nmlp0 = row(w["norm_mlp"][0])
    x2, mlp0, got = _mlp_fwd("l0_mlp", x1, nmlp0, w["mlp_w1"][0], w["mlp_w2"][0], carry=carries.get("odd"))
    if late:
        w = merge(arrived("odd", got))

    w_in1 = w["odd_w_in"][0]
    nmix1 = row(w["norm_mix"][1])
    (h1,) = _pw_fwd("l1_norm", _f_norm, [(x2, 0)], [(nmix1, 0)], [BF16], 1024, 1)
    proj1 = _mm("l1_proj", h1, w_in1, "nn")
    proj1_3 = to3(proj1)
    lb0, lb1 = row(w["hgrn_lb_logits"][0]), row(w["hgrn_lb_logits"][1])
    gla = [_gla_fwd(proj1_3, lb0, lb1, r, carry=carries.get(key)) for r, key in zip(DIRS, ("gla0", "gla1"))]
    if late:
        w = merge(arrived("gla0", gla[0][2:]))
        w = merge(arrived("gla1", gla[1][2:]))
    hnw = row(w["hgrn_norm_w"][0])
    hpost_ins = [(to2(gla[0][0]), 0), (to2(gla[1][0]), 0), (proj1, 4)]
    (yo,) = _pw_fwd("l1_hgrn_post", _f_hgrn_post, hpost_ins, [(hnw, 0)], [BF16], 1024, 1, groups=HGRN_HEADS)
    w_out1 = w["odd_w_out"][0]
    x3_ = _mm("l1_out", yo, w_out1, "nn", res=x2)
    nmlp1 = row(w["norm_mlp"][1])
    x4, mlp1, _ = _mlp_fwd("l1_mlp", x3_, nmlp1, w["mlp_w1"][1], w["mlp_w2"][1])

    dx4, dnf, loss = _loss_head(x4, tgt, row(w["norm_final"]))
    grads["norm_final"] = dnf.reshape(-1)

    dx3, dw1_1, dw2_1, dnmlp1, _ = _mlp_bwd("l1_mlp", x3_, nmlp1, w["mlp_w1"][1], w["mlp_w2"][1], mlp1, dx4)
    big = {"odd_w_out": _mm("l1_dwout", yo, dx3, "tn").reshape(4, 256, 1024)}
    dyo = _mm("l1_dyo", dx3, w_out1, "nt")
    (do, dgate1), (dhnw,) = _pw_bwd("l1_hgrn_post_b", _f_hgrn_post, hpost_ins, [(hnw, 0)], [dyo], 1024, 1, [0, 2],
                                    out_dtypes=[F32, BF16], groups=HGRN_HEADS, tm=ROWS_FWD)
    grads["hgrn_norm_w"] = dhnw
    do3 = to3(do)
    gb = [_gla_bwd(proj1_3, lb0, lb1, gla[0][1], do3, False)]
    gb.append(_gla_bwd(proj1_3, lb0, lb1, gla[1][1], do3, True, add_to=(gb[0][0], gb[0][2])))
    grads["hgrn_lb_logits"] = jnp.concatenate([gb[0][3] + gb[1][3], gb[0][4] + gb[1][4]], axis=0)
    dparts1 = [to2(gb[1][0]), to2(gb[0][1]), to2(gb[1][1]), to2(gb[1][2]), dgate1]
    dwin1 = jnp.concatenate([_mm(f"l1_dwin{i}", h1, dp, "tn") for i, dp in enumerate(dparts1)], axis=1)
    big["odd_w_in"] = dwin1.reshape(1024, 4, 1280).transpose(1, 0, 2)
    dx2, dnmix1 = _mm_sum_nt("l1_dh", dparts1, [(w_in1, i) for i in range(5)], norm_bwd=(x2, nmix1, dx3))
    big["mlp_w1_l1"], big["mlp_w2_l1"] = dw1_1, dw2_1.reshape(4, 1024, 1024)
    box = {}

    def mlp0_bwd(carry=None):
        box["mlp0"] = _mlp_bwd("l0_mlp", x1, nmlp0, w["mlp_w1"][0], w["mlp_w2"][0], mlp0, dx2, carry=carry)
        return box["mlp0"][4]

    early_sums = tuple(pair_reduce(EARLY, [big[n] for n in EARLY], mlp0_bwd)) if pair_reduce else tuple(mlp0_bwd())

    dx1, dw1_0, dw2_0, dnmlp0 = box["mlp0"][:4]
    big["mlp_w1_l0"], big["mlp_w2_l0"] = dw1_0, dw2_0.reshape(4, 1024, 1024)
    grads["norm_mlp"] = jnp.concatenate([dnmlp0, dnmlp1], axis=0)
    big["even_w_out"] = jnp.concatenate([_mm("l0_dwout_a", ya, dx1, "tn"), _mm("l0_dwout_b", ybm, dx1, "tn")],
                                        axis=0).reshape(4, 512, 1024)
    dya = _mm("l0_dya", dx1, w_out0[:1024], "nt")
    dyb = _mm("l0_dyb", dx1, w_out0[1024:], "nt")
    (dh, dgate0), _ = _pw_bwd("l0_lru_post_b", _f_lru_post, lru_post_ins, [], [dyb], 1024, 1, [0, 2], out_dtypes=[F32, BF16],
                               tm=ROWS_FWD)
    dh3 = to3(dh)

    def lru0_bwd(carry=None):
        box["lru0"] = _lru_scan_bwd(to3(ab[0][0]), hs[0], dh3, DIRS[0], carry=carry)
        return box["lru0"][2:]

    mid_sums = tuple(pair_reduce(MID, [big[n] for n in MID], lru0_bwd)) if pair_reduce else tuple(lru0_bwd())
    dpre, du_parts, dlru = [], [], {k: [] for k in ("lru_b_a", "lru_b_x", "lru_lambda")}
    for r in range(2):
        g_r, da_r = box["lru0"][:2] if r == 0 else _lru_scan_bwd(to3(ab[r][0]), hs[r], dh3, DIRS[r])
        (dpa, dpx, du_r), (dba, dbx, dlam) = _pw_bwd(f"l0_lru_gates_b{r}", _f_lru_gates, lru_ins[r], lru_par[r],
                                                     [to2(da_r), to2(g_r)], 1024, 1, [0, 1, 2],
                                                     out_dtypes=[BF16, BF16, F32])
        dpre += [dpa, dpx]
        du_parts.append(du_r)
        dlru["lru_b_a"].append(dba)
        dlru["lru_b_x"].append(dbx)
        dlru["lru_lambda"].append(dlam)
    for k, v in dlru.items():
        grads[k] = jnp.concatenate(v, axis=0)[None]
    dwg = [_diag_blocks(_mm(f"l0_dwgate{i}", u_lru, dp, "tn")) for i, dp in enumerate(dpre)]
    grads["lru_w_a"] = jnp.stack([dwg[0], dwg[2]])[None]
    grads["lru_w_x"] = jnp.stack([dwg[1], dwg[3]])[None]
    du = _mm_sum_nt("l0_du", dpre, [(wg, 0) for wg in w_gates], add=du_parts)
    (dy, dxs_skip, dz), (ddskip, dsnw) = _pw_bwd("l0_ssd_post_b", _f_ssd_post, ssd_ins, [(dskip, 0), (snw, 0)], [dya],
                                                 1024, 1, [0, 2, 3], out_dtypes=[F32, F32, BF16], groups=SSD_GROUPS)
    grads["ssd_d"] = ddskip.reshape(SSD_HEADS, SSD_HEADDIM).sum(axis=1)[None]
    grads["ssd_norm_w"] = dsnw
    dy3 = to3(dy)
    sb0 = _ssd_bwd(xbc3, dt3, alog, ssd[0][1], dy3, False, scatter=early_sums)
    sb1 = _ssd_bwd(xbc3, dt3, alog, ssd[1][1], dy3, True, add_to=(sb0[0], to3(dxs_skip), sb0[1], sb0[2]), scatter=mid_sums)
    grads["ssd_a_log"] = (sb0[3] + sb1[3])[:, :32].reshape(1, 2, 16)
    ddt = to2(sb1[2])
    (ddt_raw,), (ddtb,) = _pw_bwd("l0_dt_b", _f_softplus, [(dt_raw, 0)], [(dt_bias, 0)], [ddt], 128, 1, [0])
    grads["ssd_dt_bias"] = ddtb[:, :32].reshape(1, 2, 16)
    cb = [_conv_bwd(sb1[0], to3(proj0), conv_w, 0, conv2), _conv_bwd(sb1[1], to3(proj0), conv_w, 1, conv2),
          _conv_bwd(to3(du), to3(proj0), conv_w, 2)]
    dcw = jnp.concatenate([c_[1] for c_ in cb], axis=1)
    grads["even_conv_w"] = dcw[:4][None]
    grads["even_conv_b"] = dcw[4:5]
    dparts0 = [to2(c_[0]) for c_ in cb] + [dz, dgate0]
    dwin0 = [_mm(f"l0_dwin{i}", h0, dp, "tn") for i, dp in enumerate(dparts0)]
    big["even_w_in"] = _split_in0(dwin0, _mm("l0_dwin_dt", h0, ddt_raw, "tn"))
    dx0, dnmix0 = _mm_sum_nt("l0_dh", dparts0 + [ddt_raw], [(w_main0, i) for i in range(5)] + [(w_dt0, 0)],
                             norm_bwd=(x0, nmix0, dx1))
    grads["norm_mix"] = jnp.concatenate([dnmix0, dnmix1], axis=0)
    return loss, dx0.reshape(nb, s, d), grads, big, (early_sums + mid_sums, sb0[4:] + sb1[4:])


ANY = pl.BlockSpec(memory_space=pl.ANY)


def _place():
    return lax.axis_index("x"), lax.axis_index("y"), lax.axis_index("c")


def _remote(src, dst, send_sems, recv_sems, k, to):
    return pltpu.make_async_remote_copy(src_ref=src, dst_ref=dst, send_sem=send_sems.at[k], recv_sem=recv_sems.at[k],
                                        device_id=to, device_id_type=MESH)


def _gather_start(x_refs, out_refs, send_sems, recv_sems, finish=False):
    n = len(x_refs)
    halves = [r.shape[0] // 2 for r in x_refs]
    x, y, c = _place()
    sibling = (x, y, 1 - c)
    chips = [(1 - x, y), (x, 1 - y), (1 - x, 1 - y)]

    def blk(t, px, py, hc):
        return out_refs[t].at[2 * px + py, pl.ds(hc * halves[t], halves[t]), :]

    def src(t):
        return x_refs[t].at[pl.ds(c * halves[t], halves[t]), :]

    first = [_remote(src(t), blk(t, x, y, c), send_sems, recv_sems, 6 * t + j, (*chip, c))
             for t in range(n) for j, chip in enumerate(chips)]
    if not finish:
        for cp in first:
            cp.start()
        return
    passed = []
    for t in range(n):
        for j, chip in enumerate(chips):
            _remote(src(t), blk(t, *chip, c), send_sems, recv_sems, 6 * t + j, (*chip, c)).wait_recv()
            cp = _remote(blk(t, *chip, c), blk(t, *chip, c), send_sems, recv_sems, 6 * t + 3 + j, sibling)
            cp.start()
            passed.append(cp)
    for t in range(n):
        for j, chip in enumerate(chips):
            _remote(src(t), blk(t, *chip, 1 - c), send_sems, recv_sems, 6 * t + 3 + j, sibling).wait_recv()
    for cp in first + passed:
        cp.wait_send()


_gather_finish = functools.partial(_gather_start, finish=True)


def _gather_carry(shards):
    n = len(shards)
    return (list(shards), [jax.ShapeDtypeStruct((4,) + s.shape, s.dtype) for s in shards],
            [pltpu.SemaphoreType.DMA((6 * n,)), pltpu.SemaphoreType.DMA((6 * n,))], _gather_start, _gather_finish)


def _gather_chips(shards):
    n = len(shards)
    srcs, shapes, scratch, start, finish = _gather_carry(shards)

    def body(*refs):
        start(refs[:n], refs[n:2 * n], *refs[2 * n:])
        finish(refs[:n], refs[n:2 * n], *refs[2 * n:])

    return _pcall(body, name="gather_weights", in_specs=[ANY] * n, out_specs=(ANY,) * n, out_shape=tuple(shapes),
                  scratch_shapes=scratch, compiler_params=_params())(*shards)


def _pair_swap_start(g_refs, land_refs, send_sems, recv_sems, finish=False):
    x, y, c = _place()
    cps = []
    for t, g in enumerate(g_refs):
        half = g.shape[1] // 2
        cps += [_remote(g.at[j, pl.ds((1 - c) * half, half), :], land_refs[t].at[j], send_sems, recv_sems, 4 * t + j,
                        (x, y, 1 - c)) for j in range(4)]
    for cp in cps:
        cp.wait() if finish else cp.start()


_pair_swap_finish = functools.partial(_pair_swap_start, finish=True)


def _pair_swap_carry(gps):
    n = len(gps)
    return (list(gps), [jax.ShapeDtypeStruct((4, g.shape[1] // 2, g.shape[2]), F32) for g in gps],
            [pltpu.SemaphoreType.DMA((4 * n,)), pltpu.SemaphoreType.DMA((4 * n,))], _pair_swap_start, _pair_swap_finish)


def _pair_swap(name, gps):
    n = len(gps)
    srcs, shapes, scratch, start, finish = _pair_swap_carry(gps)

    def body(*refs):
        start(refs[:n], refs[n:2 * n], *refs[2 * n:])
        finish(refs[:n], refs[n:2 * n], *refs[2 * n:])

    return _pcall(body, name=f"pair_swap_{name}", in_specs=[ANY] * n, out_specs=(ANY,) * n, out_shape=tuple(shapes),
                  scratch_shapes=scratch, compiler_params=_params())(*gps)


def _pair_add(name, gp, land, cidx):
    _, half, cols = land.shape
    tr = _tile(half, 512)
    nh = half // tr

    def body(c_ref, g_ref, l_ref, o_ref):
        o_ref[...] = (g_ref[...] + l_ref[...]).astype(o_ref.dtype)

    grid_spec = pltpu.PrefetchScalarGridSpec(
        num_scalar_prefetch=1, grid=(4, nh),
        in_specs=[pl.BlockSpec((None, tr, cols), lambda j, i, c: (j, c[0] * nh + i, 0)),
                  pl.BlockSpec((None, tr, cols), lambda j, i, c: (j, i, 0))],
        out_specs=pl.BlockSpec((None, tr, cols), lambda j, i, c: (j, i, 0)))
    return _pcall(body, name=f"pair_add_{name}", grid_spec=grid_spec, out_shape=jax.ShapeDtypeStruct((4, half, cols), BF16),
                  compiler_params=_params())(cidx, gp, land)


def _scatter_copies(s_refs, land_refs, send_sems, recv_sems):
    x, y, c = _place()
    me = 2 * x + y
    chips = [(1 - x, y), (x, 1 - y), (1 - x, 1 - y)]
    pairs = [(t, j, px, py) for t in range(len(s_refs)) for j, (px, py) in enumerate(chips)]
    sends = [_remote(s_refs[t].at[2 * px + py], land_refs[t].at[me], send_sems, recv_sems, 3 * t + j, (px, py, c))
             for t, j, px, py in pairs]
    arrivals = [_remote(s_refs[t].at[me], land_refs[t].at[2 * px + py], send_sems, recv_sems, 3 * t + j, (px, py, c))
                for t, j, px, py in pairs]
    return sends, arrivals


def _scatter_scratch(n):
    return [pltpu.SemaphoreType.DMA((3 * n,)), pltpu.SemaphoreType.DMA((3 * n,))]


def _chip_scatter(name, css):
    n = len(css)

    def body(*refs):
        sends, arrivals = _scatter_copies(refs[:n], refs[n:2 * n], *refs[2 * n:])
        for cp in sends:
            cp.start()
        for cp in arrivals:
            cp.wait_recv()
        for cp in sends:
            cp.wait_send()

    return _pcall(body, name=f"chip_scatter_{name}", in_specs=[ANY] * n, out_specs=(ANY,) * n,
                  out_shape=tuple(jax.ShapeDtypeStruct(s.shape, s.dtype) for s in css),
                  scratch_shapes=_scatter_scratch(n), compiler_params=_params())(*css)


def _chip_sum(name, land):
    _, half, cols = land.shape
    tr = _tile(half, 512)

    def body(l_ref, o_ref):
        o_ref[...] = ((l_ref[0].astype(F32) + l_ref[1].astype(F32)) + l_ref[2].astype(F32)) + l_ref[3].astype(F32)

    return _pcall(body, name=f"chip_sum_{name}", grid=(half // tr,),
                  in_specs=[pl.BlockSpec((4, tr, cols), lambda i: (0, i, 0))],
                  out_specs=pl.BlockSpec((tr, cols), lambda i: (i, 0)),
                  out_shape=jax.ShapeDtypeStruct((half, cols), F32), compiler_params=_params())(land)


def _pair_join(reds):
    n = len(reds)

    def body(*refs):
        r_refs, out_refs = refs[:n], refs[n:2 * n]
        send_sems, recv_sems = refs[2 * n:]
        x, y, c = _place()
        cps = [_remote(r_refs[t], out_refs[t].at[c], send_sems, recv_sems, t, (x, y, 1 - c)) for t in range(n)]
        for cp in cps:
            cp.start()
        for t in range(n):
            _remote(r_refs[t], out_refs[t].at[1 - c], send_sems, recv_sems, t, (x, y, 1 - c)).wait_recv()
        for cp in cps:
            cp.wait_send()

    return _pcall(body, name="grad_pair_join", in_specs=[ANY] * n, out_specs=(ANY,) * n,
                  out_shape=tuple(jax.ShapeDtypeStruct((2,) + r.shape, F32) for r in reds),
                  scratch_shapes=[pltpu.SemaphoreType.DMA((n,)), pltpu.SemaphoreType.DMA((n,))],
                  compiler_params=_params())(*reds)


def _adamw(name, g, w, m, v):
    rows, cols = g.shape
    tr = _tile(rows, 512)

    def body(g_ref, w_ref, m_ref, v_ref, d_ref, mo_ref, vo_ref):
        gv = g_ref[...]
        mn = ADAM_B1 * m_ref[...] + (1.0 - ADAM_B1) * gv
        vn = ADAM_B2 * v_ref[...] + (1.0 - ADAM_B2) * jnp.square(gv)
        m_hat = mn / (1.0 - ADAM_B1 ** ADAM_STEP)
        v_hat = vn / (1.0 - ADAM_B2 ** ADAM_STEP)
        d_ref[...] = -ADAM_LR * (m_hat / (jnp.sqrt(v_hat) + ADAM_EPS) + ADAM_WD * w_ref[...])
        mo_ref[...] = mn
        vo_ref[...] = vn

    blk = pl.BlockSpec((tr, cols), lambda i: (i, 0))
    shp = jax.ShapeDtypeStruct((rows, cols), F32)
    return _pcall(body, name=f"adamw_{name}", grid=(rows // tr,), in_specs=[blk] * 4, out_specs=(blk,) * 3,
                  out_shape=(shp,) * 3, compiler_params=_params())(g, w, m, v)


def _pack(pieces, rows, dtype):
    flat = jnp.concatenate([p.reshape(-1).astype(dtype) for p in pieces])
    return jnp.pad(flat, (0, rows * PACK_COLS - flat.shape[0])).reshape(rows, PACK_COLS)


def _unpack(pack, shapes):
    flat = pack.reshape(-1)
    out, off = [], 0
    for shp in shapes:
        n = math.prod(shp)
        out.append(flat[off:off + n].reshape(shp))
        off += n
    return out


def _shard_of(full, axis, j):
    n = full.shape[axis] // 4
    return lax.slice_in_dim(full, j * n, (j + 1) * n, axis=axis)


def kernel(x, even_w_in, even_conv_w, even_conv_b, ssd_a_log, ssd_dt_bias, ssd_d, ssd_norm_w, lru_w_a, lru_b_a, lru_w_x, lru_b_x, lru_lambda, even_w_out, odd_w_in, hgrn_lb_logits, hgrn_norm_w, odd_w_out, norm_mix, norm_mlp, mlp_w1, mlp_w2, norm_final, loss_target, m_even_w_in, m_even_conv_w, m_even_conv_b, m_ssd_a_log, m_ssd_dt_bias, m_ssd_d, m_ssd_norm_w, m_lru_w_a, m_lru_b_a, m_lru_w_x, m_lru_b_x, m_lru_lambda, m_even_w_out, m_odd_w_in, m_hgrn_lb_logits, m_hgrn_norm_w, m_odd_w_out, m_norm_mix, m_norm_mlp, m_mlp_w1, m_mlp_w2, m_norm_final, v_even_w_in, v_even_conv_w, v_even_conv_b, v_ssd_a_log, v_ssd_dt_bias, v_ssd_d, v_ssd_norm_w, v_lru_w_a, v_lru_b_a, v_lru_w_x, v_lru_b_x, v_lru_lambda, v_even_w_out, v_odd_w_in, v_hgrn_lb_logits, v_hgrn_norm_w, v_odd_w_out, v_norm_mix, v_norm_mlp, v_mlp_w1, v_mlp_w2, v_norm_final):
    names = [n for n, _, _, _ in WEIGHTS]
    w_loc = dict(zip(names, (even_w_in, even_conv_w, even_conv_b, ssd_a_log, ssd_dt_bias, ssd_d, ssd_norm_w, lru_w_a, lru_b_a, lru_w_x, lru_b_x, lru_lambda, even_w_out, odd_w_in, hgrn_lb_logits, hgrn_norm_w, odd_w_out, norm_mix, norm_mlp, mlp_w1, mlp_w2, norm_final)))
    m_loc = dict(zip(names, (m_even_w_in, m_even_conv_w, m_even_conv_b, m_ssd_a_log, m_ssd_dt_bias, m_ssd_d, m_ssd_norm_w, m_lru_w_a, m_lru_b_a, m_lru_w_x, m_lru_b_x, m_lru_lambda, m_even_w_out, m_odd_w_in, m_hgrn_lb_logits, m_hgrn_norm_w, m_odd_w_out, m_norm_mix, m_norm_mlp, m_mlp_w1, m_mlp_w2, m_norm_final)))
    v_loc = dict(zip(names, (v_even_w_in, v_even_conv_w, v_even_conv_b, v_ssd_a_log, v_ssd_dt_bias, v_ssd_d, v_ssd_norm_w, v_lru_w_a, v_lru_b_a, v_lru_w_x, v_lru_b_x, v_lru_lambda, v_even_w_out, v_odd_w_in, v_hgrn_lb_logits, v_hgrn_norm_w, v_odd_w_out, v_norm_mix, v_norm_mlp, v_mlp_w1, v_mlp_w2, v_norm_final)))
    spec = {n: (blk, full, ax) for n, blk, full, ax in WEIGHTS}

    small = [n for n in names if n not in BIG]
    two_d = lambda n, v: v.reshape(BIG_2D[n])

    me = 2 * lax.axis_index("x") + lax.axis_index("y")
    cc = lax.axis_index("c")
    put = lambda whole, part, k: lax.dynamic_update_slice_in_dim(whole, part[None], k, axis=0)
    own = {n: two_d(n, w_loc[n]).astype(BF16) for n in BIG if not n.startswith("mlp")}
    own["small"] = _pack([w_loc[n] for n in SMALL_SHARDED], 16, F32)
    fill = lambda got, keys: [put(g, own[k], me) for g, k in zip(got, keys)]
    first = ("even_w_in", "small")
    g_in0, g_small = fill(_gather_chips([own[k] for k in first]), first)
    w_main0, w_dt0 = _assemble_in0(g_in0)
    w_full = {n: w_loc[n] for n in names if spec[n][2] is None}
    shards = [_unpack(g_small[j], [spec[n][0] for n in SMALL_SHARDED]) for j in range(4)]
    for n in ("mlp_w1", "mlp_w2"):
        for l in range(2):
            own[f"{n}_l{l}"] = w_loc[n][l].astype(BF16)
    riders = {"ssd0": ("mlp_w1_l0", "even_w_out"), "ssd1": ("mlp_w2_l0",), "odd": ("odd_w_in", "odd_w_out"),
              "gla0": ("mlp_w1_l1",), "gla1": ("mlp_w2_l1",)}
    carries = {key: _gather_carry([own[k] for k in ks]) for key, ks in riders.items()}

    def arrived(key, got):
        out = {}
        for k, g in zip(riders[key], fill(got, riders[key])):
            if k == "odd_w_in":
                out[k] = jnp.concatenate([g[j] for j in range(4)], axis=1)[None]
            elif k in ("odd_w_out", "even_w_out"):
                out[k] = g.reshape(spec[k][1])
            else:
                out[(k[:6], int(k[-1]))] = g if k.startswith("mlp_w1") else g.reshape(4096, 1024)
        return out

    for i, n in enumerate(SMALL_SHARDED):
        w_full[n] = jnp.concatenate([shards[j][i] for j in range(4)], axis=spec[n][2])

    cidx = cc.astype(jnp.int32).reshape(1)

    def pair_reduce(tags, tensors, run=None):
        lands = run(_pair_swap_carry(tensors)) if run else _pair_swap(tags[0], tensors)
        return [_pair_add(tag, g, land, cidx) for tag, g, land in zip(tags, tensors, lands)]

    loss_vec, grad_x, grads, big, (early_sums, early_landed) = _local_step(
        x, loss_target, w_full, w_main0, w_dt0, pair_reduce, (carries, arrived))
    loss = lax.psum(loss_vec[0, 0], ("x", "y", "c"))

    def dest_pack(j):
        return _pack([grads[n].reshape(spec[n][1]) if spec[n][2] is None else _shard_of(grads[n].reshape(spec[n][1]), spec[n][2], j)
                      for n in small], SMALL_ROWS, F32)

    late_tags = LATE + ("small",)
    late_sums = pair_reduce(late_tags, [big[n] for n in LATE] + [jnp.stack([dest_pack(j) for j in range(4)])])
    tags = EARLY + MID + late_tags
    chip_sums = list(early_sums) + late_sums
    landed = [put(land, lax.dynamic_index_in_dim(cs, me, axis=0, keepdims=False), me)
              for land, cs in zip(list(early_landed) + list(_chip_scatter("late", late_sums)), chip_sums)]
    halves = [_chip_sum(tag, land) for tag, land in zip(tags, landed)]
    red = {tag: put(r, h, cc).reshape(-1, r.shape[-1]) for tag, r, h in zip(tags, _pair_join(halves), halves)}
    for n in ("mlp_w1", "mlp_w2"):
        red[n] = jnp.concatenate([red[n + "_l0"], red[n + "_l1"]], axis=0)

    outs = {}
    for n, g in ((n, red[n]) for n in BIG):
        res = (g, *_adamw(n, g, two_d(n, w_loc[n]), two_d(n, m_loc[n]), two_d(n, v_loc[n])))
        outs[n] = [r.reshape(spec[n][0]) for r in res]
    blocks = [spec[n][0] for n in small]
    wp, mp, vp = (_pack([src[n] for n in small], SMALL_ROWS, F32) for src in (w_loc, m_loc, v_loc))
    res = (red["small"], *_adamw("small", red["small"], wp, mp, vp))
    unpacked = [_unpack(r, blocks) for r in res]
    for i, n in enumerate(small):
        outs[n] = [u[i] for u in unpacked]
    return (loss, grad_x, *[outs[n][k] for k in range(4) for n in names])
```

```python
import functools
import math

import jax
import jax.numpy as jnp
from jax import lax
from jax.experimental import pallas as pl
from jax.experimental.pallas import tpu as pltpu

F32 = jnp.float32
BF16 = jnp.bfloat16
MXU_DTYPE = jnp.bfloat16
MESH = pl.DeviceIdType.MESH

EPS = 1e-6
SSD_HEADS = 16
SSD_HEADDIM = 64
HEAD_SHIFT = 6
SSD_GROUPS = 4
SSD_CHUNK = 128
LRU_C = 8.0
LRU_ROWS = 256
HGRN_HEADS = 8
HGRN_HEADDIM = 128
HGRN_SUB = 32
HGRN_SUB_SHIFT = 5
HGRN_BLOCK = 128
HGRN_SCALE = HGRN_HEADDIM ** -0.5
CONV_ROWS = 512
ROWS_FWD = 512
ROWS_BWD = 256

ADAM_LR = 0.001
ADAM_B1 = 0.9
ADAM_B2 = 0.999
ADAM_EPS = 1e-08
ADAM_WD = 0.01
ADAM_STEP = 10

VMEM_LIMIT = 56 * 1024 * 1024
PACK_COLS = 1024
SMALL_ROWS = 288

WEIGHTS = (
    ("even_w_in", (1, 1024, 1288), (1, 1024, 5152), 2),
    ("even_conv_w", (1, 4, 768), (1, 4, 3072), 2),
    ("even_conv_b", (1, 3072), (1, 3072), None),
    ("ssd_a_log", (1, 2, 16), (1, 2, 16), None),
    ("ssd_dt_bias", (1, 2, 16), (1, 2, 16), None),
    ("ssd_d", (1, 16), (1, 16), None),
    ("ssd_norm_w", (1, 1024), (1, 1024), None),
    ("lru_w_a", (1, 2, 16, 64, 64), (1, 2, 16, 64, 64), None),
    ("lru_b_a", (1, 2, 256), (1, 2, 1024), 2),
    ("lru_w_x", (1, 2, 16, 64, 64), (1, 2, 16, 64, 64), None),
    ("lru_b_x", (1, 2, 256), (1, 2, 1024), 2),
    ("lru_lambda", (1, 2, 256), (1, 2, 1024), 2),
    ("even_w_out", (1, 512, 1024), (1, 2048, 1024), 1),
    ("odd_w_in", (1, 1024, 1280), (1, 1024, 5120), 2),
    ("hgrn_lb_logits", (2, 1024), (2, 1024), None),
    ("hgrn_norm_w", (1, 256), (1, 1024), 1),
    ("odd_w_out", (1, 256, 1024), (1, 1024, 1024), 1),
    ("norm_mix", (2, 1024), (2, 1024), None),
    ("norm_mlp", (2, 1024), (2, 1024), None),
    ("mlp_w1", (2, 1024, 1024), (2, 1024, 4096), 2),
    ("mlp_w2", (2, 1024, 1024), (2, 4096, 1024), 1),
    ("norm_final", (1024,), (1024,), None),
)
BIG = ("even_w_in", "even_w_out", "odd_w_in", "odd_w_out", "mlp_w1", "mlp_w2")
BIG_2D = {"even_w_in": (1024, 1288), "even_w_out": (512, 1024), "odd_w_in": (1024, 1280), "odd_w_out": (256, 1024),
          "mlp_w1": (2048, 1024), "mlp_w2": (2048, 1024)}
SMALL_SHARDED = ("even_conv_w", "lru_b_a", "lru_b_x", "lru_lambda", "hgrn_norm_w")


def _pcall(body, carry=None, **kw):
    if carry is not None:
        srcs, shapes, scratch, start, finish = carry
        grid, inner = kw["grid"], body
        as_tuple = lambda v: tuple(v) if isinstance(v, (tuple, list)) else (v,)
        out_specs, out_shape, own_scratch = as_tuple(kw["out_specs"]), as_tuple(kw["out_shape"]), list(kw.get("scratch_shapes", ()))
        a = len(kw["in_specs"])
        b = a + len(srcs)
        c = b + len(out_specs)
        d = c + len(shapes)
        e = d + len(own_scratch)

        def body(*refs):
            ids = [pl.program_id(ax) for ax in range(len(grid))]
            first = functools.reduce(jnp.logical_and, [i == 0 for i in ids])
            last = functools.reduce(jnp.logical_and, [i == g - 1 for i, g in zip(ids, grid)])
            pl.when(first)(lambda: start(refs[a:b], refs[c:d], *refs[e:]))
            inner(*refs[:a], *refs[b:c], *refs[d:e])
            pl.when(last)(lambda: finish(refs[a:b], refs[c:d], *refs[e:]))

        kw = dict(kw, in_specs=list(kw["in_specs"]) + [ANY] * len(srcs), out_specs=out_specs + (ANY,) * len(shapes),
                  out_shape=out_shape + tuple(shapes), scratch_shapes=own_scratch + list(scratch))
    return pl.pallas_call(body, **kw)


def _params(**kw):
    return pltpu.CompilerParams(vmem_limit_bytes=VMEM_LIMIT, **kw)


def _tile(n, pref):
    if n <= pref:
        return n
    t = (pref // 128) * 128
    while n % t:
        t -= 128
    return t


def _dot(a, b, dims=(((1,), (0,)), ((), ()))):
    return lax.dot_general(a, b, dims, preferred_element_type=F32)


_NN = (((1,), (0,)), ((), ()))
_NT = (((1,), (1,)), ((), ()))
_TN = (((0,), (0,)), ((), ()))


def _mx(v):
    return v.astype(MXU_DTYPE)


def _dot01(a, b, dims=_NN, *, split, terms):
    acc, rest = None, (a if split == "a" else b)
    for _ in range(terms):
        piece = _mx(rest)
        part = _dot(piece, _mx(b), dims) if split == "a" else _dot(_mx(a), piece, dims)
        acc = part if acc is None else acc + part
        rest = rest - piece.astype(F32)
    return acc


def _mm(name, a, b, mode, *, out_dtype=F32, res=None, relu2=False, relu2_of=None, col_shards=1, carry=None):
    shards = b.shape[0] if b.ndim == 3 else 0
    b2 = b.shape[1:] if shards else b.shape
    if mode == "nn":
        (m, kk), n = a.shape, b2[1] * max(shards, 1)
    elif mode == "nt":
        (m, kk), n = a.shape, b2[0]
    else:
        (kk, m), (_, n) = a.shape, b.shape
    assert res is None or relu2_of is None
    tk_pref = 1024
    if mode == "tn" and a.dtype.itemsize == 2 and b.dtype.itemsize == 2:
        tk_pref = 2048
    tm, tn, tk = _tile(m, 1024), _tile(n // col_shards, 1024), _tile(kk, tk_pref)
    nk = kk // tk
    dims = {"nn": _NN, "nt": _NT, "tn": _TN}[mode]
    a_spec = pl.BlockSpec((tk, tm), lambda i, j, k: (k, i)) if mode == "tn" else pl.BlockSpec((tm, tk), lambda i, j, k: (i, k))
    b_spec = pl.BlockSpec((tn, tk), lambda i, j, k: (j, k)) if mode == "nt" else pl.BlockSpec((tk, tn), lambda i, j, k: (k, j))
    if shards and mode == "nn":
        assert tn == b2[1]
        b_spec = pl.BlockSpec((None, tk, tn), lambda i, j, k: (j, k, 0))
    o_spec = pl.BlockSpec((tm, tn), lambda i, j, k: (i, j))
    o_shape = (m, n)
    if col_shards > 1:
        assert tn * col_shards == n and res is None and not relu2
        o_spec = pl.BlockSpec((None, tm, tn), lambda i, j, k: (j, i, 0))
        o_shape = (col_shards, m, tn)
    extra = res if res is not None else relu2_of
    has_res = extra is not None

    def body(*refs):
        a_ref, b_ref = refs[0], refs[1]
        res_ref = refs[2] if has_res else None
        outs = refs[2 + has_res:2 + has_res + 1 + relu2]

        def finish(r):
            if res is not None:
                r = r + res_ref[...]
            if relu2_of is not None:
                r = r * (2.0 * jnp.maximum(res_ref[...].astype(F32), 0.0))
            if relu2:
                outs[0][...] = r.astype(outs[0].dtype)
                outs[1][...] = jnp.square(jnp.maximum(r, 0.0)).astype(outs[1].dtype)
            else:
                outs[0][...] = r.astype(outs[0].dtype)

        prod = _dot(_mx(a_ref[...]), _mx(b_ref[...]), dims)
        if nk == 1:
            finish(prod)
            return
        acc = refs[-1]
        k = pl.program_id(2)

        @pl.when(k == 0)
        def _():
            acc[...] = prod

        @pl.when(k > 0)
        def _():
            acc[...] += prod

        @pl.when(k == nk - 1)
        def _():
            finish(acc[...])

    in_specs = [a_spec, b_spec] + ([o_spec] if has_res else [])
    if relu2:
        out_shape = (jax.ShapeDtypeStruct((m, n), BF16), jax.ShapeDtypeStruct((m, n), BF16))
        out_specs = (o_spec, o_spec)
    else:
        out_shape = jax.ShapeDtypeStruct(o_shape, out_dtype)
        out_specs = o_spec
    args = (a, b) + ((extra,) if has_res else ()) + (tuple(carry[0]) if carry else ())
    return _pcall(body, carry=carry, name=name, grid=(m // tm, n // tn, nk), in_specs=in_specs, out_specs=out_specs,
                  out_shape=out_shape, scratch_shapes=[pltpu.VMEM((tm, tn), F32)] if nk > 1 else [],
                  compiler_params=_params())(*args)


def _mm_sum_nt(name, parts, wblocks, norm_bwd=None, add=()):
    parts = [p if isinstance(p, tuple) else (p, 0, p.shape[1]) for p in parts]
    m, npart = parts[0][0].shape[0], len(parts)
    n = wblocks[0][0].shape[-2]
    tm, tn = _tile(m, 512), _tile(n, 1024)
    assert norm_bwd is None or tn == n

    def body(*refs):
        acc = _dot(_mx(refs[0][...]), _mx(refs[npart][...]), _NT)
        for k in range(1, npart):
            acc = acc + _dot(_mx(refs[k][...]), _mx(refs[npart + k][...]), _NT)
        if norm_bwd is None:
            for r in refs[2 * npart:-1]:
                acc = acc + r[...]
            refs[-1][...] = acc
            return
        x_ref, g_ref, res_ref, dx_ref, dg_ref = refs[2 * npart:]
        _, vjp = jax.vjp(_f_norm, x_ref[...], g_ref[...])
        dx, dg = vjp((acc,))
        dx_ref[...] = dx + res_ref[...]

        @pl.when(pl.program_id(0) == 0)
        def _():
            dg_ref[...] = jnp.zeros_like(dg_ref)

        dg_ref[...] += dg

    row = pl.BlockSpec((tm, tn), lambda i, j: (i, j))
    vec = pl.BlockSpec((1, tn), lambda i, j: (0, j))
    in_specs = [pl.BlockSpec((tm, wd), lambda i, j, cb=cb: (i, cb)) for _, cb, wd in parts]
    for (_, _, wd), (w, cb) in zip(parts, wblocks):
        in_specs.append(pl.BlockSpec((None, tn, wd), lambda i, j, cb=cb: (cb, j, 0)) if w.ndim == 3
                        else pl.BlockSpec((tn, wd), lambda i, j, cb=cb: (j, cb)))
    args = [p for p, _, _ in parts] + [w for w, _ in wblocks]
    if norm_bwd is None:
        return _pcall(body, name=name, grid=(m // tm, n // tn), in_specs=in_specs + [row] * len(add), out_specs=row,
                      out_shape=jax.ShapeDtypeStruct((m, n), F32), compiler_params=_params())(*args, *add)
    return _pcall(body, name=name, grid=(m // tm, 1), in_specs=in_specs + [row, vec, row], out_specs=(row, vec),
                  out_shape=(jax.ShapeDtypeStruct((m, n), F32), jax.ShapeDtypeStruct((1, n), F32)),
                  compiler_params=_params())(*args, *norm_bwd)


def _pw_fwd(name, f, ins, params, out_dtypes, tc, ncol, tm=ROWS_FWD, groups=1):
    t = ins[0][0].shape[0]
    tm = min(tm, t)
    ni, npar = len(ins), len(params)
    gw = tc // groups

    def body(*refs):
        for g in range(groups):
            sl = slice(g * gw, (g + 1) * gw)
            vals = f(*[r[:, sl].astype(F32) for r in refs[:ni]], *[r[:, sl] for r in refs[ni:ni + npar]])
            for o, v in zip(refs[ni + npar:], vals):
                o[:, sl] = v.astype(o.dtype)

    in_specs = [pl.BlockSpec((tm, tc), lambda j, i, off=off: (i, off + j)) for _, off in ins]
    in_specs += [pl.BlockSpec((1, tc), lambda j, i, off=off: (0, off + j)) for _, off in params]
    out_specs = tuple(pl.BlockSpec((tm, tc), lambda j, i: (i, j)) for _ in out_dtypes)
    out_shape = tuple(jax.ShapeDtypeStruct((t, ncol * tc), d) for d in out_dtypes)
    return _pcall(body, name=name, grid=(ncol, t // tm), in_specs=in_specs, out_specs=out_specs, out_shape=out_shape,
                  compiler_params=_params())(*[a for a, _ in ins], *[p for p, _ in params])


def _pw_bwd(name, f, ins, params, douts, tc, ncol, want, adds=None, tm=ROWS_BWD, out_dtypes=None, groups=1):
    adds = adds or {}
    out_dtypes = out_dtypes or [F32] * len(want)
    t = ins[0][0].shape[0]
    tm = min(tm, t)
    ni, npar, nd, na = len(ins), len(params), len(douts), len(adds)
    add_keys = sorted(adds)
    gw = tc // groups

    def body(*refs):
        in_refs, p_refs = refs[:ni], refs[ni:ni + npar]
        d_refs = refs[ni + npar:ni + npar + nd]
        a_refs = refs[ni + npar + nd:ni + npar + nd + na]
        o_refs = refs[ni + npar + nd + na:]
        for p in range(npar):
            @pl.when(pl.program_id(1) == 0)
            def _(o=o_refs[len(want) + p]):
                o[...] = jnp.zeros_like(o)

        for g in range(groups):
            sl = slice(g * gw, (g + 1) * gw)
            _, vjp = jax.vjp(f, *[r[:, sl].astype(F32) for r in in_refs], *[r[:, sl] for r in p_refs])
            cts = vjp(tuple(d[:, sl].astype(F32) for d in d_refs))
            for o, kidx in zip(o_refs[:len(want)], want):
                v = cts[kidx]
                if kidx in adds:
                    v = v + a_refs[add_keys.index(kidx)][:, sl]
                o[:, sl] = v.astype(o.dtype)
            for p in range(npar):
                o_refs[len(want) + p][:, sl] += cts[ni + p]

    in_specs = [pl.BlockSpec((tm, tc), lambda j, i, off=off: (i, off + j)) for _, off in ins]
    in_specs += [pl.BlockSpec((1, tc), lambda j, i, off=off: (0, off + j)) for _, off in params]
    in_specs += [pl.BlockSpec((tm, tc), lambda j, i: (i, j)) for _ in range(nd + na)]
    out_specs = tuple([pl.BlockSpec((tm, tc), lambda j, i: (i, j)) for _ in want]
                      + [pl.BlockSpec((1, tc), lambda j, i: (0, j)) for _ in params])
    out_shape = tuple([jax.ShapeDtypeStruct((t, ncol * tc), dt) for dt in out_dtypes]
                      + [jax.ShapeDtypeStruct((1, ncol * tc), F32) for _ in params])
    res = _pcall(body, name=name, grid=(ncol, t // tm), in_specs=in_specs, out_specs=out_specs, out_shape=out_shape,
                 compiler_params=_params())(*[a for a, _ in ins], *[p for p, _ in params], *douts, *[adds[k] for k in add_keys])
    return list(res[:len(want)]), list(res[len(want):])


def _rms(x, g):
    return (x * lax.rsqrt(jnp.mean(x * x, axis=-1, keepdims=True) + EPS)) * g


def _f_norm(x, g):
    return (_rms(x, g),)


def _f_softplus(d, b):
    return (jax.nn.softplus(d + b),)


def _f_ssd_post(yf, yb, xs, z, dskip, nw):
    u = (yf + yb + dskip * xs) * jax.nn.silu(z)
    return (_rms(u, nw),)


def _neg_expm1(v):
    t = jnp.tanh(0.5 * v)
    return -2.0 * t / (1.0 - t)


def _lru_gates_parts(pre_a, pre_x, u, ba, bx, lam):
    rg = jax.nn.sigmoid(pre_a + ba)
    ig = jax.nn.sigmoid(pre_x + bx)
    sp = jax.nn.softplus(-lam)
    a = jnp.exp(-LRU_C * rg * sp)
    m = jnp.sqrt(_neg_expm1(-2.0 * LRU_C * rg * sp))
    return rg, ig, sp, a, m


@jax.custom_vjp
def _f_lru_gates(pre_a, pre_x, u, ba, bx, lam):
    _, ig, _, a, m = _lru_gates_parts(pre_a, pre_x, u, ba, bx, lam)
    return a, m * (ig * u)


def _lru_gates_fwd(pre_a, pre_x, u, ba, bx, lam):
    rg, ig, sp, a, m = _lru_gates_parts(pre_a, pre_x, u, ba, bx, lam)
    return (a, m * (ig * u)), (rg, ig, sp, a, m, u, lam)


def _lru_gates_bwd(saved, cts):
    rg, ig, sp, a, m, u, lam = saved
    da, db = cts
    dlog_a = (da - db * (ig * u) * (a / m)) * a
    dpa = dlog_a * (-LRU_C * sp) * (rg * (1.0 - rg))
    dpx = db * (m * u) * (ig * (1.0 - ig))
    dlam = jnp.sum(dlog_a * (LRU_C * rg), axis=0, keepdims=True) * jax.nn.sigmoid(-lam)
    return (dpa, dpx, db * (m * ig), jnp.sum(dpa, axis=0, keepdims=True), jnp.sum(dpx, axis=0, keepdims=True), dlam)


_f_lru_gates.defvjp(_lru_gates_fwd, _lru_gates_bwd)


def _f_lru_post(hf, hb, gate):
    return ((hf + hb) * jax.nn.gelu(gate),)


def _f_hgrn_pre(fr, l0, l1):
    lb = jax.nn.sigmoid(l1 - l0)
    k = (1.0 - lb) * jax.nn.sigmoid(-fr)
    return k, jnp.log1p(-k)


def _f_hgrn_post(of, ob, gate, nw):
    return (_rms(of + ob, nw) * jax.nn.silu(gate),)


def _loss_head(x, tgt, g, tm=ROWS_FWD):
    t, d = x.shape
    tm = min(tm, t)

    def body(x_ref, t_ref, g_ref, dx_ref, dg_ref, loss_ref):
        tv = t_ref[...]

        def lf(xv, gv):
            return 0.5 * jnp.sum(jnp.mean(jnp.square(_rms(xv, gv) - tv), axis=-1))

        val, vjp = jax.vjp(lf, x_ref[...], g_ref[...])
        dx, dg = vjp(jnp.ones((), F32))
        dx_ref[...] = dx

        @pl.when(pl.program_id(0) == 0)
        def _():
            dg_ref[...] = jnp.zeros_like(dg_ref)
            loss_ref[...] = jnp.zeros_like(loss_ref)

        dg_ref[...] += dg
        loss_ref[...] += jnp.full(loss_ref.shape, val, F32)

    row = pl.BlockSpec((tm, d), lambda i: (i, 0))
    vec = pl.BlockSpec((1, d), lambda i: (0, 0))
    return _pcall(body, name="loss_head", grid=(t // tm,), in_specs=[row, row, vec],
                  out_specs=(row, vec, pl.BlockSpec((1, 128), lambda i: (0, 0))),
                  out_shape=(jax.ShapeDtypeStruct((t, d), F32), jax.ShapeDtypeStruct((1, d), F32),
                             jax.ShapeDtypeStruct((1, 128), F32)), compiler_params=_params())(x, tgt, g)


def _shifted(x, d, prev, nxt, first, last):
    r = x.shape[0]
    row = lax.broadcasted_iota(jnp.int32, x.shape, 0)
    if d < 0:
        out = pltpu.roll(x, -d, 0)
        for q in range(-d):
            pv = jnp.where(first, 0.0, prev[8 + d + q:8 + d + q + 1, :])
            out = jnp.where(row == q, pv, out)
        return out
    out = pltpu.roll(x, r - d, 0)
    for q in range(d):
        nv = jnp.where(last, 0.0, nxt[q:q + 1, :])
        out = jnp.where(row == r - d + q, nv, out)
    return out


def _conv_fwd(p3, w, b, col0, ncol, silu, tc=1024):
    nbatch, s, _ = p3.shape
    ts = min(CONV_ROWS, s)
    nblk = s // ts

    def body(x_ref, pv_ref, nx_ref, w_ref, b_ref, o_ref, *act_ref):
        i = pl.program_id(1)
        first, last = i == 0, i == nblk - 1
        x, pv, nx = x_ref[...], pv_ref[...], nx_ref[...]
        wv = w_ref[...]
        out = b_ref[...] + wv[1:2] * x
        out = out + wv[0:1] * _shifted(x, -1, pv, nx, first, last)
        out = out + wv[2:3] * _shifted(x, 1, pv, nx, first, last)
        out = out + wv[3:4] * _shifted(x, 2, pv, nx, first, last)
        o_ref[...] = out
        if silu:
            act_ref[0][...] = jax.nn.silu(out)

    nb8 = s // 8
    cur = pl.BlockSpec((None, ts, tc), lambda n, i, j: (n, i, col0 + j))
    prev = pl.BlockSpec((None, 8, tc), lambda n, i, j: (n, jnp.maximum(i * (ts // 8) - 1, 0), col0 + j))
    nxt = pl.BlockSpec((None, 8, tc), lambda n, i, j: (n, jnp.minimum((i + 1) * (ts // 8), nb8 - 1), col0 + j))
    out = pl.BlockSpec((None, ts, tc), lambda n, i, j: (n, i, j))
    shp = jax.ShapeDtypeStruct((nbatch, s, ncol * tc), F32)
    return _pcall(body, name=f"conv_fwd{col0}", grid=(nbatch, nblk, ncol),
                  in_specs=[cur, prev, nxt, pl.BlockSpec((4, tc), lambda n, i, j: (0, col0 + j)),
                            pl.BlockSpec((1, tc), lambda n, i, j: (0, col0 + j))],
                  out_specs=(out, out) if silu else out, out_shape=(shp, shp) if silu else shp,
                  compiler_params=_params())(p3, p3, p3, w, b)


def _conv_bwd(dc3, p3, w, col, conv3=None):
    nbatch, s, tc = dc3.shape
    ts = min(CONV_ROWS, s)
    nblk = s // ts
    silu = conv3 is not None

    def body(d_ref, dpv_ref, dnx_ref, x_ref, pv_ref, nx_ref, w_ref, *rest):
        n, i = pl.program_id(0), pl.program_id(1)
        first, last = i == 0, i == nblk - 1
        d, dpv, dnx = d_ref[...], dpv_ref[...], dnx_ref[...]
        if silu:
            d, dpv, dnx = [jax.vjp(jax.nn.silu, c_ref[...])[1](t)[0] for c_ref, t in zip(rest[:3], (d, dpv, dnx))]
        dx_ref, dw_ref = rest[3 * silu:]
        x, pv, nx = x_ref[...], pv_ref[...], nx_ref[...]
        wv = w_ref[...]
        dx = wv[1:2] * d
        dx = dx + wv[0:1] * _shifted(d, 1, dpv, dnx, first, last)
        dx = dx + wv[2:3] * _shifted(d, -1, dpv, dnx, first, last)
        dx = dx + wv[3:4] * _shifted(d, -2, dpv, dnx, first, last)
        dx_ref[...] = dx.astype(dx_ref.dtype)

        @pl.when((n == 0) & (i == 0))
        def _():
            dw_ref[...] = jnp.zeros_like(dw_ref)

        dw_ref[0:1, :] += jnp.sum(d * _shifted(x, -1, pv, nx, first, last), axis=0, keepdims=True)
        dw_ref[1:2, :] += jnp.sum(d * x, axis=0, keepdims=True)
        dw_ref[2:3, :] += jnp.sum(d * _shifted(x, 1, pv, nx, first, last), axis=0, keepdims=True)
        dw_ref[3:4, :] += jnp.sum(d * _shifted(x, 2, pv, nx, first, last), axis=0, keepdims=True)
        dw_ref[4:5, :] += jnp.sum(d, axis=0, keepdims=True)

    nb8 = s // 8

    def specs(j):
        cur = pl.BlockSpec((None, ts, tc), lambda n, i: (n, i, j))
        prev = pl.BlockSpec((None, 8, tc), lambda n, i: (n, jnp.maximum(i * (ts // 8) - 1, 0), j))
        nxt = pl.BlockSpec((None, 8, tc), lambda n, i: (n, jnp.minimum((i + 1) * (ts // 8), nb8 - 1), j))
        return [cur, prev, nxt]

    return _pcall(body, name=f"conv_bwd{col}", grid=(nbatch, nblk),
                  in_specs=specs(0) + specs(col) + [pl.BlockSpec((4, tc), lambda n, i: (0, col))] + specs(col) * silu,
                  out_specs=(specs(0)[0], pl.BlockSpec((8, tc), lambda n, i: (0, 0))),
                  out_shape=(jax.ShapeDtypeStruct((nbatch, s, tc), BF16), jax.ShapeDtypeStruct((8, tc), F32)),
                  compiler_params=_params())(dc3, dc3, dc3, p3, p3, p3, w, *([conv3] * 3 * silu))


def _block_scan(coef, inp, reverse):
    r = coef.shape[0]
    row = lax.broadcasted_iota(jnp.int32, coef.shape, 0)
    a, b = coef, inp
    d = 1
    while d < r:
        if reverse:
            keep = row < r - d
            a_sh, b_sh = pltpu.roll(a, r - d, 0), pltpu.roll(b, r - d, 0)
        else:
            keep = row >= d
            a_sh, b_sh = pltpu.roll(a, d, 0), pltpu.roll(b, d, 0)
        b = b + a * jnp.where(keep, b_sh, 0.0)
        a = a * jnp.where(keep, a_sh, 1.0)
        d *= 2
    return a, b


def _lru_scan(a3, b3, reverse):
    nbatch, s, w = a3.shape
    ts = min(LRU_ROWS, s)
    nblk = s // ts
    edge = 0 if reverse else ts - 1

    def body(a_ref, b_ref, h_ref, carry):
        @pl.when(pl.program_id(1) == 0)
        def _():
            carry[...] = jnp.zeros_like(carry)

        ca, hb = _block_scan(a_ref[...], b_ref[...], reverse)
        h = hb + ca * carry[0:1, :]
        h_ref[...] = h
        carry[0:1, :] = h[edge:edge + 1, :]

    blk = pl.BlockSpec((None, ts, w), (lambda n, i: (n, nblk - 1 - i, 0)) if reverse else (lambda n, i: (n, i, 0)))
    return _pcall(body, name=f"lru_scan_r{int(reverse)}", grid=(nbatch, nblk), in_specs=[blk, blk], out_specs=blk,
                  out_shape=jax.ShapeDtypeStruct((nbatch, s, w), F32), scratch_shapes=[pltpu.VMEM((8, w), F32)],
                  compiler_params=_params())(a3, b3)


def _lru_scan_bwd(a3, h3, dh3, reverse, carry=None):
    nbatch, s, w = a3.shape
    ts = min(LRU_ROWS, s)
    nblk = s // ts
    nb8 = s // 8
    tpb = ts // 8

    def body(a_ref, aa_ref, h_ref, hh_ref, dh_ref, g_ref, da_ref, carry):
        i = pl.program_id(1)

        @pl.when(i == 0)
        def _():
            carry[...] = jnp.zeros_like(carry)

        a, h = a_ref[...], h_ref[...]
        row = lax.broadcasted_iota(jnp.int32, a.shape, 0)
        if reverse:
            a_edge = jnp.where(i == 0, 0.0, aa_ref[7:8, :])
            c = jnp.where(row == 0, a_edge, pltpu.roll(a, 1, 0))
            h_edge = jnp.where(i == nblk - 1, 0.0, hh_ref[0:1, :])
            h_sh = jnp.where(row == ts - 1, h_edge, pltpu.roll(h, ts - 1, 0))
        else:
            a_edge = jnp.where(i == 0, 0.0, aa_ref[0:1, :])
            c = jnp.where(row == ts - 1, a_edge, pltpu.roll(a, ts - 1, 0))
            h_edge = jnp.where(i == nblk - 1, 0.0, hh_ref[7:8, :])
            h_sh = jnp.where(row == 0, h_edge, pltpu.roll(h, 1, 0))
        cc, gb = _block_scan(c, dh_ref[...], not reverse)
        g = gb + cc * carry[0:1, :]
        g_ref[...] = g
        carry[0:1, :] = g[ts - 1:ts, :] if reverse else g[0:1, :]
        da_ref[...] = g * h_sh

    if reverse:
        bi = lambda i: i
    else:
        bi = lambda i: nblk - 1 - i
    blk = pl.BlockSpec((None, ts, w), lambda n, i: (n, bi(i), 0))
    before = pl.BlockSpec((None, 8, w), lambda n, i: (n, jnp.maximum(bi(i) * tpb - 1, 0), 0))
    after = pl.BlockSpec((None, 8, w), lambda n, i: (n, jnp.minimum((bi(i) + 1) * tpb, nb8 - 1), 0))
    a_tile, h_tile = (before, after) if reverse else (after, before)
    return _pcall(body, carry=carry, name=f"lru_scan_bwd_r{int(reverse)}", grid=(nbatch, nblk),
                  in_specs=[blk, a_tile, blk, h_tile, blk], out_specs=(blk, blk),
                  out_shape=(jax.ShapeDtypeStruct((nbatch, s, w), F32), jax.ShapeDtypeStruct((nbatch, s, w), F32)),
                  scratch_shapes=[pltpu.VMEM((8, w), F32)],
                  compiler_params=_params())(a3, a3, h3, h3, dh3, *(carry[0] if carry else ()))


def _head_expand(lane0):
    return (jnp.right_shift(lax.broadcasted_iota(jnp.int32, (128, 1024), 1), HEAD_SHIFT) + lane0
            == lax.broadcasted_iota(jnp.int32, (128, 1024), 0)).astype(F32)


def _head_reduce(lane0):
    return (jnp.right_shift(lax.broadcasted_iota(jnp.int32, (1024, 128), 0), HEAD_SHIFT) + lane0
            == lax.broadcasted_iota(jnp.int32, (1024, 128), 1)).astype(F32)


def _time_mask(q, reverse):
    ri = lax.broadcasted_iota(jnp.int32, (q, q), 0)
    ci = lax.broadcasted_iota(jnp.int32, (q, q), 1)
    return (ri <= ci) if reverse else (ri >= ci)


def _ssd_common(xs_ref, bc_ref, dt_ref, al_ref, reverse, lane0):
    q = xs_ref.shape[0]
    edge = 0 if reverse else q - 1
    dt = dt_ref[...]
    a = -jnp.exp(al_ref[...])
    mask = _time_mask(q, reverse)
    expand = _head_expand(lane0)
    cum = _dot01(mask.astype(F32), dt * a, split="b", terms=3)
    cum_x = _dot01(cum, expand, split="a", terms=2)
    dt_x = _dot01(dt, expand, split="a", terms=2)
    last_x = cum_x[edge:edge + 1, :]
    xs = xs_ref[...]
    bc = bc_ref[...]
    return dict(q=q, edge=edge, lane0=lane0, dt=dt, a=a, mask=mask, cum_t=cum.T, cum_x=cum_x, dt_x=dt_x, xs=xs,
                v=xs * dt_x, e_c=jnp.exp(cum_x), w=jnp.exp(last_x - cum_x), e_l=jnp.exp(last_x),
                bm=bc[:, :512], cm=bc[:, 512:])


def _ssd_decay(c, h):
    row = c["lane0"] + h
    seg = c["cum_x"][:, h * SSD_HEADDIM:h * SSD_HEADDIM + 1] - c["cum_t"][row:row + 1, :]
    return jnp.where(c["mask"], jnp.exp(jnp.minimum(seg, 0.0)), 0.0)


def _head_masks():
    lane = jnp.right_shift(lax.broadcasted_iota(jnp.int32, (1, 256), 1), HEAD_SHIFT)
    return [lane == e for e in range(4)]


def _ssd_fwd(xbc3, dt3, alog, reverse, carry=None):
    nbatch, s, _ = xbc3.shape
    q = min(SSD_CHUNK, s)
    nc = s // q
    lane0 = SSD_HEADS * int(reverse)

    def body(xs_ref, bc_ref, dt_ref, al_ref, y_ref, st_ref, st):
        @pl.when(pl.program_id(1) == 0)
        def _():
            st[...] = jnp.zeros_like(st)

        st_ref[...] = st[...]
        c = _ssd_common(xs_ref, bc_ref, dt_ref, al_ref, reverse, lane0)
        hm = _head_masks()
        for g in range(SSD_GROUPS):
            sl = slice(g * 256, (g + 1) * 256)
            cg, bg = _mx(c["cm"][:, g * 128:(g + 1) * 128]), _mx(c["bm"][:, g * 128:(g + 1) * 128])
            cb = _dot(cg, bg, _NT)
            vg = c["v"][:, sl]
            s0 = st[:, sl]
            yg = _dot(cg, _mx(s0)) * c["e_c"][:, sl]
            for e in range(4):
                m = _ssd_decay(c, 4 * g + e) * cb
                yg = yg + _dot(_mx(m), _mx(jnp.where(hm[e], vg, 0.0)))
            y_ref[:, sl] = yg
            st[:, sl] = c["e_l"][:, sl] * s0 + _dot(bg, _mx(vg * c["w"][:, sl]), _TN)

    ck = (lambda i: nc - 1 - i) if reverse else (lambda i: i)
    xs_spec = pl.BlockSpec((None, q, 1024), lambda n, i: (n, ck(i), 0))
    bc_spec = pl.BlockSpec((None, q, 1024), lambda n, i: (n, ck(i), 1))
    dt_spec = pl.BlockSpec((None, q, 128), lambda n, i: (n, ck(i), 0))
    al_spec = pl.BlockSpec((1, 128), lambda n, i: (0, 0))
    st_spec = pl.BlockSpec((None, None, 128, 1024), lambda n, i: (n, ck(i), 0, 0))
    return _pcall(body, carry=carry, name=f"ssd_fwd_r{int(reverse)}", grid=(nbatch, nc),
                  in_specs=[xs_spec, bc_spec, dt_spec, al_spec], out_specs=(xs_spec, st_spec),
                  out_shape=(jax.ShapeDtypeStruct((nbatch, s, 1024), F32), jax.ShapeDtypeStruct((nbatch, nc, 128, 1024), F32)),
                  scratch_shapes=[pltpu.VMEM((128, 1024), F32)],
                  compiler_params=_params())(xbc3, xbc3, dt3, alog, *(carry[0] if carry else ()))


def _ssd_bwd(xbc3, dt3, alog, st4, dy3, reverse, add_to=(), scatter=()):
    nbatch, s, _ = xbc3.shape
    q = min(SSD_CHUNK, s)
    nc = s // q
    lane0 = SSD_HEADS * int(reverse)
    nadd, ns = len(add_to), len(scatter)

    def body(xs_ref, bc_ref, dt_ref, al_ref, st0_ref, dy_ref, *rest):
        adds, srcs, rest = rest[:nadd], rest[nadd:nadd + ns], rest[nadd + ns:]
        (dxs_ref, dbc_ref, ddt_ref, dal_ref), lands, dst = rest[:4], rest[4:4 + ns], rest[4 + ns]
        n, i = pl.program_id(0), pl.program_id(1)
        if ns:
            sends, arrivals = _scatter_copies(srcs, lands, *rest[5 + ns:])

            @pl.when((n == 0) & (i == 0))
            def _():
                for cp in sends:
                    cp.start()

        @pl.when(i == 0)
        def _():
            dst[...] = jnp.zeros_like(dst)

        @pl.when((i == 0) & (n == 0))
        def _():
            dal_ref[...] = jnp.zeros_like(dal_ref)

        c = _ssd_common(xs_ref, bc_ref, dt_ref, al_ref, reverse, lane0)
        hm = _head_masks()
        reduce_m = _head_reduce(lane0)
        s0_all, ds1_all, dy = st0_ref[...], dst[...], dy_ref[...]
        lane = lax.broadcasted_iota(jnp.int32, (q, 128), 1)
        sub = lax.broadcasted_iota(jnp.int32, (128, q), 0)
        rowacc = jnp.zeros((q, 128), F32)
        colacc_t = jnp.zeros((128, q), F32)
        dv_l, yst_l, dvbar_l, dk_l, dc_l = [], [], [], [], []
        for g in range(SSD_GROUPS):
            sl = slice(g * 256, (g + 1) * 256)
            cg, bg = _mx(c["cm"][:, g * 128:(g + 1) * 128]), _mx(c["bm"][:, g * 128:(g + 1) * 128])
            cb = _dot(cg, bg, _NT)
            vg, dyg, wg, ecg = c["v"][:, sl], dy[:, sl], c["w"][:, sl], c["e_c"][:, sl]
            s0, ds1 = _mx(s0_all[:, sl]), _mx(ds1_all[:, sl])
            dye = _mx(dyg * ecg)
            yst_l.append(_dot(cg, s0) * ecg)
            dcg = _dot(dye, s0, _NT)
            dst[:, sl] = c["e_l"][:, sl] * ds1_all[:, sl] + _dot(cg, dye, _TN)
            vbar = _mx(vg * wg)
            dvbar = _dot(bg, ds1)
            dvbar_l.append(dvbar)
            dvg = dvbar * wg
            dkg = _dot(vbar, ds1, _NT)
            for e in range(4):
                h = 4 * g + e
                m = _ssd_decay(c, h)
                dyh, vh = _mx(jnp.where(hm[e], dyg, 0.0)), _mx(jnp.where(hm[e], vg, 0.0))
                dvg = dvg + _dot(_mx(m * cb), dyh, _TN)
                dcb = _dot(dyh, vh, _NT) * m
                dcbb = _mx(dcb)
                dcg = dcg + _dot(dcbb, bg)
                dkg = dkg + _dot(dcbb, cg, _TN)
                wmat = dcb * cb
                rowacc = jnp.where(lane == lane0 + h, jnp.sum(wmat, axis=1, keepdims=True), rowacc)
                colacc_t = jnp.where(sub == lane0 + h, jnp.sum(wmat, axis=0, keepdims=True), colacc_t)
            dv_l.append(dvg)
            dk_l.append(dkg)
            dc_l.append(dcg)
        dv = jnp.concatenate(dv_l, axis=1)
        yst = jnp.concatenate(yst_l, axis=1)
        dvbar = jnp.concatenate(dvbar_l, axis=1)
        t1 = _dot01(dy * yst, reduce_m, split="a", terms=2)
        t2 = _dot01(c["v"] * c["w"] * dvbar, reduce_m, split="a", terms=2)
        dlast = jnp.sum(t2, axis=0, keepdims=True) + _dot01(
            c["e_l"] * jnp.sum(ds1_all * s0_all, axis=0, keepdims=True), reduce_m, split="a", terms=2)
        dcum = rowacc - colacc_t.T + t1 - t2
        dcum = dcum + jnp.where(lax.broadcasted_iota(jnp.int32, (q, 128), 0) == c["edge"], dlast, 0.0)
        dda = _dot01(c["mask"].astype(F32), dcum, _TN, split="b", terms=3)
        ddt = dda * c["a"] + _dot01(dv * c["xs"], reduce_m, split="a", terms=2)
        dal_ref[...] += jnp.sum(dda * c["dt"], axis=0, keepdims=True) * c["a"]
        dxs = dv * c["dt_x"]
        dbc = jnp.concatenate(dk_l + dc_l, axis=1)
        if nadd:
            for a_ref in adds[:-2]:
                dxs = dxs + a_ref[...]
            dbc = dbc + adds[-2][...]
            ddt = ddt + adds[-1][...]
        ddt_ref[...] = ddt
        dxs_ref[...] = dxs
        dbc_ref[...] = dbc
        if ns:
            @pl.when((n == nbatch - 1) & (i == nc - 1))
            def _():
                for cp in arrivals:
                    cp.wait_recv()
                for cp in sends:
                    cp.wait_send()

    ck = (lambda i: i) if reverse else (lambda i: nc - 1 - i)
    xs_spec = pl.BlockSpec((None, q, 1024), lambda n, i: (n, ck(i), 0))
    bc_spec = pl.BlockSpec((None, q, 1024), lambda n, i: (n, ck(i), 1))
    dt_spec = pl.BlockSpec((None, q, 128), lambda n, i: (n, ck(i), 0))
    al_spec = pl.BlockSpec((1, 128), lambda n, i: (0, 0))
    st_spec = pl.BlockSpec((None, None, 128, 1024), lambda n, i: (n, ck(i), 0, 0))
    return _pcall(body, name=f"ssd_bwd_r{int(reverse)}", grid=(nbatch, nc),
                  in_specs=([xs_spec, bc_spec, dt_spec, al_spec, st_spec, xs_spec] + [xs_spec] * (nadd - 1)
                            + [dt_spec] * bool(nadd) + [ANY] * ns),
                  out_specs=(xs_spec, xs_spec, dt_spec, al_spec) + (ANY,) * ns,
                  out_shape=(jax.ShapeDtypeStruct((nbatch, s, 1024), F32), jax.ShapeDtypeStruct((nbatch, s, 1024), F32),
                             jax.ShapeDtypeStruct((nbatch, s, 128), F32), jax.ShapeDtypeStruct((1, 128), F32))
                  + tuple(jax.ShapeDtypeStruct(c.shape, c.dtype) for c in scatter),
                  scratch_shapes=[pltpu.VMEM((128, 1024), F32)] + (_scatter_scratch(ns) if ns else []),
                  compiler_params=_params())(xbc3, xbc3, dt3, alog, st4, dy3, *add_to, *scatter)


def _gla_block(q, k, g, reverse):
    bq = g.shape[0]
    nsub = bq // HGRN_SUB
    edge = 0 if reverse else bq - 1
    ri = lax.broadcasted_iota(jnp.int32, (bq, bq), 0)
    ci = lax.broadcasted_iota(jnp.int32, (bq, bq), 1)
    rb, cb = jnp.right_shift(ri, HGRN_SUB_SHIFT), jnp.right_shift(ci, HGRN_SUB_SHIFT)
    mask = (ri <= ci) if reverse else (ri >= ci)
    m_within = (mask & (rb == cb)).astype(F32)
    m_before = ((cb > rb) if reverse else (cb < rb)).astype(F32)
    bl = _dot01(m_within, g, split="b", terms=3)
    c = _dot01(m_before, g, split="b", terms=3)
    last = c[edge:edge + 1, :] + bl[edge:edge + 1, :]
    ebl, enbl, ec, elc = jnp.exp(bl), jnp.exp(-bl), jnp.exp(c), jnp.exp(last - c)
    qh = q * HGRN_SCALE * ebl
    kh = k * enbl
    blk = jnp.right_shift(lax.broadcasted_iota(jnp.int32, (bq, 1), 0), HGRN_SUB_SHIFT)
    scale = []
    for i in range(nsub):
        valid = (blk >= i) if reverse else (blk <= i)
        ex = jnp.where(valid, c[i * HGRN_SUB:i * HGRN_SUB + 1, :] - c, 0.0)
        scale.append(jnp.where(valid, jnp.exp(ex), 0.0))
    return dict(bq=bq, nsub=nsub, edge=edge, mask=mask, m_within=m_within, m_before=m_before, ebl=ebl, enbl=enbl, ec=ec,
                elc=elc, e_l=jnp.exp(last), qh=qh, qt=qh * ec, kh=kh, kb=kh * elc, scale=scale)


def _gla_scores(c, hs):
    keys = [_mx(c["kh"][:, hs] * c["scale"][i][:, hs]) for i in range(c["nsub"])]
    rows = [_dot(_mx(c["qh"][i * HGRN_SUB:(i + 1) * HGRN_SUB, hs]), keys[i], _NT) for i in range(c["nsub"])]
    return jnp.where(c["mask"], jnp.concatenate(rows, axis=0), 0.0), keys


def _gla_specs(nbatch, s, w, reverse_order):
    bq = min(HGRN_BLOCK, s)
    nblk = s // bq
    bi = (lambda i: nblk - 1 - i) if reverse_order else (lambda i: i)
    col = lambda cb: pl.BlockSpec((nbatch, bq, w), lambda i: (0, bi(i), cb))
    st_spec = pl.BlockSpec((nbatch, None, 128, w), lambda i: (0, bi(i), 0, 0))
    return bq, nblk, col, st_spec


def _gla_fwd(proj3, l0, l1, reverse, carry=None):
    nbatch, s, w5 = proj3.shape
    w = w5 // 5
    bq, nblk, col, st_spec = _gla_specs(nbatch, s, w, reverse)
    vec = pl.BlockSpec((1, w), lambda i: (0, 0))

    def body(q_ref, f_ref, v_ref, l0_ref, l1_ref, o_ref, st_ref, st):
        @pl.when(pl.program_id(0) == 0)
        def _():
            st[...] = jnp.zeros_like(st)

        for b in range(nbatch):
            st_ref[b] = st[b]
            k, g = _f_hgrn_pre(f_ref[b], l0_ref[...], l1_ref[...])
            c = _gla_block(q_ref[b], k, g, reverse)
            v = v_ref[b]
            for h in range(HGRN_HEADS):
                hs = slice(h * 128, (h + 1) * 128)
                att, _ = _gla_scores(c, hs)
                vb = _mx(v[:, hs])
                s0 = st[b, :, hs]
                o_ref[b, :, hs] = _dot(_mx(att), vb) + _dot(_mx(c["qt"][:, hs]), _mx(s0), _NT)
                st[b, :, hs] = s0 * c["e_l"][:, hs] + _dot(vb, _mx(c["kb"][:, hs]), _TN)

    return _pcall(body, carry=carry, name=f"gla_fwd_r{int(reverse)}", grid=(nblk,),
                  in_specs=[col(0), col(1 + int(reverse)), col(3), vec, vec], out_specs=(col(0), st_spec),
                  out_shape=(jax.ShapeDtypeStruct((nbatch, s, w), F32), jax.ShapeDtypeStruct((nbatch, nblk, 128, w), F32)),
                  scratch_shapes=[pltpu.VMEM((nbatch, 128, w), F32)],
                  compiler_params=_params())(proj3, proj3, proj3, l0, l1, *(carry[0] if carry else ()))


def _gla_bwd(proj3, l0, l1, st4, do3, reverse, add_to=None):
    nbatch, s, w5 = proj3.shape
    w = w5 // 5
    bq, nblk, col, st_spec = _gla_specs(nbatch, s, w, not reverse)
    nadd = 0 if add_to is None else 2
    vec = pl.BlockSpec((1, w), lambda i: (0, 0))

    def body(q_ref, f_ref, v_ref, l0_ref, l1_ref, st_ref, do_ref, *rest):
        adds, (dq_ref, df_ref, dv_ref, dl0_ref, dl1_ref, dst) = rest[:nadd], rest[nadd:]

        @pl.when(pl.program_id(0) == 0)
        def _():
            dst[...] = jnp.zeros_like(dst)
            dl0_ref[...] = jnp.zeros_like(dl0_ref)
            dl1_ref[...] = jnp.zeros_like(dl1_ref)

        row = lax.broadcasted_iota(jnp.int32, (bq, 128), 0)
        for b in range(nbatch):
            (k, g), pre_vjp = jax.vjp(_f_hgrn_pre, f_ref[b], l0_ref[...], l1_ref[...])
            c = _gla_block(q_ref[b], k, g, reverse)
            s0_all, ds1_all = st_ref[b], dst[b]
            v, dy = v_ref[b], do_ref[b]
            dbl_l, dc_l, dk_l = [], [], []
            for h in range(HGRN_HEADS):
                hs = slice(h * 128, (h + 1) * 128)
                att, keys = _gla_scores(c, hs)
                qh, qt, kh, kb = c["qh"][:, hs], c["qt"][:, hs], c["kh"][:, hs], c["kb"][:, hs]
                vb, dyb = _mx(v[:, hs]), _mx(dy[:, hs])
                s0, ds1 = s0_all[:, hs], ds1_all[:, hs]
                datt = _mx(jnp.where(c["mask"], _dot(dyb, vb, _NT), 0.0))
                dqh_rows = []
                dkh = jnp.zeros((bq, 128), F32)
                dc = jnp.zeros((bq, 128), F32)
                for i in range(c["nsub"]):
                    rs = slice(i * HGRN_SUB, (i + 1) * HGRN_SUB)
                    dqh_rows.append(_dot(datt[rs], keys[i]))
                    dki = _dot(datt[rs], _mx(qh[rs]), _TN)
                    sc = c["scale"][i][:, hs]
                    dkh = dkh + dki * sc
                    dex = dki * (kh * sc)
                    dc = dc - dex + jnp.where(row == i * HGRN_SUB, jnp.sum(dex, axis=0, keepdims=True), 0.0)
                dqt = _dot(dyb, _mx(s0))
                dkb = _dot(vb, _mx(ds1))
                dv = _dot(_mx(att), dyb, _TN) + _dot(_mx(kb), _mx(ds1), _NT)
                dst[b, :, hs] = c["e_l"][:, hs] * ds1 + _dot(dyb, _mx(qt), _TN)
                dqh = jnp.concatenate(dqh_rows, axis=0) + dqt * c["ec"][:, hs]
                dkh = dkh + dkb * c["elc"][:, hs]
                kbk = dkb * kb
                dlast = jnp.sum(kbk, axis=0, keepdims=True) + c["e_l"][:, hs] * jnp.sum(ds1 * s0, axis=0, keepdims=True)
                at_edge = jnp.where(row == c["edge"], dlast, 0.0)
                dc_l.append(dc + dqt * qt - kbk + at_edge)
                dbl_l.append(dqh * qh - dkh * kh + at_edge)
                dq = dqh * c["ebl"][:, hs] * HGRN_SCALE
                if nadd:
                    dq, dv = dq + adds[0][b, :, hs], dv + adds[1][b, :, hs]
                dq_ref[b, :, hs] = dq.astype(dq_ref.dtype)
                dv_ref[b, :, hs] = dv.astype(dv_ref.dtype)
                dk_l.append(dkh * c["enbl"][:, hs])
            dg = (_dot01(c["m_within"], jnp.concatenate(dbl_l, axis=1), _TN, split="b", terms=2)
                  + _dot01(c["m_before"], jnp.concatenate(dc_l, axis=1), _TN, split="b", terms=2))
            df, d0, d1 = pre_vjp((jnp.concatenate(dk_l, axis=1), dg))
            df_ref[b] = df.astype(df_ref.dtype)
            dl0_ref[...] += d0
            dl1_ref[...] += d1

    shp_sum = jax.ShapeDtypeStruct((nbatch, s, w), BF16 if nadd else F32)
    shp_vec = jax.ShapeDtypeStruct((1, w), F32)
    return _pcall(body, name=f"gla_bwd_r{int(reverse)}", grid=(nblk,),
                  in_specs=[col(0), col(1 + int(reverse)), col(3), vec, vec, st_spec, col(0)] + [col(0)] * nadd,
                  out_specs=(col(0), col(0), col(0), vec, vec),
                  out_shape=(shp_sum, jax.ShapeDtypeStruct((nbatch, s, w), BF16), shp_sum, shp_vec, shp_vec),
                  scratch_shapes=[pltpu.VMEM((nbatch, 128, w), F32)],
                  compiler_params=_params())(proj3, proj3, proj3, l0, l1, st4, do3, *(add_to or ()))


DIRS = (False, True)


def _block_diag(w):
    eye = jnp.eye(16, dtype=w.dtype)
    return (eye[:, None, :, None] * w[:, :, None, :]).reshape(1024, 1024)


def _diag_blocks(m):
    m4 = m.reshape(16, 64, 16, 64)
    return jnp.stack([m4[i, :, i, :] for i in range(16)], axis=0)


def _pad_lanes(v, n=128):
    return jnp.pad(v, [(0, 0)] * (v.ndim - 1) + [(0, n - v.shape[-1])])


def _mlp_fwd(tag, x, nw, w1, w2, carry=None):
    (h,) = _pw_fwd(f"{tag}_norm", _f_norm, [(x, 0)], [(nw, 0)], [BF16], 1024, 1)
    a, r, *got = _mm(f"{tag}_up", h, w1, "nn", relu2=True, carry=carry)
    return _mm(f"{tag}_down", r, w2, "nn", res=x), (h, a, r), got


def _mlp_bwd(tag, x, nw, w1, w2, saved, dxo, carry=None):
    h, a, r = saved
    dw2, *got = _mm(f"{tag}_dw2", r, dxo, "tn", carry=carry) if carry else (_mm(f"{tag}_dw2", r, dxo, "tn"),)
    da = _mm(f"{tag}_da", dxo, w2, "nt", relu2_of=a, out_dtype=BF16)
    dw1 = _mm(f"{tag}_dw1", h, da, "tn", col_shards=4)
    dx, dnw = _mm_sum_nt(f"{tag}_dh", [(da, k, 1024) for k in range(4)], [(w1, k) for k in range(4)], norm_bwd=(x, nw, dxo))
    return dx, dw1, dw2, dnw, got


def _split_in0(pieces, dt_piece):
    tm = 256

    def body(p0, p1, p2, p3, p4, p5, o_ref):
        full = jnp.concatenate([p0[...], p1[...], p2[...], p3[...], p4[...], p5[:, :32]], axis=1)
        for j in range(4):
            o_ref[j] = full[:, 1288 * j:1288 * (j + 1)]

    blk = pl.BlockSpec((tm, 1024), lambda i: (i, 0))
    return _pcall(body, name="split_in0", grid=(1024 // tm,), in_specs=[blk] * 5 + [pl.BlockSpec((tm, 128), lambda i: (i, 0))],
                  out_specs=pl.BlockSpec((4, tm, 1288), lambda i: (0, i, 0)),
                  out_shape=jax.ShapeDtypeStruct((4, 1024, 1288), F32), compiler_params=_params())(*pieces, dt_piece)


def _assemble_in0(shards):
    tm = 256

    def body(s_ref, m_ref, d_ref):
        full = jnp.concatenate([s_ref[j] for j in range(4)], axis=1)
        m_ref[...] = full[:, :5120]
        d_ref[...] = jnp.concatenate([full[:, 5120:5152], jnp.zeros((tm, 96), full.dtype)], axis=1)

    return _pcall(body, name="assemble_in0", grid=(1024 // tm,), in_specs=[pl.BlockSpec((4, tm, 1288), lambda i: (0, i, 0))],
                  out_specs=(pl.BlockSpec((tm, 5120), lambda i: (i, 0)), pl.BlockSpec((tm, 128), lambda i: (i, 0))),
                  out_shape=(jax.ShapeDtypeStruct((1024, 5120), shards.dtype), jax.ShapeDtypeStruct((1024, 128), shards.dtype)),
                  compiler_params=_params())(shards)


EARLY = ("odd_w_in", "odd_w_out", "mlp_w1_l1", "mlp_w2_l1")
MID = ("even_w_out", "mlp_w1_l0", "mlp_w2_l0")
LATE = ("even_w_in",)


def _local_step(x3, tgt3, w, w_main0, w_dt0, pair_reduce=None, late=None):
    nb, s, d = x3.shape
    carries, arrived = late if late else ({}, None)
    t = nb * s
    x0 = x3.reshape(t, d)
    tgt = tgt3.reshape(t, d)
    grads = {}
    row = lambda v: v.reshape(1, -1)
    to3 = lambda v: v.reshape(nb, s, v.shape[-1])
    to2 = lambda v: v.reshape(-1, v.shape[-1])

    conv_w, conv_b = w["even_conv_w"][0], row(w["even_conv_b"][0])
    nmix0 = row(w["norm_mix"][0])
    (h0,) = _pw_fwd("l0_norm", _f_norm, [(x0, 0)], [(nmix0, 0)], [BF16], 1024, 1)
    proj0 = _mm("l0_proj", h0, w_main0, "nn")
    dt_raw = _mm("l0_proj_dt", h0, w_dt0, "nn")
    conv2, xbc3 = _conv_fwd(to3(proj0), conv_w, conv_b, 0, 2, True)
    u_lru = to2(_conv_fwd(to3(proj0), conv_w, conv_b, 2, 1, False))
    xbc = to2(xbc3)
    dt_bias = _pad_lanes(w["ssd_dt_bias"][0].reshape(1, 32))
    (dt,) = _pw_fwd("l0_dt", _f_softplus, [(dt_raw, 0)], [(dt_bias, 0)], [F32], 128, 1)
    dt3 = to3(dt)
    alog = _pad_lanes(w["ssd_a_log"][0].reshape(1, 32))
    def merge(upd):
        out = dict(w)
        for k, v in upd.items():
            if isinstance(k, tuple):
                both = list(out.get(k[0]) or [None, None])
                both[k[1]] = v
                out[k[0]] = both
            else:
                out[k] = v
        return out

    ssd = [_ssd_fwd(xbc3, dt3, alog, r, carry=carries.get(key)) for r, key in zip(DIRS, ("ssd0", "ssd1"))]
    if late:
        w = merge(arrived("ssd0", ssd[0][2:]))
        w = merge(arrived("ssd1", ssd[1][2:]))
    yf, yb = to2(ssd[0][0]), to2(ssd[1][0])
    dskip = jnp.repeat(w["ssd_d"][0], SSD_HEADDIM).reshape(1, 1024)
    snw = row(w["ssd_norm_w"][0])
    ssd_ins = [(yf, 0), (yb, 0), (xbc, 0), (proj0, 3)]
    (ya,) = _pw_fwd("l0_ssd_post", _f_ssd_post, ssd_ins, [(dskip, 0), (snw, 0)], [BF16], 1024, 1, groups=SSD_GROUPS)
    w_gates = [_block_diag(w[k][0, r]).astype(MXU_DTYPE) for r in range(2) for k in ("lru_w_a", "lru_w_x")]
    pre = [_mm(f"l0_lru_pre{i}", u_lru, wg, "nn") for i, wg in enumerate(w_gates)]
    lru_par = [[(row(w[k][0, r]), 0) for k in ("lru_b_a", "lru_b_x", "lru_lambda")] for r in range(2)]
    lru_ins = [[(pre[2 * r], 0), (pre[2 * r + 1], 0), (u_lru, 0)] for r in range(2)]
    ab = [_pw_fwd(f"l0_lru_gates{r}", _f_lru_gates, lru_ins[r], lru_par[r], [F32, F32], 1024, 1) for r in range(2)]
    hs = [_lru_scan(to3(ab[r][0]), to3(ab[r][1]), DIRS[r]) for r in range(2)]
    lru_post_ins = [(to2(hs[0]), 0), (to2(hs[1]), 0), (proj0, 4)]
    (ybm,) = _pw_fwd("l0_lru_post", _f_lru_post, lru_post_ins, [], [BF16], 1024, 1)
    w_out0 = w["even_w_out"][0]
    x1 = _mm("l0_out_a", ya, w_out0[:1024], "nn", res=x0)
    x1 = _mm("l0_out_b", ybm, w_out0[1024:], "nn", res=x1)
    nmlp0 = row(w["norm_mlp"][0])
    x2, mlp0, got = _mlp_fwd("l0_mlp", x1, nmlp0, w["mlp_w1"][0], w["mlp_w2"][0], carry=carries.get("odd"))
    if late:
        w = merge(arrived("odd", got))

    w_in1 = w["odd_w_in"][0]
    nmix1 = row(w["norm_mix"][1])
    (h1,) = _pw_fwd("l1_norm", _f_norm, [(x2, 0)], [(nmix1, 0)], [BF16], 1024, 1)
    proj1 = _mm("l1_proj", h1, w_in1, "nn")
    proj1_3 = to3(proj1)
    lb0, lb1 = row(w["hgrn_lb_logits"][0]), row(w["hgrn_lb_logits"][1])
    gla = [_gla_fwd(proj1_3, lb0, lb1, r, carry=carries.get(key)) for r, key in zip(DIRS, ("gla0", "gla1"))]
    if late:
        w = merge(arrived("gla0", gla[0][2:]))
        w = merge(arrived("gla1", gla[1][2:]))
    hnw = row(w["hgrn_norm_w"][0])
    hpost_ins = [(to2(gla[0][0]), 0), (to2(gla[1][0]), 0), (proj1, 4)]
    (yo,) = _pw_fwd("l1_hgrn_post", _f_hgrn_post, hpost_ins, [(hnw, 0)], [BF16], 1024, 1, groups=HGRN_HEADS)
    w_out1 = w["odd_w_out"][0]
    x3_ = _mm("l1_out", yo, w_out1, "nn", res=x2)
    nmlp1 = row(w["norm_mlp"][1])
    x4, mlp1, _ = _mlp_fwd("l1_mlp", x3_, nmlp1, w["mlp_w1"][1], w["mlp_w2"][1])

    dx4, dnf, loss = _loss_head(x4, tgt, row(w["norm_final"]))
    grads["norm_final"] = dnf.reshape(-1)

    dx3, dw1_1, dw2_1, dnmlp1, _ = _mlp_bwd("l1_mlp", x3_, nmlp1, w["mlp_w1"][1], w["mlp_w2"][1], mlp1, dx4)
    big = {"odd_w_out": _mm("l1_dwout", yo, dx3, "tn").reshape(4, 256, 1024)}
    dyo = _mm("l1_dyo", dx3, w_out1, "nt")
    (do, dgate1), (dhnw,) = _pw_bwd("l1_hgrn_post_b", _f_hgrn_post, hpost_ins, [(hnw, 0)], [dyo], 1024, 1, [0, 2],
                                    out_dtypes=[F32, BF16], groups=HGRN_HEADS, tm=ROWS_FWD)
    grads["hgrn_norm_w"] = dhnw
    do3 = to3(do)
    gb = [_gla_bwd(proj1_3, lb0, lb1, gla[0][1], do3, False)]
    gb.append(_gla_bwd(proj1_3, lb0, lb1, gla[1][1], do3, True, add_to=(gb[0][0], gb[0][2])))
    grads["hgrn_lb_logits"] = jnp.concatenate([gb[0][3] + gb[1][3], gb[0][4] + gb[1][4]], axis=0)
    dparts1 = [to2(gb[1][0]), to2(gb[0][1]), to2(gb[1][1]), to2(gb[1][2]), dgate1]
    dwin1 = jnp.concatenate([_mm(f"l1_dwin{i}", h1, dp, "tn") for i, dp in enumerate(dparts1)], axis=1)
    big["odd_w_in"] = dwin1.reshape(1024, 4, 1280).transpose(1, 0, 2)
    dx2, dnmix1 = _mm_sum_nt("l1_dh", dparts1, [(w_in1, i) for i in range(5)], norm_bwd=(x2, nmix1, dx3))
    big["mlp_w1_l1"], big["mlp_w2_l1"] = dw1_1, dw2_1.reshape(4, 1024, 1024)
    box = {}

    def mlp0_bwd(carry=None):
        box["mlp0"] = _mlp_bwd("l0_mlp", x1, nmlp0, w["mlp_w1"][0], w["mlp_w2"][0], mlp0, dx2, carry=carry)
        return box["mlp0"][4]

    early_sums = tuple(pair_reduce(EARLY, [big[n] for n in EARLY], mlp0_bwd)) if pair_reduce else tuple(mlp0_bwd())

    dx1, dw1_0, dw2_0, dnmlp0 = box["mlp0"][:4]
    big["mlp_w1_l0"], big["mlp_w2_l0"] = dw1_0, dw2_0.reshape(4, 1024, 1024)
    grads["norm_mlp"] = jnp.concatenate([dnmlp0, dnmlp1], axis=0)
    big["even_w_out"] = jnp.concatenate([_mm("l0_dwout_a", ya, dx1, "tn"), _mm("l0_dwout_b", ybm, dx1, "tn")],
                                        axis=0).reshape(4, 512, 1024)
    dya = _mm("l0_dya", dx1, w_out0[:1024], "nt")
    dyb = _mm("l0_dyb", dx1, w_out0[1024:], "nt")
    (dh, dgate0), _ = _pw_bwd("l0_lru_post_b", _f_lru_post, lru_post_ins, [], [dyb], 1024, 1, [0, 2], out_dtypes=[F32, BF16],
                               tm=ROWS_FWD)
    dh3 = to3(dh)

    def lru0_bwd(carry=None):
        box["lru0"] = _lru_scan_bwd(to3(ab[0][0]), hs[0], dh3, DIRS[0], carry=carry)
        return box["lru0"][2:]

    mid_sums = tuple(pair_reduce(MID, [big[n] for n in MID], lru0_bwd)) if pair_reduce else tuple(lru0_bwd())
    dpre, du_parts, dlru = [], [], {k: [] for k in ("lru_b_a", "lru_b_x", "lru_lambda")}
    for r in range(2):
        g_r, da_r = box["lru0"][:2] if r == 0 else _lru_scan_bwd(to3(ab[r][0]), hs[r], dh3, DIRS[r])
        (dpa, dpx, du_r), (dba, dbx, dlam) = _pw_bwd(f"l0_lru_gates_b{r}", _f_lru_gates, lru_ins[r], lru_par[r],
                                                     [to2(da_r), to2(g_r)], 1024, 1, [0, 1, 2],
                                                     out_dtypes=[BF16, BF16, F32])
        dpre += [dpa, dpx]
        du_parts.append(du_r)
        dlru["lru_b_a"].append(dba)
        dlru["lru_b_x"].append(dbx)
        dlru["lru_lambda"].append(dlam)
    for k, v in dlru.items():
        grads[k] = jnp.concatenate(v, axis=0)[None]
    dwg = [_diag_blocks(_mm(f"l0_dwgate{i}", u_lru, dp, "tn")) for i, dp in enumerate(dpre)]
    grads["lru_w_a"] = jnp.stack([dwg[0], dwg[2]])[None]
    grads["lru_w_x"] = jnp.stack([dwg[1], dwg[3]])[None]
    du = _mm_sum_nt("l0_du", dpre, [(wg, 0) for wg in w_gates], add=du_parts)
    (dy, dxs_skip, dz), (ddskip, dsnw) = _pw_bwd("l0_ssd_post_b", _f_ssd_post, ssd_ins, [(dskip, 0), (snw, 0)], [dya],
                                                 1024, 1, [0, 2, 3], out_dtypes=[F32, F32, BF16], groups=SSD_GROUPS)
    grads["ssd_d"] = ddskip.reshape(SSD_HEADS, SSD_HEADDIM).sum(axis=1)[None]
    grads["ssd_norm_w"] = dsnw
    dy3 = to3(dy)
    sb0 = _ssd_bwd(xbc3, dt3, alog, ssd[0][1], dy3, False, scatter=early_sums)
    sb1 = _ssd_bwd(xbc3, dt3, alog, ssd[1][1], dy3, True, add_to=(sb0[0], to3(dxs_skip), sb0[1], sb0[2]), scatter=mid_sums)
    grads["ssd_a_log"] = (sb0[3] + sb1[3])[:, :32].reshape(1, 2, 16)
    ddt = to2(sb1[2])
    (ddt_raw,), (ddtb,) = _pw_bwd("l0_dt_b", _f_softplus, [(dt_raw, 0)], [(dt_bias, 0)], [ddt], 128, 1, [0])
    grads["ssd_dt_bias"] = ddtb[:, :32].reshape(1, 2, 16)
    cb = [_conv_bwd(sb1[0], to3(proj0), conv_w, 0, conv2), _conv_bwd(sb1[1], to3(proj0), conv_w, 1, conv2),
          _conv_bwd(to3(du), to3(proj0), conv_w, 2)]
    dcw = jnp.concatenate([c_[1] for c_ in cb], axis=1)
    grads["even_conv_w"] = dcw[:4][None]
    grads["even_conv_b"] = dcw[4:5]
    dparts0 = [to2(c_[0]) for c_ in cb] + [dz, dgate0]
    dwin0 = [_mm(f"l0_dwin{i}", h0, dp, "tn") for i, dp in enumerate(dparts0)]
    big["even_w_in"] = _split_in0(dwin0, _mm("l0_dwin_dt", h0, ddt_raw, "tn"))
    dx0, dnmix0 = _mm_sum_nt("l0_dh", dparts0 + [ddt_raw], [(w_main0, i) for i in range(5)] + [(w_dt0, 0)],
                             norm_bwd=(x0, nmix0, dx1))
    grads["norm_mix"] = jnp.concatenate([dnmix0, dnmix1], axis=0)
    return loss, dx0.reshape(nb, s, d), grads, big, (early_sums + mid_sums, sb0[4:] + sb1[4:])


ANY = pl.BlockSpec(memory_space=pl.ANY)


def _place():
    return lax.axis_index("x"), lax.axis_index("y"), lax.axis_index("c")


def _remote(src, dst, send_sems, recv_sems, k, to):
    return pltpu.make_async_remote_copy(src_ref=src, dst_ref=dst, send_sem=send_sems.at[k], recv_sem=recv_sems.at[k],
                                        device_id=to, device_id_type=MESH)


def _gather_start(x_refs, out_refs, send_sems, recv_sems, finish=False):
    n = len(x_refs)
    halves = [r.shape[0] // 2 for r in x_refs]
    x, y, c = _place()
    sibling = (x, y, 1 - c)
    chips = [(1 - x, y), (x, 1 - y), (1 - x, 1 - y)]

    def blk(t, px, py, hc):
        return out_refs[t].at[2 * px + py, pl.ds(hc * halves[t], halves[t]), :]

    def src(t):
        return x_refs[t].at[pl.ds(c * halves[t], halves[t]), :]

    first = [_remote(src(t), blk(t, x, y, c), send_sems, recv_sems, 6 * t + j, (*chip, c))
             for t in range(n) for j, chip in enumerate(chips)]
    if not finish:
        for cp in first:
            cp.start()
        return
    passed = []
    for t in range(n):
        for j, chip in enumerate(chips):
            _remote(src(t), blk(t, *chip, c), send_sems, recv_sems, 6 * t + j, (*chip, c)).wait_recv()
            cp = _remote(blk(t, *chip, c), blk(t, *chip, c), send_sems, recv_sems, 6 * t + 3 + j, sibling)
            cp.start()
            passed.append(cp)
    for t in range(n):
        for j, chip in enumerate(chips):
            _remote(src(t), blk(t, *chip, 1 - c), send_sems, recv_sems, 6 * t + 3 + j, sibling).wait_recv()
    for cp in first + passed:
        cp.wait_send()


_gather_finish = functools.partial(_gather_start, finish=True)


def _gather_carry(shards):
    n = len(shards)
    return (list(shards), [jax.ShapeDtypeStruct((4,) + s.shape, s.dtype) for s in shards],
            [pltpu.SemaphoreType.DMA((6 * n,)), pltpu.SemaphoreType.DMA((6 * n,))], _gather_start, _gather_finish)


def _gather_chips(shards):
    n = len(shards)
    srcs, shapes, scratch, start, finish = _gather_carry(shards)

    def body(*refs):
        start(refs[:n], refs[n:2 * n], *refs[2 * n:])
        finish(refs[:n], refs[n:2 * n], *refs[2 * n:])

    return _pcall(body, name="gather_weights", in_specs=[ANY] * n, out_specs=(ANY,) * n, out_shape=tuple(shapes),
                  scratch_shapes=scratch, compiler_params=_params())(*shards)


def _pair_swap_start(g_refs, land_refs, send_sems, recv_sems, finish=False):
    x, y, c = _place()
    cps = []
    for t, g in enumerate(g_refs):
        half = g.shape[1] // 2
        cps += [_remote(g.at[j, pl.ds((1 - c) * half, half), :], land_refs[t].at[j], send_sems, recv_sems, 4 * t + j,
                        (x, y, 1 - c)) for j in range(4)]
    for cp in cps:
        cp.wait() if finish else cp.start()


_pair_swap_finish = functools.partial(_pair_swap_start, finish=True)


def _pair_swap_carry(gps):
    n = len(gps)
    return (list(gps), [jax.ShapeDtypeStruct((4, g.shape[1] // 2, g.shape[2]), F32) for g in gps],
            [pltpu.SemaphoreType.DMA((4 * n,)), pltpu.SemaphoreType.DMA((4 * n,))], _pair_swap_start, _pair_swap_finish)


def _pair_swap(name, gps):
    n = len(gps)
    srcs, shapes, scratch, start, finish = _pair_swap_carry(gps)

    def body(*refs):
        start(refs[:n], refs[n:2 * n], *refs[2 * n:])
        finish(refs[:n], refs[n:2 * n], *refs[2 * n:])

    return _pcall(body, name=f"pair_swap_{name}", in_specs=[ANY] * n, out_specs=(ANY,) * n, out_shape=tuple(shapes),
                  scratch_shapes=scratch, compiler_params=_params())(*gps)


def _pair_add(name, gp, land, cidx):
    _, half, cols = land.shape
    tr = _tile(half, 512)
    nh = half // tr

    def body(c_ref, g_ref, l_ref, o_ref):
        o_ref[...] = (g_ref[...] + l_ref[...]).astype(o_ref.dtype)

    grid_spec = pltpu.PrefetchScalarGridSpec(
        num_scalar_prefetch=1, grid=(4, nh),
        in_specs=[pl.BlockSpec((None, tr, cols), lambda j, i, c: (j, c[0] * nh + i, 0)),
                  pl.BlockSpec((None, tr, cols), lambda j, i, c: (j, i, 0))],
        out_specs=pl.BlockSpec((None, tr, cols), lambda j, i, c: (j, i, 0)))
    return _pcall(body, name=f"pair_add_{name}", grid_spec=grid_spec, out_shape=jax.ShapeDtypeStruct((4, half, cols), BF16),
                  compiler_params=_params())(cidx, gp, land)


def _scatter_copies(s_refs, land_refs, send_sems, recv_sems):
    x, y, c = _place()
    me = 2 * x + y
    chips = [(1 - x, y), (x, 1 - y), (1 - x, 1 - y)]
    pairs = [(t, j, px, py) for t in range(len(s_refs)) for j, (px, py) in enumerate(chips)]
    sends = [_remote(s_refs[t].at[2 * px + py], land_refs[t].at[me], send_sems, recv_sems, 3 * t + j, (px, py, c))
             for t, j, px, py in pairs]
    arrivals = [_remote(s_refs[t].at[me], land_refs[t].at[2 * px + py], send_sems, recv_sems, 3 * t + j, (px, py, c))
                for t, j, px, py in pairs]
    return sends, arrivals


def _scatter_scratch(n):
    return [pltpu.SemaphoreType.DMA((3 * n,)), pltpu.SemaphoreType.DMA((3 * n,))]


def _chip_scatter(name, css):
    n = len(css)

    def body(*refs):
        sends, arrivals = _scatter_copies(refs[:n], refs[n:2 * n], *refs[2 * n:])
        for cp in sends:
            cp.start()
        for cp in arrivals:
            cp.wait_recv()
        for cp in sends:
            cp.wait_send()

    return _pcall(body, name=f"chip_scatter_{name}", in_specs=[ANY] * n, out_specs=(ANY,) * n,
                  out_shape=tuple(jax.ShapeDtypeStruct(s.shape, s.dtype) for s in css),
                  scratch_shapes=_scatter_scratch(n), compiler_params=_params())(*css)


def _chip_sum(name, land):
    _, half, cols = land.shape
    tr = _tile(half, 512)

    def body(l_ref, o_ref):
        o_ref[...] = ((l_ref[0].astype(F32) + l_ref[1].astype(F32)) + l_ref[2].astype(F32)) + l_ref[3].astype(F32)

    return _pcall(body, name=f"chip_sum_{name}", grid=(half // tr,),
                  in_specs=[pl.BlockSpec((4, tr, cols), lambda i: (0, i, 0))],
                  out_specs=pl.BlockSpec((tr, cols), lambda i: (i, 0)),
                  out_shape=jax.ShapeDtypeStruct((half, cols), F32), compiler_params=_params())(land)


def _pair_join(reds):
    n = len(reds)

    def body(*refs):
        r_refs, out_refs = refs[:n], refs[n:2 * n]
        send_sems, recv_sems = refs[2 * n:]
        x, y, c = _place()
        cps = [_remote(r_refs[t], out_refs[t].at[c], send_sems, recv_sems, t, (x, y, 1 - c)) for t in range(n)]
        for cp in cps:
            cp.start()
        for t in range(n):
            _remote(r_refs[t], out_refs[t].at[1 - c], send_sems, recv_sems, t, (x, y, 1 - c)).wait_recv()
        for cp in cps:
            cp.wait_send()

    return _pcall(body, name="grad_pair_join", in_specs=[ANY] * n, out_specs=(ANY,) * n,
                  out_shape=tuple(jax.ShapeDtypeStruct((2,) + r.shape, F32) for r in reds),
                  scratch_shapes=[pltpu.SemaphoreType.DMA((n,)), pltpu.SemaphoreType.DMA((n,))],
                  compiler_params=_params())(*reds)


def _adamw(name, g, w, m, v):
    rows, cols = g.shape
    tr = _tile(rows, 512)

    def body(g_ref, w_ref, m_ref, v_ref, d_ref, mo_ref, vo_ref):
        gv = g_ref[...]
        mn = ADAM_B1 * m_ref[...] + (1.0 - ADAM_B1) * gv
        vn = ADAM_B2 * v_ref[...] + (1.0 - ADAM_B2) * jnp.square(gv)
        m_hat = mn / (1.0 - ADAM_B1 ** ADAM_STEP)
        v_hat = vn / (1.0 - ADAM_B2 ** ADAM_STEP)
        d_ref[...] = -ADAM_LR * (m_hat / (jnp.sqrt(v_hat) + ADAM_EPS) + ADAM_WD * w_ref[...])
        mo_ref[...] = mn
        vo_ref[...] = vn

    blk = pl.BlockSpec((tr, cols), lambda i: (i, 0))
    shp = jax.ShapeDtypeStruct((rows, cols), F32)
    return _pcall(body, name=f"adamw_{name}", grid=(rows // tr,), in_specs=[blk] * 4, out_specs=(blk,) * 3,
                  out_shape=(shp,) * 3, compiler_params=_params())(g, w, m, v)


def _pack(pieces, rows, dtype):
    flat = jnp.concatenate([p.reshape(-1).astype(dtype) for p in pieces])
    return jnp.pad(flat, (0, rows * PACK_COLS - flat.shape[0])).reshape(rows, PACK_COLS)


def _unpack(pack, shapes):
    flat = pack.reshape(-1)
    out, off = [], 0
    for shp in shapes:
        n = math.prod(shp)
        out.append(flat[off:off + n].reshape(shp))
        off += n
    return out


def _shard_of(full, axis, j):
    n = full.shape[axis] // 4
    return lax.slice_in_dim(full, j * n, (j + 1) * n, axis=axis)


def kernel(x, even_w_in, even_conv_w, even_conv_b, ssd_a_log, ssd_dt_bias, ssd_d, ssd_norm_w, lru_w_a, lru_b_a, lru_w_x, lru_b_x, lru_lambda, even_w_out, odd_w_in, hgrn_lb_logits, hgrn_norm_w, odd_w_out, norm_mix, norm_mlp, mlp_w1, mlp_w2, norm_final, loss_target, m_even_w_in, m_even_conv_w, m_even_conv_b, m_ssd_a_log, m_ssd_dt_bias, m_ssd_d, m_ssd_norm_w, m_lru_w_a, m_lru_b_a, m_lru_w_x, m_lru_b_x, m_lru_lambda, m_even_w_out, m_odd_w_in, m_hgrn_lb_logits, m_hgrn_norm_w, m_odd_w_out, m_norm_mix, m_norm_mlp, m_mlp_w1, m_mlp_w2, m_norm_final, v_even_w_in, v_even_conv_w, v_even_conv_b, v_ssd_a_log, v_ssd_dt_bias, v_ssd_d, v_ssd_norm_w, v_lru_w_a, v_lru_b_a, v_lru_w_x, v_lru_b_x, v_lru_lambda, v_even_w_out, v_odd_w_in, v_hgrn_lb_logits, v_hgrn_norm_w, v_odd_w_out, v_norm_mix, v_norm_mlp, v_mlp_w1, v_mlp_w2, v_norm_final):
    names = [n for n, _, _, _ in WEIGHTS]
    w_loc = dict(zip(names, (even_w_in, even_conv_w, even_conv_b, ssd_a_log, ssd_dt_bias, ssd_d, ssd_norm_w, lru_w_a, lru_b_a, lru_w_x, lru_b_x, lru_lambda, even_w_out, odd_w_in, hgrn_lb_logits, hgrn_norm_w, odd_w_out, norm_mix, norm_mlp, mlp_w1, mlp_w2, norm_final)))
    m_loc = dict(zip(names, (m_even_w_in, m_even_conv_w, m_even_conv_b, m_ssd_a_log, m_ssd_dt_bias, m_ssd_d, m_ssd_norm_w, m_lru_w_a, m_lru_b_a, m_lru_w_x, m_lru_b_x, m_lru_lambda, m_even_w_out, m_odd_w_in, m_hgrn_lb_logits, m_hgrn_norm_w, m_odd_w_out, m_norm_mix, m_norm_mlp, m_mlp_w1, m_mlp_w2, m_norm_final)))
    v_loc = dict(zip(names, (v_even_w_in, v_even_conv_w, v_even_conv_b, v_ssd_a_log, v_ssd_dt_bias, v_ssd_d, v_ssd_norm_w, v_lru_w_a, v_lru_b_a, v_lru_w_x, v_lru_b_x, v_lru_lambda, v_even_w_out, v_odd_w_in, v_hgrn_lb_logits, v_hgrn_norm_w, v_odd_w_out, v_norm_mix, v_norm_mlp, v_mlp_w1, v_mlp_w2, v_norm_final)))
    spec = {n: (blk, full, ax) for n, blk, full, ax in WEIGHTS}

    small = [n for n in names if n not in BIG]
    two_d = lambda n, v: v.reshape(BIG_2D[n])

    me = 2 * lax.axis_index("x") + lax.axis_index("y")
    cc = lax.axis_index("c")
    put = lambda whole, part, k: lax.dynamic_update_slice_in_dim(whole, part[None], k, axis=0)
    own = {n: two_d(n, w_loc[n]).astype(BF16) for n in BIG if not n.startswith("mlp")}
    own["small"] = _pack([w_loc[n] for n in SMALL_SHARDED], 16, F32)
    fill = lambda got, keys: [put(g, own[k], me) for g, k in zip(got, keys)]
    first = ("even_w_in", "small")
    g_in0, g_small = fill(_gather_chips([own[k] for k in first]), first)
    w_main0, w_dt0 = _assemble_in0(g_in0)
    w_full = {n: w_loc[n] for n in names if spec[n][2] is None}
    shards = [_unpack(g_small[j], [spec[n][0] for n in SMALL_SHARDED]) for j in range(4)]
    for n in ("mlp_w1", "mlp_w2"):
        for l in range(2):
            own[f"{n}_l{l}"] = w_loc[n][l].astype(BF16)
    riders = {"ssd0": ("mlp_w1_l0", "even_w_out"), "ssd1": ("mlp_w2_l0",), "odd": ("odd_w_in", "odd_w_out"),
              "gla0": ("mlp_w1_l1",), "gla1": ("mlp_w2_l1",)}
    carries = {key: _gather_carry([own[k] for k in ks]) for key, ks in riders.items()}

    def arrived(key, got):
        out = {}
        for k, g in zip(riders[key], fill(got, riders[key])):
            if k == "odd_w_in":
                out[k] = jnp.concatenate([g[j] for j in range(4)], axis=1)[None]
            elif k in ("odd_w_out", "even_w_out"):
                out[k] = g.reshape(spec[k][1])
            else:
                out[(k[:6], int(k[-1]))] = g if k.startswith("mlp_w1") else g.reshape(4096, 1024)
        return out

    for i, n in enumerate(SMALL_SHARDED):
        w_full[n] = jnp.concatenate([shards[j][i] for j in range(4)], axis=spec[n][2])

    cidx = cc.astype(jnp.int32).reshape(1)

    def pair_reduce(tags, tensors, run=None):
        lands = run(_pair_swap_carry(tensors)) if run else _pair_swap(tags[0], tensors)
        return [_pair_add(tag, g, land, cidx) for tag, g, land in zip(tags, tensors, lands)]

    loss_vec, grad_x, grads, big, (early_sums, early_landed) = _local_step(
        x, loss_target, w_full, w_main0, w_dt0, pair_reduce, (carries, arrived))
    loss = lax.psum(loss_vec[0, 0], ("x", "y", "c"))

    def dest_pack(j):
        return _pack([grads[n].reshape(spec[n][1]) if spec[n][2] is None else _shard_of(grads[n].reshape(spec[n][1]), spec[n][2], j)
                      for n in small], SMALL_ROWS, F32)

    late_tags = LATE + ("small",)
    late_sums = pair_reduce(late_tags, [big[n] for n in LATE] + [jnp.stack([dest_pack(j) for j in range(4)])])
    tags = EARLY + MID + late_tags
    chip_sums = list(early_sums) + late_sums
    landed = [put(land, lax.dynamic_index_in_dim(cs, me, axis=0, keepdims=False), me)
              for land, cs in zip(list(early_landed) + list(_chip_scatter("late", late_sums)), chip_sums)]
    halves = [_chip_sum(tag, land) for tag, land in zip(tags, landed)]
    red = {tag: put(r, h, cc).reshape(-1, r.shape[-1]) for tag, r, h in zip(tags, _pair_join(halves), halves)}
    for n in ("mlp_w1", "mlp_w2"):
        red[n] = jnp.concatenate([red[n + "_l0"], red[n + "_l1"]], axis=0)

    outs = {}
    for n, g in ((n, red[n]) for n in BIG):
        res = (g, *_adamw(n, g, two_d(n, w_loc[n]), two_d(n, m_loc[n]), two_d(n, v_loc[n])))
        outs[n] = [r.reshape(spec[n][0]) for r in res]
    blocks = [spec[n][0] for n in small]
    wp, mp, vp = (_pack([src[n] for n in small], SMALL_ROWS, F32) for src in (w_loc, m_loc, v_loc))
    res = (red["small"], *_adamw("small", red["small"], wp, mp, vp))
    unpacked = [_unpack(r, blocks) for r in res]
    for i, n in enumerate(small):
        outs[n] = [u[i] for u in unpacked]
    return (loss, grad_x, *[outs[n][k] for k in range(4) for n in names])
```
